```python
import math
import jax, jax.numpy as jnp
from jax import lax
import numpy as np

D_MODEL = 1024
BATCH = 16
SEQ = 2048
DEPTH = 1

D_RNN = 1024
N_RNN_BLOCKS = 8
RNN_BLOCK = D_RNN // N_RNN_BLOCKS
CONV_WIDTH = 4
LRU_C = 8.0
HEAD_DIM = 64
N_Q_HEADS = 16
N_KV_HEADS = 4
Q_PER_KV = N_Q_HEADS // N_KV_HEADS
D_ATTN = N_Q_HEADS * HEAD_DIM
D_KV = N_KV_HEADS * HEAD_DIM
WINDOW = 128
BLOCK = 128
ROPE_DIM = HEAD_DIM // 4
ROPE_THETA = 500000.0
NORM_EPS = 1e-6

OFF_RNN_X = 0
OFF_RNN_G = OFF_RNN_X + D_RNN
OFF_Q = OFF_RNN_G + D_RNN
OFF_K = OFF_Q + D_ATTN
OFF_V = OFF_K + D_KV
OFF_ATTN_G = OFF_V + D_KV
OFF_MERGE_R = OFF_ATTN_G + D_ATTN
OFF_MERGE_A = OFF_MERGE_R + D_MODEL
D_IN = OFF_MERGE_A + D_MODEL

kernel_name = "hybrid_rglru_swa_sink_gated_merge"


def rms_norm(x, g):
    xf = x.astype(jnp.float32)
    y = xf * lax.rsqrt(jnp.mean(xf * xf, axis=-1, keepdims=True) + NORM_EPS)
    return (y * g.astype(jnp.float32)).astype(x.dtype)


def causal_depthwise_conv(u, w, b):
    C = u.shape[-1]
    y = lax.conv_general_dilated(
        u, w.astype(u.dtype)[:, None, :], window_strides=(1,),
        padding=[(CONV_WIDTH - 1, 0)], dimension_numbers=("NWC", "WIO", "NWC"),
        feature_group_count=C)
    return y + b.astype(u.dtype)


def block_diag_linear(u, w, b):
    B, S, _ = u.shape
    ub = u.reshape(B, S, N_RNN_BLOCKS, RNN_BLOCK)
    y = jnp.einsum("bsnc,ncd->bsnd", ub, w.astype(u.dtype))
    return y.reshape(B, S, D_RNN) + b.astype(u.dtype)


def rg_lru(u, w_a, b_a, w_x, b_x, lam):
    r = jax.nn.sigmoid(block_diag_linear(u, w_a, b_a).astype(jnp.float32))
    i = jax.nn.sigmoid(block_diag_linear(u, w_x, b_x).astype(jnp.float32))
    log_a = -LRU_C * r * jax.nn.softplus(-lam.astype(jnp.float32))
    a = jnp.exp(log_a)
    b = jnp.sqrt(-jnp.expm1(2.0 * log_a)) * (i * u.astype(jnp.float32))

    def combine(c1, c2):
        a1, b1 = c1
        a2, b2 = c2
        return a1 * a2, a2 * b1 + b2

    _, h = lax.associative_scan(combine, (a, b), axis=1)
    return h.astype(u.dtype)


def rope_tables(seq_len):
    pos = jnp.arange(seq_len, dtype=jnp.float32)
    inv_freq = ROPE_THETA ** (-jnp.arange(0, ROPE_DIM, 2, dtype=jnp.float32) / ROPE_DIM)
    ang = pos[:, None] * inv_freq[None, :]
    return jnp.cos(ang)[:, None, :], jnp.sin(ang)[:, None, :]


def apply_partial_rope(t, cos, sin):
    half = ROPE_DIM // 2
    c = cos.astype(t.dtype)
    s = sin.astype(t.dtype)
    t1 = t[..., :half]
    t2 = t[..., half:ROPE_DIM]
    rot = jnp.concatenate([t1 * c - t2 * s, t2 * c + t1 * s], axis=-1)
    return jnp.concatenate([rot, t[..., ROPE_DIM:]], axis=-1)


def sliding_window_attention_with_sinks(q, k, v, sinks):
    B, S, _, _ = q.shape
    nb = S // BLOCK
    qb = q.reshape(B, nb, BLOCK, N_KV_HEADS, Q_PER_KV, HEAD_DIM)
    kb = k.reshape(B, nb, BLOCK, N_KV_HEADS, HEAD_DIM)
    vb = v.reshape(B, nb, BLOCK, N_KV_HEADS, HEAD_DIM)
    zeros = jnp.zeros_like(kb[:, :1])
    kw = jnp.concatenate([jnp.concatenate([zeros, kb[:, :-1]], axis=1), kb], axis=2)
    vw = jnp.concatenate([jnp.concatenate([zeros, vb[:, :-1]], axis=1), vb], axis=2)

    scale = 1.0 / math.sqrt(HEAD_DIM)
    scores = jnp.einsum("bnqkgd,bnjkd->bnkgqj", qb, kw).astype(jnp.float32) * scale

    blk = jnp.arange(nb)[:, None, None]
    q_abs = blk * BLOCK + jnp.arange(BLOCK)[None, :, None]
    k_abs = (blk - 1) * BLOCK + jnp.arange(2 * BLOCK)[None, None, :]
    valid = (k_abs <= q_abs) & (q_abs - k_abs < WINDOW) & (k_abs >= 0)
    scores = jnp.where(valid[None, :, None, None, :, :], scores, -jnp.inf)

    sink = sinks.astype(jnp.float32).reshape(N_KV_HEADS, Q_PER_KV)[None, None, :, :, None]
    m = jnp.maximum(jnp.max(scores, axis=-1), sink)
    p = jnp.exp(scores - m[..., None])
    denom = jnp.sum(p, axis=-1) + jnp.exp(sink - m)
    probs = (p / denom[..., None]).astype(v.dtype)
    out = jnp.einsum("bnkgqj,bnjkd->bnqkgd", probs, vw)
    return out.reshape(B, S, D_ATTN)


def _fwd_setup_inputs(seed: int = 0) -> dict:
    key = jax.random.key(seed)
    ks = jax.random.split(key, 17)
    f32 = jnp.float32
    x = jax.random.normal(ks[0], (BATCH, SEQ, D_MODEL), f32)
    norm_g = 1.0 + 0.02 * jax.random.normal(ks[1], (DEPTH, D_MODEL), f32)
    w_in = jax.random.normal(ks[2], (DEPTH, D_MODEL, D_IN), f32) * D_MODEL ** -0.5
    conv_w = jax.random.normal(ks[3], (DEPTH, CONV_WIDTH, D_RNN), f32) * CONV_WIDTH ** -0.5
    conv_b = 0.02 * jax.random.normal(ks[4], (DEPTH, D_RNN), f32)
    lru_w_a = jax.random.normal(ks[5], (DEPTH, N_RNN_BLOCKS, RNN_BLOCK, RNN_BLOCK), f32) * RNN_BLOCK ** -0.5
    lru_b_a = 0.02 * jax.random.normal(ks[6], (DEPTH, D_RNN), f32)
    lru_w_x = jax.random.normal(ks[7], (DEPTH, N_RNN_BLOCKS, RNN_BLOCK, RNN_BLOCK), f32) * RNN_BLOCK ** -0.5
    lru_b_x = 0.02 * jax.random.normal(ks[8], (DEPTH, D_RNN), f32)
    a0 = jax.random.uniform(ks[9], (DEPTH, D_RNN), f32, 0.9, 0.999)
    lru_lambda = jnp.log(a0) - jnp.log1p(-a0)
    attn_sinks = 0.5 * jax.random.normal(ks[10], (DEPTH, N_Q_HEADS), f32)
    w_rnn_out = jax.random.normal(ks[11], (DEPTH, D_RNN, D_MODEL), f32) * D_RNN ** -0.5
    w_attn_out = jax.random.normal(ks[12], (DEPTH, D_ATTN, D_MODEL), f32) * D_ATTN ** -0.5
    w_o = jax.random.normal(ks[13], (DEPTH, D_MODEL, D_MODEL), f32) * D_MODEL ** -0.5
    final_norm_g = 1.0 + 0.02 * jax.random.normal(ks[14], (D_MODEL,), f32)
    return {"x": x, "norm_g": norm_g, "w_in": w_in, "conv_w": conv_w, "conv_b": conv_b,
            "lru_w_a": lru_w_a, "lru_b_a": lru_b_a, "lru_w_x": lru_w_x, "lru_b_x": lru_b_x,
            "lru_lambda": lru_lambda, "attn_sinks": attn_sinks, "w_rnn_out": w_rnn_out,
            "w_attn_out": w_attn_out, "w_o": w_o, "final_norm_g": final_norm_g}


def _fwd_reference(x, norm_g, w_in, conv_w, conv_b, lru_w_a, lru_b_a, lru_w_x, lru_b_x,
              lru_lambda, attn_sinks, w_rnn_out, w_attn_out, w_o, final_norm_g):
    B, S, _ = x.shape
    cos, sin = rope_tables(S)
    for l in range(DEPTH):
        h = rms_norm(x, norm_g[l])
        proj = jnp.einsum("bsd,de->bse", h, w_in[l])
        u = proj[..., OFF_RNN_X:OFF_RNN_G]
        g_rnn = proj[..., OFF_RNN_G:OFF_Q]
        q = proj[..., OFF_Q:OFF_K].reshape(B, S, N_Q_HEADS, HEAD_DIM)
        k = proj[..., OFF_K:OFF_V].reshape(B, S, N_KV_HEADS, HEAD_DIM)
        v = proj[..., OFF_V:OFF_ATTN_G].reshape(B, S, N_KV_HEADS, HEAD_DIM)
        g_attn = proj[..., OFF_ATTN_G:OFF_MERGE_R]
        m_rnn = proj[..., OFF_MERGE_R:OFF_MERGE_A]
        m_attn = proj[..., OFF_MERGE_A:D_IN]

        u = causal_depthwise_conv(u, conv_w[l], conv_b[l])
        y_rnn = rg_lru(u, lru_w_a[l], lru_b_a[l], lru_w_x[l], lru_b_x[l], lru_lambda[l]) * jax.nn.silu(g_rnn)

        q = apply_partial_rope(q, cos, sin)
        k = apply_partial_rope(k, cos, sin)
        y_attn = sliding_window_attention_with_sinks(q, k, v, attn_sinks[l]) * jax.nn.silu(g_attn)

        merged = (jax.nn.sigmoid(m_rnn) * jnp.einsum("bsr,rd->bsd", y_rnn, w_rnn_out[l])
                  + jax.nn.sigmoid(m_attn) * jnp.einsum("bsa,ad->bsd", y_attn, w_attn_out[l]))
        x = x + jnp.einsum("bsd,de->bse", merged, w_o[l])
    return rms_norm(x, final_norm_g)


import jax as _jax
import jax.numpy as _jnp

TWIN_FORMAT = 'train_step'
FWD_PARAMS = ['x', 'norm_g', 'w_in', 'conv_w', 'conv_b', 'lru_w_a', 'lru_b_a', 'lru_w_x', 'lru_b_x', 'lru_lambda', 'attn_sinks', 'w_rnn_out', 'w_attn_out', 'w_o', 'final_norm_g']
TWIN_WEIGHTS = ['norm_g', 'w_in', 'conv_w', 'conv_b', 'lru_w_a', 'lru_b_a', 'lru_w_x', 'lru_b_x', 'lru_lambda', 'attn_sinks', 'w_rnn_out', 'w_attn_out', 'w_o', 'final_norm_g']
TWIN_DIFF_INPUT = 'x'
TWIN_INPUTS = ['x', 'norm_g', 'w_in', 'conv_w', 'conv_b', 'lru_w_a', 'lru_b_a', 'lru_w_x', 'lru_b_x', 'lru_lambda', 'attn_sinks', 'w_rnn_out', 'w_attn_out', 'w_o', 'final_norm_g', 'loss_target', 'm_norm_g', 'm_w_in', 'm_conv_w', 'm_conv_b', 'm_lru_w_a', 'm_lru_b_a', 'm_lru_w_x', 'm_lru_b_x', 'm_lru_lambda', 'm_attn_sinks', 'm_w_rnn_out', 'm_w_attn_out', 'm_w_o', 'm_final_norm_g', 'v_norm_g', 'v_w_in', 'v_conv_w', 'v_conv_b', 'v_lru_w_a', 'v_lru_b_a', 'v_lru_w_x', 'v_lru_b_x', 'v_lru_lambda', 'v_attn_sinks', 'v_w_rnn_out', 'v_w_attn_out', 'v_w_o', 'v_final_norm_g']
TWIN_OUTPUTS = ['loss', 'grad_x', 'grad_norm_g', 'grad_w_in', 'grad_conv_w', 'grad_conv_b', 'grad_lru_w_a', 'grad_lru_b_a', 'grad_lru_w_x', 'grad_lru_b_x', 'grad_lru_lambda', 'grad_attn_sinks', 'grad_w_rnn_out', 'grad_w_attn_out', 'grad_w_o', 'grad_final_norm_g', 'delta_norm_g', 'delta_w_in', 'delta_conv_w', 'delta_conv_b', 'delta_lru_w_a', 'delta_lru_b_a', 'delta_lru_w_x', 'delta_lru_b_x', 'delta_lru_lambda', 'delta_attn_sinks', 'delta_w_rnn_out', 'delta_w_attn_out', 'delta_w_o', 'delta_final_norm_g', 'new_m_norm_g', 'new_m_w_in', 'new_m_conv_w', 'new_m_conv_b', 'new_m_lru_w_a', 'new_m_lru_b_a', 'new_m_lru_w_x', 'new_m_lru_b_x', 'new_m_lru_lambda', 'new_m_attn_sinks', 'new_m_w_rnn_out', 'new_m_w_attn_out', 'new_m_w_o', 'new_m_final_norm_g', 'new_v_norm_g', 'new_v_w_in', 'new_v_conv_w', 'new_v_conv_b', 'new_v_lru_w_a', 'new_v_lru_b_a', 'new_v_lru_w_x', 'new_v_lru_b_x', 'new_v_lru_lambda', 'new_v_attn_sinks', 'new_v_w_rnn_out', 'new_v_w_attn_out', 'new_v_w_o', 'new_v_final_norm_g']
TWIN_LEAF_KINDS = {'loss': 'loss', 'grad_x': 'grad_x', 'grad_norm_g': 'grad_w', 'grad_w_in': 'grad_w', 'grad_conv_w': 'grad_w', 'grad_conv_b': 'grad_w', 'grad_lru_w_a': 'grad_w', 'grad_lru_b_a': 'grad_w', 'grad_lru_w_x': 'grad_w', 'grad_lru_b_x': 'grad_w', 'grad_lru_lambda': 'grad_w', 'grad_attn_sinks': 'grad_w', 'grad_w_rnn_out': 'grad_w', 'grad_w_attn_out': 'grad_w', 'grad_w_o': 'grad_w', 'grad_final_norm_g': 'grad_w', 'delta_norm_g': 'delta_w', 'delta_w_in': 'delta_w', 'delta_conv_w': 'delta_w', 'delta_conv_b': 'delta_w', 'delta_lru_w_a': 'delta_w', 'delta_lru_b_a': 'delta_w', 'delta_lru_w_x': 'delta_w', 'delta_lru_b_x': 'delta_w', 'delta_lru_lambda': 'delta_w', 'delta_attn_sinks': 'delta_w', 'delta_w_rnn_out': 'delta_w', 'delta_w_attn_out': 'delta_w', 'delta_w_o': 'delta_w', 'delta_final_norm_g': 'delta_w', 'new_m_norm_g': 'new_m', 'new_m_w_in': 'new_m', 'new_m_conv_w': 'new_m', 'new_m_conv_b': 'new_m', 'new_m_lru_w_a': 'new_m', 'new_m_lru_b_a': 'new_m', 'new_m_lru_w_x': 'new_m', 'new_m_lru_b_x': 'new_m', 'new_m_lru_lambda': 'new_m', 'new_m_attn_sinks': 'new_m', 'new_m_w_rnn_out': 'new_m', 'new_m_w_attn_out': 'new_m', 'new_m_w_o': 'new_m', 'new_m_final_norm_g': 'new_m', 'new_v_norm_g': 'new_v', 'new_v_w_in': 'new_v', 'new_v_conv_w': 'new_v', 'new_v_conv_b': 'new_v', 'new_v_lru_w_a': 'new_v', 'new_v_lru_b_a': 'new_v', 'new_v_lru_w_x': 'new_v', 'new_v_lru_b_x': 'new_v', 'new_v_lru_lambda': 'new_v', 'new_v_attn_sinks': 'new_v', 'new_v_w_rnn_out': 'new_v', 'new_v_w_attn_out': 'new_v', 'new_v_w_o': 'new_v', 'new_v_final_norm_g': 'new_v'}


def _forward(args):
    return _fwd_reference(*[args[k] for k in FWD_PARAMS])


def _output_shape():
    out = _jax.eval_shape(lambda: _forward(_fwd_setup_inputs(0)))
    return out.shape, out.dtype

N_MICROBATCH = 1
ADAM_LR = 0.001
ADAM_B1 = 0.9
ADAM_B2 = 0.999
ADAM_EPS = 1e-08
ADAM_WD = 0.01
ADAM_STEP = 10
PER_EXAMPLE_BATCH_AXIS = {'x': 0, 'loss_target': 0}
SHARED_INPUTS = []
_WEIGHT_DTYPES = {'norm_g': _jnp.float32, 'w_in': _jnp.float32, 'conv_w': _jnp.float32, 'conv_b': _jnp.float32, 'lru_w_a': _jnp.float32, 'lru_b_a': _jnp.float32, 'lru_w_x': _jnp.float32, 'lru_b_x': _jnp.float32, 'lru_lambda': _jnp.float32, 'attn_sinks': _jnp.float32, 'w_rnn_out': _jnp.float32, 'w_attn_out': _jnp.float32, 'w_o': _jnp.float32, 'final_norm_g': _jnp.float32}
MOMENT_SCALE = {'norm_g': 5.033427e-02, 'w_in': 2.017199e-02, 'conv_w': 3.396954e-02, 'conv_b': 1.503193e-01, 'lru_w_a': 8.187714e-03, 'lru_b_a': 8.517714e-03, 'lru_w_x': 1.395266e-02, 'lru_b_x': 1.288594e-02, 'lru_lambda': 1.744180e-02, 'attn_sinks': 7.870628e-03, 'w_rnn_out': 3.124209e-02, 'w_attn_out': 1.160199e-02, 'w_o': 3.203336e-02, 'final_norm_g': 3.198160e+01}


def _to_microbatches(a, axis):
    t = _jnp.moveaxis(a, axis, 0)
    t = t.reshape((N_MICROBATCH, t.shape[0] // N_MICROBATCH) + t.shape[1:])
    return _jnp.moveaxis(t, 1, axis + 1)


def setup_inputs(seed: int = 0) -> dict:
    inp = _fwd_setup_inputs(seed)
    key = _jax.random.fold_in(_jax.random.key(seed), 7919)
    shape, _ = _output_shape()
    out = dict(inp)
    out["loss_target"] = _jax.random.normal(_jax.random.fold_in(key, 0), shape, _jnp.float32)
    for i, name in enumerate(TWIN_WEIGHTS):
        w = inp[name].astype(_jnp.float32)
        if MOMENT_SCALE is None:
            s = _jnp.sqrt(_jnp.mean(_jnp.square(w)) + 1e-30)
        else:
            s = MOMENT_SCALE[name]
        km, kv = _jax.random.split(_jax.random.fold_in(key, i + 1))
        out[name] = w
        out["m_" + name] = s * _jax.random.normal(km, w.shape, _jnp.float32)
        out["v_" + name] = (s * s) * _jax.random.uniform(kv, w.shape, _jnp.float32, 0.5, 1.5)
    if N_MICROBATCH > 1:
        for name, axis in PER_EXAMPLE_BATCH_AXIS.items():
            out[name] = _to_microbatches(out[name], axis)
    return {'x': out['x'], 'norm_g': out['norm_g'], 'w_in': out['w_in'], 'conv_w': out['conv_w'], 'conv_b': out['conv_b'], 'lru_w_a': out['lru_w_a'], 'lru_b_a': out['lru_b_a'], 'lru_w_x': out['lru_w_x'], 'lru_b_x': out['lru_b_x'], 'lru_lambda': out['lru_lambda'], 'attn_sinks': out['attn_sinks'], 'w_rnn_out': out['w_rnn_out'], 'w_attn_out': out['w_attn_out'], 'w_o': out['w_o'], 'final_norm_g': out['final_norm_g'], 'loss_target': out['loss_target'], 'm_norm_g': out['m_norm_g'], 'm_w_in': out['m_w_in'], 'm_conv_w': out['m_conv_w'], 'm_conv_b': out['m_conv_b'], 'm_lru_w_a': out['m_lru_w_a'], 'm_lru_b_a': out['m_lru_b_a'], 'm_lru_w_x': out['m_lru_w_x'], 'm_lru_b_x': out['m_lru_b_x'], 'm_lru_lambda': out['m_lru_lambda'], 'm_attn_sinks': out['m_attn_sinks'], 'm_w_rnn_out': out['m_w_rnn_out'], 'm_w_attn_out': out['m_w_attn_out'], 'm_w_o': out['m_w_o'], 'm_final_norm_g': out['m_final_norm_g'], 'v_norm_g': out['v_norm_g'], 'v_w_in': out['v_w_in'], 'v_conv_w': out['v_conv_w'], 'v_conv_b': out['v_conv_b'], 'v_lru_w_a': out['v_lru_w_a'], 'v_lru_b_a': out['v_lru_b_a'], 'v_lru_w_x': out['v_lru_w_x'], 'v_lru_b_x': out['v_lru_b_x'], 'v_lru_lambda': out['v_lru_lambda'], 'v_attn_sinks': out['v_attn_sinks'], 'v_w_rnn_out': out['v_w_rnn_out'], 'v_w_attn_out': out['v_w_attn_out'], 'v_w_o': out['v_w_o'], 'v_final_norm_g': out['v_final_norm_g']}


def _loss(weights, diff, rest, loss_target):
    with _jax.named_scope("forward"):
        args = {**rest, TWIN_DIFF_INPUT: diff, **{k: w.astype(_WEIGHT_DTYPES[k]) for k, w in weights.items()}}
        y = _forward(args)
    with _jax.named_scope("loss_head"):
        err = _jnp.square(y.astype(_jnp.float32) - loss_target)
        return 0.5 * _jnp.sum(_jnp.mean(err, axis=-1)) if err.ndim else 0.5 * err


def _adamw(w, g, m, v):
    m = ADAM_B1 * m + (1.0 - ADAM_B1) * g
    v = ADAM_B2 * v + (1.0 - ADAM_B2) * _jnp.square(g)
    m_hat = m / (1.0 - ADAM_B1 ** ADAM_STEP)
    v_hat = v / (1.0 - ADAM_B2 ** ADAM_STEP)
    delta = -ADAM_LR * (m_hat / (_jnp.sqrt(v_hat) + ADAM_EPS) + ADAM_WD * w)
    return delta, m, v


def reference(x, norm_g, w_in, conv_w, conv_b, lru_w_a, lru_b_a, lru_w_x, lru_b_x, lru_lambda, attn_sinks, w_rnn_out, w_attn_out, w_o, final_norm_g, loss_target, m_norm_g, m_w_in, m_conv_w, m_conv_b, m_lru_w_a, m_lru_b_a, m_lru_w_x, m_lru_b_x, m_lru_lambda, m_attn_sinks, m_w_rnn_out, m_w_attn_out, m_w_o, m_final_norm_g, v_norm_g, v_w_in, v_conv_w, v_conv_b, v_lru_w_a, v_lru_b_a, v_lru_w_x, v_lru_b_x, v_lru_lambda, v_attn_sinks, v_w_rnn_out, v_w_attn_out, v_w_o, v_final_norm_g):
    given = dict(x=x, norm_g=norm_g, w_in=w_in, conv_w=conv_w, conv_b=conv_b, lru_w_a=lru_w_a, lru_b_a=lru_b_a, lru_w_x=lru_w_x, lru_b_x=lru_b_x, lru_lambda=lru_lambda, attn_sinks=attn_sinks, w_rnn_out=w_rnn_out, w_attn_out=w_attn_out, w_o=w_o, final_norm_g=final_norm_g, loss_target=loss_target, m_norm_g=m_norm_g, m_w_in=m_w_in, m_conv_w=m_conv_w, m_conv_b=m_conv_b, m_lru_w_a=m_lru_w_a, m_lru_b_a=m_lru_b_a, m_lru_w_x=m_lru_w_x, m_lru_b_x=m_lru_b_x, m_lru_lambda=m_lru_lambda, m_attn_sinks=m_attn_sinks, m_w_rnn_out=m_w_rnn_out, m_w_attn_out=m_w_attn_out, m_w_o=m_w_o, m_final_norm_g=m_final_norm_g, v_norm_g=v_norm_g, v_w_in=v_w_in, v_conv_w=v_conv_w, v_conv_b=v_conv_b, v_lru_w_a=v_lru_w_a, v_lru_b_a=v_lru_b_a, v_lru_w_x=v_lru_w_x, v_lru_b_x=v_lru_b_x, v_lru_lambda=v_lru_lambda, v_attn_sinks=v_attn_sinks, v_w_rnn_out=v_w_rnn_out, v_w_attn_out=v_w_attn_out, v_w_o=v_w_o, v_final_norm_g=v_final_norm_g)
    weights = {n: given[n] for n in TWIN_WEIGHTS}
    shared = {n: given[n] for n in SHARED_INPUTS}
    per_example = {n: given[n] for n in ['x']}
    grad_fn = _jax.value_and_grad(_loss, argnums=(0, 1))

    def one_microbatch(ex, loss_target):
        ex = dict(ex)
        diff = ex.pop(TWIN_DIFF_INPUT)
        return grad_fn(weights, diff, {**shared, **ex}, loss_target)

    if N_MICROBATCH == 1:
        loss, (grad_w, grad_x) = one_microbatch(per_example, given["loss_target"])
    else:
        def body(carry, xs):
            loss_sum, grad_sum = carry
            l_k, (gw_k, gx_k) = one_microbatch(xs[0], xs[1])
            with _jax.named_scope("update"):
                return (loss_sum + l_k, _jax.tree.map(_jnp.add, grad_sum, gw_k)), gx_k

        init = (_jnp.zeros((), _jnp.float32), _jax.tree.map(_jnp.zeros_like, weights))
        (loss, grad_w), grad_x = _jax.lax.scan(body, init, (per_example, given["loss_target"]))
    with _jax.named_scope("update"):
        delta_w, new_m, new_v = {}, {}, {}
        for n in TWIN_WEIGHTS:
            delta_w[n], new_m[n], new_v[n] = _adamw(weights[n], grad_w[n], given["m_" + n], given["v_" + n])
    return (loss, grad_x, *[grad_w[n] for n in TWIN_WEIGHTS], *[delta_w[n] for n in TWIN_WEIGHTS],
            *[new_m[n] for n in TWIN_WEIGHTS], *[new_v[n] for n in TWIN_WEIGHTS])
```

```python
import functools
import math

import jax
import jax.numpy as jnp
from jax import lax
from jax.experimental import pallas as pl
from jax.experimental.pallas import tpu as pltpu

f32 = jnp.float32
bf16 = jnp.bfloat16

D = 1024
D_IN = 6656
NDEV = 8
RNN_BLOCKS = 8
RB = 128
HEAD = 64
KV_HEADS = 4
GROUP = 4
QB = 128
LRU_C = 8.0
EPS = 1e-6
ROPE_DIM = 16
ROPE_THETA = 500000.0
CH = 512
SEC_START = (0, 2, 4, 6, 7, 9, 11)
SEC_CHUNKS = (2, 2, 2, 1, 2, 2, 2)
VMEM_LIMIT = 56 * 1024 * 1024

ADAM_LR, ADAM_B1, ADAM_B2, ADAM_EPS, ADAM_WD, ADAM_STEP = 0.001, 0.9, 0.999, 1e-08, 0.01, 10

MESH = pl.DeviceIdType.MESH
ANY = pl.BlockSpec(memory_space=pl.ANY)
VMEM_SPEC = pl.BlockSpec(memory_space=pltpu.VMEM)
SMEM_SPEC = pl.BlockSpec(memory_space=pltpu.SMEM)


def _pcall(body, **kw):
    return pl.pallas_call(body, **kw)


def _params(sem=None, **kw):
    if sem is not None:
        kw["dimension_semantics"] = sem
    return pltpu.CompilerParams(vmem_limit_bytes=VMEM_LIMIT, **kw)


def _sds(shape, dtype):
    return jax.ShapeDtypeStruct(shape, dtype)


def _dot(a, b, dims):
    return lax.dot_general(a, b, (dims, ((), ())), preferred_element_type=f32)


NN = ((1,), (0,))
NT = ((1,), (1,))
TN = ((0,), (0,))


def _sigmoid(v):
    return 1.0 / (1.0 + jnp.exp(-v))


def _my_place():
    return lax.axis_index("x"), lax.axis_index("y"), lax.axis_index("c")


def _gather_weights(wt, wr, wa, wo, cw):
    shards = (wt, wr, wa, wo, cw)
    nrows = tuple(a.shape[0] for a in shards)
    narr = len(shards)

    def body(wt_ref, wr_ref, wa_ref, wo_ref, cw_ref, o0, o1, o2, o3, o4, s0, s1, s2, s3,
             send_sems, recv_sems, local_sems):
        x, y, c = _my_place()
        me, sibling = (x, y, c), (x, y, 1 - c)
        chips = [(1 - x, y), (x, 1 - y), (1 - x, 1 - y)]
        ins = (wt_ref, wr_ref, wa_ref, wo_ref)
        stage = (s0, s1, s2, s3)
        for a in range(4):
            stage[a][...] = ins[a][...].astype(bf16)
        srcs = stage + (cw_ref,)
        outs = (o0, o1, o2, o3, o4)

        def rows(a, place):
            px, py, pc = place
            start = pl.multiple_of((4 * px + 2 * py + pc) * nrows[a], 8)
            return outs[a].at[pl.ds(start, nrows[a]), :]

        def copy(a, k, block, to, src=None):
            return pltpu.make_async_remote_copy(
                src_ref=rows(a, block) if src is None else src, dst_ref=rows(a, block),
                send_sem=send_sems.at[7 * a + k], recv_sem=recv_sems.at[7 * a + k],
                device_id=to, device_id_type=MESH)

        mine = [pltpu.make_async_copy(srcs[a], rows(a, me), local_sems.at[a]) for a in range(narr)]
        for cp in mine:
            cp.start()
        first = []
        for a in range(narr):
            first.append(copy(a, 0, me, sibling, src=srcs[a]))
            first += [copy(a, 1 + j, me, (*chip, c), src=srcs[a]) for j, chip in enumerate(chips)]
        for cp in first:
            cp.start()
        passed = []
        for j, chip in enumerate(chips):
            for a in range(narr):
                copy(a, 1 + j, (*chip, c), me).wait_recv()
                fwd = copy(a, 4 + j, (*chip, c), sibling)
                fwd.start()
                passed.append(fwd)
        for a in range(narr):
            copy(a, 0, sibling, me).wait_recv()
        for j, chip in enumerate(chips):
            for a in range(narr):
                copy(a, 4 + j, (*chip, 1 - c), me).wait_recv()
        for cp in first + passed:
            cp.wait_send()
        for cp in mine:
            cp.wait()

    out_shape = tuple(_sds((NDEV * r, a.shape[1]), bf16 if i < 4 else f32)
                      for i, (r, a) in enumerate(zip(nrows, shards)))
    return _pcall(
        body, name="gather_weights", out_shape=out_shape,
        in_specs=[VMEM_SPEC] * narr, out_specs=tuple([ANY] * narr),
        scratch_shapes=[pltpu.VMEM(a.shape, bf16) for a in shards[:4]] + [
            pltpu.SemaphoreType.DMA((7 * narr,)), pltpu.SemaphoreType.DMA((7 * narr,)),
            pltpu.SemaphoreType.DMA((narr,))],
        compiler_params=_params(),
    )(*shards)


def _gather_rows(blk, name):
    nrows, ncols = blk.shape

    def body(src, out, send_sems, recv_sems, local_sem):
        x, y, c = _my_place()
        me, sibling = (x, y, c), (x, y, 1 - c)
        chips = [(1 - x, y), (x, 1 - y), (1 - x, 1 - y)]

        def rows(place):
            px, py, pc = place
            start = pl.multiple_of((4 * px + 2 * py + pc) * nrows, 8)
            return out.at[pl.ds(start, nrows), :]

        def copy(k, block, to, from_src=False):
            return pltpu.make_async_remote_copy(
                src_ref=src if from_src else rows(block), dst_ref=rows(block),
                send_sem=send_sems.at[k], recv_sem=recv_sems.at[k], device_id=to, device_id_type=MESH)

        mine = pltpu.make_async_copy(src, rows(me), local_sem)
        mine.start()
        first = [copy(0, me, sibling, True)]
        first += [copy(1 + j, me, (*chip, c), True) for j, chip in enumerate(chips)]
        for cp in first:
            cp.start()
        passed = []
        for j, chip in enumerate(chips):
            copy(1 + j, (*chip, c), me).wait_recv()
            fwd = copy(4 + j, (*chip, c), sibling)
            fwd.start()
            passed.append(fwd)
        copy(0, sibling, me).wait_recv()
        for j, chip in enumerate(chips):
            copy(4 + j, (*chip, 1 - c), me).wait_recv()
        for cp in first + passed:
            cp.wait_send()
        mine.wait()

    return _pcall(
        body, name=name, out_shape=_sds((NDEV * nrows, ncols), blk.dtype),
        in_specs=[ANY], out_specs=ANY,
        scratch_shapes=[pltpu.SemaphoreType.DMA((7,)), pltpu.SemaphoreType.DMA((7,)), pltpu.SemaphoreType.DMA],
        compiler_params=_params(),
    )(blk)


def _scatter_partials(grads):
    narr = len(grads)
    nrows = tuple(g.shape[0] // NDEV for g in grads)

    def body(*refs):
        gin = refs[:narr]
        slabs = refs[narr:2 * narr]
        send_sems, recv_sems, local_sems = refs[2 * narr:]
        x, y, c = _my_place()
        my_dev = 4 * x + 2 * y + c
        peers = []
        for k in range(1, NDEV):
            kx, ky, kc = (k >> 2) & 1, (k >> 1) & 1, k & 1
            peers.append(((x + kx) % 2, (y + ky) % 2, (c + kc) % 2))

        def shard(a, dev):
            return gin[a].at[pl.ds(pl.multiple_of(dev * nrows[a], 8), nrows[a]), :]

        def copy(a, k, peer):
            px, py, pc = peer
            return pltpu.make_async_remote_copy(
                src_ref=shard(a, 4 * px + 2 * py + pc), dst_ref=slabs[a].at[my_dev],
                send_sem=send_sems.at[7 * a + k], recv_sem=recv_sems.at[7 * a + k],
                device_id=peer, device_id_type=MESH)

        def landing(a, k, peer):
            px, py, pc = peer
            return pltpu.make_async_remote_copy(
                src_ref=shard(a, my_dev), dst_ref=slabs[a].at[4 * px + 2 * py + pc],
                send_sem=send_sems.at[7 * a + k], recv_sem=recv_sems.at[7 * a + k],
                device_id=peer, device_id_type=MESH)

        mine = [pltpu.make_async_copy(shard(a, my_dev), slabs[a].at[my_dev], local_sems.at[a]) for a in range(narr)]
        for cp in mine:
            cp.start()
        sends = [copy(a, k, peer) for k, peer in enumerate(peers) for a in range(narr)]
        for cp in sends:
            cp.start()
        for k, peer in enumerate(peers):
            for a in range(narr):
                landing(a, k, peer).wait_recv()
        for cp in sends:
            cp.wait_send()
        for cp in mine:
            cp.wait()

    out_shape = tuple(_sds((NDEV, r, g.shape[1]), f32) for r, g in zip(nrows, grads))
    return _pcall(
        body, name="scatter_partials", out_shape=out_shape,
        in_specs=[ANY] * narr, out_specs=tuple([ANY] * narr),
        scratch_shapes=[pltpu.SemaphoreType.DMA((7 * narr,)), pltpu.SemaphoreType.DMA((7 * narr,)),
                        pltpu.SemaphoreType.DMA((narr,))],
        compiler_params=_params(),
    )(*grads)


def _adamw(parts, w, m, v, name):
    n, rows, cols = parts.shape
    tr = rows
    for cand in (256, 128, 104, 64, 40, 32, 16, 8):
        if rows % cand == 0:
            tr = cand
            break

    def body(p_ref, w_ref, m_ref, v_ref, g_out, d_out, m_out, v_out):
        g = p_ref[0]
        for s in range(1, n):
            g = g + p_ref[s]
        m_new = ADAM_B1 * m_ref[...] + (1.0 - ADAM_B1) * g
        v_new = ADAM_B2 * v_ref[...] + (1.0 - ADAM_B2) * (g * g)
        m_hat = m_new / (1.0 - ADAM_B1 ** ADAM_STEP)
        v_hat = v_new / (1.0 - ADAM_B2 ** ADAM_STEP)
        g_out[...] = g
        d_out[...] = -ADAM_LR * (m_hat / (jnp.sqrt(v_hat) + ADAM_EPS) + ADAM_WD * w_ref[...])
        m_out[...] = m_new
        v_out[...] = v_new

    blk = pl.BlockSpec((tr, cols), lambda i: (i, 0))
    return _pcall(
        body, name=name, grid=(rows // tr,),
        in_specs=[pl.BlockSpec((n, tr, cols), lambda i: (0, i, 0)), blk, blk, blk],
        out_specs=(blk, blk, blk, blk), out_shape=tuple(_sds((rows, cols), f32) for _ in range(4)),
        compiler_params=_params(("arbitrary",)),
    )(parts, w, m, v)


def _in_proj(x2d, norm_g, wt_full):
    T = x2d.shape[0]
    tb = min(T, 1024)

    def body(x_ref, g_ref, wt_ref, h_ref, proj_ref):
        @pl.when(pl.program_id(1) == 0)
        def _():
            xv = x_ref[...]
            ms = jnp.mean(xv * xv, axis=-1, keepdims=True)
            h_ref[...] = (xv * lax.rsqrt(ms + EPS) * g_ref[...]).astype(bf16)
        proj_ref[...] = _dot(h_ref[...], wt_ref[...], NT)

    return _pcall(
        body, name="in_proj", grid=(T // tb, D_IN // CH),
        in_specs=[pl.BlockSpec((tb, D), lambda i, j: (i, 0)), pl.BlockSpec((1, D), lambda i, j: (0, 0)),
                  pl.BlockSpec((CH, D), lambda i, j: (j, 0))],
        out_specs=(pl.BlockSpec((tb, D), lambda i, j: (i, 0)), pl.BlockSpec((tb, CH), lambda i, j: (i, j))),
        out_shape=(_sds((T, D), bf16), _sds((T, D_IN), f32)),
        compiler_params=_params(("arbitrary", "arbitrary")),
    )(x2d, norm_g, wt_full)


def _rows_iota(shape):
    return lax.broadcasted_iota(jnp.int32, shape, 0)


def _shift_down(v, k):
    return jnp.where(_rows_iota(v.shape) >= k, pltpu.roll(v, k, 0), 0.0)


def _shift_up(v, k):
    n = v.shape[0]
    return jnp.where(_rows_iota(v.shape) < n - k, pltpu.roll(v, n - k, 0), 0.0)


def _scan_forward(a, b):
    n = a.shape[0]
    r = _rows_iota(a.shape)
    s = 1
    while s < n:
        keep = r >= s
        b = jnp.where(keep, a * pltpu.roll(b, s, 0) + b, b)
        if 2 * s < n:
            a = jnp.where(keep, a * pltpu.roll(a, s, 0), a)
        s *= 2
    return b


def _scan_reverse(a, b):
    n = a.shape[0]
    r = _rows_iota(a.shape)
    s = 1
    while s < n:
        keep = r < n - s
        b = jnp.where(keep, a * pltpu.roll(b, n - s, 0) + b, b)
        if 2 * s < n:
            a = jnp.where(keep, a * pltpu.roll(a, n - s, 0), a)
        s *= 2
    return b


def _neg_expm1(v):
    series = -v * (1.0 + v * (0.5 + v * (1.0 / 6.0 + v * (1.0 / 24.0))))
    return jnp.where(v > -0.03125, series, 1.0 - jnp.exp(v))


def _softplus_neg(lam):
    return jnp.maximum(-lam, 0.0) + jnp.log(1.0 + jnp.exp(-jnp.abs(lam)))


def _lru_gates(x0, cw, cb, wa, ba, wx, bx, lam):
    u = cb + cw[3:4, :] * x0
    for k in range(3):
        u = u + cw[k:k + 1, :] * _shift_down(x0, 3 - k)
    ub = u.astype(bf16)
    r = _sigmoid(_dot(ub, wa.astype(bf16), NN) + ba)
    i = _sigmoid(_dot(ub, wx.astype(bf16), NN) + bx)
    sp = _softplus_neg(lam)
    log_a = (-LRU_C) * r * sp
    a = jnp.exp(log_a)
    mult = jnp.sqrt(_neg_expm1(2.0 * log_a))
    return u, ub, r, i, sp, a, mult


def _lru_specs(S, nb):
    col = lambda off: pl.BlockSpec((S, RB), lambda n, b, off=off: (b, off + n))
    vec = pl.BlockSpec((1, RB), lambda n, b: (0, n))
    wblk = pl.BlockSpec((None, RB, RB), lambda n, b: (n, 0, 0))
    cwblk = pl.BlockSpec((8, RB), lambda n, b: (n, 0))
    return col, vec, wblk, cwblk


def _lru_forward(proj, cw_full, conv_b, w_a, b_a, w_x, b_x, lam, S):
    T = proj.shape[0]
    nb = T // S
    col, vec, wblk, cwblk = _lru_specs(S, nb)

    def body(x0_ref, g_ref, cw_ref, cb_ref, wa_ref, ba_ref, wx_ref, bx_ref, lam_ref, y_ref, h_ref):
        x0 = x0_ref[...]
        u, ub, r, i, sp, a, mult = _lru_gates(x0, cw_ref[...], cb_ref[...], wa_ref[...], ba_ref[...],
                                              wx_ref[...], bx_ref[...], lam_ref[...])
        h = _scan_forward(a, mult * (i * u))
        g = g_ref[...]
        h_ref[...] = h
        y_ref[...] = (h * (g * _sigmoid(g))).astype(bf16)

    out = pl.BlockSpec((S, RB), lambda n, b: (b, n))
    return _pcall(
        body, name="lru_forward", grid=(RNN_BLOCKS, nb),
        in_specs=[col(0), col(8), cwblk, vec, wblk, vec, wblk, vec, vec],
        out_specs=(out, out), out_shape=(_sds((T, D), bf16), _sds((T, D), f32)),
        compiler_params=_params(("arbitrary", "arbitrary")),
    )(proj, proj, cw_full, conv_b, w_a, b_a, w_x, b_x, lam)


def _rope_tables(S):
    pos = jnp.arange(S, dtype=f32)
    inv_freq = ROPE_THETA ** (-jnp.arange(0, ROPE_DIM, 2, dtype=f32) / ROPE_DIM)
    ang = pos[:, None] * inv_freq[None, :]
    cos, sin = jnp.cos(ang), jnp.sin(ang)
    lane = jnp.arange(256) % HEAD
    cosl, sinl = cos[:, lane % 8], sin[:, lane % 8]
    c = jnp.where(lane[None, :] < ROPE_DIM, cosl, 1.0)
    s1 = jnp.where(lane[None, :] < 8, -sinl, 0.0)
    s2 = jnp.where((lane[None, :] >= 8) & (lane[None, :] < ROPE_DIM), sinl, 0.0)
    return c.astype(f32), s1.astype(f32), s2.astype(f32)


def _rope(t, c, s1, s2):
    w = t.shape[1]
    return t * c + pltpu.roll(t, w - 8, 1) * s1 + pltpu.roll(t, 8, 1) * s2


def _rope_transposed(dt, c, s1, s2):
    w = dt.shape[1]
    return dt * c + pltpu.roll(dt * s1, 8, 1) + pltpu.roll(dt * s2, w - 8, 1)


def _heads_to_rows(t):
    return jnp.concatenate([t[:, HEAD * h:HEAD * (h + 1)] for h in range(GROUP)], axis=0)


def _rows_to_heads(t):
    return jnp.concatenate([t[QB * h:QB * (h + 1), :] for h in range(GROUP)], axis=1)


def _attn_probs(q_rows, k_cat, sink_col, first_block):
    s = _dot(q_rows, k_cat, NT) * (1.0 / math.sqrt(HEAD))
    qi = _rows_iota(s.shape) % QB
    cj = lax.broadcasted_iota(jnp.int32, s.shape, 1)
    valid = (cj > qi) & (cj <= qi + QB) & ((cj >= QB) | jnp.logical_not(first_block))
    s = jnp.where(valid, s, -jnp.inf)
    m = jnp.maximum(jnp.max(s, axis=1, keepdims=True), sink_col)
    p = jnp.exp(s - m)
    e_sink = jnp.exp(sink_col - m)
    den = jnp.sum(p, axis=1, keepdims=True) + e_sink
    return p / den, e_sink / den


def _sink_column(sink_ref, kv):
    rid = _rows_iota((GROUP * QB, 1))
    col = jnp.zeros((GROUP * QB, 1), f32)
    for h in range(GROUP):
        col = jnp.where(rid // QB == h, sink_ref[0, GROUP * kv + h], col)
    return col


def _attn_in_specs(S):
    nq = S // QB
    last = nq - 1
    cur = lambda b, j: b * nq + jnp.minimum(j, last)
    prev = lambda b, j: b * nq + jnp.maximum(jnp.minimum(j, last) - 1, 0)
    tab_cur = lambda b, j: (jnp.minimum(j, last), 0)
    tab_prev = lambda b, j: (jnp.maximum(jnp.minimum(j, last) - 1, 0), 0)
    specs = [
        pl.BlockSpec((QB, D), lambda b, j: (cur(b, j), 2)),
        pl.BlockSpec((QB, 256), lambda b, j: (cur(b, j), 12)),
        pl.BlockSpec((QB, 256), lambda b, j: (prev(b, j), 12)),
        pl.BlockSpec((QB, 256), lambda b, j: (cur(b, j), 13)),
        pl.BlockSpec((QB, 256), lambda b, j: (prev(b, j), 13)),
        pl.BlockSpec((QB, 512), lambda b, j: (cur(b, j), 7)),
        pl.BlockSpec((QB, 512), lambda b, j: (cur(b, j), 8)),
        pl.BlockSpec((QB, 256), tab_cur), pl.BlockSpec((QB, 256), tab_cur), pl.BlockSpec((QB, 256), tab_cur),
        pl.BlockSpec((QB, 256), tab_prev), pl.BlockSpec((QB, 256), tab_prev), pl.BlockSpec((QB, 256), tab_prev),
        SMEM_SPEC,
    ]
    return specs, cur, prev


def _attn_forward(proj, tabs, sinks, S):
    T = proj.shape[0]
    nb, nq = T // S, S // QB
    specs, cur, _ = _attn_in_specs(S)

    def body(q_ref, kc_ref, kp_ref, vc_ref, vp_ref, gl_ref, gh_ref, cc, s1c, s2c, cp, s1p, s2p, sink_ref, y_ref):
        first = pl.program_id(1) == 0
        tc = (cc[...], s1c[...], s2c[...])
        kc = _rope(kc_ref[...], *tc)
        kp = _rope(kp_ref[...], cp[...], s1p[...], s2p[...])
        vc, vp = vc_ref[...], vp_ref[...]
        for kv in range(KV_HEADS):
            lanes = slice(256 * kv, 256 * (kv + 1))
            hl = slice(HEAD * kv, HEAD * (kv + 1))
            q_rows = _heads_to_rows(_rope(q_ref[:, lanes], *tc)).astype(bf16)
            k_cat = jnp.concatenate([kp[:, hl], kc[:, hl]], axis=0).astype(bf16)
            v_cat = jnp.concatenate([vp[:, hl], vc[:, hl]], axis=0).astype(bf16)
            probs, _ = _attn_probs(q_rows, k_cat, _sink_column(sink_ref, kv), first)
            o = _rows_to_heads(_dot(probs.astype(bf16), v_cat, NN))
            g_src = gl_ref if kv < 2 else gh_ref
            g = g_src[:, 256 * (kv % 2):256 * (kv % 2 + 1)]
            y_ref[:, lanes] = (o * (g * _sigmoid(g))).astype(bf16)

    args = [proj] * 7 + list(tabs) + list(tabs) + [sinks]
    return _pcall(
        body, name="attn_forward", grid=(nb, nq), in_specs=specs,
        out_specs=pl.BlockSpec((QB, D), lambda b, j: (cur(b, j), 0)), out_shape=_sds((T, D), bf16),
        compiler_params=_params(("arbitrary", "arbitrary")),
    )(*args)


def _merge_and_head(x2d, tgt, proj, y_rnn, y_attn, w_r, w_a, w_o, gfin):
    T = x2d.shape[0]
    tb = min(T, 256)
    nsteps = T // tb

    def body(x_ref, t_ref, mr0, mr1, ma0, ma1, yr_ref, ya_ref, wr_ref, wa_ref, wo_ref, gf_ref,
             merged_ref, dx2_ref, dpr_ref, dpa_ref, dyr_ref, dya_ref, dmr_ref, dma_ref, loss_ref, gfin_ref):
        @pl.when(pl.program_id(0) == 0)
        def _():
            loss_ref[...] = jnp.zeros_like(loss_ref)
            gfin_ref[...] = jnp.zeros_like(gfin_ref)

        sr = _sigmoid(jnp.concatenate([mr0[...], mr1[...]], axis=1))
        sa = _sigmoid(jnp.concatenate([ma0[...], ma1[...]], axis=1))
        p_r = _dot(yr_ref[...], wr_ref[...], NN)
        p_a = _dot(ya_ref[...], wa_ref[...], NN)
        merged = (sr * p_r + sa * p_a).astype(bf16)
        merged_ref[...] = merged
        x2 = x_ref[...] + _dot(merged, wo_ref[...], NN)
        rstd = lax.rsqrt(jnp.mean(x2 * x2, axis=-1, keepdims=True) + EPS)
        xh = x2 * rstd
        gf = gf_ref[...]
        err = xh * gf - t_ref[...]
        loss_ref[...] += jnp.sum(err * err)
        dy = err * (1.0 / D)
        gfin_ref[0:1, :] += jnp.sum(dy * xh, axis=0, keepdims=True)
        dxn = dy * gf
        dx2 = rstd * (dxn - xh * jnp.mean(dxn * xh, axis=-1, keepdims=True))
        dx2_ref[...] = dx2
        dmerged = _dot(dx2.astype(bf16), wo_ref[...], NT)
        dmr_ref[...] = (dmerged * p_r * (sr * (1.0 - sr))).astype(bf16)
        dma_ref[...] = (dmerged * p_a * (sa * (1.0 - sa))).astype(bf16)
        dpr = (dmerged * sr).astype(bf16)
        dpa = (dmerged * sa).astype(bf16)
        dpr_ref[...] = dpr
        dpa_ref[...] = dpa
        dyr_ref[...] = _dot(dpr, wr_ref[...], NT)
        dya_ref[...] = _dot(dpa, wa_ref[...], NT)

    tok = pl.BlockSpec((tb, D), lambda i: (i, 0))
    half = lambda c: pl.BlockSpec((tb, CH), lambda i, c=c: (i, c))
    wfull = pl.BlockSpec((D, D), lambda i: (0, 0))
    acc = pl.BlockSpec((8, D), lambda i: (0, 0))
    return _pcall(
        body, name="merge_and_head", grid=(nsteps,),
        in_specs=[tok, tok, half(9), half(10), half(11), half(12), tok, tok, wfull, wfull, wfull,
                  pl.BlockSpec((1, D), lambda i: (0, 0))],
        out_specs=(tok, tok, tok, tok, tok, tok, tok, tok, acc, acc),
        out_shape=(_sds((T, D), bf16), _sds((T, D), f32), _sds((T, D), bf16), _sds((T, D), bf16),
                   _sds((T, D), f32), _sds((T, D), f32), _sds((T, D), bf16), _sds((T, D), bf16),
                   _sds((8, D), f32), _sds((8, D), f32)),
        compiler_params=_params(("arbitrary",)),
    )(x2d, tgt, proj, proj, proj, proj, y_rnn, y_attn, w_r, w_a, w_o, gfin)


def _attn_backward(proj, dy_attn, tabs, sinks, S):
    T = proj.shape[0]
    nb, nq = T // S, S // QB
    specs, cur, prev = _attn_in_specs(S)
    specs = specs + [pl.BlockSpec((QB, D), lambda b, j: (cur(b, j), 0))]

    def body(q_ref, kc_ref, kp_ref, vc_ref, vp_ref, gl_ref, gh_ref, cc, s1c, s2c, cp, s1p, s2p, sink_ref, dy_ref,
             dq_ref, dkv_ref, dg_ref, dsink_ref, carry_k, carry_v):
        b, j = pl.program_id(0), pl.program_id(1)

        @pl.when((b == 0) & (j == 0))
        def _():
            dsink_ref[...] = jnp.zeros_like(dsink_ref)

        @pl.when(j == 0)
        def _():
            carry_k[...] = jnp.zeros_like(carry_k)
            carry_v[...] = jnp.zeros_like(carry_v)

        @pl.when(j < nq)
        def _():
            first = j == 0
            tc = (cc[...], s1c[...], s2c[...])
            tp = (cp[...], s1p[...], s2p[...])
            kc = _rope(kc_ref[...], *tc)
            kp = _rope(kp_ref[...], *tp)
            vc, vp = vc_ref[...], vp_ref[...]
            dk_prev, dk_cur, dv_prev, dv_cur = [], [], [], []
            dsink_acc = jnp.zeros((8, 128), f32)
            r8 = lax.broadcasted_iota(jnp.int32, (8, 128), 0)
            l8 = lax.broadcasted_iota(jnp.int32, (8, 128), 1)
            for kv in range(KV_HEADS):
                lanes = slice(256 * kv, 256 * (kv + 1))
                hl = slice(HEAD * kv, HEAD * (kv + 1))
                q_rows = _heads_to_rows(_rope(q_ref[:, lanes], *tc)).astype(bf16)
                k_cat = jnp.concatenate([kp[:, hl], kc[:, hl]], axis=0).astype(bf16)
                v_cat = jnp.concatenate([vp[:, hl], vc[:, hl]], axis=0).astype(bf16)
                probs, p_sink = _attn_probs(q_rows, k_cat, _sink_column(sink_ref, kv), first)
                pb = probs.astype(bf16)
                o = _rows_to_heads(_dot(pb, v_cat, NN))
                g_src = gl_ref if kv < 2 else gh_ref
                g = g_src[:, 256 * (kv % 2):256 * (kv % 2 + 1)]
                sg = _sigmoid(g)
                dy = dy_ref[:, lanes]
                dg_ref[:, lanes] = (dy * o * (sg * (1.0 + g * (1.0 - sg)))).astype(bf16)
                do_rows = _heads_to_rows(dy * (g * sg)).astype(bf16)
                dv = _dot(pb, do_rows, TN)
                dp = _dot(do_rows, v_cat, NT)
                rowdot = jnp.sum(probs * dp, axis=1, keepdims=True)
                ds = (probs * (dp - rowdot) * (1.0 / math.sqrt(HEAD))).astype(bf16)
                sink_rows = -(p_sink * rowdot)
                for h in range(GROUP):
                    val = jnp.sum(sink_rows[QB * h:QB * (h + 1), :])
                    dsink_acc = dsink_acc + jnp.where((r8 == 0) & (l8 == GROUP * kv + h), val, 0.0)
                dq = _rows_to_heads(_dot(ds, k_cat, NN))
                dq_ref[:, lanes] = _rope_transposed(dq, *tc).astype(bf16)
                dk = _dot(ds, q_rows, TN)
                dk_prev.append(dk[:QB, :])
                dk_cur.append(dk[QB:, :])
                dv_prev.append(dv[:QB, :])
                dv_cur.append(dv[QB:, :])
            dsink_ref[...] += dsink_acc
            dkp = _rope_transposed(jnp.concatenate(dk_prev, axis=1), *tp)
            dkc = _rope_transposed(jnp.concatenate(dk_cur, axis=1), *tc)
            dkv_ref[:, 0:256] = (carry_k[...] + dkp).astype(bf16)
            dkv_ref[:, 256:512] = (carry_v[...] + jnp.concatenate(dv_prev, axis=1)).astype(bf16)
            carry_k[...] = dkc
            carry_v[...] = jnp.concatenate(dv_cur, axis=1)

        @pl.when(j == nq)
        def _():
            dkv_ref[:, 0:256] = carry_k[...].astype(bf16)
            dkv_ref[:, 256:512] = carry_v[...].astype(bf16)

    lag = lambda b, j: (b * nq + jnp.maximum(j - 1, 0), 0)
    args = [proj] * 7 + list(tabs) + list(tabs) + [sinks, dy_attn]
    return _pcall(
        body, name="attn_backward", grid=(nb, nq + 1), in_specs=specs,
        out_specs=(pl.BlockSpec((QB, D), lambda b, j: (cur(b, j), 0)), pl.BlockSpec((QB, 512), lag),
                   pl.BlockSpec((QB, D), lambda b, j: (cur(b, j), 0)), pl.BlockSpec((8, 128), lambda b, j: (0, 0))),
        out_shape=(_sds((T, D), bf16), _sds((T, 512), bf16), _sds((T, D), bf16), _sds((8, 128), f32)),
        scratch_shapes=[pltpu.VMEM((QB, 256), f32), pltpu.VMEM((QB, 256), f32)],
        compiler_params=_params(("arbitrary", "arbitrary")),
    )(*args)


def _lru_backward(proj, h_all, dy_rnn, cw_full, conv_b, w_a, b_a, w_x, b_x, lam, S):
    T = proj.shape[0]
    nb = T // S
    col, vec, wblk, cwblk = _lru_specs(S, nb)
    tokblk = pl.BlockSpec((S, RB), lambda n, b: (b, n))

    def body(x0_ref, g_ref, h_ref, dy_ref, cw_ref, cb_ref, wa_ref, ba_ref, wx_ref, bx_ref, lam_ref,
             du0_ref, dg_ref, gwa_ref, gwx_ref, vec_ref, gcw_ref):
        @pl.when(pl.program_id(1) == 0)
        def _():
            gwa_ref[...] = jnp.zeros_like(gwa_ref)
            gwx_ref[...] = jnp.zeros_like(gwx_ref)
            vec_ref[...] = jnp.zeros_like(vec_ref)
            gcw_ref[...] = jnp.zeros_like(gcw_ref)

        x0 = x0_ref[...]
        cw = cw_ref[...]
        lam_v = lam_ref[...]
        u, ub, r, i, sp, a, mult = _lru_gates(x0, cw, cb_ref[...], wa_ref[...], ba_ref[...],
                                              wx_ref[...], bx_ref[...], lam_v)
        h = h_ref[...]
        g = g_ref[...]
        dy = dy_ref[...]
        sg = _sigmoid(g)
        dg_ref[...] = (dy * h * (sg * (1.0 + g * (1.0 - sg)))).astype(bf16)
        dh_total = _scan_reverse(_shift_up(a, 1), dy * (g * sg))
        da = dh_total * _shift_down(h, 1)
        iu = i * u
        dmult = dh_total * iu
        di = dh_total * mult * u
        du = dh_total * mult * i
        dlog_a = a * (da - dmult * a / mult)
        dr = dlog_a * ((-LRU_C) * sp)
        dsp = jnp.sum(dlog_a * ((-LRU_C) * r), axis=0, keepdims=True)
        dpre_r = dr * r * (1.0 - r)
        dpre_i = di * i * (1.0 - i)
        dpre_rb = dpre_r.astype(bf16)
        dpre_ib = dpre_i.astype(bf16)
        du = du + _dot(dpre_rb, wa_ref[...].astype(bf16), NT) + _dot(dpre_ib, wx_ref[...].astype(bf16), NT)
        gwa_ref[...] += _dot(ub, dpre_rb, TN)
        gwx_ref[...] += _dot(ub, dpre_ib, TN)
        vec_ref[0:1, :] += jnp.sum(du, axis=0, keepdims=True)
        vec_ref[1:2, :] += jnp.sum(dpre_r, axis=0, keepdims=True)
        vec_ref[2:3, :] += jnp.sum(dpre_i, axis=0, keepdims=True)
        vec_ref[3:4, :] += dsp * (-_sigmoid(-lam_v))
        dx0 = cw[3:4, :] * du
        gcw_ref[3:4, :] += jnp.sum(du * x0, axis=0, keepdims=True)
        for k in range(3):
            dx0 = dx0 + cw[k:k + 1, :] * _shift_up(du, 3 - k)
            gcw_ref[k:k + 1, :] += jnp.sum(du * _shift_down(x0, 3 - k), axis=0, keepdims=True)
        du0_ref[...] = dx0.astype(bf16)

    wacc = pl.BlockSpec((RB, RB), lambda n, b: (0, n))
    vacc = pl.BlockSpec((8, RB), lambda n, b: (0, n))
    cacc = pl.BlockSpec((8, RB), lambda n, b: (n, 0))
    return _pcall(
        body, name="lru_backward", grid=(RNN_BLOCKS, nb),
        in_specs=[col(0), col(8), tokblk, tokblk, cwblk, vec, wblk, vec, wblk, vec, vec],
        out_specs=(tokblk, tokblk, wacc, wacc, vacc, cacc),
        out_shape=(_sds((T, D), bf16), _sds((T, D), bf16), _sds((RB, D), f32), _sds((RB, D), f32),
                   _sds((8, D), f32), _sds((8 * RNN_BLOCKS, RB), f32)),
        compiler_params=_params(("arbitrary", "arbitrary")),
    )(proj, proj, h_all, dy_rnn, cw_full, conv_b, w_a, b_a, w_x, b_x, lam)


def _section_of_chunk(s):
    out = []
    for start, n in zip(SEC_START, SEC_CHUNKS):
        inside = (s >= start) & (s < start + n)
        out.append((inside, jnp.clip(s - start, 0, n - 1)))
    return out


def _input_grad(dsecs, wt_full, x2d, dx2, norm_g):
    T = x2d.shape[0]
    tb = min(T, 1024)
    nchunks = D_IN // CH
    nsec = len(dsecs)

    def body(*refs):
        secs = refs[:nsec]
        wt_ref, x_ref, dx2_ref, g_ref, dx_ref, gnorm_ref, acc = refs[nsec:]
        i, s = pl.program_id(0), pl.program_id(1)

        @pl.when((i == 0) & (s == 0))
        def _():
            gnorm_ref[...] = jnp.zeros_like(gnorm_ref)

        @pl.when(s == 0)
        def _():
            acc[...] = jnp.zeros_like(acc)

        for a, (start, n) in enumerate(zip(SEC_START, SEC_CHUNKS)):
            @pl.when((s >= start) & (s < start + n))
            def _(a=a):
                acc[...] += _dot(secs[a][...], wt_ref[...], NN)

        @pl.when(s == nchunks - 1)
        def _():
            xv = x_ref[...]
            rstd = lax.rsqrt(jnp.mean(xv * xv, axis=-1, keepdims=True) + EPS)
            xh = xv * rstd
            dh = acc[...]
            gnorm_ref[0:1, :] += jnp.sum(dh * xh, axis=0, keepdims=True)
            dxn = dh * g_ref[...]
            dx_ref[...] = dx2_ref[...] + rstd * (dxn - xh * jnp.mean(dxn * xh, axis=-1, keepdims=True))

    def sec_spec(a):
        return pl.BlockSpec((tb, CH), lambda i, s, a=a: (i, _section_of_chunk(s)[a][1]))

    tok = pl.BlockSpec((tb, D), lambda i, s: (i, 0))
    return _pcall(
        body, name="input_grad", grid=(T // tb, nchunks),
        in_specs=[sec_spec(a) for a in range(nsec)] + [pl.BlockSpec((CH, D), lambda i, s: (s, 0)), tok, tok,
                                                        pl.BlockSpec((1, D), lambda i, s: (0, 0))],
        out_specs=(tok, pl.BlockSpec((8, D), lambda i, s: (0, 0))),
        out_shape=(_sds((T, D), f32), _sds((8, D), f32)),
        scratch_shapes=[pltpu.VMEM((tb, D), f32)],
        compiler_params=_params(("arbitrary", "arbitrary")),
    )(*dsecs, wt_full, x2d, dx2, norm_g)


def _w_in_grad(dsecs, h_bf):
    T = h_bf.shape[0]
    tk = min(T, 1024)
    nchunks = D_IN // CH
    nsec = len(dsecs)

    def body(*refs):
        secs = refs[:nsec]
        h_ref, out_ref = refs[nsec:]
        s, t = pl.program_id(0), pl.program_id(1)

        @pl.when(t == 0)
        def _():
            out_ref[...] = jnp.zeros_like(out_ref)

        for a, (start, n) in enumerate(zip(SEC_START, SEC_CHUNKS)):
            @pl.when((s >= start) & (s < start + n))
            def _(a=a):
                out_ref[...] += _dot(secs[a][...], h_ref[...], TN)

    def sec_spec(a):
        def index(s, t, a=a):
            inside, local = _section_of_chunk(s)[a]
            return (jnp.where(inside, t, 0), local)
        return pl.BlockSpec((tk, CH), index)

    return _pcall(
        body, name="w_in_grad", grid=(nchunks, T // tk),
        in_specs=[sec_spec(a) for a in range(nsec)] + [pl.BlockSpec((tk, D), lambda s, t: (t, 0))],
        out_specs=pl.BlockSpec((CH, D), lambda s, t: (s, 0)), out_shape=_sds((D_IN, D), f32),
        compiler_params=_params(("arbitrary", "arbitrary")),
    )(*dsecs, h_bf)


def _weight_grad(a_mat, b_mat, name):
    T, M = a_mat.shape
    N = b_mat.shape[1]
    tk = min(T, 1024)
    tm = 512

    def body(a_ref, b_ref, out_ref):
        @pl.when(pl.program_id(1) == 0)
        def _():
            out_ref[...] = jnp.zeros_like(out_ref)
        out_ref[...] += _dot(a_ref[...].astype(bf16), b_ref[...].astype(bf16), TN)

    return _pcall(
        body, name=name, grid=(M // tm, T // tk),
        in_specs=[pl.BlockSpec((tk, tm), lambda m, t: (t, m)), pl.BlockSpec((tk, N), lambda m, t: (t, 0))],
        out_specs=pl.BlockSpec((tm, N), lambda m, t: (m, 0)), out_shape=_sds((M, N), f32),
        compiler_params=_params(("arbitrary", "arbitrary")),
    )(a_mat, b_mat)


def _pad_rows(v, rows=8):
    return jnp.concatenate([v, jnp.zeros((rows - v.shape[0], v.shape[1]), v.dtype)], axis=0)


def _lanes(v):
    return jnp.pad(v, ((0, 0), (0, D - v.shape[1])))


def _blocks_to_lanes(w):
    return jnp.transpose(w, (1, 0, 2)).reshape(RB, D)


def _lanes_to_blocks(w):
    return jnp.transpose(w.reshape(RB, RNN_BLOCKS, RB), (1, 0, 2))


def _small_pack(w_a, w_x, conv_b, b_a, b_x, lam, norm_g, fin_g, sinks):
    vec = _pad_rows(jnp.concatenate([conv_b, b_a, b_x, lam], axis=0))
    return jnp.concatenate([_blocks_to_lanes(w_a), _blocks_to_lanes(w_x), vec, _pad_rows(norm_g), _pad_rows(fin_g),
                            _pad_rows(_lanes(sinks)), jnp.zeros((32, D), f32)], axis=0)


def kernel(x, norm_g, w_in, conv_w, conv_b, lru_w_a, lru_b_a, lru_w_x, lru_b_x, lru_lambda, attn_sinks, w_rnn_out, w_attn_out, w_o, final_norm_g, loss_target, m_norm_g, m_w_in, m_conv_w, m_conv_b, m_lru_w_a, m_lru_b_a, m_lru_w_x, m_lru_b_x, m_lru_lambda, m_attn_sinks, m_w_rnn_out, m_w_attn_out, m_w_o, m_final_norm_g, v_norm_g, v_w_in, v_conv_w, v_conv_b, v_lru_w_a, v_lru_b_a, v_lru_w_x, v_lru_b_x, v_lru_lambda, v_attn_sinks, v_w_rnn_out, v_w_attn_out, v_w_o, v_final_norm_g):
    nb, S, _ = x.shape
    T = nb * S
    x2d = x.reshape(T, D)
    tgt = loss_target.reshape(T, D)
    fin_g = final_norm_g.reshape(1, D)
    w_a3, w_x3 = lru_w_a[0], lru_w_x[0]

    wt_full, wr_full, wa_full, wo_full, cw_full = _gather_weights(
        w_in[0].T, w_rnn_out[0], w_attn_out[0], w_o[0], _pad_rows(conv_w[0]))

    h_bf, proj = _in_proj(x2d, norm_g, wt_full)
    y_rnn, h_all = _lru_forward(proj, cw_full, conv_b, w_a3, lru_b_a, w_x3, lru_b_x, lru_lambda, S)
    tabs = _rope_tables(S)
    y_attn = _attn_forward(proj, tabs, attn_sinks, S)

    (merged, dx2, dpr, dpa, dy_rnn, dy_attn, dmr, dma, loss_blk, gfin_blk) = _merge_and_head(
        x2d, tgt, proj, y_rnn, y_attn, wr_full, wa_full, wo_full, fin_g)

    dq, dkv, dga, dsink_blk = _attn_backward(proj, dy_attn, tabs, attn_sinks, S)
    du0, dgr, gwa, gwx, gvec, gcw = _lru_backward(proj, h_all, dy_rnn, cw_full, conv_b, w_a3, lru_b_a, w_x3,
                                                  lru_b_x, lru_lambda, S)
    dsecs = (du0, dgr, dq, dkv, dga, dmr, dma)
    grad_x2d, gnorm_blk = _input_grad(dsecs, wt_full, x2d, dx2, norm_g)

    g_wt = _w_in_grad(dsecs, h_bf)
    g_wr = _weight_grad(y_rnn, dpr, "w_rnn_out_grad")
    g_wa = _weight_grad(y_attn, dpa, "w_attn_out_grad")
    g_wo = _weight_grad(merged, dx2, "w_o_grad")
    g_small = jnp.concatenate([gwa, gwx, gvec, gnorm_blk, gfin_blk, _pad_rows(_lanes(dsink_blk[0:1, 0:16])),
                               jnp.zeros((32, D), f32)], axis=0)

    p_wt, p_wr, p_wa, p_wo, p_small, p_cw = _scatter_partials([g_wt, g_wr, g_wa, g_wo, g_small, gcw])
    o_wt = _adamw(p_wt, w_in[0].T, m_w_in[0].T, v_w_in[0].T, "adamw_w_in")
    o_wr = _adamw(p_wr, w_rnn_out[0], m_w_rnn_out[0], v_w_rnn_out[0], "adamw_w_rnn_out")
    o_wa = _adamw(p_wa, w_attn_out[0], m_w_attn_out[0], v_w_attn_out[0], "adamw_w_attn_out")
    o_wo = _adamw(p_wo, w_o[0], m_w_o[0], v_w_o[0], "adamw_w_o")
    o_cw = _adamw(p_cw, _pad_rows(conv_w[0]), _pad_rows(m_conv_w[0]), _pad_rows(v_conv_w[0]), "adamw_conv_w")

    zero40 = jnp.zeros((p_small.shape[1], D), f32)
    small_sum = _adamw(p_small, zero40, zero40, zero40, "sum_small")[0]
    g_small_all = _gather_rows(small_sum, "gather_small")
    pack = lambda *t: _small_pack(*t)
    o_small = _adamw(
        g_small_all[None],
        pack(w_a3, w_x3, conv_b, lru_b_a, lru_b_x, lru_lambda, norm_g, fin_g, attn_sinks),
        pack(m_lru_w_a[0], m_lru_w_x[0], m_conv_b, m_lru_b_a, m_lru_b_x, m_lru_lambda, m_norm_g,
             m_final_norm_g.reshape(1, D), m_attn_sinks),
        pack(v_lru_w_a[0], v_lru_w_x[0], v_conv_b, v_lru_b_a, v_lru_b_x, v_lru_lambda, v_norm_g,
             v_final_norm_g.reshape(1, D), v_attn_sinks),
        "adamw_small")

    loss = lax.psum(loss_blk[0, 0] * 0.5 / D, ("x", "y", "c"))

    def unpack(kind):
        s = o_small[kind]
        return {
            "norm_g": s[264:265], "w_in": o_wt[kind].T[None], "conv_w": o_cw[kind][None, 0:4],
            "conv_b": s[256:257], "lru_w_a": _lanes_to_blocks(s[0:128])[None], "lru_b_a": s[257:258],
            "lru_w_x": _lanes_to_blocks(s[128:256])[None], "lru_b_x": s[258:259], "lru_lambda": s[259:260],
            "attn_sinks": s[280:281, 0:16], "w_rnn_out": o_wr[kind][None], "w_attn_out": o_wa[kind][None],
            "w_o": o_wo[kind][None], "final_norm_g": s[272, :],
        }

    order = ("norm_g", "w_in", "conv_w", "conv_b", "lru_w_a", "lru_b_a", "lru_w_x", "lru_b_x", "lru_lambda",
             "attn_sinks", "w_rnn_out", "w_attn_out", "w_o", "final_norm_g")
    outs = [loss, grad_x2d.reshape(nb, S, D)]
    for kind in range(4):
        d = unpack(kind)
        outs += [d[n] for n in order]
    return tuple(outs)
```

```python
import functools
import math

import jax
import jax.numpy as jnp
from jax import lax
from jax.experimental import pallas as pl
from jax.experimental.pallas import tpu as pltpu

f32 = jnp.float32
bf16 = jnp.bfloat16

D = 1024
D_IN = 6656
NDEV = 8
RNN_BLOCKS = 8
RB = 128
HEAD = 64
KV_HEADS = 4
GROUP = 4
QB = 128
LRU_C = 8.0
EPS = 1e-6
ROPE_DIM = 16
ROPE_THETA = 500000.0
CH = 512
SEC_START = (0, 2, 4, 6, 7, 9, 11)
SEC_CHUNKS = (2, 2, 2, 1, 2, 2, 2)
VMEM_LIMIT = 56 * 1024 * 1024

ADAM_LR, ADAM_B1, ADAM_B2, ADAM_EPS, ADAM_WD, ADAM_STEP = 0.001, 0.9, 0.999, 1e-08, 0.01, 10

MESH = pl.DeviceIdType.MESH
ANY = pl.BlockSpec(memory_space=pl.ANY)
VMEM_SPEC = pl.BlockSpec(memory_space=pltpu.VMEM)
SMEM_SPEC = pl.BlockSpec(memory_space=pltpu.SMEM)


def _pcall(body, **kw):
    return pl.pallas_call(body, **kw)


def _params(sem=None, **kw):
    if sem is not None:
        kw["dimension_semantics"] = sem
    return pltpu.CompilerParams(vmem_limit_bytes=VMEM_LIMIT, **kw)


def _sds(shape, dtype):
    return jax.ShapeDtypeStruct(shape, dtype)


def _dot(a, b, dims):
    return lax.dot_general(a, b, (dims, ((), ())), preferred_element_type=f32)


NN = ((1,), (0,))
NT = ((1,), (1,))
TN = ((0,), (0,))


def _sigmoid(v):
    return 1.0 / (1.0 + jnp.exp(-v))


def _my_place():
    return lax.axis_index("x"), lax.axis_index("y"), lax.axis_index("c")


def _gather_weights(wt, wr, wa, wo, cw):
    shards = (wt, wr, wa, wo, cw)
    nrows = tuple(a.shape[0] for a in shards)
    narr = len(shards)

    def body(wt_ref, wr_ref, wa_ref, wo_ref, cw_ref, o0, o1, o2, o3, o4, s0, s1, s2, s3,
             send_sems, recv_sems, local_sems):
        x, y, c = _my_place()
        me, sibling = (x, y, c), (x, y, 1 - c)
        chips = [(1 - x, y), (x, 1 - y), (1 - x, 1 - y)]
        ins = (wt_ref, wr_ref, wa_ref, wo_ref)
        stage = (s0, s1, s2, s3)
        for a in range(4):
            stage[a][...] = ins[a][...].astype(bf16)
        srcs = stage + (cw_ref,)
        outs = (o0, o1, o2, o3, o4)

        def rows(a, place):
            px, py, pc = place
            start = pl.multiple_of((4 * px + 2 * py + pc) * nrows[a], 8)
            return outs[a].at[pl.ds(start, nrows[a]), :]

        def copy(a, k, block, to, src=None):
            return pltpu.make_async_remote_copy(
                src_ref=rows(a, block) if src is None else src, dst_ref=rows(a, block),
                send_sem=send_sems.at[7 * a + k], recv_sem=recv_sems.at[7 * a + k],
                device_id=to, device_id_type=MESH)

        mine = [pltpu.make_async_copy(srcs[a], rows(a, me), local_sems.at[a]) for a in range(narr)]
        for cp in mine:
            cp.start()
        first = []
        for a in range(narr):
            first.append(copy(a, 0, me, sibling, src=srcs[a]))
            first += [copy(a, 1 + j, me, (*chip, c), src=srcs[a]) for j, chip in enumerate(chips)]
        for cp in first:
            cp.start()
        passed = []
        for j, chip in enumerate(chips):
            for a in range(narr):
                copy(a, 1 + j, (*chip, c), me).wait_recv()
                fwd = copy(a, 4 + j, (*chip, c), sibling)
                fwd.start()
                passed.append(fwd)
        for a in range(narr):
            copy(a, 0, sibling, me).wait_recv()
        for j, chip in enumerate(chips):
            for a in range(narr):
                copy(a, 4 + j, (*chip, 1 - c), me).wait_recv()
        for cp in first + passed:
            cp.wait_send()
        for cp in mine:
            cp.wait()

    out_shape = tuple(_sds((NDEV * r, a.shape[1]), bf16 if i < 4 else f32)
                      for i, (r, a) in enumerate(zip(nrows, shards)))
    return _pcall(
        body, name="gather_weights", out_shape=out_shape,
        in_specs=[VMEM_SPEC] * narr, out_specs=tuple([ANY] * narr),
        scratch_shapes=[pltpu.VMEM(a.shape, bf16) for a in shards[:4]] + [
            pltpu.SemaphoreType.DMA((7 * narr,)), pltpu.SemaphoreType.DMA((7 * narr,)),
            pltpu.SemaphoreType.DMA((narr,))],
        compiler_params=_params(),
    )(*shards)


def _gather_rows(blk, name):
    nrows, ncols = blk.shape

    def body(src, out, send_sems, recv_sems, local_sem):
        x, y, c = _my_place()
        me, sibling = (x, y, c), (x, y, 1 - c)
        chips = [(1 - x, y), (x, 1 - y), (1 - x, 1 - y)]

        def rows(place):
            px, py, pc = place
            start = pl.multiple_of((4 * px + 2 * py + pc) * nrows, 8)
            return out.at[pl.ds(start, nrows), :]

        def copy(k, block, to, from_src=False):
            return pltpu.make_async_remote_copy(
                src_ref=src if from_src else rows(block), dst_ref=rows(block),
                send_sem=send_sems.at[k], recv_sem=recv_sems.at[k], device_id=to, device_id_type=MESH)

        mine = pltpu.make_async_copy(src, rows(me), local_sem)
        mine.start()
        first = [copy(0, me, sibling, True)]
        first += [copy(1 + j, me, (*chip, c), True) for j, chip in enumerate(chips)]
        for cp in first:
            cp.start()
        passed = []
        for j, chip in enumerate(chips):
            copy(1 + j, (*chip, c), me).wait_recv()
            fwd = copy(4 + j, (*chip, c), sibling)
            fwd.start()
            passed.append(fwd)
        copy(0, sibling, me).wait_recv()
        for j, chip in enumerate(chips):
            copy(4 + j, (*chip, 1 - c), me).wait_recv()
        for cp in first + passed:
            cp.wait_send()
        mine.wait()

    return _pcall(
        body, name=name, out_shape=_sds((NDEV * nrows, ncols), blk.dtype),
        in_specs=[ANY], out_specs=ANY,
        scratch_shapes=[pltpu.SemaphoreType.DMA((7,)), pltpu.SemaphoreType.DMA((7,)), pltpu.SemaphoreType.DMA],
        compiler_params=_params(),
    )(blk)


def _pair_exchange(grads):
    narr = len(grads)
    nrows = tuple(g.shape[0] // NDEV for g in grads)
    views = [g.reshape(4, 2, r, g.shape[1]) for g, r in zip(grads, nrows)]

    def body(*refs):
        gin = refs[:narr]
        got = refs[narr:2 * narr]
        send_sems, recv_sems = refs[2 * narr:]
        x, y, c = _my_place()
        copies = [pltpu.make_async_remote_copy(
            src_ref=gin[a].at[:, pl.ds(1 - c, 1)], dst_ref=got[a],
            send_sem=send_sems.at[a], recv_sem=recv_sems.at[a],
            device_id=(x, y, 1 - c), device_id_type=MESH) for a in range(narr)]
        for cp in copies:
            cp.start()
        for cp in copies:
            cp.wait()

    out_shape = tuple(_sds((4, 1, r, g.shape[1]), f32) for r, g in zip(nrows, grads))
    got = _pcall(
        body, name="pair_exchange", out_shape=out_shape,
        in_specs=[ANY] * narr, out_specs=tuple([ANY] * narr),
        scratch_shapes=[pltpu.SemaphoreType.DMA((narr,)), pltpu.SemaphoreType.DMA((narr,))],
        compiler_params=_params(),
    )(*views)
    return views, [g.reshape(4, r, g.shape[3]) for g, r in zip(got, nrows)]


def _row_tile(rows, dtype):
    unit = 16 if dtype == bf16 else 8
    for cand in (256, 208, 128, 64, 40, 32, 16, 8):
        if rows % cand == 0 and cand % unit == 0:
            return cand
    return rows


def _chip_sum(view, got, my_core, out_dtype, name):
    _, _, r, cols = view.shape
    tr = _row_tile(r, out_dtype)

    def body(core_ref, mine_ref, got_ref, out_ref):
        out_ref[...] = (mine_ref[...] + got_ref[...]).astype(out_dtype)

    grid_spec = pltpu.PrefetchScalarGridSpec(
        num_scalar_prefetch=1, grid=(4, r // tr),
        in_specs=[pl.BlockSpec((None, None, tr, cols), lambda q, i, core: (q, core[0], i, 0)),
                  pl.BlockSpec((None, tr, cols), lambda q, i, core: (q, i, 0))],
        out_specs=pl.BlockSpec((None, tr, cols), lambda q, i, core: (q, i, 0)))
    return _pcall(body, name=name, grid_spec=grid_spec, out_shape=_sds((4, r, cols), out_dtype),
                  compiler_params=_params(("arbitrary", "arbitrary")))(my_core, view, got)


def _chip_exchange(sums):
    narr = len(sums)

    def body(*refs):
        src = refs[:narr]
        dst = refs[narr:2 * narr]
        send_sems, recv_sems, local_sems = refs[2 * narr:]
        x, y, c = _my_place()

        def copy(a, k):
            px, py = (x + (k >> 1)) % 2, (y + (k & 1)) % 2
            return pltpu.make_async_remote_copy(
                src_ref=src[a].at[2 * px + py], dst_ref=dst[a].at[k],
                send_sem=send_sems.at[3 * a + k - 1], recv_sem=recv_sems.at[3 * a + k - 1],
                device_id=(px, py, c), device_id_type=MESH)

        mine = [pltpu.make_async_copy(src[a].at[2 * x + y], dst[a].at[0], local_sems.at[a]) for a in range(narr)]
        for cp in mine:
            cp.start()
        sends = [copy(a, k) for k in (3, 1, 2) for a in range(narr)]
        for cp in sends:
            cp.start()
        for cp in sends:
            cp.wait()
        for cp in mine:
            cp.wait()

    out_shape = tuple(_sds(s.shape, s.dtype) for s in sums)
    return _pcall(
        body, name="chip_exchange", out_shape=out_shape,
        in_specs=[ANY] * narr, out_specs=tuple([ANY] * narr),
        scratch_shapes=[pltpu.SemaphoreType.DMA((3 * narr,)), pltpu.SemaphoreType.DMA((3 * narr,)),
                        pltpu.SemaphoreType.DMA((narr,))],
        compiler_params=_params(),
    )(*sums)


def _reduce_scatter(grads, wire_dtypes, my_core):
    views, got = _pair_exchange(grads)
    sums = [_chip_sum(v, g, my_core, dt, "chip_sum_%d" % a)
            for a, (v, g, dt) in enumerate(zip(views, got, wire_dtypes))]
    return _chip_exchange(sums)


def _adamw(parts, w, m, v, name):
    n, rows, cols = parts.shape
    tr = _row_tile(rows, parts.dtype)

    def body(p_ref, w_ref, m_ref, v_ref, g_out, d_out, m_out, v_out):
        g = p_ref[0].astype(f32)
        for s in range(1, n):
            g = g + p_ref[s].astype(f32)
        m_new = ADAM_B1 * m_ref[...] + (1.0 - ADAM_B1) * g
        v_new = ADAM_B2 * v_ref[...] + (1.0 - ADAM_B2) * (g * g)
        m_hat = m_new / (1.0 - ADAM_B1 ** ADAM_STEP)
        v_hat = v_new / (1.0 - ADAM_B2 ** ADAM_STEP)
        g_out[...] = g
        d_out[...] = -ADAM_LR * (m_hat / (jnp.sqrt(v_hat) + ADAM_EPS) + ADAM_WD * w_ref[...])
        m_out[...] = m_new
        v_out[...] = v_new

    blk = pl.BlockSpec((tr, cols), lambda i: (i, 0))
    return _pcall(
        body, name=name, grid=(rows // tr,),
        in_specs=[pl.BlockSpec((n, tr, cols), lambda i: (0, i, 0)), blk, blk, blk],
        out_specs=(blk, blk, blk, blk), out_shape=tuple(_sds((rows, cols), f32) for _ in range(4)),
        compiler_params=_params(("arbitrary",)),
    )(parts, w, m, v)


def _in_proj(x2d, norm_g, wt_full):
    T = x2d.shape[0]
    tb = min(T, 1024)

    def body(x_ref, g_ref, wt_ref, h_ref, proj_ref):
        @pl.when(pl.program_id(1) == 0)
        def _():
            xv = x_ref[...]
            ms = jnp.mean(xv * xv, axis=-1, keepdims=True)
            h_ref[...] = (xv * lax.rsqrt(ms + EPS) * g_ref[...]).astype(bf16)
        proj_ref[...] = _dot(h_ref[...], wt_ref[...], NT)

    return _pcall(
        body, name="in_proj", grid=(T // tb, D_IN // CH),
        in_specs=[pl.BlockSpec((tb, D), lambda i, j: (i, 0)), pl.BlockSpec((1, D), lambda i, j: (0, 0)),
                  pl.BlockSpec((CH, D), lambda i, j: (j, 0))],
        out_specs=(pl.BlockSpec((tb, D), lambda i, j: (i, 0)), pl.BlockSpec((tb, CH), lambda i, j: (i, j))),
        out_shape=(_sds((T, D), bf16), _sds((T, D_IN), f32)),
        compiler_params=_params(("arbitrary", "arbitrary")),
    )(x2d, norm_g, wt_full)


def _rows_iota(shape):
    return lax.broadcasted_iota(jnp.int32, shape, 0)


def _shift_down(v, k):
    return jnp.where(_rows_iota(v.shape) >= k, pltpu.roll(v, k, 0), 0.0)


def _shift_up(v, k):
    n = v.shape[0]
    return jnp.where(_rows_iota(v.shape) < n - k, pltpu.roll(v, n - k, 0), 0.0)


def _scan_forward(a, b):
    n = a.shape[0]
    r = _rows_iota(a.shape)
    s = 1
    while s < n:
        keep = r >= s
        b = jnp.where(keep, a * pltpu.roll(b, s, 0) + b, b)
        if 2 * s < n:
            a = jnp.where(keep, a * pltpu.roll(a, s, 0), a)
        s *= 2
    return b


def _scan_reverse(a, b):
    n = a.shape[0]
    r = _rows_iota(a.shape)
    s = 1
    while s < n:
        keep = r < n - s
        b = jnp.where(keep, a * pltpu.roll(b, n - s, 0) + b, b)
        if 2 * s < n:
            a = jnp.where(keep, a * pltpu.roll(a, n - s, 0), a)
        s *= 2
    return b


def _neg_expm1(v):
    series = -v * (1.0 + v * (0.5 + v * (1.0 / 6.0 + v * (1.0 / 24.0))))
    return jnp.where(v > -0.03125, series, 1.0 - jnp.exp(v))


def _softplus_neg(lam):
    return jnp.maximum(-lam, 0.0) + jnp.log(1.0 + jnp.exp(-jnp.abs(lam)))


def _lru_gates(x0, cw, cb, wa, ba, wx, bx, lam):
    u = cb + cw[3:4, :] * x0
    for k in range(3):
        u = u + cw[k:k + 1, :] * _shift_down(x0, 3 - k)
    ub = u.astype(bf16)
    r = _sigmoid(_dot(ub, wa.astype(bf16), NN) + ba)
    i = _sigmoid(_dot(ub, wx.astype(bf16), NN) + bx)
    sp = _softplus_neg(lam)
    log_a = (-LRU_C) * r * sp
    a = jnp.exp(log_a)
    mult = jnp.sqrt(_neg_expm1(2.0 * log_a))
    return u, ub, r, i, sp, a, mult


def _lru_specs(S, nb):
    col = lambda off: pl.BlockSpec((S, RB), lambda n, b, off=off: (b, off + n))
    vec = pl.BlockSpec((1, RB), lambda n, b: (0, n))
    wblk = pl.BlockSpec((None, RB, RB), lambda n, b: (n, 0, 0))
    cwblk = pl.BlockSpec((8, RB), lambda n, b: (n, 0))
    return col, vec, wblk, cwblk


def _lru_forward(proj, cw_full, conv_b, w_a, b_a, w_x, b_x, lam, S):
    T = proj.shape[0]
    nb = T // S
    col, vec, wblk, cwblk = _lru_specs(S, nb)

    def body(x0_ref, g_ref, cw_ref, cb_ref, wa_ref, ba_ref, wx_ref, bx_ref, lam_ref, y_ref, h_ref):
        x0 = x0_ref[...]
        u, ub, r, i, sp, a, mult = _lru_gates(x0, cw_ref[...], cb_ref[...], wa_ref[...], ba_ref[...],
                                              wx_ref[...], bx_ref[...], lam_ref[...])
        h = _scan_forward(a, mult * (i * u))
        g = g_ref[...]
        h_ref[...] = h
        y_ref[...] = (h * (g * _sigmoid(g))).astype(bf16)

    out = pl.BlockSpec((S, RB), lambda n, b: (b, n))
    return _pcall(
        body, name="lru_forward", grid=(RNN_BLOCKS, nb),
        in_specs=[col(0), col(8), cwblk, vec, wblk, vec, wblk, vec, vec],
        out_specs=(out, out), out_shape=(_sds((T, D), bf16), _sds((T, D), f32)),
        compiler_params=_params(("arbitrary", "arbitrary")),
    )(proj, proj, cw_full, conv_b, w_a, b_a, w_x, b_x, lam)


def _rope_tables(S):
    pos = jnp.arange(S, dtype=f32)
    inv_freq = ROPE_THETA ** (-jnp.arange(0, ROPE_DIM, 2, dtype=f32) / ROPE_DIM)
    ang = pos[:, None] * inv_freq[None, :]
    cos, sin = jnp.cos(ang), jnp.sin(ang)
    lane = jnp.arange(256) % HEAD
    cosl, sinl = cos[:, lane % 8], sin[:, lane % 8]
    c = jnp.where(lane[None, :] < ROPE_DIM, cosl, 1.0)
    s1 = jnp.where(lane[None, :] < 8, -sinl, 0.0)
    s2 = jnp.where((lane[None, :] >= 8) & (lane[None, :] < ROPE_DIM), sinl, 0.0)
    return c.astype(f32), s1.astype(f32), s2.astype(f32)


def _rope(t, c, s1, s2):
    w = t.shape[1]
    return t * c + pltpu.roll(t, w - 8, 1) * s1 + pltpu.roll(t, 8, 1) * s2


def _rope_transposed(dt, c, s1, s2):
    w = dt.shape[1]
    return dt * c + pltpu.roll(dt * s1, 8, 1) + pltpu.roll(dt * s2, w - 8, 1)


def _heads_to_rows(t):
    return jnp.concatenate([t[:, HEAD * h:HEAD * (h + 1)] for h in range(GROUP)], axis=0)


def _rows_to_heads(t):
    return jnp.concatenate([t[QB * h:QB * (h + 1), :] for h in range(GROUP)], axis=1)


def _attn_probs(q_rows, k_cat, sink_col, first_block):
    s = _dot(q_rows, k_cat, NT) * (1.0 / math.sqrt(HEAD))
    qi = _rows_iota(s.shape) % QB
    cj = lax.broadcasted_iota(jnp.int32, s.shape, 1)
    valid = (cj > qi) & (cj <= qi + QB) & ((cj >= QB) | jnp.logical_not(first_block))
    s = jnp.where(valid, s, -jnp.inf)
    m = jnp.maximum(jnp.max(s, axis=1, keepdims=True), sink_col)
    p = jnp.exp(s - m)
    e_sink = jnp.exp(sink_col - m)
    den = jnp.sum(p, axis=1, keepdims=True) + e_sink
    return p / den, e_sink / den


def _sink_column(sink_ref, kv):
    rid = _rows_iota((GROUP * QB, 1))
    col = jnp.zeros((GROUP * QB, 1), f32)
    for h in range(GROUP):
        col = jnp.where(rid // QB == h, sink_ref[0, GROUP * kv + h], col)
    return col


def _attn_in_specs(S):
    nq = S // QB
    last = nq - 1
    cur = lambda b, j: b * nq + jnp.minimum(j, last)
    prev = lambda b, j: b * nq + jnp.maximum(jnp.minimum(j, last) - 1, 0)
    tab_cur = lambda b, j: (jnp.minimum(j, last), 0)
    tab_prev = lambda b, j: (jnp.maximum(jnp.minimum(j, last) - 1, 0), 0)
    specs = [
        pl.BlockSpec((QB, D), lambda b, j: (cur(b, j), 2)),
        pl.BlockSpec((QB, 256), lambda b, j: (cur(b, j), 12)),
        pl.BlockSpec((QB, 256), lambda b, j: (prev(b, j), 12)),
        pl.BlockSpec((QB, 256), lambda b, j: (cur(b, j), 13)),
        pl.BlockSpec((QB, 256), lambda b, j: (prev(b, j), 13)),
        pl.BlockSpec((QB, 512), lambda b, j: (cur(b, j), 7)),
        pl.BlockSpec((QB, 512), lambda b, j: (cur(b, j), 8)),
        pl.BlockSpec((QB, 256), tab_cur), pl.BlockSpec((QB, 256), tab_cur), pl.BlockSpec((QB, 256), tab_cur),
        pl.BlockSpec((QB, 256), tab_prev), pl.BlockSpec((QB, 256), tab_prev), pl.BlockSpec((QB, 256), tab_prev),
        SMEM_SPEC,
    ]
    return specs, cur, prev


def _attn_forward(proj, tabs, sinks, S):
    T = proj.shape[0]
    nb, nq = T // S, S // QB
    specs, cur, _ = _attn_in_specs(S)

    def body(q_ref, kc_ref, kp_ref, vc_ref, vp_ref, gl_ref, gh_ref, cc, s1c, s2c, cp, s1p, s2p, sink_ref, y_ref):
        first = pl.program_id(1) == 0
        tc = (cc[...], s1c[...], s2c[...])
        kc = _rope(kc_ref[...], *tc)
        kp = _rope(kp_ref[...], cp[...], s1p[...], s2p[...])
        vc, vp = vc_ref[...], vp_ref[...]
        for kv in range(KV_HEADS):
            lanes = slice(256 * kv, 256 * (kv + 1))
            hl = slice(HEAD * kv, HEAD * (kv + 1))
            q_rows = _heads_to_rows(_rope(q_ref[:, lanes], *tc)).astype(bf16)
            k_cat = jnp.concatenate([kp[:, hl], kc[:, hl]], axis=0).astype(bf16)
            v_cat = jnp.concatenate([vp[:, hl], vc[:, hl]], axis=0).astype(bf16)
            probs, _ = _attn_probs(q_rows, k_cat, _sink_column(sink_ref, kv), first)
            o = _rows_to_heads(_dot(probs.astype(bf16), v_cat, NN))
            g_src = gl_ref if kv < 2 else gh_ref
            g = g_src[:, 256 * (kv % 2):256 * (kv % 2 + 1)]
            y_ref[:, lanes] = (o * (g * _sigmoid(g))).astype(bf16)

    args = [proj] * 7 + list(tabs) + list(tabs) + [sinks]
    return _pcall(
        body, name="attn_forward", grid=(nb, nq), in_specs=specs,
        out_specs=pl.BlockSpec((QB, D), lambda b, j: (cur(b, j), 0)), out_shape=_sds((T, D), bf16),
        compiler_params=_params(("arbitrary", "arbitrary")),
    )(*args)


def _merge_and_head(x2d, tgt, proj, y_rnn, y_attn, w_r, w_a, w_o, gfin):
    T = x2d.shape[0]
    tb = min(T, 256)
    nsteps = T // tb

    def body(x_ref, t_ref, mr0, mr1, ma0, ma1, yr_ref, ya_ref, wr_ref, wa_ref, wo_ref, gf_ref,
             merged_ref, dx2_ref, dpr_ref, dpa_ref, dyr_ref, dya_ref, dmr_ref, dma_ref, loss_ref, gfin_ref):
        @pl.when(pl.program_id(0) == 0)
        def _():
            loss_ref[...] = jnp.zeros_like(loss_ref)
            gfin_ref[...] = jnp.zeros_like(gfin_ref)

        sr = _sigmoid(jnp.concatenate([mr0[...], mr1[...]], axis=1))
        sa = _sigmoid(jnp.concatenate([ma0[...], ma1[...]], axis=1))
        p_r = _dot(yr_ref[...], wr_ref[...], NN)
        p_a = _dot(ya_ref[...], wa_ref[...], NN)
        merged = (sr * p_r + sa * p_a).astype(bf16)
        merged_ref[...] = merged
        x2 = x_ref[...] + _dot(merged, wo_ref[...], NN)
        rstd = lax.rsqrt(jnp.mean(x2 * x2, axis=-1, keepdims=True) + EPS)
        xh = x2 * rstd
        gf = gf_ref[...]
        err = xh * gf - t_ref[...]
        loss_ref[...] += jnp.sum(err * err)
        dy = err * (1.0 / D)
        gfin_ref[0:1, :] += jnp.sum(dy * xh, axis=0, keepdims=True)
        dxn = dy * gf
        dx2 = rstd * (dxn - xh * jnp.mean(dxn * xh, axis=-1, keepdims=True))
        dx2_ref[...] = dx2
        dmerged = _dot(dx2.astype(bf16), wo_ref[...], NT)
        dmr_ref[...] = (dmerged * p_r * (sr * (1.0 - sr))).astype(bf16)
        dma_ref[...] = (dmerged * p_a * (sa * (1.0 - sa))).astype(bf16)
        dpr = (dmerged * sr).astype(bf16)
        dpa = (dmerged * sa).astype(bf16)
        dpr_ref[...] = dpr
        dpa_ref[...] = dpa
        dyr_ref[...] = _dot(dpr, wr_ref[...], NT)
        dya_ref[...] = _dot(dpa, wa_ref[...], NT)

    tok = pl.BlockSpec((tb, D), lambda i: (i, 0))
    half = lambda c: pl.BlockSpec((tb, CH), lambda i, c=c: (i, c))
    wfull = pl.BlockSpec((D, D), lambda i: (0, 0))
    acc = pl.BlockSpec((8, D), lambda i: (0, 0))
    return _pcall(
        body, name="merge_and_head", grid=(nsteps,),
        in_specs=[tok, tok, half(9), half(10), half(11), half(12), tok, tok, wfull, wfull, wfull,
                  pl.BlockSpec((1, D), lambda i: (0, 0))],
        out_specs=(tok, tok, tok, tok, tok, tok, tok, tok, acc, acc),
        out_shape=(_sds((T, D), bf16), _sds((T, D), f32), _sds((T, D), bf16), _sds((T, D), bf16),
                   _sds((T, D), f32), _sds((T, D), f32), _sds((T, D), bf16), _sds((T, D), bf16),
                   _sds((8, D), f32), _sds((8, D), f32)),
        compiler_params=_params(("arbitrary",)),
    )(x2d, tgt, proj, proj, proj, proj, y_rnn, y_attn, w_r, w_a, w_o, gfin)


def _attn_backward(proj, dy_attn, tabs, sinks, S):
    T = proj.shape[0]
    nb, nq = T // S, S // QB
    specs, cur, prev = _attn_in_specs(S)
    specs = specs + [pl.BlockSpec((QB, D), lambda b, j: (cur(b, j), 0))]

    def body(q_ref, kc_ref, kp_ref, vc_ref, vp_ref, gl_ref, gh_ref, cc, s1c, s2c, cp, s1p, s2p, sink_ref, dy_ref,
             dq_ref, dkv_ref, dg_ref, dsink_ref, carry_k, carry_v):
        b, j = pl.program_id(0), pl.program_id(1)

        @pl.when((b == 0) & (j == 0))
        def _():
            dsink_ref[...] = jnp.zeros_like(dsink_ref)

        @pl.when(j == 0)
        def _():
            carry_k[...] = jnp.zeros_like(carry_k)
            carry_v[...] = jnp.zeros_like(carry_v)

        @pl.when(j < nq)
        def _():
            first = j == 0
            tc = (cc[...], s1c[...], s2c[...])
            tp = (cp[...], s1p[...], s2p[...])
            kc = _rope(kc_ref[...], *tc)
            kp = _rope(kp_ref[...], *tp)
            vc, vp = vc_ref[...], vp_ref[...]
            dk_prev, dk_cur, dv_prev, dv_cur = [], [], [], []
            dsink_acc = jnp.zeros((8, 128), f32)
            r8 = lax.broadcasted_iota(jnp.int32, (8, 128), 0)
            l8 = lax.broadcasted_iota(jnp.int32, (8, 128), 1)
            for kv in range(KV_HEADS):
                lanes = slice(256 * kv, 256 * (kv + 1))
                hl = slice(HEAD * kv, HEAD * (kv + 1))
                q_rows = _heads_to_rows(_rope(q_ref[:, lanes], *tc)).astype(bf16)
                k_cat = jnp.concatenate([kp[:, hl], kc[:, hl]], axis=0).astype(bf16)
                v_cat = jnp.concatenate([vp[:, hl], vc[:, hl]], axis=0).astype(bf16)
                probs, p_sink = _attn_probs(q_rows, k_cat, _sink_column(sink_ref, kv), first)
                pb = probs.astype(bf16)
                o = _rows_to_heads(_dot(pb, v_cat, NN))
                g_src = gl_ref if kv < 2 else gh_ref
                g = g_src[:, 256 * (kv % 2):256 * (kv % 2 + 1)]
                sg = _sigmoid(g)
                dy = dy_ref[:, lanes]
                dg_ref[:, lanes] = (dy * o * (sg * (1.0 + g * (1.0 - sg)))).astype(bf16)
                do_rows = _heads_to_rows(dy * (g * sg)).astype(bf16)
                dv = _dot(pb, do_rows, TN)
                dp = _dot(do_rows, v_cat, NT)
                rowdot = jnp.sum(probs * dp, axis=1, keepdims=True)
                ds = (probs * (dp - rowdot) * (1.0 / math.sqrt(HEAD))).astype(bf16)
                sink_rows = -(p_sink * rowdot)
                for h in range(GROUP):
                    val = jnp.sum(sink_rows[QB * h:QB * (h + 1), :])
                    dsink_acc = dsink_acc + jnp.where((r8 == 0) & (l8 == GROUP * kv + h), val, 0.0)
                dq = _rows_to_heads(_dot(ds, k_cat, NN))
                dq_ref[:, lanes] = _rope_transposed(dq, *tc).astype(bf16)
                dk = _dot(ds, q_rows, TN)
                dk_prev.append(dk[:QB, :])
                dk_cur.append(dk[QB:, :])
                dv_prev.append(dv[:QB, :])
                dv_cur.append(dv[QB:, :])
            dsink_ref[...] += dsink_acc
            dkp = _rope_transposed(jnp.concatenate(dk_prev, axis=1), *tp)
            dkc = _rope_transposed(jnp.concatenate(dk_cur, axis=1), *tc)
            dkv_ref[:, 0:256] = (carry_k[...] + dkp).astype(bf16)
            dkv_ref[:, 256:512] = (carry_v[...] + jnp.concatenate(dv_prev, axis=1)).astype(bf16)
            carry_k[...] = dkc
            carry_v[...] = jnp.concatenate(dv_cur, axis=1)

        @pl.when(j == nq)
        def _():
            dkv_ref[:, 0:256] = carry_k[...].astype(bf16)
            dkv_ref[:, 256:512] = carry_v[...].astype(bf16)

    lag = lambda b, j: (b * nq + jnp.maximum(j - 1, 0), 0)
    args = [proj] * 7 + list(tabs) + list(tabs) + [sinks, dy_attn]
    return _pcall(
        body, name="attn_backward", grid=(nb, nq + 1), in_specs=specs,
        out_specs=(pl.BlockSpec((QB, D), lambda b, j: (cur(b, j), 0)), pl.BlockSpec((QB, 512), lag),
                   pl.BlockSpec((QB, D), lambda b, j: (cur(b, j), 0)), pl.BlockSpec((8, 128), lambda b, j: (0, 0))),
        out_shape=(_sds((T, D), bf16), _sds((T, 512), bf16), _sds((T, D), bf16), _sds((8, 128), f32)),
        scratch_shapes=[pltpu.VMEM((QB, 256), f32), pltpu.VMEM((QB, 256), f32)],
        compiler_params=_params(("arbitrary", "arbitrary")),
    )(*args)


def _lru_backward(proj, h_all, dy_rnn, cw_full, conv_b, w_a, b_a, w_x, b_x, lam, S):
    T = proj.shape[0]
    nb = T // S
    col, vec, wblk, cwblk = _lru_specs(S, nb)
    tokblk = pl.BlockSpec((S, RB), lambda n, b: (b, n))

    def body(x0_ref, g_ref, h_ref, dy_ref, cw_ref, cb_ref, wa_ref, ba_ref, wx_ref, bx_ref, lam_ref,
             du0_ref, dg_ref, gwa_ref, gwx_ref, vec_ref, gcw_ref):
        @pl.when(pl.program_id(1) == 0)
        def _():
            gwa_ref[...] = jnp.zeros_like(gwa_ref)
            gwx_ref[...] = jnp.zeros_like(gwx_ref)
            vec_ref[...] = jnp.zeros_like(vec_ref)
            gcw_ref[...] = jnp.zeros_like(gcw_ref)

        x0 = x0_ref[...]
        cw = cw_ref[...]
        lam_v = lam_ref[...]
        u, ub, r, i, sp, a, mult = _lru_gates(x0, cw, cb_ref[...], wa_ref[...], ba_ref[...],
                                              wx_ref[...], bx_ref[...], lam_v)
        h = h_ref[...]
        g = g_ref[...]
        dy = dy_ref[...]
        sg = _sigmoid(g)
        dg_ref[...] = (dy * h * (sg * (1.0 + g * (1.0 - sg)))).astype(bf16)
        dh_total = _scan_reverse(_shift_up(a, 1), dy * (g * sg))
        da = dh_total * _shift_down(h, 1)
        iu = i * u
        dmult = dh_total * iu
        di = dh_total * mult * u
        du = dh_total * mult * i
        dlog_a = a * (da - dmult * a / mult)
        dr = dlog_a * ((-LRU_C) * sp)
        dsp = jnp.sum(dlog_a * ((-LRU_C) * r), axis=0, keepdims=True)
        dpre_r = dr * r * (1.0 - r)
        dpre_i = di * i * (1.0 - i)
        dpre_rb = dpre_r.astype(bf16)
        dpre_ib = dpre_i.astype(bf16)
        du = du + _dot(dpre_rb, wa_ref[...].astype(bf16), NT) + _dot(dpre_ib, wx_ref[...].astype(bf16), NT)
        gwa_ref[...] += _dot(ub, dpre_rb, TN)
        gwx_ref[...] += _dot(ub, dpre_ib, TN)
        vec_ref[0:1, :] += jnp.sum(du, axis=0, keepdims=True)
        vec_ref[1:2, :] += jnp.sum(dpre_r, axis=0, keepdims=True)
        vec_ref[2:3, :] += jnp.sum(dpre_i, axis=0, keepdims=True)
        vec_ref[3:4, :] += dsp * (-_sigmoid(-lam_v))
        dx0 = cw[3:4, :] * du
        gcw_ref[3:4, :] += jnp.sum(du * x0, axis=0, keepdims=True)
        for k in range(3):
            dx0 = dx0 + cw[k:k + 1, :] * _shift_up(du, 3 - k)
            gcw_ref[k:k + 1, :] += jnp.sum(du * _shift_down(x0, 3 - k), axis=0, keepdims=True)
        du0_ref[...] = dx0.astype(bf16)

    wacc = pl.BlockSpec((RB, RB), lambda n, b: (0, n))
    vacc = pl.BlockSpec((8, RB), lambda n, b: (0, n))
    cacc = pl.BlockSpec((8, RB), lambda n, b: (n, 0))
    return _pcall(
        body, name="lru_backward", grid=(RNN_BLOCKS, nb),
        in_specs=[col(0), col(8), tokblk, tokblk, cwblk, vec, wblk, vec, wblk, vec, vec],
        out_specs=(tokblk, tokblk, wacc, wacc, vacc, cacc),
        out_shape=(_sds((T, D), bf16), _sds((T, D), bf16), _sds((RB, D), f32), _sds((RB, D), f32),
                   _sds((8, D), f32), _sds((8 * RNN_BLOCKS, RB), f32)),
        compiler_params=_params(("arbitrary", "arbitrary")),
    )(proj, proj, h_all, dy_rnn, cw_full, conv_b, w_a, b_a, w_x, b_x, lam)


def _section_of_chunk(s):
    out = []
    for start, n in zip(SEC_START, SEC_CHUNKS):
        inside = (s >= start) & (s < start + n)
        out.append((inside, jnp.clip(s - start, 0, n - 1)))
    return out


def _input_grad(dsecs, wt_full, x2d, dx2, norm_g):
    T = x2d.shape[0]
    tb = min(T, 1024)
    nchunks = D_IN // CH
    nsec = len(dsecs)

    def body(*refs):
        secs = refs[:nsec]
        wt_ref, x_ref, dx2_ref, g_ref, dx_ref, gnorm_ref, acc = refs[nsec:]
        i, s = pl.program_id(0), pl.program_id(1)

        @pl.when((i == 0) & (s == 0))
        def _():
            gnorm_ref[...] = jnp.zeros_like(gnorm_ref)

        @pl.when(s == 0)
        def _():
            acc[...] = jnp.zeros_like(acc)

        for a, (start, n) in enumerate(zip(SEC_START, SEC_CHUNKS)):
            @pl.when((s >= start) & (s < start + n))
            def _(a=a):
                acc[...] += _dot(secs[a][...], wt_ref[...], NN)

        @pl.when(s == nchunks - 1)
        def _():
            xv = x_ref[...]
            rstd = lax.rsqrt(jnp.mean(xv * xv, axis=-1, keepdims=True) + EPS)
            xh = xv * rstd
            dh = acc[...]
            gnorm_ref[0:1, :] += jnp.sum(dh * xh, axis=0, keepdims=True)
            dxn = dh * g_ref[...]
            dx_ref[...] = dx2_ref[...] + rstd * (dxn - xh * jnp.mean(dxn * xh, axis=-1, keepdims=True))

    def sec_spec(a):
        return pl.BlockSpec((tb, CH), lambda i, s, a=a: (i, _section_of_chunk(s)[a][1]))

    tok = pl.BlockSpec((tb, D), lambda i, s: (i, 0))
    return _pcall(
        body, name="input_grad", grid=(T // tb, nchunks),
        in_specs=[sec_spec(a) for a in range(nsec)] + [pl.BlockSpec((CH, D), lambda i, s: (s, 0)), tok, tok,
                                                        pl.BlockSpec((1, D), lambda i, s: (0, 0))],
        out_specs=(tok, pl.BlockSpec((8, D), lambda i, s: (0, 0))),
        out_shape=(_sds((T, D), f32), _sds((8, D), f32)),
        scratch_shapes=[pltpu.VMEM((tb, D), f32)],
        compiler_params=_params(("arbitrary", "arbitrary")),
    )(*dsecs, wt_full, x2d, dx2, norm_g)


def _w_in_grad(dsecs, h_bf):
    T = h_bf.shape[0]
    tk = min(T, 1024)
    nchunks = D_IN // CH
    nsec = len(dsecs)

    def body(*refs):
        secs = refs[:nsec]
        h_ref, out_ref = refs[nsec:]
        s, t = pl.program_id(0), pl.program_id(1)

        @pl.when(t == 0)
        def _():
            out_ref[...] = jnp.zeros_like(out_ref)

        for a, (start, n) in enumerate(zip(SEC_START, SEC_CHUNKS)):
            @pl.when((s >= start) & (s < start + n))
            def _(a=a):
                out_ref[...] += _dot(secs[a][...], h_ref[...], TN)

    def sec_spec(a):
        def index(s, t, a=a):
            inside, local = _section_of_chunk(s)[a]
            return (jnp.where(inside, t, 0), local)
        return pl.BlockSpec((tk, CH), index)

    return _pcall(
        body, name="w_in_grad", grid=(nchunks, T // tk),
        in_specs=[sec_spec(a) for a in range(nsec)] + [pl.BlockSpec((tk, D), lambda s, t: (t, 0))],
        out_specs=pl.BlockSpec((CH, D), lambda s, t: (s, 0)), out_shape=_sds((D_IN, D), f32),
        compiler_params=_params(("arbitrary", "arbitrary")),
    )(*dsecs, h_bf)


def _weight_grad(a_mat, b_mat, name):
    T, M = a_mat.shape
    N = b_mat.shape[1]
    tk = min(T, 1024)
    tm = 512

    def body(a_ref, b_ref, out_ref):
        @pl.when(pl.program_id(1) == 0)
        def _():
            out_ref[...] = jnp.zeros_like(out_ref)
        out_ref[...] += _dot(a_ref[...].astype(bf16), b_ref[...].astype(bf16), TN)

    return _pcall(
        body, name=name, grid=(M // tm, T // tk),
        in_specs=[pl.BlockSpec((tk, tm), lambda m, t: (t, m)), pl.BlockSpec((tk, N), lambda m, t: (t, 0))],
        out_specs=pl.BlockSpec((tm, N), lambda m, t: (m, 0)), out_shape=_sds((M, N), f32),
        compiler_params=_params(("arbitrary", "arbitrary")),
    )(a_mat, b_mat)


def _pad_rows(v, rows=8):
    return jnp.concatenate([v, jnp.zeros((rows - v.shape[0], v.shape[1]), v.dtype)], axis=0)


def _lanes(v):
    return jnp.pad(v, ((0, 0), (0, D - v.shape[1])))


def _blocks_to_lanes(w):
    return jnp.transpose(w, (1, 0, 2)).reshape(RB, D)


def _lanes_to_blocks(w):
    return jnp.transpose(w.reshape(RB, RNN_BLOCKS, RB), (1, 0, 2))


def _small_pack(w_a, w_x, conv_b, b_a, b_x, lam, norm_g, fin_g, sinks):
    vec = _pad_rows(jnp.concatenate([conv_b, b_a, b_x, lam], axis=0))
    return jnp.concatenate([_blocks_to_lanes(w_a), _blocks_to_lanes(w_x), vec, _pad_rows(norm_g), _pad_rows(fin_g),
                            _pad_rows(_lanes(sinks)), jnp.zeros((32, D), f32)], axis=0)


def kernel(x, norm_g, w_in, conv_w, conv_b, lru_w_a, lru_b_a, lru_w_x, lru_b_x, lru_lambda, attn_sinks, w_rnn_out, w_attn_out, w_o, final_norm_g, loss_target, m_norm_g, m_w_in, m_conv_w, m_conv_b, m_lru_w_a, m_lru_b_a, m_lru_w_x, m_lru_b_x, m_lru_lambda, m_attn_sinks, m_w_rnn_out, m_w_attn_out, m_w_o, m_final_norm_g, v_norm_g, v_w_in, v_conv_w, v_conv_b, v_lru_w_a, v_lru_b_a, v_lru_w_x, v_lru_b_x, v_lru_lambda, v_attn_sinks, v_w_rnn_out, v_w_attn_out, v_w_o, v_final_norm_g):
    nb, S, _ = x.shape
    T = nb * S
    x2d = x.reshape(T, D)
    tgt = loss_target.reshape(T, D)
    fin_g = final_norm_g.reshape(1, D)
    w_a3, w_x3 = lru_w_a[0], lru_w_x[0]

    wt_full, wr_full, wa_full, wo_full, cw_full = _gather_weights(
        w_in[0].T, w_rnn_out[0], w_attn_out[0], w_o[0], _pad_rows(conv_w[0]))

    h_bf, proj = _in_proj(x2d, norm_g, wt_full)
    y_rnn, h_all = _lru_forward(proj, cw_full, conv_b, w_a3, lru_b_a, w_x3, lru_b_x, lru_lambda, S)
    tabs = _rope_tables(S)
    y_attn = _attn_forward(proj, tabs, attn_sinks, S)

    (merged, dx2, dpr, dpa, dy_rnn, dy_attn, dmr, dma, loss_blk, gfin_blk) = _merge_and_head(
        x2d, tgt, proj, y_rnn, y_attn, wr_full, wa_full, wo_full, fin_g)

    dq, dkv, dga, dsink_blk = _attn_backward(proj, dy_attn, tabs, attn_sinks, S)
    du0, dgr, gwa, gwx, gvec, gcw = _lru_backward(proj, h_all, dy_rnn, cw_full, conv_b, w_a3, lru_b_a, w_x3,
                                                  lru_b_x, lru_lambda, S)
    dsecs = (du0, dgr, dq, dkv, dga, dmr, dma)
    grad_x2d, gnorm_blk = _input_grad(dsecs, wt_full, x2d, dx2, norm_g)

    g_wt = _w_in_grad(dsecs, h_bf)
    g_wr = _weight_grad(y_rnn, dpr, "w_rnn_out_grad")
    g_wa = _weight_grad(y_attn, dpa, "w_attn_out_grad")
    g_wo = _weight_grad(merged, dx2, "w_o_grad")
    g_small = jnp.concatenate([gwa, gwx, gvec, gnorm_blk, gfin_blk, _pad_rows(_lanes(dsink_blk[0:1, 0:16])),
                               jnp.zeros((32, D), f32)], axis=0)

    my_core = lax.axis_index("c").astype(jnp.int32).reshape(1)
    p_wt, p_wr, p_wa, p_wo, p_small, p_cw = _reduce_scatter(
        [g_wt, g_wr, g_wa, g_wo, g_small, gcw], [bf16, bf16, bf16, bf16, f32, f32], my_core)
    o_wt = _adamw(p_wt, w_in[0].T, m_w_in[0].T, v_w_in[0].T, "adamw_w_in")
    o_wr = _adamw(p_wr, w_rnn_out[0], m_w_rnn_out[0], v_w_rnn_out[0], "adamw_w_rnn_out")
    o_wa = _adamw(p_wa, w_attn_out[0], m_w_attn_out[0], v_w_attn_out[0], "adamw_w_attn_out")
    o_wo = _adamw(p_wo, w_o[0], m_w_o[0], v_w_o[0], "adamw_w_o")
    o_cw = _adamw(p_cw, _pad_rows(conv_w[0]), _pad_rows(m_conv_w[0]), _pad_rows(v_conv_w[0]), "adamw_conv_w")

    zero40 = jnp.zeros((p_small.shape[1], D), f32)
    small_sum = _adamw(p_small, zero40, zero40, zero40, "sum_small")[0]
    g_small_all = _gather_rows(small_sum, "gather_small")
    pack = lambda *t: _small_pack(*t)
    o_small = _adamw(
        g_small_all[None],
        pack(w_a3, w_x3, conv_b, lru_b_a, lru_b_x, lru_lambda, norm_g, fin_g, attn_sinks),
        pack(m_lru_w_a[0], m_lru_w_x[0], m_conv_b, m_lru_b_a, m_lru_b_x, m_lru_lambda, m_norm_g,
             m_final_norm_g.reshape(1, D), m_attn_sinks),
        pack(v_lru_w_a[0], v_lru_w_x[0], v_conv_b, v_lru_b_a, v_lru_b_x, v_lru_lambda, v_norm_g,
             v_final_norm_g.reshape(1, D), v_attn_sinks),
        "adamw_small")

    loss = lax.psum(loss_blk[0, 0] * 0.5 / D, ("x", "y", "c"))

    def unpack(kind):
        s = o_small[kind]
        return {
            "norm_g": s[264:265], "w_in": o_wt[kind].T[None], "conv_w": o_cw[kind][None, 0:4],
            "conv_b": s[256:257], "lru_w_a": _lanes_to_blocks(s[0:128])[None], "lru_b_a": s[257:258],
            "lru_w_x": _lanes_to_blocks(s[128:256])[None], "lru_b_x": s[258:259], "lru_lambda": s[259:260],
            "attn_sinks": s[280:281, 0:16], "w_rnn_out": o_wr[kind][None], "w_attn_out": o_wa[kind][None],
            "w_o": o_wo[kind][None], "final_norm_g": s[272, :],
        }

    order = ("norm_g", "w_in", "conv_w", "conv_b", "lru_w_a", "lru_b_a", "lru_w_x", "lru_b_x", "lru_lambda",
             "attn_sinks", "w_rnn_out", "w_attn_out", "w_o", "final_norm_g")
    outs = [loss, grad_x2d.reshape(nb, S, D)]
    for kind in range(4):
        d = unpack(kind)
        outs += [d[n] for n in order]
    return tuple(outs)
```

```python
import functools
import math

import jax
import jax.numpy as jnp
from jax import lax
from jax.experimental import pallas as pl
from jax.experimental.pallas import tpu as pltpu

f32 = jnp.float32
bf16 = jnp.bfloat16

D = 1024
D_IN = 6656
NDEV = 8
RNN_BLOCKS = 8
RB = 128
HEAD = 64
KV_HEADS = 4
GROUP = 4
QB = 128
LRU_C = 8.0
EPS = 1e-6
ROPE_DIM = 16
ROPE_THETA = 500000.0
CH = 512
SEC_START = (0, 2, 4, 6, 7, 9, 11)
SEC_CHUNKS = (2, 2, 2, 1, 2, 2, 2)
VMEM_LIMIT = 56 * 1024 * 1024

ADAM_LR, ADAM_B1, ADAM_B2, ADAM_EPS, ADAM_WD, ADAM_STEP = 0.001, 0.9, 0.999, 1e-08, 0.01, 10

MESH = pl.DeviceIdType.MESH
ANY = pl.BlockSpec(memory_space=pl.ANY)
VMEM_SPEC = pl.BlockSpec(memory_space=pltpu.VMEM)
SMEM_SPEC = pl.BlockSpec(memory_space=pltpu.SMEM)


def _pcall(body, **kw):
    return pl.pallas_call(body, **kw)


def _params(sem=None, **kw):
    if sem is not None:
        kw["dimension_semantics"] = sem
    return pltpu.CompilerParams(vmem_limit_bytes=VMEM_LIMIT, **kw)


def _sds(shape, dtype):
    return jax.ShapeDtypeStruct(shape, dtype)


def _dot(a, b, dims):
    return lax.dot_general(a, b, (dims, ((), ())), preferred_element_type=f32)


NN = ((1,), (0,))
NT = ((1,), (1,))
TN = ((0,), (0,))


def _sigmoid(v):
    return 1.0 / (1.0 + jnp.exp(-v))


def _my_place():
    return lax.axis_index("x"), lax.axis_index("y"), lax.axis_index("c")


def _gather_weights(wt, wr, wa, wo, cw):
    shards = (wt, wr, wa, wo, cw)
    nrows = tuple(a.shape[0] for a in shards)
    narr = len(shards)

    def body(wt_ref, wr_ref, wa_ref, wo_ref, cw_ref, o0, o1, o2, o3, o4, s0, s1, s2, s3,
             send_sems, recv_sems, local_sems):
        x, y, c = _my_place()
        me, sibling = (x, y, c), (x, y, 1 - c)
        chips = [(1 - x, y), (x, 1 - y), (1 - x, 1 - y)]
        ins = (wt_ref, wr_ref, wa_ref, wo_ref)
        stage = (s0, s1, s2, s3)
        for a in range(4):
            stage[a][...] = ins[a][...].astype(bf16)
        srcs = stage + (cw_ref,)
        outs = (o0, o1, o2, o3, o4)

        def rows(a, place):
            px, py, pc = place
            start = pl.multiple_of((4 * px + 2 * py + pc) * nrows[a], 8)
            return outs[a].at[pl.ds(start, nrows[a]), :]

        def copy(a, k, block, to, src=None):
            return pltpu.make_async_remote_copy(
                src_ref=rows(a, block) if src is None else src, dst_ref=rows(a, block),
                send_sem=send_sems.at[7 * a + k], recv_sem=recv_sems.at[7 * a + k],
                device_id=to, device_id_type=MESH)

        mine = [pltpu.make_async_copy(srcs[a], rows(a, me), local_sems.at[a]) for a in range(narr)]
        for cp in mine:
            cp.start()
        first = []
        for a in range(narr):
            first.append(copy(a, 0, me, sibling, src=srcs[a]))
            first += [copy(a, 1 + j, me, (*chip, c), src=srcs[a]) for j, chip in enumerate(chips)]
        for cp in first:
            cp.start()
        passed = []
        for j, chip in enumerate(chips):
            for a in range(narr):
                copy(a, 1 + j, (*chip, c), me).wait_recv()
                fwd = copy(a, 4 + j, (*chip, c), sibling)
                fwd.start()
                passed.append(fwd)
        for a in range(narr):
            copy(a, 0, sibling, me).wait_recv()
        for j, chip in enumerate(chips):
            for a in range(narr):
                copy(a, 4 + j, (*chip, 1 - c), me).wait_recv()
        for cp in first + passed:
            cp.wait_send()
        for cp in mine:
            cp.wait()

    out_shape = tuple(_sds((NDEV * r, a.shape[1]), bf16 if i < 4 else f32)
                      for i, (r, a) in enumerate(zip(nrows, shards)))
    return _pcall(
        body, name="gather_weights", out_shape=out_shape,
        in_specs=[VMEM_SPEC] * narr, out_specs=tuple([ANY] * narr),
        scratch_shapes=[pltpu.VMEM(a.shape, bf16) for a in shards[:4]] + [
            pltpu.SemaphoreType.DMA((7 * narr,)), pltpu.SemaphoreType.DMA((7 * narr,)),
            pltpu.SemaphoreType.DMA((narr,))],
        compiler_params=_params(),
    )(*shards)


def _gather_rows(blk, name):
    nrows, ncols = blk.shape

    def body(src, out, send_sems, recv_sems, local_sem):
        x, y, c = _my_place()
        me, sibling = (x, y, c), (x, y, 1 - c)
        chips = [(1 - x, y), (x, 1 - y), (1 - x, 1 - y)]

        def rows(place):
            px, py, pc = place
            start = pl.multiple_of((4 * px + 2 * py + pc) * nrows, 8)
            return out.at[pl.ds(start, nrows), :]

        def copy(k, block, to, from_src=False):
            return pltpu.make_async_remote_copy(
                src_ref=src if from_src else rows(block), dst_ref=rows(block),
                send_sem=send_sems.at[k], recv_sem=recv_sems.at[k], device_id=to, device_id_type=MESH)

        mine = pltpu.make_async_copy(src, rows(me), local_sem)
        mine.start()
        first = [copy(0, me, sibling, True)]
        first += [copy(1 + j, me, (*chip, c), True) for j, chip in enumerate(chips)]
        for cp in first:
            cp.start()
        passed = []
        for j, chip in enumerate(chips):
            copy(1 + j, (*chip, c), me).wait_recv()
            fwd = copy(4 + j, (*chip, c), sibling)
            fwd.start()
            passed.append(fwd)
        copy(0, sibling, me).wait_recv()
        for j, chip in enumerate(chips):
            copy(4 + j, (*chip, 1 - c), me).wait_recv()
        for cp in first + passed:
            cp.wait_send()
        mine.wait()

    return _pcall(
        body, name=name, out_shape=_sds((NDEV * nrows, ncols), blk.dtype),
        in_specs=[ANY], out_specs=ANY,
        scratch_shapes=[pltpu.SemaphoreType.DMA((7,)), pltpu.SemaphoreType.DMA((7,)), pltpu.SemaphoreType.DMA],
        compiler_params=_params(),
    )(blk)


def _pair_exchange(grads):
    narr = len(grads)
    nrows = tuple(g.shape[0] // NDEV for g in grads)
    views = [g.reshape(4, 2, r, g.shape[1]) for g, r in zip(grads, nrows)]

    def body(*refs):
        gin = refs[:narr]
        got = refs[narr:2 * narr]
        send_sems, recv_sems = refs[2 * narr:]
        x, y, c = _my_place()
        copies = [pltpu.make_async_remote_copy(
            src_ref=gin[a].at[:, pl.ds(1 - c, 1)], dst_ref=got[a],
            send_sem=send_sems.at[a], recv_sem=recv_sems.at[a],
            device_id=(x, y, 1 - c), device_id_type=MESH) for a in range(narr)]
        for cp in copies:
            cp.start()
        for cp in copies:
            cp.wait()

    out_shape = tuple(_sds((4, 1, r, g.shape[1]), f32) for r, g in zip(nrows, grads))
    got = _pcall(
        body, name="pair_exchange", out_shape=out_shape,
        in_specs=[ANY] * narr, out_specs=tuple([ANY] * narr),
        scratch_shapes=[pltpu.SemaphoreType.DMA((narr,)), pltpu.SemaphoreType.DMA((narr,))],
        compiler_params=_params(),
    )(*views)
    return views, [g.reshape(4, r, g.shape[3]) for g, r in zip(got, nrows)]


def _row_tile(rows, dtype):
    unit = 16 if dtype == bf16 else 8
    for cand in (256, 208, 128, 64, 40, 32, 16, 8):
        if rows % cand == 0 and cand % unit == 0:
            return cand
    return rows


def _chip_sum(view, got, my_core, out_dtype, name):
    _, _, r, cols = view.shape
    tr = _row_tile(r, out_dtype)

    def body(core_ref, mine_ref, got_ref, out_ref):
        out_ref[...] = (mine_ref[...] + got_ref[...]).astype(out_dtype)

    grid_spec = pltpu.PrefetchScalarGridSpec(
        num_scalar_prefetch=1, grid=(4, r // tr),
        in_specs=[pl.BlockSpec((None, None, tr, cols), lambda q, i, core: (q, core[0], i, 0)),
                  pl.BlockSpec((None, tr, cols), lambda q, i, core: (q, i, 0))],
        out_specs=pl.BlockSpec((None, tr, cols), lambda q, i, core: (q, i, 0)))
    return _pcall(body, name=name, grid_spec=grid_spec, out_shape=_sds((4, r, cols), out_dtype),
                  compiler_params=_params(("arbitrary", "arbitrary")))(my_core, view, got)


def _chip_exchange(sums):
    narr = len(sums)

    def body(*refs):
        src = refs[:narr]
        dst = refs[narr:2 * narr]
        send_sems, recv_sems, local_sems = refs[2 * narr:]
        x, y, c = _my_place()

        def copy(a, k):
            px, py = (x + (k >> 1)) % 2, (y + (k & 1)) % 2
            return pltpu.make_async_remote_copy(
                src_ref=src[a].at[2 * px + py], dst_ref=dst[a].at[k],
                send_sem=send_sems.at[3 * a + k - 1], recv_sem=recv_sems.at[3 * a + k - 1],
                device_id=(px, py, c), device_id_type=MESH)

        mine = [pltpu.make_async_copy(src[a].at[2 * x + y], dst[a].at[0], local_sems.at[a]) for a in range(narr)]
        for cp in mine:
            cp.start()
        sends = [copy(a, k) for k in (3, 1, 2) for a in range(narr)]
        for cp in sends:
            cp.start()
        for cp in sends:
            cp.wait()
        for cp in mine:
            cp.wait()

    out_shape = tuple(_sds(s.shape, s.dtype) for s in sums)
    return _pcall(
        body, name="chip_exchange", out_shape=out_shape,
        in_specs=[ANY] * narr, out_specs=tuple([ANY] * narr),
        scratch_shapes=[pltpu.SemaphoreType.DMA((3 * narr,)), pltpu.SemaphoreType.DMA((3 * narr,)),
                        pltpu.SemaphoreType.DMA((narr,))],
        compiler_params=_params(),
    )(*sums)


def _reduce_scatter(grads, wire_dtypes, my_core):
    views, got = _pair_exchange(grads)
    sums = [_chip_sum(v, g, my_core, dt, "chip_sum_%d" % a)
            for a, (v, g, dt) in enumerate(zip(views, got, wire_dtypes))]
    return _chip_exchange(sums)


def _adamw(parts, w, m, v, name):
    n, rows, cols = parts.shape
    tr = _row_tile(rows, parts.dtype)

    def body(p_ref, w_ref, m_ref, v_ref, g_out, d_out, m_out, v_out):
        g = p_ref[0].astype(f32)
        for s in range(1, n):
            g = g + p_ref[s].astype(f32)
        m_new = ADAM_B1 * m_ref[...] + (1.0 - ADAM_B1) * g
        v_new = ADAM_B2 * v_ref[...] + (1.0 - ADAM_B2) * (g * g)
        m_hat = m_new / (1.0 - ADAM_B1 ** ADAM_STEP)
        v_hat = v_new / (1.0 - ADAM_B2 ** ADAM_STEP)
        g_out[...] = g
        d_out[...] = -ADAM_LR * (m_hat / (jnp.sqrt(v_hat) + ADAM_EPS) + ADAM_WD * w_ref[...])
        m_out[...] = m_new
        v_out[...] = v_new

    blk = pl.BlockSpec((tr, cols), lambda i: (i, 0))
    return _pcall(
        body, name=name, grid=(rows // tr,),
        in_specs=[pl.BlockSpec((n, tr, cols), lambda i: (0, i, 0)), blk, blk, blk],
        out_specs=(blk, blk, blk, blk), out_shape=tuple(_sds((rows, cols), f32) for _ in range(4)),
        compiler_params=_params(("arbitrary",)),
    )(parts, w, m, v)


def _rope(t, c, s1, s2):
    w = t.shape[1]
    return t * c + pltpu.roll(t, w - 8, 1) * s1 + pltpu.roll(t, 8, 1) * s2


def _rope_transposed(dt, c, s1, s2):
    w = dt.shape[1]
    return dt * c + pltpu.roll(dt * s1, 8, 1) + pltpu.roll(dt * s2, w - 8, 1)


def _in_proj(x2d, norm_g, wt_full, tabs, S):
    T = x2d.shape[0]
    tb = min(S, 1024)
    q_scale = 1.0 / math.sqrt(HEAD)

    def body(x_ref, g_ref, wt_ref, c_ref, s1_ref, s2_ref, h_ref, proj_ref):
        j = pl.program_id(1)

        @pl.when(j == 0)
        def _():
            xv = x_ref[...]
            ms = jnp.mean(xv * xv, axis=-1, keepdims=True)
            h_ref[...] = (xv * lax.rsqrt(ms + EPS) * g_ref[...]).astype(bf16)

        acc = _dot(h_ref[...], wt_ref[...], NT)
        is_q = (j == 4) | (j == 5)
        is_kv = j == 6

        @pl.when(is_q)
        def _():
            tab = (c_ref[...], s1_ref[...], s2_ref[...])
            for l in range(CH // 128):
                sl = slice(128 * l, 128 * (l + 1))
                proj_ref[:, sl] = (_rope(acc[:, sl], *tab) * q_scale).astype(bf16)

        @pl.when(is_kv)
        def _():
            tab = (c_ref[...], s1_ref[...], s2_ref[...])
            for l in range(2):
                sl = slice(128 * l, 128 * (l + 1))
                proj_ref[:, sl] = _rope(acc[:, sl], *tab).astype(bf16)
            proj_ref[:, 256:] = acc[:, 256:].astype(bf16)

        @pl.when(jnp.logical_not(is_q | is_kv))
        def _():
            proj_ref[...] = acc.astype(bf16)

    tab = pl.BlockSpec((tb, 128), lambda i, j: (i % (S // tb), 0))
    return _pcall(
        body, name="in_proj", grid=(T // tb, D_IN // CH),
        in_specs=[pl.BlockSpec((tb, D), lambda i, j: (i, 0)), pl.BlockSpec((1, D), lambda i, j: (0, 0)),
                  pl.BlockSpec((CH, D), lambda i, j: (j, 0)), tab, tab, tab],
        out_specs=(pl.BlockSpec((tb, D), lambda i, j: (i, 0)), pl.BlockSpec((tb, CH), lambda i, j: (i, j))),
        out_shape=(_sds((T, D), bf16), _sds((T, D_IN), bf16)),
        compiler_params=_params(("arbitrary", "arbitrary")),
    )(x2d, norm_g, wt_full, *tabs)


def _rows_iota(shape):
    return lax.broadcasted_iota(jnp.int32, shape, 0)


def _shift_down(v, k):
    return jnp.where(_rows_iota(v.shape) >= k, pltpu.roll(v, k, 0), 0.0)


def _shift_up(v, k):
    n = v.shape[0]
    return jnp.where(_rows_iota(v.shape) < n - k, pltpu.roll(v, n - k, 0), 0.0)


def _scan_forward(a, b):
    n = a.shape[0]
    r = _rows_iota(a.shape)
    s = 1
    while s < n:
        keep = r >= s
        b = jnp.where(keep, a * pltpu.roll(b, s, 0) + b, b)
        if 2 * s < n:
            a = jnp.where(keep, a * pltpu.roll(a, s, 0), a)
        s *= 2
    return b


def _scan_reverse(a, b):
    n = a.shape[0]
    r = _rows_iota(a.shape)
    s = 1
    while s < n:
        keep = r < n - s
        b = jnp.where(keep, a * pltpu.roll(b, n - s, 0) + b, b)
        if 2 * s < n:
            a = jnp.where(keep, a * pltpu.roll(a, n - s, 0), a)
        s *= 2
    return b


def _neg_expm1(v):
    series = -v * (1.0 + v * (0.5 + v * (1.0 / 6.0 + v * (1.0 / 24.0))))
    return jnp.where(v > -0.03125, series, 1.0 - jnp.exp(v))


def _softplus_neg(lam):
    return jnp.maximum(-lam, 0.0) + jnp.log(1.0 + jnp.exp(-jnp.abs(lam)))


def _lru_gates(x0, cw, cb, wa, ba, wx, bx, lam):
    u = cb + cw[3:4, :] * x0
    for k in range(3):
        u = u + cw[k:k + 1, :] * _shift_down(x0, 3 - k)
    ub = u.astype(bf16)
    r = _sigmoid(_dot(ub, wa.astype(bf16), NN) + ba)
    i = _sigmoid(_dot(ub, wx.astype(bf16), NN) + bx)
    sp = _softplus_neg(lam)
    log_a = (-LRU_C) * r * sp
    a = jnp.exp(log_a)
    mult = jnp.sqrt(_neg_expm1(2.0 * log_a))
    return u, ub, r, i, sp, a, mult


def _lru_specs(S, nb):
    col = lambda off: pl.BlockSpec((S, RB), lambda n, b, off=off: (b, off + n))
    vec = pl.BlockSpec((1, RB), lambda n, b: (0, n))
    wblk = pl.BlockSpec((None, RB, RB), lambda n, b: (n, 0, 0))
    cwblk = pl.BlockSpec((8, RB), lambda n, b: (n, 0))
    return col, vec, wblk, cwblk


def _lru_forward(proj, cw_full, conv_b, w_a, b_a, w_x, b_x, lam, S):
    T = proj.shape[0]
    nb = T // S
    col, vec, wblk, cwblk = _lru_specs(S, nb)

    def body(x0_ref, g_ref, cw_ref, cb_ref, wa_ref, ba_ref, wx_ref, bx_ref, lam_ref, y_ref, h_ref):
        x0 = x0_ref[...].astype(f32)
        u, ub, r, i, sp, a, mult = _lru_gates(x0, cw_ref[...], cb_ref[...], wa_ref[...], ba_ref[...],
                                              wx_ref[...], bx_ref[...], lam_ref[...])
        h = _scan_forward(a, mult * (i * u))
        g = g_ref[...].astype(f32)
        h_ref[...] = h
        y_ref[...] = (h * (g * _sigmoid(g))).astype(bf16)

    out = pl.BlockSpec((S, RB), lambda n, b: (b, n))
    return _pcall(
        body, name="lru_forward", grid=(RNN_BLOCKS, nb),
        in_specs=[col(0), col(8), cwblk, vec, wblk, vec, wblk, vec, vec],
        out_specs=(out, out), out_shape=(_sds((T, D), bf16), _sds((T, D), f32)),
        compiler_params=_params(("arbitrary", "arbitrary")),
    )(proj, proj, cw_full, conv_b, w_a, b_a, w_x, b_x, lam)


def _rope_tables(S):
    pos = jnp.arange(S, dtype=f32)
    inv_freq = ROPE_THETA ** (-jnp.arange(0, ROPE_DIM, 2, dtype=f32) / ROPE_DIM)
    ang = pos[:, None] * inv_freq[None, :]
    cos, sin = jnp.cos(ang), jnp.sin(ang)
    lane = jnp.arange(128) % HEAD
    cosl, sinl = cos[:, lane % 8], sin[:, lane % 8]
    c = jnp.where(lane[None, :] < ROPE_DIM, cosl, 1.0)
    s1 = jnp.where(lane[None, :] < 8, -sinl, 0.0)
    s2 = jnp.where((lane[None, :] >= 8) & (lane[None, :] < ROPE_DIM), sinl, 0.0)
    return c.astype(f32), s1.astype(f32), s2.astype(f32)


def _heads_to_rows(t):
    return jnp.concatenate([t[:, HEAD * h:HEAD * (h + 1)] for h in range(GROUP)], axis=0)


def _rows_to_heads(t):
    return jnp.concatenate([t[QB * h:QB * (h + 1), :] for h in range(GROUP)], axis=1)


def _window_bias(first_block):
    shape = (GROUP * QB, 2 * QB)
    qi = _rows_iota(shape) % QB
    cj = lax.broadcasted_iota(jnp.int32, shape, 1)
    valid = (cj > qi) & (cj <= qi + QB) & ((cj >= QB) | jnp.logical_not(first_block))
    return jnp.where(valid, 0.0, -jnp.inf)


def _attn_probs(q_rows, k_cat, sink_col, bias):
    s = _dot(q_rows, k_cat, NT) + bias
    m = jnp.maximum(jnp.max(s, axis=1, keepdims=True), sink_col)
    p = jnp.exp(s - m)
    e_sink = jnp.exp(sink_col - m)
    inv = 1.0 / (jnp.sum(p, axis=1, keepdims=True) + e_sink)
    return p * inv, e_sink * inv


def _sink_column(sink_ref, kv):
    rid = _rows_iota((GROUP * QB, 1))
    col = jnp.zeros((GROUP * QB, 1), f32)
    for h in range(GROUP):
        col = jnp.where(rid // QB == h, sink_ref[0, GROUP * kv + h], col)
    return col


def _attn_in_specs(S):
    nq = S // QB
    last = nq - 1
    cur = lambda b, j: b * nq + jnp.minimum(j, last)
    prev = lambda b, j: b * nq + jnp.maximum(jnp.minimum(j, last) - 1, 0)
    specs = [
        pl.BlockSpec((QB, D), lambda b, j: (cur(b, j), 2)),
        pl.BlockSpec((QB, 256), lambda b, j: (cur(b, j), 12)),
        pl.BlockSpec((QB, 256), lambda b, j: (prev(b, j), 12)),
        pl.BlockSpec((QB, 256), lambda b, j: (cur(b, j), 13)),
        pl.BlockSpec((QB, 256), lambda b, j: (prev(b, j), 13)),
        pl.BlockSpec((QB, 512), lambda b, j: (cur(b, j), 7)),
        pl.BlockSpec((QB, 512), lambda b, j: (cur(b, j), 8)),
        SMEM_SPEC,
    ]
    return specs, cur, prev


def _attn_forward(proj, sinks, S):
    T = proj.shape[0]
    nb, nq = T // S, S // QB
    specs, cur, _ = _attn_in_specs(S)

    def body(q_ref, kc_ref, kp_ref, vc_ref, vp_ref, gl_ref, gh_ref, sink_ref, y_ref):
        bias = _window_bias(pl.program_id(1) == 0)
        kc, kp, vc, vp = kc_ref[...], kp_ref[...], vc_ref[...], vp_ref[...]
        for kv in range(KV_HEADS):
            lanes = slice(256 * kv, 256 * (kv + 1))
            hl = slice(HEAD * kv, HEAD * (kv + 1))
            q_rows = _heads_to_rows(q_ref[:, lanes])
            k_cat = jnp.concatenate([kp[:, hl], kc[:, hl]], axis=0)
            v_cat = jnp.concatenate([vp[:, hl], vc[:, hl]], axis=0)
            probs, _ = _attn_probs(q_rows, k_cat, _sink_column(sink_ref, kv), bias)
            o = _rows_to_heads(_dot(probs.astype(bf16), v_cat, NN))
            g_src = gl_ref if kv < 2 else gh_ref
            g = g_src[:, 256 * (kv % 2):256 * (kv % 2 + 1)].astype(f32)
            y_ref[:, lanes] = (o * (g * _sigmoid(g))).astype(bf16)

    args = [proj] * 7 + [sinks]
    return _pcall(
        body, name="attn_forward", grid=(nb, nq), in_specs=specs,
        out_specs=pl.BlockSpec((QB, D), lambda b, j: (cur(b, j), 0)), out_shape=_sds((T, D), bf16),
        compiler_params=_params(("arbitrary", "arbitrary")),
    )(*args)


def _merge_and_head(x2d, tgt, proj, y_rnn, y_attn, w_r, w_a, w_o, gfin):
    T = x2d.shape[0]
    tb = min(T, 256)
    nsteps = T // tb

    def body(x_ref, t_ref, mr0, mr1, ma0, ma1, yr_ref, ya_ref, wr_ref, wa_ref, wo_ref, gf_ref,
             merged_ref, dx2_ref, dpr_ref, dpa_ref, dyr_ref, dya_ref, dmr_ref, dma_ref, loss_ref, gfin_ref):
        @pl.when(pl.program_id(0) == 0)
        def _():
            loss_ref[...] = jnp.zeros_like(loss_ref)
            gfin_ref[...] = jnp.zeros_like(gfin_ref)

        sr = _sigmoid(jnp.concatenate([mr0[...], mr1[...]], axis=1).astype(f32))
        sa = _sigmoid(jnp.concatenate([ma0[...], ma1[...]], axis=1).astype(f32))
        p_r = _dot(yr_ref[...], wr_ref[...], NN)
        p_a = _dot(ya_ref[...], wa_ref[...], NN)
        merged = (sr * p_r + sa * p_a).astype(bf16)
        merged_ref[...] = merged
        x2 = x_ref[...] + _dot(merged, wo_ref[...], NN)
        rstd = lax.rsqrt(jnp.mean(x2 * x2, axis=-1, keepdims=True) + EPS)
        xh = x2 * rstd
        gf = gf_ref[...]
        err = xh * gf - t_ref[...]
        loss_ref[...] += jnp.sum(err * err)
        dy = err * (1.0 / D)
        gfin_ref[0:1, :] += jnp.sum(dy * xh, axis=0, keepdims=True)
        dxn = dy * gf
        dx2 = rstd * (dxn - xh * jnp.mean(dxn * xh, axis=-1, keepdims=True))
        dx2_ref[...] = dx2
        dmerged = _dot(dx2.astype(bf16), wo_ref[...], NT)
        dmr_ref[...] = (dmerged * p_r * (sr * (1.0 - sr))).astype(bf16)
        dma_ref[...] = (dmerged * p_a * (sa * (1.0 - sa))).astype(bf16)
        dpr = (dmerged * sr).astype(bf16)
        dpa = (dmerged * sa).astype(bf16)
        dpr_ref[...] = dpr
        dpa_ref[...] = dpa
        dyr_ref[...] = _dot(dpr, wr_ref[...], NT)
        dya_ref[...] = _dot(dpa, wa_ref[...], NT)

    tok = pl.BlockSpec((tb, D), lambda i: (i, 0))
    half = lambda c: pl.BlockSpec((tb, CH), lambda i, c=c: (i, c))
    wfull = pl.BlockSpec((D, D), lambda i: (0, 0))
    acc = pl.BlockSpec((8, D), lambda i: (0, 0))
    return _pcall(
        body, name="merge_and_head", grid=(nsteps,),
        in_specs=[tok, tok, half(9), half(10), half(11), half(12), tok, tok, wfull, wfull, wfull,
                  pl.BlockSpec((1, D), lambda i: (0, 0))],
        out_specs=(tok, tok, tok, tok, tok, tok, tok, tok, acc, acc),
        out_shape=(_sds((T, D), bf16), _sds((T, D), f32), _sds((T, D), bf16), _sds((T, D), bf16),
                   _sds((T, D), f32), _sds((T, D), f32), _sds((T, D), bf16), _sds((T, D), bf16),
                   _sds((8, D), f32), _sds((8, D), f32)),
        compiler_params=_params(("arbitrary",)),
    )(x2d, tgt, proj, proj, proj, proj, y_rnn, y_attn, w_r, w_a, w_o, gfin)


def _attn_backward(proj, dy_attn, tabs, sinks, S):
    T = proj.shape[0]
    nb, nq = T // S, S // QB
    specs, cur, prev = _attn_in_specs(S)
    last = nq - 1
    tab_cur = pl.BlockSpec((QB, 128), lambda b, j: (jnp.minimum(j, last), 0))
    tab_prev = pl.BlockSpec((QB, 128), lambda b, j: (jnp.maximum(jnp.minimum(j, last) - 1, 0), 0))
    specs = specs + [pl.BlockSpec((QB, D), lambda b, j: (cur(b, j), 0))] + [tab_cur] * 3 + [tab_prev] * 3
    q_scale = 1.0 / math.sqrt(HEAD)

    def rope_back(dt, tab):
        return jnp.concatenate([_rope_transposed(dt[:, 128 * l:128 * (l + 1)], *tab) for l in range(2)], axis=1)

    def body(q_ref, kc_ref, kp_ref, vc_ref, vp_ref, gl_ref, gh_ref, sink_ref, dy_ref, cc, s1c, s2c, cp, s1p, s2p,
             dq_ref, dkv_ref, dg_ref, dsink_ref, carry_k, carry_v):
        b, j = pl.program_id(0), pl.program_id(1)

        @pl.when((b == 0) & (j == 0))
        def _():
            dsink_ref[...] = jnp.zeros_like(dsink_ref)

        @pl.when(j == 0)
        def _():
            carry_k[...] = jnp.zeros_like(carry_k)
            carry_v[...] = jnp.zeros_like(carry_v)

        @pl.when(j < nq)
        def _():
            bias = _window_bias(j == 0)
            tc = (cc[...], s1c[...], s2c[...])
            tp = (cp[...], s1p[...], s2p[...])
            kc, kp, vc, vp = kc_ref[...], kp_ref[...], vc_ref[...], vp_ref[...]
            dk_prev, dk_cur, dv_prev, dv_cur = [], [], [], []
            dsink_acc = jnp.zeros((8, 128), f32)
            r8 = lax.broadcasted_iota(jnp.int32, (8, 128), 0)
            l8 = lax.broadcasted_iota(jnp.int32, (8, 128), 1)
            for kv in range(KV_HEADS):
                lanes = slice(256 * kv, 256 * (kv + 1))
                hl = slice(HEAD * kv, HEAD * (kv + 1))
                q_rows = _heads_to_rows(q_ref[:, lanes])
                k_cat = jnp.concatenate([kp[:, hl], kc[:, hl]], axis=0)
                v_cat = jnp.concatenate([vp[:, hl], vc[:, hl]], axis=0)
                probs, p_sink = _attn_probs(q_rows, k_cat, _sink_column(sink_ref, kv), bias)
                pb = probs.astype(bf16)
                o = _rows_to_heads(_dot(pb, v_cat, NN))
                g_src = gl_ref if kv < 2 else gh_ref
                g = g_src[:, 256 * (kv % 2):256 * (kv % 2 + 1)].astype(f32)
                sg = _sigmoid(g)
                dy = dy_ref[:, lanes]
                dg_ref[:, lanes] = (dy * o * (sg * (1.0 + g * (1.0 - sg)))).astype(bf16)
                do_rows = _heads_to_rows(dy * (g * sg)).astype(bf16)
                dv = _dot(pb, do_rows, TN)
                dp = _dot(do_rows, v_cat, NT)
                rowdot = jnp.sum(probs * dp, axis=1, keepdims=True)
                ds = (probs * (dp - rowdot)).astype(bf16)
                sink_rows = -(p_sink * rowdot)
                for h in range(GROUP):
                    val = jnp.sum(sink_rows[QB * h:QB * (h + 1), :])
                    dsink_acc = dsink_acc + jnp.where((r8 == 0) & (l8 == GROUP * kv + h), val, 0.0)
                dq = _rows_to_heads(_dot(ds, k_cat, NN)) * q_scale
                dq_ref[:, lanes] = rope_back(dq, tc).astype(bf16)
                dk = _dot(ds, q_rows, TN)
                dk_prev.append(dk[:QB, :])
                dk_cur.append(dk[QB:, :])
                dv_prev.append(dv[:QB, :])
                dv_cur.append(dv[QB:, :])
            dsink_ref[...] += dsink_acc
            dkp = rope_back(jnp.concatenate(dk_prev, axis=1), tp)
            dkc = rope_back(jnp.concatenate(dk_cur, axis=1), tc)
            dkv_ref[:, 0:256] = (carry_k[...] + dkp).astype(bf16)
            dkv_ref[:, 256:512] = (carry_v[...] + jnp.concatenate(dv_prev, axis=1)).astype(bf16)
            carry_k[...] = dkc
            carry_v[...] = jnp.concatenate(dv_cur, axis=1)

        @pl.when(j == nq)
        def _():
            dkv_ref[:, 0:256] = carry_k[...].astype(bf16)
            dkv_ref[:, 256:512] = carry_v[...].astype(bf16)

    lag = lambda b, j: (b * nq + jnp.maximum(j - 1, 0), 0)
    args = [proj] * 7 + [sinks, dy_attn] + list(tabs) + list(tabs)
    return _pcall(
        body, name="attn_backward", grid=(nb, nq + 1), in_specs=specs,
        out_specs=(pl.BlockSpec((QB, D), lambda b, j: (cur(b, j), 0)), pl.BlockSpec((QB, 512), lag),
                   pl.BlockSpec((QB, D), lambda b, j: (cur(b, j), 0)), pl.BlockSpec((8, 128), lambda b, j: (0, 0))),
        out_shape=(_sds((T, D), bf16), _sds((T, 512), bf16), _sds((T, D), bf16), _sds((8, 128), f32)),
        scratch_shapes=[pltpu.VMEM((QB, 256), f32), pltpu.VMEM((QB, 256), f32)],
        compiler_params=_params(("arbitrary", "arbitrary")),
    )(*args)


def _lru_backward(proj, h_all, dy_rnn, cw_full, conv_b, w_a, b_a, w_x, b_x, lam, S):
    T = proj.shape[0]
    nb = T // S
    col, vec, wblk, cwblk = _lru_specs(S, nb)
    tokblk = pl.BlockSpec((S, RB), lambda n, b: (b, n))

    def body(x0_ref, g_ref, h_ref, dy_ref, cw_ref, cb_ref, wa_ref, ba_ref, wx_ref, bx_ref, lam_ref,
             du0_ref, dg_ref, gwa_ref, gwx_ref, vec_ref, gcw_ref):
        @pl.when(pl.program_id(1) == 0)
        def _():
            gwa_ref[...] = jnp.zeros_like(gwa_ref)
            gwx_ref[...] = jnp.zeros_like(gwx_ref)
            vec_ref[...] = jnp.zeros_like(vec_ref)
            gcw_ref[...] = jnp.zeros_like(gcw_ref)

        x0 = x0_ref[...].astype(f32)
        cw = cw_ref[...]
        lam_v = lam_ref[...]
        u, ub, r, i, sp, a, mult = _lru_gates(x0, cw, cb_ref[...], wa_ref[...], ba_ref[...],
                                              wx_ref[...], bx_ref[...], lam_v)
        h = h_ref[...]
        g = g_ref[...].astype(f32)
        dy = dy_ref[...]
        sg = _sigmoid(g)
        dg_ref[...] = (dy * h * (sg * (1.0 + g * (1.0 - sg)))).astype(bf16)
        dh_total = _scan_reverse(_shift_up(a, 1), dy * (g * sg))
        da = dh_total * _shift_down(h, 1)
        iu = i * u
        dmult = dh_total * iu
        di = dh_total * mult * u
        du = dh_total * mult * i
        dlog_a = a * (da - dmult * a / mult)
        dr = dlog_a * ((-LRU_C) * sp)
        dsp = jnp.sum(dlog_a * ((-LRU_C) * r), axis=0, keepdims=True)
        dpre_r = dr * r * (1.0 - r)
        dpre_i = di * i * (1.0 - i)
        dpre_rb = dpre_r.astype(bf16)
        dpre_ib = dpre_i.astype(bf16)
        du = du + _dot(dpre_rb, wa_ref[...].astype(bf16), NT) + _dot(dpre_ib, wx_ref[...].astype(bf16), NT)
        gwa_ref[...] += _dot(ub, dpre_rb, TN)
        gwx_ref[...] += _dot(ub, dpre_ib, TN)
        vec_ref[0:1, :] += jnp.sum(du, axis=0, keepdims=True)
        vec_ref[1:2, :] += jnp.sum(dpre_r, axis=0, keepdims=True)
        vec_ref[2:3, :] += jnp.sum(dpre_i, axis=0, keepdims=True)
        vec_ref[3:4, :] += dsp * (-_sigmoid(-lam_v))
        dx0 = cw[3:4, :] * du
        gcw_ref[3:4, :] += jnp.sum(du * x0, axis=0, keepdims=True)
        for k in range(3):
            dx0 = dx0 + cw[k:k + 1, :] * _shift_up(du, 3 - k)
            gcw_ref[k:k + 1, :] += jnp.sum(du * _shift_down(x0, 3 - k), axis=0, keepdims=True)
        du0_ref[...] = dx0.astype(bf16)

    wacc = pl.BlockSpec((RB, RB), lambda n, b: (0, n))
    vacc = pl.BlockSpec((8, RB), lambda n, b: (0, n))
    cacc = pl.BlockSpec((8, RB), lambda n, b: (n, 0))
    return _pcall(
        body, name="lru_backward", grid=(RNN_BLOCKS, nb),
        in_specs=[col(0), col(8), tokblk, tokblk, cwblk, vec, wblk, vec, wblk, vec, vec],
        out_specs=(tokblk, tokblk, wacc, wacc, vacc, cacc),
        out_shape=(_sds((T, D), bf16), _sds((T, D), bf16), _sds((RB, D), f32), _sds((RB, D), f32),
                   _sds((8, D), f32), _sds((8 * RNN_BLOCKS, RB), f32)),
        compiler_params=_params(("arbitrary", "arbitrary")),
    )(proj, proj, h_all, dy_rnn, cw_full, conv_b, w_a, b_a, w_x, b_x, lam)


def _section_of_chunk(s):
    out = []
    for start, n in zip(SEC_START, SEC_CHUNKS):
        inside = (s >= start) & (s < start + n)
        out.append((inside, jnp.clip(s - start, 0, n - 1)))
    return out


def _input_grad(dsecs, wt_full, x2d, dx2, norm_g):
    T = x2d.shape[0]
    tb = min(T, 1024)
    nchunks = D_IN // CH
    nsec = len(dsecs)

    def body(*refs):
        secs = refs[:nsec]
        wt_ref, x_ref, dx2_ref, g_ref, dx_ref, gnorm_ref, acc = refs[nsec:]
        i, s = pl.program_id(0), pl.program_id(1)

        @pl.when((i == 0) & (s == 0))
        def _():
            gnorm_ref[...] = jnp.zeros_like(gnorm_ref)

        @pl.when(s == 0)
        def _():
            acc[...] = jnp.zeros_like(acc)

        for a, (start, n) in enumerate(zip(SEC_START, SEC_CHUNKS)):
            @pl.when((s >= start) & (s < start + n))
            def _(a=a):
                acc[...] += _dot(secs[a][...], wt_ref[...], NN)

        @pl.when(s == nchunks - 1)
        def _():
            xv = x_ref[...]
            rstd = lax.rsqrt(jnp.mean(xv * xv, axis=-1, keepdims=True) + EPS)
            xh = xv * rstd
            dh = acc[...]
            gnorm_ref[0:1, :] += jnp.sum(dh * xh, axis=0, keepdims=True)
            dxn = dh * g_ref[...]
            dx_ref[...] = dx2_ref[...] + rstd * (dxn - xh * jnp.mean(dxn * xh, axis=-1, keepdims=True))

    def sec_spec(a):
        return pl.BlockSpec((tb, CH), lambda i, s, a=a: (i, _section_of_chunk(s)[a][1]))

    tok = pl.BlockSpec((tb, D), lambda i, s: (i, 0))
    return _pcall(
        body, name="input_grad", grid=(T // tb, nchunks),
        in_specs=[sec_spec(a) for a in range(nsec)] + [pl.BlockSpec((CH, D), lambda i, s: (s, 0)), tok, tok,
                                                        pl.BlockSpec((1, D), lambda i, s: (0, 0))],
        out_specs=(tok, pl.BlockSpec((8, D), lambda i, s: (0, 0))),
        out_shape=(_sds((T, D), f32), _sds((8, D), f32)),
        scratch_shapes=[pltpu.VMEM((tb, D), f32)],
        compiler_params=_params(("arbitrary", "arbitrary")),
    )(*dsecs, wt_full, x2d, dx2, norm_g)


def _w_in_grad(dsecs, h_bf):
    T = h_bf.shape[0]
    tk = min(T, 1024)
    nchunks = D_IN // CH
    nsec = len(dsecs)

    def body(*refs):
        secs = refs[:nsec]
        h_ref, out_ref = refs[nsec:]
        s, t = pl.program_id(0), pl.program_id(1)

        @pl.when(t == 0)
        def _():
            out_ref[...] = jnp.zeros_like(out_ref)

        for a, (start, n) in enumerate(zip(SEC_START, SEC_CHUNKS)):
            @pl.when((s >= start) & (s < start + n))
            def _(a=a):
                out_ref[...] += _dot(secs[a][...], h_ref[...], TN)

    def sec_spec(a):
        def index(s, t, a=a):
            inside, local = _section_of_chunk(s)[a]
            return (jnp.where(inside, t, 0), local)
        return pl.BlockSpec((tk, CH), index)

    return _pcall(
        body, name="w_in_grad", grid=(nchunks, T // tk),
        in_specs=[sec_spec(a) for a in range(nsec)] + [pl.BlockSpec((tk, D), lambda s, t: (t, 0))],
        out_specs=pl.BlockSpec((CH, D), lambda s, t: (s, 0)), out_shape=_sds((D_IN, D), f32),
        compiler_params=_params(("arbitrary", "arbitrary")),
    )(*dsecs, h_bf)


def _weight_grad(a_mat, b_mat, name):
    T, M = a_mat.shape
    N = b_mat.shape[1]
    tk = min(T, 1024)
    tm = 512

    def body(a_ref, b_ref, out_ref):
        @pl.when(pl.program_id(1) == 0)
        def _():
            out_ref[...] = jnp.zeros_like(out_ref)
        out_ref[...] += _dot(a_ref[...].astype(bf16), b_ref[...].astype(bf16), TN)

    return _pcall(
        body, name=name, grid=(M // tm, T // tk),
        in_specs=[pl.BlockSpec((tk, tm), lambda m, t: (t, m)), pl.BlockSpec((tk, N), lambda m, t: (t, 0))],
        out_specs=pl.BlockSpec((tm, N), lambda m, t: (m, 0)), out_shape=_sds((M, N), f32),
        compiler_params=_params(("arbitrary", "arbitrary")),
    )(a_mat, b_mat)


def _pad_rows(v, rows=8):
    return jnp.concatenate([v, jnp.zeros((rows - v.shape[0], v.shape[1]), v.dtype)], axis=0)


def _lanes(v):
    return jnp.pad(v, ((0, 0), (0, D - v.shape[1])))


def _blocks_to_lanes(w):
    return jnp.transpose(w, (1, 0, 2)).reshape(RB, D)


def _lanes_to_blocks(w):
    return jnp.transpose(w.reshape(RB, RNN_BLOCKS, RB), (1, 0, 2))


def _small_pack(w_a, w_x, conv_b, b_a, b_x, lam, norm_g, fin_g, sinks):
    vec = _pad_rows(jnp.concatenate([conv_b, b_a, b_x, lam], axis=0))
    return jnp.concatenate([_blocks_to_lanes(w_a), _blocks_to_lanes(w_x), vec, _pad_rows(norm_g), _pad_rows(fin_g),
                            _pad_rows(_lanes(sinks)), jnp.zeros((32, D), f32)], axis=0)


def kernel(x, norm_g, w_in, conv_w, conv_b, lru_w_a, lru_b_a, lru_w_x, lru_b_x, lru_lambda, attn_sinks, w_rnn_out, w_attn_out, w_o, final_norm_g, loss_target, m_norm_g, m_w_in, m_conv_w, m_conv_b, m_lru_w_a, m_lru_b_a, m_lru_w_x, m_lru_b_x, m_lru_lambda, m_attn_sinks, m_w_rnn_out, m_w_attn_out, m_w_o, m_final_norm_g, v_norm_g, v_w_in, v_conv_w, v_conv_b, v_lru_w_a, v_lru_b_a, v_lru_w_x, v_lru_b_x, v_lru_lambda, v_attn_sinks, v_w_rnn_out, v_w_attn_out, v_w_o, v_final_norm_g):
    nb, S, _ = x.shape
    T = nb * S
    x2d = x.reshape(T, D)
    tgt = loss_target.reshape(T, D)
    fin_g = final_norm_g.reshape(1, D)
    w_a3, w_x3 = lru_w_a[0], lru_w_x[0]

    wt_full, wr_full, wa_full, wo_full, cw_full = _gather_weights(
        w_in[0].T, w_rnn_out[0], w_attn_out[0], w_o[0], _pad_rows(conv_w[0]))

    tabs = _rope_tables(S)
    h_bf, proj = _in_proj(x2d, norm_g, wt_full, tabs, S)
    y_rnn, h_all = _lru_forward(proj, cw_full, conv_b, w_a3, lru_b_a, w_x3, lru_b_x, lru_lambda, S)
    y_attn = _attn_forward(proj, attn_sinks, S)

    (merged, dx2, dpr, dpa, dy_rnn, dy_attn, dmr, dma, loss_blk, gfin_blk) = _merge_and_head(
        x2d, tgt, proj, y_rnn, y_attn, wr_full, wa_full, wo_full, fin_g)

    dq, dkv, dga, dsink_blk = _attn_backward(proj, dy_attn, tabs, attn_sinks, S)
    du0, dgr, gwa, gwx, gvec, gcw = _lru_backward(proj, h_all, dy_rnn, cw_full, conv_b, w_a3, lru_b_a, w_x3,
                                                  lru_b_x, lru_lambda, S)
    dsecs = (du0, dgr, dq, dkv, dga, dmr, dma)
    grad_x2d, gnorm_blk = _input_grad(dsecs, wt_full, x2d, dx2, norm_g)

    g_wt = _w_in_grad(dsecs, h_bf)
    g_wr = _weight_grad(y_rnn, dpr, "w_rnn_out_grad")
    g_wa = _weight_grad(y_attn, dpa, "w_attn_out_grad")
    g_wo = _weight_grad(merged, dx2, "w_o_grad")
    g_small = jnp.concatenate([gwa, gwx, gvec, gnorm_blk, gfin_blk, _pad_rows(_lanes(dsink_blk[0:1, 0:16])),
                               jnp.zeros((32, D), f32)], axis=0)

    my_core = lax.axis_index("c").astype(jnp.int32).reshape(1)
    p_wt, p_wr, p_wa, p_wo, p_small, p_cw = _reduce_scatter(
        [g_wt, g_wr, g_wa, g_wo, g_small, gcw], [bf16, bf16, bf16, bf16, f32, f32], my_core)
    o_wt = _adamw(p_wt, w_in[0].T, m_w_in[0].T, v_w_in[0].T, "adamw_w_in")
    o_wr = _adamw(p_wr, w_rnn_out[0], m_w_rnn_out[0], v_w_rnn_out[0], "adamw_w_rnn_out")
    o_wa = _adamw(p_wa, w_attn_out[0], m_w_attn_out[0], v_w_attn_out[0], "adamw_w_attn_out")
    o_wo = _adamw(p_wo, w_o[0], m_w_o[0], v_w_o[0], "adamw_w_o")
    o_cw = _adamw(p_cw, _pad_rows(conv_w[0]), _pad_rows(m_conv_w[0]), _pad_rows(v_conv_w[0]), "adamw_conv_w")

    zero40 = jnp.zeros((p_small.shape[1], D), f32)
    small_sum = _adamw(p_small, zero40, zero40, zero40, "sum_small")[0]
    g_small_all = _gather_rows(small_sum, "gather_small")
    pack = lambda *t: _small_pack(*t)
    o_small = _adamw(
        g_small_all[None],
        pack(w_a3, w_x3, conv_b, lru_b_a, lru_b_x, lru_lambda, norm_g, fin_g, attn_sinks),
        pack(m_lru_w_a[0], m_lru_w_x[0], m_conv_b, m_lru_b_a, m_lru_b_x, m_lru_lambda, m_norm_g,
             m_final_norm_g.reshape(1, D), m_attn_sinks),
        pack(v_lru_w_a[0], v_lru_w_x[0], v_conv_b, v_lru_b_a, v_lru_b_x, v_lru_lambda, v_norm_g,
             v_final_norm_g.reshape(1, D), v_attn_sinks),
        "adamw_small")

    loss = lax.psum(loss_blk[0, 0] * 0.5 / D, ("x", "y", "c"))

    def unpack(kind):
        s = o_small[kind]
        return {
            "norm_g": s[264:265], "w_in": o_wt[kind].T[None], "conv_w": o_cw[kind][None, 0:4],
            "conv_b": s[256:257], "lru_w_a": _lanes_to_blocks(s[0:128])[None], "lru_b_a": s[257:258],
            "lru_w_x": _lanes_to_blocks(s[128:256])[None], "lru_b_x": s[258:259], "lru_lambda": s[259:260],
            "attn_sinks": s[280:281, 0:16], "w_rnn_out": o_wr[kind][None], "w_attn_out": o_wa[kind][None],
            "w_o": o_wo[kind][None], "final_norm_g": s[272, :],
        }

    order = ("norm_g", "w_in", "conv_w", "conv_b", "lru_w_a", "lru_b_a", "lru_w_x", "lru_b_x", "lru_lambda",
             "attn_sinks", "w_rnn_out", "w_attn_out", "w_o", "final_norm_g")
    outs = [loss, grad_x2d.reshape(nb, S, D)]
    for kind in range(4):
        d = unpack(kind)
        outs += [d[n] for n in order]
    return tuple(outs)
```

```python
import functools
import math

import jax
import jax.numpy as jnp
from jax import lax
from jax.experimental import pallas as pl
from jax.experimental.pallas import tpu as pltpu

f32 = jnp.float32
bf16 = jnp.bfloat16

D = 1024
D_IN = 6656
NDEV = 8
RNN_BLOCKS = 8
RB = 128
HEAD = 64
KV_HEADS = 4
GROUP = 4
QB = 128
LRU_C = 8.0
EPS = 1e-6
ROPE_DIM = 16
ROPE_THETA = 500000.0
CH = 512
SEC_START = (0, 2, 4, 6, 7, 9, 11)
SEC_CHUNKS = (2, 2, 2, 1, 2, 2, 2)
VMEM_LIMIT = 56 * 1024 * 1024

ADAM_LR, ADAM_B1, ADAM_B2, ADAM_EPS, ADAM_WD, ADAM_STEP = 0.001, 0.9, 0.999, 1e-08, 0.01, 10

MESH = pl.DeviceIdType.MESH
ANY = pl.BlockSpec(memory_space=pl.ANY)
VMEM_SPEC = pl.BlockSpec(memory_space=pltpu.VMEM)
SMEM_SPEC = pl.BlockSpec(memory_space=pltpu.SMEM)


def _pcall(body, **kw):
    return pl.pallas_call(body, **kw)


def _params(sem=None, **kw):
    if sem is not None:
        kw["dimension_semantics"] = sem
    return pltpu.CompilerParams(vmem_limit_bytes=VMEM_LIMIT, **kw)


def _sds(shape, dtype):
    return jax.ShapeDtypeStruct(shape, dtype)


def _dot(a, b, dims):
    return lax.dot_general(a, b, (dims, ((), ())), preferred_element_type=f32)


NN = ((1,), (0,))
NT = ((1,), (1,))
TN = ((0,), (0,))


def _sigmoid(v):
    return 0.5 * jnp.tanh(0.5 * v) + 0.5


def _sigmoid_positive(v):
    return 1.0 / (1.0 + jnp.exp(-v))


def _my_place():
    return lax.axis_index("x"), lax.axis_index("y"), lax.axis_index("c")


def _gather_weights(wt, wr, wa, wo, cw):
    shards = (wt, wr, wa, wo, cw)
    nrows = tuple(a.shape[0] for a in shards)
    narr = len(shards)

    def body(wt_ref, wr_ref, wa_ref, wo_ref, cw_ref, o0, o1, o2, o3, o4, s0, s1, s2, s3,
             send_sems, recv_sems, local_sems):
        x, y, c = _my_place()
        me, sibling = (x, y, c), (x, y, 1 - c)
        chips = [(1 - x, y), (x, 1 - y), (1 - x, 1 - y)]
        ins = (wt_ref, wr_ref, wa_ref, wo_ref)
        stage = (s0, s1, s2, s3)
        for a in range(4):
            stage[a][...] = ins[a][...].astype(bf16)
        srcs = stage + (cw_ref,)
        outs = (o0, o1, o2, o3, o4)

        def rows(a, place):
            px, py, pc = place
            start = pl.multiple_of((4 * px + 2 * py + pc) * nrows[a], 8)
            return outs[a].at[pl.ds(start, nrows[a]), :]

        def copy(a, k, block, to, src=None):
            return pltpu.make_async_remote_copy(
                src_ref=rows(a, block) if src is None else src, dst_ref=rows(a, block),
                send_sem=send_sems.at[7 * a + k], recv_sem=recv_sems.at[7 * a + k],
                device_id=to, device_id_type=MESH)

        mine = [pltpu.make_async_copy(srcs[a], rows(a, me), local_sems.at[a]) for a in range(narr)]
        for cp in mine:
            cp.start()
        first = []
        for a in range(narr):
            first.append(copy(a, 0, me, sibling, src=srcs[a]))
            first += [copy(a, 1 + j, me, (*chip, c), src=srcs[a]) for j, chip in enumerate(chips)]
        for cp in first:
            cp.start()
        passed = []
        for j, chip in enumerate(chips):
            for a in range(narr):
                copy(a, 1 + j, (*chip, c), me).wait_recv()
                fwd = copy(a, 4 + j, (*chip, c), sibling)
                fwd.start()
                passed.append(fwd)
        for a in range(narr):
            copy(a, 0, sibling, me).wait_recv()
        for j, chip in enumerate(chips):
            for a in range(narr):
                copy(a, 4 + j, (*chip, 1 - c), me).wait_recv()
        for cp in first + passed:
            cp.wait_send()
        for cp in mine:
            cp.wait()

    out_shape = tuple(_sds((NDEV * r, a.shape[1]), bf16 if i < 4 else f32)
                      for i, (r, a) in enumerate(zip(nrows, shards)))
    return _pcall(
        body, name="gather_weights", out_shape=out_shape,
        in_specs=[VMEM_SPEC] * narr, out_specs=tuple([ANY] * narr),
        scratch_shapes=[pltpu.VMEM(a.shape, bf16) for a in shards[:4]] + [
            pltpu.SemaphoreType.DMA((7 * narr,)), pltpu.SemaphoreType.DMA((7 * narr,)),
            pltpu.SemaphoreType.DMA((narr,))],
        compiler_params=_params(),
    )(*shards)


def _gather_rows(blk, name):
    nrows, ncols = blk.shape

    def body(src, out, send_sems, recv_sems, local_sem):
        x, y, c = _my_place()
        me, sibling = (x, y, c), (x, y, 1 - c)
        chips = [(1 - x, y), (x, 1 - y), (1 - x, 1 - y)]

        def rows(place):
            px, py, pc = place
            start = pl.multiple_of((4 * px + 2 * py + pc) * nrows, 8)
            return out.at[pl.ds(start, nrows), :]

        def copy(k, block, to, from_src=False):
            return pltpu.make_async_remote_copy(
                src_ref=src if from_src else rows(block), dst_ref=rows(block),
                send_sem=send_sems.at[k], recv_sem=recv_sems.at[k], device_id=to, device_id_type=MESH)

        mine = pltpu.make_async_copy(src, rows(me), local_sem)
        mine.start()
        first = [copy(0, me, sibling, True)]
        first += [copy(1 + j, me, (*chip, c), True) for j, chip in enumerate(chips)]
        for cp in first:
            cp.start()
        passed = []
        for j, chip in enumerate(chips):
            copy(1 + j, (*chip, c), me).wait_recv()
            fwd = copy(4 + j, (*chip, c), sibling)
            fwd.start()
            passed.append(fwd)
        copy(0, sibling, me).wait_recv()
        for j, chip in enumerate(chips):
            copy(4 + j, (*chip, 1 - c), me).wait_recv()
        for cp in first + passed:
            cp.wait_send()
        mine.wait()

    return _pcall(
        body, name=name, out_shape=_sds((NDEV * nrows, ncols), blk.dtype),
        in_specs=[ANY], out_specs=ANY,
        scratch_shapes=[pltpu.SemaphoreType.DMA((7,)), pltpu.SemaphoreType.DMA((7,)), pltpu.SemaphoreType.DMA],
        compiler_params=_params(),
    )(blk)


def _pair_exchange(grads):
    narr = len(grads)
    nrows = tuple(g.shape[0] // NDEV for g in grads)
    views = [g.reshape(4, 2, r, g.shape[1]) for g, r in zip(grads, nrows)]

    def body(*refs):
        gin = refs[:narr]
        got = refs[narr:2 * narr]
        send_sems, recv_sems = refs[2 * narr:]
        x, y, c = _my_place()
        copies = [pltpu.make_async_remote_copy(
            src_ref=gin[a].at[:, pl.ds(1 - c, 1)], dst_ref=got[a],
            send_sem=send_sems.at[a], recv_sem=recv_sems.at[a],
            device_id=(x, y, 1 - c), device_id_type=MESH) for a in range(narr)]
        for cp in copies:
            cp.start()
        for cp in copies:
            cp.wait()

    out_shape = tuple(_sds((4, 1, r, g.shape[1]), f32) for r, g in zip(nrows, grads))
    got = _pcall(
        body, name="pair_exchange", out_shape=out_shape,
        in_specs=[ANY] * narr, out_specs=tuple([ANY] * narr),
        scratch_shapes=[pltpu.SemaphoreType.DMA((narr,)), pltpu.SemaphoreType.DMA((narr,))],
        compiler_params=_params(),
    )(*views)
    return views, [g.reshape(4, r, g.shape[3]) for g, r in zip(got, nrows)]


def _row_tile(rows, dtype):
    unit = 16 if dtype == bf16 else 8
    for cand in (256, 208, 128, 64, 40, 32, 16, 8):
        if rows % cand == 0 and cand % unit == 0:
            return cand
    return rows


def _chip_sum(view, got, my_core, out_dtype, name):
    _, _, r, cols = view.shape
    tr = _row_tile(r, out_dtype)

    def body(core_ref, mine_ref, got_ref, out_ref):
        out_ref[...] = (mine_ref[...] + got_ref[...]).astype(out_dtype)

    grid_spec = pltpu.PrefetchScalarGridSpec(
        num_scalar_prefetch=1, grid=(4, r // tr),
        in_specs=[pl.BlockSpec((None, None, tr, cols), lambda q, i, core: (q, core[0], i, 0)),
                  pl.BlockSpec((None, tr, cols), lambda q, i, core: (q, i, 0))],
        out_specs=pl.BlockSpec((None, tr, cols), lambda q, i, core: (q, i, 0)))
    return _pcall(body, name=name, grid_spec=grid_spec, out_shape=_sds((4, r, cols), out_dtype),
                  compiler_params=_params(("arbitrary", "arbitrary")))(my_core, view, got)


def _chip_exchange(sums):
    narr = len(sums)

    def body(*refs):
        src = refs[:narr]
        dst = refs[narr:2 * narr]
        send_sems, recv_sems, local_sems = refs[2 * narr:]
        x, y, c = _my_place()

        def copy(a, k):
            px, py = (x + (k >> 1)) % 2, (y + (k & 1)) % 2
            return pltpu.make_async_remote_copy(
                src_ref=src[a].at[2 * px + py], dst_ref=dst[a].at[k],
                send_sem=send_sems.at[3 * a + k - 1], recv_sem=recv_sems.at[3 * a + k - 1],
                device_id=(px, py, c), device_id_type=MESH)

        mine = [pltpu.make_async_copy(src[a].at[2 * x + y], dst[a].at[0], local_sems.at[a]) for a in range(narr)]
        for cp in mine:
            cp.start()
        sends = [copy(a, k) for k in (3, 1, 2) for a in range(narr)]
        for cp in sends:
            cp.start()
        for cp in sends:
            cp.wait()
        for cp in mine:
            cp.wait()

    out_shape = tuple(_sds(s.shape, s.dtype) for s in sums)
    return _pcall(
        body, name="chip_exchange", out_shape=out_shape,
        in_specs=[ANY] * narr, out_specs=tuple([ANY] * narr),
        scratch_shapes=[pltpu.SemaphoreType.DMA((3 * narr,)), pltpu.SemaphoreType.DMA((3 * narr,)),
                        pltpu.SemaphoreType.DMA((narr,))],
        compiler_params=_params(),
    )(*sums)


def _reduce_scatter(grads, wire_dtypes, my_core):
    views, got = _pair_exchange(grads)
    sums = [_chip_sum(v, g, my_core, dt, "chip_sum_%d" % a)
            for a, (v, g, dt) in enumerate(zip(views, got, wire_dtypes))]
    return _chip_exchange(sums)


def _adamw(parts, w, m, v, name):
    n, rows, cols = parts.shape
    tr = _row_tile(rows, parts.dtype)

    def body(p_ref, w_ref, m_ref, v_ref, g_out, d_out, m_out, v_out):
        g = p_ref[0].astype(f32)
        for s in range(1, n):
            g = g + p_ref[s].astype(f32)
        m_new = ADAM_B1 * m_ref[...] + (1.0 - ADAM_B1) * g
        v_new = ADAM_B2 * v_ref[...] + (1.0 - ADAM_B2) * (g * g)
        m_hat = m_new / (1.0 - ADAM_B1 ** ADAM_STEP)
        v_hat = v_new / (1.0 - ADAM_B2 ** ADAM_STEP)
        g_out[...] = g
        d_out[...] = -ADAM_LR * (m_hat / (jnp.sqrt(v_hat) + ADAM_EPS) + ADAM_WD * w_ref[...])
        m_out[...] = m_new
        v_out[...] = v_new

    blk = pl.BlockSpec((tr, cols), lambda i: (i, 0))
    return _pcall(
        body, name=name, grid=(rows // tr,),
        in_specs=[pl.BlockSpec((n, tr, cols), lambda i: (0, i, 0)), blk, blk, blk],
        out_specs=(blk, blk, blk, blk), out_shape=tuple(_sds((rows, cols), f32) for _ in range(4)),
        compiler_params=_params(("arbitrary",)),
    )(parts, w, m, v)


def _rope(t, c, s1, s2):
    w = t.shape[1]
    return t * c + pltpu.roll(t, w - 8, 1) * s1 + pltpu.roll(t, 8, 1) * s2


def _rope_transposed(dt, c, s1, s2):
    w = dt.shape[1]
    return dt * c + pltpu.roll(dt * s1, 8, 1) + pltpu.roll(dt * s2, w - 8, 1)


def _in_proj(x2d, norm_g, wt_full, tabs, S):
    T = x2d.shape[0]
    tb = min(S, 1024)
    q_scale = 1.0 / math.sqrt(HEAD)

    def body(x_ref, g_ref, wt_ref, c_ref, s1_ref, s2_ref, h_ref, proj_ref):
        j = pl.program_id(1)

        @pl.when(j == 0)
        def _():
            xv = x_ref[...]
            ms = jnp.mean(xv * xv, axis=-1, keepdims=True)
            h_ref[...] = (xv * lax.rsqrt(ms + EPS) * g_ref[...]).astype(bf16)

        acc = _dot(h_ref[...], wt_ref[...], NT)
        is_q = (j == 4) | (j == 5)
        is_kv = j == 6

        @pl.when(is_q)
        def _():
            tab = (c_ref[...], s1_ref[...], s2_ref[...])
            for l in range(CH // 128):
                sl = slice(128 * l, 128 * (l + 1))
                proj_ref[:, sl] = (_rope(acc[:, sl], *tab) * q_scale).astype(bf16)

        @pl.when(is_kv)
        def _():
            tab = (c_ref[...], s1_ref[...], s2_ref[...])
            for l in range(2):
                sl = slice(128 * l, 128 * (l + 1))
                proj_ref[:, sl] = _rope(acc[:, sl], *tab).astype(bf16)
            proj_ref[:, 256:] = acc[:, 256:].astype(bf16)

        @pl.when(jnp.logical_not(is_q | is_kv))
        def _():
            proj_ref[...] = acc.astype(bf16)

    tab = pl.BlockSpec((tb, 128), lambda i, j: (i % (S // tb), 0))
    return _pcall(
        body, name="in_proj", grid=(T // tb, D_IN // CH),
        in_specs=[pl.BlockSpec((tb, D), lambda i, j: (i, 0)), pl.BlockSpec((1, D), lambda i, j: (0, 0)),
                  pl.BlockSpec((CH, D), lambda i, j: (j, 0)), tab, tab, tab],
        out_specs=(pl.BlockSpec((tb, D), lambda i, j: (i, 0)), pl.BlockSpec((tb, CH), lambda i, j: (i, j))),
        out_shape=(_sds((T, D), bf16), _sds((T, D_IN), bf16)),
        compiler_params=_params(("arbitrary", "arbitrary")),
    )(x2d, norm_g, wt_full, *tabs)


def _rows_iota(shape):
    return lax.broadcasted_iota(jnp.int32, shape, 0)


def _shift_down(v, k):
    return jnp.where(_rows_iota(v.shape) >= k, pltpu.roll(v, k, 0), 0.0)


def _shift_up(v, k):
    n = v.shape[0]
    return jnp.where(_rows_iota(v.shape) < n - k, pltpu.roll(v, n - k, 0), 0.0)


def _linear_scan(a, b, a_s, b_s, edge_s, out_ref, reverse):
    n = a.shape[0]
    ng = n // 8
    a3, b3 = a.reshape(ng, 8, RB), b.reshape(ng, 8, RB)
    rid = lax.broadcasted_iota(jnp.int32, a3.shape, 1)
    for s in (1, 2, 4):
        keep, shift = (rid < 8 - s, 8 - s) if reverse else (rid >= s, s)
        b3 = jnp.where(keep, a3 * pltpu.roll(b3, shift, 1) + b3, b3)
        a3 = jnp.where(keep, a3 * pltpu.roll(a3, shift, 1), a3)
    a_s[...] = a3.reshape(n, RB)
    b_s[...] = b3.reshape(n, RB)
    edge = 0 if reverse else 7
    ea, eb = a_s[pl.ds(edge, ng, stride=8), :], b_s[pl.ds(edge, ng, stride=8), :]
    r = _rows_iota(ea.shape)
    s = 1
    while s < ng:
        keep, shift = (r < ng - s, ng - s) if reverse else (r >= s, s)
        eb = jnp.where(keep, ea * pltpu.roll(eb, shift, 0) + eb, eb)
        if 2 * s < ng:
            ea = jnp.where(keep, ea * pltpu.roll(ea, shift, 0), ea)
        s *= 2
    edge_s[...] = _shift_up(eb, 1) if reverse else _shift_down(eb, 1)

    def eight_groups(i, carry):
        for k in range(8):
            j = i * 8 + k
            rows = pl.ds(pl.multiple_of(j * 8, 8), 8)
            out_ref[rows, :] = b_s[rows, :] + a_s[rows, :] * edge_s[pl.ds(j, 1), :]
        return carry

    lax.fori_loop(0, ng // 8, eight_groups, 0)


def _neg_expm1(v):
    series = -v * (1.0 + v * (0.5 + v * (1.0 / 6.0 + v * (1.0 / 24.0))))
    return jnp.where(v > -0.03125, series, 1.0 - jnp.exp(v))


def _softplus_neg(lam):
    return jnp.maximum(-lam, 0.0) + jnp.log(1.0 + jnp.exp(-jnp.abs(lam)))


def _lru_gates(x0, cw, cb, wa, ba, wx, bx, lam):
    u = cb + cw[3:4, :] * x0
    for k in range(3):
        u = u + cw[k:k + 1, :] * _shift_down(x0, 3 - k)
    ub = u.astype(bf16)
    r = _sigmoid_positive(_dot(ub, wa.astype(bf16), NN) + ba)
    i = _sigmoid(_dot(ub, wx.astype(bf16), NN) + bx)
    sp = _softplus_neg(lam)
    log_a = (-LRU_C) * r * sp
    a = jnp.exp(log_a)
    mult = jnp.sqrt(_neg_expm1(2.0 * log_a))
    return u, ub, r, i, sp, a, mult


def _lru_specs(S, nb):
    col = lambda off: pl.BlockSpec((S, RB), lambda n, b, off=off: (b, off + n))
    vec = pl.BlockSpec((1, RB), lambda n, b: (0, n))
    wblk = pl.BlockSpec((None, RB, RB), lambda n, b: (n, 0, 0))
    cwblk = pl.BlockSpec((8, RB), lambda n, b: (n, 0))
    return col, vec, wblk, cwblk


def _lru_forward(proj, cw_full, conv_b, w_a, b_a, w_x, b_x, lam, S):
    T = proj.shape[0]
    nb = T // S
    col, vec, wblk, cwblk = _lru_specs(S, nb)

    def body(x0_ref, g_ref, cw_ref, cb_ref, wa_ref, ba_ref, wx_ref, bx_ref, lam_ref, y_ref, h_ref, a_s, b_s, edge_s):
        x0 = x0_ref[...].astype(f32)
        u, ub, r, i, sp, a, mult = _lru_gates(x0, cw_ref[...], cb_ref[...], wa_ref[...], ba_ref[...],
                                              wx_ref[...], bx_ref[...], lam_ref[...])
        _linear_scan(a, mult * (i * u), a_s, b_s, edge_s, h_ref, reverse=False)
        g = g_ref[...].astype(f32)
        y_ref[...] = (h_ref[...] * (g * _sigmoid(g))).astype(bf16)

    out = pl.BlockSpec((S, RB), lambda n, b: (b, n))
    return _pcall(
        body, name="lru_forward", grid=(RNN_BLOCKS, nb),
        in_specs=[col(0), col(8), cwblk, vec, wblk, vec, wblk, vec, vec],
        out_specs=(out, out), out_shape=(_sds((T, D), bf16), _sds((T, D), f32)),
        scratch_shapes=[pltpu.VMEM((S, RB), f32), pltpu.VMEM((S, RB), f32), pltpu.VMEM((S // 8, RB), f32)],
        compiler_params=_params(("arbitrary", "arbitrary")),
    )(proj, proj, cw_full, conv_b, w_a, b_a, w_x, b_x, lam)


def _rope_tables(S):
    pos = jnp.arange(S, dtype=f32)
    inv_freq = ROPE_THETA ** (-jnp.arange(0, ROPE_DIM, 2, dtype=f32) / ROPE_DIM)
    ang = pos[:, None] * inv_freq[None, :]
    cos, sin = jnp.cos(ang), jnp.sin(ang)
    lane = jnp.arange(128) % HEAD
    cosl, sinl = cos[:, lane % 8], sin[:, lane % 8]
    c = jnp.where(lane[None, :] < ROPE_DIM, cosl, 1.0)
    s1 = jnp.where(lane[None, :] < 8, -sinl, 0.0)
    s2 = jnp.where((lane[None, :] >= 8) & (lane[None, :] < ROPE_DIM), sinl, 0.0)
    return c.astype(f32), s1.astype(f32), s2.astype(f32)


def _heads_to_rows(t):
    return jnp.concatenate([t[:, HEAD * h:HEAD * (h + 1)] for h in range(GROUP)], axis=0)


def _rows_to_heads(t):
    return jnp.concatenate([t[QB * h:QB * (h + 1), :] for h in range(GROUP)], axis=1)


def _window_bias(first_block):
    shape = (GROUP * QB, 2 * QB)
    qi = _rows_iota(shape) % QB
    cj = lax.broadcasted_iota(jnp.int32, shape, 1)
    valid = (cj > qi) & (cj <= qi + QB) & ((cj >= QB) | jnp.logical_not(first_block))
    return jnp.where(valid, 0.0, -jnp.inf)


def _attn_probs(q_rows, k_cat, sink_col, bias):
    s = _dot(q_rows, k_cat, NT) + bias
    m = jnp.maximum(jnp.max(s, axis=1, keepdims=True), sink_col)
    p = jnp.exp(s - m)
    e_sink = jnp.exp(sink_col - m)
    inv = 1.0 / (jnp.sum(p, axis=1, keepdims=True) + e_sink)
    return p * inv, e_sink * inv


def _sink_column(sink_ref, kv):
    rid = _rows_iota((GROUP * QB, 1))
    col = jnp.zeros((GROUP * QB, 1), f32)
    for h in range(GROUP):
        col = jnp.where(rid // QB == h, sink_ref[0, GROUP * kv + h], col)
    return col


def _attn_in_specs(S):
    nq = S // QB
    last = nq - 1
    cur = lambda b, j: b * nq + jnp.minimum(j, last)
    prev = lambda b, j: b * nq + jnp.maximum(jnp.minimum(j, last) - 1, 0)
    specs = [
        pl.BlockSpec((QB, D), lambda b, j: (cur(b, j), 2)),
        pl.BlockSpec((QB, 256), lambda b, j: (cur(b, j), 12)),
        pl.BlockSpec((QB, 256), lambda b, j: (prev(b, j), 12)),
        pl.BlockSpec((QB, 256), lambda b, j: (cur(b, j), 13)),
        pl.BlockSpec((QB, 256), lambda b, j: (prev(b, j), 13)),
        pl.BlockSpec((QB, 512), lambda b, j: (cur(b, j), 7)),
        pl.BlockSpec((QB, 512), lambda b, j: (cur(b, j), 8)),
        SMEM_SPEC,
    ]
    return specs, cur, prev


def _attn_forward(proj, sinks, S):
    T = proj.shape[0]
    nb, nq = T // S, S // QB
    specs, cur, _ = _attn_in_specs(S)

    def body(q_ref, kc_ref, kp_ref, vc_ref, vp_ref, gl_ref, gh_ref, sink_ref, y_ref):
        bias = _window_bias(pl.program_id(1) == 0)
        kc, kp, vc, vp = kc_ref[...], kp_ref[...], vc_ref[...], vp_ref[...]
        for kv in range(KV_HEADS):
            lanes = slice(256 * kv, 256 * (kv + 1))
            hl = slice(HEAD * kv, HEAD * (kv + 1))
            q_rows = _heads_to_rows(q_ref[:, lanes])
            k_cat = jnp.concatenate([kp[:, hl], kc[:, hl]], axis=0)
            v_cat = jnp.concatenate([vp[:, hl], vc[:, hl]], axis=0)
            probs, _ = _attn_probs(q_rows, k_cat, _sink_column(sink_ref, kv), bias)
            o = _rows_to_heads(_dot(probs.astype(bf16), v_cat, NN))
            g_src = gl_ref if kv < 2 else gh_ref
            g = g_src[:, 256 * (kv % 2):256 * (kv % 2 + 1)].astype(f32)
            y_ref[:, lanes] = (o * (g * _sigmoid(g))).astype(bf16)

    args = [proj] * 7 + [sinks]
    return _pcall(
        body, name="attn_forward", grid=(nb, nq), in_specs=specs,
        out_specs=pl.BlockSpec((QB, D), lambda b, j: (cur(b, j), 0)), out_shape=_sds((T, D), bf16),
        compiler_params=_params(("arbitrary", "arbitrary")),
    )(*args)


def _merge_and_head(x2d, tgt, proj, y_rnn, y_attn, w_r, w_a, w_o, gfin):
    T = x2d.shape[0]
    tb = min(T, 256)
    nsteps = T // tb

    def body(x_ref, t_ref, mr0, mr1, ma0, ma1, yr_ref, ya_ref, wr_ref, wa_ref, wo_ref, gf_ref,
             merged_ref, dx2_ref, dpr_ref, dpa_ref, dyr_ref, dya_ref, dmr_ref, dma_ref, loss_ref, gfin_ref):
        @pl.when(pl.program_id(0) == 0)
        def _():
            loss_ref[...] = jnp.zeros_like(loss_ref)
            gfin_ref[...] = jnp.zeros_like(gfin_ref)

        sr = _sigmoid(jnp.concatenate([mr0[...], mr1[...]], axis=1).astype(f32))
        sa = _sigmoid(jnp.concatenate([ma0[...], ma1[...]], axis=1).astype(f32))
        p_r = _dot(yr_ref[...], wr_ref[...], NN)
        p_a = _dot(ya_ref[...], wa_ref[...], NN)
        merged = (sr * p_r + sa * p_a).astype(bf16)
        merged_ref[...] = merged
        x2 = x_ref[...] + _dot(merged, wo_ref[...], NN)
        rstd = lax.rsqrt(jnp.mean(x2 * x2, axis=-1, keepdims=True) + EPS)
        xh = x2 * rstd
        gf = gf_ref[...]
        err = xh * gf - t_ref[...]
        loss_ref[...] += jnp.sum(err * err)
        dy = err * (1.0 / D)
        gfin_ref[0:1, :] += jnp.sum(dy * xh, axis=0, keepdims=True)
        dxn = dy * gf
        dx2 = rstd * (dxn - xh * jnp.mean(dxn * xh, axis=-1, keepdims=True))
        dx2_ref[...] = dx2
        dmerged = _dot(dx2.astype(bf16), wo_ref[...], NT)
        dmr_ref[...] = (dmerged * p_r * (sr * (1.0 - sr))).astype(bf16)
        dma_ref[...] = (dmerged * p_a * (sa * (1.0 - sa))).astype(bf16)
        dpr = (dmerged * sr).astype(bf16)
        dpa = (dmerged * sa).astype(bf16)
        dpr_ref[...] = dpr
        dpa_ref[...] = dpa
        dyr_ref[...] = _dot(dpr, wr_ref[...], NT)
        dya_ref[...] = _dot(dpa, wa_ref[...], NT)

    tok = pl.BlockSpec((tb, D), lambda i: (i, 0))
    half = lambda c: pl.BlockSpec((tb, CH), lambda i, c=c: (i, c))
    wfull = pl.BlockSpec((D, D), lambda i: (0, 0))
    acc = pl.BlockSpec((8, D), lambda i: (0, 0))
    return _pcall(
        body, name="merge_and_head", grid=(nsteps,),
        in_specs=[tok, tok, half(9), half(10), half(11), half(12), tok, tok, wfull, wfull, wfull,
                  pl.BlockSpec((1, D), lambda i: (0, 0))],
        out_specs=(tok, tok, tok, tok, tok, tok, tok, tok, acc, acc),
        out_shape=(_sds((T, D), bf16), _sds((T, D), f32), _sds((T, D), bf16), _sds((T, D), bf16),
                   _sds((T, D), f32), _sds((T, D), f32), _sds((T, D), bf16), _sds((T, D), bf16),
                   _sds((8, D), f32), _sds((8, D), f32)),
        compiler_params=_params(("arbitrary",)),
    )(x2d, tgt, proj, proj, proj, proj, y_rnn, y_attn, w_r, w_a, w_o, gfin)


def _attn_backward(proj, dy_attn, tabs, sinks, S):
    T = proj.shape[0]
    nb, nq = T // S, S // QB
    specs, cur, prev = _attn_in_specs(S)
    last = nq - 1
    tab_cur = pl.BlockSpec((QB, 128), lambda b, j: (jnp.minimum(j, last), 0))
    tab_prev = pl.BlockSpec((QB, 128), lambda b, j: (jnp.maximum(jnp.minimum(j, last) - 1, 0), 0))
    specs = specs + [pl.BlockSpec((QB, D), lambda b, j: (cur(b, j), 0))] + [tab_cur] * 3 + [tab_prev] * 3
    q_scale = 1.0 / math.sqrt(HEAD)

    def rope_back(dt, tab):
        return jnp.concatenate([_rope_transposed(dt[:, 128 * l:128 * (l + 1)], *tab) for l in range(2)], axis=1)

    def body(q_ref, kc_ref, kp_ref, vc_ref, vp_ref, gl_ref, gh_ref, sink_ref, dy_ref, cc, s1c, s2c, cp, s1p, s2p,
             dq_ref, dkv_ref, dg_ref, dsink_ref, carry_k, carry_v):
        b, j = pl.program_id(0), pl.program_id(1)

        @pl.when((b == 0) & (j == 0))
        def _():
            dsink_ref[...] = jnp.zeros_like(dsink_ref)

        @pl.when(j == 0)
        def _():
            carry_k[...] = jnp.zeros_like(carry_k)
            carry_v[...] = jnp.zeros_like(carry_v)

        @pl.when(j < nq)
        def _():
            bias = _window_bias(j == 0)
            tc = (cc[...], s1c[...], s2c[...])
            tp = (cp[...], s1p[...], s2p[...])
            kc, kp, vc, vp = kc_ref[...], kp_ref[...], vc_ref[...], vp_ref[...]
            dk_prev, dk_cur, dv_prev, dv_cur = [], [], [], []
            dsink_acc = jnp.zeros((8, 128), f32)
            r8 = lax.broadcasted_iota(jnp.int32, (8, 128), 0)
            l8 = lax.broadcasted_iota(jnp.int32, (8, 128), 1)
            for kv in range(KV_HEADS):
                lanes = slice(256 * kv, 256 * (kv + 1))
                hl = slice(HEAD * kv, HEAD * (kv + 1))
                q_rows = _heads_to_rows(q_ref[:, lanes])
                k_cat = jnp.concatenate([kp[:, hl], kc[:, hl]], axis=0)
                v_cat = jnp.concatenate([vp[:, hl], vc[:, hl]], axis=0)
                probs, p_sink = _attn_probs(q_rows, k_cat, _sink_column(sink_ref, kv), bias)
                pb = probs.astype(bf16)
                o = _rows_to_heads(_dot(pb, v_cat, NN))
                g_src = gl_ref if kv < 2 else gh_ref
                g = g_src[:, 256 * (kv % 2):256 * (kv % 2 + 1)].astype(f32)
                sg = _sigmoid(g)
                dy = dy_ref[:, lanes]
                dg_ref[:, lanes] = (dy * o * (sg * (1.0 + g * (1.0 - sg)))).astype(bf16)
                do_rows = _heads_to_rows(dy * (g * sg)).astype(bf16)
                dv = _dot(pb, do_rows, TN)
                dp = _dot(do_rows, v_cat, NT)
                rowdot = jnp.sum(probs * dp, axis=1, keepdims=True)
                ds = (probs * (dp - rowdot)).astype(bf16)
                sink_rows = -(p_sink * rowdot)
                for h in range(GROUP):
                    val = jnp.sum(sink_rows[QB * h:QB * (h + 1), :])
                    dsink_acc = dsink_acc + jnp.where((r8 == 0) & (l8 == GROUP * kv + h), val, 0.0)
                dq = _rows_to_heads(_dot(ds, k_cat, NN)) * q_scale
                dq_ref[:, lanes] = rope_back(dq, tc).astype(bf16)
                dk = _dot(ds, q_rows, TN)
                dk_prev.append(dk[:QB, :])
                dk_cur.append(dk[QB:, :])
                dv_prev.append(dv[:QB, :])
                dv_cur.append(dv[QB:, :])
            dsink_ref[...] += dsink_acc
            dkp = rope_back(jnp.concatenate(dk_prev, axis=1), tp)
            dkc = rope_back(jnp.concatenate(dk_cur, axis=1), tc)
            dkv_ref[:, 0:256] = (carry_k[...] + dkp).astype(bf16)
            dkv_ref[:, 256:512] = (carry_v[...] + jnp.concatenate(dv_prev, axis=1)).astype(bf16)
            carry_k[...] = dkc
            carry_v[...] = jnp.concatenate(dv_cur, axis=1)

        @pl.when(j == nq)
        def _():
            dkv_ref[:, 0:256] = carry_k[...].astype(bf16)
            dkv_ref[:, 256:512] = carry_v[...].astype(bf16)

    lag = lambda b, j: (b * nq + jnp.maximum(j - 1, 0), 0)
    args = [proj] * 7 + [sinks, dy_attn] + list(tabs) + list(tabs)
    return _pcall(
        body, name="attn_backward", grid=(nb, nq + 1), in_specs=specs,
        out_specs=(pl.BlockSpec((QB, D), lambda b, j: (cur(b, j), 0)), pl.BlockSpec((QB, 512), lag),
                   pl.BlockSpec((QB, D), lambda b, j: (cur(b, j), 0)), pl.BlockSpec((8, 128), lambda b, j: (0, 0))),
        out_shape=(_sds((T, D), bf16), _sds((T, 512), bf16), _sds((T, D), bf16), _sds((8, 128), f32)),
        scratch_shapes=[pltpu.VMEM((QB, 256), f32), pltpu.VMEM((QB, 256), f32)],
        compiler_params=_params(("arbitrary", "arbitrary")),
    )(*args)


def _lru_backward(proj, h_all, dy_rnn, cw_full, conv_b, w_a, b_a, w_x, b_x, lam, S):
    T = proj.shape[0]
    nb = T // S
    col, vec, wblk, cwblk = _lru_specs(S, nb)
    tokblk = pl.BlockSpec((S, RB), lambda n, b: (b, n))

    def body(x0_ref, g_ref, h_ref, dy_ref, cw_ref, cb_ref, wa_ref, ba_ref, wx_ref, bx_ref, lam_ref,
             du0_ref, dg_ref, gwa_ref, gwx_ref, vec_ref, gcw_ref, a_s, b_s, dh_s, edge_s):
        @pl.when(pl.program_id(1) == 0)
        def _():
            gwa_ref[...] = jnp.zeros_like(gwa_ref)
            gwx_ref[...] = jnp.zeros_like(gwx_ref)
            vec_ref[...] = jnp.zeros_like(vec_ref)
            gcw_ref[...] = jnp.zeros_like(gcw_ref)

        x0 = x0_ref[...].astype(f32)
        cw = cw_ref[...]
        lam_v = lam_ref[...]
        u, ub, r, i, sp, a, mult = _lru_gates(x0, cw, cb_ref[...], wa_ref[...], ba_ref[...],
                                              wx_ref[...], bx_ref[...], lam_v)
        h = h_ref[...]
        g = g_ref[...].astype(f32)
        dy = dy_ref[...]
        sg = _sigmoid(g)
        dg_ref[...] = (dy * h * (sg * (1.0 + g * (1.0 - sg)))).astype(bf16)
        _linear_scan(_shift_up(a, 1), dy * (g * sg), a_s, b_s, edge_s, dh_s, reverse=True)
        dh_total = dh_s[...]
        da = dh_total * _shift_down(h, 1)
        iu = i * u
        dmult = dh_total * iu
        di = dh_total * mult * u
        du = dh_total * mult * i
        dlog_a = a * (da - dmult * a / mult)
        dr = dlog_a * ((-LRU_C) * sp)
        dsp = jnp.sum(dlog_a * ((-LRU_C) * r), axis=0, keepdims=True)
        dpre_r = dr * r * (1.0 - r)
        dpre_i = di * i * (1.0 - i)
        dpre_rb = dpre_r.astype(bf16)
        dpre_ib = dpre_i.astype(bf16)
        du = du + _dot(dpre_rb, wa_ref[...].astype(bf16), NT) + _dot(dpre_ib, wx_ref[...].astype(bf16), NT)
        gwa_ref[...] += _dot(ub, dpre_rb, TN)
        gwx_ref[...] += _dot(ub, dpre_ib, TN)
        vec_ref[0:1, :] += jnp.sum(du, axis=0, keepdims=True)
        vec_ref[1:2, :] += jnp.sum(dpre_r, axis=0, keepdims=True)
        vec_ref[2:3, :] += jnp.sum(dpre_i, axis=0, keepdims=True)
        vec_ref[3:4, :] += dsp * (-_sigmoid(-lam_v))
        dx0 = cw[3:4, :] * du
        gcw_ref[3:4, :] += jnp.sum(du * x0, axis=0, keepdims=True)
        for k in range(3):
            dx0 = dx0 + cw[k:k + 1, :] * _shift_up(du, 3 - k)
            gcw_ref[k:k + 1, :] += jnp.sum(du * _shift_down(x0, 3 - k), axis=0, keepdims=True)
        du0_ref[...] = dx0.astype(bf16)

    wacc = pl.BlockSpec((RB, RB), lambda n, b: (0, n))
    vacc = pl.BlockSpec((8, RB), lambda n, b: (0, n))
    cacc = pl.BlockSpec((8, RB), lambda n, b: (n, 0))
    return _pcall(
        body, name="lru_backward", grid=(RNN_BLOCKS, nb),
        in_specs=[col(0), col(8), tokblk, tokblk, cwblk, vec, wblk, vec, wblk, vec, vec],
        out_specs=(tokblk, tokblk, wacc, wacc, vacc, cacc),
        out_shape=(_sds((T, D), bf16), _sds((T, D), bf16), _sds((RB, D), f32), _sds((RB, D), f32),
                   _sds((8, D), f32), _sds((8 * RNN_BLOCKS, RB), f32)),
        scratch_shapes=[pltpu.VMEM((S, RB), f32)] * 3 + [pltpu.VMEM((S // 8, RB), f32)],
        compiler_params=_params(("arbitrary", "arbitrary")),
    )(proj, proj, h_all, dy_rnn, cw_full, conv_b, w_a, b_a, w_x, b_x, lam)


def _section_of_chunk(s):
    out = []
    for start, n in zip(SEC_START, SEC_CHUNKS):
        inside = (s >= start) & (s < start + n)
        out.append((inside, jnp.clip(s - start, 0, n - 1)))
    return out


def _input_grad(dsecs, wt_full, x2d, dx2, norm_g):
    T = x2d.shape[0]
    tb = min(T, 1024)
    nchunks = D_IN // CH
    nsec = len(dsecs)

    def body(*refs):
        secs = refs[:nsec]
        wt_ref, x_ref, dx2_ref, g_ref, dx_ref, gnorm_ref, acc = refs[nsec:]
        i, s = pl.program_id(0), pl.program_id(1)

        @pl.when((i == 0) & (s == 0))
        def _():
            gnorm_ref[...] = jnp.zeros_like(gnorm_ref)

        @pl.when(s == 0)
        def _():
            acc[...] = jnp.zeros_like(acc)

        for a, (start, n) in enumerate(zip(SEC_START, SEC_CHUNKS)):
            @pl.when((s >= start) & (s < start + n))
            def _(a=a):
                acc[...] += _dot(secs[a][...], wt_ref[...], NN)

        @pl.when(s == nchunks - 1)
        def _():
            xv = x_ref[...]
            rstd = lax.rsqrt(jnp.mean(xv * xv, axis=-1, keepdims=True) + EPS)
            xh = xv * rstd
            dh = acc[...]
            gnorm_ref[0:1, :] += jnp.sum(dh * xh, axis=0, keepdims=True)
            dxn = dh * g_ref[...]
            dx_ref[...] = dx2_ref[...] + rstd * (dxn - xh * jnp.mean(dxn * xh, axis=-1, keepdims=True))

    def sec_spec(a):
        return pl.BlockSpec((tb, CH), lambda i, s, a=a: (i, _section_of_chunk(s)[a][1]))

    tok = pl.BlockSpec((tb, D), lambda i, s: (i, 0))
    return _pcall(
        body, name="input_grad", grid=(T // tb, nchunks),
        in_specs=[sec_spec(a) for a in range(nsec)] + [pl.BlockSpec((CH, D), lambda i, s: (s, 0)), tok, tok,
                                                        pl.BlockSpec((1, D), lambda i, s: (0, 0))],
        out_specs=(tok, pl.BlockSpec((8, D), lambda i, s: (0, 0))),
        out_shape=(_sds((T, D), f32), _sds((8, D), f32)),
        scratch_shapes=[pltpu.VMEM((tb, D), f32)],
        compiler_params=_params(("arbitrary", "arbitrary")),
    )(*dsecs, wt_full, x2d, dx2, norm_g)


def _w_in_grad(dsecs, h_bf):
    T = h_bf.shape[0]
    tk = min(T, 1024)
    nchunks = D_IN // CH
    nsec = len(dsecs)

    def body(*refs):
        secs = refs[:nsec]
        h_ref, out_ref = refs[nsec:]
        s, t = pl.program_id(0), pl.program_id(1)

        @pl.when(t == 0)
        def _():
            out_ref[...] = jnp.zeros_like(out_ref)

        for a, (start, n) in enumerate(zip(SEC_START, SEC_CHUNKS)):
            @pl.when((s >= start) & (s < start + n))
            def _(a=a):
                out_ref[...] += _dot(secs[a][...], h_ref[...], TN)

    def sec_spec(a):
        def index(s, t, a=a):
            inside, local = _section_of_chunk(s)[a]
            return (jnp.where(inside, t, 0), local)
        return pl.BlockSpec((tk, CH), index)

    return _pcall(
        body, name="w_in_grad", grid=(nchunks, T // tk),
        in_specs=[sec_spec(a) for a in range(nsec)] + [pl.BlockSpec((tk, D), lambda s, t: (t, 0))],
        out_specs=pl.BlockSpec((CH, D), lambda s, t: (s, 0)), out_shape=_sds((D_IN, D), f32),
        compiler_params=_params(("arbitrary", "arbitrary")),
    )(*dsecs, h_bf)


def _weight_grad(a_mat, b_mat, name):
    T, M = a_mat.shape
    N = b_mat.shape[1]
    tk = min(T, 1024)
    tm = 512

    def body(a_ref, b_ref, out_ref):
        @pl.when(pl.program_id(1) == 0)
        def _():
            out_ref[...] = jnp.zeros_like(out_ref)
        out_ref[...] += _dot(a_ref[...].astype(bf16), b_ref[...].astype(bf16), TN)

    return _pcall(
        body, name=name, grid=(M // tm, T // tk),
        in_specs=[pl.BlockSpec((tk, tm), lambda m, t: (t, m)), pl.BlockSpec((tk, N), lambda m, t: (t, 0))],
        out_specs=pl.BlockSpec((tm, N), lambda m, t: (m, 0)), out_shape=_sds((M, N), f32),
        compiler_params=_params(("arbitrary", "arbitrary")),
    )(a_mat, b_mat)


def _pad_rows(v, rows=8):
    return jnp.concatenate([v, jnp.zeros((rows - v.shape[0], v.shape[1]), v.dtype)], axis=0)


def _lanes(v):
    return jnp.pad(v, ((0, 0), (0, D - v.shape[1])))


def _blocks_to_lanes(w):
    return jnp.transpose(w, (1, 0, 2)).reshape(RB, D)


def _lanes_to_blocks(w):
    return jnp.transpose(w.reshape(RB, RNN_BLOCKS, RB), (1, 0, 2))


def _small_pack(w_a, w_x, conv_b, b_a, b_x, lam, norm_g, fin_g, sinks):
    vec = _pad_rows(jnp.concatenate([conv_b, b_a, b_x, lam], axis=0))
    return jnp.concatenate([_blocks_to_lanes(w_a), _blocks_to_lanes(w_x), vec, _pad_rows(norm_g), _pad_rows(fin_g),
                            _pad_rows(_lanes(sinks)), jnp.zeros((32, D), f32)], axis=0)


def kernel(x, norm_g, w_in, conv_w, conv_b, lru_w_a, lru_b_a, lru_w_x, lru_b_x, lru_lambda, attn_sinks, w_rnn_out, w_attn_out, w_o, final_norm_g, loss_target, m_norm_g, m_w_in, m_conv_w, m_conv_b, m_lru_w_a, m_lru_b_a, m_lru_w_x, m_lru_b_x, m_lru_lambda, m_attn_sinks, m_w_rnn_out, m_w_attn_out, m_w_o, m_final_norm_g, v_norm_g, v_w_in, v_conv_w, v_conv_b, v_lru_w_a, v_lru_b_a, v_lru_w_x, v_lru_b_x, v_lru_lambda, v_attn_sinks, v_w_rnn_out, v_w_attn_out, v_w_o, v_final_norm_g):
    nb, S, _ = x.shape
    T = nb * S
    x2d = x.reshape(T, D)
    tgt = loss_target.reshape(T, D)
    fin_g = final_norm_g.reshape(1, D)
    w_a3, w_x3 = lru_w_a[0], lru_w_x[0]

    wt_full, wr_full, wa_full, wo_full, cw_full = _gather_weights(
        w_in[0].T, w_rnn_out[0], w_attn_out[0], w_o[0], _pad_rows(conv_w[0]))

    tabs = _rope_tables(S)
    h_bf, proj = _in_proj(x2d, norm_g, wt_full, tabs, S)
    y_rnn, h_all = _lru_forward(proj, cw_full, conv_b, w_a3, lru_b_a, w_x3, lru_b_x, lru_lambda, S)
    y_attn = _attn_forward(proj, attn_sinks, S)

    (merged, dx2, dpr, dpa, dy_rnn, dy_attn, dmr, dma, loss_blk, gfin_blk) = _merge_and_head(
        x2d, tgt, proj, y_rnn, y_attn, wr_full, wa_full, wo_full, fin_g)

    dq, dkv, dga, dsink_blk = _attn_backward(proj, dy_attn, tabs, attn_sinks, S)
    du0, dgr, gwa, gwx, gvec, gcw = _lru_backward(proj, h_all, dy_rnn, cw_full, conv_b, w_a3, lru_b_a, w_x3,
                                                  lru_b_x, lru_lambda, S)
    dsecs = (du0, dgr, dq, dkv, dga, dmr, dma)
    grad_x2d, gnorm_blk = _input_grad(dsecs, wt_full, x2d, dx2, norm_g)

    g_wt = _w_in_grad(dsecs, h_bf)
    g_wr = _weight_grad(y_rnn, dpr, "w_rnn_out_grad")
    g_wa = _weight_grad(y_attn, dpa, "w_attn_out_grad")
    g_wo = _weight_grad(merged, dx2, "w_o_grad")
    g_small = jnp.concatenate([gwa, gwx, gvec, gnorm_blk, gfin_blk, _pad_rows(_lanes(dsink_blk[0:1, 0:16])),
                               jnp.zeros((32, D), f32)], axis=0)

    my_core = lax.axis_index("c").astype(jnp.int32).reshape(1)
    p_wt, p_wr, p_wa, p_wo, p_small, p_cw = _reduce_scatter(
        [g_wt, g_wr, g_wa, g_wo, g_small, gcw], [bf16, bf16, bf16, bf16, f32, f32], my_core)
    o_wt = _adamw(p_wt, w_in[0].T, m_w_in[0].T, v_w_in[0].T, "adamw_w_in")
    o_wr = _adamw(p_wr, w_rnn_out[0], m_w_rnn_out[0], v_w_rnn_out[0], "adamw_w_rnn_out")
    o_wa = _adamw(p_wa, w_attn_out[0], m_w_attn_out[0], v_w_attn_out[0], "adamw_w_attn_out")
    o_wo = _adamw(p_wo, w_o[0], m_w_o[0], v_w_o[0], "adamw_w_o")
    o_cw = _adamw(p_cw, _pad_rows(conv_w[0]), _pad_rows(m_conv_w[0]), _pad_rows(v_conv_w[0]), "adamw_conv_w")

    zero40 = jnp.zeros((p_small.shape[1], D), f32)
    small_sum = _adamw(p_small, zero40, zero40, zero40, "sum_small")[0]
    g_small_all = _gather_rows(small_sum, "gather_small")
    pack = lambda *t: _small_pack(*t)
    o_small = _adamw(
        g_small_all[None],
        pack(w_a3, w_x3, conv_b, lru_b_a, lru_b_x, lru_lambda, norm_g, fin_g, attn_sinks),
        pack(m_lru_w_a[0], m_lru_w_x[0], m_conv_b, m_lru_b_a, m_lru_b_x, m_lru_lambda, m_norm_g,
             m_final_norm_g.reshape(1, D), m_attn_sinks),
        pack(v_lru_w_a[0], v_lru_w_x[0], v_conv_b, v_lru_b_a, v_lru_b_x, v_lru_lambda, v_norm_g,
             v_final_norm_g.reshape(1, D), v_attn_sinks),
        "adamw_small")

    loss = lax.psum(loss_blk[0, 0] * 0.5 / D, ("x", "y", "c"))

    def unpack(kind):
        s = o_small[kind]
        return {
            "norm_g": s[264:265], "w_in": o_wt[kind].T[None], "conv_w": o_cw[kind][None, 0:4],
            "conv_b": s[256:257], "lru_w_a": _lanes_to_blocks(s[0:128])[None], "lru_b_a": s[257:258],
            "lru_w_x": _lanes_to_blocks(s[128:256])[None], "lru_b_x": s[258:259], "lru_lambda": s[259:260],
            "attn_sinks": s[280:281, 0:16], "w_rnn_out": o_wr[kind][None], "w_attn_out": o_wa[kind][None],
            "w_o": o_wo[kind][None], "final_norm_g": s[272, :],
        }

    order = ("norm_g", "w_in", "conv_w", "conv_b", "lru_w_a", "lru_b_a", "lru_w_x", "lru_b_x", "lru_lambda",
             "attn_sinks", "w_rnn_out", "w_attn_out", "w_o", "final_norm_g")
    outs = [loss, grad_x2d.reshape(nb, S, D)]
    for kind in range(4):
        d = unpack(kind)
        outs += [d[n] for n in order]
    return tuple(outs)
```

```python
import functools
import math

import jax
import jax.numpy as jnp
from jax import lax
from jax.experimental import pallas as pl
from jax.experimental.pallas import tpu as pltpu

f32 = jnp.float32
bf16 = jnp.bfloat16

D = 1024
D_IN = 6656
NDEV = 8
RNN_BLOCKS = 8
RB = 128
HEAD = 64
KV_HEADS = 4
GROUP = 4
QB = 128
LRU_C = 8.0
EPS = 1e-6
ROPE_DIM = 16
ROPE_THETA = 500000.0
CH = 512
SEC_START = (0, 2, 4, 6, 7, 9, 11)
SEC_CHUNKS = (2, 2, 2, 1, 2, 2, 2)
VMEM_LIMIT = 56 * 1024 * 1024

ADAM_LR, ADAM_B1, ADAM_B2, ADAM_EPS, ADAM_WD, ADAM_STEP = 0.001, 0.9, 0.999, 1e-08, 0.01, 10

MESH = pl.DeviceIdType.MESH
ANY = pl.BlockSpec(memory_space=pl.ANY)
VMEM_SPEC = pl.BlockSpec(memory_space=pltpu.VMEM)
SMEM_SPEC = pl.BlockSpec(memory_space=pltpu.SMEM)


def _pcall(body, **kw):
    return pl.pallas_call(body, **kw)


def _params(sem=None, **kw):
    if sem is not None:
        kw["dimension_semantics"] = sem
    return pltpu.CompilerParams(vmem_limit_bytes=VMEM_LIMIT, **kw)


def _sds(shape, dtype):
    return jax.ShapeDtypeStruct(shape, dtype)


def _dot(a, b, dims):
    return lax.dot_general(a, b, (dims, ((), ())), preferred_element_type=f32)


NN = ((1,), (0,))
NT = ((1,), (1,))
TN = ((0,), (0,))


def _sigmoid(v):
    return 0.5 * jnp.tanh(0.5 * v) + 0.5


def _sigmoid_positive(v):
    return 1.0 / (1.0 + jnp.exp(-v))


def _my_place():
    return lax.axis_index("x"), lax.axis_index("y"), lax.axis_index("c")


def _gather_weights(wt, cw):
    shards = (wt, cw)
    nrows = tuple(a.shape[0] for a in shards)
    narr = len(shards)

    def body(wt_ref, cw_ref, o0, o1, s0, send_sems, recv_sems, local_sems):
        x, y, c = _my_place()
        me, sibling = (x, y, c), (x, y, 1 - c)
        chips = [(1 - x, y), (x, 1 - y), (1 - x, 1 - y)]
        s0[...] = wt_ref[...].astype(bf16)
        srcs = (s0, cw_ref)
        outs = (o0, o1)

        def rows(a, place):
            px, py, pc = place
            start = pl.multiple_of((4 * px + 2 * py + pc) * nrows[a], 8)
            return outs[a].at[pl.ds(start, nrows[a]), :]

        def copy(a, k, block, to, src=None):
            return pltpu.make_async_remote_copy(
                src_ref=rows(a, block) if src is None else src, dst_ref=rows(a, block),
                send_sem=send_sems.at[7 * a + k], recv_sem=recv_sems.at[7 * a + k],
                device_id=to, device_id_type=MESH)

        mine = [pltpu.make_async_copy(srcs[a], rows(a, me), local_sems.at[a]) for a in range(narr)]
        for cp in mine:
            cp.start()
        first = []
        for a in range(narr):
            first.append(copy(a, 0, me, sibling, src=srcs[a]))
            first += [copy(a, 1 + j, me, (*chip, c), src=srcs[a]) for j, chip in enumerate(chips)]
        for cp in first:
            cp.start()
        passed = []
        for j, chip in enumerate(chips):
            for a in range(narr):
                copy(a, 1 + j, (*chip, c), me).wait_recv()
                fwd = copy(a, 4 + j, (*chip, c), sibling)
                fwd.start()
                passed.append(fwd)
        for a in range(narr):
            copy(a, 0, sibling, me).wait_recv()
        for j, chip in enumerate(chips):
            for a in range(narr):
                copy(a, 4 + j, (*chip, 1 - c), me).wait_recv()
        for cp in first + passed:
            cp.wait_send()
        for cp in mine:
            cp.wait()

    out_shape = (_sds((NDEV * nrows[0], wt.shape[1]), bf16), _sds((NDEV * nrows[1], cw.shape[1]), f32))
    return _pcall(
        body, name="gather_weights", out_shape=out_shape,
        in_specs=[VMEM_SPEC] * narr, out_specs=tuple([ANY] * narr),
        scratch_shapes=[pltpu.VMEM(wt.shape, bf16),
                        pltpu.SemaphoreType.DMA((7 * narr,)), pltpu.SemaphoreType.DMA((7 * narr,)),
                        pltpu.SemaphoreType.DMA((narr,))],
        compiler_params=_params(),
    )(*shards)


def _peer(k):
    x, y, c = _my_place()
    return (x + ((k >> 2) & 1)) % 2, (y + ((k >> 1) & 1)) % 2, (c + (k & 1)) % 2


def _direct_gather_copies(srcs, outs, send_sems, recv_sems, local_sems):
    x, y, c = _my_place()
    me = 4 * x + 2 * y + c
    local, remote = [], []
    for a, (src, out) in enumerate(zip(srcs, outs)):
        r = src.shape[0]
        mine = out.at[pl.ds(pl.multiple_of(me * r, 8), r), :]
        local.append(pltpu.make_async_copy(src, mine, local_sems.at[a]))
        for k in range(1, NDEV):
            remote.append(pltpu.make_async_remote_copy(
                src_ref=src, dst_ref=mine, send_sem=send_sems.at[7 * a + k - 1], recv_sem=recv_sems.at[7 * a + k - 1],
                device_id=_peer(k), device_id_type=MESH))
    return local, remote


def _chip_exchange_copies(src, dst, send_sems, recv_sems, local_sems):
    x, y, c = _my_place()
    local, remote = [], []
    for a in range(len(src)):
        local.append(pltpu.make_async_copy(src[a].at[2 * x + y], dst[a].at[0], local_sems.at[a]))
    for k in (3, 1, 2):
        px, py = (x + (k >> 1)) % 2, (y + (k & 1)) % 2
        for a in range(len(src)):
            remote.append(pltpu.make_async_remote_copy(
                src_ref=src[a].at[2 * px + py], dst_ref=dst[a].at[k],
                send_sem=send_sems.at[3 * a + k - 1], recv_sem=recv_sems.at[3 * a + k - 1],
                device_id=(px, py, c), device_id_type=MESH))
    return local, remote


def _exchange_scratch(narr, per_array):
    return [pltpu.SemaphoreType.DMA((per_array * narr,)), pltpu.SemaphoreType.DMA((per_array * narr,)),
            pltpu.SemaphoreType.DMA((narr,))]


def _start_all(copies):
    local, remote = copies
    for cp in local + remote:
        cp.start()


def _wait_all(copies):
    local, remote = copies
    for cp in remote + local:
        cp.wait()


def _gather_rows(blk, name):
    nrows, ncols = blk.shape

    def body(src, out, send_sems, recv_sems, local_sem):
        x, y, c = _my_place()
        me, sibling = (x, y, c), (x, y, 1 - c)
        chips = [(1 - x, y), (x, 1 - y), (1 - x, 1 - y)]

        def rows(place):
            px, py, pc = place
            start = pl.multiple_of((4 * px + 2 * py + pc) * nrows, 8)
            return out.at[pl.ds(start, nrows), :]

        def copy(k, block, to, from_src=False):
            return pltpu.make_async_remote_copy(
                src_ref=src if from_src else rows(block), dst_ref=rows(block),
                send_sem=send_sems.at[k], recv_sem=recv_sems.at[k], device_id=to, device_id_type=MESH)

        mine = pltpu.make_async_copy(src, rows(me), local_sem)
        mine.start()
        first = [copy(0, me, sibling, True)]
        first += [copy(1 + j, me, (*chip, c), True) for j, chip in enumerate(chips)]
        for cp in first:
            cp.start()
        passed = []
        for j, chip in enumerate(chips):
            copy(1 + j, (*chip, c), me).wait_recv()
            fwd = copy(4 + j, (*chip, c), sibling)
            fwd.start()
            passed.append(fwd)
        copy(0, sibling, me).wait_recv()
        for j, chip in enumerate(chips):
            copy(4 + j, (*chip, 1 - c), me).wait_recv()
        for cp in first + passed:
            cp.wait_send()
        mine.wait()

    return _pcall(
        body, name=name, out_shape=_sds((NDEV * nrows, ncols), blk.dtype),
        in_specs=[ANY], out_specs=ANY,
        scratch_shapes=[pltpu.SemaphoreType.DMA((7,)), pltpu.SemaphoreType.DMA((7,)), pltpu.SemaphoreType.DMA],
        compiler_params=_params(),
    )(blk)


def _pair_exchange(grads, name):
    narr = len(grads)
    nrows = tuple(g.shape[0] // NDEV for g in grads)
    views = [g.reshape(4, 2, r, g.shape[1]) for g, r in zip(grads, nrows)]

    def body(*refs):
        gin = refs[:narr]
        got = refs[narr:2 * narr]
        send_sems, recv_sems = refs[2 * narr:]
        x, y, c = _my_place()
        copies = [pltpu.make_async_remote_copy(
            src_ref=gin[a].at[:, pl.ds(1 - c, 1)], dst_ref=got[a],
            send_sem=send_sems.at[a], recv_sem=recv_sems.at[a],
            device_id=(x, y, 1 - c), device_id_type=MESH) for a in range(narr)]
        for cp in copies:
            cp.start()
        for cp in copies:
            cp.wait()

    out_shape = tuple(_sds((4, 1, r, g.shape[1]), f32) for r, g in zip(nrows, grads))
    got = _pcall(
        body, name=name, out_shape=out_shape,
        in_specs=[ANY] * narr, out_specs=tuple([ANY] * narr),
        scratch_shapes=[pltpu.SemaphoreType.DMA((narr,)), pltpu.SemaphoreType.DMA((narr,))],
        compiler_params=_params(),
    )(*views)
    return views, [g.reshape(4, r, g.shape[3]) for g, r in zip(got, nrows)]


def _row_tile(rows, dtype):
    unit = 16 if dtype == bf16 else 8
    for cand in (256, 208, 128, 64, 40, 32, 16, 8):
        if rows % cand == 0 and cand % unit == 0:
            return cand
    return rows


def _chip_sum(view, got, my_core, out_dtype, name):
    _, _, r, cols = view.shape
    tr = _row_tile(r, out_dtype)

    def body(core_ref, mine_ref, got_ref, out_ref):
        out_ref[...] = (mine_ref[...] + got_ref[...]).astype(out_dtype)

    grid_spec = pltpu.PrefetchScalarGridSpec(
        num_scalar_prefetch=1, grid=(4, r // tr),
        in_specs=[pl.BlockSpec((None, None, tr, cols), lambda q, i, core: (q, core[0], i, 0)),
                  pl.BlockSpec((None, tr, cols), lambda q, i, core: (q, i, 0))],
        out_specs=pl.BlockSpec((None, tr, cols), lambda q, i, core: (q, i, 0)))
    return _pcall(body, name=name, grid_spec=grid_spec, out_shape=_sds((4, r, cols), out_dtype),
                  compiler_params=_params(("arbitrary", "arbitrary")))(my_core, view, got)


def _chip_exchange(sums):
    narr = len(sums)

    def body(*refs):
        copies = _chip_exchange_copies(refs[:narr], refs[narr:2 * narr], *refs[2 * narr:])
        _start_all(copies)
        _wait_all(copies)

    out_shape = tuple(_sds(s.shape, s.dtype) for s in sums)
    return _pcall(
        body, name="chip_exchange", out_shape=out_shape,
        in_specs=[ANY] * narr, out_specs=tuple([ANY] * narr),
        scratch_shapes=_exchange_scratch(narr, 3), compiler_params=_params(),
    )(*sums)


def _pair_sums(grads, wire_dtypes, my_core, tag):
    views, got = _pair_exchange(grads, "pair_exchange_" + tag)
    return [_chip_sum(v, g, my_core, dt, "chip_sum_%s%d" % (tag, a))
            for a, (v, g, dt) in enumerate(zip(views, got, wire_dtypes))]


def _adamw(parts, w, m, v, name):
    n, rows, cols = parts.shape
    tr = _row_tile(rows, parts.dtype)

    def body(p_ref, w_ref, m_ref, v_ref, g_out, d_out, m_out, v_out):
        g = p_ref[0].astype(f32)
        for s in range(1, n):
            g = g + p_ref[s].astype(f32)
        m_new = ADAM_B1 * m_ref[...] + (1.0 - ADAM_B1) * g
        v_new = ADAM_B2 * v_ref[...] + (1.0 - ADAM_B2) * (g * g)
        m_hat = m_new / (1.0 - ADAM_B1 ** ADAM_STEP)
        v_hat = v_new / (1.0 - ADAM_B2 ** ADAM_STEP)
        g_out[...] = g
        d_out[...] = -ADAM_LR * (m_hat / (jnp.sqrt(v_hat) + ADAM_EPS) + ADAM_WD * w_ref[...])
        m_out[...] = m_new
        v_out[...] = v_new

    blk = pl.BlockSpec((tr, cols), lambda i: (i, 0))
    return _pcall(
        body, name=name, grid=(rows // tr,),
        in_specs=[pl.BlockSpec((n, tr, cols), lambda i: (0, i, 0)), blk, blk, blk],
        out_specs=(blk, blk, blk, blk), out_shape=tuple(_sds((rows, cols), f32) for _ in range(4)),
        compiler_params=_params(("arbitrary",)),
    )(parts, w, m, v)


def _rope(t, c, s1, s2):
    w = t.shape[1]
    return t * c + pltpu.roll(t, w - 8, 1) * s1 + pltpu.roll(t, 8, 1) * s2


def _rope_transposed(dt, c, s1, s2):
    w = dt.shape[1]
    return dt * c + pltpu.roll(dt * s1, 8, 1) + pltpu.roll(dt * s2, w - 8, 1)


def _in_proj(x2d, norm_g, wt_full, tabs, S, out_shards):
    T = x2d.shape[0]
    tb = min(S, 1024)
    q_scale = 1.0 / math.sqrt(HEAD)
    nw = len(out_shards)
    grid = (T // tb, D_IN // CH)

    def body(x_ref, g_ref, wt_ref, c_ref, s1_ref, s2_ref, *rest):
        shards = rest[:nw]
        h_ref, proj_ref = rest[nw:nw + 2]
        gathered = rest[nw + 2:2 * nw + 2]
        stage = rest[2 * nw + 2:3 * nw + 2]
        sems = rest[3 * nw + 2:]
        i, j = pl.program_id(0), pl.program_id(1)

        @pl.when((i == 0) & (j == 0))
        def _():
            for a in range(nw):
                stage[a][...] = shards[a][...].astype(bf16)
            _start_all(_direct_gather_copies(stage, gathered, *sems))

        @pl.when((i == grid[0] - 1) & (j == grid[1] - 1))
        def _():
            _wait_all(_direct_gather_copies(stage, gathered, *sems))

        @pl.when(j == 0)
        def _():
            xv = x_ref[...]
            ms = jnp.mean(xv * xv, axis=-1, keepdims=True)
            h_ref[...] = (xv * lax.rsqrt(ms + EPS) * g_ref[...]).astype(bf16)

        acc = _dot(h_ref[...], wt_ref[...], NT)
        is_q = (j == 4) | (j == 5)
        is_kv = j == 6

        @pl.when(is_q)
        def _():
            tab = (c_ref[...], s1_ref[...], s2_ref[...])
            for l in range(CH // 128):
                sl = slice(128 * l, 128 * (l + 1))
                proj_ref[:, sl] = (_rope(acc[:, sl], *tab) * q_scale).astype(bf16)

        @pl.when(is_kv)
        def _():
            tab = (c_ref[...], s1_ref[...], s2_ref[...])
            for l in range(2):
                sl = slice(128 * l, 128 * (l + 1))
                proj_ref[:, sl] = _rope(acc[:, sl], *tab).astype(bf16)
            proj_ref[:, 256:] = acc[:, 256:].astype(bf16)

        @pl.when(jnp.logical_not(is_q | is_kv))
        def _():
            proj_ref[...] = acc.astype(bf16)

    tab = pl.BlockSpec((tb, 128), lambda i, j: (i % (S // tb), 0))
    shard_specs = [pl.BlockSpec(w.shape, lambda i, j: (0, 0)) for w in out_shards]
    res = _pcall(
        body, name="in_proj", grid=grid,
        in_specs=[pl.BlockSpec((tb, D), lambda i, j: (i, 0)), pl.BlockSpec((1, D), lambda i, j: (0, 0)),
                  pl.BlockSpec((CH, D), lambda i, j: (j, 0)), tab, tab, tab] + shard_specs,
        out_specs=(pl.BlockSpec((tb, D), lambda i, j: (i, 0)), pl.BlockSpec((tb, CH), lambda i, j: (i, j)))
        + tuple([ANY] * nw),
        out_shape=(_sds((T, D), bf16), _sds((T, D_IN), bf16))
        + tuple(_sds((NDEV * w.shape[0], w.shape[1]), bf16) for w in out_shards),
        scratch_shapes=[pltpu.VMEM(w.shape, bf16) for w in out_shards] + _exchange_scratch(nw, 7),
        compiler_params=_params(("arbitrary", "arbitrary")),
    )(x2d, norm_g, wt_full, *tabs, *out_shards)
    return res[0], res[1], res[2:]


def _rows_iota(shape):
    return lax.broadcasted_iota(jnp.int32, shape, 0)


def _shift_down(v, k):
    return jnp.where(_rows_iota(v.shape) >= k, pltpu.roll(v, k, 0), 0.0)


def _shift_up(v, k):
    n = v.shape[0]
    return jnp.where(_rows_iota(v.shape) < n - k, pltpu.roll(v, n - k, 0), 0.0)


def _linear_scan(a, b, a_s, b_s, edge_s, out_ref, reverse):
    n = a.shape[0]
    ng = n // 8
    a3, b3 = a.reshape(ng, 8, RB), b.reshape(ng, 8, RB)
    rid = lax.broadcasted_iota(jnp.int32, a3.shape, 1)
    for s in (1, 2, 4):
        keep, shift = (rid < 8 - s, 8 - s) if reverse else (rid >= s, s)
        b3 = jnp.where(keep, a3 * pltpu.roll(b3, shift, 1) + b3, b3)
        a3 = jnp.where(keep, a3 * pltpu.roll(a3, shift, 1), a3)
    a_s[...] = a3.reshape(n, RB)
    b_s[...] = b3.reshape(n, RB)
    edge = 0 if reverse else 7
    ea, eb = a_s[pl.ds(edge, ng, stride=8), :], b_s[pl.ds(edge, ng, stride=8), :]
    r = _rows_iota(ea.shape)
    s = 1
    while s < ng:
        keep, shift = (r < ng - s, ng - s) if reverse else (r >= s, s)
        eb = jnp.where(keep, ea * pltpu.roll(eb, shift, 0) + eb, eb)
        if 2 * s < ng:
            ea = jnp.where(keep, ea * pltpu.roll(ea, shift, 0), ea)
        s *= 2
    edge_s[...] = _shift_up(eb, 1) if reverse else _shift_down(eb, 1)

    def eight_groups(i, carry):
        for k in range(8):
            j = i * 8 + k
            rows = pl.ds(pl.multiple_of(j * 8, 8), 8)
            out_ref[rows, :] = b_s[rows, :] + a_s[rows, :] * edge_s[pl.ds(j, 1), :]
        return carry

    lax.fori_loop(0, ng // 8, eight_groups, 0)


def _neg_expm1(v):
    series = -v * (1.0 + v * (0.5 + v * (1.0 / 6.0 + v * (1.0 / 24.0))))
    return jnp.where(v > -0.03125, series, 1.0 - jnp.exp(v))


def _softplus_neg(lam):
    return jnp.maximum(-lam, 0.0) + jnp.log(1.0 + jnp.exp(-jnp.abs(lam)))


def _lru_gates(x0, cw, cb, wa, ba, wx, bx, lam):
    u = cb + cw[3:4, :] * x0
    for k in range(3):
        u = u + cw[k:k + 1, :] * _shift_down(x0, 3 - k)
    ub = u.astype(bf16)
    r = _sigmoid_positive(_dot(ub, wa.astype(bf16), NN) + ba)
    i = _sigmoid(_dot(ub, wx.astype(bf16), NN) + bx)
    sp = _softplus_neg(lam)
    log_a = (-LRU_C) * r * sp
    a = jnp.exp(log_a)
    mult = jnp.sqrt(_neg_expm1(2.0 * log_a))
    return u, ub, r, i, sp, a, mult


def _lru_specs(S, nb):
    col = lambda off: pl.BlockSpec((S, RB), lambda n, b, off=off: (b, off + n))
    vec = pl.BlockSpec((1, RB), lambda n, b: (0, n))
    wblk = pl.BlockSpec((None, RB, RB), lambda n, b: (n, 0, 0))
    cwblk = pl.BlockSpec((8, RB), lambda n, b: (n, 0))
    return col, vec, wblk, cwblk


def _lru_forward(proj, cw_full, conv_b, w_a, b_a, w_x, b_x, lam, S):
    T = proj.shape[0]
    nb = T // S
    col, vec, wblk, cwblk = _lru_specs(S, nb)

    def body(x0_ref, g_ref, cw_ref, cb_ref, wa_ref, ba_ref, wx_ref, bx_ref, lam_ref, y_ref, h_ref, a_s, b_s, edge_s):
        x0 = x0_ref[...].astype(f32)
        u, ub, r, i, sp, a, mult = _lru_gates(x0, cw_ref[...], cb_ref[...], wa_ref[...], ba_ref[...],
                                              wx_ref[...], bx_ref[...], lam_ref[...])
        _linear_scan(a, mult * (i * u), a_s, b_s, edge_s, h_ref, reverse=False)
        g = g_ref[...].astype(f32)
        y_ref[...] = (h_ref[...] * (g * _sigmoid(g))).astype(bf16)

    out = pl.BlockSpec((S, RB), lambda n, b: (b, n))
    return _pcall(
        body, name="lru_forward", grid=(RNN_BLOCKS, nb),
        in_specs=[col(0), col(8), cwblk, vec, wblk, vec, wblk, vec, vec],
        out_specs=(out, out), out_shape=(_sds((T, D), bf16), _sds((T, D), f32)),
        scratch_shapes=[pltpu.VMEM((S, RB), f32), pltpu.VMEM((S, RB), f32), pltpu.VMEM((S // 8, RB), f32)],
        compiler_params=_params(("arbitrary", "arbitrary")),
    )(proj, proj, cw_full, conv_b, w_a, b_a, w_x, b_x, lam)


def _rope_tables(S):
    pos = jnp.arange(S, dtype=f32)
    inv_freq = ROPE_THETA ** (-jnp.arange(0, ROPE_DIM, 2, dtype=f32) / ROPE_DIM)
    ang = pos[:, None] * inv_freq[None, :]
    cos, sin = jnp.cos(ang), jnp.sin(ang)
    lane = jnp.arange(128) % HEAD
    cosl, sinl = cos[:, lane % 8], sin[:, lane % 8]
    c = jnp.where(lane[None, :] < ROPE_DIM, cosl, 1.0)
    s1 = jnp.where(lane[None, :] < 8, -sinl, 0.0)
    s2 = jnp.where((lane[None, :] >= 8) & (lane[None, :] < ROPE_DIM), sinl, 0.0)
    return c.astype(f32), s1.astype(f32), s2.astype(f32)


def _heads_to_rows(t):
    return jnp.concatenate([t[:, HEAD * h:HEAD * (h + 1)] for h in range(GROUP)], axis=0)


def _rows_to_heads(t):
    return jnp.concatenate([t[QB * h:QB * (h + 1), :] for h in range(GROUP)], axis=1)


def _window_bias(first_block):
    shape = (GROUP * QB, 2 * QB)
    qi = _rows_iota(shape) % QB
    cj = lax.broadcasted_iota(jnp.int32, shape, 1)
    valid = (cj > qi) & (cj <= qi + QB) & ((cj >= QB) | jnp.logical_not(first_block))
    return jnp.where(valid, 0.0, -jnp.inf)


def _attn_probs(q_rows, k_cat, sink_col, bias):
    s = _dot(q_rows, k_cat, NT) + bias
    m = jnp.maximum(jnp.max(s, axis=1, keepdims=True), sink_col)
    p = jnp.exp(s - m)
    e_sink = jnp.exp(sink_col - m)
    inv = 1.0 / (jnp.sum(p, axis=1, keepdims=True) + e_sink)
    return p * inv, e_sink * inv


def _sink_column(sink_ref, kv):
    rid = _rows_iota((GROUP * QB, 1))
    col = jnp.zeros((GROUP * QB, 1), f32)
    for h in range(GROUP):
        col = jnp.where(rid // QB == h, sink_ref[0, GROUP * kv + h], col)
    return col


def _attn_in_specs(S):
    nq = S // QB
    last = nq - 1
    cur = lambda b, j: b * nq + jnp.minimum(j, last)
    prev = lambda b, j: b * nq + jnp.maximum(jnp.minimum(j, last) - 1, 0)
    specs = [
        pl.BlockSpec((QB, D), lambda b, j: (cur(b, j), 2)),
        pl.BlockSpec((QB, 256), lambda b, j: (cur(b, j), 12)),
        pl.BlockSpec((QB, 256), lambda b, j: (prev(b, j), 12)),
        pl.BlockSpec((QB, 256), lambda b, j: (cur(b, j), 13)),
        pl.BlockSpec((QB, 256), lambda b, j: (prev(b, j), 13)),
        pl.BlockSpec((QB, 512), lambda b, j: (cur(b, j), 7)),
        pl.BlockSpec((QB, 512), lambda b, j: (cur(b, j), 8)),
        SMEM_SPEC,
    ]
    return specs, cur, prev


def _attn_forward(proj, sinks, S):
    T = proj.shape[0]
    nb, nq = T // S, S // QB
    specs, cur, _ = _attn_in_specs(S)

    def body(q_ref, kc_ref, kp_ref, vc_ref, vp_ref, gl_ref, gh_ref, sink_ref, y_ref):
        bias = _window_bias(pl.program_id(1) == 0)
        kc, kp, vc, vp = kc_ref[...], kp_ref[...], vc_ref[...], vp_ref[...]
        for kv in range(KV_HEADS):
            lanes = slice(256 * kv, 256 * (kv + 1))
            hl = slice(HEAD * kv, HEAD * (kv + 1))
            q_rows = _heads_to_rows(q_ref[:, lanes])
            k_cat = jnp.concatenate([kp[:, hl], kc[:, hl]], axis=0)
            v_cat = jnp.concatenate([vp[:, hl], vc[:, hl]], axis=0)
            probs, _ = _attn_probs(q_rows, k_cat, _sink_column(sink_ref, kv), bias)
            o = _rows_to_heads(_dot(probs.astype(bf16), v_cat, NN))
            g_src = gl_ref if kv < 2 else gh_ref
            g = g_src[:, 256 * (kv % 2):256 * (kv % 2 + 1)].astype(f32)
            y_ref[:, lanes] = (o * (g * _sigmoid(g))).astype(bf16)

    args = [proj] * 7 + [sinks]
    return _pcall(
        body, name="attn_forward", grid=(nb, nq), in_specs=specs,
        out_specs=pl.BlockSpec((QB, D), lambda b, j: (cur(b, j), 0)), out_shape=_sds((T, D), bf16),
        compiler_params=_params(("arbitrary", "arbitrary")),
    )(*args)


def _merge_and_head(x2d, tgt, proj, y_rnn, y_attn, w_r, w_a, w_o, gfin):
    T = x2d.shape[0]
    tb = min(T, 256)
    nsteps = T // tb

    def body(x_ref, t_ref, mr0, mr1, ma0, ma1, yr_ref, ya_ref, wr_ref, wa_ref, wo_ref, gf_ref,
             merged_ref, dx2_ref, dpr_ref, dpa_ref, dyr_ref, dya_ref, dmr_ref, dma_ref, loss_ref, gfin_ref):
        @pl.when(pl.program_id(0) == 0)
        def _():
            loss_ref[...] = jnp.zeros_like(loss_ref)
            gfin_ref[...] = jnp.zeros_like(gfin_ref)

        sr = _sigmoid(jnp.concatenate([mr0[...], mr1[...]], axis=1).astype(f32))
        sa = _sigmoid(jnp.concatenate([ma0[...], ma1[...]], axis=1).astype(f32))
        p_r = _dot(yr_ref[...], wr_ref[...], NN)
        p_a = _dot(ya_ref[...], wa_ref[...], NN)
        merged = (sr * p_r + sa * p_a).astype(bf16)
        merged_ref[...] = merged
        x2 = x_ref[...] + _dot(merged, wo_ref[...], NN)
        rstd = lax.rsqrt(jnp.mean(x2 * x2, axis=-1, keepdims=True) + EPS)
        xh = x2 * rstd
        gf = gf_ref[...]
        err = xh * gf - t_ref[...]
        loss_ref[...] += jnp.sum(err * err)
        dy = err * (1.0 / D)
        gfin_ref[0:1, :] += jnp.sum(dy * xh, axis=0, keepdims=True)
        dxn = dy * gf
        dx2 = rstd * (dxn - xh * jnp.mean(dxn * xh, axis=-1, keepdims=True))
        dx2_ref[...] = dx2
        dmerged = _dot(dx2.astype(bf16), wo_ref[...], NT)
        dmr_ref[...] = (dmerged * p_r * (sr * (1.0 - sr))).astype(bf16)
        dma_ref[...] = (dmerged * p_a * (sa * (1.0 - sa))).astype(bf16)
        dpr = (dmerged * sr).astype(bf16)
        dpa = (dmerged * sa).astype(bf16)
        dpr_ref[...] = dpr
        dpa_ref[...] = dpa
        dyr_ref[...] = _dot(dpr, wr_ref[...], NT)
        dya_ref[...] = _dot(dpa, wa_ref[...], NT)

    tok = pl.BlockSpec((tb, D), lambda i: (i, 0))
    half = lambda c: pl.BlockSpec((tb, CH), lambda i, c=c: (i, c))
    wfull = pl.BlockSpec((D, D), lambda i: (0, 0))
    acc = pl.BlockSpec((8, D), lambda i: (0, 0))
    return _pcall(
        body, name="merge_and_head", grid=(nsteps,),
        in_specs=[tok, tok, half(9), half(10), half(11), half(12), tok, tok, wfull, wfull, wfull,
                  pl.BlockSpec((1, D), lambda i: (0, 0))],
        out_specs=(tok, tok, tok, tok, tok, tok, tok, tok, acc, acc),
        out_shape=(_sds((T, D), bf16), _sds((T, D), f32), _sds((T, D), bf16), _sds((T, D), bf16),
                   _sds((T, D), f32), _sds((T, D), f32), _sds((T, D), bf16), _sds((T, D), bf16),
                   _sds((8, D), f32), _sds((8, D), f32)),
        compiler_params=_params(("arbitrary",)),
    )(x2d, tgt, proj, proj, proj, proj, y_rnn, y_attn, w_r, w_a, w_o, gfin)


def _attn_backward(proj, dy_attn, tabs, sinks, S, chip_sums):
    T = proj.shape[0]
    nb, nq = T // S, S // QB
    nex = len(chip_sums)
    specs, cur, prev = _attn_in_specs(S)
    last = nq - 1
    tab_cur = pl.BlockSpec((QB, 128), lambda b, j: (jnp.minimum(j, last), 0))
    tab_prev = pl.BlockSpec((QB, 128), lambda b, j: (jnp.maximum(jnp.minimum(j, last) - 1, 0), 0))
    specs = specs + [pl.BlockSpec((QB, D), lambda b, j: (cur(b, j), 0))] + [tab_cur] * 3 + [tab_prev] * 3
    q_scale = 1.0 / math.sqrt(HEAD)

    def rope_back(dt, tab):
        return jnp.concatenate([_rope_transposed(dt[:, 128 * l:128 * (l + 1)], *tab) for l in range(2)], axis=1)

    def body(q_ref, kc_ref, kp_ref, vc_ref, vp_ref, gl_ref, gh_ref, sink_ref, dy_ref, cc, s1c, s2c, cp, s1p, s2p,
             *rest):
        ex_src = rest[:nex]
        dq_ref, dkv_ref, dg_ref, dsink_ref = rest[nex:nex + 4]
        ex_dst = rest[nex + 4:2 * nex + 4]
        carry_k, carry_v = rest[2 * nex + 4:2 * nex + 6]
        sems = rest[2 * nex + 6:]
        b, j = pl.program_id(0), pl.program_id(1)

        @pl.when((b == 0) & (j == 0))
        def _():
            dsink_ref[...] = jnp.zeros_like(dsink_ref)
            _start_all(_chip_exchange_copies(ex_src, ex_dst, *sems))

        @pl.when((b == nb - 1) & (j == nq))
        def _():
            _wait_all(_chip_exchange_copies(ex_src, ex_dst, *sems))

        @pl.when(j == 0)
        def _():
            carry_k[...] = jnp.zeros_like(carry_k)
            carry_v[...] = jnp.zeros_like(carry_v)

        @pl.when(j < nq)
        def _():
            bias = _window_bias(j == 0)
            tc = (cc[...], s1c[...], s2c[...])
            tp = (cp[...], s1p[...], s2p[...])
            kc, kp, vc, vp = kc_ref[...], kp_ref[...], vc_ref[...], vp_ref[...]
            dk_prev, dk_cur, dv_prev, dv_cur = [], [], [], []
            dsink_acc = jnp.zeros((8, 128), f32)
            r8 = lax.broadcasted_iota(jnp.int32, (8, 128), 0)
            l8 = lax.broadcasted_iota(jnp.int32, (8, 128), 1)
            for kv in range(KV_HEADS):
                lanes = slice(256 * kv, 256 * (kv + 1))
                hl = slice(HEAD * kv, HEAD * (kv + 1))
                q_rows = _heads_to_rows(q_ref[:, lanes])
                k_cat = jnp.concatenate([kp[:, hl], kc[:, hl]], axis=0)
                v_cat = jnp.concatenate([vp[:, hl], vc[:, hl]], axis=0)
                probs, p_sink = _attn_probs(q_rows, k_cat, _sink_column(sink_ref, kv), bias)
                pb = probs.astype(bf16)
                o = _rows_to_heads(_dot(pb, v_cat, NN))
                g_src = gl_ref if kv < 2 else gh_ref
                g = g_src[:, 256 * (kv % 2):256 * (kv % 2 + 1)].astype(f32)
                sg = _sigmoid(g)
                dy = dy_ref[:, lanes]
                dg_ref[:, lanes] = (dy * o * (sg * (1.0 + g * (1.0 - sg)))).astype(bf16)
                do_rows = _heads_to_rows(dy * (g * sg)).astype(bf16)
                dv = _dot(pb, do_rows, TN)
                dp = _dot(do_rows, v_cat, NT)
                rowdot = jnp.sum(probs * dp, axis=1, keepdims=True)
                ds = (probs * (dp - rowdot)).astype(bf16)
                sink_rows = -(p_sink * rowdot)
                for h in range(GROUP):
                    val = jnp.sum(sink_rows[QB * h:QB * (h + 1), :])
                    dsink_acc = dsink_acc + jnp.where((r8 == 0) & (l8 == GROUP * kv + h), val, 0.0)
                dq = _rows_to_heads(_dot(ds, k_cat, NN)) * q_scale
                dq_ref[:, lanes] = rope_back(dq, tc).astype(bf16)
                dk = _dot(ds, q_rows, TN)
                dk_prev.append(dk[:QB, :])
                dk_cur.append(dk[QB:, :])
                dv_prev.append(dv[:QB, :])
                dv_cur.append(dv[QB:, :])
            dsink_ref[...] += dsink_acc
            dkp = rope_back(jnp.concatenate(dk_prev, axis=1), tp)
            dkc = rope_back(jnp.concatenate(dk_cur, axis=1), tc)
            dkv_ref[:, 0:256] = (carry_k[...] + dkp).astype(bf16)
            dkv_ref[:, 256:512] = (carry_v[...] + jnp.concatenate(dv_prev, axis=1)).astype(bf16)
            carry_k[...] = dkc
            carry_v[...] = jnp.concatenate(dv_cur, axis=1)

        @pl.when(j == nq)
        def _():
            dkv_ref[:, 0:256] = carry_k[...].astype(bf16)
            dkv_ref[:, 256:512] = carry_v[...].astype(bf16)

    lag = lambda b, j: (b * nq + jnp.maximum(j - 1, 0), 0)
    args = [proj] * 7 + [sinks, dy_attn] + list(tabs) + list(tabs) + list(chip_sums)
    res = _pcall(
        body, name="attn_backward", grid=(nb, nq + 1), in_specs=specs + [ANY] * nex,
        out_specs=(pl.BlockSpec((QB, D), lambda b, j: (cur(b, j), 0)), pl.BlockSpec((QB, 512), lag),
                   pl.BlockSpec((QB, D), lambda b, j: (cur(b, j), 0)), pl.BlockSpec((8, 128), lambda b, j: (0, 0)))
        + tuple([ANY] * nex),
        out_shape=(_sds((T, D), bf16), _sds((T, 512), bf16), _sds((T, D), bf16), _sds((8, 128), f32))
        + tuple(_sds(s.shape, s.dtype) for s in chip_sums),
        scratch_shapes=[pltpu.VMEM((QB, 256), f32), pltpu.VMEM((QB, 256), f32)] + _exchange_scratch(nex, 3),
        compiler_params=_params(("arbitrary", "arbitrary")),
    )(*args)
    return res[:4], res[4:]


def _lru_backward(proj, h_all, dy_rnn, cw_full, conv_b, w_a, b_a, w_x, b_x, lam, S):
    T = proj.shape[0]
    nb = T // S
    col, vec, wblk, cwblk = _lru_specs(S, nb)
    tokblk = pl.BlockSpec((S, RB), lambda n, b: (b, n))

    def body(x0_ref, g_ref, h_ref, dy_ref, cw_ref, cb_ref, wa_ref, ba_ref, wx_ref, bx_ref, lam_ref,
             du0_ref, dg_ref, gwa_ref, gwx_ref, vec_ref, gcw_ref, a_s, b_s, dh_s, edge_s):
        @pl.when(pl.program_id(1) == 0)
        def _():
            gwa_ref[...] = jnp.zeros_like(gwa_ref)
            gwx_ref[...] = jnp.zeros_like(gwx_ref)
            vec_ref[...] = jnp.zeros_like(vec_ref)
            gcw_ref[...] = jnp.zeros_like(gcw_ref)

        x0 = x0_ref[...].astype(f32)
        cw = cw_ref[...]
        lam_v = lam_ref[...]
        u, ub, r, i, sp, a, mult = _lru_gates(x0, cw, cb_ref[...], wa_ref[...], ba_ref[...],
                                              wx_ref[...], bx_ref[...], lam_v)
        h = h_ref[...]
        g = g_ref[...].astype(f32)
        dy = dy_ref[...]
        sg = _sigmoid(g)
        dg_ref[...] = (dy * h * (sg * (1.0 + g * (1.0 - sg)))).astype(bf16)
        _linear_scan(_shift_up(a, 1), dy * (g * sg), a_s, b_s, edge_s, dh_s, reverse=True)
        dh_total = dh_s[...]
        da = dh_total * _shift_down(h, 1)
        iu = i * u
        dmult = dh_total * iu
        di = dh_total * mult * u
        du = dh_total * mult * i
        dlog_a = a * (da - dmult * a / mult)
        dr = dlog_a * ((-LRU_C) * sp)
        dsp = jnp.sum(dlog_a * ((-LRU_C) * r), axis=0, keepdims=True)
        dpre_r = dr * r * (1.0 - r)
        dpre_i = di * i * (1.0 - i)
        dpre_rb = dpre_r.astype(bf16)
        dpre_ib = dpre_i.astype(bf16)
        du = du + _dot(dpre_rb, wa_ref[...].astype(bf16), NT) + _dot(dpre_ib, wx_ref[...].astype(bf16), NT)
        gwa_ref[...] += _dot(ub, dpre_rb, TN)
        gwx_ref[...] += _dot(ub, dpre_ib, TN)
        vec_ref[0:1, :] += jnp.sum(du, axis=0, keepdims=True)
        vec_ref[1:2, :] += jnp.sum(dpre_r, axis=0, keepdims=True)
        vec_ref[2:3, :] += jnp.sum(dpre_i, axis=0, keepdims=True)
        vec_ref[3:4, :] += dsp * (-_sigmoid(-lam_v))
        dx0 = cw[3:4, :] * du
        gcw_ref[3:4, :] += jnp.sum(du * x0, axis=0, keepdims=True)
        for k in range(3):
            dx0 = dx0 + cw[k:k + 1, :] * _shift_up(du, 3 - k)
            gcw_ref[k:k + 1, :] += jnp.sum(du * _shift_down(x0, 3 - k), axis=0, keepdims=True)
        du0_ref[...] = dx0.astype(bf16)

    wacc = pl.BlockSpec((RB, RB), lambda n, b: (0, n))
    vacc = pl.BlockSpec((8, RB), lambda n, b: (0, n))
    cacc = pl.BlockSpec((8, RB), lambda n, b: (n, 0))
    return _pcall(
        body, name="lru_backward", grid=(RNN_BLOCKS, nb),
        in_specs=[col(0), col(8), tokblk, tokblk, cwblk, vec, wblk, vec, wblk, vec, vec],
        out_specs=(tokblk, tokblk, wacc, wacc, vacc, cacc),
        out_shape=(_sds((T, D), bf16), _sds((T, D), bf16), _sds((RB, D), f32), _sds((RB, D), f32),
                   _sds((8, D), f32), _sds((8 * RNN_BLOCKS, RB), f32)),
        scratch_shapes=[pltpu.VMEM((S, RB), f32)] * 3 + [pltpu.VMEM((S // 8, RB), f32)],
        compiler_params=_params(("arbitrary", "arbitrary")),
    )(proj, proj, h_all, dy_rnn, cw_full, conv_b, w_a, b_a, w_x, b_x, lam)


def _section_of_chunk(s):
    out = []
    for start, n in zip(SEC_START, SEC_CHUNKS):
        inside = (s >= start) & (s < start + n)
        out.append((inside, jnp.clip(s - start, 0, n - 1)))
    return out


def _input_grad(dsecs, wt_full, x2d, dx2, norm_g, chip_sums):
    T = x2d.shape[0]
    tb = min(T, 1024)
    nchunks = D_IN // CH
    nsec = len(dsecs)
    nex = len(chip_sums)
    ntok = T // tb

    def body(*refs):
        secs = refs[:nsec]
        wt_ref, x_ref, dx2_ref, g_ref = refs[nsec:nsec + 4]
        ex_src = refs[nsec + 4:nsec + 4 + nex]
        dx_ref, gnorm_ref = refs[nsec + 4 + nex:nsec + 6 + nex]
        ex_dst = refs[nsec + 6 + nex:nsec + 6 + 2 * nex]
        acc = refs[nsec + 6 + 2 * nex]
        sems = refs[nsec + 7 + 2 * nex:]
        i, s = pl.program_id(0), pl.program_id(1)

        @pl.when((i == 0) & (s == 0))
        def _():
            gnorm_ref[...] = jnp.zeros_like(gnorm_ref)
            _start_all(_chip_exchange_copies(ex_src, ex_dst, *sems))

        @pl.when((i == ntok - 1) & (s == nchunks - 1))
        def _():
            _wait_all(_chip_exchange_copies(ex_src, ex_dst, *sems))

        @pl.when(s == 0)
        def _():
            acc[...] = jnp.zeros_like(acc)

        for a, (start, n) in enumerate(zip(SEC_START, SEC_CHUNKS)):
            @pl.when((s >= start) & (s < start + n))
            def _(a=a):
                acc[...] += _dot(secs[a][...], wt_ref[...], NN)

        @pl.when(s == nchunks - 1)
        def _():
            xv = x_ref[...]
            rstd = lax.rsqrt(jnp.mean(xv * xv, axis=-1, keepdims=True) + EPS)
            xh = xv * rstd
            dh = acc[...]
            gnorm_ref[0:1, :] += jnp.sum(dh * xh, axis=0, keepdims=True)
            dxn = dh * g_ref[...]
            dx_ref[...] = dx2_ref[...] + rstd * (dxn - xh * jnp.mean(dxn * xh, axis=-1, keepdims=True))

    def sec_spec(a):
        return pl.BlockSpec((tb, CH), lambda i, s, a=a: (i, _section_of_chunk(s)[a][1]))

    tok = pl.BlockSpec((tb, D), lambda i, s: (i, 0))
    res = _pcall(
        body, name="input_grad", grid=(ntok, nchunks),
        in_specs=[sec_spec(a) for a in range(nsec)] + [pl.BlockSpec((CH, D), lambda i, s: (s, 0)), tok, tok,
                                                        pl.BlockSpec((1, D), lambda i, s: (0, 0))] + [ANY] * nex,
        out_specs=(tok, pl.BlockSpec((8, D), lambda i, s: (0, 0))) + tuple([ANY] * nex),
        out_shape=(_sds((T, D), f32), _sds((8, D), f32)) + tuple(_sds(c.shape, c.dtype) for c in chip_sums),
        scratch_shapes=[pltpu.VMEM((tb, D), f32)] + _exchange_scratch(nex, 3),
        compiler_params=_params(("arbitrary", "arbitrary")),
    )(*dsecs, wt_full, x2d, dx2, norm_g, *chip_sums)
    return res[0], res[1], res[2:]


def _w_in_grad(dsecs, h_bf):
    T = h_bf.shape[0]
    tk = min(T, 1024)
    nchunks = D_IN // CH
    nsec = len(dsecs)

    def body(*refs):
        secs = refs[:nsec]
        h_ref, out_ref = refs[nsec:]
        s, t = pl.program_id(0), pl.program_id(1)

        @pl.when(t == 0)
        def _():
            out_ref[...] = jnp.zeros_like(out_ref)

        for a, (start, n) in enumerate(zip(SEC_START, SEC_CHUNKS)):
            @pl.when((s >= start) & (s < start + n))
            def _(a=a):
                out_ref[...] += _dot(secs[a][...], h_ref[...], TN)

    def sec_spec(a):
        def index(s, t, a=a):
            inside, local = _section_of_chunk(s)[a]
            return (jnp.where(inside, t, 0), local)
        return pl.BlockSpec((tk, CH), index)

    return _pcall(
        body, name="w_in_grad", grid=(nchunks, T // tk),
        in_specs=[sec_spec(a) for a in range(nsec)] + [pl.BlockSpec((tk, D), lambda s, t: (t, 0))],
        out_specs=pl.BlockSpec((CH, D), lambda s, t: (s, 0)), out_shape=_sds((D_IN, D), f32),
        compiler_params=_params(("arbitrary", "arbitrary")),
    )(*dsecs, h_bf)


def _weight_grad(a_mat, b_mat, name):
    T, M = a_mat.shape
    N = b_mat.shape[1]
    tk = min(T, 1024)
    tm = 512

    def body(a_ref, b_ref, out_ref):
        @pl.when(pl.program_id(1) == 0)
        def _():
            out_ref[...] = jnp.zeros_like(out_ref)
        out_ref[...] += _dot(a_ref[...].astype(bf16), b_ref[...].astype(bf16), TN)

    return _pcall(
        body, name=name, grid=(M // tm, T // tk),
        in_specs=[pl.BlockSpec((tk, tm), lambda m, t: (t, m)), pl.BlockSpec((tk, N), lambda m, t: (t, 0))],
        out_specs=pl.BlockSpec((tm, N), lambda m, t: (m, 0)), out_shape=_sds((M, N), f32),
        compiler_params=_params(("arbitrary", "arbitrary")),
    )(a_mat, b_mat)


def _pad_rows(v, rows=8):
    return jnp.concatenate([v, jnp.zeros((rows - v.shape[0], v.shape[1]), v.dtype)], axis=0)


def _lanes(v):
    return jnp.pad(v, ((0, 0), (0, D - v.shape[1])))


def _blocks_to_lanes(w):
    return jnp.transpose(w, (1, 0, 2)).reshape(RB, D)


def _lanes_to_blocks(w):
    return jnp.transpose(w.reshape(RB, RNN_BLOCKS, RB), (1, 0, 2))


def _small_pack(w_a, w_x, conv_b, b_a, b_x, lam, norm_g, fin_g, sinks):
    vec = _pad_rows(jnp.concatenate([conv_b, b_a, b_x, lam], axis=0))
    return jnp.concatenate([_blocks_to_lanes(w_a), _blocks_to_lanes(w_x), vec, _pad_rows(norm_g), _pad_rows(fin_g),
                            _pad_rows(_lanes(sinks)), jnp.zeros((32, D), f32)], axis=0)


def kernel(x, norm_g, w_in, conv_w, conv_b, lru_w_a, lru_b_a, lru_w_x, lru_b_x, lru_lambda, attn_sinks, w_rnn_out, w_attn_out, w_o, final_norm_g, loss_target, m_norm_g, m_w_in, m_conv_w, m_conv_b, m_lru_w_a, m_lru_b_a, m_lru_w_x, m_lru_b_x, m_lru_lambda, m_attn_sinks, m_w_rnn_out, m_w_attn_out, m_w_o, m_final_norm_g, v_norm_g, v_w_in, v_conv_w, v_conv_b, v_lru_w_a, v_lru_b_a, v_lru_w_x, v_lru_b_x, v_lru_lambda, v_attn_sinks, v_w_rnn_out, v_w_attn_out, v_w_o, v_final_norm_g):
    nb, S, _ = x.shape
    T = nb * S
    x2d = x.reshape(T, D)
    tgt = loss_target.reshape(T, D)
    fin_g = final_norm_g.reshape(1, D)
    w_a3, w_x3 = lru_w_a[0], lru_w_x[0]

    my_core = lax.axis_index("c").astype(jnp.int32).reshape(1)
    wt_full, cw_full = _gather_weights(w_in[0].T, _pad_rows(conv_w[0]))

    tabs = _rope_tables(S)
    h_bf, proj, (wr_full, wa_full, wo_full) = _in_proj(x2d, norm_g, wt_full, tabs, S,
                                                       (w_rnn_out[0], w_attn_out[0], w_o[0]))
    y_rnn, h_all = _lru_forward(proj, cw_full, conv_b, w_a3, lru_b_a, w_x3, lru_b_x, lru_lambda, S)
    y_attn = _attn_forward(proj, attn_sinks, S)

    (merged, dx2, dpr, dpa, dy_rnn, dy_attn, dmr, dma, loss_blk, gfin_blk) = _merge_and_head(
        x2d, tgt, proj, y_rnn, y_attn, wr_full, wa_full, wo_full, fin_g)

    g_wr = _weight_grad(y_rnn, dpr, "w_rnn_out_grad")
    g_wa = _weight_grad(y_attn, dpa, "w_attn_out_grad")
    g_wo = _weight_grad(merged, dx2, "w_o_grad")
    sums_out = _pair_sums([g_wr, g_wa, g_wo], [bf16, bf16, bf16], my_core, "out")

    (dq, dkv, dga, dsink_blk), (p_wr, p_wa, p_wo) = _attn_backward(proj, dy_attn, tabs, attn_sinks, S, sums_out)
    du0, dgr, gwa, gwx, gvec, gcw = _lru_backward(proj, h_all, dy_rnn, cw_full, conv_b, w_a3, lru_b_a, w_x3,
                                                  lru_b_x, lru_lambda, S)
    dsecs = (du0, dgr, dq, dkv, dga, dmr, dma)

    g_wt = _w_in_grad(dsecs, h_bf)
    sums_in = _pair_sums([g_wt], [bf16], my_core, "in")
    grad_x2d, gnorm_blk, (p_wt,) = _input_grad(dsecs, wt_full, x2d, dx2, norm_g, sums_in)

    g_small = jnp.concatenate([gwa, gwx, gvec, gnorm_blk, gfin_blk, _pad_rows(_lanes(dsink_blk[0:1, 0:16])),
                               loss_blk, jnp.zeros((24, D), f32)], axis=0)
    p_small, p_cw = _chip_exchange(_pair_sums([g_small, gcw], [f32, f32], my_core, "small"))

    o_wt = _adamw(p_wt, w_in[0].T, m_w_in[0].T, v_w_in[0].T, "adamw_w_in")
    o_wr = _adamw(p_wr, w_rnn_out[0], m_w_rnn_out[0], v_w_rnn_out[0], "adamw_w_rnn_out")
    o_wa = _adamw(p_wa, w_attn_out[0], m_w_attn_out[0], v_w_attn_out[0], "adamw_w_attn_out")
    o_wo = _adamw(p_wo, w_o[0], m_w_o[0], v_w_o[0], "adamw_w_o")
    o_cw = _adamw(p_cw, _pad_rows(conv_w[0]), _pad_rows(m_conv_w[0]), _pad_rows(v_conv_w[0]), "adamw_conv_w")

    zero40 = jnp.zeros((p_small.shape[1], D), f32)
    small_sum = _adamw(p_small, zero40, zero40, zero40, "sum_small")[0]
    g_small_all = _gather_rows(small_sum, "gather_small")
    pack = lambda *t: _small_pack(*t)
    o_small = _adamw(
        g_small_all[None],
        pack(w_a3, w_x3, conv_b, lru_b_a, lru_b_x, lru_lambda, norm_g, fin_g, attn_sinks),
        pack(m_lru_w_a[0], m_lru_w_x[0], m_conv_b, m_lru_b_a, m_lru_b_x, m_lru_lambda, m_norm_g,
             m_final_norm_g.reshape(1, D), m_attn_sinks),
        pack(v_lru_w_a[0], v_lru_w_x[0], v_conv_b, v_lru_b_a, v_lru_b_x, v_lru_lambda, v_norm_g,
             v_final_norm_g.reshape(1, D), v_attn_sinks),
        "adamw_small")

    loss = g_small_all[288, 0] * (0.5 / D)

    def unpack(kind):
        s = o_small[kind]
        return {
            "norm_g": s[264:265], "w_in": o_wt[kind].T[None], "conv_w": o_cw[kind][None, 0:4],
            "conv_b": s[256:257], "lru_w_a": _lanes_to_blocks(s[0:128])[None], "lru_b_a": s[257:258],
            "lru_w_x": _lanes_to_blocks(s[128:256])[None], "lru_b_x": s[258:259], "lru_lambda": s[259:260],
            "attn_sinks": s[280:281, 0:16], "w_rnn_out": o_wr[kind][None], "w_attn_out": o_wa[kind][None],
            "w_o": o_wo[kind][None], "final_norm_g": s[272, :],
        }

    order = ("norm_g", "w_in", "conv_w", "conv_b", "lru_w_a", "lru_b_a", "lru_w_x", "lru_b_x", "lru_lambda",
             "attn_sinks", "w_rnn_out", "w_attn_out", "w_o", "final_norm_g")
    outs = [loss, grad_x2d.reshape(nb, S, D)]
    for kind in range(4):
        d = unpack(kind)
        outs += [d[n] for n in order]
    return tuple(outs)
```

```python
import functools
import math

import jax
import jax.numpy as jnp
from jax import lax
from jax.experimental import pallas as pl
from jax.experimental.pallas import tpu as pltpu

f32 = jnp.float32
bf16 = jnp.bfloat16

D = 1024
D_IN = 6656
NDEV = 8
RNN_BLOCKS = 8
RB = 128
HEAD = 64
KV_HEADS = 4
GROUP = 4
QB = 128
LRU_C = 8.0
EPS = 1e-6
ROPE_DIM = 16
ROPE_THETA = 500000.0
CH = 512
SEC_START = (0, 2, 4, 6, 7, 9, 11)
SEC_CHUNKS = (2, 2, 2, 1, 2, 2, 2)
VMEM_LIMIT = 56 * 1024 * 1024

ADAM_LR, ADAM_B1, ADAM_B2, ADAM_EPS, ADAM_WD, ADAM_STEP = 0.001, 0.9, 0.999, 1e-08, 0.01, 10

MESH = pl.DeviceIdType.MESH
ANY = pl.BlockSpec(memory_space=pl.ANY)
VMEM_SPEC = pl.BlockSpec(memory_space=pltpu.VMEM)
SMEM_SPEC = pl.BlockSpec(memory_space=pltpu.SMEM)


def _pcall(body, **kw):
    return pl.pallas_call(body, **kw)


def _params(sem=None, **kw):
    if sem is not None:
        kw["dimension_semantics"] = sem
    return pltpu.CompilerParams(vmem_limit_bytes=VMEM_LIMIT, **kw)


def _sds(shape, dtype):
    return jax.ShapeDtypeStruct(shape, dtype)


def _dot(a, b, dims):
    return lax.dot_general(a, b, (dims, ((), ())), preferred_element_type=f32)


NN = ((1,), (0,))
NT = ((1,), (1,))
TN = ((0,), (0,))


def _sigmoid(v):
    return 0.5 * jnp.tanh(0.5 * v) + 0.5


def _sigmoid_positive(v):
    return 1.0 / (1.0 + jnp.exp(-v))


def _my_place():
    return lax.axis_index("x"), lax.axis_index("y"), lax.axis_index("c")


def _peer(k):
    x, y, c = _my_place()
    return (x + ((k >> 2) & 1)) % 2, (y + ((k >> 1) & 1)) % 2, (c + (k & 1)) % 2


def _direct_gather_copies(srcs, outs, send_sems, recv_sems, local_sems):
    x, y, c = _my_place()
    me = 4 * x + 2 * y + c
    local, remote = [], []
    for a, (src, out) in enumerate(zip(srcs, outs)):
        r = src.shape[0]
        mine = out.at[pl.ds(pl.multiple_of(me * r, 8), r), :]
        local.append(pltpu.make_async_copy(src, mine, local_sems.at[a]))
        for k in range(1, NDEV):
            remote.append(pltpu.make_async_remote_copy(
                src_ref=src, dst_ref=mine, send_sem=send_sems.at[7 * a + k - 1], recv_sem=recv_sems.at[7 * a + k - 1],
                device_id=_peer(k), device_id_type=MESH))
    return local, remote


def _chip_exchange_copies(src, dst, send_sems, recv_sems, local_sems):
    x, y, c = _my_place()
    local, remote = [], []
    for a in range(len(src)):
        local.append(pltpu.make_async_copy(src[a].at[2 * x + y], dst[a].at[0], local_sems.at[a]))
    for k in (3, 1, 2):
        px, py = (x + (k >> 1)) % 2, (y + (k & 1)) % 2
        for a in range(len(src)):
            remote.append(pltpu.make_async_remote_copy(
                src_ref=src[a].at[2 * px + py], dst_ref=dst[a].at[k],
                send_sem=send_sems.at[3 * a + k - 1], recv_sem=recv_sems.at[3 * a + k - 1],
                device_id=(px, py, c), device_id_type=MESH))
    return local, remote


def _exchange_scratch(narr, per_array):
    return [pltpu.SemaphoreType.DMA((per_array * narr,)), pltpu.SemaphoreType.DMA((per_array * narr,)),
            pltpu.SemaphoreType.DMA((narr,))]


def _start_all(copies):
    local, remote = copies
    for cp in local + remote:
        cp.start()


def _wait_all(copies):
    local, remote = copies
    for cp in remote + local:
        cp.wait()


def _gather_rows(blk, name):
    nrows, ncols = blk.shape

    def body(src, out, send_sems, recv_sems, local_sem):
        x, y, c = _my_place()
        me, sibling = (x, y, c), (x, y, 1 - c)
        chips = [(1 - x, y), (x, 1 - y), (1 - x, 1 - y)]

        def rows(place):
            px, py, pc = place
            start = pl.multiple_of((4 * px + 2 * py + pc) * nrows, 8)
            return out.at[pl.ds(start, nrows), :]

        def copy(k, block, to, from_src=False):
            return pltpu.make_async_remote_copy(
                src_ref=src if from_src else rows(block), dst_ref=rows(block),
                send_sem=send_sems.at[k], recv_sem=recv_sems.at[k], device_id=to, device_id_type=MESH)

        mine = pltpu.make_async_copy(src, rows(me), local_sem)
        mine.start()
        first = [copy(0, me, sibling, True)]
        first += [copy(1 + j, me, (*chip, c), True) for j, chip in enumerate(chips)]
        for cp in first:
            cp.start()
        passed = []
        for j, chip in enumerate(chips):
            copy(1 + j, (*chip, c), me).wait_recv()
            fwd = copy(4 + j, (*chip, c), sibling)
            fwd.start()
            passed.append(fwd)
        copy(0, sibling, me).wait_recv()
        for j, chip in enumerate(chips):
            copy(4 + j, (*chip, 1 - c), me).wait_recv()
        for cp in first + passed:
            cp.wait_send()
        mine.wait()

    return _pcall(
        body, name=name, out_shape=_sds((NDEV * nrows, ncols), blk.dtype),
        in_specs=[ANY], out_specs=ANY,
        scratch_shapes=[pltpu.SemaphoreType.DMA((7,)), pltpu.SemaphoreType.DMA((7,)), pltpu.SemaphoreType.DMA],
        compiler_params=_params(),
    )(blk)


def _pair_exchange(grads, name):
    narr = len(grads)
    nrows = tuple(g.shape[0] // NDEV for g in grads)
    views = [g.reshape(4, 2, r, g.shape[1]) for g, r in zip(grads, nrows)]

    def body(*refs):
        gin = refs[:narr]
        got = refs[narr:2 * narr]
        send_sems, recv_sems = refs[2 * narr:]
        x, y, c = _my_place()
        copies = [pltpu.make_async_remote_copy(
            src_ref=gin[a].at[:, pl.ds(1 - c, 1)], dst_ref=got[a],
            send_sem=send_sems.at[a], recv_sem=recv_sems.at[a],
            device_id=(x, y, 1 - c), device_id_type=MESH) for a in range(narr)]
        for cp in copies:
            cp.start()
        for cp in copies:
            cp.wait()

    out_shape = tuple(_sds((4, 1, r, g.shape[1]), f32) for r, g in zip(nrows, grads))
    got = _pcall(
        body, name=name, out_shape=out_shape,
        in_specs=[ANY] * narr, out_specs=tuple([ANY] * narr),
        scratch_shapes=[pltpu.SemaphoreType.DMA((narr,)), pltpu.SemaphoreType.DMA((narr,))],
        compiler_params=_params(),
    )(*views)
    return views, [g.reshape(4, r, g.shape[3]) for g, r in zip(got, nrows)]


def _row_tile(rows, dtype):
    unit = 16 if dtype == bf16 else 8
    for cand in (256, 208, 128, 64, 40, 32, 16, 8):
        if rows % cand == 0 and cand % unit == 0:
            return cand
    return rows


def _chip_sum(view, got, my_core, out_dtype, name):
    _, _, r, cols = view.shape
    tr = _row_tile(r, out_dtype)

    def body(core_ref, mine_ref, got_ref, out_ref):
        out_ref[...] = (mine_ref[...] + got_ref[...]).astype(out_dtype)

    grid_spec = pltpu.PrefetchScalarGridSpec(
        num_scalar_prefetch=1, grid=(4, r // tr),
        in_specs=[pl.BlockSpec((None, None, tr, cols), lambda q, i, core: (q, core[0], i, 0)),
                  pl.BlockSpec((None, tr, cols), lambda q, i, core: (q, i, 0))],
        out_specs=pl.BlockSpec((None, tr, cols), lambda q, i, core: (q, i, 0)))
    return _pcall(body, name=name, grid_spec=grid_spec, out_shape=_sds((4, r, cols), out_dtype),
                  compiler_params=_params(("arbitrary", "arbitrary")))(my_core, view, got)


def _chip_exchange(sums):
    narr = len(sums)

    def body(*refs):
        copies = _chip_exchange_copies(refs[:narr], refs[narr:2 * narr], *refs[2 * narr:])
        _start_all(copies)
        _wait_all(copies)

    out_shape = tuple(_sds(s.shape, s.dtype) for s in sums)
    return _pcall(
        body, name="chip_exchange", out_shape=out_shape,
        in_specs=[ANY] * narr, out_specs=tuple([ANY] * narr),
        scratch_shapes=_exchange_scratch(narr, 3), compiler_params=_params(),
    )(*sums)


def _pair_sums(grads, wire_dtypes, my_core, tag):
    views, got = _pair_exchange(grads, "pair_exchange_" + tag)
    return [_chip_sum(v, g, my_core, dt, "chip_sum_%s%d" % (tag, a))
            for a, (v, g, dt) in enumerate(zip(views, got, wire_dtypes))]


def _adamw(parts, w, m, v, name):
    n, rows, cols = parts.shape
    tr = _row_tile(rows, parts.dtype)

    def body(p_ref, w_ref, m_ref, v_ref, g_out, d_out, m_out, v_out):
        g = p_ref[0].astype(f32)
        for s in range(1, n):
            g = g + p_ref[s].astype(f32)
        m_new = ADAM_B1 * m_ref[...] + (1.0 - ADAM_B1) * g
        v_new = ADAM_B2 * v_ref[...] + (1.0 - ADAM_B2) * (g * g)
        m_hat = m_new / (1.0 - ADAM_B1 ** ADAM_STEP)
        v_hat = v_new / (1.0 - ADAM_B2 ** ADAM_STEP)
        g_out[...] = g
        d_out[...] = -ADAM_LR * (m_hat / (jnp.sqrt(v_hat) + ADAM_EPS) + ADAM_WD * w_ref[...])
        m_out[...] = m_new
        v_out[...] = v_new

    blk = pl.BlockSpec((tr, cols), lambda i: (i, 0))
    return _pcall(
        body, name=name, grid=(rows // tr,),
        in_specs=[pl.BlockSpec((n, tr, cols), lambda i: (0, i, 0)), blk, blk, blk],
        out_specs=(blk, blk, blk, blk), out_shape=tuple(_sds((rows, cols), f32) for _ in range(4)),
        compiler_params=_params(("arbitrary",)),
    )(parts, w, m, v)


def _rope(t, c, s1, s2):
    w = t.shape[1]
    return t * c + pltpu.roll(t, w - 8, 1) * s1 + pltpu.roll(t, 8, 1) * s2


def _rope_transposed(dt, c, s1, s2):
    w = dt.shape[1]
    return dt * c + pltpu.roll(dt * s1, 8, 1) + pltpu.roll(dt * s2, w - 8, 1)


PAIR_ROWS = D_IN // 4
SUB_COLS = ((0, 512), (512, 512), (1024, 512), (1536, 128))
Q_SLABS = range(3, 11)
K_SLABS = range(11, 13)


def _in_proj_gather(x2d, norm_g, wt_shard, cw_shard, tabs, S, out_shards, chip_order):
    T = x2d.shape[0]
    tb = min(S, 1024)
    ntok = T // tb
    nsb = S // tb
    q_scale = 1.0 / math.sqrt(HEAD)
    shard_rows = wt_shard.shape[0]
    small = (cw_shard,) + tuple(out_shards)
    nsm = len(small)

    def body(order_ref, x_ref, g_ref, c_ref, s1_ref, s2_ref, wt_hbm, *rest):
        small_in = rest[:nsm]
        h_ref, proj_ref, wt_out = rest[nsm:nsm + 3]
        small_out = rest[nsm + 3:2 * nsm + 3]
        wt_vm, h_vm = rest[2 * nsm + 3:2 * nsm + 5]
        stage = rest[2 * nsm + 5:3 * nsm + 4]
        wsend, wrecv, wlocal = rest[3 * nsm + 4:3 * nsm + 7]
        dsems = rest[3 * nsm + 7:]
        jj, i = pl.program_id(0), pl.program_id(1)
        x, y, c = _my_place()
        me, sibling = (x, y, c), (x, y, 1 - c)
        chips = [(1 - x, y), (x, 1 - y), (1 - x, 1 - y)]

        def rows(place):
            px, py, pc = place
            return wt_vm.at[pl.ds(pl.multiple_of((4 * px + 2 * py + pc) * shard_rows, 16), shard_rows), :]

        def copy(k, block, to, src=None):
            return pltpu.make_async_remote_copy(
                src_ref=rows(block) if src is None else src, dst_ref=rows(block),
                send_sem=wsend.at[k], recv_sem=wrecv.at[k], device_id=to, device_id_type=MESH)

        def small_copies():
            srcs = (small_in[0],) + tuple(stage)
            return _direct_gather_copies(srcs, small_out, *dsems)

        own = pltpu.make_async_copy(wt_hbm, rows(me), wlocal.at[0])
        keep = pltpu.make_async_copy(wt_vm, wt_out, wlocal.at[1])

        @pl.when((jj == 0) & (i == 0))
        def _():
            own.start()
            copy(0, me, sibling, src=wt_hbm).start()
            for j, chip in enumerate(chips):
                copy(1 + j, me, (*chip, c), src=wt_hbm).start()
            for a in range(nsm - 1):
                stage[a][...] = small_in[1 + a][...].astype(bf16)
            _start_all(small_copies())
            own.wait()
            copy(0, sibling, me).wait_recv()

        for j, chip in enumerate(chips):
            @pl.when((jj == 1 + j) & (i == 0))
            def _(j=j, chip=chip):
                copy(1 + j, (*chip, c), me).wait_recv()
                copy(4 + j, (*chip, c), sibling).start()
                copy(4 + j, (*chip, 1 - c), me).wait_recv()

        @pl.when((jj == 3) & (i == 0))
        def _():
            keep.start()

        @pl.when((jj == 3) & (i == ntok - 1))
        def _():
            copy(0, me, sibling, src=wt_hbm).wait_send()
            for j, chip in enumerate(chips):
                copy(1 + j, me, (*chip, c), src=wt_hbm).wait_send()
                copy(4 + j, (*chip, c), sibling).wait_send()
            _wait_all(small_copies())
            keep.wait()

        tok = pl.ds(pl.multiple_of(i * tb, tb), tb)

        @pl.when(jj == 0)
        def _():
            xv = x_ref[...]
            ms = jnp.mean(xv * xv, axis=-1, keepdims=True)
            hb = (xv * lax.rsqrt(ms + EPS) * g_ref[...]).astype(bf16)
            h_ref[...] = hb
            h_vm[tok, :] = hb

        block = order_ref[jj]
        hb = h_vm[tok, :]

        def piece(c0, w):
            w_rows = wt_vm[pl.ds(pl.multiple_of(block * PAIR_ROWS + c0, 128), w), :]
            return _dot(hb, w_rows, NT)

        @pl.when(block != 1)
        def _():
            for c0, w in SUB_COLS:
                proj_ref[:, c0:c0 + w] = piece(c0, w).astype(bf16)

        @pl.when(block == 1)
        def _():
            tab = (c_ref[...], s1_ref[...], s2_ref[...])
            for c0, w in SUB_COLS:
                acc = piece(c0, w)
                for l in range(w // 128):
                    slab = (c0 + 128 * l) // 128
                    part = acc[:, 128 * l:128 * (l + 1)]
                    if slab in Q_SLABS:
                        part = _rope(part, *tab) * q_scale
                    elif slab in K_SLABS:
                        part = _rope(part, *tab)
                    proj_ref[:, 128 * slab:128 * (slab + 1)] = part.astype(bf16)

    first_pass = lambda jj, i, order: (jnp.where(jj == 0, i, ntok - 1), 0)
    const = lambda jj, i, order: (0, 0)
    tab = pl.BlockSpec((tb, 128), lambda jj, i, order: (jnp.where(order[jj] == 1, i % nsb, 0), 0))
    grid_spec = pltpu.PrefetchScalarGridSpec(
        num_scalar_prefetch=1, grid=(4, ntok),
        in_specs=[pl.BlockSpec((tb, D), first_pass), pl.BlockSpec((1, D), const), tab, tab, tab, ANY]
        + [pl.BlockSpec(w.shape, const) for w in small],
        out_specs=(pl.BlockSpec((tb, D), first_pass),
                   pl.BlockSpec((tb, PAIR_ROWS), lambda jj, i, order: (i, order[jj])), ANY) + tuple([ANY] * nsm),
        scratch_shapes=[pltpu.VMEM((D_IN, D), bf16), pltpu.VMEM((T, D), bf16)]
        + [pltpu.VMEM(w.shape, bf16) for w in out_shards]
        + [pltpu.SemaphoreType.DMA((7,)), pltpu.SemaphoreType.DMA((7,)), pltpu.SemaphoreType.DMA((2,))]
        + _exchange_scratch(nsm, 7))
    res = _pcall(
        body, name="in_proj", grid_spec=grid_spec,
        out_shape=(_sds((T, D), bf16), _sds((T, D_IN), bf16), _sds((D_IN, D), bf16),
                   _sds((NDEV * cw_shard.shape[0], cw_shard.shape[1]), f32))
        + tuple(_sds((NDEV * w.shape[0], w.shape[1]), bf16) for w in out_shards),
        compiler_params=_params(("arbitrary", "arbitrary")),
    )(chip_order, x2d, norm_g, *tabs, wt_shard, *small)
    return res[0], res[1], res[2], res[3], res[4:]


def _rows_iota(shape):
    return lax.broadcasted_iota(jnp.int32, shape, 0)


def _shift_down(v, k):
    return jnp.where(_rows_iota(v.shape) >= k, pltpu.roll(v, k, 0), 0.0)


def _shift_up(v, k):
    n = v.shape[0]
    return jnp.where(_rows_iota(v.shape) < n - k, pltpu.roll(v, n - k, 0), 0.0)


def _linear_scan(a, b, a_s, b_s, edge_s, out_ref, reverse):
    n = a.shape[0]
    ng = n // 8
    a3, b3 = a.reshape(ng, 8, RB), b.reshape(ng, 8, RB)
    rid = lax.broadcasted_iota(jnp.int32, a3.shape, 1)
    for s in (1, 2, 4):
        keep, shift = (rid < 8 - s, 8 - s) if reverse else (rid >= s, s)
        b3 = jnp.where(keep, a3 * pltpu.roll(b3, shift, 1) + b3, b3)
        a3 = jnp.where(keep, a3 * pltpu.roll(a3, shift, 1), a3)
    a_s[...] = a3.reshape(n, RB)
    b_s[...] = b3.reshape(n, RB)
    edge = 0 if reverse else 7
    ea, eb = a_s[pl.ds(edge, ng, stride=8), :], b_s[pl.ds(edge, ng, stride=8), :]
    r = _rows_iota(ea.shape)
    s = 1
    while s < ng:
        keep, shift = (r < ng - s, ng - s) if reverse else (r >= s, s)
        eb = jnp.where(keep, ea * pltpu.roll(eb, shift, 0) + eb, eb)
        if 2 * s < ng:
            ea = jnp.where(keep, ea * pltpu.roll(ea, shift, 0), ea)
        s *= 2
    edge_s[...] = _shift_up(eb, 1) if reverse else _shift_down(eb, 1)

    def eight_groups(i, carry):
        for k in range(8):
            j = i * 8 + k
            rows = pl.ds(pl.multiple_of(j * 8, 8), 8)
            out_ref[rows, :] = b_s[rows, :] + a_s[rows, :] * edge_s[pl.ds(j, 1), :]
        return carry

    lax.fori_loop(0, ng // 8, eight_groups, 0)


def _neg_expm1(v):
    series = -v * (1.0 + v * (0.5 + v * (1.0 / 6.0 + v * (1.0 / 24.0))))
    return jnp.where(v > -0.03125, series, 1.0 - jnp.exp(v))


def _softplus_neg(lam):
    return jnp.maximum(-lam, 0.0) + jnp.log(1.0 + jnp.exp(-jnp.abs(lam)))


def _lru_gates(x0, cw, cb, wa, ba, wx, bx, lam):
    u = cb + cw[3:4, :] * x0
    for k in range(3):
        u = u + cw[k:k + 1, :] * _shift_down(x0, 3 - k)
    ub = u.astype(bf16)
    r = _sigmoid_positive(_dot(ub, wa.astype(bf16), NN) + ba)
    i = _sigmoid(_dot(ub, wx.astype(bf16), NN) + bx)
    sp = _softplus_neg(lam)
    log_a = (-LRU_C) * r * sp
    a = jnp.exp(log_a)
    mult = jnp.sqrt(_neg_expm1(2.0 * log_a))
    return u, ub, r, i, sp, a, mult


def _lru_specs(S, nb):
    col = lambda off: pl.BlockSpec((S, RB), lambda n, b, off=off: (b, off + n))
    vec = pl.BlockSpec((1, RB), lambda n, b: (0, n))
    wblk = pl.BlockSpec((None, RB, RB), lambda n, b: (n, 0, 0))
    cwblk = pl.BlockSpec((8, RB), lambda n, b: (n, 0))
    return col, vec, wblk, cwblk


def _lru_forward(proj, cw_full, conv_b, w_a, b_a, w_x, b_x, lam, S):
    T = proj.shape[0]
    nb = T // S
    col, vec, wblk, cwblk = _lru_specs(S, nb)

    def body(x0_ref, g_ref, cw_ref, cb_ref, wa_ref, ba_ref, wx_ref, bx_ref, lam_ref, y_ref, h_ref, a_s, b_s, edge_s):
        x0 = x0_ref[...].astype(f32)
        u, ub, r, i, sp, a, mult = _lru_gates(x0, cw_ref[...], cb_ref[...], wa_ref[...], ba_ref[...],
                                              wx_ref[...], bx_ref[...], lam_ref[...])
        _linear_scan(a, mult * (i * u), a_s, b_s, edge_s, h_ref, reverse=False)
        g = g_ref[...].astype(f32)
        y_ref[...] = (h_ref[...] * (g * _sigmoid(g))).astype(bf16)

    out = pl.BlockSpec((S, RB), lambda n, b: (b, n))
    return _pcall(
        body, name="lru_forward", grid=(RNN_BLOCKS, nb),
        in_specs=[col(0), col(8), cwblk, vec, wblk, vec, wblk, vec, vec],
        out_specs=(out, out), out_shape=(_sds((T, D), bf16), _sds((T, D), f32)),
        scratch_shapes=[pltpu.VMEM((S, RB), f32), pltpu.VMEM((S, RB), f32), pltpu.VMEM((S // 8, RB), f32)],
        compiler_params=_params(("arbitrary", "arbitrary")),
    )(proj, proj, cw_full, conv_b, w_a, b_a, w_x, b_x, lam)


def _rope_tables(S):
    pos = jnp.arange(S, dtype=f32)
    inv_freq = ROPE_THETA ** (-jnp.arange(0, ROPE_DIM, 2, dtype=f32) / ROPE_DIM)
    ang = pos[:, None] * inv_freq[None, :]
    cos, sin = jnp.cos(ang), jnp.sin(ang)
    lane = jnp.arange(128) % HEAD
    cosl, sinl = cos[:, lane % 8], sin[:, lane % 8]
    c = jnp.where(lane[None, :] < ROPE_DIM, cosl, 1.0)
    s1 = jnp.where(lane[None, :] < 8, -sinl, 0.0)
    s2 = jnp.where((lane[None, :] >= 8) & (lane[None, :] < ROPE_DIM), sinl, 0.0)
    return c.astype(f32), s1.astype(f32), s2.astype(f32)


def _heads_to_rows(t):
    return jnp.concatenate([t[:, HEAD * h:HEAD * (h + 1)] for h in range(GROUP)], axis=0)


def _rows_to_heads(t):
    return jnp.concatenate([t[QB * h:QB * (h + 1), :] for h in range(GROUP)], axis=1)


def _window_bias(first_block):
    shape = (GROUP * QB, 2 * QB)
    qi = _rows_iota(shape) % QB
    cj = lax.broadcasted_iota(jnp.int32, shape, 1)
    valid = (cj > qi) & (cj <= qi + QB) & ((cj >= QB) | jnp.logical_not(first_block))
    return jnp.where(valid, 0.0, -jnp.inf)


def _attn_probs(q_rows, k_cat, sink_col, bias):
    s = _dot(q_rows, k_cat, NT) + bias
    m = jnp.maximum(jnp.max(s, axis=1, keepdims=True), sink_col)
    p = jnp.exp(s - m)
    e_sink = jnp.exp(sink_col - m)
    inv = 1.0 / (jnp.sum(p, axis=1, keepdims=True) + e_sink)
    return p * inv, e_sink * inv


def _sink_column(sink_ref, kv):
    rid = _rows_iota((GROUP * QB, 1))
    col = jnp.zeros((GROUP * QB, 1), f32)
    for h in range(GROUP):
        col = jnp.where(rid // QB == h, sink_ref[0, GROUP * kv + h], col)
    return col


def _attn_in_specs(S):
    nq = S // QB
    last = nq - 1
    cur = lambda b, j: b * nq + jnp.minimum(j, last)
    prev = lambda b, j: b * nq + jnp.maximum(jnp.minimum(j, last) - 1, 0)
    specs = [
        pl.BlockSpec((QB, D), lambda b, j: (cur(b, j), 2)),
        pl.BlockSpec((QB, 256), lambda b, j: (cur(b, j), 12)),
        pl.BlockSpec((QB, 256), lambda b, j: (prev(b, j), 12)),
        pl.BlockSpec((QB, 256), lambda b, j: (cur(b, j), 13)),
        pl.BlockSpec((QB, 256), lambda b, j: (prev(b, j), 13)),
        pl.BlockSpec((QB, 512), lambda b, j: (cur(b, j), 7)),
        pl.BlockSpec((QB, 512), lambda b, j: (cur(b, j), 8)),
        SMEM_SPEC,
    ]
    return specs, cur, prev


def _attn_forward(proj, sinks, S):
    T = proj.shape[0]
    nb, nq = T // S, S // QB
    specs, cur, _ = _attn_in_specs(S)

    def body(q_ref, kc_ref, kp_ref, vc_ref, vp_ref, gl_ref, gh_ref, sink_ref, y_ref):
        bias = _window_bias(pl.program_id(1) == 0)
        kc, kp, vc, vp = kc_ref[...], kp_ref[...], vc_ref[...], vp_ref[...]
        for kv in range(KV_HEADS):
            lanes = slice(256 * kv, 256 * (kv + 1))
            hl = slice(HEAD * kv, HEAD * (kv + 1))
            q_rows = _heads_to_rows(q_ref[:, lanes])
            k_cat = jnp.concatenate([kp[:, hl], kc[:, hl]], axis=0)
            v_cat = jnp.concatenate([vp[:, hl], vc[:, hl]], axis=0)
            probs, _ = _attn_probs(q_rows, k_cat, _sink_column(sink_ref, kv), bias)
            o = _rows_to_heads(_dot(probs.astype(bf16), v_cat, NN))
            g_src = gl_ref if kv < 2 else gh_ref
            g = g_src[:, 256 * (kv % 2):256 * (kv % 2 + 1)].astype(f32)
            y_ref[:, lanes] = (o * (g * _sigmoid(g))).astype(bf16)

    args = [proj] * 7 + [sinks]
    return _pcall(
        body, name="attn_forward", grid=(nb, nq), in_specs=specs,
        out_specs=pl.BlockSpec((QB, D), lambda b, j: (cur(b, j), 0)), out_shape=_sds((T, D), bf16),
        compiler_params=_params(("arbitrary", "arbitrary")),
    )(*args)


def _merge_and_head(x2d, tgt, proj, y_rnn, y_attn, w_r, w_a, w_o, gfin):
    T = x2d.shape[0]
    tb = min(T, 256)
    nsteps = T // tb

    def body(x_ref, t_ref, mr0, mr1, ma0, ma1, yr_ref, ya_ref, wr_ref, wa_ref, wo_ref, gf_ref,
             merged_ref, dx2_ref, dpr_ref, dpa_ref, dyr_ref, dya_ref, dmr_ref, dma_ref, loss_ref, gfin_ref):
        @pl.when(pl.program_id(0) == 0)
        def _():
            loss_ref[...] = jnp.zeros_like(loss_ref)
            gfin_ref[...] = jnp.zeros_like(gfin_ref)

        sr = _sigmoid(jnp.concatenate([mr0[...], mr1[...]], axis=1).astype(f32))
        sa = _sigmoid(jnp.concatenate([ma0[...], ma1[...]], axis=1).astype(f32))
        p_r = _dot(yr_ref[...], wr_ref[...], NN)
        p_a = _dot(ya_ref[...], wa_ref[...], NN)
        merged = (sr * p_r + sa * p_a).astype(bf16)
        merged_ref[...] = merged
        x2 = x_ref[...] + _dot(merged, wo_ref[...], NN)
        rstd = lax.rsqrt(jnp.mean(x2 * x2, axis=-1, keepdims=True) + EPS)
        xh = x2 * rstd
        gf = gf_ref[...]
        err = xh * gf - t_ref[...]
        loss_ref[...] += jnp.sum(err * err)
        dy = err * (1.0 / D)
        gfin_ref[0:1, :] += jnp.sum(dy * xh, axis=0, keepdims=True)
        dxn = dy * gf
        dx2 = rstd * (dxn - xh * jnp.mean(dxn * xh, axis=-1, keepdims=True))
        dx2_ref[...] = dx2
        dmerged = _dot(dx2.astype(bf16), wo_ref[...], NT)
        dmr_ref[...] = (dmerged * p_r * (sr * (1.0 - sr))).astype(bf16)
        dma_ref[...] = (dmerged * p_a * (sa * (1.0 - sa))).astype(bf16)
        dpr = (dmerged * sr).astype(bf16)
        dpa = (dmerged * sa).astype(bf16)
        dpr_ref[...] = dpr
        dpa_ref[...] = dpa
        dyr_ref[...] = _dot(dpr, wr_ref[...], NT)
        dya_ref[...] = _dot(dpa, wa_ref[...], NT)

    tok = pl.BlockSpec((tb, D), lambda i: (i, 0))
    half = lambda c: pl.BlockSpec((tb, CH), lambda i, c=c: (i, c))
    wfull = pl.BlockSpec((D, D), lambda i: (0, 0))
    acc = pl.BlockSpec((8, D), lambda i: (0, 0))
    return _pcall(
        body, name="merge_and_head", grid=(nsteps,),
        in_specs=[tok, tok, half(9), half(10), half(11), half(12), tok, tok, wfull, wfull, wfull,
                  pl.BlockSpec((1, D), lambda i: (0, 0))],
        out_specs=(tok, tok, tok, tok, tok, tok, tok, tok, acc, acc),
        out_shape=(_sds((T, D), bf16), _sds((T, D), f32), _sds((T, D), bf16), _sds((T, D), bf16),
                   _sds((T, D), f32), _sds((T, D), f32), _sds((T, D), bf16), _sds((T, D), bf16),
                   _sds((8, D), f32), _sds((8, D), f32)),
        compiler_params=_params(("arbitrary",)),
    )(x2d, tgt, proj, proj, proj, proj, y_rnn, y_attn, w_r, w_a, w_o, gfin)


def _attn_backward(proj, dy_attn, tabs, sinks, S, chip_sums):
    T = proj.shape[0]
    nb, nq = T // S, S // QB
    nex = len(chip_sums)
    specs, cur, prev = _attn_in_specs(S)
    last = nq - 1
    tab_cur = pl.BlockSpec((QB, 128), lambda b, j: (jnp.minimum(j, last), 0))
    tab_prev = pl.BlockSpec((QB, 128), lambda b, j: (jnp.maximum(jnp.minimum(j, last) - 1, 0), 0))
    specs = specs + [pl.BlockSpec((QB, D), lambda b, j: (cur(b, j), 0))] + [tab_cur] * 3 + [tab_prev] * 3
    q_scale = 1.0 / math.sqrt(HEAD)

    def rope_back(dt, tab):
        return jnp.concatenate([_rope_transposed(dt[:, 128 * l:128 * (l + 1)], *tab) for l in range(2)], axis=1)

    def body(q_ref, kc_ref, kp_ref, vc_ref, vp_ref, gl_ref, gh_ref, sink_ref, dy_ref, cc, s1c, s2c, cp, s1p, s2p,
             *rest):
        ex_src = rest[:nex]
        dq_ref, dkv_ref, dg_ref, dsink_ref = rest[nex:nex + 4]
        ex_dst = rest[nex + 4:2 * nex + 4]
        carry_k, carry_v = rest[2 * nex + 4:2 * nex + 6]
        sems = rest[2 * nex + 6:]
        b, j = pl.program_id(0), pl.program_id(1)

        @pl.when((b == 0) & (j == 0))
        def _():
            dsink_ref[...] = jnp.zeros_like(dsink_ref)
            _start_all(_chip_exchange_copies(ex_src, ex_dst, *sems))

        @pl.when((b == nb - 1) & (j == nq))
        def _():
            _wait_all(_chip_exchange_copies(ex_src, ex_dst, *sems))

        @pl.when(j == 0)
        def _():
            carry_k[...] = jnp.zeros_like(carry_k)
            carry_v[...] = jnp.zeros_like(carry_v)

        @pl.when(j < nq)
        def _():
            bias = _window_bias(j == 0)
            tc = (cc[...], s1c[...], s2c[...])
            tp = (cp[...], s1p[...], s2p[...])
            kc, kp, vc, vp = kc_ref[...], kp_ref[...], vc_ref[...], vp_ref[...]
            dk_prev, dk_cur, dv_prev, dv_cur = [], [], [], []
            dsink_acc = jnp.zeros((8, 128), f32)
            r8 = lax.broadcasted_iota(jnp.int32, (8, 128), 0)
            l8 = lax.broadcasted_iota(jnp.int32, (8, 128), 1)
            for kv in range(KV_HEADS):
                lanes = slice(256 * kv, 256 * (kv + 1))
                hl = slice(HEAD * kv, HEAD * (kv + 1))
                q_rows = _heads_to_rows(q_ref[:, lanes])
                k_cat = jnp.concatenate([kp[:, hl], kc[:, hl]], axis=0)
                v_cat = jnp.concatenate([vp[:, hl], vc[:, hl]], axis=0)
                probs, p_sink = _attn_probs(q_rows, k_cat, _sink_column(sink_ref, kv), bias)
                pb = probs.astype(bf16)
                o = _rows_to_heads(_dot(pb, v_cat, NN))
                g_src = gl_ref if kv < 2 else gh_ref
                g = g_src[:, 256 * (kv % 2):256 * (kv % 2 + 1)].astype(f32)
                sg = _sigmoid(g)
                dy = dy_ref[:, lanes]
                dg_ref[:, lanes] = (dy * o * (sg * (1.0 + g * (1.0 - sg)))).astype(bf16)
                do_rows = _heads_to_rows(dy * (g * sg)).astype(bf16)
                dv = _dot(pb, do_rows, TN)
                dp = _dot(do_rows, v_cat, NT)
                rowdot = jnp.sum(probs * dp, axis=1, keepdims=True)
                ds = (probs * (dp - rowdot)).astype(bf16)
                sink_rows = -(p_sink * rowdot)
                for h in range(GROUP):
                    val = jnp.sum(sink_rows[QB * h:QB * (h + 1), :])
                    dsink_acc = dsink_acc + jnp.where((r8 == 0) & (l8 == GROUP * kv + h), val, 0.0)
                dq = _rows_to_heads(_dot(ds, k_cat, NN)) * q_scale
                dq_ref[:, lanes] = rope_back(dq, tc).astype(bf16)
                dk = _dot(ds, q_rows, TN)
                dk_prev.append(dk[:QB, :])
                dk_cur.append(dk[QB:, :])
                dv_prev.append(dv[:QB, :])
                dv_cur.append(dv[QB:, :])
            dsink_ref[...] += dsink_acc
            dkp = rope_back(jnp.concatenate(dk_prev, axis=1), tp)
            dkc = rope_back(jnp.concatenate(dk_cur, axis=1), tc)
            dkv_ref[:, 0:256] = (carry_k[...] + dkp).astype(bf16)
            dkv_ref[:, 256:512] = (carry_v[...] + jnp.concatenate(dv_prev, axis=1)).astype(bf16)
            carry_k[...] = dkc
            carry_v[...] = jnp.concatenate(dv_cur, axis=1)

        @pl.when(j == nq)
        def _():
            dkv_ref[:, 0:256] = carry_k[...].astype(bf16)
            dkv_ref[:, 256:512] = carry_v[...].astype(bf16)

    lag = lambda b, j: (b * nq + jnp.maximum(j - 1, 0), 0)
    args = [proj] * 7 + [sinks, dy_attn] + list(tabs) + list(tabs) + list(chip_sums)
    res = _pcall(
        body, name="attn_backward", grid=(nb, nq + 1), in_specs=specs + [ANY] * nex,
        out_specs=(pl.BlockSpec((QB, D), lambda b, j: (cur(b, j), 0)), pl.BlockSpec((QB, 512), lag),
                   pl.BlockSpec((QB, D), lambda b, j: (cur(b, j), 0)), pl.BlockSpec((8, 128), lambda b, j: (0, 0)))
        + tuple([ANY] * nex),
        out_shape=(_sds((T, D), bf16), _sds((T, 512), bf16), _sds((T, D), bf16), _sds((8, 128), f32))
        + tuple(_sds(s.shape, s.dtype) for s in chip_sums),
        scratch_shapes=[pltpu.VMEM((QB, 256), f32), pltpu.VMEM((QB, 256), f32)] + _exchange_scratch(nex, 3),
        compiler_params=_params(("arbitrary", "arbitrary")),
    )(*args)
    return res[:4], res[4:]


def _lru_backward(proj, h_all, dy_rnn, cw_full, conv_b, w_a, b_a, w_x, b_x, lam, S):
    T = proj.shape[0]
    nb = T // S
    col, vec, wblk, cwblk = _lru_specs(S, nb)
    tokblk = pl.BlockSpec((S, RB), lambda n, b: (b, n))

    def body(x0_ref, g_ref, h_ref, dy_ref, cw_ref, cb_ref, wa_ref, ba_ref, wx_ref, bx_ref, lam_ref,
             du0_ref, dg_ref, gwa_ref, gwx_ref, vec_ref, gcw_ref, a_s, b_s, dh_s, edge_s):
        @pl.when(pl.program_id(1) == 0)
        def _():
            gwa_ref[...] = jnp.zeros_like(gwa_ref)
            gwx_ref[...] = jnp.zeros_like(gwx_ref)
            vec_ref[...] = jnp.zeros_like(vec_ref)
            gcw_ref[...] = jnp.zeros_like(gcw_ref)

        x0 = x0_ref[...].astype(f32)
        cw = cw_ref[...]
        lam_v = lam_ref[...]
        u, ub, r, i, sp, a, mult = _lru_gates(x0, cw, cb_ref[...], wa_ref[...], ba_ref[...],
                                              wx_ref[...], bx_ref[...], lam_v)
        h = h_ref[...]
        g = g_ref[...].astype(f32)
        dy = dy_ref[...]
        sg = _sigmoid(g)
        dg_ref[...] = (dy * h * (sg * (1.0 + g * (1.0 - sg)))).astype(bf16)
        _linear_scan(_shift_up(a, 1), dy * (g * sg), a_s, b_s, edge_s, dh_s, reverse=True)
        dh_total = dh_s[...]
        da = dh_total * _shift_down(h, 1)
        iu = i * u
        dmult = dh_total * iu
        di = dh_total * mult * u
        du = dh_total * mult * i
        dlog_a = a * (da - dmult * a / mult)
        dr = dlog_a * ((-LRU_C) * sp)
        dsp = jnp.sum(dlog_a * ((-LRU_C) * r), axis=0, keepdims=True)
        dpre_r = dr * r * (1.0 - r)
        dpre_i = di * i * (1.0 - i)
        dpre_rb = dpre_r.astype(bf16)
        dpre_ib = dpre_i.astype(bf16)
        du = du + _dot(dpre_rb, wa_ref[...].astype(bf16), NT) + _dot(dpre_ib, wx_ref[...].astype(bf16), NT)
        gwa_ref[...] += _dot(ub, dpre_rb, TN)
        gwx_ref[...] += _dot(ub, dpre_ib, TN)
        vec_ref[0:1, :] += jnp.sum(du, axis=0, keepdims=True)
        vec_ref[1:2, :] += jnp.sum(dpre_r, axis=0, keepdims=True)
        vec_ref[2:3, :] += jnp.sum(dpre_i, axis=0, keepdims=True)
        vec_ref[3:4, :] += dsp * (-_sigmoid(-lam_v))
        dx0 = cw[3:4, :] * du
        gcw_ref[3:4, :] += jnp.sum(du * x0, axis=0, keepdims=True)
        for k in range(3):
            dx0 = dx0 + cw[k:k + 1, :] * _shift_up(du, 3 - k)
            gcw_ref[k:k + 1, :] += jnp.sum(du * _shift_down(x0, 3 - k), axis=0, keepdims=True)
        du0_ref[...] = dx0.astype(bf16)

    wacc = pl.BlockSpec((RB, RB), lambda n, b: (0, n))
    vacc = pl.BlockSpec((8, RB), lambda n, b: (0, n))
    cacc = pl.BlockSpec((8, RB), lambda n, b: (n, 0))
    return _pcall(
        body, name="lru_backward", grid=(RNN_BLOCKS, nb),
        in_specs=[col(0), col(8), tokblk, tokblk, cwblk, vec, wblk, vec, wblk, vec, vec],
        out_specs=(tokblk, tokblk, wacc, wacc, vacc, cacc),
        out_shape=(_sds((T, D), bf16), _sds((T, D), bf16), _sds((RB, D), f32), _sds((RB, D), f32),
                   _sds((8, D), f32), _sds((8 * RNN_BLOCKS, RB), f32)),
        scratch_shapes=[pltpu.VMEM((S, RB), f32)] * 3 + [pltpu.VMEM((S // 8, RB), f32)],
        compiler_params=_params(("arbitrary", "arbitrary")),
    )(proj, proj, h_all, dy_rnn, cw_full, conv_b, w_a, b_a, w_x, b_x, lam)


def _section_of_chunk(s):
    out = []
    for start, n in zip(SEC_START, SEC_CHUNKS):
        inside = (s >= start) & (s < start + n)
        out.append((inside, jnp.clip(s - start, 0, n - 1)))
    return out


def _input_grad(dsecs, wt_full, x2d, dx2, norm_g, chip_sums):
    T = x2d.shape[0]
    tb = min(T, 1024)
    nchunks = D_IN // CH
    nsec = len(dsecs)
    nex = len(chip_sums)
    ntok = T // tb

    def body(*refs):
        secs = refs[:nsec]
        wt_ref, x_ref, dx2_ref, g_ref = refs[nsec:nsec + 4]
        ex_src = refs[nsec + 4:nsec + 4 + nex]
        dx_ref, gnorm_ref = refs[nsec + 4 + nex:nsec + 6 + nex]
        ex_dst = refs[nsec + 6 + nex:nsec + 6 + 2 * nex]
        acc = refs[nsec + 6 + 2 * nex]
        sems = refs[nsec + 7 + 2 * nex:]
        i, s = pl.program_id(0), pl.program_id(1)

        @pl.when((i == 0) & (s == 0))
        def _():
            gnorm_ref[...] = jnp.zeros_like(gnorm_ref)
            _start_all(_chip_exchange_copies(ex_src, ex_dst, *sems))

        @pl.when((i == ntok - 1) & (s == nchunks - 1))
        def _():
            _wait_all(_chip_exchange_copies(ex_src, ex_dst, *sems))

        @pl.when(s == 0)
        def _():
            acc[...] = jnp.zeros_like(acc)

        for a, (start, n) in enumerate(zip(SEC_START, SEC_CHUNKS)):
            @pl.when((s >= start) & (s < start + n))
            def _(a=a):
                acc[...] += _dot(secs[a][...], wt_ref[...], NN)

        @pl.when(s == nchunks - 1)
        def _():
            xv = x_ref[...]
            rstd = lax.rsqrt(jnp.mean(xv * xv, axis=-1, keepdims=True) + EPS)
            xh = xv * rstd
            dh = acc[...]
            gnorm_ref[0:1, :] += jnp.sum(dh * xh, axis=0, keepdims=True)
            dxn = dh * g_ref[...]
            dx_ref[...] = dx2_ref[...] + rstd * (dxn - xh * jnp.mean(dxn * xh, axis=-1, keepdims=True))

    def sec_spec(a):
        return pl.BlockSpec((tb, CH), lambda i, s, a=a: (i, _section_of_chunk(s)[a][1]))

    tok = pl.BlockSpec((tb, D), lambda i, s: (i, 0))
    res = _pcall(
        body, name="input_grad", grid=(ntok, nchunks),
        in_specs=[sec_spec(a) for a in range(nsec)] + [pl.BlockSpec((CH, D), lambda i, s: (s, 0)), tok, tok,
                                                        pl.BlockSpec((1, D), lambda i, s: (0, 0))] + [ANY] * nex,
        out_specs=(tok, pl.BlockSpec((8, D), lambda i, s: (0, 0))) + tuple([ANY] * nex),
        out_shape=(_sds((T, D), f32), _sds((8, D), f32)) + tuple(_sds(c.shape, c.dtype) for c in chip_sums),
        scratch_shapes=[pltpu.VMEM((tb, D), f32)] + _exchange_scratch(nex, 3),
        compiler_params=_params(("arbitrary", "arbitrary")),
    )(*dsecs, wt_full, x2d, dx2, norm_g, *chip_sums)
    return res[0], res[1], res[2:]


def _w_in_grad(dsecs, h_bf):
    T = h_bf.shape[0]
    tk = min(T, 1024)
    nchunks = D_IN // CH
    nsec = len(dsecs)

    def body(*refs):
        secs = refs[:nsec]
        h_ref, out_ref = refs[nsec:]
        s, t = pl.program_id(0), pl.program_id(1)

        @pl.when(t == 0)
        def _():
            out_ref[...] = jnp.zeros_like(out_ref)

        h_rows = h_ref[pl.ds(pl.multiple_of(t * tk, tk), tk), :]
        for a, (start, n) in enumerate(zip(SEC_START, SEC_CHUNKS)):
            @pl.when((s >= start) & (s < start + n))
            def _(a=a):
                out_ref[...] += _dot(secs[a][...], h_rows, TN)

    def sec_spec(a):
        def index(s, t, a=a):
            inside, local = _section_of_chunk(s)[a]
            return (jnp.where(inside, t, 0), local)
        return pl.BlockSpec((tk, CH), index)

    return _pcall(
        body, name="w_in_grad", grid=(nchunks, T // tk),
        in_specs=[sec_spec(a) for a in range(nsec)] + [pl.BlockSpec((T, D), lambda s, t: (0, 0))],
        out_specs=pl.BlockSpec((CH, D), lambda s, t: (s, 0)), out_shape=_sds((D_IN, D), f32),
        compiler_params=_params(("arbitrary", "arbitrary")),
    )(*dsecs, h_bf)


def _weight_grad(a_mat, b_mat, name):
    T, M = a_mat.shape
    N = b_mat.shape[1]
    tk = min(T, 1024)
    tm = 512

    def body(a_ref, b_ref, out_ref):
        @pl.when(pl.program_id(1) == 0)
        def _():
            out_ref[...] = jnp.zeros_like(out_ref)
        out_ref[...] += _dot(a_ref[...].astype(bf16), b_ref[...].astype(bf16), TN)

    return _pcall(
        body, name=name, grid=(M // tm, T // tk),
        in_specs=[pl.BlockSpec((tk, tm), lambda m, t: (t, m)), pl.BlockSpec((tk, N), lambda m, t: (t, 0))],
        out_specs=pl.BlockSpec((tm, N), lambda m, t: (m, 0)), out_shape=_sds((M, N), f32),
        compiler_params=_params(("arbitrary", "arbitrary")),
    )(a_mat, b_mat)


def _pad_rows(v, rows=8):
    return jnp.concatenate([v, jnp.zeros((rows - v.shape[0], v.shape[1]), v.dtype)], axis=0)


def _lanes(v):
    return jnp.pad(v, ((0, 0), (0, D - v.shape[1])))


def _blocks_to_lanes(w):
    return jnp.transpose(w, (1, 0, 2)).reshape(RB, D)


def _lanes_to_blocks(w):
    return jnp.transpose(w.reshape(RB, RNN_BLOCKS, RB), (1, 0, 2))


def _small_pack(w_a, w_x, conv_b, b_a, b_x, lam, norm_g, fin_g, sinks):
    vec = _pad_rows(jnp.concatenate([conv_b, b_a, b_x, lam], axis=0))
    return jnp.concatenate([_blocks_to_lanes(w_a), _blocks_to_lanes(w_x), vec, _pad_rows(norm_g), _pad_rows(fin_g),
                            _pad_rows(_lanes(sinks)), jnp.zeros((32, D), f32)], axis=0)


def kernel(x, norm_g, w_in, conv_w, conv_b, lru_w_a, lru_b_a, lru_w_x, lru_b_x, lru_lambda, attn_sinks, w_rnn_out, w_attn_out, w_o, final_norm_g, loss_target, m_norm_g, m_w_in, m_conv_w, m_conv_b, m_lru_w_a, m_lru_b_a, m_lru_w_x, m_lru_b_x, m_lru_lambda, m_attn_sinks, m_w_rnn_out, m_w_attn_out, m_w_o, m_final_norm_g, v_norm_g, v_w_in, v_conv_w, v_conv_b, v_lru_w_a, v_lru_b_a, v_lru_w_x, v_lru_b_x, v_lru_lambda, v_attn_sinks, v_w_rnn_out, v_w_attn_out, v_w_o, v_final_norm_g):
    nb, S, _ = x.shape
    T = nb * S
    x2d = x.reshape(T, D)
    tgt = loss_target.reshape(T, D)
    fin_g = final_norm_g.reshape(1, D)
    w_a3, w_x3 = lru_w_a[0], lru_w_x[0]

    my_core = lax.axis_index("c").astype(jnp.int32).reshape(1)
    cx, cy = lax.axis_index("x"), lax.axis_index("y")
    chip_order = jnp.stack([2 * cx + cy, 2 * (1 - cx) + cy, 2 * cx + (1 - cy),
                            2 * (1 - cx) + (1 - cy)]).astype(jnp.int32)

    tabs = _rope_tables(S)
    h_bf, proj, wt_full, cw_full, (wr_full, wa_full, wo_full) = _in_proj_gather(
        x2d, norm_g, w_in[0].T.astype(bf16), _pad_rows(conv_w[0]), tabs, S,
        (w_rnn_out[0], w_attn_out[0], w_o[0]), chip_order)
    y_rnn, h_all = _lru_forward(proj, cw_full, conv_b, w_a3, lru_b_a, w_x3, lru_b_x, lru_lambda, S)
    y_attn = _attn_forward(proj, attn_sinks, S)

    (merged, dx2, dpr, dpa, dy_rnn, dy_attn, dmr, dma, loss_blk, gfin_blk) = _merge_and_head(
        x2d, tgt, proj, y_rnn, y_attn, wr_full, wa_full, wo_full, fin_g)

    g_wr = _weight_grad(y_rnn, dpr, "w_rnn_out_grad")
    g_wa = _weight_grad(y_attn, dpa, "w_attn_out_grad")
    g_wo = _weight_grad(merged, dx2, "w_o_grad")
    sums_out = _pair_sums([g_wr, g_wa, g_wo], [bf16, bf16, bf16], my_core, "out")

    (dq, dkv, dga, dsink_blk), (p_wr, p_wa, p_wo) = _attn_backward(proj, dy_attn, tabs, attn_sinks, S, sums_out)
    du0, dgr, gwa, gwx, gvec, gcw = _lru_backward(proj, h_all, dy_rnn, cw_full, conv_b, w_a3, lru_b_a, w_x3,
                                                  lru_b_x, lru_lambda, S)
    dsecs = (du0, dgr, dq, dkv, dga, dmr, dma)

    g_wt = _w_in_grad(dsecs, h_bf)
    sums_in = _pair_sums([g_wt], [bf16], my_core, "in")
    grad_x2d, gnorm_blk, (p_wt,) = _input_grad(dsecs, wt_full, x2d, dx2, norm_g, sums_in)

    g_small = jnp.concatenate([gwa, gwx, gvec, gnorm_blk, gfin_blk, _pad_rows(_lanes(dsink_blk[0:1, 0:16])),
                               loss_blk, jnp.zeros((24, D), f32)], axis=0)
    p_small, p_cw = _chip_exchange(_pair_sums([g_small, gcw], [f32, f32], my_core, "small"))

    o_wt = _adamw(p_wt, w_in[0].T, m_w_in[0].T, v_w_in[0].T, "adamw_w_in")
    o_wr = _adamw(p_wr, w_rnn_out[0], m_w_rnn_out[0], v_w_rnn_out[0], "adamw_w_rnn_out")
    o_wa = _adamw(p_wa, w_attn_out[0], m_w_attn_out[0], v_w_attn_out[0], "adamw_w_attn_out")
    o_wo = _adamw(p_wo, w_o[0], m_w_o[0], v_w_o[0], "adamw_w_o")
    o_cw = _adamw(p_cw, _pad_rows(conv_w[0]), _pad_rows(m_conv_w[0]), _pad_rows(v_conv_w[0]), "adamw_conv_w")

    zero40 = jnp.zeros((p_small.shape[1], D), f32)
    small_sum = _adamw(p_small, zero40, zero40, zero40, "sum_small")[0]
    g_small_all = _gather_rows(small_sum, "gather_small")
    pack = lambda *t: _small_pack(*t)
    o_small = _adamw(
        g_small_all[None],
        pack(w_a3, w_x3, conv_b, lru_b_a, lru_b_x, lru_lambda, norm_g, fin_g, attn_sinks),
        pack(m_lru_w_a[0], m_lru_w_x[0], m_conv_b, m_lru_b_a, m_lru_b_x, m_lru_lambda, m_norm_g,
             m_final_norm_g.reshape(1, D), m_attn_sinks),
        pack(v_lru_w_a[0], v_lru_w_x[0], v_conv_b, v_lru_b_a, v_lru_b_x, v_lru_lambda, v_norm_g,
             v_final_norm_g.reshape(1, D), v_attn_sinks),
        "adamw_small")

    loss = g_small_all[288, 0] * (0.5 / D)

    def unpack(kind):
        s = o_small[kind]
        return {
            "norm_g": s[264:265], "w_in": o_wt[kind].T[None], "conv_w": o_cw[kind][None, 0:4],
            "conv_b": s[256:257], "lru_w_a": _lanes_to_blocks(s[0:128])[None], "lru_b_a": s[257:258],
            "lru_w_x": _lanes_to_blocks(s[128:256])[None], "lru_b_x": s[258:259], "lru_lambda": s[259:260],
            "attn_sinks": s[280:281, 0:16], "w_rnn_out": o_wr[kind][None], "w_attn_out": o_wa[kind][None],
            "w_o": o_wo[kind][None], "final_norm_g": s[272, :],
        }

    order = ("norm_g", "w_in", "conv_w", "conv_b", "lru_w_a", "lru_b_a", "lru_w_x", "lru_b_x", "lru_lambda",
             "attn_sinks", "w_rnn_out", "w_attn_out", "w_o", "final_norm_g")
    outs = [loss, grad_x2d.reshape(nb, S, D)]
    for kind in range(4):
        d = unpack(kind)
        outs += [d[n] for n in order]
    return tuple(outs)
```

```python
import functools
import math

import jax
import jax.numpy as jnp
from jax import lax
from jax.experimental import pallas as pl
from jax.experimental.pallas import tpu as pltpu

f32 = jnp.float32
bf16 = jnp.bfloat16

D = 1024
D_IN = 6656
NDEV = 8
RNN_BLOCKS = 8
RB = 128
HEAD = 64
KV_HEADS = 4
GROUP = 4
QB = 128
LRU_C = 8.0
EPS = 1e-6
ROPE_DIM = 16
ROPE_THETA = 500000.0
CH = 512
SEC_START = (0, 2, 4, 6, 7, 9, 11)
SEC_CHUNKS = (2, 2, 2, 1, 2, 2, 2)
VMEM_LIMIT = 62 * 1024 * 1024

ADAM_LR, ADAM_B1, ADAM_B2, ADAM_EPS, ADAM_WD, ADAM_STEP = 0.001, 0.9, 0.999, 1e-08, 0.01, 10

MESH = pl.DeviceIdType.MESH
ANY = pl.BlockSpec(memory_space=pl.ANY)
VMEM_SPEC = pl.BlockSpec(memory_space=pltpu.VMEM)
SMEM_SPEC = pl.BlockSpec(memory_space=pltpu.SMEM)


def _pcall(body, **kw):
    return pl.pallas_call(body, **kw)


def _params(sem=None, **kw):
    if sem is not None:
        kw["dimension_semantics"] = sem
    return pltpu.CompilerParams(vmem_limit_bytes=VMEM_LIMIT, **kw)


def _sds(shape, dtype):
    return jax.ShapeDtypeStruct(shape, dtype)


def _dot(a, b, dims):
    return lax.dot_general(a, b, (dims, ((), ())), preferred_element_type=f32)


NN = ((1,), (0,))
NT = ((1,), (1,))
TN = ((0,), (0,))


def _sigmoid(v):
    return 0.5 * jnp.tanh(0.5 * v) + 0.5


def _sigmoid_positive(v):
    return 1.0 / (1.0 + jnp.exp(-v))


def _my_place():
    return lax.axis_index("x"), lax.axis_index("y"), lax.axis_index("c")


def _peer(k):
    x, y, c = _my_place()
    return (x + ((k >> 2) & 1)) % 2, (y + ((k >> 1) & 1)) % 2, (c + (k & 1)) % 2


def _direct_gather_copies(srcs, outs, send_sems, recv_sems, local_sems):
    x, y, c = _my_place()
    me = 4 * x + 2 * y + c
    local, remote = [], []
    for a, (src, out) in enumerate(zip(srcs, outs)):
        r = src.shape[0]
        mine = out.at[pl.ds(pl.multiple_of(me * r, 8), r), :]
        local.append(pltpu.make_async_copy(src, mine, local_sems.at[a]))
        for k in range(1, NDEV):
            remote.append(pltpu.make_async_remote_copy(
                src_ref=src, dst_ref=mine, send_sem=send_sems.at[7 * a + k - 1], recv_sem=recv_sems.at[7 * a + k - 1],
                device_id=_peer(k), device_id_type=MESH))
    return local, remote


def _chip_exchange_copies(src, dst, send_sems, recv_sems, local_sems):
    x, y, c = _my_place()
    local, remote = [], []
    for a in range(len(src)):
        local.append(pltpu.make_async_copy(src[a].at[2 * x + y], dst[a].at[0], local_sems.at[a]))
    for k in (3, 1, 2):
        px, py = (x + (k >> 1)) % 2, (y + (k & 1)) % 2
        for a in range(len(src)):
            remote.append(pltpu.make_async_remote_copy(
                src_ref=src[a].at[2 * px + py], dst_ref=dst[a].at[k],
                send_sem=send_sems.at[3 * a + k - 1], recv_sem=recv_sems.at[3 * a + k - 1],
                device_id=(px, py, c), device_id_type=MESH))
    return local, remote


def _exchange_scratch(narr, per_array):
    return [pltpu.SemaphoreType.DMA((per_array * narr,)), pltpu.SemaphoreType.DMA((per_array * narr,)),
            pltpu.SemaphoreType.DMA((narr,))]


def _start_all(copies):
    local, remote = copies
    for cp in local + remote:
        cp.start()


def _wait_all(copies):
    local, remote = copies
    for cp in remote + local:
        cp.wait()


def _gather_rows(blk, name):
    nrows, ncols = blk.shape

    def body(src, out, send_sems, recv_sems, local_sem):
        x, y, c = _my_place()
        me, sibling = (x, y, c), (x, y, 1 - c)
        chips = [(1 - x, y), (x, 1 - y), (1 - x, 1 - y)]

        def rows(place):
            px, py, pc = place
            start = pl.multiple_of((4 * px + 2 * py + pc) * nrows, 8)
            return out.at[pl.ds(start, nrows), :]

        def copy(k, block, to, from_src=False):
            return pltpu.make_async_remote_copy(
                src_ref=src if from_src else rows(block), dst_ref=rows(block),
                send_sem=send_sems.at[k], recv_sem=recv_sems.at[k], device_id=to, device_id_type=MESH)

        mine = pltpu.make_async_copy(src, rows(me), local_sem)
        mine.start()
        first = [copy(0, me, sibling, True)]
        first += [copy(1 + j, me, (*chip, c), True) for j, chip in enumerate(chips)]
        for cp in first:
            cp.start()
        passed = []
        for j, chip in enumerate(chips):
            copy(1 + j, (*chip, c), me).wait_recv()
            fwd = copy(4 + j, (*chip, c), sibling)
            fwd.start()
            passed.append(fwd)
        copy(0, sibling, me).wait_recv()
        for j, chip in enumerate(chips):
            copy(4 + j, (*chip, 1 - c), me).wait_recv()
        for cp in first + passed:
            cp.wait_send()
        mine.wait()

    return _pcall(
        body, name=name, out_shape=_sds((NDEV * nrows, ncols), blk.dtype),
        in_specs=[ANY], out_specs=ANY,
        scratch_shapes=[pltpu.SemaphoreType.DMA((7,)), pltpu.SemaphoreType.DMA((7,)), pltpu.SemaphoreType.DMA],
        compiler_params=_params(),
    )(blk)


def _pair_exchange(grads, name):
    narr = len(grads)
    nrows = tuple(g.shape[0] // NDEV for g in grads)
    views = [g.reshape(4, 2, r, g.shape[1]) for g, r in zip(grads, nrows)]

    def body(*refs):
        gin = refs[:narr]
        got = refs[narr:2 * narr]
        send_sems, recv_sems = refs[2 * narr:]
        x, y, c = _my_place()
        copies = [pltpu.make_async_remote_copy(
            src_ref=gin[a].at[:, pl.ds(1 - c, 1)], dst_ref=got[a],
            send_sem=send_sems.at[a], recv_sem=recv_sems.at[a],
            device_id=(x, y, 1 - c), device_id_type=MESH) for a in range(narr)]
        for cp in copies:
            cp.start()
        for cp in copies:
            cp.wait()

    out_shape = tuple(_sds((4, 1, r, g.shape[1]), f32) for r, g in zip(nrows, grads))
    got = _pcall(
        body, name=name, out_shape=out_shape,
        in_specs=[ANY] * narr, out_specs=tuple([ANY] * narr),
        scratch_shapes=[pltpu.SemaphoreType.DMA((narr,)), pltpu.SemaphoreType.DMA((narr,))],
        compiler_params=_params(),
    )(*views)
    return views, [g.reshape(4, r, g.shape[3]) for g, r in zip(got, nrows)]


def _row_tile(rows, dtype):
    unit = 16 if dtype == bf16 else 8
    for cand in (256, 208, 128, 64, 40, 32, 16, 8):
        if rows % cand == 0 and cand % unit == 0:
            return cand
    return rows


def _chip_sum(view, got, my_core, out_dtype, name):
    _, _, r, cols = view.shape
    tr = _row_tile(r, out_dtype)

    def body(core_ref, mine_ref, got_ref, out_ref):
        out_ref[...] = (mine_ref[...] + got_ref[...]).astype(out_dtype)

    grid_spec = pltpu.PrefetchScalarGridSpec(
        num_scalar_prefetch=1, grid=(4, r // tr),
        in_specs=[pl.BlockSpec((None, None, tr, cols), lambda q, i, core: (q, core[0], i, 0)),
                  pl.BlockSpec((None, tr, cols), lambda q, i, core: (q, i, 0))],
        out_specs=pl.BlockSpec((None, tr, cols), lambda q, i, core: (q, i, 0)))
    return _pcall(body, name=name, grid_spec=grid_spec, out_shape=_sds((4, r, cols), out_dtype),
                  compiler_params=_params(("arbitrary", "arbitrary")))(my_core, view, got)


def _chip_exchange(sums):
    narr = len(sums)

    def body(*refs):
        copies = _chip_exchange_copies(refs[:narr], refs[narr:2 * narr], *refs[2 * narr:])
        _start_all(copies)
        _wait_all(copies)

    out_shape = tuple(_sds(s.shape, s.dtype) for s in sums)
    return _pcall(
        body, name="chip_exchange", out_shape=out_shape,
        in_specs=[ANY] * narr, out_specs=tuple([ANY] * narr),
        scratch_shapes=_exchange_scratch(narr, 3), compiler_params=_params(),
    )(*sums)


def _pair_sums(grads, wire_dtypes, my_core, tag):
    views, got = _pair_exchange(grads, "pair_exchange_" + tag)
    return [_chip_sum(v, g, my_core, dt, "chip_sum_%s%d" % (tag, a))
            for a, (v, g, dt) in enumerate(zip(views, got, wire_dtypes))]


def _adamw(parts, w, m, v, name):
    n, rows, cols = parts.shape
    tr = _row_tile(rows, parts.dtype)

    def body(p_ref, w_ref, m_ref, v_ref, g_out, d_out, m_out, v_out):
        g = p_ref[0].astype(f32)
        for s in range(1, n):
            g = g + p_ref[s].astype(f32)
        m_new = ADAM_B1 * m_ref[...] + (1.0 - ADAM_B1) * g
        v_new = ADAM_B2 * v_ref[...] + (1.0 - ADAM_B2) * (g * g)
        m_hat = m_new / (1.0 - ADAM_B1 ** ADAM_STEP)
        v_hat = v_new / (1.0 - ADAM_B2 ** ADAM_STEP)
        g_out[...] = g
        d_out[...] = -ADAM_LR * (m_hat / (jnp.sqrt(v_hat) + ADAM_EPS) + ADAM_WD * w_ref[...])
        m_out[...] = m_new
        v_out[...] = v_new

    blk = pl.BlockSpec((tr, cols), lambda i: (i, 0))
    return _pcall(
        body, name=name, grid=(rows // tr,),
        in_specs=[pl.BlockSpec((n, tr, cols), lambda i: (0, i, 0)), blk, blk, blk],
        out_specs=(blk, blk, blk, blk), out_shape=tuple(_sds((rows, cols), f32) for _ in range(4)),
        compiler_params=_params(("arbitrary",)),
    )(parts, w, m, v)


def _rope(t, c, s1, s2):
    w = t.shape[1]
    return t * c + pltpu.roll(t, w - 8, 1) * s1 + pltpu.roll(t, 8, 1) * s2


def _rope_transposed(dt, c, s1, s2):
    w = dt.shape[1]
    return dt * c + pltpu.roll(dt * s1, 8, 1) + pltpu.roll(dt * s2, w - 8, 1)


PAIR_ROWS = D_IN // 4
SUB_COLS = ((0, 512), (512, 512), (1024, 512), (1536, 128))
Q_SLABS = range(3, 11)
K_SLABS = range(11, 13)


def _in_proj_gather(x2d, norm_g, wt_shard, cw_shard, tabs, S, out_shards, chip_order):
    T = x2d.shape[0]
    tb = min(S, 1024)
    ntok = T // tb
    nsb = S // tb
    q_scale = 1.0 / math.sqrt(HEAD)
    shard_rows = wt_shard.shape[0]
    small = (cw_shard,) + tuple(out_shards)
    nsm = len(small)

    def body(order_ref, x_ref, g_ref, c_ref, s1_ref, s2_ref, wt_hbm, *rest):
        small_in = rest[:nsm]
        h_ref, proj_ref, wt_out = rest[nsm:nsm + 3]
        small_out = rest[nsm + 3:2 * nsm + 3]
        wt_vm, h_vm = rest[2 * nsm + 3:2 * nsm + 5]
        stage = rest[2 * nsm + 5:3 * nsm + 4]
        wsend, wrecv, wlocal = rest[3 * nsm + 4:3 * nsm + 7]
        dsems = rest[3 * nsm + 7:]
        jj, i = pl.program_id(0), pl.program_id(1)
        x, y, c = _my_place()
        me, sibling = (x, y, c), (x, y, 1 - c)
        chips = [(1 - x, y), (x, 1 - y), (1 - x, 1 - y)]

        def rows(place):
            px, py, pc = place
            return wt_vm.at[pl.ds(pl.multiple_of((4 * px + 2 * py + pc) * shard_rows, 16), shard_rows), :]

        def copy(k, block, to, src=None):
            return pltpu.make_async_remote_copy(
                src_ref=rows(block) if src is None else src, dst_ref=rows(block),
                send_sem=wsend.at[k], recv_sem=wrecv.at[k], device_id=to, device_id_type=MESH)

        def small_copies():
            srcs = (small_in[0],) + tuple(stage)
            return _direct_gather_copies(srcs, small_out, *dsems)

        own = pltpu.make_async_copy(wt_hbm, rows(me), wlocal.at[0])
        keep = pltpu.make_async_copy(wt_vm, wt_out, wlocal.at[1])

        @pl.when((jj == 0) & (i == 0))
        def _():
            own.start()
            copy(0, me, sibling, src=wt_hbm).start()
            for j, chip in enumerate(chips):
                copy(1 + j, me, (*chip, c), src=wt_hbm).start()
            for a in range(nsm - 1):
                stage[a][...] = small_in[1 + a][...].astype(bf16)
            _start_all(small_copies())
            own.wait()
            copy(0, sibling, me).wait_recv()

        for j, chip in enumerate(chips):
            @pl.when((jj == 1 + j) & (i == 0))
            def _(j=j, chip=chip):
                copy(1 + j, (*chip, c), me).wait_recv()
                copy(4 + j, (*chip, c), sibling).start()
                copy(4 + j, (*chip, 1 - c), me).wait_recv()

        @pl.when((jj == 3) & (i == 0))
        def _():
            keep.start()

        @pl.when((jj == 3) & (i == ntok - 1))
        def _():
            copy(0, me, sibling, src=wt_hbm).wait_send()
            for j, chip in enumerate(chips):
                copy(1 + j, me, (*chip, c), src=wt_hbm).wait_send()
                copy(4 + j, (*chip, c), sibling).wait_send()
            _wait_all(small_copies())
            keep.wait()

        tok = pl.ds(pl.multiple_of(i * tb, tb), tb)

        @pl.when(jj == 0)
        def _():
            xv = x_ref[...]
            ms = jnp.mean(xv * xv, axis=-1, keepdims=True)
            hb = (xv * lax.rsqrt(ms + EPS) * g_ref[...]).astype(bf16)
            h_ref[...] = hb
            h_vm[tok, :] = hb

        block = order_ref[jj]
        hb = h_vm[tok, :]

        def piece(c0, w):
            w_rows = wt_vm[pl.ds(pl.multiple_of(block * PAIR_ROWS + c0, 128), w), :]
            return _dot(hb, w_rows, NT)

        @pl.when(block != 1)
        def _():
            for c0, w in SUB_COLS:
                proj_ref[:, c0:c0 + w] = piece(c0, w).astype(bf16)

        @pl.when(block == 1)
        def _():
            tab = (c_ref[...], s1_ref[...], s2_ref[...])
            for c0, w in SUB_COLS:
                acc = piece(c0, w)
                for l in range(w // 128):
                    slab = (c0 + 128 * l) // 128
                    part = acc[:, 128 * l:128 * (l + 1)]
                    if slab in Q_SLABS:
                        part = _rope(part, *tab) * q_scale
                    elif slab in K_SLABS:
                        part = _rope(part, *tab)
                    proj_ref[:, 128 * slab:128 * (slab + 1)] = part.astype(bf16)

    first_pass = lambda jj, i, order: (jnp.where(jj == 0, i, ntok - 1), 0)
    const = lambda jj, i, order: (0, 0)
    tab = pl.BlockSpec((tb, 128), lambda jj, i, order: (jnp.where(order[jj] == 1, i % nsb, 0), 0))
    grid_spec = pltpu.PrefetchScalarGridSpec(
        num_scalar_prefetch=1, grid=(4, ntok),
        in_specs=[pl.BlockSpec((tb, D), first_pass), pl.BlockSpec((1, D), const), tab, tab, tab, ANY]
        + [pl.BlockSpec(w.shape, const) for w in small],
        out_specs=(pl.BlockSpec((tb, D), first_pass),
                   pl.BlockSpec((tb, PAIR_ROWS), lambda jj, i, order: (i, order[jj])), ANY) + tuple([ANY] * nsm),
        scratch_shapes=[pltpu.VMEM((D_IN, D), bf16), pltpu.VMEM((T, D), bf16)]
        + [pltpu.VMEM(w.shape, bf16) for w in out_shards]
        + [pltpu.SemaphoreType.DMA((7,)), pltpu.SemaphoreType.DMA((7,)), pltpu.SemaphoreType.DMA((2,))]
        + _exchange_scratch(nsm, 7))
    res = _pcall(
        body, name="in_proj", grid_spec=grid_spec,
        out_shape=(_sds((T, D), bf16), _sds((T, D_IN), bf16), _sds((D_IN, D), bf16),
                   _sds((NDEV * cw_shard.shape[0], cw_shard.shape[1]), f32))
        + tuple(_sds((NDEV * w.shape[0], w.shape[1]), bf16) for w in out_shards),
        compiler_params=_params(("arbitrary", "arbitrary")),
    )(chip_order, x2d, norm_g, *tabs, wt_shard, *small)
    return res[0], res[1], res[2], res[3], res[4:]


def _rows_iota(shape):
    return lax.broadcasted_iota(jnp.int32, shape, 0)


def _shift_down(v, k):
    return jnp.where(_rows_iota(v.shape) >= k, pltpu.roll(v, k, 0), 0.0)


def _shift_up(v, k):
    n = v.shape[0]
    return jnp.where(_rows_iota(v.shape) < n - k, pltpu.roll(v, n - k, 0), 0.0)


def _linear_scan(a, b, a_s, b_s, edge_s, out_ref, reverse):
    n = a.shape[0]
    ng = n // 8
    a3, b3 = a.reshape(ng, 8, RB), b.reshape(ng, 8, RB)
    rid = lax.broadcasted_iota(jnp.int32, a3.shape, 1)
    for s in (1, 2, 4):
        keep, shift = (rid < 8 - s, 8 - s) if reverse else (rid >= s, s)
        b3 = jnp.where(keep, a3 * pltpu.roll(b3, shift, 1) + b3, b3)
        a3 = jnp.where(keep, a3 * pltpu.roll(a3, shift, 1), a3)
    a_s[...] = a3.reshape(n, RB)
    b_s[...] = b3.reshape(n, RB)
    edge = 0 if reverse else 7
    ea, eb = a_s[pl.ds(edge, ng, stride=8), :], b_s[pl.ds(edge, ng, stride=8), :]
    r = _rows_iota(ea.shape)
    s = 1
    while s < ng:
        keep, shift = (r < ng - s, ng - s) if reverse else (r >= s, s)
        eb = jnp.where(keep, ea * pltpu.roll(eb, shift, 0) + eb, eb)
        if 2 * s < ng:
            ea = jnp.where(keep, ea * pltpu.roll(ea, shift, 0), ea)
        s *= 2
    edge_s[...] = _shift_up(eb, 1) if reverse else _shift_down(eb, 1)

    def eight_groups(i, carry):
        for k in range(8):
            j = i * 8 + k
            rows = pl.ds(pl.multiple_of(j * 8, 8), 8)
            out_ref[rows, :] = b_s[rows, :] + a_s[rows, :] * edge_s[pl.ds(j, 1), :]
        return carry

    lax.fori_loop(0, ng // 8, eight_groups, 0)


def _neg_expm1(v):
    series = -v * (1.0 + v * (0.5 + v * (1.0 / 6.0 + v * (1.0 / 24.0))))
    return jnp.where(v > -0.03125, series, 1.0 - jnp.exp(v))


def _softplus_neg(lam):
    return jnp.maximum(-lam, 0.0) + jnp.log(1.0 + jnp.exp(-jnp.abs(lam)))


def _lru_gates(x0, cw, cb, wa, ba, wx, bx, lam):
    u = cb + cw[3:4, :] * x0
    for k in range(3):
        u = u + cw[k:k + 1, :] * _shift_down(x0, 3 - k)
    ub = u.astype(bf16)
    r = _sigmoid_positive(_dot(ub, wa.astype(bf16), NN) + ba)
    i = _sigmoid(_dot(ub, wx.astype(bf16), NN) + bx)
    sp = _softplus_neg(lam)
    log_a = (-LRU_C) * r * sp
    a = jnp.exp(log_a)
    mult = jnp.sqrt(_neg_expm1(2.0 * log_a))
    return u, ub, r, i, sp, a, mult


def _lru_specs(S, nb):
    col = lambda off: pl.BlockSpec((S, RB), lambda n, b, off=off: (b, off + n))
    vec = pl.BlockSpec((1, RB), lambda n, b: (0, n))
    wblk = pl.BlockSpec((None, RB, RB), lambda n, b: (n, 0, 0))
    cwblk = pl.BlockSpec((8, RB), lambda n, b: (n, 0))
    return col, vec, wblk, cwblk


def _lru_forward(proj, cw_full, conv_b, w_a, b_a, w_x, b_x, lam, S):
    T = proj.shape[0]
    nb = T // S
    col, vec, wblk, cwblk = _lru_specs(S, nb)

    def body(x0_ref, g_ref, cw_ref, cb_ref, wa_ref, ba_ref, wx_ref, bx_ref, lam_ref, y_ref, h_ref, a_s, b_s, edge_s):
        x0 = x0_ref[...].astype(f32)
        u, ub, r, i, sp, a, mult = _lru_gates(x0, cw_ref[...], cb_ref[...], wa_ref[...], ba_ref[...],
                                              wx_ref[...], bx_ref[...], lam_ref[...])
        _linear_scan(a, mult * (i * u), a_s, b_s, edge_s, h_ref, reverse=False)
        g = g_ref[...].astype(f32)
        y_ref[...] = (h_ref[...] * (g * _sigmoid(g))).astype(bf16)

    out = pl.BlockSpec((S, RB), lambda n, b: (b, n))
    return _pcall(
        body, name="lru_forward", grid=(RNN_BLOCKS, nb),
        in_specs=[col(0), col(8), cwblk, vec, wblk, vec, wblk, vec, vec],
        out_specs=(out, out), out_shape=(_sds((T, D), bf16), _sds((T, D), f32)),
        scratch_shapes=[pltpu.VMEM((S, RB), f32), pltpu.VMEM((S, RB), f32), pltpu.VMEM((S // 8, RB), f32)],
        compiler_params=_params(("arbitrary", "arbitrary")),
    )(proj, proj, cw_full, conv_b, w_a, b_a, w_x, b_x, lam)


def _rope_tables(S):
    pos = jnp.arange(S, dtype=f32)
    inv_freq = ROPE_THETA ** (-jnp.arange(0, ROPE_DIM, 2, dtype=f32) / ROPE_DIM)
    ang = pos[:, None] * inv_freq[None, :]
    cos, sin = jnp.cos(ang), jnp.sin(ang)
    lane = jnp.arange(128) % HEAD
    cosl, sinl = cos[:, lane % 8], sin[:, lane % 8]
    c = jnp.where(lane[None, :] < ROPE_DIM, cosl, 1.0)
    s1 = jnp.where(lane[None, :] < 8, -sinl, 0.0)
    s2 = jnp.where((lane[None, :] >= 8) & (lane[None, :] < ROPE_DIM), sinl, 0.0)
    return c.astype(f32), s1.astype(f32), s2.astype(f32)


def _heads_to_rows(t):
    return jnp.concatenate([t[:, HEAD * h:HEAD * (h + 1)] for h in range(GROUP)], axis=0)


def _rows_to_heads(t):
    return jnp.concatenate([t[QB * h:QB * (h + 1), :] for h in range(GROUP)], axis=1)


def _window_bias(first_block):
    shape = (GROUP * QB, 2 * QB)
    qi = _rows_iota(shape) % QB
    cj = lax.broadcasted_iota(jnp.int32, shape, 1)
    valid = (cj > qi) & (cj <= qi + QB) & ((cj >= QB) | jnp.logical_not(first_block))
    return jnp.where(valid, 0.0, -jnp.inf)


def _attn_probs(q_rows, k_cat, sink_col, bias):
    s = _dot(q_rows, k_cat, NT) + bias
    m = jnp.maximum(jnp.max(s, axis=1, keepdims=True), sink_col)
    p = jnp.exp(s - m)
    e_sink = jnp.exp(sink_col - m)
    inv = 1.0 / (jnp.sum(p, axis=1, keepdims=True) + e_sink)
    return p * inv, e_sink * inv


def _sink_column(sink_ref, kv):
    rid = _rows_iota((GROUP * QB, 1))
    col = jnp.zeros((GROUP * QB, 1), f32)
    for h in range(GROUP):
        col = jnp.where(rid // QB == h, sink_ref[0, GROUP * kv + h], col)
    return col


def _attn_in_specs(S):
    nq = S // QB
    last = nq - 1
    cur = lambda b, j: b * nq + jnp.minimum(j, last)
    prev = lambda b, j: b * nq + jnp.maximum(jnp.minimum(j, last) - 1, 0)
    specs = [
        pl.BlockSpec((QB, D), lambda b, j: (cur(b, j), 2)),
        pl.BlockSpec((QB, 256), lambda b, j: (cur(b, j), 12)),
        pl.BlockSpec((QB, 256), lambda b, j: (prev(b, j), 12)),
        pl.BlockSpec((QB, 256), lambda b, j: (cur(b, j), 13)),
        pl.BlockSpec((QB, 256), lambda b, j: (prev(b, j), 13)),
        pl.BlockSpec((QB, 512), lambda b, j: (cur(b, j), 7)),
        pl.BlockSpec((QB, 512), lambda b, j: (cur(b, j), 8)),
        SMEM_SPEC,
    ]
    return specs, cur, prev


def _attn_forward(proj, sinks, S):
    T = proj.shape[0]
    nb, nq = T // S, S // QB
    specs, cur, _ = _attn_in_specs(S)

    def body(q_ref, kc_ref, kp_ref, vc_ref, vp_ref, gl_ref, gh_ref, sink_ref, y_ref):
        bias = _window_bias(pl.program_id(1) == 0)
        kc, kp, vc, vp = kc_ref[...], kp_ref[...], vc_ref[...], vp_ref[...]
        for kv in range(KV_HEADS):
            lanes = slice(256 * kv, 256 * (kv + 1))
            hl = slice(HEAD * kv, HEAD * (kv + 1))
            q_rows = _heads_to_rows(q_ref[:, lanes])
            k_cat = jnp.concatenate([kp[:, hl], kc[:, hl]], axis=0)
            v_cat = jnp.concatenate([vp[:, hl], vc[:, hl]], axis=0)
            probs, _ = _attn_probs(q_rows, k_cat, _sink_column(sink_ref, kv), bias)
            o = _rows_to_heads(_dot(probs.astype(bf16), v_cat, NN))
            g_src = gl_ref if kv < 2 else gh_ref
            g = g_src[:, 256 * (kv % 2):256 * (kv % 2 + 1)].astype(f32)
            y_ref[:, lanes] = (o * (g * _sigmoid(g))).astype(bf16)

    args = [proj] * 7 + [sinks]
    return _pcall(
        body, name="attn_forward", grid=(nb, nq), in_specs=specs,
        out_specs=pl.BlockSpec((QB, D), lambda b, j: (cur(b, j), 0)), out_shape=_sds((T, D), bf16),
        compiler_params=_params(("arbitrary", "arbitrary")),
    )(*args)


def _merge_and_head(x2d, tgt, proj, y_rnn, y_attn, w_r, w_a, w_o, gfin):
    T = x2d.shape[0]
    tb = min(T, 512)
    nsteps = T // tb

    def body(x_ref, t_ref, mr0, mr1, ma0, ma1, yr_ref, ya_ref, wr_ref, wa_ref, wo_ref, gf_ref,
             dx2_ref, dyr_ref, dya_ref, dmr_ref, dma_ref, loss_ref, gfin_ref, gwr_out, gwa_out, gwo_out,
             gwr_acc, gwa_acc, gwo_acc, out_sems):
        step = pl.program_id(0)

        @pl.when(step == 0)
        def _():
            loss_ref[...] = jnp.zeros_like(loss_ref)
            gfin_ref[...] = jnp.zeros_like(gfin_ref)
            gwr_acc[...] = jnp.zeros_like(gwr_acc)
            gwa_acc[...] = jnp.zeros_like(gwa_acc)
            gwo_acc[...] = jnp.zeros_like(gwo_acc)

        sr = _sigmoid(jnp.concatenate([mr0[...], mr1[...]], axis=1).astype(f32))
        sa = _sigmoid(jnp.concatenate([ma0[...], ma1[...]], axis=1).astype(f32))
        p_r = _dot(yr_ref[...], wr_ref[...], NN)
        p_a = _dot(ya_ref[...], wa_ref[...], NN)
        merged = (sr * p_r + sa * p_a).astype(bf16)
        x2 = x_ref[...] + _dot(merged, wo_ref[...], NN)
        rstd = lax.rsqrt(jnp.mean(x2 * x2, axis=-1, keepdims=True) + EPS)
        xh = x2 * rstd
        gf = gf_ref[...]
        err = xh * gf - t_ref[...]
        loss_ref[...] += jnp.sum(err * err)
        dy = err * (1.0 / D)
        gfin_ref[0:1, :] += jnp.sum(dy * xh, axis=0, keepdims=True)
        dxn = dy * gf
        dx2 = rstd * (dxn - xh * jnp.mean(dxn * xh, axis=-1, keepdims=True))
        dx2_ref[...] = dx2
        dx2b = dx2.astype(bf16)
        dmerged = _dot(dx2b, wo_ref[...], NT)
        dmr_ref[...] = (dmerged * p_r * (sr * (1.0 - sr))).astype(bf16)
        dma_ref[...] = (dmerged * p_a * (sa * (1.0 - sa))).astype(bf16)
        dpr = (dmerged * sr).astype(bf16)
        dpa = (dmerged * sa).astype(bf16)
        dyr_ref[...] = _dot(dpr, wr_ref[...], NT).astype(bf16)
        dya_ref[...] = _dot(dpa, wa_ref[...], NT).astype(bf16)
        gwr_acc[...] += _dot(yr_ref[...], dpr, TN)
        gwa_acc[...] += _dot(ya_ref[...], dpa, TN)
        gwo_acc[...] += _dot(merged, dx2b, TN)

        @pl.when(step == nsteps - 1)
        def _():
            copies = [pltpu.make_async_copy(src, dst, out_sems.at[k]) for k, (src, dst) in enumerate(
                ((gwr_acc, gwr_out), (gwa_acc, gwa_out), (gwo_acc, gwo_out)))]
            for cp in copies:
                cp.start()
            for cp in copies:
                cp.wait()

    tok = pl.BlockSpec((tb, D), lambda i: (i, 0))
    half = lambda c: pl.BlockSpec((tb, CH), lambda i, c=c: (i, c))
    wfull = pl.BlockSpec((D, D), lambda i: (0, 0), pipeline_mode=pl.Buffered(1))
    acc = pl.BlockSpec((8, D), lambda i: (0, 0))
    return _pcall(
        body, name="merge_and_head", grid=(nsteps,),
        in_specs=[tok, tok, half(9), half(10), half(11), half(12), tok, tok, wfull, wfull, wfull,
                  pl.BlockSpec((1, D), lambda i: (0, 0))],
        out_specs=(tok, tok, tok, tok, tok, acc, acc, ANY, ANY, ANY),
        out_shape=(_sds((T, D), f32), _sds((T, D), bf16), _sds((T, D), bf16), _sds((T, D), bf16),
                   _sds((T, D), bf16), _sds((8, D), f32), _sds((8, D), f32),
                   _sds((D, D), f32), _sds((D, D), f32), _sds((D, D), f32)),
        scratch_shapes=[pltpu.VMEM((D, D), f32)] * 3 + [pltpu.SemaphoreType.DMA((3,))],
        compiler_params=_params(("arbitrary",)),
    )(x2d, tgt, proj, proj, proj, proj, y_rnn, y_attn, w_r, w_a, w_o, gfin)


def _attn_backward(proj, dy_attn, tabs, sinks, S, chip_sums):
    T = proj.shape[0]
    nb, nq = T // S, S // QB
    nex = len(chip_sums)
    specs, cur, prev = _attn_in_specs(S)
    last = nq - 1
    tab_cur = pl.BlockSpec((QB, 128), lambda b, j: (jnp.minimum(j, last), 0))
    tab_prev = pl.BlockSpec((QB, 128), lambda b, j: (jnp.maximum(jnp.minimum(j, last) - 1, 0), 0))
    specs = specs + [pl.BlockSpec((QB, D), lambda b, j: (cur(b, j), 0))] + [tab_cur] * 3 + [tab_prev] * 3
    q_scale = 1.0 / math.sqrt(HEAD)

    def rope_back(dt, tab):
        return jnp.concatenate([_rope_transposed(dt[:, 128 * l:128 * (l + 1)], *tab) for l in range(2)], axis=1)

    def body(q_ref, kc_ref, kp_ref, vc_ref, vp_ref, gl_ref, gh_ref, sink_ref, dy_ref, cc, s1c, s2c, cp, s1p, s2p,
             *rest):
        ex_src = rest[:nex]
        dq_ref, dkv_ref, dg_ref, dsink_ref = rest[nex:nex + 4]
        ex_dst = rest[nex + 4:2 * nex + 4]
        carry_k, carry_v = rest[2 * nex + 4:2 * nex + 6]
        sems = rest[2 * nex + 6:]
        b, j = pl.program_id(0), pl.program_id(1)

        @pl.when((b == 0) & (j == 0))
        def _():
            dsink_ref[...] = jnp.zeros_like(dsink_ref)
            _start_all(_chip_exchange_copies(ex_src, ex_dst, *sems))

        @pl.when((b == nb - 1) & (j == nq))
        def _():
            _wait_all(_chip_exchange_copies(ex_src, ex_dst, *sems))

        @pl.when(j == 0)
        def _():
            carry_k[...] = jnp.zeros_like(carry_k)
            carry_v[...] = jnp.zeros_like(carry_v)

        @pl.when(j < nq)
        def _():
            bias = _window_bias(j == 0)
            tc = (cc[...], s1c[...], s2c[...])
            tp = (cp[...], s1p[...], s2p[...])
            kc, kp, vc, vp = kc_ref[...], kp_ref[...], vc_ref[...], vp_ref[...]
            dk_prev, dk_cur, dv_prev, dv_cur = [], [], [], []
            dsink_acc = jnp.zeros((8, 128), f32)
            r8 = lax.broadcasted_iota(jnp.int32, (8, 128), 0)
            l8 = lax.broadcasted_iota(jnp.int32, (8, 128), 1)
            for kv in range(KV_HEADS):
                lanes = slice(256 * kv, 256 * (kv + 1))
                hl = slice(HEAD * kv, HEAD * (kv + 1))
                q_rows = _heads_to_rows(q_ref[:, lanes])
                k_cat = jnp.concatenate([kp[:, hl], kc[:, hl]], axis=0)
                v_cat = jnp.concatenate([vp[:, hl], vc[:, hl]], axis=0)
                probs, p_sink = _attn_probs(q_rows, k_cat, _sink_column(sink_ref, kv), bias)
                pb = probs.astype(bf16)
                o = _rows_to_heads(_dot(pb, v_cat, NN))
                g_src = gl_ref if kv < 2 else gh_ref
                g = g_src[:, 256 * (kv % 2):256 * (kv % 2 + 1)].astype(f32)
                sg = _sigmoid(g)
                dy = dy_ref[:, lanes].astype(f32)
                dg_ref[:, lanes] = (dy * o * (sg * (1.0 + g * (1.0 - sg)))).astype(bf16)
                do_rows = _heads_to_rows(dy * (g * sg)).astype(bf16)
                dv = _dot(pb, do_rows, TN)
                dp = _dot(do_rows, v_cat, NT)
                rowdot = jnp.sum(probs * dp, axis=1, keepdims=True)
                ds = (probs * (dp - rowdot)).astype(bf16)
                sink_rows = -(p_sink * rowdot)
                for h in range(GROUP):
                    val = jnp.sum(sink_rows[QB * h:QB * (h + 1), :])
                    dsink_acc = dsink_acc + jnp.where((r8 == 0) & (l8 == GROUP * kv + h), val, 0.0)
                dq = _rows_to_heads(_dot(ds, k_cat, NN)) * q_scale
                dq_ref[:, lanes] = rope_back(dq, tc).astype(bf16)
                dk = _dot(ds, q_rows, TN)
                dk_prev.append(dk[:QB, :])
                dk_cur.append(dk[QB:, :])
                dv_prev.append(dv[:QB, :])
                dv_cur.append(dv[QB:, :])
            dsink_ref[...] += dsink_acc
            dkp = rope_back(jnp.concatenate(dk_prev, axis=1), tp)
            dkc = rope_back(jnp.concatenate(dk_cur, axis=1), tc)
            dkv_ref[:, 0:256] = (carry_k[...] + dkp).astype(bf16)
            dkv_ref[:, 256:512] = (carry_v[...] + jnp.concatenate(dv_prev, axis=1)).astype(bf16)
            carry_k[...] = dkc
            carry_v[...] = jnp.concatenate(dv_cur, axis=1)

        @pl.when(j == nq)
        def _():
            dkv_ref[:, 0:256] = carry_k[...].astype(bf16)
            dkv_ref[:, 256:512] = carry_v[...].astype(bf16)

    lag = lambda b, j: (b * nq + jnp.maximum(j - 1, 0), 0)
    args = [proj] * 7 + [sinks, dy_attn] + list(tabs) + list(tabs) + list(chip_sums)
    res = _pcall(
        body, name="attn_backward", grid=(nb, nq + 1), in_specs=specs + [ANY] * nex,
        out_specs=(pl.BlockSpec((QB, D), lambda b, j: (cur(b, j), 0)), pl.BlockSpec((QB, 512), lag),
                   pl.BlockSpec((QB, D), lambda b, j: (cur(b, j), 0)), pl.BlockSpec((8, 128), lambda b, j: (0, 0)))
        + tuple([ANY] * nex),
        out_shape=(_sds((T, D), bf16), _sds((T, 512), bf16), _sds((T, D), bf16), _sds((8, 128), f32))
        + tuple(_sds(s.shape, s.dtype) for s in chip_sums),
        scratch_shapes=[pltpu.VMEM((QB, 256), f32), pltpu.VMEM((QB, 256), f32)] + _exchange_scratch(nex, 3),
        compiler_params=_params(("arbitrary", "arbitrary")),
    )(*args)
    return res[:4], res[4:]


def _lru_backward(proj, h_all, dy_rnn, cw_full, conv_b, w_a, b_a, w_x, b_x, lam, S):
    T = proj.shape[0]
    nb = T // S
    col, vec, wblk, cwblk = _lru_specs(S, nb)
    tokblk = pl.BlockSpec((S, RB), lambda n, b: (b, n))

    def body(x0_ref, g_ref, h_ref, dy_ref, cw_ref, cb_ref, wa_ref, ba_ref, wx_ref, bx_ref, lam_ref,
             du0_ref, dg_ref, gwa_ref, gwx_ref, vec_ref, gcw_ref, a_s, b_s, dh_s, edge_s):
        @pl.when(pl.program_id(1) == 0)
        def _():
            gwa_ref[...] = jnp.zeros_like(gwa_ref)
            gwx_ref[...] = jnp.zeros_like(gwx_ref)
            vec_ref[...] = jnp.zeros_like(vec_ref)
            gcw_ref[...] = jnp.zeros_like(gcw_ref)

        x0 = x0_ref[...].astype(f32)
        cw = cw_ref[...]
        lam_v = lam_ref[...]
        u, ub, r, i, sp, a, mult = _lru_gates(x0, cw, cb_ref[...], wa_ref[...], ba_ref[...],
                                              wx_ref[...], bx_ref[...], lam_v)
        h = h_ref[...]
        g = g_ref[...].astype(f32)
        dy = dy_ref[...].astype(f32)
        sg = _sigmoid(g)
        dg_ref[...] = (dy * h * (sg * (1.0 + g * (1.0 - sg)))).astype(bf16)
        _linear_scan(_shift_up(a, 1), dy * (g * sg), a_s, b_s, edge_s, dh_s, reverse=True)
        dh_total = dh_s[...]
        da = dh_total * _shift_down(h, 1)
        iu = i * u
        dmult = dh_total * iu
        di = dh_total * mult * u
        du = dh_total * mult * i
        dlog_a = a * (da - dmult * a / mult)
        dr = dlog_a * ((-LRU_C) * sp)
        dsp = jnp.sum(dlog_a * ((-LRU_C) * r), axis=0, keepdims=True)
        dpre_r = dr * r * (1.0 - r)
        dpre_i = di * i * (1.0 - i)
        dpre_rb = dpre_r.astype(bf16)
        dpre_ib = dpre_i.astype(bf16)
        du = du + _dot(dpre_rb, wa_ref[...].astype(bf16), NT) + _dot(dpre_ib, wx_ref[...].astype(bf16), NT)
        gwa_ref[...] += _dot(ub, dpre_rb, TN)
        gwx_ref[...] += _dot(ub, dpre_ib, TN)
        vec_ref[0:1, :] += jnp.sum(du, axis=0, keepdims=True)
        vec_ref[1:2, :] += jnp.sum(dpre_r, axis=0, keepdims=True)
        vec_ref[2:3, :] += jnp.sum(dpre_i, axis=0, keepdims=True)
        vec_ref[3:4, :] += dsp * (-_sigmoid(-lam_v))
        dx0 = cw[3:4, :] * du
        gcw_ref[3:4, :] += jnp.sum(du * x0, axis=0, keepdims=True)
        for k in range(3):
            dx0 = dx0 + cw[k:k + 1, :] * _shift_up(du, 3 - k)
            gcw_ref[k:k + 1, :] += jnp.sum(du * _shift_down(x0, 3 - k), axis=0, keepdims=True)
        du0_ref[...] = dx0.astype(bf16)

    wacc = pl.BlockSpec((RB, RB), lambda n, b: (0, n))
    vacc = pl.BlockSpec((8, RB), lambda n, b: (0, n))
    cacc = pl.BlockSpec((8, RB), lambda n, b: (n, 0))
    return _pcall(
        body, name="lru_backward", grid=(RNN_BLOCKS, nb),
        in_specs=[col(0), col(8), tokblk, tokblk, cwblk, vec, wblk, vec, wblk, vec, vec],
        out_specs=(tokblk, tokblk, wacc, wacc, vacc, cacc),
        out_shape=(_sds((T, D), bf16), _sds((T, D), bf16), _sds((RB, D), f32), _sds((RB, D), f32),
                   _sds((8, D), f32), _sds((8 * RNN_BLOCKS, RB), f32)),
        scratch_shapes=[pltpu.VMEM((S, RB), f32)] * 3 + [pltpu.VMEM((S // 8, RB), f32)],
        compiler_params=_params(("arbitrary", "arbitrary")),
    )(proj, proj, h_all, dy_rnn, cw_full, conv_b, w_a, b_a, w_x, b_x, lam)


def _section_of_chunk(s):
    out = []
    for start, n in zip(SEC_START, SEC_CHUNKS):
        inside = (s >= start) & (s < start + n)
        out.append((inside, jnp.clip(s - start, 0, n - 1)))
    return out


def _input_grad(dsecs, wt_full, x2d, dx2, norm_g, chip_sums):
    T = x2d.shape[0]
    tb = min(T, 1024)
    nchunks = D_IN // CH
    nsec = len(dsecs)
    nex = len(chip_sums)
    ntok = T // tb

    def body(*refs):
        secs = refs[:nsec]
        wt_ref, x_ref, dx2_ref, g_ref = refs[nsec:nsec + 4]
        ex_src = refs[nsec + 4:nsec + 4 + nex]
        dx_ref, gnorm_ref = refs[nsec + 4 + nex:nsec + 6 + nex]
        ex_dst = refs[nsec + 6 + nex:nsec + 6 + 2 * nex]
        acc = refs[nsec + 6 + 2 * nex]
        sems = refs[nsec + 7 + 2 * nex:]
        i, s = pl.program_id(0), pl.program_id(1)

        @pl.when((i == 0) & (s == 0))
        def _():
            gnorm_ref[...] = jnp.zeros_like(gnorm_ref)
            _start_all(_chip_exchange_copies(ex_src, ex_dst, *sems))

        @pl.when((i == ntok - 1) & (s == nchunks - 1))
        def _():
            _wait_all(_chip_exchange_copies(ex_src, ex_dst, *sems))

        @pl.when(s == 0)
        def _():
            acc[...] = jnp.zeros_like(acc)

        for a, (start, n) in enumerate(zip(SEC_START, SEC_CHUNKS)):
            @pl.when((s >= start) & (s < start + n))
            def _(a=a):
                acc[...] += _dot(secs[a][...], wt_ref[...], NN)

        @pl.when(s == nchunks - 1)
        def _():
            xv = x_ref[...]
            rstd = lax.rsqrt(jnp.mean(xv * xv, axis=-1, keepdims=True) + EPS)
            xh = xv * rstd
            dh = acc[...]
            gnorm_ref[0:1, :] += jnp.sum(dh * xh, axis=0, keepdims=True)
            dxn = dh * g_ref[...]
            dx_ref[...] = dx2_ref[...] + rstd * (dxn - xh * jnp.mean(dxn * xh, axis=-1, keepdims=True))

    def sec_spec(a):
        return pl.BlockSpec((tb, CH), lambda i, s, a=a: (i, _section_of_chunk(s)[a][1]))

    tok = pl.BlockSpec((tb, D), lambda i, s: (i, 0))
    res = _pcall(
        body, name="input_grad", grid=(ntok, nchunks),
        in_specs=[sec_spec(a) for a in range(nsec)] + [pl.BlockSpec((CH, D), lambda i, s: (s, 0)), tok, tok,
                                                        pl.BlockSpec((1, D), lambda i, s: (0, 0))] + [ANY] * nex,
        out_specs=(tok, pl.BlockSpec((8, D), lambda i, s: (0, 0))) + tuple([ANY] * nex),
        out_shape=(_sds((T, D), f32), _sds((8, D), f32)) + tuple(_sds(c.shape, c.dtype) for c in chip_sums),
        scratch_shapes=[pltpu.VMEM((tb, D), f32)] + _exchange_scratch(nex, 3),
        compiler_params=_params(("arbitrary", "arbitrary")),
    )(*dsecs, wt_full, x2d, dx2, norm_g, *chip_sums)
    return res[0], res[1], res[2:]


def _w_in_grad(dsecs, h_bf):
    T = h_bf.shape[0]
    tk = min(T, 1024)
    nchunks = D_IN // CH
    nsec = len(dsecs)

    def body(*refs):
        secs = refs[:nsec]
        h_ref, out_ref = refs[nsec:]
        s, t = pl.program_id(0), pl.program_id(1)

        @pl.when(t == 0)
        def _():
            out_ref[...] = jnp.zeros_like(out_ref)

        h_rows = h_ref[pl.ds(pl.multiple_of(t * tk, tk), tk), :]
        for a, (start, n) in enumerate(zip(SEC_START, SEC_CHUNKS)):
            @pl.when((s >= start) & (s < start + n))
            def _(a=a):
                out_ref[...] += _dot(secs[a][...], h_rows, TN)

    def sec_spec(a):
        def index(s, t, a=a):
            inside, local = _section_of_chunk(s)[a]
            return (jnp.where(inside, t, 0), local)
        return pl.BlockSpec((tk, CH), index)

    return _pcall(
        body, name="w_in_grad", grid=(nchunks, T // tk),
        in_specs=[sec_spec(a) for a in range(nsec)] + [pl.BlockSpec((T, D), lambda s, t: (0, 0))],
        out_specs=pl.BlockSpec((CH, D), lambda s, t: (s, 0)), out_shape=_sds((D_IN, D), f32),
        compiler_params=_params(("arbitrary", "arbitrary")),
    )(*dsecs, h_bf)


def _pad_rows(v, rows=8):
    return jnp.concatenate([v, jnp.zeros((rows - v.shape[0], v.shape[1]), v.dtype)], axis=0)


def _lanes(v):
    return jnp.pad(v, ((0, 0), (0, D - v.shape[1])))


def _blocks_to_lanes(w):
    return jnp.transpose(w, (1, 0, 2)).reshape(RB, D)


def _lanes_to_blocks(w):
    return jnp.transpose(w.reshape(RB, RNN_BLOCKS, RB), (1, 0, 2))


def _small_pack(w_a, w_x, conv_b, b_a, b_x, lam, norm_g, fin_g, sinks):
    vec = _pad_rows(jnp.concatenate([conv_b, b_a, b_x, lam], axis=0))
    return jnp.concatenate([_blocks_to_lanes(w_a), _blocks_to_lanes(w_x), vec, _pad_rows(norm_g), _pad_rows(fin_g),
                            _pad_rows(_lanes(sinks)), jnp.zeros((32, D), f32)], axis=0)


def kernel(x, norm_g, w_in, conv_w, conv_b, lru_w_a, lru_b_a, lru_w_x, lru_b_x, lru_lambda, attn_sinks, w_rnn_out, w_attn_out, w_o, final_norm_g, loss_target, m_norm_g, m_w_in, m_conv_w, m_conv_b, m_lru_w_a, m_lru_b_a, m_lru_w_x, m_lru_b_x, m_lru_lambda, m_attn_sinks, m_w_rnn_out, m_w_attn_out, m_w_o, m_final_norm_g, v_norm_g, v_w_in, v_conv_w, v_conv_b, v_lru_w_a, v_lru_b_a, v_lru_w_x, v_lru_b_x, v_lru_lambda, v_attn_sinks, v_w_rnn_out, v_w_attn_out, v_w_o, v_final_norm_g):
    nb, S, _ = x.shape
    T = nb * S
    x2d = x.reshape(T, D)
    tgt = loss_target.reshape(T, D)
    fin_g = final_norm_g.reshape(1, D)
    w_a3, w_x3 = lru_w_a[0], lru_w_x[0]

    my_core = lax.axis_index("c").astype(jnp.int32).reshape(1)
    cx, cy = lax.axis_index("x"), lax.axis_index("y")
    chip_order = jnp.stack([2 * cx + cy, 2 * (1 - cx) + cy, 2 * cx + (1 - cy),
                            2 * (1 - cx) + (1 - cy)]).astype(jnp.int32)

    tabs = _rope_tables(S)
    h_bf, proj, wt_full, cw_full, (wr_full, wa_full, wo_full) = _in_proj_gather(
        x2d, norm_g, w_in[0].T.astype(bf16), _pad_rows(conv_w[0]), tabs, S,
        (w_rnn_out[0], w_attn_out[0], w_o[0]), chip_order)
    y_rnn, h_all = _lru_forward(proj, cw_full, conv_b, w_a3, lru_b_a, w_x3, lru_b_x, lru_lambda, S)
    y_attn = _attn_forward(proj, attn_sinks, S)

    (dx2, dy_rnn, dy_attn, dmr, dma, loss_blk, gfin_blk, g_wr, g_wa, g_wo) = _merge_and_head(
        x2d, tgt, proj, y_rnn, y_attn, wr_full, wa_full, wo_full, fin_g)
    sums_out = _pair_sums([g_wr, g_wa, g_wo], [bf16, bf16, bf16], my_core, "out")

    (dq, dkv, dga, dsink_blk), (p_wr, p_wa, p_wo) = _attn_backward(proj, dy_attn, tabs, attn_sinks, S, sums_out)
    du0, dgr, gwa, gwx, gvec, gcw = _lru_backward(proj, h_all, dy_rnn, cw_full, conv_b, w_a3, lru_b_a, w_x3,
                                                  lru_b_x, lru_lambda, S)
    dsecs = (du0, dgr, dq, dkv, dga, dmr, dma)

    g_wt = _w_in_grad(dsecs, h_bf)
    sums_in = _pair_sums([g_wt], [bf16], my_core, "in")
    grad_x2d, gnorm_blk, (p_wt,) = _input_grad(dsecs, wt_full, x2d, dx2, norm_g, sums_in)

    g_small = jnp.concatenate([gwa, gwx, gvec, gnorm_blk, gfin_blk, _pad_rows(_lanes(dsink_blk[0:1, 0:16])),
                               loss_blk, jnp.zeros((24, D), f32)], axis=0)
    p_small, p_cw = _chip_exchange(_pair_sums([g_small, gcw], [f32, f32], my_core, "small"))

    o_wt = _adamw(p_wt, w_in[0].T, m_w_in[0].T, v_w_in[0].T, "adamw_w_in")
    o_wr = _adamw(p_wr, w_rnn_out[0], m_w_rnn_out[0], v_w_rnn_out[0], "adamw_w_rnn_out")
    o_wa = _adamw(p_wa, w_attn_out[0], m_w_attn_out[0], v_w_attn_out[0], "adamw_w_attn_out")
    o_wo = _adamw(p_wo, w_o[0], m_w_o[0], v_w_o[0], "adamw_w_o")
    o_cw = _adamw(p_cw, _pad_rows(conv_w[0]), _pad_rows(m_conv_w[0]), _pad_rows(v_conv_w[0]), "adamw_conv_w")

    zero40 = jnp.zeros((p_small.shape[1], D), f32)
    small_sum = _adamw(p_small, zero40, zero40, zero40, "sum_small")[0]
    g_small_all = _gather_rows(small_sum, "gather_small")
    pack = lambda *t: _small_pack(*t)
    o_small = _adamw(
        g_small_all[None],
        pack(w_a3, w_x3, conv_b, lru_b_a, lru_b_x, lru_lambda, norm_g, fin_g, attn_sinks),
        pack(m_lru_w_a[0], m_lru_w_x[0], m_conv_b, m_lru_b_a, m_lru_b_x, m_lru_lambda, m_norm_g,
             m_final_norm_g.reshape(1, D), m_attn_sinks),
        pack(v_lru_w_a[0], v_lru_w_x[0], v_conv_b, v_lru_b_a, v_lru_b_x, v_lru_lambda, v_norm_g,
             v_final_norm_g.reshape(1, D), v_attn_sinks),
        "adamw_small")

    loss = g_small_all[288, 0] * (0.5 / D)

    def unpack(kind):
        s = o_small[kind]
        return {
            "norm_g": s[264:265], "w_in": o_wt[kind].T[None], "conv_w": o_cw[kind][None, 0:4],
            "conv_b": s[256:257], "lru_w_a": _lanes_to_blocks(s[0:128])[None], "lru_b_a": s[257:258],
            "lru_w_x": _lanes_to_blocks(s[128:256])[None], "lru_b_x": s[258:259], "lru_lambda": s[259:260],
            "attn_sinks": s[280:281, 0:16], "w_rnn_out": o_wr[kind][None], "w_attn_out": o_wa[kind][None],
            "w_o": o_wo[kind][None], "final_norm_g": s[272, :],
        }

    order = ("norm_g", "w_in", "conv_w", "conv_b", "lru_w_a", "lru_b_a", "lru_w_x", "lru_b_x", "lru_lambda",
             "attn_sinks", "w_rnn_out", "w_attn_out", "w_o", "final_norm_g")
    outs = [loss, grad_x2d.reshape(nb, S, D)]
    for kind in range(4):
        d = unpack(kind)
        outs += [d[n] for n in order]
    return tuple(outs)
```

```python
import functools
import math

import jax
import jax.numpy as jnp
from jax import lax
from jax.experimental import pallas as pl
from jax.experimental.pallas import tpu as pltpu

f32 = jnp.float32
bf16 = jnp.bfloat16

D = 1024
D_IN = 6656
NDEV = 8
RNN_BLOCKS = 8
RB = 128
HEAD = 64
KV_HEADS = 4
GROUP = 4
QB = 128
LRU_C = 8.0
EPS = 1e-6
ROPE_DIM = 16
ROPE_THETA = 500000.0
CH = 512
SEC_START = (0, 2, 4, 6, 7, 9, 11)
SEC_CHUNKS = (2, 2, 2, 1, 2, 2, 2)
VMEM_LIMIT = 62 * 1024 * 1024

ADAM_LR, ADAM_B1, ADAM_B2, ADAM_EPS, ADAM_WD, ADAM_STEP = 0.001, 0.9, 0.999, 1e-08, 0.01, 10

MESH = pl.DeviceIdType.MESH
ANY = pl.BlockSpec(memory_space=pl.ANY)
VMEM_SPEC = pl.BlockSpec(memory_space=pltpu.VMEM)
SMEM_SPEC = pl.BlockSpec(memory_space=pltpu.SMEM)


def _pcall(body, **kw):
    return pl.pallas_call(body, **kw)


def _params(sem=None, **kw):
    if sem is not None:
        kw["dimension_semantics"] = sem
    return pltpu.CompilerParams(vmem_limit_bytes=VMEM_LIMIT, **kw)


def _sds(shape, dtype):
    return jax.ShapeDtypeStruct(shape, dtype)


def _dot(a, b, dims):
    return lax.dot_general(a, b, (dims, ((), ())), preferred_element_type=f32)


NN = ((1,), (0,))
NT = ((1,), (1,))
TN = ((0,), (0,))


def _sigmoid(v):
    return 0.5 * jnp.tanh(0.5 * v) + 0.5


def _sigmoid_positive(v):
    return 1.0 / (1.0 + jnp.exp(-v))


def _my_place():
    return lax.axis_index("x"), lax.axis_index("y"), lax.axis_index("c")


def _peer(k):
    x, y, c = _my_place()
    return (x + ((k >> 2) & 1)) % 2, (y + ((k >> 1) & 1)) % 2, (c + (k & 1)) % 2


def _direct_gather_copies(srcs, outs, send_sems, recv_sems, local_sems):
    x, y, c = _my_place()
    me = 4 * x + 2 * y + c
    local, remote = [], []
    for a, (src, out) in enumerate(zip(srcs, outs)):
        r = src.shape[0]
        mine = out.at[pl.ds(pl.multiple_of(me * r, 8), r), :]
        local.append(pltpu.make_async_copy(src, mine, local_sems.at[a]))
        for k in range(1, NDEV):
            remote.append(pltpu.make_async_remote_copy(
                src_ref=src, dst_ref=mine, send_sem=send_sems.at[7 * a + k - 1], recv_sem=recv_sems.at[7 * a + k - 1],
                device_id=_peer(k), device_id_type=MESH))
    return local, remote


def _chip_exchange_copies(src, dst, send_sems, recv_sems, local_sems):
    x, y, c = _my_place()
    local, remote = [], []
    for a in range(len(src)):
        local.append(pltpu.make_async_copy(src[a].at[2 * x + y], dst[a].at[0], local_sems.at[a]))
    for k in (3, 1, 2):
        px, py = (x + (k >> 1)) % 2, (y + (k & 1)) % 2
        for a in range(len(src)):
            remote.append(pltpu.make_async_remote_copy(
                src_ref=src[a].at[2 * px + py], dst_ref=dst[a].at[k],
                send_sem=send_sems.at[3 * a + k - 1], recv_sem=recv_sems.at[3 * a + k - 1],
                device_id=(px, py, c), device_id_type=MESH))
    return local, remote


def _exchange_scratch(narr, per_array):
    return [pltpu.SemaphoreType.DMA((per_array * narr,)), pltpu.SemaphoreType.DMA((per_array * narr,)),
            pltpu.SemaphoreType.DMA((narr,))]


def _start_all(copies):
    local, remote = copies
    for cp in local + remote:
        cp.start()


def _wait_all(copies):
    local, remote = copies
    for cp in remote + local:
        cp.wait()


def _pair_exchange(grads, name):
    narr = len(grads)
    nrows = tuple(g.shape[0] // NDEV for g in grads)
    views = [g.reshape(4, 2, r, g.shape[1]) for g, r in zip(grads, nrows)]

    def body(*refs):
        gin = refs[:narr]
        got = refs[narr:2 * narr]
        send_sems, recv_sems = refs[2 * narr:]
        x, y, c = _my_place()
        copies = [pltpu.make_async_remote_copy(
            src_ref=gin[a].at[:, pl.ds(1 - c, 1)], dst_ref=got[a],
            send_sem=send_sems.at[a], recv_sem=recv_sems.at[a],
            device_id=(x, y, 1 - c), device_id_type=MESH) for a in range(narr)]
        for cp in copies:
            cp.start()
        for cp in copies:
            cp.wait()

    out_shape = tuple(_sds((4, 1, r, g.shape[1]), g.dtype) for r, g in zip(nrows, grads))
    got = _pcall(
        body, name=name, out_shape=out_shape,
        in_specs=[ANY] * narr, out_specs=tuple([ANY] * narr),
        scratch_shapes=[pltpu.SemaphoreType.DMA((narr,)), pltpu.SemaphoreType.DMA((narr,))],
        compiler_params=_params(),
    )(*views)
    return views, [g.reshape(4, r, g.shape[3]) for g, r in zip(got, nrows)]


def _row_tile(rows, dtype):
    unit = 16 if dtype == bf16 else 8
    for cand in (256, 208, 128, 64, 40, 32, 16, 8):
        if rows % cand == 0 and cand % unit == 0:
            return cand
    return rows


def _chip_sum(view, got, my_core, out_dtype, name):
    _, _, r, cols = view.shape
    tr = _row_tile(r, out_dtype)

    def body(core_ref, mine_ref, got_ref, out_ref):
        out_ref[...] = (mine_ref[...].astype(f32) + got_ref[...].astype(f32)).astype(out_dtype)

    grid_spec = pltpu.PrefetchScalarGridSpec(
        num_scalar_prefetch=1, grid=(4, r // tr),
        in_specs=[pl.BlockSpec((None, None, tr, cols), lambda q, i, core: (q, core[0], i, 0)),
                  pl.BlockSpec((None, tr, cols), lambda q, i, core: (q, i, 0))],
        out_specs=pl.BlockSpec((None, tr, cols), lambda q, i, core: (q, i, 0)))
    return _pcall(body, name=name, grid_spec=grid_spec, out_shape=_sds((4, r, cols), out_dtype),
                  compiler_params=_params(("arbitrary", "arbitrary")))(my_core, view, got)


def _pair_sums(grads, wire_dtypes, my_core, tag):
    views, got = _pair_exchange(grads, "pair_exchange_" + tag)
    return [_chip_sum(v, g, my_core, dt, "chip_sum_%s%d" % (tag, a))
            for a, (v, g, dt) in enumerate(zip(views, got, wire_dtypes))]


def _adam_math(g, w, m, v):
    m_new = ADAM_B1 * m + (1.0 - ADAM_B1) * g
    v_new = ADAM_B2 * v + (1.0 - ADAM_B2) * (g * g)
    m_hat = m_new / (1.0 - ADAM_B1 ** ADAM_STEP)
    v_hat = v_new / (1.0 - ADAM_B2 ** ADAM_STEP)
    return -ADAM_LR * (m_hat / (jnp.sqrt(v_hat) + ADAM_EPS) + ADAM_WD * w), m_new, v_new


def _adamw(parts, w, m, v, name):
    n, rows, cols = parts.shape
    tr = _row_tile(rows, parts.dtype)

    def body(p_ref, w_ref, m_ref, v_ref, g_out, d_out, m_out, v_out):
        g = p_ref[0].astype(f32)
        for s in range(1, n):
            g = g + p_ref[s].astype(f32)
        g_out[...] = g
        d_out[...], m_out[...], v_out[...] = _adam_math(g, w_ref[...], m_ref[...], v_ref[...])

    blk = pl.BlockSpec((tr, cols), lambda i: (i, 0))
    return _pcall(
        body, name=name, grid=(rows // tr,),
        in_specs=[pl.BlockSpec((n, tr, cols), lambda i: (0, i, 0)), blk, blk, blk],
        out_specs=(blk, blk, blk, blk), out_shape=tuple(_sds((rows, cols), f32) for _ in range(4)),
        compiler_params=_params(("arbitrary",)),
    )(parts, w, m, v)


def _rope(t, c, s1, s2):
    w = t.shape[1]
    return t * c + pltpu.roll(t, w - 8, 1) * s1 + pltpu.roll(t, 8, 1) * s2


def _rope_transposed(dt, c, s1, s2):
    w = dt.shape[1]
    return dt * c + pltpu.roll(dt * s1, 8, 1) + pltpu.roll(dt * s2, w - 8, 1)


PAIR_ROWS = D_IN // 4
SUB_COLS = ((0, 512), (512, 512), (1024, 512), (1536, 128))
Q_SLABS = range(3, 11)
K_SLABS = range(11, 13)


def _in_proj_gather(x2d, norm_g, wt_shard, cw_shard, tabs, S, out_shards, chip_order):
    T = x2d.shape[0]
    tb = min(S, 1024)
    ntok = T // tb
    nsb = S // tb
    q_scale = 1.0 / math.sqrt(HEAD)
    shard_rows = wt_shard.shape[0]
    small = (cw_shard,) + tuple(out_shards)
    nsm = len(small)

    def body(order_ref, x_ref, g_ref, c_ref, s1_ref, s2_ref, wt_hbm, *rest):
        small_in = rest[:nsm]
        h_ref, proj_ref, wt_out = rest[nsm:nsm + 3]
        small_out = rest[nsm + 3:2 * nsm + 3]
        wt_vm, h_vm = rest[2 * nsm + 3:2 * nsm + 5]
        stage = rest[2 * nsm + 5:3 * nsm + 4]
        wsend, wrecv, wlocal = rest[3 * nsm + 4:3 * nsm + 7]
        dsems = rest[3 * nsm + 7:]
        jj, i = pl.program_id(0), pl.program_id(1)
        x, y, c = _my_place()
        me, sibling = (x, y, c), (x, y, 1 - c)
        chips = [(1 - x, y), (x, 1 - y), (1 - x, 1 - y)]

        def rows(place):
            px, py, pc = place
            return wt_vm.at[pl.ds(pl.multiple_of((4 * px + 2 * py + pc) * shard_rows, 16), shard_rows), :]

        def copy(k, block, to, src=None):
            return pltpu.make_async_remote_copy(
                src_ref=rows(block) if src is None else src, dst_ref=rows(block),
                send_sem=wsend.at[k], recv_sem=wrecv.at[k], device_id=to, device_id_type=MESH)

        def small_copies():
            srcs = (small_in[0],) + tuple(stage)
            return _direct_gather_copies(srcs, small_out, *dsems)

        own = pltpu.make_async_copy(wt_hbm, rows(me), wlocal.at[0])
        keep = pltpu.make_async_copy(wt_vm, wt_out, wlocal.at[1])

        @pl.when((jj == 0) & (i == 0))
        def _():
            own.start()
            copy(0, me, sibling, src=wt_hbm).start()
            for j, chip in enumerate(chips):
                copy(1 + j, me, (*chip, c), src=wt_hbm).start()
            for a in range(nsm - 1):
                stage[a][...] = small_in[1 + a][...].astype(bf16)
            _start_all(small_copies())
            own.wait()
            copy(0, sibling, me).wait_recv()

        for j, chip in enumerate(chips):
            @pl.when((jj == 1 + j) & (i == 0))
            def _(j=j, chip=chip):
                copy(1 + j, (*chip, c), me).wait_recv()
                copy(4 + j, (*chip, c), sibling).start()
                copy(4 + j, (*chip, 1 - c), me).wait_recv()

        @pl.when((jj == 3) & (i == 0))
        def _():
            keep.start()

        @pl.when((jj == 3) & (i == ntok - 1))
        def _():
            copy(0, me, sibling, src=wt_hbm).wait_send()
            for j, chip in enumerate(chips):
                copy(1 + j, me, (*chip, c), src=wt_hbm).wait_send()
                copy(4 + j, (*chip, c), sibling).wait_send()
            _wait_all(small_copies())
            keep.wait()

        tok = pl.ds(pl.multiple_of(i * tb, tb), tb)

        @pl.when(jj == 0)
        def _():
            xv = x_ref[...]
            ms = jnp.mean(xv * xv, axis=-1, keepdims=True)
            hb = (xv * lax.rsqrt(ms + EPS) * g_ref[...]).astype(bf16)
            h_ref[...] = hb
            h_vm[tok, :] = hb

        block = order_ref[jj]
        hb = h_vm[tok, :]

        def piece(c0, w):
            w_rows = wt_vm[pl.ds(pl.multiple_of(block * PAIR_ROWS + c0, 128), w), :]
            return _dot(hb, w_rows, NT)

        @pl.when(block != 1)
        def _():
            for c0, w in SUB_COLS:
                proj_ref[:, c0:c0 + w] = piece(c0, w).astype(bf16)

        @pl.when(block == 1)
        def _():
            tab = (c_ref[...], s1_ref[...], s2_ref[...])
            for c0, w in SUB_COLS:
                acc = piece(c0, w)
                for l in range(w // 128):
                    slab = (c0 + 128 * l) // 128
                    part = acc[:, 128 * l:128 * (l + 1)]
                    if slab in Q_SLABS:
                        part = _rope(part, *tab) * q_scale
                    elif slab in K_SLABS:
                        part = _rope(part, *tab)
                    proj_ref[:, 128 * slab:128 * (slab + 1)] = part.astype(bf16)

    first_pass = lambda jj, i, order: (jnp.where(jj == 0, i, ntok - 1), 0)
    const = lambda jj, i, order: (0, 0)
    tab = pl.BlockSpec((tb, 128), lambda jj, i, order: (jnp.where(order[jj] == 1, i % nsb, 0), 0))
    grid_spec = pltpu.PrefetchScalarGridSpec(
        num_scalar_prefetch=1, grid=(4, ntok),
        in_specs=[pl.BlockSpec((tb, D), first_pass), pl.BlockSpec((1, D), const), tab, tab, tab, ANY]
        + [pl.BlockSpec(w.shape, const) for w in small],
        out_specs=(pl.BlockSpec((tb, D), first_pass),
                   pl.BlockSpec((tb, PAIR_ROWS), lambda jj, i, order: (i, order[jj])), ANY) + tuple([ANY] * nsm),
        scratch_shapes=[pltpu.VMEM((D_IN, D), bf16), pltpu.VMEM((T, D), bf16)]
        + [pltpu.VMEM(w.shape, bf16) for w in out_shards]
        + [pltpu.SemaphoreType.DMA((7,)), pltpu.SemaphoreType.DMA((7,)), pltpu.SemaphoreType.DMA((2,))]
        + _exchange_scratch(nsm, 7))
    res = _pcall(
        body, name="in_proj", grid_spec=grid_spec,
        out_shape=(_sds((T, D), bf16), _sds((T, D_IN), bf16), _sds((D_IN, D), bf16),
                   _sds((NDEV * cw_shard.shape[0], cw_shard.shape[1]), f32))
        + tuple(_sds((NDEV * w.shape[0], w.shape[1]), bf16) for w in out_shards),
        compiler_params=_params(("arbitrary", "arbitrary")),
    )(chip_order, x2d, norm_g, *tabs, wt_shard, *small)
    return res[0], res[1], res[2], res[3], res[4:]


def _rows_iota(shape):
    return lax.broadcasted_iota(jnp.int32, shape, 0)


def _shift_down(v, k):
    return jnp.where(_rows_iota(v.shape) >= k, pltpu.roll(v, k, 0), 0.0)


def _shift_up(v, k):
    n = v.shape[0]
    return jnp.where(_rows_iota(v.shape) < n - k, pltpu.roll(v, n - k, 0), 0.0)


def _linear_scan(a, b, a_s, b_s, edge_s, out_ref, reverse):
    n = a.shape[0]
    ng = n // 8
    a3, b3 = a.reshape(ng, 8, RB), b.reshape(ng, 8, RB)
    rid = lax.broadcasted_iota(jnp.int32, a3.shape, 1)
    for s in (1, 2, 4):
        keep, shift = (rid < 8 - s, 8 - s) if reverse else (rid >= s, s)
        b3 = jnp.where(keep, a3 * pltpu.roll(b3, shift, 1) + b3, b3)
        a3 = jnp.where(keep, a3 * pltpu.roll(a3, shift, 1), a3)
    a_s[...] = a3.reshape(n, RB)
    b_s[...] = b3.reshape(n, RB)
    edge = 0 if reverse else 7
    ea, eb = a_s[pl.ds(edge, ng, stride=8), :], b_s[pl.ds(edge, ng, stride=8), :]
    r = _rows_iota(ea.shape)
    s = 1
    while s < ng:
        keep, shift = (r < ng - s, ng - s) if reverse else (r >= s, s)
        eb = jnp.where(keep, ea * pltpu.roll(eb, shift, 0) + eb, eb)
        if 2 * s < ng:
            ea = jnp.where(keep, ea * pltpu.roll(ea, shift, 0), ea)
        s *= 2
    edge_s[...] = _shift_up(eb, 1) if reverse else _shift_down(eb, 1)

    def eight_groups(i, carry):
        for k in range(8):
            j = i * 8 + k
            rows = pl.ds(pl.multiple_of(j * 8, 8), 8)
            out_ref[rows, :] = b_s[rows, :] + a_s[rows, :] * edge_s[pl.ds(j, 1), :]
        return carry

    lax.fori_loop(0, ng // 8, eight_groups, 0)


def _neg_expm1(v):
    series = -v * (1.0 + v * (0.5 + v * (1.0 / 6.0)))
    return jnp.where(v > -0.015625, series, 1.0 - jnp.exp(v))


def _softplus_neg(lam):
    return jnp.maximum(-lam, 0.0) + jnp.log(1.0 + jnp.exp(-jnp.abs(lam)))


def _lru_gates(x0, cw, cb, wa, ba, wx, bx, lam):
    taps = [_shift_down(x0, 3 - k) for k in range(3)] + [x0]
    u = cb + cw[3:4, :] * x0
    for k in range(3):
        u = u + cw[k:k + 1, :] * taps[k]
    ub = u.astype(bf16)
    r = _sigmoid_positive(_dot(ub, wa.astype(bf16), NN) + ba)
    i = _sigmoid(_dot(ub, wx.astype(bf16), NN) + bx)
    sp = _softplus_neg(lam)
    log_a = (-LRU_C) * r * sp
    a = jnp.exp(log_a)
    w = _neg_expm1(2.0 * log_a)
    inv_mult = lax.rsqrt(w)
    return u, ub, r, i, sp, a, w * inv_mult, inv_mult, taps


def _lru_specs(S, nb):
    col = lambda off: pl.BlockSpec((S, RB), lambda n, b, off=off: (b, off + n))
    vec = pl.BlockSpec((1, RB), lambda n, b: (0, n))
    wblk = pl.BlockSpec((None, RB, RB), lambda n, b: (n, 0, 0))
    cwblk = pl.BlockSpec((8, RB), lambda n, b: (n, 0))
    return col, vec, wblk, cwblk


def _lru_forward(proj, cw_full, conv_b, w_a, b_a, w_x, b_x, lam, S):
    T = proj.shape[0]
    nb = T // S
    col, vec, wblk, cwblk = _lru_specs(S, nb)

    def body(x0_ref, g_ref, cw_ref, cb_ref, wa_ref, ba_ref, wx_ref, bx_ref, lam_ref, y_ref, h_ref, a_s, b_s, edge_s):
        x0 = x0_ref[...].astype(f32)
        u, ub, r, i, sp, a, mult, _, _ = _lru_gates(x0, cw_ref[...], cb_ref[...], wa_ref[...], ba_ref[...],
                                                    wx_ref[...], bx_ref[...], lam_ref[...])
        _linear_scan(a, mult * (i * u), a_s, b_s, edge_s, h_ref, reverse=False)
        g = g_ref[...].astype(f32)
        y_ref[...] = (h_ref[...] * (g * _sigmoid(g))).astype(bf16)

    out = pl.BlockSpec((S, RB), lambda n, b: (b, n))
    return _pcall(
        body, name="lru_forward", grid=(RNN_BLOCKS, nb),
        in_specs=[col(0), col(8), cwblk, vec, wblk, vec, wblk, vec, vec],
        out_specs=(out, out), out_shape=(_sds((T, D), bf16), _sds((T, D), f32)),
        scratch_shapes=[pltpu.VMEM((S, RB), f32), pltpu.VMEM((S, RB), f32), pltpu.VMEM((S // 8, RB), f32)],
        compiler_params=_params(("arbitrary", "arbitrary")),
    )(proj, proj, cw_full, conv_b, w_a, b_a, w_x, b_x, lam)


def _rope_tables(S):
    pos = jnp.arange(S, dtype=f32)
    inv_freq = ROPE_THETA ** (-jnp.arange(0, ROPE_DIM, 2, dtype=f32) / ROPE_DIM)
    ang = pos[:, None] * inv_freq[None, :]
    cos, sin = jnp.cos(ang), jnp.sin(ang)
    lane = jnp.arange(128) % HEAD
    cosl, sinl = cos[:, lane % 8], sin[:, lane % 8]
    c = jnp.where(lane[None, :] < ROPE_DIM, cosl, 1.0)
    s1 = jnp.where(lane[None, :] < 8, -sinl, 0.0)
    s2 = jnp.where((lane[None, :] >= 8) & (lane[None, :] < ROPE_DIM), sinl, 0.0)
    return c.astype(f32), s1.astype(f32), s2.astype(f32)


def _heads_to_rows(t):
    return jnp.concatenate([t[:, HEAD * h:HEAD * (h + 1)] for h in range(GROUP)], axis=0)


def _rows_to_heads(t):
    return jnp.concatenate([t[QB * h:QB * (h + 1), :] for h in range(GROUP)], axis=1)


def _window_bias(first_block):
    shape = (GROUP * QB, 2 * QB)
    qi = _rows_iota(shape) % QB
    cj = lax.broadcasted_iota(jnp.int32, shape, 1)
    valid = (cj > qi) & (cj <= qi + QB) & ((cj >= QB) | jnp.logical_not(first_block))
    return jnp.where(valid, 0.0, -jnp.inf)


def _attn_probs(q_rows, k_cat, sink_col, bias):
    s = _dot(q_rows, k_cat, NT) + bias
    m = jnp.maximum(jnp.max(s, axis=1, keepdims=True), sink_col)
    p = jnp.exp(s - m)
    e_sink = jnp.exp(sink_col - m)
    inv = 1.0 / (jnp.sum(p, axis=1, keepdims=True) + e_sink)
    return p * inv, e_sink * inv


def _sink_column(sink_ref, kv):
    rid = _rows_iota((GROUP * QB, 1))
    col = jnp.zeros((GROUP * QB, 1), f32)
    for h in range(GROUP):
        col = jnp.where(rid // QB == h, sink_ref[0, GROUP * kv + h], col)
    return col


def _attn_in_specs(S):
    nq = S // QB
    last = nq - 1
    cur = lambda b, j: b * nq + jnp.minimum(j, last)
    prev = lambda b, j: b * nq + jnp.maximum(jnp.minimum(j, last) - 1, 0)
    specs = [
        pl.BlockSpec((QB, D), lambda b, j: (cur(b, j), 2)),
        pl.BlockSpec((QB, 256), lambda b, j: (cur(b, j), 12)),
        pl.BlockSpec((QB, 256), lambda b, j: (prev(b, j), 12)),
        pl.BlockSpec((QB, 256), lambda b, j: (cur(b, j), 13)),
        pl.BlockSpec((QB, 256), lambda b, j: (prev(b, j), 13)),
        pl.BlockSpec((QB, 512), lambda b, j: (cur(b, j), 7)),
        pl.BlockSpec((QB, 512), lambda b, j: (cur(b, j), 8)),
        SMEM_SPEC,
    ]
    return specs, cur, prev


def _attn_forward(proj, sinks, S):
    T = proj.shape[0]
    nb, nq = T // S, S // QB
    specs, cur, _ = _attn_in_specs(S)

    def body(q_ref, kc_ref, kp_ref, vc_ref, vp_ref, gl_ref, gh_ref, sink_ref, y_ref):
        bias = _window_bias(pl.program_id(1) == 0)
        kc, kp, vc, vp = kc_ref[...], kp_ref[...], vc_ref[...], vp_ref[...]
        for kv in range(KV_HEADS):
            lanes = slice(256 * kv, 256 * (kv + 1))
            hl = slice(HEAD * kv, HEAD * (kv + 1))
            q_rows = _heads_to_rows(q_ref[:, lanes])
            k_cat = jnp.concatenate([kp[:, hl], kc[:, hl]], axis=0)
            v_cat = jnp.concatenate([vp[:, hl], vc[:, hl]], axis=0)
            probs, _ = _attn_probs(q_rows, k_cat, _sink_column(sink_ref, kv), bias)
            o = _rows_to_heads(_dot(probs.astype(bf16), v_cat, NN))
            g_src = gl_ref if kv < 2 else gh_ref
            g = g_src[:, 256 * (kv % 2):256 * (kv % 2 + 1)].astype(f32)
            y_ref[:, lanes] = (o * (g * _sigmoid(g))).astype(bf16)

    args = [proj] * 7 + [sinks]
    return _pcall(
        body, name="attn_forward", grid=(nb, nq), in_specs=specs,
        out_specs=pl.BlockSpec((QB, D), lambda b, j: (cur(b, j), 0)), out_shape=_sds((T, D), bf16),
        compiler_params=_params(("arbitrary", "arbitrary")),
    )(*args)


def _merge_and_head(x2d, tgt, proj, y_rnn, y_attn, w_r, w_a, w_o, gfin):
    T = x2d.shape[0]
    tb = min(T, 512)
    nsteps = T // tb

    def body(x_ref, t_ref, mr0, mr1, ma0, ma1, yr_ref, ya_ref, wr_ref, wa_ref, wo_ref, gf_ref,
             dx2_ref, dyr_ref, dya_ref, dmr_ref, dma_ref, loss_ref, gfin_ref, gwr_out, gwa_out, gwo_out,
             gwr_acc, gwa_acc, gwo_acc, out_sems):
        step = pl.program_id(0)

        @pl.when(step == 0)
        def _():
            loss_ref[...] = jnp.zeros_like(loss_ref)
            gfin_ref[...] = jnp.zeros_like(gfin_ref)
            gwr_acc[...] = jnp.zeros_like(gwr_acc)
            gwa_acc[...] = jnp.zeros_like(gwa_acc)
            gwo_acc[...] = jnp.zeros_like(gwo_acc)

        sr = _sigmoid(jnp.concatenate([mr0[...], mr1[...]], axis=1).astype(f32))
        sa = _sigmoid(jnp.concatenate([ma0[...], ma1[...]], axis=1).astype(f32))
        p_r = _dot(yr_ref[...], wr_ref[...], NN)
        p_a = _dot(ya_ref[...], wa_ref[...], NN)
        merged = (sr * p_r + sa * p_a).astype(bf16)
        x2 = x_ref[...] + _dot(merged, wo_ref[...], NN)
        rstd = lax.rsqrt(jnp.mean(x2 * x2, axis=-1, keepdims=True) + EPS)
        xh = x2 * rstd
        gf = gf_ref[...]
        err = xh * gf - t_ref[...]
        loss_ref[...] += jnp.sum(err * err)
        dy = err * (1.0 / D)
        gfin_ref[0:1, :] += jnp.sum(dy * xh, axis=0, keepdims=True)
        dxn = dy * gf
        dx2 = rstd * (dxn - xh * jnp.mean(dxn * xh, axis=-1, keepdims=True))
        dx2_ref[...] = dx2
        dx2b = dx2.astype(bf16)
        dmerged = _dot(dx2b, wo_ref[...], NT)
        dmr_ref[...] = (dmerged * p_r * (sr * (1.0 - sr))).astype(bf16)
        dma_ref[...] = (dmerged * p_a * (sa * (1.0 - sa))).astype(bf16)
        dpr = (dmerged * sr).astype(bf16)
        dpa = (dmerged * sa).astype(bf16)
        dyr_ref[...] = _dot(dpr, wr_ref[...], NT).astype(bf16)
        dya_ref[...] = _dot(dpa, wa_ref[...], NT).astype(bf16)
        gwr_acc[...] += _dot(yr_ref[...], dpr, TN)
        gwa_acc[...] += _dot(ya_ref[...], dpa, TN)
        gwo_acc[...] += _dot(merged, dx2b, TN)

        @pl.when(step == nsteps - 1)
        def _():
            copies = [pltpu.make_async_copy(src, dst, out_sems.at[k]) for k, (src, dst) in enumerate(
                ((gwr_acc, gwr_out), (gwa_acc, gwa_out), (gwo_acc, gwo_out)))]
            for cp in copies:
                cp.start()
            for cp in copies:
                cp.wait()

    tok = pl.BlockSpec((tb, D), lambda i: (i, 0))
    half = lambda c: pl.BlockSpec((tb, CH), lambda i, c=c: (i, c))
    wfull = pl.BlockSpec((D, D), lambda i: (0, 0), pipeline_mode=pl.Buffered(1))
    acc = pl.BlockSpec((8, D), lambda i: (0, 0))
    return _pcall(
        body, name="merge_and_head", grid=(nsteps,),
        in_specs=[tok, tok, half(9), half(10), half(11), half(12), tok, tok, wfull, wfull, wfull,
                  pl.BlockSpec((1, D), lambda i: (0, 0))],
        out_specs=(tok, tok, tok, tok, tok, acc, acc, ANY, ANY, ANY),
        out_shape=(_sds((T, D), f32), _sds((T, D), bf16), _sds((T, D), bf16), _sds((T, D), bf16),
                   _sds((T, D), bf16), _sds((8, D), f32), _sds((8, D), f32),
                   _sds((D, D), f32), _sds((D, D), f32), _sds((D, D), f32)),
        scratch_shapes=[pltpu.VMEM((D, D), f32)] * 3 + [pltpu.SemaphoreType.DMA((3,))],
        compiler_params=_params(("arbitrary",)),
    )(x2d, tgt, proj, proj, proj, proj, y_rnn, y_attn, w_r, w_a, w_o, gfin)


def _attn_backward(proj, dy_attn, tabs, sinks, S, chip_sums):
    T = proj.shape[0]
    nb, nq = T // S, S // QB
    nex = len(chip_sums)
    specs, cur, prev = _attn_in_specs(S)
    last = nq - 1
    tab_cur = pl.BlockSpec((QB, 128), lambda b, j: (jnp.minimum(j, last), 0))
    tab_prev = pl.BlockSpec((QB, 128), lambda b, j: (jnp.maximum(jnp.minimum(j, last) - 1, 0), 0))
    specs = specs + [pl.BlockSpec((QB, D), lambda b, j: (cur(b, j), 0))] + [tab_cur] * 3 + [tab_prev] * 3
    q_scale = 1.0 / math.sqrt(HEAD)

    def rope_back(dt, tab):
        return jnp.concatenate([_rope_transposed(dt[:, 128 * l:128 * (l + 1)], *tab) for l in range(2)], axis=1)

    def body(q_ref, kc_ref, kp_ref, vc_ref, vp_ref, gl_ref, gh_ref, sink_ref, dy_ref, cc, s1c, s2c, cp, s1p, s2p,
             *rest):
        ex_src = rest[:nex]
        dq_ref, dkv_ref, dg_ref, dsink_ref = rest[nex:nex + 4]
        ex_dst = rest[nex + 4:2 * nex + 4]
        carry_k, carry_v = rest[2 * nex + 4:2 * nex + 6]
        sems = rest[2 * nex + 6:]
        b, j = pl.program_id(0), pl.program_id(1)

        @pl.when((b == 0) & (j == 0))
        def _():
            dsink_ref[...] = jnp.zeros_like(dsink_ref)
            _start_all(_chip_exchange_copies(ex_src, ex_dst, *sems))

        @pl.when((b == nb - 1) & (j == nq))
        def _():
            _wait_all(_chip_exchange_copies(ex_src, ex_dst, *sems))

        @pl.when(j == 0)
        def _():
            carry_k[...] = jnp.zeros_like(carry_k)
            carry_v[...] = jnp.zeros_like(carry_v)

        @pl.when(j < nq)
        def _():
            bias = _window_bias(j == 0)
            tc = (cc[...], s1c[...], s2c[...])
            tp = (cp[...], s1p[...], s2p[...])
            kc, kp, vc, vp = kc_ref[...], kp_ref[...], vc_ref[...], vp_ref[...]
            dk_prev, dk_cur, dv_prev, dv_cur = [], [], [], []
            dsink_acc = jnp.zeros((8, 128), f32)
            r8 = lax.broadcasted_iota(jnp.int32, (8, 128), 0)
            l8 = lax.broadcasted_iota(jnp.int32, (8, 128), 1)
            for kv in range(KV_HEADS):
                lanes = slice(256 * kv, 256 * (kv + 1))
                hl = slice(HEAD * kv, HEAD * (kv + 1))
                q_rows = _heads_to_rows(q_ref[:, lanes])
                k_cat = jnp.concatenate([kp[:, hl], kc[:, hl]], axis=0)
                v_cat = jnp.concatenate([vp[:, hl], vc[:, hl]], axis=0)
                probs, p_sink = _attn_probs(q_rows, k_cat, _sink_column(sink_ref, kv), bias)
                pb = probs.astype(bf16)
                o = _rows_to_heads(_dot(pb, v_cat, NN))
                g_src = gl_ref if kv < 2 else gh_ref
                g = g_src[:, 256 * (kv % 2):256 * (kv % 2 + 1)].astype(f32)
                sg = _sigmoid(g)
                dy = dy_ref[:, lanes].astype(f32)
                dg_ref[:, lanes] = (dy * o * (sg * (1.0 + g * (1.0 - sg)))).astype(bf16)
                do_rows = _heads_to_rows(dy * (g * sg)).astype(bf16)
                dv = _dot(pb, do_rows, TN)
                dp = _dot(do_rows, v_cat, NT)
                rowdot = jnp.sum(probs * dp, axis=1, keepdims=True)
                ds = (probs * (dp - rowdot)).astype(bf16)
                sink_rows = -(p_sink * rowdot)
                for h in range(GROUP):
                    val = jnp.sum(sink_rows[QB * h:QB * (h + 1), :])
                    dsink_acc = dsink_acc + jnp.where((r8 == 0) & (l8 == GROUP * kv + h), val, 0.0)
                dq = _rows_to_heads(_dot(ds, k_cat, NN)) * q_scale
                dq_ref[:, lanes] = rope_back(dq, tc).astype(bf16)
                dk = _dot(ds, q_rows, TN)
                dk_prev.append(dk[:QB, :])
                dk_cur.append(dk[QB:, :])
                dv_prev.append(dv[:QB, :])
                dv_cur.append(dv[QB:, :])
            dsink_ref[...] += dsink_acc
            dkp = rope_back(jnp.concatenate(dk_prev, axis=1), tp)
            dkc = rope_back(jnp.concatenate(dk_cur, axis=1), tc)
            dkv_ref[:, 0:256] = (carry_k[...] + dkp).astype(bf16)
            dkv_ref[:, 256:512] = (carry_v[...] + jnp.concatenate(dv_prev, axis=1)).astype(bf16)
            carry_k[...] = dkc
            carry_v[...] = jnp.concatenate(dv_cur, axis=1)

        @pl.when(j == nq)
        def _():
            dkv_ref[:, 0:256] = carry_k[...].astype(bf16)
            dkv_ref[:, 256:512] = carry_v[...].astype(bf16)

    lag = lambda b, j: (b * nq + jnp.maximum(j - 1, 0), 0)
    args = [proj] * 7 + [sinks, dy_attn] + list(tabs) + list(tabs) + list(chip_sums)
    res = _pcall(
        body, name="attn_backward", grid=(nb, nq + 1), in_specs=specs + [ANY] * nex,
        out_specs=(pl.BlockSpec((QB, D), lambda b, j: (cur(b, j), 0)), pl.BlockSpec((QB, 512), lag),
                   pl.BlockSpec((QB, D), lambda b, j: (cur(b, j), 0)), pl.BlockSpec((8, 128), lambda b, j: (0, 0)))
        + tuple([ANY] * nex),
        out_shape=(_sds((T, D), bf16), _sds((T, 512), bf16), _sds((T, D), bf16), _sds((8, 128), f32))
        + tuple(_sds(s.shape, s.dtype) for s in chip_sums),
        scratch_shapes=[pltpu.VMEM((QB, 256), f32), pltpu.VMEM((QB, 256), f32)] + _exchange_scratch(nex, 3),
        compiler_params=_params(("arbitrary", "arbitrary")),
    )(*args)
    return res[:4], res[4:]


def _lru_backward(proj, h_all, dy_rnn, cw_full, conv_b, w_a, b_a, w_x, b_x, lam, S):
    T = proj.shape[0]
    nb = T // S
    col, vec, wblk, cwblk = _lru_specs(S, nb)
    tokblk = pl.BlockSpec((S, RB), lambda n, b: (b, n))

    def body(x0_ref, g_ref, h_ref, dy_ref, cw_ref, cb_ref, wa_ref, ba_ref, wx_ref, bx_ref, lam_ref,
             du0_ref, dg_ref, gwa_ref, gwx_ref, vec_ref, gcw_ref, a_s, b_s, dh_s, edge_s):
        @pl.when(pl.program_id(1) == 0)
        def _():
            gwa_ref[...] = jnp.zeros_like(gwa_ref)
            gwx_ref[...] = jnp.zeros_like(gwx_ref)
            vec_ref[...] = jnp.zeros_like(vec_ref)
            gcw_ref[...] = jnp.zeros_like(gcw_ref)

        x0 = x0_ref[...].astype(f32)
        cw = cw_ref[...]
        lam_v = lam_ref[...]
        u, ub, r, i, sp, a, mult, inv_mult, taps = _lru_gates(x0, cw, cb_ref[...], wa_ref[...], ba_ref[...],
                                                              wx_ref[...], bx_ref[...], lam_v)
        h = h_ref[...]
        g = g_ref[...].astype(f32)
        dy = dy_ref[...].astype(f32)
        sg = _sigmoid(g)
        dg_ref[...] = (dy * h * (sg * (1.0 + g * (1.0 - sg)))).astype(bf16)
        _linear_scan(_shift_up(a, 1), dy * (g * sg), a_s, b_s, edge_s, dh_s, reverse=True)
        dh_total = dh_s[...]
        da = dh_total * _shift_down(h, 1)
        dmult = dh_total * (i * u)
        db = dh_total * mult
        di = db * u
        du = db * i
        dlog_a_c = ((-LRU_C) * a) * (da - dmult * (a * inv_mult))
        dr = dlog_a_c * sp
        dsp = jnp.sum(dlog_a_c * r, axis=0, keepdims=True)
        dpre_r = dr * r * (1.0 - r)
        dpre_i = di * i * (1.0 - i)
        dpre_rb = dpre_r.astype(bf16)
        dpre_ib = dpre_i.astype(bf16)
        du = du + _dot(dpre_rb, wa_ref[...].astype(bf16), NT) + _dot(dpre_ib, wx_ref[...].astype(bf16), NT)
        gwa_ref[...] += _dot(ub, dpre_rb, TN)
        gwx_ref[...] += _dot(ub, dpre_ib, TN)
        vec_ref[0:1, :] += jnp.sum(du, axis=0, keepdims=True)
        vec_ref[1:2, :] += jnp.sum(dpre_r, axis=0, keepdims=True)
        vec_ref[2:3, :] += jnp.sum(dpre_i, axis=0, keepdims=True)
        vec_ref[3:4, :] += dsp * (-_sigmoid(-lam_v))
        dx0 = cw[3:4, :] * du
        for k in range(3):
            dx0 = dx0 + cw[k:k + 1, :] * _shift_up(du, 3 - k)
        for k in range(4):
            gcw_ref[k:k + 1, :] += jnp.sum(du * taps[k], axis=0, keepdims=True)
        du0_ref[...] = dx0.astype(bf16)

    wacc = pl.BlockSpec((RB, RB), lambda n, b: (0, n))
    vacc = pl.BlockSpec((8, RB), lambda n, b: (0, n))
    cacc = pl.BlockSpec((8, RB), lambda n, b: (n, 0))
    return _pcall(
        body, name="lru_backward", grid=(RNN_BLOCKS, nb),
        in_specs=[col(0), col(8), tokblk, tokblk, cwblk, vec, wblk, vec, wblk, vec, vec],
        out_specs=(tokblk, tokblk, wacc, wacc, vacc, cacc),
        out_shape=(_sds((T, D), bf16), _sds((T, D), bf16), _sds((RB, D), f32), _sds((RB, D), f32),
                   _sds((8, D), f32), _sds((8 * RNN_BLOCKS, RB), f32)),
        scratch_shapes=[pltpu.VMEM((S, RB), f32)] * 3 + [pltpu.VMEM((S // 8, RB), f32)],
        compiler_params=_params(("arbitrary", "arbitrary")),
    )(proj, proj, h_all, dy_rnn, cw_full, conv_b, w_a, b_a, w_x, b_x, lam)


def _section_of_chunk(s):
    out = []
    for start, n in zip(SEC_START, SEC_CHUNKS):
        inside = (s >= start) & (s < start + n)
        out.append((inside, jnp.clip(s - start, 0, n - 1)))
    return out


def _input_grad(dsecs, wt_full, x2d, dx2, norm_g, chip_sums):
    T = x2d.shape[0]
    tb = min(T, 1024)
    nchunks = D_IN // CH
    nsec = len(dsecs)
    nex = len(chip_sums)
    ntok = T // tb

    def body(*refs):
        secs = refs[:nsec]
        wt_ref, x_ref, dx2_ref, g_ref = refs[nsec:nsec + 4]
        ex_src = refs[nsec + 4:nsec + 4 + nex]
        dx_ref, gnorm_ref = refs[nsec + 4 + nex:nsec + 6 + nex]
        ex_dst = refs[nsec + 6 + nex:nsec + 6 + 2 * nex]
        acc = refs[nsec + 6 + 2 * nex]
        sems = refs[nsec + 7 + 2 * nex:]
        i, s = pl.program_id(0), pl.program_id(1)

        @pl.when((i == 0) & (s == 0))
        def _():
            gnorm_ref[...] = jnp.zeros_like(gnorm_ref)
            _start_all(_chip_exchange_copies(ex_src, ex_dst, *sems))

        @pl.when((i == ntok - 1) & (s == nchunks - 1))
        def _():
            _wait_all(_chip_exchange_copies(ex_src, ex_dst, *sems))

        @pl.when(s == 0)
        def _():
            acc[...] = jnp.zeros_like(acc)

        for a, (start, n) in enumerate(zip(SEC_START, SEC_CHUNKS)):
            @pl.when((s >= start) & (s < start + n))
            def _(a=a):
                acc[...] += _dot(secs[a][...], wt_ref[...], NN)

        @pl.when(s == nchunks - 1)
        def _():
            xv = x_ref[...]
            rstd = lax.rsqrt(jnp.mean(xv * xv, axis=-1, keepdims=True) + EPS)
            xh = xv * rstd
            dh = acc[...]
            gnorm_ref[0:1, :] += jnp.sum(dh * xh, axis=0, keepdims=True)
            dxn = dh * g_ref[...]
            dx_ref[...] = dx2_ref[...] + rstd * (dxn - xh * jnp.mean(dxn * xh, axis=-1, keepdims=True))

    def sec_spec(a):
        return pl.BlockSpec((tb, CH), lambda i, s, a=a: (i, _section_of_chunk(s)[a][1]))

    tok = pl.BlockSpec((tb, D), lambda i, s: (i, 0))
    res = _pcall(
        body, name="input_grad", grid=(ntok, nchunks),
        in_specs=[sec_spec(a) for a in range(nsec)] + [pl.BlockSpec((CH, D), lambda i, s: (s, 0)), tok, tok,
                                                        pl.BlockSpec((1, D), lambda i, s: (0, 0))] + [ANY] * nex,
        out_specs=(tok, pl.BlockSpec((8, D), lambda i, s: (0, 0))) + tuple([ANY] * nex),
        out_shape=(_sds((T, D), f32), _sds((8, D), f32)) + tuple(_sds(c.shape, c.dtype) for c in chip_sums),
        scratch_shapes=[pltpu.VMEM((tb, D), f32)] + _exchange_scratch(nex, 3),
        compiler_params=_params(("arbitrary", "arbitrary")),
    )(*dsecs, wt_full, x2d, dx2, norm_g, *chip_sums)
    return res[0], res[1], res[2:]


def _w_in_grad(dsecs, h_bf):
    T = h_bf.shape[0]
    tk = min(T, 1024)
    nchunks = D_IN // CH
    nsec = len(dsecs)
    nt = T // tk

    def body(*refs):
        secs = refs[:nsec]
        h_ref, out_ref, acc = refs[nsec:]
        s, t = pl.program_id(0), pl.program_id(1)

        @pl.when(t == 0)
        def _():
            acc[...] = jnp.zeros_like(acc)

        h_rows = h_ref[pl.ds(pl.multiple_of(t * tk, tk), tk), :]
        for a, (start, n) in enumerate(zip(SEC_START, SEC_CHUNKS)):
            @pl.when((s >= start) & (s < start + n))
            def _(a=a):
                acc[...] += _dot(secs[a][...], h_rows, TN)

        @pl.when(t == nt - 1)
        def _():
            out_ref[...] = acc[...].astype(bf16)

    def sec_spec(a):
        def index(s, t, a=a):
            inside, local = _section_of_chunk(s)[a]
            return (jnp.where(inside, t, 0), local)
        return pl.BlockSpec((tk, CH), index)

    return _pcall(
        body, name="w_in_grad", grid=(nchunks, T // tk),
        in_specs=[sec_spec(a) for a in range(nsec)] + [pl.BlockSpec((T, D), lambda s, t: (0, 0))],
        out_specs=pl.BlockSpec((CH, D), lambda s, t: (s, 0)), out_shape=_sds((D_IN, D), bf16),
        scratch_shapes=[pltpu.VMEM((CH, D), f32)],
        compiler_params=_params(("arbitrary", "arbitrary")),
    )(*dsecs, h_bf)


SMALL_NAMES = ("lru_w_a", "lru_w_x", "conv_b", "lru_b_a", "lru_b_x", "lru_lambda", "norm_g", "final_norm_g",
               "attn_sinks", "conv_w")
MISC_ROW = {"conv_b": 0, "lru_b_a": 1, "lru_b_x": 2, "lru_lambda": 3, "norm_g": 8, "final_norm_g": 16,
            "attn_sinks": 24, "loss": 32}


def _small_step(gwa, gwx, gvec, gnorm_blk, gfin_blk, dsink_blk, loss_blk, gcw, params):
    srcs_rows = (RB // NDEV, RB // NDEV, 8, 8)
    flat = [t for n in SMALL_NAMES for t in params[n]]
    nin = 8 + len(flat)
    nout = 4 * len(SMALL_NAMES) + 1

    def body(*refs):
        gwa_ref, gwx_ref, gvec_ref, gnorm_ref, gfin_ref, dsink_ref, loss_ref, gcw_ref = refs[:8]
        prm = {n: refs[8 + 3 * k:11 + 3 * k] for k, n in enumerate(SMALL_NAMES)}
        outs = {n: refs[nin + 4 * k:nin + 4 * k + 4] for k, n in enumerate(SMALL_NAMES)}
        loss_out = refs[nin + nout - 1]
        (misc, got_a, got_x, got_m, got_c, red_a, red_x, red_m, all_a, all_x, all_m,
         sa, ra, sb, rb) = refs[nin + nout:]
        x, y, c = _my_place()
        me = 4 * x + 2 * y + c

        misc[...] = jnp.zeros_like(misc)
        misc[0:8, :] = gvec_ref[...]
        misc[8:16, :] = gnorm_ref[...]
        misc[16:24, :] = gfin_ref[...]
        misc[24:32, 0:128] = dsink_ref[...]
        misc[32:40, :] = loss_ref[...]

        srcs = (gwa_ref, gwx_ref, misc, gcw_ref)
        gots = (got_a, got_x, got_m, got_c)

        def shard(ref, rows, dev):
            return ref.at[pl.ds(pl.multiple_of(dev * rows, 8), rows), :]

        scatter = []
        for k in range(1, NDEV):
            px, py, pc = _peer(k)
            for a in range(4):
                scatter.append(pltpu.make_async_remote_copy(
                    src_ref=shard(srcs[a], srcs_rows[a], 4 * px + 2 * py + pc), dst_ref=gots[a].at[k - 1],
                    send_sem=sa.at[4 * (k - 1) + a], recv_sem=ra.at[4 * (k - 1) + a],
                    device_id=(px, py, pc), device_id_type=MESH))
        for cp in scatter:
            cp.start()
        for cp in scatter:
            cp.wait()

        def reduced(a):
            rows = srcs_rows[a]
            total = srcs[a][pl.ds(pl.multiple_of(me * rows, 8), rows), :]
            for k in range(NDEV - 1):
                total = total + gots[a][k]
            return total

        reds = (red_a, red_x, red_m)
        alls = (all_a, all_x, all_m)
        for a in range(3):
            val = reduced(a)
            reds[a][...] = val
            alls[a][pl.ds(pl.multiple_of(me * srcs_rows[a], 8), srcs_rows[a]), :] = val
        gather = []
        for k in range(1, NDEV):
            peer = _peer(k)
            for a in range(3):
                gather.append(pltpu.make_async_remote_copy(
                    src_ref=reds[a], dst_ref=shard(alls[a], srcs_rows[a], me),
                    send_sem=sb.at[3 * (k - 1) + a], recv_sem=rb.at[3 * (k - 1) + a],
                    device_id=peer, device_id_type=MESH))
        for cp in gather:
            cp.start()
        g_conv = reduced(3)[0:4, :]
        for cp in gather:
            cp.wait()

        def update(name, g, pick=lambda r: r[...]):
            w_ref, m_ref, v_ref = prm[name]
            delta, m_new, v_new = _adam_math(g, pick(w_ref), pick(m_ref), pick(v_ref))
            return g, delta, m_new, v_new

        for n in range(RNN_BLOCKS):
            lanes = slice(RB * n, RB * (n + 1))
            for name, full in (("lru_w_a", all_a), ("lru_w_x", all_x)):
                for out, val in zip(outs[name], update(name, full[:, lanes], pick=lambda r, n=n: r[n])):
                    out[n] = val
        for name in ("conv_b", "lru_b_a", "lru_b_x", "lru_lambda", "norm_g", "final_norm_g"):
            row = MISC_ROW[name]
            for out, val in zip(outs[name], update(name, all_m[row:row + 1, :])):
                out[...] = val
        row = MISC_ROW["attn_sinks"]
        for out, val in zip(outs["attn_sinks"], update("attn_sinks", all_m[row:row + 1, 0:16])):
            out[...] = val
        for out, val in zip(outs["conv_w"], update("conv_w", g_conv)):
            out[...] = val
        row = MISC_ROW["loss"]
        loss_out[...] = all_m[row:row + 8, 0:128] * (0.5 / D)

    out_shape = tuple(_sds(params[n][0].shape, f32) for n in SMALL_NAMES for _ in range(4)) + (_sds((8, 128), f32),)
    scratch = [pltpu.VMEM((64, D), f32),
               pltpu.VMEM((NDEV - 1, RB // NDEV, D), f32), pltpu.VMEM((NDEV - 1, RB // NDEV, D), f32),
               pltpu.VMEM((NDEV - 1, 8, D), f32), pltpu.VMEM((NDEV - 1, 8, RB), f32),
               pltpu.VMEM((RB // NDEV, D), f32), pltpu.VMEM((RB // NDEV, D), f32), pltpu.VMEM((8, D), f32),
               pltpu.VMEM((RB, D), f32), pltpu.VMEM((RB, D), f32), pltpu.VMEM((64, D), f32),
               pltpu.SemaphoreType.DMA((4 * (NDEV - 1),)), pltpu.SemaphoreType.DMA((4 * (NDEV - 1),)),
               pltpu.SemaphoreType.DMA((3 * (NDEV - 1),)), pltpu.SemaphoreType.DMA((3 * (NDEV - 1),))]
    res = _pcall(
        body, name="small_step", out_shape=out_shape,
        in_specs=[VMEM_SPEC] * nin, out_specs=tuple([VMEM_SPEC] * nout),
        scratch_shapes=scratch, compiler_params=_params(),
    )(gwa, gwx, gvec, gnorm_blk, gfin_blk, dsink_blk, loss_blk, gcw, *flat)
    return {n: res[4 * k:4 * k + 4] for k, n in enumerate(SMALL_NAMES)}, res[-1]


def _pad_rows(v, rows=8):
    return jnp.concatenate([v, jnp.zeros((rows - v.shape[0], v.shape[1]), v.dtype)], axis=0)


def kernel(x, norm_g, w_in, conv_w, conv_b, lru_w_a, lru_b_a, lru_w_x, lru_b_x, lru_lambda, attn_sinks, w_rnn_out, w_attn_out, w_o, final_norm_g, loss_target, m_norm_g, m_w_in, m_conv_w, m_conv_b, m_lru_w_a, m_lru_b_a, m_lru_w_x, m_lru_b_x, m_lru_lambda, m_attn_sinks, m_w_rnn_out, m_w_attn_out, m_w_o, m_final_norm_g, v_norm_g, v_w_in, v_conv_w, v_conv_b, v_lru_w_a, v_lru_b_a, v_lru_w_x, v_lru_b_x, v_lru_lambda, v_attn_sinks, v_w_rnn_out, v_w_attn_out, v_w_o, v_final_norm_g):
    nb, S, _ = x.shape
    T = nb * S
    x2d = x.reshape(T, D)
    tgt = loss_target.reshape(T, D)
    fin_g = final_norm_g.reshape(1, D)
    w_a3, w_x3 = lru_w_a[0], lru_w_x[0]

    my_core = lax.axis_index("c").astype(jnp.int32).reshape(1)
    cx, cy = lax.axis_index("x"), lax.axis_index("y")
    chip_order = jnp.stack([2 * cx + cy, 2 * (1 - cx) + cy, 2 * cx + (1 - cy),
                            2 * (1 - cx) + (1 - cy)]).astype(jnp.int32)

    tabs = _rope_tables(S)
    h_bf, proj, wt_full, cw_full, (wr_full, wa_full, wo_full) = _in_proj_gather(
        x2d, norm_g, w_in[0].T.astype(bf16), _pad_rows(conv_w[0]), tabs, S,
        (w_rnn_out[0], w_attn_out[0], w_o[0]), chip_order)
    y_rnn, h_all = _lru_forward(proj, cw_full, conv_b, w_a3, lru_b_a, w_x3, lru_b_x, lru_lambda, S)
    y_attn = _attn_forward(proj, attn_sinks, S)

    (dx2, dy_rnn, dy_attn, dmr, dma, loss_blk, gfin_blk, g_wr, g_wa, g_wo) = _merge_and_head(
        x2d, tgt, proj, y_rnn, y_attn, wr_full, wa_full, wo_full, fin_g)
    sums_out = _pair_sums([g_wr, g_wa, g_wo], [bf16, bf16, bf16], my_core, "out")

    (dq, dkv, dga, dsink_blk), (p_wr, p_wa, p_wo) = _attn_backward(proj, dy_attn, tabs, attn_sinks, S, sums_out)
    du0, dgr, gwa, gwx, gvec, gcw = _lru_backward(proj, h_all, dy_rnn, cw_full, conv_b, w_a3, lru_b_a, w_x3,
                                                  lru_b_x, lru_lambda, S)
    dsecs = (du0, dgr, dq, dkv, dga, dmr, dma)

    g_wt = _w_in_grad(dsecs, h_bf)
    sums_in = _pair_sums([g_wt], [bf16], my_core, "in")
    grad_x2d, gnorm_blk, (p_wt,) = _input_grad(dsecs, wt_full, x2d, dx2, norm_g, sums_in)

    small, loss_out = _small_step(gwa, gwx, gvec, gnorm_blk, gfin_blk, dsink_blk, loss_blk, gcw, {
        "lru_w_a": (w_a3, m_lru_w_a[0], v_lru_w_a[0]), "lru_w_x": (w_x3, m_lru_w_x[0], v_lru_w_x[0]),
        "conv_b": (conv_b, m_conv_b, v_conv_b), "lru_b_a": (lru_b_a, m_lru_b_a, v_lru_b_a),
        "lru_b_x": (lru_b_x, m_lru_b_x, v_lru_b_x), "lru_lambda": (lru_lambda, m_lru_lambda, v_lru_lambda),
        "norm_g": (norm_g, m_norm_g, v_norm_g),
        "final_norm_g": (fin_g, m_final_norm_g.reshape(1, D), v_final_norm_g.reshape(1, D)),
        "attn_sinks": (attn_sinks, m_attn_sinks, v_attn_sinks),
        "conv_w": (conv_w[0], m_conv_w[0], v_conv_w[0])})

    o_wt = _adamw(p_wt, w_in[0].T, m_w_in[0].T, v_w_in[0].T, "adamw_w_in")
    o_wr = _adamw(p_wr, w_rnn_out[0], m_w_rnn_out[0], v_w_rnn_out[0], "adamw_w_rnn_out")
    o_wa = _adamw(p_wa, w_attn_out[0], m_w_attn_out[0], v_w_attn_out[0], "adamw_w_attn_out")
    o_wo = _adamw(p_wo, w_o[0], m_w_o[0], v_w_o[0], "adamw_w_o")

    def result(kind):
        d = {n: small[n][kind] for n in ("conv_b", "lru_b_a", "lru_b_x", "lru_lambda", "norm_g", "attn_sinks")}
        d.update({n: small[n][kind][None] for n in ("lru_w_a", "lru_w_x", "conv_w")})
        d["final_norm_g"] = small["final_norm_g"][kind].reshape(D)
        d.update({"w_in": o_wt[kind].T[None], "w_rnn_out": o_wr[kind][None], "w_attn_out": o_wa[kind][None],
                  "w_o": o_wo[kind][None]})
        return d

    order = ("norm_g", "w_in", "conv_w", "conv_b", "lru_w_a", "lru_b_a", "lru_w_x", "lru_b_x", "lru_lambda",
             "attn_sinks", "w_rnn_out", "w_attn_out", "w_o", "final_norm_g")
    outs = [loss_out[0, 0], grad_x2d.reshape(nb, S, D)]
    for kind in range(4):
        d = result(kind)
        outs += [d[n] for n in order]
    return tuple(outs)
```

```python
import functools
import math

import jax
import jax.numpy as jnp
from jax import lax
from jax.experimental import pallas as pl
from jax.experimental.pallas import tpu as pltpu

f32 = jnp.float32
bf16 = jnp.bfloat16

D = 1024
D_IN = 6656
NDEV = 8
RNN_BLOCKS = 8
RB = 128
HEAD = 64
KV_HEADS = 4
GROUP = 4
QB = 128
LRU_C = 8.0
EPS = 1e-6
ROPE_DIM = 16
ROPE_THETA = 500000.0
CH = 512
SEC_START = (0, 2, 4, 6, 7, 9, 11)
SEC_CHUNKS = (2, 2, 2, 1, 2, 2, 2)
VMEM_LIMIT = 62 * 1024 * 1024

ADAM_LR, ADAM_B1, ADAM_B2, ADAM_EPS, ADAM_WD, ADAM_STEP = 0.001, 0.9, 0.999, 1e-08, 0.01, 10

MESH = pl.DeviceIdType.MESH
ANY = pl.BlockSpec(memory_space=pl.ANY)
VMEM_SPEC = pl.BlockSpec(memory_space=pltpu.VMEM)
SMEM_SPEC = pl.BlockSpec(memory_space=pltpu.SMEM)


def _pcall(body, **kw):
    return pl.pallas_call(body, **kw)


def _params(sem=None, **kw):
    if sem is not None:
        kw["dimension_semantics"] = sem
    return pltpu.CompilerParams(vmem_limit_bytes=VMEM_LIMIT, **kw)


def _sds(shape, dtype):
    return jax.ShapeDtypeStruct(shape, dtype)


def _dot(a, b, dims):
    return lax.dot_general(a, b, (dims, ((), ())), preferred_element_type=f32)


NN = ((1,), (0,))
NT = ((1,), (1,))
TN = ((0,), (0,))


def _sigmoid(v):
    return 0.5 * jnp.tanh(0.5 * v) + 0.5


def _sigmoid_positive(v):
    return 1.0 / (1.0 + jnp.exp(-v))


def _my_place():
    return lax.axis_index("x"), lax.axis_index("y"), lax.axis_index("c")


def _peer(k):
    x, y, c = _my_place()
    return (x + ((k >> 2) & 1)) % 2, (y + ((k >> 1) & 1)) % 2, (c + (k & 1)) % 2


def _direct_gather_copies(srcs, outs, send_sems, recv_sems, local_sems):
    x, y, c = _my_place()
    me = 4 * x + 2 * y + c
    local, remote = [], []
    for a, (src, out) in enumerate(zip(srcs, outs)):
        r = src.shape[0]
        mine = out.at[pl.ds(pl.multiple_of(me * r, 8), r), :]
        local.append(pltpu.make_async_copy(src, mine, local_sems.at[a]))
        for k in range(1, NDEV):
            remote.append(pltpu.make_async_remote_copy(
                src_ref=src, dst_ref=mine, send_sem=send_sems.at[7 * a + k - 1], recv_sem=recv_sems.at[7 * a + k - 1],
                device_id=_peer(k), device_id_type=MESH))
    return local, remote


def _chip_exchange_copies(src, dst, send_sems, recv_sems, local_sems):
    x, y, c = _my_place()
    local, remote = [], []
    for a in range(len(src)):
        local.append(pltpu.make_async_copy(src[a].at[2 * x + y], dst[a].at[0], local_sems.at[a]))
    for k in (3, 1, 2):
        px, py = (x + (k >> 1)) % 2, (y + (k & 1)) % 2
        for a in range(len(src)):
            remote.append(pltpu.make_async_remote_copy(
                src_ref=src[a].at[2 * px + py], dst_ref=dst[a].at[k],
                send_sem=send_sems.at[3 * a + k - 1], recv_sem=recv_sems.at[3 * a + k - 1],
                device_id=(px, py, c), device_id_type=MESH))
    return local, remote


def _exchange_scratch(narr, per_array):
    return [pltpu.SemaphoreType.DMA((per_array * narr,)), pltpu.SemaphoreType.DMA((per_array * narr,)),
            pltpu.SemaphoreType.DMA((narr,))]


def _start_all(copies):
    local, remote = copies
    for cp in local + remote:
        cp.start()


def _wait_all(copies):
    local, remote = copies
    for cp in remote + local:
        cp.wait()


def _pair_exchange(grads, name):
    narr = len(grads)
    nrows = tuple(g.shape[0] // NDEV for g in grads)
    views = [g.reshape(4, 2, r, g.shape[1]) for g, r in zip(grads, nrows)]

    def body(*refs):
        gin = refs[:narr]
        got = refs[narr:2 * narr]
        send_sems, recv_sems = refs[2 * narr:]
        x, y, c = _my_place()
        copies = [pltpu.make_async_remote_copy(
            src_ref=gin[a].at[:, pl.ds(1 - c, 1)], dst_ref=got[a],
            send_sem=send_sems.at[a], recv_sem=recv_sems.at[a],
            device_id=(x, y, 1 - c), device_id_type=MESH) for a in range(narr)]
        for cp in copies:
            cp.start()
        for cp in copies:
            cp.wait()

    out_shape = tuple(_sds((4, 1, r, g.shape[1]), g.dtype) for r, g in zip(nrows, grads))
    got = _pcall(
        body, name=name, out_shape=out_shape,
        in_specs=[ANY] * narr, out_specs=tuple([ANY] * narr),
        scratch_shapes=[pltpu.SemaphoreType.DMA((narr,)), pltpu.SemaphoreType.DMA((narr,))],
        compiler_params=_params(),
    )(*views)
    return views, [g.reshape(4, r, g.shape[3]) for g, r in zip(got, nrows)]


def _row_tile(rows, dtype):
    unit = 16 if dtype == bf16 else 8
    for cand in (256, 208, 128, 64, 40, 32, 16, 8):
        if rows % cand == 0 and cand % unit == 0:
            return cand
    return rows


def _chip_sum(view, got, my_core, out_dtype, name):
    _, _, r, cols = view.shape
    tr = _row_tile(r, out_dtype)

    def body(core_ref, mine_ref, got_ref, out_ref):
        out_ref[...] = (mine_ref[...].astype(f32) + got_ref[...].astype(f32)).astype(out_dtype)

    grid_spec = pltpu.PrefetchScalarGridSpec(
        num_scalar_prefetch=1, grid=(4, r // tr),
        in_specs=[pl.BlockSpec((None, None, tr, cols), lambda q, i, core: (q, core[0], i, 0)),
                  pl.BlockSpec((None, tr, cols), lambda q, i, core: (q, i, 0))],
        out_specs=pl.BlockSpec((None, tr, cols), lambda q, i, core: (q, i, 0)))
    return _pcall(body, name=name, grid_spec=grid_spec, out_shape=_sds((4, r, cols), out_dtype),
                  compiler_params=_params(("arbitrary", "arbitrary")))(my_core, view, got)


def _pair_sums(grads, wire_dtypes, my_core, tag):
    views, got = _pair_exchange(grads, "pair_exchange_" + tag)
    return [_chip_sum(v, g, my_core, dt, "chip_sum_%s%d" % (tag, a))
            for a, (v, g, dt) in enumerate(zip(views, got, wire_dtypes))]


def _adam_math(g, w, m, v):
    m_new = ADAM_B1 * m + (1.0 - ADAM_B1) * g
    v_new = ADAM_B2 * v + (1.0 - ADAM_B2) * (g * g)
    m_hat = m_new / (1.0 - ADAM_B1 ** ADAM_STEP)
    v_hat = v_new / (1.0 - ADAM_B2 ** ADAM_STEP)
    return -ADAM_LR * (m_hat / (jnp.sqrt(v_hat) + ADAM_EPS) + ADAM_WD * w), m_new, v_new


def _adamw(parts, w, m, v, name):
    n, rows, cols = parts.shape
    tr = _row_tile(rows, parts.dtype)

    def body(p_ref, w_ref, m_ref, v_ref, g_out, d_out, m_out, v_out):
        g = p_ref[0].astype(f32)
        for s in range(1, n):
            g = g + p_ref[s].astype(f32)
        g_out[...] = g
        d_out[...], m_out[...], v_out[...] = _adam_math(g, w_ref[...], m_ref[...], v_ref[...])

    blk = pl.BlockSpec((tr, cols), lambda i: (i, 0))
    return _pcall(
        body, name=name, grid=(rows // tr,),
        in_specs=[pl.BlockSpec((n, tr, cols), lambda i: (0, i, 0)), blk, blk, blk],
        out_specs=(blk, blk, blk, blk), out_shape=tuple(_sds((rows, cols), f32) for _ in range(4)),
        compiler_params=_params(("arbitrary",)),
    )(parts, w, m, v)


def _rope(t, c, s1, s2):
    w = t.shape[1]
    return t * c + pltpu.roll(t, w - 8, 1) * s1 + pltpu.roll(t, 8, 1) * s2


def _rope_transposed(dt, c, s1, s2):
    w = dt.shape[1]
    return dt * c + pltpu.roll(dt * s1, 8, 1) + pltpu.roll(dt * s2, w - 8, 1)


PAIR_ROWS = D_IN // 4
SUB_COLS = ((0, 512), (512, 512), (1024, 512), (1536, 128))
Q_SLABS = range(3, 11)
K_SLABS = range(11, 13)


def _in_proj_gather(x2d, norm_g, wt_shard, cw_shard, tabs, S, out_shards, chip_order):
    T = x2d.shape[0]
    tb = min(S, 1024)
    ntok = T // tb
    nsb = S // tb
    q_scale = 1.0 / math.sqrt(HEAD)
    shard_rows = wt_shard.shape[0]
    small = (cw_shard,) + tuple(out_shards)
    nsm = len(small)

    def body(order_ref, x_ref, g_ref, c_ref, s1_ref, s2_ref, wt_hbm, *rest):
        small_in = rest[:nsm]
        h_ref, proj_ref, wt_out = rest[nsm:nsm + 3]
        small_out = rest[nsm + 3:2 * nsm + 3]
        wt_vm, h_vm = rest[2 * nsm + 3:2 * nsm + 5]
        stage = rest[2 * nsm + 5:3 * nsm + 4]
        wsend, wrecv, wlocal = rest[3 * nsm + 4:3 * nsm + 7]
        dsems = rest[3 * nsm + 7:]
        jj, i = pl.program_id(0), pl.program_id(1)
        x, y, c = _my_place()
        me, sibling = (x, y, c), (x, y, 1 - c)
        chips = [(1 - x, y), (x, 1 - y), (1 - x, 1 - y)]

        def rows(place):
            px, py, pc = place
            return wt_vm.at[pl.ds(pl.multiple_of((4 * px + 2 * py + pc) * shard_rows, 16), shard_rows), :]

        def copy(k, block, to, src=None):
            return pltpu.make_async_remote_copy(
                src_ref=rows(block) if src is None else src, dst_ref=rows(block),
                send_sem=wsend.at[k], recv_sem=wrecv.at[k], device_id=to, device_id_type=MESH)

        def small_copies():
            srcs = (small_in[0],) + tuple(stage)
            return _direct_gather_copies(srcs, small_out, *dsems)

        own = pltpu.make_async_copy(wt_hbm, rows(me), wlocal.at[0])
        keep = pltpu.make_async_copy(wt_vm, wt_out, wlocal.at[1])

        @pl.when((jj == 0) & (i == 0))
        def _():
            own.start()
            copy(0, me, sibling, src=wt_hbm).start()
            for j, chip in enumerate(chips):
                copy(1 + j, me, (*chip, c), src=wt_hbm).start()
            for a in range(nsm - 1):
                stage[a][...] = small_in[1 + a][...].astype(bf16)
            _start_all(small_copies())
            own.wait()
            copy(0, sibling, me).wait_recv()

        for j, chip in enumerate(chips):
            @pl.when((jj == 1 + j) & (i == 0))
            def _(j=j, chip=chip):
                copy(1 + j, (*chip, c), me).wait_recv()
                copy(4 + j, (*chip, c), sibling).start()
                copy(4 + j, (*chip, 1 - c), me).wait_recv()

        @pl.when((jj == 3) & (i == 0))
        def _():
            keep.start()

        @pl.when((jj == 3) & (i == ntok - 1))
        def _():
            copy(0, me, sibling, src=wt_hbm).wait_send()
            for j, chip in enumerate(chips):
                copy(1 + j, me, (*chip, c), src=wt_hbm).wait_send()
                copy(4 + j, (*chip, c), sibling).wait_send()
            _wait_all(small_copies())
            keep.wait()

        tok = pl.ds(pl.multiple_of(i * tb, tb), tb)

        @pl.when(jj == 0)
        def _():
            xv = x_ref[...]
            ms = jnp.mean(xv * xv, axis=-1, keepdims=True)
            hb = (xv * lax.rsqrt(ms + EPS) * g_ref[...]).astype(bf16)
            h_ref[...] = hb
            h_vm[tok, :] = hb

        block = order_ref[jj]
        hb = h_vm[tok, :]

        def piece(c0, w):
            w_rows = wt_vm[pl.ds(pl.multiple_of(block * PAIR_ROWS + c0, 128), w), :]
            return _dot(hb, w_rows, NT)

        @pl.when(block != 1)
        def _():
            for c0, w in SUB_COLS:
                proj_ref[:, c0:c0 + w] = piece(c0, w).astype(bf16)

        @pl.when(block == 1)
        def _():
            tab = (c_ref[...], s1_ref[...], s2_ref[...])
            for c0, w in SUB_COLS:
                acc = piece(c0, w)
                for l in range(w // 128):
                    slab = (c0 + 128 * l) // 128
                    part = acc[:, 128 * l:128 * (l + 1)]
                    if slab in Q_SLABS:
                        part = _rope(part, *tab) * q_scale
                    elif slab in K_SLABS:
                        part = _rope(part, *tab)
                    proj_ref[:, 128 * slab:128 * (slab + 1)] = part.astype(bf16)

    first_pass = lambda jj, i, order: (jnp.where(jj == 0, i, ntok - 1), 0)
    const = lambda jj, i, order: (0, 0)
    tab = pl.BlockSpec((tb, 128), lambda jj, i, order: (jnp.where(order[jj] == 1, i % nsb, 0), 0))
    grid_spec = pltpu.PrefetchScalarGridSpec(
        num_scalar_prefetch=1, grid=(4, ntok),
        in_specs=[pl.BlockSpec((tb, D), first_pass), pl.BlockSpec((1, D), const), tab, tab, tab, ANY]
        + [pl.BlockSpec(w.shape, const) for w in small],
        out_specs=(pl.BlockSpec((tb, D), first_pass),
                   pl.BlockSpec((tb, PAIR_ROWS), lambda jj, i, order: (i, order[jj])), ANY) + tuple([ANY] * nsm),
        scratch_shapes=[pltpu.VMEM((D_IN, D), bf16), pltpu.VMEM((T, D), bf16)]
        + [pltpu.VMEM(w.shape, bf16) for w in out_shards]
        + [pltpu.SemaphoreType.DMA((7,)), pltpu.SemaphoreType.DMA((7,)), pltpu.SemaphoreType.DMA((2,))]
        + _exchange_scratch(nsm, 7))
    res = _pcall(
        body, name="in_proj", grid_spec=grid_spec,
        out_shape=(_sds((T, D), bf16), _sds((T, D_IN), bf16), _sds((D_IN, D), bf16),
                   _sds((NDEV * cw_shard.shape[0], cw_shard.shape[1]), f32))
        + tuple(_sds((NDEV * w.shape[0], w.shape[1]), bf16) for w in out_shards),
        compiler_params=_params(("arbitrary", "arbitrary")),
    )(chip_order, x2d, norm_g, *tabs, wt_shard, *small)
    return res[0], res[1], res[2], res[3], res[4:]


def _rows_iota(shape):
    return lax.broadcasted_iota(jnp.int32, shape, 0)


def _shift_down(v, k):
    return jnp.where(_rows_iota(v.shape) >= k, pltpu.roll(v, k, 0), 0.0)


def _shift_up(v, k):
    n = v.shape[0]
    return jnp.where(_rows_iota(v.shape) < n - k, pltpu.roll(v, n - k, 0), 0.0)


def _linear_scan(a, b, a_s, b_s, edge_s, out_ref, reverse):
    n = a.shape[0]
    ng = n // 8
    a3, b3 = a.reshape(ng, 8, RB), b.reshape(ng, 8, RB)
    rid = lax.broadcasted_iota(jnp.int32, a3.shape, 1)
    for s in (1, 2, 4):
        keep, shift = (rid < 8 - s, 8 - s) if reverse else (rid >= s, s)
        b3 = jnp.where(keep, a3 * pltpu.roll(b3, shift, 1) + b3, b3)
        a3 = jnp.where(keep, a3 * pltpu.roll(a3, shift, 1), a3)
    a_s[...] = a3.reshape(n, RB)
    b_s[...] = b3.reshape(n, RB)
    edge = 0 if reverse else 7
    ea, eb = a_s[pl.ds(edge, ng, stride=8), :], b_s[pl.ds(edge, ng, stride=8), :]
    r = _rows_iota(ea.shape)
    s = 1
    while s < ng:
        keep, shift = (r < ng - s, ng - s) if reverse else (r >= s, s)
        eb = jnp.where(keep, ea * pltpu.roll(eb, shift, 0) + eb, eb)
        if 2 * s < ng:
            ea = jnp.where(keep, ea * pltpu.roll(ea, shift, 0), ea)
        s *= 2
    edge_s[...] = _shift_up(eb, 1) if reverse else _shift_down(eb, 1)

    def eight_groups(i, carry):
        for k in range(8):
            j = i * 8 + k
            rows = pl.ds(pl.multiple_of(j * 8, 8), 8)
            out_ref[rows, :] = b_s[rows, :] + a_s[rows, :] * edge_s[pl.ds(j, 1), :]
        return carry

    lax.fori_loop(0, ng // 8, eight_groups, 0)


def _neg_expm1(v):
    series = -v * (1.0 + v * (0.5 + v * (1.0 / 6.0)))
    return jnp.where(v > -0.015625, series, 1.0 - jnp.exp(v))


def _softplus_neg(lam):
    return jnp.maximum(-lam, 0.0) + jnp.log(1.0 + jnp.exp(-jnp.abs(lam)))


def _lru_gates(x0, cw, cb, wa, ba, wx, bx, lam):
    taps = [_shift_down(x0, 3 - k) for k in range(3)] + [x0]
    u = cb + cw[3:4, :] * x0
    for k in range(3):
        u = u + cw[k:k + 1, :] * taps[k]
    ub = u.astype(bf16)
    r = _sigmoid_positive(_dot(ub, wa.astype(bf16), NN) + ba)
    i = _sigmoid(_dot(ub, wx.astype(bf16), NN) + bx)
    sp = _softplus_neg(lam)
    log_a = (-LRU_C) * r * sp
    a = jnp.exp(log_a)
    w = _neg_expm1(2.0 * log_a)
    inv_mult = lax.rsqrt(w)
    return u, ub, r, i, sp, a, w * inv_mult, inv_mult, taps


def _lru_specs(S, nb):
    col = lambda off: pl.BlockSpec((S, RB), lambda n, b, off=off: (b, off + n))
    vec = pl.BlockSpec((1, RB), lambda n, b: (0, n))
    wblk = pl.BlockSpec((None, RB, RB), lambda n, b: (n, 0, 0))
    cwblk = pl.BlockSpec((8, RB), lambda n, b: (n, 0))
    return col, vec, wblk, cwblk


def _lru_forward(proj, cw_full, conv_b, w_a, b_a, w_x, b_x, lam, S):
    T = proj.shape[0]
    nb = T // S
    col, vec, wblk, cwblk = _lru_specs(S, nb)

    def body(x0_ref, g_ref, cw_ref, cb_ref, wa_ref, ba_ref, wx_ref, bx_ref, lam_ref, y_ref, h_ref, a_s, b_s, edge_s):
        x0 = x0_ref[...].astype(f32)
        u, ub, r, i, sp, a, mult, _, _ = _lru_gates(x0, cw_ref[...], cb_ref[...], wa_ref[...], ba_ref[...],
                                                    wx_ref[...], bx_ref[...], lam_ref[...])
        _linear_scan(a, mult * (i * u), a_s, b_s, edge_s, h_ref, reverse=False)
        g = g_ref[...].astype(f32)
        y_ref[...] = (h_ref[...] * (g * _sigmoid(g))).astype(bf16)

    out = pl.BlockSpec((S, RB), lambda n, b: (b, n))
    return _pcall(
        body, name="lru_forward", grid=(RNN_BLOCKS, nb),
        in_specs=[col(0), col(8), cwblk, vec, wblk, vec, wblk, vec, vec],
        out_specs=(out, out), out_shape=(_sds((T, D), bf16), _sds((T, D), f32)),
        scratch_shapes=[pltpu.VMEM((S, RB), f32), pltpu.VMEM((S, RB), f32), pltpu.VMEM((S // 8, RB), f32)],
        compiler_params=_params(("arbitrary", "arbitrary")),
    )(proj, proj, cw_full, conv_b, w_a, b_a, w_x, b_x, lam)


def _rope_tables(S):
    pos = jnp.arange(S, dtype=f32)
    inv_freq = ROPE_THETA ** (-jnp.arange(0, ROPE_DIM, 2, dtype=f32) / ROPE_DIM)
    ang = pos[:, None] * inv_freq[None, :]
    cos, sin = jnp.cos(ang), jnp.sin(ang)
    lane = jnp.arange(128) % HEAD
    cosl, sinl = cos[:, lane % 8], sin[:, lane % 8]
    c = jnp.where(lane[None, :] < ROPE_DIM, cosl, 1.0)
    s1 = jnp.where(lane[None, :] < 8, -sinl, 0.0)
    s2 = jnp.where((lane[None, :] >= 8) & (lane[None, :] < ROPE_DIM), sinl, 0.0)
    return c.astype(f32), s1.astype(f32), s2.astype(f32)


def _heads_to_rows(t):
    return jnp.concatenate([t[:, HEAD * h:HEAD * (h + 1)] for h in range(GROUP)], axis=0)


def _rows_to_heads(t):
    return jnp.concatenate([t[QB * h:QB * (h + 1), :] for h in range(GROUP)], axis=1)


def _window_bias(first_block):
    shape = (GROUP * QB, 2 * QB)
    qi = _rows_iota(shape) % QB
    cj = lax.broadcasted_iota(jnp.int32, shape, 1)
    valid = (cj > qi) & (cj <= qi + QB) & ((cj >= QB) | jnp.logical_not(first_block))
    return jnp.where(valid, 0.0, -jnp.inf)


def _attn_probs(q_rows, k_cat, sink_col, bias):
    s = _dot(q_rows, k_cat, NT) + bias
    m = jnp.maximum(jnp.max(s, axis=1, keepdims=True), sink_col)
    p = jnp.exp(s - m)
    e_sink = jnp.exp(sink_col - m)
    inv = 1.0 / (jnp.sum(p, axis=1, keepdims=True) + e_sink)
    return p * inv, e_sink * inv


def _sink_column(sink_ref, kv):
    rid = _rows_iota((GROUP * QB, 1))
    col = jnp.zeros((GROUP * QB, 1), f32)
    for h in range(GROUP):
        col = jnp.where(rid // QB == h, sink_ref[0, GROUP * kv + h], col)
    return col


def _attn_in_specs(S):
    nq = S // QB
    last = nq - 1
    cur = lambda b, j: b * nq + jnp.minimum(j, last)
    prev = lambda b, j: b * nq + jnp.maximum(jnp.minimum(j, last) - 1, 0)
    specs = [
        pl.BlockSpec((QB, D), lambda b, j: (cur(b, j), 2)),
        pl.BlockSpec((QB, 256), lambda b, j: (cur(b, j), 12)),
        pl.BlockSpec((QB, 256), lambda b, j: (prev(b, j), 12)),
        pl.BlockSpec((QB, 256), lambda b, j: (cur(b, j), 13)),
        pl.BlockSpec((QB, 256), lambda b, j: (prev(b, j), 13)),
        pl.BlockSpec((QB, 512), lambda b, j: (cur(b, j), 7)),
        pl.BlockSpec((QB, 512), lambda b, j: (cur(b, j), 8)),
        SMEM_SPEC,
    ]
    return specs, cur, prev


def _attn_forward(proj, sinks, S):
    T = proj.shape[0]
    nb, nq = T // S, S // QB
    specs, cur, _ = _attn_in_specs(S)

    def body(q_ref, kc_ref, kp_ref, vc_ref, vp_ref, gl_ref, gh_ref, sink_ref, y_ref):
        bias = _window_bias(pl.program_id(1) == 0)
        kc, kp, vc, vp = kc_ref[...], kp_ref[...], vc_ref[...], vp_ref[...]
        for kv in range(KV_HEADS):
            lanes = slice(256 * kv, 256 * (kv + 1))
            hl = slice(HEAD * kv, HEAD * (kv + 1))
            q_rows = _heads_to_rows(q_ref[:, lanes])
            k_cat = jnp.concatenate([kp[:, hl], kc[:, hl]], axis=0)
            v_cat = jnp.concatenate([vp[:, hl], vc[:, hl]], axis=0)
            probs, _ = _attn_probs(q_rows, k_cat, _sink_column(sink_ref, kv), bias)
            o = _rows_to_heads(_dot(probs.astype(bf16), v_cat, NN))
            g_src = gl_ref if kv < 2 else gh_ref
            g = g_src[:, 256 * (kv % 2):256 * (kv % 2 + 1)].astype(f32)
            y_ref[:, lanes] = (o * (g * _sigmoid(g))).astype(bf16)

    args = [proj] * 7 + [sinks]
    return _pcall(
        body, name="attn_forward", grid=(nb, nq), in_specs=specs,
        out_specs=pl.BlockSpec((QB, D), lambda b, j: (cur(b, j), 0)), out_shape=_sds((T, D), bf16),
        compiler_params=_params(("arbitrary", "arbitrary")),
    )(*args)


def _merge_and_head(x2d, tgt, proj, y_rnn, y_attn, w_r, w_a, w_o, gfin):
    T = x2d.shape[0]
    tb = min(T, 512)
    nsteps = T // tb

    def body(x_ref, t_ref, mr0, mr1, ma0, ma1, yr_ref, ya_ref, wr_ref, wa_ref, wo_ref, gf_ref,
             dx2_ref, dyr_ref, dya_ref, dmr_ref, dma_ref, loss_ref, gfin_ref, gwr_out, gwa_out, gwo_out,
             gwr_acc, gwa_acc, gwo_acc, out_sems):
        step = pl.program_id(0)

        @pl.when(step == 0)
        def _():
            loss_ref[...] = jnp.zeros_like(loss_ref)
            gfin_ref[...] = jnp.zeros_like(gfin_ref)
            gwr_acc[...] = jnp.zeros_like(gwr_acc)
            gwa_acc[...] = jnp.zeros_like(gwa_acc)
            gwo_acc[...] = jnp.zeros_like(gwo_acc)

        sr = _sigmoid(jnp.concatenate([mr0[...], mr1[...]], axis=1).astype(f32))
        sa = _sigmoid(jnp.concatenate([ma0[...], ma1[...]], axis=1).astype(f32))
        p_r = _dot(yr_ref[...], wr_ref[...], NN)
        p_a = _dot(ya_ref[...], wa_ref[...], NN)
        merged = (sr * p_r + sa * p_a).astype(bf16)
        x2 = x_ref[...] + _dot(merged, wo_ref[...], NN)
        rstd = lax.rsqrt(jnp.mean(x2 * x2, axis=-1, keepdims=True) + EPS)
        xh = x2 * rstd
        gf = gf_ref[...]
        err = xh * gf - t_ref[...]
        loss_ref[...] += jnp.sum(err * err)
        dy = err * (1.0 / D)
        gfin_ref[0:1, :] += jnp.sum(dy * xh, axis=0, keepdims=True)
        dxn = dy * gf
        dx2 = rstd * (dxn - xh * jnp.mean(dxn * xh, axis=-1, keepdims=True))
        dx2_ref[...] = dx2
        dx2b = dx2.astype(bf16)
        dmerged = _dot(dx2b, wo_ref[...], NT)
        dmr_ref[...] = (dmerged * p_r * (sr * (1.0 - sr))).astype(bf16)
        dma_ref[...] = (dmerged * p_a * (sa * (1.0 - sa))).astype(bf16)
        dpr = (dmerged * sr).astype(bf16)
        dpa = (dmerged * sa).astype(bf16)
        dyr_ref[...] = _dot(dpr, wr_ref[...], NT).astype(bf16)
        dya_ref[...] = _dot(dpa, wa_ref[...], NT).astype(bf16)
        gwr_acc[...] += _dot(yr_ref[...], dpr, TN)
        gwa_acc[...] += _dot(ya_ref[...], dpa, TN)
        gwo_acc[...] += _dot(merged, dx2b, TN)

        @pl.when(step == nsteps - 1)
        def _():
            copies = [pltpu.make_async_copy(src, dst, out_sems.at[k]) for k, (src, dst) in enumerate(
                ((gwr_acc, gwr_out), (gwa_acc, gwa_out), (gwo_acc, gwo_out)))]
            for cp in copies:
                cp.start()
            for cp in copies:
                cp.wait()

    tok = pl.BlockSpec((tb, D), lambda i: (i, 0))
    half = lambda c: pl.BlockSpec((tb, CH), lambda i, c=c: (i, c))
    wfull = pl.BlockSpec((D, D), lambda i: (0, 0), pipeline_mode=pl.Buffered(1))
    acc = pl.BlockSpec((8, D), lambda i: (0, 0))
    return _pcall(
        body, name="merge_and_head", grid=(nsteps,),
        in_specs=[tok, tok, half(9), half(10), half(11), half(12), tok, tok, wfull, wfull, wfull,
                  pl.BlockSpec((1, D), lambda i: (0, 0))],
        out_specs=(tok, tok, tok, tok, tok, acc, acc, ANY, ANY, ANY),
        out_shape=(_sds((T, D), f32), _sds((T, D), bf16), _sds((T, D), bf16), _sds((T, D), bf16),
                   _sds((T, D), bf16), _sds((8, D), f32), _sds((8, D), f32),
                   _sds((D, D), f32), _sds((D, D), f32), _sds((D, D), f32)),
        scratch_shapes=[pltpu.VMEM((D, D), f32)] * 3 + [pltpu.SemaphoreType.DMA((3,))],
        compiler_params=_params(("arbitrary",)),
    )(x2d, tgt, proj, proj, proj, proj, y_rnn, y_attn, w_r, w_a, w_o, gfin)


def _attn_backward(proj, dy_attn, tabs, sinks, S, chip_sums):
    T = proj.shape[0]
    nb, nq = T // S, S // QB
    nex = len(chip_sums)
    specs, cur, prev = _attn_in_specs(S)
    last = nq - 1
    tab_cur = pl.BlockSpec((QB, 128), lambda b, j: (jnp.minimum(j, last), 0))
    tab_prev = pl.BlockSpec((QB, 128), lambda b, j: (jnp.maximum(jnp.minimum(j, last) - 1, 0), 0))
    specs = specs + [pl.BlockSpec((QB, D), lambda b, j: (cur(b, j), 0))] + [tab_cur] * 3 + [tab_prev] * 3
    q_scale = 1.0 / math.sqrt(HEAD)

    def rope_back(dt, tab):
        return jnp.concatenate([_rope_transposed(dt[:, 128 * l:128 * (l + 1)], *tab) for l in range(2)], axis=1)

    def body(q_ref, kc_ref, kp_ref, vc_ref, vp_ref, gl_ref, gh_ref, sink_ref, dy_ref, cc, s1c, s2c, cp, s1p, s2p,
             *rest):
        ex_src = rest[:nex]
        dq_ref, dkv_ref, dg_ref, dsink_ref = rest[nex:nex + 4]
        ex_dst = rest[nex + 4:2 * nex + 4]
        carry_k, carry_v = rest[2 * nex + 4:2 * nex + 6]
        sems = rest[2 * nex + 6:]
        b, j = pl.program_id(0), pl.program_id(1)

        @pl.when((b == 0) & (j == 0))
        def _():
            dsink_ref[...] = jnp.zeros_like(dsink_ref)
            _start_all(_chip_exchange_copies(ex_src, ex_dst, *sems))

        @pl.when((b == nb - 1) & (j == nq))
        def _():
            _wait_all(_chip_exchange_copies(ex_src, ex_dst, *sems))

        @pl.when(j == 0)
        def _():
            carry_k[...] = jnp.zeros_like(carry_k)
            carry_v[...] = jnp.zeros_like(carry_v)

        @pl.when(j < nq)
        def _():
            bias = _window_bias(j == 0)
            tc = (cc[...], s1c[...], s2c[...])
            tp = (cp[...], s1p[...], s2p[...])
            kc, kp, vc, vp = kc_ref[...], kp_ref[...], vc_ref[...], vp_ref[...]
            dk_prev, dk_cur, dv_prev, dv_cur = [], [], [], []
            dsink_acc = jnp.zeros((8, 128), f32)
            r8 = lax.broadcasted_iota(jnp.int32, (8, 128), 0)
            l8 = lax.broadcasted_iota(jnp.int32, (8, 128), 1)
            for kv in range(KV_HEADS):
                lanes = slice(256 * kv, 256 * (kv + 1))
                hl = slice(HEAD * kv, HEAD * (kv + 1))
                q_rows = _heads_to_rows(q_ref[:, lanes])
                k_cat = jnp.concatenate([kp[:, hl], kc[:, hl]], axis=0)
                v_cat = jnp.concatenate([vp[:, hl], vc[:, hl]], axis=0)
                probs, p_sink = _attn_probs(q_rows, k_cat, _sink_column(sink_ref, kv), bias)
                pb = probs.astype(bf16)
                o = _rows_to_heads(_dot(pb, v_cat, NN))
                g_src = gl_ref if kv < 2 else gh_ref
                g = g_src[:, 256 * (kv % 2):256 * (kv % 2 + 1)].astype(f32)
                sg = _sigmoid(g)
                dy = dy_ref[:, lanes].astype(f32)
                dg_ref[:, lanes] = (dy * o * (sg * (1.0 + g * (1.0 - sg)))).astype(bf16)
                do_rows = _heads_to_rows(dy * (g * sg)).astype(bf16)
                dv = _dot(pb, do_rows, TN)
                dp = _dot(do_rows, v_cat, NT)
                rowdot = jnp.sum(probs * dp, axis=1, keepdims=True)
                ds = (probs * (dp - rowdot)).astype(bf16)
                sink_rows = -(p_sink * rowdot)
                for h in range(GROUP):
                    val = jnp.sum(sink_rows[QB * h:QB * (h + 1), :])
                    dsink_acc = dsink_acc + jnp.where((r8 == 0) & (l8 == GROUP * kv + h), val, 0.0)
                dq = _rows_to_heads(_dot(ds, k_cat, NN)) * q_scale
                dq_ref[:, lanes] = rope_back(dq, tc).astype(bf16)
                dk = _dot(ds, q_rows, TN)
                dk_prev.append(dk[:QB, :])
                dk_cur.append(dk[QB:, :])
                dv_prev.append(dv[:QB, :])
                dv_cur.append(dv[QB:, :])
            dsink_ref[...] += dsink_acc
            dkp = rope_back(jnp.concatenate(dk_prev, axis=1), tp)
            dkc = rope_back(jnp.concatenate(dk_cur, axis=1), tc)
            dkv_ref[:, 0:256] = (carry_k[...] + dkp).astype(bf16)
            dkv_ref[:, 256:512] = (carry_v[...] + jnp.concatenate(dv_prev, axis=1)).astype(bf16)
            carry_k[...] = dkc
            carry_v[...] = jnp.concatenate(dv_cur, axis=1)

        @pl.when(j == nq)
        def _():
            dkv_ref[:, 0:256] = carry_k[...].astype(bf16)
            dkv_ref[:, 256:512] = carry_v[...].astype(bf16)

    lag = lambda b, j: (b * nq + jnp.maximum(j - 1, 0), 0)
    args = [proj] * 7 + [sinks, dy_attn] + list(tabs) + list(tabs) + list(chip_sums)
    res = _pcall(
        body, name="attn_backward", grid=(nb, nq + 1), in_specs=specs + [ANY] * nex,
        out_specs=(pl.BlockSpec((QB, D), lambda b, j: (cur(b, j), 0)), pl.BlockSpec((QB, 512), lag),
                   pl.BlockSpec((QB, D), lambda b, j: (cur(b, j), 0)), pl.BlockSpec((8, 128), lambda b, j: (0, 0)))
        + tuple([ANY] * nex),
        out_shape=(_sds((T, D), bf16), _sds((T, 512), bf16), _sds((T, D), bf16), _sds((8, 128), f32))
        + tuple(_sds(s.shape, s.dtype) for s in chip_sums),
        scratch_shapes=[pltpu.VMEM((QB, 256), f32), pltpu.VMEM((QB, 256), f32)] + _exchange_scratch(nex, 3),
        compiler_params=_params(("arbitrary", "arbitrary")),
    )(*args)
    return res[:4], res[4:]


def _lru_backward(proj, h_all, dy_rnn, cw_full, conv_b, w_a, b_a, w_x, b_x, lam, S):
    T = proj.shape[0]
    nb = T // S
    col, vec, wblk, cwblk = _lru_specs(S, nb)
    tokblk = pl.BlockSpec((S, RB), lambda n, b: (b, n))

    def body(x0_ref, g_ref, h_ref, dy_ref, cw_ref, cb_ref, wa_ref, ba_ref, wx_ref, bx_ref, lam_ref,
             du0_ref, dg_ref, gwa_ref, gwx_ref, vec_ref, gcw_ref, a_s, b_s, dh_s, edge_s):
        @pl.when(pl.program_id(1) == 0)
        def _():
            gwa_ref[...] = jnp.zeros_like(gwa_ref)
            gwx_ref[...] = jnp.zeros_like(gwx_ref)
            vec_ref[...] = jnp.zeros_like(vec_ref)
            gcw_ref[...] = jnp.zeros_like(gcw_ref)

        x0 = x0_ref[...].astype(f32)
        cw = cw_ref[...]
        lam_v = lam_ref[...]
        u, ub, r, i, sp, a, mult, inv_mult, taps = _lru_gates(x0, cw, cb_ref[...], wa_ref[...], ba_ref[...],
                                                              wx_ref[...], bx_ref[...], lam_v)
        h = h_ref[...]
        g = g_ref[...].astype(f32)
        dy = dy_ref[...].astype(f32)
        sg = _sigmoid(g)
        dg_ref[...] = (dy * h * (sg * (1.0 + g * (1.0 - sg)))).astype(bf16)
        _linear_scan(_shift_up(a, 1), dy * (g * sg), a_s, b_s, edge_s, dh_s, reverse=True)
        dh_total = dh_s[...]
        da = dh_total * _shift_down(h, 1)
        dmult = dh_total * (i * u)
        db = dh_total * mult
        di = db * u
        du = db * i
        dlog_a_c = ((-LRU_C) * a) * (da - dmult * (a * inv_mult))
        dr = dlog_a_c * sp
        dsp = jnp.sum(dlog_a_c * r, axis=0, keepdims=True)
        dpre_r = dr * r * (1.0 - r)
        dpre_i = di * i * (1.0 - i)
        dpre_rb = dpre_r.astype(bf16)
        dpre_ib = dpre_i.astype(bf16)
        du = du + _dot(dpre_rb, wa_ref[...].astype(bf16), NT) + _dot(dpre_ib, wx_ref[...].astype(bf16), NT)
        gwa_ref[...] += _dot(ub, dpre_rb, TN)
        gwx_ref[...] += _dot(ub, dpre_ib, TN)
        vec_ref[0:1, :] += jnp.sum(du, axis=0, keepdims=True)
        vec_ref[1:2, :] += jnp.sum(dpre_r, axis=0, keepdims=True)
        vec_ref[2:3, :] += jnp.sum(dpre_i, axis=0, keepdims=True)
        vec_ref[3:4, :] += dsp * (-_sigmoid(-lam_v))
        dx0 = cw[3:4, :] * du
        for k in range(3):
            dx0 = dx0 + cw[k:k + 1, :] * _shift_up(du, 3 - k)
        for k in range(4):
            gcw_ref[k:k + 1, :] += jnp.sum(du * taps[k], axis=0, keepdims=True)
        du0_ref[...] = dx0.astype(bf16)

    wacc = pl.BlockSpec((RB, RB), lambda n, b: (0, n))
    vacc = pl.BlockSpec((8, RB), lambda n, b: (0, n))
    cacc = pl.BlockSpec((8, RB), lambda n, b: (n, 0))
    return _pcall(
        body, name="lru_backward", grid=(RNN_BLOCKS, nb),
        in_specs=[col(0), col(8), tokblk, tokblk, cwblk, vec, wblk, vec, wblk, vec, vec],
        out_specs=(tokblk, tokblk, wacc, wacc, vacc, cacc),
        out_shape=(_sds((T, D), bf16), _sds((T, D), bf16), _sds((RB, D), f32), _sds((RB, D), f32),
                   _sds((8, D), f32), _sds((8 * RNN_BLOCKS, RB), f32)),
        scratch_shapes=[pltpu.VMEM((S, RB), f32)] * 3 + [pltpu.VMEM((S // 8, RB), f32)],
        compiler_params=_params(("arbitrary", "arbitrary")),
    )(proj, proj, h_all, dy_rnn, cw_full, conv_b, w_a, b_a, w_x, b_x, lam)


def _section_of_chunk(s):
    out = []
    for start, n in zip(SEC_START, SEC_CHUNKS):
        inside = (s >= start) & (s < start + n)
        out.append((inside, jnp.clip(s - start, 0, n - 1)))
    return out


def _input_grad(dsecs, wt_full, x2d, dx2, norm_g, chip_sums):
    T = x2d.shape[0]
    tb = min(T, 512)
    nsec = len(dsecs)
    nex = len(chip_sums)
    ntok = T // tb

    def body(*refs):
        secs = refs[:nsec]
        wt_ref, x_ref, dx2_ref, g_ref = refs[nsec:nsec + 4]
        ex_src = refs[nsec + 4:nsec + 4 + nex]
        dx_ref, gnorm_ref = refs[nsec + 4 + nex:nsec + 6 + nex]
        ex_dst = refs[nsec + 6 + nex:nsec + 6 + 2 * nex]
        sems = refs[nsec + 6 + 2 * nex:]
        i = pl.program_id(0)

        @pl.when(i == 0)
        def _():
            gnorm_ref[...] = jnp.zeros_like(gnorm_ref)
            _start_all(_chip_exchange_copies(ex_src, ex_dst, *sems))

        dh = None
        for a, (start, n) in enumerate(zip(SEC_START, SEC_CHUNKS)):
            part = _dot(secs[a][...], wt_ref[CH * start:CH * (start + n), :], NN)
            dh = part if dh is None else dh + part
        xv = x_ref[...]
        rstd = lax.rsqrt(jnp.mean(xv * xv, axis=-1, keepdims=True) + EPS)
        xh = xv * rstd
        gnorm_ref[0:1, :] += jnp.sum(dh * xh, axis=0, keepdims=True)
        dxn = dh * g_ref[...]
        dx_ref[...] = dx2_ref[...] + rstd * (dxn - xh * jnp.mean(dxn * xh, axis=-1, keepdims=True))

        @pl.when(i == ntok - 1)
        def _():
            _wait_all(_chip_exchange_copies(ex_src, ex_dst, *sems))

    tok = pl.BlockSpec((tb, D), lambda i: (i, 0))
    res = _pcall(
        body, name="input_grad", grid=(ntok,),
        in_specs=[pl.BlockSpec((tb, sec.shape[1]), lambda i: (i, 0)) for sec in dsecs]
        + [pl.BlockSpec((D_IN, D), lambda i: (0, 0), pipeline_mode=pl.Buffered(1)), tok, tok,
           pl.BlockSpec((1, D), lambda i: (0, 0))] + [ANY] * nex,
        out_specs=(tok, pl.BlockSpec((8, D), lambda i: (0, 0))) + tuple([ANY] * nex),
        out_shape=(_sds((T, D), f32), _sds((8, D), f32)) + tuple(_sds(c.shape, c.dtype) for c in chip_sums),
        scratch_shapes=_exchange_scratch(nex, 3),
        compiler_params=_params(("arbitrary",)),
    )(*dsecs, wt_full, x2d, dx2, norm_g, *chip_sums)
    return res[0], res[1], res[2:]


def _w_in_grad(dsecs, h_bf):
    T = h_bf.shape[0]
    tk = min(T, 2048)
    nchunks = D_IN // CH
    nsec = len(dsecs)
    nt = T // tk

    def body(*refs):
        secs = refs[:nsec]
        h_ref, out_ref, acc = refs[nsec:]
        s, t = pl.program_id(0), pl.program_id(1)

        @pl.when(t == 0)
        def _():
            acc[...] = jnp.zeros_like(acc)

        h_rows = h_ref[pl.ds(pl.multiple_of(t * tk, tk), tk), :]
        for a, (start, n) in enumerate(zip(SEC_START, SEC_CHUNKS)):
            @pl.when((s >= start) & (s < start + n))
            def _(a=a):
                acc[...] += _dot(secs[a][...], h_rows, TN)

        @pl.when(t == nt - 1)
        def _():
            out_ref[...] = acc[...].astype(bf16)

    def sec_spec(a):
        def index(s, t, a=a):
            inside, local = _section_of_chunk(s)[a]
            return (jnp.where(inside, t, 0), local)
        return pl.BlockSpec((tk, CH), index)

    return _pcall(
        body, name="w_in_grad", grid=(nchunks, T // tk),
        in_specs=[sec_spec(a) for a in range(nsec)]
        + [pl.BlockSpec((T, D), lambda s, t: (0, 0), pipeline_mode=pl.Buffered(1))],
        out_specs=pl.BlockSpec((CH, D), lambda s, t: (s, 0)), out_shape=_sds((D_IN, D), bf16),
        scratch_shapes=[pltpu.VMEM((CH, D), f32)],
        compiler_params=_params(("arbitrary", "arbitrary")),
    )(*dsecs, h_bf)


SMALL_NAMES = ("lru_w_a", "lru_w_x", "conv_b", "lru_b_a", "lru_b_x", "lru_lambda", "norm_g", "final_norm_g",
               "attn_sinks", "conv_w")
MISC_ROW = {"conv_b": 0, "lru_b_a": 1, "lru_b_x": 2, "lru_lambda": 3, "norm_g": 8, "final_norm_g": 16,
            "attn_sinks": 24, "loss": 32}


def _small_step(gwa, gwx, gvec, gnorm_blk, gfin_blk, dsink_blk, loss_blk, gcw, params):
    srcs_rows = (RB // NDEV, RB // NDEV, 8, 8)
    flat = [t for n in SMALL_NAMES for t in params[n]]
    nin = 8 + len(flat)
    nout = 4 * len(SMALL_NAMES) + 1

    def body(*refs):
        gwa_ref, gwx_ref, gvec_ref, gnorm_ref, gfin_ref, dsink_ref, loss_ref, gcw_ref = refs[:8]
        prm = {n: refs[8 + 3 * k:11 + 3 * k] for k, n in enumerate(SMALL_NAMES)}
        outs = {n: refs[nin + 4 * k:nin + 4 * k + 4] for k, n in enumerate(SMALL_NAMES)}
        loss_out = refs[nin + nout - 1]
        (misc, got_a, got_x, got_m, got_c, red_a, red_x, red_m, all_a, all_x, all_m,
         sa, ra, sb, rb) = refs[nin + nout:]
        x, y, c = _my_place()
        me = 4 * x + 2 * y + c

        misc[...] = jnp.zeros_like(misc)
        misc[0:8, :] = gvec_ref[...]
        misc[8:16, :] = gnorm_ref[...]
        misc[16:24, :] = gfin_ref[...]
        misc[24:32, 0:128] = dsink_ref[...]
        misc[32:40, :] = loss_ref[...]

        srcs = (gwa_ref, gwx_ref, misc, gcw_ref)
        gots = (got_a, got_x, got_m, got_c)

        def shard(ref, rows, dev):
            return ref.at[pl.ds(pl.multiple_of(dev * rows, 8), rows), :]

        scatter = []
        for k in range(1, NDEV):
            px, py, pc = _peer(k)
            for a in range(4):
                scatter.append(pltpu.make_async_remote_copy(
                    src_ref=shard(srcs[a], srcs_rows[a], 4 * px + 2 * py + pc), dst_ref=gots[a].at[k - 1],
                    send_sem=sa.at[4 * (k - 1) + a], recv_sem=ra.at[4 * (k - 1) + a],
                    device_id=(px, py, pc), device_id_type=MESH))
        for cp in scatter:
            cp.start()
        for cp in scatter:
            cp.wait()

        def reduced(a):
            rows = srcs_rows[a]
            total = srcs[a][pl.ds(pl.multiple_of(me * rows, 8), rows), :]
            for k in range(NDEV - 1):
                total = total + gots[a][k]
            return total

        reds = (red_a, red_x, red_m)
        alls = (all_a, all_x, all_m)
        for a in range(3):
            val = reduced(a)
            reds[a][...] = val
            alls[a][pl.ds(pl.multiple_of(me * srcs_rows[a], 8), srcs_rows[a]), :] = val
        gather = []
        for k in range(1, NDEV):
            peer = _peer(k)
            for a in range(3):
                gather.append(pltpu.make_async_remote_copy(
                    src_ref=reds[a], dst_ref=shard(alls[a], srcs_rows[a], me),
                    send_sem=sb.at[3 * (k - 1) + a], recv_sem=rb.at[3 * (k - 1) + a],
                    device_id=peer, device_id_type=MESH))
        for cp in gather:
            cp.start()
        g_conv = reduced(3)[0:4, :]
        for cp in gather:
            cp.wait()

        def update(name, g, pick=lambda r: r[...]):
            w_ref, m_ref, v_ref = prm[name]
            delta, m_new, v_new = _adam_math(g, pick(w_ref), pick(m_ref), pick(v_ref))
            return g, delta, m_new, v_new

        for n in range(RNN_BLOCKS):
            lanes = slice(RB * n, RB * (n + 1))
            for name, full in (("lru_w_a", all_a), ("lru_w_x", all_x)):
                for out, val in zip(outs[name], update(name, full[:, lanes], pick=lambda r, n=n: r[n])):
                    out[n] = val
        for name in ("conv_b", "lru_b_a", "lru_b_x", "lru_lambda", "norm_g", "final_norm_g"):
            row = MISC_ROW[name]
            for out, val in zip(outs[name], update(name, all_m[row:row + 1, :])):
                out[...] = val
        row = MISC_ROW["attn_sinks"]
        for out, val in zip(outs["attn_sinks"], update("attn_sinks", all_m[row:row + 1, 0:16])):
            out[...] = val
        for out, val in zip(outs["conv_w"], update("conv_w", g_conv)):
            out[...] = val
        row = MISC_ROW["loss"]
        loss_out[...] = all_m[row:row + 8, 0:128] * (0.5 / D)

    out_shape = tuple(_sds(params[n][0].shape, f32) for n in SMALL_NAMES for _ in range(4)) + (_sds((8, 128), f32),)
    scratch = [pltpu.VMEM((64, D), f32),
               pltpu.VMEM((NDEV - 1, RB // NDEV, D), f32), pltpu.VMEM((NDEV - 1, RB // NDEV, D), f32),
               pltpu.VMEM((NDEV - 1, 8, D), f32), pltpu.VMEM((NDEV - 1, 8, RB), f32),
               pltpu.VMEM((RB // NDEV, D), f32), pltpu.VMEM((RB // NDEV, D), f32), pltpu.VMEM((8, D), f32),
               pltpu.VMEM((RB, D), f32), pltpu.VMEM((RB, D), f32), pltpu.VMEM((64, D), f32),
               pltpu.SemaphoreType.DMA((4 * (NDEV - 1),)), pltpu.SemaphoreType.DMA((4 * (NDEV - 1),)),
               pltpu.SemaphoreType.DMA((3 * (NDEV - 1),)), pltpu.SemaphoreType.DMA((3 * (NDEV - 1),))]
    res = _pcall(
        body, name="small_step", out_shape=out_shape,
        in_specs=[VMEM_SPEC] * nin, out_specs=tuple([VMEM_SPEC] * nout),
        scratch_shapes=scratch, compiler_params=_params(),
    )(gwa, gwx, gvec, gnorm_blk, gfin_blk, dsink_blk, loss_blk, gcw, *flat)
    return {n: res[4 * k:4 * k + 4] for k, n in enumerate(SMALL_NAMES)}, res[-1]


def _pad_rows(v, rows=8):
    return jnp.concatenate([v, jnp.zeros((rows - v.shape[0], v.shape[1]), v.dtype)], axis=0)


def kernel(x, norm_g, w_in, conv_w, conv_b, lru_w_a, lru_b_a, lru_w_x, lru_b_x, lru_lambda, attn_sinks, w_rnn_out, w_attn_out, w_o, final_norm_g, loss_target, m_norm_g, m_w_in, m_conv_w, m_conv_b, m_lru_w_a, m_lru_b_a, m_lru_w_x, m_lru_b_x, m_lru_lambda, m_attn_sinks, m_w_rnn_out, m_w_attn_out, m_w_o, m_final_norm_g, v_norm_g, v_w_in, v_conv_w, v_conv_b, v_lru_w_a, v_lru_b_a, v_lru_w_x, v_lru_b_x, v_lru_lambda, v_attn_sinks, v_w_rnn_out, v_w_attn_out, v_w_o, v_final_norm_g):
    nb, S, _ = x.shape
    T = nb * S
    x2d = x.reshape(T, D)
    tgt = loss_target.reshape(T, D)
    fin_g = final_norm_g.reshape(1, D)
    w_a3, w_x3 = lru_w_a[0], lru_w_x[0]

    my_core = lax.axis_index("c").astype(jnp.int32).reshape(1)
    cx, cy = lax.axis_index("x"), lax.axis_index("y")
    chip_order = jnp.stack([2 * cx + cy, 2 * (1 - cx) + cy, 2 * cx + (1 - cy),
                            2 * (1 - cx) + (1 - cy)]).astype(jnp.int32)

    tabs = _rope_tables(S)
    h_bf, proj, wt_full, cw_full, (wr_full, wa_full, wo_full) = _in_proj_gather(
        x2d, norm_g, w_in[0].T.astype(bf16), _pad_rows(conv_w[0]), tabs, S,
        (w_rnn_out[0], w_attn_out[0], w_o[0]), chip_order)
    y_rnn, h_all = _lru_forward(proj, cw_full, conv_b, w_a3, lru_b_a, w_x3, lru_b_x, lru_lambda, S)
    y_attn = _attn_forward(proj, attn_sinks, S)

    (dx2, dy_rnn, dy_attn, dmr, dma, loss_blk, gfin_blk, g_wr, g_wa, g_wo) = _merge_and_head(
        x2d, tgt, proj, y_rnn, y_attn, wr_full, wa_full, wo_full, fin_g)
    sums_out = _pair_sums([g_wr, g_wa, g_wo], [bf16, bf16, bf16], my_core, "out")

    (dq, dkv, dga, dsink_blk), (p_wr, p_wa, p_wo) = _attn_backward(proj, dy_attn, tabs, attn_sinks, S, sums_out)
    du0, dgr, gwa, gwx, gvec, gcw = _lru_backward(proj, h_all, dy_rnn, cw_full, conv_b, w_a3, lru_b_a, w_x3,
                                                  lru_b_x, lru_lambda, S)
    dsecs = (du0, dgr, dq, dkv, dga, dmr, dma)

    g_wt = _w_in_grad(dsecs, h_bf)
    sums_in = _pair_sums([g_wt], [bf16], my_core, "in")
    grad_x2d, gnorm_blk, (p_wt,) = _input_grad(dsecs, wt_full, x2d, dx2, norm_g, sums_in)

    small, loss_out = _small_step(gwa, gwx, gvec, gnorm_blk, gfin_blk, dsink_blk, loss_blk, gcw, {
        "lru_w_a": (w_a3, m_lru_w_a[0], v_lru_w_a[0]), "lru_w_x": (w_x3, m_lru_w_x[0], v_lru_w_x[0]),
        "conv_b": (conv_b, m_conv_b, v_conv_b), "lru_b_a": (lru_b_a, m_lru_b_a, v_lru_b_a),
        "lru_b_x": (lru_b_x, m_lru_b_x, v_lru_b_x), "lru_lambda": (lru_lambda, m_lru_lambda, v_lru_lambda),
        "norm_g": (norm_g, m_norm_g, v_norm_g),
        "final_norm_g": (fin_g, m_final_norm_g.reshape(1, D), v_final_norm_g.reshape(1, D)),
        "attn_sinks": (attn_sinks, m_attn_sinks, v_attn_sinks),
        "conv_w": (conv_w[0], m_conv_w[0], v_conv_w[0])})

    o_wt = _adamw(p_wt, w_in[0].T, m_w_in[0].T, v_w_in[0].T, "adamw_w_in")
    o_wr = _adamw(p_wr, w_rnn_out[0], m_w_rnn_out[0], v_w_rnn_out[0], "adamw_w_rnn_out")
    o_wa = _adamw(p_wa, w_attn_out[0], m_w_attn_out[0], v_w_attn_out[0], "adamw_w_attn_out")
    o_wo = _adamw(p_wo, w_o[0], m_w_o[0], v_w_o[0], "adamw_w_o")

    def result(kind):
        d = {n: small[n][kind] for n in ("conv_b", "lru_b_a", "lru_b_x", "lru_lambda", "norm_g", "attn_sinks")}
        d.update({n: small[n][kind][None] for n in ("lru_w_a", "lru_w_x", "conv_w")})
        d["final_norm_g"] = small["final_norm_g"][kind].reshape(D)
        d.update({"w_in": o_wt[kind].T[None], "w_rnn_out": o_wr[kind][None], "w_attn_out": o_wa[kind][None],
                  "w_o": o_wo[kind][None]})
        return d

    order = ("norm_g", "w_in", "conv_w", "conv_b", "lru_w_a", "lru_b_a", "lru_w_x", "lru_b_x", "lru_lambda",
             "attn_sinks", "w_rnn_out", "w_attn_out", "w_o", "final_norm_g")
    outs = [loss_out[0, 0], grad_x2d.reshape(nb, S, D)]
    for kind in range(4):
        d = result(kind)
        outs += [d[n] for n in order]
    return tuple(outs)
```

```python
import functools
import math

import jax
import jax.numpy as jnp
from jax import lax
from jax.experimental import pallas as pl
from jax.experimental.pallas import tpu as pltpu

f32 = jnp.float32
bf16 = jnp.bfloat16

D = 1024
D_IN = 6656
NDEV = 8
RNN_BLOCKS = 8
RB = 128
HEAD = 64
KV_HEADS = 4
GROUP = 4
QB = 128
LRU_C = 8.0
EPS = 1e-6
ROPE_DIM = 16
ROPE_THETA = 500000.0
CH = 512
SEC_START = (0, 2, 4, 6, 7, 9, 11)
SEC_CHUNKS = (2, 2, 2, 1, 2, 2, 2)
VMEM_LIMIT = 62 * 1024 * 1024

ADAM_LR, ADAM_B1, ADAM_B2, ADAM_EPS, ADAM_WD, ADAM_STEP = 0.001, 0.9, 0.999, 1e-08, 0.01, 10

MESH = pl.DeviceIdType.MESH
ANY = pl.BlockSpec(memory_space=pl.ANY)
VMEM_SPEC = pl.BlockSpec(memory_space=pltpu.VMEM)
SMEM_SPEC = pl.BlockSpec(memory_space=pltpu.SMEM)


def _pcall(body, **kw):
    return pl.pallas_call(body, **kw)


def _params(sem=None, **kw):
    if sem is not None:
        kw["dimension_semantics"] = sem
    return pltpu.CompilerParams(vmem_limit_bytes=VMEM_LIMIT, **kw)


def _sds(shape, dtype):
    return jax.ShapeDtypeStruct(shape, dtype)


def _dot(a, b, dims):
    return lax.dot_general(a, b, (dims, ((), ())), preferred_element_type=f32)


NN = ((1,), (0,))
NT = ((1,), (1,))
TN = ((0,), (0,))


def _sigmoid(v):
    return 0.5 * jnp.tanh(0.5 * v) + 0.5


def _sigmoid_positive(v):
    return 1.0 / (1.0 + jnp.exp(-v))


def _my_place():
    return lax.axis_index("x"), lax.axis_index("y"), lax.axis_index("c")


def _peer(k):
    x, y, c = _my_place()
    return (x + ((k >> 2) & 1)) % 2, (y + ((k >> 1) & 1)) % 2, (c + (k & 1)) % 2


def _direct_gather_copies(srcs, outs, send_sems, recv_sems, local_sems):
    x, y, c = _my_place()
    me = 4 * x + 2 * y + c
    local, remote = [], []
    for a, (src, out) in enumerate(zip(srcs, outs)):
        r = src.shape[0]
        mine = out.at[pl.ds(pl.multiple_of(me * r, 8), r), :]
        local.append(pltpu.make_async_copy(src, mine, local_sems.at[a]))
        for k in range(1, NDEV):
            remote.append(pltpu.make_async_remote_copy(
                src_ref=src, dst_ref=mine, send_sem=send_sems.at[7 * a + k - 1], recv_sem=recv_sems.at[7 * a + k - 1],
                device_id=_peer(k), device_id_type=MESH))
    return local, remote


def _chip_exchange_copies(src, dst, send_sems, recv_sems, local_sems):
    x, y, c = _my_place()
    local, remote = [], []
    for a in range(len(src)):
        local.append(pltpu.make_async_copy(src[a].at[2 * x + y], dst[a].at[0], local_sems.at[a]))
    for k in (3, 1, 2):
        px, py = (x + (k >> 1)) % 2, (y + (k & 1)) % 2
        for a in range(len(src)):
            remote.append(pltpu.make_async_remote_copy(
                src_ref=src[a].at[2 * px + py], dst_ref=dst[a].at[k],
                send_sem=send_sems.at[3 * a + k - 1], recv_sem=recv_sems.at[3 * a + k - 1],
                device_id=(px, py, c), device_id_type=MESH))
    return local, remote


def _exchange_scratch(narr, per_array):
    return [pltpu.SemaphoreType.DMA((per_array * narr,)), pltpu.SemaphoreType.DMA((per_array * narr,)),
            pltpu.SemaphoreType.DMA((narr,))]


def _start_all(copies):
    local, remote = copies
    for cp in local + remote:
        cp.start()


def _wait_all(copies):
    local, remote = copies
    for cp in remote + local:
        cp.wait()


def _pair_exchange(grads, name):
    narr = len(grads)
    nrows = tuple(g.shape[0] // NDEV for g in grads)
    views = [g.reshape(4, 2, r, g.shape[1]) for g, r in zip(grads, nrows)]

    def body(*refs):
        gin = refs[:narr]
        got = refs[narr:2 * narr]
        send_sems, recv_sems = refs[2 * narr:]
        x, y, c = _my_place()
        copies = [pltpu.make_async_remote_copy(
            src_ref=gin[a].at[:, pl.ds(1 - c, 1)], dst_ref=got[a],
            send_sem=send_sems.at[a], recv_sem=recv_sems.at[a],
            device_id=(x, y, 1 - c), device_id_type=MESH) for a in range(narr)]
        for cp in copies:
            cp.start()
        for cp in copies:
            cp.wait()

    out_shape = tuple(_sds((4, 1, r, g.shape[1]), g.dtype) for r, g in zip(nrows, grads))
    got = _pcall(
        body, name=name, out_shape=out_shape,
        in_specs=[ANY] * narr, out_specs=tuple([ANY] * narr),
        scratch_shapes=[pltpu.SemaphoreType.DMA((narr,)), pltpu.SemaphoreType.DMA((narr,))],
        compiler_params=_params(),
    )(*views)
    return views, [g.reshape(4, r, g.shape[3]) for g, r in zip(got, nrows)]


def _row_tile(rows, dtype):
    unit = 16 if dtype == bf16 else 8
    for cand in (256, 208, 128, 64, 40, 32, 16, 8):
        if rows % cand == 0 and cand % unit == 0:
            return cand
    return rows


def _chip_sum(view, got, my_core, out_dtype, name):
    _, _, r, cols = view.shape
    tr = _row_tile(r, out_dtype)

    def body(core_ref, mine_ref, got_ref, out_ref):
        out_ref[...] = (mine_ref[...].astype(f32) + got_ref[...].astype(f32)).astype(out_dtype)

    grid_spec = pltpu.PrefetchScalarGridSpec(
        num_scalar_prefetch=1, grid=(4, r // tr),
        in_specs=[pl.BlockSpec((None, None, tr, cols), lambda q, i, core: (q, core[0], i, 0)),
                  pl.BlockSpec((None, tr, cols), lambda q, i, core: (q, i, 0))],
        out_specs=pl.BlockSpec((None, tr, cols), lambda q, i, core: (q, i, 0)))
    return _pcall(body, name=name, grid_spec=grid_spec, out_shape=_sds((4, r, cols), out_dtype),
                  compiler_params=_params(("arbitrary", "arbitrary")))(my_core, view, got)


def _pair_sums(grads, wire_dtypes, my_core, tag):
    views, got = _pair_exchange(grads, "pair_exchange_" + tag)
    return [_chip_sum(v, g, my_core, dt, "chip_sum_%s%d" % (tag, a))
            for a, (v, g, dt) in enumerate(zip(views, got, wire_dtypes))]


def _adam_math(g, w, m, v):
    m_new = ADAM_B1 * m + (1.0 - ADAM_B1) * g
    v_new = ADAM_B2 * v + (1.0 - ADAM_B2) * (g * g)
    m_hat = m_new / (1.0 - ADAM_B1 ** ADAM_STEP)
    v_hat = v_new / (1.0 - ADAM_B2 ** ADAM_STEP)
    return -ADAM_LR * (m_hat / (jnp.sqrt(v_hat) + ADAM_EPS) + ADAM_WD * w), m_new, v_new


def _adamw(parts, w, m, v, name, first=None):
    n, rows, cols = parts.shape
    tr = _row_tile(rows, parts.dtype)
    lead = () if first is None else (first,)

    def body(*refs):
        p_ref, w_ref, m_ref, v_ref, g_out, d_out, m_out, v_out = refs[len(lead):]
        g = refs[0][...].astype(f32) if lead else p_ref[0].astype(f32)
        for s in range(0 if lead else 1, n):
            g = g + p_ref[s].astype(f32)
        g_out[...] = g
        d_out[...], m_out[...], v_out[...] = _adam_math(g, w_ref[...], m_ref[...], v_ref[...])

    blk = pl.BlockSpec((tr, cols), lambda i: (i, 0))
    return _pcall(
        body, name=name, grid=(rows // tr,),
        in_specs=[blk] * len(lead) + [pl.BlockSpec((n, tr, cols), lambda i: (0, i, 0)), blk, blk, blk],
        out_specs=(blk, blk, blk, blk), out_shape=tuple(_sds((rows, cols), f32) for _ in range(4)),
        compiler_params=_params(("arbitrary",)),
    )(*lead, parts, w, m, v)


def _rope(t, c, s1, s2):
    w = t.shape[1]
    return t * c + pltpu.roll(t, w - 8, 1) * s1 + pltpu.roll(t, 8, 1) * s2


def _rope_transposed(dt, c, s1, s2):
    w = dt.shape[1]
    return dt * c + pltpu.roll(dt * s1, 8, 1) + pltpu.roll(dt * s2, w - 8, 1)


PAIR_ROWS = D_IN // 4
SUB_COLS = ((0, 512), (512, 512), (1024, 512), (1536, 128))
Q_SLABS = range(3, 11)
K_SLABS = range(11, 13)


def _in_proj_gather(x2d, norm_g, wt_shard, cw_shard, tabs, S, out_shards, chip_order):
    T = x2d.shape[0]
    tb = min(S, 1024)
    ntok = T // tb
    nsb = S // tb
    q_scale = 1.0 / math.sqrt(HEAD)
    shard_rows = wt_shard.shape[0]
    small = (cw_shard,) + tuple(out_shards)
    nsm = len(small)

    def body(order_ref, x_ref, g_ref, c_ref, s1_ref, s2_ref, wt_hbm, *rest):
        small_in = rest[:nsm]
        h_ref, proj_ref, wt_out = rest[nsm:nsm + 3]
        small_out = rest[nsm + 3:2 * nsm + 3]
        wt_vm, h_vm = rest[2 * nsm + 3:2 * nsm + 5]
        stage = rest[2 * nsm + 5:3 * nsm + 4]
        wsend, wrecv, wlocal = rest[3 * nsm + 4:3 * nsm + 7]
        dsems = rest[3 * nsm + 7:]
        jj, i = pl.program_id(0), pl.program_id(1)
        x, y, c = _my_place()
        me, sibling = (x, y, c), (x, y, 1 - c)
        chips = [(1 - x, y), (x, 1 - y), (1 - x, 1 - y)]

        def rows(place):
            px, py, pc = place
            return wt_vm.at[pl.ds(pl.multiple_of((4 * px + 2 * py + pc) * shard_rows, 16), shard_rows), :]

        def copy(k, block, to, src=None):
            return pltpu.make_async_remote_copy(
                src_ref=rows(block) if src is None else src, dst_ref=rows(block),
                send_sem=wsend.at[k], recv_sem=wrecv.at[k], device_id=to, device_id_type=MESH)

        def small_copies():
            srcs = (small_in[0],) + tuple(stage)
            return _direct_gather_copies(srcs, small_out, *dsems)

        own = pltpu.make_async_copy(wt_hbm, rows(me), wlocal.at[0])
        keep = pltpu.make_async_copy(wt_vm, wt_out, wlocal.at[1])

        @pl.when((jj == 0) & (i == 0))
        def _():
            own.start()
            copy(0, me, sibling, src=wt_hbm).start()
            for j, chip in enumerate(chips):
                copy(1 + j, me, (*chip, c), src=wt_hbm).start()
            for a in range(nsm - 1):
                stage[a][...] = small_in[1 + a][...].astype(bf16)
            _start_all(small_copies())
            own.wait()
            copy(0, sibling, me).wait_recv()

        for j, chip in enumerate(chips):
            @pl.when((jj == 1 + j) & (i == 0))
            def _(j=j, chip=chip):
                copy(1 + j, (*chip, c), me).wait_recv()
                copy(4 + j, (*chip, c), sibling).start()
                copy(4 + j, (*chip, 1 - c), me).wait_recv()

        @pl.when((jj == 3) & (i == 0))
        def _():
            keep.start()

        @pl.when((jj == 3) & (i == ntok - 1))
        def _():
            copy(0, me, sibling, src=wt_hbm).wait_send()
            for j, chip in enumerate(chips):
                copy(1 + j, me, (*chip, c), src=wt_hbm).wait_send()
                copy(4 + j, (*chip, c), sibling).wait_send()
            _wait_all(small_copies())
            keep.wait()

        tok = pl.ds(pl.multiple_of(i * tb, tb), tb)

        @pl.when(jj == 0)
        def _():
            xv = x_ref[...]
            ms = jnp.mean(xv * xv, axis=-1, keepdims=True)
            hb = (xv * lax.rsqrt(ms + EPS) * g_ref[...]).astype(bf16)
            h_ref[...] = hb
            h_vm[tok, :] = hb

        block = order_ref[jj]
        hb = h_vm[tok, :]

        def piece(c0, w):
            w_rows = wt_vm[pl.ds(pl.multiple_of(block * PAIR_ROWS + c0, 128), w), :]
            return _dot(hb, w_rows, NT)

        @pl.when(block != 1)
        def _():
            for c0, w in SUB_COLS:
                proj_ref[:, c0:c0 + w] = piece(c0, w).astype(bf16)

        @pl.when(block == 1)
        def _():
            tab = (c_ref[...], s1_ref[...], s2_ref[...])
            for c0, w in SUB_COLS:
                acc = piece(c0, w)
                for l in range(w // 128):
                    slab = (c0 + 128 * l) // 128
                    part = acc[:, 128 * l:128 * (l + 1)]
                    if slab in Q_SLABS:
                        part = _rope(part, *tab) * q_scale
                    elif slab in K_SLABS:
                        part = _rope(part, *tab)
                    proj_ref[:, 128 * slab:128 * (slab + 1)] = part.astype(bf16)

    first_pass = lambda jj, i, order: (jnp.where(jj == 0, i, ntok - 1), 0)
    const = lambda jj, i, order: (0, 0)
    tab = pl.BlockSpec((tb, 128), lambda jj, i, order: (jnp.where(order[jj] == 1, i % nsb, 0), 0))
    grid_spec = pltpu.PrefetchScalarGridSpec(
        num_scalar_prefetch=1, grid=(4, ntok),
        in_specs=[pl.BlockSpec((tb, D), first_pass), pl.BlockSpec((1, D), const), tab, tab, tab, ANY]
        + [pl.BlockSpec(w.shape, const) for w in small],
        out_specs=(pl.BlockSpec((tb, D), first_pass),
                   pl.BlockSpec((tb, PAIR_ROWS), lambda jj, i, order: (i, order[jj])), ANY) + tuple([ANY] * nsm),
        scratch_shapes=[pltpu.VMEM((D_IN, D), bf16), pltpu.VMEM((T, D), bf16)]
        + [pltpu.VMEM(w.shape, bf16) for w in out_shards]
        + [pltpu.SemaphoreType.DMA((7,)), pltpu.SemaphoreType.DMA((7,)), pltpu.SemaphoreType.DMA((2,))]
        + _exchange_scratch(nsm, 7))
    res = _pcall(
        body, name="in_proj", grid_spec=grid_spec,
        out_shape=(_sds((T, D), bf16), _sds((T, D_IN), bf16), _sds((D_IN, D), bf16),
                   _sds((NDEV * cw_shard.shape[0], cw_shard.shape[1]), f32))
        + tuple(_sds((NDEV * w.shape[0], w.shape[1]), bf16) for w in out_shards),
        compiler_params=_params(("arbitrary", "arbitrary")),
    )(chip_order, x2d, norm_g, *tabs, wt_shard, *small)
    return res[0], res[1], res[2], res[3], res[4:]


def _rows_iota(shape):
    return lax.broadcasted_iota(jnp.int32, shape, 0)


def _shift_down(v, k):
    return jnp.where(_rows_iota(v.shape) >= k, pltpu.roll(v, k, 0), 0.0)


def _shift_up(v, k):
    n = v.shape[0]
    return jnp.where(_rows_iota(v.shape) < n - k, pltpu.roll(v, n - k, 0), 0.0)


def _linear_scan(a, b, a_s, b_s, edge_s, out_ref, reverse):
    n = a.shape[0]
    ng = n // 8
    a3, b3 = a.reshape(ng, 8, RB), b.reshape(ng, 8, RB)
    rid = lax.broadcasted_iota(jnp.int32, a3.shape, 1)
    for s in (1, 2, 4):
        keep, shift = (rid < 8 - s, 8 - s) if reverse else (rid >= s, s)
        b3 = jnp.where(keep, a3 * pltpu.roll(b3, shift, 1) + b3, b3)
        a3 = jnp.where(keep, a3 * pltpu.roll(a3, shift, 1), a3)
    a_s[...] = a3.reshape(n, RB)
    b_s[...] = b3.reshape(n, RB)
    edge = 0 if reverse else 7
    ea, eb = a_s[pl.ds(edge, ng, stride=8), :], b_s[pl.ds(edge, ng, stride=8), :]
    r = _rows_iota(ea.shape)
    s = 1
    while s < ng:
        keep, shift = (r < ng - s, ng - s) if reverse else (r >= s, s)
        eb = jnp.where(keep, ea * pltpu.roll(eb, shift, 0) + eb, eb)
        if 2 * s < ng:
            ea = jnp.where(keep, ea * pltpu.roll(ea, shift, 0), ea)
        s *= 2
    edge_s[...] = _shift_up(eb, 1) if reverse else _shift_down(eb, 1)

    def eight_groups(i, carry):
        for k in range(8):
            j = i * 8 + k
            rows = pl.ds(pl.multiple_of(j * 8, 8), 8)
            out_ref[rows, :] = b_s[rows, :] + a_s[rows, :] * edge_s[pl.ds(j, 1), :]
        return carry

    lax.fori_loop(0, ng // 8, eight_groups, 0)


def _neg_expm1(v):
    series = -v * (1.0 + v * (0.5 + v * (1.0 / 6.0)))
    return jnp.where(v > -0.015625, series, 1.0 - jnp.exp(v))


def _softplus_neg(lam):
    return jnp.maximum(-lam, 0.0) + jnp.log(1.0 + jnp.exp(-jnp.abs(lam)))


def _lru_gates(x0, cw, cb, wa, ba, wx, bx, lam):
    taps = [_shift_down(x0, 3 - k) for k in range(3)] + [x0]
    u = cb + cw[3:4, :] * x0
    for k in range(3):
        u = u + cw[k:k + 1, :] * taps[k]
    ub = u.astype(bf16)
    r = _sigmoid_positive(_dot(ub, wa.astype(bf16), NN) + ba)
    i = _sigmoid(_dot(ub, wx.astype(bf16), NN) + bx)
    sp = _softplus_neg(lam)
    log_a = (-LRU_C) * r * sp
    a = jnp.exp(log_a)
    w = _neg_expm1(2.0 * log_a)
    inv_mult = lax.rsqrt(w)
    return u, ub, r, i, sp, a, w * inv_mult, inv_mult, taps


def _lru_specs(S, nb):
    col = lambda off: pl.BlockSpec((S, RB), lambda n, b, off=off: (b, off + n))
    vec = pl.BlockSpec((1, RB), lambda n, b: (0, n))
    wblk = pl.BlockSpec((None, RB, RB), lambda n, b: (n, 0, 0))
    cwblk = pl.BlockSpec((8, RB), lambda n, b: (n, 0))
    return col, vec, wblk, cwblk


def _lru_forward(proj, cw_full, conv_b, w_a, b_a, w_x, b_x, lam, S):
    T = proj.shape[0]
    nb = T // S
    col, vec, wblk, cwblk = _lru_specs(S, nb)

    def body(x0_ref, g_ref, cw_ref, cb_ref, wa_ref, ba_ref, wx_ref, bx_ref, lam_ref, y_ref, h_ref, a_s, b_s, edge_s):
        x0 = x0_ref[...].astype(f32)
        u, ub, r, i, sp, a, mult, _, _ = _lru_gates(x0, cw_ref[...], cb_ref[...], wa_ref[...], ba_ref[...],
                                                    wx_ref[...], bx_ref[...], lam_ref[...])
        _linear_scan(a, mult * (i * u), a_s, b_s, edge_s, h_ref, reverse=False)
        g = g_ref[...].astype(f32)
        y_ref[...] = (h_ref[...] * (g * _sigmoid(g))).astype(bf16)

    out = pl.BlockSpec((S, RB), lambda n, b: (b, n))
    return _pcall(
        body, name="lru_forward", grid=(RNN_BLOCKS, nb),
        in_specs=[col(0), col(8), cwblk, vec, wblk, vec, wblk, vec, vec],
        out_specs=(out, out), out_shape=(_sds((T, D), bf16), _sds((T, D), f32)),
        scratch_shapes=[pltpu.VMEM((S, RB), f32), pltpu.VMEM((S, RB), f32), pltpu.VMEM((S // 8, RB), f32)],
        compiler_params=_params(("arbitrary", "arbitrary")),
    )(proj, proj, cw_full, conv_b, w_a, b_a, w_x, b_x, lam)


def _rope_tables(S):
    pos = jnp.arange(S, dtype=f32)
    inv_freq = ROPE_THETA ** (-jnp.arange(0, ROPE_DIM, 2, dtype=f32) / ROPE_DIM)
    ang = pos[:, None] * inv_freq[None, :]
    cos, sin = jnp.cos(ang), jnp.sin(ang)
    lane = jnp.arange(128) % HEAD
    cosl, sinl = cos[:, lane % 8], sin[:, lane % 8]
    c = jnp.where(lane[None, :] < ROPE_DIM, cosl, 1.0)
    s1 = jnp.where(lane[None, :] < 8, -sinl, 0.0)
    s2 = jnp.where((lane[None, :] >= 8) & (lane[None, :] < ROPE_DIM), sinl, 0.0)
    return c.astype(f32), s1.astype(f32), s2.astype(f32)


def _heads_to_rows(t):
    return jnp.concatenate([t[:, HEAD * h:HEAD * (h + 1)] for h in range(GROUP)], axis=0)


def _rows_to_heads(t):
    return jnp.concatenate([t[QB * h:QB * (h + 1), :] for h in range(GROUP)], axis=1)


def _window_bias(first_block):
    shape = (GROUP * QB, 2 * QB)
    qi = _rows_iota(shape) % QB
    cj = lax.broadcasted_iota(jnp.int32, shape, 1)
    valid = (cj > qi) & (cj <= qi + QB) & ((cj >= QB) | jnp.logical_not(first_block))
    return jnp.where(valid, 0.0, -jnp.inf)


def _attn_probs(q_rows, k_cat, sink_col, bias):
    s = _dot(q_rows, k_cat, NT) + bias
    m = jnp.maximum(jnp.max(s, axis=1, keepdims=True), sink_col)
    p = jnp.exp(s - m)
    e_sink = jnp.exp(sink_col - m)
    inv = 1.0 / (jnp.sum(p, axis=1, keepdims=True) + e_sink)
    return p * inv, e_sink * inv


def _sink_column(sink_ref, kv):
    rid = _rows_iota((GROUP * QB, 1))
    col = jnp.zeros((GROUP * QB, 1), f32)
    for h in range(GROUP):
        col = jnp.where(rid // QB == h, sink_ref[0, GROUP * kv + h], col)
    return col


def _attn_in_specs(S):
    nq = S // QB
    last = nq - 1
    cur = lambda b, j: b * nq + jnp.minimum(j, last)
    prev = lambda b, j: b * nq + jnp.maximum(jnp.minimum(j, last) - 1, 0)
    specs = [
        pl.BlockSpec((QB, D), lambda b, j: (cur(b, j), 2)),
        pl.BlockSpec((QB, 256), lambda b, j: (cur(b, j), 12)),
        pl.BlockSpec((QB, 256), lambda b, j: (prev(b, j), 12)),
        pl.BlockSpec((QB, 256), lambda b, j: (cur(b, j), 13)),
        pl.BlockSpec((QB, 256), lambda b, j: (prev(b, j), 13)),
        pl.BlockSpec((QB, 512), lambda b, j: (cur(b, j), 7)),
        pl.BlockSpec((QB, 512), lambda b, j: (cur(b, j), 8)),
        SMEM_SPEC,
    ]
    return specs, cur, prev


def _attn_forward(proj, sinks, S):
    T = proj.shape[0]
    nb, nq = T // S, S // QB
    specs, cur, _ = _attn_in_specs(S)

    def body(q_ref, kc_ref, kp_ref, vc_ref, vp_ref, gl_ref, gh_ref, sink_ref, y_ref):
        bias = _window_bias(pl.program_id(1) == 0)
        kc, kp, vc, vp = kc_ref[...], kp_ref[...], vc_ref[...], vp_ref[...]
        for kv in range(KV_HEADS):
            lanes = slice(256 * kv, 256 * (kv + 1))
            hl = slice(HEAD * kv, HEAD * (kv + 1))
            q_rows = _heads_to_rows(q_ref[:, lanes])
            k_cat = jnp.concatenate([kp[:, hl], kc[:, hl]], axis=0)
            v_cat = jnp.concatenate([vp[:, hl], vc[:, hl]], axis=0)
            probs, _ = _attn_probs(q_rows, k_cat, _sink_column(sink_ref, kv), bias)
            o = _rows_to_heads(_dot(probs.astype(bf16), v_cat, NN))
            g_src = gl_ref if kv < 2 else gh_ref
            g = g_src[:, 256 * (kv % 2):256 * (kv % 2 + 1)].astype(f32)
            y_ref[:, lanes] = (o * (g * _sigmoid(g))).astype(bf16)

    args = [proj] * 7 + [sinks]
    return _pcall(
        body, name="attn_forward", grid=(nb, nq), in_specs=specs,
        out_specs=pl.BlockSpec((QB, D), lambda b, j: (cur(b, j), 0)), out_shape=_sds((T, D), bf16),
        compiler_params=_params(("arbitrary", "arbitrary")),
    )(*args)


def _merge_and_head(x2d, tgt, proj, y_rnn, y_attn, w_r, w_a, w_o, gfin):
    T = x2d.shape[0]
    tb = min(T, 512)
    nsteps = T // tb

    def body(x_ref, t_ref, mr0, mr1, ma0, ma1, yr_ref, ya_ref, wr_ref, wa_ref, wo_ref, gf_ref,
             dx2_ref, dyr_ref, dya_ref, dmr_ref, dma_ref, loss_ref, gfin_ref, gwr_out, gwa_out, gwo_out,
             gwr_acc, gwa_acc, gwo_acc, out_sems):
        step = pl.program_id(0)

        @pl.when(step == 0)
        def _():
            loss_ref[...] = jnp.zeros_like(loss_ref)
            gfin_ref[...] = jnp.zeros_like(gfin_ref)
            gwr_acc[...] = jnp.zeros_like(gwr_acc)
            gwa_acc[...] = jnp.zeros_like(gwa_acc)
            gwo_acc[...] = jnp.zeros_like(gwo_acc)

        sr = _sigmoid(jnp.concatenate([mr0[...], mr1[...]], axis=1).astype(f32))
        sa = _sigmoid(jnp.concatenate([ma0[...], ma1[...]], axis=1).astype(f32))
        p_r = _dot(yr_ref[...], wr_ref[...], NN)
        p_a = _dot(ya_ref[...], wa_ref[...], NN)
        merged = (sr * p_r + sa * p_a).astype(bf16)
        x2 = x_ref[...] + _dot(merged, wo_ref[...], NN)
        rstd = lax.rsqrt(jnp.mean(x2 * x2, axis=-1, keepdims=True) + EPS)
        xh = x2 * rstd
        gf = gf_ref[...]
        err = xh * gf - t_ref[...]
        loss_ref[...] += jnp.sum(err * err)
        dy = err * (1.0 / D)
        gfin_ref[0:1, :] += jnp.sum(dy * xh, axis=0, keepdims=True)
        dxn = dy * gf
        dx2 = rstd * (dxn - xh * jnp.mean(dxn * xh, axis=-1, keepdims=True))
        dx2_ref[...] = dx2
        dx2b = dx2.astype(bf16)
        dmerged = _dot(dx2b, wo_ref[...], NT)
        dmr_ref[...] = (dmerged * p_r * (sr * (1.0 - sr))).astype(bf16)
        dma_ref[...] = (dmerged * p_a * (sa * (1.0 - sa))).astype(bf16)
        dpr = (dmerged * sr).astype(bf16)
        dpa = (dmerged * sa).astype(bf16)
        dyr_ref[...] = _dot(dpr, wr_ref[...], NT).astype(bf16)
        dya_ref[...] = _dot(dpa, wa_ref[...], NT).astype(bf16)
        gwr_acc[...] += _dot(yr_ref[...], dpr, TN)
        gwa_acc[...] += _dot(ya_ref[...], dpa, TN)
        gwo_acc[...] += _dot(merged, dx2b, TN)

        @pl.when(step == nsteps - 1)
        def _():
            copies = [pltpu.make_async_copy(src, dst, out_sems.at[k]) for k, (src, dst) in enumerate(
                ((gwr_acc, gwr_out), (gwa_acc, gwa_out), (gwo_acc, gwo_out)))]
            for cp in copies:
                cp.start()
            for cp in copies:
                cp.wait()

    tok = pl.BlockSpec((tb, D), lambda i: (i, 0))
    half = lambda c: pl.BlockSpec((tb, CH), lambda i, c=c: (i, c))
    wfull = pl.BlockSpec((D, D), lambda i: (0, 0), pipeline_mode=pl.Buffered(1))
    acc = pl.BlockSpec((8, D), lambda i: (0, 0))
    return _pcall(
        body, name="merge_and_head", grid=(nsteps,),
        in_specs=[tok, tok, half(9), half(10), half(11), half(12), tok, tok, wfull, wfull, wfull,
                  pl.BlockSpec((1, D), lambda i: (0, 0))],
        out_specs=(tok, tok, tok, tok, tok, acc, acc, ANY, ANY, ANY),
        out_shape=(_sds((T, D), f32), _sds((T, D), bf16), _sds((T, D), bf16), _sds((T, D), bf16),
                   _sds((T, D), bf16), _sds((8, D), f32), _sds((8, D), f32),
                   _sds((D, D), f32), _sds((D, D), f32), _sds((D, D), f32)),
        scratch_shapes=[pltpu.VMEM((D, D), f32)] * 3 + [pltpu.SemaphoreType.DMA((3,))],
        compiler_params=_params(("arbitrary",)),
    )(x2d, tgt, proj, proj, proj, proj, y_rnn, y_attn, w_r, w_a, w_o, gfin)


def _attn_backward(proj, dy_attn, tabs, sinks, S, chip_sums):
    T = proj.shape[0]
    nb, nq = T // S, S // QB
    nex = len(chip_sums)
    specs, cur, prev = _attn_in_specs(S)
    last = nq - 1
    tab_cur = pl.BlockSpec((QB, 128), lambda b, j: (jnp.minimum(j, last), 0))
    tab_prev = pl.BlockSpec((QB, 128), lambda b, j: (jnp.maximum(jnp.minimum(j, last) - 1, 0), 0))
    specs = specs + [pl.BlockSpec((QB, D), lambda b, j: (cur(b, j), 0))] + [tab_cur] * 3 + [tab_prev] * 3
    q_scale = 1.0 / math.sqrt(HEAD)

    def rope_back(dt, tab):
        return jnp.concatenate([_rope_transposed(dt[:, 128 * l:128 * (l + 1)], *tab) for l in range(2)], axis=1)

    def body(q_ref, kc_ref, kp_ref, vc_ref, vp_ref, gl_ref, gh_ref, sink_ref, dy_ref, cc, s1c, s2c, cp, s1p, s2p,
             *rest):
        ex_src = rest[:nex]
        dq_ref, dkv_ref, dg_ref, dsink_ref = rest[nex:nex + 4]
        ex_dst = rest[nex + 4:2 * nex + 4]
        carry_k, carry_v = rest[2 * nex + 4:2 * nex + 6]
        sems = rest[2 * nex + 6:]
        b, j = pl.program_id(0), pl.program_id(1)

        @pl.when((b == 0) & (j == 0))
        def _():
            dsink_ref[...] = jnp.zeros_like(dsink_ref)
            _start_all(_chip_exchange_copies(ex_src, ex_dst, *sems))

        @pl.when((b == nb - 1) & (j == nq))
        def _():
            _wait_all(_chip_exchange_copies(ex_src, ex_dst, *sems))

        @pl.when(j == 0)
        def _():
            carry_k[...] = jnp.zeros_like(carry_k)
            carry_v[...] = jnp.zeros_like(carry_v)

        @pl.when(j < nq)
        def _():
            bias = _window_bias(j == 0)
            tc = (cc[...], s1c[...], s2c[...])
            tp = (cp[...], s1p[...], s2p[...])
            kc, kp, vc, vp = kc_ref[...], kp_ref[...], vc_ref[...], vp_ref[...]
            dk_prev, dk_cur, dv_prev, dv_cur = [], [], [], []
            dsink_acc = jnp.zeros((8, 128), f32)
            r8 = lax.broadcasted_iota(jnp.int32, (8, 128), 0)
            l8 = lax.broadcasted_iota(jnp.int32, (8, 128), 1)
            for kv in range(KV_HEADS):
                lanes = slice(256 * kv, 256 * (kv + 1))
                hl = slice(HEAD * kv, HEAD * (kv + 1))
                q_rows = _heads_to_rows(q_ref[:, lanes])
                k_cat = jnp.concatenate([kp[:, hl], kc[:, hl]], axis=0)
                v_cat = jnp.concatenate([vp[:, hl], vc[:, hl]], axis=0)
                probs, p_sink = _attn_probs(q_rows, k_cat, _sink_column(sink_ref, kv), bias)
                pb = probs.astype(bf16)
                o = _rows_to_heads(_dot(pb, v_cat, NN))
                g_src = gl_ref if kv < 2 else gh_ref
                g = g_src[:, 256 * (kv % 2):256 * (kv % 2 + 1)].astype(f32)
                sg = _sigmoid(g)
                dy = dy_ref[:, lanes].astype(f32)
                dg_ref[:, lanes] = (dy * o * (sg * (1.0 + g * (1.0 - sg)))).astype(bf16)
                do_rows = _heads_to_rows(dy * (g * sg)).astype(bf16)
                dv = _dot(pb, do_rows, TN)
                dp = _dot(do_rows, v_cat, NT)
                rowdot = jnp.sum(probs * dp, axis=1, keepdims=True)
                ds = (probs * (dp - rowdot)).astype(bf16)
                sink_rows = -(p_sink * rowdot)
                for h in range(GROUP):
                    val = jnp.sum(sink_rows[QB * h:QB * (h + 1), :])
                    dsink_acc = dsink_acc + jnp.where((r8 == 0) & (l8 == GROUP * kv + h), val, 0.0)
                dq = _rows_to_heads(_dot(ds, k_cat, NN)) * q_scale
                dq_ref[:, lanes] = rope_back(dq, tc).astype(bf16)
                dk = _dot(ds, q_rows, TN)
                dk_prev.append(dk[:QB, :])
                dk_cur.append(dk[QB:, :])
                dv_prev.append(dv[:QB, :])
                dv_cur.append(dv[QB:, :])
            dsink_ref[...] += dsink_acc
            dkp = rope_back(jnp.concatenate(dk_prev, axis=1), tp)
            dkc = rope_back(jnp.concatenate(dk_cur, axis=1), tc)
            dkv_ref[:, 0:256] = (carry_k[...] + dkp).astype(bf16)
            dkv_ref[:, 256:512] = (carry_v[...] + jnp.concatenate(dv_prev, axis=1)).astype(bf16)
            carry_k[...] = dkc
            carry_v[...] = jnp.concatenate(dv_cur, axis=1)

        @pl.when(j == nq)
        def _():
            dkv_ref[:, 0:256] = carry_k[...].astype(bf16)
            dkv_ref[:, 256:512] = carry_v[...].astype(bf16)

    lag = lambda b, j: (b * nq + jnp.maximum(j - 1, 0), 0)
    args = [proj] * 7 + [sinks, dy_attn] + list(tabs) + list(tabs) + list(chip_sums)
    res = _pcall(
        body, name="attn_backward", grid=(nb, nq + 1), in_specs=specs + [ANY] * nex,
        out_specs=(pl.BlockSpec((QB, D), lambda b, j: (cur(b, j), 0)), pl.BlockSpec((QB, 512), lag),
                   pl.BlockSpec((QB, D), lambda b, j: (cur(b, j), 0)), pl.BlockSpec((8, 128), lambda b, j: (0, 0)))
        + tuple([ANY] * nex),
        out_shape=(_sds((T, D), bf16), _sds((T, 512), bf16), _sds((T, D), bf16), _sds((8, 128), f32))
        + tuple(_sds(s.shape, s.dtype) for s in chip_sums),
        scratch_shapes=[pltpu.VMEM((QB, 256), f32), pltpu.VMEM((QB, 256), f32)] + _exchange_scratch(nex, 3),
        compiler_params=_params(("arbitrary", "arbitrary")),
    )(*args)
    return res[:4], res[4:]


def _lru_backward(proj, h_all, dy_rnn, cw_full, conv_b, w_a, b_a, w_x, b_x, lam, S):
    T = proj.shape[0]
    nb = T // S
    col, vec, wblk, cwblk = _lru_specs(S, nb)
    tokblk = pl.BlockSpec((S, RB), lambda n, b: (b, n))

    def body(x0_ref, g_ref, h_ref, dy_ref, cw_ref, cb_ref, wa_ref, ba_ref, wx_ref, bx_ref, lam_ref,
             du0_ref, dg_ref, gwa_ref, gwx_ref, vec_ref, gcw_ref, a_s, b_s, dh_s, edge_s):
        @pl.when(pl.program_id(1) == 0)
        def _():
            gwa_ref[...] = jnp.zeros_like(gwa_ref)
            gwx_ref[...] = jnp.zeros_like(gwx_ref)
            vec_ref[...] = jnp.zeros_like(vec_ref)
            gcw_ref[...] = jnp.zeros_like(gcw_ref)

        x0 = x0_ref[...].astype(f32)
        cw = cw_ref[...]
        lam_v = lam_ref[...]
        u, ub, r, i, sp, a, mult, inv_mult, taps = _lru_gates(x0, cw, cb_ref[...], wa_ref[...], ba_ref[...],
                                                              wx_ref[...], bx_ref[...], lam_v)
        h = h_ref[...]
        g = g_ref[...].astype(f32)
        dy = dy_ref[...].astype(f32)
        sg = _sigmoid(g)
        dg_ref[...] = (dy * h * (sg * (1.0 + g * (1.0 - sg)))).astype(bf16)
        _linear_scan(_shift_up(a, 1), dy * (g * sg), a_s, b_s, edge_s, dh_s, reverse=True)
        dh_total = dh_s[...]
        da = dh_total * _shift_down(h, 1)
        dmult = dh_total * (i * u)
        db = dh_total * mult
        di = db * u
        du = db * i
        dlog_a_c = ((-LRU_C) * a) * (da - dmult * (a * inv_mult))
        dr = dlog_a_c * sp
        dsp = jnp.sum(dlog_a_c * r, axis=0, keepdims=True)
        dpre_r = dr * r * (1.0 - r)
        dpre_i = di * i * (1.0 - i)
        dpre_rb = dpre_r.astype(bf16)
        dpre_ib = dpre_i.astype(bf16)
        du = du + _dot(dpre_rb, wa_ref[...].astype(bf16), NT) + _dot(dpre_ib, wx_ref[...].astype(bf16), NT)
        gwa_ref[...] += _dot(ub, dpre_rb, TN)
        gwx_ref[...] += _dot(ub, dpre_ib, TN)
        vec_ref[0:1, :] += jnp.sum(du, axis=0, keepdims=True)
        vec_ref[1:2, :] += jnp.sum(dpre_r, axis=0, keepdims=True)
        vec_ref[2:3, :] += jnp.sum(dpre_i, axis=0, keepdims=True)
        vec_ref[3:4, :] += dsp * (-_sigmoid(-lam_v))
        dx0 = cw[3:4, :] * du
        for k in range(3):
            dx0 = dx0 + cw[k:k + 1, :] * _shift_up(du, 3 - k)
        for k in range(4):
            gcw_ref[k:k + 1, :] += jnp.sum(du * taps[k], axis=0, keepdims=True)
        du0_ref[...] = dx0.astype(bf16)

    wacc = pl.BlockSpec((RB, RB), lambda n, b: (0, n))
    vacc = pl.BlockSpec((8, RB), lambda n, b: (0, n))
    cacc = pl.BlockSpec((8, RB), lambda n, b: (n, 0))
    return _pcall(
        body, name="lru_backward", grid=(RNN_BLOCKS, nb),
        in_specs=[col(0), col(8), tokblk, tokblk, cwblk, vec, wblk, vec, wblk, vec, vec],
        out_specs=(tokblk, tokblk, wacc, wacc, vacc, cacc),
        out_shape=(_sds((T, D), bf16), _sds((T, D), bf16), _sds((RB, D), f32), _sds((RB, D), f32),
                   _sds((8, D), f32), _sds((8 * RNN_BLOCKS, RB), f32)),
        scratch_shapes=[pltpu.VMEM((S, RB), f32)] * 3 + [pltpu.VMEM((S // 8, RB), f32)],
        compiler_params=_params(("arbitrary", "arbitrary")),
    )(proj, proj, h_all, dy_rnn, cw_full, conv_b, w_a, b_a, w_x, b_x, lam)


def _section_of_chunk(s):
    out = []
    for start, n in zip(SEC_START, SEC_CHUNKS):
        inside = (s >= start) & (s < start + n)
        out.append((inside, jnp.clip(s - start, 0, n - 1)))
    return out


EFFECT = pltpu.SideEffectType.DATAFLOW_SIDE_EFFECTING
HBM_SPEC = pl.BlockSpec(memory_space=pltpu.HBM)
SEM_SPEC = pl.BlockSpec(memory_space=pltpu.SEMAPHORE)


def _split_exchange_copies(src_ref, land_ref, send_sems, recv_sems):
    x, y, c = _my_place()
    copies = []
    for k in (3, 1, 2):
        px, py = (x + (k >> 1)) % 2, (y + (k & 1)) % 2
        copies.append(pltpu.make_async_remote_copy(
            src_ref=src_ref.at[2 * px + py], dst_ref=land_ref.at[k - 1], send_sem=send_sems[k - 1],
            recv_sem=recv_sems[k - 1], device_id=(px, py, c), device_id_type=MESH))
    return copies


def _exchange_start(chip_sum):
    _, r, cols = chip_sum.shape

    def body(src_ref, land_ref, s0, s1, s2, r0, r1, r2, src_thru, land_thru, token):
        for cp in _split_exchange_copies(src_ref, land_ref, (s0, s1, s2), (r0, r1, r2)):
            cp.start()
        token[...] = jnp.zeros_like(token)

    land = pltpu.with_memory_space_constraint(lax.empty((3, r, cols), chip_sum.dtype), pltpu.HBM)
    res = _pcall(
        body, name="exchange_start",
        out_shape=tuple([pltpu.SemaphoreType.DMA(())] * 6) + (
            pltpu.HBM(chip_sum.shape, chip_sum.dtype), pltpu.HBM((3, r, cols), chip_sum.dtype), _sds((8, 128), f32)),
        in_specs=(HBM_SPEC, HBM_SPEC), out_specs=tuple([SEM_SPEC] * 6) + (HBM_SPEC, HBM_SPEC, VMEM_SPEC),
        input_output_aliases={0: 6, 1: 7},
        compiler_params=pltpu.CompilerParams(has_side_effects=EFFECT),
    )(pltpu.with_memory_space_constraint(chip_sum, pltpu.HBM), land)
    return res[:6], res[6], res[7], res[8]


def _exchange_wait(sems, src_thru, land_thru, after):
    def body(src_ref, land_ref, s0, s1, s2, r0, r1, r2, after_ref, src_dead, got_ref):
        for cp in _split_exchange_copies(src_ref, land_ref, (s0, s1, s2), (r0, r1, r2)):
            cp.wait_send()
            cp.wait_recv()

    return _pcall(
        body, name="exchange_wait",
        out_shape=(pltpu.HBM(src_thru.shape, src_thru.dtype), pltpu.HBM(land_thru.shape, land_thru.dtype)),
        in_specs=(HBM_SPEC, HBM_SPEC) + tuple([SEM_SPEC] * 6) + (ANY,), out_specs=(HBM_SPEC, HBM_SPEC),
        input_output_aliases={0: 0, 1: 1},
        compiler_params=pltpu.CompilerParams(has_side_effects=EFFECT),
    )(src_thru, land_thru, *sems, after)[1]


def _input_grad(dsecs, wt_full, x2d, dx2, norm_g):
    T = x2d.shape[0]
    tb = min(T, 512)
    nsec = len(dsecs)
    ntok = T // tb

    def body(*refs):
        secs = refs[:nsec]
        wt_ref, x_ref, dx2_ref, g_ref, dx_ref, gnorm_ref = refs[nsec:]
        i = pl.program_id(0)

        @pl.when(i == 0)
        def _():
            gnorm_ref[...] = jnp.zeros_like(gnorm_ref)

        dh = None
        for a, (start, n) in enumerate(zip(SEC_START, SEC_CHUNKS)):
            part = _dot(secs[a][...], wt_ref[CH * start:CH * (start + n), :], NN)
            dh = part if dh is None else dh + part
        xv = x_ref[...]
        rstd = lax.rsqrt(jnp.mean(xv * xv, axis=-1, keepdims=True) + EPS)
        xh = xv * rstd
        gnorm_ref[0:1, :] += jnp.sum(dh * xh, axis=0, keepdims=True)
        dxn = dh * g_ref[...]
        dx_ref[...] = dx2_ref[...] + rstd * (dxn - xh * jnp.mean(dxn * xh, axis=-1, keepdims=True))

    tok = pl.BlockSpec((tb, D), lambda i: (i, 0))
    return _pcall(
        body, name="input_grad", grid=(ntok,),
        in_specs=[pl.BlockSpec((tb, sec.shape[1]), lambda i: (i, 0)) for sec in dsecs]
        + [pl.BlockSpec((D_IN, D), lambda i: (0, 0), pipeline_mode=pl.Buffered(1)), tok, tok,
           pl.BlockSpec((1, D), lambda i: (0, 0))],
        out_specs=(tok, pl.BlockSpec((8, D), lambda i: (0, 0))),
        out_shape=(_sds((T, D), f32), _sds((8, D), f32)),
        compiler_params=_params(("arbitrary",)),
    )(*dsecs, wt_full, x2d, dx2, norm_g)


def _w_in_grad(dsecs, h_bf):
    T = h_bf.shape[0]
    tk = min(T, 2048)
    nchunks = D_IN // CH
    nsec = len(dsecs)
    nt = T // tk

    def body(*refs):
        secs = refs[:nsec]
        h_ref, out_ref, acc = refs[nsec:]
        s, t = pl.program_id(0), pl.program_id(1)

        @pl.when(t == 0)
        def _():
            acc[...] = jnp.zeros_like(acc)

        h_rows = h_ref[pl.ds(pl.multiple_of(t * tk, tk), tk), :]
        for a, (start, n) in enumerate(zip(SEC_START, SEC_CHUNKS)):
            @pl.when((s >= start) & (s < start + n))
            def _(a=a):
                acc[...] += _dot(secs[a][...], h_rows, TN)

        @pl.when(t == nt - 1)
        def _():
            out_ref[...] = acc[...].astype(bf16)

    def sec_spec(a):
        def index(s, t, a=a):
            inside, local = _section_of_chunk(s)[a]
            return (jnp.where(inside, t, 0), local)
        return pl.BlockSpec((tk, CH), index)

    return _pcall(
        body, name="w_in_grad", grid=(nchunks, T // tk),
        in_specs=[sec_spec(a) for a in range(nsec)]
        + [pl.BlockSpec((T, D), lambda s, t: (0, 0), pipeline_mode=pl.Buffered(1))],
        out_specs=pl.BlockSpec((CH, D), lambda s, t: (s, 0)), out_shape=_sds((D_IN, D), bf16),
        scratch_shapes=[pltpu.VMEM((CH, D), f32)],
        compiler_params=_params(("arbitrary", "arbitrary")),
    )(*dsecs, h_bf)


SMALL_NAMES = ("lru_w_a", "lru_w_x", "conv_b", "lru_b_a", "lru_b_x", "lru_lambda", "norm_g", "final_norm_g",
               "attn_sinks", "conv_w")
MISC_ROW = {"conv_b": 0, "lru_b_a": 1, "lru_b_x": 2, "lru_lambda": 3, "norm_g": 8, "final_norm_g": 16,
            "attn_sinks": 24, "loss": 32}


def _small_step(gwa, gwx, gvec, gnorm_blk, gfin_blk, dsink_blk, loss_blk, gcw, params):
    srcs_rows = (RB // NDEV, RB // NDEV, 8, 8)
    flat = [t for n in SMALL_NAMES for t in params[n]]
    nin = 8 + len(flat)
    nout = 4 * len(SMALL_NAMES) + 1

    def body(*refs):
        gwa_ref, gwx_ref, gvec_ref, gnorm_ref, gfin_ref, dsink_ref, loss_ref, gcw_ref = refs[:8]
        prm = {n: refs[8 + 3 * k:11 + 3 * k] for k, n in enumerate(SMALL_NAMES)}
        outs = {n: refs[nin + 4 * k:nin + 4 * k + 4] for k, n in enumerate(SMALL_NAMES)}
        loss_out = refs[nin + nout - 1]
        (misc, got_a, got_x, got_m, got_c, red_a, red_x, red_m, all_a, all_x, all_m,
         sa, ra, sb, rb) = refs[nin + nout:]
        x, y, c = _my_place()
        me = 4 * x + 2 * y + c

        misc[...] = jnp.zeros_like(misc)
        misc[0:8, :] = gvec_ref[...]
        misc[8:16, :] = gnorm_ref[...]
        misc[16:24, :] = gfin_ref[...]
        misc[24:32, 0:128] = dsink_ref[...]
        misc[32:40, :] = loss_ref[...]

        srcs = (gwa_ref, gwx_ref, misc, gcw_ref)
        gots = (got_a, got_x, got_m, got_c)

        def shard(ref, rows, dev):
            return ref.at[pl.ds(pl.multiple_of(dev * rows, 8), rows), :]

        scatter = []
        for k in range(1, NDEV):
            px, py, pc = _peer(k)
            for a in range(4):
                scatter.append(pltpu.make_async_remote_copy(
                    src_ref=shard(srcs[a], srcs_rows[a], 4 * px + 2 * py + pc), dst_ref=gots[a].at[k - 1],
                    send_sem=sa.at[4 * (k - 1) + a], recv_sem=ra.at[4 * (k - 1) + a],
                    device_id=(px, py, pc), device_id_type=MESH))
        for cp in scatter:
            cp.start()
        for cp in scatter:
            cp.wait()

        def reduced(a):
            rows = srcs_rows[a]
            total = srcs[a][pl.ds(pl.multiple_of(me * rows, 8), rows), :]
            for k in range(NDEV - 1):
                total = total + gots[a][k]
            return total

        reds = (red_a, red_x, red_m)
        alls = (all_a, all_x, all_m)
        for a in range(3):
            val = reduced(a)
            reds[a][...] = val
            alls[a][pl.ds(pl.multiple_of(me * srcs_rows[a], 8), srcs_rows[a]), :] = val
        gather = []
        for k in range(1, NDEV):
            peer = _peer(k)
            for a in range(3):
                gather.append(pltpu.make_async_remote_copy(
                    src_ref=reds[a], dst_ref=shard(alls[a], srcs_rows[a], me),
                    send_sem=sb.at[3 * (k - 1) + a], recv_sem=rb.at[3 * (k - 1) + a],
                    device_id=peer, device_id_type=MESH))
        for cp in gather:
            cp.start()
        g_conv = reduced(3)[0:4, :]
        for cp in gather:
            cp.wait()

        def update(name, g, pick=lambda r: r[...]):
            w_ref, m_ref, v_ref = prm[name]
            delta, m_new, v_new = _adam_math(g, pick(w_ref), pick(m_ref), pick(v_ref))
            return g, delta, m_new, v_new

        for n in range(RNN_BLOCKS):
            lanes = slice(RB * n, RB * (n + 1))
            for name, full in (("lru_w_a", all_a), ("lru_w_x", all_x)):
                for out, val in zip(outs[name], update(name, full[:, lanes], pick=lambda r, n=n: r[n])):
                    out[n] = val
        for name in ("conv_b", "lru_b_a", "lru_b_x", "lru_lambda", "norm_g", "final_norm_g"):
            row = MISC_ROW[name]
            for out, val in zip(outs[name], update(name, all_m[row:row + 1, :])):
                out[...] = val
        row = MISC_ROW["attn_sinks"]
        for out, val in zip(outs["attn_sinks"], update("attn_sinks", all_m[row:row + 1, 0:16])):
            out[...] = val
        for out, val in zip(outs["conv_w"], update("conv_w", g_conv)):
            out[...] = val
        row = MISC_ROW["loss"]
        loss_out[...] = all_m[row:row + 8, 0:128] * (0.5 / D)

    out_shape = tuple(_sds(params[n][0].shape, f32) for n in SMALL_NAMES for _ in range(4)) + (_sds((8, 128), f32),)
    scratch = [pltpu.VMEM((64, D), f32),
               pltpu.VMEM((NDEV - 1, RB // NDEV, D), f32), pltpu.VMEM((NDEV - 1, RB // NDEV, D), f32),
               pltpu.VMEM((NDEV - 1, 8, D), f32), pltpu.VMEM((NDEV - 1, 8, RB), f32),
               pltpu.VMEM((RB // NDEV, D), f32), pltpu.VMEM((RB // NDEV, D), f32), pltpu.VMEM((8, D), f32),
               pltpu.VMEM((RB, D), f32), pltpu.VMEM((RB, D), f32), pltpu.VMEM((64, D), f32),
               pltpu.SemaphoreType.DMA((4 * (NDEV - 1),)), pltpu.SemaphoreType.DMA((4 * (NDEV - 1),)),
               pltpu.SemaphoreType.DMA((3 * (NDEV - 1),)), pltpu.SemaphoreType.DMA((3 * (NDEV - 1),))]
    res = _pcall(
        body, name="small_step", out_shape=out_shape,
        in_specs=[VMEM_SPEC] * nin, out_specs=tuple([VMEM_SPEC] * nout),
        scratch_shapes=scratch, compiler_params=_params(),
    )(gwa, gwx, gvec, gnorm_blk, gfin_blk, dsink_blk, loss_blk, gcw, *flat)
    return {n: res[4 * k:4 * k + 4] for k, n in enumerate(SMALL_NAMES)}, res[-1]


def _pad_rows(v, rows=8):
    return jnp.concatenate([v, jnp.zeros((rows - v.shape[0], v.shape[1]), v.dtype)], axis=0)


def kernel(x, norm_g, w_in, conv_w, conv_b, lru_w_a, lru_b_a, lru_w_x, lru_b_x, lru_lambda, attn_sinks, w_rnn_out, w_attn_out, w_o, final_norm_g, loss_target, m_norm_g, m_w_in, m_conv_w, m_conv_b, m_lru_w_a, m_lru_b_a, m_lru_w_x, m_lru_b_x, m_lru_lambda, m_attn_sinks, m_w_rnn_out, m_w_attn_out, m_w_o, m_final_norm_g, v_norm_g, v_w_in, v_conv_w, v_conv_b, v_lru_w_a, v_lru_b_a, v_lru_w_x, v_lru_b_x, v_lru_lambda, v_attn_sinks, v_w_rnn_out, v_w_attn_out, v_w_o, v_final_norm_g):
    nb, S, _ = x.shape
    T = nb * S
    x2d = x.reshape(T, D)
    tgt = loss_target.reshape(T, D)
    fin_g = final_norm_g.reshape(1, D)
    w_a3, w_x3 = lru_w_a[0], lru_w_x[0]

    my_core = lax.axis_index("c").astype(jnp.int32).reshape(1)
    cx, cy = lax.axis_index("x"), lax.axis_index("y")
    chip_order = jnp.stack([2 * cx + cy, 2 * (1 - cx) + cy, 2 * cx + (1 - cy),
                            2 * (1 - cx) + (1 - cy)]).astype(jnp.int32)

    tabs = _rope_tables(S)
    h_bf, proj, wt_full, cw_full, (wr_full, wa_full, wo_full) = _in_proj_gather(
        x2d, norm_g, w_in[0].T.astype(bf16), _pad_rows(conv_w[0]), tabs, S,
        (w_rnn_out[0], w_attn_out[0], w_o[0]), chip_order)
    y_rnn, h_all = _lru_forward(proj, cw_full, conv_b, w_a3, lru_b_a, w_x3, lru_b_x, lru_lambda, S)
    y_attn = _attn_forward(proj, attn_sinks, S)

    (dx2, dy_rnn, dy_attn, dmr, dma, loss_blk, gfin_blk, g_wr, g_wa, g_wo) = _merge_and_head(
        x2d, tgt, proj, y_rnn, y_attn, wr_full, wa_full, wo_full, fin_g)
    sums_out = _pair_sums([g_wr, g_wa, g_wo], [bf16, bf16, bf16], my_core, "out")

    (dq, dkv, dga, dsink_blk), (p_wr, p_wa, p_wo) = _attn_backward(proj, dy_attn, tabs, attn_sinks, S, sums_out)
    du0, dgr, gwa, gwx, gvec, gcw = _lru_backward(proj, h_all, dy_rnn, cw_full, conv_b, w_a3, lru_b_a, w_x3,
                                                  lru_b_x, lru_lambda, S)
    dsecs = (du0, dgr, dq, dkv, dga, dmr, dma)

    g_wt = _w_in_grad(dsecs, h_bf)
    (sum_in,) = _pair_sums([g_wt], [bf16], my_core, "in")
    ex_sems, sum_in, landing, token = _exchange_start(sum_in)
    grad_x2d, gnorm_blk = _input_grad(dsecs, wt_full, x2d, dx2, norm_g + token[0, 0])
    p_wt = _exchange_wait(ex_sems, sum_in, landing, gnorm_blk)
    p_wt_own = lax.dynamic_index_in_dim(sum_in, 2 * cx + cy, axis=0, keepdims=False)

    small, loss_out = _small_step(gwa, gwx, gvec, gnorm_blk, gfin_blk, dsink_blk, loss_blk, gcw, {
        "lru_w_a": (w_a3, m_lru_w_a[0], v_lru_w_a[0]), "lru_w_x": (w_x3, m_lru_w_x[0], v_lru_w_x[0]),
        "conv_b": (conv_b, m_conv_b, v_conv_b), "lru_b_a": (lru_b_a, m_lru_b_a, v_lru_b_a),
        "lru_b_x": (lru_b_x, m_lru_b_x, v_lru_b_x), "lru_lambda": (lru_lambda, m_lru_lambda, v_lru_lambda),
        "norm_g": (norm_g, m_norm_g, v_norm_g),
        "final_norm_g": (fin_g, m_final_norm_g.reshape(1, D), v_final_norm_g.reshape(1, D)),
        "attn_sinks": (attn_sinks, m_attn_sinks, v_attn_sinks),
        "conv_w": (conv_w[0], m_conv_w[0], v_conv_w[0])})

    o_wt = _adamw(p_wt, w_in[0].T, m_w_in[0].T, v_w_in[0].T, "adamw_w_in", first=p_wt_own)
    o_wr = _adamw(p_wr, w_rnn_out[0], m_w_rnn_out[0], v_w_rnn_out[0], "adamw_w_rnn_out")
    o_wa = _adamw(p_wa, w_attn_out[0], m_w_attn_out[0], v_w_attn_out[0], "adamw_w_attn_out")
    o_wo = _adamw(p_wo, w_o[0], m_w_o[0], v_w_o[0], "adamw_w_o")

    def result(kind):
        d = {n: small[n][kind] for n in ("conv_b", "lru_b_a", "lru_b_x", "lru_lambda", "norm_g", "attn_sinks")}
        d.update({n: small[n][kind][None] for n in ("lru_w_a", "lru_w_x", "conv_w")})
        d["final_norm_g"] = small["final_norm_g"][kind].reshape(D)
        d.update({"w_in": o_wt[kind].T[None], "w_rnn_out": o_wr[kind][None], "w_attn_out": o_wa[kind][None],
                  "w_o": o_wo[kind][None]})
        return d

    order = ("norm_g", "w_in", "conv_w", "conv_b", "lru_w_a", "lru_b_a", "lru_w_x", "lru_b_x", "lru_lambda",
             "attn_sinks", "w_rnn_out", "w_attn_out", "w_o", "final_norm_g")
    outs = [loss_out[0, 0], grad_x2d.reshape(nb, S, D)]
    for kind in range(4):
        d = result(kind)
        outs += [d[n] for n in order]
    return tuple(outs)
```

```python
import functools
import math

import jax
import jax.numpy as jnp
from jax import lax
from jax.experimental import pallas as pl
from jax.experimental.pallas import tpu as pltpu

f32 = jnp.float32
bf16 = jnp.bfloat16

D = 1024
D_IN = 6656
NDEV = 8
RNN_BLOCKS = 8
RB = 128
HEAD = 64
KV_HEADS = 4
GROUP = 4
QB = 128
LRU_C = 8.0
EPS = 1e-6
ROPE_DIM = 16
ROPE_THETA = 500000.0
CH = 512
SEC_START = (0, 2, 4, 6, 7, 9, 11)
SEC_CHUNKS = (2, 2, 2, 1, 2, 2, 2)
VMEM_LIMIT = 62 * 1024 * 1024

ADAM_LR, ADAM_B1, ADAM_B2, ADAM_EPS, ADAM_WD, ADAM_STEP = 0.001, 0.9, 0.999, 1e-08, 0.01, 10

MESH = pl.DeviceIdType.MESH
ANY = pl.BlockSpec(memory_space=pl.ANY)
VMEM_SPEC = pl.BlockSpec(memory_space=pltpu.VMEM)
SMEM_SPEC = pl.BlockSpec(memory_space=pltpu.SMEM)


def _pcall(body, **kw):
    return pl.pallas_call(body, **kw)


def _params(sem=None, **kw):
    if sem is not None:
        kw["dimension_semantics"] = sem
    return pltpu.CompilerParams(vmem_limit_bytes=VMEM_LIMIT, **kw)


def _sds(shape, dtype):
    return jax.ShapeDtypeStruct(shape, dtype)


def _dot(a, b, dims):
    return lax.dot_general(a, b, (dims, ((), ())), preferred_element_type=f32)


NN = ((1,), (0,))
NT = ((1,), (1,))
TN = ((0,), (0,))


def _sigmoid(v):
    return 0.5 * jnp.tanh(0.5 * v) + 0.5


def _sigmoid_positive(v):
    return 1.0 / (1.0 + jnp.exp(-v))


def _my_place():
    return lax.axis_index("x"), lax.axis_index("y"), lax.axis_index("c")


def _peer(k):
    x, y, c = _my_place()
    return (x + ((k >> 2) & 1)) % 2, (y + ((k >> 1) & 1)) % 2, (c + (k & 1)) % 2


def _direct_gather_copies(srcs, outs, send_sems, recv_sems, local_sems):
    x, y, c = _my_place()
    me = 4 * x + 2 * y + c
    local, remote = [], []
    for a, (src, out) in enumerate(zip(srcs, outs)):
        r = src.shape[0]
        mine = out.at[pl.ds(pl.multiple_of(me * r, 8), r), :]
        local.append(pltpu.make_async_copy(src, mine, local_sems.at[a]))
        for k in range(1, NDEV):
            remote.append(pltpu.make_async_remote_copy(
                src_ref=src, dst_ref=mine, send_sem=send_sems.at[7 * a + k - 1], recv_sem=recv_sems.at[7 * a + k - 1],
                device_id=_peer(k), device_id_type=MESH))
    return local, remote


def _chip_exchange_copies(src, dst, send_sems, recv_sems, local_sems):
    x, y, c = _my_place()
    local, remote = [], []
    for a in range(len(src)):
        local.append(pltpu.make_async_copy(src[a].at[2 * x + y], dst[a].at[0], local_sems.at[a]))
    for k in (3, 1, 2):
        px, py = (x + (k >> 1)) % 2, (y + (k & 1)) % 2
        for a in range(len(src)):
            remote.append(pltpu.make_async_remote_copy(
                src_ref=src[a].at[2 * px + py], dst_ref=dst[a].at[k],
                send_sem=send_sems.at[3 * a + k - 1], recv_sem=recv_sems.at[3 * a + k - 1],
                device_id=(px, py, c), device_id_type=MESH))
    return local, remote


def _exchange_scratch(narr, per_array):
    return [pltpu.SemaphoreType.DMA((per_array * narr,)), pltpu.SemaphoreType.DMA((per_array * narr,)),
            pltpu.SemaphoreType.DMA((narr,))]


def _start_all(copies):
    local, remote = copies
    for cp in local + remote:
        cp.start()


def _wait_all(copies):
    local, remote = copies
    for cp in remote + local:
        cp.wait()


def _pair_exchange(grads, name):
    narr = len(grads)
    nrows = tuple(g.shape[0] // NDEV for g in grads)
    views = [g.reshape(4, 2, r, g.shape[1]) for g, r in zip(grads, nrows)]

    def body(*refs):
        gin = refs[:narr]
        got = refs[narr:2 * narr]
        send_sems, recv_sems = refs[2 * narr:]
        x, y, c = _my_place()
        copies = [pltpu.make_async_remote_copy(
            src_ref=gin[a].at[:, pl.ds(1 - c, 1)], dst_ref=got[a],
            send_sem=send_sems.at[a], recv_sem=recv_sems.at[a],
            device_id=(x, y, 1 - c), device_id_type=MESH) for a in range(narr)]
        for cp in copies:
            cp.start()
        for cp in copies:
            cp.wait()

    out_shape = tuple(_sds((4, 1, r, g.shape[1]), g.dtype) for r, g in zip(nrows, grads))
    got = _pcall(
        body, name=name, out_shape=out_shape,
        in_specs=[ANY] * narr, out_specs=tuple([ANY] * narr),
        scratch_shapes=[pltpu.SemaphoreType.DMA((narr,)), pltpu.SemaphoreType.DMA((narr,))],
        compiler_params=_params(),
    )(*views)
    return views, [g.reshape(4, r, g.shape[3]) for g, r in zip(got, nrows)]


def _row_tile(rows, dtype):
    unit = 16 if dtype == bf16 else 8
    for cand in (256, 208, 128, 64, 40, 32, 16, 8):
        if rows % cand == 0 and cand % unit == 0:
            return cand
    return rows


def _chip_sum(view, got, my_core, out_dtype, name):
    _, _, r, cols = view.shape
    tr = _row_tile(r, out_dtype)

    def body(core_ref, mine_ref, got_ref, out_ref):
        out_ref[...] = (mine_ref[...].astype(f32) + got_ref[...].astype(f32)).astype(out_dtype)

    grid_spec = pltpu.PrefetchScalarGridSpec(
        num_scalar_prefetch=1, grid=(4, r // tr),
        in_specs=[pl.BlockSpec((None, None, tr, cols), lambda q, i, core: (q, core[0], i, 0)),
                  pl.BlockSpec((None, tr, cols), lambda q, i, core: (q, i, 0))],
        out_specs=pl.BlockSpec((None, tr, cols), lambda q, i, core: (q, i, 0)))
    return _pcall(body, name=name, grid_spec=grid_spec, out_shape=_sds((4, r, cols), out_dtype),
                  compiler_params=_params(("arbitrary", "arbitrary")))(my_core, view, got)


def _pair_sums(grads, wire_dtypes, my_core, tag):
    views, got = _pair_exchange(grads, "pair_exchange_" + tag)
    return [_chip_sum(v, g, my_core, dt, "chip_sum_%s%d" % (tag, a))
            for a, (v, g, dt) in enumerate(zip(views, got, wire_dtypes))]


def _adam_math(g, w, m, v):
    m_new = ADAM_B1 * m + (1.0 - ADAM_B1) * g
    v_new = ADAM_B2 * v + (1.0 - ADAM_B2) * (g * g)
    m_hat = m_new / (1.0 - ADAM_B1 ** ADAM_STEP)
    v_hat = v_new / (1.0 - ADAM_B2 ** ADAM_STEP)
    return -ADAM_LR * (m_hat / (jnp.sqrt(v_hat) + ADAM_EPS) + ADAM_WD * w), m_new, v_new


def _adamw(parts, w, m, v, name, first=None):
    n, rows, cols = parts.shape
    tr = _row_tile(rows, parts.dtype)
    lead = () if first is None else (first,)

    def body(*refs):
        p_ref, w_ref, m_ref, v_ref, g_out, d_out, m_out, v_out = refs[len(lead):]
        g = refs[0][...].astype(f32) if lead else p_ref[0].astype(f32)
        for s in range(0 if lead else 1, n):
            g = g + p_ref[s].astype(f32)
        g_out[...] = g
        d_out[...], m_out[...], v_out[...] = _adam_math(g, w_ref[...], m_ref[...], v_ref[...])

    blk = pl.BlockSpec((tr, cols), lambda i: (i, 0))
    return _pcall(
        body, name=name, grid=(rows // tr,),
        in_specs=[blk] * len(lead) + [pl.BlockSpec((n, tr, cols), lambda i: (0, i, 0)), blk, blk, blk],
        out_specs=(blk, blk, blk, blk), out_shape=tuple(_sds((rows, cols), f32) for _ in range(4)),
        compiler_params=_params(("arbitrary",)),
    )(*lead, parts, w, m, v)


def _rope(t, c, s1, s2):
    w = t.shape[1]
    return t * c + pltpu.roll(t, w - 8, 1) * s1 + pltpu.roll(t, 8, 1) * s2


def _rope_transposed(dt, c, s1, s2):
    w = dt.shape[1]
    return dt * c + pltpu.roll(dt * s1, 8, 1) + pltpu.roll(dt * s2, w - 8, 1)


PAIR_ROWS = D_IN // 4
SUB_COLS = ((0, 512), (512, 512), (1024, 512), (1536, 128))
Q_SLABS = range(3, 11)
K_SLABS = range(11, 13)


def _in_proj_gather(x2d, norm_g, wt_shard, cw_shard, tabs, S, out_shards, chip_order):
    T = x2d.shape[0]
    tb = min(S, 1024)
    ntok = T // tb
    nsb = S // tb
    q_scale = 1.0 / math.sqrt(HEAD)
    shard_rows = wt_shard.shape[0]
    small = (cw_shard,) + tuple(out_shards)
    nsm = len(small)

    def body(order_ref, x_ref, g_ref, c_ref, s1_ref, s2_ref, wt_hbm, *rest):
        small_in = rest[:nsm]
        h_ref, proj_ref, wt_out = rest[nsm:nsm + 3]
        small_out = rest[nsm + 3:2 * nsm + 3]
        wt_vm, h_vm = rest[2 * nsm + 3:2 * nsm + 5]
        stage = rest[2 * nsm + 5:3 * nsm + 4]
        wsend, wrecv, wlocal = rest[3 * nsm + 4:3 * nsm + 7]
        dsems = rest[3 * nsm + 7:]
        jj, i = pl.program_id(0), pl.program_id(1)
        x, y, c = _my_place()
        me, sibling = (x, y, c), (x, y, 1 - c)
        chips = [(1 - x, y), (x, 1 - y), (1 - x, 1 - y)]

        def rows(place):
            px, py, pc = place
            return wt_vm.at[pl.ds(pl.multiple_of((4 * px + 2 * py + pc) * shard_rows, 16), shard_rows), :]

        def copy(k, block, to, src=None):
            return pltpu.make_async_remote_copy(
                src_ref=rows(block) if src is None else src, dst_ref=rows(block),
                send_sem=wsend.at[k], recv_sem=wrecv.at[k], device_id=to, device_id_type=MESH)

        def small_copies():
            srcs = (small_in[0],) + tuple(stage)
            return _direct_gather_copies(srcs, small_out, *dsems)

        own = pltpu.make_async_copy(wt_hbm, rows(me), wlocal.at[0])
        keep = pltpu.make_async_copy(wt_vm, wt_out, wlocal.at[1])

        @pl.when((jj == 0) & (i == 0))
        def _():
            own.start()
            copy(0, me, sibling, src=wt_hbm).start()
            for j, chip in enumerate(chips):
                copy(1 + j, me, (*chip, c), src=wt_hbm).start()
            for a in range(nsm - 1):
                stage[a][...] = small_in[1 + a][...].astype(bf16)
            _start_all(small_copies())
            own.wait()
            copy(0, sibling, me).wait_recv()

        for j, chip in enumerate(chips):
            @pl.when((jj == 1 + j) & (i == 0))
            def _(j=j, chip=chip):
                copy(1 + j, (*chip, c), me).wait_recv()
                copy(4 + j, (*chip, c), sibling).start()
                copy(4 + j, (*chip, 1 - c), me).wait_recv()

        @pl.when((jj == 3) & (i == 0))
        def _():
            keep.start()

        @pl.when((jj == 3) & (i == ntok - 1))
        def _():
            copy(0, me, sibling, src=wt_hbm).wait_send()
            for j, chip in enumerate(chips):
                copy(1 + j, me, (*chip, c), src=wt_hbm).wait_send()
                copy(4 + j, (*chip, c), sibling).wait_send()
            _wait_all(small_copies())
            keep.wait()

        tok = pl.ds(pl.multiple_of(i * tb, tb), tb)

        @pl.when(jj == 0)
        def _():
            xv = x_ref[...]
            ms = jnp.mean(xv * xv, axis=-1, keepdims=True)
            hb = (xv * lax.rsqrt(ms + EPS) * g_ref[...]).astype(bf16)
            h_ref[...] = hb
            h_vm[tok, :] = hb

        block = order_ref[jj]
        hb = h_vm[tok, :]

        def piece(c0, w):
            w_rows = wt_vm[pl.ds(pl.multiple_of(block * PAIR_ROWS + c0, 128), w), :]
            return _dot(hb, w_rows, NT)

        @pl.when(block != 1)
        def _():
            for c0, w in SUB_COLS:
                proj_ref[:, c0:c0 + w] = piece(c0, w).astype(bf16)

        @pl.when(block == 1)
        def _():
            tab = (c_ref[...], s1_ref[...], s2_ref[...])
            for c0, w in SUB_COLS:
                acc = piece(c0, w)
                for l in range(w // 128):
                    slab = (c0 + 128 * l) // 128
                    part = acc[:, 128 * l:128 * (l + 1)]
                    if slab in Q_SLABS:
                        part = _rope(part, *tab) * q_scale
                    elif slab in K_SLABS:
                        part = _rope(part, *tab)
                    proj_ref[:, 128 * slab:128 * (slab + 1)] = part.astype(bf16)

    first_pass = lambda jj, i, order: (jnp.where(jj == 0, i, ntok - 1), 0)
    const = lambda jj, i, order: (0, 0)
    tab = pl.BlockSpec((tb, 128), lambda jj, i, order: (jnp.where(order[jj] == 1, i % nsb, 0), 0))
    grid_spec = pltpu.PrefetchScalarGridSpec(
        num_scalar_prefetch=1, grid=(4, ntok),
        in_specs=[pl.BlockSpec((tb, D), first_pass), pl.BlockSpec((1, D), const), tab, tab, tab, ANY]
        + [pl.BlockSpec(w.shape, const) for w in small],
        out_specs=(pl.BlockSpec((tb, D), first_pass),
                   pl.BlockSpec((tb, PAIR_ROWS), lambda jj, i, order: (i, order[jj])), ANY) + tuple([ANY] * nsm),
        scratch_shapes=[pltpu.VMEM((D_IN, D), bf16), pltpu.VMEM((T, D), bf16)]
        + [pltpu.VMEM(w.shape, bf16) for w in out_shards]
        + [pltpu.SemaphoreType.DMA((7,)), pltpu.SemaphoreType.DMA((7,)), pltpu.SemaphoreType.DMA((2,))]
        + _exchange_scratch(nsm, 7))
    res = _pcall(
        body, name="in_proj", grid_spec=grid_spec,
        out_shape=(_sds((T, D), bf16), _sds((T, D_IN), bf16), _sds((D_IN, D), bf16),
                   _sds((NDEV * cw_shard.shape[0], cw_shard.shape[1]), f32))
        + tuple(_sds((NDEV * w.shape[0], w.shape[1]), bf16) for w in out_shards),
        compiler_params=_params(("arbitrary", "arbitrary")),
    )(chip_order, x2d, norm_g, *tabs, wt_shard, *small)
    return res[0], res[1], res[2], res[3], res[4:]


def _rows_iota(shape):
    return lax.broadcasted_iota(jnp.int32, shape, 0)


def _shift_down(v, k):
    return jnp.where(_rows_iota(v.shape) >= k, pltpu.roll(v, k, 0), 0.0)


def _shift_up(v, k):
    n = v.shape[0]
    return jnp.where(_rows_iota(v.shape) < n - k, pltpu.roll(v, n - k, 0), 0.0)


def _linear_scan(a, b, a_s, b_s, edge_s, out_ref, reverse):
    n = a.shape[0]
    ng = n // 8
    a3, b3 = a.reshape(ng, 8, RB), b.reshape(ng, 8, RB)
    rid = lax.broadcasted_iota(jnp.int32, a3.shape, 1)
    for s in (1, 2, 4):
        keep, shift = (rid < 8 - s, 8 - s) if reverse else (rid >= s, s)
        b3 = jnp.where(keep, a3 * pltpu.roll(b3, shift, 1) + b3, b3)
        a3 = jnp.where(keep, a3 * pltpu.roll(a3, shift, 1), a3)
    a_s[...] = a3.reshape(n, RB)
    b_s[...] = b3.reshape(n, RB)
    edge = 0 if reverse else 7
    ea, eb = a_s[pl.ds(edge, ng, stride=8), :], b_s[pl.ds(edge, ng, stride=8), :]
    r = _rows_iota(ea.shape)
    s = 1
    while s < ng:
        keep, shift = (r < ng - s, ng - s) if reverse else (r >= s, s)
        eb = jnp.where(keep, ea * pltpu.roll(eb, shift, 0) + eb, eb)
        if 2 * s < ng:
            ea = jnp.where(keep, ea * pltpu.roll(ea, shift, 0), ea)
        s *= 2
    edge_s[...] = _shift_up(eb, 1) if reverse else _shift_down(eb, 1)

    def eight_groups(i, carry):
        for k in range(8):
            j = i * 8 + k
            rows = pl.ds(pl.multiple_of(j * 8, 8), 8)
            out_ref[rows, :] = b_s[rows, :] + a_s[rows, :] * edge_s[pl.ds(j, 1), :]
        return carry

    lax.fori_loop(0, ng // 8, eight_groups, 0)


def _neg_expm1(v):
    series = -v * (1.0 + v * (0.5 + v * (1.0 / 6.0)))
    return jnp.where(v > -0.015625, series, 1.0 - jnp.exp(v))


def _softplus_neg(lam):
    return jnp.maximum(-lam, 0.0) + jnp.log(1.0 + jnp.exp(-jnp.abs(lam)))


def _lru_gates(x0, cw, cb, wa, ba, wx, bx, lam):
    taps = [_shift_down(x0, 3 - k) for k in range(3)] + [x0]
    u = cb + cw[3:4, :] * x0
    for k in range(3):
        u = u + cw[k:k + 1, :] * taps[k]
    ub = u.astype(bf16)
    r = _sigmoid_positive(_dot(ub, wa.astype(bf16), NN) + ba)
    i = _sigmoid(_dot(ub, wx.astype(bf16), NN) + bx)
    sp = _softplus_neg(lam)
    log_a = (-LRU_C) * r * sp
    a = jnp.exp(log_a)
    w = _neg_expm1(2.0 * log_a)
    inv_mult = lax.rsqrt(w)
    return u, ub, r, i, sp, a, w * inv_mult, inv_mult, taps


def _lru_specs(S, nb):
    col = lambda off: pl.BlockSpec((S, RB), lambda n, b, off=off: (b, off + n))
    vec = pl.BlockSpec((1, RB), lambda n, b: (0, n))
    wblk = pl.BlockSpec((None, RB, RB), lambda n, b: (n, 0, 0))
    cwblk = pl.BlockSpec((8, RB), lambda n, b: (n, 0))
    return col, vec, wblk, cwblk


def _lru_forward(proj, cw_full, conv_b, w_a, b_a, w_x, b_x, lam, S):
    T = proj.shape[0]
    nb = T // S
    col, vec, wblk, cwblk = _lru_specs(S, nb)

    def body(x0_ref, g_ref, cw_ref, cb_ref, wa_ref, ba_ref, wx_ref, bx_ref, lam_ref, y_ref, h_ref, a_s, b_s, edge_s):
        x0 = x0_ref[...].astype(f32)
        u, ub, r, i, sp, a, mult, _, _ = _lru_gates(x0, cw_ref[...], cb_ref[...], wa_ref[...], ba_ref[...],
                                                    wx_ref[...], bx_ref[...], lam_ref[...])
        _linear_scan(a, mult * (i * u), a_s, b_s, edge_s, h_ref, reverse=False)
        g = g_ref[...].astype(f32)
        y_ref[...] = (h_ref[...] * (g * _sigmoid(g))).astype(bf16)

    out = pl.BlockSpec((S, RB), lambda n, b: (b, n))
    return _pcall(
        body, name="lru_forward", grid=(RNN_BLOCKS, nb),
        in_specs=[col(0), col(8), cwblk, vec, wblk, vec, wblk, vec, vec],
        out_specs=(out, out), out_shape=(_sds((T, D), bf16), _sds((T, D), f32)),
        scratch_shapes=[pltpu.VMEM((S, RB), f32), pltpu.VMEM((S, RB), f32), pltpu.VMEM((S // 8, RB), f32)],
        compiler_params=_params(("arbitrary", "arbitrary")),
    )(proj, proj, cw_full, conv_b, w_a, b_a, w_x, b_x, lam)


def _rope_tables(S):
    pos = jnp.arange(S, dtype=f32)
    inv_freq = ROPE_THETA ** (-jnp.arange(0, ROPE_DIM, 2, dtype=f32) / ROPE_DIM)
    ang = pos[:, None] * inv_freq[None, :]
    cos, sin = jnp.cos(ang), jnp.sin(ang)
    lane = jnp.arange(128) % HEAD
    cosl, sinl = cos[:, lane % 8], sin[:, lane % 8]
    c = jnp.where(lane[None, :] < ROPE_DIM, cosl, 1.0)
    s1 = jnp.where(lane[None, :] < 8, -sinl, 0.0)
    s2 = jnp.where((lane[None, :] >= 8) & (lane[None, :] < ROPE_DIM), sinl, 0.0)
    return c.astype(f32), s1.astype(f32), s2.astype(f32)


def _heads_to_rows(t):
    return jnp.concatenate([t[:, HEAD * h:HEAD * (h + 1)] for h in range(GROUP)], axis=0)


def _rows_to_heads(t):
    return jnp.concatenate([t[QB * h:QB * (h + 1), :] for h in range(GROUP)], axis=1)


def _window_bias(first_block):
    shape = (GROUP * QB, 2 * QB)
    qi = _rows_iota(shape) % QB
    cj = lax.broadcasted_iota(jnp.int32, shape, 1)
    valid = (cj > qi) & (cj <= qi + QB) & ((cj >= QB) | jnp.logical_not(first_block))
    return jnp.where(valid, 0.0, -jnp.inf)


def _attn_probs(q_rows, k_cat, sink_col, bias):
    s = _dot(q_rows, k_cat, NT) + bias
    m = jnp.maximum(jnp.max(s, axis=1, keepdims=True), sink_col)
    p = jnp.exp(s - m)
    e_sink = jnp.exp(sink_col - m)
    inv = 1.0 / (jnp.sum(p, axis=1, keepdims=True) + e_sink)
    return p * inv, e_sink * inv


def _sink_column(sink_ref, kv):
    rid = _rows_iota((GROUP * QB, 1))
    col = jnp.zeros((GROUP * QB, 1), f32)
    for h in range(GROUP):
        col = jnp.where(rid // QB == h, sink_ref[0, GROUP * kv + h], col)
    return col


def _attn_in_specs(S):
    nq = S // QB
    last = nq - 1
    cur = lambda b, j: b * nq + jnp.minimum(j, last)
    prev = lambda b, j: b * nq + jnp.maximum(jnp.minimum(j, last) - 1, 0)
    specs = [
        pl.BlockSpec((QB, D), lambda b, j: (cur(b, j), 2)),
        pl.BlockSpec((QB, 256), lambda b, j: (cur(b, j), 12)),
        pl.BlockSpec((QB, 256), lambda b, j: (prev(b, j), 12)),
        pl.BlockSpec((QB, 256), lambda b, j: (cur(b, j), 13)),
        pl.BlockSpec((QB, 256), lambda b, j: (prev(b, j), 13)),
        pl.BlockSpec((QB, 512), lambda b, j: (cur(b, j), 7)),
        pl.BlockSpec((QB, 512), lambda b, j: (cur(b, j), 8)),
        SMEM_SPEC,
    ]
    return specs, cur, prev


def _attn_forward(proj, sinks, S, out_shards):
    T = proj.shape[0]
    nb, nq = T // S, S // QB
    specs, cur, _ = _attn_in_specs(S)
    nw = len(out_shards)

    def body(q_ref, kc_ref, kp_ref, vc_ref, vp_ref, gl_ref, gh_ref, sink_ref, *rest):
        shards = rest[:nw]
        y_ref = rest[nw]
        gathered = rest[nw + 1:2 * nw + 1]
        stage = rest[2 * nw + 1:3 * nw + 1]
        sems = rest[3 * nw + 1:]
        b, j = pl.program_id(0), pl.program_id(1)

        @pl.when((b == 0) & (j == 0))
        def _():
            for a in range(nw):
                stage[a][...] = shards[a][...].astype(bf16)
            _start_all(_direct_gather_copies(stage, gathered, *sems))

        @pl.when((b == nb - 1) & (j == nq - 1))
        def _():
            _wait_all(_direct_gather_copies(stage, gathered, *sems))

        bias = _window_bias(j == 0)
        kc, kp, vc, vp = kc_ref[...], kp_ref[...], vc_ref[...], vp_ref[...]
        for kv in range(KV_HEADS):
            lanes = slice(256 * kv, 256 * (kv + 1))
            hl = slice(HEAD * kv, HEAD * (kv + 1))
            q_rows = _heads_to_rows(q_ref[:, lanes])
            k_cat = jnp.concatenate([kp[:, hl], kc[:, hl]], axis=0)
            v_cat = jnp.concatenate([vp[:, hl], vc[:, hl]], axis=0)
            probs, _ = _attn_probs(q_rows, k_cat, _sink_column(sink_ref, kv), bias)
            o = _rows_to_heads(_dot(probs.astype(bf16), v_cat, NN))
            g_src = gl_ref if kv < 2 else gh_ref
            g = g_src[:, 256 * (kv % 2):256 * (kv % 2 + 1)].astype(f32)
            y_ref[:, lanes] = (o * (g * _sigmoid(g))).astype(bf16)

    args = [proj] * 7 + [sinks] + list(out_shards)
    res = _pcall(
        body, name="attn_forward", grid=(nb, nq),
        in_specs=specs + [pl.BlockSpec(w.shape, lambda b, j: (0, 0)) for w in out_shards],
        out_specs=(pl.BlockSpec((QB, D), lambda b, j: (cur(b, j), 0)),) + tuple([ANY] * nw),
        out_shape=(_sds((T, D), bf16),) + tuple(_sds((NDEV * w.shape[0], w.shape[1]), bf16) for w in out_shards),
        scratch_shapes=[pltpu.VMEM(w.shape, bf16) for w in out_shards] + _exchange_scratch(nw, 7),
        compiler_params=_params(("arbitrary", "arbitrary")),
    )(*args)
    return res[0], res[1:]


def _merge_and_head(x2d, tgt, proj, y_rnn, y_attn, w_r, w_a, w_o, gfin):
    T = x2d.shape[0]
    tb = min(T, 512)
    nsteps = T // tb

    def body(x_ref, t_ref, mr0, mr1, ma0, ma1, yr_ref, ya_ref, wr_ref, wa_ref, wo_ref, gf_ref,
             dx2_ref, dyr_ref, dya_ref, dmr_ref, dma_ref, loss_ref, gfin_ref, gwr_out, gwa_out, gwo_out,
             gwr_acc, gwa_acc, gwo_acc, out_sems):
        step = pl.program_id(0)

        @pl.when(step == 0)
        def _():
            loss_ref[...] = jnp.zeros_like(loss_ref)
            gfin_ref[...] = jnp.zeros_like(gfin_ref)
            gwr_acc[...] = jnp.zeros_like(gwr_acc)
            gwa_acc[...] = jnp.zeros_like(gwa_acc)
            gwo_acc[...] = jnp.zeros_like(gwo_acc)

        sr = _sigmoid(jnp.concatenate([mr0[...], mr1[...]], axis=1).astype(f32))
        sa = _sigmoid(jnp.concatenate([ma0[...], ma1[...]], axis=1).astype(f32))
        p_r = _dot(yr_ref[...], wr_ref[...], NN)
        p_a = _dot(ya_ref[...], wa_ref[...], NN)
        merged = (sr * p_r + sa * p_a).astype(bf16)
        x2 = x_ref[...] + _dot(merged, wo_ref[...], NN)
        rstd = lax.rsqrt(jnp.mean(x2 * x2, axis=-1, keepdims=True) + EPS)
        xh = x2 * rstd
        gf = gf_ref[...]
        err = xh * gf - t_ref[...]
        loss_ref[...] += jnp.sum(err * err)
        dy = err * (1.0 / D)
        gfin_ref[0:1, :] += jnp.sum(dy * xh, axis=0, keepdims=True)
        dxn = dy * gf
        dx2 = rstd * (dxn - xh * jnp.mean(dxn * xh, axis=-1, keepdims=True))
        dx2_ref[...] = dx2
        dx2b = dx2.astype(bf16)
        dmerged = _dot(dx2b, wo_ref[...], NT)
        dmr_ref[...] = (dmerged * p_r * (sr * (1.0 - sr))).astype(bf16)
        dma_ref[...] = (dmerged * p_a * (sa * (1.0 - sa))).astype(bf16)
        dpr = (dmerged * sr).astype(bf16)
        dpa = (dmerged * sa).astype(bf16)
        dyr_ref[...] = _dot(dpr, wr_ref[...], NT).astype(bf16)
        dya_ref[...] = _dot(dpa, wa_ref[...], NT).astype(bf16)
        gwr_acc[...] += _dot(yr_ref[...], dpr, TN)
        gwa_acc[...] += _dot(ya_ref[...], dpa, TN)
        gwo_acc[...] += _dot(merged, dx2b, TN)

        @pl.when(step == nsteps - 1)
        def _():
            copies = [pltpu.make_async_copy(src, dst, out_sems.at[k]) for k, (src, dst) in enumerate(
                ((gwr_acc, gwr_out), (gwa_acc, gwa_out), (gwo_acc, gwo_out)))]
            for cp in copies:
                cp.start()
            for cp in copies:
                cp.wait()

    tok = pl.BlockSpec((tb, D), lambda i: (i, 0))
    half = lambda c: pl.BlockSpec((tb, CH), lambda i, c=c: (i, c))
    wfull = pl.BlockSpec((D, D), lambda i: (0, 0), pipeline_mode=pl.Buffered(1))
    acc = pl.BlockSpec((8, D), lambda i: (0, 0))
    return _pcall(
        body, name="merge_and_head", grid=(nsteps,),
        in_specs=[tok, tok, half(9), half(10), half(11), half(12), tok, tok, wfull, wfull, wfull,
                  pl.BlockSpec((1, D), lambda i: (0, 0))],
        out_specs=(tok, tok, tok, tok, tok, acc, acc, ANY, ANY, ANY),
        out_shape=(_sds((T, D), f32), _sds((T, D), bf16), _sds((T, D), bf16), _sds((T, D), bf16),
                   _sds((T, D), bf16), _sds((8, D), f32), _sds((8, D), f32),
                   _sds((D, D), f32), _sds((D, D), f32), _sds((D, D), f32)),
        scratch_shapes=[pltpu.VMEM((D, D), f32)] * 3 + [pltpu.SemaphoreType.DMA((3,))],
        compiler_params=_params(("arbitrary",)),
    )(x2d, tgt, proj, proj, proj, proj, y_rnn, y_attn, w_r, w_a, w_o, gfin)


def _attn_backward(proj, dy_attn, tabs, sinks, S, chip_sums):
    T = proj.shape[0]
    nb, nq = T // S, S // QB
    nex = len(chip_sums)
    specs, cur, prev = _attn_in_specs(S)
    last = nq - 1
    tab_cur = pl.BlockSpec((QB, 128), lambda b, j: (jnp.minimum(j, last), 0))
    tab_prev = pl.BlockSpec((QB, 128), lambda b, j: (jnp.maximum(jnp.minimum(j, last) - 1, 0), 0))
    specs = specs + [pl.BlockSpec((QB, D), lambda b, j: (cur(b, j), 0))] + [tab_cur] * 3 + [tab_prev] * 3
    q_scale = 1.0 / math.sqrt(HEAD)

    def rope_back(dt, tab):
        return jnp.concatenate([_rope_transposed(dt[:, 128 * l:128 * (l + 1)], *tab) for l in range(2)], axis=1)

    def body(q_ref, kc_ref, kp_ref, vc_ref, vp_ref, gl_ref, gh_ref, sink_ref, dy_ref, cc, s1c, s2c, cp, s1p, s2p,
             *rest):
        ex_src = rest[:nex]
        dq_ref, dkv_ref, dg_ref, dsink_ref = rest[nex:nex + 4]
        ex_dst = rest[nex + 4:2 * nex + 4]
        carry_k, carry_v = rest[2 * nex + 4:2 * nex + 6]
        sems = rest[2 * nex + 6:]
        b, j = pl.program_id(0), pl.program_id(1)

        @pl.when((b == 0) & (j == 0))
        def _():
            dsink_ref[...] = jnp.zeros_like(dsink_ref)
            _start_all(_chip_exchange_copies(ex_src, ex_dst, *sems))

        @pl.when((b == nb - 1) & (j == nq))
        def _():
            _wait_all(_chip_exchange_copies(ex_src, ex_dst, *sems))

        @pl.when(j == 0)
        def _():
            carry_k[...] = jnp.zeros_like(carry_k)
            carry_v[...] = jnp.zeros_like(carry_v)

        @pl.when(j < nq)
        def _():
            bias = _window_bias(j == 0)
            tc = (cc[...], s1c[...], s2c[...])
            tp = (cp[...], s1p[...], s2p[...])
            kc, kp, vc, vp = kc_ref[...], kp_ref[...], vc_ref[...], vp_ref[...]
            dk_prev, dk_cur, dv_prev, dv_cur = [], [], [], []
            dsink_acc = jnp.zeros((8, 128), f32)
            r8 = lax.broadcasted_iota(jnp.int32, (8, 128), 0)
            l8 = lax.broadcasted_iota(jnp.int32, (8, 128), 1)
            for kv in range(KV_HEADS):
                lanes = slice(256 * kv, 256 * (kv + 1))
                hl = slice(HEAD * kv, HEAD * (kv + 1))
                q_rows = _heads_to_rows(q_ref[:, lanes])
                k_cat = jnp.concatenate([kp[:, hl], kc[:, hl]], axis=0)
                v_cat = jnp.concatenate([vp[:, hl], vc[:, hl]], axis=0)
                probs, p_sink = _attn_probs(q_rows, k_cat, _sink_column(sink_ref, kv), bias)
                pb = probs.astype(bf16)
                o = _rows_to_heads(_dot(pb, v_cat, NN))
                g_src = gl_ref if kv < 2 else gh_ref
                g = g_src[:, 256 * (kv % 2):256 * (kv % 2 + 1)].astype(f32)
                sg = _sigmoid(g)
                dy = dy_ref[:, lanes].astype(f32)
                dg_ref[:, lanes] = (dy * o * (sg * (1.0 + g * (1.0 - sg)))).astype(bf16)
                do_rows = _heads_to_rows(dy * (g * sg)).astype(bf16)
                dv = _dot(pb, do_rows, TN)
                dp = _dot(do_rows, v_cat, NT)
                rowdot = jnp.sum(probs * dp, axis=1, keepdims=True)
                ds = (probs * (dp - rowdot)).astype(bf16)
                sink_rows = -(p_sink * rowdot)
                for h in range(GROUP):
                    val = jnp.sum(sink_rows[QB * h:QB * (h + 1), :])
                    dsink_acc = dsink_acc + jnp.where((r8 == 0) & (l8 == GROUP * kv + h), val, 0.0)
                dq = _rows_to_heads(_dot(ds, k_cat, NN)) * q_scale
                dq_ref[:, lanes] = rope_back(dq, tc).astype(bf16)
                dk = _dot(ds, q_rows, TN)
                dk_prev.append(dk[:QB, :])
                dk_cur.append(dk[QB:, :])
                dv_prev.append(dv[:QB, :])
                dv_cur.append(dv[QB:, :])
            dsink_ref[...] += dsink_acc
            dkp = rope_back(jnp.concatenate(dk_prev, axis=1), tp)
            dkc = rope_back(jnp.concatenate(dk_cur, axis=1), tc)
            dkv_ref[:, 0:256] = (carry_k[...] + dkp).astype(bf16)
            dkv_ref[:, 256:512] = (carry_v[...] + jnp.concatenate(dv_prev, axis=1)).astype(bf16)
            carry_k[...] = dkc
            carry_v[...] = jnp.concatenate(dv_cur, axis=1)

        @pl.when(j == nq)
        def _():
            dkv_ref[:, 0:256] = carry_k[...].astype(bf16)
            dkv_ref[:, 256:512] = carry_v[...].astype(bf16)

    lag = lambda b, j: (b * nq + jnp.maximum(j - 1, 0), 0)
    args = [proj] * 7 + [sinks, dy_attn] + list(tabs) + list(tabs) + list(chip_sums)
    res = _pcall(
        body, name="attn_backward", grid=(nb, nq + 1), in_specs=specs + [ANY] * nex,
        out_specs=(pl.BlockSpec((QB, D), lambda b, j: (cur(b, j), 0)), pl.BlockSpec((QB, 512), lag),
                   pl.BlockSpec((QB, D), lambda b, j: (cur(b, j), 0)), pl.BlockSpec((8, 128), lambda b, j: (0, 0)))
        + tuple([ANY] * nex),
        out_shape=(_sds((T, D), bf16), _sds((T, 512), bf16), _sds((T, D), bf16), _sds((8, 128), f32))
        + tuple(_sds(s.shape, s.dtype) for s in chip_sums),
        scratch_shapes=[pltpu.VMEM((QB, 256), f32), pltpu.VMEM((QB, 256), f32)] + _exchange_scratch(nex, 3),
        compiler_params=_params(("arbitrary", "arbitrary")),
    )(*args)
    return res[:4], res[4:]


def _lru_backward(proj, h_all, dy_rnn, cw_full, conv_b, w_a, b_a, w_x, b_x, lam, S):
    T = proj.shape[0]
    nb = T // S
    col, vec, wblk, cwblk = _lru_specs(S, nb)
    tokblk = pl.BlockSpec((S, RB), lambda n, b: (b, n))

    def body(x0_ref, g_ref, h_ref, dy_ref, cw_ref, cb_ref, wa_ref, ba_ref, wx_ref, bx_ref, lam_ref,
             du0_ref, dg_ref, gwa_ref, gwx_ref, vec_ref, gcw_ref, a_s, b_s, dh_s, edge_s):
        @pl.when(pl.program_id(1) == 0)
        def _():
            gwa_ref[...] = jnp.zeros_like(gwa_ref)
            gwx_ref[...] = jnp.zeros_like(gwx_ref)
            vec_ref[...] = jnp.zeros_like(vec_ref)
            gcw_ref[...] = jnp.zeros_like(gcw_ref)

        x0 = x0_ref[...].astype(f32)
        cw = cw_ref[...]
        lam_v = lam_ref[...]
        u, ub, r, i, sp, a, mult, inv_mult, taps = _lru_gates(x0, cw, cb_ref[...], wa_ref[...], ba_ref[...],
                                                              wx_ref[...], bx_ref[...], lam_v)
        h = h_ref[...]
        g = g_ref[...].astype(f32)
        dy = dy_ref[...].astype(f32)
        sg = _sigmoid(g)
        dg_ref[...] = (dy * h * (sg * (1.0 + g * (1.0 - sg)))).astype(bf16)
        _linear_scan(_shift_up(a, 1), dy * (g * sg), a_s, b_s, edge_s, dh_s, reverse=True)
        dh_total = dh_s[...]
        da = dh_total * _shift_down(h, 1)
        dmult = dh_total * (i * u)
        db = dh_total * mult
        di = db * u
        du = db * i
        dlog_a_c = ((-LRU_C) * a) * (da - dmult * (a * inv_mult))
        dr = dlog_a_c * sp
        dsp = jnp.sum(dlog_a_c * r, axis=0, keepdims=True)
        dpre_r = dr * r * (1.0 - r)
        dpre_i = di * i * (1.0 - i)
        dpre_rb = dpre_r.astype(bf16)
        dpre_ib = dpre_i.astype(bf16)
        du = du + _dot(dpre_rb, wa_ref[...].astype(bf16), NT) + _dot(dpre_ib, wx_ref[...].astype(bf16), NT)
        gwa_ref[...] += _dot(ub, dpre_rb, TN)
        gwx_ref[...] += _dot(ub, dpre_ib, TN)
        vec_ref[0:1, :] += jnp.sum(du, axis=0, keepdims=True)
        vec_ref[1:2, :] += jnp.sum(dpre_r, axis=0, keepdims=True)
        vec_ref[2:3, :] += jnp.sum(dpre_i, axis=0, keepdims=True)
        vec_ref[3:4, :] += dsp * (-_sigmoid(-lam_v))
        dx0 = cw[3:4, :] * du
        for k in range(3):
            dx0 = dx0 + cw[k:k + 1, :] * _shift_up(du, 3 - k)
        for k in range(4):
            gcw_ref[k:k + 1, :] += jnp.sum(du * taps[k], axis=0, keepdims=True)
        du0_ref[...] = dx0.astype(bf16)

    wacc = pl.BlockSpec((RB, RB), lambda n, b: (0, n))
    vacc = pl.BlockSpec((8, RB), lambda n, b: (0, n))
    cacc = pl.BlockSpec((8, RB), lambda n, b: (n, 0))
    return _pcall(
        body, name="lru_backward", grid=(RNN_BLOCKS, nb),
        in_specs=[col(0), col(8), tokblk, tokblk, cwblk, vec, wblk, vec, wblk, vec, vec],
        out_specs=(tokblk, tokblk, wacc, wacc, vacc, cacc),
        out_shape=(_sds((T, D), bf16), _sds((T, D), bf16), _sds((RB, D), f32), _sds((RB, D), f32),
                   _sds((8, D), f32), _sds((8 * RNN_BLOCKS, RB), f32)),
        scratch_shapes=[pltpu.VMEM((S, RB), f32)] * 3 + [pltpu.VMEM((S // 8, RB), f32)],
        compiler_params=_params(("arbitrary", "arbitrary")),
    )(proj, proj, h_all, dy_rnn, cw_full, conv_b, w_a, b_a, w_x, b_x, lam)


def _section_of_chunk(s):
    out = []
    for start, n in zip(SEC_START, SEC_CHUNKS):
        inside = (s >= start) & (s < start + n)
        out.append((inside, jnp.clip(s - start, 0, n - 1)))
    return out


EFFECT = pltpu.SideEffectType.DATAFLOW_SIDE_EFFECTING
HBM_SPEC = pl.BlockSpec(memory_space=pltpu.HBM)
SEM_SPEC = pl.BlockSpec(memory_space=pltpu.SEMAPHORE)


def _split_exchange_copies(src_ref, land_ref, send_sems, recv_sems):
    x, y, c = _my_place()
    copies = []
    for k in (3, 1, 2):
        px, py = (x + (k >> 1)) % 2, (y + (k & 1)) % 2
        copies.append(pltpu.make_async_remote_copy(
            src_ref=src_ref.at[2 * px + py], dst_ref=land_ref.at[k - 1], send_sem=send_sems[k - 1],
            recv_sem=recv_sems[k - 1], device_id=(px, py, c), device_id_type=MESH))
    return copies


def _exchange_start(chip_sum):
    _, r, cols = chip_sum.shape

    def body(src_ref, land_ref, s0, s1, s2, r0, r1, r2, src_thru, land_thru, token):
        for cp in _split_exchange_copies(src_ref, land_ref, (s0, s1, s2), (r0, r1, r2)):
            cp.start()
        token[...] = jnp.zeros_like(token)

    land = pltpu.with_memory_space_constraint(lax.empty((3, r, cols), chip_sum.dtype), pltpu.HBM)
    res = _pcall(
        body, name="exchange_start",
        out_shape=tuple([pltpu.SemaphoreType.DMA(())] * 6) + (
            pltpu.HBM(chip_sum.shape, chip_sum.dtype), pltpu.HBM((3, r, cols), chip_sum.dtype), _sds((8, 128), f32)),
        in_specs=(HBM_SPEC, HBM_SPEC), out_specs=tuple([SEM_SPEC] * 6) + (HBM_SPEC, HBM_SPEC, VMEM_SPEC),
        input_output_aliases={0: 6, 1: 7},
        compiler_params=pltpu.CompilerParams(has_side_effects=EFFECT),
    )(pltpu.with_memory_space_constraint(chip_sum, pltpu.HBM), land)
    return res[:6], res[6], res[7], res[8]


def _exchange_wait(sems, src_thru, land_thru, after):
    def body(src_ref, land_ref, s0, s1, s2, r0, r1, r2, after_ref, src_dead, got_ref):
        for cp in _split_exchange_copies(src_ref, land_ref, (s0, s1, s2), (r0, r1, r2)):
            cp.wait_send()
            cp.wait_recv()

    return _pcall(
        body, name="exchange_wait",
        out_shape=(pltpu.HBM(src_thru.shape, src_thru.dtype), pltpu.HBM(land_thru.shape, land_thru.dtype)),
        in_specs=(HBM_SPEC, HBM_SPEC) + tuple([SEM_SPEC] * 6) + (ANY,), out_specs=(HBM_SPEC, HBM_SPEC),
        input_output_aliases={0: 0, 1: 1},
        compiler_params=pltpu.CompilerParams(has_side_effects=EFFECT),
    )(src_thru, land_thru, *sems, after)[1]


def _input_grad(dsecs, wt_full, x2d, dx2, norm_g):
    T = x2d.shape[0]
    tb = min(T, 512)
    nsec = len(dsecs)
    ntok = T // tb

    def body(*refs):
        secs = refs[:nsec]
        wt_ref, x_ref, dx2_ref, g_ref, dx_ref, gnorm_ref = refs[nsec:]
        i = pl.program_id(0)

        @pl.when(i == 0)
        def _():
            gnorm_ref[...] = jnp.zeros_like(gnorm_ref)

        dh = None
        for a, (start, n) in enumerate(zip(SEC_START, SEC_CHUNKS)):
            part = _dot(secs[a][...], wt_ref[CH * start:CH * (start + n), :], NN)
            dh = part if dh is None else dh + part
        xv = x_ref[...]
        rstd = lax.rsqrt(jnp.mean(xv * xv, axis=-1, keepdims=True) + EPS)
        xh = xv * rstd
        gnorm_ref[0:1, :] += jnp.sum(dh * xh, axis=0, keepdims=True)
        dxn = dh * g_ref[...]
        dx_ref[...] = dx2_ref[...] + rstd * (dxn - xh * jnp.mean(dxn * xh, axis=-1, keepdims=True))

    tok = pl.BlockSpec((tb, D), lambda i: (i, 0))
    return _pcall(
        body, name="input_grad", grid=(ntok,),
        in_specs=[pl.BlockSpec((tb, sec.shape[1]), lambda i: (i, 0)) for sec in dsecs]
        + [pl.BlockSpec((D_IN, D), lambda i: (0, 0), pipeline_mode=pl.Buffered(1)), tok, tok,
           pl.BlockSpec((1, D), lambda i: (0, 0))],
        out_specs=(tok, pl.BlockSpec((8, D), lambda i: (0, 0))),
        out_shape=(_sds((T, D), f32), _sds((8, D), f32)),
        compiler_params=_params(("arbitrary",)),
    )(*dsecs, wt_full, x2d, dx2, norm_g)


def _w_in_grad(dsecs, h_bf):
    T = h_bf.shape[0]
    tk = min(T, 2048)
    nchunks = D_IN // CH
    nsec = len(dsecs)
    nt = T // tk

    def body(*refs):
        secs = refs[:nsec]
        h_ref, out_ref, acc = refs[nsec:]
        s, t = pl.program_id(0), pl.program_id(1)

        @pl.when(t == 0)
        def _():
            acc[...] = jnp.zeros_like(acc)

        h_rows = h_ref[pl.ds(pl.multiple_of(t * tk, tk), tk), :]
        for a, (start, n) in enumerate(zip(SEC_START, SEC_CHUNKS)):
            @pl.when((s >= start) & (s < start + n))
            def _(a=a):
                acc[...] += _dot(secs[a][...], h_rows, TN)

        @pl.when(t == nt - 1)
        def _():
            out_ref[...] = acc[...].astype(bf16)

    def sec_spec(a):
        def index(s, t, a=a):
            inside, local = _section_of_chunk(s)[a]
            return (jnp.where(inside, t, 0), local)
        return pl.BlockSpec((tk, CH), index)

    return _pcall(
        body, name="w_in_grad", grid=(nchunks, T // tk),
        in_specs=[sec_spec(a) for a in range(nsec)]
        + [pl.BlockSpec((T, D), lambda s, t: (0, 0), pipeline_mode=pl.Buffered(1))],
        out_specs=pl.BlockSpec((CH, D), lambda s, t: (s, 0)), out_shape=_sds((D_IN, D), bf16),
        scratch_shapes=[pltpu.VMEM((CH, D), f32)],
        compiler_params=_params(("arbitrary", "arbitrary")),
    )(*dsecs, h_bf)


SMALL_NAMES = ("lru_w_a", "lru_w_x", "conv_b", "lru_b_a", "lru_b_x", "lru_lambda", "norm_g", "final_norm_g",
               "attn_sinks", "conv_w")
MISC_ROW = {"conv_b": 0, "lru_b_a": 1, "lru_b_x": 2, "lru_lambda": 3, "norm_g": 8, "final_norm_g": 16,
            "attn_sinks": 24, "loss": 32}


def _small_step(gwa, gwx, gvec, gnorm_blk, gfin_blk, dsink_blk, loss_blk, gcw, params):
    srcs_rows = (RB // NDEV, RB // NDEV, 8, 8)
    flat = [t for n in SMALL_NAMES for t in params[n]]
    nin = 8 + len(flat)
    nout = 4 * len(SMALL_NAMES) + 1

    def body(*refs):
        gwa_ref, gwx_ref, gvec_ref, gnorm_ref, gfin_ref, dsink_ref, loss_ref, gcw_ref = refs[:8]
        prm = {n: refs[8 + 3 * k:11 + 3 * k] for k, n in enumerate(SMALL_NAMES)}
        outs = {n: refs[nin + 4 * k:nin + 4 * k + 4] for k, n in enumerate(SMALL_NAMES)}
        loss_out = refs[nin + nout - 1]
        (misc, got_a, got_x, got_m, got_c, red_a, red_x, red_m, all_a, all_x, all_m,
         sa, ra, sb, rb) = refs[nin + nout:]
        x, y, c = _my_place()
        me = 4 * x + 2 * y + c

        misc[...] = jnp.zeros_like(misc)
        misc[0:8, :] = gvec_ref[...]
        misc[8:16, :] = gnorm_ref[...]
        misc[16:24, :] = gfin_ref[...]
        misc[24:32, 0:128] = dsink_ref[...]
        misc[32:40, :] = loss_ref[...]

        srcs = (gwa_ref, gwx_ref, misc, gcw_ref)
        gots = (got_a, got_x, got_m, got_c)

        def shard(ref, rows, dev):
            return ref.at[pl.ds(pl.multiple_of(dev * rows, 8), rows), :]

        scatter = []
        for k in range(1, NDEV):
            px, py, pc = _peer(k)
            for a in range(4):
                scatter.append(pltpu.make_async_remote_copy(
                    src_ref=shard(srcs[a], srcs_rows[a], 4 * px + 2 * py + pc), dst_ref=gots[a].at[k - 1],
                    send_sem=sa.at[4 * (k - 1) + a], recv_sem=ra.at[4 * (k - 1) + a],
                    device_id=(px, py, pc), device_id_type=MESH))
        for cp in scatter:
            cp.start()
        for cp in scatter:
            cp.wait()

        def reduced(a):
            rows = srcs_rows[a]
            total = srcs[a][pl.ds(pl.multiple_of(me * rows, 8), rows), :]
            for k in range(NDEV - 1):
                total = total + gots[a][k]
            return total

        reds = (red_a, red_x, red_m)
        alls = (all_a, all_x, all_m)
        for a in range(3):
            val = reduced(a)
            reds[a][...] = val
            alls[a][pl.ds(pl.multiple_of(me * srcs_rows[a], 8), srcs_rows[a]), :] = val
        gather = []
        for k in range(1, NDEV):
            peer = _peer(k)
            for a in range(3):
                gather.append(pltpu.make_async_remote_copy(
                    src_ref=reds[a], dst_ref=shard(alls[a], srcs_rows[a], me),
                    send_sem=sb.at[3 * (k - 1) + a], recv_sem=rb.at[3 * (k - 1) + a],
                    device_id=peer, device_id_type=MESH))
        for cp in gather:
            cp.start()
        g_conv = reduced(3)[0:4, :]
        for cp in gather:
            cp.wait()

        def update(name, g, pick=lambda r: r[...]):
            w_ref, m_ref, v_ref = prm[name]
            delta, m_new, v_new = _adam_math(g, pick(w_ref), pick(m_ref), pick(v_ref))
            return g, delta, m_new, v_new

        for n in range(RNN_BLOCKS):
            lanes = slice(RB * n, RB * (n + 1))
            for name, full in (("lru_w_a", all_a), ("lru_w_x", all_x)):
                for out, val in zip(outs[name], update(name, full[:, lanes], pick=lambda r, n=n: r[n])):
                    out[n] = val
        for name in ("conv_b", "lru_b_a", "lru_b_x", "lru_lambda", "norm_g", "final_norm_g"):
            row = MISC_ROW[name]
            for out, val in zip(outs[name], update(name, all_m[row:row + 1, :])):
                out[...] = val
        row = MISC_ROW["attn_sinks"]
        for out, val in zip(outs["attn_sinks"], update("attn_sinks", all_m[row:row + 1, 0:16])):
            out[...] = val
        for out, val in zip(outs["conv_w"], update("conv_w", g_conv)):
            out[...] = val
        row = MISC_ROW["loss"]
        loss_out[...] = all_m[row:row + 8, 0:128] * (0.5 / D)

    out_shape = tuple(_sds(params[n][0].shape, f32) for n in SMALL_NAMES for _ in range(4)) + (_sds((8, 128), f32),)
    scratch = [pltpu.VMEM((64, D), f32),
               pltpu.VMEM((NDEV - 1, RB // NDEV, D), f32), pltpu.VMEM((NDEV - 1, RB // NDEV, D), f32),
               pltpu.VMEM((NDEV - 1, 8, D), f32), pltpu.VMEM((NDEV - 1, 8, RB), f32),
               pltpu.VMEM((RB // NDEV, D), f32), pltpu.VMEM((RB // NDEV, D), f32), pltpu.VMEM((8, D), f32),
               pltpu.VMEM((RB, D), f32), pltpu.VMEM((RB, D), f32), pltpu.VMEM((64, D), f32),
               pltpu.SemaphoreType.DMA((4 * (NDEV - 1),)), pltpu.SemaphoreType.DMA((4 * (NDEV - 1),)),
               pltpu.SemaphoreType.DMA((3 * (NDEV - 1),)), pltpu.SemaphoreType.DMA((3 * (NDEV - 1),))]
    res = _pcall(
        body, name="small_step", out_shape=out_shape,
        in_specs=[VMEM_SPEC] * nin, out_specs=tuple([VMEM_SPEC] * nout),
        scratch_shapes=scratch, compiler_params=_params(),
    )(gwa, gwx, gvec, gnorm_blk, gfin_blk, dsink_blk, loss_blk, gcw, *flat)
    return {n: res[4 * k:4 * k + 4] for k, n in enumerate(SMALL_NAMES)}, res[-1]


def _pad_rows(v, rows=8):
    return jnp.concatenate([v, jnp.zeros((rows - v.shape[0], v.shape[1]), v.dtype)], axis=0)


def kernel(x, norm_g, w_in, conv_w, conv_b, lru_w_a, lru_b_a, lru_w_x, lru_b_x, lru_lambda, attn_sinks, w_rnn_out, w_attn_out, w_o, final_norm_g, loss_target, m_norm_g, m_w_in, m_conv_w, m_conv_b, m_lru_w_a, m_lru_b_a, m_lru_w_x, m_lru_b_x, m_lru_lambda, m_attn_sinks, m_w_rnn_out, m_w_attn_out, m_w_o, m_final_norm_g, v_norm_g, v_w_in, v_conv_w, v_conv_b, v_lru_w_a, v_lru_b_a, v_lru_w_x, v_lru_b_x, v_lru_lambda, v_attn_sinks, v_w_rnn_out, v_w_attn_out, v_w_o, v_final_norm_g):
    nb, S, _ = x.shape
    T = nb * S
    x2d = x.reshape(T, D)
    tgt = loss_target.reshape(T, D)
    fin_g = final_norm_g.reshape(1, D)
    w_a3, w_x3 = lru_w_a[0], lru_w_x[0]

    my_core = lax.axis_index("c").astype(jnp.int32).reshape(1)
    cx, cy = lax.axis_index("x"), lax.axis_index("y")
    chip_order = jnp.stack([2 * cx + cy, 2 * (1 - cx) + cy, 2 * cx + (1 - cy),
                            2 * (1 - cx) + (1 - cy)]).astype(jnp.int32)

    tabs = _rope_tables(S)
    h_bf, proj, wt_full, cw_full, _ = _in_proj_gather(
        x2d, norm_g, w_in[0].T.astype(bf16), _pad_rows(conv_w[0]), tabs, S, (), chip_order)
    y_rnn, h_all = _lru_forward(proj, cw_full, conv_b, w_a3, lru_b_a, w_x3, lru_b_x, lru_lambda, S)
    y_attn, (wr_full, wa_full, wo_full) = _attn_forward(proj, attn_sinks, S,
                                                        (w_rnn_out[0], w_attn_out[0], w_o[0]))

    (dx2, dy_rnn, dy_attn, dmr, dma, loss_blk, gfin_blk, g_wr, g_wa, g_wo) = _merge_and_head(
        x2d, tgt, proj, y_rnn, y_attn, wr_full, wa_full, wo_full, fin_g)
    sums_out = _pair_sums([g_wr, g_wa, g_wo], [bf16, bf16, bf16], my_core, "out")

    (dq, dkv, dga, dsink_blk), (p_wr, p_wa, p_wo) = _attn_backward(proj, dy_attn, tabs, attn_sinks, S, sums_out)
    du0, dgr, gwa, gwx, gvec, gcw = _lru_backward(proj, h_all, dy_rnn, cw_full, conv_b, w_a3, lru_b_a, w_x3,
                                                  lru_b_x, lru_lambda, S)
    dsecs = (du0, dgr, dq, dkv, dga, dmr, dma)

    g_wt = _w_in_grad(dsecs, h_bf)
    (sum_in,) = _pair_sums([g_wt], [bf16], my_core, "in")
    ex_sems, sum_in, landing, token = _exchange_start(sum_in)
    grad_x2d, gnorm_blk = _input_grad(dsecs, wt_full, x2d, dx2, norm_g + token[0, 0])
    p_wt = _exchange_wait(ex_sems, sum_in, landing, gnorm_blk)
    p_wt_own = lax.dynamic_index_in_dim(sum_in, 2 * cx + cy, axis=0, keepdims=False)

    small, loss_out = _small_step(gwa, gwx, gvec, gnorm_blk, gfin_blk, dsink_blk, loss_blk, gcw, {
        "lru_w_a": (w_a3, m_lru_w_a[0], v_lru_w_a[0]), "lru_w_x": (w_x3, m_lru_w_x[0], v_lru_w_x[0]),
        "conv_b": (conv_b, m_conv_b, v_conv_b), "lru_b_a": (lru_b_a, m_lru_b_a, v_lru_b_a),
        "lru_b_x": (lru_b_x, m_lru_b_x, v_lru_b_x), "lru_lambda": (lru_lambda, m_lru_lambda, v_lru_lambda),
        "norm_g": (norm_g, m_norm_g, v_norm_g),
        "final_norm_g": (fin_g, m_final_norm_g.reshape(1, D), v_final_norm_g.reshape(1, D)),
        "attn_sinks": (attn_sinks, m_attn_sinks, v_attn_sinks),
        "conv_w": (conv_w[0], m_conv_w[0], v_conv_w[0])})

    o_wt = _adamw(p_wt, w_in[0].T, m_w_in[0].T, v_w_in[0].T, "adamw_w_in", first=p_wt_own)
    o_wr = _adamw(p_wr, w_rnn_out[0], m_w_rnn_out[0], v_w_rnn_out[0], "adamw_w_rnn_out")
    o_wa = _adamw(p_wa, w_attn_out[0], m_w_attn_out[0], v_w_attn_out[0], "adamw_w_attn_out")
    o_wo = _adamw(p_wo, w_o[0], m_w_o[0], v_w_o[0], "adamw_w_o")

    def result(kind):
        d = {n: small[n][kind] for n in ("conv_b", "lru_b_a", "lru_b_x", "lru_lambda", "norm_g", "attn_sinks")}
        d.update({n: small[n][kind][None] for n in ("lru_w_a", "lru_w_x", "conv_w")})
        d["final_norm_g"] = small["final_norm_g"][kind].reshape(D)
        d.update({"w_in": o_wt[kind].T[None], "w_rnn_out": o_wr[kind][None], "w_attn_out": o_wa[kind][None],
                  "w_o": o_wo[kind][None]})
        return d

    order = ("norm_g", "w_in", "conv_w", "conv_b", "lru_w_a", "lru_b_a", "lru_w_x", "lru_b_x", "lru_lambda",
             "attn_sinks", "w_rnn_out", "w_attn_out", "w_o", "final_norm_g")
    outs = [loss_out[0, 0], grad_x2d.reshape(nb, S, D)]
    for kind in range(4):
        d = result(kind)
        outs += [d[n] for n in order]
    return tuple(outs)
```

```python
import functools
import math

import jax
import jax.numpy as jnp
from jax import lax
from jax.experimental import pallas as pl
from jax.experimental.pallas import tpu as pltpu

f32 = jnp.float32
bf16 = jnp.bfloat16

D = 1024
D_IN = 6656
NDEV = 8
RNN_BLOCKS = 8
RB = 128
HEAD = 64
KV_HEADS = 4
GROUP = 4
QB = 128
LRU_C = 8.0
EPS = 1e-6
ROPE_DIM = 16
ROPE_THETA = 500000.0
CH = 512
SEC_START = (0, 2, 4, 6, 7, 9, 11)
SEC_CHUNKS = (2, 2, 2, 1, 2, 2, 2)
VMEM_LIMIT = 62 * 1024 * 1024

ADAM_LR, ADAM_B1, ADAM_B2, ADAM_EPS, ADAM_WD, ADAM_STEP = 0.001, 0.9, 0.999, 1e-08, 0.01, 10

MESH = pl.DeviceIdType.MESH
ANY = pl.BlockSpec(memory_space=pl.ANY)
VMEM_SPEC = pl.BlockSpec(memory_space=pltpu.VMEM)
SMEM_SPEC = pl.BlockSpec(memory_space=pltpu.SMEM)


def _pcall(body, **kw):
    return pl.pallas_call(body, **kw)


def _params(sem=None, **kw):
    if sem is not None:
        kw["dimension_semantics"] = sem
    return pltpu.CompilerParams(vmem_limit_bytes=VMEM_LIMIT, **kw)


def _sds(shape, dtype):
    return jax.ShapeDtypeStruct(shape, dtype)


def _dot(a, b, dims):
    return lax.dot_general(a, b, (dims, ((), ())), preferred_element_type=f32)


NN = ((1,), (0,))
NT = ((1,), (1,))
TN = ((0,), (0,))


def _sigmoid(v):
    return 0.5 * jnp.tanh(0.5 * v) + 0.5


def _sigmoid_positive(v):
    return 1.0 / (1.0 + jnp.exp(-v))


def _my_place():
    return lax.axis_index("x"), lax.axis_index("y"), lax.axis_index("c")


def _peer(k):
    x, y, c = _my_place()
    return (x + ((k >> 2) & 1)) % 2, (y + ((k >> 1) & 1)) % 2, (c + (k & 1)) % 2


def _direct_gather_copies(srcs, outs, send_sems, recv_sems, local_sems):
    x, y, c = _my_place()
    me = 4 * x + 2 * y + c
    local, remote = [], []
    for a, (src, out) in enumerate(zip(srcs, outs)):
        r = src.shape[0]
        mine = out.at[pl.ds(pl.multiple_of(me * r, 8), r), :]
        local.append(pltpu.make_async_copy(src, mine, local_sems.at[a]))
        for k in range(1, NDEV):
            remote.append(pltpu.make_async_remote_copy(
                src_ref=src, dst_ref=mine, send_sem=send_sems.at[7 * a + k - 1], recv_sem=recv_sems.at[7 * a + k - 1],
                device_id=_peer(k), device_id_type=MESH))
    return local, remote


def _two_level_gather(srcs, outs, send_sems, recv_sems, local_sems):
    x, y, c = _my_place()
    me, sibling = (x, y, c), (x, y, 1 - c)
    chips = [(1 - x, y), (x, 1 - y), (1 - x, 1 - y)]
    narr = len(srcs)

    def rows(a, place):
        px, py, pc = place
        r = srcs[a].shape[0]
        return outs[a].at[pl.ds(pl.multiple_of((4 * px + 2 * py + pc) * r, 8), r), :]

    def copy(a, k, block, to, from_src=False):
        return pltpu.make_async_remote_copy(
            src_ref=srcs[a] if from_src else rows(a, block), dst_ref=rows(a, block),
            send_sem=send_sems.at[7 * a + k], recv_sem=recv_sems.at[7 * a + k], device_id=to, device_id_type=MESH)

    def mine(a):
        return pltpu.make_async_copy(srcs[a], rows(a, me), local_sems.at[a])

    def start():
        for a in range(narr):
            mine(a).start()
            copy(a, 0, me, sibling, True).start()
            for j, chip in enumerate(chips):
                copy(a, 1 + j, me, (*chip, c), True).start()

    def forward():
        for j, chip in enumerate(chips):
            for a in range(narr):
                copy(a, 1 + j, (*chip, c), me).wait_recv()
                copy(a, 4 + j, (*chip, c), sibling).start()

    def finish():
        for a in range(narr):
            copy(a, 0, sibling, me).wait_recv()
            for j, chip in enumerate(chips):
                copy(a, 4 + j, (*chip, 1 - c), me).wait_recv()
            copy(a, 0, me, sibling, True).wait_send()
            for j, chip in enumerate(chips):
                copy(a, 1 + j, me, (*chip, c), True).wait_send()
                copy(a, 4 + j, (*chip, c), sibling).wait_send()
            mine(a).wait()

    return start, forward, finish


def _chip_exchange_copies(src, dst, send_sems, recv_sems, local_sems):
    x, y, c = _my_place()
    local, remote = [], []
    for a in range(len(src)):
        local.append(pltpu.make_async_copy(src[a].at[2 * x + y], dst[a].at[0], local_sems.at[a]))
    for k in (3, 1, 2):
        px, py = (x + (k >> 1)) % 2, (y + (k & 1)) % 2
        for a in range(len(src)):
            remote.append(pltpu.make_async_remote_copy(
                src_ref=src[a].at[2 * px + py], dst_ref=dst[a].at[k],
                send_sem=send_sems.at[3 * a + k - 1], recv_sem=recv_sems.at[3 * a + k - 1],
                device_id=(px, py, c), device_id_type=MESH))
    return local, remote


def _exchange_scratch(narr, per_array):
    return [pltpu.SemaphoreType.DMA((per_array * narr,)), pltpu.SemaphoreType.DMA((per_array * narr,)),
            pltpu.SemaphoreType.DMA((narr,))]


def _start_all(copies):
    local, remote = copies
    for cp in local + remote:
        cp.start()


def _wait_all(copies):
    local, remote = copies
    for cp in remote + local:
        cp.wait()


def _pair_exchange(grads, name):
    narr = len(grads)
    nrows = tuple(g.shape[0] // NDEV for g in grads)
    views = [g.reshape(4, 2, r, g.shape[1]) for g, r in zip(grads, nrows)]

    def body(*refs):
        gin = refs[:narr]
        got = refs[narr:2 * narr]
        send_sems, recv_sems = refs[2 * narr:]
        x, y, c = _my_place()
        copies = [pltpu.make_async_remote_copy(
            src_ref=gin[a].at[:, pl.ds(1 - c, 1)], dst_ref=got[a],
            send_sem=send_sems.at[a], recv_sem=recv_sems.at[a],
            device_id=(x, y, 1 - c), device_id_type=MESH) for a in range(narr)]
        for cp in copies:
            cp.start()
        for cp in copies:
            cp.wait()

    out_shape = tuple(_sds((4, 1, r, g.shape[1]), g.dtype) for r, g in zip(nrows, grads))
    got = _pcall(
        body, name=name, out_shape=out_shape,
        in_specs=[ANY] * narr, out_specs=tuple([ANY] * narr),
        scratch_shapes=[pltpu.SemaphoreType.DMA((narr,)), pltpu.SemaphoreType.DMA((narr,))],
        compiler_params=_params(),
    )(*views)
    return views, [g.reshape(4, r, g.shape[3]) for g, r in zip(got, nrows)]


def _row_tile(rows, dtype):
    unit = 16 if dtype == bf16 else 8
    for cand in (256, 208, 128, 64, 40, 32, 16, 8):
        if rows % cand == 0 and cand % unit == 0:
            return cand
    return rows


def _chip_sum(view, got, my_core, out_dtype, name):
    _, _, r, cols = view.shape
    tr = _row_tile(r, out_dtype)

    def body(core_ref, mine_ref, got_ref, out_ref):
        out_ref[...] = (mine_ref[...].astype(f32) + got_ref[...].astype(f32)).astype(out_dtype)

    grid_spec = pltpu.PrefetchScalarGridSpec(
        num_scalar_prefetch=1, grid=(4, r // tr),
        in_specs=[pl.BlockSpec((None, None, tr, cols), lambda q, i, core: (q, core[0], i, 0)),
                  pl.BlockSpec((None, tr, cols), lambda q, i, core: (q, i, 0))],
        out_specs=pl.BlockSpec((None, tr, cols), lambda q, i, core: (q, i, 0)))
    return _pcall(body, name=name, grid_spec=grid_spec, out_shape=_sds((4, r, cols), out_dtype),
                  compiler_params=_params(("arbitrary", "arbitrary")))(my_core, view, got)


def _pair_sums(grads, wire_dtypes, my_core, tag):
    views, got = _pair_exchange(grads, "pair_exchange_" + tag)
    return [_chip_sum(v, g, my_core, dt, "chip_sum_%s%d" % (tag, a))
            for a, (v, g, dt) in enumerate(zip(views, got, wire_dtypes))]


def _adam_math(g, w, m, v):
    m_new = ADAM_B1 * m + (1.0 - ADAM_B1) * g
    v_new = ADAM_B2 * v + (1.0 - ADAM_B2) * (g * g)
    m_hat = m_new / (1.0 - ADAM_B1 ** ADAM_STEP)
    v_hat = v_new / (1.0 - ADAM_B2 ** ADAM_STEP)
    return -ADAM_LR * (m_hat / (jnp.sqrt(v_hat) + ADAM_EPS) + ADAM_WD * w), m_new, v_new


def _adamw(parts, w, m, v, name, first=None):
    n, rows, cols = parts.shape
    tr = _row_tile(rows, parts.dtype)
    lead = () if first is None else (first,)

    def body(*refs):
        p_ref, w_ref, m_ref, v_ref, g_out, d_out, m_out, v_out = refs[len(lead):]
        g = refs[0][...].astype(f32) if lead else p_ref[0].astype(f32)
        for s in range(0 if lead else 1, n):
            g = g + p_ref[s].astype(f32)
        g_out[...] = g
        d_out[...], m_out[...], v_out[...] = _adam_math(g, w_ref[...], m_ref[...], v_ref[...])

    blk = pl.BlockSpec((tr, cols), lambda i: (i, 0))
    return _pcall(
        body, name=name, grid=(rows // tr,),
        in_specs=[blk] * len(lead) + [pl.BlockSpec((n, tr, cols), lambda i: (0, i, 0)), blk, blk, blk],
        out_specs=(blk, blk, blk, blk), out_shape=tuple(_sds((rows, cols), f32) for _ in range(4)),
        compiler_params=_params(("arbitrary",)),
    )(*lead, parts, w, m, v)


def _rope(t, c, s1, s2):
    w = t.shape[1]
    return t * c + pltpu.roll(t, w - 8, 1) * s1 + pltpu.roll(t, 8, 1) * s2


def _rope_transposed(dt, c, s1, s2):
    w = dt.shape[1]
    return dt * c + pltpu.roll(dt * s1, 8, 1) + pltpu.roll(dt * s2, w - 8, 1)


PAIR_ROWS = D_IN // 4
SUB_COLS = ((0, 512), (512, 512), (1024, 512), (1536, 128))
Q_SLABS = range(3, 11)
K_SLABS = range(11, 13)


def _in_proj_gather(x2d, norm_g, wt_shard, cw_shard, tabs, S, out_shards, chip_order):
    T = x2d.shape[0]
    tb = min(S, 1024)
    ntok = T // tb
    nsb = S // tb
    q_scale = 1.0 / math.sqrt(HEAD)
    shard_rows = wt_shard.shape[0]
    small = (cw_shard,) + tuple(out_shards)
    nsm = len(small)

    def body(order_ref, x_ref, g_ref, c_ref, s1_ref, s2_ref, wt_hbm, *rest):
        small_in = rest[:nsm]
        h_ref, proj_ref, wt_out = rest[nsm:nsm + 3]
        small_out = rest[nsm + 3:2 * nsm + 3]
        wt_vm, h_vm = rest[2 * nsm + 3:2 * nsm + 5]
        stage = rest[2 * nsm + 5:3 * nsm + 4]
        wsend, wrecv, wlocal = rest[3 * nsm + 4:3 * nsm + 7]
        dsems = rest[3 * nsm + 7:]
        jj, i = pl.program_id(0), pl.program_id(1)
        x, y, c = _my_place()
        me, sibling = (x, y, c), (x, y, 1 - c)
        chips = [(1 - x, y), (x, 1 - y), (1 - x, 1 - y)]

        def rows(place):
            px, py, pc = place
            return wt_vm.at[pl.ds(pl.multiple_of((4 * px + 2 * py + pc) * shard_rows, 16), shard_rows), :]

        def copy(k, block, to, src=None):
            return pltpu.make_async_remote_copy(
                src_ref=rows(block) if src is None else src, dst_ref=rows(block),
                send_sem=wsend.at[k], recv_sem=wrecv.at[k], device_id=to, device_id_type=MESH)

        def small_copies():
            srcs = (small_in[0],) + tuple(stage)
            return _direct_gather_copies(srcs, small_out, *dsems)

        own = pltpu.make_async_copy(wt_hbm, rows(me), wlocal.at[0])
        keep = pltpu.make_async_copy(wt_vm, wt_out, wlocal.at[1])

        @pl.when((jj == 0) & (i == 0))
        def _():
            own.start()
            copy(0, me, sibling, src=wt_hbm).start()
            for j, chip in enumerate(chips):
                copy(1 + j, me, (*chip, c), src=wt_hbm).start()
            for a in range(nsm - 1):
                stage[a][...] = small_in[1 + a][...].astype(bf16)
            _start_all(small_copies())
            own.wait()
            copy(0, sibling, me).wait_recv()

        for j, chip in enumerate(chips):
            @pl.when((jj == 1 + j) & (i == 0))
            def _(j=j, chip=chip):
                copy(1 + j, (*chip, c), me).wait_recv()
                copy(4 + j, (*chip, c), sibling).start()
                copy(4 + j, (*chip, 1 - c), me).wait_recv()

        @pl.when((jj == 3) & (i == 0))
        def _():
            keep.start()

        @pl.when((jj == 3) & (i == ntok - 1))
        def _():
            copy(0, me, sibling, src=wt_hbm).wait_send()
            for j, chip in enumerate(chips):
                copy(1 + j, me, (*chip, c), src=wt_hbm).wait_send()
                copy(4 + j, (*chip, c), sibling).wait_send()
            _wait_all(small_copies())
            keep.wait()

        tok = pl.ds(pl.multiple_of(i * tb, tb), tb)

        @pl.when(jj == 0)
        def _():
            xv = x_ref[...]
            ms = jnp.mean(xv * xv, axis=-1, keepdims=True)
            hb = (xv * lax.rsqrt(ms + EPS) * g_ref[...]).astype(bf16)
            h_ref[...] = hb
            h_vm[tok, :] = hb

        block = order_ref[jj]
        hb = h_vm[tok, :]

        def piece(c0, w):
            w_rows = wt_vm[pl.ds(pl.multiple_of(block * PAIR_ROWS + c0, 128), w), :]
            return _dot(hb, w_rows, NT)

        @pl.when(block != 1)
        def _():
            for c0, w in SUB_COLS:
                proj_ref[:, c0:c0 + w] = piece(c0, w).astype(bf16)

        @pl.when(block == 1)
        def _():
            tab = (c_ref[...], s1_ref[...], s2_ref[...])
            for c0, w in SUB_COLS:
                acc = piece(c0, w)
                for l in range(w // 128):
                    slab = (c0 + 128 * l) // 128
                    part = acc[:, 128 * l:128 * (l + 1)]
                    if slab in Q_SLABS:
                        part = _rope(part, *tab) * q_scale
                    elif slab in K_SLABS:
                        part = _rope(part, *tab)
                    proj_ref[:, 128 * slab:128 * (slab + 1)] = part.astype(bf16)

    first_pass = lambda jj, i, order: (jnp.where(jj == 0, i, ntok - 1), 0)
    const = lambda jj, i, order: (0, 0)
    tab = pl.BlockSpec((tb, 128), lambda jj, i, order: (jnp.where(order[jj] == 1, i % nsb, 0), 0))
    grid_spec = pltpu.PrefetchScalarGridSpec(
        num_scalar_prefetch=1, grid=(4, ntok),
        in_specs=[pl.BlockSpec((tb, D), first_pass), pl.BlockSpec((1, D), const), tab, tab, tab, ANY]
        + [pl.BlockSpec(w.shape, const) for w in small],
        out_specs=(pl.BlockSpec((tb, D), first_pass),
                   pl.BlockSpec((tb, PAIR_ROWS), lambda jj, i, order: (i, order[jj])), ANY) + tuple([ANY] * nsm),
        scratch_shapes=[pltpu.VMEM((D_IN, D), bf16), pltpu.VMEM((T, D), bf16)]
        + [pltpu.VMEM(w.shape, bf16) for w in out_shards]
        + [pltpu.SemaphoreType.DMA((7,)), pltpu.SemaphoreType.DMA((7,)), pltpu.SemaphoreType.DMA((2,))]
        + _exchange_scratch(nsm, 7))
    res = _pcall(
        body, name="in_proj", grid_spec=grid_spec,
        out_shape=(_sds((T, D), bf16), _sds((T, D_IN), bf16), _sds((D_IN, D), bf16),
                   _sds((NDEV * cw_shard.shape[0], cw_shard.shape[1]), f32))
        + tuple(_sds((NDEV * w.shape[0], w.shape[1]), bf16) for w in out_shards),
        compiler_params=_params(("arbitrary", "arbitrary")),
    )(chip_order, x2d, norm_g, *tabs, wt_shard, *small)
    return res[0], res[1], res[2], res[3], res[4:]


def _rows_iota(shape):
    return lax.broadcasted_iota(jnp.int32, shape, 0)


def _shift_down(v, k):
    return jnp.where(_rows_iota(v.shape) >= k, pltpu.roll(v, k, 0), 0.0)


def _shift_up(v, k):
    n = v.shape[0]
    return jnp.where(_rows_iota(v.shape) < n - k, pltpu.roll(v, n - k, 0), 0.0)


def _linear_scan(a, b, a_s, b_s, edge_s, out_ref, reverse):
    n = a.shape[0]
    ng = n // 8
    a3, b3 = a.reshape(ng, 8, RB), b.reshape(ng, 8, RB)
    rid = lax.broadcasted_iota(jnp.int32, a3.shape, 1)
    for s in (1, 2, 4):
        keep, shift = (rid < 8 - s, 8 - s) if reverse else (rid >= s, s)
        b3 = jnp.where(keep, a3 * pltpu.roll(b3, shift, 1) + b3, b3)
        a3 = jnp.where(keep, a3 * pltpu.roll(a3, shift, 1), a3)
    a_s[...] = a3.reshape(n, RB)
    b_s[...] = b3.reshape(n, RB)
    edge = 0 if reverse else 7
    ea, eb = a_s[pl.ds(edge, ng, stride=8), :], b_s[pl.ds(edge, ng, stride=8), :]
    r = _rows_iota(ea.shape)
    s = 1
    while s < ng:
        keep, shift = (r < ng - s, ng - s) if reverse else (r >= s, s)
        eb = jnp.where(keep, ea * pltpu.roll(eb, shift, 0) + eb, eb)
        if 2 * s < ng:
            ea = jnp.where(keep, ea * pltpu.roll(ea, shift, 0), ea)
        s *= 2
    edge_s[...] = _shift_up(eb, 1) if reverse else _shift_down(eb, 1)

    def eight_groups(i, carry):
        for k in range(8):
            j = i * 8 + k
            rows = pl.ds(pl.multiple_of(j * 8, 8), 8)
            out_ref[rows, :] = b_s[rows, :] + a_s[rows, :] * edge_s[pl.ds(j, 1), :]
        return carry

    lax.fori_loop(0, ng // 8, eight_groups, 0)


def _neg_expm1(v):
    series = -v * (1.0 + v * (0.5 + v * (1.0 / 6.0)))
    return jnp.where(v > -0.015625, series, 1.0 - jnp.exp(v))


def _softplus_neg(lam):
    return jnp.maximum(-lam, 0.0) + jnp.log(1.0 + jnp.exp(-jnp.abs(lam)))


def _lru_gates(x0, cw, cb, wa, ba, wx, bx, lam):
    taps = [_shift_down(x0, 3 - k) for k in range(3)] + [x0]
    u = cb + cw[3:4, :] * x0
    for k in range(3):
        u = u + cw[k:k + 1, :] * taps[k]
    ub = u.astype(bf16)
    r = _sigmoid_positive(_dot(ub, wa.astype(bf16), NN) + ba)
    i = _sigmoid(_dot(ub, wx.astype(bf16), NN) + bx)
    sp = _softplus_neg(lam)
    log_a = (-LRU_C) * r * sp
    a = jnp.exp(log_a)
    w = _neg_expm1(2.0 * log_a)
    inv_mult = lax.rsqrt(w)
    return u, ub, r, i, sp, a, w * inv_mult, inv_mult, taps


def _lru_specs(S, nb):
    col = lambda off: pl.BlockSpec((S, RB), lambda n, b, off=off: (b, off + n))
    vec = pl.BlockSpec((1, RB), lambda n, b: (0, n))
    wblk = pl.BlockSpec((None, RB, RB), lambda n, b: (n, 0, 0))
    cwblk = pl.BlockSpec((8, RB), lambda n, b: (n, 0))
    return col, vec, wblk, cwblk


def _lru_forward(proj, cw_full, conv_b, w_a, b_a, w_x, b_x, lam, S):
    T = proj.shape[0]
    nb = T // S
    col, vec, wblk, cwblk = _lru_specs(S, nb)

    def body(x0_ref, g_ref, cw_ref, cb_ref, wa_ref, ba_ref, wx_ref, bx_ref, lam_ref, y_ref, h_ref, a_s, b_s, edge_s):
        x0 = x0_ref[...].astype(f32)
        u, ub, r, i, sp, a, mult, _, _ = _lru_gates(x0, cw_ref[...], cb_ref[...], wa_ref[...], ba_ref[...],
                                                    wx_ref[...], bx_ref[...], lam_ref[...])
        _linear_scan(a, mult * (i * u), a_s, b_s, edge_s, h_ref, reverse=False)
        g = g_ref[...].astype(f32)
        y_ref[...] = (h_ref[...] * (g * _sigmoid(g))).astype(bf16)

    out = pl.BlockSpec((S, RB), lambda n, b: (b, n))
    return _pcall(
        body, name="lru_forward", grid=(RNN_BLOCKS, nb),
        in_specs=[col(0), col(8), cwblk, vec, wblk, vec, wblk, vec, vec],
        out_specs=(out, out), out_shape=(_sds((T, D), bf16), _sds((T, D), f32)),
        scratch_shapes=[pltpu.VMEM((S, RB), f32), pltpu.VMEM((S, RB), f32), pltpu.VMEM((S // 8, RB), f32)],
        compiler_params=_params(("arbitrary", "arbitrary")),
    )(proj, proj, cw_full, conv_b, w_a, b_a, w_x, b_x, lam)


def _rope_tables(S):
    pos = jnp.arange(S, dtype=f32)
    inv_freq = ROPE_THETA ** (-jnp.arange(0, ROPE_DIM, 2, dtype=f32) / ROPE_DIM)
    ang = pos[:, None] * inv_freq[None, :]
    cos, sin = jnp.cos(ang), jnp.sin(ang)
    lane = jnp.arange(128) % HEAD
    cosl, sinl = cos[:, lane % 8], sin[:, lane % 8]
    c = jnp.where(lane[None, :] < ROPE_DIM, cosl, 1.0)
    s1 = jnp.where(lane[None, :] < 8, -sinl, 0.0)
    s2 = jnp.where((lane[None, :] >= 8) & (lane[None, :] < ROPE_DIM), sinl, 0.0)
    return c.astype(f32), s1.astype(f32), s2.astype(f32)


def _heads_to_rows(t):
    return jnp.concatenate([t[:, HEAD * h:HEAD * (h + 1)] for h in range(GROUP)], axis=0)


def _rows_to_heads(t):
    return jnp.concatenate([t[QB * h:QB * (h + 1), :] for h in range(GROUP)], axis=1)


def _window_bias(first_block):
    shape = (GROUP * QB, 2 * QB)
    qi = _rows_iota(shape) % QB
    cj = lax.broadcasted_iota(jnp.int32, shape, 1)
    valid = (cj > qi) & (cj <= qi + QB) & ((cj >= QB) | jnp.logical_not(first_block))
    return jnp.where(valid, 0.0, -jnp.inf)


def _attn_probs(q_rows, k_cat, sink_col, bias):
    s = _dot(q_rows, k_cat, NT) + bias
    m = jnp.maximum(jnp.max(s, axis=1, keepdims=True), sink_col)
    p = jnp.exp(s - m)
    e_sink = jnp.exp(sink_col - m)
    inv = 1.0 / (jnp.sum(p, axis=1, keepdims=True) + e_sink)
    return p * inv, e_sink * inv


def _sink_column(sink_ref, kv):
    rid = _rows_iota((GROUP * QB, 1))
    col = jnp.zeros((GROUP * QB, 1), f32)
    for h in range(GROUP):
        col = jnp.where(rid // QB == h, sink_ref[0, GROUP * kv + h], col)
    return col


def _attn_in_specs(S):
    nq = S // QB
    last = nq - 1
    cur = lambda b, j: b * nq + jnp.minimum(j, last)
    prev = lambda b, j: b * nq + jnp.maximum(jnp.minimum(j, last) - 1, 0)
    specs = [
        pl.BlockSpec((QB, D), lambda b, j: (cur(b, j), 2)),
        pl.BlockSpec((QB, 256), lambda b, j: (cur(b, j), 12)),
        pl.BlockSpec((QB, 256), lambda b, j: (prev(b, j), 12)),
        pl.BlockSpec((QB, 256), lambda b, j: (cur(b, j), 13)),
        pl.BlockSpec((QB, 256), lambda b, j: (prev(b, j), 13)),
        pl.BlockSpec((QB, 512), lambda b, j: (cur(b, j), 7)),
        pl.BlockSpec((QB, 512), lambda b, j: (cur(b, j), 8)),
        SMEM_SPEC,
    ]
    return specs, cur, prev


def _attn_forward(proj, sinks, S, out_shards):
    T = proj.shape[0]
    nb, nq = T // S, S // QB
    specs, cur, _ = _attn_in_specs(S)
    nw = len(out_shards)

    def body(q_ref, kc_ref, kp_ref, vc_ref, vp_ref, gl_ref, gh_ref, sink_ref, *rest):
        shards = rest[:nw]
        y_ref = rest[nw]
        gathered = rest[nw + 1:2 * nw + 1]
        stage = rest[2 * nw + 1:3 * nw + 1]
        sems = rest[3 * nw + 1:]
        b, j = pl.program_id(0), pl.program_id(1)

        start, forward, finish = _two_level_gather(stage, gathered, *sems)

        @pl.when((b == 0) & (j == 0))
        def _():
            for a in range(nw):
                stage[a][...] = shards[a][...].astype(bf16)
            start()

        @pl.when((b == 0) & (j == nq // 2))
        def _():
            forward()

        @pl.when((b == nb - 1) & (j == nq - 1))
        def _():
            finish()

        bias = _window_bias(j == 0)
        kc, kp, vc, vp = kc_ref[...], kp_ref[...], vc_ref[...], vp_ref[...]
        for kv in range(KV_HEADS):
            lanes = slice(256 * kv, 256 * (kv + 1))
            hl = slice(HEAD * kv, HEAD * (kv + 1))
            q_rows = _heads_to_rows(q_ref[:, lanes])
            k_cat = jnp.concatenate([kp[:, hl], kc[:, hl]], axis=0)
            v_cat = jnp.concatenate([vp[:, hl], vc[:, hl]], axis=0)
            probs, _ = _attn_probs(q_rows, k_cat, _sink_column(sink_ref, kv), bias)
            o = _rows_to_heads(_dot(probs.astype(bf16), v_cat, NN))
            g_src = gl_ref if kv < 2 else gh_ref
            g = g_src[:, 256 * (kv % 2):256 * (kv % 2 + 1)].astype(f32)
            y_ref[:, lanes] = (o * (g * _sigmoid(g))).astype(bf16)

    args = [proj] * 7 + [sinks] + list(out_shards)
    res = _pcall(
        body, name="attn_forward", grid=(nb, nq),
        in_specs=specs + [pl.BlockSpec(w.shape, lambda b, j: (0, 0)) for w in out_shards],
        out_specs=(pl.BlockSpec((QB, D), lambda b, j: (cur(b, j), 0)),) + tuple([ANY] * nw),
        out_shape=(_sds((T, D), bf16),) + tuple(_sds((NDEV * w.shape[0], w.shape[1]), bf16) for w in out_shards),
        scratch_shapes=[pltpu.VMEM(w.shape, bf16) for w in out_shards] + _exchange_scratch(nw, 7),
        compiler_params=_params(("arbitrary", "arbitrary")),
    )(*args)
    return res[0], res[1:]


def _merge_and_head(x2d, tgt, proj, y_rnn, y_attn, w_r, w_a, w_o, gfin):
    T = x2d.shape[0]
    tb = min(T, 512)
    nsteps = T // tb

    def body(x_ref, t_ref, mr0, mr1, ma0, ma1, yr_ref, ya_ref, wr_ref, wa_ref, wo_ref, gf_ref,
             dx2_ref, dyr_ref, dya_ref, dmr_ref, dma_ref, loss_ref, gfin_ref, gwr_out, gwa_out, gwo_out,
             gwr_acc, gwa_acc, gwo_acc, out_sems):
        step = pl.program_id(0)

        @pl.when(step == 0)
        def _():
            loss_ref[...] = jnp.zeros_like(loss_ref)
            gfin_ref[...] = jnp.zeros_like(gfin_ref)
            gwr_acc[...] = jnp.zeros_like(gwr_acc)
            gwa_acc[...] = jnp.zeros_like(gwa_acc)
            gwo_acc[...] = jnp.zeros_like(gwo_acc)

        sr = _sigmoid(jnp.concatenate([mr0[...], mr1[...]], axis=1).astype(f32))
        sa = _sigmoid(jnp.concatenate([ma0[...], ma1[...]], axis=1).astype(f32))
        p_r = _dot(yr_ref[...], wr_ref[...], NN)
        p_a = _dot(ya_ref[...], wa_ref[...], NN)
        merged = (sr * p_r + sa * p_a).astype(bf16)
        x2 = x_ref[...] + _dot(merged, wo_ref[...], NN)
        rstd = lax.rsqrt(jnp.mean(x2 * x2, axis=-1, keepdims=True) + EPS)
        xh = x2 * rstd
        gf = gf_ref[...]
        err = xh * gf - t_ref[...]
        loss_ref[...] += jnp.sum(err * err)
        dy = err * (1.0 / D)
        gfin_ref[0:1, :] += jnp.sum(dy * xh, axis=0, keepdims=True)
        dxn = dy * gf
        dx2 = rstd * (dxn - xh * jnp.mean(dxn * xh, axis=-1, keepdims=True))
        dx2_ref[...] = dx2
        dx2b = dx2.astype(bf16)
        dmerged = _dot(dx2b, wo_ref[...], NT)
        dmr_ref[...] = (dmerged * p_r * (sr * (1.0 - sr))).astype(bf16)
        dma_ref[...] = (dmerged * p_a * (sa * (1.0 - sa))).astype(bf16)
        dpr = (dmerged * sr).astype(bf16)
        dpa = (dmerged * sa).astype(bf16)
        dyr_ref[...] = _dot(dpr, wr_ref[...], NT).astype(bf16)
        dya_ref[...] = _dot(dpa, wa_ref[...], NT).astype(bf16)
        gwr_acc[...] += _dot(yr_ref[...], dpr, TN)
        gwa_acc[...] += _dot(ya_ref[...], dpa, TN)
        gwo_acc[...] += _dot(merged, dx2b, TN)

        @pl.when(step == nsteps - 1)
        def _():
            copies = [pltpu.make_async_copy(src, dst, out_sems.at[k]) for k, (src, dst) in enumerate(
                ((gwr_acc, gwr_out), (gwa_acc, gwa_out), (gwo_acc, gwo_out)))]
            for cp in copies:
                cp.start()
            for cp in copies:
                cp.wait()

    tok = pl.BlockSpec((tb, D), lambda i: (i, 0))
    half = lambda c: pl.BlockSpec((tb, CH), lambda i, c=c: (i, c))
    wfull = pl.BlockSpec((D, D), lambda i: (0, 0), pipeline_mode=pl.Buffered(1))
    acc = pl.BlockSpec((8, D), lambda i: (0, 0))
    return _pcall(
        body, name="merge_and_head", grid=(nsteps,),
        in_specs=[tok, tok, half(9), half(10), half(11), half(12), tok, tok, wfull, wfull, wfull,
                  pl.BlockSpec((1, D), lambda i: (0, 0))],
        out_specs=(tok, tok, tok, tok, tok, acc, acc, ANY, ANY, ANY),
        out_shape=(_sds((T, D), f32), _sds((T, D), bf16), _sds((T, D), bf16), _sds((T, D), bf16),
                   _sds((T, D), bf16), _sds((8, D), f32), _sds((8, D), f32),
                   _sds((D, D), f32), _sds((D, D), f32), _sds((D, D), f32)),
        scratch_shapes=[pltpu.VMEM((D, D), f32)] * 3 + [pltpu.SemaphoreType.DMA((3,))],
        compiler_params=_params(("arbitrary",)),
    )(x2d, tgt, proj, proj, proj, proj, y_rnn, y_attn, w_r, w_a, w_o, gfin)


def _attn_backward(proj, dy_attn, tabs, sinks, S, chip_sums):
    T = proj.shape[0]
    nb, nq = T // S, S // QB
    nex = len(chip_sums)
    specs, cur, prev = _attn_in_specs(S)
    last = nq - 1
    tab_cur = pl.BlockSpec((QB, 128), lambda b, j: (jnp.minimum(j, last), 0))
    tab_prev = pl.BlockSpec((QB, 128), lambda b, j: (jnp.maximum(jnp.minimum(j, last) - 1, 0), 0))
    specs = specs + [pl.BlockSpec((QB, D), lambda b, j: (cur(b, j), 0))] + [tab_cur] * 3 + [tab_prev] * 3
    q_scale = 1.0 / math.sqrt(HEAD)

    def rope_back(dt, tab):
        return jnp.concatenate([_rope_transposed(dt[:, 128 * l:128 * (l + 1)], *tab) for l in range(2)], axis=1)

    def body(q_ref, kc_ref, kp_ref, vc_ref, vp_ref, gl_ref, gh_ref, sink_ref, dy_ref, cc, s1c, s2c, cp, s1p, s2p,
             *rest):
        ex_src = rest[:nex]
        dq_ref, dkv_ref, dg_ref, dsink_ref = rest[nex:nex + 4]
        ex_dst = rest[nex + 4:2 * nex + 4]
        carry_k, carry_v = rest[2 * nex + 4:2 * nex + 6]
        sems = rest[2 * nex + 6:]
        b, j = pl.program_id(0), pl.program_id(1)

        @pl.when((b == 0) & (j == 0))
        def _():
            dsink_ref[...] = jnp.zeros_like(dsink_ref)
            _start_all(_chip_exchange_copies(ex_src, ex_dst, *sems))

        @pl.when((b == nb - 1) & (j == nq))
        def _():
            _wait_all(_chip_exchange_copies(ex_src, ex_dst, *sems))

        @pl.when(j == 0)
        def _():
            carry_k[...] = jnp.zeros_like(carry_k)
            carry_v[...] = jnp.zeros_like(carry_v)

        @pl.when(j < nq)
        def _():
            bias = _window_bias(j == 0)
            tc = (cc[...], s1c[...], s2c[...])
            tp = (cp[...], s1p[...], s2p[...])
            kc, kp, vc, vp = kc_ref[...], kp_ref[...], vc_ref[...], vp_ref[...]
            dk_prev, dk_cur, dv_prev, dv_cur = [], [], [], []
            dsink_acc = jnp.zeros((8, 128), f32)
            r8 = lax.broadcasted_iota(jnp.int32, (8, 128), 0)
            l8 = lax.broadcasted_iota(jnp.int32, (8, 128), 1)
            for kv in range(KV_HEADS):
                lanes = slice(256 * kv, 256 * (kv + 1))
                hl = slice(HEAD * kv, HEAD * (kv + 1))
                q_rows = _heads_to_rows(q_ref[:, lanes])
                k_cat = jnp.concatenate([kp[:, hl], kc[:, hl]], axis=0)
                v_cat = jnp.concatenate([vp[:, hl], vc[:, hl]], axis=0)
                probs, p_sink = _attn_probs(q_rows, k_cat, _sink_column(sink_ref, kv), bias)
                pb = probs.astype(bf16)
                o = _rows_to_heads(_dot(pb, v_cat, NN))
                g_src = gl_ref if kv < 2 else gh_ref
                g = g_src[:, 256 * (kv % 2):256 * (kv % 2 + 1)].astype(f32)
                sg = _sigmoid(g)
                dy = dy_ref[:, lanes].astype(f32)
                dg_ref[:, lanes] = (dy * o * (sg * (1.0 + g * (1.0 - sg)))).astype(bf16)
                do_rows = _heads_to_rows(dy * (g * sg)).astype(bf16)
                dv = _dot(pb, do_rows, TN)
                dp = _dot(do_rows, v_cat, NT)
                rowdot = jnp.sum(probs * dp, axis=1, keepdims=True)
                ds = (probs * (dp - rowdot)).astype(bf16)
                sink_rows = -(p_sink * rowdot)
                for h in range(GROUP):
                    val = jnp.sum(sink_rows[QB * h:QB * (h + 1), :])
                    dsink_acc = dsink_acc + jnp.where((r8 == 0) & (l8 == GROUP * kv + h), val, 0.0)
                dq = _rows_to_heads(_dot(ds, k_cat, NN)) * q_scale
                dq_ref[:, lanes] = rope_back(dq, tc).astype(bf16)
                dk = _dot(ds, q_rows, TN)
                dk_prev.append(dk[:QB, :])
                dk_cur.append(dk[QB:, :])
                dv_prev.append(dv[:QB, :])
                dv_cur.append(dv[QB:, :])
            dsink_ref[...] += dsink_acc
            dkp = rope_back(jnp.concatenate(dk_prev, axis=1), tp)
            dkc = rope_back(jnp.concatenate(dk_cur, axis=1), tc)
            dkv_ref[:, 0:256] = (carry_k[...] + dkp).astype(bf16)
            dkv_ref[:, 256:512] = (carry_v[...] + jnp.concatenate(dv_prev, axis=1)).astype(bf16)
            carry_k[...] = dkc
            carry_v[...] = jnp.concatenate(dv_cur, axis=1)

        @pl.when(j == nq)
        def _():
            dkv_ref[:, 0:256] = carry_k[...].astype(bf16)
            dkv_ref[:, 256:512] = carry_v[...].astype(bf16)

    lag = lambda b, j: (b * nq + jnp.maximum(j - 1, 0), 0)
    args = [proj] * 7 + [sinks, dy_attn] + list(tabs) + list(tabs) + list(chip_sums)
    res = _pcall(
        body, name="attn_backward", grid=(nb, nq + 1), in_specs=specs + [ANY] * nex,
        out_specs=(pl.BlockSpec((QB, D), lambda b, j: (cur(b, j), 0)), pl.BlockSpec((QB, 512), lag),
                   pl.BlockSpec((QB, D), lambda b, j: (cur(b, j), 0)), pl.BlockSpec((8, 128), lambda b, j: (0, 0)))
        + tuple([ANY] * nex),
        out_shape=(_sds((T, D), bf16), _sds((T, 512), bf16), _sds((T, D), bf16), _sds((8, 128), f32))
        + tuple(_sds(s.shape, s.dtype) for s in chip_sums),
        scratch_shapes=[pltpu.VMEM((QB, 256), f32), pltpu.VMEM((QB, 256), f32)] + _exchange_scratch(nex, 3),
        compiler_params=_params(("arbitrary", "arbitrary")),
    )(*args)
    return res[:4], res[4:]


def _lru_backward(proj, h_all, dy_rnn, cw_full, conv_b, w_a, b_a, w_x, b_x, lam, S):
    T = proj.shape[0]
    nb = T // S
    col, vec, wblk, cwblk = _lru_specs(S, nb)
    tokblk = pl.BlockSpec((S, RB), lambda n, b: (b, n))

    def body(x0_ref, g_ref, h_ref, dy_ref, cw_ref, cb_ref, wa_ref, ba_ref, wx_ref, bx_ref, lam_ref,
             du0_ref, dg_ref, gwa_ref, gwx_ref, vec_ref, gcw_ref, a_s, b_s, dh_s, edge_s):
        @pl.when(pl.program_id(1) == 0)
        def _():
            gwa_ref[...] = jnp.zeros_like(gwa_ref)
            gwx_ref[...] = jnp.zeros_like(gwx_ref)
            vec_ref[...] = jnp.zeros_like(vec_ref)
            gcw_ref[...] = jnp.zeros_like(gcw_ref)

        x0 = x0_ref[...].astype(f32)
        cw = cw_ref[...]
        lam_v = lam_ref[...]
        u, ub, r, i, sp, a, mult, inv_mult, taps = _lru_gates(x0, cw, cb_ref[...], wa_ref[...], ba_ref[...],
                                                              wx_ref[...], bx_ref[...], lam_v)
        h = h_ref[...]
        g = g_ref[...].astype(f32)
        dy = dy_ref[...].astype(f32)
        sg = _sigmoid(g)
        dg_ref[...] = (dy * h * (sg * (1.0 + g * (1.0 - sg)))).astype(bf16)
        _linear_scan(_shift_up(a, 1), dy * (g * sg), a_s, b_s, edge_s, dh_s, reverse=True)
        dh_total = dh_s[...]
        da = dh_total * _shift_down(h, 1)
        dmult = dh_total * (i * u)
        db = dh_total * mult
        di = db * u
        du = db * i
        dlog_a_c = ((-LRU_C) * a) * (da - dmult * (a * inv_mult))
        dr = dlog_a_c * sp
        dsp = jnp.sum(dlog_a_c * r, axis=0, keepdims=True)
        dpre_r = dr * r * (1.0 - r)
        dpre_i = di * i * (1.0 - i)
        dpre_rb = dpre_r.astype(bf16)
        dpre_ib = dpre_i.astype(bf16)
        du = du + _dot(dpre_rb, wa_ref[...].astype(bf16), NT) + _dot(dpre_ib, wx_ref[...].astype(bf16), NT)
        gwa_ref[...] += _dot(ub, dpre_rb, TN)
        gwx_ref[...] += _dot(ub, dpre_ib, TN)
        vec_ref[0:1, :] += jnp.sum(du, axis=0, keepdims=True)
        vec_ref[1:2, :] += jnp.sum(dpre_r, axis=0, keepdims=True)
        vec_ref[2:3, :] += jnp.sum(dpre_i, axis=0, keepdims=True)
        vec_ref[3:4, :] += dsp * (-_sigmoid(-lam_v))
        dx0 = cw[3:4, :] * du
        for k in range(3):
            dx0 = dx0 + cw[k:k + 1, :] * _shift_up(du, 3 - k)
        for k in range(4):
            gcw_ref[k:k + 1, :] += jnp.sum(du * taps[k], axis=0, keepdims=True)
        du0_ref[...] = dx0.astype(bf16)

    wacc = pl.BlockSpec((RB, RB), lambda n, b: (0, n))
    vacc = pl.BlockSpec((8, RB), lambda n, b: (0, n))
    cacc = pl.BlockSpec((8, RB), lambda n, b: (n, 0))
    return _pcall(
        body, name="lru_backward", grid=(RNN_BLOCKS, nb),
        in_specs=[col(0), col(8), tokblk, tokblk, cwblk, vec, wblk, vec, wblk, vec, vec],
        out_specs=(tokblk, tokblk, wacc, wacc, vacc, cacc),
        out_shape=(_sds((T, D), bf16), _sds((T, D), bf16), _sds((RB, D), f32), _sds((RB, D), f32),
                   _sds((8, D), f32), _sds((8 * RNN_BLOCKS, RB), f32)),
        scratch_shapes=[pltpu.VMEM((S, RB), f32)] * 3 + [pltpu.VMEM((S // 8, RB), f32)],
        compiler_params=_params(("arbitrary", "arbitrary")),
    )(proj, proj, h_all, dy_rnn, cw_full, conv_b, w_a, b_a, w_x, b_x, lam)


def _section_of_chunk(s):
    out = []
    for start, n in zip(SEC_START, SEC_CHUNKS):
        inside = (s >= start) & (s < start + n)
        out.append((inside, jnp.clip(s - start, 0, n - 1)))
    return out


EFFECT = pltpu.SideEffectType.DATAFLOW_SIDE_EFFECTING
HBM_SPEC = pl.BlockSpec(memory_space=pltpu.HBM)
SEM_SPEC = pl.BlockSpec(memory_space=pltpu.SEMAPHORE)


def _split_exchange_copies(src_ref, land_ref, send_sems, recv_sems):
    x, y, c = _my_place()
    copies = []
    for k in (3, 1, 2):
        px, py = (x + (k >> 1)) % 2, (y + (k & 1)) % 2
        copies.append(pltpu.make_async_remote_copy(
            src_ref=src_ref.at[2 * px + py], dst_ref=land_ref.at[k - 1], send_sem=send_sems[k - 1],
            recv_sem=recv_sems[k - 1], device_id=(px, py, c), device_id_type=MESH))
    return copies


def _exchange_start(chip_sum):
    _, r, cols = chip_sum.shape

    def body(src_ref, land_ref, s0, s1, s2, r0, r1, r2, src_thru, land_thru, token):
        for cp in _split_exchange_copies(src_ref, land_ref, (s0, s1, s2), (r0, r1, r2)):
            cp.start()
        token[...] = jnp.zeros_like(token)

    land = pltpu.with_memory_space_constraint(lax.empty((3, r, cols), chip_sum.dtype), pltpu.HBM)
    res = _pcall(
        body, name="exchange_start",
        out_shape=tuple([pltpu.SemaphoreType.DMA(())] * 6) + (
            pltpu.HBM(chip_sum.shape, chip_sum.dtype), pltpu.HBM((3, r, cols), chip_sum.dtype), _sds((8, 128), f32)),
        in_specs=(HBM_SPEC, HBM_SPEC), out_specs=tuple([SEM_SPEC] * 6) + (HBM_SPEC, HBM_SPEC, VMEM_SPEC),
        input_output_aliases={0: 6, 1: 7},
        compiler_params=pltpu.CompilerParams(has_side_effects=EFFECT),
    )(pltpu.with_memory_space_constraint(chip_sum, pltpu.HBM), land)
    return res[:6], res[6], res[7], res[8]


def _exchange_wait(sems, src_thru, land_thru, after):
    def body(src_ref, land_ref, s0, s1, s2, r0, r1, r2, after_ref, src_dead, got_ref):
        for cp in _split_exchange_copies(src_ref, land_ref, (s0, s1, s2), (r0, r1, r2)):
            cp.wait_send()
            cp.wait_recv()

    return _pcall(
        body, name="exchange_wait",
        out_shape=(pltpu.HBM(src_thru.shape, src_thru.dtype), pltpu.HBM(land_thru.shape, land_thru.dtype)),
        in_specs=(HBM_SPEC, HBM_SPEC) + tuple([SEM_SPEC] * 6) + (ANY,), out_specs=(HBM_SPEC, HBM_SPEC),
        input_output_aliases={0: 0, 1: 1},
        compiler_params=pltpu.CompilerParams(has_side_effects=EFFECT),
    )(src_thru, land_thru, *sems, after)[1]


def _input_grad(dsecs, wt_full, x2d, dx2, norm_g):
    T = x2d.shape[0]
    tb = min(T, 512)
    nsec = len(dsecs)
    ntok = T // tb

    def body(*refs):
        secs = refs[:nsec]
        wt_ref, x_ref, dx2_ref, g_ref, dx_ref, gnorm_ref = refs[nsec:]
        i = pl.program_id(0)

        @pl.when(i == 0)
        def _():
            gnorm_ref[...] = jnp.zeros_like(gnorm_ref)

        dh = None
        for a, (start, n) in enumerate(zip(SEC_START, SEC_CHUNKS)):
            part = _dot(secs[a][...], wt_ref[CH * start:CH * (start + n), :], NN)
            dh = part if dh is None else dh + part
        xv = x_ref[...]
        rstd = lax.rsqrt(jnp.mean(xv * xv, axis=-1, keepdims=True) + EPS)
        xh = xv * rstd
        gnorm_ref[0:1, :] += jnp.sum(dh * xh, axis=0, keepdims=True)
        dxn = dh * g_ref[...]
        dx_ref[...] = dx2_ref[...] + rstd * (dxn - xh * jnp.mean(dxn * xh, axis=-1, keepdims=True))

    tok = pl.BlockSpec((tb, D), lambda i: (i, 0))
    return _pcall(
        body, name="input_grad", grid=(ntok,),
        in_specs=[pl.BlockSpec((tb, sec.shape[1]), lambda i: (i, 0)) for sec in dsecs]
        + [pl.BlockSpec((D_IN, D), lambda i: (0, 0), pipeline_mode=pl.Buffered(1)), tok, tok,
           pl.BlockSpec((1, D), lambda i: (0, 0))],
        out_specs=(tok, pl.BlockSpec((8, D), lambda i: (0, 0))),
        out_shape=(_sds((T, D), f32), _sds((8, D), f32)),
        compiler_params=_params(("arbitrary",)),
    )(*dsecs, wt_full, x2d, dx2, norm_g)


def _w_in_grad(dsecs, h_bf):
    T = h_bf.shape[0]
    tk = min(T, 2048)
    nchunks = D_IN // CH
    nsec = len(dsecs)
    nt = T // tk

    def body(*refs):
        secs = refs[:nsec]
        h_ref, out_ref, acc = refs[nsec:]
        s, t = pl.program_id(0), pl.program_id(1)

        @pl.when(t == 0)
        def _():
            acc[...] = jnp.zeros_like(acc)

        h_rows = h_ref[pl.ds(pl.multiple_of(t * tk, tk), tk), :]
        for a, (start, n) in enumerate(zip(SEC_START, SEC_CHUNKS)):
            @pl.when((s >= start) & (s < start + n))
            def _(a=a):
                acc[...] += _dot(secs[a][...], h_rows, TN)

        @pl.when(t == nt - 1)
        def _():
            out_ref[...] = acc[...].astype(bf16)

    def sec_spec(a):
        def index(s, t, a=a):
            inside, local = _section_of_chunk(s)[a]
            return (jnp.where(inside, t, 0), local)
        return pl.BlockSpec((tk, CH), index)

    return _pcall(
        body, name="w_in_grad", grid=(nchunks, T // tk),
        in_specs=[sec_spec(a) for a in range(nsec)]
        + [pl.BlockSpec((T, D), lambda s, t: (0, 0), pipeline_mode=pl.Buffered(1))],
        out_specs=pl.BlockSpec((CH, D), lambda s, t: (s, 0)), out_shape=_sds((D_IN, D), bf16),
        scratch_shapes=[pltpu.VMEM((CH, D), f32)],
        compiler_params=_params(("arbitrary", "arbitrary")),
    )(*dsecs, h_bf)


SMALL_NAMES = ("lru_w_a", "lru_w_x", "conv_b", "lru_b_a", "lru_b_x", "lru_lambda", "norm_g", "final_norm_g",
               "attn_sinks", "conv_w")
MISC_ROW = {"conv_b": 0, "lru_b_a": 1, "lru_b_x": 2, "lru_lambda": 3, "norm_g": 8, "final_norm_g": 16,
            "attn_sinks": 24, "loss": 32}


def _small_step(gwa, gwx, gvec, gnorm_blk, gfin_blk, dsink_blk, loss_blk, gcw, params):
    srcs_rows = (RB // NDEV, RB // NDEV, 8, 8)
    flat = [t for n in SMALL_NAMES for t in params[n]]
    nin = 8 + len(flat)
    nout = 4 * len(SMALL_NAMES) + 1

    def body(*refs):
        gwa_ref, gwx_ref, gvec_ref, gnorm_ref, gfin_ref, dsink_ref, loss_ref, gcw_ref = refs[:8]
        prm = {n: refs[8 + 3 * k:11 + 3 * k] for k, n in enumerate(SMALL_NAMES)}
        outs = {n: refs[nin + 4 * k:nin + 4 * k + 4] for k, n in enumerate(SMALL_NAMES)}
        loss_out = refs[nin + nout - 1]
        (misc, got_a, got_x, got_m, got_c, red_a, red_x, red_m, all_a, all_x, all_m,
         sa, ra, sb, rb) = refs[nin + nout:]
        x, y, c = _my_place()
        me = 4 * x + 2 * y + c

        misc[...] = jnp.zeros_like(misc)
        misc[0:8, :] = gvec_ref[...]
        misc[8:16, :] = gnorm_ref[...]
        misc[16:24, :] = gfin_ref[...]
        misc[24:32, 0:128] = dsink_ref[...]
        misc[32:40, :] = loss_ref[...]

        srcs = (gwa_ref, gwx_ref, misc, gcw_ref)
        gots = (got_a, got_x, got_m, got_c)

        def shard(ref, rows, dev):
            return ref.at[pl.ds(pl.multiple_of(dev * rows, 8), rows), :]

        scatter = []
        for k in range(1, NDEV):
            px, py, pc = _peer(k)
            for a in range(4):
                scatter.append(pltpu.make_async_remote_copy(
                    src_ref=shard(srcs[a], srcs_rows[a], 4 * px + 2 * py + pc), dst_ref=gots[a].at[k - 1],
                    send_sem=sa.at[4 * (k - 1) + a], recv_sem=ra.at[4 * (k - 1) + a],
                    device_id=(px, py, pc), device_id_type=MESH))
        for cp in scatter:
            cp.start()
        for cp in scatter:
            cp.wait()

        def reduced(a):
            rows = srcs_rows[a]
            total = srcs[a][pl.ds(pl.multiple_of(me * rows, 8), rows), :]
            for k in range(NDEV - 1):
                total = total + gots[a][k]
            return total

        reds = (red_a, red_x, red_m)
        alls = (all_a, all_x, all_m)
        for a in range(3):
            val = reduced(a)
            reds[a][...] = val
            alls[a][pl.ds(pl.multiple_of(me * srcs_rows[a], 8), srcs_rows[a]), :] = val
        gather = []
        for k in range(1, NDEV):
            peer = _peer(k)
            for a in range(3):
                gather.append(pltpu.make_async_remote_copy(
                    src_ref=reds[a], dst_ref=shard(alls[a], srcs_rows[a], me),
                    send_sem=sb.at[3 * (k - 1) + a], recv_sem=rb.at[3 * (k - 1) + a],
                    device_id=peer, device_id_type=MESH))
        for cp in gather:
            cp.start()
        g_conv = reduced(3)[0:4, :]
        for cp in gather:
            cp.wait()

        def update(name, g, pick=lambda r: r[...]):
            w_ref, m_ref, v_ref = prm[name]
            delta, m_new, v_new = _adam_math(g, pick(w_ref), pick(m_ref), pick(v_ref))
            return g, delta, m_new, v_new

        for n in range(RNN_BLOCKS):
            lanes = slice(RB * n, RB * (n + 1))
            for name, full in (("lru_w_a", all_a), ("lru_w_x", all_x)):
                for out, val in zip(outs[name], update(name, full[:, lanes], pick=lambda r, n=n: r[n])):
                    out[n] = val
        for name in ("conv_b", "lru_b_a", "lru_b_x", "lru_lambda", "norm_g", "final_norm_g"):
            row = MISC_ROW[name]
            for out, val in zip(outs[name], update(name, all_m[row:row + 1, :])):
                out[...] = val
        row = MISC_ROW["attn_sinks"]
        for out, val in zip(outs["attn_sinks"], update("attn_sinks", all_m[row:row + 1, 0:16])):
            out[...] = val
        for out, val in zip(outs["conv_w"], update("conv_w", g_conv)):
            out[...] = val
        row = MISC_ROW["loss"]
        loss_out[...] = all_m[row:row + 8, 0:128] * (0.5 / D)

    out_shape = tuple(_sds(params[n][0].shape, f32) for n in SMALL_NAMES for _ in range(4)) + (_sds((8, 128), f32),)
    scratch = [pltpu.VMEM((64, D), f32),
               pltpu.VMEM((NDEV - 1, RB // NDEV, D), f32), pltpu.VMEM((NDEV - 1, RB // NDEV, D), f32),
               pltpu.VMEM((NDEV - 1, 8, D), f32), pltpu.VMEM((NDEV - 1, 8, RB), f32),
               pltpu.VMEM((RB // NDEV, D), f32), pltpu.VMEM((RB // NDEV, D), f32), pltpu.VMEM((8, D), f32),
               pltpu.VMEM((RB, D), f32), pltpu.VMEM((RB, D), f32), pltpu.VMEM((64, D), f32),
               pltpu.SemaphoreType.DMA((4 * (NDEV - 1),)), pltpu.SemaphoreType.DMA((4 * (NDEV - 1),)),
               pltpu.SemaphoreType.DMA((3 * (NDEV - 1),)), pltpu.SemaphoreType.DMA((3 * (NDEV - 1),))]
    res = _pcall(
        body, name="small_step", out_shape=out_shape,
        in_specs=[VMEM_SPEC] * nin, out_specs=tuple([VMEM_SPEC] * nout),
        scratch_shapes=scratch, compiler_params=_params(),
    )(gwa, gwx, gvec, gnorm_blk, gfin_blk, dsink_blk, loss_blk, gcw, *flat)
    return {n: res[4 * k:4 * k + 4] for k, n in enumerate(SMALL_NAMES)}, res[-1]


def _pad_rows(v, rows=8):
    return jnp.concatenate([v, jnp.zeros((rows - v.shape[0], v.shape[1]), v.dtype)], axis=0)


def kernel(x, norm_g, w_in, conv_w, conv_b, lru_w_a, lru_b_a, lru_w_x, lru_b_x, lru_lambda, attn_sinks, w_rnn_out, w_attn_out, w_o, final_norm_g, loss_target, m_norm_g, m_w_in, m_conv_w, m_conv_b, m_lru_w_a, m_lru_b_a, m_lru_w_x, m_lru_b_x, m_lru_lambda, m_attn_sinks, m_w_rnn_out, m_w_attn_out, m_w_o, m_final_norm_g, v_norm_g, v_w_in, v_conv_w, v_conv_b, v_lru_w_a, v_lru_b_a, v_lru_w_x, v_lru_b_x, v_lru_lambda, v_attn_sinks, v_w_rnn_out, v_w_attn_out, v_w_o, v_final_norm_g):
    nb, S, _ = x.shape
    T = nb * S
    x2d = x.reshape(T, D)
    tgt = loss_target.reshape(T, D)
    fin_g = final_norm_g.reshape(1, D)
    w_a3, w_x3 = lru_w_a[0], lru_w_x[0]

    my_core = lax.axis_index("c").astype(jnp.int32).reshape(1)
    cx, cy = lax.axis_index("x"), lax.axis_index("y")
    chip_order = jnp.stack([2 * cx + cy, 2 * (1 - cx) + cy, 2 * cx + (1 - cy),
                            2 * (1 - cx) + (1 - cy)]).astype(jnp.int32)

    tabs = _rope_tables(S)
    h_bf, proj, wt_full, cw_full, _ = _in_proj_gather(
        x2d, norm_g, w_in[0].T.astype(bf16), _pad_rows(conv_w[0]), tabs, S, (), chip_order)
    y_rnn, h_all = _lru_forward(proj, cw_full, conv_b, w_a3, lru_b_a, w_x3, lru_b_x, lru_lambda, S)
    y_attn, (wr_full, wa_full, wo_full) = _attn_forward(proj, attn_sinks, S,
                                                        (w_rnn_out[0], w_attn_out[0], w_o[0]))

    (dx2, dy_rnn, dy_attn, dmr, dma, loss_blk, gfin_blk, g_wr, g_wa, g_wo) = _merge_and_head(
        x2d, tgt, proj, y_rnn, y_attn, wr_full, wa_full, wo_full, fin_g)
    sums_out = _pair_sums([g_wr, g_wa, g_wo], [bf16, bf16, bf16], my_core, "out")

    (dq, dkv, dga, dsink_blk), (p_wr, p_wa, p_wo) = _attn_backward(proj, dy_attn, tabs, attn_sinks, S, sums_out)
    du0, dgr, gwa, gwx, gvec, gcw = _lru_backward(proj, h_all, dy_rnn, cw_full, conv_b, w_a3, lru_b_a, w_x3,
                                                  lru_b_x, lru_lambda, S)
    dsecs = (du0, dgr, dq, dkv, dga, dmr, dma)

    g_wt = _w_in_grad(dsecs, h_bf)
    (sum_in,) = _pair_sums([g_wt], [bf16], my_core, "in")
    ex_sems, sum_in, landing, token = _exchange_start(sum_in)
    grad_x2d, gnorm_blk = _input_grad(dsecs, wt_full, x2d, dx2, norm_g + token[0, 0])
    p_wt = _exchange_wait(ex_sems, sum_in, landing, gnorm_blk)
    p_wt_own = lax.dynamic_index_in_dim(sum_in, 2 * cx + cy, axis=0, keepdims=False)

    small, loss_out = _small_step(gwa, gwx, gvec, gnorm_blk, gfin_blk, dsink_blk, loss_blk, gcw, {
        "lru_w_a": (w_a3, m_lru_w_a[0], v_lru_w_a[0]), "lru_w_x": (w_x3, m_lru_w_x[0], v_lru_w_x[0]),
        "conv_b": (conv_b, m_conv_b, v_conv_b), "lru_b_a": (lru_b_a, m_lru_b_a, v_lru_b_a),
        "lru_b_x": (lru_b_x, m_lru_b_x, v_lru_b_x), "lru_lambda": (lru_lambda, m_lru_lambda, v_lru_lambda),
        "norm_g": (norm_g, m_norm_g, v_norm_g),
        "final_norm_g": (fin_g, m_final_norm_g.reshape(1, D), v_final_norm_g.reshape(1, D)),
        "attn_sinks": (attn_sinks, m_attn_sinks, v_attn_sinks),
        "conv_w": (conv_w[0], m_conv_w[0], v_conv_w[0])})

    o_wt = _adamw(p_wt, w_in[0].T, m_w_in[0].T, v_w_in[0].T, "adamw_w_in", first=p_wt_own)
    o_wr = _adamw(p_wr, w_rnn_out[0], m_w_rnn_out[0], v_w_rnn_out[0], "adamw_w_rnn_out")
    o_wa = _adamw(p_wa, w_attn_out[0], m_w_attn_out[0], v_w_attn_out[0], "adamw_w_attn_out")
    o_wo = _adamw(p_wo, w_o[0], m_w_o[0], v_w_o[0], "adamw_w_o")

    def result(kind):
        d = {n: small[n][kind] for n in ("conv_b", "lru_b_a", "lru_b_x", "lru_lambda", "norm_g", "attn_sinks")}
        d.update({n: small[n][kind][None] for n in ("lru_w_a", "lru_w_x", "conv_w")})
        d["final_norm_g"] = small["final_norm_g"][kind].reshape(D)
        d.update({"w_in": o_wt[kind].T[None], "w_rnn_out": o_wr[kind][None], "w_attn_out": o_wa[kind][None],
                  "w_o": o_wo[kind][None]})
        return d

    order = ("norm_g", "w_in", "conv_w", "conv_b", "lru_w_a", "lru_b_a", "lru_w_x", "lru_b_x", "lru_lambda",
             "attn_sinks", "w_rnn_out", "w_attn_out", "w_o", "final_norm_g")
    outs = [loss_out[0, 0], grad_x2d.reshape(nb, S, D)]
    for kind in range(4):
        d = result(kind)
        outs += [d[n] for n in order]
    return tuple(outs)
```

```python
import functools
import math

import jax
import jax.numpy as jnp
from jax import lax
from jax.experimental import pallas as pl
from jax.experimental.pallas import tpu as pltpu

f32 = jnp.float32
bf16 = jnp.bfloat16

D = 1024
D_IN = 6656
NDEV = 8
RNN_BLOCKS = 8
RB = 128
HEAD = 64
KV_HEADS = 4
GROUP = 4
QB = 128
LRU_C = 8.0
EPS = 1e-6
ROPE_DIM = 16
ROPE_THETA = 500000.0
CH = 512
SEC_START = (0, 2, 4, 6, 7, 9, 11)
SEC_CHUNKS = (2, 2, 2, 1, 2, 2, 2)
VMEM_LIMIT = 62 * 1024 * 1024

ADAM_LR, ADAM_B1, ADAM_B2, ADAM_EPS, ADAM_WD, ADAM_STEP = 0.001, 0.9, 0.999, 1e-08, 0.01, 10

MESH = pl.DeviceIdType.MESH
ANY = pl.BlockSpec(memory_space=pl.ANY)
VMEM_SPEC = pl.BlockSpec(memory_space=pltpu.VMEM)
SMEM_SPEC = pl.BlockSpec(memory_space=pltpu.SMEM)


def _pcall(body, **kw):
    return pl.pallas_call(body, **kw)


def _params(sem=None, **kw):
    if sem is not None:
        kw["dimension_semantics"] = sem
    return pltpu.CompilerParams(vmem_limit_bytes=VMEM_LIMIT, **kw)


def _sds(shape, dtype):
    return jax.ShapeDtypeStruct(shape, dtype)


def _dot(a, b, dims):
    return lax.dot_general(a, b, (dims, ((), ())), preferred_element_type=f32)


NN = ((1,), (0,))
NT = ((1,), (1,))
TN = ((0,), (0,))


def _sigmoid(v):
    return 0.5 * jnp.tanh(0.5 * v) + 0.5


def _sigmoid_positive(v):
    return 1.0 / (1.0 + jnp.exp(-v))


def _my_place():
    return lax.axis_index("x"), lax.axis_index("y"), lax.axis_index("c")


def _peer(k):
    x, y, c = _my_place()
    return (x + ((k >> 2) & 1)) % 2, (y + ((k >> 1) & 1)) % 2, (c + (k & 1)) % 2


def _direct_gather_copies(srcs, outs, send_sems, recv_sems, local_sems):
    x, y, c = _my_place()
    me = 4 * x + 2 * y + c
    local, remote = [], []
    for a, (src, out) in enumerate(zip(srcs, outs)):
        r = src.shape[0]
        mine = out.at[pl.ds(pl.multiple_of(me * r, 8), r), :]
        local.append(pltpu.make_async_copy(src, mine, local_sems.at[a]))
        for k in range(1, NDEV):
            remote.append(pltpu.make_async_remote_copy(
                src_ref=src, dst_ref=mine, send_sem=send_sems.at[7 * a + k - 1], recv_sem=recv_sems.at[7 * a + k - 1],
                device_id=_peer(k), device_id_type=MESH))
    return local, remote


def _chip_exchange_copies(src, dst, send_sems, recv_sems, local_sems):
    x, y, c = _my_place()
    local, remote = [], []
    for a in range(len(src)):
        local.append(pltpu.make_async_copy(src[a].at[2 * x + y], dst[a].at[0], local_sems.at[a]))
    for k in (3, 1, 2):
        px, py = (x + (k >> 1)) % 2, (y + (k & 1)) % 2
        for a in range(len(src)):
            remote.append(pltpu.make_async_remote_copy(
                src_ref=src[a].at[2 * px + py], dst_ref=dst[a].at[k],
                send_sem=send_sems.at[3 * a + k - 1], recv_sem=recv_sems.at[3 * a + k - 1],
                device_id=(px, py, c), device_id_type=MESH))
    return local, remote


def _exchange_scratch(narr, per_array):
    return [pltpu.SemaphoreType.DMA((per_array * narr,)), pltpu.SemaphoreType.DMA((per_array * narr,)),
            pltpu.SemaphoreType.DMA((narr,))]


def _start_all(copies):
    local, remote = copies
    for cp in local + remote:
        cp.start()


def _wait_all(copies):
    local, remote = copies
    for cp in remote + local:
        cp.wait()


def _pair_exchange(grads, name):
    narr = len(grads)
    nrows = tuple(g.shape[0] // NDEV for g in grads)
    views = [g.reshape(4, 2, r, g.shape[1]) for g, r in zip(grads, nrows)]

    def body(*refs):
        gin = refs[:narr]
        got = refs[narr:2 * narr]
        send_sems, recv_sems = refs[2 * narr:]
        x, y, c = _my_place()
        copies = [pltpu.make_async_remote_copy(
            src_ref=gin[a].at[:, pl.ds(1 - c, 1)], dst_ref=got[a],
            send_sem=send_sems.at[a], recv_sem=recv_sems.at[a],
            device_id=(x, y, 1 - c), device_id_type=MESH) for a in range(narr)]
        for cp in copies:
            cp.start()
        for cp in copies:
            cp.wait()

    out_shape = tuple(_sds((4, 1, r, g.shape[1]), g.dtype) for r, g in zip(nrows, grads))
    got = _pcall(
        body, name=name, out_shape=out_shape,
        in_specs=[ANY] * narr, out_specs=tuple([ANY] * narr),
        scratch_shapes=[pltpu.SemaphoreType.DMA((narr,)), pltpu.SemaphoreType.DMA((narr,))],
        compiler_params=_params(),
    )(*views)
    return views, [g.reshape(4, r, g.shape[3]) for g, r in zip(got, nrows)]


def _row_tile(rows, dtype):
    unit = 16 if dtype == bf16 else 8
    for cand in (256, 208, 128, 64, 40, 32, 16, 8):
        if rows % cand == 0 and cand % unit == 0:
            return cand
    return rows


def _chip_sum(view, got, my_core, out_dtype, name):
    _, _, r, cols = view.shape
    tr = _row_tile(r, out_dtype)

    def body(core_ref, mine_ref, got_ref, out_ref):
        out_ref[...] = (mine_ref[...].astype(f32) + got_ref[...].astype(f32)).astype(out_dtype)

    grid_spec = pltpu.PrefetchScalarGridSpec(
        num_scalar_prefetch=1, grid=(4, r // tr),
        in_specs=[pl.BlockSpec((None, None, tr, cols), lambda q, i, core: (q, core[0], i, 0)),
                  pl.BlockSpec((None, tr, cols), lambda q, i, core: (q, i, 0))],
        out_specs=pl.BlockSpec((None, tr, cols), lambda q, i, core: (q, i, 0)))
    return _pcall(body, name=name, grid_spec=grid_spec, out_shape=_sds((4, r, cols), out_dtype),
                  compiler_params=_params(("arbitrary", "arbitrary")))(my_core, view, got)


def _pair_sums(grads, wire_dtypes, my_core, tag):
    views, got = _pair_exchange(grads, "pair_exchange_" + tag)
    return [_chip_sum(v, g, my_core, dt, "chip_sum_%s%d" % (tag, a))
            for a, (v, g, dt) in enumerate(zip(views, got, wire_dtypes))]


def _adam_math(g, w, m, v):
    m_new = ADAM_B1 * m + (1.0 - ADAM_B1) * g
    v_new = ADAM_B2 * v + (1.0 - ADAM_B2) * (g * g)
    m_hat = m_new / (1.0 - ADAM_B1 ** ADAM_STEP)
    v_hat = v_new / (1.0 - ADAM_B2 ** ADAM_STEP)
    return -ADAM_LR * (m_hat / (jnp.sqrt(v_hat) + ADAM_EPS) + ADAM_WD * w), m_new, v_new


def _adamw(parts, w, m, v, name, first=None):
    n, rows, cols = parts.shape
    tr = _row_tile(rows, parts.dtype)
    lead = () if first is None else (first,)

    def body(*refs):
        p_ref, w_ref, m_ref, v_ref, g_out, d_out, m_out, v_out = refs[len(lead):]
        g = refs[0][...].astype(f32) if lead else p_ref[0].astype(f32)
        for s in range(0 if lead else 1, n):
            g = g + p_ref[s].astype(f32)
        g_out[...] = g
        d_out[...], m_out[...], v_out[...] = _adam_math(g, w_ref[...], m_ref[...], v_ref[...])

    blk = pl.BlockSpec((tr, cols), lambda i: (i, 0))
    return _pcall(
        body, name=name, grid=(rows // tr,),
        in_specs=[blk] * len(lead) + [pl.BlockSpec((n, tr, cols), lambda i: (0, i, 0)), blk, blk, blk],
        out_specs=(blk, blk, blk, blk), out_shape=tuple(_sds((rows, cols), f32) for _ in range(4)),
        compiler_params=_params(("arbitrary",)),
    )(*lead, parts, w, m, v)


def _rope(t, c, s1, s2):
    w = t.shape[1]
    return t * c + pltpu.roll(t, w - 8, 1) * s1 + pltpu.roll(t, 8, 1) * s2


def _rope_transposed(dt, c, s1, s2):
    w = dt.shape[1]
    return dt * c + pltpu.roll(dt * s1, 8, 1) + pltpu.roll(dt * s2, w - 8, 1)


PAIR_ROWS = D_IN // 4
SUB_COLS = ((0, 512), (512, 512), (1024, 512), (1536, 128))
Q_SLABS = range(3, 11)
K_SLABS = range(11, 13)


def _in_proj_gather(x2d, norm_g, wt_shard, cw_shard, tabs, S, out_shards, chip_order):
    T = x2d.shape[0]
    tb = min(S, 1024)
    ntok = T // tb
    nsb = S // tb
    q_scale = 1.0 / math.sqrt(HEAD)
    shard_rows = wt_shard.shape[0]
    small = (cw_shard,) + tuple(out_shards)
    nsm = len(small)

    def body(order_ref, x_ref, g_ref, c_ref, s1_ref, s2_ref, wt_hbm, *rest):
        small_in = rest[:nsm]
        h_ref, proj_ref, wt_out = rest[nsm:nsm + 3]
        small_out = rest[nsm + 3:2 * nsm + 3]
        wt_vm, h_vm = rest[2 * nsm + 3:2 * nsm + 5]
        stage = rest[2 * nsm + 5:3 * nsm + 4]
        wsend, wrecv, wlocal = rest[3 * nsm + 4:3 * nsm + 7]
        dsems = rest[3 * nsm + 7:]
        jj, i = pl.program_id(0), pl.program_id(1)
        x, y, c = _my_place()
        me, sibling = (x, y, c), (x, y, 1 - c)
        chips = [(1 - x, y), (x, 1 - y), (1 - x, 1 - y)]

        def rows(place):
            px, py, pc = place
            return wt_vm.at[pl.ds(pl.multiple_of((4 * px + 2 * py + pc) * shard_rows, 16), shard_rows), :]

        def copy(k, block, to, src=None):
            return pltpu.make_async_remote_copy(
                src_ref=rows(block) if src is None else src, dst_ref=rows(block),
                send_sem=wsend.at[k], recv_sem=wrecv.at[k], device_id=to, device_id_type=MESH)

        def small_copies():
            srcs = (small_in[0],) + tuple(stage)
            return _direct_gather_copies(srcs, small_out, *dsems)

        own = pltpu.make_async_copy(wt_hbm, rows(me), wlocal.at[0])
        keep = pltpu.make_async_copy(wt_vm, wt_out, wlocal.at[1])

        @pl.when((jj == 0) & (i == 0))
        def _():
            own.start()
            copy(0, me, sibling, src=wt_hbm).start()
            for j, chip in enumerate(chips):
                copy(1 + j, me, (*chip, c), src=wt_hbm).start()
            for a in range(nsm - 1):
                stage[a][...] = small_in[1 + a][...].astype(bf16)
            _start_all(small_copies())
            own.wait()
            copy(0, sibling, me).wait_recv()

        for j, chip in enumerate(chips):
            @pl.when((jj == 1 + j) & (i == 0))
            def _(j=j, chip=chip):
                copy(1 + j, (*chip, c), me).wait_recv()
                copy(4 + j, (*chip, c), sibling).start()
                copy(4 + j, (*chip, 1 - c), me).wait_recv()

        @pl.when((jj == 3) & (i == 0))
        def _():
            keep.start()

        @pl.when((jj == 3) & (i == ntok - 1))
        def _():
            copy(0, me, sibling, src=wt_hbm).wait_send()
            for j, chip in enumerate(chips):
                copy(1 + j, me, (*chip, c), src=wt_hbm).wait_send()
                copy(4 + j, (*chip, c), sibling).wait_send()
            _wait_all(small_copies())
            keep.wait()

        tok = pl.ds(pl.multiple_of(i * tb, tb), tb)

        @pl.when(jj == 0)
        def _():
            xv = x_ref[...]
            ms = jnp.mean(xv * xv, axis=-1, keepdims=True)
            hb = (xv * lax.rsqrt(ms + EPS) * g_ref[...]).astype(bf16)
            h_ref[...] = hb
            h_vm[tok, :] = hb

        block = order_ref[jj]
        hb = h_vm[tok, :]

        def piece(c0, w):
            w_rows = wt_vm[pl.ds(pl.multiple_of(block * PAIR_ROWS + c0, 128), w), :]
            return _dot(hb, w_rows, NT)

        @pl.when(block != 1)
        def _():
            for c0, w in SUB_COLS:
                proj_ref[:, c0:c0 + w] = piece(c0, w).astype(bf16)

        @pl.when(block == 1)
        def _():
            tab = (c_ref[...], s1_ref[...], s2_ref[...])
            for c0, w in SUB_COLS:
                acc = piece(c0, w)
                for l in range(w // 128):
                    slab = (c0 + 128 * l) // 128
                    part = acc[:, 128 * l:128 * (l + 1)]
                    if slab in Q_SLABS:
                        part = _rope(part, *tab) * q_scale
                    elif slab in K_SLABS:
                        part = _rope(part, *tab)
                    proj_ref[:, 128 * slab:128 * (slab + 1)] = part.astype(bf16)

    first_pass = lambda jj, i, order: (jnp.where(jj == 0, i, ntok - 1), 0)
    const = lambda jj, i, order: (0, 0)
    tab = pl.BlockSpec((tb, 128), lambda jj, i, order: (jnp.where(order[jj] == 1, i % nsb, 0), 0))
    grid_spec = pltpu.PrefetchScalarGridSpec(
        num_scalar_prefetch=1, grid=(4, ntok),
        in_specs=[pl.BlockSpec((tb, D), first_pass), pl.BlockSpec((1, D), const), tab, tab, tab, ANY]
        + [pl.BlockSpec(w.shape, const) for w in small],
        out_specs=(pl.BlockSpec((tb, D), first_pass),
                   pl.BlockSpec((tb, PAIR_ROWS), lambda jj, i, order: (i, order[jj])), ANY) + tuple([ANY] * nsm),
        scratch_shapes=[pltpu.VMEM((D_IN, D), bf16), pltpu.VMEM((T, D), bf16)]
        + [pltpu.VMEM(w.shape, bf16) for w in out_shards]
        + [pltpu.SemaphoreType.DMA((7,)), pltpu.SemaphoreType.DMA((7,)), pltpu.SemaphoreType.DMA((2,))]
        + _exchange_scratch(nsm, 7))
    res = _pcall(
        body, name="in_proj", grid_spec=grid_spec,
        out_shape=(_sds((T, D), bf16), _sds((T, D_IN), bf16), _sds((D_IN, D), bf16),
                   _sds((NDEV * cw_shard.shape[0], cw_shard.shape[1]), f32))
        + tuple(_sds((NDEV * w.shape[0], w.shape[1]), bf16) for w in out_shards),
        compiler_params=_params(("arbitrary", "arbitrary")),
    )(chip_order, x2d, norm_g, *tabs, wt_shard, *small)
    return res[0], res[1], res[2], res[3], res[4:]


def _rows_iota(shape):
    return lax.broadcasted_iota(jnp.int32, shape, 0)


def _shift_down(v, k):
    return jnp.where(_rows_iota(v.shape) >= k, pltpu.roll(v, k, 0), 0.0)


def _shift_up(v, k):
    n = v.shape[0]
    return jnp.where(_rows_iota(v.shape) < n - k, pltpu.roll(v, n - k, 0), 0.0)


def _linear_scan(a, b, a_s, b_s, edge_s, out_ref, reverse):
    n = a.shape[0]
    ng = n // 8
    a3, b3 = a.reshape(ng, 8, RB), b.reshape(ng, 8, RB)
    rid = lax.broadcasted_iota(jnp.int32, a3.shape, 1)
    for s in (1, 2, 4):
        keep, shift = (rid < 8 - s, 8 - s) if reverse else (rid >= s, s)
        b3 = jnp.where(keep, a3 * pltpu.roll(b3, shift, 1) + b3, b3)
        a3 = jnp.where(keep, a3 * pltpu.roll(a3, shift, 1), a3)
    a_s[...] = a3.reshape(n, RB)
    b_s[...] = b3.reshape(n, RB)
    edge = 0 if reverse else 7
    ea, eb = a_s[pl.ds(edge, ng, stride=8), :], b_s[pl.ds(edge, ng, stride=8), :]
    r = _rows_iota(ea.shape)
    s = 1
    while s < ng:
        keep, shift = (r < ng - s, ng - s) if reverse else (r >= s, s)
        eb = jnp.where(keep, ea * pltpu.roll(eb, shift, 0) + eb, eb)
        if 2 * s < ng:
            ea = jnp.where(keep, ea * pltpu.roll(ea, shift, 0), ea)
        s *= 2
    edge_s[...] = _shift_up(eb, 1) if reverse else _shift_down(eb, 1)

    def eight_groups(i, carry):
        for k in range(8):
            j = i * 8 + k
            rows = pl.ds(pl.multiple_of(j * 8, 8), 8)
            out_ref[rows, :] = b_s[rows, :] + a_s[rows, :] * edge_s[pl.ds(j, 1), :]
        return carry

    lax.fori_loop(0, ng // 8, eight_groups, 0)


def _neg_expm1(v):
    series = -v * (1.0 + v * (0.5 + v * (1.0 / 6.0)))
    return jnp.where(v > -0.015625, series, 1.0 - jnp.exp(v))


def _softplus_neg(lam):
    return jnp.maximum(-lam, 0.0) + jnp.log(1.0 + jnp.exp(-jnp.abs(lam)))


def _lru_gates(x0, cw, cb, wa, ba, wx, bx, lam):
    taps = [_shift_down(x0, 3 - k) for k in range(3)] + [x0]
    u = cb + cw[3:4, :] * x0
    for k in range(3):
        u = u + cw[k:k + 1, :] * taps[k]
    ub = u.astype(bf16)
    r = _sigmoid_positive(_dot(ub, wa.astype(bf16), NN) + ba)
    i = _sigmoid(_dot(ub, wx.astype(bf16), NN) + bx)
    sp = _softplus_neg(lam)
    log_a = (-LRU_C) * r * sp
    a = jnp.exp(log_a)
    w = _neg_expm1(2.0 * log_a)
    inv_mult = lax.rsqrt(w)
    return u, ub, r, i, sp, a, w * inv_mult, inv_mult, taps


def _lru_specs(S, nb):
    col = lambda off: pl.BlockSpec((S, RB), lambda n, b, off=off: (b, off + n))
    vec = pl.BlockSpec((1, RB), lambda n, b: (0, n))
    wblk = pl.BlockSpec((None, RB, RB), lambda n, b: (n, 0, 0))
    cwblk = pl.BlockSpec((8, RB), lambda n, b: (n, 0))
    return col, vec, wblk, cwblk


def _lru_forward(proj, cw_full, conv_b, w_a, b_a, w_x, b_x, lam, S):
    T = proj.shape[0]
    nb = T // S
    col, vec, wblk, cwblk = _lru_specs(S, nb)

    def body(x0_ref, g_ref, cw_ref, cb_ref, wa_ref, ba_ref, wx_ref, bx_ref, lam_ref, y_ref, h_ref, a_s, b_s, edge_s):
        x0 = x0_ref[...].astype(f32)
        u, ub, r, i, sp, a, mult, _, _ = _lru_gates(x0, cw_ref[...], cb_ref[...], wa_ref[...], ba_ref[...],
                                                    wx_ref[...], bx_ref[...], lam_ref[...])
        _linear_scan(a, mult * (i * u), a_s, b_s, edge_s, h_ref, reverse=False)
        g = g_ref[...].astype(f32)
        y_ref[...] = (h_ref[...] * (g * _sigmoid(g))).astype(bf16)

    out = pl.BlockSpec((S, RB), lambda n, b: (b, n))
    return _pcall(
        body, name="lru_forward", grid=(RNN_BLOCKS, nb),
        in_specs=[col(0), col(8), cwblk, vec, wblk, vec, wblk, vec, vec],
        out_specs=(out, out), out_shape=(_sds((T, D), bf16), _sds((T, D), f32)),
        scratch_shapes=[pltpu.VMEM((S, RB), f32), pltpu.VMEM((S, RB), f32), pltpu.VMEM((S // 8, RB), f32)],
        compiler_params=_params(("arbitrary", "arbitrary")),
    )(proj, proj, cw_full, conv_b, w_a, b_a, w_x, b_x, lam)


def _rope_tables(S):
    pos = jnp.arange(S, dtype=f32)
    inv_freq = ROPE_THETA ** (-jnp.arange(0, ROPE_DIM, 2, dtype=f32) / ROPE_DIM)
    ang = pos[:, None] * inv_freq[None, :]
    cos, sin = jnp.cos(ang), jnp.sin(ang)
    lane = jnp.arange(128) % HEAD
    cosl, sinl = cos[:, lane % 8], sin[:, lane % 8]
    c = jnp.where(lane[None, :] < ROPE_DIM, cosl, 1.0)
    s1 = jnp.where(lane[None, :] < 8, -sinl, 0.0)
    s2 = jnp.where((lane[None, :] >= 8) & (lane[None, :] < ROPE_DIM), sinl, 0.0)
    return c.astype(f32), s1.astype(f32), s2.astype(f32)


def _heads_to_rows(t):
    return jnp.concatenate([t[:, HEAD * h:HEAD * (h + 1)] for h in range(GROUP)], axis=0)


def _rows_to_heads(t):
    return jnp.concatenate([t[QB * h:QB * (h + 1), :] for h in range(GROUP)], axis=1)


def _window_bias(first_block):
    shape = (GROUP * QB, 2 * QB)
    qi = _rows_iota(shape) % QB
    cj = lax.broadcasted_iota(jnp.int32, shape, 1)
    valid = (cj > qi) & (cj <= qi + QB) & ((cj >= QB) | jnp.logical_not(first_block))
    return jnp.where(valid, 0.0, -jnp.inf)


def _attn_probs(q_rows, k_cat, sink_col, bias):
    s = _dot(q_rows, k_cat, NT) + bias
    m = jnp.maximum(jnp.max(s, axis=1, keepdims=True), sink_col)
    p = jnp.exp(s - m)
    e_sink = jnp.exp(sink_col - m)
    inv = 1.0 / (jnp.sum(p, axis=1, keepdims=True) + e_sink)
    return p * inv, e_sink * inv


def _sink_column(sink_ref, kv):
    rid = _rows_iota((GROUP * QB, 1))
    col = jnp.zeros((GROUP * QB, 1), f32)
    for h in range(GROUP):
        col = jnp.where(rid // QB == h, sink_ref[0, GROUP * kv + h], col)
    return col


def _attn_in_specs(S):
    nq = S // QB
    last = nq - 1
    cur = lambda b, j: b * nq + jnp.minimum(j, last)
    prev = lambda b, j: b * nq + jnp.maximum(jnp.minimum(j, last) - 1, 0)
    specs = [
        pl.BlockSpec((QB, D), lambda b, j: (cur(b, j), 2)),
        pl.BlockSpec((QB, 256), lambda b, j: (cur(b, j), 12)),
        pl.BlockSpec((QB, 256), lambda b, j: (prev(b, j), 12)),
        pl.BlockSpec((QB, 256), lambda b, j: (cur(b, j), 13)),
        pl.BlockSpec((QB, 256), lambda b, j: (prev(b, j), 13)),
        pl.BlockSpec((QB, 512), lambda b, j: (cur(b, j), 7)),
        pl.BlockSpec((QB, 512), lambda b, j: (cur(b, j), 8)),
        SMEM_SPEC,
    ]
    return specs, cur, prev


def _attn_forward(proj, sinks, S):
    T = proj.shape[0]
    nb, nq = T // S, S // QB
    specs, cur, _ = _attn_in_specs(S)

    def body(q_ref, kc_ref, kp_ref, vc_ref, vp_ref, gl_ref, gh_ref, sink_ref, y_ref):
        bias = _window_bias(pl.program_id(1) == 0)
        kc, kp, vc, vp = kc_ref[...], kp_ref[...], vc_ref[...], vp_ref[...]
        for kv in range(KV_HEADS):
            lanes = slice(256 * kv, 256 * (kv + 1))
            hl = slice(HEAD * kv, HEAD * (kv + 1))
            q_rows = _heads_to_rows(q_ref[:, lanes])
            k_cat = jnp.concatenate([kp[:, hl], kc[:, hl]], axis=0)
            v_cat = jnp.concatenate([vp[:, hl], vc[:, hl]], axis=0)
            probs, _ = _attn_probs(q_rows, k_cat, _sink_column(sink_ref, kv), bias)
            o = _rows_to_heads(_dot(probs.astype(bf16), v_cat, NN))
            g_src = gl_ref if kv < 2 else gh_ref
            g = g_src[:, 256 * (kv % 2):256 * (kv % 2 + 1)].astype(f32)
            y_ref[:, lanes] = (o * (g * _sigmoid(g))).astype(bf16)

    args = [proj] * 7 + [sinks]
    return _pcall(
        body, name="attn_forward", grid=(nb, nq), in_specs=specs,
        out_specs=pl.BlockSpec((QB, D), lambda b, j: (cur(b, j), 0)), out_shape=_sds((T, D), bf16),
        compiler_params=_params(("arbitrary", "arbitrary")),
    )(*args)


def _merge_and_head(x2d, tgt, proj, y_rnn, y_attn, w_land, gfin):
    T = x2d.shape[0]
    tb = min(T, 512)
    nsteps = T // tb

    def body(x_ref, t_ref, mr0, mr1, ma0, ma1, yr_ref, ya_ref, wr_ref, wa_ref, wo_ref, gf_ref,
             dx2_ref, dyr_ref, dya_ref, dmr_ref, dma_ref, loss_ref, gfin_ref, gwr_out, gwa_out, gwo_out,
             gwr_acc, gwa_acc, gwo_acc, out_sems):
        step = pl.program_id(0)

        @pl.when(step == 0)
        def _():
            loss_ref[...] = jnp.zeros_like(loss_ref)
            gfin_ref[...] = jnp.zeros_like(gfin_ref)
            gwr_acc[...] = jnp.zeros_like(gwr_acc)
            gwa_acc[...] = jnp.zeros_like(gwa_acc)
            gwo_acc[...] = jnp.zeros_like(gwo_acc)

        sr = _sigmoid(jnp.concatenate([mr0[...], mr1[...]], axis=1).astype(f32))
        sa = _sigmoid(jnp.concatenate([ma0[...], ma1[...]], axis=1).astype(f32))
        p_r = _dot(yr_ref[...], wr_ref[...], NN)
        p_a = _dot(ya_ref[...], wa_ref[...], NN)
        merged = (sr * p_r + sa * p_a).astype(bf16)
        x2 = x_ref[...] + _dot(merged, wo_ref[...], NN)
        rstd = lax.rsqrt(jnp.mean(x2 * x2, axis=-1, keepdims=True) + EPS)
        xh = x2 * rstd
        gf = gf_ref[...]
        err = xh * gf - t_ref[...]
        loss_ref[...] += jnp.sum(err * err)
        dy = err * (1.0 / D)
        gfin_ref[0:1, :] += jnp.sum(dy * xh, axis=0, keepdims=True)
        dxn = dy * gf
        dx2 = rstd * (dxn - xh * jnp.mean(dxn * xh, axis=-1, keepdims=True))
        dx2_ref[...] = dx2
        dx2b = dx2.astype(bf16)
        dmerged = _dot(dx2b, wo_ref[...], NT)
        dmr_ref[...] = (dmerged * p_r * (sr * (1.0 - sr))).astype(bf16)
        dma_ref[...] = (dmerged * p_a * (sa * (1.0 - sa))).astype(bf16)
        dpr = (dmerged * sr).astype(bf16)
        dpa = (dmerged * sa).astype(bf16)
        dyr_ref[...] = _dot(dpr, wr_ref[...], NT).astype(bf16)
        dya_ref[...] = _dot(dpa, wa_ref[...], NT).astype(bf16)
        gwr_acc[...] += _dot(yr_ref[...], dpr, TN)
        gwa_acc[...] += _dot(ya_ref[...], dpa, TN)
        gwo_acc[...] += _dot(merged, dx2b, TN)

        @pl.when(step == nsteps - 1)
        def _():
            copies = [pltpu.make_async_copy(src, dst, out_sems.at[k]) for k, (src, dst) in enumerate(
                ((gwr_acc, gwr_out), (gwa_acc, gwa_out), (gwo_acc, gwo_out)))]
            for cp in copies:
                cp.start()
            for cp in copies:
                cp.wait()

    tok = pl.BlockSpec((tb, D), lambda i: (i, 0))
    half = lambda c: pl.BlockSpec((tb, CH), lambda i, c=c: (i, c))
    wspec = lambda a: pl.BlockSpec((None, D, D), lambda i, a=a: (a, 0, 0), pipeline_mode=pl.Buffered(1))
    acc = pl.BlockSpec((8, D), lambda i: (0, 0))
    return _pcall(
        body, name="merge_and_head", grid=(nsteps,),
        in_specs=[tok, tok, half(9), half(10), half(11), half(12), tok, tok, wspec(0), wspec(1), wspec(2),
                  pl.BlockSpec((1, D), lambda i: (0, 0))],
        out_specs=(tok, tok, tok, tok, tok, acc, acc, ANY, ANY, ANY),
        out_shape=(_sds((T, D), f32), _sds((T, D), bf16), _sds((T, D), bf16), _sds((T, D), bf16),
                   _sds((T, D), bf16), _sds((8, D), f32), _sds((8, D), f32),
                   _sds((D, D), f32), _sds((D, D), f32), _sds((D, D), f32)),
        scratch_shapes=[pltpu.VMEM((D, D), f32)] * 3 + [pltpu.SemaphoreType.DMA((3,))],
        compiler_params=_params(("arbitrary",)),
    )(x2d, tgt, proj, proj, proj, proj, y_rnn, y_attn, w_land, w_land, w_land, gfin)


def _attn_backward(proj, dy_attn, tabs, sinks, S, chip_sums):
    T = proj.shape[0]
    nb, nq = T // S, S // QB
    nex = len(chip_sums)
    specs, cur, prev = _attn_in_specs(S)
    last = nq - 1
    tab_cur = pl.BlockSpec((QB, 128), lambda b, j: (jnp.minimum(j, last), 0))
    tab_prev = pl.BlockSpec((QB, 128), lambda b, j: (jnp.maximum(jnp.minimum(j, last) - 1, 0), 0))
    specs = specs + [pl.BlockSpec((QB, D), lambda b, j: (cur(b, j), 0))] + [tab_cur] * 3 + [tab_prev] * 3
    q_scale = 1.0 / math.sqrt(HEAD)

    def rope_back(dt, tab):
        return jnp.concatenate([_rope_transposed(dt[:, 128 * l:128 * (l + 1)], *tab) for l in range(2)], axis=1)

    def body(q_ref, kc_ref, kp_ref, vc_ref, vp_ref, gl_ref, gh_ref, sink_ref, dy_ref, cc, s1c, s2c, cp, s1p, s2p,
             *rest):
        ex_src = rest[:nex]
        dq_ref, dkv_ref, dg_ref, dsink_ref = rest[nex:nex + 4]
        ex_dst = rest[nex + 4:2 * nex + 4]
        carry_k, carry_v = rest[2 * nex + 4:2 * nex + 6]
        sems = rest[2 * nex + 6:]
        b, j = pl.program_id(0), pl.program_id(1)

        @pl.when((b == 0) & (j == 0))
        def _():
            dsink_ref[...] = jnp.zeros_like(dsink_ref)
            _start_all(_chip_exchange_copies(ex_src, ex_dst, *sems))

        @pl.when((b == nb - 1) & (j == nq))
        def _():
            _wait_all(_chip_exchange_copies(ex_src, ex_dst, *sems))

        @pl.when(j == 0)
        def _():
            carry_k[...] = jnp.zeros_like(carry_k)
            carry_v[...] = jnp.zeros_like(carry_v)

        @pl.when(j < nq)
        def _():
            bias = _window_bias(j == 0)
            tc = (cc[...], s1c[...], s2c[...])
            tp = (cp[...], s1p[...], s2p[...])
            kc, kp, vc, vp = kc_ref[...], kp_ref[...], vc_ref[...], vp_ref[...]
            dk_prev, dk_cur, dv_prev, dv_cur = [], [], [], []
            dsink_acc = jnp.zeros((8, 128), f32)
            r8 = lax.broadcasted_iota(jnp.int32, (8, 128), 0)
            l8 = lax.broadcasted_iota(jnp.int32, (8, 128), 1)
            for kv in range(KV_HEADS):
                lanes = slice(256 * kv, 256 * (kv + 1))
                hl = slice(HEAD * kv, HEAD * (kv + 1))
                q_rows = _heads_to_rows(q_ref[:, lanes])
                k_cat = jnp.concatenate([kp[:, hl], kc[:, hl]], axis=0)
                v_cat = jnp.concatenate([vp[:, hl], vc[:, hl]], axis=0)
                probs, p_sink = _attn_probs(q_rows, k_cat, _sink_column(sink_ref, kv), bias)
                pb = probs.astype(bf16)
                o = _rows_to_heads(_dot(pb, v_cat, NN))
                g_src = gl_ref if kv < 2 else gh_ref
                g = g_src[:, 256 * (kv % 2):256 * (kv % 2 + 1)].astype(f32)
                sg = _sigmoid(g)
                dy = dy_ref[:, lanes].astype(f32)
                dg_ref[:, lanes] = (dy * o * (sg * (1.0 + g * (1.0 - sg)))).astype(bf16)
                do_rows = _heads_to_rows(dy * (g * sg)).astype(bf16)
                dv = _dot(pb, do_rows, TN)
                dp = _dot(do_rows, v_cat, NT)
                rowdot = jnp.sum(probs * dp, axis=1, keepdims=True)
                ds = (probs * (dp - rowdot)).astype(bf16)
                sink_rows = -(p_sink * rowdot)
                for h in range(GROUP):
                    val = jnp.sum(sink_rows[QB * h:QB * (h + 1), :])
                    dsink_acc = dsink_acc + jnp.where((r8 == 0) & (l8 == GROUP * kv + h), val, 0.0)
                dq = _rows_to_heads(_dot(ds, k_cat, NN)) * q_scale
                dq_ref[:, lanes] = rope_back(dq, tc).astype(bf16)
                dk = _dot(ds, q_rows, TN)
                dk_prev.append(dk[:QB, :])
                dk_cur.append(dk[QB:, :])
                dv_prev.append(dv[:QB, :])
                dv_cur.append(dv[QB:, :])
            dsink_ref[...] += dsink_acc
            dkp = rope_back(jnp.concatenate(dk_prev, axis=1), tp)
            dkc = rope_back(jnp.concatenate(dk_cur, axis=1), tc)
            dkv_ref[:, 0:256] = (carry_k[...] + dkp).astype(bf16)
            dkv_ref[:, 256:512] = (carry_v[...] + jnp.concatenate(dv_prev, axis=1)).astype(bf16)
            carry_k[...] = dkc
            carry_v[...] = jnp.concatenate(dv_cur, axis=1)

        @pl.when(j == nq)
        def _():
            dkv_ref[:, 0:256] = carry_k[...].astype(bf16)
            dkv_ref[:, 256:512] = carry_v[...].astype(bf16)

    lag = lambda b, j: (b * nq + jnp.maximum(j - 1, 0), 0)
    args = [proj] * 7 + [sinks, dy_attn] + list(tabs) + list(tabs) + list(chip_sums)
    res = _pcall(
        body, name="attn_backward", grid=(nb, nq + 1), in_specs=specs + [ANY] * nex,
        out_specs=(pl.BlockSpec((QB, D), lambda b, j: (cur(b, j), 0)), pl.BlockSpec((QB, 512), lag),
                   pl.BlockSpec((QB, D), lambda b, j: (cur(b, j), 0)), pl.BlockSpec((8, 128), lambda b, j: (0, 0)))
        + tuple([ANY] * nex),
        out_shape=(_sds((T, D), bf16), _sds((T, 512), bf16), _sds((T, D), bf16), _sds((8, 128), f32))
        + tuple(_sds(s.shape, s.dtype) for s in chip_sums),
        scratch_shapes=[pltpu.VMEM((QB, 256), f32), pltpu.VMEM((QB, 256), f32)] + _exchange_scratch(nex, 3),
        compiler_params=_params(("arbitrary", "arbitrary")),
    )(*args)
    return res[:4], res[4:]


def _lru_backward(proj, h_all, dy_rnn, cw_full, conv_b, w_a, b_a, w_x, b_x, lam, S):
    T = proj.shape[0]
    nb = T // S
    col, vec, wblk, cwblk = _lru_specs(S, nb)
    tokblk = pl.BlockSpec((S, RB), lambda n, b: (b, n))

    def body(x0_ref, g_ref, h_ref, dy_ref, cw_ref, cb_ref, wa_ref, ba_ref, wx_ref, bx_ref, lam_ref,
             du0_ref, dg_ref, gwa_ref, gwx_ref, vec_ref, gcw_ref, a_s, b_s, dh_s, edge_s):
        @pl.when(pl.program_id(1) == 0)
        def _():
            gwa_ref[...] = jnp.zeros_like(gwa_ref)
            gwx_ref[...] = jnp.zeros_like(gwx_ref)
            vec_ref[...] = jnp.zeros_like(vec_ref)
            gcw_ref[...] = jnp.zeros_like(gcw_ref)

        x0 = x0_ref[...].astype(f32)
        cw = cw_ref[...]
        lam_v = lam_ref[...]
        u, ub, r, i, sp, a, mult, inv_mult, taps = _lru_gates(x0, cw, cb_ref[...], wa_ref[...], ba_ref[...],
                                                              wx_ref[...], bx_ref[...], lam_v)
        h = h_ref[...]
        g = g_ref[...].astype(f32)
        dy = dy_ref[...].astype(f32)
        sg = _sigmoid(g)
        dg_ref[...] = (dy * h * (sg * (1.0 + g * (1.0 - sg)))).astype(bf16)
        _linear_scan(_shift_up(a, 1), dy * (g * sg), a_s, b_s, edge_s, dh_s, reverse=True)
        dh_total = dh_s[...]
        da = dh_total * _shift_down(h, 1)
        dmult = dh_total * (i * u)
        db = dh_total * mult
        di = db * u
        du = db * i
        dlog_a_c = ((-LRU_C) * a) * (da - dmult * (a * inv_mult))
        dr = dlog_a_c * sp
        dsp = jnp.sum(dlog_a_c * r, axis=0, keepdims=True)
        dpre_r = dr * r * (1.0 - r)
        dpre_i = di * i * (1.0 - i)
        dpre_rb = dpre_r.astype(bf16)
        dpre_ib = dpre_i.astype(bf16)
        du = du + _dot(dpre_rb, wa_ref[...].astype(bf16), NT) + _dot(dpre_ib, wx_ref[...].astype(bf16), NT)
        gwa_ref[...] += _dot(ub, dpre_rb, TN)
        gwx_ref[...] += _dot(ub, dpre_ib, TN)
        vec_ref[0:1, :] += jnp.sum(du, axis=0, keepdims=True)
        vec_ref[1:2, :] += jnp.sum(dpre_r, axis=0, keepdims=True)
        vec_ref[2:3, :] += jnp.sum(dpre_i, axis=0, keepdims=True)
        vec_ref[3:4, :] += dsp * (-_sigmoid(-lam_v))
        dx0 = cw[3:4, :] * du
        for k in range(3):
            dx0 = dx0 + cw[k:k + 1, :] * _shift_up(du, 3 - k)
        for k in range(4):
            gcw_ref[k:k + 1, :] += jnp.sum(du * taps[k], axis=0, keepdims=True)
        du0_ref[...] = dx0.astype(bf16)

    wacc = pl.BlockSpec((RB, RB), lambda n, b: (0, n))
    vacc = pl.BlockSpec((8, RB), lambda n, b: (0, n))
    cacc = pl.BlockSpec((8, RB), lambda n, b: (n, 0))
    return _pcall(
        body, name="lru_backward", grid=(RNN_BLOCKS, nb),
        in_specs=[col(0), col(8), tokblk, tokblk, cwblk, vec, wblk, vec, wblk, vec, vec],
        out_specs=(tokblk, tokblk, wacc, wacc, vacc, cacc),
        out_shape=(_sds((T, D), bf16), _sds((T, D), bf16), _sds((RB, D), f32), _sds((RB, D), f32),
                   _sds((8, D), f32), _sds((8 * RNN_BLOCKS, RB), f32)),
        scratch_shapes=[pltpu.VMEM((S, RB), f32)] * 3 + [pltpu.VMEM((S // 8, RB), f32)],
        compiler_params=_params(("arbitrary", "arbitrary")),
    )(proj, proj, h_all, dy_rnn, cw_full, conv_b, w_a, b_a, w_x, b_x, lam)


def _section_of_chunk(s):
    out = []
    for start, n in zip(SEC_START, SEC_CHUNKS):
        inside = (s >= start) & (s < start + n)
        out.append((inside, jnp.clip(s - start, 0, n - 1)))
    return out


EFFECT = pltpu.SideEffectType.DATAFLOW_SIDE_EFFECTING
HBM_SPEC = pl.BlockSpec(memory_space=pltpu.HBM)
SEM_SPEC = pl.BlockSpec(memory_space=pltpu.SEMAPHORE)


def _split_exchange_copies(src_ref, land_ref, send_sems, recv_sems):
    x, y, c = _my_place()
    copies = []
    for k in (3, 1, 2):
        px, py = (x + (k >> 1)) % 2, (y + (k & 1)) % 2
        copies.append(pltpu.make_async_remote_copy(
            src_ref=src_ref.at[2 * px + py], dst_ref=land_ref.at[k - 1], send_sem=send_sems[k - 1],
            recv_sem=recv_sems[k - 1], device_id=(px, py, c), device_id_type=MESH))
    return copies


def _exchange_start(chip_sum):
    _, r, cols = chip_sum.shape

    def body(src_ref, land_ref, s0, s1, s2, r0, r1, r2, src_thru, land_thru, token):
        for cp in _split_exchange_copies(src_ref, land_ref, (s0, s1, s2), (r0, r1, r2)):
            cp.start()
        token[...] = jnp.zeros_like(token)

    land = pltpu.with_memory_space_constraint(lax.empty((3, r, cols), chip_sum.dtype), pltpu.HBM)
    res = _pcall(
        body, name="exchange_start",
        out_shape=tuple([pltpu.SemaphoreType.DMA(())] * 6) + (
            pltpu.HBM(chip_sum.shape, chip_sum.dtype), pltpu.HBM((3, r, cols), chip_sum.dtype), _sds((8, 128), f32)),
        in_specs=(HBM_SPEC, HBM_SPEC), out_specs=tuple([SEM_SPEC] * 6) + (HBM_SPEC, HBM_SPEC, VMEM_SPEC),
        input_output_aliases={0: 6, 1: 7},
        compiler_params=pltpu.CompilerParams(has_side_effects=EFFECT),
    )(pltpu.with_memory_space_constraint(chip_sum, pltpu.HBM), land)
    return res[:6], res[6], res[7], res[8]


def _exchange_wait(sems, src_thru, land_thru, after):
    def body(src_ref, land_ref, s0, s1, s2, r0, r1, r2, after_ref, src_dead, got_ref):
        for cp in _split_exchange_copies(src_ref, land_ref, (s0, s1, s2), (r0, r1, r2)):
            cp.wait_send()
            cp.wait_recv()

    return _pcall(
        body, name="exchange_wait",
        out_shape=(pltpu.HBM(src_thru.shape, src_thru.dtype), pltpu.HBM(land_thru.shape, land_thru.dtype)),
        in_specs=(HBM_SPEC, HBM_SPEC) + tuple([SEM_SPEC] * 6) + (ANY,), out_specs=(HBM_SPEC, HBM_SPEC),
        input_output_aliases={0: 0, 1: 1},
        compiler_params=pltpu.CompilerParams(has_side_effects=EFFECT),
    )(src_thru, land_thru, *sems, after)[1]


def _split_gather_copies(src_ref, land_ref, send_sems, recv_sems):
    x, y, c = _my_place()
    mine = land_ref.at[:, 4 * x + 2 * y + c]
    return [pltpu.make_async_remote_copy(src_ref=src_ref, dst_ref=mine, send_sem=send_sems[k - 1],
                                         recv_sem=recv_sems[k - 1], device_id=_peer(k), device_id_type=MESH)
            for k in range(1, NDEV)]


def _gather_start(block, after):
    n = NDEV - 1

    def body(src_ref, land_ref, after_ref, *rest):
        for cp in _split_gather_copies(src_ref, land_ref, rest[:n], rest[n:2 * n]):
            cp.start()
        rest[2 * n + 2][...] = jnp.zeros_like(rest[2 * n + 2])

    land_shape = (block.shape[0], NDEV) + block.shape[1:]
    land = pltpu.with_memory_space_constraint(lax.empty(land_shape, block.dtype), pltpu.HBM)
    res = _pcall(
        body, name="gather_start",
        out_shape=tuple([pltpu.SemaphoreType.DMA(())] * (2 * n)) + (
            pltpu.HBM(block.shape, block.dtype), pltpu.HBM(land_shape, block.dtype), _sds((8, 128), f32)),
        in_specs=(HBM_SPEC, HBM_SPEC, ANY), out_specs=tuple([SEM_SPEC] * (2 * n)) + (HBM_SPEC, HBM_SPEC, VMEM_SPEC),
        input_output_aliases={0: 2 * n, 1: 2 * n + 1},
        compiler_params=pltpu.CompilerParams(has_side_effects=EFFECT),
    )(pltpu.with_memory_space_constraint(block, pltpu.HBM), land, after)
    return res[:2 * n], res[2 * n], res[2 * n + 1], res[2 * n + 2]


def _gather_wait(sems, src_thru, land_thru, after):
    n = NDEV - 1

    def body(src_ref, land_ref, *rest):
        for cp in _split_gather_copies(src_ref, land_ref, rest[:n], rest[n:2 * n]):
            cp.wait_send()
            cp.wait_recv()

    return _pcall(
        body, name="gather_wait",
        out_shape=(pltpu.HBM(src_thru.shape, src_thru.dtype), pltpu.HBM(land_thru.shape, land_thru.dtype)),
        in_specs=(HBM_SPEC, HBM_SPEC) + tuple([SEM_SPEC] * (2 * n)) + (ANY,), out_specs=(HBM_SPEC, HBM_SPEC),
        input_output_aliases={0: 0, 1: 1},
        compiler_params=pltpu.CompilerParams(has_side_effects=EFFECT),
    )(src_thru, land_thru, *sems, after)[1]


def _input_grad(dsecs, wt_full, x2d, dx2, norm_g):
    T = x2d.shape[0]
    tb = min(T, 512)
    nsec = len(dsecs)
    ntok = T // tb

    def body(*refs):
        secs = refs[:nsec]
        wt_ref, x_ref, dx2_ref, g_ref, dx_ref, gnorm_ref = refs[nsec:]
        i = pl.program_id(0)

        @pl.when(i == 0)
        def _():
            gnorm_ref[...] = jnp.zeros_like(gnorm_ref)

        dh = None
        for a, (start, n) in enumerate(zip(SEC_START, SEC_CHUNKS)):
            part = _dot(secs[a][...], wt_ref[CH * start:CH * (start + n), :], NN)
            dh = part if dh is None else dh + part
        xv = x_ref[...]
        rstd = lax.rsqrt(jnp.mean(xv * xv, axis=-1, keepdims=True) + EPS)
        xh = xv * rstd
        gnorm_ref[0:1, :] += jnp.sum(dh * xh, axis=0, keepdims=True)
        dxn = dh * g_ref[...]
        dx_ref[...] = dx2_ref[...] + rstd * (dxn - xh * jnp.mean(dxn * xh, axis=-1, keepdims=True))

    tok = pl.BlockSpec((tb, D), lambda i: (i, 0))
    return _pcall(
        body, name="input_grad", grid=(ntok,),
        in_specs=[pl.BlockSpec((tb, sec.shape[1]), lambda i: (i, 0)) for sec in dsecs]
        + [pl.BlockSpec((D_IN, D), lambda i: (0, 0), pipeline_mode=pl.Buffered(1)), tok, tok,
           pl.BlockSpec((1, D), lambda i: (0, 0))],
        out_specs=(tok, pl.BlockSpec((8, D), lambda i: (0, 0))),
        out_shape=(_sds((T, D), f32), _sds((8, D), f32)),
        compiler_params=_params(("arbitrary",)),
    )(*dsecs, wt_full, x2d, dx2, norm_g)


def _w_in_grad(dsecs, h_bf):
    T = h_bf.shape[0]
    tk = min(T, 2048)
    nchunks = D_IN // CH
    nsec = len(dsecs)
    nt = T // tk

    def body(*refs):
        secs = refs[:nsec]
        h_ref, out_ref, acc = refs[nsec:]
        s, t = pl.program_id(0), pl.program_id(1)

        @pl.when(t == 0)
        def _():
            acc[...] = jnp.zeros_like(acc)

        h_rows = h_ref[pl.ds(pl.multiple_of(t * tk, tk), tk), :]
        for a, (start, n) in enumerate(zip(SEC_START, SEC_CHUNKS)):
            @pl.when((s >= start) & (s < start + n))
            def _(a=a):
                acc[...] += _dot(secs[a][...], h_rows, TN)

        @pl.when(t == nt - 1)
        def _():
            out_ref[...] = acc[...].astype(bf16)

    def sec_spec(a):
        def index(s, t, a=a):
            inside, local = _section_of_chunk(s)[a]
            return (jnp.where(inside, t, 0), local)
        return pl.BlockSpec((tk, CH), index)

    return _pcall(
        body, name="w_in_grad", grid=(nchunks, T // tk),
        in_specs=[sec_spec(a) for a in range(nsec)]
        + [pl.BlockSpec((T, D), lambda s, t: (0, 0), pipeline_mode=pl.Buffered(1))],
        out_specs=pl.BlockSpec((CH, D), lambda s, t: (s, 0)), out_shape=_sds((D_IN, D), bf16),
        scratch_shapes=[pltpu.VMEM((CH, D), f32)],
        compiler_params=_params(("arbitrary", "arbitrary")),
    )(*dsecs, h_bf)


SMALL_NAMES = ("lru_w_a", "lru_w_x", "conv_b", "lru_b_a", "lru_b_x", "lru_lambda", "norm_g", "final_norm_g",
               "attn_sinks", "conv_w")
MISC_ROW = {"conv_b": 0, "lru_b_a": 1, "lru_b_x": 2, "lru_lambda": 3, "norm_g": 8, "final_norm_g": 16,
            "attn_sinks": 24, "loss": 32}


def _small_step(gwa, gwx, gvec, gnorm_blk, gfin_blk, dsink_blk, loss_blk, gcw, params):
    srcs_rows = (RB // NDEV, RB // NDEV, 8, 8)
    flat = [t for n in SMALL_NAMES for t in params[n]]
    nin = 8 + len(flat)
    nout = 4 * len(SMALL_NAMES) + 1

    def body(*refs):
        gwa_ref, gwx_ref, gvec_ref, gnorm_ref, gfin_ref, dsink_ref, loss_ref, gcw_ref = refs[:8]
        prm = {n: refs[8 + 3 * k:11 + 3 * k] for k, n in enumerate(SMALL_NAMES)}
        outs = {n: refs[nin + 4 * k:nin + 4 * k + 4] for k, n in enumerate(SMALL_NAMES)}
        loss_out = refs[nin + nout - 1]
        (misc, got_a, got_x, got_m, got_c, red_a, red_x, red_m, all_a, all_x, all_m,
         sa, ra, sb, rb) = refs[nin + nout:]
        x, y, c = _my_place()
        me = 4 * x + 2 * y + c

        misc[...] = jnp.zeros_like(misc)
        misc[0:8, :] = gvec_ref[...]
        misc[8:16, :] = gnorm_ref[...]
        misc[16:24, :] = gfin_ref[...]
        misc[24:32, 0:128] = dsink_ref[...]
        misc[32:40, :] = loss_ref[...]

        srcs = (gwa_ref, gwx_ref, misc, gcw_ref)
        gots = (got_a, got_x, got_m, got_c)

        def shard(ref, rows, dev):
            return ref.at[pl.ds(pl.multiple_of(dev * rows, 8), rows), :]

        scatter = []
        for k in range(1, NDEV):
            px, py, pc = _peer(k)
            for a in range(4):
                scatter.append(pltpu.make_async_remote_copy(
                    src_ref=shard(srcs[a], srcs_rows[a], 4 * px + 2 * py + pc), dst_ref=gots[a].at[k - 1],
                    send_sem=sa.at[4 * (k - 1) + a], recv_sem=ra.at[4 * (k - 1) + a],
                    device_id=(px, py, pc), device_id_type=MESH))
        for cp in scatter:
            cp.start()
        for cp in scatter:
            cp.wait()

        def reduced(a):
            rows = srcs_rows[a]
            total = srcs[a][pl.ds(pl.multiple_of(me * rows, 8), rows), :]
            for k in range(NDEV - 1):
                total = total + gots[a][k]
            return total

        reds = (red_a, red_x, red_m)
        alls = (all_a, all_x, all_m)
        for a in range(3):
            val = reduced(a)
            reds[a][...] = val
            alls[a][pl.ds(pl.multiple_of(me * srcs_rows[a], 8), srcs_rows[a]), :] = val
        gather = []
        for k in range(1, NDEV):
            peer = _peer(k)
            for a in range(3):
                gather.append(pltpu.make_async_remote_copy(
                    src_ref=reds[a], dst_ref=shard(alls[a], srcs_rows[a], me),
                    send_sem=sb.at[3 * (k - 1) + a], recv_sem=rb.at[3 * (k - 1) + a],
                    device_id=peer, device_id_type=MESH))
        for cp in gather:
            cp.start()
        g_conv = reduced(3)[0:4, :]
        for cp in gather:
            cp.wait()

        def update(name, g, pick=lambda r: r[...]):
            w_ref, m_ref, v_ref = prm[name]
            delta, m_new, v_new = _adam_math(g, pick(w_ref), pick(m_ref), pick(v_ref))
            return g, delta, m_new, v_new

        for n in range(RNN_BLOCKS):
            lanes = slice(RB * n, RB * (n + 1))
            for name, full in (("lru_w_a", all_a), ("lru_w_x", all_x)):
                for out, val in zip(outs[name], update(name, full[:, lanes], pick=lambda r, n=n: r[n])):
                    out[n] = val
        for name in ("conv_b", "lru_b_a", "lru_b_x", "lru_lambda", "norm_g", "final_norm_g"):
            row = MISC_ROW[name]
            for out, val in zip(outs[name], update(name, all_m[row:row + 1, :])):
                out[...] = val
        row = MISC_ROW["attn_sinks"]
        for out, val in zip(outs["attn_sinks"], update("attn_sinks", all_m[row:row + 1, 0:16])):
            out[...] = val
        for out, val in zip(outs["conv_w"], update("conv_w", g_conv)):
            out[...] = val
        row = MISC_ROW["loss"]
        loss_out[...] = all_m[row:row + 8, 0:128] * (0.5 / D)

    out_shape = tuple(_sds(params[n][0].shape, f32) for n in SMALL_NAMES for _ in range(4)) + (_sds((8, 128), f32),)
    scratch = [pltpu.VMEM((64, D), f32),
               pltpu.VMEM((NDEV - 1, RB // NDEV, D), f32), pltpu.VMEM((NDEV - 1, RB // NDEV, D), f32),
               pltpu.VMEM((NDEV - 1, 8, D), f32), pltpu.VMEM((NDEV - 1, 8, RB), f32),
               pltpu.VMEM((RB // NDEV, D), f32), pltpu.VMEM((RB // NDEV, D), f32), pltpu.VMEM((8, D), f32),
               pltpu.VMEM((RB, D), f32), pltpu.VMEM((RB, D), f32), pltpu.VMEM((64, D), f32),
               pltpu.SemaphoreType.DMA((4 * (NDEV - 1),)), pltpu.SemaphoreType.DMA((4 * (NDEV - 1),)),
               pltpu.SemaphoreType.DMA((3 * (NDEV - 1),)), pltpu.SemaphoreType.DMA((3 * (NDEV - 1),))]
    res = _pcall(
        body, name="small_step", out_shape=out_shape,
        in_specs=[VMEM_SPEC] * nin, out_specs=tuple([VMEM_SPEC] * nout),
        scratch_shapes=scratch, compiler_params=_params(),
    )(gwa, gwx, gvec, gnorm_blk, gfin_blk, dsink_blk, loss_blk, gcw, *flat)
    return {n: res[4 * k:4 * k + 4] for k, n in enumerate(SMALL_NAMES)}, res[-1]


def _pad_rows(v, rows=8):
    return jnp.concatenate([v, jnp.zeros((rows - v.shape[0], v.shape[1]), v.dtype)], axis=0)


def kernel(x, norm_g, w_in, conv_w, conv_b, lru_w_a, lru_b_a, lru_w_x, lru_b_x, lru_lambda, attn_sinks, w_rnn_out, w_attn_out, w_o, final_norm_g, loss_target, m_norm_g, m_w_in, m_conv_w, m_conv_b, m_lru_w_a, m_lru_b_a, m_lru_w_x, m_lru_b_x, m_lru_lambda, m_attn_sinks, m_w_rnn_out, m_w_attn_out, m_w_o, m_final_norm_g, v_norm_g, v_w_in, v_conv_w, v_conv_b, v_lru_w_a, v_lru_b_a, v_lru_w_x, v_lru_b_x, v_lru_lambda, v_attn_sinks, v_w_rnn_out, v_w_attn_out, v_w_o, v_final_norm_g):
    nb, S, _ = x.shape
    T = nb * S
    x2d = x.reshape(T, D)
    tgt = loss_target.reshape(T, D)
    fin_g = final_norm_g.reshape(1, D)
    w_a3, w_x3 = lru_w_a[0], lru_w_x[0]

    my_core = lax.axis_index("c").astype(jnp.int32).reshape(1)
    cx, cy = lax.axis_index("x"), lax.axis_index("y")
    chip_order = jnp.stack([2 * cx + cy, 2 * (1 - cx) + cy, 2 * cx + (1 - cy),
                            2 * (1 - cx) + (1 - cy)]).astype(jnp.int32)

    tabs = _rope_tables(S)
    h_bf, proj, wt_full, cw_full, _ = _in_proj_gather(
        x2d, norm_g, w_in[0].T.astype(bf16), _pad_rows(conv_w[0]), tabs, S, (), chip_order)
    w_out3 = jnp.stack([w_rnn_out[0], w_attn_out[0], w_o[0]]).astype(bf16)
    g_sems, w_out3, w_land, g_token = _gather_start(w_out3, cw_full)
    y_rnn, h_all = _lru_forward(proj, cw_full, conv_b + g_token[0, 0], w_a3, lru_b_a, w_x3, lru_b_x, lru_lambda, S)
    y_attn = _attn_forward(proj, attn_sinks, S)
    w_land = _gather_wait(g_sems, w_out3, w_land, y_attn)
    w_land = lax.dynamic_update_slice(w_land, w_out3[:, None], (0, 4 * cx + 2 * cy + lax.axis_index("c"), 0, 0))
    w_land = pltpu.with_memory_space_constraint(w_land.reshape(3, D, D), pltpu.HBM)

    (dx2, dy_rnn, dy_attn, dmr, dma, loss_blk, gfin_blk, g_wr, g_wa, g_wo) = _merge_and_head(
        x2d, tgt, proj, y_rnn, y_attn, w_land, fin_g)
    sums_out = _pair_sums([g_wr, g_wa, g_wo], [bf16, bf16, bf16], my_core, "out")

    (dq, dkv, dga, dsink_blk), (p_wr, p_wa, p_wo) = _attn_backward(proj, dy_attn, tabs, attn_sinks, S, sums_out)
    du0, dgr, gwa, gwx, gvec, gcw = _lru_backward(proj, h_all, dy_rnn, cw_full, conv_b, w_a3, lru_b_a, w_x3,
                                                  lru_b_x, lru_lambda, S)
    dsecs = (du0, dgr, dq, dkv, dga, dmr, dma)

    g_wt = _w_in_grad(dsecs, h_bf)
    (sum_in,) = _pair_sums([g_wt], [bf16], my_core, "in")
    ex_sems, sum_in, landing, token = _exchange_start(sum_in)
    grad_x2d, gnorm_blk = _input_grad(dsecs, wt_full, x2d, dx2, norm_g + token[0, 0])
    p_wt = _exchange_wait(ex_sems, sum_in, landing, gnorm_blk)
    p_wt_own = lax.dynamic_index_in_dim(sum_in, 2 * cx + cy, axis=0, keepdims=False)

    small, loss_out = _small_step(gwa, gwx, gvec, gnorm_blk, gfin_blk, dsink_blk, loss_blk, gcw, {
        "lru_w_a": (w_a3, m_lru_w_a[0], v_lru_w_a[0]), "lru_w_x": (w_x3, m_lru_w_x[0], v_lru_w_x[0]),
        "conv_b": (conv_b, m_conv_b, v_conv_b), "lru_b_a": (lru_b_a, m_lru_b_a, v_lru_b_a),
        "lru_b_x": (lru_b_x, m_lru_b_x, v_lru_b_x), "lru_lambda": (lru_lambda, m_lru_lambda, v_lru_lambda),
        "norm_g": (norm_g, m_norm_g, v_norm_g),
        "final_norm_g": (fin_g, m_final_norm_g.reshape(1, D), v_final_norm_g.reshape(1, D)),
        "attn_sinks": (attn_sinks, m_attn_sinks, v_attn_sinks),
        "conv_w": (conv_w[0], m_conv_w[0], v_conv_w[0])})

    o_wt = _adamw(p_wt, w_in[0].T, m_w_in[0].T, v_w_in[0].T, "adamw_w_in", first=p_wt_own)
    o_wr = _adamw(p_wr, w_rnn_out[0], m_w_rnn_out[0], v_w_rnn_out[0], "adamw_w_rnn_out")
    o_wa = _adamw(p_wa, w_attn_out[0], m_w_attn_out[0], v_w_attn_out[0], "adamw_w_attn_out")
    o_wo = _adamw(p_wo, w_o[0], m_w_o[0], v_w_o[0], "adamw_w_o")

    def result(kind):
        d = {n: small[n][kind] for n in ("conv_b", "lru_b_a", "lru_b_x", "lru_lambda", "norm_g", "attn_sinks")}
        d.update({n: small[n][kind][None] for n in ("lru_w_a", "lru_w_x", "conv_w")})
        d["final_norm_g"] = small["final_norm_g"][kind].reshape(D)
        d.update({"w_in": o_wt[kind].T[None], "w_rnn_out": o_wr[kind][None], "w_attn_out": o_wa[kind][None],
                  "w_o": o_wo[kind][None]})
        return d

    order = ("norm_g", "w_in", "conv_w", "conv_b", "lru_w_a", "lru_b_a", "lru_w_x", "lru_b_x", "lru_lambda",
             "attn_sinks", "w_rnn_out", "w_attn_out", "w_o", "final_norm_g")
    outs = [loss_out[0, 0], grad_x2d.reshape(nb, S, D)]
    for kind in range(4):
        d = result(kind)
        outs += [d[n] for n in order]
    return tuple(outs)
```

```python
import functools
import math

import jax
import jax.numpy as jnp
from jax import lax
from jax.experimental import pallas as pl
from jax.experimental.pallas import tpu as pltpu

f32 = jnp.float32
bf16 = jnp.bfloat16

D = 1024
D_IN = 6656
NDEV = 8
RNN_BLOCKS = 8
RB = 128
HEAD = 64
KV_HEADS = 4
GROUP = 4
QB = 128
LRU_C = 8.0
EPS = 1e-6
ROPE_DIM = 16
ROPE_THETA = 500000.0
CH = 512
SEC_START = (0, 2, 4, 6, 7, 9, 11)
SEC_CHUNKS = (2, 2, 2, 1, 2, 2, 2)
VMEM_LIMIT = 62 * 1024 * 1024

ADAM_LR, ADAM_B1, ADAM_B2, ADAM_EPS, ADAM_WD, ADAM_STEP = 0.001, 0.9, 0.999, 1e-08, 0.01, 10

MESH = pl.DeviceIdType.MESH
ANY = pl.BlockSpec(memory_space=pl.ANY)
VMEM_SPEC = pl.BlockSpec(memory_space=pltpu.VMEM)
SMEM_SPEC = pl.BlockSpec(memory_space=pltpu.SMEM)


def _pcall(body, **kw):
    return pl.pallas_call(body, **kw)


def _params(sem=None, **kw):
    if sem is not None:
        kw["dimension_semantics"] = sem
    return pltpu.CompilerParams(vmem_limit_bytes=VMEM_LIMIT, **kw)


def _sds(shape, dtype):
    return jax.ShapeDtypeStruct(shape, dtype)


def _dot(a, b, dims):
    return lax.dot_general(a, b, (dims, ((), ())), preferred_element_type=f32)


NN = ((1,), (0,))
NT = ((1,), (1,))
TN = ((0,), (0,))


def _sigmoid(v):
    return 0.5 * jnp.tanh(0.5 * v) + 0.5


def _sigmoid_positive(v):
    return 1.0 / (1.0 + jnp.exp(-v))


def _my_place():
    return lax.axis_index("x"), lax.axis_index("y"), lax.axis_index("c")


def _peer(k):
    x, y, c = _my_place()
    return (x + ((k >> 2) & 1)) % 2, (y + ((k >> 1) & 1)) % 2, (c + (k & 1)) % 2


def _direct_gather_copies(srcs, outs, send_sems, recv_sems, local_sems):
    x, y, c = _my_place()
    me = 4 * x + 2 * y + c
    local, remote = [], []
    for a, (src, out) in enumerate(zip(srcs, outs)):
        r = src.shape[0]
        mine = out.at[pl.ds(pl.multiple_of(me * r, 8), r), :]
        local.append(pltpu.make_async_copy(src, mine, local_sems.at[a]))
        for k in range(1, NDEV):
            remote.append(pltpu.make_async_remote_copy(
                src_ref=src, dst_ref=mine, send_sem=send_sems.at[7 * a + k - 1], recv_sem=recv_sems.at[7 * a + k - 1],
                device_id=_peer(k), device_id_type=MESH))
    return local, remote


def _chip_exchange_copies(src, dst, send_sems, recv_sems, local_sems):
    x, y, c = _my_place()
    local, remote = [], []
    for a in range(len(src)):
        local.append(pltpu.make_async_copy(src[a].at[2 * x + y], dst[a].at[0], local_sems.at[a]))
    for k in (3, 1, 2):
        px, py = (x + (k >> 1)) % 2, (y + (k & 1)) % 2
        for a in range(len(src)):
            remote.append(pltpu.make_async_remote_copy(
                src_ref=src[a].at[2 * px + py], dst_ref=dst[a].at[k],
                send_sem=send_sems.at[3 * a + k - 1], recv_sem=recv_sems.at[3 * a + k - 1],
                device_id=(px, py, c), device_id_type=MESH))
    return local, remote


def _exchange_scratch(narr, per_array):
    return [pltpu.SemaphoreType.DMA((per_array * narr,)), pltpu.SemaphoreType.DMA((per_array * narr,)),
            pltpu.SemaphoreType.DMA((narr,))]


def _start_all(copies):
    local, remote = copies
    for cp in local + remote:
        cp.start()


def _wait_all(copies):
    local, remote = copies
    for cp in remote + local:
        cp.wait()


def _pair_exchange(grads, name):
    narr = len(grads)
    nrows = tuple(g.shape[0] // NDEV for g in grads)
    views = [g.reshape(4, 2, r, g.shape[1]) for g, r in zip(grads, nrows)]

    def body(*refs):
        gin = refs[:narr]
        got = refs[narr:2 * narr]
        send_sems, recv_sems = refs[2 * narr:]
        x, y, c = _my_place()
        copies = [pltpu.make_async_remote_copy(
            src_ref=gin[a].at[:, pl.ds(1 - c, 1)], dst_ref=got[a],
            send_sem=send_sems.at[a], recv_sem=recv_sems.at[a],
            device_id=(x, y, 1 - c), device_id_type=MESH) for a in range(narr)]
        for cp in copies:
            cp.start()
        for cp in copies:
            cp.wait()

    out_shape = tuple(_sds((4, 1, r, g.shape[1]), g.dtype) for r, g in zip(nrows, grads))
    got = _pcall(
        body, name=name, out_shape=out_shape,
        in_specs=[ANY] * narr, out_specs=tuple([ANY] * narr),
        scratch_shapes=[pltpu.SemaphoreType.DMA((narr,)), pltpu.SemaphoreType.DMA((narr,))],
        compiler_params=_params(),
    )(*views)
    return views, [g.reshape(4, r, g.shape[3]) for g, r in zip(got, nrows)]


def _row_tile(rows, dtype):
    unit = 16 if dtype == bf16 else 8
    for cand in (256, 208, 128, 64, 40, 32, 16, 8):
        if rows % cand == 0 and cand % unit == 0:
            return cand
    return rows


def _chip_sum(views, gots, my_core, out_dtype, name):
    narr = len(views)
    _, _, r, cols = views[0].shape
    tr = _row_tile(r, out_dtype)

    def body(core_ref, *refs):
        for a in range(narr):
            mine_ref, got_ref, out_ref = refs[a], refs[narr + a], refs[2 * narr + a]
            out_ref[...] = (mine_ref[...].astype(f32) + got_ref[...].astype(f32)).astype(out_dtype)

    slab = pl.BlockSpec((None, tr, cols), lambda q, i, core: (q, i, 0))
    grid_spec = pltpu.PrefetchScalarGridSpec(
        num_scalar_prefetch=1, grid=(4, r // tr),
        in_specs=[pl.BlockSpec((None, None, tr, cols), lambda q, i, core: (q, core[0], i, 0))] * narr + [slab] * narr,
        out_specs=tuple([slab] * narr))
    return _pcall(body, name=name, grid_spec=grid_spec,
                  out_shape=tuple(_sds((4, r, cols), out_dtype) for _ in range(narr)),
                  compiler_params=_params(("arbitrary", "arbitrary")))(my_core, *views, *gots)


def _pair_sums(grads, wire_dtype, my_core, tag):
    views, got = _pair_exchange(grads, "pair_exchange_" + tag)
    return _chip_sum(views, got, my_core, wire_dtype, "chip_sum_" + tag)


def _adam_math(g, w, m, v):
    m_new = ADAM_B1 * m + (1.0 - ADAM_B1) * g
    v_new = ADAM_B2 * v + (1.0 - ADAM_B2) * (g * g)
    m_hat = m_new / (1.0 - ADAM_B1 ** ADAM_STEP)
    v_hat = v_new / (1.0 - ADAM_B2 ** ADAM_STEP)
    return -ADAM_LR * (m_hat / (jnp.sqrt(v_hat) + ADAM_EPS) + ADAM_WD * w), m_new, v_new


def _adamw_transposed(first, parts, w, m, v, name):
    n, r, c = parts.shape

    def body(f_ref, p_ref, w_ref, m_ref, v_ref, g_out, d_out, m_out, v_out):
        g = f_ref[...].astype(f32)
        for s in range(n):
            g = g + p_ref[s].astype(f32)
        g = g.T
        g_out[...] = g
        d_out[...], m_out[...], v_out[...] = _adam_math(g, w_ref[...], m_ref[...], v_ref[...])

    return _pcall(
        body, name=name, out_shape=tuple(_sds((c, r), f32) for _ in range(4)),
        in_specs=[VMEM_SPEC] * 5, out_specs=tuple([VMEM_SPEC] * 4), compiler_params=_params(),
    )(first, parts, w, m, v)


def _adamw_group(parts, ws, ms, vs, name):
    nw = len(ws)

    def body(*refs):
        p_refs, w_refs, m_refs, v_refs = (refs[k * nw:(k + 1) * nw] for k in range(4))
        outs = refs[4 * nw:]
        for k in range(nw):
            g = p_refs[k][0].astype(f32)
            for s in range(1, p_refs[k].shape[0]):
                g = g + p_refs[k][s].astype(f32)
            g_out, d_out, m_out, v_out = outs[4 * k:4 * k + 4]
            g_out[...] = g
            d_out[...], m_out[...], v_out[...] = _adam_math(g, w_refs[k][...], m_refs[k][...], v_refs[k][...])

    res = _pcall(
        body, name=name, out_shape=tuple(_sds(w.shape, f32) for w in ws for _ in range(4)),
        in_specs=[VMEM_SPEC] * (4 * nw), out_specs=tuple([VMEM_SPEC] * (4 * nw)), compiler_params=_params(),
    )(*parts, *ws, *ms, *vs)
    return [res[4 * k:4 * k + 4] for k in range(nw)]


def _rope(t, c, s1, s2):
    w = t.shape[1]
    return t * c + pltpu.roll(t, w - 8, 1) * s1 + pltpu.roll(t, 8, 1) * s2


def _rope_transposed(dt, c, s1, s2):
    w = dt.shape[1]
    return dt * c + pltpu.roll(dt * s1, 8, 1) + pltpu.roll(dt * s2, w - 8, 1)


PAIR_ROWS = D_IN // 4
SUB_COLS = ((0, 512), (512, 512), (1024, 512), (1536, 128))
Q_SLABS = range(3, 11)
K_SLABS = range(11, 13)


def _in_proj_gather(x2d, norm_g, wt_shard, cw_shard, tabs, S, out_shards, chip_order):
    T = x2d.shape[0]
    tb = min(S, 1024)
    ntok = T // tb
    nsb = S // tb
    q_scale = 1.0 / math.sqrt(HEAD)
    shard_rows = wt_shard.shape[0]
    small = (cw_shard,) + tuple(out_shards)
    nsm = len(small)

    def body(order_ref, x_ref, g_ref, c_ref, s1_ref, s2_ref, wt_hbm, *rest):
        small_in = rest[:nsm]
        h_ref, proj_ref, wt_out = rest[nsm:nsm + 3]
        small_out = rest[nsm + 3:2 * nsm + 3]
        wt_vm, h_vm = rest[2 * nsm + 3:2 * nsm + 5]
        stage = rest[2 * nsm + 5:3 * nsm + 4]
        wsend, wrecv, wlocal = rest[3 * nsm + 4:3 * nsm + 7]
        dsems = rest[3 * nsm + 7:]
        jj, i = pl.program_id(0), pl.program_id(1)
        x, y, c = _my_place()
        me, sibling = (x, y, c), (x, y, 1 - c)
        chips = [(1 - x, y), (x, 1 - y), (1 - x, 1 - y)]

        def rows(place):
            px, py, pc = place
            return wt_vm.at[pl.ds(pl.multiple_of((4 * px + 2 * py + pc) * shard_rows, 16), shard_rows), :]

        def copy(k, block, to, src=None):
            return pltpu.make_async_remote_copy(
                src_ref=rows(block) if src is None else src, dst_ref=rows(block),
                send_sem=wsend.at[k], recv_sem=wrecv.at[k], device_id=to, device_id_type=MESH)

        def small_copies():
            srcs = (small_in[0],) + tuple(stage)
            return _direct_gather_copies(srcs, small_out, *dsems)

        own = pltpu.make_async_copy(wt_hbm, rows(me), wlocal.at[0])
        keep = pltpu.make_async_copy(wt_vm, wt_out, wlocal.at[1])

        @pl.when((jj == 0) & (i == 0))
        def _():
            own.start()
            copy(0, me, sibling, src=wt_hbm).start()
            for j, chip in enumerate(chips):
                copy(1 + j, me, (*chip, c), src=wt_hbm).start()
            for a in range(nsm - 1):
                stage[a][...] = small_in[1 + a][...].astype(bf16)
            _start_all(small_copies())
            own.wait()
            copy(0, sibling, me).wait_recv()

        for j, chip in enumerate(chips):
            @pl.when((jj == 1 + j) & (i == 0))
            def _(j=j, chip=chip):
                copy(1 + j, (*chip, c), me).wait_recv()
                copy(4 + j, (*chip, c), sibling).start()
                copy(4 + j, (*chip, 1 - c), me).wait_recv()

        @pl.when((jj == 3) & (i == 0))
        def _():
            keep.start()

        @pl.when((jj == 3) & (i == ntok - 1))
        def _():
            copy(0, me, sibling, src=wt_hbm).wait_send()
            for j, chip in enumerate(chips):
                copy(1 + j, me, (*chip, c), src=wt_hbm).wait_send()
                copy(4 + j, (*chip, c), sibling).wait_send()
            _wait_all(small_copies())
            keep.wait()

        tok = pl.ds(pl.multiple_of(i * tb, tb), tb)

        @pl.when(jj == 0)
        def _():
            xv = x_ref[...]
            ms = jnp.mean(xv * xv, axis=-1, keepdims=True)
            hb = (xv * lax.rsqrt(ms + EPS) * g_ref[...]).astype(bf16)
            h_ref[...] = hb
            h_vm[tok, :] = hb

        block = order_ref[jj]
        hb = h_vm[tok, :]

        def piece(c0, w):
            w_rows = wt_vm[pl.ds(pl.multiple_of(block * PAIR_ROWS + c0, 128), w), :]
            return _dot(hb, w_rows, NT)

        @pl.when(block != 1)
        def _():
            for c0, w in SUB_COLS:
                proj_ref[:, c0:c0 + w] = piece(c0, w).astype(bf16)

        @pl.when(block == 1)
        def _():
            tab = (c_ref[...], s1_ref[...], s2_ref[...])
            for c0, w in SUB_COLS:
                acc = piece(c0, w)
                for l in range(w // 128):
                    slab = (c0 + 128 * l) // 128
                    part = acc[:, 128 * l:128 * (l + 1)]
                    if slab in Q_SLABS:
                        part = _rope(part, *tab) * q_scale
                    elif slab in K_SLABS:
                        part = _rope(part, *tab)
                    proj_ref[:, 128 * slab:128 * (slab + 1)] = part.astype(bf16)

    first_pass = lambda jj, i, order: (jnp.where(jj == 0, i, ntok - 1), 0)
    const = lambda jj, i, order: (0, 0)
    tab = pl.BlockSpec((tb, 128), lambda jj, i, order: (jnp.where(order[jj] == 1, i % nsb, 0), 0))
    grid_spec = pltpu.PrefetchScalarGridSpec(
        num_scalar_prefetch=1, grid=(4, ntok),
        in_specs=[pl.BlockSpec((tb, D), first_pass), pl.BlockSpec((1, D), const), tab, tab, tab, ANY]
        + [pl.BlockSpec(w.shape, const) for w in small],
        out_specs=(pl.BlockSpec((tb, D), first_pass),
                   pl.BlockSpec((tb, PAIR_ROWS), lambda jj, i, order: (i, order[jj])), ANY) + tuple([ANY] * nsm),
        scratch_shapes=[pltpu.VMEM((D_IN, D), bf16), pltpu.VMEM((T, D), bf16)]
        + [pltpu.VMEM(w.shape, bf16) for w in out_shards]
        + [pltpu.SemaphoreType.DMA((7,)), pltpu.SemaphoreType.DMA((7,)), pltpu.SemaphoreType.DMA((2,))]
        + _exchange_scratch(nsm, 7))
    res = _pcall(
        body, name="in_proj", grid_spec=grid_spec,
        out_shape=(_sds((T, D), bf16), _sds((T, D_IN), bf16), _sds((D_IN, D), bf16),
                   _sds((NDEV * cw_shard.shape[0], cw_shard.shape[1]), f32))
        + tuple(_sds((NDEV * w.shape[0], w.shape[1]), bf16) for w in out_shards),
        compiler_params=_params(("arbitrary", "arbitrary")),
    )(chip_order, x2d, norm_g, *tabs, wt_shard, *small)
    return res[0], res[1], res[2], res[3], res[4:]


def _rows_iota(shape):
    return lax.broadcasted_iota(jnp.int32, shape, 0)


def _shift_down(v, k):
    return jnp.where(_rows_iota(v.shape) >= k, pltpu.roll(v, k, 0), 0.0)


def _shift_up(v, k):
    n = v.shape[0]
    return jnp.where(_rows_iota(v.shape) < n - k, pltpu.roll(v, n - k, 0), 0.0)


def _linear_scan(a, b, a_s, b_s, edge_s, out_ref, reverse):
    n = a.shape[0]
    ng = n // 8
    a3, b3 = a.reshape(ng, 8, RB), b.reshape(ng, 8, RB)
    rid = lax.broadcasted_iota(jnp.int32, a3.shape, 1)
    for s in (1, 2, 4):
        keep, shift = (rid < 8 - s, 8 - s) if reverse else (rid >= s, s)
        b3 = jnp.where(keep, a3 * pltpu.roll(b3, shift, 1) + b3, b3)
        a3 = jnp.where(keep, a3 * pltpu.roll(a3, shift, 1), a3)
    a_s[...] = a3.reshape(n, RB)
    b_s[...] = b3.reshape(n, RB)
    edge = 0 if reverse else 7
    ea, eb = a_s[pl.ds(edge, ng, stride=8), :], b_s[pl.ds(edge, ng, stride=8), :]
    r = _rows_iota(ea.shape)
    s = 1
    while s < ng:
        keep, shift = (r < ng - s, ng - s) if reverse else (r >= s, s)
        eb = jnp.where(keep, ea * pltpu.roll(eb, shift, 0) + eb, eb)
        if 2 * s < ng:
            ea = jnp.where(keep, ea * pltpu.roll(ea, shift, 0), ea)
        s *= 2
    edge_s[...] = _shift_up(eb, 1) if reverse else _shift_down(eb, 1)

    def eight_groups(i, carry):
        for k in range(8):
            j = i * 8 + k
            rows = pl.ds(pl.multiple_of(j * 8, 8), 8)
            out_ref[rows, :] = b_s[rows, :] + a_s[rows, :] * edge_s[pl.ds(j, 1), :]
        return carry

    lax.fori_loop(0, ng // 8, eight_groups, 0)


def _neg_expm1(v):
    series = -v * (1.0 + v * (0.5 + v * (1.0 / 6.0)))
    return jnp.where(v > -0.015625, series, 1.0 - jnp.exp(v))


def _softplus_neg(lam):
    return jnp.maximum(-lam, 0.0) + jnp.log(1.0 + jnp.exp(-jnp.abs(lam)))


def _lru_gates(x0, cw, cb, wa, ba, wx, bx, lam):
    taps = [_shift_down(x0, 3 - k) for k in range(3)] + [x0]
    u = cb + cw[3:4, :] * x0
    for k in range(3):
        u = u + cw[k:k + 1, :] * taps[k]
    ub = u.astype(bf16)
    r = _sigmoid_positive(_dot(ub, wa.astype(bf16), NN) + ba)
    i = _sigmoid(_dot(ub, wx.astype(bf16), NN) + bx)
    sp = _softplus_neg(lam)
    log_a = (-LRU_C) * r * sp
    a = jnp.exp(log_a)
    w = _neg_expm1(2.0 * log_a)
    inv_mult = lax.rsqrt(w)
    return u, ub, r, i, sp, a, w * inv_mult, inv_mult, taps


def _lru_specs(S, nb):
    col = lambda off: pl.BlockSpec((S, RB), lambda n, b, off=off: (b, off + n))
    vec = pl.BlockSpec((1, RB), lambda n, b: (0, n))
    wblk = pl.BlockSpec((None, RB, RB), lambda n, b: (n, 0, 0))
    cwblk = pl.BlockSpec((8, RB), lambda n, b: (n, 0))
    return col, vec, wblk, cwblk


def _lru_forward(proj, cw_full, conv_b, w_a, b_a, w_x, b_x, lam, S):
    T = proj.shape[0]
    nb = T // S
    col, vec, wblk, cwblk = _lru_specs(S, nb)

    def body(x0_ref, g_ref, cw_ref, cb_ref, wa_ref, ba_ref, wx_ref, bx_ref, lam_ref, y_ref, h_ref, a_s, b_s, edge_s):
        x0 = x0_ref[...].astype(f32)
        u, ub, r, i, sp, a, mult, _, _ = _lru_gates(x0, cw_ref[...], cb_ref[...], wa_ref[...], ba_ref[...],
                                                    wx_ref[...], bx_ref[...], lam_ref[...])
        _linear_scan(a, mult * (i * u), a_s, b_s, edge_s, h_ref, reverse=False)
        g = g_ref[...].astype(f32)
        y_ref[...] = (h_ref[...] * (g * _sigmoid(g))).astype(bf16)

    out = pl.BlockSpec((S, RB), lambda n, b: (b, n))
    return _pcall(
        body, name="lru_forward", grid=(RNN_BLOCKS, nb),
        in_specs=[col(0), col(8), cwblk, vec, wblk, vec, wblk, vec, vec],
        out_specs=(out, out), out_shape=(_sds((T, D), bf16), _sds((T, D), f32)),
        scratch_shapes=[pltpu.VMEM((S, RB), f32), pltpu.VMEM((S, RB), f32), pltpu.VMEM((S // 8, RB), f32)],
        compiler_params=_params(("arbitrary", "arbitrary")),
    )(proj, proj, cw_full, conv_b, w_a, b_a, w_x, b_x, lam)


def _rope_tables(S):
    pos = jnp.arange(S, dtype=f32)
    inv_freq = ROPE_THETA ** (-jnp.arange(0, ROPE_DIM, 2, dtype=f32) / ROPE_DIM)
    ang = pos[:, None] * inv_freq[None, :]
    cos, sin = jnp.cos(ang), jnp.sin(ang)
    lane = jnp.arange(128) % HEAD
    cosl, sinl = cos[:, lane % 8], sin[:, lane % 8]
    c = jnp.where(lane[None, :] < ROPE_DIM, cosl, 1.0)
    s1 = jnp.where(lane[None, :] < 8, -sinl, 0.0)
    s2 = jnp.where((lane[None, :] >= 8) & (lane[None, :] < ROPE_DIM), sinl, 0.0)
    return c.astype(f32), s1.astype(f32), s2.astype(f32)


def _heads_to_rows(t):
    return jnp.concatenate([t[:, HEAD * h:HEAD * (h + 1)] for h in range(GROUP)], axis=0)


def _rows_to_heads(t):
    return jnp.concatenate([t[QB * h:QB * (h + 1), :] for h in range(GROUP)], axis=1)


def _window_bias(first_block):
    shape = (GROUP * QB, 2 * QB)
    qi = _rows_iota(shape) % QB
    cj = lax.broadcasted_iota(jnp.int32, shape, 1)
    valid = (cj > qi) & (cj <= qi + QB) & ((cj >= QB) | jnp.logical_not(first_block))
    return jnp.where(valid, 0.0, -jnp.inf)


def _attn_probs(q_rows, k_cat, sink_col, bias):
    s = _dot(q_rows, k_cat, NT) + bias
    m = jnp.maximum(jnp.max(s, axis=1, keepdims=True), sink_col)
    p = jnp.exp(s - m)
    e_sink = jnp.exp(sink_col - m)
    inv = 1.0 / (jnp.sum(p, axis=1, keepdims=True) + e_sink)
    return p * inv, e_sink * inv


def _sink_column(sink_ref, kv):
    rid = _rows_iota((GROUP * QB, 1))
    col = jnp.zeros((GROUP * QB, 1), f32)
    for h in range(GROUP):
        col = jnp.where(rid // QB == h, sink_ref[0, GROUP * kv + h], col)
    return col


def _attn_in_specs(S):
    nq = S // QB
    last = nq - 1
    cur = lambda b, j: b * nq + jnp.minimum(j, last)
    prev = lambda b, j: b * nq + jnp.maximum(jnp.minimum(j, last) - 1, 0)
    specs = [
        pl.BlockSpec((QB, D), lambda b, j: (cur(b, j), 2)),
        pl.BlockSpec((QB, 256), lambda b, j: (cur(b, j), 12)),
        pl.BlockSpec((QB, 256), lambda b, j: (prev(b, j), 12)),
        pl.BlockSpec((QB, 256), lambda b, j: (cur(b, j), 13)),
        pl.BlockSpec((QB, 256), lambda b, j: (prev(b, j), 13)),
        pl.BlockSpec((QB, 512), lambda b, j: (cur(b, j), 7)),
        pl.BlockSpec((QB, 512), lambda b, j: (cur(b, j), 8)),
        SMEM_SPEC,
    ]
    return specs, cur, prev


def _attn_forward(proj, sinks, S):
    T = proj.shape[0]
    nb, nq = T // S, S // QB
    specs, cur, _ = _attn_in_specs(S)

    def body(q_ref, kc_ref, kp_ref, vc_ref, vp_ref, gl_ref, gh_ref, sink_ref, y_ref):
        bias = _window_bias(pl.program_id(1) == 0)
        kc, kp, vc, vp = kc_ref[...], kp_ref[...], vc_ref[...], vp_ref[...]
        for kv in range(KV_HEADS):
            lanes = slice(256 * kv, 256 * (kv + 1))
            hl = slice(HEAD * kv, HEAD * (kv + 1))
            q_rows = _heads_to_rows(q_ref[:, lanes])
            k_cat = jnp.concatenate([kp[:, hl], kc[:, hl]], axis=0)
            v_cat = jnp.concatenate([vp[:, hl], vc[:, hl]], axis=0)
            probs, _ = _attn_probs(q_rows, k_cat, _sink_column(sink_ref, kv), bias)
            o = _rows_to_heads(_dot(probs.astype(bf16), v_cat, NN))
            g_src = gl_ref if kv < 2 else gh_ref
            g = g_src[:, 256 * (kv % 2):256 * (kv % 2 + 1)].astype(f32)
            y_ref[:, lanes] = (o * (g * _sigmoid(g))).astype(bf16)

    args = [proj] * 7 + [sinks]
    return _pcall(
        body, name="attn_forward", grid=(nb, nq), in_specs=specs,
        out_specs=pl.BlockSpec((QB, D), lambda b, j: (cur(b, j), 0)), out_shape=_sds((T, D), bf16),
        compiler_params=_params(("arbitrary", "arbitrary")),
    )(*args)


def _merge_and_head(x2d, tgt, proj, y_rnn, y_attn, w_land, gfin):
    T = x2d.shape[0]
    tb = min(T, 512)
    nsteps = T // tb

    def body(x_ref, t_ref, mr0, mr1, ma0, ma1, yr_ref, ya_ref, wr_ref, wa_ref, wo_ref, gf_ref,
             dx2_ref, dyr_ref, dya_ref, dmr_ref, dma_ref, loss_ref, gfin_ref, gwr_out, gwa_out, gwo_out,
             gwr_acc, gwa_acc, gwo_acc, out_sems):
        step = pl.program_id(0)

        @pl.when(step == 0)
        def _():
            loss_ref[...] = jnp.zeros_like(loss_ref)
            gfin_ref[...] = jnp.zeros_like(gfin_ref)
            gwr_acc[...] = jnp.zeros_like(gwr_acc)
            gwa_acc[...] = jnp.zeros_like(gwa_acc)
            gwo_acc[...] = jnp.zeros_like(gwo_acc)

        sr = _sigmoid(jnp.concatenate([mr0[...], mr1[...]], axis=1).astype(f32))
        sa = _sigmoid(jnp.concatenate([ma0[...], ma1[...]], axis=1).astype(f32))
        p_r = _dot(yr_ref[...], wr_ref[...], NN)
        p_a = _dot(ya_ref[...], wa_ref[...], NN)
        merged = (sr * p_r + sa * p_a).astype(bf16)
        x2 = x_ref[...] + _dot(merged, wo_ref[...], NN)
        rstd = lax.rsqrt(jnp.mean(x2 * x2, axis=-1, keepdims=True) + EPS)
        xh = x2 * rstd
        gf = gf_ref[...]
        err = xh * gf - t_ref[...]
        loss_ref[...] += jnp.sum(err * err)
        dy = err * (1.0 / D)
        gfin_ref[0:1, :] += jnp.sum(dy * xh, axis=0, keepdims=True)
        dxn = dy * gf
        dx2 = rstd * (dxn - xh * jnp.mean(dxn * xh, axis=-1, keepdims=True))
        dx2_ref[...] = dx2
        dx2b = dx2.astype(bf16)
        dmerged = _dot(dx2b, wo_ref[...], NT)
        dmr_ref[...] = (dmerged * p_r * (sr * (1.0 - sr))).astype(bf16)
        dma_ref[...] = (dmerged * p_a * (sa * (1.0 - sa))).astype(bf16)
        dpr = (dmerged * sr).astype(bf16)
        dpa = (dmerged * sa).astype(bf16)
        dyr_ref[...] = _dot(dpr, wr_ref[...], NT).astype(bf16)
        dya_ref[...] = _dot(dpa, wa_ref[...], NT).astype(bf16)
        gwr_acc[...] += _dot(yr_ref[...], dpr, TN)
        gwa_acc[...] += _dot(ya_ref[...], dpa, TN)
        gwo_acc[...] += _dot(merged, dx2b, TN)

        @pl.when(step == nsteps - 1)
        def _():
            copies = [pltpu.make_async_copy(src, dst, out_sems.at[k]) for k, (src, dst) in enumerate(
                ((gwr_acc, gwr_out), (gwa_acc, gwa_out), (gwo_acc, gwo_out)))]
            for cp in copies:
                cp.start()
            for cp in copies:
                cp.wait()

    tok = pl.BlockSpec((tb, D), lambda i: (i, 0))
    half = lambda c: pl.BlockSpec((tb, CH), lambda i, c=c: (i, c))
    wspec = lambda a: pl.BlockSpec((None, D, D), lambda i, a=a: (a, 0, 0), pipeline_mode=pl.Buffered(1))
    acc = pl.BlockSpec((8, D), lambda i: (0, 0))
    return _pcall(
        body, name="merge_and_head", grid=(nsteps,),
        in_specs=[tok, tok, half(9), half(10), half(11), half(12), tok, tok, wspec(0), wspec(1), wspec(2),
                  pl.BlockSpec((1, D), lambda i: (0, 0))],
        out_specs=(tok, tok, tok, tok, tok, acc, acc, ANY, ANY, ANY),
        out_shape=(_sds((T, D), f32), _sds((T, D), bf16), _sds((T, D), bf16), _sds((T, D), bf16),
                   _sds((T, D), bf16), _sds((8, D), f32), _sds((8, D), f32),
                   _sds((D, D), f32), _sds((D, D), f32), _sds((D, D), f32)),
        scratch_shapes=[pltpu.VMEM((D, D), f32)] * 3 + [pltpu.SemaphoreType.DMA((3,))],
        compiler_params=_params(("arbitrary",)),
    )(x2d, tgt, proj, proj, proj, proj, y_rnn, y_attn, w_land, w_land, w_land, gfin)


def _attn_backward(proj, dy_attn, tabs, sinks, S, chip_sums):
    T = proj.shape[0]
    nb, nq = T // S, S // QB
    nex = len(chip_sums)
    specs, cur, prev = _attn_in_specs(S)
    last = nq - 1
    tab_cur = pl.BlockSpec((QB, 128), lambda b, j: (jnp.minimum(j, last), 0))
    tab_prev = pl.BlockSpec((QB, 128), lambda b, j: (jnp.maximum(jnp.minimum(j, last) - 1, 0), 0))
    specs = specs + [pl.BlockSpec((QB, D), lambda b, j: (cur(b, j), 0))] + [tab_cur] * 3 + [tab_prev] * 3
    q_scale = 1.0 / math.sqrt(HEAD)

    def rope_back(dt, tab):
        return jnp.concatenate([_rope_transposed(dt[:, 128 * l:128 * (l + 1)], *tab) for l in range(2)], axis=1)

    def body(q_ref, kc_ref, kp_ref, vc_ref, vp_ref, gl_ref, gh_ref, sink_ref, dy_ref, cc, s1c, s2c, cp, s1p, s2p,
             *rest):
        ex_src = rest[:nex]
        dq_ref, dkv_ref, dg_ref, dsink_ref = rest[nex:nex + 4]
        ex_dst = rest[nex + 4:2 * nex + 4]
        carry_k, carry_v = rest[2 * nex + 4:2 * nex + 6]
        sems = rest[2 * nex + 6:]
        b, j = pl.program_id(0), pl.program_id(1)

        @pl.when((b == 0) & (j == 0))
        def _():
            dsink_ref[...] = jnp.zeros_like(dsink_ref)
            _start_all(_chip_exchange_copies(ex_src, ex_dst, *sems))

        @pl.when((b == nb - 1) & (j == nq))
        def _():
            _wait_all(_chip_exchange_copies(ex_src, ex_dst, *sems))

        @pl.when(j == 0)
        def _():
            carry_k[...] = jnp.zeros_like(carry_k)
            carry_v[...] = jnp.zeros_like(carry_v)

        @pl.when(j < nq)
        def _():
            bias = _window_bias(j == 0)
            tc = (cc[...], s1c[...], s2c[...])
            tp = (cp[...], s1p[...], s2p[...])
            kc, kp, vc, vp = kc_ref[...], kp_ref[...], vc_ref[...], vp_ref[...]
            dk_prev, dk_cur, dv_prev, dv_cur = [], [], [], []
            dsink_acc = jnp.zeros((8, 128), f32)
            r8 = lax.broadcasted_iota(jnp.int32, (8, 128), 0)
            l8 = lax.broadcasted_iota(jnp.int32, (8, 128), 1)
            for kv in range(KV_HEADS):
                lanes = slice(256 * kv, 256 * (kv + 1))
                hl = slice(HEAD * kv, HEAD * (kv + 1))
                q_rows = _heads_to_rows(q_ref[:, lanes])
                k_cat = jnp.concatenate([kp[:, hl], kc[:, hl]], axis=0)
                v_cat = jnp.concatenate([vp[:, hl], vc[:, hl]], axis=0)
                probs, p_sink = _attn_probs(q_rows, k_cat, _sink_column(sink_ref, kv), bias)
                pb = probs.astype(bf16)
                o = _rows_to_heads(_dot(pb, v_cat, NN))
                g_src = gl_ref if kv < 2 else gh_ref
                g = g_src[:, 256 * (kv % 2):256 * (kv % 2 + 1)].astype(f32)
                sg = _sigmoid(g)
                dy = dy_ref[:, lanes].astype(f32)
                dg_ref[:, lanes] = (dy * o * (sg * (1.0 + g * (1.0 - sg)))).astype(bf16)
                do_rows = _heads_to_rows(dy * (g * sg)).astype(bf16)
                dv = _dot(pb, do_rows, TN)
                dp = _dot(do_rows, v_cat, NT)
                rowdot = jnp.sum(probs * dp, axis=1, keepdims=True)
                ds = (probs * (dp - rowdot)).astype(bf16)
                sink_rows = -(p_sink * rowdot)
                for h in range(GROUP):
                    val = jnp.sum(sink_rows[QB * h:QB * (h + 1), :])
                    dsink_acc = dsink_acc + jnp.where((r8 == 0) & (l8 == GROUP * kv + h), val, 0.0)
                dq = _rows_to_heads(_dot(ds, k_cat, NN)) * q_scale
                dq_ref[:, lanes] = rope_back(dq, tc).astype(bf16)
                dk = _dot(ds, q_rows, TN)
                dk_prev.append(dk[:QB, :])
                dk_cur.append(dk[QB:, :])
                dv_prev.append(dv[:QB, :])
                dv_cur.append(dv[QB:, :])
            dsink_ref[...] += dsink_acc
            dkp = rope_back(jnp.concatenate(dk_prev, axis=1), tp)
            dkc = rope_back(jnp.concatenate(dk_cur, axis=1), tc)
            dkv_ref[:, 0:256] = (carry_k[...] + dkp).astype(bf16)
            dkv_ref[:, 256:512] = (carry_v[...] + jnp.concatenate(dv_prev, axis=1)).astype(bf16)
            carry_k[...] = dkc
            carry_v[...] = jnp.concatenate(dv_cur, axis=1)

        @pl.when(j == nq)
        def _():
            dkv_ref[:, 0:256] = carry_k[...].astype(bf16)
            dkv_ref[:, 256:512] = carry_v[...].astype(bf16)

    lag = lambda b, j: (b * nq + jnp.maximum(j - 1, 0), 0)
    args = [proj] * 7 + [sinks, dy_attn] + list(tabs) + list(tabs) + list(chip_sums)
    res = _pcall(
        body, name="attn_backward", grid=(nb, nq + 1), in_specs=specs + [ANY] * nex,
        out_specs=(pl.BlockSpec((QB, D), lambda b, j: (cur(b, j), 0)), pl.BlockSpec((QB, 512), lag),
                   pl.BlockSpec((QB, D), lambda b, j: (cur(b, j), 0)), pl.BlockSpec((8, 128), lambda b, j: (0, 0)))
        + tuple([ANY] * nex),
        out_shape=(_sds((T, D), bf16), _sds((T, 512), bf16), _sds((T, D), bf16), _sds((8, 128), f32))
        + tuple(_sds(s.shape, s.dtype) for s in chip_sums),
        scratch_shapes=[pltpu.VMEM((QB, 256), f32), pltpu.VMEM((QB, 256), f32)] + _exchange_scratch(nex, 3),
        compiler_params=_params(("arbitrary", "arbitrary")),
    )(*args)
    return res[:4], res[4:]


def _lru_backward(proj, h_all, dy_rnn, cw_full, conv_b, w_a, b_a, w_x, b_x, lam, S):
    T = proj.shape[0]
    nb = T // S
    col, vec, wblk, cwblk = _lru_specs(S, nb)
    tokblk = pl.BlockSpec((S, RB), lambda n, b: (b, n))

    def body(x0_ref, g_ref, h_ref, dy_ref, cw_ref, cb_ref, wa_ref, ba_ref, wx_ref, bx_ref, lam_ref,
             du0_ref, dg_ref, gwa_ref, gwx_ref, vec_ref, gcw_ref, a_s, b_s, dh_s, edge_s):
        @pl.when(pl.program_id(1) == 0)
        def _():
            gwa_ref[...] = jnp.zeros_like(gwa_ref)
            gwx_ref[...] = jnp.zeros_like(gwx_ref)
            vec_ref[...] = jnp.zeros_like(vec_ref)
            gcw_ref[...] = jnp.zeros_like(gcw_ref)

        x0 = x0_ref[...].astype(f32)
        cw = cw_ref[...]
        lam_v = lam_ref[...]
        u, ub, r, i, sp, a, mult, inv_mult, taps = _lru_gates(x0, cw, cb_ref[...], wa_ref[...], ba_ref[...],
                                                              wx_ref[...], bx_ref[...], lam_v)
        h = h_ref[...]
        g = g_ref[...].astype(f32)
        dy = dy_ref[...].astype(f32)
        sg = _sigmoid(g)
        dg_ref[...] = (dy * h * (sg * (1.0 + g * (1.0 - sg)))).astype(bf16)
        _linear_scan(_shift_up(a, 1), dy * (g * sg), a_s, b_s, edge_s, dh_s, reverse=True)
        dh_total = dh_s[...]
        da = dh_total * _shift_down(h, 1)
        dmult = dh_total * (i * u)
        db = dh_total * mult
        di = db * u
        du = db * i
        dlog_a_c = ((-LRU_C) * a) * (da - dmult * (a * inv_mult))
        dr = dlog_a_c * sp
        dsp = jnp.sum(dlog_a_c * r, axis=0, keepdims=True)
        dpre_r = dr * r * (1.0 - r)
        dpre_i = di * i * (1.0 - i)
        dpre_rb = dpre_r.astype(bf16)
        dpre_ib = dpre_i.astype(bf16)
        du = du + _dot(dpre_rb, wa_ref[...].astype(bf16), NT) + _dot(dpre_ib, wx_ref[...].astype(bf16), NT)
        gwa_ref[...] += _dot(ub, dpre_rb, TN)
        gwx_ref[...] += _dot(ub, dpre_ib, TN)
        vec_ref[0:1, :] += jnp.sum(du, axis=0, keepdims=True)
        vec_ref[1:2, :] += jnp.sum(dpre_r, axis=0, keepdims=True)
        vec_ref[2:3, :] += jnp.sum(dpre_i, axis=0, keepdims=True)
        vec_ref[3:4, :] += dsp * (-_sigmoid(-lam_v))
        dx0 = cw[3:4, :] * du
        for k in range(3):
            dx0 = dx0 + cw[k:k + 1, :] * _shift_up(du, 3 - k)
        for k in range(4):
            gcw_ref[k:k + 1, :] += jnp.sum(du * taps[k], axis=0, keepdims=True)
        du0_ref[...] = dx0.astype(bf16)

    wacc = pl.BlockSpec((RB, RB), lambda n, b: (0, n))
    vacc = pl.BlockSpec((8, RB), lambda n, b: (0, n))
    cacc = pl.BlockSpec((8, RB), lambda n, b: (n, 0))
    return _pcall(
        body, name="lru_backward", grid=(RNN_BLOCKS, nb),
        in_specs=[col(0), col(8), tokblk, tokblk, cwblk, vec, wblk, vec, wblk, vec, vec],
        out_specs=(tokblk, tokblk, wacc, wacc, vacc, cacc),
        out_shape=(_sds((T, D), bf16), _sds((T, D), bf16), _sds((RB, D), f32), _sds((RB, D), f32),
                   _sds((8, D), f32), _sds((8 * RNN_BLOCKS, RB), f32)),
        scratch_shapes=[pltpu.VMEM((S, RB), f32)] * 3 + [pltpu.VMEM((S // 8, RB), f32)],
        compiler_params=_params(("arbitrary", "arbitrary")),
    )(proj, proj, h_all, dy_rnn, cw_full, conv_b, w_a, b_a, w_x, b_x, lam)


def _section_of_chunk(s):
    out = []
    for start, n in zip(SEC_START, SEC_CHUNKS):
        inside = (s >= start) & (s < start + n)
        out.append((inside, jnp.clip(s - start, 0, n - 1)))
    return out


EFFECT = pltpu.SideEffectType.DATAFLOW_SIDE_EFFECTING
HBM_SPEC = pl.BlockSpec(memory_space=pltpu.HBM)
SEM_SPEC = pl.BlockSpec(memory_space=pltpu.SEMAPHORE)


def _split_exchange_copies(src_ref, land_ref, send_sems, recv_sems):
    x, y, c = _my_place()
    copies = []
    for k in (3, 1, 2):
        px, py = (x + (k >> 1)) % 2, (y + (k & 1)) % 2
        copies.append(pltpu.make_async_remote_copy(
            src_ref=src_ref.at[2 * px + py], dst_ref=land_ref.at[k - 1], send_sem=send_sems[k - 1],
            recv_sem=recv_sems[k - 1], device_id=(px, py, c), device_id_type=MESH))
    return copies


def _exchange_start(chip_sum):
    _, r, cols = chip_sum.shape

    def body(src_ref, land_ref, s0, s1, s2, r0, r1, r2, src_thru, land_thru, token):
        for cp in _split_exchange_copies(src_ref, land_ref, (s0, s1, s2), (r0, r1, r2)):
            cp.start()
        token[...] = jnp.zeros_like(token)

    land = pltpu.with_memory_space_constraint(lax.empty((3, r, cols), chip_sum.dtype), pltpu.HBM)
    res = _pcall(
        body, name="exchange_start",
        out_shape=tuple([pltpu.SemaphoreType.DMA(())] * 6) + (
            pltpu.HBM(chip_sum.shape, chip_sum.dtype), pltpu.HBM((3, r, cols), chip_sum.dtype), _sds((8, 128), f32)),
        in_specs=(HBM_SPEC, HBM_SPEC), out_specs=tuple([SEM_SPEC] * 6) + (HBM_SPEC, HBM_SPEC, VMEM_SPEC),
        input_output_aliases={0: 6, 1: 7},
        compiler_params=pltpu.CompilerParams(has_side_effects=EFFECT),
    )(pltpu.with_memory_space_constraint(chip_sum, pltpu.HBM), land)
    return res[:6], res[6], res[7], res[8]


def _exchange_wait(sems, src_thru, land_thru, after):
    def body(src_ref, land_ref, s0, s1, s2, r0, r1, r2, after_ref, src_dead, got_ref):
        for cp in _split_exchange_copies(src_ref, land_ref, (s0, s1, s2), (r0, r1, r2)):
            cp.wait_send()
            cp.wait_recv()

    return _pcall(
        body, name="exchange_wait",
        out_shape=(pltpu.HBM(src_thru.shape, src_thru.dtype), pltpu.HBM(land_thru.shape, land_thru.dtype)),
        in_specs=(HBM_SPEC, HBM_SPEC) + tuple([SEM_SPEC] * 6) + (ANY,), out_specs=(HBM_SPEC, HBM_SPEC),
        input_output_aliases={0: 0, 1: 1},
        compiler_params=pltpu.CompilerParams(has_side_effects=EFFECT),
    )(src_thru, land_thru, *sems, after)[1]


def _split_gather_copies(src_ref, land_ref, send_sems, recv_sems):
    x, y, c = _my_place()
    mine = land_ref.at[:, 4 * x + 2 * y + c]
    return [pltpu.make_async_remote_copy(src_ref=src_ref, dst_ref=mine, send_sem=send_sems[k - 1],
                                         recv_sem=recv_sems[k - 1], device_id=_peer(k), device_id_type=MESH)
            for k in range(1, NDEV)]


def _gather_start(block, after):
    n = NDEV - 1

    def body(src_ref, land_ref, after_ref, *rest):
        for cp in _split_gather_copies(src_ref, land_ref, rest[:n], rest[n:2 * n]):
            cp.start()
        rest[2 * n + 2][...] = jnp.zeros_like(rest[2 * n + 2])

    land_shape = (block.shape[0], NDEV) + block.shape[1:]
    land = pltpu.with_memory_space_constraint(lax.empty(land_shape, block.dtype), pltpu.HBM)
    res = _pcall(
        body, name="gather_start",
        out_shape=tuple([pltpu.SemaphoreType.DMA(())] * (2 * n)) + (
            pltpu.HBM(block.shape, block.dtype), pltpu.HBM(land_shape, block.dtype), _sds((8, 128), f32)),
        in_specs=(HBM_SPEC, HBM_SPEC, ANY), out_specs=tuple([SEM_SPEC] * (2 * n)) + (HBM_SPEC, HBM_SPEC, VMEM_SPEC),
        input_output_aliases={0: 2 * n, 1: 2 * n + 1},
        compiler_params=pltpu.CompilerParams(has_side_effects=EFFECT),
    )(pltpu.with_memory_space_constraint(block, pltpu.HBM), land, after)
    return res[:2 * n], res[2 * n], res[2 * n + 1], res[2 * n + 2]


def _gather_wait(sems, src_thru, land_thru, after):
    n = NDEV - 1

    def body(src_ref, land_ref, *rest):
        for cp in _split_gather_copies(src_ref, land_ref, rest[:n], rest[n:2 * n]):
            cp.wait_send()
            cp.wait_recv()

    return _pcall(
        body, name="gather_wait",
        out_shape=(pltpu.HBM(src_thru.shape, src_thru.dtype), pltpu.HBM(land_thru.shape, land_thru.dtype)),
        in_specs=(HBM_SPEC, HBM_SPEC) + tuple([SEM_SPEC] * (2 * n)) + (ANY,), out_specs=(HBM_SPEC, HBM_SPEC),
        input_output_aliases={0: 0, 1: 1},
        compiler_params=pltpu.CompilerParams(has_side_effects=EFFECT),
    )(src_thru, land_thru, *sems, after)[1]


def _input_grad(dsecs, wt_full, x2d, dx2, norm_g):
    T = x2d.shape[0]
    tb = min(T, 512)
    nsec = len(dsecs)
    ntok = T // tb

    def body(*refs):
        secs = refs[:nsec]
        wt_ref, x_ref, dx2_ref, g_ref, dx_ref, gnorm_ref = refs[nsec:]
        i = pl.program_id(0)

        @pl.when(i == 0)
        def _():
            gnorm_ref[...] = jnp.zeros_like(gnorm_ref)

        dh = None
        for a, (start, n) in enumerate(zip(SEC_START, SEC_CHUNKS)):
            part = _dot(secs[a][...], wt_ref[CH * start:CH * (start + n), :], NN)
            dh = part if dh is None else dh + part
        xv = x_ref[...]
        rstd = lax.rsqrt(jnp.mean(xv * xv, axis=-1, keepdims=True) + EPS)
        xh = xv * rstd
        gnorm_ref[0:1, :] += jnp.sum(dh * xh, axis=0, keepdims=True)
        dxn = dh * g_ref[...]
        dx_ref[...] = dx2_ref[...] + rstd * (dxn - xh * jnp.mean(dxn * xh, axis=-1, keepdims=True))

    tok = pl.BlockSpec((tb, D), lambda i: (i, 0))
    return _pcall(
        body, name="input_grad", grid=(ntok,),
        in_specs=[pl.BlockSpec((tb, sec.shape[1]), lambda i: (i, 0)) for sec in dsecs]
        + [pl.BlockSpec((D_IN, D), lambda i: (0, 0), pipeline_mode=pl.Buffered(1)), tok, tok,
           pl.BlockSpec((1, D), lambda i: (0, 0))],
        out_specs=(tok, pl.BlockSpec((8, D), lambda i: (0, 0))),
        out_shape=(_sds((T, D), f32), _sds((8, D), f32)),
        compiler_params=_params(("arbitrary",)),
    )(*dsecs, wt_full, x2d, dx2, norm_g)


def _w_in_grad(dsecs, h_bf):
    T = h_bf.shape[0]
    tk = min(T, 2048)
    nchunks = D_IN // CH
    nsec = len(dsecs)
    nt = T // tk

    def body(*refs):
        secs = refs[:nsec]
        h_ref, out_ref, acc = refs[nsec:]
        s, t = pl.program_id(0), pl.program_id(1)

        @pl.when(t == 0)
        def _():
            acc[...] = jnp.zeros_like(acc)

        h_rows = h_ref[pl.ds(pl.multiple_of(t * tk, tk), tk), :]
        for a, (start, n) in enumerate(zip(SEC_START, SEC_CHUNKS)):
            @pl.when((s >= start) & (s < start + n))
            def _(a=a):
                acc[...] += _dot(secs[a][...], h_rows, TN)

        @pl.when(t == nt - 1)
        def _():
            out_ref[...] = acc[...].astype(bf16)

    def sec_spec(a):
        def index(s, t, a=a):
            inside, local = _section_of_chunk(s)[a]
            return (jnp.where(inside, t, 0), local)
        return pl.BlockSpec((tk, CH), index)

    return _pcall(
        body, name="w_in_grad", grid=(nchunks, T // tk),
        in_specs=[sec_spec(a) for a in range(nsec)]
        + [pl.BlockSpec((T, D), lambda s, t: (0, 0), pipeline_mode=pl.Buffered(1))],
        out_specs=pl.BlockSpec((CH, D), lambda s, t: (s, 0)), out_shape=_sds((D_IN, D), bf16),
        scratch_shapes=[pltpu.VMEM((CH, D), f32)],
        compiler_params=_params(("arbitrary", "arbitrary")),
    )(*dsecs, h_bf)


SMALL_NAMES = ("lru_w_a", "lru_w_x", "conv_b", "lru_b_a", "lru_b_x", "lru_lambda", "norm_g", "final_norm_g",
               "attn_sinks", "conv_w")
MISC_ROW = {"conv_b": 0, "lru_b_a": 1, "lru_b_x": 2, "lru_lambda": 3, "norm_g": 8, "final_norm_g": 16,
            "attn_sinks": 24, "loss": 32}


def _small_step(gwa, gwx, gvec, gnorm_blk, gfin_blk, dsink_blk, loss_blk, gcw, params):
    srcs_rows = (RB // NDEV, RB // NDEV, 8, 8)
    flat = [t for n in SMALL_NAMES for t in params[n]]
    nin = 8 + len(flat)
    nout = 4 * len(SMALL_NAMES) + 1

    def body(*refs):
        gwa_ref, gwx_ref, gvec_ref, gnorm_ref, gfin_ref, dsink_ref, loss_ref, gcw_ref = refs[:8]
        prm = {n: refs[8 + 3 * k:11 + 3 * k] for k, n in enumerate(SMALL_NAMES)}
        outs = {n: refs[nin + 4 * k:nin + 4 * k + 4] for k, n in enumerate(SMALL_NAMES)}
        loss_out = refs[nin + nout - 1]
        (misc, got_a, got_x, got_m, got_c, red_a, red_x, red_m, all_a, all_x, all_m,
         sa, ra, sb, rb) = refs[nin + nout:]
        x, y, c = _my_place()
        me = 4 * x + 2 * y + c

        misc[...] = jnp.zeros_like(misc)
        misc[0:8, :] = gvec_ref[...]
        misc[8:16, :] = gnorm_ref[...]
        misc[16:24, :] = gfin_ref[...]
        misc[24:32, 0:128] = dsink_ref[...]
        misc[32:40, :] = loss_ref[...]

        srcs = (gwa_ref, gwx_ref, misc, gcw_ref)
        gots = (got_a, got_x, got_m, got_c)

        def shard(ref, rows, dev):
            return ref.at[pl.ds(pl.multiple_of(dev * rows, 8), rows), :]

        scatter = []
        for k in range(1, NDEV):
            px, py, pc = _peer(k)
            for a in range(4):
                scatter.append(pltpu.make_async_remote_copy(
                    src_ref=shard(srcs[a], srcs_rows[a], 4 * px + 2 * py + pc), dst_ref=gots[a].at[k - 1],
                    send_sem=sa.at[4 * (k - 1) + a], recv_sem=ra.at[4 * (k - 1) + a],
                    device_id=(px, py, pc), device_id_type=MESH))
        for cp in scatter:
            cp.start()
        for cp in scatter:
            cp.wait()

        def reduced(a):
            rows = srcs_rows[a]
            total = srcs[a][pl.ds(pl.multiple_of(me * rows, 8), rows), :]
            for k in range(NDEV - 1):
                total = total + gots[a][k]
            return total

        reds = (red_a, red_x, red_m)
        alls = (all_a, all_x, all_m)
        for a in range(3):
            val = reduced(a)
            reds[a][...] = val
            alls[a][pl.ds(pl.multiple_of(me * srcs_rows[a], 8), srcs_rows[a]), :] = val
        gather = []
        for k in range(1, NDEV):
            peer = _peer(k)
            for a in range(3):
                gather.append(pltpu.make_async_remote_copy(
                    src_ref=reds[a], dst_ref=shard(alls[a], srcs_rows[a], me),
                    send_sem=sb.at[3 * (k - 1) + a], recv_sem=rb.at[3 * (k - 1) + a],
                    device_id=peer, device_id_type=MESH))
        for cp in gather:
            cp.start()
        g_conv = reduced(3)[0:4, :]
        for cp in gather:
            cp.wait()

        def update(name, g, pick=lambda r: r[...]):
            w_ref, m_ref, v_ref = prm[name]
            delta, m_new, v_new = _adam_math(g, pick(w_ref), pick(m_ref), pick(v_ref))
            return g, delta, m_new, v_new

        for n in range(RNN_BLOCKS):
            lanes = slice(RB * n, RB * (n + 1))
            for name, full in (("lru_w_a", all_a), ("lru_w_x", all_x)):
                for out, val in zip(outs[name], update(name, full[:, lanes], pick=lambda r, n=n: r[n])):
                    out[n] = val
        for name in ("conv_b", "lru_b_a", "lru_b_x", "lru_lambda", "norm_g", "final_norm_g"):
            row = MISC_ROW[name]
            for out, val in zip(outs[name], update(name, all_m[row:row + 1, :])):
                out[...] = val
        row = MISC_ROW["attn_sinks"]
        for out, val in zip(outs["attn_sinks"], update("attn_sinks", all_m[row:row + 1, 0:16])):
            out[...] = val
        for out, val in zip(outs["conv_w"], update("conv_w", g_conv)):
            out[...] = val
        row = MISC_ROW["loss"]
        loss_out[...] = all_m[row:row + 8, 0:128] * (0.5 / D)

    out_shape = tuple(_sds(params[n][0].shape, f32) for n in SMALL_NAMES for _ in range(4)) + (_sds((8, 128), f32),)
    scratch = [pltpu.VMEM((64, D), f32),
               pltpu.VMEM((NDEV - 1, RB // NDEV, D), f32), pltpu.VMEM((NDEV - 1, RB // NDEV, D), f32),
               pltpu.VMEM((NDEV - 1, 8, D), f32), pltpu.VMEM((NDEV - 1, 8, RB), f32),
               pltpu.VMEM((RB // NDEV, D), f32), pltpu.VMEM((RB // NDEV, D), f32), pltpu.VMEM((8, D), f32),
               pltpu.VMEM((RB, D), f32), pltpu.VMEM((RB, D), f32), pltpu.VMEM((64, D), f32),
               pltpu.SemaphoreType.DMA((4 * (NDEV - 1),)), pltpu.SemaphoreType.DMA((4 * (NDEV - 1),)),
               pltpu.SemaphoreType.DMA((3 * (NDEV - 1),)), pltpu.SemaphoreType.DMA((3 * (NDEV - 1),))]
    res = _pcall(
        body, name="small_step", out_shape=out_shape,
        in_specs=[VMEM_SPEC] * nin, out_specs=tuple([VMEM_SPEC] * nout),
        scratch_shapes=scratch, compiler_params=_params(),
    )(gwa, gwx, gvec, gnorm_blk, gfin_blk, dsink_blk, loss_blk, gcw, *flat)
    return {n: res[4 * k:4 * k + 4] for k, n in enumerate(SMALL_NAMES)}, res[-1]


def _pad_rows(v, rows=8):
    return jnp.concatenate([v, jnp.zeros((rows - v.shape[0], v.shape[1]), v.dtype)], axis=0)


def kernel(x, norm_g, w_in, conv_w, conv_b, lru_w_a, lru_b_a, lru_w_x, lru_b_x, lru_lambda, attn_sinks, w_rnn_out, w_attn_out, w_o, final_norm_g, loss_target, m_norm_g, m_w_in, m_conv_w, m_conv_b, m_lru_w_a, m_lru_b_a, m_lru_w_x, m_lru_b_x, m_lru_lambda, m_attn_sinks, m_w_rnn_out, m_w_attn_out, m_w_o, m_final_norm_g, v_norm_g, v_w_in, v_conv_w, v_conv_b, v_lru_w_a, v_lru_b_a, v_lru_w_x, v_lru_b_x, v_lru_lambda, v_attn_sinks, v_w_rnn_out, v_w_attn_out, v_w_o, v_final_norm_g):
    nb, S, _ = x.shape
    T = nb * S
    x2d = x.reshape(T, D)
    tgt = loss_target.reshape(T, D)
    fin_g = final_norm_g.reshape(1, D)
    w_a3, w_x3 = lru_w_a[0], lru_w_x[0]

    my_core = lax.axis_index("c").astype(jnp.int32).reshape(1)
    cx, cy = lax.axis_index("x"), lax.axis_index("y")
    chip_order = jnp.stack([2 * cx + cy, 2 * (1 - cx) + cy, 2 * cx + (1 - cy),
                            2 * (1 - cx) + (1 - cy)]).astype(jnp.int32)

    tabs = _rope_tables(S)
    h_bf, proj, wt_full, cw_full, _ = _in_proj_gather(
        x2d, norm_g, w_in[0].T.astype(bf16), _pad_rows(conv_w[0]), tabs, S, (), chip_order)
    w_out3 = jnp.stack([w_rnn_out[0], w_attn_out[0], w_o[0]]).astype(bf16)
    g_sems, w_out3, w_land, g_token = _gather_start(w_out3, cw_full)
    y_rnn, h_all = _lru_forward(proj, cw_full, conv_b + g_token[0, 0], w_a3, lru_b_a, w_x3, lru_b_x, lru_lambda, S)
    y_attn = _attn_forward(proj, attn_sinks, S)
    w_land = _gather_wait(g_sems, w_out3, w_land, y_attn)
    w_land = lax.dynamic_update_slice(w_land, w_out3[:, None], (0, 4 * cx + 2 * cy + lax.axis_index("c"), 0, 0))
    w_land = pltpu.with_memory_space_constraint(w_land.reshape(3, D, D), pltpu.HBM)

    (dx2, dy_rnn, dy_attn, dmr, dma, loss_blk, gfin_blk, g_wr, g_wa, g_wo) = _merge_and_head(
        x2d, tgt, proj, y_rnn, y_attn, w_land, fin_g)
    sums_out = _pair_sums([g_wr, g_wa, g_wo], bf16, my_core, "out")

    (dq, dkv, dga, dsink_blk), (p_wr, p_wa, p_wo) = _attn_backward(proj, dy_attn, tabs, attn_sinks, S, sums_out)
    du0, dgr, gwa, gwx, gvec, gcw = _lru_backward(proj, h_all, dy_rnn, cw_full, conv_b, w_a3, lru_b_a, w_x3,
                                                  lru_b_x, lru_lambda, S)
    dsecs = (du0, dgr, dq, dkv, dga, dmr, dma)

    g_wt = _w_in_grad(dsecs, h_bf)
    (sum_in,) = _pair_sums([g_wt], bf16, my_core, "in")
    ex_sems, sum_in, landing, token = _exchange_start(sum_in)
    grad_x2d, gnorm_blk = _input_grad(dsecs, wt_full, x2d, dx2, norm_g + token[0, 0])
    p_wt = _exchange_wait(ex_sems, sum_in, landing, gnorm_blk)
    p_wt_own = lax.dynamic_index_in_dim(sum_in, 2 * cx + cy, axis=0, keepdims=False)

    small, loss_out = _small_step(gwa, gwx, gvec, gnorm_blk, gfin_blk, dsink_blk, loss_blk, gcw, {
        "lru_w_a": (w_a3, m_lru_w_a[0], v_lru_w_a[0]), "lru_w_x": (w_x3, m_lru_w_x[0], v_lru_w_x[0]),
        "conv_b": (conv_b, m_conv_b, v_conv_b), "lru_b_a": (lru_b_a, m_lru_b_a, v_lru_b_a),
        "lru_b_x": (lru_b_x, m_lru_b_x, v_lru_b_x), "lru_lambda": (lru_lambda, m_lru_lambda, v_lru_lambda),
        "norm_g": (norm_g, m_norm_g, v_norm_g),
        "final_norm_g": (fin_g, m_final_norm_g.reshape(1, D), v_final_norm_g.reshape(1, D)),
        "attn_sinks": (attn_sinks, m_attn_sinks, v_attn_sinks),
        "conv_w": (conv_w[0], m_conv_w[0], v_conv_w[0])})

    o_wt = _adamw_transposed(p_wt_own, p_wt, w_in[0], m_w_in[0], v_w_in[0], "adamw_w_in")
    o_wr, o_wa, o_wo = _adamw_group(
        (p_wr, p_wa, p_wo), (w_rnn_out[0], w_attn_out[0], w_o[0]),
        (m_w_rnn_out[0], m_w_attn_out[0], m_w_o[0]), (v_w_rnn_out[0], v_w_attn_out[0], v_w_o[0]), "adamw_w_out")

    def result(kind):
        d = {n: small[n][kind] for n in ("conv_b", "lru_b_a", "lru_b_x", "lru_lambda", "norm_g", "attn_sinks")}
        d.update({n: small[n][kind][None] for n in ("lru_w_a", "lru_w_x", "conv_w")})
        d["final_norm_g"] = small["final_norm_g"][kind].reshape(D)
        d.update({"w_in": o_wt[kind][None], "w_rnn_out": o_wr[kind][None], "w_attn_out": o_wa[kind][None],
                  "w_o": o_wo[kind][None]})
        return d

    order = ("norm_g", "w_in", "conv_w", "conv_b", "lru_w_a", "lru_b_a", "lru_w_x", "lru_b_x", "lru_lambda",
             "attn_sinks", "w_rnn_out", "w_attn_out", "w_o", "final_norm_g")
    outs = [loss_out[0, 0], grad_x2d.reshape(nb, S, D)]
    for kind in range(4):
        d = result(kind)
        outs += [d[n] for n in order]
    return tuple(outs)
```

```python
import functools
import math

import jax
import jax.numpy as jnp
from jax import lax
from jax.experimental import pallas as pl
from jax.experimental.pallas import tpu as pltpu

f32 = jnp.float32
bf16 = jnp.bfloat16

D = 1024
D_IN = 6656
NDEV = 8
RNN_BLOCKS = 8
RB = 128
HEAD = 64
KV_HEADS = 4
GROUP = 4
QB = 128
LRU_C = 8.0
EPS = 1e-6
ROPE_DIM = 16
ROPE_THETA = 500000.0
CH = 512
SEC_START = (0, 2, 4, 6, 7, 9, 11)
SEC_CHUNKS = (2, 2, 2, 1, 2, 2, 2)
VMEM_LIMIT = 62 * 1024 * 1024

ADAM_LR, ADAM_B1, ADAM_B2, ADAM_EPS, ADAM_WD, ADAM_STEP = 0.001, 0.9, 0.999, 1e-08, 0.01, 10

MESH = pl.DeviceIdType.MESH
ANY = pl.BlockSpec(memory_space=pl.ANY)
VMEM_SPEC = pl.BlockSpec(memory_space=pltpu.VMEM)
SMEM_SPEC = pl.BlockSpec(memory_space=pltpu.SMEM)


def _pcall(body, **kw):
    return pl.pallas_call(body, **kw)


def _params(sem=None, **kw):
    if sem is not None:
        kw["dimension_semantics"] = sem
    return pltpu.CompilerParams(vmem_limit_bytes=VMEM_LIMIT, **kw)


def _sds(shape, dtype):
    return jax.ShapeDtypeStruct(shape, dtype)


def _dot(a, b, dims):
    return lax.dot_general(a, b, (dims, ((), ())), preferred_element_type=f32)


NN = ((1,), (0,))
NT = ((1,), (1,))
TN = ((0,), (0,))


def _sigmoid(v):
    return 0.5 * jnp.tanh(0.5 * v) + 0.5


def _sigmoid_positive(v):
    return 1.0 / (1.0 + jnp.exp(-v))


def _my_place():
    return lax.axis_index("x"), lax.axis_index("y"), lax.axis_index("c")


def _peer(k):
    x, y, c = _my_place()
    return (x + ((k >> 2) & 1)) % 2, (y + ((k >> 1) & 1)) % 2, (c + (k & 1)) % 2


def _direct_gather_copies(srcs, outs, send_sems, recv_sems, local_sems):
    x, y, c = _my_place()
    me = 4 * x + 2 * y + c
    local, remote = [], []
    for a, (src, out) in enumerate(zip(srcs, outs)):
        r = src.shape[0]
        mine = out.at[pl.ds(pl.multiple_of(me * r, 8), r), :]
        local.append(pltpu.make_async_copy(src, mine, local_sems.at[a]))
        for k in range(1, NDEV):
            remote.append(pltpu.make_async_remote_copy(
                src_ref=src, dst_ref=mine, send_sem=send_sems.at[7 * a + k - 1], recv_sem=recv_sems.at[7 * a + k - 1],
                device_id=_peer(k), device_id_type=MESH))
    return local, remote


def _chip_exchange_copies(src, dst, send_sems, recv_sems, local_sems):
    x, y, c = _my_place()
    local, remote = [], []
    for a in range(len(src)):
        local.append(pltpu.make_async_copy(src[a].at[2 * x + y], dst[a].at[0], local_sems.at[a]))
    for k in (3, 1, 2):
        px, py = (x + (k >> 1)) % 2, (y + (k & 1)) % 2
        for a in range(len(src)):
            remote.append(pltpu.make_async_remote_copy(
                src_ref=src[a].at[2 * px + py], dst_ref=dst[a].at[k],
                send_sem=send_sems.at[3 * a + k - 1], recv_sem=recv_sems.at[3 * a + k - 1],
                device_id=(px, py, c), device_id_type=MESH))
    return local, remote


def _exchange_scratch(narr, per_array):
    return [pltpu.SemaphoreType.DMA((per_array * narr,)), pltpu.SemaphoreType.DMA((per_array * narr,)),
            pltpu.SemaphoreType.DMA((narr,))]


def _start_all(copies):
    local, remote = copies
    for cp in local + remote:
        cp.start()


def _wait_all(copies):
    local, remote = copies
    for cp in remote + local:
        cp.wait()


def _pair_exchange(grads, name):
    narr = len(grads)
    nrows = tuple(g.shape[0] // NDEV for g in grads)
    views = [g.reshape(4, 2, r, g.shape[1]) for g, r in zip(grads, nrows)]

    def body(*refs):
        gin = refs[:narr]
        got = refs[narr:2 * narr]
        send_sems, recv_sems = refs[2 * narr:]
        x, y, c = _my_place()
        copies = [pltpu.make_async_remote_copy(
            src_ref=gin[a].at[:, pl.ds(1 - c, 1)], dst_ref=got[a],
            send_sem=send_sems.at[a], recv_sem=recv_sems.at[a],
            device_id=(x, y, 1 - c), device_id_type=MESH) for a in range(narr)]
        for cp in copies:
            cp.start()
        for cp in copies:
            cp.wait()

    out_shape = tuple(_sds((4, 1, r, g.shape[1]), g.dtype) for r, g in zip(nrows, grads))
    got = _pcall(
        body, name=name, out_shape=out_shape,
        in_specs=[ANY] * narr, out_specs=tuple([ANY] * narr),
        scratch_shapes=[pltpu.SemaphoreType.DMA((narr,)), pltpu.SemaphoreType.DMA((narr,))],
        compiler_params=_params(),
    )(*views)
    return views, [g.reshape(4, r, g.shape[3]) for g, r in zip(got, nrows)]


def _row_tile(rows, dtype):
    unit = 16 if dtype == bf16 else 8
    for cand in (256, 208, 128, 64, 40, 32, 16, 8):
        if rows % cand == 0 and cand % unit == 0:
            return cand
    return rows


def _chip_sum(views, gots, my_core, out_dtype, name):
    narr = len(views)
    _, _, r, cols = views[0].shape
    tr = _row_tile(r, out_dtype)

    def body(core_ref, *refs):
        for a in range(narr):
            mine_ref, got_ref, out_ref = refs[a], refs[narr + a], refs[2 * narr + a]
            out_ref[...] = (mine_ref[...].astype(f32) + got_ref[...].astype(f32)).astype(out_dtype)

    slab = pl.BlockSpec((None, tr, cols), lambda q, i, core: (q, i, 0))
    grid_spec = pltpu.PrefetchScalarGridSpec(
        num_scalar_prefetch=1, grid=(4, r // tr),
        in_specs=[pl.BlockSpec((None, None, tr, cols), lambda q, i, core: (q, core[0], i, 0))] * narr + [slab] * narr,
        out_specs=tuple([slab] * narr))
    return _pcall(body, name=name, grid_spec=grid_spec,
                  out_shape=tuple(_sds((4, r, cols), out_dtype) for _ in range(narr)),
                  compiler_params=_params(("arbitrary", "arbitrary")))(my_core, *views, *gots)


def _pair_sums(grads, wire_dtype, my_core, tag):
    views, got = _pair_exchange(grads, "pair_exchange_" + tag)
    return _chip_sum(views, got, my_core, wire_dtype, "chip_sum_" + tag)


def _adam_math(g, w, m, v):
    m_new = ADAM_B1 * m + (1.0 - ADAM_B1) * g
    v_new = ADAM_B2 * v + (1.0 - ADAM_B2) * (g * g)
    m_hat = m_new / (1.0 - ADAM_B1 ** ADAM_STEP)
    v_hat = v_new / (1.0 - ADAM_B2 ** ADAM_STEP)
    return -ADAM_LR * (m_hat / (jnp.sqrt(v_hat) + ADAM_EPS) + ADAM_WD * w), m_new, v_new


def _adamw(first, parts, w, m, v, name):
    n, rows, cols = parts.shape
    tr = _row_tile(rows, parts.dtype)

    def body(f_ref, p_ref, w_ref, m_ref, v_ref, g_out, d_out, m_out, v_out):
        g = f_ref[...].astype(f32)
        for s in range(n):
            g = g + p_ref[s].astype(f32)
        g_out[...] = g
        d_out[...], m_out[...], v_out[...] = _adam_math(g, w_ref[...], m_ref[...], v_ref[...])

    blk = pl.BlockSpec((tr, cols), lambda i: (i, 0))
    return _pcall(
        body, name=name, grid=(rows // tr,),
        in_specs=[blk, pl.BlockSpec((n, tr, cols), lambda i: (0, i, 0)), blk, blk, blk],
        out_specs=(blk, blk, blk, blk), out_shape=tuple(_sds((rows, cols), f32) for _ in range(4)),
        compiler_params=_params(("arbitrary",)),
    )(first, parts, w, m, v)


def _adamw_group(parts, ws, ms, vs, name):
    nw = len(ws)

    def body(*refs):
        p_refs, w_refs, m_refs, v_refs = (refs[k * nw:(k + 1) * nw] for k in range(4))
        outs = refs[4 * nw:]
        for k in range(nw):
            g = p_refs[k][0].astype(f32)
            for s in range(1, p_refs[k].shape[0]):
                g = g + p_refs[k][s].astype(f32)
            g_out, d_out, m_out, v_out = outs[4 * k:4 * k + 4]
            g_out[...] = g
            d_out[...], m_out[...], v_out[...] = _adam_math(g, w_refs[k][...], m_refs[k][...], v_refs[k][...])

    res = _pcall(
        body, name=name, out_shape=tuple(_sds(w.shape, f32) for w in ws for _ in range(4)),
        in_specs=[VMEM_SPEC] * (4 * nw), out_specs=tuple([VMEM_SPEC] * (4 * nw)), compiler_params=_params(),
    )(*parts, *ws, *ms, *vs)
    return [res[4 * k:4 * k + 4] for k in range(nw)]


def _rope(t, c, s1, s2):
    w = t.shape[1]
    return t * c + pltpu.roll(t, w - 8, 1) * s1 + pltpu.roll(t, 8, 1) * s2


def _rope_transposed(dt, c, s1, s2):
    w = dt.shape[1]
    return dt * c + pltpu.roll(dt * s1, 8, 1) + pltpu.roll(dt * s2, w - 8, 1)


PAIR_ROWS = D_IN // 4
SUB_COLS = ((0, 512), (512, 512), (1024, 512), (1536, 128))
Q_SLABS = range(3, 11)
K_SLABS = range(11, 13)


def _in_proj_gather(x2d, norm_g, wt_shard, cw_shard, tabs, S, out_shards, chip_order):
    T = x2d.shape[0]
    tb = min(S, 1024)
    ntok = T // tb
    nsb = S // tb
    q_scale = 1.0 / math.sqrt(HEAD)
    shard_rows = wt_shard.shape[0]
    small = (cw_shard,) + tuple(out_shards)
    nsm = len(small)

    def body(order_ref, x_ref, g_ref, c_ref, s1_ref, s2_ref, wt_hbm, *rest):
        small_in = rest[:nsm]
        h_ref, proj_ref, wt_out = rest[nsm:nsm + 3]
        small_out = rest[nsm + 3:2 * nsm + 3]
        wt_vm, h_vm = rest[2 * nsm + 3:2 * nsm + 5]
        stage = rest[2 * nsm + 5:3 * nsm + 4]
        wsend, wrecv, wlocal = rest[3 * nsm + 4:3 * nsm + 7]
        dsems = rest[3 * nsm + 7:]
        jj, i = pl.program_id(0), pl.program_id(1)
        x, y, c = _my_place()
        me, sibling = (x, y, c), (x, y, 1 - c)
        chips = [(1 - x, y), (x, 1 - y), (1 - x, 1 - y)]

        def rows(place):
            px, py, pc = place
            return wt_vm.at[pl.ds(pl.multiple_of((4 * px + 2 * py + pc) * shard_rows, 16), shard_rows), :]

        def copy(k, block, to, src=None):
            return pltpu.make_async_remote_copy(
                src_ref=rows(block) if src is None else src, dst_ref=rows(block),
                send_sem=wsend.at[k], recv_sem=wrecv.at[k], device_id=to, device_id_type=MESH)

        def small_copies():
            srcs = (small_in[0],) + tuple(stage)
            return _direct_gather_copies(srcs, small_out, *dsems)

        own = pltpu.make_async_copy(wt_hbm, rows(me), wlocal.at[0])
        keep = pltpu.make_async_copy(wt_vm, wt_out, wlocal.at[1])

        @pl.when((jj == 0) & (i == 0))
        def _():
            own.start()
            copy(0, me, sibling, src=wt_hbm).start()
            for j, chip in enumerate(chips):
                copy(1 + j, me, (*chip, c), src=wt_hbm).start()
            for a in range(nsm - 1):
                stage[a][...] = small_in[1 + a][...].astype(bf16)
            _start_all(small_copies())
            own.wait()
            copy(0, sibling, me).wait_recv()

        for j, chip in enumerate(chips):
            @pl.when((jj == 1 + j) & (i == 0))
            def _(j=j, chip=chip):
                copy(1 + j, (*chip, c), me).wait_recv()
                copy(4 + j, (*chip, c), sibling).start()
                copy(4 + j, (*chip, 1 - c), me).wait_recv()

        @pl.when((jj == 3) & (i == 0))
        def _():
            keep.start()

        @pl.when((jj == 3) & (i == ntok - 1))
        def _():
            copy(0, me, sibling, src=wt_hbm).wait_send()
            for j, chip in enumerate(chips):
                copy(1 + j, me, (*chip, c), src=wt_hbm).wait_send()
                copy(4 + j, (*chip, c), sibling).wait_send()
            _wait_all(small_copies())
            keep.wait()

        tok = pl.ds(pl.multiple_of(i * tb, tb), tb)

        @pl.when(jj == 0)
        def _():
            xv = x_ref[...]
            ms = jnp.mean(xv * xv, axis=-1, keepdims=True)
            hb = (xv * lax.rsqrt(ms + EPS) * g_ref[...]).astype(bf16)
            h_ref[...] = hb
            h_vm[tok, :] = hb

        block = order_ref[jj]
        hb = h_vm[tok, :]

        def piece(c0, w):
            w_rows = wt_vm[pl.ds(pl.multiple_of(block * PAIR_ROWS + c0, 128), w), :]
            return _dot(hb, w_rows, NT)

        @pl.when(block != 1)
        def _():
            for c0, w in SUB_COLS:
                proj_ref[:, c0:c0 + w] = piece(c0, w).astype(bf16)

        @pl.when(block == 1)
        def _():
            tab = (c_ref[...], s1_ref[...], s2_ref[...])
            for c0, w in SUB_COLS:
                acc = piece(c0, w)
                for l in range(w // 128):
                    slab = (c0 + 128 * l) // 128
                    part = acc[:, 128 * l:128 * (l + 1)]
                    if slab in Q_SLABS:
                        part = _rope(part, *tab) * q_scale
                    elif slab in K_SLABS:
                        part = _rope(part, *tab)
                    proj_ref[:, 128 * slab:128 * (slab + 1)] = part.astype(bf16)

    first_pass = lambda jj, i, order: (jnp.where(jj == 0, i, ntok - 1), 0)
    const = lambda jj, i, order: (0, 0)
    tab = pl.BlockSpec((tb, 128), lambda jj, i, order: (jnp.where(order[jj] == 1, i % nsb, 0), 0))
    grid_spec = pltpu.PrefetchScalarGridSpec(
        num_scalar_prefetch=1, grid=(4, ntok),
        in_specs=[pl.BlockSpec((tb, D), first_pass), pl.BlockSpec((1, D), const), tab, tab, tab, ANY]
        + [pl.BlockSpec(w.shape, const) for w in small],
        out_specs=(pl.BlockSpec((tb, D), first_pass),
                   pl.BlockSpec((tb, PAIR_ROWS), lambda jj, i, order: (i, order[jj])), ANY) + tuple([ANY] * nsm),
        scratch_shapes=[pltpu.VMEM((D_IN, D), bf16), pltpu.VMEM((T, D), bf16)]
        + [pltpu.VMEM(w.shape, bf16) for w in out_shards]
        + [pltpu.SemaphoreType.DMA((7,)), pltpu.SemaphoreType.DMA((7,)), pltpu.SemaphoreType.DMA((2,))]
        + _exchange_scratch(nsm, 7))
    res = _pcall(
        body, name="in_proj", grid_spec=grid_spec,
        out_shape=(_sds((T, D), bf16), _sds((T, D_IN), bf16), _sds((D_IN, D), bf16),
                   _sds((NDEV * cw_shard.shape[0], cw_shard.shape[1]), f32))
        + tuple(_sds((NDEV * w.shape[0], w.shape[1]), bf16) for w in out_shards),
        compiler_params=_params(("arbitrary", "arbitrary")),
    )(chip_order, x2d, norm_g, *tabs, wt_shard, *small)
    return res[0], res[1], res[2], res[3], res[4:]


def _rows_iota(shape):
    return lax.broadcasted_iota(jnp.int32, shape, 0)


def _shift_down(v, k):
    return jnp.where(_rows_iota(v.shape) >= k, pltpu.roll(v, k, 0), 0.0)


def _shift_up(v, k):
    n = v.shape[0]
    return jnp.where(_rows_iota(v.shape) < n - k, pltpu.roll(v, n - k, 0), 0.0)


def _linear_scan(a, b, a_s, b_s, edge_s, out_ref, reverse):
    n = a.shape[0]
    ng = n // 8
    a3, b3 = a.reshape(ng, 8, RB), b.reshape(ng, 8, RB)
    rid = lax.broadcasted_iota(jnp.int32, a3.shape, 1)
    for s in (1, 2, 4):
        keep, shift = (rid < 8 - s, 8 - s) if reverse else (rid >= s, s)
        b3 = jnp.where(keep, a3 * pltpu.roll(b3, shift, 1) + b3, b3)
        a3 = jnp.where(keep, a3 * pltpu.roll(a3, shift, 1), a3)
    a_s[...] = a3.reshape(n, RB)
    b_s[...] = b3.reshape(n, RB)
    edge = 0 if reverse else 7
    ea, eb = a_s[pl.ds(edge, ng, stride=8), :], b_s[pl.ds(edge, ng, stride=8), :]
    r = _rows_iota(ea.shape)
    s = 1
    while s < ng:
        keep, shift = (r < ng - s, ng - s) if reverse else (r >= s, s)
        eb = jnp.where(keep, ea * pltpu.roll(eb, shift, 0) + eb, eb)
        if 2 * s < ng:
            ea = jnp.where(keep, ea * pltpu.roll(ea, shift, 0), ea)
        s *= 2
    edge_s[...] = _shift_up(eb, 1) if reverse else _shift_down(eb, 1)

    def eight_groups(i, carry):
        for k in range(8):
            j = i * 8 + k
            rows = pl.ds(pl.multiple_of(j * 8, 8), 8)
            out_ref[rows, :] = b_s[rows, :] + a_s[rows, :] * edge_s[pl.ds(j, 1), :]
        return carry

    lax.fori_loop(0, ng // 8, eight_groups, 0)


def _neg_expm1(v):
    series = -v * (1.0 + v * (0.5 + v * (1.0 / 6.0)))
    return jnp.where(v > -0.015625, series, 1.0 - jnp.exp(v))


def _softplus_neg(lam):
    return jnp.maximum(-lam, 0.0) + jnp.log(1.0 + jnp.exp(-jnp.abs(lam)))


def _lru_gates(x0, cw, cb, wa, ba, wx, bx, lam):
    taps = [_shift_down(x0, 3 - k) for k in range(3)] + [x0]
    u = cb + cw[3:4, :] * x0
    for k in range(3):
        u = u + cw[k:k + 1, :] * taps[k]
    ub = u.astype(bf16)
    r = _sigmoid_positive(_dot(ub, wa.astype(bf16), NN) + ba)
    i = _sigmoid(_dot(ub, wx.astype(bf16), NN) + bx)
    sp = _softplus_neg(lam)
    log_a = (-LRU_C) * r * sp
    a = jnp.exp(log_a)
    w = _neg_expm1(2.0 * log_a)
    inv_mult = lax.rsqrt(w)
    return u, ub, r, i, sp, a, w * inv_mult, inv_mult, taps


def _lru_specs(S, nb):
    col = lambda off: pl.BlockSpec((S, RB), lambda n, b, off=off: (b, off + n))
    vec = pl.BlockSpec((1, RB), lambda n, b: (0, n))
    wblk = pl.BlockSpec((None, RB, RB), lambda n, b: (n, 0, 0))
    cwblk = pl.BlockSpec((8, RB), lambda n, b: (n, 0))
    return col, vec, wblk, cwblk


def _lru_forward(proj, cw_full, conv_b, w_a, b_a, w_x, b_x, lam, S):
    T = proj.shape[0]
    nb = T // S
    col, vec, wblk, cwblk = _lru_specs(S, nb)

    def body(x0_ref, g_ref, cw_ref, cb_ref, wa_ref, ba_ref, wx_ref, bx_ref, lam_ref, y_ref, h_ref, a_s, b_s, edge_s):
        x0 = x0_ref[...].astype(f32)
        u, ub, r, i, sp, a, mult, _, _ = _lru_gates(x0, cw_ref[...], cb_ref[...], wa_ref[...], ba_ref[...],
                                                    wx_ref[...], bx_ref[...], lam_ref[...])
        _linear_scan(a, mult * (i * u), a_s, b_s, edge_s, h_ref, reverse=False)
        g = g_ref[...].astype(f32)
        y_ref[...] = (h_ref[...] * (g * _sigmoid(g))).astype(bf16)

    out = pl.BlockSpec((S, RB), lambda n, b: (b, n))
    return _pcall(
        body, name="lru_forward", grid=(RNN_BLOCKS, nb),
        in_specs=[col(0), col(8), cwblk, vec, wblk, vec, wblk, vec, vec],
        out_specs=(out, out), out_shape=(_sds((T, D), bf16), _sds((T, D), f32)),
        scratch_shapes=[pltpu.VMEM((S, RB), f32), pltpu.VMEM((S, RB), f32), pltpu.VMEM((S // 8, RB), f32)],
        compiler_params=_params(("arbitrary", "arbitrary")),
    )(proj, proj, cw_full, conv_b, w_a, b_a, w_x, b_x, lam)


def _rope_tables(S):
    pos = jnp.arange(S, dtype=f32)
    inv_freq = ROPE_THETA ** (-jnp.arange(0, ROPE_DIM, 2, dtype=f32) / ROPE_DIM)
    ang = pos[:, None] * inv_freq[None, :]
    cos, sin = jnp.cos(ang), jnp.sin(ang)
    lane = jnp.arange(128) % HEAD
    cosl, sinl = cos[:, lane % 8], sin[:, lane % 8]
    c = jnp.where(lane[None, :] < ROPE_DIM, cosl, 1.0)
    s1 = jnp.where(lane[None, :] < 8, -sinl, 0.0)
    s2 = jnp.where((lane[None, :] >= 8) & (lane[None, :] < ROPE_DIM), sinl, 0.0)
    return c.astype(f32), s1.astype(f32), s2.astype(f32)


def _heads_to_rows(t):
    return jnp.concatenate([t[:, HEAD * h:HEAD * (h + 1)] for h in range(GROUP)], axis=0)


def _rows_to_heads(t):
    return jnp.concatenate([t[QB * h:QB * (h + 1), :] for h in range(GROUP)], axis=1)


def _window_bias(first_block):
    shape = (GROUP * QB, 2 * QB)
    qi = _rows_iota(shape) % QB
    cj = lax.broadcasted_iota(jnp.int32, shape, 1)
    valid = (cj > qi) & (cj <= qi + QB) & ((cj >= QB) | jnp.logical_not(first_block))
    return jnp.where(valid, 0.0, -jnp.inf)


def _attn_probs(q_rows, k_cat, sink_col, bias):
    s = _dot(q_rows, k_cat, NT) + bias
    m = jnp.maximum(jnp.max(s, axis=1, keepdims=True), sink_col)
    p = jnp.exp(s - m)
    e_sink = jnp.exp(sink_col - m)
    inv = 1.0 / (jnp.sum(p, axis=1, keepdims=True) + e_sink)
    return p * inv, e_sink * inv


def _sink_column(sink_ref, kv):
    rid = _rows_iota((GROUP * QB, 1))
    col = jnp.zeros((GROUP * QB, 1), f32)
    for h in range(GROUP):
        col = jnp.where(rid // QB == h, sink_ref[0, GROUP * kv + h], col)
    return col


def _attn_in_specs(S):
    nq = S // QB
    last = nq - 1
    cur = lambda b, j: b * nq + jnp.minimum(j, last)
    prev = lambda b, j: b * nq + jnp.maximum(jnp.minimum(j, last) - 1, 0)
    specs = [
        pl.BlockSpec((QB, D), lambda b, j: (cur(b, j), 2)),
        pl.BlockSpec((QB, 256), lambda b, j: (cur(b, j), 12)),
        pl.BlockSpec((QB, 256), lambda b, j: (prev(b, j), 12)),
        pl.BlockSpec((QB, 256), lambda b, j: (cur(b, j), 13)),
        pl.BlockSpec((QB, 256), lambda b, j: (prev(b, j), 13)),
        pl.BlockSpec((QB, 512), lambda b, j: (cur(b, j), 7)),
        pl.BlockSpec((QB, 512), lambda b, j: (cur(b, j), 8)),
        SMEM_SPEC,
    ]
    return specs, cur, prev


def _attn_forward(proj, sinks, S):
    T = proj.shape[0]
    nb, nq = T // S, S // QB
    specs, cur, _ = _attn_in_specs(S)

    def body(q_ref, kc_ref, kp_ref, vc_ref, vp_ref, gl_ref, gh_ref, sink_ref, y_ref):
        bias = _window_bias(pl.program_id(1) == 0)
        kc, kp, vc, vp = kc_ref[...], kp_ref[...], vc_ref[...], vp_ref[...]
        for kv in range(KV_HEADS):
            lanes = slice(256 * kv, 256 * (kv + 1))
            hl = slice(HEAD * kv, HEAD * (kv + 1))
            q_rows = _heads_to_rows(q_ref[:, lanes])
            k_cat = jnp.concatenate([kp[:, hl], kc[:, hl]], axis=0)
            v_cat = jnp.concatenate([vp[:, hl], vc[:, hl]], axis=0)
            probs, _ = _attn_probs(q_rows, k_cat, _sink_column(sink_ref, kv), bias)
            o = _rows_to_heads(_dot(probs.astype(bf16), v_cat, NN))
            g_src = gl_ref if kv < 2 else gh_ref
            g = g_src[:, 256 * (kv % 2):256 * (kv % 2 + 1)].astype(f32)
            y_ref[:, lanes] = (o * (g * _sigmoid(g))).astype(bf16)

    args = [proj] * 7 + [sinks]
    return _pcall(
        body, name="attn_forward", grid=(nb, nq), in_specs=specs,
        out_specs=pl.BlockSpec((QB, D), lambda b, j: (cur(b, j), 0)), out_shape=_sds((T, D), bf16),
        compiler_params=_params(("arbitrary", "arbitrary")),
    )(*args)


def _merge_and_head(x2d, tgt, proj, y_rnn, y_attn, w_land, gfin):
    T = x2d.shape[0]
    tb = min(T, 512)
    nsteps = T // tb

    def body(x_ref, t_ref, mr0, mr1, ma0, ma1, yr_ref, ya_ref, wr_ref, wa_ref, wo_ref, gf_ref,
             dx2_ref, dyr_ref, dya_ref, dmr_ref, dma_ref, loss_ref, gfin_ref, gwr_out, gwa_out, gwo_out,
             gwr_acc, gwa_acc, gwo_acc, out_sems):
        step = pl.program_id(0)

        @pl.when(step == 0)
        def _():
            loss_ref[...] = jnp.zeros_like(loss_ref)
            gfin_ref[...] = jnp.zeros_like(gfin_ref)
            gwr_acc[...] = jnp.zeros_like(gwr_acc)
            gwa_acc[...] = jnp.zeros_like(gwa_acc)
            gwo_acc[...] = jnp.zeros_like(gwo_acc)

        sr = _sigmoid(jnp.concatenate([mr0[...], mr1[...]], axis=1).astype(f32))
        sa = _sigmoid(jnp.concatenate([ma0[...], ma1[...]], axis=1).astype(f32))
        p_r = _dot(yr_ref[...], wr_ref[...], NN)
        p_a = _dot(ya_ref[...], wa_ref[...], NN)
        merged = (sr * p_r + sa * p_a).astype(bf16)
        x2 = x_ref[...] + _dot(merged, wo_ref[...], NN)
        rstd = lax.rsqrt(jnp.mean(x2 * x2, axis=-1, keepdims=True) + EPS)
        xh = x2 * rstd
        gf = gf_ref[...]
        err = xh * gf - t_ref[...]
        loss_ref[...] += jnp.sum(err * err)
        dy = err * (1.0 / D)
        gfin_ref[0:1, :] += jnp.sum(dy * xh, axis=0, keepdims=True)
        dxn = dy * gf
        dx2 = rstd * (dxn - xh * jnp.mean(dxn * xh, axis=-1, keepdims=True))
        dx2_ref[...] = dx2
        dx2b = dx2.astype(bf16)
        dmerged = _dot(dx2b, wo_ref[...], NT)
        dmr_ref[...] = (dmerged * p_r * (sr * (1.0 - sr))).astype(bf16)
        dma_ref[...] = (dmerged * p_a * (sa * (1.0 - sa))).astype(bf16)
        dpr = (dmerged * sr).astype(bf16)
        dpa = (dmerged * sa).astype(bf16)
        dyr_ref[...] = _dot(dpr, wr_ref[...], NT).astype(bf16)
        dya_ref[...] = _dot(dpa, wa_ref[...], NT).astype(bf16)
        gwr_acc[...] += _dot(yr_ref[...], dpr, TN)
        gwa_acc[...] += _dot(ya_ref[...], dpa, TN)
        gwo_acc[...] += _dot(merged, dx2b, TN)

        @pl.when(step == nsteps - 1)
        def _():
            copies = [pltpu.make_async_copy(src, dst, out_sems.at[k]) for k, (src, dst) in enumerate(
                ((gwr_acc, gwr_out), (gwa_acc, gwa_out), (gwo_acc, gwo_out)))]
            for cp in copies:
                cp.start()
            for cp in copies:
                cp.wait()

    tok = pl.BlockSpec((tb, D), lambda i: (i, 0))
    half = lambda c: pl.BlockSpec((tb, CH), lambda i, c=c: (i, c))
    wspec = lambda a: pl.BlockSpec((None, D, D), lambda i, a=a: (a, 0, 0), pipeline_mode=pl.Buffered(1))
    acc = pl.BlockSpec((8, D), lambda i: (0, 0))
    return _pcall(
        body, name="merge_and_head", grid=(nsteps,),
        in_specs=[tok, tok, half(9), half(10), half(11), half(12), tok, tok, wspec(0), wspec(1), wspec(2),
                  pl.BlockSpec((1, D), lambda i: (0, 0))],
        out_specs=(tok, tok, tok, tok, tok, acc, acc, ANY, ANY, ANY),
        out_shape=(_sds((T, D), f32), _sds((T, D), bf16), _sds((T, D), bf16), _sds((T, D), bf16),
                   _sds((T, D), bf16), _sds((8, D), f32), _sds((8, D), f32),
                   _sds((D, D), f32), _sds((D, D), f32), _sds((D, D), f32)),
        scratch_shapes=[pltpu.VMEM((D, D), f32)] * 3 + [pltpu.SemaphoreType.DMA((3,))],
        compiler_params=_params(("arbitrary",)),
    )(x2d, tgt, proj, proj, proj, proj, y_rnn, y_attn, w_land, w_land, w_land, gfin)


def _attn_backward(proj, dy_attn, tabs, sinks, S, chip_sums):
    T = proj.shape[0]
    nb, nq = T // S, S // QB
    nex = len(chip_sums)
    specs, cur, prev = _attn_in_specs(S)
    last = nq - 1
    tab_cur = pl.BlockSpec((QB, 128), lambda b, j: (jnp.minimum(j, last), 0))
    tab_prev = pl.BlockSpec((QB, 128), lambda b, j: (jnp.maximum(jnp.minimum(j, last) - 1, 0), 0))
    specs = specs + [pl.BlockSpec((QB, D), lambda b, j: (cur(b, j), 0))] + [tab_cur] * 3 + [tab_prev] * 3
    q_scale = 1.0 / math.sqrt(HEAD)

    def rope_back(dt, tab):
        return jnp.concatenate([_rope_transposed(dt[:, 128 * l:128 * (l + 1)], *tab) for l in range(2)], axis=1)

    def body(q_ref, kc_ref, kp_ref, vc_ref, vp_ref, gl_ref, gh_ref, sink_ref, dy_ref, cc, s1c, s2c, cp, s1p, s2p,
             *rest):
        ex_src = rest[:nex]
        dq_ref, dkv_ref, dg_ref, dsink_ref = rest[nex:nex + 4]
        ex_dst = rest[nex + 4:2 * nex + 4]
        carry_k, carry_v = rest[2 * nex + 4:2 * nex + 6]
        sems = rest[2 * nex + 6:]
        b, j = pl.program_id(0), pl.program_id(1)

        @pl.when((b == 0) & (j == 0))
        def _():
            dsink_ref[...] = jnp.zeros_like(dsink_ref)
            _start_all(_chip_exchange_copies(ex_src, ex_dst, *sems))

        @pl.when((b == nb - 1) & (j == nq))
        def _():
            _wait_all(_chip_exchange_copies(ex_src, ex_dst, *sems))

        @pl.when(j == 0)
        def _():
            carry_k[...] = jnp.zeros_like(carry_k)
            carry_v[...] = jnp.zeros_like(carry_v)

        @pl.when(j < nq)
        def _():
            bias = _window_bias(j == 0)
            tc = (cc[...], s1c[...], s2c[...])
            tp = (cp[...], s1p[...], s2p[...])
            kc, kp, vc, vp = kc_ref[...], kp_ref[...], vc_ref[...], vp_ref[...]
            dk_prev, dk_cur, dv_prev, dv_cur = [], [], [], []
            dsink_acc = jnp.zeros((8, 128), f32)
            r8 = lax.broadcasted_iota(jnp.int32, (8, 128), 0)
            l8 = lax.broadcasted_iota(jnp.int32, (8, 128), 1)
            for kv in range(KV_HEADS):
                lanes = slice(256 * kv, 256 * (kv + 1))
                hl = slice(HEAD * kv, HEAD * (kv + 1))
                q_rows = _heads_to_rows(q_ref[:, lanes])
                k_cat = jnp.concatenate([kp[:, hl], kc[:, hl]], axis=0)
                v_cat = jnp.concatenate([vp[:, hl], vc[:, hl]], axis=0)
                probs, p_sink = _attn_probs(q_rows, k_cat, _sink_column(sink_ref, kv), bias)
                pb = probs.astype(bf16)
                o = _rows_to_heads(_dot(pb, v_cat, NN))
                g_src = gl_ref if kv < 2 else gh_ref
                g = g_src[:, 256 * (kv % 2):256 * (kv % 2 + 1)].astype(f32)
                sg = _sigmoid(g)
                dy = dy_ref[:, lanes].astype(f32)
                dg_ref[:, lanes] = (dy * o * (sg * (1.0 + g * (1.0 - sg)))).astype(bf16)
                do_rows = _heads_to_rows(dy * (g * sg)).astype(bf16)
                dv = _dot(pb, do_rows, TN)
                dp = _dot(do_rows, v_cat, NT)
                rowdot = jnp.sum(probs * dp, axis=1, keepdims=True)
                ds = (probs * (dp - rowdot)).astype(bf16)
                sink_rows = -(p_sink * rowdot)
                for h in range(GROUP):
                    val = jnp.sum(sink_rows[QB * h:QB * (h + 1), :])
                    dsink_acc = dsink_acc + jnp.where((r8 == 0) & (l8 == GROUP * kv + h), val, 0.0)
                dq = _rows_to_heads(_dot(ds, k_cat, NN)) * q_scale
                dq_ref[:, lanes] = rope_back(dq, tc).astype(bf16)
                dk = _dot(ds, q_rows, TN)
                dk_prev.append(dk[:QB, :])
                dk_cur.append(dk[QB:, :])
                dv_prev.append(dv[:QB, :])
                dv_cur.append(dv[QB:, :])
            dsink_ref[...] += dsink_acc
            dkp = rope_back(jnp.concatenate(dk_prev, axis=1), tp)
            dkc = rope_back(jnp.concatenate(dk_cur, axis=1), tc)
            dkv_ref[:, 0:256] = (carry_k[...] + dkp).astype(bf16)
            dkv_ref[:, 256:512] = (carry_v[...] + jnp.concatenate(dv_prev, axis=1)).astype(bf16)
            carry_k[...] = dkc
            carry_v[...] = jnp.concatenate(dv_cur, axis=1)

        @pl.when(j == nq)
        def _():
            dkv_ref[:, 0:256] = carry_k[...].astype(bf16)
            dkv_ref[:, 256:512] = carry_v[...].astype(bf16)

    lag = lambda b, j: (b * nq + jnp.maximum(j - 1, 0), 0)
    args = [proj] * 7 + [sinks, dy_attn] + list(tabs) + list(tabs) + list(chip_sums)
    res = _pcall(
        body, name="attn_backward", grid=(nb, nq + 1), in_specs=specs + [ANY] * nex,
        out_specs=(pl.BlockSpec((QB, D), lambda b, j: (cur(b, j), 0)), pl.BlockSpec((QB, 512), lag),
                   pl.BlockSpec((QB, D), lambda b, j: (cur(b, j), 0)), pl.BlockSpec((8, 128), lambda b, j: (0, 0)))
        + tuple([ANY] * nex),
        out_shape=(_sds((T, D), bf16), _sds((T, 512), bf16), _sds((T, D), bf16), _sds((8, 128), f32))
        + tuple(_sds(s.shape, s.dtype) for s in chip_sums),
        scratch_shapes=[pltpu.VMEM((QB, 256), f32), pltpu.VMEM((QB, 256), f32)] + _exchange_scratch(nex, 3),
        compiler_params=_params(("arbitrary", "arbitrary")),
    )(*args)
    return res[:4], res[4:]


def _lru_backward(proj, h_all, dy_rnn, cw_full, conv_b, w_a, b_a, w_x, b_x, lam, S):
    T = proj.shape[0]
    nb = T // S
    col, vec, wblk, cwblk = _lru_specs(S, nb)
    tokblk = pl.BlockSpec((S, RB), lambda n, b: (b, n))

    def body(x0_ref, g_ref, h_ref, dy_ref, cw_ref, cb_ref, wa_ref, ba_ref, wx_ref, bx_ref, lam_ref,
             du0_ref, dg_ref, gwa_ref, gwx_ref, vec_ref, gcw_ref, a_s, b_s, dh_s, edge_s):
        @pl.when(pl.program_id(1) == 0)
        def _():
            gwa_ref[...] = jnp.zeros_like(gwa_ref)
            gwx_ref[...] = jnp.zeros_like(gwx_ref)
            vec_ref[...] = jnp.zeros_like(vec_ref)
            gcw_ref[...] = jnp.zeros_like(gcw_ref)

        x0 = x0_ref[...].astype(f32)
        cw = cw_ref[...]
        lam_v = lam_ref[...]
        u, ub, r, i, sp, a, mult, inv_mult, taps = _lru_gates(x0, cw, cb_ref[...], wa_ref[...], ba_ref[...],
                                                              wx_ref[...], bx_ref[...], lam_v)
        h = h_ref[...]
        g = g_ref[...].astype(f32)
        dy = dy_ref[...].astype(f32)
        sg = _sigmoid(g)
        dg_ref[...] = (dy * h * (sg * (1.0 + g * (1.0 - sg)))).astype(bf16)
        _linear_scan(_shift_up(a, 1), dy * (g * sg), a_s, b_s, edge_s, dh_s, reverse=True)
        dh_total = dh_s[...]
        da = dh_total * _shift_down(h, 1)
        dmult = dh_total * (i * u)
        db = dh_total * mult
        di = db * u
        du = db * i
        dlog_a_c = ((-LRU_C) * a) * (da - dmult * (a * inv_mult))
        dr = dlog_a_c * sp
        dsp = jnp.sum(dlog_a_c * r, axis=0, keepdims=True)
        dpre_r = dr * r * (1.0 - r)
        dpre_i = di * i * (1.0 - i)
        dpre_rb = dpre_r.astype(bf16)
        dpre_ib = dpre_i.astype(bf16)
        du = du + _dot(dpre_rb, wa_ref[...].astype(bf16), NT) + _dot(dpre_ib, wx_ref[...].astype(bf16), NT)
        gwa_ref[...] += _dot(ub, dpre_rb, TN)
        gwx_ref[...] += _dot(ub, dpre_ib, TN)
        vec_ref[0:1, :] += jnp.sum(du, axis=0, keepdims=True)
        vec_ref[1:2, :] += jnp.sum(dpre_r, axis=0, keepdims=True)
        vec_ref[2:3, :] += jnp.sum(dpre_i, axis=0, keepdims=True)
        vec_ref[3:4, :] += dsp * (-_sigmoid(-lam_v))
        dx0 = cw[3:4, :] * du
        for k in range(3):
            dx0 = dx0 + cw[k:k + 1, :] * _shift_up(du, 3 - k)
        for k in range(4):
            gcw_ref[k:k + 1, :] += jnp.sum(du * taps[k], axis=0, keepdims=True)
        du0_ref[...] = dx0.astype(bf16)

    wacc = pl.BlockSpec((RB, RB), lambda n, b: (0, n))
    vacc = pl.BlockSpec((8, RB), lambda n, b: (0, n))
    cacc = pl.BlockSpec((8, RB), lambda n, b: (n, 0))
    return _pcall(
        body, name="lru_backward", grid=(RNN_BLOCKS, nb),
        in_specs=[col(0), col(8), tokblk, tokblk, cwblk, vec, wblk, vec, wblk, vec, vec],
        out_specs=(tokblk, tokblk, wacc, wacc, vacc, cacc),
        out_shape=(_sds((T, D), bf16), _sds((T, D), bf16), _sds((RB, D), f32), _sds((RB, D), f32),
                   _sds((8, D), f32), _sds((8 * RNN_BLOCKS, RB), f32)),
        scratch_shapes=[pltpu.VMEM((S, RB), f32)] * 3 + [pltpu.VMEM((S // 8, RB), f32)],
        compiler_params=_params(("arbitrary", "arbitrary")),
    )(proj, proj, h_all, dy_rnn, cw_full, conv_b, w_a, b_a, w_x, b_x, lam)


def _section_of_chunk(s):
    out = []
    for start, n in zip(SEC_START, SEC_CHUNKS):
        inside = (s >= start) & (s < start + n)
        out.append((inside, jnp.clip(s - start, 0, n - 1)))
    return out


EFFECT = pltpu.SideEffectType.DATAFLOW_SIDE_EFFECTING
HBM_SPEC = pl.BlockSpec(memory_space=pltpu.HBM)
SEM_SPEC = pl.BlockSpec(memory_space=pltpu.SEMAPHORE)


def _split_exchange_copies(src_ref, land_ref, send_sems, recv_sems):
    x, y, c = _my_place()
    copies = []
    for k in (3, 1, 2):
        px, py = (x + (k >> 1)) % 2, (y + (k & 1)) % 2
        copies.append(pltpu.make_async_remote_copy(
            src_ref=src_ref.at[2 * px + py], dst_ref=land_ref.at[k - 1], send_sem=send_sems[k - 1],
            recv_sem=recv_sems[k - 1], device_id=(px, py, c), device_id_type=MESH))
    return copies


def _exchange_start(chip_sum):
    _, r, cols = chip_sum.shape

    def body(src_ref, land_ref, s0, s1, s2, r0, r1, r2, src_thru, land_thru, token):
        for cp in _split_exchange_copies(src_ref, land_ref, (s0, s1, s2), (r0, r1, r2)):
            cp.start()
        token[...] = jnp.zeros_like(token)

    land = pltpu.with_memory_space_constraint(lax.empty((3, r, cols), chip_sum.dtype), pltpu.HBM)
    res = _pcall(
        body, name="exchange_start",
        out_shape=tuple([pltpu.SemaphoreType.DMA(())] * 6) + (
            pltpu.HBM(chip_sum.shape, chip_sum.dtype), pltpu.HBM((3, r, cols), chip_sum.dtype), _sds((8, 128), f32)),
        in_specs=(HBM_SPEC, HBM_SPEC), out_specs=tuple([SEM_SPEC] * 6) + (HBM_SPEC, HBM_SPEC, VMEM_SPEC),
        input_output_aliases={0: 6, 1: 7},
        compiler_params=pltpu.CompilerParams(has_side_effects=EFFECT),
    )(pltpu.with_memory_space_constraint(chip_sum, pltpu.HBM), land)
    return res[:6], res[6], res[7], res[8]


def _exchange_wait(sems, src_thru, land_thru, after):
    def body(src_ref, land_ref, s0, s1, s2, r0, r1, r2, after_ref, src_dead, got_ref):
        for cp in _split_exchange_copies(src_ref, land_ref, (s0, s1, s2), (r0, r1, r2)):
            cp.wait_send()
            cp.wait_recv()

    return _pcall(
        body, name="exchange_wait",
        out_shape=(pltpu.HBM(src_thru.shape, src_thru.dtype), pltpu.HBM(land_thru.shape, land_thru.dtype)),
        in_specs=(HBM_SPEC, HBM_SPEC) + tuple([SEM_SPEC] * 6) + (ANY,), out_specs=(HBM_SPEC, HBM_SPEC),
        input_output_aliases={0: 0, 1: 1},
        compiler_params=pltpu.CompilerParams(has_side_effects=EFFECT),
    )(src_thru, land_thru, *sems, after)[1]


def _split_gather_copies(src_ref, land_ref, send_sems, recv_sems):
    x, y, c = _my_place()
    mine = land_ref.at[:, 4 * x + 2 * y + c]
    return [pltpu.make_async_remote_copy(src_ref=src_ref, dst_ref=mine, send_sem=send_sems[k - 1],
                                         recv_sem=recv_sems[k - 1], device_id=_peer(k), device_id_type=MESH)
            for k in range(1, NDEV)]


def _gather_start(block, after):
    n = NDEV - 1

    def body(src_ref, land_ref, after_ref, *rest):
        for cp in _split_gather_copies(src_ref, land_ref, rest[:n], rest[n:2 * n]):
            cp.start()
        rest[2 * n + 2][...] = jnp.zeros_like(rest[2 * n + 2])

    land_shape = (block.shape[0], NDEV) + block.shape[1:]
    land = pltpu.with_memory_space_constraint(lax.empty(land_shape, block.dtype), pltpu.HBM)
    res = _pcall(
        body, name="gather_start",
        out_shape=tuple([pltpu.SemaphoreType.DMA(())] * (2 * n)) + (
            pltpu.HBM(block.shape, block.dtype), pltpu.HBM(land_shape, block.dtype), _sds((8, 128), f32)),
        in_specs=(HBM_SPEC, HBM_SPEC, ANY), out_specs=tuple([SEM_SPEC] * (2 * n)) + (HBM_SPEC, HBM_SPEC, VMEM_SPEC),
        input_output_aliases={0: 2 * n, 1: 2 * n + 1},
        compiler_params=pltpu.CompilerParams(has_side_effects=EFFECT),
    )(pltpu.with_memory_space_constraint(block, pltpu.HBM), land, after)
    return res[:2 * n], res[2 * n], res[2 * n + 1], res[2 * n + 2]


def _gather_wait(sems, src_thru, land_thru, after):
    n = NDEV - 1

    def body(src_ref, land_ref, *rest):
        for cp in _split_gather_copies(src_ref, land_ref, rest[:n], rest[n:2 * n]):
            cp.wait_send()
            cp.wait_recv()

    return _pcall(
        body, name="gather_wait",
        out_shape=(pltpu.HBM(src_thru.shape, src_thru.dtype), pltpu.HBM(land_thru.shape, land_thru.dtype)),
        in_specs=(HBM_SPEC, HBM_SPEC) + tuple([SEM_SPEC] * (2 * n)) + (ANY,), out_specs=(HBM_SPEC, HBM_SPEC),
        input_output_aliases={0: 0, 1: 1},
        compiler_params=pltpu.CompilerParams(has_side_effects=EFFECT),
    )(src_thru, land_thru, *sems, after)[1]


def _input_grad(dsecs, wt_full, x2d, dx2, norm_g):
    T = x2d.shape[0]
    tb = min(T, 512)
    nsec = len(dsecs)
    ntok = T // tb

    def body(*refs):
        secs = refs[:nsec]
        wt_ref, x_ref, dx2_ref, g_ref, dx_ref, gnorm_ref = refs[nsec:]
        i = pl.program_id(0)

        @pl.when(i == 0)
        def _():
            gnorm_ref[...] = jnp.zeros_like(gnorm_ref)

        dh = None
        for a, (start, n) in enumerate(zip(SEC_START, SEC_CHUNKS)):
            part = _dot(secs[a][...], wt_ref[CH * start:CH * (start + n), :], NN)
            dh = part if dh is None else dh + part
        xv = x_ref[...]
        rstd = lax.rsqrt(jnp.mean(xv * xv, axis=-1, keepdims=True) + EPS)
        xh = xv * rstd
        gnorm_ref[0:1, :] += jnp.sum(dh * xh, axis=0, keepdims=True)
        dxn = dh * g_ref[...]
        dx_ref[...] = dx2_ref[...] + rstd * (dxn - xh * jnp.mean(dxn * xh, axis=-1, keepdims=True))

    tok = pl.BlockSpec((tb, D), lambda i: (i, 0))
    return _pcall(
        body, name="input_grad", grid=(ntok,),
        in_specs=[pl.BlockSpec((tb, sec.shape[1]), lambda i: (i, 0)) for sec in dsecs]
        + [pl.BlockSpec((D_IN, D), lambda i: (0, 0), pipeline_mode=pl.Buffered(1)), tok, tok,
           pl.BlockSpec((1, D), lambda i: (0, 0))],
        out_specs=(tok, pl.BlockSpec((8, D), lambda i: (0, 0))),
        out_shape=(_sds((T, D), f32), _sds((8, D), f32)),
        compiler_params=_params(("arbitrary",)),
    )(*dsecs, wt_full, x2d, dx2, norm_g)


def _w_in_grad(dsecs, h_bf):
    T = h_bf.shape[0]
    tk = min(T, 2048)
    nchunks = D_IN // CH
    nsec = len(dsecs)
    nt = T // tk

    def body(*refs):
        secs = refs[:nsec]
        h_ref, out_ref, acc = refs[nsec:]
        s, t = pl.program_id(0), pl.program_id(1)

        @pl.when(t == 0)
        def _():
            acc[...] = jnp.zeros_like(acc)

        h_rows = h_ref[pl.ds(pl.multiple_of(t * tk, tk), tk), :]
        for a, (start, n) in enumerate(zip(SEC_START, SEC_CHUNKS)):
            @pl.when((s >= start) & (s < start + n))
            def _(a=a):
                acc[...] += _dot(secs[a][...], h_rows, TN)

        @pl.when(t == nt - 1)
        def _():
            out_ref[...] = acc[...].astype(bf16)

    def sec_spec(a):
        def index(s, t, a=a):
            inside, local = _section_of_chunk(s)[a]
            return (jnp.where(inside, t, 0), local)
        return pl.BlockSpec((tk, CH), index)

    return _pcall(
        body, name="w_in_grad", grid=(nchunks, T // tk),
        in_specs=[sec_spec(a) for a in range(nsec)]
        + [pl.BlockSpec((T, D), lambda s, t: (0, 0), pipeline_mode=pl.Buffered(1))],
        out_specs=pl.BlockSpec((CH, D), lambda s, t: (s, 0)), out_shape=_sds((D_IN, D), bf16),
        scratch_shapes=[pltpu.VMEM((CH, D), f32)],
        compiler_params=_params(("arbitrary", "arbitrary")),
    )(*dsecs, h_bf)


SMALL_NAMES = ("lru_w_a", "lru_w_x", "conv_b", "lru_b_a", "lru_b_x", "lru_lambda", "norm_g", "final_norm_g",
               "attn_sinks", "conv_w")
MISC_ROW = {"conv_b": 0, "lru_b_a": 1, "lru_b_x": 2, "lru_lambda": 3, "norm_g": 8, "final_norm_g": 16,
            "attn_sinks": 24, "loss": 32}


def _small_step(gwa, gwx, gvec, gnorm_blk, gfin_blk, dsink_blk, loss_blk, gcw, params):
    srcs_rows = (RB // NDEV, RB // NDEV, 8, 8)
    flat = [t for n in SMALL_NAMES for t in params[n]]
    nin = 8 + len(flat)
    nout = 4 * len(SMALL_NAMES) + 1

    def body(*refs):
        gwa_ref, gwx_ref, gvec_ref, gnorm_ref, gfin_ref, dsink_ref, loss_ref, gcw_ref = refs[:8]
        prm = {n: refs[8 + 3 * k:11 + 3 * k] for k, n in enumerate(SMALL_NAMES)}
        outs = {n: refs[nin + 4 * k:nin + 4 * k + 4] for k, n in enumerate(SMALL_NAMES)}
        loss_out = refs[nin + nout - 1]
        (misc, got_a, got_x, got_m, got_c, red_a, red_x, red_m, all_a, all_x, all_m,
         sa, ra, sb, rb) = refs[nin + nout:]
        x, y, c = _my_place()
        me = 4 * x + 2 * y + c

        misc[...] = jnp.zeros_like(misc)
        misc[0:8, :] = gvec_ref[...]
        misc[8:16, :] = gnorm_ref[...]
        misc[16:24, :] = gfin_ref[...]
        misc[24:32, 0:128] = dsink_ref[...]
        misc[32:40, :] = loss_ref[...]

        srcs = (gwa_ref, gwx_ref, misc, gcw_ref)
        gots = (got_a, got_x, got_m, got_c)

        def shard(ref, rows, dev):
            return ref.at[pl.ds(pl.multiple_of(dev * rows, 8), rows), :]

        scatter = []
        for k in range(1, NDEV):
            px, py, pc = _peer(k)
            for a in range(4):
                scatter.append(pltpu.make_async_remote_copy(
                    src_ref=shard(srcs[a], srcs_rows[a], 4 * px + 2 * py + pc), dst_ref=gots[a].at[k - 1],
                    send_sem=sa.at[4 * (k - 1) + a], recv_sem=ra.at[4 * (k - 1) + a],
                    device_id=(px, py, pc), device_id_type=MESH))
        for cp in scatter:
            cp.start()
        for cp in scatter:
            cp.wait()

        def reduced(a):
            rows = srcs_rows[a]
            total = srcs[a][pl.ds(pl.multiple_of(me * rows, 8), rows), :]
            for k in range(NDEV - 1):
                total = total + gots[a][k]
            return total

        reds = (red_a, red_x, red_m)
        alls = (all_a, all_x, all_m)
        for a in range(3):
            val = reduced(a)
            reds[a][...] = val
            alls[a][pl.ds(pl.multiple_of(me * srcs_rows[a], 8), srcs_rows[a]), :] = val
        gather = []
        for k in range(1, NDEV):
            peer = _peer(k)
            for a in range(3):
                gather.append(pltpu.make_async_remote_copy(
                    src_ref=reds[a], dst_ref=shard(alls[a], srcs_rows[a], me),
                    send_sem=sb.at[3 * (k - 1) + a], recv_sem=rb.at[3 * (k - 1) + a],
                    device_id=peer, device_id_type=MESH))
        for cp in gather:
            cp.start()
        g_conv = reduced(3)[0:4, :]
        for cp in gather:
            cp.wait()

        def update(name, g, pick=lambda r: r[...]):
            w_ref, m_ref, v_ref = prm[name]
            delta, m_new, v_new = _adam_math(g, pick(w_ref), pick(m_ref), pick(v_ref))
            return g, delta, m_new, v_new

        for n in range(RNN_BLOCKS):
            lanes = slice(RB * n, RB * (n + 1))
            for name, full in (("lru_w_a", all_a), ("lru_w_x", all_x)):
                for out, val in zip(outs[name], update(name, full[:, lanes], pick=lambda r, n=n: r[n])):
                    out[n] = val
        for name in ("conv_b", "lru_b_a", "lru_b_x", "lru_lambda", "norm_g", "final_norm_g"):
            row = MISC_ROW[name]
            for out, val in zip(outs[name], update(name, all_m[row:row + 1, :])):
                out[...] = val
        row = MISC_ROW["attn_sinks"]
        for out, val in zip(outs["attn_sinks"], update("attn_sinks", all_m[row:row + 1, 0:16])):
            out[...] = val
        for out, val in zip(outs["conv_w"], update("conv_w", g_conv)):
            out[...] = val
        row = MISC_ROW["loss"]
        loss_out[...] = all_m[row:row + 8, 0:128] * (0.5 / D)

    out_shape = tuple(_sds(params[n][0].shape, f32) for n in SMALL_NAMES for _ in range(4)) + (_sds((8, 128), f32),)
    scratch = [pltpu.VMEM((64, D), f32),
               pltpu.VMEM((NDEV - 1, RB // NDEV, D), f32), pltpu.VMEM((NDEV - 1, RB // NDEV, D), f32),
               pltpu.VMEM((NDEV - 1, 8, D), f32), pltpu.VMEM((NDEV - 1, 8, RB), f32),
               pltpu.VMEM((RB // NDEV, D), f32), pltpu.VMEM((RB // NDEV, D), f32), pltpu.VMEM((8, D), f32),
               pltpu.VMEM((RB, D), f32), pltpu.VMEM((RB, D), f32), pltpu.VMEM((64, D), f32),
               pltpu.SemaphoreType.DMA((4 * (NDEV - 1),)), pltpu.SemaphoreType.DMA((4 * (NDEV - 1),)),
               pltpu.SemaphoreType.DMA((3 * (NDEV - 1),)), pltpu.SemaphoreType.DMA((3 * (NDEV - 1),))]
    res = _pcall(
        body, name="small_step", out_shape=out_shape,
        in_specs=[VMEM_SPEC] * nin, out_specs=tuple([VMEM_SPEC] * nout),
        scratch_shapes=scratch, compiler_params=_params(),
    )(gwa, gwx, gvec, gnorm_blk, gfin_blk, dsink_blk, loss_blk, gcw, *flat)
    return {n: res[4 * k:4 * k + 4] for k, n in enumerate(SMALL_NAMES)}, res[-1]


def _pad_rows(v, rows=8):
    return jnp.concatenate([v, jnp.zeros((rows - v.shape[0], v.shape[1]), v.dtype)], axis=0)


def kernel(x, norm_g, w_in, conv_w, conv_b, lru_w_a, lru_b_a, lru_w_x, lru_b_x, lru_lambda, attn_sinks, w_rnn_out, w_attn_out, w_o, final_norm_g, loss_target, m_norm_g, m_w_in, m_conv_w, m_conv_b, m_lru_w_a, m_lru_b_a, m_lru_w_x, m_lru_b_x, m_lru_lambda, m_attn_sinks, m_w_rnn_out, m_w_attn_out, m_w_o, m_final_norm_g, v_norm_g, v_w_in, v_conv_w, v_conv_b, v_lru_w_a, v_lru_b_a, v_lru_w_x, v_lru_b_x, v_lru_lambda, v_attn_sinks, v_w_rnn_out, v_w_attn_out, v_w_o, v_final_norm_g):
    nb, S, _ = x.shape
    T = nb * S
    x2d = x.reshape(T, D)
    tgt = loss_target.reshape(T, D)
    fin_g = final_norm_g.reshape(1, D)
    w_a3, w_x3 = lru_w_a[0], lru_w_x[0]

    my_core = lax.axis_index("c").astype(jnp.int32).reshape(1)
    cx, cy = lax.axis_index("x"), lax.axis_index("y")
    chip_order = jnp.stack([2 * cx + cy, 2 * (1 - cx) + cy, 2 * cx + (1 - cy),
                            2 * (1 - cx) + (1 - cy)]).astype(jnp.int32)

    tabs = _rope_tables(S)
    h_bf, proj, wt_full, cw_full, _ = _in_proj_gather(
        x2d, norm_g, w_in[0].T.astype(bf16), _pad_rows(conv_w[0]), tabs, S, (), chip_order)
    w_out3 = jnp.stack([w_rnn_out[0], w_attn_out[0], w_o[0]]).astype(bf16)
    g_sems, w_out3, w_land, g_token = _gather_start(w_out3, cw_full)
    y_rnn, h_all = _lru_forward(proj, cw_full, conv_b + g_token[0, 0], w_a3, lru_b_a, w_x3, lru_b_x, lru_lambda, S)
    y_attn = _attn_forward(proj, attn_sinks, S)
    w_land = _gather_wait(g_sems, w_out3, w_land, y_attn)
    w_land = lax.dynamic_update_slice(w_land, w_out3[:, None], (0, 4 * cx + 2 * cy + lax.axis_index("c"), 0, 0))
    w_land = pltpu.with_memory_space_constraint(w_land.reshape(3, D, D), pltpu.HBM)

    (dx2, dy_rnn, dy_attn, dmr, dma, loss_blk, gfin_blk, g_wr, g_wa, g_wo) = _merge_and_head(
        x2d, tgt, proj, y_rnn, y_attn, w_land, fin_g)
    sums_out = _pair_sums([g_wr, g_wa, g_wo], bf16, my_core, "out")

    (dq, dkv, dga, dsink_blk), (p_wr, p_wa, p_wo) = _attn_backward(proj, dy_attn, tabs, attn_sinks, S, sums_out)
    du0, dgr, gwa, gwx, gvec, gcw = _lru_backward(proj, h_all, dy_rnn, cw_full, conv_b, w_a3, lru_b_a, w_x3,
                                                  lru_b_x, lru_lambda, S)
    dsecs = (du0, dgr, dq, dkv, dga, dmr, dma)

    g_wt = _w_in_grad(dsecs, h_bf)
    (sum_in,) = _pair_sums([g_wt], bf16, my_core, "in")
    ex_sems, sum_in, landing, token = _exchange_start(sum_in)
    grad_x2d, gnorm_blk = _input_grad(dsecs, wt_full, x2d, dx2, norm_g + token[0, 0])
    p_wt = _exchange_wait(ex_sems, sum_in, landing, gnorm_blk)
    p_wt_own = lax.dynamic_index_in_dim(sum_in, 2 * cx + cy, axis=0, keepdims=False)

    small, loss_out = _small_step(gwa, gwx, gvec, gnorm_blk, gfin_blk, dsink_blk, loss_blk, gcw, {
        "lru_w_a": (w_a3, m_lru_w_a[0], v_lru_w_a[0]), "lru_w_x": (w_x3, m_lru_w_x[0], v_lru_w_x[0]),
        "conv_b": (conv_b, m_conv_b, v_conv_b), "lru_b_a": (lru_b_a, m_lru_b_a, v_lru_b_a),
        "lru_b_x": (lru_b_x, m_lru_b_x, v_lru_b_x), "lru_lambda": (lru_lambda, m_lru_lambda, v_lru_lambda),
        "norm_g": (norm_g, m_norm_g, v_norm_g),
        "final_norm_g": (fin_g, m_final_norm_g.reshape(1, D), v_final_norm_g.reshape(1, D)),
        "attn_sinks": (attn_sinks, m_attn_sinks, v_attn_sinks),
        "conv_w": (conv_w[0], m_conv_w[0], v_conv_w[0])})

    o_wt = _adamw(p_wt_own, p_wt, w_in[0].T, m_w_in[0].T, v_w_in[0].T, "adamw_w_in")
    o_wr, o_wa, o_wo = _adamw_group(
        (p_wr, p_wa, p_wo), (w_rnn_out[0], w_attn_out[0], w_o[0]),
        (m_w_rnn_out[0], m_w_attn_out[0], m_w_o[0]), (v_w_rnn_out[0], v_w_attn_out[0], v_w_o[0]), "adamw_w_out")

    def result(kind):
        d = {n: small[n][kind] for n in ("conv_b", "lru_b_a", "lru_b_x", "lru_lambda", "norm_g", "attn_sinks")}
        d.update({n: small[n][kind][None] for n in ("lru_w_a", "lru_w_x", "conv_w")})
        d["final_norm_g"] = small["final_norm_g"][kind].reshape(D)
        d.update({"w_in": o_wt[kind].T[None], "w_rnn_out": o_wr[kind][None], "w_attn_out": o_wa[kind][None],
                  "w_o": o_wo[kind][None]})
        return d

    order = ("norm_g", "w_in", "conv_w", "conv_b", "lru_w_a", "lru_b_a", "lru_w_x", "lru_b_x", "lru_lambda",
             "attn_sinks", "w_rnn_out", "w_attn_out", "w_o", "final_norm_g")
    outs = [loss_out[0, 0], grad_x2d.reshape(nb, S, D)]
    for kind in range(4):
        d = result(kind)
        outs += [d[n] for n in order]
    return tuple(outs)
```

```python
import functools
import math

import jax
import jax.numpy as jnp
from jax import lax
from jax.experimental import pallas as pl
from jax.experimental.pallas import tpu as pltpu

f32 = jnp.float32
bf16 = jnp.bfloat16

D = 1024
D_IN = 6656
NDEV = 8
RNN_BLOCKS = 8
RB = 128
HEAD = 64
KV_HEADS = 4
GROUP = 4
QB = 128
LRU_C = 8.0
EPS = 1e-6
ROPE_DIM = 16
ROPE_THETA = 500000.0
CH = 512
SEC_START = (0, 2, 4, 6, 7, 9, 11)
SEC_CHUNKS = (2, 2, 2, 1, 2, 2, 2)
VMEM_LIMIT = 62 * 1024 * 1024

ADAM_LR, ADAM_B1, ADAM_B2, ADAM_EPS, ADAM_WD, ADAM_STEP = 0.001, 0.9, 0.999, 1e-08, 0.01, 10

MESH = pl.DeviceIdType.MESH
ANY = pl.BlockSpec(memory_space=pl.ANY)
VMEM_SPEC = pl.BlockSpec(memory_space=pltpu.VMEM)
SMEM_SPEC = pl.BlockSpec(memory_space=pltpu.SMEM)


def _pcall(body, **kw):
    return pl.pallas_call(body, **kw)


def _params(sem=None, **kw):
    if sem is not None:
        kw["dimension_semantics"] = sem
    return pltpu.CompilerParams(vmem_limit_bytes=VMEM_LIMIT, **kw)


def _sds(shape, dtype):
    return jax.ShapeDtypeStruct(shape, dtype)


def _dot(a, b, dims):
    return lax.dot_general(a, b, (dims, ((), ())), preferred_element_type=f32)


NN = ((1,), (0,))
NT = ((1,), (1,))
TN = ((0,), (0,))


def _sigmoid(v):
    return 0.5 * jnp.tanh(0.5 * v) + 0.5


def _sigmoid_positive(v):
    return 1.0 / (1.0 + jnp.exp(-v))


def _my_place():
    return lax.axis_index("x"), lax.axis_index("y"), lax.axis_index("c")


def _peer(k):
    x, y, c = _my_place()
    return (x + ((k >> 2) & 1)) % 2, (y + ((k >> 1) & 1)) % 2, (c + (k & 1)) % 2


def _direct_gather_copies(srcs, outs, send_sems, recv_sems, local_sems):
    x, y, c = _my_place()
    me = 4 * x + 2 * y + c
    local, remote = [], []
    for a, (src, out) in enumerate(zip(srcs, outs)):
        r = src.shape[0]
        mine = out.at[pl.ds(pl.multiple_of(me * r, 8), r), :]
        local.append(pltpu.make_async_copy(src, mine, local_sems.at[a]))
        for k in range(1, NDEV):
            remote.append(pltpu.make_async_remote_copy(
                src_ref=src, dst_ref=mine, send_sem=send_sems.at[7 * a + k - 1], recv_sem=recv_sems.at[7 * a + k - 1],
                device_id=_peer(k), device_id_type=MESH))
    return local, remote


def _chip_exchange_copies(src, dst, send_sems, recv_sems, local_sems):
    x, y, c = _my_place()
    local, remote = [], []
    for a in range(len(src)):
        local.append(pltpu.make_async_copy(src[a].at[2 * x + y], dst[a].at[0], local_sems.at[a]))
    for k in (3, 1, 2):
        px, py = (x + (k >> 1)) % 2, (y + (k & 1)) % 2
        for a in range(len(src)):
            remote.append(pltpu.make_async_remote_copy(
                src_ref=src[a].at[2 * px + py], dst_ref=dst[a].at[k],
                send_sem=send_sems.at[3 * a + k - 1], recv_sem=recv_sems.at[3 * a + k - 1],
                device_id=(px, py, c), device_id_type=MESH))
    return local, remote


def _exchange_scratch(narr, per_array):
    return [pltpu.SemaphoreType.DMA((per_array * narr,)), pltpu.SemaphoreType.DMA((per_array * narr,)),
            pltpu.SemaphoreType.DMA((narr,))]


def _start_all(copies):
    local, remote = copies
    for cp in local + remote:
        cp.start()


def _wait_all(copies):
    local, remote = copies
    for cp in remote + local:
        cp.wait()


def _pair_exchange(grads, name):
    narr = len(grads)
    nrows = tuple(g.shape[0] // NDEV for g in grads)
    views = [g.reshape(4, 2, r, g.shape[1]) for g, r in zip(grads, nrows)]

    def body(*refs):
        gin = refs[:narr]
        got = refs[narr:2 * narr]
        send_sems, recv_sems = refs[2 * narr:]
        x, y, c = _my_place()
        copies = [pltpu.make_async_remote_copy(
            src_ref=gin[a].at[:, pl.ds(1 - c, 1)], dst_ref=got[a],
            send_sem=send_sems.at[a], recv_sem=recv_sems.at[a],
            device_id=(x, y, 1 - c), device_id_type=MESH) for a in range(narr)]
        for cp in copies:
            cp.start()
        for cp in copies:
            cp.wait()

    out_shape = tuple(_sds((4, 1, r, g.shape[1]), g.dtype) for r, g in zip(nrows, grads))
    got = _pcall(
        body, name=name, out_shape=out_shape,
        in_specs=[ANY] * narr, out_specs=tuple([ANY] * narr),
        scratch_shapes=[pltpu.SemaphoreType.DMA((narr,)), pltpu.SemaphoreType.DMA((narr,))],
        compiler_params=_params(),
    )(*views)
    return views, [g.reshape(4, r, g.shape[3]) for g, r in zip(got, nrows)]


def _row_tile(rows, dtype):
    unit = 16 if dtype == bf16 else 8
    for cand in (256, 208, 128, 64, 40, 32, 16, 8):
        if rows % cand == 0 and cand % unit == 0:
            return cand
    return rows


def _chip_sum(views, gots, my_core, out_dtype, name):
    narr = len(views)
    _, _, r, cols = views[0].shape
    tr = _row_tile(r, out_dtype)

    def body(core_ref, *refs):
        for a in range(narr):
            mine_ref, got_ref, out_ref = refs[a], refs[narr + a], refs[2 * narr + a]
            out_ref[...] = (mine_ref[...].astype(f32) + got_ref[...].astype(f32)).astype(out_dtype)

    slab = pl.BlockSpec((None, tr, cols), lambda q, i, core: (q, i, 0))
    grid_spec = pltpu.PrefetchScalarGridSpec(
        num_scalar_prefetch=1, grid=(4, r // tr),
        in_specs=[pl.BlockSpec((None, None, tr, cols), lambda q, i, core: (q, core[0], i, 0))] * narr + [slab] * narr,
        out_specs=tuple([slab] * narr))
    return _pcall(body, name=name, grid_spec=grid_spec,
                  out_shape=tuple(_sds((4, r, cols), out_dtype) for _ in range(narr)),
                  compiler_params=_params(("arbitrary", "arbitrary")))(my_core, *views, *gots)


def _pair_sums(grads, wire_dtype, my_core, tag):
    views, got = _pair_exchange(grads, "pair_exchange_" + tag)
    return _chip_sum(views, got, my_core, wire_dtype, "chip_sum_" + tag)


def _adam_math(g, w, m, v):
    m_new = ADAM_B1 * m + (1.0 - ADAM_B1) * g
    v_new = ADAM_B2 * v + (1.0 - ADAM_B2) * (g * g)
    m_hat = m_new / (1.0 - ADAM_B1 ** ADAM_STEP)
    v_hat = v_new / (1.0 - ADAM_B2 ** ADAM_STEP)
    return -ADAM_LR * (m_hat / (jnp.sqrt(v_hat) + ADAM_EPS) + ADAM_WD * w), m_new, v_new


def _adamw(first, parts, w, m, v, name):
    n, rows, cols = parts.shape
    tr = _row_tile(rows, parts.dtype)

    def body(f_ref, p_ref, w_ref, m_ref, v_ref, g_out, d_out, m_out, v_out):
        g = f_ref[...].astype(f32)
        for s in range(n):
            g = g + p_ref[s].astype(f32)
        g_out[...] = g
        d_out[...], m_out[...], v_out[...] = _adam_math(g, w_ref[...], m_ref[...], v_ref[...])

    blk = pl.BlockSpec((tr, cols), lambda i: (i, 0))
    return _pcall(
        body, name=name, grid=(rows // tr,),
        in_specs=[blk, pl.BlockSpec((n, tr, cols), lambda i: (0, i, 0)), blk, blk, blk],
        out_specs=(blk, blk, blk, blk), out_shape=tuple(_sds((rows, cols), f32) for _ in range(4)),
        compiler_params=_params(("arbitrary",)),
    )(first, parts, w, m, v)


def _adamw_group(parts, ws, ms, vs, name):
    nw = len(ws)

    def body(*refs):
        p_refs, w_refs, m_refs, v_refs = (refs[k * nw:(k + 1) * nw] for k in range(4))
        outs = refs[4 * nw:]
        for k in range(nw):
            g = p_refs[k][0].astype(f32)
            for s in range(1, p_refs[k].shape[0]):
                g = g + p_refs[k][s].astype(f32)
            g_out, d_out, m_out, v_out = outs[4 * k:4 * k + 4]
            g_out[...] = g
            d_out[...], m_out[...], v_out[...] = _adam_math(g, w_refs[k][...], m_refs[k][...], v_refs[k][...])

    res = _pcall(
        body, name=name, out_shape=tuple(_sds(w.shape, f32) for w in ws for _ in range(4)),
        in_specs=[VMEM_SPEC] * (4 * nw), out_specs=tuple([VMEM_SPEC] * (4 * nw)), compiler_params=_params(),
    )(*parts, *ws, *ms, *vs)
    return [res[4 * k:4 * k + 4] for k in range(nw)]


def _rope(t, c, s1, s2):
    w = t.shape[1]
    return t * c + pltpu.roll(t, w - 8, 1) * s1 + pltpu.roll(t, 8, 1) * s2


def _rope_transposed(dt, c, s1, s2):
    w = dt.shape[1]
    return dt * c + pltpu.roll(dt * s1, 8, 1) + pltpu.roll(dt * s2, w - 8, 1)


PAIR_ROWS = D_IN // 4
SUB_COLS = ((0, 512), (512, 512), (1024, 512), (1536, 128))
Q_SLABS = range(3, 11)
K_SLABS = range(11, 13)


def _in_proj_gather(x2d, norm_g, wt_shard, cw_shard, tabs, S, out_shards, chip_order):
    T = x2d.shape[0]
    tb = min(S, 1024)
    ntok = T // tb
    nsb = S // tb
    q_scale = 1.0 / math.sqrt(HEAD)
    shard_rows = wt_shard.shape[0]
    small = (cw_shard,) + tuple(out_shards)
    nsm = len(small)

    def body(order_ref, x_ref, g_ref, c_ref, s1_ref, s2_ref, wt_hbm, *rest):
        small_in = rest[:nsm]
        h_ref, proj_ref, wt_out = rest[nsm:nsm + 3]
        small_out = rest[nsm + 3:2 * nsm + 3]
        wt_vm, h_vm = rest[2 * nsm + 3:2 * nsm + 5]
        stage = rest[2 * nsm + 5:3 * nsm + 4]
        wsend, wrecv, wlocal = rest[3 * nsm + 4:3 * nsm + 7]
        dsems = rest[3 * nsm + 7:]
        jj, i = pl.program_id(0), pl.program_id(1)
        x, y, c = _my_place()
        me, sibling = (x, y, c), (x, y, 1 - c)
        chips = [(1 - x, y), (x, 1 - y), (1 - x, 1 - y)]

        def rows(place):
            px, py, pc = place
            return wt_vm.at[pl.ds(pl.multiple_of((4 * px + 2 * py + pc) * shard_rows, 16), shard_rows), :]

        def copy(k, block, to, src=None):
            return pltpu.make_async_remote_copy(
                src_ref=rows(block) if src is None else src, dst_ref=rows(block),
                send_sem=wsend.at[k], recv_sem=wrecv.at[k], device_id=to, device_id_type=MESH)

        def small_copies():
            srcs = (small_in[0],) + tuple(stage)
            return _direct_gather_copies(srcs, small_out, *dsems)

        own = pltpu.make_async_copy(wt_hbm, rows(me), wlocal.at[0])
        keep = pltpu.make_async_copy(wt_vm, wt_out, wlocal.at[1])

        @pl.when((jj == 0) & (i == 0))
        def _():
            own.start()
            copy(0, me, sibling, src=wt_hbm).start()
            for j, chip in enumerate(chips):
                copy(1 + j, me, (*chip, c), src=wt_hbm).start()
            for a in range(nsm - 1):
                stage[a][...] = small_in[1 + a][...].astype(bf16)
            _start_all(small_copies())
            own.wait()
            copy(0, sibling, me).wait_recv()

        for j, chip in enumerate(chips):
            @pl.when((jj == 1 + j) & (i == 0))
            def _(j=j, chip=chip):
                copy(1 + j, (*chip, c), me).wait_recv()
                copy(4 + j, (*chip, c), sibling).start()
                copy(4 + j, (*chip, 1 - c), me).wait_recv()

        @pl.when((jj == 3) & (i == 0))
        def _():
            keep.start()

        @pl.when((jj == 3) & (i == ntok - 1))
        def _():
            copy(0, me, sibling, src=wt_hbm).wait_send()
            for j, chip in enumerate(chips):
                copy(1 + j, me, (*chip, c), src=wt_hbm).wait_send()
                copy(4 + j, (*chip, c), sibling).wait_send()
            _wait_all(small_copies())
            keep.wait()

        tok = pl.ds(pl.multiple_of(i * tb, tb), tb)

        @pl.when(jj == 0)
        def _():
            xv = x_ref[...]
            ms = jnp.mean(xv * xv, axis=-1, keepdims=True)
            hb = (xv * lax.rsqrt(ms + EPS) * g_ref[...]).astype(bf16)
            h_ref[...] = hb
            h_vm[tok, :] = hb

        block = order_ref[jj]
        hb = h_vm[tok, :]

        def piece(c0, w):
            w_rows = wt_vm[pl.ds(pl.multiple_of(block * PAIR_ROWS + c0, 128), w), :]
            return _dot(hb, w_rows, NT)

        @pl.when(block != 1)
        def _():
            for c0, w in SUB_COLS:
                proj_ref[:, c0:c0 + w] = piece(c0, w).astype(bf16)

        @pl.when(block == 1)
        def _():
            tab = (c_ref[...], s1_ref[...], s2_ref[...])
            for c0, w in SUB_COLS:
                acc = piece(c0, w)
                for l in range(w // 128):
                    slab = (c0 + 128 * l) // 128
                    part = acc[:, 128 * l:128 * (l + 1)]
                    if slab in Q_SLABS:
                        part = _rope(part, *tab) * q_scale
                    elif slab in K_SLABS:
                        part = _rope(part, *tab)
                    proj_ref[:, 128 * slab:128 * (slab + 1)] = part.astype(bf16)

    first_pass = lambda jj, i, order: (jnp.where(jj == 0, i, ntok - 1), 0)
    const = lambda jj, i, order: (0, 0)
    tab = pl.BlockSpec((tb, 128), lambda jj, i, order: (jnp.where(order[jj] == 1, i % nsb, 0), 0))
    grid_spec = pltpu.PrefetchScalarGridSpec(
        num_scalar_prefetch=1, grid=(4, ntok),
        in_specs=[pl.BlockSpec((tb, D), first_pass), pl.BlockSpec((1, D), const), tab, tab, tab, ANY]
        + [pl.BlockSpec(w.shape, const) for w in small],
        out_specs=(pl.BlockSpec((tb, D), first_pass),
                   pl.BlockSpec((tb, PAIR_ROWS), lambda jj, i, order: (i, order[jj])), ANY) + tuple([ANY] * nsm),
        scratch_shapes=[pltpu.VMEM((D_IN, D), bf16), pltpu.VMEM((T, D), bf16)]
        + [pltpu.VMEM(w.shape, bf16) for w in out_shards]
        + [pltpu.SemaphoreType.DMA((7,)), pltpu.SemaphoreType.DMA((7,)), pltpu.SemaphoreType.DMA((2,))]
        + _exchange_scratch(nsm, 7))
    res = _pcall(
        body, name="in_proj", grid_spec=grid_spec,
        out_shape=(_sds((T, D), bf16), _sds((T, D_IN), bf16), _sds((D_IN, D), bf16),
                   _sds((NDEV * cw_shard.shape[0], cw_shard.shape[1]), f32))
        + tuple(_sds((NDEV * w.shape[0], w.shape[1]), bf16) for w in out_shards),
        compiler_params=_params(("arbitrary", "arbitrary")),
    )(chip_order, x2d, norm_g, *tabs, wt_shard, *small)
    return res[0], res[1], res[2], res[3], res[4:]


def _rows_iota(shape):
    return lax.broadcasted_iota(jnp.int32, shape, 0)


def _shift_down(v, k):
    return jnp.where(_rows_iota(v.shape) >= k, pltpu.roll(v, k, 0), 0.0)


def _shift_up(v, k):
    n = v.shape[0]
    return jnp.where(_rows_iota(v.shape) < n - k, pltpu.roll(v, n - k, 0), 0.0)


def _linear_scan(a, b, a_s, b_s, edge_s, out_ref, reverse):
    n = a.shape[0]
    ng = n // 8
    a3, b3 = a.reshape(ng, 8, RB), b.reshape(ng, 8, RB)
    rid = lax.broadcasted_iota(jnp.int32, a3.shape, 1)
    for s in (1, 2, 4):
        keep, shift = (rid < 8 - s, 8 - s) if reverse else (rid >= s, s)
        b3 = jnp.where(keep, a3 * pltpu.roll(b3, shift, 1) + b3, b3)
        a3 = jnp.where(keep, a3 * pltpu.roll(a3, shift, 1), a3)
    a_s[...] = a3.reshape(n, RB)
    b_s[...] = b3.reshape(n, RB)
    edge = 0 if reverse else 7
    ea, eb = a_s[pl.ds(edge, ng, stride=8), :], b_s[pl.ds(edge, ng, stride=8), :]
    r = _rows_iota(ea.shape)
    s = 1
    while s < ng:
        keep, shift = (r < ng - s, ng - s) if reverse else (r >= s, s)
        eb = jnp.where(keep, ea * pltpu.roll(eb, shift, 0) + eb, eb)
        if 2 * s < ng:
            ea = jnp.where(keep, ea * pltpu.roll(ea, shift, 0), ea)
        s *= 2
    edge_s[...] = _shift_up(eb, 1) if reverse else _shift_down(eb, 1)

    def eight_groups(i, carry):
        for k in range(8):
            j = i * 8 + k
            rows = pl.ds(pl.multiple_of(j * 8, 8), 8)
            out_ref[rows, :] = b_s[rows, :] + a_s[rows, :] * edge_s[pl.ds(j, 1), :]
        return carry

    lax.fori_loop(0, ng // 8, eight_groups, 0)


def _neg_expm1(v):
    series = -v * (1.0 + v * (0.5 + v * (1.0 / 6.0)))
    return jnp.where(v > -0.015625, series, 1.0 - jnp.exp(v))


def _softplus_neg(lam):
    return jnp.maximum(-lam, 0.0) + jnp.log(1.0 + jnp.exp(-jnp.abs(lam)))


def _lru_gates(x0, cw, cb, wa, ba, wx, bx, lam):
    taps = [_shift_down(x0, 3 - k) for k in range(3)] + [x0]
    u = cb + cw[3:4, :] * x0
    for k in range(3):
        u = u + cw[k:k + 1, :] * taps[k]
    ub = u.astype(bf16)
    r = _sigmoid_positive(_dot(ub, wa.astype(bf16), NN) + ba)
    i = _sigmoid(_dot(ub, wx.astype(bf16), NN) + bx)
    sp = _softplus_neg(lam)
    log_a = (-LRU_C) * r * sp
    a = jnp.exp(log_a)
    w = _neg_expm1(2.0 * log_a)
    inv_mult = lax.rsqrt(w)
    return u, ub, r, i, sp, a, w * inv_mult, inv_mult, taps


def _lru_specs(S, nb):
    col = lambda off: pl.BlockSpec((S, RB), lambda n, b, off=off: (b, off + n))
    vec = pl.BlockSpec((1, RB), lambda n, b: (0, n))
    wblk = pl.BlockSpec((None, RB, RB), lambda n, b: (n, 0, 0))
    cwblk = pl.BlockSpec((8, RB), lambda n, b: (n, 0))
    return col, vec, wblk, cwblk


def _lru_forward(proj, cw_full, conv_b, w_a, b_a, w_x, b_x, lam, S):
    T = proj.shape[0]
    nb = T // S
    col, vec, wblk, cwblk = _lru_specs(S, nb)

    def body(x0_ref, g_ref, cw_ref, cb_ref, wa_ref, ba_ref, wx_ref, bx_ref, lam_ref, y_ref, h_ref, a_s, b_s, edge_s):
        x0 = x0_ref[...].astype(f32)
        u, ub, r, i, sp, a, mult, _, _ = _lru_gates(x0, cw_ref[...], cb_ref[...], wa_ref[...], ba_ref[...],
                                                    wx_ref[...], bx_ref[...], lam_ref[...])
        _linear_scan(a, mult * (i * u), a_s, b_s, edge_s, h_ref, reverse=False)
        g = g_ref[...].astype(f32)
        y_ref[...] = (h_ref[...] * (g * _sigmoid(g))).astype(bf16)

    out = pl.BlockSpec((S, RB), lambda n, b: (b, n))
    return _pcall(
        body, name="lru_forward", grid=(RNN_BLOCKS, nb),
        in_specs=[col(0), col(8), cwblk, vec, wblk, vec, wblk, vec, vec],
        out_specs=(out, out), out_shape=(_sds((T, D), bf16), _sds((T, D), f32)),
        scratch_shapes=[pltpu.VMEM((S, RB), f32), pltpu.VMEM((S, RB), f32), pltpu.VMEM((S // 8, RB), f32)],
        compiler_params=_params(("arbitrary", "arbitrary")),
    )(proj, proj, cw_full, conv_b, w_a, b_a, w_x, b_x, lam)


def _rope_tables(S):
    pos = jnp.arange(S, dtype=f32)
    inv_freq = ROPE_THETA ** (-jnp.arange(0, ROPE_DIM, 2, dtype=f32) / ROPE_DIM)
    ang = pos[:, None] * inv_freq[None, :]
    cos, sin = jnp.cos(ang), jnp.sin(ang)
    lane = jnp.arange(128) % HEAD
    cosl, sinl = cos[:, lane % 8], sin[:, lane % 8]
    c = jnp.where(lane[None, :] < ROPE_DIM, cosl, 1.0)
    s1 = jnp.where(lane[None, :] < 8, -sinl, 0.0)
    s2 = jnp.where((lane[None, :] >= 8) & (lane[None, :] < ROPE_DIM), sinl, 0.0)
    return c.astype(f32), s1.astype(f32), s2.astype(f32)


def _heads_to_rows(t):
    return jnp.concatenate([t[:, HEAD * h:HEAD * (h + 1)] for h in range(GROUP)], axis=0)


def _rows_to_heads(t):
    return jnp.concatenate([t[QB * h:QB * (h + 1), :] for h in range(GROUP)], axis=1)


def _window_bias(first_block):
    shape = (GROUP * QB, 2 * QB)
    qi = _rows_iota(shape) % QB
    cj = lax.broadcasted_iota(jnp.int32, shape, 1)
    valid = (cj > qi) & (cj <= qi + QB) & ((cj >= QB) | jnp.logical_not(first_block))
    return jnp.where(valid, 0.0, -jnp.inf)


def _attn_probs(q_rows, k_cat, sink_col, bias):
    s = _dot(q_rows, k_cat, NT) + bias
    m = jnp.maximum(jnp.max(s, axis=1, keepdims=True), sink_col)
    p = jnp.exp(s - m)
    e_sink = jnp.exp(sink_col - m)
    inv = 1.0 / (jnp.sum(p, axis=1, keepdims=True) + e_sink)
    return p * inv, e_sink * inv


def _sink_column(sink_ref, kv):
    rid = _rows_iota((GROUP * QB, 1))
    col = jnp.zeros((GROUP * QB, 1), f32)
    for h in range(GROUP):
        col = jnp.where(rid // QB == h, sink_ref[0, GROUP * kv + h], col)
    return col


def _attn_in_specs(S):
    nq = S // QB
    last = nq - 1
    cur = lambda b, j: b * nq + jnp.minimum(j, last)
    prev = lambda b, j: b * nq + jnp.maximum(jnp.minimum(j, last) - 1, 0)
    specs = [
        pl.BlockSpec((QB, D), lambda b, j: (cur(b, j), 2)),
        pl.BlockSpec((QB, 256), lambda b, j: (cur(b, j), 12)),
        pl.BlockSpec((QB, 256), lambda b, j: (prev(b, j), 12)),
        pl.BlockSpec((QB, 256), lambda b, j: (cur(b, j), 13)),
        pl.BlockSpec((QB, 256), lambda b, j: (prev(b, j), 13)),
        pl.BlockSpec((QB, 512), lambda b, j: (cur(b, j), 7)),
        pl.BlockSpec((QB, 512), lambda b, j: (cur(b, j), 8)),
        SMEM_SPEC,
    ]
    return specs, cur, prev


def _attn_forward(proj, sinks, S):
    T = proj.shape[0]
    nb, nq = T // S, S // QB
    specs, cur, _ = _attn_in_specs(S)

    def body(q_ref, kc_ref, kp_ref, vc_ref, vp_ref, gl_ref, gh_ref, sink_ref, y_ref):
        bias = _window_bias(pl.program_id(1) == 0)
        kc, kp, vc, vp = kc_ref[...], kp_ref[...], vc_ref[...], vp_ref[...]
        for kv in range(KV_HEADS):
            lanes = slice(256 * kv, 256 * (kv + 1))
            hl = slice(HEAD * kv, HEAD * (kv + 1))
            q_rows = _heads_to_rows(q_ref[:, lanes])
            k_cat = jnp.concatenate([kp[:, hl], kc[:, hl]], axis=0)
            v_cat = jnp.concatenate([vp[:, hl], vc[:, hl]], axis=0)
            probs, _ = _attn_probs(q_rows, k_cat, _sink_column(sink_ref, kv), bias)
            o = _rows_to_heads(_dot(probs.astype(bf16), v_cat, NN))
            g_src = gl_ref if kv < 2 else gh_ref
            g = g_src[:, 256 * (kv % 2):256 * (kv % 2 + 1)].astype(f32)
            y_ref[:, lanes] = (o * (g * _sigmoid(g))).astype(bf16)

    args = [proj] * 7 + [sinks]
    return _pcall(
        body, name="attn_forward", grid=(nb, nq), in_specs=specs,
        out_specs=pl.BlockSpec((QB, D), lambda b, j: (cur(b, j), 0)), out_shape=_sds((T, D), bf16),
        compiler_params=_params(("arbitrary", "arbitrary")),
    )(*args)


def _merge_and_head(x2d, tgt, proj, y_rnn, y_attn, w_land, gfin):
    T = x2d.shape[0]
    tb = min(T, 512)
    nsteps = T // tb

    def body(x_ref, t_ref, mr0, mr1, ma0, ma1, yr_ref, ya_ref, wr_ref, wa_ref, wo_ref, gf_ref,
             dx2_ref, dyr_ref, dya_ref, dmr_ref, dma_ref, loss_ref, gfin_ref, gwr_out, gwa_out, gwo_out,
             gwr_acc, gwa_acc, gwo_acc, out_sems):
        step = pl.program_id(0)

        @pl.when(step == 0)
        def _():
            loss_ref[...] = jnp.zeros_like(loss_ref)
            gfin_ref[...] = jnp.zeros_like(gfin_ref)
            gwr_acc[...] = jnp.zeros_like(gwr_acc)
            gwa_acc[...] = jnp.zeros_like(gwa_acc)
            gwo_acc[...] = jnp.zeros_like(gwo_acc)

        sr = _sigmoid(jnp.concatenate([mr0[...], mr1[...]], axis=1).astype(f32))
        sa = _sigmoid(jnp.concatenate([ma0[...], ma1[...]], axis=1).astype(f32))
        p_r = _dot(yr_ref[...], wr_ref[...], NN)
        p_a = _dot(ya_ref[...], wa_ref[...], NN)
        merged = (sr * p_r + sa * p_a).astype(bf16)
        x2 = x_ref[...] + _dot(merged, wo_ref[...], NN)
        rstd = lax.rsqrt(jnp.mean(x2 * x2, axis=-1, keepdims=True) + EPS)
        xh = x2 * rstd
        gf = gf_ref[...]
        err = xh * gf - t_ref[...]
        loss_ref[...] += jnp.sum(err * err)
        dy = err * (1.0 / D)
        gfin_ref[0:1, :] += jnp.sum(dy * xh, axis=0, keepdims=True)
        dxn = dy * gf
        dx2 = rstd * (dxn - xh * jnp.mean(dxn * xh, axis=-1, keepdims=True))
        dx2_ref[...] = dx2
        dx2b = dx2.astype(bf16)
        dmerged = _dot(dx2b, wo_ref[...], NT)
        dmr_ref[...] = (dmerged * p_r * (sr * (1.0 - sr))).astype(bf16)
        dma_ref[...] = (dmerged * p_a * (sa * (1.0 - sa))).astype(bf16)
        dpr = (dmerged * sr).astype(bf16)
        dpa = (dmerged * sa).astype(bf16)
        dyr_ref[...] = _dot(dpr, wr_ref[...], NT).astype(bf16)
        dya_ref[...] = _dot(dpa, wa_ref[...], NT).astype(bf16)
        gwr_acc[...] += _dot(yr_ref[...], dpr, TN)
        gwa_acc[...] += _dot(ya_ref[...], dpa, TN)
        gwo_acc[...] += _dot(merged, dx2b, TN)

        @pl.when(step == nsteps - 1)
        def _():
            copies = [pltpu.make_async_copy(src, dst, out_sems.at[k]) for k, (src, dst) in enumerate(
                ((gwr_acc, gwr_out), (gwa_acc, gwa_out), (gwo_acc, gwo_out)))]
            for cp in copies:
                cp.start()
            for cp in copies:
                cp.wait()

    tok = pl.BlockSpec((tb, D), lambda i: (i, 0))
    half = lambda c: pl.BlockSpec((tb, CH), lambda i, c=c: (i, c))
    wspec = lambda a: pl.BlockSpec((None, D, D), lambda i, a=a: (a, 0, 0), pipeline_mode=pl.Buffered(1))
    acc = pl.BlockSpec((8, D), lambda i: (0, 0))
    return _pcall(
        body, name="merge_and_head", grid=(nsteps,),
        in_specs=[tok, tok, half(9), half(10), half(11), half(12), tok, tok, wspec(0), wspec(1), wspec(2),
                  pl.BlockSpec((1, D), lambda i: (0, 0))],
        out_specs=(tok, tok, tok, tok, tok, acc, acc, ANY, ANY, ANY),
        out_shape=(_sds((T, D), f32), _sds((T, D), bf16), _sds((T, D), bf16), _sds((T, D), bf16),
                   _sds((T, D), bf16), _sds((8, D), f32), _sds((8, D), f32),
                   _sds((D, D), f32), _sds((D, D), f32), _sds((D, D), f32)),
        scratch_shapes=[pltpu.VMEM((D, D), f32)] * 3 + [pltpu.SemaphoreType.DMA((3,))],
        compiler_params=_params(("arbitrary",)),
    )(x2d, tgt, proj, proj, proj, proj, y_rnn, y_attn, w_land, w_land, w_land, gfin)


def _attn_backward(proj, dy_attn, tabs, sinks, S, chip_sums):
    T = proj.shape[0]
    nb, nq = T // S, S // QB
    nex = len(chip_sums)
    specs, cur, prev = _attn_in_specs(S)
    last = nq - 1
    tab_cur = pl.BlockSpec((QB, 128), lambda b, j: (jnp.minimum(j, last), 0))
    tab_prev = pl.BlockSpec((QB, 128), lambda b, j: (jnp.maximum(jnp.minimum(j, last) - 1, 0), 0))
    specs = specs + [pl.BlockSpec((QB, D), lambda b, j: (cur(b, j), 0))] + [tab_cur] * 3 + [tab_prev] * 3
    q_scale = 1.0 / math.sqrt(HEAD)

    def rope_back(dt, tab):
        return jnp.concatenate([_rope_transposed(dt[:, 128 * l:128 * (l + 1)], *tab) for l in range(2)], axis=1)

    def body(q_ref, kc_ref, kp_ref, vc_ref, vp_ref, gl_ref, gh_ref, sink_ref, dy_ref, cc, s1c, s2c, cp, s1p, s2p,
             *rest):
        ex_src = rest[:nex]
        dq_ref, dkv_ref, dg_ref, dsink_ref = rest[nex:nex + 4]
        ex_dst = rest[nex + 4:2 * nex + 4]
        carry_k, carry_v = rest[2 * nex + 4:2 * nex + 6]
        sems = rest[2 * nex + 6:]
        b, j = pl.program_id(0), pl.program_id(1)

        @pl.when((b == 0) & (j == 0))
        def _():
            dsink_ref[...] = jnp.zeros_like(dsink_ref)
            _start_all(_chip_exchange_copies(ex_src, ex_dst, *sems))

        @pl.when((b == nb - 1) & (j == nq))
        def _():
            _wait_all(_chip_exchange_copies(ex_src, ex_dst, *sems))

        @pl.when(j == 0)
        def _():
            carry_k[...] = jnp.zeros_like(carry_k)
            carry_v[...] = jnp.zeros_like(carry_v)

        @pl.when(j < nq)
        def _():
            bias = _window_bias(j == 0)
            tc = (cc[...], s1c[...], s2c[...])
            tp = (cp[...], s1p[...], s2p[...])
            kc, kp, vc, vp = kc_ref[...], kp_ref[...], vc_ref[...], vp_ref[...]
            dk_prev, dk_cur, dv_prev, dv_cur = [], [], [], []
            dsink_acc = jnp.zeros((8, 128), f32)
            r8 = lax.broadcasted_iota(jnp.int32, (8, 128), 0)
            l8 = lax.broadcasted_iota(jnp.int32, (8, 128), 1)
            for kv in range(KV_HEADS):
                lanes = slice(256 * kv, 256 * (kv + 1))
                hl = slice(HEAD * kv, HEAD * (kv + 1))
                q_rows = _heads_to_rows(q_ref[:, lanes])
                k_cat = jnp.concatenate([kp[:, hl], kc[:, hl]], axis=0)
                v_cat = jnp.concatenate([vp[:, hl], vc[:, hl]], axis=0)
                probs, p_sink = _attn_probs(q_rows, k_cat, _sink_column(sink_ref, kv), bias)
                pb = probs.astype(bf16)
                o = _rows_to_heads(_dot(pb, v_cat, NN))
                g_src = gl_ref if kv < 2 else gh_ref
                g = g_src[:, 256 * (kv % 2):256 * (kv % 2 + 1)].astype(f32)
                sg = _sigmoid(g)
                dy = dy_ref[:, lanes].astype(f32)
                dg_ref[:, lanes] = (dy * o * (sg * (1.0 + g * (1.0 - sg)))).astype(bf16)
                do_rows = _heads_to_rows(dy * (g * sg)).astype(bf16)
                dv = _dot(pb, do_rows, TN)
                dp = _dot(do_rows, v_cat, NT)
                rowdot = jnp.sum(probs * dp, axis=1, keepdims=True)
                ds = (probs * (dp - rowdot)).astype(bf16)
                sink_rows = -(p_sink * rowdot)
                for h in range(GROUP):
                    val = jnp.sum(sink_rows[QB * h:QB * (h + 1), :])
                    dsink_acc = dsink_acc + jnp.where((r8 == 0) & (l8 == GROUP * kv + h), val, 0.0)
                dq = _rows_to_heads(_dot(ds, k_cat, NN)) * q_scale
                dq_ref[:, lanes] = rope_back(dq, tc).astype(bf16)
                dk = _dot(ds, q_rows, TN)
                dk_prev.append(dk[:QB, :])
                dk_cur.append(dk[QB:, :])
                dv_prev.append(dv[:QB, :])
                dv_cur.append(dv[QB:, :])
            dsink_ref[...] += dsink_acc
            dkp = rope_back(jnp.concatenate(dk_prev, axis=1), tp)
            dkc = rope_back(jnp.concatenate(dk_cur, axis=1), tc)
            dkv_ref[:, 0:256] = (carry_k[...] + dkp).astype(bf16)
            dkv_ref[:, 256:512] = (carry_v[...] + jnp.concatenate(dv_prev, axis=1)).astype(bf16)
            carry_k[...] = dkc
            carry_v[...] = jnp.concatenate(dv_cur, axis=1)

        @pl.when(j == nq)
        def _():
            dkv_ref[:, 0:256] = carry_k[...].astype(bf16)
            dkv_ref[:, 256:512] = carry_v[...].astype(bf16)

    lag = lambda b, j: (b * nq + jnp.maximum(j - 1, 0), 0)
    args = [proj] * 7 + [sinks, dy_attn] + list(tabs) + list(tabs) + list(chip_sums)
    res = _pcall(
        body, name="attn_backward", grid=(nb, nq + 1), in_specs=specs + [ANY] * nex,
        out_specs=(pl.BlockSpec((QB, D), lambda b, j: (cur(b, j), 0)), pl.BlockSpec((QB, 512), lag),
                   pl.BlockSpec((QB, D), lambda b, j: (cur(b, j), 0)), pl.BlockSpec((8, 128), lambda b, j: (0, 0)))
        + tuple([ANY] * nex),
        out_shape=(_sds((T, D), bf16), _sds((T, 512), bf16), _sds((T, D), bf16), _sds((8, 128), f32))
        + tuple(_sds(s.shape, s.dtype) for s in chip_sums),
        scratch_shapes=[pltpu.VMEM((QB, 256), f32), pltpu.VMEM((QB, 256), f32)] + _exchange_scratch(nex, 3),
        compiler_params=_params(("arbitrary", "arbitrary")),
    )(*args)
    return res[:4], res[4:]


def _lru_backward(proj, h_all, dy_rnn, cw_full, conv_b, w_a, b_a, w_x, b_x, lam, S):
    T = proj.shape[0]
    nb = T // S
    col, vec, wblk, cwblk = _lru_specs(S, nb)
    tokblk = pl.BlockSpec((S, RB), lambda n, b: (b, n))

    def body(x0_ref, g_ref, h_ref, dy_ref, cw_ref, cb_ref, wa_ref, ba_ref, wx_ref, bx_ref, lam_ref,
             du0_ref, dg_ref, gwa_ref, gwx_ref, vec_ref, gcw_ref, a_s, b_s, dh_s, edge_s):
        @pl.when(pl.program_id(1) == 0)
        def _():
            gwa_ref[...] = jnp.zeros_like(gwa_ref)
            gwx_ref[...] = jnp.zeros_like(gwx_ref)
            vec_ref[...] = jnp.zeros_like(vec_ref)
            gcw_ref[...] = jnp.zeros_like(gcw_ref)

        x0 = x0_ref[...].astype(f32)
        cw = cw_ref[...]
        lam_v = lam_ref[...]
        u, ub, r, i, sp, a, mult, inv_mult, taps = _lru_gates(x0, cw, cb_ref[...], wa_ref[...], ba_ref[...],
                                                              wx_ref[...], bx_ref[...], lam_v)
        h = h_ref[...]
        g = g_ref[...].astype(f32)
        dy = dy_ref[...].astype(f32)
        sg = _sigmoid(g)
        dg_ref[...] = (dy * h * (sg * (1.0 + g * (1.0 - sg)))).astype(bf16)
        _linear_scan(_shift_up(a, 1), dy * (g * sg), a_s, b_s, edge_s, dh_s, reverse=True)
        dh_total = dh_s[...]
        da = dh_total * _shift_down(h, 1)
        dmult = dh_total * (i * u)
        db = dh_total * mult
        di = db * u
        du = db * i
        dlog_a_c = ((-LRU_C) * a) * (da - dmult * (a * inv_mult))
        dr = dlog_a_c * sp
        dsp = jnp.sum(dlog_a_c * r, axis=0, keepdims=True)
        dpre_r = dr * r * (1.0 - r)
        dpre_i = di * i * (1.0 - i)
        dpre_rb = dpre_r.astype(bf16)
        dpre_ib = dpre_i.astype(bf16)
        du = du + _dot(dpre_rb, wa_ref[...].astype(bf16), NT) + _dot(dpre_ib, wx_ref[...].astype(bf16), NT)
        gwa_ref[...] += _dot(ub, dpre_rb, TN)
        gwx_ref[...] += _dot(ub, dpre_ib, TN)
        vec_ref[0:1, :] += jnp.sum(du, axis=0, keepdims=True)
        vec_ref[1:2, :] += jnp.sum(dpre_r, axis=0, keepdims=True)
        vec_ref[2:3, :] += jnp.sum(dpre_i, axis=0, keepdims=True)
        vec_ref[3:4, :] += dsp * (-_sigmoid(-lam_v))
        dx0 = cw[3:4, :] * du
        for k in range(3):
            dx0 = dx0 + cw[k:k + 1, :] * _shift_up(du, 3 - k)
        for k in range(4):
            gcw_ref[k:k + 1, :] += jnp.sum(du * taps[k], axis=0, keepdims=True)
        du0_ref[...] = dx0.astype(bf16)

    wacc = pl.BlockSpec((RB, RB), lambda n, b: (0, n))
    vacc = pl.BlockSpec((8, RB), lambda n, b: (0, n))
    cacc = pl.BlockSpec((8, RB), lambda n, b: (n, 0))
    return _pcall(
        body, name="lru_backward", grid=(RNN_BLOCKS, nb),
        in_specs=[col(0), col(8), tokblk, tokblk, cwblk, vec, wblk, vec, wblk, vec, vec],
        out_specs=(tokblk, tokblk, wacc, wacc, vacc, cacc),
        out_shape=(_sds((T, D), bf16), _sds((T, D), bf16), _sds((RB, D), f32), _sds((RB, D), f32),
                   _sds((8, D), f32), _sds((8 * RNN_BLOCKS, RB), f32)),
        scratch_shapes=[pltpu.VMEM((S, RB), f32)] * 3 + [pltpu.VMEM((S // 8, RB), f32)],
        compiler_params=_params(("arbitrary", "arbitrary")),
    )(proj, proj, h_all, dy_rnn, cw_full, conv_b, w_a, b_a, w_x, b_x, lam)


def _section_of_chunk(s):
    out = []
    for start, n in zip(SEC_START, SEC_CHUNKS):
        inside = (s >= start) & (s < start + n)
        out.append((inside, jnp.clip(s - start, 0, n - 1)))
    return out


EFFECT = pltpu.SideEffectType.DATAFLOW_SIDE_EFFECTING
HBM_SPEC = pl.BlockSpec(memory_space=pltpu.HBM)
SEM_SPEC = pl.BlockSpec(memory_space=pltpu.SEMAPHORE)


def _split_exchange_copies(src_ref, land_ref, send_sems, recv_sems):
    x, y, c = _my_place()
    copies = []
    for k in (3, 1, 2):
        px, py = (x + (k >> 1)) % 2, (y + (k & 1)) % 2
        copies.append(pltpu.make_async_remote_copy(
            src_ref=src_ref.at[2 * px + py], dst_ref=land_ref.at[k - 1], send_sem=send_sems[k - 1],
            recv_sem=recv_sems[k - 1], device_id=(px, py, c), device_id_type=MESH))
    return copies


def _exchange_start(chip_sum):
    _, r, cols = chip_sum.shape

    def body(src_ref, land_ref, s0, s1, s2, r0, r1, r2, src_thru, land_thru, token):
        for cp in _split_exchange_copies(src_ref, land_ref, (s0, s1, s2), (r0, r1, r2)):
            cp.start()
        token[...] = jnp.zeros_like(token)

    land = pltpu.with_memory_space_constraint(lax.empty((3, r, cols), chip_sum.dtype), pltpu.HBM)
    res = _pcall(
        body, name="exchange_start",
        out_shape=tuple([pltpu.SemaphoreType.DMA(())] * 6) + (
            pltpu.HBM(chip_sum.shape, chip_sum.dtype), pltpu.HBM((3, r, cols), chip_sum.dtype), _sds((8, 128), f32)),
        in_specs=(HBM_SPEC, HBM_SPEC), out_specs=tuple([SEM_SPEC] * 6) + (HBM_SPEC, HBM_SPEC, VMEM_SPEC),
        input_output_aliases={0: 6, 1: 7},
        compiler_params=pltpu.CompilerParams(has_side_effects=EFFECT),
    )(pltpu.with_memory_space_constraint(chip_sum, pltpu.HBM), land)
    return res[:6], res[6], res[7], res[8]


def _exchange_wait(sems, src_thru, land_thru, after):
    def body(src_ref, land_ref, s0, s1, s2, r0, r1, r2, after_ref, src_dead, got_ref):
        for cp in _split_exchange_copies(src_ref, land_ref, (s0, s1, s2), (r0, r1, r2)):
            cp.wait_send()
            cp.wait_recv()

    return _pcall(
        body, name="exchange_wait",
        out_shape=(pltpu.HBM(src_thru.shape, src_thru.dtype), pltpu.HBM(land_thru.shape, land_thru.dtype)),
        in_specs=(HBM_SPEC, HBM_SPEC) + tuple([SEM_SPEC] * 6) + (ANY,), out_specs=(HBM_SPEC, HBM_SPEC),
        input_output_aliases={0: 0, 1: 1},
        compiler_params=pltpu.CompilerParams(has_side_effects=EFFECT),
    )(src_thru, land_thru, *sems, after)[1]


def _split_gather_copies(src_ref, land_ref, send_sems, recv_sems):
    x, y, c = _my_place()
    mine = land_ref.at[:, 4 * x + 2 * y + c]
    return [pltpu.make_async_remote_copy(src_ref=src_ref, dst_ref=mine, send_sem=send_sems[k - 1],
                                         recv_sem=recv_sems[k - 1], device_id=_peer(k), device_id_type=MESH)
            for k in range(1, NDEV)]


def _gather_start(block, after):
    n = NDEV - 1

    def body(src_ref, land_ref, after_ref, *rest):
        for cp in _split_gather_copies(src_ref, land_ref, rest[:n], rest[n:2 * n]):
            cp.start()
        rest[2 * n + 2][...] = jnp.zeros_like(rest[2 * n + 2])

    land_shape = (block.shape[0], NDEV) + block.shape[1:]
    land = pltpu.with_memory_space_constraint(lax.empty(land_shape, block.dtype), pltpu.HBM)
    res = _pcall(
        body, name="gather_start",
        out_shape=tuple([pltpu.SemaphoreType.DMA(())] * (2 * n)) + (
            pltpu.HBM(block.shape, block.dtype), pltpu.HBM(land_shape, block.dtype), _sds((8, 128), f32)),
        in_specs=(HBM_SPEC, HBM_SPEC, ANY), out_specs=tuple([SEM_SPEC] * (2 * n)) + (HBM_SPEC, HBM_SPEC, VMEM_SPEC),
        input_output_aliases={0: 2 * n, 1: 2 * n + 1},
        compiler_params=pltpu.CompilerParams(has_side_effects=EFFECT),
    )(pltpu.with_memory_space_constraint(block, pltpu.HBM), land, after)
    return res[:2 * n], res[2 * n], res[2 * n + 1], res[2 * n + 2]


def _gather_wait(sems, src_thru, land_thru, after):
    n = NDEV - 1

    def body(src_ref, land_ref, *rest):
        for cp in _split_gather_copies(src_ref, land_ref, rest[:n], rest[n:2 * n]):
            cp.wait_send()
            cp.wait_recv()

    return _pcall(
        body, name="gather_wait",
        out_shape=(pltpu.HBM(src_thru.shape, src_thru.dtype), pltpu.HBM(land_thru.shape, land_thru.dtype)),
        in_specs=(HBM_SPEC, HBM_SPEC) + tuple([SEM_SPEC] * (2 * n)) + (ANY,), out_specs=(HBM_SPEC, HBM_SPEC),
        input_output_aliases={0: 0, 1: 1},
        compiler_params=pltpu.CompilerParams(has_side_effects=EFFECT),
    )(src_thru, land_thru, *sems, after)[1]


def _input_grad(dsecs, wt_full, x2d, dx2, norm_g):
    T = x2d.shape[0]
    tb = min(T, 512)
    nsec = len(dsecs)
    ntok = T // tb

    def body(*refs):
        secs = refs[:nsec]
        wt_ref, x_ref, dx2_ref, g_ref, dx_ref, gnorm_ref = refs[nsec:]
        i = pl.program_id(0)

        @pl.when(i == 0)
        def _():
            gnorm_ref[...] = jnp.zeros_like(gnorm_ref)

        dh = None
        for a, (start, n) in enumerate(zip(SEC_START, SEC_CHUNKS)):
            part = _dot(secs[a][...], wt_ref[CH * start:CH * (start + n), :], NN)
            dh = part if dh is None else dh + part
        xv = x_ref[...]
        rstd = lax.rsqrt(jnp.mean(xv * xv, axis=-1, keepdims=True) + EPS)
        xh = xv * rstd
        gnorm_ref[0:1, :] += jnp.sum(dh * xh, axis=0, keepdims=True)
        dxn = dh * g_ref[...]
        dx_ref[...] = dx2_ref[...] + rstd * (dxn - xh * jnp.mean(dxn * xh, axis=-1, keepdims=True))

    tok = pl.BlockSpec((tb, D), lambda i: (i, 0))
    return _pcall(
        body, name="input_grad", grid=(ntok,),
        in_specs=[pl.BlockSpec((tb, sec.shape[1]), lambda i: (i, 0)) for sec in dsecs]
        + [pl.BlockSpec((D_IN, D), lambda i: (0, 0), pipeline_mode=pl.Buffered(1)), tok, tok,
           pl.BlockSpec((1, D), lambda i: (0, 0))],
        out_specs=(tok, pl.BlockSpec((8, D), lambda i: (0, 0))),
        out_shape=(_sds((T, D), f32), _sds((8, D), f32)),
        compiler_params=_params(("arbitrary",)),
    )(*dsecs, wt_full, x2d, dx2, norm_g)


def _w_in_grad(dsecs, h_bf):
    T = h_bf.shape[0]
    tk = min(T, 2048)
    nchunks = D_IN // CH
    nsec = len(dsecs)
    nt = T // tk

    def body(*refs):
        secs = refs[:nsec]
        h_ref, out_ref, acc = refs[nsec:]
        s, t = pl.program_id(0), pl.program_id(1)

        @pl.when(t == 0)
        def _():
            acc[...] = jnp.zeros_like(acc)

        h_rows = h_ref[pl.ds(pl.multiple_of(t * tk, tk), tk), :]
        for a, (start, n) in enumerate(zip(SEC_START, SEC_CHUNKS)):
            @pl.when((s >= start) & (s < start + n))
            def _(a=a):
                acc[...] += _dot(secs[a][...], h_rows, TN)

        @pl.when(t == nt - 1)
        def _():
            out_ref[...] = acc[...].astype(bf16)

    def sec_spec(a):
        def index(s, t, a=a):
            inside, local = _section_of_chunk(s)[a]
            return (jnp.where(inside, t, 0), local)
        return pl.BlockSpec((tk, CH), index)

    return _pcall(
        body, name="w_in_grad", grid=(nchunks, T // tk),
        in_specs=[sec_spec(a) for a in range(nsec)]
        + [pl.BlockSpec((T, D), lambda s, t: (0, 0), pipeline_mode=pl.Buffered(1))],
        out_specs=pl.BlockSpec((CH, D), lambda s, t: (s, 0)), out_shape=_sds((D_IN, D), bf16),
        scratch_shapes=[pltpu.VMEM((CH, D), f32)],
        compiler_params=_params(("arbitrary", "arbitrary")),
    )(*dsecs, h_bf)


SMALL_NAMES = ("lru_w_a", "lru_w_x", "conv_b", "lru_b_a", "lru_b_x", "lru_lambda", "norm_g", "final_norm_g",
               "attn_sinks", "conv_w")
MISC_ROW = {"conv_b": 0, "lru_b_a": 1, "lru_b_x": 2, "lru_lambda": 3, "norm_g": 8, "final_norm_g": 16,
            "attn_sinks": 24, "loss": 32}


def _small_step(gwa, gwx, gvec, gnorm_blk, gfin_blk, dsink_blk, loss_blk, gcw, params):
    srcs_rows = (RB // NDEV, RB // NDEV, 8, 8)
    flat = [t for n in SMALL_NAMES for t in params[n]]
    nin = 8 + len(flat)
    nout = 4 * len(SMALL_NAMES) + 1

    def body(*refs):
        gwa_ref, gwx_ref, gvec_ref, gnorm_ref, gfin_ref, dsink_ref, loss_ref, gcw_ref = refs[:8]
        outs = {n: refs[nin + 4 * k:nin + 4 * k + 4] for k, n in enumerate(SMALL_NAMES)}
        loss_out = refs[nin + nout - 1]
        (misc, got_a, got_x, got_m, got_c, red_a, red_x, red_m, all_a, all_x, all_m,
         sa, ra, sb, rb) = refs[nin + nout:nin + nout + 15]
        staged = refs[nin + nout + 15:nin + nout + 15 + len(flat)]
        stage_sems = refs[nin + nout + 15 + len(flat)]
        prm = {n: staged[3 * k:3 * k + 3] for k, n in enumerate(SMALL_NAMES)}
        x, y, c = _my_place()
        me = 4 * x + 2 * y + c

        fetch = [pltpu.make_async_copy(refs[8 + k], staged[k], stage_sems.at[k]) for k in range(len(flat))]
        for cp in fetch:
            cp.start()

        misc[...] = jnp.zeros_like(misc)
        misc[0:8, :] = gvec_ref[...]
        misc[8:16, :] = gnorm_ref[...]
        misc[16:24, :] = gfin_ref[...]
        misc[24:32, 0:128] = dsink_ref[...]
        misc[32:40, :] = loss_ref[...]

        srcs = (gwa_ref, gwx_ref, misc, gcw_ref)
        gots = (got_a, got_x, got_m, got_c)

        def shard(ref, rows, dev):
            return ref.at[pl.ds(pl.multiple_of(dev * rows, 8), rows), :]

        scatter = []
        for k in range(1, NDEV):
            px, py, pc = _peer(k)
            for a in range(4):
                scatter.append(pltpu.make_async_remote_copy(
                    src_ref=shard(srcs[a], srcs_rows[a], 4 * px + 2 * py + pc), dst_ref=gots[a].at[k - 1],
                    send_sem=sa.at[4 * (k - 1) + a], recv_sem=ra.at[4 * (k - 1) + a],
                    device_id=(px, py, pc), device_id_type=MESH))
        for cp in scatter:
            cp.start()
        for cp in scatter:
            cp.wait()

        def reduced(a):
            rows = srcs_rows[a]
            total = srcs[a][pl.ds(pl.multiple_of(me * rows, 8), rows), :]
            for k in range(NDEV - 1):
                total = total + gots[a][k]
            return total

        reds = (red_a, red_x, red_m)
        alls = (all_a, all_x, all_m)
        for a in range(3):
            val = reduced(a)
            reds[a][...] = val
            alls[a][pl.ds(pl.multiple_of(me * srcs_rows[a], 8), srcs_rows[a]), :] = val
        gather = []
        for k in range(1, NDEV):
            peer = _peer(k)
            for a in range(3):
                gather.append(pltpu.make_async_remote_copy(
                    src_ref=reds[a], dst_ref=shard(alls[a], srcs_rows[a], me),
                    send_sem=sb.at[3 * (k - 1) + a], recv_sem=rb.at[3 * (k - 1) + a],
                    device_id=peer, device_id_type=MESH))
        for cp in gather:
            cp.start()
        g_conv = reduced(3)[0:4, :]
        for cp in gather:
            cp.wait()
        for cp in fetch:
            cp.wait()

        def update(name, g, pick=lambda r: r[...]):
            w_ref, m_ref, v_ref = prm[name]
            delta, m_new, v_new = _adam_math(g, pick(w_ref), pick(m_ref), pick(v_ref))
            return g, delta, m_new, v_new

        for n in range(RNN_BLOCKS):
            lanes = slice(RB * n, RB * (n + 1))
            for name, full in (("lru_w_a", all_a), ("lru_w_x", all_x)):
                for out, val in zip(outs[name], update(name, full[:, lanes], pick=lambda r, n=n: r[n])):
                    out[n] = val
        for name in ("conv_b", "lru_b_a", "lru_b_x", "lru_lambda", "norm_g", "final_norm_g"):
            row = MISC_ROW[name]
            for out, val in zip(outs[name], update(name, all_m[row:row + 1, :])):
                out[...] = val
        row = MISC_ROW["attn_sinks"]
        for out, val in zip(outs["attn_sinks"], update("attn_sinks", all_m[row:row + 1, 0:16])):
            out[...] = val
        for out, val in zip(outs["conv_w"], update("conv_w", g_conv)):
            out[...] = val
        row = MISC_ROW["loss"]
        loss_out[...] = all_m[row:row + 8, 0:128] * (0.5 / D)

    out_shape = tuple(_sds(params[n][0].shape, f32) for n in SMALL_NAMES for _ in range(4)) + (_sds((8, 128), f32),)
    scratch = [pltpu.VMEM((64, D), f32),
               pltpu.VMEM((NDEV - 1, RB // NDEV, D), f32), pltpu.VMEM((NDEV - 1, RB // NDEV, D), f32),
               pltpu.VMEM((NDEV - 1, 8, D), f32), pltpu.VMEM((NDEV - 1, 8, RB), f32),
               pltpu.VMEM((RB // NDEV, D), f32), pltpu.VMEM((RB // NDEV, D), f32), pltpu.VMEM((8, D), f32),
               pltpu.VMEM((RB, D), f32), pltpu.VMEM((RB, D), f32), pltpu.VMEM((64, D), f32),
               pltpu.SemaphoreType.DMA((4 * (NDEV - 1),)), pltpu.SemaphoreType.DMA((4 * (NDEV - 1),)),
               pltpu.SemaphoreType.DMA((3 * (NDEV - 1),)), pltpu.SemaphoreType.DMA((3 * (NDEV - 1),))]
    scratch += [pltpu.VMEM(t.shape, f32) for t in flat] + [pltpu.SemaphoreType.DMA((len(flat),))]
    res = _pcall(
        body, name="small_step", out_shape=out_shape,
        in_specs=[VMEM_SPEC] * 8 + [ANY] * len(flat), out_specs=tuple([VMEM_SPEC] * nout),
        scratch_shapes=scratch, compiler_params=_params(),
    )(gwa, gwx, gvec, gnorm_blk, gfin_blk, dsink_blk, loss_blk, gcw, *flat)
    return {n: res[4 * k:4 * k + 4] for k, n in enumerate(SMALL_NAMES)}, res[-1]


def _pad_rows(v, rows=8):
    return jnp.concatenate([v, jnp.zeros((rows - v.shape[0], v.shape[1]), v.dtype)], axis=0)


def kernel(x, norm_g, w_in, conv_w, conv_b, lru_w_a, lru_b_a, lru_w_x, lru_b_x, lru_lambda, attn_sinks, w_rnn_out, w_attn_out, w_o, final_norm_g, loss_target, m_norm_g, m_w_in, m_conv_w, m_conv_b, m_lru_w_a, m_lru_b_a, m_lru_w_x, m_lru_b_x, m_lru_lambda, m_attn_sinks, m_w_rnn_out, m_w_attn_out, m_w_o, m_final_norm_g, v_norm_g, v_w_in, v_conv_w, v_conv_b, v_lru_w_a, v_lru_b_a, v_lru_w_x, v_lru_b_x, v_lru_lambda, v_attn_sinks, v_w_rnn_out, v_w_attn_out, v_w_o, v_final_norm_g):
    nb, S, _ = x.shape
    T = nb * S
    x2d = x.reshape(T, D)
    tgt = loss_target.reshape(T, D)
    fin_g = final_norm_g.reshape(1, D)
    w_a3, w_x3 = lru_w_a[0], lru_w_x[0]

    my_core = lax.axis_index("c").astype(jnp.int32).reshape(1)
    cx, cy = lax.axis_index("x"), lax.axis_index("y")
    chip_order = jnp.stack([2 * cx + cy, 2 * (1 - cx) + cy, 2 * cx + (1 - cy),
                            2 * (1 - cx) + (1 - cy)]).astype(jnp.int32)

    tabs = _rope_tables(S)
    h_bf, proj, wt_full, cw_full, _ = _in_proj_gather(
        x2d, norm_g, w_in[0].T.astype(bf16), _pad_rows(conv_w[0]), tabs, S, (), chip_order)
    w_out3 = jnp.stack([w_rnn_out[0], w_attn_out[0], w_o[0]]).astype(bf16)
    g_sems, w_out3, w_land, g_token = _gather_start(w_out3, cw_full)
    y_rnn, h_all = _lru_forward(proj, cw_full, conv_b + g_token[0, 0], w_a3, lru_b_a, w_x3, lru_b_x, lru_lambda, S)
    y_attn = _attn_forward(proj, attn_sinks, S)
    w_land = _gather_wait(g_sems, w_out3, w_land, y_attn)
    w_land = lax.dynamic_update_slice(w_land, w_out3[:, None], (0, 4 * cx + 2 * cy + lax.axis_index("c"), 0, 0))
    w_land = pltpu.with_memory_space_constraint(w_land.reshape(3, D, D), pltpu.HBM)

    (dx2, dy_rnn, dy_attn, dmr, dma, loss_blk, gfin_blk, g_wr, g_wa, g_wo) = _merge_and_head(
        x2d, tgt, proj, y_rnn, y_attn, w_land, fin_g)
    sums_out = _pair_sums([g_wr, g_wa, g_wo], bf16, my_core, "out")

    (dq, dkv, dga, dsink_blk), (p_wr, p_wa, p_wo) = _attn_backward(proj, dy_attn, tabs, attn_sinks, S, sums_out)
    du0, dgr, gwa, gwx, gvec, gcw = _lru_backward(proj, h_all, dy_rnn, cw_full, conv_b, w_a3, lru_b_a, w_x3,
                                                  lru_b_x, lru_lambda, S)
    dsecs = (du0, dgr, dq, dkv, dga, dmr, dma)

    g_wt = _w_in_grad(dsecs, h_bf)
    (sum_in,) = _pair_sums([g_wt], bf16, my_core, "in")
    ex_sems, sum_in, landing, token = _exchange_start(sum_in)
    grad_x2d, gnorm_blk = _input_grad(dsecs, wt_full, x2d, dx2, norm_g + token[0, 0])
    p_wt = _exchange_wait(ex_sems, sum_in, landing, gnorm_blk)
    p_wt_own = lax.dynamic_index_in_dim(sum_in, 2 * cx + cy, axis=0, keepdims=False)

    small, loss_out = _small_step(gwa, gwx, gvec, gnorm_blk, gfin_blk, dsink_blk, loss_blk, gcw, {
        "lru_w_a": (w_a3, m_lru_w_a[0], v_lru_w_a[0]), "lru_w_x": (w_x3, m_lru_w_x[0], v_lru_w_x[0]),
        "conv_b": (conv_b, m_conv_b, v_conv_b), "lru_b_a": (lru_b_a, m_lru_b_a, v_lru_b_a),
        "lru_b_x": (lru_b_x, m_lru_b_x, v_lru_b_x), "lru_lambda": (lru_lambda, m_lru_lambda, v_lru_lambda),
        "norm_g": (norm_g, m_norm_g, v_norm_g),
        "final_norm_g": (fin_g, m_final_norm_g.reshape(1, D), v_final_norm_g.reshape(1, D)),
        "attn_sinks": (attn_sinks, m_attn_sinks, v_attn_sinks),
        "conv_w": (conv_w[0], m_conv_w[0], v_conv_w[0])})

    o_wt = _adamw(p_wt_own, p_wt, w_in[0].T, m_w_in[0].T, v_w_in[0].T, "adamw_w_in")
    o_wr, o_wa, o_wo = _adamw_group(
        (p_wr, p_wa, p_wo), (w_rnn_out[0], w_attn_out[0], w_o[0]),
        (m_w_rnn_out[0], m_w_attn_out[0], m_w_o[0]), (v_w_rnn_out[0], v_w_attn_out[0], v_w_o[0]), "adamw_w_out")

    def result(kind):
        d = {n: small[n][kind] for n in ("conv_b", "lru_b_a", "lru_b_x", "lru_lambda", "norm_g", "attn_sinks")}
        d.update({n: small[n][kind][None] for n in ("lru_w_a", "lru_w_x", "conv_w")})
        d["final_norm_g"] = small["final_norm_g"][kind].reshape(D)
        d.update({"w_in": o_wt[kind].T[None], "w_rnn_out": o_wr[kind][None], "w_attn_out": o_wa[kind][None],
                  "w_o": o_wo[kind][None]})
        return d

    order = ("norm_g", "w_in", "conv_w", "conv_b", "lru_w_a", "lru_b_a", "lru_w_x", "lru_b_x", "lru_lambda",
             "attn_sinks", "w_rnn_out", "w_attn_out", "w_o", "final_norm_g")
    outs = [loss_out[0, 0], grad_x2d.reshape(nb, S, D)]
    for kind in range(4):
        d = result(kind)
        outs += [d[n] for n in order]
    return tuple(outs)
```

```python
import functools
import math

import jax
import jax.numpy as jnp
from jax import lax
from jax.experimental import pallas as pl
from jax.experimental.pallas import tpu as pltpu

f32 = jnp.float32
bf16 = jnp.bfloat16

D = 1024
D_IN = 6656
NDEV = 8
RNN_BLOCKS = 8
RB = 128
HEAD = 64
KV_HEADS = 4
GROUP = 4
QB = 128
LRU_C = 8.0
EPS = 1e-6
ROPE_DIM = 16
ROPE_THETA = 500000.0
CH = 512
SEC_START = (0, 2, 4, 6, 7, 9, 11)
SEC_CHUNKS = (2, 2, 2, 1, 2, 2, 2)
VMEM_LIMIT = 62 * 1024 * 1024

ADAM_LR, ADAM_B1, ADAM_B2, ADAM_EPS, ADAM_WD, ADAM_STEP = 0.001, 0.9, 0.999, 1e-08, 0.01, 10

MESH = pl.DeviceIdType.MESH
ANY = pl.BlockSpec(memory_space=pl.ANY)
VMEM_SPEC = pl.BlockSpec(memory_space=pltpu.VMEM)
SMEM_SPEC = pl.BlockSpec(memory_space=pltpu.SMEM)


def _pcall(body, **kw):
    return pl.pallas_call(body, **kw)


def _params(sem=None, **kw):
    if sem is not None:
        kw["dimension_semantics"] = sem
    return pltpu.CompilerParams(vmem_limit_bytes=VMEM_LIMIT, **kw)


def _sds(shape, dtype):
    return jax.ShapeDtypeStruct(shape, dtype)


def _dot(a, b, dims):
    return lax.dot_general(a, b, (dims, ((), ())), preferred_element_type=f32)


NN = ((1,), (0,))
NT = ((1,), (1,))
TN = ((0,), (0,))


def _sigmoid(v):
    return 0.5 * jnp.tanh(0.5 * v) + 0.5


def _sigmoid_positive(v):
    return 1.0 / (1.0 + jnp.exp(-v))


def _my_place():
    return lax.axis_index("x"), lax.axis_index("y"), lax.axis_index("c")


def _peer(k):
    x, y, c = _my_place()
    return (x + ((k >> 2) & 1)) % 2, (y + ((k >> 1) & 1)) % 2, (c + (k & 1)) % 2


def _direct_gather_copies(srcs, outs, send_sems, recv_sems, local_sems):
    x, y, c = _my_place()
    me = 4 * x + 2 * y + c
    local, remote = [], []
    for a, (src, out) in enumerate(zip(srcs, outs)):
        r = src.shape[0]
        mine = out.at[pl.ds(pl.multiple_of(me * r, 8), r), :]
        local.append(pltpu.make_async_copy(src, mine, local_sems.at[a]))
        for k in range(1, NDEV):
            remote.append(pltpu.make_async_remote_copy(
                src_ref=src, dst_ref=mine, send_sem=send_sems.at[7 * a + k - 1], recv_sem=recv_sems.at[7 * a + k - 1],
                device_id=_peer(k), device_id_type=MESH))
    return local, remote


def _chip_exchange_copies(src, dst, send_sems, recv_sems, local_sems):
    x, y, c = _my_place()
    local, remote = [], []
    for a in range(len(src)):
        local.append(pltpu.make_async_copy(src[a].at[2 * x + y], dst[a].at[0], local_sems.at[a]))
    for k in (3, 1, 2):
        px, py = (x + (k >> 1)) % 2, (y + (k & 1)) % 2
        for a in range(len(src)):
            remote.append(pltpu.make_async_remote_copy(
                src_ref=src[a].at[2 * px + py], dst_ref=dst[a].at[k],
                send_sem=send_sems.at[3 * a + k - 1], recv_sem=recv_sems.at[3 * a + k - 1],
                device_id=(px, py, c), device_id_type=MESH))
    return local, remote


def _exchange_scratch(narr, per_array):
    return [pltpu.SemaphoreType.DMA((per_array * narr,)), pltpu.SemaphoreType.DMA((per_array * narr,)),
            pltpu.SemaphoreType.DMA((narr,))]


def _start_all(copies):
    local, remote = copies
    for cp in local + remote:
        cp.start()


def _wait_all(copies):
    local, remote = copies
    for cp in remote + local:
        cp.wait()


def _pair_exchange(grads, name):
    narr = len(grads)
    nrows = tuple(g.shape[0] // NDEV for g in grads)
    views = [g.reshape(4, 2, r, g.shape[1]) for g, r in zip(grads, nrows)]

    def body(*refs):
        gin = refs[:narr]
        got = refs[narr:2 * narr]
        send_sems, recv_sems = refs[2 * narr:]
        x, y, c = _my_place()
        copies = [pltpu.make_async_remote_copy(
            src_ref=gin[a].at[:, pl.ds(1 - c, 1)], dst_ref=got[a],
            send_sem=send_sems.at[a], recv_sem=recv_sems.at[a],
            device_id=(x, y, 1 - c), device_id_type=MESH) for a in range(narr)]
        for cp in copies:
            cp.start()
        for cp in copies:
            cp.wait()

    out_shape = tuple(_sds((4, 1, r, g.shape[1]), g.dtype) for r, g in zip(nrows, grads))
    got = _pcall(
        body, name=name, out_shape=out_shape,
        in_specs=[ANY] * narr, out_specs=tuple([ANY] * narr),
        scratch_shapes=[pltpu.SemaphoreType.DMA((narr,)), pltpu.SemaphoreType.DMA((narr,))],
        compiler_params=_params(),
    )(*views)
    return views, [g.reshape(4, r, g.shape[3]) for g, r in zip(got, nrows)]


def _row_tile(rows, dtype):
    unit = 16 if dtype == bf16 else 8
    for cand in (256, 208, 128, 64, 40, 32, 16, 8):
        if rows % cand == 0 and cand % unit == 0:
            return cand
    return rows


def _chip_sum(views, gots, my_core, out_dtype, name):
    narr = len(views)
    _, _, r, cols = views[0].shape
    tr = _row_tile(r, out_dtype)

    def body(core_ref, *refs):
        for a in range(narr):
            mine_ref, got_ref, out_ref = refs[a], refs[narr + a], refs[2 * narr + a]
            out_ref[...] = (mine_ref[...].astype(f32) + got_ref[...].astype(f32)).astype(out_dtype)

    slab = pl.BlockSpec((None, tr, cols), lambda q, i, core: (q, i, 0))
    grid_spec = pltpu.PrefetchScalarGridSpec(
        num_scalar_prefetch=1, grid=(4, r // tr),
        in_specs=[pl.BlockSpec((None, None, tr, cols), lambda q, i, core: (q, core[0], i, 0))] * narr + [slab] * narr,
        out_specs=tuple([slab] * narr))
    return _pcall(body, name=name, grid_spec=grid_spec,
                  out_shape=tuple(_sds((4, r, cols), out_dtype) for _ in range(narr)),
                  compiler_params=_params(("arbitrary", "arbitrary")))(my_core, *views, *gots)


def _pair_sums(grads, wire_dtype, my_core, tag):
    views, got = _pair_exchange(grads, "pair_exchange_" + tag)
    return _chip_sum(views, got, my_core, wire_dtype, "chip_sum_" + tag)


def _adam_math(g, w, m, v):
    m_new = ADAM_B1 * m + (1.0 - ADAM_B1) * g
    v_new = ADAM_B2 * v + (1.0 - ADAM_B2) * (g * g)
    m_hat = m_new / (1.0 - ADAM_B1 ** ADAM_STEP)
    v_hat = v_new / (1.0 - ADAM_B2 ** ADAM_STEP)
    return -ADAM_LR * (m_hat / (jnp.sqrt(v_hat) + ADAM_EPS) + ADAM_WD * w), m_new, v_new


def _adamw(first, parts, w, m, v, name):
    n, rows, cols = parts.shape
    tr = _row_tile(rows, parts.dtype)

    def body(f_ref, p_ref, w_ref, m_ref, v_ref, g_out, d_out, m_out, v_out):
        g = f_ref[...].astype(f32)
        for s in range(n):
            g = g + p_ref[s].astype(f32)
        g_out[...] = g
        d_out[...], m_out[...], v_out[...] = _adam_math(g, w_ref[...], m_ref[...], v_ref[...])

    blk = pl.BlockSpec((tr, cols), lambda i: (i, 0))
    return _pcall(
        body, name=name, grid=(rows // tr,),
        in_specs=[blk, pl.BlockSpec((n, tr, cols), lambda i: (0, i, 0)), blk, blk, blk],
        out_specs=(blk, blk, blk, blk), out_shape=tuple(_sds((rows, cols), f32) for _ in range(4)),
        compiler_params=_params(("arbitrary",)),
    )(first, parts, w, m, v)


def _adamw_group(parts, ws, ms, vs, name):
    nw = len(ws)

    def body(*refs):
        p_refs, w_refs, m_refs, v_refs = (refs[k * nw:(k + 1) * nw] for k in range(4))
        outs = refs[4 * nw:]
        for k in range(nw):
            g = p_refs[k][0].astype(f32)
            for s in range(1, p_refs[k].shape[0]):
                g = g + p_refs[k][s].astype(f32)
            g_out, d_out, m_out, v_out = outs[4 * k:4 * k + 4]
            g_out[...] = g
            d_out[...], m_out[...], v_out[...] = _adam_math(g, w_refs[k][...], m_refs[k][...], v_refs[k][...])

    res = _pcall(
        body, name=name, out_shape=tuple(_sds(w.shape, f32) for w in ws for _ in range(4)),
        in_specs=[VMEM_SPEC] * (4 * nw), out_specs=tuple([VMEM_SPEC] * (4 * nw)), compiler_params=_params(),
    )(*parts, *ws, *ms, *vs)
    return [res[4 * k:4 * k + 4] for k in range(nw)]


def _rope(t, c, s1, s2):
    w = t.shape[1]
    return t * c + pltpu.roll(t, w - 8, 1) * s1 + pltpu.roll(t, 8, 1) * s2


def _rope_transposed(dt, c, s1, s2):
    w = dt.shape[1]
    return dt * c + pltpu.roll(dt * s1, 8, 1) + pltpu.roll(dt * s2, w - 8, 1)


PAIR_ROWS = D_IN // 4
SUB_COLS = ((0, 512), (512, 512), (1024, 512), (1536, 128))
Q_SLABS = range(3, 11)
K_SLABS = range(11, 13)


def _in_proj_gather(x2d, norm_g, wt_shard, cw_shard, tabs, S, out_shards, chip_order):
    T = x2d.shape[0]
    tb = min(S, 1024)
    ntok = T // tb
    nsb = S // tb
    q_scale = 1.0 / math.sqrt(HEAD)
    shard_rows = wt_shard.shape[0]
    small = (cw_shard,) + tuple(out_shards)
    nsm = len(small)

    def body(order_ref, x_ref, g_ref, c_ref, s1_ref, s2_ref, wt_hbm, *rest):
        small_in = rest[:nsm]
        h_ref, proj_ref, wt_out = rest[nsm:nsm + 3]
        small_out = rest[nsm + 3:2 * nsm + 3]
        wt_vm, h_vm = rest[2 * nsm + 3:2 * nsm + 5]
        stage = rest[2 * nsm + 5:3 * nsm + 4]
        wsend, wrecv, wlocal = rest[3 * nsm + 4:3 * nsm + 7]
        dsems = rest[3 * nsm + 7:]
        jj, i = pl.program_id(0), pl.program_id(1)
        x, y, c = _my_place()
        me, sibling = (x, y, c), (x, y, 1 - c)
        chips = [(1 - x, y), (x, 1 - y), (1 - x, 1 - y)]

        def rows(place):
            px, py, pc = place
            return wt_vm.at[pl.ds(pl.multiple_of((4 * px + 2 * py + pc) * shard_rows, 16), shard_rows), :]

        def copy(k, block, to, src=None):
            return pltpu.make_async_remote_copy(
                src_ref=rows(block) if src is None else src, dst_ref=rows(block),
                send_sem=wsend.at[k], recv_sem=wrecv.at[k], device_id=to, device_id_type=MESH)

        def small_copies():
            srcs = (small_in[0],) + tuple(stage)
            return _direct_gather_copies(srcs, small_out, *dsems)

        own = pltpu.make_async_copy(wt_hbm, rows(me), wlocal.at[0])
        keep = pltpu.make_async_copy(wt_vm, wt_out, wlocal.at[1])

        @pl.when((jj == 0) & (i == 0))
        def _():
            own.start()
            copy(0, me, sibling, src=wt_hbm).start()
            for j, chip in enumerate(chips):
                copy(1 + j, me, (*chip, c), src=wt_hbm).start()
            for a in range(nsm - 1):
                stage[a][...] = small_in[1 + a][...].astype(bf16)
            _start_all(small_copies())
            own.wait()
            copy(0, sibling, me).wait_recv()

        for j, chip in enumerate(chips):
            @pl.when((jj == 1 + j) & (i == 0))
            def _(j=j, chip=chip):
                copy(1 + j, (*chip, c), me).wait_recv()
                copy(4 + j, (*chip, c), sibling).start()
                copy(4 + j, (*chip, 1 - c), me).wait_recv()

        @pl.when((jj == 3) & (i == 0))
        def _():
            keep.start()

        @pl.when((jj == 3) & (i == ntok - 1))
        def _():
            copy(0, me, sibling, src=wt_hbm).wait_send()
            for j, chip in enumerate(chips):
                copy(1 + j, me, (*chip, c), src=wt_hbm).wait_send()
                copy(4 + j, (*chip, c), sibling).wait_send()
            _wait_all(small_copies())
            keep.wait()

        tok = pl.ds(pl.multiple_of(i * tb, tb), tb)

        @pl.when(jj == 0)
        def _():
            xv = x_ref[...]
            ms = jnp.mean(xv * xv, axis=-1, keepdims=True)
            hb = (xv * lax.rsqrt(ms + EPS) * g_ref[...]).astype(bf16)
            h_ref[...] = hb
            h_vm[tok, :] = hb

        block = order_ref[jj]
        hb = h_vm[tok, :]

        def piece(c0, w):
            w_rows = wt_vm[pl.ds(pl.multiple_of(block * PAIR_ROWS + c0, 128), w), :]
            return _dot(hb, w_rows, NT)

        @pl.when(block != 1)
        def _():
            for c0, w in SUB_COLS:
                proj_ref[:, c0:c0 + w] = piece(c0, w).astype(bf16)

        @pl.when(block == 1)
        def _():
            tab = (c_ref[...], s1_ref[...], s2_ref[...])
            for c0, w in SUB_COLS:
                acc = piece(c0, w)
                for l in range(w // 128):
                    slab = (c0 + 128 * l) // 128
                    part = acc[:, 128 * l:128 * (l + 1)]
                    if slab in Q_SLABS:
                        part = _rope(part, *tab) * q_scale
                    elif slab in K_SLABS:
                        part = _rope(part, *tab)
                    proj_ref[:, 128 * slab:128 * (slab + 1)] = part.astype(bf16)

    first_pass = lambda jj, i, order: (jnp.where(jj == 0, i, ntok - 1), 0)
    const = lambda jj, i, order: (0, 0)
    tab = pl.BlockSpec((tb, 128), lambda jj, i, order: (jnp.where(order[jj] == 1, i % nsb, 0), 0))
    grid_spec = pltpu.PrefetchScalarGridSpec(
        num_scalar_prefetch=1, grid=(4, ntok),
        in_specs=[pl.BlockSpec((tb, D), first_pass), pl.BlockSpec((1, D), const), tab, tab, tab, ANY]
        + [pl.BlockSpec(w.shape, const) for w in small],
        out_specs=(pl.BlockSpec((tb, D), first_pass),
                   pl.BlockSpec((tb, PAIR_ROWS), lambda jj, i, order: (i, order[jj])), ANY) + tuple([ANY] * nsm),
        scratch_shapes=[pltpu.VMEM((D_IN, D), bf16), pltpu.VMEM((T, D), bf16)]
        + [pltpu.VMEM(w.shape, bf16) for w in out_shards]
        + [pltpu.SemaphoreType.DMA((7,)), pltpu.SemaphoreType.DMA((7,)), pltpu.SemaphoreType.DMA((2,))]
        + _exchange_scratch(nsm, 7))
    res = _pcall(
        body, name="in_proj", grid_spec=grid_spec,
        out_shape=(_sds((T, D), bf16), _sds((T, D_IN), bf16), _sds((D_IN, D), bf16),
                   _sds((NDEV * cw_shard.shape[0], cw_shard.shape[1]), f32))
        + tuple(_sds((NDEV * w.shape[0], w.shape[1]), bf16) for w in out_shards),
        compiler_params=_params(("arbitrary", "arbitrary")),
    )(chip_order, x2d, norm_g, *tabs, wt_shard, *small)
    return res[0], res[1], res[2], res[3], res[4:]


def _rows_iota(shape):
    return lax.broadcasted_iota(jnp.int32, shape, 0)


def _shift_down(v, k):
    return jnp.where(_rows_iota(v.shape) >= k, pltpu.roll(v, k, 0), 0.0)


def _shift_up(v, k):
    n = v.shape[0]
    return jnp.where(_rows_iota(v.shape) < n - k, pltpu.roll(v, n - k, 0), 0.0)


def _linear_scan(a, b, a_s, b_s, edge_s, out_ref, reverse):
    n = a.shape[0]
    ng = n // 8
    a3, b3 = a.reshape(ng, 8, RB), b.reshape(ng, 8, RB)
    rid = lax.broadcasted_iota(jnp.int32, a3.shape, 1)
    for s in (1, 2, 4):
        keep, shift = (rid < 8 - s, 8 - s) if reverse else (rid >= s, s)
        b3 = jnp.where(keep, a3 * pltpu.roll(b3, shift, 1) + b3, b3)
        a3 = jnp.where(keep, a3 * pltpu.roll(a3, shift, 1), a3)
    a_s[...] = a3.reshape(n, RB)
    b_s[...] = b3.reshape(n, RB)
    edge = 0 if reverse else 7
    ea, eb = a_s[pl.ds(edge, ng, stride=8), :], b_s[pl.ds(edge, ng, stride=8), :]
    r = _rows_iota(ea.shape)
    s = 1
    while s < ng:
        keep, shift = (r < ng - s, ng - s) if reverse else (r >= s, s)
        eb = jnp.where(keep, ea * pltpu.roll(eb, shift, 0) + eb, eb)
        if 2 * s < ng:
            ea = jnp.where(keep, ea * pltpu.roll(ea, shift, 0), ea)
        s *= 2
    edge_s[...] = _shift_up(eb, 1) if reverse else _shift_down(eb, 1)

    def eight_groups(i, carry):
        for k in range(8):
            j = i * 8 + k
            rows = pl.ds(pl.multiple_of(j * 8, 8), 8)
            out_ref[rows, :] = b_s[rows, :] + a_s[rows, :] * edge_s[pl.ds(j, 1), :]
        return carry

    lax.fori_loop(0, ng // 8, eight_groups, 0)


def _neg_expm1(v):
    series = -v * (1.0 + v * (0.5 + v * (1.0 / 6.0)))
    return jnp.where(v > -0.015625, series, 1.0 - jnp.exp(v))


def _softplus_neg(lam):
    return jnp.maximum(-lam, 0.0) + jnp.log(1.0 + jnp.exp(-jnp.abs(lam)))


def _lru_gates(x0, cw, cb, wa, ba, wx, bx, lam):
    taps = [_shift_down(x0, 3 - k) for k in range(3)] + [x0]
    u = cb + cw[3:4, :] * x0
    for k in range(3):
        u = u + cw[k:k + 1, :] * taps[k]
    ub = u.astype(bf16)
    r = _sigmoid_positive(_dot(ub, wa.astype(bf16), NN) + ba)
    i = _sigmoid(_dot(ub, wx.astype(bf16), NN) + bx)
    sp = _softplus_neg(lam)
    log_a = (-LRU_C) * r * sp
    a = jnp.exp(log_a)
    w = _neg_expm1(2.0 * log_a)
    inv_mult = lax.rsqrt(w)
    return u, ub, r, i, sp, a, w * inv_mult, inv_mult, taps


def _lru_specs(S, nb):
    col = lambda off: pl.BlockSpec((S, RB), lambda n, b, off=off: (b, off + n))
    vec = pl.BlockSpec((1, RB), lambda n, b: (0, n))
    wblk = pl.BlockSpec((None, RB, RB), lambda n, b: (n, 0, 0))
    cwblk = pl.BlockSpec((8, RB), lambda n, b: (n, 0))
    return col, vec, wblk, cwblk


def _lru_forward(proj, cw_full, conv_b, w_a, b_a, w_x, b_x, lam, S):
    T = proj.shape[0]
    nb = T // S
    col, vec, wblk, cwblk = _lru_specs(S, nb)

    def body(x0_ref, g_ref, cw_ref, cb_ref, wa_ref, ba_ref, wx_ref, bx_ref, lam_ref, y_ref, h_ref, a_s, b_s, edge_s):
        x0 = x0_ref[...].astype(f32)
        u, ub, r, i, sp, a, mult, _, _ = _lru_gates(x0, cw_ref[...], cb_ref[...], wa_ref[...], ba_ref[...],
                                                    wx_ref[...], bx_ref[...], lam_ref[...])
        _linear_scan(a, mult * (i * u), a_s, b_s, edge_s, h_ref, reverse=False)
        g = g_ref[...].astype(f32)
        y_ref[...] = (h_ref[...] * (g * _sigmoid(g))).astype(bf16)

    out = pl.BlockSpec((S, RB), lambda n, b: (b, n))
    return _pcall(
        body, name="lru_forward", grid=(RNN_BLOCKS, nb),
        in_specs=[col(0), col(8), cwblk, vec, wblk, vec, wblk, vec, vec],
        out_specs=(out, out), out_shape=(_sds((T, D), bf16), _sds((T, D), f32)),
        scratch_shapes=[pltpu.VMEM((S, RB), f32), pltpu.VMEM((S, RB), f32), pltpu.VMEM((S // 8, RB), f32)],
        compiler_params=_params(("arbitrary", "arbitrary")),
    )(proj, proj, cw_full, conv_b, w_a, b_a, w_x, b_x, lam)


def _rope_tables(S):
    pos = jnp.arange(S, dtype=f32)
    inv_freq = ROPE_THETA ** (-jnp.arange(0, ROPE_DIM, 2, dtype=f32) / ROPE_DIM)
    ang = pos[:, None] * inv_freq[None, :]
    cos, sin = jnp.cos(ang), jnp.sin(ang)
    lane = jnp.arange(128) % HEAD
    cosl, sinl = cos[:, lane % 8], sin[:, lane % 8]
    c = jnp.where(lane[None, :] < ROPE_DIM, cosl, 1.0)
    s1 = jnp.where(lane[None, :] < 8, -sinl, 0.0)
    s2 = jnp.where((lane[None, :] >= 8) & (lane[None, :] < ROPE_DIM), sinl, 0.0)
    return c.astype(f32), s1.astype(f32), s2.astype(f32)


def _heads_to_rows(t):
    return jnp.concatenate([t[:, HEAD * h:HEAD * (h + 1)] for h in range(GROUP)], axis=0)


def _rows_to_heads(t):
    return jnp.concatenate([t[QB * h:QB * (h + 1), :] for h in range(GROUP)], axis=1)


def _window_bias(first_block):
    shape = (GROUP * QB, 2 * QB)
    qi = _rows_iota(shape) % QB
    cj = lax.broadcasted_iota(jnp.int32, shape, 1)
    valid = (cj > qi) & (cj <= qi + QB) & ((cj >= QB) | jnp.logical_not(first_block))
    return jnp.where(valid, 0.0, -jnp.inf)


def _attn_probs(q_rows, k_cat, sink_col, bias):
    s = _dot(q_rows, k_cat, NT) + bias
    m = jnp.maximum(jnp.max(s, axis=1, keepdims=True), sink_col)
    p = jnp.exp(s - m)
    e_sink = jnp.exp(sink_col - m)
    inv = 1.0 / (jnp.sum(p, axis=1, keepdims=True) + e_sink)
    return p * inv, e_sink * inv


def _sink_column(sink_ref, kv):
    rid = _rows_iota((GROUP * QB, 1))
    col = jnp.zeros((GROUP * QB, 1), f32)
    for h in range(GROUP):
        col = jnp.where(rid // QB == h, sink_ref[0, GROUP * kv + h], col)
    return col


def _attn_in_specs(S):
    nq = S // QB
    last = nq - 1
    cur = lambda b, j: b * nq + jnp.minimum(j, last)
    prev = lambda b, j: b * nq + jnp.maximum(jnp.minimum(j, last) - 1, 0)
    specs = [
        pl.BlockSpec((QB, D), lambda b, j: (cur(b, j), 2)),
        pl.BlockSpec((QB, 256), lambda b, j: (cur(b, j), 12)),
        pl.BlockSpec((QB, 256), lambda b, j: (prev(b, j), 12)),
        pl.BlockSpec((QB, 256), lambda b, j: (cur(b, j), 13)),
        pl.BlockSpec((QB, 256), lambda b, j: (prev(b, j), 13)),
        pl.BlockSpec((QB, 512), lambda b, j: (cur(b, j), 7)),
        pl.BlockSpec((QB, 512), lambda b, j: (cur(b, j), 8)),
        SMEM_SPEC,
    ]
    return specs, cur, prev


def _attn_forward(proj, sinks, S, out_shards):
    T = proj.shape[0]
    nb, nq = T // S, S // QB
    specs, cur, _ = _attn_in_specs(S)
    nw = len(out_shards)

    def body(q_ref, kc_ref, kp_ref, vc_ref, vp_ref, gl_ref, gh_ref, sink_ref, *rest):
        shards = rest[:nw]
        y_ref = rest[nw]
        gathered = rest[nw + 1:2 * nw + 1]
        stage = rest[2 * nw + 1:3 * nw + 1]
        sems = rest[3 * nw + 1:]
        b, j = pl.program_id(0), pl.program_id(1)

        @pl.when((b == 0) & (j == 0))
        def _():
            for a in range(nw):
                stage[a][...] = shards[a][...].astype(bf16)
            _start_all(_direct_gather_copies(stage, gathered, *sems))

        @pl.when((b == nb - 1) & (j == nq - 1))
        def _():
            _wait_all(_direct_gather_copies(stage, gathered, *sems))

        bias = _window_bias(j == 0)
        kc, kp, vc, vp = kc_ref[...], kp_ref[...], vc_ref[...], vp_ref[...]
        for kv in range(KV_HEADS):
            lanes = slice(256 * kv, 256 * (kv + 1))
            hl = slice(HEAD * kv, HEAD * (kv + 1))
            q_rows = _heads_to_rows(q_ref[:, lanes])
            k_cat = jnp.concatenate([kp[:, hl], kc[:, hl]], axis=0)
            v_cat = jnp.concatenate([vp[:, hl], vc[:, hl]], axis=0)
            probs, _ = _attn_probs(q_rows, k_cat, _sink_column(sink_ref, kv), bias)
            o = _rows_to_heads(_dot(probs.astype(bf16), v_cat, NN))
            g_src = gl_ref if kv < 2 else gh_ref
            g = g_src[:, 256 * (kv % 2):256 * (kv % 2 + 1)].astype(f32)
            y_ref[:, lanes] = (o * (g * _sigmoid(g))).astype(bf16)

    args = [proj] * 7 + [sinks] + list(out_shards)
    res = _pcall(
        body, name="attn_forward", grid=(nb, nq),
        in_specs=specs + [pl.BlockSpec(w.shape, lambda b, j: (0, 0)) for w in out_shards],
        out_specs=(pl.BlockSpec((QB, D), lambda b, j: (cur(b, j), 0)),) + tuple([ANY] * nw),
        out_shape=(_sds((T, D), bf16),) + tuple(_sds((NDEV * w.shape[0], w.shape[1]), bf16) for w in out_shards),
        scratch_shapes=[pltpu.VMEM(w.shape, bf16) for w in out_shards] + _exchange_scratch(nw, 7),
        compiler_params=_params(("arbitrary", "arbitrary")),
    )(*args)
    return res[0], res[1:]


def _merge_and_head(x2d, tgt, proj, y_rnn, y_attn, w_r, w_a, w_o, gfin):
    T = x2d.shape[0]
    tb = min(T, 512)
    nsteps = T // tb

    def body(x_ref, t_ref, mr0, mr1, ma0, ma1, yr_ref, ya_ref, wr_ref, wa_ref, wo_ref, gf_ref,
             dx2_ref, dyr_ref, dya_ref, dmr_ref, dma_ref, loss_ref, gfin_ref, gwr_out, gwa_out, gwo_out,
             gwr_acc, gwa_acc, gwo_acc, out_sems):
        step = pl.program_id(0)

        @pl.when(step == 0)
        def _():
            loss_ref[...] = jnp.zeros_like(loss_ref)
            gfin_ref[...] = jnp.zeros_like(gfin_ref)
            gwr_acc[...] = jnp.zeros_like(gwr_acc)
            gwa_acc[...] = jnp.zeros_like(gwa_acc)
            gwo_acc[...] = jnp.zeros_like(gwo_acc)

        sr = _sigmoid(jnp.concatenate([mr0[...], mr1[...]], axis=1).astype(f32))
        sa = _sigmoid(jnp.concatenate([ma0[...], ma1[...]], axis=1).astype(f32))
        p_r = _dot(yr_ref[...], wr_ref[...], NN)
        p_a = _dot(ya_ref[...], wa_ref[...], NN)
        merged = (sr * p_r + sa * p_a).astype(bf16)
        x2 = x_ref[...] + _dot(merged, wo_ref[...], NN)
        rstd = lax.rsqrt(jnp.mean(x2 * x2, axis=-1, keepdims=True) + EPS)
        xh = x2 * rstd
        gf = gf_ref[...]
        err = xh * gf - t_ref[...]
        loss_ref[...] += jnp.sum(err * err)
        dy = err * (1.0 / D)
        gfin_ref[0:1, :] += jnp.sum(dy * xh, axis=0, keepdims=True)
        dxn = dy * gf
        dx2 = rstd * (dxn - xh * jnp.mean(dxn * xh, axis=-1, keepdims=True))
        dx2_ref[...] = dx2
        dx2b = dx2.astype(bf16)
        dmerged = _dot(dx2b, wo_ref[...], NT)
        dmr_ref[...] = (dmerged * p_r * (sr * (1.0 - sr))).astype(bf16)
        dma_ref[...] = (dmerged * p_a * (sa * (1.0 - sa))).astype(bf16)
        dpr = (dmerged * sr).astype(bf16)
        dpa = (dmerged * sa).astype(bf16)
        dyr_ref[...] = _dot(dpr, wr_ref[...], NT).astype(bf16)
        dya_ref[...] = _dot(dpa, wa_ref[...], NT).astype(bf16)
        gwr_acc[...] += _dot(yr_ref[...], dpr, TN)
        gwa_acc[...] += _dot(ya_ref[...], dpa, TN)
        gwo_acc[...] += _dot(merged, dx2b, TN)

        @pl.when(step == nsteps - 1)
        def _():
            copies = [pltpu.make_async_copy(src, dst, out_sems.at[k]) for k, (src, dst) in enumerate(
                ((gwr_acc, gwr_out), (gwa_acc, gwa_out), (gwo_acc, gwo_out)))]
            for cp in copies:
                cp.start()
            for cp in copies:
                cp.wait()

    tok = pl.BlockSpec((tb, D), lambda i: (i, 0))
    half = lambda c: pl.BlockSpec((tb, CH), lambda i, c=c: (i, c))
    wfull = pl.BlockSpec((D, D), lambda i: (0, 0), pipeline_mode=pl.Buffered(1))
    acc = pl.BlockSpec((8, D), lambda i: (0, 0))
    return _pcall(
        body, name="merge_and_head", grid=(nsteps,),
        in_specs=[tok, tok, half(9), half(10), half(11), half(12), tok, tok, wfull, wfull, wfull,
                  pl.BlockSpec((1, D), lambda i: (0, 0))],
        out_specs=(tok, tok, tok, tok, tok, acc, acc, ANY, ANY, ANY),
        out_shape=(_sds((T, D), f32), _sds((T, D), bf16), _sds((T, D), bf16), _sds((T, D), bf16),
                   _sds((T, D), bf16), _sds((8, D), f32), _sds((8, D), f32),
                   _sds((D, D), f32), _sds((D, D), f32), _sds((D, D), f32)),
        scratch_shapes=[pltpu.VMEM((D, D), f32)] * 3 + [pltpu.SemaphoreType.DMA((3,))],
        compiler_params=_params(("arbitrary",)),
    )(x2d, tgt, proj, proj, proj, proj, y_rnn, y_attn, w_r, w_a, w_o, gfin)


def _attn_backward(proj, dy_attn, tabs, sinks, S, chip_sums):
    T = proj.shape[0]
    nb, nq = T // S, S // QB
    nex = len(chip_sums)
    specs, cur, prev = _attn_in_specs(S)
    last = nq - 1
    tab_cur = pl.BlockSpec((QB, 128), lambda b, j: (jnp.minimum(j, last), 0))
    tab_prev = pl.BlockSpec((QB, 128), lambda b, j: (jnp.maximum(jnp.minimum(j, last) - 1, 0), 0))
    specs = specs + [pl.BlockSpec((QB, D), lambda b, j: (cur(b, j), 0))] + [tab_cur] * 3 + [tab_prev] * 3
    q_scale = 1.0 / math.sqrt(HEAD)

    def rope_back(dt, tab):
        return jnp.concatenate([_rope_transposed(dt[:, 128 * l:128 * (l + 1)], *tab) for l in range(2)], axis=1)

    def body(q_ref, kc_ref, kp_ref, vc_ref, vp_ref, gl_ref, gh_ref, sink_ref, dy_ref, cc, s1c, s2c, cp, s1p, s2p,
             *rest):
        ex_src = rest[:nex]
        dq_ref, dkv_ref, dg_ref, dsink_ref = rest[nex:nex + 4]
        ex_dst = rest[nex + 4:2 * nex + 4]
        carry_k, carry_v = rest[2 * nex + 4:2 * nex + 6]
        sems = rest[2 * nex + 6:]
        b, j = pl.program_id(0), pl.program_id(1)

        @pl.when((b == 0) & (j == 0))
        def _():
            dsink_ref[...] = jnp.zeros_like(dsink_ref)
            _start_all(_chip_exchange_copies(ex_src, ex_dst, *sems))

        @pl.when((b == nb - 1) & (j == nq))
        def _():
            _wait_all(_chip_exchange_copies(ex_src, ex_dst, *sems))

        @pl.when(j == 0)
        def _():
            carry_k[...] = jnp.zeros_like(carry_k)
            carry_v[...] = jnp.zeros_like(carry_v)

        @pl.when(j < nq)
        def _():
            bias = _window_bias(j == 0)
            tc = (cc[...], s1c[...], s2c[...])
            tp = (cp[...], s1p[...], s2p[...])
            kc, kp, vc, vp = kc_ref[...], kp_ref[...], vc_ref[...], vp_ref[...]
            dk_prev, dk_cur, dv_prev, dv_cur = [], [], [], []
            dsink_acc = jnp.zeros((8, 128), f32)
            r8 = lax.broadcasted_iota(jnp.int32, (8, 128), 0)
            l8 = lax.broadcasted_iota(jnp.int32, (8, 128), 1)
            for kv in range(KV_HEADS):
                lanes = slice(256 * kv, 256 * (kv + 1))
                hl = slice(HEAD * kv, HEAD * (kv + 1))
                q_rows = _heads_to_rows(q_ref[:, lanes])
                k_cat = jnp.concatenate([kp[:, hl], kc[:, hl]], axis=0)
                v_cat = jnp.concatenate([vp[:, hl], vc[:, hl]], axis=0)
                probs, p_sink = _attn_probs(q_rows, k_cat, _sink_column(sink_ref, kv), bias)
                pb = probs.astype(bf16)
                o = _rows_to_heads(_dot(pb, v_cat, NN))
                g_src = gl_ref if kv < 2 else gh_ref
                g = g_src[:, 256 * (kv % 2):256 * (kv % 2 + 1)].astype(f32)
                sg = _sigmoid(g)
                dy = dy_ref[:, lanes].astype(f32)
                dg_ref[:, lanes] = (dy * o * (sg * (1.0 + g * (1.0 - sg)))).astype(bf16)
                do_rows = _heads_to_rows(dy * (g * sg)).astype(bf16)
                dv = _dot(pb, do_rows, TN)
                dp = _dot(do_rows, v_cat, NT)
                rowdot = jnp.sum(probs * dp, axis=1, keepdims=True)
                ds = (probs * (dp - rowdot)).astype(bf16)
                sink_rows = -(p_sink * rowdot)
                for h in range(GROUP):
                    val = jnp.sum(sink_rows[QB * h:QB * (h + 1), :])
                    dsink_acc = dsink_acc + jnp.where((r8 == 0) & (l8 == GROUP * kv + h), val, 0.0)
                dq = _rows_to_heads(_dot(ds, k_cat, NN)) * q_scale
                dq_ref[:, lanes] = rope_back(dq, tc).astype(bf16)
                dk = _dot(ds, q_rows, TN)
                dk_prev.append(dk[:QB, :])
                dk_cur.append(dk[QB:, :])
                dv_prev.append(dv[:QB, :])
                dv_cur.append(dv[QB:, :])
            dsink_ref[...] += dsink_acc
            dkp = rope_back(jnp.concatenate(dk_prev, axis=1), tp)
            dkc = rope_back(jnp.concatenate(dk_cur, axis=1), tc)
            dkv_ref[:, 0:256] = (carry_k[...] + dkp).astype(bf16)
            dkv_ref[:, 256:512] = (carry_v[...] + jnp.concatenate(dv_prev, axis=1)).astype(bf16)
            carry_k[...] = dkc
            carry_v[...] = jnp.concatenate(dv_cur, axis=1)

        @pl.when(j == nq)
        def _():
            dkv_ref[:, 0:256] = carry_k[...].astype(bf16)
            dkv_ref[:, 256:512] = carry_v[...].astype(bf16)

    lag = lambda b, j: (b * nq + jnp.maximum(j - 1, 0), 0)
    args = [proj] * 7 + [sinks, dy_attn] + list(tabs) + list(tabs) + list(chip_sums)
    res = _pcall(
        body, name="attn_backward", grid=(nb, nq + 1), in_specs=specs + [ANY] * nex,
        out_specs=(pl.BlockSpec((QB, D), lambda b, j: (cur(b, j), 0)), pl.BlockSpec((QB, 512), lag),
                   pl.BlockSpec((QB, D), lambda b, j: (cur(b, j), 0)), pl.BlockSpec((8, 128), lambda b, j: (0, 0)))
        + tuple([ANY] * nex),
        out_shape=(_sds((T, D), bf16), _sds((T, 512), bf16), _sds((T, D), bf16), _sds((8, 128), f32))
        + tuple(_sds(s.shape, s.dtype) for s in chip_sums),
        scratch_shapes=[pltpu.VMEM((QB, 256), f32), pltpu.VMEM((QB, 256), f32)] + _exchange_scratch(nex, 3),
        compiler_params=_params(("arbitrary", "arbitrary")),
    )(*args)
    return res[:4], res[4:]


def _lru_backward(proj, h_all, dy_rnn, cw_full, conv_b, w_a, b_a, w_x, b_x, lam, S):
    T = proj.shape[0]
    nb = T // S
    col, vec, wblk, cwblk = _lru_specs(S, nb)
    tokblk = pl.BlockSpec((S, RB), lambda n, b: (b, n))

    def body(x0_ref, g_ref, h_ref, dy_ref, cw_ref, cb_ref, wa_ref, ba_ref, wx_ref, bx_ref, lam_ref,
             du0_ref, dg_ref, gwa_ref, gwx_ref, vec_ref, gcw_ref, a_s, b_s, dh_s, edge_s):
        @pl.when(pl.program_id(1) == 0)
        def _():
            gwa_ref[...] = jnp.zeros_like(gwa_ref)
            gwx_ref[...] = jnp.zeros_like(gwx_ref)
            vec_ref[...] = jnp.zeros_like(vec_ref)
            gcw_ref[...] = jnp.zeros_like(gcw_ref)

        x0 = x0_ref[...].astype(f32)
        cw = cw_ref[...]
        lam_v = lam_ref[...]
        u, ub, r, i, sp, a, mult, inv_mult, taps = _lru_gates(x0, cw, cb_ref[...], wa_ref[...], ba_ref[...],
                                                              wx_ref[...], bx_ref[...], lam_v)
        h = h_ref[...]
        g = g_ref[...].astype(f32)
        dy = dy_ref[...].astype(f32)
        sg = _sigmoid(g)
        dg_ref[...] = (dy * h * (sg * (1.0 + g * (1.0 - sg)))).astype(bf16)
        _linear_scan(_shift_up(a, 1), dy * (g * sg), a_s, b_s, edge_s, dh_s, reverse=True)
        dh_total = dh_s[...]
        da = dh_total * _shift_down(h, 1)
        dmult = dh_total * (i * u)
        db = dh_total * mult
        di = db * u
        du = db * i
        dlog_a_c = ((-LRU_C) * a) * (da - dmult * (a * inv_mult))
        dr = dlog_a_c * sp
        dsp = jnp.sum(dlog_a_c * r, axis=0, keepdims=True)
        dpre_r = dr * r * (1.0 - r)
        dpre_i = di * i * (1.0 - i)
        dpre_rb = dpre_r.astype(bf16)
        dpre_ib = dpre_i.astype(bf16)
        du = du + _dot(dpre_rb, wa_ref[...].astype(bf16), NT) + _dot(dpre_ib, wx_ref[...].astype(bf16), NT)
        gwa_ref[...] += _dot(ub, dpre_rb, TN)
        gwx_ref[...] += _dot(ub, dpre_ib, TN)
        vec_ref[0:1, :] += jnp.sum(du, axis=0, keepdims=True)
        vec_ref[1:2, :] += jnp.sum(dpre_r, axis=0, keepdims=True)
        vec_ref[2:3, :] += jnp.sum(dpre_i, axis=0, keepdims=True)
        vec_ref[3:4, :] += dsp * (-_sigmoid(-lam_v))
        dx0 = cw[3:4, :] * du
        for k in range(3):
            dx0 = dx0 + cw[k:k + 1, :] * _shift_up(du, 3 - k)
        for k in range(4):
            gcw_ref[k:k + 1, :] += jnp.sum(du * taps[k], axis=0, keepdims=True)
        du0_ref[...] = dx0.astype(bf16)

    wacc = pl.BlockSpec((RB, RB), lambda n, b: (0, n))
    vacc = pl.BlockSpec((8, RB), lambda n, b: (0, n))
    cacc = pl.BlockSpec((8, RB), lambda n, b: (n, 0))
    return _pcall(
        body, name="lru_backward", grid=(RNN_BLOCKS, nb),
        in_specs=[col(0), col(8), tokblk, tokblk, cwblk, vec, wblk, vec, wblk, vec, vec],
        out_specs=(tokblk, tokblk, wacc, wacc, vacc, cacc),
        out_shape=(_sds((T, D), bf16), _sds((T, D), bf16), _sds((RB, D), f32), _sds((RB, D), f32),
                   _sds((8, D), f32), _sds((8 * RNN_BLOCKS, RB), f32)),
        scratch_shapes=[pltpu.VMEM((S, RB), f32)] * 3 + [pltpu.VMEM((S // 8, RB), f32)],
        compiler_params=_params(("arbitrary", "arbitrary")),
    )(proj, proj, h_all, dy_rnn, cw_full, conv_b, w_a, b_a, w_x, b_x, lam)


def _section_of_chunk(s):
    out = []
    for start, n in zip(SEC_START, SEC_CHUNKS):
        inside = (s >= start) & (s < start + n)
        out.append((inside, jnp.clip(s - start, 0, n - 1)))
    return out


EFFECT = pltpu.SideEffectType.DATAFLOW_SIDE_EFFECTING
HBM_SPEC = pl.BlockSpec(memory_space=pltpu.HBM)
SEM_SPEC = pl.BlockSpec(memory_space=pltpu.SEMAPHORE)


def _split_exchange_copies(src_ref, land_ref, send_sems, recv_sems):
    x, y, c = _my_place()
    copies = []
    for k in (3, 1, 2):
        px, py = (x + (k >> 1)) % 2, (y + (k & 1)) % 2
        copies.append(pltpu.make_async_remote_copy(
            src_ref=src_ref.at[2 * px + py], dst_ref=land_ref.at[k - 1], send_sem=send_sems[k - 1],
            recv_sem=recv_sems[k - 1], device_id=(px, py, c), device_id_type=MESH))
    return copies


def _exchange_start(chip_sum):
    _, r, cols = chip_sum.shape

    def body(src_ref, land_ref, s0, s1, s2, r0, r1, r2, src_thru, land_thru, token):
        for cp in _split_exchange_copies(src_ref, land_ref, (s0, s1, s2), (r0, r1, r2)):
            cp.start()
        token[...] = jnp.zeros_like(token)

    land = pltpu.with_memory_space_constraint(lax.empty((3, r, cols), chip_sum.dtype), pltpu.HBM)
    res = _pcall(
        body, name="exchange_start",
        out_shape=tuple([pltpu.SemaphoreType.DMA(())] * 6) + (
            pltpu.HBM(chip_sum.shape, chip_sum.dtype), pltpu.HBM((3, r, cols), chip_sum.dtype), _sds((8, 128), f32)),
        in_specs=(HBM_SPEC, HBM_SPEC), out_specs=tuple([SEM_SPEC] * 6) + (HBM_SPEC, HBM_SPEC, VMEM_SPEC),
        input_output_aliases={0: 6, 1: 7},
        compiler_params=pltpu.CompilerParams(has_side_effects=EFFECT),
    )(pltpu.with_memory_space_constraint(chip_sum, pltpu.HBM), land)
    return res[:6], res[6], res[7], res[8]


def _exchange_wait(sems, src_thru, land_thru, after):
    def body(src_ref, land_ref, s0, s1, s2, r0, r1, r2, after_ref, src_dead, got_ref):
        for cp in _split_exchange_copies(src_ref, land_ref, (s0, s1, s2), (r0, r1, r2)):
            cp.wait_send()
            cp.wait_recv()

    return _pcall(
        body, name="exchange_wait",
        out_shape=(pltpu.HBM(src_thru.shape, src_thru.dtype), pltpu.HBM(land_thru.shape, land_thru.dtype)),
        in_specs=(HBM_SPEC, HBM_SPEC) + tuple([SEM_SPEC] * 6) + (ANY,), out_specs=(HBM_SPEC, HBM_SPEC),
        input_output_aliases={0: 0, 1: 1},
        compiler_params=pltpu.CompilerParams(has_side_effects=EFFECT),
    )(src_thru, land_thru, *sems, after)[1]


def _input_grad(dsecs, wt_full, x2d, dx2, norm_g):
    T = x2d.shape[0]
    tb = min(T, 512)
    nsec = len(dsecs)
    ntok = T // tb

    def body(*refs):
        secs = refs[:nsec]
        wt_ref, x_ref, dx2_ref, g_ref, dx_ref, gnorm_ref = refs[nsec:]
        i = pl.program_id(0)

        @pl.when(i == 0)
        def _():
            gnorm_ref[...] = jnp.zeros_like(gnorm_ref)

        dh = None
        for a, (start, n) in enumerate(zip(SEC_START, SEC_CHUNKS)):
            part = _dot(secs[a][...], wt_ref[CH * start:CH * (start + n), :], NN)
            dh = part if dh is None else dh + part
        xv = x_ref[...]
        rstd = lax.rsqrt(jnp.mean(xv * xv, axis=-1, keepdims=True) + EPS)
        xh = xv * rstd
        gnorm_ref[0:1, :] += jnp.sum(dh * xh, axis=0, keepdims=True)
        dxn = dh * g_ref[...]
        dx_ref[...] = dx2_ref[...] + rstd * (dxn - xh * jnp.mean(dxn * xh, axis=-1, keepdims=True))

    tok = pl.BlockSpec((tb, D), lambda i: (i, 0))
    return _pcall(
        body, name="input_grad", grid=(ntok,),
        in_specs=[pl.BlockSpec((tb, sec.shape[1]), lambda i: (i, 0)) for sec in dsecs]
        + [pl.BlockSpec((D_IN, D), lambda i: (0, 0), pipeline_mode=pl.Buffered(1)), tok, tok,
           pl.BlockSpec((1, D), lambda i: (0, 0))],
        out_specs=(tok, pl.BlockSpec((8, D), lambda i: (0, 0))),
        out_shape=(_sds((T, D), f32), _sds((8, D), f32)),
        compiler_params=_params(("arbitrary",)),
    )(*dsecs, wt_full, x2d, dx2, norm_g)


def _w_in_grad(dsecs, h_bf):
    T = h_bf.shape[0]
    tk = min(T, 2048)
    nchunks = D_IN // CH
    nsec = len(dsecs)
    nt = T // tk

    def body(*refs):
        secs = refs[:nsec]
        h_ref, out_ref, acc = refs[nsec:]
        s, t = pl.program_id(0), pl.program_id(1)

        @pl.when(t == 0)
        def _():
            acc[...] = jnp.zeros_like(acc)

        h_rows = h_ref[pl.ds(pl.multiple_of(t * tk, tk), tk), :]
        for a, (start, n) in enumerate(zip(SEC_START, SEC_CHUNKS)):
            @pl.when((s >= start) & (s < start + n))
            def _(a=a):
                acc[...] += _dot(secs[a][...], h_rows, TN)

        @pl.when(t == nt - 1)
        def _():
            out_ref[...] = acc[...].astype(bf16)

    def sec_spec(a):
        def index(s, t, a=a):
            inside, local = _section_of_chunk(s)[a]
            return (jnp.where(inside, t, 0), local)
        return pl.BlockSpec((tk, CH), index)

    return _pcall(
        body, name="w_in_grad", grid=(nchunks, T // tk),
        in_specs=[sec_spec(a) for a in range(nsec)]
        + [pl.BlockSpec((T, D), lambda s, t: (0, 0), pipeline_mode=pl.Buffered(1))],
        out_specs=pl.BlockSpec((CH, D), lambda s, t: (s, 0)), out_shape=_sds((D_IN, D), bf16),
        scratch_shapes=[pltpu.VMEM((CH, D), f32)],
        compiler_params=_params(("arbitrary", "arbitrary")),
    )(*dsecs, h_bf)


SMALL_NAMES = ("lru_w_a", "lru_w_x", "conv_b", "lru_b_a", "lru_b_x", "lru_lambda", "norm_g", "final_norm_g",
               "attn_sinks", "conv_w")
MISC_ROW = {"conv_b": 0, "lru_b_a": 1, "lru_b_x": 2, "lru_lambda": 3, "norm_g": 8, "final_norm_g": 16,
            "attn_sinks": 24, "loss": 32}


def _small_step(gwa, gwx, gvec, gnorm_blk, gfin_blk, dsink_blk, loss_blk, gcw, params):
    srcs_rows = (RB // NDEV, RB // NDEV, 8, 8)
    flat = [t for n in SMALL_NAMES for t in params[n]]
    nin = 8 + len(flat)
    nout = 4 * len(SMALL_NAMES) + 1

    def body(*refs):
        gwa_ref, gwx_ref, gvec_ref, gnorm_ref, gfin_ref, dsink_ref, loss_ref, gcw_ref = refs[:8]
        prm = {n: refs[8 + 3 * k:11 + 3 * k] for k, n in enumerate(SMALL_NAMES)}
        outs = {n: refs[nin + 4 * k:nin + 4 * k + 4] for k, n in enumerate(SMALL_NAMES)}
        loss_out = refs[nin + nout - 1]
        (misc, got_a, got_x, got_m, got_c, red_a, red_x, red_m, all_a, all_x, all_m,
         sa, ra, sb, rb) = refs[nin + nout:]
        x, y, c = _my_place()
        me = 4 * x + 2 * y + c

        misc[...] = jnp.zeros_like(misc)
        misc[0:8, :] = gvec_ref[...]
        misc[8:16, :] = gnorm_ref[...]
        misc[16:24, :] = gfin_ref[...]
        misc[24:32, 0:128] = dsink_ref[...]
        misc[32:40, :] = loss_ref[...]

        srcs = (gwa_ref, gwx_ref, misc, gcw_ref)
        gots = (got_a, got_x, got_m, got_c)

        def shard(ref, rows, dev):
            return ref.at[pl.ds(pl.multiple_of(dev * rows, 8), rows), :]

        scatter = []
        for k in range(1, NDEV):
            px, py, pc = _peer(k)
            for a in range(4):
                scatter.append(pltpu.make_async_remote_copy(
                    src_ref=shard(srcs[a], srcs_rows[a], 4 * px + 2 * py + pc), dst_ref=gots[a].at[k - 1],
                    send_sem=sa.at[4 * (k - 1) + a], recv_sem=ra.at[4 * (k - 1) + a],
                    device_id=(px, py, pc), device_id_type=MESH))
        for cp in scatter:
            cp.start()
        for cp in scatter:
            cp.wait()

        def reduced(a):
            rows = srcs_rows[a]
            total = srcs[a][pl.ds(pl.multiple_of(me * rows, 8), rows), :]
            for k in range(NDEV - 1):
                total = total + gots[a][k]
            return total

        reds = (red_a, red_x, red_m)
        alls = (all_a, all_x, all_m)
        for a in range(3):
            val = reduced(a)
            reds[a][...] = val
            alls[a][pl.ds(pl.multiple_of(me * srcs_rows[a], 8), srcs_rows[a]), :] = val
        gather = []
        for k in range(1, NDEV):
            peer = _peer(k)
            for a in range(3):
                gather.append(pltpu.make_async_remote_copy(
                    src_ref=reds[a], dst_ref=shard(alls[a], srcs_rows[a], me),
                    send_sem=sb.at[3 * (k - 1) + a], recv_sem=rb.at[3 * (k - 1) + a],
                    device_id=peer, device_id_type=MESH))
        for cp in gather:
            cp.start()
        g_conv = reduced(3)[0:4, :]
        for cp in gather:
            cp.wait()

        def update(name, g, pick=lambda r: r[...]):
            w_ref, m_ref, v_ref = prm[name]
            delta, m_new, v_new = _adam_math(g, pick(w_ref), pick(m_ref), pick(v_ref))
            return g, delta, m_new, v_new

        for n in range(RNN_BLOCKS):
            lanes = slice(RB * n, RB * (n + 1))
            for name, full in (("lru_w_a", all_a), ("lru_w_x", all_x)):
                for out, val in zip(outs[name], update(name, full[:, lanes], pick=lambda r, n=n: r[n])):
                    out[n] = val
        for name in ("conv_b", "lru_b_a", "lru_b_x", "lru_lambda", "norm_g", "final_norm_g"):
            row = MISC_ROW[name]
            for out, val in zip(outs[name], update(name, all_m[row:row + 1, :])):
                out[...] = val
        row = MISC_ROW["attn_sinks"]
        for out, val in zip(outs["attn_sinks"], update("attn_sinks", all_m[row:row + 1, 0:16])):
            out[...] = val
        for out, val in zip(outs["conv_w"], update("conv_w", g_conv)):
            out[...] = val
        row = MISC_ROW["loss"]
        loss_out[...] = all_m[row:row + 8, 0:128] * (0.5 / D)

    out_shape = tuple(_sds(params[n][0].shape, f32) for n in SMALL_NAMES for _ in range(4)) + (_sds((8, 128), f32),)
    scratch = [pltpu.VMEM((64, D), f32),
               pltpu.VMEM((NDEV - 1, RB // NDEV, D), f32), pltpu.VMEM((NDEV - 1, RB // NDEV, D), f32),
               pltpu.VMEM((NDEV - 1, 8, D), f32), pltpu.VMEM((NDEV - 1, 8, RB), f32),
               pltpu.VMEM((RB // NDEV, D), f32), pltpu.VMEM((RB // NDEV, D), f32), pltpu.VMEM((8, D), f32),
               pltpu.VMEM((RB, D), f32), pltpu.VMEM((RB, D), f32), pltpu.VMEM((64, D), f32),
               pltpu.SemaphoreType.DMA((4 * (NDEV - 1),)), pltpu.SemaphoreType.DMA((4 * (NDEV - 1),)),
               pltpu.SemaphoreType.DMA((3 * (NDEV - 1),)), pltpu.SemaphoreType.DMA((3 * (NDEV - 1),))]
    res = _pcall(
        body, name="small_step", out_shape=out_shape,
        in_specs=[VMEM_SPEC] * nin, out_specs=tuple([VMEM_SPEC] * nout),
        scratch_shapes=scratch, compiler_params=_params(),
    )(gwa, gwx, gvec, gnorm_blk, gfin_blk, dsink_blk, loss_blk, gcw, *flat)
    return {n: res[4 * k:4 * k + 4] for k, n in enumerate(SMALL_NAMES)}, res[-1]


def _pad_rows(v, rows=8):
    return jnp.concatenate([v, jnp.zeros((rows - v.shape[0], v.shape[1]), v.dtype)], axis=0)


def kernel(x, norm_g, w_in, conv_w, conv_b, lru_w_a, lru_b_a, lru_w_x, lru_b_x, lru_lambda, attn_sinks, w_rnn_out, w_attn_out, w_o, final_norm_g, loss_target, m_norm_g, m_w_in, m_conv_w, m_conv_b, m_lru_w_a, m_lru_b_a, m_lru_w_x, m_lru_b_x, m_lru_lambda, m_attn_sinks, m_w_rnn_out, m_w_attn_out, m_w_o, m_final_norm_g, v_norm_g, v_w_in, v_conv_w, v_conv_b, v_lru_w_a, v_lru_b_a, v_lru_w_x, v_lru_b_x, v_lru_lambda, v_attn_sinks, v_w_rnn_out, v_w_attn_out, v_w_o, v_final_norm_g):
    nb, S, _ = x.shape
    T = nb * S
    x2d = x.reshape(T, D)
    tgt = loss_target.reshape(T, D)
    fin_g = final_norm_g.reshape(1, D)
    w_a3, w_x3 = lru_w_a[0], lru_w_x[0]

    my_core = lax.axis_index("c").astype(jnp.int32).reshape(1)
    cx, cy = lax.axis_index("x"), lax.axis_index("y")
    chip_order = jnp.stack([2 * cx + cy, 2 * (1 - cx) + cy, 2 * cx + (1 - cy),
                            2 * (1 - cx) + (1 - cy)]).astype(jnp.int32)

    tabs = _rope_tables(S)
    h_bf, proj, wt_full, cw_full, _ = _in_proj_gather(
        x2d, norm_g, w_in[0].T.astype(bf16), _pad_rows(conv_w[0]), tabs, S, (), chip_order)
    y_rnn, h_all = _lru_forward(proj, cw_full, conv_b, w_a3, lru_b_a, w_x3, lru_b_x, lru_lambda, S)
    y_attn, (wr_full, wa_full, wo_full) = _attn_forward(proj, attn_sinks, S,
                                                        (w_rnn_out[0], w_attn_out[0], w_o[0]))

    (dx2, dy_rnn, dy_attn, dmr, dma, loss_blk, gfin_blk, g_wr, g_wa, g_wo) = _merge_and_head(
        x2d, tgt, proj, y_rnn, y_attn, wr_full, wa_full, wo_full, fin_g)
    sums_out = _pair_sums([g_wr, g_wa, g_wo], bf16, my_core, "out")

    (dq, dkv, dga, dsink_blk), (p_wr, p_wa, p_wo) = _attn_backward(proj, dy_attn, tabs, attn_sinks, S, sums_out)
    du0, dgr, gwa, gwx, gvec, gcw = _lru_backward(proj, h_all, dy_rnn, cw_full, conv_b, w_a3, lru_b_a, w_x3,
                                                  lru_b_x, lru_lambda, S)
    dsecs = (du0, dgr, dq, dkv, dga, dmr, dma)

    g_wt = _w_in_grad(dsecs, h_bf)
    (sum_in,) = _pair_sums([g_wt], bf16, my_core, "in")
    ex_sems, sum_in, landing, token = _exchange_start(sum_in)
    grad_x2d, gnorm_blk = _input_grad(dsecs, wt_full, x2d, dx2, norm_g + token[0, 0])
    p_wt = _exchange_wait(ex_sems, sum_in, landing, gnorm_blk)
    p_wt_own = lax.dynamic_index_in_dim(sum_in, 2 * cx + cy, axis=0, keepdims=False)

    small, loss_out = _small_step(gwa, gwx, gvec, gnorm_blk, gfin_blk, dsink_blk, loss_blk, gcw, {
        "lru_w_a": (w_a3, m_lru_w_a[0], v_lru_w_a[0]), "lru_w_x": (w_x3, m_lru_w_x[0], v_lru_w_x[0]),
        "conv_b": (conv_b, m_conv_b, v_conv_b), "lru_b_a": (lru_b_a, m_lru_b_a, v_lru_b_a),
        "lru_b_x": (lru_b_x, m_lru_b_x, v_lru_b_x), "lru_lambda": (lru_lambda, m_lru_lambda, v_lru_lambda),
        "norm_g": (norm_g, m_norm_g, v_norm_g),
        "final_norm_g": (fin_g, m_final_norm_g.reshape(1, D), v_final_norm_g.reshape(1, D)),
        "attn_sinks": (attn_sinks, m_attn_sinks, v_attn_sinks),
        "conv_w": (conv_w[0], m_conv_w[0], v_conv_w[0])})

    o_wt = _adamw(p_wt_own, p_wt, w_in[0].T, m_w_in[0].T, v_w_in[0].T, "adamw_w_in")
    o_wr, o_wa, o_wo = _adamw_group(
        (p_wr, p_wa, p_wo), (w_rnn_out[0], w_attn_out[0], w_o[0]),
        (m_w_rnn_out[0], m_w_attn_out[0], m_w_o[0]), (v_w_rnn_out[0], v_w_attn_out[0], v_w_o[0]), "adamw_w_out")

    def result(kind):
        d = {n: small[n][kind] for n in ("conv_b", "lru_b_a", "lru_b_x", "lru_lambda", "norm_g", "attn_sinks")}
        d.update({n: small[n][kind][None] for n in ("lru_w_a", "lru_w_x", "conv_w")})
        d["final_norm_g"] = small["final_norm_g"][kind].reshape(D)
        d.update({"w_in": o_wt[kind].T[None], "w_rnn_out": o_wr[kind][None], "w_attn_out": o_wa[kind][None],
                  "w_o": o_wo[kind][None]})
        return d

    order = ("norm_g", "w_in", "conv_w", "conv_b", "lru_w_a", "lru_b_a", "lru_w_x", "lru_b_x", "lru_lambda",
             "attn_sinks", "w_rnn_out", "w_attn_out", "w_o", "final_norm_g")
    outs = [loss_out[0, 0], grad_x2d.reshape(nb, S, D)]
    for kind in range(4):
        d = result(kind)
        outs += [d[n] for n in order]
    return tuple(outs)
```

```python
import functools
import math

import jax
import jax.numpy as jnp
from jax import lax
from jax.experimental import pallas as pl
from jax.experimental.pallas import tpu as pltpu

f32 = jnp.float32
bf16 = jnp.bfloat16

D = 1024
D_IN = 6656
NDEV = 8
RNN_BLOCKS = 8
RB = 128
HEAD = 64
KV_HEADS = 4
GROUP = 4
QB = 128
LRU_C = 8.0
EPS = 1e-6
ROPE_DIM = 16
ROPE_THETA = 500000.0
CH = 512
SEC_START = (0, 2, 4, 6, 7, 9, 11)
SEC_CHUNKS = (2, 2, 2, 1, 2, 2, 2)
VMEM_LIMIT = 62 * 1024 * 1024

ADAM_LR, ADAM_B1, ADAM_B2, ADAM_EPS, ADAM_WD, ADAM_STEP = 0.001, 0.9, 0.999, 1e-08, 0.01, 10

MESH = pl.DeviceIdType.MESH
ANY = pl.BlockSpec(memory_space=pl.ANY)
VMEM_SPEC = pl.BlockSpec(memory_space=pltpu.VMEM)
SMEM_SPEC = pl.BlockSpec(memory_space=pltpu.SMEM)


def _pcall(body, **kw):
    return pl.pallas_call(body, **kw)


def _params(sem=None, **kw):
    if sem is not None:
        kw["dimension_semantics"] = sem
    return pltpu.CompilerParams(vmem_limit_bytes=VMEM_LIMIT, **kw)


def _sds(shape, dtype):
    return jax.ShapeDtypeStruct(shape, dtype)


def _dot(a, b, dims):
    return lax.dot_general(a, b, (dims, ((), ())), preferred_element_type=f32)


NN = ((1,), (0,))
NT = ((1,), (1,))
TN = ((0,), (0,))


def _sigmoid(v):
    return 0.5 * jnp.tanh(0.5 * v) + 0.5


def _sigmoid_positive(v):
    return 1.0 / (1.0 + jnp.exp(-v))


def _my_place():
    return lax.axis_index("x"), lax.axis_index("y"), lax.axis_index("c")


def _peer(k):
    x, y, c = _my_place()
    return (x + ((k >> 2) & 1)) % 2, (y + ((k >> 1) & 1)) % 2, (c + (k & 1)) % 2


def _direct_gather_copies(srcs, outs, send_sems, recv_sems, local_sems):
    x, y, c = _my_place()
    me = 4 * x + 2 * y + c
    local, remote = [], []
    for a, (src, out) in enumerate(zip(srcs, outs)):
        r = src.shape[0]
        mine = out.at[pl.ds(pl.multiple_of(me * r, 8), r), :]
        local.append(pltpu.make_async_copy(src, mine, local_sems.at[a]))
        for k in range(1, NDEV):
            remote.append(pltpu.make_async_remote_copy(
                src_ref=src, dst_ref=mine, send_sem=send_sems.at[7 * a + k - 1], recv_sem=recv_sems.at[7 * a + k - 1],
                device_id=_peer(k), device_id_type=MESH))
    return local, remote


def _chip_exchange_copies(src, dst, send_sems, recv_sems, local_sems):
    x, y, c = _my_place()
    local, remote = [], []
    for a in range(len(src)):
        local.append(pltpu.make_async_copy(src[a].at[2 * x + y], dst[a].at[0], local_sems.at[a]))
    for k in (3, 1, 2):
        px, py = (x + (k >> 1)) % 2, (y + (k & 1)) % 2
        for a in range(len(src)):
            remote.append(pltpu.make_async_remote_copy(
                src_ref=src[a].at[2 * px + py], dst_ref=dst[a].at[k],
                send_sem=send_sems.at[3 * a + k - 1], recv_sem=recv_sems.at[3 * a + k - 1],
                device_id=(px, py, c), device_id_type=MESH))
    return local, remote


def _exchange_scratch(narr, per_array):
    return [pltpu.SemaphoreType.DMA((per_array * narr,)), pltpu.SemaphoreType.DMA((per_array * narr,)),
            pltpu.SemaphoreType.DMA((narr,))]


def _start_all(copies):
    local, remote = copies
    for cp in local + remote:
        cp.start()


def _wait_all(copies):
    local, remote = copies
    for cp in remote + local:
        cp.wait()


def _pair_exchange(grads, name):
    narr = len(grads)
    nrows = tuple(g.shape[0] // NDEV for g in grads)
    views = [g.reshape(4, 2, r, g.shape[1]) for g, r in zip(grads, nrows)]

    def body(*refs):
        gin = refs[:narr]
        got = refs[narr:2 * narr]
        send_sems, recv_sems = refs[2 * narr:]
        x, y, c = _my_place()
        copies = [pltpu.make_async_remote_copy(
            src_ref=gin[a].at[:, pl.ds(1 - c, 1)], dst_ref=got[a],
            send_sem=send_sems.at[a], recv_sem=recv_sems.at[a],
            device_id=(x, y, 1 - c), device_id_type=MESH) for a in range(narr)]
        for cp in copies:
            cp.start()
        for cp in copies:
            cp.wait()

    out_shape = tuple(_sds((4, 1, r, g.shape[1]), g.dtype) for r, g in zip(nrows, grads))
    got = _pcall(
        body, name=name, out_shape=out_shape,
        in_specs=[ANY] * narr, out_specs=tuple([ANY] * narr),
        scratch_shapes=[pltpu.SemaphoreType.DMA((narr,)), pltpu.SemaphoreType.DMA((narr,))],
        compiler_params=_params(),
    )(*views)
    return views, [g.reshape(4, r, g.shape[3]) for g, r in zip(got, nrows)]


def _row_tile(rows, dtype):
    unit = 16 if dtype == bf16 else 8
    for cand in (256, 208, 128, 64, 40, 32, 16, 8):
        if rows % cand == 0 and cand % unit == 0:
            return cand
    return rows


def _chip_sum(views, gots, my_core, out_dtype, name):
    narr = len(views)
    _, _, r, cols = views[0].shape
    tr = _row_tile(r, out_dtype)

    def body(core_ref, *refs):
        for a in range(narr):
            mine_ref, got_ref, out_ref = refs[a], refs[narr + a], refs[2 * narr + a]
            out_ref[...] = (mine_ref[...].astype(f32) + got_ref[...].astype(f32)).astype(out_dtype)

    slab = pl.BlockSpec((None, tr, cols), lambda q, i, core: (q, i, 0))
    grid_spec = pltpu.PrefetchScalarGridSpec(
        num_scalar_prefetch=1, grid=(4, r // tr),
        in_specs=[pl.BlockSpec((None, None, tr, cols), lambda q, i, core: (q, core[0], i, 0))] * narr + [slab] * narr,
        out_specs=tuple([slab] * narr))
    return _pcall(body, name=name, grid_spec=grid_spec,
                  out_shape=tuple(_sds((4, r, cols), out_dtype) for _ in range(narr)),
                  compiler_params=_params(("arbitrary", "arbitrary")))(my_core, *views, *gots)


def _pair_sums(grads, wire_dtype, my_core, tag):
    views, got = _pair_exchange(grads, "pair_exchange_" + tag)
    return _chip_sum(views, got, my_core, wire_dtype, "chip_sum_" + tag)


def _adam_math(g, w, m, v):
    m_new = ADAM_B1 * m + (1.0 - ADAM_B1) * g
    v_new = ADAM_B2 * v + (1.0 - ADAM_B2) * (g * g)
    m_hat = m_new / (1.0 - ADAM_B1 ** ADAM_STEP)
    v_hat = v_new / (1.0 - ADAM_B2 ** ADAM_STEP)
    return -ADAM_LR * (m_hat / (jnp.sqrt(v_hat) + ADAM_EPS) + ADAM_WD * w), m_new, v_new


def _adamw(first, parts, w, m, v, name):
    n, rows, cols = parts.shape
    tr = _row_tile(rows, parts.dtype)

    def body(f_ref, p_ref, w_ref, m_ref, v_ref, g_out, d_out, m_out, v_out):
        g = f_ref[...].astype(f32)
        for s in range(n):
            g = g + p_ref[s].astype(f32)
        g_out[...] = g
        d_out[...], m_out[...], v_out[...] = _adam_math(g, w_ref[...], m_ref[...], v_ref[...])

    blk = pl.BlockSpec((tr, cols), lambda i: (i, 0))
    return _pcall(
        body, name=name, grid=(rows // tr,),
        in_specs=[blk, pl.BlockSpec((n, tr, cols), lambda i: (0, i, 0)), blk, blk, blk],
        out_specs=(blk, blk, blk, blk), out_shape=tuple(_sds((rows, cols), f32) for _ in range(4)),
        compiler_params=_params(("arbitrary",)),
    )(first, parts, w, m, v)


def _adamw_group(parts, ws, ms, vs, name):
    nw = len(ws)

    def body(*refs):
        p_refs, w_refs, m_refs, v_refs = (refs[k * nw:(k + 1) * nw] for k in range(4))
        outs = refs[4 * nw:]
        for k in range(nw):
            g = p_refs[k][0].astype(f32)
            for s in range(1, p_refs[k].shape[0]):
                g = g + p_refs[k][s].astype(f32)
            g_out, d_out, m_out, v_out = outs[4 * k:4 * k + 4]
            g_out[...] = g
            d_out[...], m_out[...], v_out[...] = _adam_math(g, w_refs[k][...], m_refs[k][...], v_refs[k][...])

    res = _pcall(
        body, name=name, out_shape=tuple(_sds(w.shape, f32) for w in ws for _ in range(4)),
        in_specs=[VMEM_SPEC] * (4 * nw), out_specs=tuple([VMEM_SPEC] * (4 * nw)), compiler_params=_params(),
    )(*parts, *ws, *ms, *vs)
    return [res[4 * k:4 * k + 4] for k in range(nw)]


def _rope(t, c, s1, s2):
    w = t.shape[1]
    return t * c + pltpu.roll(t, w - 8, 1) * s1 + pltpu.roll(t, 8, 1) * s2


def _rope_transposed(dt, c, s1, s2):
    w = dt.shape[1]
    return dt * c + pltpu.roll(dt * s1, 8, 1) + pltpu.roll(dt * s2, w - 8, 1)


PAIR_ROWS = D_IN // 4
SUB_COLS = ((0, 512), (512, 512), (1024, 512), (1536, 128))
Q_SLABS = range(3, 11)
K_SLABS = range(11, 13)


def _in_proj_gather(x2d, norm_g, wt_shard, cw_shard, tabs, S, out_shards, chip_order):
    T = x2d.shape[0]
    tb = min(S, 1024)
    ntok = T // tb
    nsb = S // tb
    q_scale = 1.0 / math.sqrt(HEAD)
    shard_rows = wt_shard.shape[0]
    small = (cw_shard,) + tuple(out_shards)
    nsm = len(small)

    def body(order_ref, x_ref, g_ref, c_ref, s1_ref, s2_ref, wt_hbm, *rest):
        small_in = rest[:nsm]
        h_ref, proj_ref, wt_out = rest[nsm:nsm + 3]
        small_out = rest[nsm + 3:2 * nsm + 3]
        wt_vm, h_vm = rest[2 * nsm + 3:2 * nsm + 5]
        stage = rest[2 * nsm + 5:3 * nsm + 4]
        wsend, wrecv, wlocal = rest[3 * nsm + 4:3 * nsm + 7]
        dsems = rest[3 * nsm + 7:]
        jj, i = pl.program_id(0), pl.program_id(1)
        x, y, c = _my_place()
        me, sibling = (x, y, c), (x, y, 1 - c)
        chips = [(1 - x, y), (x, 1 - y), (1 - x, 1 - y)]

        def rows(place):
            px, py, pc = place
            return wt_vm.at[pl.ds(pl.multiple_of((4 * px + 2 * py + pc) * shard_rows, 16), shard_rows), :]

        def copy(k, block, to, src=None):
            return pltpu.make_async_remote_copy(
                src_ref=rows(block) if src is None else src, dst_ref=rows(block),
                send_sem=wsend.at[k], recv_sem=wrecv.at[k], device_id=to, device_id_type=MESH)

        def small_copies():
            srcs = (small_in[0],) + tuple(stage)
            return _direct_gather_copies(srcs, small_out, *dsems)

        own = pltpu.make_async_copy(wt_hbm, rows(me), wlocal.at[0])
        keep = pltpu.make_async_copy(wt_vm, wt_out, wlocal.at[1])

        @pl.when((jj == 0) & (i == 0))
        def _():
            own.start()
            copy(0, me, sibling, src=wt_hbm).start()
            for j, chip in enumerate(chips):
                copy(1 + j, me, (*chip, c), src=wt_hbm).start()
            for a in range(nsm - 1):
                stage[a][...] = small_in[1 + a][...].astype(bf16)
            _start_all(small_copies())
            own.wait()
            copy(0, sibling, me).wait_recv()

        for j, chip in enumerate(chips):
            @pl.when((jj == 1 + j) & (i == 0))
            def _(j=j, chip=chip):
                copy(1 + j, (*chip, c), me).wait_recv()
                copy(4 + j, (*chip, c), sibling).start()
                copy(4 + j, (*chip, 1 - c), me).wait_recv()

        @pl.when((jj == 3) & (i == 0))
        def _():
            keep.start()

        @pl.when((jj == 3) & (i == ntok - 1))
        def _():
            copy(0, me, sibling, src=wt_hbm).wait_send()
            for j, chip in enumerate(chips):
                copy(1 + j, me, (*chip, c), src=wt_hbm).wait_send()
                copy(4 + j, (*chip, c), sibling).wait_send()
            _wait_all(small_copies())
            keep.wait()

        tok = pl.ds(pl.multiple_of(i * tb, tb), tb)

        @pl.when(jj == 0)
        def _():
            xv = x_ref[...]
            ms = jnp.mean(xv * xv, axis=-1, keepdims=True)
            hb = (xv * lax.rsqrt(ms + EPS) * g_ref[...]).astype(bf16)
            h_ref[...] = hb
            h_vm[tok, :] = hb

        block = order_ref[jj]
        hb = h_vm[tok, :]

        def piece(c0, w):
            w_rows = wt_vm[pl.ds(pl.multiple_of(block * PAIR_ROWS + c0, 128), w), :]
            return _dot(hb, w_rows, NT)

        @pl.when(block != 1)
        def _():
            for c0, w in SUB_COLS:
                proj_ref[:, c0:c0 + w] = piece(c0, w).astype(bf16)

        @pl.when(block == 1)
        def _():
            tab = (c_ref[...], s1_ref[...], s2_ref[...])
            for c0, w in SUB_COLS:
                acc = piece(c0, w)
                for l in range(w // 128):
                    slab = (c0 + 128 * l) // 128
                    part = acc[:, 128 * l:128 * (l + 1)]
                    if slab in Q_SLABS:
                        part = _rope(part, *tab) * q_scale
                    elif slab in K_SLABS:
                        part = _rope(part, *tab)
                    proj_ref[:, 128 * slab:128 * (slab + 1)] = part.astype(bf16)

    first_pass = lambda jj, i, order: (jnp.where(jj == 0, i, ntok - 1), 0)
    const = lambda jj, i, order: (0, 0)
    tab = pl.BlockSpec((tb, 128), lambda jj, i, order: (jnp.where(order[jj] == 1, i % nsb, 0), 0))
    grid_spec = pltpu.PrefetchScalarGridSpec(
        num_scalar_prefetch=1, grid=(4, ntok),
        in_specs=[pl.BlockSpec((tb, D), first_pass), pl.BlockSpec((1, D), const), tab, tab, tab, ANY]
        + [pl.BlockSpec(w.shape, const) for w in small],
        out_specs=(pl.BlockSpec((tb, D), first_pass),
                   pl.BlockSpec((tb, PAIR_ROWS), lambda jj, i, order: (i, order[jj])), ANY) + tuple([ANY] * nsm),
        scratch_shapes=[pltpu.VMEM((D_IN, D), bf16), pltpu.VMEM((T, D), bf16)]
        + [pltpu.VMEM(w.shape, bf16) for w in out_shards]
        + [pltpu.SemaphoreType.DMA((7,)), pltpu.SemaphoreType.DMA((7,)), pltpu.SemaphoreType.DMA((2,))]
        + _exchange_scratch(nsm, 7))
    res = _pcall(
        body, name="in_proj", grid_spec=grid_spec,
        out_shape=(_sds((T, D), bf16), _sds((T, D_IN), bf16), _sds((D_IN, D), bf16),
                   _sds((NDEV * cw_shard.shape[0], cw_shard.shape[1]), f32))
        + tuple(_sds((NDEV * w.shape[0], w.shape[1]), bf16) for w in out_shards),
        compiler_params=_params(("arbitrary", "arbitrary")),
    )(chip_order, x2d, norm_g, *tabs, wt_shard, *small)
    return res[0], res[1], res[2], res[3], res[4:]


def _rows_iota(shape):
    return lax.broadcasted_iota(jnp.int32, shape, 0)


def _shift_down(v, k):
    return jnp.where(_rows_iota(v.shape) >= k, pltpu.roll(v, k, 0), 0.0)


def _shift_up(v, k):
    n = v.shape[0]
    return jnp.where(_rows_iota(v.shape) < n - k, pltpu.roll(v, n - k, 0), 0.0)


def _linear_scan(a, b, a_s, b_s, edge_s, out_ref, reverse):
    n = a.shape[0]
    ng = n // 8
    a3, b3 = a.reshape(ng, 8, RB), b.reshape(ng, 8, RB)
    rid = lax.broadcasted_iota(jnp.int32, a3.shape, 1)
    for s in (1, 2, 4):
        keep, shift = (rid < 8 - s, 8 - s) if reverse else (rid >= s, s)
        b3 = jnp.where(keep, a3 * pltpu.roll(b3, shift, 1) + b3, b3)
        a3 = jnp.where(keep, a3 * pltpu.roll(a3, shift, 1), a3)
    a_s[...] = a3.reshape(n, RB)
    b_s[...] = b3.reshape(n, RB)
    edge = 0 if reverse else 7
    ea, eb = a_s[pl.ds(edge, ng, stride=8), :], b_s[pl.ds(edge, ng, stride=8), :]
    r = _rows_iota(ea.shape)
    s = 1
    while s < ng:
        keep, shift = (r < ng - s, ng - s) if reverse else (r >= s, s)
        eb = jnp.where(keep, ea * pltpu.roll(eb, shift, 0) + eb, eb)
        if 2 * s < ng:
            ea = jnp.where(keep, ea * pltpu.roll(ea, shift, 0), ea)
        s *= 2
    edge_s[...] = _shift_up(eb, 1) if reverse else _shift_down(eb, 1)

    def eight_groups(i, carry):
        for k in range(8):
            j = i * 8 + k
            rows = pl.ds(pl.multiple_of(j * 8, 8), 8)
            out_ref[rows, :] = b_s[rows, :] + a_s[rows, :] * edge_s[pl.ds(j, 1), :]
        return carry

    lax.fori_loop(0, ng // 8, eight_groups, 0)


def _neg_expm1(v):
    series = -v * (1.0 + v * (0.5 + v * (1.0 / 6.0)))
    return jnp.where(v > -0.015625, series, 1.0 - jnp.exp(v))


def _softplus_neg(lam):
    return jnp.maximum(-lam, 0.0) + jnp.log(1.0 + jnp.exp(-jnp.abs(lam)))


def _lru_gates(x0, cw, cb, wa, ba, wx, bx, lam):
    taps = [_shift_down(x0, 3 - k) for k in range(3)] + [x0]
    u = cb + cw[3:4, :] * x0
    for k in range(3):
        u = u + cw[k:k + 1, :] * taps[k]
    ub = u.astype(bf16)
    r = _sigmoid_positive(_dot(ub, wa.astype(bf16), NN) + ba)
    i = _sigmoid(_dot(ub, wx.astype(bf16), NN) + bx)
    sp = _softplus_neg(lam)
    log_a = (-LRU_C) * r * sp
    a = jnp.exp(log_a)
    w = _neg_expm1(2.0 * log_a)
    inv_mult = lax.rsqrt(w)
    return u, ub, r, i, sp, a, w * inv_mult, inv_mult, taps


def _lru_specs(S, nb):
    col = lambda off: pl.BlockSpec((S, RB), lambda n, b, off=off: (b, off + n))
    vec = pl.BlockSpec((1, RB), lambda n, b: (0, n))
    wblk = pl.BlockSpec((None, RB, RB), lambda n, b: (n, 0, 0))
    cwblk = pl.BlockSpec((8, RB), lambda n, b: (n, 0))
    return col, vec, wblk, cwblk


NPEER = NDEV - 1


def _split_gather_copies(src_ref, land_ref, send_sems, recv_sems):
    x, y, c = _my_place()
    mine = land_ref.at[:, 4 * x + 2 * y + c]
    return [pltpu.make_async_remote_copy(src_ref=src_ref, dst_ref=mine, send_sem=send_sems[k - 1],
                                         recv_sem=recv_sems[k - 1], device_id=_peer(k), device_id_type=MESH)
            for k in range(1, NDEV)]


def _lru_forward(proj, cw_full, conv_b, w_a, b_a, w_x, b_x, lam, S, block):
    T = proj.shape[0]
    nb = T // S
    col, vec, wblk, cwblk = _lru_specs(S, nb)
    land_shape = (block.shape[0], NDEV) + block.shape[1:]

    def body(x0_ref, g_ref, cw_ref, cb_ref, wa_ref, ba_ref, wx_ref, bx_ref, lam_ref, blk_ref, land_ref,
             y_ref, h_ref, *rest):
        a_s, b_s, edge_s = rest[2 * NPEER + 2:]

        @pl.when((pl.program_id(0) == 0) & (pl.program_id(1) == 0))
        def _():
            for cp in _split_gather_copies(blk_ref, land_ref, rest[:NPEER], rest[NPEER:2 * NPEER]):
                cp.start()

        x0 = x0_ref[...].astype(f32)
        u, ub, r, i, sp, a, mult, _, _ = _lru_gates(x0, cw_ref[...], cb_ref[...], wa_ref[...], ba_ref[...],
                                                    wx_ref[...], bx_ref[...], lam_ref[...])
        _linear_scan(a, mult * (i * u), a_s, b_s, edge_s, h_ref, reverse=False)
        g = g_ref[...].astype(f32)
        y_ref[...] = (h_ref[...] * (g * _sigmoid(g))).astype(bf16)

    out = pl.BlockSpec((S, RB), lambda n, b: (b, n))
    land = pltpu.with_memory_space_constraint(lax.empty(land_shape, block.dtype), pltpu.HBM)
    res = _pcall(
        body, name="lru_forward", grid=(RNN_BLOCKS, nb),
        in_specs=[col(0), col(8), cwblk, vec, wblk, vec, wblk, vec, vec, HBM_SPEC, HBM_SPEC],
        out_specs=(out, out) + tuple([SEM_SPEC] * (2 * NPEER)) + (HBM_SPEC, HBM_SPEC),
        out_shape=(_sds((T, D), bf16), _sds((T, D), f32)) + tuple([pltpu.SemaphoreType.DMA(())] * (2 * NPEER))
        + (pltpu.HBM(block.shape, block.dtype), pltpu.HBM(land_shape, block.dtype)),
        input_output_aliases={9: 2 * NPEER + 2, 10: 2 * NPEER + 3},
        scratch_shapes=[pltpu.VMEM((S, RB), f32), pltpu.VMEM((S, RB), f32), pltpu.VMEM((S // 8, RB), f32)],
        compiler_params=_params(("arbitrary", "arbitrary"), has_side_effects=EFFECT),
    )(proj, proj, cw_full, conv_b, w_a, b_a, w_x, b_x, lam, pltpu.with_memory_space_constraint(block, pltpu.HBM), land)
    return res[0], res[1], res[2:2 * NPEER + 2], res[2 * NPEER + 2], res[2 * NPEER + 3]


def _rope_tables(S):
    pos = jnp.arange(S, dtype=f32)
    inv_freq = ROPE_THETA ** (-jnp.arange(0, ROPE_DIM, 2, dtype=f32) / ROPE_DIM)
    ang = pos[:, None] * inv_freq[None, :]
    cos, sin = jnp.cos(ang), jnp.sin(ang)
    lane = jnp.arange(128) % HEAD
    cosl, sinl = cos[:, lane % 8], sin[:, lane % 8]
    c = jnp.where(lane[None, :] < ROPE_DIM, cosl, 1.0)
    s1 = jnp.where(lane[None, :] < 8, -sinl, 0.0)
    s2 = jnp.where((lane[None, :] >= 8) & (lane[None, :] < ROPE_DIM), sinl, 0.0)
    return c.astype(f32), s1.astype(f32), s2.astype(f32)


def _heads_to_rows(t):
    return jnp.concatenate([t[:, HEAD * h:HEAD * (h + 1)] for h in range(GROUP)], axis=0)


def _rows_to_heads(t):
    return jnp.concatenate([t[QB * h:QB * (h + 1), :] for h in range(GROUP)], axis=1)


def _window_bias(first_block):
    shape = (GROUP * QB, 2 * QB)
    qi = _rows_iota(shape) % QB
    cj = lax.broadcasted_iota(jnp.int32, shape, 1)
    valid = (cj > qi) & (cj <= qi + QB) & ((cj >= QB) | jnp.logical_not(first_block))
    return jnp.where(valid, 0.0, -jnp.inf)


def _attn_probs(q_rows, k_cat, sink_col, bias):
    s = _dot(q_rows, k_cat, NT) + bias
    m = jnp.maximum(jnp.max(s, axis=1, keepdims=True), sink_col)
    p = jnp.exp(s - m)
    e_sink = jnp.exp(sink_col - m)
    inv = 1.0 / (jnp.sum(p, axis=1, keepdims=True) + e_sink)
    return p * inv, e_sink * inv


def _sink_column(sink_ref, kv):
    rid = _rows_iota((GROUP * QB, 1))
    col = jnp.zeros((GROUP * QB, 1), f32)
    for h in range(GROUP):
        col = jnp.where(rid // QB == h, sink_ref[0, GROUP * kv + h], col)
    return col


def _attn_in_specs(S):
    nq = S // QB
    last = nq - 1
    cur = lambda b, j: b * nq + jnp.minimum(j, last)
    prev = lambda b, j: b * nq + jnp.maximum(jnp.minimum(j, last) - 1, 0)
    specs = [
        pl.BlockSpec((QB, D), lambda b, j: (cur(b, j), 2)),
        pl.BlockSpec((QB, 256), lambda b, j: (cur(b, j), 12)),
        pl.BlockSpec((QB, 256), lambda b, j: (prev(b, j), 12)),
        pl.BlockSpec((QB, 256), lambda b, j: (cur(b, j), 13)),
        pl.BlockSpec((QB, 256), lambda b, j: (prev(b, j), 13)),
        pl.BlockSpec((QB, 512), lambda b, j: (cur(b, j), 7)),
        pl.BlockSpec((QB, 512), lambda b, j: (cur(b, j), 8)),
        SMEM_SPEC,
    ]
    return specs, cur, prev


def _attn_forward(proj, sinks, S, sems, block, land):
    T = proj.shape[0]
    nb, nq = T // S, S // QB
    specs, cur, _ = _attn_in_specs(S)

    def body(q_ref, kc_ref, kp_ref, vc_ref, vp_ref, gl_ref, gh_ref, sink_ref, blk_ref, land_ref, *rest):
        y_ref = rest[2 * NPEER]
        b, j = pl.program_id(0), pl.program_id(1)

        @pl.when((b == nb - 1) & (j == nq - 1))
        def _():
            for cp in _split_gather_copies(blk_ref, land_ref, rest[:NPEER], rest[NPEER:2 * NPEER]):
                cp.wait_send()
                cp.wait_recv()

        bias = _window_bias(j == 0)
        kc, kp, vc, vp = kc_ref[...], kp_ref[...], vc_ref[...], vp_ref[...]
        for kv in range(KV_HEADS):
            lanes = slice(256 * kv, 256 * (kv + 1))
            hl = slice(HEAD * kv, HEAD * (kv + 1))
            q_rows = _heads_to_rows(q_ref[:, lanes])
            k_cat = jnp.concatenate([kp[:, hl], kc[:, hl]], axis=0)
            v_cat = jnp.concatenate([vp[:, hl], vc[:, hl]], axis=0)
            probs, _ = _attn_probs(q_rows, k_cat, _sink_column(sink_ref, kv), bias)
            o = _rows_to_heads(_dot(probs.astype(bf16), v_cat, NN))
            g_src = gl_ref if kv < 2 else gh_ref
            g = g_src[:, 256 * (kv % 2):256 * (kv % 2 + 1)].astype(f32)
            y_ref[:, lanes] = (o * (g * _sigmoid(g))).astype(bf16)

    args = [proj] * 7 + [sinks, block, land] + list(sems)
    res = _pcall(
        body, name="attn_forward", grid=(nb, nq),
        in_specs=specs + [HBM_SPEC, HBM_SPEC] + [SEM_SPEC] * (2 * NPEER),
        out_specs=(pl.BlockSpec((QB, D), lambda b, j: (cur(b, j), 0)), HBM_SPEC, HBM_SPEC),
        out_shape=(_sds((T, D), bf16), pltpu.HBM(block.shape, block.dtype), pltpu.HBM(land.shape, land.dtype)),
        input_output_aliases={8: 1, 9: 2},
        compiler_params=_params(("arbitrary", "arbitrary"), has_side_effects=EFFECT),
    )(*args)
    return res[0], res[2]


def _merge_and_head(x2d, tgt, proj, y_rnn, y_attn, w_land, gfin):
    T = x2d.shape[0]
    tb = min(T, 512)
    nsteps = T // tb

    def body(x_ref, t_ref, mr0, mr1, ma0, ma1, yr_ref, ya_ref, wr_ref, wa_ref, wo_ref, gf_ref,
             dx2_ref, dyr_ref, dya_ref, dmr_ref, dma_ref, loss_ref, gfin_ref, gwr_out, gwa_out, gwo_out,
             gwr_acc, gwa_acc, gwo_acc, out_sems):
        step = pl.program_id(0)

        @pl.when(step == 0)
        def _():
            loss_ref[...] = jnp.zeros_like(loss_ref)
            gfin_ref[...] = jnp.zeros_like(gfin_ref)
            gwr_acc[...] = jnp.zeros_like(gwr_acc)
            gwa_acc[...] = jnp.zeros_like(gwa_acc)
            gwo_acc[...] = jnp.zeros_like(gwo_acc)

        sr = _sigmoid(jnp.concatenate([mr0[...], mr1[...]], axis=1).astype(f32))
        sa = _sigmoid(jnp.concatenate([ma0[...], ma1[...]], axis=1).astype(f32))
        p_r = _dot(yr_ref[...], wr_ref[...], NN)
        p_a = _dot(ya_ref[...], wa_ref[...], NN)
        merged = (sr * p_r + sa * p_a).astype(bf16)
        x2 = x_ref[...] + _dot(merged, wo_ref[...], NN)
        rstd = lax.rsqrt(jnp.mean(x2 * x2, axis=-1, keepdims=True) + EPS)
        xh = x2 * rstd
        gf = gf_ref[...]
        err = xh * gf - t_ref[...]
        loss_ref[...] += jnp.sum(err * err)
        dy = err * (1.0 / D)
        gfin_ref[0:1, :] += jnp.sum(dy * xh, axis=0, keepdims=True)
        dxn = dy * gf
        dx2 = rstd * (dxn - xh * jnp.mean(dxn * xh, axis=-1, keepdims=True))
        dx2_ref[...] = dx2
        dx2b = dx2.astype(bf16)
        dmerged = _dot(dx2b, wo_ref[...], NT)
        dmr_ref[...] = (dmerged * p_r * (sr * (1.0 - sr))).astype(bf16)
        dma_ref[...] = (dmerged * p_a * (sa * (1.0 - sa))).astype(bf16)
        dpr = (dmerged * sr).astype(bf16)
        dpa = (dmerged * sa).astype(bf16)
        dyr_ref[...] = _dot(dpr, wr_ref[...], NT).astype(bf16)
        dya_ref[...] = _dot(dpa, wa_ref[...], NT).astype(bf16)
        gwr_acc[...] += _dot(yr_ref[...], dpr, TN)
        gwa_acc[...] += _dot(ya_ref[...], dpa, TN)
        gwo_acc[...] += _dot(merged, dx2b, TN)

        @pl.when(step == nsteps - 1)
        def _():
            copies = [pltpu.make_async_copy(src, dst, out_sems.at[k]) for k, (src, dst) in enumerate(
                ((gwr_acc, gwr_out), (gwa_acc, gwa_out), (gwo_acc, gwo_out)))]
            for cp in copies:
                cp.start()
            for cp in copies:
                cp.wait()

    tok = pl.BlockSpec((tb, D), lambda i: (i, 0))
    half = lambda c: pl.BlockSpec((tb, CH), lambda i, c=c: (i, c))
    wspec = lambda a: pl.BlockSpec((None, D, D), lambda i, a=a: (a, 0, 0), pipeline_mode=pl.Buffered(1))
    acc = pl.BlockSpec((8, D), lambda i: (0, 0))
    return _pcall(
        body, name="merge_and_head", grid=(nsteps,),
        in_specs=[tok, tok, half(9), half(10), half(11), half(12), tok, tok, wspec(0), wspec(1), wspec(2),
                  pl.BlockSpec((1, D), lambda i: (0, 0))],
        out_specs=(tok, tok, tok, tok, tok, acc, acc, ANY, ANY, ANY),
        out_shape=(_sds((T, D), f32), _sds((T, D), bf16), _sds((T, D), bf16), _sds((T, D), bf16),
                   _sds((T, D), bf16), _sds((8, D), f32), _sds((8, D), f32),
                   _sds((D, D), f32), _sds((D, D), f32), _sds((D, D), f32)),
        scratch_shapes=[pltpu.VMEM((D, D), f32)] * 3 + [pltpu.SemaphoreType.DMA((3,))],
        compiler_params=_params(("arbitrary",)),
    )(x2d, tgt, proj, proj, proj, proj, y_rnn, y_attn, w_land, w_land, w_land, gfin)


def _attn_backward(proj, dy_attn, tabs, sinks, S, chip_sums):
    T = proj.shape[0]
    nb, nq = T // S, S // QB
    nex = len(chip_sums)
    specs, cur, prev = _attn_in_specs(S)
    last = nq - 1
    tab_cur = pl.BlockSpec((QB, 128), lambda b, j: (jnp.minimum(j, last), 0))
    tab_prev = pl.BlockSpec((QB, 128), lambda b, j: (jnp.maximum(jnp.minimum(j, last) - 1, 0), 0))
    specs = specs + [pl.BlockSpec((QB, D), lambda b, j: (cur(b, j), 0))] + [tab_cur] * 3 + [tab_prev] * 3
    q_scale = 1.0 / math.sqrt(HEAD)

    def rope_back(dt, tab):
        return jnp.concatenate([_rope_transposed(dt[:, 128 * l:128 * (l + 1)], *tab) for l in range(2)], axis=1)

    def body(q_ref, kc_ref, kp_ref, vc_ref, vp_ref, gl_ref, gh_ref, sink_ref, dy_ref, cc, s1c, s2c, cp, s1p, s2p,
             *rest):
        ex_src = rest[:nex]
        dq_ref, dkv_ref, dg_ref, dsink_ref = rest[nex:nex + 4]
        ex_dst = rest[nex + 4:2 * nex + 4]
        carry_k, carry_v = rest[2 * nex + 4:2 * nex + 6]
        sems = rest[2 * nex + 6:]
        b, j = pl.program_id(0), pl.program_id(1)

        @pl.when((b == 0) & (j == 0))
        def _():
            dsink_ref[...] = jnp.zeros_like(dsink_ref)
            _start_all(_chip_exchange_copies(ex_src, ex_dst, *sems))

        @pl.when((b == nb - 1) & (j == nq))
        def _():
            _wait_all(_chip_exchange_copies(ex_src, ex_dst, *sems))

        @pl.when(j == 0)
        def _():
            carry_k[...] = jnp.zeros_like(carry_k)
            carry_v[...] = jnp.zeros_like(carry_v)

        @pl.when(j < nq)
        def _():
            bias = _window_bias(j == 0)
            tc = (cc[...], s1c[...], s2c[...])
            tp = (cp[...], s1p[...], s2p[...])
            kc, kp, vc, vp = kc_ref[...], kp_ref[...], vc_ref[...], vp_ref[...]
            dk_prev, dk_cur, dv_prev, dv_cur = [], [], [], []
            dsink_acc = jnp.zeros((8, 128), f32)
            r8 = lax.broadcasted_iota(jnp.int32, (8, 128), 0)
            l8 = lax.broadcasted_iota(jnp.int32, (8, 128), 1)
            for kv in range(KV_HEADS):
                lanes = slice(256 * kv, 256 * (kv + 1))
                hl = slice(HEAD * kv, HEAD * (kv + 1))
                q_rows = _heads_to_rows(q_ref[:, lanes])
                k_cat = jnp.concatenate([kp[:, hl], kc[:, hl]], axis=0)
                v_cat = jnp.concatenate([vp[:, hl], vc[:, hl]], axis=0)
                probs, p_sink = _attn_probs(q_rows, k_cat, _sink_column(sink_ref, kv), bias)
                pb = probs.astype(bf16)
                o = _rows_to_heads(_dot(pb, v_cat, NN))
                g_src = gl_ref if kv < 2 else gh_ref
                g = g_src[:, 256 * (kv % 2):256 * (kv % 2 + 1)].astype(f32)
                sg = _sigmoid(g)
                dy = dy_ref[:, lanes].astype(f32)
                dg_ref[:, lanes] = (dy * o * (sg * (1.0 + g * (1.0 - sg)))).astype(bf16)
                do_rows = _heads_to_rows(dy * (g * sg)).astype(bf16)
                dv = _dot(pb, do_rows, TN)
                dp = _dot(do_rows, v_cat, NT)
                rowdot = jnp.sum(probs * dp, axis=1, keepdims=True)
                ds = (probs * (dp - rowdot)).astype(bf16)
                sink_rows = -(p_sink * rowdot)
                for h in range(GROUP):
                    val = jnp.sum(sink_rows[QB * h:QB * (h + 1), :])
                    dsink_acc = dsink_acc + jnp.where((r8 == 0) & (l8 == GROUP * kv + h), val, 0.0)
                dq = _rows_to_heads(_dot(ds, k_cat, NN)) * q_scale
                dq_ref[:, lanes] = rope_back(dq, tc).astype(bf16)
                dk = _dot(ds, q_rows, TN)
                dk_prev.append(dk[:QB, :])
                dk_cur.append(dk[QB:, :])
                dv_prev.append(dv[:QB, :])
                dv_cur.append(dv[QB:, :])
            dsink_ref[...] += dsink_acc
            dkp = rope_back(jnp.concatenate(dk_prev, axis=1), tp)
            dkc = rope_back(jnp.concatenate(dk_cur, axis=1), tc)
            dkv_ref[:, 0:256] = (carry_k[...] + dkp).astype(bf16)
            dkv_ref[:, 256:512] = (carry_v[...] + jnp.concatenate(dv_prev, axis=1)).astype(bf16)
            carry_k[...] = dkc
            carry_v[...] = jnp.concatenate(dv_cur, axis=1)

        @pl.when(j == nq)
        def _():
            dkv_ref[:, 0:256] = carry_k[...].astype(bf16)
            dkv_ref[:, 256:512] = carry_v[...].astype(bf16)

    lag = lambda b, j: (b * nq + jnp.maximum(j - 1, 0), 0)
    args = [proj] * 7 + [sinks, dy_attn] + list(tabs) + list(tabs) + list(chip_sums)
    res = _pcall(
        body, name="attn_backward", grid=(nb, nq + 1), in_specs=specs + [ANY] * nex,
        out_specs=(pl.BlockSpec((QB, D), lambda b, j: (cur(b, j), 0)), pl.BlockSpec((QB, 512), lag),
                   pl.BlockSpec((QB, D), lambda b, j: (cur(b, j), 0)), pl.BlockSpec((8, 128), lambda b, j: (0, 0)))
        + tuple([ANY] * nex),
        out_shape=(_sds((T, D), bf16), _sds((T, 512), bf16), _sds((T, D), bf16), _sds((8, 128), f32))
        + tuple(_sds(s.shape, s.dtype) for s in chip_sums),
        scratch_shapes=[pltpu.VMEM((QB, 256), f32), pltpu.VMEM((QB, 256), f32)] + _exchange_scratch(nex, 3),
        compiler_params=_params(("arbitrary", "arbitrary")),
    )(*args)
    return res[:4], res[4:]


def _lru_backward(proj, h_all, dy_rnn, cw_full, conv_b, w_a, b_a, w_x, b_x, lam, S):
    T = proj.shape[0]
    nb = T // S
    col, vec, wblk, cwblk = _lru_specs(S, nb)
    tokblk = pl.BlockSpec((S, RB), lambda n, b: (b, n))

    def body(x0_ref, g_ref, h_ref, dy_ref, cw_ref, cb_ref, wa_ref, ba_ref, wx_ref, bx_ref, lam_ref,
             du0_ref, dg_ref, gwa_ref, gwx_ref, vec_ref, gcw_ref, a_s, b_s, dh_s, edge_s):
        @pl.when(pl.program_id(1) == 0)
        def _():
            gwa_ref[...] = jnp.zeros_like(gwa_ref)
            gwx_ref[...] = jnp.zeros_like(gwx_ref)
            vec_ref[...] = jnp.zeros_like(vec_ref)
            gcw_ref[...] = jnp.zeros_like(gcw_ref)

        x0 = x0_ref[...].astype(f32)
        cw = cw_ref[...]
        lam_v = lam_ref[...]
        u, ub, r, i, sp, a, mult, inv_mult, taps = _lru_gates(x0, cw, cb_ref[...], wa_ref[...], ba_ref[...],
                                                              wx_ref[...], bx_ref[...], lam_v)
        h = h_ref[...]
        g = g_ref[...].astype(f32)
        dy = dy_ref[...].astype(f32)
        sg = _sigmoid(g)
        dg_ref[...] = (dy * h * (sg * (1.0 + g * (1.0 - sg)))).astype(bf16)
        _linear_scan(_shift_up(a, 1), dy * (g * sg), a_s, b_s, edge_s, dh_s, reverse=True)
        dh_total = dh_s[...]
        da = dh_total * _shift_down(h, 1)
        dmult = dh_total * (i * u)
        db = dh_total * mult
        di = db * u
        du = db * i
        dlog_a_c = ((-LRU_C) * a) * (da - dmult * (a * inv_mult))
        dr = dlog_a_c * sp
        dsp = jnp.sum(dlog_a_c * r, axis=0, keepdims=True)
        dpre_r = dr * r * (1.0 - r)
        dpre_i = di * i * (1.0 - i)
        dpre_rb = dpre_r.astype(bf16)
        dpre_ib = dpre_i.astype(bf16)
        du = du + _dot(dpre_rb, wa_ref[...].astype(bf16), NT) + _dot(dpre_ib, wx_ref[...].astype(bf16), NT)
        gwa_ref[...] += _dot(ub, dpre_rb, TN)
        gwx_ref[...] += _dot(ub, dpre_ib, TN)
        vec_ref[0:1, :] += jnp.sum(du, axis=0, keepdims=True)
        vec_ref[1:2, :] += jnp.sum(dpre_r, axis=0, keepdims=True)
        vec_ref[2:3, :] += jnp.sum(dpre_i, axis=0, keepdims=True)
        vec_ref[3:4, :] += dsp * (-_sigmoid(-lam_v))
        dx0 = cw[3:4, :] * du
        for k in range(3):
            dx0 = dx0 + cw[k:k + 1, :] * _shift_up(du, 3 - k)
        for k in range(4):
            gcw_ref[k:k + 1, :] += jnp.sum(du * taps[k], axis=0, keepdims=True)
        du0_ref[...] = dx0.astype(bf16)

    wacc = pl.BlockSpec((RB, RB), lambda n, b: (0, n))
    vacc = pl.BlockSpec((8, RB), lambda n, b: (0, n))
    cacc = pl.BlockSpec((8, RB), lambda n, b: (n, 0))
    return _pcall(
        body, name="lru_backward", grid=(RNN_BLOCKS, nb),
        in_specs=[col(0), col(8), tokblk, tokblk, cwblk, vec, wblk, vec, wblk, vec, vec],
        out_specs=(tokblk, tokblk, wacc, wacc, vacc, cacc),
        out_shape=(_sds((T, D), bf16), _sds((T, D), bf16), _sds((RB, D), f32), _sds((RB, D), f32),
                   _sds((8, D), f32), _sds((8 * RNN_BLOCKS, RB), f32)),
        scratch_shapes=[pltpu.VMEM((S, RB), f32)] * 3 + [pltpu.VMEM((S // 8, RB), f32)],
        compiler_params=_params(("arbitrary", "arbitrary")),
    )(proj, proj, h_all, dy_rnn, cw_full, conv_b, w_a, b_a, w_x, b_x, lam)


def _section_of_chunk(s):
    out = []
    for start, n in zip(SEC_START, SEC_CHUNKS):
        inside = (s >= start) & (s < start + n)
        out.append((inside, jnp.clip(s - start, 0, n - 1)))
    return out


EFFECT = pltpu.SideEffectType.DATAFLOW_SIDE_EFFECTING
HBM_SPEC = pl.BlockSpec(memory_space=pltpu.HBM)
SEM_SPEC = pl.BlockSpec(memory_space=pltpu.SEMAPHORE)


def _split_exchange_copies(src_ref, land_ref, send_sems, recv_sems):
    x, y, c = _my_place()
    copies = []
    for k in (3, 1, 2):
        px, py = (x + (k >> 1)) % 2, (y + (k & 1)) % 2
        copies.append(pltpu.make_async_remote_copy(
            src_ref=src_ref.at[2 * px + py], dst_ref=land_ref.at[k - 1], send_sem=send_sems[k - 1],
            recv_sem=recv_sems[k - 1], device_id=(px, py, c), device_id_type=MESH))
    return copies


def _exchange_start(chip_sum):
    _, r, cols = chip_sum.shape

    def body(src_ref, land_ref, s0, s1, s2, r0, r1, r2, src_thru, land_thru, token):
        for cp in _split_exchange_copies(src_ref, land_ref, (s0, s1, s2), (r0, r1, r2)):
            cp.start()
        token[...] = jnp.zeros_like(token)

    land = pltpu.with_memory_space_constraint(lax.empty((3, r, cols), chip_sum.dtype), pltpu.HBM)
    res = _pcall(
        body, name="exchange_start",
        out_shape=tuple([pltpu.SemaphoreType.DMA(())] * 6) + (
            pltpu.HBM(chip_sum.shape, chip_sum.dtype), pltpu.HBM((3, r, cols), chip_sum.dtype), _sds((8, 128), f32)),
        in_specs=(HBM_SPEC, HBM_SPEC), out_specs=tuple([SEM_SPEC] * 6) + (HBM_SPEC, HBM_SPEC, VMEM_SPEC),
        input_output_aliases={0: 6, 1: 7},
        compiler_params=pltpu.CompilerParams(has_side_effects=EFFECT),
    )(pltpu.with_memory_space_constraint(chip_sum, pltpu.HBM), land)
    return res[:6], res[6], res[7], res[8]


def _exchange_wait(sems, src_thru, land_thru, after):
    def body(src_ref, land_ref, s0, s1, s2, r0, r1, r2, after_ref, src_dead, got_ref):
        for cp in _split_exchange_copies(src_ref, land_ref, (s0, s1, s2), (r0, r1, r2)):
            cp.wait_send()
            cp.wait_recv()

    return _pcall(
        body, name="exchange_wait",
        out_shape=(pltpu.HBM(src_thru.shape, src_thru.dtype), pltpu.HBM(land_thru.shape, land_thru.dtype)),
        in_specs=(HBM_SPEC, HBM_SPEC) + tuple([SEM_SPEC] * 6) + (ANY,), out_specs=(HBM_SPEC, HBM_SPEC),
        input_output_aliases={0: 0, 1: 1},
        compiler_params=pltpu.CompilerParams(has_side_effects=EFFECT),
    )(src_thru, land_thru, *sems, after)[1]


def _input_grad(dsecs, wt_full, x2d, dx2, norm_g):
    T = x2d.shape[0]
    tb = min(T, 512)
    nsec = len(dsecs)
    ntok = T // tb

    def body(*refs):
        secs = refs[:nsec]
        wt_ref, x_ref, dx2_ref, g_ref, dx_ref, gnorm_ref = refs[nsec:]
        i = pl.program_id(0)

        @pl.when(i == 0)
        def _():
            gnorm_ref[...] = jnp.zeros_like(gnorm_ref)

        dh = None
        for a, (start, n) in enumerate(zip(SEC_START, SEC_CHUNKS)):
            part = _dot(secs[a][...], wt_ref[CH * start:CH * (start + n), :], NN)
            dh = part if dh is None else dh + part
        xv = x_ref[...]
        rstd = lax.rsqrt(jnp.mean(xv * xv, axis=-1, keepdims=True) + EPS)
        xh = xv * rstd
        gnorm_ref[0:1, :] += jnp.sum(dh * xh, axis=0, keepdims=True)
        dxn = dh * g_ref[...]
        dx_ref[...] = dx2_ref[...] + rstd * (dxn - xh * jnp.mean(dxn * xh, axis=-1, keepdims=True))

    tok = pl.BlockSpec((tb, D), lambda i: (i, 0))
    return _pcall(
        body, name="input_grad", grid=(ntok,),
        in_specs=[pl.BlockSpec((tb, sec.shape[1]), lambda i: (i, 0)) for sec in dsecs]
        + [pl.BlockSpec((D_IN, D), lambda i: (0, 0), pipeline_mode=pl.Buffered(1)), tok, tok,
           pl.BlockSpec((1, D), lambda i: (0, 0))],
        out_specs=(tok, pl.BlockSpec((8, D), lambda i: (0, 0))),
        out_shape=(_sds((T, D), f32), _sds((8, D), f32)),
        compiler_params=_params(("arbitrary",)),
    )(*dsecs, wt_full, x2d, dx2, norm_g)


def _w_in_grad(dsecs, h_bf):
    T = h_bf.shape[0]
    tk = min(T, 2048)
    nchunks = D_IN // CH
    nsec = len(dsecs)
    nt = T // tk

    def body(*refs):
        secs = refs[:nsec]
        h_ref, out_ref, acc = refs[nsec:]
        s, t = pl.program_id(0), pl.program_id(1)

        @pl.when(t == 0)
        def _():
            acc[...] = jnp.zeros_like(acc)

        h_rows = h_ref[pl.ds(pl.multiple_of(t * tk, tk), tk), :]
        for a, (start, n) in enumerate(zip(SEC_START, SEC_CHUNKS)):
            @pl.when((s >= start) & (s < start + n))
            def _(a=a):
                acc[...] += _dot(secs[a][...], h_rows, TN)

        @pl.when(t == nt - 1)
        def _():
            out_ref[...] = acc[...].astype(bf16)

    def sec_spec(a):
        def index(s, t, a=a):
            inside, local = _section_of_chunk(s)[a]
            return (jnp.where(inside, t, 0), local)
        return pl.BlockSpec((tk, CH), index)

    return _pcall(
        body, name="w_in_grad", grid=(nchunks, T // tk),
        in_specs=[sec_spec(a) for a in range(nsec)]
        + [pl.BlockSpec((T, D), lambda s, t: (0, 0), pipeline_mode=pl.Buffered(1))],
        out_specs=pl.BlockSpec((CH, D), lambda s, t: (s, 0)), out_shape=_sds((D_IN, D), bf16),
        scratch_shapes=[pltpu.VMEM((CH, D), f32)],
        compiler_params=_params(("arbitrary", "arbitrary")),
    )(*dsecs, h_bf)


SMALL_NAMES = ("lru_w_a", "lru_w_x", "conv_b", "lru_b_a", "lru_b_x", "lru_lambda", "norm_g", "final_norm_g",
               "attn_sinks", "conv_w")
MISC_ROW = {"conv_b": 0, "lru_b_a": 1, "lru_b_x": 2, "lru_lambda": 3, "norm_g": 8, "final_norm_g": 16,
            "attn_sinks": 24, "loss": 32}


def _small_step(gwa, gwx, gvec, gnorm_blk, gfin_blk, dsink_blk, loss_blk, gcw, params):
    srcs_rows = (RB // NDEV, RB // NDEV, 8, 8)
    flat = [t for n in SMALL_NAMES for t in params[n]]
    nin = 8 + len(flat)
    nout = 4 * len(SMALL_NAMES) + 1

    def body(*refs):
        gwa_ref, gwx_ref, gvec_ref, gnorm_ref, gfin_ref, dsink_ref, loss_ref, gcw_ref = refs[:8]
        prm = {n: refs[8 + 3 * k:11 + 3 * k] for k, n in enumerate(SMALL_NAMES)}
        outs = {n: refs[nin + 4 * k:nin + 4 * k + 4] for k, n in enumerate(SMALL_NAMES)}
        loss_out = refs[nin + nout - 1]
        (misc, got_a, got_x, got_m, got_c, red_a, red_x, red_m, all_a, all_x, all_m,
         sa, ra, sb, rb) = refs[nin + nout:]
        x, y, c = _my_place()
        me = 4 * x + 2 * y + c

        misc[...] = jnp.zeros_like(misc)
        misc[0:8, :] = gvec_ref[...]
        misc[8:16, :] = gnorm_ref[...]
        misc[16:24, :] = gfin_ref[...]
        misc[24:32, 0:128] = dsink_ref[...]
        misc[32:40, :] = loss_ref[...]

        srcs = (gwa_ref, gwx_ref, misc, gcw_ref)
        gots = (got_a, got_x, got_m, got_c)

        def shard(ref, rows, dev):
            return ref.at[pl.ds(pl.multiple_of(dev * rows, 8), rows), :]

        scatter = []
        for k in range(1, NDEV):
            px, py, pc = _peer(k)
            for a in range(4):
                scatter.append(pltpu.make_async_remote_copy(
                    src_ref=shard(srcs[a], srcs_rows[a], 4 * px + 2 * py + pc), dst_ref=gots[a].at[k - 1],
                    send_sem=sa.at[4 * (k - 1) + a], recv_sem=ra.at[4 * (k - 1) + a],
                    device_id=(px, py, pc), device_id_type=MESH))
        for cp in scatter:
            cp.start()
        for cp in scatter:
            cp.wait()

        def reduced(a):
            rows = srcs_rows[a]
            total = srcs[a][pl.ds(pl.multiple_of(me * rows, 8), rows), :]
            for k in range(NDEV - 1):
                total = total + gots[a][k]
            return total

        reds = (red_a, red_x, red_m)
        alls = (all_a, all_x, all_m)
        for a in range(3):
            val = reduced(a)
            reds[a][...] = val
            alls[a][pl.ds(pl.multiple_of(me * srcs_rows[a], 8), srcs_rows[a]), :] = val
        gather = []
        for k in range(1, NDEV):
            peer = _peer(k)
            for a in range(3):
                gather.append(pltpu.make_async_remote_copy(
                    src_ref=reds[a], dst_ref=shard(alls[a], srcs_rows[a], me),
                    send_sem=sb.at[3 * (k - 1) + a], recv_sem=rb.at[3 * (k - 1) + a],
                    device_id=peer, device_id_type=MESH))
        for cp in gather:
            cp.start()
        g_conv = reduced(3)[0:4, :]
        for cp in gather:
            cp.wait()

        def update(name, g, pick=lambda r: r[...]):
            w_ref, m_ref, v_ref = prm[name]
            delta, m_new, v_new = _adam_math(g, pick(w_ref), pick(m_ref), pick(v_ref))
            return g, delta, m_new, v_new

        for n in range(RNN_BLOCKS):
            lanes = slice(RB * n, RB * (n + 1))
            for name, full in (("lru_w_a", all_a), ("lru_w_x", all_x)):
                for out, val in zip(outs[name], update(name, full[:, lanes], pick=lambda r, n=n: r[n])):
                    out[n] = val
        for name in ("conv_b", "lru_b_a", "lru_b_x", "lru_lambda", "norm_g", "final_norm_g"):
            row = MISC_ROW[name]
            for out, val in zip(outs[name], update(name, all_m[row:row + 1, :])):
                out[...] = val
        row = MISC_ROW["attn_sinks"]
        for out, val in zip(outs["attn_sinks"], update("attn_sinks", all_m[row:row + 1, 0:16])):
            out[...] = val
        for out, val in zip(outs["conv_w"], update("conv_w", g_conv)):
            out[...] = val
        row = MISC_ROW["loss"]
        loss_out[...] = all_m[row:row + 8, 0:128] * (0.5 / D)

    out_shape = tuple(_sds(params[n][0].shape, f32) for n in SMALL_NAMES for _ in range(4)) + (_sds((8, 128), f32),)
    scratch = [pltpu.VMEM((64, D), f32),
               pltpu.VMEM((NDEV - 1, RB // NDEV, D), f32), pltpu.VMEM((NDEV - 1, RB // NDEV, D), f32),
               pltpu.VMEM((NDEV - 1, 8, D), f32), pltpu.VMEM((NDEV - 1, 8, RB), f32),
               pltpu.VMEM((RB // NDEV, D), f32), pltpu.VMEM((RB // NDEV, D), f32), pltpu.VMEM((8, D), f32),
               pltpu.VMEM((RB, D), f32), pltpu.VMEM((RB, D), f32), pltpu.VMEM((64, D), f32),
               pltpu.SemaphoreType.DMA((4 * (NDEV - 1),)), pltpu.SemaphoreType.DMA((4 * (NDEV - 1),)),
               pltpu.SemaphoreType.DMA((3 * (NDEV - 1),)), pltpu.SemaphoreType.DMA((3 * (NDEV - 1),))]
    res = _pcall(
        body, name="small_step", out_shape=out_shape,
        in_specs=[VMEM_SPEC] * nin, out_specs=tuple([VMEM_SPEC] * nout),
        scratch_shapes=scratch, compiler_params=_params(),
    )(gwa, gwx, gvec, gnorm_blk, gfin_blk, dsink_blk, loss_blk, gcw, *flat)
    return {n: res[4 * k:4 * k + 4] for k, n in enumerate(SMALL_NAMES)}, res[-1]


def _pad_rows(v, rows=8):
    return jnp.concatenate([v, jnp.zeros((rows - v.shape[0], v.shape[1]), v.dtype)], axis=0)


def kernel(x, norm_g, w_in, conv_w, conv_b, lru_w_a, lru_b_a, lru_w_x, lru_b_x, lru_lambda, attn_sinks, w_rnn_out, w_attn_out, w_o, final_norm_g, loss_target, m_norm_g, m_w_in, m_conv_w, m_conv_b, m_lru_w_a, m_lru_b_a, m_lru_w_x, m_lru_b_x, m_lru_lambda, m_attn_sinks, m_w_rnn_out, m_w_attn_out, m_w_o, m_final_norm_g, v_norm_g, v_w_in, v_conv_w, v_conv_b, v_lru_w_a, v_lru_b_a, v_lru_w_x, v_lru_b_x, v_lru_lambda, v_attn_sinks, v_w_rnn_out, v_w_attn_out, v_w_o, v_final_norm_g):
    nb, S, _ = x.shape
    T = nb * S
    x2d = x.reshape(T, D)
    tgt = loss_target.reshape(T, D)
    fin_g = final_norm_g.reshape(1, D)
    w_a3, w_x3 = lru_w_a[0], lru_w_x[0]

    my_core = lax.axis_index("c").astype(jnp.int32).reshape(1)
    cx, cy = lax.axis_index("x"), lax.axis_index("y")
    chip_order = jnp.stack([2 * cx + cy, 2 * (1 - cx) + cy, 2 * cx + (1 - cy),
                            2 * (1 - cx) + (1 - cy)]).astype(jnp.int32)

    tabs = _rope_tables(S)
    h_bf, proj, wt_full, cw_full, _ = _in_proj_gather(
        x2d, norm_g, w_in[0].T.astype(bf16), _pad_rows(conv_w[0]), tabs, S, (), chip_order)
    w_out3 = jnp.stack([w_rnn_out[0], w_attn_out[0], w_o[0]]).astype(bf16)
    y_rnn, h_all, g_sems, w_out3, w_land = _lru_forward(proj, cw_full, conv_b, w_a3, lru_b_a, w_x3, lru_b_x,
                                                        lru_lambda, S, w_out3)
    y_attn, w_land = _attn_forward(proj, attn_sinks, S, g_sems, w_out3, w_land)
    w_land = lax.dynamic_update_slice(w_land, w_out3[:, None], (0, 4 * cx + 2 * cy + lax.axis_index("c"), 0, 0))
    w_land = pltpu.with_memory_space_constraint(w_land.reshape(3, D, D), pltpu.HBM)

    (dx2, dy_rnn, dy_attn, dmr, dma, loss_blk, gfin_blk, g_wr, g_wa, g_wo) = _merge_and_head(
        x2d, tgt, proj, y_rnn, y_attn, w_land, fin_g)
    sums_out = _pair_sums([g_wr, g_wa, g_wo], bf16, my_core, "out")

    (dq, dkv, dga, dsink_blk), (p_wr, p_wa, p_wo) = _attn_backward(proj, dy_attn, tabs, attn_sinks, S, sums_out)
    du0, dgr, gwa, gwx, gvec, gcw = _lru_backward(proj, h_all, dy_rnn, cw_full, conv_b, w_a3, lru_b_a, w_x3,
                                                  lru_b_x, lru_lambda, S)
    dsecs = (du0, dgr, dq, dkv, dga, dmr, dma)

    g_wt = _w_in_grad(dsecs, h_bf)
    (sum_in,) = _pair_sums([g_wt], bf16, my_core, "in")
    ex_sems, sum_in, landing, token = _exchange_start(sum_in)
    grad_x2d, gnorm_blk = _input_grad(dsecs, wt_full, x2d, dx2, norm_g + token[0, 0])
    p_wt = _exchange_wait(ex_sems, sum_in, landing, gnorm_blk)
    p_wt_own = lax.dynamic_index_in_dim(sum_in, 2 * cx + cy, axis=0, keepdims=False)

    small, loss_out = _small_step(gwa, gwx, gvec, gnorm_blk, gfin_blk, dsink_blk, loss_blk, gcw, {
        "lru_w_a": (w_a3, m_lru_w_a[0], v_lru_w_a[0]), "lru_w_x": (w_x3, m_lru_w_x[0], v_lru_w_x[0]),
        "conv_b": (conv_b, m_conv_b, v_conv_b), "lru_b_a": (lru_b_a, m_lru_b_a, v_lru_b_a),
        "lru_b_x": (lru_b_x, m_lru_b_x, v_lru_b_x), "lru_lambda": (lru_lambda, m_lru_lambda, v_lru_lambda),
        "norm_g": (norm_g, m_norm_g, v_norm_g),
        "final_norm_g": (fin_g, m_final_norm_g.reshape(1, D), v_final_norm_g.reshape(1, D)),
        "attn_sinks": (attn_sinks, m_attn_sinks, v_attn_sinks),
        "conv_w": (conv_w[0], m_conv_w[0], v_conv_w[0])})

    o_wt = _adamw(p_wt_own, p_wt, w_in[0].T, m_w_in[0].T, v_w_in[0].T, "adamw_w_in")
    o_wr, o_wa, o_wo = _adamw_group(
        (p_wr, p_wa, p_wo), (w_rnn_out[0], w_attn_out[0], w_o[0]),
        (m_w_rnn_out[0], m_w_attn_out[0], m_w_o[0]), (v_w_rnn_out[0], v_w_attn_out[0], v_w_o[0]), "adamw_w_out")

    def result(kind):
        d = {n: small[n][kind] for n in ("conv_b", "lru_b_a", "lru_b_x", "lru_lambda", "norm_g", "attn_sinks")}
        d.update({n: small[n][kind][None] for n in ("lru_w_a", "lru_w_x", "conv_w")})
        d["final_norm_g"] = small["final_norm_g"][kind].reshape(D)
        d.update({"w_in": o_wt[kind].T[None], "w_rnn_out": o_wr[kind][None], "w_attn_out": o_wa[kind][None],
                  "w_o": o_wo[kind][None]})
        return d

    order = ("norm_g", "w_in", "conv_w", "conv_b", "lru_w_a", "lru_b_a", "lru_w_x", "lru_b_x", "lru_lambda",
             "attn_sinks", "w_rnn_out", "w_attn_out", "w_o", "final_norm_g")
    outs = [loss_out[0, 0], grad_x2d.reshape(nb, S, D)]
    for kind in range(4):
        d = result(kind)
        outs += [d[n] for n in order]
    return tuple(outs)
```

```python
import functools
import math

import jax
import jax.numpy as jnp
from jax import lax
from jax.experimental import pallas as pl
from jax.experimental.pallas import tpu as pltpu

f32 = jnp.float32
bf16 = jnp.bfloat16

D = 1024
D_IN = 6656
NDEV = 8
RNN_BLOCKS = 8
RB = 128
HEAD = 64
KV_HEADS = 4
GROUP = 4
QB = 128
LRU_C = 8.0
EPS = 1e-6
ROPE_DIM = 16
ROPE_THETA = 500000.0
CH = 512
SEC_START = (0, 2, 4, 6, 7, 9, 11)
SEC_CHUNKS = (2, 2, 2, 1, 2, 2, 2)
VMEM_LIMIT = 62 * 1024 * 1024

ADAM_LR, ADAM_B1, ADAM_B2, ADAM_EPS, ADAM_WD, ADAM_STEP = 0.001, 0.9, 0.999, 1e-08, 0.01, 10

MESH = pl.DeviceIdType.MESH
ANY = pl.BlockSpec(memory_space=pl.ANY)
VMEM_SPEC = pl.BlockSpec(memory_space=pltpu.VMEM)
SMEM_SPEC = pl.BlockSpec(memory_space=pltpu.SMEM)


def _pcall(body, **kw):
    return pl.pallas_call(body, **kw)


def _params(sem=None, **kw):
    if sem is not None:
        kw["dimension_semantics"] = sem
    return pltpu.CompilerParams(vmem_limit_bytes=VMEM_LIMIT, **kw)


def _sds(shape, dtype):
    return jax.ShapeDtypeStruct(shape, dtype)


def _dot(a, b, dims):
    return lax.dot_general(a, b, (dims, ((), ())), preferred_element_type=f32)


NN = ((1,), (0,))
NT = ((1,), (1,))
TN = ((0,), (0,))


def _sigmoid(v):
    return 0.5 * jnp.tanh(0.5 * v) + 0.5


def _sigmoid_positive(v):
    return 1.0 / (1.0 + jnp.exp(-v))


def _my_place():
    return lax.axis_index("x"), lax.axis_index("y"), lax.axis_index("c")


def _peer(k):
    x, y, c = _my_place()
    return (x + ((k >> 2) & 1)) % 2, (y + ((k >> 1) & 1)) % 2, (c + (k & 1)) % 2


def _direct_gather_copies(srcs, outs, send_sems, recv_sems, local_sems):
    x, y, c = _my_place()
    me = 4 * x + 2 * y + c
    local, remote = [], []
    for a, (src, out) in enumerate(zip(srcs, outs)):
        r = src.shape[0]
        mine = out.at[pl.ds(pl.multiple_of(me * r, 8), r), :]
        local.append(pltpu.make_async_copy(src, mine, local_sems.at[a]))
        for k in range(1, NDEV):
            remote.append(pltpu.make_async_remote_copy(
                src_ref=src, dst_ref=mine, send_sem=send_sems.at[7 * a + k - 1], recv_sem=recv_sems.at[7 * a + k - 1],
                device_id=_peer(k), device_id_type=MESH))
    return local, remote


def _chip_exchange_copies(src, dst, send_sems, recv_sems, local_sems):
    x, y, c = _my_place()
    local, remote = [], []
    for a in range(len(src)):
        local.append(pltpu.make_async_copy(src[a].at[2 * x + y], dst[a].at[0], local_sems.at[a]))
    for k in (3, 1, 2):
        px, py = (x + (k >> 1)) % 2, (y + (k & 1)) % 2
        for a in range(len(src)):
            remote.append(pltpu.make_async_remote_copy(
                src_ref=src[a].at[2 * px + py], dst_ref=dst[a].at[k],
                send_sem=send_sems.at[3 * a + k - 1], recv_sem=recv_sems.at[3 * a + k - 1],
                device_id=(px, py, c), device_id_type=MESH))
    return local, remote


def _exchange_scratch(narr, per_array):
    return [pltpu.SemaphoreType.DMA((per_array * narr,)), pltpu.SemaphoreType.DMA((per_array * narr,)),
            pltpu.SemaphoreType.DMA((narr,))]


def _start_all(copies):
    local, remote = copies
    for cp in local + remote:
        cp.start()


def _wait_all(copies):
    local, remote = copies
    for cp in remote + local:
        cp.wait()


def _pair_exchange(grads, name):
    narr = len(grads)
    nrows = tuple(g.shape[0] // NDEV for g in grads)
    views = [g.reshape(4, 2, r, g.shape[1]) for g, r in zip(grads, nrows)]

    def body(*refs):
        gin = refs[:narr]
        got = refs[narr:2 * narr]
        send_sems, recv_sems = refs[2 * narr:]
        x, y, c = _my_place()
        copies = [pltpu.make_async_remote_copy(
            src_ref=gin[a].at[:, pl.ds(1 - c, 1)], dst_ref=got[a],
            send_sem=send_sems.at[a], recv_sem=recv_sems.at[a],
            device_id=(x, y, 1 - c), device_id_type=MESH) for a in range(narr)]
        for cp in copies:
            cp.start()
        for cp in copies:
            cp.wait()

    out_shape = tuple(_sds((4, 1, r, g.shape[1]), g.dtype) for r, g in zip(nrows, grads))
    got = _pcall(
        body, name=name, out_shape=out_shape,
        in_specs=[ANY] * narr, out_specs=tuple([ANY] * narr),
        scratch_shapes=[pltpu.SemaphoreType.DMA((narr,)), pltpu.SemaphoreType.DMA((narr,))],
        compiler_params=_params(),
    )(*views)
    return views, [g.reshape(4, r, g.shape[3]) for g, r in zip(got, nrows)]


def _row_tile(rows, dtype):
    unit = 16 if dtype == bf16 else 8
    for cand in (256, 208, 128, 64, 40, 32, 16, 8):
        if rows % cand == 0 and cand % unit == 0:
            return cand
    return rows


def _chip_sum(views, gots, my_core, out_dtype, name):
    narr = len(views)
    _, _, r, cols = views[0].shape
    tr = _row_tile(r, out_dtype)

    def body(core_ref, *refs):
        for a in range(narr):
            mine_ref, got_ref, out_ref = refs[a], refs[narr + a], refs[2 * narr + a]
            out_ref[...] = (mine_ref[...].astype(f32) + got_ref[...].astype(f32)).astype(out_dtype)

    slab = pl.BlockSpec((None, tr, cols), lambda q, i, core: (q, i, 0))
    grid_spec = pltpu.PrefetchScalarGridSpec(
        num_scalar_prefetch=1, grid=(4, r // tr),
        in_specs=[pl.BlockSpec((None, None, tr, cols), lambda q, i, core: (q, core[0], i, 0))] * narr + [slab] * narr,
        out_specs=tuple([slab] * narr))
    return _pcall(body, name=name, grid_spec=grid_spec,
                  out_shape=tuple(_sds((4, r, cols), out_dtype) for _ in range(narr)),
                  compiler_params=_params(("arbitrary", "arbitrary")))(my_core, *views, *gots)


def _pair_sums(grads, wire_dtype, my_core, tag):
    views, got = _pair_exchange(grads, "pair_exchange_" + tag)
    return _chip_sum(views, got, my_core, wire_dtype, "chip_sum_" + tag)


def _adam_math(g, w, m, v):
    m_new = ADAM_B1 * m + (1.0 - ADAM_B1) * g
    v_new = ADAM_B2 * v + (1.0 - ADAM_B2) * (g * g)
    m_hat = m_new / (1.0 - ADAM_B1 ** ADAM_STEP)
    v_hat = v_new / (1.0 - ADAM_B2 ** ADAM_STEP)
    return -ADAM_LR * (m_hat / (jnp.sqrt(v_hat) + ADAM_EPS) + ADAM_WD * w), m_new, v_new


def _adamw(first, parts, w, m, v, name):
    n, rows, cols = parts.shape
    tr = _row_tile(rows, parts.dtype)

    def body(f_ref, p_ref, w_ref, m_ref, v_ref, g_out, d_out, m_out, v_out):
        g = f_ref[...].astype(f32)
        for s in range(n):
            g = g + p_ref[s].astype(f32)
        g_out[...] = g
        d_out[...], m_out[...], v_out[...] = _adam_math(g, w_ref[...], m_ref[...], v_ref[...])

    blk = pl.BlockSpec((tr, cols), lambda i: (i, 0))
    return _pcall(
        body, name=name, grid=(rows // tr,),
        in_specs=[blk, pl.BlockSpec((n, tr, cols), lambda i: (0, i, 0)), blk, blk, blk],
        out_specs=(blk, blk, blk, blk), out_shape=tuple(_sds((rows, cols), f32) for _ in range(4)),
        compiler_params=_params(("arbitrary",)),
    )(first, parts, w, m, v)


def _adamw_group(parts, ws, ms, vs, name):
    nw = len(ws)

    def body(*refs):
        p_refs, w_refs, m_refs, v_refs = (refs[k * nw:(k + 1) * nw] for k in range(4))
        outs = refs[4 * nw:]
        for k in range(nw):
            g = p_refs[k][0].astype(f32)
            for s in range(1, p_refs[k].shape[0]):
                g = g + p_refs[k][s].astype(f32)
            g_out, d_out, m_out, v_out = outs[4 * k:4 * k + 4]
            g_out[...] = g
            d_out[...], m_out[...], v_out[...] = _adam_math(g, w_refs[k][...], m_refs[k][...], v_refs[k][...])

    res = _pcall(
        body, name=name, out_shape=tuple(_sds(w.shape, f32) for w in ws for _ in range(4)),
        in_specs=[VMEM_SPEC] * (4 * nw), out_specs=tuple([VMEM_SPEC] * (4 * nw)), compiler_params=_params(),
    )(*parts, *ws, *ms, *vs)
    return [res[4 * k:4 * k + 4] for k in range(nw)]


def _rope(t, c, s1, s2):
    w = t.shape[1]
    return t * c + pltpu.roll(t, w - 8, 1) * s1 + pltpu.roll(t, 8, 1) * s2


def _rope_transposed(dt, c, s1, s2):
    w = dt.shape[1]
    return dt * c + pltpu.roll(dt * s1, 8, 1) + pltpu.roll(dt * s2, w - 8, 1)


PAIR_ROWS = D_IN // 4
SUB_COLS = ((0, 512), (512, 512), (1024, 512), (1536, 128))
Q_SLABS = range(3, 11)
K_SLABS = range(11, 13)


def _in_proj_gather(x2d, norm_g, wt_shard, cw_shard, tabs, S, out_shards, chip_order):
    T = x2d.shape[0]
    tb = min(S, 1024)
    ntok = T // tb
    nsb = S // tb
    q_scale = 1.0 / math.sqrt(HEAD)
    shard_rows = wt_shard.shape[0]
    small = (cw_shard,) + tuple(out_shards)
    nsm = len(small)

    def body(order_ref, x_ref, g_ref, c_ref, s1_ref, s2_ref, wt_hbm, *rest):
        small_in = rest[:nsm]
        h_ref, proj_ref, wt_out = rest[nsm:nsm + 3]
        small_out = rest[nsm + 3:2 * nsm + 3]
        wt_vm, h_vm = rest[2 * nsm + 3:2 * nsm + 5]
        stage = rest[2 * nsm + 5:3 * nsm + 4]
        wsend, wrecv, wlocal = rest[3 * nsm + 4:3 * nsm + 7]
        dsems = rest[3 * nsm + 7:]
        jj, i = pl.program_id(0), pl.program_id(1)
        x, y, c = _my_place()
        me, sibling = (x, y, c), (x, y, 1 - c)
        chips = [(1 - x, y), (x, 1 - y), (1 - x, 1 - y)]

        def rows(place):
            px, py, pc = place
            return wt_vm.at[pl.ds(pl.multiple_of((4 * px + 2 * py + pc) * shard_rows, 16), shard_rows), :]

        def copy(k, block, to, src=None):
            return pltpu.make_async_remote_copy(
                src_ref=rows(block) if src is None else src, dst_ref=rows(block),
                send_sem=wsend.at[k], recv_sem=wrecv.at[k], device_id=to, device_id_type=MESH)

        def small_copies():
            srcs = (small_in[0],) + tuple(stage)
            return _direct_gather_copies(srcs, small_out, *dsems)

        own = pltpu.make_async_copy(wt_hbm, rows(me), wlocal.at[0])
        keep = pltpu.make_async_copy(wt_vm, wt_out, wlocal.at[1])

        @pl.when((jj == 0) & (i == 0))
        def _():
            own.start()
            copy(0, me, sibling, src=wt_hbm).start()
            for j, chip in enumerate(chips):
                copy(1 + j, me, (*chip, c), src=wt_hbm).start()
            for a in range(nsm - 1):
                stage[a][...] = small_in[1 + a][...].astype(bf16)
            _start_all(small_copies())
            own.wait()
            copy(0, sibling, me).wait_recv()

        for j, chip in enumerate(chips):
            @pl.when((jj == 1 + j) & (i == 0))
            def _(j=j, chip=chip):
                copy(1 + j, (*chip, c), me).wait_recv()
                copy(4 + j, (*chip, c), sibling).start()
                copy(4 + j, (*chip, 1 - c), me).wait_recv()

        @pl.when((jj == 3) & (i == 0))
        def _():
            keep.start()

        @pl.when((jj == 3) & (i == ntok - 1))
        def _():
            copy(0, me, sibling, src=wt_hbm).wait_send()
            for j, chip in enumerate(chips):
                copy(1 + j, me, (*chip, c), src=wt_hbm).wait_send()
                copy(4 + j, (*chip, c), sibling).wait_send()
            _wait_all(small_copies())
            keep.wait()

        tok = pl.ds(pl.multiple_of(i * tb, tb), tb)

        @pl.when(jj == 0)
        def _():
            xv = x_ref[...]
            ms = jnp.mean(xv * xv, axis=-1, keepdims=True)
            hb = (xv * lax.rsqrt(ms + EPS) * g_ref[...]).astype(bf16)
            h_ref[...] = hb
            h_vm[tok, :] = hb

        block = order_ref[jj]
        hb = h_vm[tok, :]

        def piece(c0, w):
            w_rows = wt_vm[pl.ds(pl.multiple_of(block * PAIR_ROWS + c0, 128), w), :]
            return _dot(hb, w_rows, NT)

        @pl.when(block != 1)
        def _():
            for c0, w in SUB_COLS:
                proj_ref[:, c0:c0 + w] = piece(c0, w).astype(bf16)

        @pl.when(block == 1)
        def _():
            tab = (c_ref[...], s1_ref[...], s2_ref[...])
            for c0, w in SUB_COLS:
                acc = piece(c0, w)
                for l in range(w // 128):
                    slab = (c0 + 128 * l) // 128
                    part = acc[:, 128 * l:128 * (l + 1)]
                    if slab in Q_SLABS:
                        part = _rope(part, *tab) * q_scale
                    elif slab in K_SLABS:
                        part = _rope(part, *tab)
                    proj_ref[:, 128 * slab:128 * (slab + 1)] = part.astype(bf16)

    first_pass = lambda jj, i, order: (jnp.where(jj == 0, i, ntok - 1), 0)
    const = lambda jj, i, order: (0, 0)
    tab = pl.BlockSpec((tb, 128), lambda jj, i, order: (jnp.where(order[jj] == 1, i % nsb, 0), 0))
    grid_spec = pltpu.PrefetchScalarGridSpec(
        num_scalar_prefetch=1, grid=(4, ntok),
        in_specs=[pl.BlockSpec((tb, D), first_pass), pl.BlockSpec((1, D), const), tab, tab, tab, ANY]
        + [pl.BlockSpec(w.shape, const) for w in small],
        out_specs=(pl.BlockSpec((tb, D), first_pass),
                   pl.BlockSpec((tb, PAIR_ROWS), lambda jj, i, order: (i, order[jj])), ANY) + tuple([ANY] * nsm),
        scratch_shapes=[pltpu.VMEM((D_IN, D), bf16), pltpu.VMEM((T, D), bf16)]
        + [pltpu.VMEM(w.shape, bf16) for w in out_shards]
        + [pltpu.SemaphoreType.DMA((7,)), pltpu.SemaphoreType.DMA((7,)), pltpu.SemaphoreType.DMA((2,))]
        + _exchange_scratch(nsm, 7))
    res = _pcall(
        body, name="in_proj", grid_spec=grid_spec,
        out_shape=(_sds((T, D), bf16), _sds((T, D_IN), bf16), _sds((D_IN, D), bf16),
                   _sds((NDEV * cw_shard.shape[0], cw_shard.shape[1]), f32))
        + tuple(_sds((NDEV * w.shape[0], w.shape[1]), bf16) for w in out_shards),
        compiler_params=_params(("arbitrary", "arbitrary")),
    )(chip_order, x2d, norm_g, *tabs, wt_shard, *small)
    return res[0], res[1], res[2], res[3], res[4:]


def _rows_iota(shape):
    return lax.broadcasted_iota(jnp.int32, shape, 0)


def _shift_down(v, k):
    return jnp.where(_rows_iota(v.shape) >= k, pltpu.roll(v, k, 0), 0.0)


def _shift_up(v, k):
    n = v.shape[0]
    return jnp.where(_rows_iota(v.shape) < n - k, pltpu.roll(v, n - k, 0), 0.0)


def _linear_scan(a, b, a_s, b_s, edge_s, out_ref, reverse):
    n = a.shape[0]
    ng = n // 8
    a3, b3 = a.reshape(ng, 8, RB), b.reshape(ng, 8, RB)
    rid = lax.broadcasted_iota(jnp.int32, a3.shape, 1)
    for s in (1, 2, 4):
        keep, shift = (rid < 8 - s, 8 - s) if reverse else (rid >= s, s)
        b3 = jnp.where(keep, a3 * pltpu.roll(b3, shift, 1) + b3, b3)
        a3 = jnp.where(keep, a3 * pltpu.roll(a3, shift, 1), a3)
    a_s[...] = a3.reshape(n, RB)
    b_s[...] = b3.reshape(n, RB)
    edge = 0 if reverse else 7
    ea, eb = a_s[pl.ds(edge, ng, stride=8), :], b_s[pl.ds(edge, ng, stride=8), :]
    r = _rows_iota(ea.shape)
    s = 1
    while s < ng:
        keep, shift = (r < ng - s, ng - s) if reverse else (r >= s, s)
        eb = jnp.where(keep, ea * pltpu.roll(eb, shift, 0) + eb, eb)
        if 2 * s < ng:
            ea = jnp.where(keep, ea * pltpu.roll(ea, shift, 0), ea)
        s *= 2
    edge_s[...] = _shift_up(eb, 1) if reverse else _shift_down(eb, 1)

    def eight_groups(i, carry):
        for k in range(8):
            j = i * 8 + k
            rows = pl.ds(pl.multiple_of(j * 8, 8), 8)
            out_ref[rows, :] = b_s[rows, :] + a_s[rows, :] * edge_s[pl.ds(j, 1), :]
        return carry

    lax.fori_loop(0, ng // 8, eight_groups, 0)


def _neg_expm1(v):
    series = -v * (1.0 + v * (0.5 + v * (1.0 / 6.0)))
    return jnp.where(v > -0.015625, series, 1.0 - jnp.exp(v))


def _softplus_neg(lam):
    return jnp.maximum(-lam, 0.0) + jnp.log(1.0 + jnp.exp(-jnp.abs(lam)))


def _lru_gates(x0, cw, cb, wa, ba, wx, bx, lam):
    taps = [_shift_down(x0, 3 - k) for k in range(3)] + [x0]
    u = cb + cw[3:4, :] * x0
    for k in range(3):
        u = u + cw[k:k + 1, :] * taps[k]
    ub = u.astype(bf16)
    r = _sigmoid_positive(_dot(ub, wa.astype(bf16), NN) + ba)
    i = _sigmoid(_dot(ub, wx.astype(bf16), NN) + bx)
    sp = _softplus_neg(lam)
    log_a = (-LRU_C) * r * sp
    a = jnp.exp(log_a)
    w = _neg_expm1(2.0 * log_a)
    inv_mult = lax.rsqrt(w)
    return u, ub, r, i, sp, a, w * inv_mult, inv_mult, taps


def _lru_specs(S, nb):
    col = lambda off: pl.BlockSpec((S, RB), lambda n, b, off=off: (b, off + n))
    vec = pl.BlockSpec((1, RB), lambda n, b: (0, n))
    wblk = pl.BlockSpec((None, RB, RB), lambda n, b: (n, 0, 0))
    cwblk = pl.BlockSpec((8, RB), lambda n, b: (n, 0))
    return col, vec, wblk, cwblk


def _lru_forward(proj, cw_full, conv_b, w_a, b_a, w_x, b_x, lam, S):
    T = proj.shape[0]
    nb = T // S
    col, vec, wblk, cwblk = _lru_specs(S, nb)

    def body(x0_ref, g_ref, cw_ref, cb_ref, wa_ref, ba_ref, wx_ref, bx_ref, lam_ref, y_ref, h_ref, a_s, b_s, edge_s):
        x0 = x0_ref[...].astype(f32)
        u, ub, r, i, sp, a, mult, _, _ = _lru_gates(x0, cw_ref[...], cb_ref[...], wa_ref[...], ba_ref[...],
                                                    wx_ref[...], bx_ref[...], lam_ref[...])
        _linear_scan(a, mult * (i * u), a_s, b_s, edge_s, h_ref, reverse=False)
        g = g_ref[...].astype(f32)
        y_ref[...] = (h_ref[...] * (g * _sigmoid(g))).astype(bf16)

    out = pl.BlockSpec((S, RB), lambda n, b: (b, n))
    return _pcall(
        body, name="lru_forward", grid=(RNN_BLOCKS, nb),
        in_specs=[col(0), col(8), cwblk, vec, wblk, vec, wblk, vec, vec],
        out_specs=(out, out), out_shape=(_sds((T, D), bf16), _sds((T, D), f32)),
        scratch_shapes=[pltpu.VMEM((S, RB), f32), pltpu.VMEM((S, RB), f32), pltpu.VMEM((S // 8, RB), f32)],
        compiler_params=_params(("arbitrary", "arbitrary")),
    )(proj, proj, cw_full, conv_b, w_a, b_a, w_x, b_x, lam)


def _rope_tables(S):
    pos = jnp.arange(S, dtype=f32)
    inv_freq = ROPE_THETA ** (-jnp.arange(0, ROPE_DIM, 2, dtype=f32) / ROPE_DIM)
    ang = pos[:, None] * inv_freq[None, :]
    cos, sin = jnp.cos(ang), jnp.sin(ang)
    lane = jnp.arange(128) % HEAD
    cosl, sinl = cos[:, lane % 8], sin[:, lane % 8]
    c = jnp.where(lane[None, :] < ROPE_DIM, cosl, 1.0)
    s1 = jnp.where(lane[None, :] < 8, -sinl, 0.0)
    s2 = jnp.where((lane[None, :] >= 8) & (lane[None, :] < ROPE_DIM), sinl, 0.0)
    return c.astype(f32), s1.astype(f32), s2.astype(f32)


def _heads_to_rows(t):
    return jnp.concatenate([t[:, HEAD * h:HEAD * (h + 1)] for h in range(GROUP)], axis=0)


def _rows_to_heads(t):
    return jnp.concatenate([t[QB * h:QB * (h + 1), :] for h in range(GROUP)], axis=1)


def _window_bias(first_block):
    shape = (GROUP * QB, 2 * QB)
    qi = _rows_iota(shape) % QB
    cj = lax.broadcasted_iota(jnp.int32, shape, 1)
    valid = (cj > qi) & (cj <= qi + QB) & ((cj >= QB) | jnp.logical_not(first_block))
    return jnp.where(valid, 0.0, -jnp.inf)


def _attn_probs(q_rows, k_cat, sink_col, bias):
    s = _dot(q_rows, k_cat, NT) + bias
    m = jnp.maximum(jnp.max(s, axis=1, keepdims=True), sink_col)
    p = jnp.exp(s - m)
    e_sink = jnp.exp(sink_col - m)
    inv = 1.0 / (jnp.sum(p, axis=1, keepdims=True) + e_sink)
    return p * inv, e_sink * inv


def _sink_column(sink_ref, kv):
    rid = _rows_iota((GROUP * QB, 1))
    col = jnp.zeros((GROUP * QB, 1), f32)
    for h in range(GROUP):
        col = jnp.where(rid // QB == h, sink_ref[0, GROUP * kv + h], col)
    return col


def _attn_in_specs(S):
    nq = S // QB
    last = nq - 1
    cur = lambda b, j: b * nq + jnp.minimum(j, last)
    prev = lambda b, j: b * nq + jnp.maximum(jnp.minimum(j, last) - 1, 0)
    specs = [
        pl.BlockSpec((QB, D), lambda b, j: (cur(b, j), 2)),
        pl.BlockSpec((QB, 256), lambda b, j: (cur(b, j), 12)),
        pl.BlockSpec((QB, 256), lambda b, j: (prev(b, j), 12)),
        pl.BlockSpec((QB, 256), lambda b, j: (cur(b, j), 13)),
        pl.BlockSpec((QB, 256), lambda b, j: (prev(b, j), 13)),
        pl.BlockSpec((QB, 512), lambda b, j: (cur(b, j), 7)),
        pl.BlockSpec((QB, 512), lambda b, j: (cur(b, j), 8)),
        SMEM_SPEC,
    ]
    return specs, cur, prev


def _attn_forward(proj, sinks, S, out_shards):
    T = proj.shape[0]
    nb, nq = T // S, S // QB
    specs, cur, _ = _attn_in_specs(S)
    nw = len(out_shards)

    def body(q_ref, kc_ref, kp_ref, vc_ref, vp_ref, gl_ref, gh_ref, sink_ref, *rest):
        shards = rest[:nw]
        y_ref = rest[nw]
        gathered = rest[nw + 1:2 * nw + 1]
        stage = rest[2 * nw + 1:3 * nw + 1]
        sems = rest[3 * nw + 1:]
        b, j = pl.program_id(0), pl.program_id(1)

        @pl.when((b == 0) & (j == 0))
        def _():
            for a in range(nw):
                stage[a][...] = shards[a][...].astype(bf16)
            _start_all(_direct_gather_copies(stage, gathered, *sems))

        @pl.when((b == nb - 1) & (j == nq - 1))
        def _():
            _wait_all(_direct_gather_copies(stage, gathered, *sems))

        bias = _window_bias(j == 0)
        kc, kp, vc, vp = kc_ref[...], kp_ref[...], vc_ref[...], vp_ref[...]
        for kv in range(KV_HEADS):
            lanes = slice(256 * kv, 256 * (kv + 1))
            hl = slice(HEAD * kv, HEAD * (kv + 1))
            q_rows = _heads_to_rows(q_ref[:, lanes])
            k_cat = jnp.concatenate([kp[:, hl], kc[:, hl]], axis=0)
            v_cat = jnp.concatenate([vp[:, hl], vc[:, hl]], axis=0)
            probs, _ = _attn_probs(q_rows, k_cat, _sink_column(sink_ref, kv), bias)
            o = _rows_to_heads(_dot(probs.astype(bf16), v_cat, NN))
            g_src = gl_ref if kv < 2 else gh_ref
            g = g_src[:, 256 * (kv % 2):256 * (kv % 2 + 1)].astype(f32)
            y_ref[:, lanes] = (o * (g * _sigmoid(g))).astype(bf16)

    args = [proj] * 7 + [sinks] + list(out_shards)
    res = _pcall(
        body, name="attn_forward", grid=(nb, nq),
        in_specs=specs + [pl.BlockSpec(w.shape, lambda b, j: (0, 0)) for w in out_shards],
        out_specs=(pl.BlockSpec((QB, D), lambda b, j: (cur(b, j), 0)),) + tuple([ANY] * nw),
        out_shape=(_sds((T, D), bf16),) + tuple(_sds((NDEV * w.shape[0], w.shape[1]), bf16) for w in out_shards),
        scratch_shapes=[pltpu.VMEM(w.shape, bf16) for w in out_shards] + _exchange_scratch(nw, 7),
        compiler_params=_params(("arbitrary", "arbitrary")),
    )(*args)
    return res[0], res[1:]


def _merge_and_head(x2d, tgt, proj, y_rnn, y_attn, w_r, w_a, w_o, gfin):
    T = x2d.shape[0]
    tb = min(T, 512)
    nsteps = T // tb

    def body(x_ref, t_ref, mr0, mr1, ma0, ma1, yr_ref, ya_ref, wr_ref, wa_ref, wo_ref, gf_ref,
             dx2_ref, dyr_ref, dya_ref, dmr_ref, dma_ref, loss_ref, gfin_ref, gwr_out, gwa_out, gwo_out,
             gwr_acc, gwa_acc, gwo_acc, out_sems):
        step = pl.program_id(0)

        @pl.when(step == 0)
        def _():
            loss_ref[...] = jnp.zeros_like(loss_ref)
            gfin_ref[...] = jnp.zeros_like(gfin_ref)
            gwr_acc[...] = jnp.zeros_like(gwr_acc)
            gwa_acc[...] = jnp.zeros_like(gwa_acc)
            gwo_acc[...] = jnp.zeros_like(gwo_acc)

        sr = _sigmoid(jnp.concatenate([mr0[...], mr1[...]], axis=1).astype(f32))
        sa = _sigmoid(jnp.concatenate([ma0[...], ma1[...]], axis=1).astype(f32))
        p_r = _dot(yr_ref[...], wr_ref[...], NN)
        p_a = _dot(ya_ref[...], wa_ref[...], NN)
        merged = (sr * p_r + sa * p_a).astype(bf16)
        x2 = x_ref[...] + _dot(merged, wo_ref[...], NN)
        rstd = lax.rsqrt(jnp.mean(x2 * x2, axis=-1, keepdims=True) + EPS)
        xh = x2 * rstd
        gf = gf_ref[...]
        err = xh * gf - t_ref[...]
        loss_ref[...] += jnp.sum(err * err)
        dy = err * (1.0 / D)
        gfin_ref[0:1, :] += jnp.sum(dy * xh, axis=0, keepdims=True)
        dxn = dy * gf
        dx2 = rstd * (dxn - xh * jnp.mean(dxn * xh, axis=-1, keepdims=True))
        dx2_ref[...] = dx2
        dx2b = dx2.astype(bf16)
        dmerged = _dot(dx2b, wo_ref[...], NT)
        dmr_ref[...] = (dmerged * p_r * (sr * (1.0 - sr))).astype(bf16)
        dma_ref[...] = (dmerged * p_a * (sa * (1.0 - sa))).astype(bf16)
        dpr = (dmerged * sr).astype(bf16)
        dpa = (dmerged * sa).astype(bf16)
        dyr_ref[...] = _dot(dpr, wr_ref[...], NT).astype(bf16)
        dya_ref[...] = _dot(dpa, wa_ref[...], NT).astype(bf16)
        gwr_acc[...] += _dot(yr_ref[...], dpr, TN)
        gwa_acc[...] += _dot(ya_ref[...], dpa, TN)
        gwo_acc[...] += _dot(merged, dx2b, TN)

        @pl.when(step == nsteps - 1)
        def _():
            copies = [pltpu.make_async_copy(src, dst, out_sems.at[k]) for k, (src, dst) in enumerate(
                ((gwr_acc, gwr_out), (gwa_acc, gwa_out), (gwo_acc, gwo_out)))]
            for cp in copies:
                cp.start()
            for cp in copies:
                cp.wait()

    tok = pl.BlockSpec((tb, D), lambda i: (i, 0))
    half = lambda c: pl.BlockSpec((tb, CH), lambda i, c=c: (i, c))
    wfull = pl.BlockSpec((D, D), lambda i: (0, 0), pipeline_mode=pl.Buffered(1))
    acc = pl.BlockSpec((8, D), lambda i: (0, 0))
    return _pcall(
        body, name="merge_and_head", grid=(nsteps,),
        in_specs=[tok, tok, half(9), half(10), half(11), half(12), tok, tok, wfull, wfull, wfull,
                  pl.BlockSpec((1, D), lambda i: (0, 0))],
        out_specs=(tok, tok, tok, tok, tok, acc, acc, ANY, ANY, ANY),
        out_shape=(_sds((T, D), f32), _sds((T, D), bf16), _sds((T, D), bf16), _sds((T, D), bf16),
                   _sds((T, D), bf16), _sds((8, D), f32), _sds((8, D), f32),
                   _sds((D, D), f32), _sds((D, D), f32), _sds((D, D), f32)),
        scratch_shapes=[pltpu.VMEM((D, D), f32)] * 3 + [pltpu.SemaphoreType.DMA((3,))],
        compiler_params=_params(("arbitrary",)),
    )(x2d, tgt, proj, proj, proj, proj, y_rnn, y_attn, w_r, w_a, w_o, gfin)


def _attn_backward(proj, dy_attn, tabs, sinks, S, chip_sums):
    T = proj.shape[0]
    nb, nq = T // S, S // QB
    nex = len(chip_sums)
    specs, cur, prev = _attn_in_specs(S)
    last = nq - 1
    tab_cur = pl.BlockSpec((QB, 128), lambda b, j: (jnp.minimum(j, last), 0))
    tab_prev = pl.BlockSpec((QB, 128), lambda b, j: (jnp.maximum(jnp.minimum(j, last) - 1, 0), 0))
    specs = specs + [pl.BlockSpec((QB, D), lambda b, j: (cur(b, j), 0))] + [tab_cur] * 3 + [tab_prev] * 3
    q_scale = 1.0 / math.sqrt(HEAD)

    def rope_back(dt, tab):
        return jnp.concatenate([_rope_transposed(dt[:, 128 * l:128 * (l + 1)], *tab) for l in range(2)], axis=1)

    def body(q_ref, kc_ref, kp_ref, vc_ref, vp_ref, gl_ref, gh_ref, sink_ref, dy_ref, cc, s1c, s2c, cp, s1p, s2p,
             *rest):
        ex_src = rest[:nex]
        dq_ref, dkv_ref, dg_ref, dsink_ref = rest[nex:nex + 4]
        ex_dst = rest[nex + 4:2 * nex + 4]
        carry_k, carry_v = rest[2 * nex + 4:2 * nex + 6]
        sems = rest[2 * nex + 6:]
        b, j = pl.program_id(0), pl.program_id(1)

        @pl.when((b == 0) & (j == 0))
        def _():
            dsink_ref[...] = jnp.zeros_like(dsink_ref)
            _start_all(_chip_exchange_copies(ex_src, ex_dst, *sems))

        @pl.when((b == nb - 1) & (j == nq))
        def _():
            _wait_all(_chip_exchange_copies(ex_src, ex_dst, *sems))

        @pl.when(j == 0)
        def _():
            carry_k[...] = jnp.zeros_like(carry_k)
            carry_v[...] = jnp.zeros_like(carry_v)

        @pl.when(j < nq)
        def _():
            bias = _window_bias(j == 0)
            tc = (cc[...], s1c[...], s2c[...])
            tp = (cp[...], s1p[...], s2p[...])
            kc, kp, vc, vp = kc_ref[...], kp_ref[...], vc_ref[...], vp_ref[...]
            dk_prev, dk_cur, dv_prev, dv_cur = [], [], [], []
            dsink_acc = jnp.zeros((8, 128), f32)
            r8 = lax.broadcasted_iota(jnp.int32, (8, 128), 0)
            l8 = lax.broadcasted_iota(jnp.int32, (8, 128), 1)
            for kv in range(KV_HEADS):
                lanes = slice(256 * kv, 256 * (kv + 1))
                hl = slice(HEAD * kv, HEAD * (kv + 1))
                q_rows = _heads_to_rows(q_ref[:, lanes])
                k_cat = jnp.concatenate([kp[:, hl], kc[:, hl]], axis=0)
                v_cat = jnp.concatenate([vp[:, hl], vc[:, hl]], axis=0)
                probs, p_sink = _attn_probs(q_rows, k_cat, _sink_column(sink_ref, kv), bias)
                pb = probs.astype(bf16)
                o = _rows_to_heads(_dot(pb, v_cat, NN))
                g_src = gl_ref if kv < 2 else gh_ref
                g = g_src[:, 256 * (kv % 2):256 * (kv % 2 + 1)].astype(f32)
                sg = _sigmoid(g)
                dy = dy_ref[:, lanes].astype(f32)
                dg_ref[:, lanes] = (dy * o * (sg * (1.0 + g * (1.0 - sg)))).astype(bf16)
                do_rows = _heads_to_rows(dy * (g * sg)).astype(bf16)
                dv = _dot(pb, do_rows, TN)
                dp = _dot(do_rows, v_cat, NT)
                rowdot = jnp.sum(probs * dp, axis=1, keepdims=True)
                ds = (probs * (dp - rowdot)).astype(bf16)
                sink_rows = -(p_sink * rowdot)
                for h in range(GROUP):
                    val = jnp.sum(sink_rows[QB * h:QB * (h + 1), :])
                    dsink_acc = dsink_acc + jnp.where((r8 == 0) & (l8 == GROUP * kv + h), val, 0.0)
                dq = _rows_to_heads(_dot(ds, k_cat, NN)) * q_scale
                dq_ref[:, lanes] = rope_back(dq, tc).astype(bf16)
                dk = _dot(ds, q_rows, TN)
                dk_prev.append(dk[:QB, :])
                dk_cur.append(dk[QB:, :])
                dv_prev.append(dv[:QB, :])
                dv_cur.append(dv[QB:, :])
            dsink_ref[...] += dsink_acc
            dkp = rope_back(jnp.concatenate(dk_prev, axis=1), tp)
            dkc = rope_back(jnp.concatenate(dk_cur, axis=1), tc)
            dkv_ref[:, 0:256] = (carry_k[...] + dkp).astype(bf16)
            dkv_ref[:, 256:512] = (carry_v[...] + jnp.concatenate(dv_prev, axis=1)).astype(bf16)
            carry_k[...] = dkc
            carry_v[...] = jnp.concatenate(dv_cur, axis=1)

        @pl.when(j == nq)
        def _():
            dkv_ref[:, 0:256] = carry_k[...].astype(bf16)
            dkv_ref[:, 256:512] = carry_v[...].astype(bf16)

    lag = lambda b, j: (b * nq + jnp.maximum(j - 1, 0), 0)
    args = [proj] * 7 + [sinks, dy_attn] + list(tabs) + list(tabs) + list(chip_sums)
    res = _pcall(
        body, name="attn_backward", grid=(nb, nq + 1), in_specs=specs + [ANY] * nex,
        out_specs=(pl.BlockSpec((QB, D), lambda b, j: (cur(b, j), 0)), pl.BlockSpec((QB, 512), lag),
                   pl.BlockSpec((QB, D), lambda b, j: (cur(b, j), 0)), pl.BlockSpec((8, 128), lambda b, j: (0, 0)))
        + tuple([ANY] * nex),
        out_shape=(_sds((T, D), bf16), _sds((T, 512), bf16), _sds((T, D), bf16), _sds((8, 128), f32))
        + tuple(_sds(s.shape, s.dtype) for s in chip_sums),
        scratch_shapes=[pltpu.VMEM((QB, 256), f32), pltpu.VMEM((QB, 256), f32)] + _exchange_scratch(nex, 3),
        compiler_params=_params(("arbitrary", "arbitrary")),
    )(*args)
    return res[:4], res[4:]


def _lru_backward(proj, h_all, dy_rnn, cw_full, conv_b, w_a, b_a, w_x, b_x, lam, S):
    T = proj.shape[0]
    nb = T // S
    col, vec, wblk, cwblk = _lru_specs(S, nb)
    tokblk = pl.BlockSpec((S, RB), lambda n, b: (b, n))

    def body(x0_ref, g_ref, h_ref, dy_ref, cw_ref, cb_ref, wa_ref, ba_ref, wx_ref, bx_ref, lam_ref,
             du0_ref, dg_ref, gwa_ref, gwx_ref, vec_ref, gcw_ref, a_s, b_s, dh_s, edge_s):
        @pl.when(pl.program_id(1) == 0)
        def _():
            gwa_ref[...] = jnp.zeros_like(gwa_ref)
            gwx_ref[...] = jnp.zeros_like(gwx_ref)
            vec_ref[...] = jnp.zeros_like(vec_ref)
            gcw_ref[...] = jnp.zeros_like(gcw_ref)

        x0 = x0_ref[...].astype(f32)
        cw = cw_ref[...]
        lam_v = lam_ref[...]
        u, ub, r, i, sp, a, mult, inv_mult, taps = _lru_gates(x0, cw, cb_ref[...], wa_ref[...], ba_ref[...],
                                                              wx_ref[...], bx_ref[...], lam_v)
        h = h_ref[...]
        g = g_ref[...].astype(f32)
        dy = dy_ref[...].astype(f32)
        sg = _sigmoid(g)
        dg_ref[...] = (dy * h * (sg * (1.0 + g * (1.0 - sg)))).astype(bf16)
        _linear_scan(_shift_up(a, 1), dy * (g * sg), a_s, b_s, edge_s, dh_s, reverse=True)
        dh_total = dh_s[...]
        da = dh_total * _shift_down(h, 1)
        dmult = dh_total * (i * u)
        db = dh_total * mult
        di = db * u
        du = db * i
        dlog_a_c = ((-LRU_C) * a) * (da - dmult * (a * inv_mult))
        dr = dlog_a_c * sp
        dsp = jnp.sum(dlog_a_c * r, axis=0, keepdims=True)
        dpre_r = dr * r * (1.0 - r)
        dpre_i = di * i * (1.0 - i)
        dpre_rb = dpre_r.astype(bf16)
        dpre_ib = dpre_i.astype(bf16)
        du = du + _dot(dpre_rb, wa_ref[...].astype(bf16), NT) + _dot(dpre_ib, wx_ref[...].astype(bf16), NT)
        gwa_ref[...] += _dot(ub, dpre_rb, TN)
        gwx_ref[...] += _dot(ub, dpre_ib, TN)
        vec_ref[0:1, :] += jnp.sum(du, axis=0, keepdims=True)
        vec_ref[1:2, :] += jnp.sum(dpre_r, axis=0, keepdims=True)
        vec_ref[2:3, :] += jnp.sum(dpre_i, axis=0, keepdims=True)
        vec_ref[3:4, :] += dsp * (-_sigmoid(-lam_v))
        dx0 = cw[3:4, :] * du
        for k in range(3):
            dx0 = dx0 + cw[k:k + 1, :] * _shift_up(du, 3 - k)
        for k in range(4):
            gcw_ref[k:k + 1, :] += jnp.sum(du * taps[k], axis=0, keepdims=True)
        du0_ref[...] = dx0.astype(bf16)

    wacc = pl.BlockSpec((RB, RB), lambda n, b: (0, n))
    vacc = pl.BlockSpec((8, RB), lambda n, b: (0, n))
    cacc = pl.BlockSpec((8, RB), lambda n, b: (n, 0))
    return _pcall(
        body, name="lru_backward", grid=(RNN_BLOCKS, nb),
        in_specs=[col(0), col(8), tokblk, tokblk, cwblk, vec, wblk, vec, wblk, vec, vec],
        out_specs=(tokblk, tokblk, wacc, wacc, vacc, cacc),
        out_shape=(_sds((T, D), bf16), _sds((T, D), bf16), _sds((RB, D), f32), _sds((RB, D), f32),
                   _sds((8, D), f32), _sds((8 * RNN_BLOCKS, RB), f32)),
        scratch_shapes=[pltpu.VMEM((S, RB), f32)] * 3 + [pltpu.VMEM((S // 8, RB), f32)],
        compiler_params=_params(("arbitrary", "arbitrary")),
    )(proj, proj, h_all, dy_rnn, cw_full, conv_b, w_a, b_a, w_x, b_x, lam)


def _section_of_chunk(s):
    out = []
    for start, n in zip(SEC_START, SEC_CHUNKS):
        inside = (s >= start) & (s < start + n)
        out.append((inside, jnp.clip(s - start, 0, n - 1)))
    return out


EFFECT = pltpu.SideEffectType.DATAFLOW_SIDE_EFFECTING
HBM_SPEC = pl.BlockSpec(memory_space=pltpu.HBM)
SEM_SPEC = pl.BlockSpec(memory_space=pltpu.SEMAPHORE)


def _split_exchange_copies(src_ref, land_ref, send_sems, recv_sems):
    x, y, c = _my_place()
    copies = []
    for k in (3, 1, 2):
        px, py = (x + (k >> 1)) % 2, (y + (k & 1)) % 2
        copies.append(pltpu.make_async_remote_copy(
            src_ref=src_ref.at[2 * px + py], dst_ref=land_ref.at[k - 1], send_sem=send_sems[k - 1],
            recv_sem=recv_sems[k - 1], device_id=(px, py, c), device_id_type=MESH))
    return copies


def _exchange_start(chip_sum):
    _, r, cols = chip_sum.shape

    def body(src_ref, land_ref, s0, s1, s2, r0, r1, r2, src_thru, land_thru, token):
        for cp in _split_exchange_copies(src_ref, land_ref, (s0, s1, s2), (r0, r1, r2)):
            cp.start()
        token[...] = jnp.zeros_like(token)

    land = pltpu.with_memory_space_constraint(lax.empty((3, r, cols), chip_sum.dtype), pltpu.HBM)
    res = _pcall(
        body, name="exchange_start",
        out_shape=tuple([pltpu.SemaphoreType.DMA(())] * 6) + (
            pltpu.HBM(chip_sum.shape, chip_sum.dtype), pltpu.HBM((3, r, cols), chip_sum.dtype), _sds((8, 128), f32)),
        in_specs=(HBM_SPEC, HBM_SPEC), out_specs=tuple([SEM_SPEC] * 6) + (HBM_SPEC, HBM_SPEC, VMEM_SPEC),
        input_output_aliases={0: 6, 1: 7},
        compiler_params=pltpu.CompilerParams(has_side_effects=EFFECT),
    )(pltpu.with_memory_space_constraint(chip_sum, pltpu.HBM), land)
    return res[:6], res[6], res[7], res[8]


def _exchange_wait(sems, src_thru, land_thru, after):
    def body(src_ref, land_ref, s0, s1, s2, r0, r1, r2, after_ref, src_dead, got_ref):
        for cp in _split_exchange_copies(src_ref, land_ref, (s0, s1, s2), (r0, r1, r2)):
            cp.wait_send()
            cp.wait_recv()

    return _pcall(
        body, name="exchange_wait",
        out_shape=(pltpu.HBM(src_thru.shape, src_thru.dtype), pltpu.HBM(land_thru.shape, land_thru.dtype)),
        in_specs=(HBM_SPEC, HBM_SPEC) + tuple([SEM_SPEC] * 6) + (ANY,), out_specs=(HBM_SPEC, HBM_SPEC),
        input_output_aliases={0: 0, 1: 1},
        compiler_params=pltpu.CompilerParams(has_side_effects=EFFECT),
    )(src_thru, land_thru, *sems, after)[1]


def _input_grad(dsecs, wt_full, x2d, dx2, norm_g):
    T = x2d.shape[0]
    tb = min(T, 512)
    nsec = len(dsecs)
    ntok = T // tb

    def body(*refs):
        secs = refs[:nsec]
        wt_ref, x_ref, dx2_ref, g_ref, dx_ref, gnorm_ref = refs[nsec:]
        i = pl.program_id(0)

        @pl.when(i == 0)
        def _():
            gnorm_ref[...] = jnp.zeros_like(gnorm_ref)

        dh = None
        for a, (start, n) in enumerate(zip(SEC_START, SEC_CHUNKS)):
            part = _dot(secs[a][...], wt_ref[CH * start:CH * (start + n), :], NN)
            dh = part if dh is None else dh + part
        xv = x_ref[...]
        rstd = lax.rsqrt(jnp.mean(xv * xv, axis=-1, keepdims=True) + EPS)
        xh = xv * rstd
        gnorm_ref[0:1, :] += jnp.sum(dh * xh, axis=0, keepdims=True)
        dxn = dh * g_ref[...]
        dx_ref[...] = dx2_ref[...] + rstd * (dxn - xh * jnp.mean(dxn * xh, axis=-1, keepdims=True))

    tok = pl.BlockSpec((tb, D), lambda i: (i, 0))
    return _pcall(
        body, name="input_grad", grid=(ntok,),
        in_specs=[pl.BlockSpec((tb, sec.shape[1]), lambda i: (i, 0)) for sec in dsecs]
        + [pl.BlockSpec((D_IN, D), lambda i: (0, 0), pipeline_mode=pl.Buffered(1)), tok, tok,
           pl.BlockSpec((1, D), lambda i: (0, 0))],
        out_specs=(tok, pl.BlockSpec((8, D), lambda i: (0, 0))),
        out_shape=(_sds((T, D), f32), _sds((8, D), f32)),
        compiler_params=_params(("arbitrary",)),
    )(*dsecs, wt_full, x2d, dx2, norm_g)


def _w_in_grad(dsecs, h_bf):
    T = h_bf.shape[0]
    tk = min(T, 2048)
    nchunks = D_IN // CH
    nsec = len(dsecs)
    nt = T // tk

    def body(*refs):
        secs = refs[:nsec]
        h_ref, out_ref, acc = refs[nsec:]
        s, t = pl.program_id(0), pl.program_id(1)

        @pl.when(t == 0)
        def _():
            acc[...] = jnp.zeros_like(acc)

        h_rows = h_ref[pl.ds(pl.multiple_of(t * tk, tk), tk), :]
        for a, (start, n) in enumerate(zip(SEC_START, SEC_CHUNKS)):
            @pl.when((s >= start) & (s < start + n))
            def _(a=a):
                acc[...] += _dot(secs[a][...], h_rows, TN)

        @pl.when(t == nt - 1)
        def _():
            out_ref[...] = acc[...].astype(bf16)

    def sec_spec(a):
        def index(s, t, a=a):
            inside, local = _section_of_chunk(s)[a]
            return (jnp.where(inside, t, 0), local)
        return pl.BlockSpec((tk, CH), index)

    return _pcall(
        body, name="w_in_grad", grid=(nchunks, T // tk),
        in_specs=[sec_spec(a) for a in range(nsec)]
        + [pl.BlockSpec((T, D), lambda s, t: (0, 0), pipeline_mode=pl.Buffered(1))],
        out_specs=pl.BlockSpec((CH, D), lambda s, t: (s, 0)), out_shape=_sds((D_IN, D), bf16),
        scratch_shapes=[pltpu.VMEM((CH, D), f32)],
        compiler_params=_params(("arbitrary", "arbitrary")),
    )(*dsecs, h_bf)


SMALL_NAMES = ("lru_w_a", "lru_w_x", "conv_b", "lru_b_a", "lru_b_x", "lru_lambda", "norm_g", "final_norm_g",
               "attn_sinks", "conv_w")
MISC_ROW = {"conv_b": 0, "lru_b_a": 1, "lru_b_x": 2, "lru_lambda": 3, "norm_g": 8, "final_norm_g": 16,
            "attn_sinks": 24, "loss": 32}


def _small_step(gwa, gwx, gvec, gnorm_blk, gfin_blk, dsink_blk, loss_blk, gcw, params):
    srcs_rows = (RB // NDEV, RB // NDEV, 8, 8)
    flat = [t for n in SMALL_NAMES for t in params[n]]
    nout = 4 * len(SMALL_NAMES) + 1

    def reduce_body(gwa_ref, gwx_ref, gvec_ref, gnorm_ref, gfin_ref, dsink_ref, loss_ref, gcw_ref,
                    all_a, all_x, all_m, conv_out,
                    misc, got_a, got_x, got_m, got_c, red_a, red_x, red_m, sa, ra, sb, rb):
        x, y, c = _my_place()
        me = 4 * x + 2 * y + c

        misc[...] = jnp.zeros_like(misc)
        misc[0:8, :] = gvec_ref[...]
        misc[8:16, :] = gnorm_ref[...]
        misc[16:24, :] = gfin_ref[...]
        misc[24:32, 0:128] = dsink_ref[...]
        misc[32:40, :] = loss_ref[...]

        srcs = (gwa_ref, gwx_ref, misc, gcw_ref)
        gots = (got_a, got_x, got_m, got_c)

        def shard(ref, rows, dev):
            return ref.at[pl.ds(pl.multiple_of(dev * rows, 8), rows), :]

        scatter = []
        for k in range(1, NDEV):
            px, py, pc = _peer(k)
            for a in range(4):
                scatter.append(pltpu.make_async_remote_copy(
                    src_ref=shard(srcs[a], srcs_rows[a], 4 * px + 2 * py + pc), dst_ref=gots[a].at[k - 1],
                    send_sem=sa.at[4 * (k - 1) + a], recv_sem=ra.at[4 * (k - 1) + a],
                    device_id=(px, py, pc), device_id_type=MESH))
        for cp in scatter:
            cp.start()
        for cp in scatter:
            cp.wait()

        def reduced(a):
            rows = srcs_rows[a]
            total = srcs[a][pl.ds(pl.multiple_of(me * rows, 8), rows), :]
            for k in range(NDEV - 1):
                total = total + gots[a][k]
            return total

        reds = (red_a, red_x, red_m)
        alls = (all_a, all_x, all_m)
        for a in range(3):
            val = reduced(a)
            reds[a][...] = val
            alls[a][pl.ds(pl.multiple_of(me * srcs_rows[a], 8), srcs_rows[a]), :] = val
        gather = []
        for k in range(1, NDEV):
            peer = _peer(k)
            for a in range(3):
                gather.append(pltpu.make_async_remote_copy(
                    src_ref=reds[a], dst_ref=shard(alls[a], srcs_rows[a], me),
                    send_sem=sb.at[3 * (k - 1) + a], recv_sem=rb.at[3 * (k - 1) + a],
                    device_id=peer, device_id_type=MESH))
        for cp in gather:
            cp.start()
        conv_out[...] = reduced(3)
        for cp in gather:
            cp.wait()

    def adam_body(*refs):
        all_a, all_x, all_m, conv_ref = refs[:4]
        prm = {n: refs[4 + 3 * k:7 + 3 * k] for k, n in enumerate(SMALL_NAMES)}
        nin = 4 + len(flat)
        outs = {n: refs[nin + 4 * k:nin + 4 * k + 4] for k, n in enumerate(SMALL_NAMES)}
        loss_out = refs[nin + nout - 1]
        g_conv = conv_ref[0:4, :]

        def update(name, g, pick=lambda r: r[...]):
            w_ref, m_ref, v_ref = prm[name]
            delta, m_new, v_new = _adam_math(g, pick(w_ref), pick(m_ref), pick(v_ref))
            return g, delta, m_new, v_new

        for n in range(RNN_BLOCKS):
            lanes = slice(RB * n, RB * (n + 1))
            for name, full in (("lru_w_a", all_a), ("lru_w_x", all_x)):
                for out, val in zip(outs[name], update(name, full[:, lanes], pick=lambda r, n=n: r[n])):
                    out[n] = val
        for name in ("conv_b", "lru_b_a", "lru_b_x", "lru_lambda", "norm_g", "final_norm_g"):
            row = MISC_ROW[name]
            for out, val in zip(outs[name], update(name, all_m[row:row + 1, :])):
                out[...] = val
        row = MISC_ROW["attn_sinks"]
        for out, val in zip(outs["attn_sinks"], update("attn_sinks", all_m[row:row + 1, 0:16])):
            out[...] = val
        for out, val in zip(outs["conv_w"], update("conv_w", g_conv)):
            out[...] = val
        row = MISC_ROW["loss"]
        loss_out[...] = all_m[row:row + 8, 0:128] * (0.5 / D)

    scratch = [pltpu.VMEM((64, D), f32),
               pltpu.VMEM((NDEV - 1, RB // NDEV, D), f32), pltpu.VMEM((NDEV - 1, RB // NDEV, D), f32),
               pltpu.VMEM((NDEV - 1, 8, D), f32), pltpu.VMEM((NDEV - 1, 8, RB), f32),
               pltpu.VMEM((RB // NDEV, D), f32), pltpu.VMEM((RB // NDEV, D), f32), pltpu.VMEM((8, D), f32),
               pltpu.SemaphoreType.DMA((4 * (NDEV - 1),)), pltpu.SemaphoreType.DMA((4 * (NDEV - 1),)),
               pltpu.SemaphoreType.DMA((3 * (NDEV - 1),)), pltpu.SemaphoreType.DMA((3 * (NDEV - 1),))]
    sums = _pcall(
        reduce_body, name="small_reduce",
        out_shape=(_sds((RB, D), f32), _sds((RB, D), f32), _sds((64, D), f32), _sds((8, RB), f32)),
        in_specs=[VMEM_SPEC] * 8, out_specs=tuple([VMEM_SPEC] * 4),
        scratch_shapes=scratch, compiler_params=_params(),
    )(gwa, gwx, gvec, gnorm_blk, gfin_blk, dsink_blk, loss_blk, gcw)
    out_shape = tuple(_sds(params[n][0].shape, f32) for n in SMALL_NAMES for _ in range(4)) + (_sds((8, 128), f32),)
    res = _pcall(
        adam_body, name="small_adamw", out_shape=out_shape,
        in_specs=[VMEM_SPEC] * (4 + len(flat)), out_specs=tuple([VMEM_SPEC] * nout), compiler_params=_params(),
    )(*sums, *flat)
    return {n: res[4 * k:4 * k + 4] for k, n in enumerate(SMALL_NAMES)}, res[-1]


def _pad_rows(v, rows=8):
    return jnp.concatenate([v, jnp.zeros((rows - v.shape[0], v.shape[1]), v.dtype)], axis=0)


def kernel(x, norm_g, w_in, conv_w, conv_b, lru_w_a, lru_b_a, lru_w_x, lru_b_x, lru_lambda, attn_sinks, w_rnn_out, w_attn_out, w_o, final_norm_g, loss_target, m_norm_g, m_w_in, m_conv_w, m_conv_b, m_lru_w_a, m_lru_b_a, m_lru_w_x, m_lru_b_x, m_lru_lambda, m_attn_sinks, m_w_rnn_out, m_w_attn_out, m_w_o, m_final_norm_g, v_norm_g, v_w_in, v_conv_w, v_conv_b, v_lru_w_a, v_lru_b_a, v_lru_w_x, v_lru_b_x, v_lru_lambda, v_attn_sinks, v_w_rnn_out, v_w_attn_out, v_w_o, v_final_norm_g):
    nb, S, _ = x.shape
    T = nb * S
    x2d = x.reshape(T, D)
    tgt = loss_target.reshape(T, D)
    fin_g = final_norm_g.reshape(1, D)
    w_a3, w_x3 = lru_w_a[0], lru_w_x[0]

    my_core = lax.axis_index("c").astype(jnp.int32).reshape(1)
    cx, cy = lax.axis_index("x"), lax.axis_index("y")
    chip_order = jnp.stack([2 * cx + cy, 2 * (1 - cx) + cy, 2 * cx + (1 - cy),
                            2 * (1 - cx) + (1 - cy)]).astype(jnp.int32)

    tabs = _rope_tables(S)
    h_bf, proj, wt_full, cw_full, _ = _in_proj_gather(
        x2d, norm_g, w_in[0].T.astype(bf16), _pad_rows(conv_w[0]), tabs, S, (), chip_order)
    y_rnn, h_all = _lru_forward(proj, cw_full, conv_b, w_a3, lru_b_a, w_x3, lru_b_x, lru_lambda, S)
    y_attn, (wr_full, wa_full, wo_full) = _attn_forward(proj, attn_sinks, S,
                                                        (w_rnn_out[0], w_attn_out[0], w_o[0]))

    (dx2, dy_rnn, dy_attn, dmr, dma, loss_blk, gfin_blk, g_wr, g_wa, g_wo) = _merge_and_head(
        x2d, tgt, proj, y_rnn, y_attn, wr_full, wa_full, wo_full, fin_g)
    sums_out = _pair_sums([g_wr, g_wa, g_wo], bf16, my_core, "out")

    (dq, dkv, dga, dsink_blk), (p_wr, p_wa, p_wo) = _attn_backward(proj, dy_attn, tabs, attn_sinks, S, sums_out)
    du0, dgr, gwa, gwx, gvec, gcw = _lru_backward(proj, h_all, dy_rnn, cw_full, conv_b, w_a3, lru_b_a, w_x3,
                                                  lru_b_x, lru_lambda, S)
    dsecs = (du0, dgr, dq, dkv, dga, dmr, dma)

    g_wt = _w_in_grad(dsecs, h_bf)
    (sum_in,) = _pair_sums([g_wt], bf16, my_core, "in")
    ex_sems, sum_in, landing, token = _exchange_start(sum_in)
    grad_x2d, gnorm_blk = _input_grad(dsecs, wt_full, x2d, dx2, norm_g + token[0, 0])
    p_wt = _exchange_wait(ex_sems, sum_in, landing, gnorm_blk)
    p_wt_own = lax.dynamic_index_in_dim(sum_in, 2 * cx + cy, axis=0, keepdims=False)

    small, loss_out = _small_step(gwa, gwx, gvec, gnorm_blk, gfin_blk, dsink_blk, loss_blk, gcw, {
        "lru_w_a": (w_a3, m_lru_w_a[0], v_lru_w_a[0]), "lru_w_x": (w_x3, m_lru_w_x[0], v_lru_w_x[0]),
        "conv_b": (conv_b, m_conv_b, v_conv_b), "lru_b_a": (lru_b_a, m_lru_b_a, v_lru_b_a),
        "lru_b_x": (lru_b_x, m_lru_b_x, v_lru_b_x), "lru_lambda": (lru_lambda, m_lru_lambda, v_lru_lambda),
        "norm_g": (norm_g, m_norm_g, v_norm_g),
        "final_norm_g": (fin_g, m_final_norm_g.reshape(1, D), v_final_norm_g.reshape(1, D)),
        "attn_sinks": (attn_sinks, m_attn_sinks, v_attn_sinks),
        "conv_w": (conv_w[0], m_conv_w[0], v_conv_w[0])})

    o_wt = _adamw(p_wt_own, p_wt, w_in[0].T, m_w_in[0].T, v_w_in[0].T, "adamw_w_in")
    o_wr, o_wa, o_wo = _adamw_group(
        (p_wr, p_wa, p_wo), (w_rnn_out[0], w_attn_out[0], w_o[0]),
        (m_w_rnn_out[0], m_w_attn_out[0], m_w_o[0]), (v_w_rnn_out[0], v_w_attn_out[0], v_w_o[0]), "adamw_w_out")

    def result(kind):
        d = {n: small[n][kind] for n in ("conv_b", "lru_b_a", "lru_b_x", "lru_lambda", "norm_g", "attn_sinks")}
        d.update({n: small[n][kind][None] for n in ("lru_w_a", "lru_w_x", "conv_w")})
        d["final_norm_g"] = small["final_norm_g"][kind].reshape(D)
        d.update({"w_in": o_wt[kind].T[None], "w_rnn_out": o_wr[kind][None], "w_attn_out": o_wa[kind][None],
                  "w_o": o_wo[kind][None]})
        return d

    order = ("norm_g", "w_in", "conv_w", "conv_b", "lru_w_a", "lru_b_a", "lru_w_x", "lru_b_x", "lru_lambda",
             "attn_sinks", "w_rnn_out", "w_attn_out", "w_o", "final_norm_g")
    outs = [loss_out[0, 0], grad_x2d.reshape(nb, S, D)]
    for kind in range(4):
        d = result(kind)
        outs += [d[n] for n in order]
    return tuple(outs)
```

```python
import functools
import math

import jax
import jax.numpy as jnp
from jax import lax
from jax.experimental import pallas as pl
from jax.experimental.pallas import tpu as pltpu

f32 = jnp.float32
bf16 = jnp.bfloat16

D = 1024
D_IN = 6656
NDEV = 8
RNN_BLOCKS = 8
RB = 128
HEAD = 64
KV_HEADS = 4
GROUP = 4
QB = 128
LRU_C = 8.0
EPS = 1e-6
ROPE_DIM = 16
ROPE_THETA = 500000.0
CH = 512
SEC_START = (0, 2, 4, 6, 7, 9, 11)
SEC_CHUNKS = (2, 2, 2, 1, 2, 2, 2)
VMEM_LIMIT = 62 * 1024 * 1024

ADAM_LR, ADAM_B1, ADAM_B2, ADAM_EPS, ADAM_WD, ADAM_STEP = 0.001, 0.9, 0.999, 1e-08, 0.01, 10

MESH = pl.DeviceIdType.MESH
ANY = pl.BlockSpec(memory_space=pl.ANY)
VMEM_SPEC = pl.BlockSpec(memory_space=pltpu.VMEM)
SMEM_SPEC = pl.BlockSpec(memory_space=pltpu.SMEM)


def _pcall(body, **kw):
    return pl.pallas_call(body, **kw)


def _params(sem=None, **kw):
    if sem is not None:
        kw["dimension_semantics"] = sem
    return pltpu.CompilerParams(vmem_limit_bytes=VMEM_LIMIT, **kw)


def _sds(shape, dtype):
    return jax.ShapeDtypeStruct(shape, dtype)


def _dot(a, b, dims):
    return lax.dot_general(a, b, (dims, ((), ())), preferred_element_type=f32)


NN = ((1,), (0,))
NT = ((1,), (1,))
TN = ((0,), (0,))


def _sigmoid(v):
    return 0.5 * jnp.tanh(0.5 * v) + 0.5


def _sigmoid_positive(v):
    return 1.0 / (1.0 + jnp.exp(-v))


def _my_place():
    return lax.axis_index("x"), lax.axis_index("y"), lax.axis_index("c")


def _peer(k):
    x, y, c = _my_place()
    return (x + ((k >> 2) & 1)) % 2, (y + ((k >> 1) & 1)) % 2, (c + (k & 1)) % 2


def _direct_gather_copies(srcs, outs, send_sems, recv_sems, local_sems):
    x, y, c = _my_place()
    me = 4 * x + 2 * y + c
    local, remote = [], []
    for a, (src, out) in enumerate(zip(srcs, outs)):
        r = src.shape[0]
        mine = out.at[pl.ds(pl.multiple_of(me * r, 8), r), :]
        local.append(pltpu.make_async_copy(src, mine, local_sems.at[a]))
        for k in range(1, NDEV):
            remote.append(pltpu.make_async_remote_copy(
                src_ref=src, dst_ref=mine, send_sem=send_sems.at[7 * a + k - 1], recv_sem=recv_sems.at[7 * a + k - 1],
                device_id=_peer(k), device_id_type=MESH))
    return local, remote


def _chip_exchange_copies(src, dst, send_sems, recv_sems, local_sems):
    x, y, c = _my_place()
    local, remote = [], []
    for a in range(len(src)):
        local.append(pltpu.make_async_copy(src[a].at[2 * x + y], dst[a].at[0], local_sems.at[a]))
    for k in (3, 1, 2):
        px, py = (x + (k >> 1)) % 2, (y + (k & 1)) % 2
        for a in range(len(src)):
            remote.append(pltpu.make_async_remote_copy(
                src_ref=src[a].at[2 * px + py], dst_ref=dst[a].at[k],
                send_sem=send_sems.at[3 * a + k - 1], recv_sem=recv_sems.at[3 * a + k - 1],
                device_id=(px, py, c), device_id_type=MESH))
    return local, remote


def _exchange_scratch(narr, per_array):
    return [pltpu.SemaphoreType.DMA((per_array * narr,)), pltpu.SemaphoreType.DMA((per_array * narr,)),
            pltpu.SemaphoreType.DMA((narr,))]


def _start_all(copies):
    local, remote = copies
    for cp in local + remote:
        cp.start()


def _wait_all(copies):
    local, remote = copies
    for cp in remote + local:
        cp.wait()


def _row_tile(rows, dtype):
    unit = 16 if dtype == bf16 else 8
    for cand in (256, 208, 128, 64, 40, 32, 16, 8):
        if rows % cand == 0 and cand % unit == 0:
            return cand
    return rows


def _pair_sums(grads, wire_dtype, my_core, tag):
    narr = len(grads)
    r, cols = grads[0].shape[0] // NDEV, grads[0].shape[1]
    views = [g.reshape(4, 2, r, cols) for g in grads]
    tr = _row_tile(r, wire_dtype)
    nt = r // tr

    def body(core_ref, *refs):
        mine = refs[:narr]
        whole = refs[narr:2 * narr]
        outs = refs[2 * narr:3 * narr]
        got = refs[3 * narr:4 * narr]
        send_sems, recv_sems = refs[4 * narr:]
        q, i = pl.program_id(0), pl.program_id(1)
        x, y, c = _my_place()

        def copy(a, chip):
            return pltpu.make_async_remote_copy(
                src_ref=whole[a].at[chip, 1 - c], dst_ref=got[a].at[chip],
                send_sem=send_sems.at[4 * a + chip], recv_sem=recv_sems.at[4 * a + chip],
                device_id=(x, y, 1 - c), device_id_type=MESH)

        @pl.when((q == 0) & (i == 0))
        def _():
            for chip in range(4):
                for a in range(narr):
                    copy(a, chip).start()

        for chip in range(4):
            @pl.when((q == chip) & (i == 0))
            def _(chip=chip):
                for a in range(narr):
                    copy(a, chip).wait_recv()

        rows = pl.ds(pl.multiple_of(i * tr, tr), tr)
        for a in range(narr):
            outs[a][...] = (mine[a][...].astype(f32) + got[a][q, rows, :].astype(f32)).astype(wire_dtype)

        @pl.when((q == 3) & (i == nt - 1))
        def _():
            for chip in range(4):
                for a in range(narr):
                    copy(a, chip).wait_send()

    slab = pl.BlockSpec((None, tr, cols), lambda q, i, core: (q, i, 0))
    grid_spec = pltpu.PrefetchScalarGridSpec(
        num_scalar_prefetch=1, grid=(4, nt),
        in_specs=[pl.BlockSpec((None, None, tr, cols), lambda q, i, core: (q, core[0], i, 0))] * narr + [ANY] * narr,
        out_specs=tuple([slab] * narr),
        scratch_shapes=[pltpu.VMEM((4, r, cols), grads[0].dtype)] * narr
        + [pltpu.SemaphoreType.DMA((4 * narr,)), pltpu.SemaphoreType.DMA((4 * narr,))])
    return _pcall(body, name="pair_sums_" + tag, grid_spec=grid_spec,
                  out_shape=tuple(_sds((4, r, cols), wire_dtype) for _ in range(narr)),
                  compiler_params=_params(("arbitrary", "arbitrary")))(my_core, *views, *views)


def _adam_math(g, w, m, v):
    m_new = ADAM_B1 * m + (1.0 - ADAM_B1) * g
    v_new = ADAM_B2 * v + (1.0 - ADAM_B2) * (g * g)
    m_hat = m_new / (1.0 - ADAM_B1 ** ADAM_STEP)
    v_hat = v_new / (1.0 - ADAM_B2 ** ADAM_STEP)
    return -ADAM_LR * (m_hat / (jnp.sqrt(v_hat) + ADAM_EPS) + ADAM_WD * w), m_new, v_new


def _adamw(first, parts, w, m, v, name):
    n, rows, cols = parts.shape
    tr = _row_tile(rows, parts.dtype)

    def body(f_ref, p_ref, w_ref, m_ref, v_ref, g_out, d_out, m_out, v_out):
        g = f_ref[...].astype(f32)
        for s in range(n):
            g = g + p_ref[s].astype(f32)
        g_out[...] = g
        d_out[...], m_out[...], v_out[...] = _adam_math(g, w_ref[...], m_ref[...], v_ref[...])

    blk = pl.BlockSpec((tr, cols), lambda i: (i, 0))
    return _pcall(
        body, name=name, grid=(rows // tr,),
        in_specs=[blk, pl.BlockSpec((n, tr, cols), lambda i: (0, i, 0)), blk, blk, blk],
        out_specs=(blk, blk, blk, blk), out_shape=tuple(_sds((rows, cols), f32) for _ in range(4)),
        compiler_params=_params(("arbitrary",)),
    )(first, parts, w, m, v)


def _adamw_group(parts, ws, ms, vs, name):
    nw = len(ws)

    def body(*refs):
        p_refs, w_refs, m_refs, v_refs = (refs[k * nw:(k + 1) * nw] for k in range(4))
        outs = refs[4 * nw:]
        for k in range(nw):
            g = p_refs[k][0].astype(f32)
            for s in range(1, p_refs[k].shape[0]):
                g = g + p_refs[k][s].astype(f32)
            g_out, d_out, m_out, v_out = outs[4 * k:4 * k + 4]
            g_out[...] = g
            d_out[...], m_out[...], v_out[...] = _adam_math(g, w_refs[k][...], m_refs[k][...], v_refs[k][...])

    res = _pcall(
        body, name=name, out_shape=tuple(_sds(w.shape, f32) for w in ws for _ in range(4)),
        in_specs=[VMEM_SPEC] * (4 * nw), out_specs=tuple([VMEM_SPEC] * (4 * nw)), compiler_params=_params(),
    )(*parts, *ws, *ms, *vs)
    return [res[4 * k:4 * k + 4] for k in range(nw)]


def _rope(t, c, s1, s2):
    w = t.shape[1]
    return t * c + pltpu.roll(t, w - 8, 1) * s1 + pltpu.roll(t, 8, 1) * s2


def _rope_transposed(dt, c, s1, s2):
    w = dt.shape[1]
    return dt * c + pltpu.roll(dt * s1, 8, 1) + pltpu.roll(dt * s2, w - 8, 1)


PAIR_ROWS = D_IN // 4
SUB_COLS = ((0, 512), (512, 512), (1024, 512), (1536, 128))
Q_SLABS = range(3, 11)
K_SLABS = range(11, 13)


def _in_proj_gather(x2d, norm_g, wt_shard, cw_shard, tabs, S, out_shards, chip_order):
    T = x2d.shape[0]
    tb = min(S, 1024)
    ntok = T // tb
    nsb = S // tb
    q_scale = 1.0 / math.sqrt(HEAD)
    shard_rows = wt_shard.shape[0]
    small = (cw_shard,) + tuple(out_shards)
    nsm = len(small)

    def body(order_ref, x_ref, g_ref, c_ref, s1_ref, s2_ref, wt_hbm, *rest):
        small_in = rest[:nsm]
        h_ref, proj_ref, wt_out = rest[nsm:nsm + 3]
        small_out = rest[nsm + 3:2 * nsm + 3]
        wt_vm, h_vm = rest[2 * nsm + 3:2 * nsm + 5]
        stage = rest[2 * nsm + 5:3 * nsm + 4]
        wsend, wrecv, wlocal = rest[3 * nsm + 4:3 * nsm + 7]
        dsems = rest[3 * nsm + 7:]
        jj, i = pl.program_id(0), pl.program_id(1)
        x, y, c = _my_place()
        me, sibling = (x, y, c), (x, y, 1 - c)
        chips = [(1 - x, y), (x, 1 - y), (1 - x, 1 - y)]

        def rows(place):
            px, py, pc = place
            return wt_vm.at[pl.ds(pl.multiple_of((4 * px + 2 * py + pc) * shard_rows, 16), shard_rows), :]

        def copy(k, block, to, src=None):
            return pltpu.make_async_remote_copy(
                src_ref=rows(block) if src is None else src, dst_ref=rows(block),
                send_sem=wsend.at[k], recv_sem=wrecv.at[k], device_id=to, device_id_type=MESH)

        def small_copies():
            srcs = (small_in[0],) + tuple(stage)
            return _direct_gather_copies(srcs, small_out, *dsems)

        own = pltpu.make_async_copy(wt_hbm, rows(me), wlocal.at[0])
        keep = pltpu.make_async_copy(wt_vm, wt_out, wlocal.at[1])

        @pl.when((jj == 0) & (i == 0))
        def _():
            own.start()
            copy(0, me, sibling, src=wt_hbm).start()
            for j, chip in enumerate(chips):
                copy(1 + j, me, (*chip, c), src=wt_hbm).start()
            for a in range(nsm - 1):
                stage[a][...] = small_in[1 + a][...].astype(bf16)
            _start_all(small_copies())
            own.wait()
            copy(0, sibling, me).wait_recv()

        for j, chip in enumerate(chips):
            @pl.when((jj == 1 + j) & (i == 0))
            def _(j=j, chip=chip):
                copy(1 + j, (*chip, c), me).wait_recv()
                copy(4 + j, (*chip, c), sibling).start()
                copy(4 + j, (*chip, 1 - c), me).wait_recv()

        @pl.when((jj == 3) & (i == 0))
        def _():
            keep.start()

        @pl.when((jj == 3) & (i == ntok - 1))
        def _():
            copy(0, me, sibling, src=wt_hbm).wait_send()
            for j, chip in enumerate(chips):
                copy(1 + j, me, (*chip, c), src=wt_hbm).wait_send()
                copy(4 + j, (*chip, c), sibling).wait_send()
            _wait_all(small_copies())
            keep.wait()

        tok = pl.ds(pl.multiple_of(i * tb, tb), tb)

        @pl.when(jj == 0)
        def _():
            xv = x_ref[...]
            ms = jnp.mean(xv * xv, axis=-1, keepdims=True)
            hb = (xv * lax.rsqrt(ms + EPS) * g_ref[...]).astype(bf16)
            h_ref[...] = hb
            h_vm[tok, :] = hb

        block = order_ref[jj]
        hb = h_vm[tok, :]

        def piece(c0, w):
            w_rows = wt_vm[pl.ds(pl.multiple_of(block * PAIR_ROWS + c0, 128), w), :]
            return _dot(hb, w_rows, NT)

        @pl.when(block != 1)
        def _():
            for c0, w in SUB_COLS:
                proj_ref[:, c0:c0 + w] = piece(c0, w).astype(bf16)

        @pl.when(block == 1)
        def _():
            tab = (c_ref[...], s1_ref[...], s2_ref[...])
            for c0, w in SUB_COLS:
                acc = piece(c0, w)
                for l in range(w // 128):
                    slab = (c0 + 128 * l) // 128
                    part = acc[:, 128 * l:128 * (l + 1)]
                    if slab in Q_SLABS:
                        part = _rope(part, *tab) * q_scale
                    elif slab in K_SLABS:
                        part = _rope(part, *tab)
                    proj_ref[:, 128 * slab:128 * (slab + 1)] = part.astype(bf16)

    first_pass = lambda jj, i, order: (jnp.where(jj == 0, i, ntok - 1), 0)
    const = lambda jj, i, order: (0, 0)
    tab = pl.BlockSpec((tb, 128), lambda jj, i, order: (jnp.where(order[jj] == 1, i % nsb, 0), 0))
    grid_spec = pltpu.PrefetchScalarGridSpec(
        num_scalar_prefetch=1, grid=(4, ntok),
        in_specs=[pl.BlockSpec((tb, D), first_pass), pl.BlockSpec((1, D), const), tab, tab, tab, ANY]
        + [pl.BlockSpec(w.shape, const) for w in small],
        out_specs=(pl.BlockSpec((tb, D), first_pass),
                   pl.BlockSpec((tb, PAIR_ROWS), lambda jj, i, order: (i, order[jj])), ANY) + tuple([ANY] * nsm),
        scratch_shapes=[pltpu.VMEM((D_IN, D), bf16), pltpu.VMEM((T, D), bf16)]
        + [pltpu.VMEM(w.shape, bf16) for w in out_shards]
        + [pltpu.SemaphoreType.DMA((7,)), pltpu.SemaphoreType.DMA((7,)), pltpu.SemaphoreType.DMA((2,))]
        + _exchange_scratch(nsm, 7))
    res = _pcall(
        body, name="in_proj", grid_spec=grid_spec,
        out_shape=(_sds((T, D), bf16), _sds((T, D_IN), bf16), _sds((D_IN, D), bf16),
                   _sds((NDEV * cw_shard.shape[0], cw_shard.shape[1]), f32))
        + tuple(_sds((NDEV * w.shape[0], w.shape[1]), bf16) for w in out_shards),
        compiler_params=_params(("arbitrary", "arbitrary")),
    )(chip_order, x2d, norm_g, *tabs, wt_shard, *small)
    return res[0], res[1], res[2], res[3], res[4:]


def _rows_iota(shape):
    return lax.broadcasted_iota(jnp.int32, shape, 0)


def _shift_down(v, k):
    return jnp.where(_rows_iota(v.shape) >= k, pltpu.roll(v, k, 0), 0.0)


def _shift_up(v, k):
    n = v.shape[0]
    return jnp.where(_rows_iota(v.shape) < n - k, pltpu.roll(v, n - k, 0), 0.0)


def _linear_scan(a, b, a_s, b_s, edge_s, out_ref, reverse):
    n = a.shape[0]
    ng = n // 8
    a3, b3 = a.reshape(ng, 8, RB), b.reshape(ng, 8, RB)
    rid = lax.broadcasted_iota(jnp.int32, a3.shape, 1)
    for s in (1, 2, 4):
        keep, shift = (rid < 8 - s, 8 - s) if reverse else (rid >= s, s)
        b3 = jnp.where(keep, a3 * pltpu.roll(b3, shift, 1) + b3, b3)
        a3 = jnp.where(keep, a3 * pltpu.roll(a3, shift, 1), a3)
    a_s[...] = a3.reshape(n, RB)
    b_s[...] = b3.reshape(n, RB)
    edge = 0 if reverse else 7
    ea, eb = a_s[pl.ds(edge, ng, stride=8), :], b_s[pl.ds(edge, ng, stride=8), :]
    r = _rows_iota(ea.shape)
    s = 1
    while s < ng:
        keep, shift = (r < ng - s, ng - s) if reverse else (r >= s, s)
        eb = jnp.where(keep, ea * pltpu.roll(eb, shift, 0) + eb, eb)
        if 2 * s < ng:
            ea = jnp.where(keep, ea * pltpu.roll(ea, shift, 0), ea)
        s *= 2
    edge_s[...] = _shift_up(eb, 1) if reverse else _shift_down(eb, 1)

    def eight_groups(i, carry):
        for k in range(8):
            j = i * 8 + k
            rows = pl.ds(pl.multiple_of(j * 8, 8), 8)
            out_ref[rows, :] = b_s[rows, :] + a_s[rows, :] * edge_s[pl.ds(j, 1), :]
        return carry

    lax.fori_loop(0, ng // 8, eight_groups, 0)


def _neg_expm1(v):
    series = -v * (1.0 + v * (0.5 + v * (1.0 / 6.0)))
    return jnp.where(v > -0.015625, series, 1.0 - jnp.exp(v))


def _softplus_neg(lam):
    return jnp.maximum(-lam, 0.0) + jnp.log(1.0 + jnp.exp(-jnp.abs(lam)))


def _lru_gates(x0, cw, cb, wa, ba, wx, bx, lam):
    taps = [_shift_down(x0, 3 - k) for k in range(3)] + [x0]
    u = cb + cw[3:4, :] * x0
    for k in range(3):
        u = u + cw[k:k + 1, :] * taps[k]
    ub = u.astype(bf16)
    r = _sigmoid_positive(_dot(ub, wa.astype(bf16), NN) + ba)
    i = _sigmoid(_dot(ub, wx.astype(bf16), NN) + bx)
    sp = _softplus_neg(lam)
    log_a = (-LRU_C) * r * sp
    a = jnp.exp(log_a)
    w = _neg_expm1(2.0 * log_a)
    inv_mult = lax.rsqrt(w)
    return u, ub, r, i, sp, a, w * inv_mult, inv_mult, taps


def _lru_specs(S, nb):
    col = lambda off: pl.BlockSpec((S, RB), lambda n, b, off=off: (b, off + n))
    vec = pl.BlockSpec((1, RB), lambda n, b: (0, n))
    wblk = pl.BlockSpec((None, RB, RB), lambda n, b: (n, 0, 0))
    cwblk = pl.BlockSpec((8, RB), lambda n, b: (n, 0))
    return col, vec, wblk, cwblk


def _lru_forward(proj, cw_full, conv_b, w_a, b_a, w_x, b_x, lam, S):
    T = proj.shape[0]
    nb = T // S
    col, vec, wblk, cwblk = _lru_specs(S, nb)

    def body(x0_ref, g_ref, cw_ref, cb_ref, wa_ref, ba_ref, wx_ref, bx_ref, lam_ref, y_ref, h_ref, a_s, b_s, edge_s):
        x0 = x0_ref[...].astype(f32)
        u, ub, r, i, sp, a, mult, _, _ = _lru_gates(x0, cw_ref[...], cb_ref[...], wa_ref[...], ba_ref[...],
                                                    wx_ref[...], bx_ref[...], lam_ref[...])
        _linear_scan(a, mult * (i * u), a_s, b_s, edge_s, h_ref, reverse=False)
        g = g_ref[...].astype(f32)
        y_ref[...] = (h_ref[...] * (g * _sigmoid(g))).astype(bf16)

    out = pl.BlockSpec((S, RB), lambda n, b: (b, n))
    return _pcall(
        body, name="lru_forward", grid=(RNN_BLOCKS, nb),
        in_specs=[col(0), col(8), cwblk, vec, wblk, vec, wblk, vec, vec],
        out_specs=(out, out), out_shape=(_sds((T, D), bf16), _sds((T, D), f32)),
        scratch_shapes=[pltpu.VMEM((S, RB), f32), pltpu.VMEM((S, RB), f32), pltpu.VMEM((S // 8, RB), f32)],
        compiler_params=_params(("arbitrary", "arbitrary")),
    )(proj, proj, cw_full, conv_b, w_a, b_a, w_x, b_x, lam)


def _rope_tables(S):
    pos = jnp.arange(S, dtype=f32)
    inv_freq = ROPE_THETA ** (-jnp.arange(0, ROPE_DIM, 2, dtype=f32) / ROPE_DIM)
    ang = pos[:, None] * inv_freq[None, :]
    cos, sin = jnp.cos(ang), jnp.sin(ang)
    lane = jnp.arange(128) % HEAD
    cosl, sinl = cos[:, lane % 8], sin[:, lane % 8]
    c = jnp.where(lane[None, :] < ROPE_DIM, cosl, 1.0)
    s1 = jnp.where(lane[None, :] < 8, -sinl, 0.0)
    s2 = jnp.where((lane[None, :] >= 8) & (lane[None, :] < ROPE_DIM), sinl, 0.0)
    return c.astype(f32), s1.astype(f32), s2.astype(f32)


def _heads_to_rows(t):
    return jnp.concatenate([t[:, HEAD * h:HEAD * (h + 1)] for h in range(GROUP)], axis=0)


def _rows_to_heads(t):
    return jnp.concatenate([t[QB * h:QB * (h + 1), :] for h in range(GROUP)], axis=1)


def _window_bias(first_block):
    shape = (GROUP * QB, 2 * QB)
    qi = _rows_iota(shape) % QB
    cj = lax.broadcasted_iota(jnp.int32, shape, 1)
    valid = (cj > qi) & (cj <= qi + QB) & ((cj >= QB) | jnp.logical_not(first_block))
    return jnp.where(valid, 0.0, -jnp.inf)


def _attn_probs(q_rows, k_cat, sink_col, bias):
    s = _dot(q_rows, k_cat, NT) + bias
    m = jnp.maximum(jnp.max(s, axis=1, keepdims=True), sink_col)
    p = jnp.exp(s - m)
    e_sink = jnp.exp(sink_col - m)
    inv = 1.0 / (jnp.sum(p, axis=1, keepdims=True) + e_sink)
    return p * inv, e_sink * inv


def _sink_column(sink_ref, kv):
    rid = _rows_iota((GROUP * QB, 1))
    col = jnp.zeros((GROUP * QB, 1), f32)
    for h in range(GROUP):
        col = jnp.where(rid // QB == h, sink_ref[0, GROUP * kv + h], col)
    return col


def _attn_in_specs(S):
    nq = S // QB
    last = nq - 1
    cur = lambda b, j: b * nq + jnp.minimum(j, last)
    prev = lambda b, j: b * nq + jnp.maximum(jnp.minimum(j, last) - 1, 0)
    specs = [
        pl.BlockSpec((QB, D), lambda b, j: (cur(b, j), 2)),
        pl.BlockSpec((QB, 256), lambda b, j: (cur(b, j), 12)),
        pl.BlockSpec((QB, 256), lambda b, j: (prev(b, j), 12)),
        pl.BlockSpec((QB, 256), lambda b, j: (cur(b, j), 13)),
        pl.BlockSpec((QB, 256), lambda b, j: (prev(b, j), 13)),
        pl.BlockSpec((QB, 512), lambda b, j: (cur(b, j), 7)),
        pl.BlockSpec((QB, 512), lambda b, j: (cur(b, j), 8)),
        SMEM_SPEC,
    ]
    return specs, cur, prev


def _attn_forward(proj, sinks, S, out_shards):
    T = proj.shape[0]
    nb, nq = T // S, S // QB
    specs, cur, _ = _attn_in_specs(S)
    nw = len(out_shards)

    def body(q_ref, kc_ref, kp_ref, vc_ref, vp_ref, gl_ref, gh_ref, sink_ref, *rest):
        shards = rest[:nw]
        y_ref = rest[nw]
        gathered = rest[nw + 1:2 * nw + 1]
        stage = rest[2 * nw + 1:3 * nw + 1]
        sems = rest[3 * nw + 1:]
        b, j = pl.program_id(0), pl.program_id(1)

        @pl.when((b == 0) & (j == 0))
        def _():
            for a in range(nw):
                stage[a][...] = shards[a][...].astype(bf16)
            _start_all(_direct_gather_copies(stage, gathered, *sems))

        @pl.when((b == nb - 1) & (j == nq - 1))
        def _():
            _wait_all(_direct_gather_copies(stage, gathered, *sems))

        bias = _window_bias(j == 0)
        kc, kp, vc, vp = kc_ref[...], kp_ref[...], vc_ref[...], vp_ref[...]
        for kv in range(KV_HEADS):
            lanes = slice(256 * kv, 256 * (kv + 1))
            hl = slice(HEAD * kv, HEAD * (kv + 1))
            q_rows = _heads_to_rows(q_ref[:, lanes])
            k_cat = jnp.concatenate([kp[:, hl], kc[:, hl]], axis=0)
            v_cat = jnp.concatenate([vp[:, hl], vc[:, hl]], axis=0)
            probs, _ = _attn_probs(q_rows, k_cat, _sink_column(sink_ref, kv), bias)
            o = _rows_to_heads(_dot(probs.astype(bf16), v_cat, NN))
            g_src = gl_ref if kv < 2 else gh_ref
            g = g_src[:, 256 * (kv % 2):256 * (kv % 2 + 1)].astype(f32)
            y_ref[:, lanes] = (o * (g * _sigmoid(g))).astype(bf16)

    args = [proj] * 7 + [sinks] + list(out_shards)
    res = _pcall(
        body, name="attn_forward", grid=(nb, nq),
        in_specs=specs + [pl.BlockSpec(w.shape, lambda b, j: (0, 0)) for w in out_shards],
        out_specs=(pl.BlockSpec((QB, D), lambda b, j: (cur(b, j), 0)),) + tuple([ANY] * nw),
        out_shape=(_sds((T, D), bf16),) + tuple(_sds((NDEV * w.shape[0], w.shape[1]), bf16) for w in out_shards),
        scratch_shapes=[pltpu.VMEM(w.shape, bf16) for w in out_shards] + _exchange_scratch(nw, 7),
        compiler_params=_params(("arbitrary", "arbitrary")),
    )(*args)
    return res[0], res[1:]


def _merge_and_head(x2d, tgt, proj, y_rnn, y_attn, w_r, w_a, w_o, gfin):
    T = x2d.shape[0]
    tb = min(T, 512)
    nsteps = T // tb

    def body(x_ref, t_ref, mr0, mr1, ma0, ma1, yr_ref, ya_ref, wr_ref, wa_ref, wo_ref, gf_ref,
             dx2_ref, dyr_ref, dya_ref, dmr_ref, dma_ref, loss_ref, gfin_ref, gwr_out, gwa_out, gwo_out,
             gwr_acc, gwa_acc, gwo_acc, out_sems):
        step = pl.program_id(0)

        @pl.when(step == 0)
        def _():
            loss_ref[...] = jnp.zeros_like(loss_ref)
            gfin_ref[...] = jnp.zeros_like(gfin_ref)
            gwr_acc[...] = jnp.zeros_like(gwr_acc)
            gwa_acc[...] = jnp.zeros_like(gwa_acc)
            gwo_acc[...] = jnp.zeros_like(gwo_acc)

        sr = _sigmoid(jnp.concatenate([mr0[...], mr1[...]], axis=1).astype(f32))
        sa = _sigmoid(jnp.concatenate([ma0[...], ma1[...]], axis=1).astype(f32))
        p_r = _dot(yr_ref[...], wr_ref[...], NN)
        p_a = _dot(ya_ref[...], wa_ref[...], NN)
        merged = (sr * p_r + sa * p_a).astype(bf16)
        x2 = x_ref[...] + _dot(merged, wo_ref[...], NN)
        rstd = lax.rsqrt(jnp.mean(x2 * x2, axis=-1, keepdims=True) + EPS)
        xh = x2 * rstd
        gf = gf_ref[...]
        err = xh * gf - t_ref[...]
        loss_ref[...] += jnp.sum(err * err)
        dy = err * (1.0 / D)
        gfin_ref[0:1, :] += jnp.sum(dy * xh, axis=0, keepdims=True)
        dxn = dy * gf
        dx2 = rstd * (dxn - xh * jnp.mean(dxn * xh, axis=-1, keepdims=True))
        dx2_ref[...] = dx2
        dx2b = dx2.astype(bf16)
        dmerged = _dot(dx2b, wo_ref[...], NT)
        dmr_ref[...] = (dmerged * p_r * (sr * (1.0 - sr))).astype(bf16)
        dma_ref[...] = (dmerged * p_a * (sa * (1.0 - sa))).astype(bf16)
        dpr = (dmerged * sr).astype(bf16)
        dpa = (dmerged * sa).astype(bf16)
        dyr_ref[...] = _dot(dpr, wr_ref[...], NT).astype(bf16)
        dya_ref[...] = _dot(dpa, wa_ref[...], NT).astype(bf16)
        gwr_acc[...] += _dot(yr_ref[...], dpr, TN)
        gwa_acc[...] += _dot(ya_ref[...], dpa, TN)
        gwo_acc[...] += _dot(merged, dx2b, TN)

        @pl.when(step == nsteps - 1)
        def _():
            copies = [pltpu.make_async_copy(src, dst, out_sems.at[k]) for k, (src, dst) in enumerate(
                ((gwr_acc, gwr_out), (gwa_acc, gwa_out), (gwo_acc, gwo_out)))]
            for cp in copies:
                cp.start()
            for cp in copies:
                cp.wait()

    tok = pl.BlockSpec((tb, D), lambda i: (i, 0))
    half = lambda c: pl.BlockSpec((tb, CH), lambda i, c=c: (i, c))
    wfull = pl.BlockSpec((D, D), lambda i: (0, 0), pipeline_mode=pl.Buffered(1))
    acc = pl.BlockSpec((8, D), lambda i: (0, 0))
    return _pcall(
        body, name="merge_and_head", grid=(nsteps,),
        in_specs=[tok, tok, half(9), half(10), half(11), half(12), tok, tok, wfull, wfull, wfull,
                  pl.BlockSpec((1, D), lambda i: (0, 0))],
        out_specs=(tok, tok, tok, tok, tok, acc, acc, ANY, ANY, ANY),
        out_shape=(_sds((T, D), f32), _sds((T, D), bf16), _sds((T, D), bf16), _sds((T, D), bf16),
                   _sds((T, D), bf16), _sds((8, D), f32), _sds((8, D), f32),
                   _sds((D, D), f32), _sds((D, D), f32), _sds((D, D), f32)),
        scratch_shapes=[pltpu.VMEM((D, D), f32)] * 3 + [pltpu.SemaphoreType.DMA((3,))],
        compiler_params=_params(("arbitrary",)),
    )(x2d, tgt, proj, proj, proj, proj, y_rnn, y_attn, w_r, w_a, w_o, gfin)


def _attn_backward(proj, dy_attn, tabs, sinks, S, chip_sums):
    T = proj.shape[0]
    nb, nq = T // S, S // QB
    nex = len(chip_sums)
    specs, cur, prev = _attn_in_specs(S)
    last = nq - 1
    tab_cur = pl.BlockSpec((QB, 128), lambda b, j: (jnp.minimum(j, last), 0))
    tab_prev = pl.BlockSpec((QB, 128), lambda b, j: (jnp.maximum(jnp.minimum(j, last) - 1, 0), 0))
    specs = specs + [pl.BlockSpec((QB, D), lambda b, j: (cur(b, j), 0))] + [tab_cur] * 3 + [tab_prev] * 3
    q_scale = 1.0 / math.sqrt(HEAD)

    def rope_back(dt, tab):
        return jnp.concatenate([_rope_transposed(dt[:, 128 * l:128 * (l + 1)], *tab) for l in range(2)], axis=1)

    def body(q_ref, kc_ref, kp_ref, vc_ref, vp_ref, gl_ref, gh_ref, sink_ref, dy_ref, cc, s1c, s2c, cp, s1p, s2p,
             *rest):
        ex_src = rest[:nex]
        dq_ref, dkv_ref, dg_ref, dsink_ref = rest[nex:nex + 4]
        ex_dst = rest[nex + 4:2 * nex + 4]
        carry_k, carry_v = rest[2 * nex + 4:2 * nex + 6]
        sems = rest[2 * nex + 6:]
        b, j = pl.program_id(0), pl.program_id(1)

        @pl.when((b == 0) & (j == 0))
        def _():
            dsink_ref[...] = jnp.zeros_like(dsink_ref)
            _start_all(_chip_exchange_copies(ex_src, ex_dst, *sems))

        @pl.when((b == nb - 1) & (j == nq))
        def _():
            _wait_all(_chip_exchange_copies(ex_src, ex_dst, *sems))

        @pl.when(j == 0)
        def _():
            carry_k[...] = jnp.zeros_like(carry_k)
            carry_v[...] = jnp.zeros_like(carry_v)

        @pl.when(j < nq)
        def _():
            bias = _window_bias(j == 0)
            tc = (cc[...], s1c[...], s2c[...])
            tp = (cp[...], s1p[...], s2p[...])
            kc, kp, vc, vp = kc_ref[...], kp_ref[...], vc_ref[...], vp_ref[...]
            dk_prev, dk_cur, dv_prev, dv_cur = [], [], [], []
            dsink_acc = jnp.zeros((8, 128), f32)
            r8 = lax.broadcasted_iota(jnp.int32, (8, 128), 0)
            l8 = lax.broadcasted_iota(jnp.int32, (8, 128), 1)
            for kv in range(KV_HEADS):
                lanes = slice(256 * kv, 256 * (kv + 1))
                hl = slice(HEAD * kv, HEAD * (kv + 1))
                q_rows = _heads_to_rows(q_ref[:, lanes])
                k_cat = jnp.concatenate([kp[:, hl], kc[:, hl]], axis=0)
                v_cat = jnp.concatenate([vp[:, hl], vc[:, hl]], axis=0)
                probs, p_sink = _attn_probs(q_rows, k_cat, _sink_column(sink_ref, kv), bias)
                pb = probs.astype(bf16)
                o = _rows_to_heads(_dot(pb, v_cat, NN))
                g_src = gl_ref if kv < 2 else gh_ref
                g = g_src[:, 256 * (kv % 2):256 * (kv % 2 + 1)].astype(f32)
                sg = _sigmoid(g)
                dy = dy_ref[:, lanes].astype(f32)
                dg_ref[:, lanes] = (dy * o * (sg * (1.0 + g * (1.0 - sg)))).astype(bf16)
                do_rows = _heads_to_rows(dy * (g * sg)).astype(bf16)
                dv = _dot(pb, do_rows, TN)
                dp = _dot(do_rows, v_cat, NT)
                rowdot = jnp.sum(probs * dp, axis=1, keepdims=True)
                ds = (probs * (dp - rowdot)).astype(bf16)
                sink_rows = -(p_sink * rowdot)
                for h in range(GROUP):
                    val = jnp.sum(sink_rows[QB * h:QB * (h + 1), :])
                    dsink_acc = dsink_acc + jnp.where((r8 == 0) & (l8 == GROUP * kv + h), val, 0.0)
                dq = _rows_to_heads(_dot(ds, k_cat, NN)) * q_scale
                dq_ref[:, lanes] = rope_back(dq, tc).astype(bf16)
                dk = _dot(ds, q_rows, TN)
                dk_prev.append(dk[:QB, :])
                dk_cur.append(dk[QB:, :])
                dv_prev.append(dv[:QB, :])
                dv_cur.append(dv[QB:, :])
            dsink_ref[...] += dsink_acc
            dkp = rope_back(jnp.concatenate(dk_prev, axis=1), tp)
            dkc = rope_back(jnp.concatenate(dk_cur, axis=1), tc)
            dkv_ref[:, 0:256] = (carry_k[...] + dkp).astype(bf16)
            dkv_ref[:, 256:512] = (carry_v[...] + jnp.concatenate(dv_prev, axis=1)).astype(bf16)
            carry_k[...] = dkc
            carry_v[...] = jnp.concatenate(dv_cur, axis=1)

        @pl.when(j == nq)
        def _():
            dkv_ref[:, 0:256] = carry_k[...].astype(bf16)
            dkv_ref[:, 256:512] = carry_v[...].astype(bf16)

    lag = lambda b, j: (b * nq + jnp.maximum(j - 1, 0), 0)
    args = [proj] * 7 + [sinks, dy_attn] + list(tabs) + list(tabs) + list(chip_sums)
    res = _pcall(
        body, name="attn_backward", grid=(nb, nq + 1), in_specs=specs + [ANY] * nex,
        out_specs=(pl.BlockSpec((QB, D), lambda b, j: (cur(b, j), 0)), pl.BlockSpec((QB, 512), lag),
                   pl.BlockSpec((QB, D), lambda b, j: (cur(b, j), 0)), pl.BlockSpec((8, 128), lambda b, j: (0, 0)))
        + tuple([ANY] * nex),
        out_shape=(_sds((T, D), bf16), _sds((T, 512), bf16), _sds((T, D), bf16), _sds((8, 128), f32))
        + tuple(_sds(s.shape, s.dtype) for s in chip_sums),
        scratch_shapes=[pltpu.VMEM((QB, 256), f32), pltpu.VMEM((QB, 256), f32)] + _exchange_scratch(nex, 3),
        compiler_params=_params(("arbitrary", "arbitrary")),
    )(*args)
    return res[:4], res[4:]


def _lru_backward(proj, h_all, dy_rnn, cw_full, conv_b, w_a, b_a, w_x, b_x, lam, S):
    T = proj.shape[0]
    nb = T // S
    col, vec, wblk, cwblk = _lru_specs(S, nb)
    tokblk = pl.BlockSpec((S, RB), lambda n, b: (b, n))

    def body(x0_ref, g_ref, h_ref, dy_ref, cw_ref, cb_ref, wa_ref, ba_ref, wx_ref, bx_ref, lam_ref,
             du0_ref, dg_ref, gwa_ref, gwx_ref, vec_ref, gcw_ref, a_s, b_s, dh_s, edge_s):
        @pl.when(pl.program_id(1) == 0)
        def _():
            gwa_ref[...] = jnp.zeros_like(gwa_ref)
            gwx_ref[...] = jnp.zeros_like(gwx_ref)
            vec_ref[...] = jnp.zeros_like(vec_ref)
            gcw_ref[...] = jnp.zeros_like(gcw_ref)

        x0 = x0_ref[...].astype(f32)
        cw = cw_ref[...]
        lam_v = lam_ref[...]
        u, ub, r, i, sp, a, mult, inv_mult, taps = _lru_gates(x0, cw, cb_ref[...], wa_ref[...], ba_ref[...],
                                                              wx_ref[...], bx_ref[...], lam_v)
        h = h_ref[...]
        g = g_ref[...].astype(f32)
        dy = dy_ref[...].astype(f32)
        sg = _sigmoid(g)
        dg_ref[...] = (dy * h * (sg * (1.0 + g * (1.0 - sg)))).astype(bf16)
        _linear_scan(_shift_up(a, 1), dy * (g * sg), a_s, b_s, edge_s, dh_s, reverse=True)
        dh_total = dh_s[...]
        da = dh_total * _shift_down(h, 1)
        dmult = dh_total * (i * u)
        db = dh_total * mult
        di = db * u
        du = db * i
        dlog_a_c = ((-LRU_C) * a) * (da - dmult * (a * inv_mult))
        dr = dlog_a_c * sp
        dsp = jnp.sum(dlog_a_c * r, axis=0, keepdims=True)
        dpre_r = dr * r * (1.0 - r)
        dpre_i = di * i * (1.0 - i)
        dpre_rb = dpre_r.astype(bf16)
        dpre_ib = dpre_i.astype(bf16)
        du = du + _dot(dpre_rb, wa_ref[...].astype(bf16), NT) + _dot(dpre_ib, wx_ref[...].astype(bf16), NT)
        gwa_ref[...] += _dot(ub, dpre_rb, TN)
        gwx_ref[...] += _dot(ub, dpre_ib, TN)
        vec_ref[0:1, :] += jnp.sum(du, axis=0, keepdims=True)
        vec_ref[1:2, :] += jnp.sum(dpre_r, axis=0, keepdims=True)
        vec_ref[2:3, :] += jnp.sum(dpre_i, axis=0, keepdims=True)
        vec_ref[3:4, :] += dsp * (-_sigmoid(-lam_v))
        dx0 = cw[3:4, :] * du
        for k in range(3):
            dx0 = dx0 + cw[k:k + 1, :] * _shift_up(du, 3 - k)
        for k in range(4):
            gcw_ref[k:k + 1, :] += jnp.sum(du * taps[k], axis=0, keepdims=True)
        du0_ref[...] = dx0.astype(bf16)

    wacc = pl.BlockSpec((RB, RB), lambda n, b: (0, n))
    vacc = pl.BlockSpec((8, RB), lambda n, b: (0, n))
    cacc = pl.BlockSpec((8, RB), lambda n, b: (n, 0))
    return _pcall(
        body, name="lru_backward", grid=(RNN_BLOCKS, nb),
        in_specs=[col(0), col(8), tokblk, tokblk, cwblk, vec, wblk, vec, wblk, vec, vec],
        out_specs=(tokblk, tokblk, wacc, wacc, vacc, cacc),
        out_shape=(_sds((T, D), bf16), _sds((T, D), bf16), _sds((RB, D), f32), _sds((RB, D), f32),
                   _sds((8, D), f32), _sds((8 * RNN_BLOCKS, RB), f32)),
        scratch_shapes=[pltpu.VMEM((S, RB), f32)] * 3 + [pltpu.VMEM((S // 8, RB), f32)],
        compiler_params=_params(("arbitrary", "arbitrary")),
    )(proj, proj, h_all, dy_rnn, cw_full, conv_b, w_a, b_a, w_x, b_x, lam)


def _section_of_chunk(s):
    out = []
    for start, n in zip(SEC_START, SEC_CHUNKS):
        inside = (s >= start) & (s < start + n)
        out.append((inside, jnp.clip(s - start, 0, n - 1)))
    return out


EFFECT = pltpu.SideEffectType.DATAFLOW_SIDE_EFFECTING
HBM_SPEC = pl.BlockSpec(memory_space=pltpu.HBM)
SEM_SPEC = pl.BlockSpec(memory_space=pltpu.SEMAPHORE)


def _split_exchange_copies(src_ref, land_ref, send_sems, recv_sems):
    x, y, c = _my_place()
    copies = []
    for k in (3, 1, 2):
        px, py = (x + (k >> 1)) % 2, (y + (k & 1)) % 2
        copies.append(pltpu.make_async_remote_copy(
            src_ref=src_ref.at[2 * px + py], dst_ref=land_ref.at[k - 1], send_sem=send_sems[k - 1],
            recv_sem=recv_sems[k - 1], device_id=(px, py, c), device_id_type=MESH))
    return copies


def _exchange_start(chip_sum):
    _, r, cols = chip_sum.shape

    def body(src_ref, land_ref, s0, s1, s2, r0, r1, r2, src_thru, land_thru, token):
        for cp in _split_exchange_copies(src_ref, land_ref, (s0, s1, s2), (r0, r1, r2)):
            cp.start()
        token[...] = jnp.zeros_like(token)

    land = pltpu.with_memory_space_constraint(lax.empty((3, r, cols), chip_sum.dtype), pltpu.HBM)
    res = _pcall(
        body, name="exchange_start",
        out_shape=tuple([pltpu.SemaphoreType.DMA(())] * 6) + (
            pltpu.HBM(chip_sum.shape, chip_sum.dtype), pltpu.HBM((3, r, cols), chip_sum.dtype), _sds((8, 128), f32)),
        in_specs=(HBM_SPEC, HBM_SPEC), out_specs=tuple([SEM_SPEC] * 6) + (HBM_SPEC, HBM_SPEC, VMEM_SPEC),
        input_output_aliases={0: 6, 1: 7},
        compiler_params=pltpu.CompilerParams(has_side_effects=EFFECT),
    )(pltpu.with_memory_space_constraint(chip_sum, pltpu.HBM), land)
    return res[:6], res[6], res[7], res[8]


def _exchange_wait(sems, src_thru, land_thru, after):
    def body(src_ref, land_ref, s0, s1, s2, r0, r1, r2, after_ref, src_dead, got_ref):
        for cp in _split_exchange_copies(src_ref, land_ref, (s0, s1, s2), (r0, r1, r2)):
            cp.wait_send()
            cp.wait_recv()

    return _pcall(
        body, name="exchange_wait",
        out_shape=(pltpu.HBM(src_thru.shape, src_thru.dtype), pltpu.HBM(land_thru.shape, land_thru.dtype)),
        in_specs=(HBM_SPEC, HBM_SPEC) + tuple([SEM_SPEC] * 6) + (ANY,), out_specs=(HBM_SPEC, HBM_SPEC),
        input_output_aliases={0: 0, 1: 1},
        compiler_params=pltpu.CompilerParams(has_side_effects=EFFECT),
    )(src_thru, land_thru, *sems, after)[1]


def _input_grad(dsecs, wt_full, x2d, dx2, norm_g):
    T = x2d.shape[0]
    tb = min(T, 512)
    nsec = len(dsecs)
    ntok = T // tb

    def body(*refs):
        secs = refs[:nsec]
        wt_ref, x_ref, dx2_ref, g_ref, dx_ref, gnorm_ref = refs[nsec:]
        i = pl.program_id(0)

        @pl.when(i == 0)
        def _():
            gnorm_ref[...] = jnp.zeros_like(gnorm_ref)

        dh = None
        for a, (start, n) in enumerate(zip(SEC_START, SEC_CHUNKS)):
            part = _dot(secs[a][...], wt_ref[CH * start:CH * (start + n), :], NN)
            dh = part if dh is None else dh + part
        xv = x_ref[...]
        rstd = lax.rsqrt(jnp.mean(xv * xv, axis=-1, keepdims=True) + EPS)
        xh = xv * rstd
        gnorm_ref[0:1, :] += jnp.sum(dh * xh, axis=0, keepdims=True)
        dxn = dh * g_ref[...]
        dx_ref[...] = dx2_ref[...] + rstd * (dxn - xh * jnp.mean(dxn * xh, axis=-1, keepdims=True))

    tok = pl.BlockSpec((tb, D), lambda i: (i, 0))
    return _pcall(
        body, name="input_grad", grid=(ntok,),
        in_specs=[pl.BlockSpec((tb, sec.shape[1]), lambda i: (i, 0)) for sec in dsecs]
        + [pl.BlockSpec((D_IN, D), lambda i: (0, 0), pipeline_mode=pl.Buffered(1)), tok, tok,
           pl.BlockSpec((1, D), lambda i: (0, 0))],
        out_specs=(tok, pl.BlockSpec((8, D), lambda i: (0, 0))),
        out_shape=(_sds((T, D), f32), _sds((8, D), f32)),
        compiler_params=_params(("arbitrary",)),
    )(*dsecs, wt_full, x2d, dx2, norm_g)


def _w_in_grad(dsecs, h_bf):
    T = h_bf.shape[0]
    tk = min(T, 2048)
    nchunks = D_IN // CH
    nsec = len(dsecs)
    nt = T // tk

    def body(*refs):
        secs = refs[:nsec]
        h_ref, out_ref, acc = refs[nsec:]
        s, t = pl.program_id(0), pl.program_id(1)

        @pl.when(t == 0)
        def _():
            acc[...] = jnp.zeros_like(acc)

        h_rows = h_ref[pl.ds(pl.multiple_of(t * tk, tk), tk), :]
        for a, (start, n) in enumerate(zip(SEC_START, SEC_CHUNKS)):
            @pl.when((s >= start) & (s < start + n))
            def _(a=a):
                acc[...] += _dot(secs[a][...], h_rows, TN)

        @pl.when(t == nt - 1)
        def _():
            out_ref[...] = acc[...].astype(bf16)

    def sec_spec(a):
        def index(s, t, a=a):
            inside, local = _section_of_chunk(s)[a]
            return (jnp.where(inside, t, 0), local)
        return pl.BlockSpec((tk, CH), index)

    return _pcall(
        body, name="w_in_grad", grid=(nchunks, T // tk),
        in_specs=[sec_spec(a) for a in range(nsec)]
        + [pl.BlockSpec((T, D), lambda s, t: (0, 0), pipeline_mode=pl.Buffered(1))],
        out_specs=pl.BlockSpec((CH, D), lambda s, t: (s, 0)), out_shape=_sds((D_IN, D), bf16),
        scratch_shapes=[pltpu.VMEM((CH, D), f32)],
        compiler_params=_params(("arbitrary", "arbitrary")),
    )(*dsecs, h_bf)


SMALL_NAMES = ("lru_w_a", "lru_w_x", "conv_b", "lru_b_a", "lru_b_x", "lru_lambda", "norm_g", "final_norm_g",
               "attn_sinks", "conv_w")
MISC_ROW = {"conv_b": 0, "lru_b_a": 1, "lru_b_x": 2, "lru_lambda": 3, "norm_g": 8, "final_norm_g": 16,
            "attn_sinks": 24, "loss": 32}


def _small_step(gwa, gwx, gvec, gnorm_blk, gfin_blk, dsink_blk, loss_blk, gcw, params):
    srcs_rows = (RB // NDEV, RB // NDEV, 8, 8)
    flat = [t for n in SMALL_NAMES for t in params[n]]
    nout = 4 * len(SMALL_NAMES) + 1

    def reduce_body(gwa_ref, gwx_ref, gvec_ref, gnorm_ref, gfin_ref, dsink_ref, loss_ref, gcw_ref,
                    all_a, all_x, all_m, conv_out,
                    misc, got_a, got_x, got_m, got_c, red_a, red_x, red_m, sa, ra, sb, rb):
        x, y, c = _my_place()
        me = 4 * x + 2 * y + c

        misc[...] = jnp.zeros_like(misc)
        misc[0:8, :] = gvec_ref[...]
        misc[8:16, :] = gnorm_ref[...]
        misc[16:24, :] = gfin_ref[...]
        misc[24:32, 0:128] = dsink_ref[...]
        misc[32:40, :] = loss_ref[...]

        srcs = (gwa_ref, gwx_ref, misc, gcw_ref)
        gots = (got_a, got_x, got_m, got_c)

        def shard(ref, rows, dev):
            return ref.at[pl.ds(pl.multiple_of(dev * rows, 8), rows), :]

        scatter = []
        for k in range(1, NDEV):
            px, py, pc = _peer(k)
            for a in range(4):
                scatter.append(pltpu.make_async_remote_copy(
                    src_ref=shard(srcs[a], srcs_rows[a], 4 * px + 2 * py + pc), dst_ref=gots[a].at[k - 1],
                    send_sem=sa.at[4 * (k - 1) + a], recv_sem=ra.at[4 * (k - 1) + a],
                    device_id=(px, py, pc), device_id_type=MESH))
        for cp in scatter:
            cp.start()
        for cp in scatter:
            cp.wait()

        def reduced(a):
            rows = srcs_rows[a]
            total = srcs[a][pl.ds(pl.multiple_of(me * rows, 8), rows), :]
            for k in range(NDEV - 1):
                total = total + gots[a][k]
            return total

        reds = (red_a, red_x, red_m)
        alls = (all_a, all_x, all_m)
        for a in range(3):
            val = reduced(a)
            reds[a][...] = val
            alls[a][pl.ds(pl.multiple_of(me * srcs_rows[a], 8), srcs_rows[a]), :] = val
        gather = []
        for k in range(1, NDEV):
            peer = _peer(k)
            for a in range(3):
                gather.append(pltpu.make_async_remote_copy(
                    src_ref=reds[a], dst_ref=shard(alls[a], srcs_rows[a], me),
                    send_sem=sb.at[3 * (k - 1) + a], recv_sem=rb.at[3 * (k - 1) + a],
                    device_id=peer, device_id_type=MESH))
        for cp in gather:
            cp.start()
        conv_out[...] = reduced(3)
        for cp in gather:
            cp.wait()

    def adam_body(*refs):
        all_a, all_x, all_m, conv_ref = refs[:4]
        prm = {n: refs[4 + 3 * k:7 + 3 * k] for k, n in enumerate(SMALL_NAMES)}
        nin = 4 + len(flat)
        outs = {n: refs[nin + 4 * k:nin + 4 * k + 4] for k, n in enumerate(SMALL_NAMES)}
        loss_out = refs[nin + nout - 1]
        g_conv = conv_ref[0:4, :]

        def update(name, g, pick=lambda r: r[...]):
            w_ref, m_ref, v_ref = prm[name]
            delta, m_new, v_new = _adam_math(g, pick(w_ref), pick(m_ref), pick(v_ref))
            return g, delta, m_new, v_new

        for n in range(RNN_BLOCKS):
            lanes = slice(RB * n, RB * (n + 1))
            for name, full in (("lru_w_a", all_a), ("lru_w_x", all_x)):
                for out, val in zip(outs[name], update(name, full[:, lanes], pick=lambda r, n=n: r[n])):
                    out[n] = val
        for name in ("conv_b", "lru_b_a", "lru_b_x", "lru_lambda", "norm_g", "final_norm_g"):
            row = MISC_ROW[name]
            for out, val in zip(outs[name], update(name, all_m[row:row + 1, :])):
                out[...] = val
        row = MISC_ROW["attn_sinks"]
        for out, val in zip(outs["attn_sinks"], update("attn_sinks", all_m[row:row + 1, 0:16])):
            out[...] = val
        for out, val in zip(outs["conv_w"], update("conv_w", g_conv)):
            out[...] = val
        row = MISC_ROW["loss"]
        loss_out[...] = all_m[row:row + 8, 0:128] * (0.5 / D)

    scratch = [pltpu.VMEM((64, D), f32),
               pltpu.VMEM((NDEV - 1, RB // NDEV, D), f32), pltpu.VMEM((NDEV - 1, RB // NDEV, D), f32),
               pltpu.VMEM((NDEV - 1, 8, D), f32), pltpu.VMEM((NDEV - 1, 8, RB), f32),
               pltpu.VMEM((RB // NDEV, D), f32), pltpu.VMEM((RB // NDEV, D), f32), pltpu.VMEM((8, D), f32),
               pltpu.SemaphoreType.DMA((4 * (NDEV - 1),)), pltpu.SemaphoreType.DMA((4 * (NDEV - 1),)),
               pltpu.SemaphoreType.DMA((3 * (NDEV - 1),)), pltpu.SemaphoreType.DMA((3 * (NDEV - 1),))]
    sums = _pcall(
        reduce_body, name="small_reduce",
        out_shape=(_sds((RB, D), f32), _sds((RB, D), f32), _sds((64, D), f32), _sds((8, RB), f32)),
        in_specs=[VMEM_SPEC] * 8, out_specs=tuple([VMEM_SPEC] * 4),
        scratch_shapes=scratch, compiler_params=_params(),
    )(gwa, gwx, gvec, gnorm_blk, gfin_blk, dsink_blk, loss_blk, gcw)
    out_shape = tuple(_sds(params[n][0].shape, f32) for n in SMALL_NAMES for _ in range(4)) + (_sds((8, 128), f32),)
    res = _pcall(
        adam_body, name="small_adamw", out_shape=out_shape,
        in_specs=[VMEM_SPEC] * (4 + len(flat)), out_specs=tuple([VMEM_SPEC] * nout), compiler_params=_params(),
    )(*sums, *flat)
    return {n: res[4 * k:4 * k + 4] for k, n in enumerate(SMALL_NAMES)}, res[-1]


def _pad_rows(v, rows=8):
    return jnp.concatenate([v, jnp.zeros((rows - v.shape[0], v.shape[1]), v.dtype)], axis=0)


def kernel(x, norm_g, w_in, conv_w, conv_b, lru_w_a, lru_b_a, lru_w_x, lru_b_x, lru_lambda, attn_sinks, w_rnn_out, w_attn_out, w_o, final_norm_g, loss_target, m_norm_g, m_w_in, m_conv_w, m_conv_b, m_lru_w_a, m_lru_b_a, m_lru_w_x, m_lru_b_x, m_lru_lambda, m_attn_sinks, m_w_rnn_out, m_w_attn_out, m_w_o, m_final_norm_g, v_norm_g, v_w_in, v_conv_w, v_conv_b, v_lru_w_a, v_lru_b_a, v_lru_w_x, v_lru_b_x, v_lru_lambda, v_attn_sinks, v_w_rnn_out, v_w_attn_out, v_w_o, v_final_norm_g):
    nb, S, _ = x.shape
    T = nb * S
    x2d = x.reshape(T, D)
    tgt = loss_target.reshape(T, D)
    fin_g = final_norm_g.reshape(1, D)
    w_a3, w_x3 = lru_w_a[0], lru_w_x[0]

    my_core = lax.axis_index("c").astype(jnp.int32).reshape(1)
    cx, cy = lax.axis_index("x"), lax.axis_index("y")
    chip_order = jnp.stack([2 * cx + cy, 2 * (1 - cx) + cy, 2 * cx + (1 - cy),
                            2 * (1 - cx) + (1 - cy)]).astype(jnp.int32)

    tabs = _rope_tables(S)
    h_bf, proj, wt_full, cw_full, _ = _in_proj_gather(
        x2d, norm_g, w_in[0].T.astype(bf16), _pad_rows(conv_w[0]), tabs, S, (), chip_order)
    y_rnn, h_all = _lru_forward(proj, cw_full, conv_b, w_a3, lru_b_a, w_x3, lru_b_x, lru_lambda, S)
    y_attn, (wr_full, wa_full, wo_full) = _attn_forward(proj, attn_sinks, S,
                                                        (w_rnn_out[0], w_attn_out[0], w_o[0]))

    (dx2, dy_rnn, dy_attn, dmr, dma, loss_blk, gfin_blk, g_wr, g_wa, g_wo) = _merge_and_head(
        x2d, tgt, proj, y_rnn, y_attn, wr_full, wa_full, wo_full, fin_g)
    sums_out = _pair_sums([g_wr, g_wa, g_wo], bf16, my_core, "out")

    (dq, dkv, dga, dsink_blk), (p_wr, p_wa, p_wo) = _attn_backward(proj, dy_attn, tabs, attn_sinks, S, sums_out)
    du0, dgr, gwa, gwx, gvec, gcw = _lru_backward(proj, h_all, dy_rnn, cw_full, conv_b, w_a3, lru_b_a, w_x3,
                                                  lru_b_x, lru_lambda, S)
    dsecs = (du0, dgr, dq, dkv, dga, dmr, dma)

    g_wt = _w_in_grad(dsecs, h_bf)
    (sum_in,) = _pair_sums([g_wt], bf16, my_core, "in")
    ex_sems, sum_in, landing, token = _exchange_start(sum_in)
    grad_x2d, gnorm_blk = _input_grad(dsecs, wt_full, x2d, dx2, norm_g + token[0, 0])
    p_wt = _exchange_wait(ex_sems, sum_in, landing, gnorm_blk)
    p_wt_own = lax.dynamic_index_in_dim(sum_in, 2 * cx + cy, axis=0, keepdims=False)

    small, loss_out = _small_step(gwa, gwx, gvec, gnorm_blk, gfin_blk, dsink_blk, loss_blk, gcw, {
        "lru_w_a": (w_a3, m_lru_w_a[0], v_lru_w_a[0]), "lru_w_x": (w_x3, m_lru_w_x[0], v_lru_w_x[0]),
        "conv_b": (conv_b, m_conv_b, v_conv_b), "lru_b_a": (lru_b_a, m_lru_b_a, v_lru_b_a),
        "lru_b_x": (lru_b_x, m_lru_b_x, v_lru_b_x), "lru_lambda": (lru_lambda, m_lru_lambda, v_lru_lambda),
        "norm_g": (norm_g, m_norm_g, v_norm_g),
        "final_norm_g": (fin_g, m_final_norm_g.reshape(1, D), v_final_norm_g.reshape(1, D)),
        "attn_sinks": (attn_sinks, m_attn_sinks, v_attn_sinks),
        "conv_w": (conv_w[0], m_conv_w[0], v_conv_w[0])})

    o_wt = _adamw(p_wt_own, p_wt, w_in[0].T, m_w_in[0].T, v_w_in[0].T, "adamw_w_in")
    o_wr, o_wa, o_wo = _adamw_group(
        (p_wr, p_wa, p_wo), (w_rnn_out[0], w_attn_out[0], w_o[0]),
        (m_w_rnn_out[0], m_w_attn_out[0], m_w_o[0]), (v_w_rnn_out[0], v_w_attn_out[0], v_w_o[0]), "adamw_w_out")

    def result(kind):
        d = {n: small[n][kind] for n in ("conv_b", "lru_b_a", "lru_b_x", "lru_lambda", "norm_g", "attn_sinks")}
        d.update({n: small[n][kind][None] for n in ("lru_w_a", "lru_w_x", "conv_w")})
        d["final_norm_g"] = small["final_norm_g"][kind].reshape(D)
        d.update({"w_in": o_wt[kind].T[None], "w_rnn_out": o_wr[kind][None], "w_attn_out": o_wa[kind][None],
                  "w_o": o_wo[kind][None]})
        return d

    order = ("norm_g", "w_in", "conv_w", "conv_b", "lru_w_a", "lru_b_a", "lru_w_x", "lru_b_x", "lru_lambda",
             "attn_sinks", "w_rnn_out", "w_attn_out", "w_o", "final_norm_g")
    outs = [loss_out[0, 0], grad_x2d.reshape(nb, S, D)]
    for kind in range(4):
        d = result(kind)
        outs += [d[n] for n in order]
    return tuple(outs)
```

```python
import functools
import math

import jax
import jax.numpy as jnp
from jax import lax
from jax.experimental import pallas as pl
from jax.experimental.pallas import tpu as pltpu

f32 = jnp.float32
bf16 = jnp.bfloat16

D = 1024
D_IN = 6656
NDEV = 8
RNN_BLOCKS = 8
RB = 128
HEAD = 64
KV_HEADS = 4
GROUP = 4
QB = 128
LRU_C = 8.0
EPS = 1e-6
ROPE_DIM = 16
ROPE_THETA = 500000.0
CH = 512
SEC_START = (0, 2, 4, 6, 7, 9, 11)
SEC_CHUNKS = (2, 2, 2, 1, 2, 2, 2)
VMEM_LIMIT = 62 * 1024 * 1024

ADAM_LR, ADAM_B1, ADAM_B2, ADAM_EPS, ADAM_WD, ADAM_STEP = 0.001, 0.9, 0.999, 1e-08, 0.01, 10

MESH = pl.DeviceIdType.MESH
ANY = pl.BlockSpec(memory_space=pl.ANY)
VMEM_SPEC = pl.BlockSpec(memory_space=pltpu.VMEM)
SMEM_SPEC = pl.BlockSpec(memory_space=pltpu.SMEM)


def _pcall(body, **kw):
    return pl.pallas_call(body, **kw)


def _params(sem=None, **kw):
    if sem is not None:
        kw["dimension_semantics"] = sem
    return pltpu.CompilerParams(vmem_limit_bytes=VMEM_LIMIT, **kw)


def _sds(shape, dtype):
    return jax.ShapeDtypeStruct(shape, dtype)


def _dot(a, b, dims):
    return lax.dot_general(a, b, (dims, ((), ())), preferred_element_type=f32)


NN = ((1,), (0,))
NT = ((1,), (1,))
TN = ((0,), (0,))


def _sigmoid(v):
    return 0.5 * jnp.tanh(0.5 * v) + 0.5


def _sigmoid_positive(v):
    return 1.0 / (1.0 + jnp.exp(-v))


def _my_place():
    return lax.axis_index("x"), lax.axis_index("y"), lax.axis_index("c")


def _peer(k):
    x, y, c = _my_place()
    return (x + ((k >> 2) & 1)) % 2, (y + ((k >> 1) & 1)) % 2, (c + (k & 1)) % 2


def _direct_gather_copies(srcs, outs, send_sems, recv_sems, local_sems):
    x, y, c = _my_place()
    me = 4 * x + 2 * y + c
    local, remote = [], []
    for a, (src, out) in enumerate(zip(srcs, outs)):
        r = src.shape[0]
        mine = out.at[pl.ds(pl.multiple_of(me * r, 8), r), :]
        local.append(pltpu.make_async_copy(src, mine, local_sems.at[a]))
        for k in range(1, NDEV):
            remote.append(pltpu.make_async_remote_copy(
                src_ref=src, dst_ref=mine, send_sem=send_sems.at[7 * a + k - 1], recv_sem=recv_sems.at[7 * a + k - 1],
                device_id=_peer(k), device_id_type=MESH))
    return local, remote


def _chip_exchange_copies(src, dst, send_sems, recv_sems, local_sems):
    x, y, c = _my_place()
    local, remote = [], []
    for a in range(len(src)):
        local.append(pltpu.make_async_copy(src[a].at[2 * x + y], dst[a].at[0], local_sems.at[a]))
    for k in (3, 1, 2):
        px, py = (x + (k >> 1)) % 2, (y + (k & 1)) % 2
        for a in range(len(src)):
            remote.append(pltpu.make_async_remote_copy(
                src_ref=src[a].at[2 * px + py], dst_ref=dst[a].at[k],
                send_sem=send_sems.at[3 * a + k - 1], recv_sem=recv_sems.at[3 * a + k - 1],
                device_id=(px, py, c), device_id_type=MESH))
    return local, remote


def _exchange_scratch(narr, per_array):
    return [pltpu.SemaphoreType.DMA((per_array * narr,)), pltpu.SemaphoreType.DMA((per_array * narr,)),
            pltpu.SemaphoreType.DMA((narr,))]


def _start_all(copies):
    local, remote = copies
    for cp in local + remote:
        cp.start()


def _wait_all(copies):
    local, remote = copies
    for cp in remote + local:
        cp.wait()


def _row_tile(rows, dtype):
    unit = 16 if dtype == bf16 else 8
    for cand in (256, 208, 128, 64, 40, 32, 16, 8):
        if rows % cand == 0 and cand % unit == 0:
            return cand
    return rows


def _pair_sums(grads, wire_dtype, my_core, tag):
    narr = len(grads)
    r, cols = grads[0].shape[0] // NDEV, grads[0].shape[1]
    views = [g.reshape(4, 2, r, cols) for g in grads]
    tr = _row_tile(r, wire_dtype)
    nt = r // tr

    def body(core_ref, *refs):
        mine = refs[:narr]
        whole = refs[narr:2 * narr]
        outs = refs[2 * narr:3 * narr]
        got = refs[3 * narr:4 * narr]
        send_sems, recv_sems = refs[4 * narr:]
        q, i = pl.program_id(0), pl.program_id(1)
        x, y, c = _my_place()

        def copy(a, chip):
            return pltpu.make_async_remote_copy(
                src_ref=whole[a].at[chip, 1 - c], dst_ref=got[a].at[chip],
                send_sem=send_sems.at[4 * a + chip], recv_sem=recv_sems.at[4 * a + chip],
                device_id=(x, y, 1 - c), device_id_type=MESH)

        @pl.when((q == 0) & (i == 0))
        def _():
            for chip in range(4):
                for a in range(narr):
                    copy(a, chip).start()

        for chip in range(4):
            @pl.when((q == chip) & (i == 0))
            def _(chip=chip):
                for a in range(narr):
                    copy(a, chip).wait_recv()

        rows = pl.ds(pl.multiple_of(i * tr, tr), tr)
        for a in range(narr):
            outs[a][...] = (mine[a][...].astype(f32) + got[a][q, rows, :].astype(f32)).astype(wire_dtype)

        @pl.when((q == 3) & (i == nt - 1))
        def _():
            for chip in range(4):
                for a in range(narr):
                    copy(a, chip).wait_send()

    slab = pl.BlockSpec((None, tr, cols), lambda q, i, core: (q, i, 0))
    grid_spec = pltpu.PrefetchScalarGridSpec(
        num_scalar_prefetch=1, grid=(4, nt),
        in_specs=[pl.BlockSpec((None, None, tr, cols), lambda q, i, core: (q, core[0], i, 0))] * narr + [ANY] * narr,
        out_specs=tuple([slab] * narr),
        scratch_shapes=[pltpu.VMEM((4, r, cols), grads[0].dtype)] * narr
        + [pltpu.SemaphoreType.DMA((4 * narr,)), pltpu.SemaphoreType.DMA((4 * narr,))])
    return _pcall(body, name="pair_sums_" + tag, grid_spec=grid_spec,
                  out_shape=tuple(_sds((4, r, cols), wire_dtype) for _ in range(narr)),
                  compiler_params=_params(("arbitrary", "arbitrary")))(my_core, *views, *views)


def _adam_math(g, w, m, v):
    m_new = ADAM_B1 * m + (1.0 - ADAM_B1) * g
    v_new = ADAM_B2 * v + (1.0 - ADAM_B2) * (g * g)
    m_hat = m_new / (1.0 - ADAM_B1 ** ADAM_STEP)
    v_hat = v_new / (1.0 - ADAM_B2 ** ADAM_STEP)
    return -ADAM_LR * (m_hat / (jnp.sqrt(v_hat) + ADAM_EPS) + ADAM_WD * w), m_new, v_new


def _adamw(first, parts, w, m, v, name):
    n, rows, cols = parts.shape
    tr = _row_tile(rows, parts.dtype)

    def body(f_ref, p_ref, w_ref, m_ref, v_ref, g_out, d_out, m_out, v_out):
        g = f_ref[...].astype(f32)
        for s in range(n):
            g = g + p_ref[s].astype(f32)
        g_out[...] = g
        d_out[...], m_out[...], v_out[...] = _adam_math(g, w_ref[...], m_ref[...], v_ref[...])

    blk = pl.BlockSpec((tr, cols), lambda i: (i, 0))
    return _pcall(
        body, name=name, grid=(rows // tr,),
        in_specs=[blk, pl.BlockSpec((n, tr, cols), lambda i: (0, i, 0)), blk, blk, blk],
        out_specs=(blk, blk, blk, blk), out_shape=tuple(_sds((rows, cols), f32) for _ in range(4)),
        compiler_params=_params(("arbitrary",)),
    )(first, parts, w, m, v)


def _adamw_group(parts, ws, ms, vs, name):
    nw = len(ws)

    def body(*refs):
        p_refs, w_refs, m_refs, v_refs = (refs[k * nw:(k + 1) * nw] for k in range(4))
        outs = refs[4 * nw:]
        for k in range(nw):
            g = p_refs[k][0].astype(f32)
            for s in range(1, p_refs[k].shape[0]):
                g = g + p_refs[k][s].astype(f32)
            g_out, d_out, m_out, v_out = outs[4 * k:4 * k + 4]
            g_out[...] = g
            d_out[...], m_out[...], v_out[...] = _adam_math(g, w_refs[k][...], m_refs[k][...], v_refs[k][...])

    res = _pcall(
        body, name=name, out_shape=tuple(_sds(w.shape, f32) for w in ws for _ in range(4)),
        in_specs=[VMEM_SPEC] * (4 * nw), out_specs=tuple([VMEM_SPEC] * (4 * nw)), compiler_params=_params(),
    )(*parts, *ws, *ms, *vs)
    return [res[4 * k:4 * k + 4] for k in range(nw)]


def _rope(t, c, s1, s2):
    w = t.shape[1]
    return t * c + pltpu.roll(t, w - 8, 1) * s1 + pltpu.roll(t, 8, 1) * s2


def _rope_transposed(dt, c, s1, s2):
    w = dt.shape[1]
    return dt * c + pltpu.roll(dt * s1, 8, 1) + pltpu.roll(dt * s2, w - 8, 1)


PAIR_ROWS = D_IN // 4
SUB_COLS = ((0, 512), (512, 512), (1024, 512), (1536, 128))
Q_SLABS = range(3, 11)
K_SLABS = range(11, 13)


def _in_proj_gather(x2d, norm_g, wt_shard, cw_shard, tabs, S, out_shards, chip_order):
    T = x2d.shape[0]
    tb = min(S, 1024)
    ntok = T // tb
    nsb = S // tb
    q_scale = 1.0 / math.sqrt(HEAD)
    shard_rows = wt_shard.shape[0]
    small = (cw_shard,) + tuple(out_shards)
    nsm = len(small)

    def body(order_ref, x_ref, g_ref, c_ref, s1_ref, s2_ref, wt_hbm, *rest):
        small_in = rest[:nsm]
        h_ref, proj_ref, wt_out = rest[nsm:nsm + 3]
        small_out = rest[nsm + 3:2 * nsm + 3]
        wt_vm, h_vm = rest[2 * nsm + 3:2 * nsm + 5]
        stage = rest[2 * nsm + 5:3 * nsm + 4]
        wsend, wrecv, wlocal = rest[3 * nsm + 4:3 * nsm + 7]
        dsems = rest[3 * nsm + 7:]
        jj, i = pl.program_id(0), pl.program_id(1)
        x, y, c = _my_place()
        me, sibling = (x, y, c), (x, y, 1 - c)
        chips = [(1 - x, y), (x, 1 - y), (1 - x, 1 - y)]

        def rows(place):
            px, py, pc = place
            return wt_vm.at[pl.ds(pl.multiple_of((4 * px + 2 * py + pc) * shard_rows, 16), shard_rows), :]

        def copy(k, block, to, src=None):
            return pltpu.make_async_remote_copy(
                src_ref=rows(block) if src is None else src, dst_ref=rows(block),
                send_sem=wsend.at[k], recv_sem=wrecv.at[k], device_id=to, device_id_type=MESH)

        def small_copies():
            srcs = (small_in[0],) + tuple(stage)
            return _direct_gather_copies(srcs, small_out, *dsems)

        own = pltpu.make_async_copy(wt_hbm, rows(me), wlocal.at[0])
        keep = pltpu.make_async_copy(wt_vm, wt_out, wlocal.at[1])

        @pl.when((jj == 0) & (i == 0))
        def _():
            own.start()
            copy(0, me, sibling, src=wt_hbm).start()
            for j, chip in enumerate(chips):
                copy(1 + j, me, (*chip, c), src=wt_hbm).start()
            for a in range(nsm - 1):
                stage[a][...] = small_in[1 + a][...].astype(bf16)
            _start_all(small_copies())
            own.wait()
            copy(0, sibling, me).wait_recv()

        for j, chip in enumerate(chips):
            @pl.when((jj == 1 + j) & (i == 0))
            def _(j=j, chip=chip):
                copy(1 + j, (*chip, c), me).wait_recv()
                copy(4 + j, (*chip, c), sibling).start()
                copy(4 + j, (*chip, 1 - c), me).wait_recv()

        @pl.when((jj == 3) & (i == 0))
        def _():
            keep.start()

        @pl.when((jj == 3) & (i == ntok - 1))
        def _():
            copy(0, me, sibling, src=wt_hbm).wait_send()
            for j, chip in enumerate(chips):
                copy(1 + j, me, (*chip, c), src=wt_hbm).wait_send()
                copy(4 + j, (*chip, c), sibling).wait_send()
            _wait_all(small_copies())
            keep.wait()

        tok = pl.ds(pl.multiple_of(i * tb, tb), tb)

        @pl.when(jj == 0)
        def _():
            xv = x_ref[...]
            ms = jnp.mean(xv * xv, axis=-1, keepdims=True)
            hb = (xv * lax.rsqrt(ms + EPS) * g_ref[...]).astype(bf16)
            h_ref[...] = hb
            h_vm[tok, :] = hb

        block = order_ref[jj]
        hb = h_vm[tok, :]

        def piece(c0, w):
            w_rows = wt_vm[pl.ds(pl.multiple_of(block * PAIR_ROWS + c0, 128), w), :]
            return _dot(hb, w_rows, NT)

        @pl.when(block != 1)
        def _():
            for c0, w in SUB_COLS:
                proj_ref[:, c0:c0 + w] = piece(c0, w).astype(bf16)

        @pl.when(block == 1)
        def _():
            tab = (c_ref[...], s1_ref[...], s2_ref[...])
            for c0, w in SUB_COLS:
                acc = piece(c0, w)
                for l in range(w // 128):
                    slab = (c0 + 128 * l) // 128
                    part = acc[:, 128 * l:128 * (l + 1)]
                    if slab in Q_SLABS:
                        part = _rope(part, *tab) * q_scale
                    elif slab in K_SLABS:
                        part = _rope(part, *tab)
                    proj_ref[:, 128 * slab:128 * (slab + 1)] = part.astype(bf16)

    first_pass = lambda jj, i, order: (jnp.where(jj == 0, i, ntok - 1), 0)
    const = lambda jj, i, order: (0, 0)
    tab = pl.BlockSpec((tb, 128), lambda jj, i, order: (jnp.where(order[jj] == 1, i % nsb, 0), 0))
    grid_spec = pltpu.PrefetchScalarGridSpec(
        num_scalar_prefetch=1, grid=(4, ntok),
        in_specs=[pl.BlockSpec((tb, D), first_pass), pl.BlockSpec((1, D), const), tab, tab, tab, ANY]
        + [pl.BlockSpec(w.shape, const) for w in small],
        out_specs=(pl.BlockSpec((tb, D), first_pass),
                   pl.BlockSpec((tb, PAIR_ROWS), lambda jj, i, order: (i, order[jj])), ANY) + tuple([ANY] * nsm),
        scratch_shapes=[pltpu.VMEM((D_IN, D), bf16), pltpu.VMEM((T, D), bf16)]
        + [pltpu.VMEM(w.shape, bf16) for w in out_shards]
        + [pltpu.SemaphoreType.DMA((7,)), pltpu.SemaphoreType.DMA((7,)), pltpu.SemaphoreType.DMA((2,))]
        + _exchange_scratch(nsm, 7))
    res = _pcall(
        body, name="in_proj", grid_spec=grid_spec,
        out_shape=(_sds((T, D), bf16), _sds((T, D_IN), bf16), _sds((D_IN, D), bf16),
                   _sds((NDEV * cw_shard.shape[0], cw_shard.shape[1]), f32))
        + tuple(_sds((NDEV * w.shape[0], w.shape[1]), bf16) for w in out_shards),
        compiler_params=_params(("arbitrary", "arbitrary")),
    )(chip_order, x2d, norm_g, *tabs, wt_shard, *small)
    return res[0], res[1], res[2], res[3], res[4:]


def _rows_iota(shape):
    return lax.broadcasted_iota(jnp.int32, shape, 0)


def _shift_down(v, k):
    return jnp.where(_rows_iota(v.shape) >= k, pltpu.roll(v, k, 0), 0.0)


def _shift_up(v, k):
    n = v.shape[0]
    return jnp.where(_rows_iota(v.shape) < n - k, pltpu.roll(v, n - k, 0), 0.0)


def _linear_scan(a, b, a_s, b_s, edge_s, out_ref, reverse):
    n = a.shape[0]
    ng = n // 8
    a3, b3 = a.reshape(ng, 8, RB), b.reshape(ng, 8, RB)
    rid = lax.broadcasted_iota(jnp.int32, a3.shape, 1)
    for s in (1, 2, 4):
        keep, shift = (rid < 8 - s, 8 - s) if reverse else (rid >= s, s)
        b3 = jnp.where(keep, a3 * pltpu.roll(b3, shift, 1) + b3, b3)
        a3 = jnp.where(keep, a3 * pltpu.roll(a3, shift, 1), a3)
    a_s[...] = a3.reshape(n, RB)
    b_s[...] = b3.reshape(n, RB)
    edge = 0 if reverse else 7
    ea, eb = a_s[pl.ds(edge, ng, stride=8), :], b_s[pl.ds(edge, ng, stride=8), :]
    r = _rows_iota(ea.shape)
    s = 1
    while s < ng:
        keep, shift = (r < ng - s, ng - s) if reverse else (r >= s, s)
        eb = jnp.where(keep, ea * pltpu.roll(eb, shift, 0) + eb, eb)
        if 2 * s < ng:
            ea = jnp.where(keep, ea * pltpu.roll(ea, shift, 0), ea)
        s *= 2
    edge_s[...] = _shift_up(eb, 1) if reverse else _shift_down(eb, 1)

    def eight_groups(i, carry):
        for k in range(8):
            j = i * 8 + k
            rows = pl.ds(pl.multiple_of(j * 8, 8), 8)
            out_ref[rows, :] = b_s[rows, :] + a_s[rows, :] * edge_s[pl.ds(j, 1), :]
        return carry

    lax.fori_loop(0, ng // 8, eight_groups, 0)


def _neg_expm1(v):
    series = -v * (1.0 + v * (0.5 + v * (1.0 / 6.0)))
    return jnp.where(v > -0.015625, series, 1.0 - jnp.exp(v))


def _softplus_neg(lam):
    return jnp.maximum(-lam, 0.0) + jnp.log(1.0 + jnp.exp(-jnp.abs(lam)))


def _lru_gates(x0, cw, cb, wa, ba, wx, bx, lam):
    taps = [_shift_down(x0, 3 - k) for k in range(3)] + [x0]
    u = cb + cw[3:4, :] * x0
    for k in range(3):
        u = u + cw[k:k + 1, :] * taps[k]
    ub = u.astype(bf16)
    r = _sigmoid_positive(_dot(ub, wa.astype(bf16), NN) + ba)
    i = _sigmoid(_dot(ub, wx.astype(bf16), NN) + bx)
    sp = _softplus_neg(lam)
    log_a = (-LRU_C) * r * sp
    a = jnp.exp(log_a)
    w = _neg_expm1(2.0 * log_a)
    inv_mult = lax.rsqrt(w)
    return u, ub, r, i, sp, a, w * inv_mult, inv_mult, taps


def _lru_specs(S, nb):
    col = lambda off: pl.BlockSpec((S, RB), lambda n, b, off=off: (b, off + n))
    vec = pl.BlockSpec((1, RB), lambda n, b: (0, n))
    wblk = pl.BlockSpec((None, RB, RB), lambda n, b: (n, 0, 0))
    cwblk = pl.BlockSpec((8, RB), lambda n, b: (n, 0))
    return col, vec, wblk, cwblk


def _lru_forward(proj, cw_full, conv_b, w_a, b_a, w_x, b_x, lam, S):
    T = proj.shape[0]
    nb = T // S
    col, vec, wblk, cwblk = _lru_specs(S, nb)

    def body(x0_ref, g_ref, cw_ref, cb_ref, wa_ref, ba_ref, wx_ref, bx_ref, lam_ref, y_ref, h_ref, a_s, b_s, edge_s):
        x0 = x0_ref[...].astype(f32)
        u, ub, r, i, sp, a, mult, _, _ = _lru_gates(x0, cw_ref[...], cb_ref[...], wa_ref[...], ba_ref[...],
                                                    wx_ref[...], bx_ref[...], lam_ref[...])
        _linear_scan(a, mult * (i * u), a_s, b_s, edge_s, h_ref, reverse=False)
        g = g_ref[...].astype(f32)
        y_ref[...] = (h_ref[...] * (g * _sigmoid(g))).astype(bf16)

    out = pl.BlockSpec((S, RB), lambda n, b: (b, n))
    return _pcall(
        body, name="lru_forward", grid=(RNN_BLOCKS, nb),
        in_specs=[col(0), col(8), cwblk, vec, wblk, vec, wblk, vec, vec],
        out_specs=(out, out), out_shape=(_sds((T, D), bf16), _sds((T, D), f32)),
        scratch_shapes=[pltpu.VMEM((S, RB), f32), pltpu.VMEM((S, RB), f32), pltpu.VMEM((S // 8, RB), f32)],
        compiler_params=_params(("arbitrary", "arbitrary")),
    )(proj, proj, cw_full, conv_b, w_a, b_a, w_x, b_x, lam)


def _rope_tables(S):
    pos = jnp.arange(S, dtype=f32)
    inv_freq = ROPE_THETA ** (-jnp.arange(0, ROPE_DIM, 2, dtype=f32) / ROPE_DIM)
    ang = pos[:, None] * inv_freq[None, :]
    cos, sin = jnp.cos(ang), jnp.sin(ang)
    lane = jnp.arange(128) % HEAD
    cosl, sinl = cos[:, lane % 8], sin[:, lane % 8]
    c = jnp.where(lane[None, :] < ROPE_DIM, cosl, 1.0)
    s1 = jnp.where(lane[None, :] < 8, -sinl, 0.0)
    s2 = jnp.where((lane[None, :] >= 8) & (lane[None, :] < ROPE_DIM), sinl, 0.0)
    return c.astype(f32), s1.astype(f32), s2.astype(f32)


def _heads_to_rows(t):
    return jnp.concatenate([t[:, HEAD * h:HEAD * (h + 1)] for h in range(GROUP)], axis=0)


def _rows_to_heads(t):
    return jnp.concatenate([t[QB * h:QB * (h + 1), :] for h in range(GROUP)], axis=1)


def _window_bias(first_block):
    shape = (GROUP * QB, 2 * QB)
    qi = _rows_iota(shape) % QB
    cj = lax.broadcasted_iota(jnp.int32, shape, 1)
    valid = (cj > qi) & (cj <= qi + QB) & ((cj >= QB) | jnp.logical_not(first_block))
    return jnp.where(valid, 0.0, -jnp.inf)


def _attn_probs(q_rows, k_cat, sink_col, bias):
    s = _dot(q_rows, k_cat, NT) + bias
    m = jnp.maximum(jnp.max(s, axis=1, keepdims=True), sink_col)
    p = jnp.exp(s - m)
    e_sink = jnp.exp(sink_col - m)
    inv = 1.0 / (jnp.sum(p, axis=1, keepdims=True) + e_sink)
    return p * inv, e_sink * inv


def _sink_column(sink_ref, kv):
    rid = _rows_iota((GROUP * QB, 1))
    col = jnp.zeros((GROUP * QB, 1), f32)
    for h in range(GROUP):
        col = jnp.where(rid // QB == h, sink_ref[0, GROUP * kv + h], col)
    return col


def _attn_in_specs(S):
    nq = S // QB
    last = nq - 1
    cur = lambda b, j: b * nq + jnp.minimum(j, last)
    prev = lambda b, j: b * nq + jnp.maximum(jnp.minimum(j, last) - 1, 0)
    specs = [
        pl.BlockSpec((QB, D), lambda b, j: (cur(b, j), 2)),
        pl.BlockSpec((QB, 256), lambda b, j: (cur(b, j), 12)),
        pl.BlockSpec((QB, 256), lambda b, j: (prev(b, j), 12)),
        pl.BlockSpec((QB, 256), lambda b, j: (cur(b, j), 13)),
        pl.BlockSpec((QB, 256), lambda b, j: (prev(b, j), 13)),
        pl.BlockSpec((QB, 512), lambda b, j: (cur(b, j), 7)),
        pl.BlockSpec((QB, 512), lambda b, j: (cur(b, j), 8)),
        SMEM_SPEC,
    ]
    return specs, cur, prev


def _attn_forward(proj, sinks, S, out_shards):
    T = proj.shape[0]
    nb, nq = T // S, S // QB
    specs, cur, _ = _attn_in_specs(S)
    nw = len(out_shards)

    def body(q_ref, kc_ref, kp_ref, vc_ref, vp_ref, gl_ref, gh_ref, sink_ref, *rest):
        shards = rest[:nw]
        y_ref = rest[nw]
        gathered = rest[nw + 1:2 * nw + 1]
        stage = rest[2 * nw + 1:3 * nw + 1]
        sems = rest[3 * nw + 1:]
        b, j = pl.program_id(0), pl.program_id(1)

        @pl.when((b == 0) & (j == 0))
        def _():
            for a in range(nw):
                stage[a][...] = shards[a][...].astype(bf16)
            _start_all(_direct_gather_copies(stage, gathered, *sems))

        @pl.when((b == nb - 1) & (j == nq - 1))
        def _():
            _wait_all(_direct_gather_copies(stage, gathered, *sems))

        bias = _window_bias(j == 0)
        kc, kp, vc, vp = kc_ref[...], kp_ref[...], vc_ref[...], vp_ref[...]
        for kv in range(KV_HEADS):
            lanes = slice(256 * kv, 256 * (kv + 1))
            hl = slice(HEAD * kv, HEAD * (kv + 1))
            q_rows = _heads_to_rows(q_ref[:, lanes])
            k_cat = jnp.concatenate([kp[:, hl], kc[:, hl]], axis=0)
            v_cat = jnp.concatenate([vp[:, hl], vc[:, hl]], axis=0)
            probs, _ = _attn_probs(q_rows, k_cat, _sink_column(sink_ref, kv), bias)
            o = _rows_to_heads(_dot(probs.astype(bf16), v_cat, NN))
            g_src = gl_ref if kv < 2 else gh_ref
            g = g_src[:, 256 * (kv % 2):256 * (kv % 2 + 1)].astype(f32)
            y_ref[:, lanes] = (o * (g * _sigmoid(g))).astype(bf16)

    args = [proj] * 7 + [sinks] + list(out_shards)
    res = _pcall(
        body, name="attn_forward", grid=(nb, nq),
        in_specs=specs + [pl.BlockSpec(w.shape, lambda b, j: (0, 0)) for w in out_shards],
        out_specs=(pl.BlockSpec((QB, D), lambda b, j: (cur(b, j), 0)),) + tuple([ANY] * nw),
        out_shape=(_sds((T, D), bf16),) + tuple(_sds((NDEV * w.shape[0], w.shape[1]), bf16) for w in out_shards),
        scratch_shapes=[pltpu.VMEM(w.shape, bf16) for w in out_shards] + _exchange_scratch(nw, 7),
        compiler_params=_params(("arbitrary", "arbitrary")),
    )(*args)
    return res[0], res[1:]


def _merge_and_head(x2d, tgt, proj, y_rnn, y_attn, w_r, w_a, w_o, gfin):
    T = x2d.shape[0]
    tb = min(T, 512)
    nsteps = T // tb

    def body(x_ref, t_ref, mr0, mr1, ma0, ma1, yr_ref, ya_ref, wr_ref, wa_ref, wo_ref, gf_ref,
             dx2_ref, dyr_ref, dya_ref, dmr_ref, dma_ref, loss_ref, gfin_ref, gwr_out, gwa_out, gwo_out,
             gwr_acc, gwa_acc, gwo_acc, out_sems):
        step = pl.program_id(0)

        @pl.when(step == 0)
        def _():
            loss_ref[...] = jnp.zeros_like(loss_ref)
            gfin_ref[...] = jnp.zeros_like(gfin_ref)
            gwr_acc[...] = jnp.zeros_like(gwr_acc)
            gwa_acc[...] = jnp.zeros_like(gwa_acc)
            gwo_acc[...] = jnp.zeros_like(gwo_acc)

        sr = _sigmoid(jnp.concatenate([mr0[...], mr1[...]], axis=1).astype(f32))
        sa = _sigmoid(jnp.concatenate([ma0[...], ma1[...]], axis=1).astype(f32))
        p_r = _dot(yr_ref[...], wr_ref[...], NN)
        p_a = _dot(ya_ref[...], wa_ref[...], NN)
        merged = (sr * p_r + sa * p_a).astype(bf16)
        x2 = x_ref[...] + _dot(merged, wo_ref[...], NN)
        rstd = lax.rsqrt(jnp.mean(x2 * x2, axis=-1, keepdims=True) + EPS)
        xh = x2 * rstd
        gf = gf_ref[...]
        err = xh * gf - t_ref[...]
        loss_ref[...] += jnp.sum(err * err)
        dy = err * (1.0 / D)
        gfin_ref[0:1, :] += jnp.sum(dy * xh, axis=0, keepdims=True)
        dxn = dy * gf
        dx2 = rstd * (dxn - xh * jnp.mean(dxn * xh, axis=-1, keepdims=True))
        dx2_ref[...] = dx2
        dx2b = dx2.astype(bf16)
        dmerged = _dot(dx2b, wo_ref[...], NT)
        dmr_ref[...] = (dmerged * p_r * (sr * (1.0 - sr))).astype(bf16)
        dma_ref[...] = (dmerged * p_a * (sa * (1.0 - sa))).astype(bf16)
        dpr = (dmerged * sr).astype(bf16)
        dpa = (dmerged * sa).astype(bf16)
        dyr_ref[...] = _dot(dpr, wr_ref[...], NT).astype(bf16)
        dya_ref[...] = _dot(dpa, wa_ref[...], NT).astype(bf16)
        gwr_acc[...] += _dot(yr_ref[...], dpr, TN)
        gwa_acc[...] += _dot(ya_ref[...], dpa, TN)
        gwo_acc[...] += _dot(merged, dx2b, TN)

        @pl.when(step == nsteps - 1)
        def _():
            copies = [pltpu.make_async_copy(src, dst, out_sems.at[k]) for k, (src, dst) in enumerate(
                ((gwr_acc, gwr_out), (gwa_acc, gwa_out), (gwo_acc, gwo_out)))]
            for cp in copies:
                cp.start()
            for cp in copies:
                cp.wait()

    tok = pl.BlockSpec((tb, D), lambda i: (i, 0))
    half = lambda c: pl.BlockSpec((tb, CH), lambda i, c=c: (i, c))
    wfull = pl.BlockSpec((D, D), lambda i: (0, 0), pipeline_mode=pl.Buffered(1))
    acc = pl.BlockSpec((8, D), lambda i: (0, 0))
    return _pcall(
        body, name="merge_and_head", grid=(nsteps,),
        in_specs=[tok, tok, half(9), half(10), half(11), half(12), tok, tok, wfull, wfull, wfull,
                  pl.BlockSpec((1, D), lambda i: (0, 0))],
        out_specs=(tok, tok, tok, tok, tok, acc, acc, ANY, ANY, ANY),
        out_shape=(_sds((T, D), f32), _sds((T, D), bf16), _sds((T, D), bf16), _sds((T, D), bf16),
                   _sds((T, D), bf16), _sds((8, D), f32), _sds((8, D), f32),
                   _sds((D, D), f32), _sds((D, D), f32), _sds((D, D), f32)),
        scratch_shapes=[pltpu.VMEM((D, D), f32)] * 3 + [pltpu.SemaphoreType.DMA((3,))],
        compiler_params=_params(("arbitrary",)),
    )(x2d, tgt, proj, proj, proj, proj, y_rnn, y_attn, w_r, w_a, w_o, gfin)


def _attn_backward(proj, dy_attn, tabs, sinks, S, chip_sums):
    T = proj.shape[0]
    nb, nq = T // S, S // QB
    nex = len(chip_sums)
    specs, cur, prev = _attn_in_specs(S)
    last = nq - 1
    tab_cur = pl.BlockSpec((QB, 128), lambda b, j: (jnp.minimum(j, last), 0))
    tab_prev = pl.BlockSpec((QB, 128), lambda b, j: (jnp.maximum(jnp.minimum(j, last) - 1, 0), 0))
    specs = specs + [pl.BlockSpec((QB, D), lambda b, j: (cur(b, j), 0))] + [tab_cur] * 3 + [tab_prev] * 3
    q_scale = 1.0 / math.sqrt(HEAD)

    def rope_back(dt, tab):
        return jnp.concatenate([_rope_transposed(dt[:, 128 * l:128 * (l + 1)], *tab) for l in range(2)], axis=1)

    def body(q_ref, kc_ref, kp_ref, vc_ref, vp_ref, gl_ref, gh_ref, sink_ref, dy_ref, cc, s1c, s2c, cp, s1p, s2p,
             *rest):
        ex_src = rest[:nex]
        dq_ref, dkv_ref, dg_ref, dsink_ref = rest[nex:nex + 4]
        ex_dst = rest[nex + 4:2 * nex + 4]
        carry_k, carry_v = rest[2 * nex + 4:2 * nex + 6]
        sems = rest[2 * nex + 6:]
        b, j = pl.program_id(0), pl.program_id(1)

        @pl.when((b == 0) & (j == 0))
        def _():
            dsink_ref[...] = jnp.zeros_like(dsink_ref)
            _start_all(_chip_exchange_copies(ex_src, ex_dst, *sems))

        @pl.when((b == nb - 1) & (j == nq))
        def _():
            _wait_all(_chip_exchange_copies(ex_src, ex_dst, *sems))

        @pl.when(j == 0)
        def _():
            carry_k[...] = jnp.zeros_like(carry_k)
            carry_v[...] = jnp.zeros_like(carry_v)

        @pl.when(j < nq)
        def _():
            bias = _window_bias(j == 0)
            tc = (cc[...], s1c[...], s2c[...])
            tp = (cp[...], s1p[...], s2p[...])
            kc, kp, vc, vp = kc_ref[...], kp_ref[...], vc_ref[...], vp_ref[...]
            dk_prev, dk_cur, dv_prev, dv_cur = [], [], [], []
            dsink_acc = jnp.zeros((8, 128), f32)
            r8 = lax.broadcasted_iota(jnp.int32, (8, 128), 0)
            l8 = lax.broadcasted_iota(jnp.int32, (8, 128), 1)
            for kv in range(KV_HEADS):
                lanes = slice(256 * kv, 256 * (kv + 1))
                hl = slice(HEAD * kv, HEAD * (kv + 1))
                q_rows = _heads_to_rows(q_ref[:, lanes])
                k_cat = jnp.concatenate([kp[:, hl], kc[:, hl]], axis=0)
                v_cat = jnp.concatenate([vp[:, hl], vc[:, hl]], axis=0)
                probs, p_sink = _attn_probs(q_rows, k_cat, _sink_column(sink_ref, kv), bias)
                pb = probs.astype(bf16)
                o = _rows_to_heads(_dot(pb, v_cat, NN))
                g_src = gl_ref if kv < 2 else gh_ref
                g = g_src[:, 256 * (kv % 2):256 * (kv % 2 + 1)].astype(f32)
                sg = _sigmoid(g)
                dy = dy_ref[:, lanes].astype(f32)
                dg_ref[:, lanes] = (dy * o * (sg * (1.0 + g * (1.0 - sg)))).astype(bf16)
                do_rows = _heads_to_rows(dy * (g * sg)).astype(bf16)
                dv = _dot(pb, do_rows, TN)
                dp = _dot(do_rows, v_cat, NT)
                rowdot = jnp.sum(probs * dp, axis=1, keepdims=True)
                ds = (probs * (dp - rowdot)).astype(bf16)
                sink_rows = -(p_sink * rowdot)
                for h in range(GROUP):
                    val = jnp.sum(sink_rows[QB * h:QB * (h + 1), :])
                    dsink_acc = dsink_acc + jnp.where((r8 == 0) & (l8 == GROUP * kv + h), val, 0.0)
                dq = _rows_to_heads(_dot(ds, k_cat, NN)) * q_scale
                dq_ref[:, lanes] = rope_back(dq, tc).astype(bf16)
                dk = _dot(ds, q_rows, TN)
                dk_prev.append(dk[:QB, :])
                dk_cur.append(dk[QB:, :])
                dv_prev.append(dv[:QB, :])
                dv_cur.append(dv[QB:, :])
            dsink_ref[...] += dsink_acc
            dkp = rope_back(jnp.concatenate(dk_prev, axis=1), tp)
            dkc = rope_back(jnp.concatenate(dk_cur, axis=1), tc)
            dkv_ref[:, 0:256] = (carry_k[...] + dkp).astype(bf16)
            dkv_ref[:, 256:512] = (carry_v[...] + jnp.concatenate(dv_prev, axis=1)).astype(bf16)
            carry_k[...] = dkc
            carry_v[...] = jnp.concatenate(dv_cur, axis=1)

        @pl.when(j == nq)
        def _():
            dkv_ref[:, 0:256] = carry_k[...].astype(bf16)
            dkv_ref[:, 256:512] = carry_v[...].astype(bf16)

    lag = lambda b, j: (b * nq + jnp.maximum(j - 1, 0), 0)
    args = [proj] * 7 + [sinks, dy_attn] + list(tabs) + list(tabs) + list(chip_sums)
    res = _pcall(
        body, name="attn_backward", grid=(nb, nq + 1), in_specs=specs + [ANY] * nex,
        out_specs=(pl.BlockSpec((QB, D), lambda b, j: (cur(b, j), 0)), pl.BlockSpec((QB, 512), lag),
                   pl.BlockSpec((QB, D), lambda b, j: (cur(b, j), 0)), pl.BlockSpec((8, 128), lambda b, j: (0, 0)))
        + tuple([ANY] * nex),
        out_shape=(_sds((T, D), bf16), _sds((T, 512), bf16), _sds((T, D), bf16), _sds((8, 128), f32))
        + tuple(_sds(s.shape, s.dtype) for s in chip_sums),
        scratch_shapes=[pltpu.VMEM((QB, 256), f32), pltpu.VMEM((QB, 256), f32)] + _exchange_scratch(nex, 3),
        compiler_params=_params(("arbitrary", "arbitrary")),
    )(*args)
    return res[:4], res[4:]


def _lru_backward(proj, h_all, dy_rnn, cw_full, conv_b, w_a, b_a, w_x, b_x, lam, S):
    T = proj.shape[0]
    nb = T // S
    col, vec, wblk, cwblk = _lru_specs(S, nb)
    tokblk = pl.BlockSpec((S, RB), lambda n, b: (b, n))

    def body(x0_ref, g_ref, h_ref, dy_ref, cw_ref, cb_ref, wa_ref, ba_ref, wx_ref, bx_ref, lam_ref,
             du0_ref, dg_ref, gwa_ref, gwx_ref, vec_ref, gcw_ref, a_s, b_s, dh_s, edge_s):
        @pl.when(pl.program_id(1) == 0)
        def _():
            gwa_ref[...] = jnp.zeros_like(gwa_ref)
            gwx_ref[...] = jnp.zeros_like(gwx_ref)
            vec_ref[...] = jnp.zeros_like(vec_ref)
            gcw_ref[...] = jnp.zeros_like(gcw_ref)

        x0 = x0_ref[...].astype(f32)
        cw = cw_ref[...]
        lam_v = lam_ref[...]
        u, ub, r, i, sp, a, mult, inv_mult, taps = _lru_gates(x0, cw, cb_ref[...], wa_ref[...], ba_ref[...],
                                                              wx_ref[...], bx_ref[...], lam_v)
        h = h_ref[...]
        g = g_ref[...].astype(f32)
        dy = dy_ref[...].astype(f32)
        sg = _sigmoid(g)
        dg_ref[...] = (dy * h * (sg * (1.0 + g * (1.0 - sg)))).astype(bf16)
        _linear_scan(_shift_up(a, 1), dy * (g * sg), a_s, b_s, edge_s, dh_s, reverse=True)
        dh_total = dh_s[...]
        da = dh_total * _shift_down(h, 1)
        dmult = dh_total * (i * u)
        db = dh_total * mult
        di = db * u
        du = db * i
        dlog_a_c = ((-LRU_C) * a) * (da - dmult * (a * inv_mult))
        dr = dlog_a_c * sp
        dsp = jnp.sum(dlog_a_c * r, axis=0, keepdims=True)
        dpre_r = dr * r * (1.0 - r)
        dpre_i = di * i * (1.0 - i)
        dpre_rb = dpre_r.astype(bf16)
        dpre_ib = dpre_i.astype(bf16)
        du = du + _dot(dpre_rb, wa_ref[...].astype(bf16), NT) + _dot(dpre_ib, wx_ref[...].astype(bf16), NT)
        gwa_ref[...] += _dot(ub, dpre_rb, TN)
        gwx_ref[...] += _dot(ub, dpre_ib, TN)
        vec_ref[0:1, :] += jnp.sum(du, axis=0, keepdims=True)
        vec_ref[1:2, :] += jnp.sum(dpre_r, axis=0, keepdims=True)
        vec_ref[2:3, :] += jnp.sum(dpre_i, axis=0, keepdims=True)
        vec_ref[3:4, :] += dsp * (-_sigmoid(-lam_v))
        dx0 = cw[3:4, :] * du
        for k in range(3):
            dx0 = dx0 + cw[k:k + 1, :] * _shift_up(du, 3 - k)
        for k in range(4):
            gcw_ref[k:k + 1, :] += jnp.sum(du * taps[k], axis=0, keepdims=True)
        du0_ref[...] = dx0.astype(bf16)

    wacc = pl.BlockSpec((RB, RB), lambda n, b: (0, n))
    vacc = pl.BlockSpec((8, RB), lambda n, b: (0, n))
    cacc = pl.BlockSpec((8, RB), lambda n, b: (n, 0))
    return _pcall(
        body, name="lru_backward", grid=(RNN_BLOCKS, nb),
        in_specs=[col(0), col(8), tokblk, tokblk, cwblk, vec, wblk, vec, wblk, vec, vec],
        out_specs=(tokblk, tokblk, wacc, wacc, vacc, cacc),
        out_shape=(_sds((T, D), bf16), _sds((T, D), bf16), _sds((RB, D), f32), _sds((RB, D), f32),
                   _sds((8, D), f32), _sds((8 * RNN_BLOCKS, RB), f32)),
        scratch_shapes=[pltpu.VMEM((S, RB), f32)] * 3 + [pltpu.VMEM((S // 8, RB), f32)],
        compiler_params=_params(("arbitrary", "arbitrary")),
    )(proj, proj, h_all, dy_rnn, cw_full, conv_b, w_a, b_a, w_x, b_x, lam)


def _section_of_chunk(s):
    out = []
    for start, n in zip(SEC_START, SEC_CHUNKS):
        inside = (s >= start) & (s < start + n)
        out.append((inside, jnp.clip(s - start, 0, n - 1)))
    return out


EFFECT = pltpu.SideEffectType.DATAFLOW_SIDE_EFFECTING
HBM_SPEC = pl.BlockSpec(memory_space=pltpu.HBM)
SEM_SPEC = pl.BlockSpec(memory_space=pltpu.SEMAPHORE)


def _split_exchange_copies(src_ref, land_ref, send_sems, recv_sems):
    x, y, c = _my_place()
    copies = []
    for k in (3, 1, 2):
        px, py = (x + (k >> 1)) % 2, (y + (k & 1)) % 2
        copies.append(pltpu.make_async_remote_copy(
            src_ref=src_ref.at[2 * px + py], dst_ref=land_ref.at[k - 1], send_sem=send_sems[k - 1],
            recv_sem=recv_sems[k - 1], device_id=(px, py, c), device_id_type=MESH))
    return copies


def _exchange_start(chip_sum):
    _, r, cols = chip_sum.shape

    def body(src_ref, land_ref, s0, s1, s2, r0, r1, r2, src_thru, land_thru, token):
        for cp in _split_exchange_copies(src_ref, land_ref, (s0, s1, s2), (r0, r1, r2)):
            cp.start()
        token[...] = jnp.zeros_like(token)

    land = pltpu.with_memory_space_constraint(lax.empty((3, r, cols), chip_sum.dtype), pltpu.HBM)
    res = _pcall(
        body, name="exchange_start",
        out_shape=tuple([pltpu.SemaphoreType.DMA(())] * 6) + (
            pltpu.HBM(chip_sum.shape, chip_sum.dtype), pltpu.HBM((3, r, cols), chip_sum.dtype), _sds((8, 128), f32)),
        in_specs=(HBM_SPEC, HBM_SPEC), out_specs=tuple([SEM_SPEC] * 6) + (HBM_SPEC, HBM_SPEC, VMEM_SPEC),
        input_output_aliases={0: 6, 1: 7},
        compiler_params=pltpu.CompilerParams(has_side_effects=EFFECT),
    )(pltpu.with_memory_space_constraint(chip_sum, pltpu.HBM), land)
    return res[:6], res[6], res[7], res[8]


def _exchange_wait(sems, src_thru, land_thru, after):
    def body(src_ref, land_ref, s0, s1, s2, r0, r1, r2, *rest):
        for cp in _split_exchange_copies(src_ref, land_ref, (s0, s1, s2), (r0, r1, r2)):
            cp.wait_send()
            cp.wait_recv()

    return _pcall(
        body, name="exchange_wait",
        out_shape=(pltpu.HBM(src_thru.shape, src_thru.dtype), pltpu.HBM(land_thru.shape, land_thru.dtype)),
        in_specs=(HBM_SPEC, HBM_SPEC) + tuple([SEM_SPEC] * 6) + tuple([ANY] * len(after)),
        out_specs=(HBM_SPEC, HBM_SPEC), input_output_aliases={0: 0, 1: 1},
        compiler_params=pltpu.CompilerParams(has_side_effects=EFFECT),
    )(src_thru, land_thru, *sems, *after)[1]


def _input_grad(dsecs, wt_full, x2d, dx2, norm_g):
    T = x2d.shape[0]
    tb = min(T, 512)
    nsec = len(dsecs)
    ntok = T // tb

    def body(*refs):
        secs = refs[:nsec]
        wt_ref, x_ref, dx2_ref, g_ref, dx_ref, gnorm_ref = refs[nsec:]
        i = pl.program_id(0)

        @pl.when(i == 0)
        def _():
            gnorm_ref[...] = jnp.zeros_like(gnorm_ref)

        dh = None
        for a, (start, n) in enumerate(zip(SEC_START, SEC_CHUNKS)):
            part = _dot(secs[a][...], wt_ref[CH * start:CH * (start + n), :], NN)
            dh = part if dh is None else dh + part
        xv = x_ref[...]
        rstd = lax.rsqrt(jnp.mean(xv * xv, axis=-1, keepdims=True) + EPS)
        xh = xv * rstd
        gnorm_ref[0:1, :] += jnp.sum(dh * xh, axis=0, keepdims=True)
        dxn = dh * g_ref[...]
        dx_ref[...] = dx2_ref[...] + rstd * (dxn - xh * jnp.mean(dxn * xh, axis=-1, keepdims=True))

    tok = pl.BlockSpec((tb, D), lambda i: (i, 0))
    return _pcall(
        body, name="input_grad", grid=(ntok,),
        in_specs=[pl.BlockSpec((tb, sec.shape[1]), lambda i: (i, 0)) for sec in dsecs]
        + [pl.BlockSpec((D_IN, D), lambda i: (0, 0), pipeline_mode=pl.Buffered(1)), tok, tok,
           pl.BlockSpec((1, D), lambda i: (0, 0))],
        out_specs=(tok, pl.BlockSpec((8, D), lambda i: (0, 0))),
        out_shape=(_sds((T, D), f32), _sds((8, D), f32)),
        compiler_params=_params(("arbitrary",)),
    )(*dsecs, wt_full, x2d, dx2, norm_g)


def _w_in_grad(dsecs, h_bf):
    T = h_bf.shape[0]
    tk = min(T, 2048)
    nchunks = D_IN // CH
    nsec = len(dsecs)
    nt = T // tk

    def body(*refs):
        secs = refs[:nsec]
        h_ref, out_ref, acc = refs[nsec:]
        s, t = pl.program_id(0), pl.program_id(1)

        @pl.when(t == 0)
        def _():
            acc[...] = jnp.zeros_like(acc)

        h_rows = h_ref[pl.ds(pl.multiple_of(t * tk, tk), tk), :]
        for a, (start, n) in enumerate(zip(SEC_START, SEC_CHUNKS)):
            @pl.when((s >= start) & (s < start + n))
            def _(a=a):
                acc[...] += _dot(secs[a][...], h_rows, TN)

        @pl.when(t == nt - 1)
        def _():
            out_ref[...] = acc[...].astype(bf16)

    def sec_spec(a):
        def index(s, t, a=a):
            inside, local = _section_of_chunk(s)[a]
            return (jnp.where(inside, t, 0), local)
        return pl.BlockSpec((tk, CH), index)

    return _pcall(
        body, name="w_in_grad", grid=(nchunks, T // tk),
        in_specs=[sec_spec(a) for a in range(nsec)]
        + [pl.BlockSpec((T, D), lambda s, t: (0, 0), pipeline_mode=pl.Buffered(1))],
        out_specs=pl.BlockSpec((CH, D), lambda s, t: (s, 0)), out_shape=_sds((D_IN, D), bf16),
        scratch_shapes=[pltpu.VMEM((CH, D), f32)],
        compiler_params=_params(("arbitrary", "arbitrary")),
    )(*dsecs, h_bf)


SMALL_NAMES = ("lru_w_a", "lru_w_x", "conv_b", "lru_b_a", "lru_b_x", "lru_lambda", "norm_g", "final_norm_g",
               "attn_sinks", "conv_w")
MISC_ROW = {"conv_b": 0, "lru_b_a": 1, "lru_b_x": 2, "lru_lambda": 3, "norm_g": 8, "final_norm_g": 16,
            "attn_sinks": 24, "loss": 32}


def _small_step(gwa, gwx, gvec, gnorm_blk, gfin_blk, dsink_blk, loss_blk, gcw, params, after):
    srcs_rows = (RB // NDEV, RB // NDEV, 8, 8)
    flat = [t for n in SMALL_NAMES for t in params[n]]
    nout = 4 * len(SMALL_NAMES) + 1

    def reduce_body(gwa_ref, gwx_ref, gvec_ref, gnorm_ref, gfin_ref, dsink_ref, loss_ref, gcw_ref, after_ref,
                    all_a, all_x, all_m, conv_out,
                    misc, got_a, got_x, got_m, got_c, red_a, red_x, red_m, sa, ra, sb, rb):
        x, y, c = _my_place()
        me = 4 * x + 2 * y + c

        misc[...] = jnp.zeros_like(misc)
        misc[0:8, :] = gvec_ref[...]
        misc[8:16, :] = gnorm_ref[...]
        misc[16:24, :] = gfin_ref[...]
        misc[24:32, 0:128] = dsink_ref[...]
        misc[32:40, :] = loss_ref[...]

        srcs = (gwa_ref, gwx_ref, misc, gcw_ref)
        gots = (got_a, got_x, got_m, got_c)

        def shard(ref, rows, dev):
            return ref.at[pl.ds(pl.multiple_of(dev * rows, 8), rows), :]

        scatter = []
        for k in range(1, NDEV):
            px, py, pc = _peer(k)
            for a in range(4):
                scatter.append(pltpu.make_async_remote_copy(
                    src_ref=shard(srcs[a], srcs_rows[a], 4 * px + 2 * py + pc), dst_ref=gots[a].at[k - 1],
                    send_sem=sa.at[4 * (k - 1) + a], recv_sem=ra.at[4 * (k - 1) + a],
                    device_id=(px, py, pc), device_id_type=MESH))
        for cp in scatter:
            cp.start()
        for cp in scatter:
            cp.wait()

        def reduced(a):
            rows = srcs_rows[a]
            total = srcs[a][pl.ds(pl.multiple_of(me * rows, 8), rows), :]
            for k in range(NDEV - 1):
                total = total + gots[a][k]
            return total

        reds = (red_a, red_x, red_m)
        alls = (all_a, all_x, all_m)
        for a in range(3):
            val = reduced(a)
            reds[a][...] = val
            alls[a][pl.ds(pl.multiple_of(me * srcs_rows[a], 8), srcs_rows[a]), :] = val
        gather = []
        for k in range(1, NDEV):
            peer = _peer(k)
            for a in range(3):
                gather.append(pltpu.make_async_remote_copy(
                    src_ref=reds[a], dst_ref=shard(alls[a], srcs_rows[a], me),
                    send_sem=sb.at[3 * (k - 1) + a], recv_sem=rb.at[3 * (k - 1) + a],
                    device_id=peer, device_id_type=MESH))
        for cp in gather:
            cp.start()
        conv_out[...] = reduced(3)
        for cp in gather:
            cp.wait()

    def adam_body(*refs):
        all_a, all_x, all_m, conv_ref = refs[:4]
        prm = {n: refs[4 + 3 * k:7 + 3 * k] for k, n in enumerate(SMALL_NAMES)}
        nin = 4 + len(flat)
        outs = {n: refs[nin + 4 * k:nin + 4 * k + 4] for k, n in enumerate(SMALL_NAMES)}
        loss_out = refs[nin + nout - 1]
        g_conv = conv_ref[0:4, :]

        def update(name, g, pick=lambda r: r[...]):
            w_ref, m_ref, v_ref = prm[name]
            delta, m_new, v_new = _adam_math(g, pick(w_ref), pick(m_ref), pick(v_ref))
            return g, delta, m_new, v_new

        for n in range(RNN_BLOCKS):
            lanes = slice(RB * n, RB * (n + 1))
            for name, full in (("lru_w_a", all_a), ("lru_w_x", all_x)):
                for out, val in zip(outs[name], update(name, full[:, lanes], pick=lambda r, n=n: r[n])):
                    out[n] = val
        for name in ("conv_b", "lru_b_a", "lru_b_x", "lru_lambda", "norm_g", "final_norm_g"):
            row = MISC_ROW[name]
            for out, val in zip(outs[name], update(name, all_m[row:row + 1, :])):
                out[...] = val
        row = MISC_ROW["attn_sinks"]
        for out, val in zip(outs["attn_sinks"], update("attn_sinks", all_m[row:row + 1, 0:16])):
            out[...] = val
        for out, val in zip(outs["conv_w"], update("conv_w", g_conv)):
            out[...] = val
        row = MISC_ROW["loss"]
        loss_out[...] = all_m[row:row + 8, 0:128] * (0.5 / D)

    scratch = [pltpu.VMEM((64, D), f32),
               pltpu.VMEM((NDEV - 1, RB // NDEV, D), f32), pltpu.VMEM((NDEV - 1, RB // NDEV, D), f32),
               pltpu.VMEM((NDEV - 1, 8, D), f32), pltpu.VMEM((NDEV - 1, 8, RB), f32),
               pltpu.VMEM((RB // NDEV, D), f32), pltpu.VMEM((RB // NDEV, D), f32), pltpu.VMEM((8, D), f32),
               pltpu.SemaphoreType.DMA((4 * (NDEV - 1),)), pltpu.SemaphoreType.DMA((4 * (NDEV - 1),)),
               pltpu.SemaphoreType.DMA((3 * (NDEV - 1),)), pltpu.SemaphoreType.DMA((3 * (NDEV - 1),))]
    sums = _pcall(
        reduce_body, name="small_reduce",
        out_shape=(_sds((RB, D), f32), _sds((RB, D), f32), _sds((64, D), f32), _sds((8, RB), f32)),
        in_specs=[VMEM_SPEC] * 8 + [ANY], out_specs=tuple([VMEM_SPEC] * 4),
        scratch_shapes=scratch, compiler_params=_params(),
    )(gwa, gwx, gvec, gnorm_blk, gfin_blk, dsink_blk, loss_blk, gcw, after)
    out_shape = tuple(_sds(params[n][0].shape, f32) for n in SMALL_NAMES for _ in range(4)) + (_sds((8, 128), f32),)
    res = _pcall(
        adam_body, name="small_adamw", out_shape=out_shape,
        in_specs=[VMEM_SPEC] * (4 + len(flat)), out_specs=tuple([VMEM_SPEC] * nout), compiler_params=_params(),
    )(*sums, *flat)
    return {n: res[4 * k:4 * k + 4] for k, n in enumerate(SMALL_NAMES)}, res[-1]


def _pad_rows(v, rows=8):
    return jnp.concatenate([v, jnp.zeros((rows - v.shape[0], v.shape[1]), v.dtype)], axis=0)


def kernel(x, norm_g, w_in, conv_w, conv_b, lru_w_a, lru_b_a, lru_w_x, lru_b_x, lru_lambda, attn_sinks, w_rnn_out, w_attn_out, w_o, final_norm_g, loss_target, m_norm_g, m_w_in, m_conv_w, m_conv_b, m_lru_w_a, m_lru_b_a, m_lru_w_x, m_lru_b_x, m_lru_lambda, m_attn_sinks, m_w_rnn_out, m_w_attn_out, m_w_o, m_final_norm_g, v_norm_g, v_w_in, v_conv_w, v_conv_b, v_lru_w_a, v_lru_b_a, v_lru_w_x, v_lru_b_x, v_lru_lambda, v_attn_sinks, v_w_rnn_out, v_w_attn_out, v_w_o, v_final_norm_g):
    nb, S, _ = x.shape
    T = nb * S
    x2d = x.reshape(T, D)
    tgt = loss_target.reshape(T, D)
    fin_g = final_norm_g.reshape(1, D)
    w_a3, w_x3 = lru_w_a[0], lru_w_x[0]

    my_core = lax.axis_index("c").astype(jnp.int32).reshape(1)
    cx, cy = lax.axis_index("x"), lax.axis_index("y")
    chip_order = jnp.stack([2 * cx + cy, 2 * (1 - cx) + cy, 2 * cx + (1 - cy),
                            2 * (1 - cx) + (1 - cy)]).astype(jnp.int32)

    tabs = _rope_tables(S)
    h_bf, proj, wt_full, cw_full, _ = _in_proj_gather(
        x2d, norm_g, w_in[0].T.astype(bf16), _pad_rows(conv_w[0]), tabs, S, (), chip_order)
    y_rnn, h_all = _lru_forward(proj, cw_full, conv_b, w_a3, lru_b_a, w_x3, lru_b_x, lru_lambda, S)
    y_attn, (wr_full, wa_full, wo_full) = _attn_forward(proj, attn_sinks, S,
                                                        (w_rnn_out[0], w_attn_out[0], w_o[0]))

    (dx2, dy_rnn, dy_attn, dmr, dma, loss_blk, gfin_blk, g_wr, g_wa, g_wo) = _merge_and_head(
        x2d, tgt, proj, y_rnn, y_attn, wr_full, wa_full, wo_full, fin_g)
    sums_out = _pair_sums([g_wr, g_wa, g_wo], bf16, my_core, "out")

    (dq, dkv, dga, dsink_blk), (p_wr, p_wa, p_wo) = _attn_backward(proj, dy_attn, tabs, attn_sinks, S, sums_out)
    du0, dgr, gwa, gwx, gvec, gcw = _lru_backward(proj, h_all, dy_rnn, cw_full, conv_b, w_a3, lru_b_a, w_x3,
                                                  lru_b_x, lru_lambda, S)
    dsecs = (du0, dgr, dq, dkv, dga, dmr, dma)

    g_wt = _w_in_grad(dsecs, h_bf)
    (sum_in,) = _pair_sums([g_wt], bf16, my_core, "in")
    ex_sems, sum_in, landing, token = _exchange_start(sum_in)
    grad_x2d, gnorm_blk = _input_grad(dsecs, wt_full, x2d, dx2, norm_g + token[0, 0])
    o_wr, o_wa, o_wo = _adamw_group(
        (p_wr, p_wa, p_wo), (w_rnn_out[0], w_attn_out[0], w_o[0]),
        (m_w_rnn_out[0], m_w_attn_out[0], m_w_o[0]), (v_w_rnn_out[0], v_w_attn_out[0], v_w_o[0]), "adamw_w_out")
    p_wt = _exchange_wait(ex_sems, sum_in, landing, (gnorm_blk, o_wr[0]))
    p_wt_own = lax.dynamic_index_in_dim(sum_in, 2 * cx + cy, axis=0, keepdims=False)
    o_wt = _adamw(p_wt_own, p_wt, w_in[0].T, m_w_in[0].T, v_w_in[0].T, "adamw_w_in")

    small, loss_out = _small_step(gwa, gwx, gvec, gnorm_blk, gfin_blk, dsink_blk, loss_blk, gcw, {
        "lru_w_a": (w_a3, m_lru_w_a[0], v_lru_w_a[0]), "lru_w_x": (w_x3, m_lru_w_x[0], v_lru_w_x[0]),
        "conv_b": (conv_b, m_conv_b, v_conv_b), "lru_b_a": (lru_b_a, m_lru_b_a, v_lru_b_a),
        "lru_b_x": (lru_b_x, m_lru_b_x, v_lru_b_x), "lru_lambda": (lru_lambda, m_lru_lambda, v_lru_lambda),
        "norm_g": (norm_g, m_norm_g, v_norm_g),
        "final_norm_g": (fin_g, m_final_norm_g.reshape(1, D), v_final_norm_g.reshape(1, D)),
        "attn_sinks": (attn_sinks, m_attn_sinks, v_attn_sinks),
        "conv_w": (conv_w[0], m_conv_w[0], v_conv_w[0])}, o_wt[0])

    def result(kind):
        d = {n: small[n][kind] for n in ("conv_b", "lru_b_a", "lru_b_x", "lru_lambda", "norm_g", "attn_sinks")}
        d.update({n: small[n][kind][None] for n in ("lru_w_a", "lru_w_x", "conv_w")})
        d["final_norm_g"] = small["final_norm_g"][kind].reshape(D)
        d.update({"w_in": o_wt[kind].T[None], "w_rnn_out": o_wr[kind][None], "w_attn_out": o_wa[kind][None],
                  "w_o": o_wo[kind][None]})
        return d

    order = ("norm_g", "w_in", "conv_w", "conv_b", "lru_w_a", "lru_b_a", "lru_w_x", "lru_b_x", "lru_lambda",
             "attn_sinks", "w_rnn_out", "w_attn_out", "w_o", "final_norm_g")
    outs = [loss_out[0, 0], grad_x2d.reshape(nb, S, D)]
    for kind in range(4):
        d = result(kind)
        outs += [d[n] for n in order]
    return tuple(outs)
```

```python
import functools
import math

import jax
import jax.numpy as jnp
from jax import lax
from jax.experimental import pallas as pl
from jax.experimental.pallas import tpu as pltpu

f32 = jnp.float32
bf16 = jnp.bfloat16

D = 1024
D_IN = 6656
NDEV = 8
RNN_BLOCKS = 8
RB = 128
HEAD = 64
KV_HEADS = 4
GROUP = 4
QB = 128
LRU_C = 8.0
EPS = 1e-6
ROPE_DIM = 16
ROPE_THETA = 500000.0
CH = 512
SEC_START = (0, 2, 4, 6, 7, 9, 11)
SEC_CHUNKS = (2, 2, 2, 1, 2, 2, 2)
VMEM_LIMIT = 62 * 1024 * 1024

ADAM_LR, ADAM_B1, ADAM_B2, ADAM_EPS, ADAM_WD, ADAM_STEP = 0.001, 0.9, 0.999, 1e-08, 0.01, 10

MESH = pl.DeviceIdType.MESH
ANY = pl.BlockSpec(memory_space=pl.ANY)
VMEM_SPEC = pl.BlockSpec(memory_space=pltpu.VMEM)
SMEM_SPEC = pl.BlockSpec(memory_space=pltpu.SMEM)


def _pcall(body, **kw):
    return pl.pallas_call(body, **kw)


def _params(sem=None, **kw):
    if sem is not None:
        kw["dimension_semantics"] = sem
    return pltpu.CompilerParams(vmem_limit_bytes=VMEM_LIMIT, **kw)


def _sds(shape, dtype):
    return jax.ShapeDtypeStruct(shape, dtype)


def _dot(a, b, dims):
    return lax.dot_general(a, b, (dims, ((), ())), preferred_element_type=f32)


NN = ((1,), (0,))
NT = ((1,), (1,))
TN = ((0,), (0,))


def _sigmoid(v):
    return 0.5 * jnp.tanh(0.5 * v) + 0.5


def _sigmoid_positive(v):
    return 1.0 / (1.0 + jnp.exp(-v))


def _my_place():
    return lax.axis_index("x"), lax.axis_index("y"), lax.axis_index("c")


def _peer(k):
    x, y, c = _my_place()
    return (x + ((k >> 2) & 1)) % 2, (y + ((k >> 1) & 1)) % 2, (c + (k & 1)) % 2


def _direct_gather_copies(srcs, outs, send_sems, recv_sems, local_sems):
    x, y, c = _my_place()
    me = 4 * x + 2 * y + c
    local, remote = [], []
    for a, (src, out) in enumerate(zip(srcs, outs)):
        r = src.shape[0]
        mine = out.at[pl.ds(pl.multiple_of(me * r, 8), r), :]
        local.append(pltpu.make_async_copy(src, mine, local_sems.at[a]))
        for k in range(1, NDEV):
            remote.append(pltpu.make_async_remote_copy(
                src_ref=src, dst_ref=mine, send_sem=send_sems.at[7 * a + k - 1], recv_sem=recv_sems.at[7 * a + k - 1],
                device_id=_peer(k), device_id_type=MESH))
    return local, remote


def _chip_exchange_copies(src, dst, send_sems, recv_sems, local_sems):
    x, y, c = _my_place()
    local, remote = [], []
    for a in range(len(src)):
        local.append(pltpu.make_async_copy(src[a].at[2 * x + y], dst[a].at[0], local_sems.at[a]))
    for k in (3, 1, 2):
        px, py = (x + (k >> 1)) % 2, (y + (k & 1)) % 2
        for a in range(len(src)):
            remote.append(pltpu.make_async_remote_copy(
                src_ref=src[a].at[2 * px + py], dst_ref=dst[a].at[k],
                send_sem=send_sems.at[3 * a + k - 1], recv_sem=recv_sems.at[3 * a + k - 1],
                device_id=(px, py, c), device_id_type=MESH))
    return local, remote


def _exchange_scratch(narr, per_array):
    return [pltpu.SemaphoreType.DMA((per_array * narr,)), pltpu.SemaphoreType.DMA((per_array * narr,)),
            pltpu.SemaphoreType.DMA((narr,))]


def _start_all(copies):
    local, remote = copies
    for cp in local + remote:
        cp.start()


def _wait_all(copies):
    local, remote = copies
    for cp in remote + local:
        cp.wait()


def _row_tile(rows, dtype):
    unit = 16 if dtype == bf16 else 8
    for cand in (256, 208, 128, 64, 40, 32, 16, 8):
        if rows % cand == 0 and cand % unit == 0:
            return cand
    return rows


def _pair_sums(grads, wire_dtype, my_core, tag):
    narr = len(grads)
    r, cols = grads[0].shape[0] // NDEV, grads[0].shape[1]
    views = [g.reshape(4, 2, r, cols) for g in grads]
    tr = _row_tile(r, wire_dtype)
    nt = r // tr

    def body(core_ref, *refs):
        mine = refs[:narr]
        whole = refs[narr:2 * narr]
        outs = refs[2 * narr:3 * narr]
        got = refs[3 * narr:4 * narr]
        send_sems, recv_sems = refs[4 * narr:]
        q, i = pl.program_id(0), pl.program_id(1)
        x, y, c = _my_place()

        def copy(a, chip):
            return pltpu.make_async_remote_copy(
                src_ref=whole[a].at[chip, 1 - c], dst_ref=got[a].at[chip],
                send_sem=send_sems.at[4 * a + chip], recv_sem=recv_sems.at[4 * a + chip],
                device_id=(x, y, 1 - c), device_id_type=MESH)

        @pl.when((q == 0) & (i == 0))
        def _():
            for chip in range(4):
                for a in range(narr):
                    copy(a, chip).start()

        for chip in range(4):
            @pl.when((q == chip) & (i == 0))
            def _(chip=chip):
                for a in range(narr):
                    copy(a, chip).wait_recv()

        rows = pl.ds(pl.multiple_of(i * tr, tr), tr)
        for a in range(narr):
            outs[a][...] = (mine[a][...].astype(f32) + got[a][q, rows, :].astype(f32)).astype(wire_dtype)

        @pl.when((q == 3) & (i == nt - 1))
        def _():
            for chip in range(4):
                for a in range(narr):
                    copy(a, chip).wait_send()

    slab = pl.BlockSpec((None, tr, cols), lambda q, i, core: (q, i, 0))
    grid_spec = pltpu.PrefetchScalarGridSpec(
        num_scalar_prefetch=1, grid=(4, nt),
        in_specs=[pl.BlockSpec((None, None, tr, cols), lambda q, i, core: (q, core[0], i, 0))] * narr + [ANY] * narr,
        out_specs=tuple([slab] * narr),
        scratch_shapes=[pltpu.VMEM((4, r, cols), grads[0].dtype)] * narr
        + [pltpu.SemaphoreType.DMA((4 * narr,)), pltpu.SemaphoreType.DMA((4 * narr,))])
    return _pcall(body, name="pair_sums_" + tag, grid_spec=grid_spec,
                  out_shape=tuple(_sds((4, r, cols), wire_dtype) for _ in range(narr)),
                  compiler_params=_params(("arbitrary", "arbitrary")))(my_core, *views, *views)


def _adam_math(g, w, m, v):
    m_new = ADAM_B1 * m + (1.0 - ADAM_B1) * g
    v_new = ADAM_B2 * v + (1.0 - ADAM_B2) * (g * g)
    m_hat = m_new / (1.0 - ADAM_B1 ** ADAM_STEP)
    v_hat = v_new / (1.0 - ADAM_B2 ** ADAM_STEP)
    return -ADAM_LR * (m_hat / (jnp.sqrt(v_hat) + ADAM_EPS) + ADAM_WD * w), m_new, v_new


def _adamw(first, parts, w, m, v, name):
    n, rows, cols = parts.shape
    tr = _row_tile(rows, parts.dtype)

    def body(f_ref, p_ref, w_ref, m_ref, v_ref, g_out, d_out, m_out, v_out):
        g = f_ref[...].astype(f32)
        for s in range(n):
            g = g + p_ref[s].astype(f32)
        g_out[...] = g
        d_out[...], m_out[...], v_out[...] = _adam_math(g, w_ref[...], m_ref[...], v_ref[...])

    blk = pl.BlockSpec((tr, cols), lambda i: (i, 0))
    return _pcall(
        body, name=name, grid=(rows // tr,),
        in_specs=[blk, pl.BlockSpec((n, tr, cols), lambda i: (0, i, 0)), blk, blk, blk],
        out_specs=(blk, blk, blk, blk), out_shape=tuple(_sds((rows, cols), f32) for _ in range(4)),
        compiler_params=_params(("arbitrary",)),
    )(first, parts, w, m, v)


def _adamw_group(parts, ws, ms, vs, name):
    nw = len(ws)

    def body(*refs):
        p_refs, w_refs, m_refs, v_refs = (refs[k * nw:(k + 1) * nw] for k in range(4))
        outs = refs[4 * nw:]
        for k in range(nw):
            g = p_refs[k][0].astype(f32)
            for s in range(1, p_refs[k].shape[0]):
                g = g + p_refs[k][s].astype(f32)
            g_out, d_out, m_out, v_out = outs[4 * k:4 * k + 4]
            g_out[...] = g
            d_out[...], m_out[...], v_out[...] = _adam_math(g, w_refs[k][...], m_refs[k][...], v_refs[k][...])

    res = _pcall(
        body, name=name, out_shape=tuple(_sds(w.shape, f32) for w in ws for _ in range(4)),
        in_specs=[VMEM_SPEC] * (4 * nw), out_specs=tuple([VMEM_SPEC] * (4 * nw)), compiler_params=_params(),
    )(*parts, *ws, *ms, *vs)
    return [res[4 * k:4 * k + 4] for k in range(nw)]


def _rope(t, c, s1, s2):
    w = t.shape[1]
    return t * c + pltpu.roll(t, w - 8, 1) * s1 + pltpu.roll(t, 8, 1) * s2


def _rope_transposed(dt, c, s1, s2):
    w = dt.shape[1]
    return dt * c + pltpu.roll(dt * s1, 8, 1) + pltpu.roll(dt * s2, w - 8, 1)


PAIR_ROWS = D_IN // 4
SUB_COLS = ((0, 512), (512, 512), (1024, 512), (1536, 128))
Q_SLABS = range(3, 11)
K_SLABS = range(11, 13)


def _in_proj_gather(x2d, norm_g, wt_shard, cw_shard, tabs, S, out_shards, chip_order):
    T = x2d.shape[0]
    tb = min(S, 1024)
    ntok = T // tb
    nsb = S // tb
    q_scale = 1.0 / math.sqrt(HEAD)
    shard_rows = wt_shard.shape[0]
    small = (cw_shard,) + tuple(out_shards)
    nsm = len(small)

    def body(order_ref, x_ref, g_ref, c_ref, s1_ref, s2_ref, wt_hbm, *rest):
        small_in = rest[:nsm]
        h_ref, proj_ref, wt_out = rest[nsm:nsm + 3]
        small_out = rest[nsm + 3:2 * nsm + 3]
        wt_vm, h_vm = rest[2 * nsm + 3:2 * nsm + 5]
        stage = rest[2 * nsm + 5:3 * nsm + 4]
        wsend, wrecv, wlocal = rest[3 * nsm + 4:3 * nsm + 7]
        dsems = rest[3 * nsm + 7:]
        jj, i = pl.program_id(0), pl.program_id(1)
        x, y, c = _my_place()
        me, sibling = (x, y, c), (x, y, 1 - c)
        chips = [(1 - x, y), (x, 1 - y), (1 - x, 1 - y)]

        def rows(place):
            px, py, pc = place
            return wt_vm.at[pl.ds(pl.multiple_of((4 * px + 2 * py + pc) * shard_rows, 16), shard_rows), :]

        def copy(k, block, to, src=None):
            return pltpu.make_async_remote_copy(
                src_ref=rows(block) if src is None else src, dst_ref=rows(block),
                send_sem=wsend.at[k], recv_sem=wrecv.at[k], device_id=to, device_id_type=MESH)

        def small_copies():
            srcs = (small_in[0],) + tuple(stage)
            return _direct_gather_copies(srcs, small_out, *dsems)

        own = pltpu.make_async_copy(wt_hbm, rows(me), wlocal.at[0])
        keep = pltpu.make_async_copy(wt_vm, wt_out, wlocal.at[1])

        @pl.when((jj == 0) & (i == 0))
        def _():
            own.start()
            copy(0, me, sibling, src=wt_hbm).start()
            for j, chip in enumerate(chips):
                copy(1 + j, me, (*chip, c), src=wt_hbm).start()
            for a in range(nsm - 1):
                stage[a][...] = small_in[1 + a][...].astype(bf16)
            _start_all(small_copies())
            own.wait()
            copy(0, sibling, me).wait_recv()

        for j, chip in enumerate(chips):
            @pl.when((jj == 1 + j) & (i == 0))
            def _(j=j, chip=chip):
                copy(1 + j, (*chip, c), me).wait_recv()
                copy(4 + j, (*chip, c), sibling).start()
                copy(4 + j, (*chip, 1 - c), me).wait_recv()

        @pl.when((jj == 3) & (i == 0))
        def _():
            keep.start()

        @pl.when((jj == 3) & (i == ntok - 1))
        def _():
            copy(0, me, sibling, src=wt_hbm).wait_send()
            for j, chip in enumerate(chips):
                copy(1 + j, me, (*chip, c), src=wt_hbm).wait_send()
                copy(4 + j, (*chip, c), sibling).wait_send()
            _wait_all(small_copies())
            keep.wait()

        tok = pl.ds(pl.multiple_of(i * tb, tb), tb)

        @pl.when(jj == 0)
        def _():
            xv = x_ref[...]
            ms = jnp.mean(xv * xv, axis=-1, keepdims=True)
            hb = (xv * lax.rsqrt(ms + EPS) * g_ref[...]).astype(bf16)
            h_ref[...] = hb
            h_vm[tok, :] = hb

        block = order_ref[jj]
        hb = h_vm[tok, :]

        def piece(c0, w):
            w_rows = wt_vm[pl.ds(pl.multiple_of(block * PAIR_ROWS + c0, 128), w), :]
            return _dot(hb, w_rows, NT)

        @pl.when(block != 1)
        def _():
            for c0, w in SUB_COLS:
                proj_ref[:, c0:c0 + w] = piece(c0, w).astype(bf16)

        @pl.when(block == 1)
        def _():
            tab = (c_ref[...], s1_ref[...], s2_ref[...])
            for c0, w in SUB_COLS:
                acc = piece(c0, w)
                for l in range(w // 128):
                    slab = (c0 + 128 * l) // 128
                    part = acc[:, 128 * l:128 * (l + 1)]
                    if slab in Q_SLABS:
                        part = _rope(part, *tab) * q_scale
                    elif slab in K_SLABS:
                        part = _rope(part, *tab)
                    proj_ref[:, 128 * slab:128 * (slab + 1)] = part.astype(bf16)

    first_pass = lambda jj, i, order: (jnp.where(jj == 0, i, ntok - 1), 0)
    const = lambda jj, i, order: (0, 0)
    tab = pl.BlockSpec((tb, 128), lambda jj, i, order: (jnp.where(order[jj] == 1, i % nsb, 0), 0))
    grid_spec = pltpu.PrefetchScalarGridSpec(
        num_scalar_prefetch=1, grid=(4, ntok),
        in_specs=[pl.BlockSpec((tb, D), first_pass), pl.BlockSpec((1, D), const), tab, tab, tab, ANY]
        + [pl.BlockSpec(w.shape, const) for w in small],
        out_specs=(pl.BlockSpec((tb, D), first_pass),
                   pl.BlockSpec((tb, PAIR_ROWS), lambda jj, i, order: (i, order[jj])), ANY) + tuple([ANY] * nsm),
        scratch_shapes=[pltpu.VMEM((D_IN, D), bf16), pltpu.VMEM((T, D), bf16)]
        + [pltpu.VMEM(w.shape, bf16) for w in out_shards]
        + [pltpu.SemaphoreType.DMA((7,)), pltpu.SemaphoreType.DMA((7,)), pltpu.SemaphoreType.DMA((2,))]
        + _exchange_scratch(nsm, 7))
    res = _pcall(
        body, name="in_proj", grid_spec=grid_spec,
        out_shape=(_sds((T, D), bf16), _sds((T, D_IN), bf16), _sds((D_IN, D), bf16),
                   _sds((NDEV * cw_shard.shape[0], cw_shard.shape[1]), f32))
        + tuple(_sds((NDEV * w.shape[0], w.shape[1]), bf16) for w in out_shards),
        compiler_params=_params(("arbitrary", "arbitrary")),
    )(chip_order, x2d, norm_g, *tabs, wt_shard, *small)
    return res[0], res[1], res[2], res[3], res[4:]


def _rows_iota(shape):
    return lax.broadcasted_iota(jnp.int32, shape, 0)


def _shift_down(v, k):
    return jnp.where(_rows_iota(v.shape) >= k, pltpu.roll(v, k, 0), 0.0)


def _shift_up(v, k):
    n = v.shape[0]
    return jnp.where(_rows_iota(v.shape) < n - k, pltpu.roll(v, n - k, 0), 0.0)


def _linear_scan(a, b, a_s, b_s, edge_s, out_ref, reverse):
    n = a.shape[0]
    ng = n // 8
    a3, b3 = a.reshape(ng, 8, RB), b.reshape(ng, 8, RB)
    rid = lax.broadcasted_iota(jnp.int32, a3.shape, 1)
    for s in (1, 2, 4):
        keep, shift = (rid < 8 - s, 8 - s) if reverse else (rid >= s, s)
        b3 = jnp.where(keep, a3 * pltpu.roll(b3, shift, 1) + b3, b3)
        a3 = jnp.where(keep, a3 * pltpu.roll(a3, shift, 1), a3)
    a_s[...] = a3.reshape(n, RB)
    b_s[...] = b3.reshape(n, RB)
    edge = 0 if reverse else 7
    ea, eb = a_s[pl.ds(edge, ng, stride=8), :], b_s[pl.ds(edge, ng, stride=8), :]
    r = _rows_iota(ea.shape)
    s = 1
    while s < ng:
        keep, shift = (r < ng - s, ng - s) if reverse else (r >= s, s)
        eb = jnp.where(keep, ea * pltpu.roll(eb, shift, 0) + eb, eb)
        if 2 * s < ng:
            ea = jnp.where(keep, ea * pltpu.roll(ea, shift, 0), ea)
        s *= 2
    edge_s[...] = _shift_up(eb, 1) if reverse else _shift_down(eb, 1)

    def eight_groups(i, carry):
        for k in range(8):
            j = i * 8 + k
            rows = pl.ds(pl.multiple_of(j * 8, 8), 8)
            out_ref[rows, :] = b_s[rows, :] + a_s[rows, :] * edge_s[pl.ds(j, 1), :]
        return carry

    lax.fori_loop(0, ng // 8, eight_groups, 0)


def _neg_expm1(v):
    series = -v * (1.0 + v * (0.5 + v * (1.0 / 6.0)))
    return jnp.where(v > -0.015625, series, 1.0 - jnp.exp(v))


def _softplus_neg(lam):
    return jnp.maximum(-lam, 0.0) + jnp.log(1.0 + jnp.exp(-jnp.abs(lam)))


def _lru_gates(x0, cw, cb, wa, ba, wx, bx, lam):
    taps = [_shift_down(x0, 3 - k) for k in range(3)] + [x0]
    u = cb + cw[3:4, :] * x0
    for k in range(3):
        u = u + cw[k:k + 1, :] * taps[k]
    ub = u.astype(bf16)
    r = _sigmoid_positive(_dot(ub, wa.astype(bf16), NN) + ba)
    i = _sigmoid(_dot(ub, wx.astype(bf16), NN) + bx)
    sp = _softplus_neg(lam)
    log_a = (-LRU_C) * r * sp
    a = jnp.exp(log_a)
    w = _neg_expm1(2.0 * log_a)
    inv_mult = lax.rsqrt(w)
    return u, ub, r, i, sp, a, w * inv_mult, inv_mult, taps


def _lru_specs(S, nb):
    col = lambda off: pl.BlockSpec((S, RB), lambda n, b, off=off: (b, off + n))
    vec = pl.BlockSpec((1, RB), lambda n, b: (0, n))
    wblk = pl.BlockSpec((None, RB, RB), lambda n, b: (n, 0, 0))
    cwblk = pl.BlockSpec((8, RB), lambda n, b: (n, 0))
    return col, vec, wblk, cwblk


def _lru_forward(proj, cw_full, conv_b, w_a, b_a, w_x, b_x, lam, S):
    T = proj.shape[0]
    nb = T // S
    col, vec, wblk, cwblk = _lru_specs(S, nb)

    def body(x0_ref, g_ref, cw_ref, cb_ref, wa_ref, ba_ref, wx_ref, bx_ref, lam_ref, y_ref, h_ref, a_s, b_s, edge_s):
        x0 = x0_ref[...].astype(f32)
        u, ub, r, i, sp, a, mult, _, _ = _lru_gates(x0, cw_ref[...], cb_ref[...], wa_ref[...], ba_ref[...],
                                                    wx_ref[...], bx_ref[...], lam_ref[...])
        _linear_scan(a, mult * (i * u), a_s, b_s, edge_s, h_ref, reverse=False)
        g = g_ref[...].astype(f32)
        y_ref[...] = (h_ref[...] * (g * _sigmoid(g))).astype(bf16)

    out = pl.BlockSpec((S, RB), lambda n, b: (b, n))
    return _pcall(
        body, name="lru_forward", grid=(RNN_BLOCKS, nb),
        in_specs=[col(0), col(8), cwblk, vec, wblk, vec, wblk, vec, vec],
        out_specs=(out, out), out_shape=(_sds((T, D), bf16), _sds((T, D), f32)),
        scratch_shapes=[pltpu.VMEM((S, RB), f32), pltpu.VMEM((S, RB), f32), pltpu.VMEM((S // 8, RB), f32)],
        compiler_params=_params(("arbitrary", "arbitrary")),
    )(proj, proj, cw_full, conv_b, w_a, b_a, w_x, b_x, lam)


def _rope_tables(S):
    pos = jnp.arange(S, dtype=f32)
    inv_freq = ROPE_THETA ** (-jnp.arange(0, ROPE_DIM, 2, dtype=f32) / ROPE_DIM)
    ang = pos[:, None] * inv_freq[None, :]
    cos, sin = jnp.cos(ang), jnp.sin(ang)
    lane = jnp.arange(128) % HEAD
    cosl, sinl = cos[:, lane % 8], sin[:, lane % 8]
    c = jnp.where(lane[None, :] < ROPE_DIM, cosl, 1.0)
    s1 = jnp.where(lane[None, :] < 8, -sinl, 0.0)
    s2 = jnp.where((lane[None, :] >= 8) & (lane[None, :] < ROPE_DIM), sinl, 0.0)
    return c.astype(f32), s1.astype(f32), s2.astype(f32)


def _heads_to_rows(t):
    return jnp.concatenate([t[:, HEAD * h:HEAD * (h + 1)] for h in range(GROUP)], axis=0)


def _rows_to_heads(t):
    return jnp.concatenate([t[QB * h:QB * (h + 1), :] for h in range(GROUP)], axis=1)


def _window_bias(first_block):
    shape = (GROUP * QB, 2 * QB)
    qi = _rows_iota(shape) % QB
    cj = lax.broadcasted_iota(jnp.int32, shape, 1)
    valid = (cj > qi) & (cj <= qi + QB) & ((cj >= QB) | jnp.logical_not(first_block))
    return jnp.where(valid, 0.0, -jnp.inf)


def _attn_probs(q_rows, k_cat, sink_col, bias):
    s = _dot(q_rows, k_cat, NT) + bias
    m = jnp.maximum(jnp.max(s, axis=1, keepdims=True), sink_col)
    p = jnp.exp(s - m)
    e_sink = jnp.exp(sink_col - m)
    inv = 1.0 / (jnp.sum(p, axis=1, keepdims=True) + e_sink)
    return p * inv, e_sink * inv


def _sink_column(sink_ref, kv):
    rid = _rows_iota((GROUP * QB, 1))
    col = jnp.zeros((GROUP * QB, 1), f32)
    for h in range(GROUP):
        col = jnp.where(rid // QB == h, sink_ref[0, GROUP * kv + h], col)
    return col


def _attn_in_specs(S):
    nq = S // QB
    last = nq - 1
    cur = lambda b, j: b * nq + jnp.minimum(j, last)
    prev = lambda b, j: b * nq + jnp.maximum(jnp.minimum(j, last) - 1, 0)
    specs = [
        pl.BlockSpec((QB, D), lambda b, j: (cur(b, j), 2)),
        pl.BlockSpec((QB, 256), lambda b, j: (cur(b, j), 12)),
        pl.BlockSpec((QB, 256), lambda b, j: (prev(b, j), 12)),
        pl.BlockSpec((QB, 256), lambda b, j: (cur(b, j), 13)),
        pl.BlockSpec((QB, 256), lambda b, j: (prev(b, j), 13)),
        pl.BlockSpec((QB, 512), lambda b, j: (cur(b, j), 7)),
        pl.BlockSpec((QB, 512), lambda b, j: (cur(b, j), 8)),
        SMEM_SPEC,
    ]
    return specs, cur, prev


def _attn_forward(proj, sinks, S, out_shards):
    T = proj.shape[0]
    nb, nq = T // S, S // QB
    specs, cur, _ = _attn_in_specs(S)
    nw = len(out_shards)

    def body(q_ref, kc_ref, kp_ref, vc_ref, vp_ref, gl_ref, gh_ref, sink_ref, *rest):
        shards = rest[:nw]
        y_ref = rest[nw]
        gathered = rest[nw + 1:2 * nw + 1]
        stage = rest[2 * nw + 1:3 * nw + 1]
        sems = rest[3 * nw + 1:]
        b, j = pl.program_id(0), pl.program_id(1)

        @pl.when((b == 0) & (j == 0))
        def _():
            for a in range(nw):
                stage[a][...] = shards[a][...].astype(bf16)
            _start_all(_direct_gather_copies(stage, gathered, *sems))

        @pl.when((b == nb - 1) & (j == nq - 1))
        def _():
            _wait_all(_direct_gather_copies(stage, gathered, *sems))

        bias = _window_bias(j == 0)
        kc, kp, vc, vp = kc_ref[...], kp_ref[...], vc_ref[...], vp_ref[...]
        for kv in range(KV_HEADS):
            lanes = slice(256 * kv, 256 * (kv + 1))
            hl = slice(HEAD * kv, HEAD * (kv + 1))
            q_rows = _heads_to_rows(q_ref[:, lanes])
            k_cat = jnp.concatenate([kp[:, hl], kc[:, hl]], axis=0)
            v_cat = jnp.concatenate([vp[:, hl], vc[:, hl]], axis=0)
            probs, _ = _attn_probs(q_rows, k_cat, _sink_column(sink_ref, kv), bias)
            o = _rows_to_heads(_dot(probs.astype(bf16), v_cat, NN))
            g_src = gl_ref if kv < 2 else gh_ref
            g = g_src[:, 256 * (kv % 2):256 * (kv % 2 + 1)].astype(f32)
            y_ref[:, lanes] = (o * (g * _sigmoid(g))).astype(bf16)

    args = [proj] * 7 + [sinks] + list(out_shards)
    res = _pcall(
        body, name="attn_forward", grid=(nb, nq),
        in_specs=specs + [pl.BlockSpec(w.shape, lambda b, j: (0, 0)) for w in out_shards],
        out_specs=(pl.BlockSpec((QB, D), lambda b, j: (cur(b, j), 0)),) + tuple([ANY] * nw),
        out_shape=(_sds((T, D), bf16),) + tuple(_sds((NDEV * w.shape[0], w.shape[1]), bf16) for w in out_shards),
        scratch_shapes=[pltpu.VMEM(w.shape, bf16) for w in out_shards] + _exchange_scratch(nw, 7),
        compiler_params=_params(("arbitrary", "arbitrary")),
    )(*args)
    return res[0], res[1:]


def _merge_and_head(x2d, tgt, proj, y_rnn, y_attn, w_r, w_a, w_o, gfin):
    T = x2d.shape[0]
    tb = min(T, 512)
    nsteps = T // tb

    def body(x_ref, t_ref, mr0, mr1, ma0, ma1, yr_ref, ya_ref, wr_ref, wa_ref, wo_ref, gf_ref,
             dx2_ref, dyr_ref, dya_ref, dmr_ref, dma_ref, loss_ref, gfin_ref, gwr_out, gwa_out, gwo_out,
             gwr_acc, gwa_acc, gwo_acc, out_sems):
        step = pl.program_id(0)

        @pl.when(step == 0)
        def _():
            loss_ref[...] = jnp.zeros_like(loss_ref)
            gfin_ref[...] = jnp.zeros_like(gfin_ref)
            gwr_acc[...] = jnp.zeros_like(gwr_acc)
            gwa_acc[...] = jnp.zeros_like(gwa_acc)
            gwo_acc[...] = jnp.zeros_like(gwo_acc)

        sr = _sigmoid(jnp.concatenate([mr0[...], mr1[...]], axis=1).astype(f32))
        sa = _sigmoid(jnp.concatenate([ma0[...], ma1[...]], axis=1).astype(f32))
        p_r = _dot(yr_ref[...], wr_ref[...], NN)
        p_a = _dot(ya_ref[...], wa_ref[...], NN)
        merged = (sr * p_r + sa * p_a).astype(bf16)
        x2 = x_ref[...] + _dot(merged, wo_ref[...], NN)
        rstd = lax.rsqrt(jnp.mean(x2 * x2, axis=-1, keepdims=True) + EPS)
        xh = x2 * rstd
        gf = gf_ref[...]
        err = xh * gf - t_ref[...]
        loss_ref[...] += jnp.sum(err * err)
        dy = err * (1.0 / D)
        gfin_ref[0:1, :] += jnp.sum(dy * xh, axis=0, keepdims=True)
        dxn = dy * gf
        dx2 = rstd * (dxn - xh * jnp.mean(dxn * xh, axis=-1, keepdims=True))
        dx2_ref[...] = dx2
        dx2b = dx2.astype(bf16)
        dmerged = _dot(dx2b, wo_ref[...], NT)
        dmr_ref[...] = (dmerged * p_r * (sr * (1.0 - sr))).astype(bf16)
        dma_ref[...] = (dmerged * p_a * (sa * (1.0 - sa))).astype(bf16)
        dpr = (dmerged * sr).astype(bf16)
        dpa = (dmerged * sa).astype(bf16)
        dyr_ref[...] = _dot(dpr, wr_ref[...], NT).astype(bf16)
        dya_ref[...] = _dot(dpa, wa_ref[...], NT).astype(bf16)
        gwr_acc[...] += _dot(yr_ref[...], dpr, TN)
        gwa_acc[...] += _dot(ya_ref[...], dpa, TN)
        gwo_acc[...] += _dot(merged, dx2b, TN)

        @pl.when(step == nsteps - 1)
        def _():
            copies = [pltpu.make_async_copy(src, dst, out_sems.at[k]) for k, (src, dst) in enumerate(
                ((gwr_acc, gwr_out), (gwa_acc, gwa_out), (gwo_acc, gwo_out)))]
            for cp in copies:
                cp.start()
            for cp in copies:
                cp.wait()

    tok = pl.BlockSpec((tb, D), lambda i: (i, 0))
    half = lambda c: pl.BlockSpec((tb, CH), lambda i, c=c: (i, c))
    wfull = pl.BlockSpec((D, D), lambda i: (0, 0), pipeline_mode=pl.Buffered(1))
    acc = pl.BlockSpec((8, D), lambda i: (0, 0))
    return _pcall(
        body, name="merge_and_head", grid=(nsteps,),
        in_specs=[tok, tok, half(9), half(10), half(11), half(12), tok, tok, wfull, wfull, wfull,
                  pl.BlockSpec((1, D), lambda i: (0, 0))],
        out_specs=(tok, tok, tok, tok, tok, acc, acc, ANY, ANY, ANY),
        out_shape=(_sds((T, D), f32), _sds((T, D), bf16), _sds((T, D), bf16), _sds((T, D), bf16),
                   _sds((T, D), bf16), _sds((8, D), f32), _sds((8, D), f32),
                   _sds((D, D), f32), _sds((D, D), f32), _sds((D, D), f32)),
        scratch_shapes=[pltpu.VMEM((D, D), f32)] * 3 + [pltpu.SemaphoreType.DMA((3,))],
        compiler_params=_params(("arbitrary",)),
    )(x2d, tgt, proj, proj, proj, proj, y_rnn, y_attn, w_r, w_a, w_o, gfin)


def _attn_backward(proj, dy_attn, tabs, sinks, S, chip_sums):
    T = proj.shape[0]
    nb, nq = T // S, S // QB
    nex = len(chip_sums)
    specs, cur, prev = _attn_in_specs(S)
    last = nq - 1
    tab_cur = pl.BlockSpec((QB, 128), lambda b, j: (jnp.minimum(j, last), 0))
    tab_prev = pl.BlockSpec((QB, 128), lambda b, j: (jnp.maximum(jnp.minimum(j, last) - 1, 0), 0))
    specs = specs + [pl.BlockSpec((QB, D), lambda b, j: (cur(b, j), 0))] + [tab_cur] * 3 + [tab_prev] * 3
    q_scale = 1.0 / math.sqrt(HEAD)

    def rope_back(dt, tab):
        return jnp.concatenate([_rope_transposed(dt[:, 128 * l:128 * (l + 1)], *tab) for l in range(2)], axis=1)

    def body(q_ref, kc_ref, kp_ref, vc_ref, vp_ref, gl_ref, gh_ref, sink_ref, dy_ref, cc, s1c, s2c, cp, s1p, s2p,
             *rest):
        ex_src = rest[:nex]
        dq_ref, dkv_ref, dg_ref, dsink_ref = rest[nex:nex + 4]
        ex_dst = rest[nex + 4:2 * nex + 4]
        carry_k, carry_v = rest[2 * nex + 4:2 * nex + 6]
        sems = rest[2 * nex + 6:]
        b, j = pl.program_id(0), pl.program_id(1)

        @pl.when((b == 0) & (j == 0))
        def _():
            dsink_ref[...] = jnp.zeros_like(dsink_ref)
            _start_all(_chip_exchange_copies(ex_src, ex_dst, *sems))

        @pl.when((b == nb - 1) & (j == nq))
        def _():
            _wait_all(_chip_exchange_copies(ex_src, ex_dst, *sems))

        @pl.when(j == 0)
        def _():
            carry_k[...] = jnp.zeros_like(carry_k)
            carry_v[...] = jnp.zeros_like(carry_v)

        @pl.when(j < nq)
        def _():
            bias = _window_bias(j == 0)
            tc = (cc[...], s1c[...], s2c[...])
            tp = (cp[...], s1p[...], s2p[...])
            kc, kp, vc, vp = kc_ref[...], kp_ref[...], vc_ref[...], vp_ref[...]
            dk_prev, dk_cur, dv_prev, dv_cur = [], [], [], []
            dsink_acc = jnp.zeros((8, 128), f32)
            r8 = lax.broadcasted_iota(jnp.int32, (8, 128), 0)
            l8 = lax.broadcasted_iota(jnp.int32, (8, 128), 1)
            for kv in range(KV_HEADS):
                lanes = slice(256 * kv, 256 * (kv + 1))
                hl = slice(HEAD * kv, HEAD * (kv + 1))
                q_rows = _heads_to_rows(q_ref[:, lanes])
                k_cat = jnp.concatenate([kp[:, hl], kc[:, hl]], axis=0)
                v_cat = jnp.concatenate([vp[:, hl], vc[:, hl]], axis=0)
                probs, p_sink = _attn_probs(q_rows, k_cat, _sink_column(sink_ref, kv), bias)
                pb = probs.astype(bf16)
                o = _rows_to_heads(_dot(pb, v_cat, NN))
                g_src = gl_ref if kv < 2 else gh_ref
                g = g_src[:, 256 * (kv % 2):256 * (kv % 2 + 1)].astype(f32)
                sg = _sigmoid(g)
                dy = dy_ref[:, lanes].astype(f32)
                dg_ref[:, lanes] = (dy * o * (sg * (1.0 + g * (1.0 - sg)))).astype(bf16)
                do_rows = _heads_to_rows(dy * (g * sg)).astype(bf16)
                dv = _dot(pb, do_rows, TN)
                dp = _dot(do_rows, v_cat, NT)
                rowdot = jnp.sum(probs * dp, axis=1, keepdims=True)
                ds = (probs * (dp - rowdot)).astype(bf16)
                sink_rows = -(p_sink * rowdot)
                for h in range(GROUP):
                    val = jnp.sum(sink_rows[QB * h:QB * (h + 1), :])
                    dsink_acc = dsink_acc + jnp.where((r8 == 0) & (l8 == GROUP * kv + h), val, 0.0)
                dq = _rows_to_heads(_dot(ds, k_cat, NN)) * q_scale
                dq_ref[:, lanes] = rope_back(dq, tc).astype(bf16)
                dk = _dot(ds, q_rows, TN)
                dk_prev.append(dk[:QB, :])
                dk_cur.append(dk[QB:, :])
                dv_prev.append(dv[:QB, :])
                dv_cur.append(dv[QB:, :])
            dsink_ref[...] += dsink_acc
            dkp = rope_back(jnp.concatenate(dk_prev, axis=1), tp)
            dkc = rope_back(jnp.concatenate(dk_cur, axis=1), tc)
            dkv_ref[:, 0:256] = (carry_k[...] + dkp).astype(bf16)
            dkv_ref[:, 256:512] = (carry_v[...] + jnp.concatenate(dv_prev, axis=1)).astype(bf16)
            carry_k[...] = dkc
            carry_v[...] = jnp.concatenate(dv_cur, axis=1)

        @pl.when(j == nq)
        def _():
            dkv_ref[:, 0:256] = carry_k[...].astype(bf16)
            dkv_ref[:, 256:512] = carry_v[...].astype(bf16)

    lag = lambda b, j: (b * nq + jnp.maximum(j - 1, 0), 0)
    args = [proj] * 7 + [sinks, dy_attn] + list(tabs) + list(tabs) + list(chip_sums)
    res = _pcall(
        body, name="attn_backward", grid=(nb, nq + 1), in_specs=specs + [ANY] * nex,
        out_specs=(pl.BlockSpec((QB, D), lambda b, j: (cur(b, j), 0)), pl.BlockSpec((QB, 512), lag),
                   pl.BlockSpec((QB, D), lambda b, j: (cur(b, j), 0)), pl.BlockSpec((8, 128), lambda b, j: (0, 0)))
        + tuple([ANY] * nex),
        out_shape=(_sds((T, D), bf16), _sds((T, 512), bf16), _sds((T, D), bf16), _sds((8, 128), f32))
        + tuple(_sds(s.shape, s.dtype) for s in chip_sums),
        scratch_shapes=[pltpu.VMEM((QB, 256), f32), pltpu.VMEM((QB, 256), f32)] + _exchange_scratch(nex, 3),
        compiler_params=_params(("arbitrary", "arbitrary")),
    )(*args)
    return res[:4], res[4:]


def _lru_backward(proj, h_all, dy_rnn, cw_full, conv_b, w_a, b_a, w_x, b_x, lam, S):
    T = proj.shape[0]
    nb = T // S
    col, vec, wblk, cwblk = _lru_specs(S, nb)
    tokblk = pl.BlockSpec((S, RB), lambda n, b: (b, n))

    def body(x0_ref, g_ref, h_ref, dy_ref, cw_ref, cb_ref, wa_ref, ba_ref, wx_ref, bx_ref, lam_ref,
             du0_ref, dg_ref, gwa_ref, gwx_ref, vec_ref, gcw_ref, a_s, b_s, dh_s, edge_s):
        @pl.when(pl.program_id(1) == 0)
        def _():
            gwa_ref[...] = jnp.zeros_like(gwa_ref)
            gwx_ref[...] = jnp.zeros_like(gwx_ref)
            vec_ref[...] = jnp.zeros_like(vec_ref)
            gcw_ref[...] = jnp.zeros_like(gcw_ref)

        x0 = x0_ref[...].astype(f32)
        cw = cw_ref[...]
        lam_v = lam_ref[...]
        u, ub, r, i, sp, a, mult, inv_mult, taps = _lru_gates(x0, cw, cb_ref[...], wa_ref[...], ba_ref[...],
                                                              wx_ref[...], bx_ref[...], lam_v)
        h = h_ref[...]
        g = g_ref[...].astype(f32)
        dy = dy_ref[...].astype(f32)
        sg = _sigmoid(g)
        dg_ref[...] = (dy * h * (sg * (1.0 + g * (1.0 - sg)))).astype(bf16)
        _linear_scan(_shift_up(a, 1), dy * (g * sg), a_s, b_s, edge_s, dh_s, reverse=True)
        dh_total = dh_s[...]
        da = dh_total * _shift_down(h, 1)
        dmult = dh_total * (i * u)
        db = dh_total * mult
        di = db * u
        du = db * i
        dlog_a_c = ((-LRU_C) * a) * (da - dmult * (a * inv_mult))
        dr = dlog_a_c * sp
        dsp = jnp.sum(dlog_a_c * r, axis=0, keepdims=True)
        dpre_r = dr * r * (1.0 - r)
        dpre_i = di * i * (1.0 - i)
        dpre_rb = dpre_r.astype(bf16)
        dpre_ib = dpre_i.astype(bf16)
        du = du + _dot(dpre_rb, wa_ref[...].astype(bf16), NT) + _dot(dpre_ib, wx_ref[...].astype(bf16), NT)
        gwa_ref[...] += _dot(ub, dpre_rb, TN)
        gwx_ref[...] += _dot(ub, dpre_ib, TN)
        vec_ref[0:1, :] += jnp.sum(du, axis=0, keepdims=True)
        vec_ref[1:2, :] += jnp.sum(dpre_r, axis=0, keepdims=True)
        vec_ref[2:3, :] += jnp.sum(dpre_i, axis=0, keepdims=True)
        vec_ref[3:4, :] += dsp * (-_sigmoid(-lam_v))
        dx0 = cw[3:4, :] * du
        for k in range(3):
            dx0 = dx0 + cw[k:k + 1, :] * _shift_up(du, 3 - k)
        for k in range(4):
            gcw_ref[k:k + 1, :] += jnp.sum(du * taps[k], axis=0, keepdims=True)
        du0_ref[...] = dx0.astype(bf16)

    wacc = pl.BlockSpec((RB, RB), lambda n, b: (0, n))
    vacc = pl.BlockSpec((8, RB), lambda n, b: (0, n))
    cacc = pl.BlockSpec((8, RB), lambda n, b: (n, 0))
    return _pcall(
        body, name="lru_backward", grid=(RNN_BLOCKS, nb),
        in_specs=[col(0), col(8), tokblk, tokblk, cwblk, vec, wblk, vec, wblk, vec, vec],
        out_specs=(tokblk, tokblk, wacc, wacc, vacc, cacc),
        out_shape=(_sds((T, D), bf16), _sds((T, D), bf16), _sds((RB, D), f32), _sds((RB, D), f32),
                   _sds((8, D), f32), _sds((8 * RNN_BLOCKS, RB), f32)),
        scratch_shapes=[pltpu.VMEM((S, RB), f32)] * 3 + [pltpu.VMEM((S // 8, RB), f32)],
        compiler_params=_params(("arbitrary", "arbitrary")),
    )(proj, proj, h_all, dy_rnn, cw_full, conv_b, w_a, b_a, w_x, b_x, lam)


def _section_of_chunk(s):
    out = []
    for start, n in zip(SEC_START, SEC_CHUNKS):
        inside = (s >= start) & (s < start + n)
        out.append((inside, jnp.clip(s - start, 0, n - 1)))
    return out


EFFECT = pltpu.SideEffectType.DATAFLOW_SIDE_EFFECTING
HBM_SPEC = pl.BlockSpec(memory_space=pltpu.HBM)
SEM_SPEC = pl.BlockSpec(memory_space=pltpu.SEMAPHORE)


def _split_exchange_copies(src_ref, land_ref, send_sems, recv_sems):
    x, y, c = _my_place()
    copies = []
    for k in (3, 1, 2):
        px, py = (x + (k >> 1)) % 2, (y + (k & 1)) % 2
        copies.append(pltpu.make_async_remote_copy(
            src_ref=src_ref.at[2 * px + py], dst_ref=land_ref.at[k - 1], send_sem=send_sems[k - 1],
            recv_sem=recv_sems[k - 1], device_id=(px, py, c), device_id_type=MESH))
    return copies


def _exchange_start(chip_sum):
    _, r, cols = chip_sum.shape

    def body(src_ref, land_ref, s0, s1, s2, r0, r1, r2, src_thru, land_thru, token):
        for cp in _split_exchange_copies(src_ref, land_ref, (s0, s1, s2), (r0, r1, r2)):
            cp.start()
        token[...] = jnp.zeros_like(token)

    land = pltpu.with_memory_space_constraint(lax.empty((3, r, cols), chip_sum.dtype), pltpu.HBM)
    res = _pcall(
        body, name="exchange_start",
        out_shape=tuple([pltpu.SemaphoreType.DMA(())] * 6) + (
            pltpu.HBM(chip_sum.shape, chip_sum.dtype), pltpu.HBM((3, r, cols), chip_sum.dtype), _sds((8, 128), f32)),
        in_specs=(HBM_SPEC, HBM_SPEC), out_specs=tuple([SEM_SPEC] * 6) + (HBM_SPEC, HBM_SPEC, VMEM_SPEC),
        input_output_aliases={0: 6, 1: 7},
        compiler_params=pltpu.CompilerParams(has_side_effects=EFFECT),
    )(pltpu.with_memory_space_constraint(chip_sum, pltpu.HBM), land)
    return res[:6], res[6], res[7], res[8]


def _exchange_wait(sems, src_thru, land_thru, after):
    def body(src_ref, land_ref, s0, s1, s2, r0, r1, r2, *rest):
        for cp in _split_exchange_copies(src_ref, land_ref, (s0, s1, s2), (r0, r1, r2)):
            cp.wait_send()
            cp.wait_recv()

    return _pcall(
        body, name="exchange_wait",
        out_shape=(pltpu.HBM(src_thru.shape, src_thru.dtype), pltpu.HBM(land_thru.shape, land_thru.dtype)),
        in_specs=(HBM_SPEC, HBM_SPEC) + tuple([SEM_SPEC] * 6) + tuple([ANY] * len(after)),
        out_specs=(HBM_SPEC, HBM_SPEC), input_output_aliases={0: 0, 1: 1},
        compiler_params=pltpu.CompilerParams(has_side_effects=EFFECT),
    )(src_thru, land_thru, *sems, *after)[1]


def _input_grad(dsecs, wt_full, x2d, dx2, norm_g):
    T = x2d.shape[0]
    tb = min(T, 512)
    nsec = len(dsecs)
    ntok = T // tb

    def body(*refs):
        secs = refs[:nsec]
        wt_ref, x_ref, dx2_ref, g_ref, dx_ref, gnorm_ref = refs[nsec:]
        i = pl.program_id(0)

        @pl.when(i == 0)
        def _():
            gnorm_ref[...] = jnp.zeros_like(gnorm_ref)

        dh = None
        for a, (start, n) in enumerate(zip(SEC_START, SEC_CHUNKS)):
            part = _dot(secs[a][...], wt_ref[CH * start:CH * (start + n), :], NN)
            dh = part if dh is None else dh + part
        xv = x_ref[...]
        rstd = lax.rsqrt(jnp.mean(xv * xv, axis=-1, keepdims=True) + EPS)
        xh = xv * rstd
        gnorm_ref[0:1, :] += jnp.sum(dh * xh, axis=0, keepdims=True)
        dxn = dh * g_ref[...]
        dx_ref[...] = dx2_ref[...] + rstd * (dxn - xh * jnp.mean(dxn * xh, axis=-1, keepdims=True))

    tok = pl.BlockSpec((tb, D), lambda i: (i, 0))
    return _pcall(
        body, name="input_grad", grid=(ntok,),
        in_specs=[pl.BlockSpec((tb, sec.shape[1]), lambda i: (i, 0)) for sec in dsecs]
        + [pl.BlockSpec((D_IN, D), lambda i: (0, 0), pipeline_mode=pl.Buffered(1)), tok, tok,
           pl.BlockSpec((1, D), lambda i: (0, 0))],
        out_specs=(tok, pl.BlockSpec((8, D), lambda i: (0, 0))),
        out_shape=(_sds((T, D), f32), _sds((8, D), f32)),
        compiler_params=_params(("arbitrary",)),
    )(*dsecs, wt_full, x2d, dx2, norm_g)


def _w_in_grad(dsecs, h_bf):
    T = h_bf.shape[0]
    tk = min(T, 2048)
    nchunks = D_IN // CH
    nsec = len(dsecs)
    nt = T // tk

    def body(*refs):
        secs = refs[:nsec]
        h_ref, out_ref, acc = refs[nsec:]
        s, t = pl.program_id(0), pl.program_id(1)

        @pl.when(t == 0)
        def _():
            acc[...] = jnp.zeros_like(acc)

        h_rows = h_ref[pl.ds(pl.multiple_of(t * tk, tk), tk), :]
        for a, (start, n) in enumerate(zip(SEC_START, SEC_CHUNKS)):
            @pl.when((s >= start) & (s < start + n))
            def _(a=a):
                acc[...] += _dot(secs[a][...], h_rows, TN)

        @pl.when(t == nt - 1)
        def _():
            out_ref[...] = acc[...].astype(bf16)

    def sec_spec(a):
        def index(s, t, a=a):
            inside, local = _section_of_chunk(s)[a]
            return (jnp.where(inside, t, 0), local)
        return pl.BlockSpec((tk, CH), index)

    return _pcall(
        body, name="w_in_grad", grid=(nchunks, T // tk),
        in_specs=[sec_spec(a) for a in range(nsec)]
        + [pl.BlockSpec((T, D), lambda s, t: (0, 0), pipeline_mode=pl.Buffered(1))],
        out_specs=pl.BlockSpec((CH, D), lambda s, t: (s, 0)), out_shape=_sds((D_IN, D), bf16),
        scratch_shapes=[pltpu.VMEM((CH, D), f32)],
        compiler_params=_params(("arbitrary", "arbitrary")),
    )(*dsecs, h_bf)


SMALL_NAMES = ("lru_w_a", "lru_w_x", "conv_b", "lru_b_a", "lru_b_x", "lru_lambda", "norm_g", "final_norm_g",
               "attn_sinks", "conv_w")
MISC_ROW = {"conv_b": 0, "lru_b_a": 1, "lru_b_x": 2, "lru_lambda": 3, "norm_g": 8, "final_norm_g": 16,
            "attn_sinks": 24, "loss": 32}


def _small_step(gwa, gwx, gvec, gnorm_blk, gfin_blk, dsink_blk, loss_blk, gcw, params, after):
    srcs_rows = (RB // NDEV, RB // NDEV, 8, 8)
    flat = [t for n in SMALL_NAMES for t in params[n]]
    nout = 4 * len(SMALL_NAMES) + 1

    ra_, rx_, rm_, rc_ = srcs_rows
    slab_rows = ra_ + rx_ + rm_ + rc_
    keep_rows = ra_ + rx_ + rm_

    def reduce_body(gwa_ref, gwx_ref, gvec_ref, gnorm_ref, gfin_ref, dsink_ref, loss_ref, gcw_ref, after_ref,
                    all_a, all_x, all_m, conv_out, misc, outbox, inbox, mine, everyone, sa, ra, sb, rb):
        x, y, c = _my_place()
        me = 4 * x + 2 * y + c

        misc[...] = jnp.zeros_like(misc)
        misc[0:8, :] = gvec_ref[...]
        misc[8:16, :] = gnorm_ref[...]
        misc[16:24, :] = gfin_ref[...]
        misc[24:32, 0:128] = dsink_ref[...]
        misc[32:40, :] = loss_ref[...]

        outbox[...] = jnp.zeros_like(outbox)
        for d in range(NDEV):
            outbox[d, 0:ra_, :] = gwa_ref[ra_ * d:ra_ * (d + 1), :]
            outbox[d, ra_:ra_ + rx_, :] = gwx_ref[rx_ * d:rx_ * (d + 1), :]
            outbox[d, ra_ + rx_:keep_rows, :] = misc[rm_ * d:rm_ * (d + 1), :]
            outbox[d, keep_rows:slab_rows, 0:RB] = gcw_ref[rc_ * d:rc_ * (d + 1), :]

        scatter = []
        for k in range(1, NDEV):
            px, py, pc = _peer(k)
            scatter.append(pltpu.make_async_remote_copy(
                src_ref=outbox.at[4 * px + 2 * py + pc], dst_ref=inbox.at[k - 1],
                send_sem=sa.at[k - 1], recv_sem=ra.at[k - 1], device_id=(px, py, pc), device_id_type=MESH))
        for cp in scatter:
            cp.start()
        for cp in scatter:
            cp.wait()

        total = outbox[me]
        for k in range(NDEV - 1):
            total = total + inbox[k]
        conv_out[...] = total[keep_rows:slab_rows, 0:RB]
        mine[...] = total[0:keep_rows, :]
        everyone[me] = total[0:keep_rows, :]
        gather = [pltpu.make_async_remote_copy(
            src_ref=mine, dst_ref=everyone.at[me], send_sem=sb.at[k - 1], recv_sem=rb.at[k - 1],
            device_id=_peer(k), device_id_type=MESH) for k in range(1, NDEV)]
        for cp in gather:
            cp.start()
        for cp in gather:
            cp.wait()
        for d in range(NDEV):
            all_a[ra_ * d:ra_ * (d + 1), :] = everyone[d, 0:ra_, :]
            all_x[rx_ * d:rx_ * (d + 1), :] = everyone[d, ra_:ra_ + rx_, :]
            all_m[rm_ * d:rm_ * (d + 1), :] = everyone[d, ra_ + rx_:keep_rows, :]

    def adam_body(*refs):
        all_a, all_x, all_m, conv_ref = refs[:4]
        prm = {n: refs[4 + 3 * k:7 + 3 * k] for k, n in enumerate(SMALL_NAMES)}
        nin = 4 + len(flat)
        outs = {n: refs[nin + 4 * k:nin + 4 * k + 4] for k, n in enumerate(SMALL_NAMES)}
        loss_out = refs[nin + nout - 1]
        g_conv = conv_ref[0:4, :]

        def update(name, g, pick=lambda r: r[...]):
            w_ref, m_ref, v_ref = prm[name]
            delta, m_new, v_new = _adam_math(g, pick(w_ref), pick(m_ref), pick(v_ref))
            return g, delta, m_new, v_new

        for n in range(RNN_BLOCKS):
            lanes = slice(RB * n, RB * (n + 1))
            for name, full in (("lru_w_a", all_a), ("lru_w_x", all_x)):
                for out, val in zip(outs[name], update(name, full[:, lanes], pick=lambda r, n=n: r[n])):
                    out[n] = val
        for name in ("conv_b", "lru_b_a", "lru_b_x", "lru_lambda", "norm_g", "final_norm_g"):
            row = MISC_ROW[name]
            for out, val in zip(outs[name], update(name, all_m[row:row + 1, :])):
                out[...] = val
        row = MISC_ROW["attn_sinks"]
        for out, val in zip(outs["attn_sinks"], update("attn_sinks", all_m[row:row + 1, 0:16])):
            out[...] = val
        for out, val in zip(outs["conv_w"], update("conv_w", g_conv)):
            out[...] = val
        row = MISC_ROW["loss"]
        loss_out[...] = all_m[row:row + 8, 0:128] * (0.5 / D)

    scratch = [pltpu.VMEM((64, D), f32), pltpu.VMEM((NDEV, slab_rows, D), f32),
               pltpu.VMEM((NDEV - 1, slab_rows, D), f32), pltpu.VMEM((keep_rows, D), f32),
               pltpu.VMEM((NDEV, keep_rows, D), f32)] + [pltpu.SemaphoreType.DMA((NDEV - 1,))] * 4
    sums = _pcall(
        reduce_body, name="small_reduce",
        out_shape=(_sds((RB, D), f32), _sds((RB, D), f32), _sds((64, D), f32), _sds((8, RB), f32)),
        in_specs=[VMEM_SPEC] * 8 + [ANY], out_specs=tuple([VMEM_SPEC] * 4),
        scratch_shapes=scratch, compiler_params=_params(),
    )(gwa, gwx, gvec, gnorm_blk, gfin_blk, dsink_blk, loss_blk, gcw, after)
    out_shape = tuple(_sds(params[n][0].shape, f32) for n in SMALL_NAMES for _ in range(4)) + (_sds((8, 128), f32),)
    res = _pcall(
        adam_body, name="small_adamw", out_shape=out_shape,
        in_specs=[VMEM_SPEC] * (4 + len(flat)), out_specs=tuple([VMEM_SPEC] * nout), compiler_params=_params(),
    )(*sums, *flat)
    return {n: res[4 * k:4 * k + 4] for k, n in enumerate(SMALL_NAMES)}, res[-1]


def _pad_rows(v, rows=8):
    return jnp.concatenate([v, jnp.zeros((rows - v.shape[0], v.shape[1]), v.dtype)], axis=0)


def kernel(x, norm_g, w_in, conv_w, conv_b, lru_w_a, lru_b_a, lru_w_x, lru_b_x, lru_lambda, attn_sinks, w_rnn_out, w_attn_out, w_o, final_norm_g, loss_target, m_norm_g, m_w_in, m_conv_w, m_conv_b, m_lru_w_a, m_lru_b_a, m_lru_w_x, m_lru_b_x, m_lru_lambda, m_attn_sinks, m_w_rnn_out, m_w_attn_out, m_w_o, m_final_norm_g, v_norm_g, v_w_in, v_conv_w, v_conv_b, v_lru_w_a, v_lru_b_a, v_lru_w_x, v_lru_b_x, v_lru_lambda, v_attn_sinks, v_w_rnn_out, v_w_attn_out, v_w_o, v_final_norm_g):
    nb, S, _ = x.shape
    T = nb * S
    x2d = x.reshape(T, D)
    tgt = loss_target.reshape(T, D)
    fin_g = final_norm_g.reshape(1, D)
    w_a3, w_x3 = lru_w_a[0], lru_w_x[0]

    my_core = lax.axis_index("c").astype(jnp.int32).reshape(1)
    cx, cy = lax.axis_index("x"), lax.axis_index("y")
    chip_order = jnp.stack([2 * cx + cy, 2 * (1 - cx) + cy, 2 * cx + (1 - cy),
                            2 * (1 - cx) + (1 - cy)]).astype(jnp.int32)

    tabs = _rope_tables(S)
    h_bf, proj, wt_full, cw_full, _ = _in_proj_gather(
        x2d, norm_g, w_in[0].T.astype(bf16), _pad_rows(conv_w[0]), tabs, S, (), chip_order)
    y_rnn, h_all = _lru_forward(proj, cw_full, conv_b, w_a3, lru_b_a, w_x3, lru_b_x, lru_lambda, S)
    y_attn, (wr_full, wa_full, wo_full) = _attn_forward(proj, attn_sinks, S,
                                                        (w_rnn_out[0], w_attn_out[0], w_o[0]))

    (dx2, dy_rnn, dy_attn, dmr, dma, loss_blk, gfin_blk, g_wr, g_wa, g_wo) = _merge_and_head(
        x2d, tgt, proj, y_rnn, y_attn, wr_full, wa_full, wo_full, fin_g)
    sums_out = _pair_sums([g_wr, g_wa, g_wo], bf16, my_core, "out")

    (dq, dkv, dga, dsink_blk), (p_wr, p_wa, p_wo) = _attn_backward(proj, dy_attn, tabs, attn_sinks, S, sums_out)
    du0, dgr, gwa, gwx, gvec, gcw = _lru_backward(proj, h_all, dy_rnn, cw_full, conv_b, w_a3, lru_b_a, w_x3,
                                                  lru_b_x, lru_lambda, S)
    dsecs = (du0, dgr, dq, dkv, dga, dmr, dma)

    g_wt = _w_in_grad(dsecs, h_bf)
    (sum_in,) = _pair_sums([g_wt], bf16, my_core, "in")
    ex_sems, sum_in, landing, token = _exchange_start(sum_in)
    grad_x2d, gnorm_blk = _input_grad(dsecs, wt_full, x2d, dx2, norm_g + token[0, 0])
    o_wr, o_wa, o_wo = _adamw_group(
        (p_wr, p_wa, p_wo), (w_rnn_out[0], w_attn_out[0], w_o[0]),
        (m_w_rnn_out[0], m_w_attn_out[0], m_w_o[0]), (v_w_rnn_out[0], v_w_attn_out[0], v_w_o[0]), "adamw_w_out")
    p_wt = _exchange_wait(ex_sems, sum_in, landing, (gnorm_blk, o_wr[0]))
    p_wt_own = lax.dynamic_index_in_dim(sum_in, 2 * cx + cy, axis=0, keepdims=False)
    o_wt = _adamw(p_wt_own, p_wt, w_in[0].T, m_w_in[0].T, v_w_in[0].T, "adamw_w_in")

    small, loss_out = _small_step(gwa, gwx, gvec, gnorm_blk, gfin_blk, dsink_blk, loss_blk, gcw, {
        "lru_w_a": (w_a3, m_lru_w_a[0], v_lru_w_a[0]), "lru_w_x": (w_x3, m_lru_w_x[0], v_lru_w_x[0]),
        "conv_b": (conv_b, m_conv_b, v_conv_b), "lru_b_a": (lru_b_a, m_lru_b_a, v_lru_b_a),
        "lru_b_x": (lru_b_x, m_lru_b_x, v_lru_b_x), "lru_lambda": (lru_lambda, m_lru_lambda, v_lru_lambda),
        "norm_g": (norm_g, m_norm_g, v_norm_g),
        "final_norm_g": (fin_g, m_final_norm_g.reshape(1, D), v_final_norm_g.reshape(1, D)),
        "attn_sinks": (attn_sinks, m_attn_sinks, v_attn_sinks),
        "conv_w": (conv_w[0], m_conv_w[0], v_conv_w[0])}, o_wt[0])

    def result(kind):
        d = {n: small[n][kind] for n in ("conv_b", "lru_b_a", "lru_b_x", "lru_lambda", "norm_g", "attn_sinks")}
        d.update({n: small[n][kind][None] for n in ("lru_w_a", "lru_w_x", "conv_w")})
        d["final_norm_g"] = small["final_norm_g"][kind].reshape(D)
        d.update({"w_in": o_wt[kind].T[None], "w_rnn_out": o_wr[kind][None], "w_attn_out": o_wa[kind][None],
                  "w_o": o_wo[kind][None]})
        return d

    order = ("norm_g", "w_in", "conv_w", "conv_b", "lru_w_a", "lru_b_a", "lru_w_x", "lru_b_x", "lru_lambda",
             "attn_sinks", "w_rnn_out", "w_attn_out", "w_o", "final_norm_g")
    outs = [loss_out[0, 0], grad_x2d.reshape(nb, S, D)]
    for kind in range(4):
        d = result(kind)
        outs += [d[n] for n in order]
    return tuple(outs)
```

```python
import functools
import math

import jax
import jax.numpy as jnp
from jax import lax
from jax.experimental import pallas as pl
from jax.experimental.pallas import tpu as pltpu

f32 = jnp.float32
bf16 = jnp.bfloat16

D = 1024
D_IN = 6656
NDEV = 8
RNN_BLOCKS = 8
RB = 128
HEAD = 64
KV_HEADS = 4
GROUP = 4
QB = 128
LRU_C = 8.0
EPS = 1e-6
ROPE_DIM = 16
ROPE_THETA = 500000.0
CH = 512
SEC_START = (0, 2, 4, 6, 7, 9, 11)
SEC_CHUNKS = (2, 2, 2, 1, 2, 2, 2)
VMEM_LIMIT = 62 * 1024 * 1024

ADAM_LR, ADAM_B1, ADAM_B2, ADAM_EPS, ADAM_WD, ADAM_STEP = 0.001, 0.9, 0.999, 1e-08, 0.01, 10

MESH = pl.DeviceIdType.MESH
ANY = pl.BlockSpec(memory_space=pl.ANY)
VMEM_SPEC = pl.BlockSpec(memory_space=pltpu.VMEM)
SMEM_SPEC = pl.BlockSpec(memory_space=pltpu.SMEM)


def _pcall(body, **kw):
    return pl.pallas_call(body, **kw)


def _params(sem=None, **kw):
    if sem is not None:
        kw["dimension_semantics"] = sem
    return pltpu.CompilerParams(vmem_limit_bytes=VMEM_LIMIT, **kw)


def _sds(shape, dtype):
    return jax.ShapeDtypeStruct(shape, dtype)


def _dot(a, b, dims):
    return lax.dot_general(a, b, (dims, ((), ())), preferred_element_type=f32)


NN = ((1,), (0,))
NT = ((1,), (1,))
TN = ((0,), (0,))


def _sigmoid(v):
    return 0.5 * jnp.tanh(0.5 * v) + 0.5


def _sigmoid_positive(v):
    return 1.0 / (1.0 + jnp.exp(-v))


def _my_place():
    return lax.axis_index("x"), lax.axis_index("y"), lax.axis_index("c")


def _peer(k):
    x, y, c = _my_place()
    return (x + ((k >> 2) & 1)) % 2, (y + ((k >> 1) & 1)) % 2, (c + (k & 1)) % 2


def _direct_gather_copies(srcs, outs, send_sems, recv_sems, local_sems):
    x, y, c = _my_place()
    me = 4 * x + 2 * y + c
    local, remote = [], []
    for a, (src, out) in enumerate(zip(srcs, outs)):
        r = src.shape[0]
        mine = out.at[pl.ds(pl.multiple_of(me * r, 8), r), :]
        local.append(pltpu.make_async_copy(src, mine, local_sems.at[a]))
        for k in range(1, NDEV):
            remote.append(pltpu.make_async_remote_copy(
                src_ref=src, dst_ref=mine, send_sem=send_sems.at[7 * a + k - 1], recv_sem=recv_sems.at[7 * a + k - 1],
                device_id=_peer(k), device_id_type=MESH))
    return local, remote


def _two_level_gather(srcs, outs, send_sems, recv_sems, local_sems):
    x, y, c = _my_place()
    me, sibling = (x, y, c), (x, y, 1 - c)
    chips = [(1 - x, y), (x, 1 - y), (1 - x, 1 - y)]
    narr = len(srcs)

    def rows(a, place):
        px, py, pc = place
        r = srcs[a].shape[0]
        return outs[a].at[pl.ds(pl.multiple_of((4 * px + 2 * py + pc) * r, 8), r), :]

    def copy(a, k, block, to, from_src=False):
        return pltpu.make_async_remote_copy(
            src_ref=srcs[a] if from_src else rows(a, block), dst_ref=rows(a, block),
            send_sem=send_sems.at[7 * a + k], recv_sem=recv_sems.at[7 * a + k], device_id=to, device_id_type=MESH)

    def mine(a):
        return pltpu.make_async_copy(srcs[a], rows(a, me), local_sems.at[a])

    def start():
        for a in range(narr):
            mine(a).start()
            copy(a, 0, me, sibling, True).start()
            for j, chip in enumerate(chips):
                copy(a, 1 + j, me, (*chip, c), True).start()

    def forward():
        for j, chip in enumerate(chips):
            for a in range(narr):
                copy(a, 1 + j, (*chip, c), me).wait_recv()
                copy(a, 4 + j, (*chip, c), sibling).start()

    def finish():
        for a in range(narr):
            copy(a, 0, sibling, me).wait_recv()
            for j, chip in enumerate(chips):
                copy(a, 4 + j, (*chip, 1 - c), me).wait_recv()
            copy(a, 0, me, sibling, True).wait_send()
            for j, chip in enumerate(chips):
                copy(a, 1 + j, me, (*chip, c), True).wait_send()
                copy(a, 4 + j, (*chip, c), sibling).wait_send()
            mine(a).wait()

    return start, forward, finish


def _chip_exchange_copies(src, dst, send_sems, recv_sems, local_sems):
    x, y, c = _my_place()
    local, remote = [], []
    for a in range(len(src)):
        local.append(pltpu.make_async_copy(src[a].at[2 * x + y], dst[a].at[0], local_sems.at[a]))
    for k in (3, 1, 2):
        px, py = (x + (k >> 1)) % 2, (y + (k & 1)) % 2
        for a in range(len(src)):
            remote.append(pltpu.make_async_remote_copy(
                src_ref=src[a].at[2 * px + py], dst_ref=dst[a].at[k],
                send_sem=send_sems.at[3 * a + k - 1], recv_sem=recv_sems.at[3 * a + k - 1],
                device_id=(px, py, c), device_id_type=MESH))
    return local, remote


def _exchange_scratch(narr, per_array):
    return [pltpu.SemaphoreType.DMA((per_array * narr,)), pltpu.SemaphoreType.DMA((per_array * narr,)),
            pltpu.SemaphoreType.DMA((narr,))]


def _start_all(copies):
    local, remote = copies
    for cp in local + remote:
        cp.start()


def _wait_all(copies):
    local, remote = copies
    for cp in remote + local:
        cp.wait()


def _row_tile(rows, dtype):
    unit = 16 if dtype == bf16 else 8
    for cand in (256, 208, 128, 64, 40, 32, 16, 8):
        if rows % cand == 0 and cand % unit == 0:
            return cand
    return rows


def _pair_sums(grads, wire_dtype, my_core, tag):
    narr = len(grads)
    r, cols = grads[0].shape[0] // NDEV, grads[0].shape[1]
    views = [g.reshape(4, 2, r, cols) for g in grads]
    tr = _row_tile(r, wire_dtype)
    nt = r // tr

    def body(core_ref, *refs):
        mine = refs[:narr]
        whole = refs[narr:2 * narr]
        outs = refs[2 * narr:3 * narr]
        got = refs[3 * narr:4 * narr]
        send_sems, recv_sems = refs[4 * narr:]
        q, i = pl.program_id(0), pl.program_id(1)
        x, y, c = _my_place()

        def copy(a, chip):
            return pltpu.make_async_remote_copy(
                src_ref=whole[a].at[chip, 1 - c], dst_ref=got[a].at[chip],
                send_sem=send_sems.at[4 * a + chip], recv_sem=recv_sems.at[4 * a + chip],
                device_id=(x, y, 1 - c), device_id_type=MESH)

        @pl.when((q == 0) & (i == 0))
        def _():
            for chip in range(4):
                for a in range(narr):
                    copy(a, chip).start()

        for chip in range(4):
            @pl.when((q == chip) & (i == 0))
            def _(chip=chip):
                for a in range(narr):
                    copy(a, chip).wait_recv()

        rows = pl.ds(pl.multiple_of(i * tr, tr), tr)
        for a in range(narr):
            outs[a][...] = (mine[a][...].astype(f32) + got[a][q, rows, :].astype(f32)).astype(wire_dtype)

        @pl.when((q == 3) & (i == nt - 1))
        def _():
            for chip in range(4):
                for a in range(narr):
                    copy(a, chip).wait_send()

    slab = pl.BlockSpec((None, tr, cols), lambda q, i, core: (q, i, 0))
    grid_spec = pltpu.PrefetchScalarGridSpec(
        num_scalar_prefetch=1, grid=(4, nt),
        in_specs=[pl.BlockSpec((None, None, tr, cols), lambda q, i, core: (q, core[0], i, 0))] * narr + [ANY] * narr,
        out_specs=tuple([slab] * narr),
        scratch_shapes=[pltpu.VMEM((4, r, cols), grads[0].dtype)] * narr
        + [pltpu.SemaphoreType.DMA((4 * narr,)), pltpu.SemaphoreType.DMA((4 * narr,))])
    return _pcall(body, name="pair_sums_" + tag, grid_spec=grid_spec,
                  out_shape=tuple(_sds((4, r, cols), wire_dtype) for _ in range(narr)),
                  compiler_params=_params(("arbitrary", "arbitrary")))(my_core, *views, *views)


def _adam_math(g, w, m, v):
    m_new = ADAM_B1 * m + (1.0 - ADAM_B1) * g
    v_new = ADAM_B2 * v + (1.0 - ADAM_B2) * (g * g)
    m_hat = m_new / (1.0 - ADAM_B1 ** ADAM_STEP)
    v_hat = v_new / (1.0 - ADAM_B2 ** ADAM_STEP)
    return -ADAM_LR * (m_hat / (jnp.sqrt(v_hat) + ADAM_EPS) + ADAM_WD * w), m_new, v_new


def _adamw(first, parts, w, m, v, name):
    n, rows, cols = parts.shape
    tr = _row_tile(rows, parts.dtype)

    def body(f_ref, p_ref, w_ref, m_ref, v_ref, g_out, d_out, m_out, v_out):
        g = f_ref[...].astype(f32)
        for s in range(n):
            g = g + p_ref[s].astype(f32)
        g_out[...] = g
        d_out[...], m_out[...], v_out[...] = _adam_math(g, w_ref[...], m_ref[...], v_ref[...])

    blk = pl.BlockSpec((tr, cols), lambda i: (i, 0))
    return _pcall(
        body, name=name, grid=(rows // tr,),
        in_specs=[blk, pl.BlockSpec((n, tr, cols), lambda i: (0, i, 0)), blk, blk, blk],
        out_specs=(blk, blk, blk, blk), out_shape=tuple(_sds((rows, cols), f32) for _ in range(4)),
        compiler_params=_params(("arbitrary",)),
    )(first, parts, w, m, v)


def _adamw_group(parts, ws, ms, vs, name):
    nw = len(ws)

    def body(*refs):
        p_refs, w_refs, m_refs, v_refs = (refs[k * nw:(k + 1) * nw] for k in range(4))
        outs = refs[4 * nw:]
        for k in range(nw):
            g = p_refs[k][0].astype(f32)
            for s in range(1, p_refs[k].shape[0]):
                g = g + p_refs[k][s].astype(f32)
            g_out, d_out, m_out, v_out = outs[4 * k:4 * k + 4]
            g_out[...] = g
            d_out[...], m_out[...], v_out[...] = _adam_math(g, w_refs[k][...], m_refs[k][...], v_refs[k][...])

    res = _pcall(
        body, name=name, out_shape=tuple(_sds(w.shape, f32) for w in ws for _ in range(4)),
        in_specs=[VMEM_SPEC] * (4 * nw), out_specs=tuple([VMEM_SPEC] * (4 * nw)), compiler_params=_params(),
    )(*parts, *ws, *ms, *vs)
    return [res[4 * k:4 * k + 4] for k in range(nw)]


def _rope(t, c, s1, s2):
    w = t.shape[1]
    return t * c + pltpu.roll(t, w - 8, 1) * s1 + pltpu.roll(t, 8, 1) * s2


def _rope_transposed(dt, c, s1, s2):
    w = dt.shape[1]
    return dt * c + pltpu.roll(dt * s1, 8, 1) + pltpu.roll(dt * s2, w - 8, 1)


PAIR_ROWS = D_IN // 4
SUB_COLS = ((0, 512), (512, 512), (1024, 512), (1536, 128))
Q_SLABS = range(3, 11)
K_SLABS = range(11, 13)


def _in_proj_gather(x2d, norm_g, wt_shard, cw_shard, tabs, S, out_shards, chip_order):
    T = x2d.shape[0]
    tb = min(S, 1024)
    ntok = T // tb
    nsb = S // tb
    q_scale = 1.0 / math.sqrt(HEAD)
    shard_rows = wt_shard.shape[0]
    small = (cw_shard,) + tuple(out_shards)
    nsm = len(small)

    def body(order_ref, x_ref, g_ref, c_ref, s1_ref, s2_ref, wt_hbm, *rest):
        small_in = rest[:nsm]
        h_ref, proj_ref, wt_out = rest[nsm:nsm + 3]
        small_out = rest[nsm + 3:2 * nsm + 3]
        wt_vm, h_vm = rest[2 * nsm + 3:2 * nsm + 5]
        stage = rest[2 * nsm + 5:3 * nsm + 4]
        wsend, wrecv, wlocal = rest[3 * nsm + 4:3 * nsm + 7]
        dsems = rest[3 * nsm + 7:]
        jj, i = pl.program_id(0), pl.program_id(1)
        x, y, c = _my_place()
        me, sibling = (x, y, c), (x, y, 1 - c)
        chips = [(1 - x, y), (x, 1 - y), (1 - x, 1 - y)]

        def rows(place):
            px, py, pc = place
            return wt_vm.at[pl.ds(pl.multiple_of((4 * px + 2 * py + pc) * shard_rows, 16), shard_rows), :]

        def copy(k, block, to, src=None):
            return pltpu.make_async_remote_copy(
                src_ref=rows(block) if src is None else src, dst_ref=rows(block),
                send_sem=wsend.at[k], recv_sem=wrecv.at[k], device_id=to, device_id_type=MESH)

        def small_copies():
            srcs = (small_in[0],) + tuple(stage)
            return _direct_gather_copies(srcs, small_out, *dsems)

        own = pltpu.make_async_copy(wt_hbm, rows(me), wlocal.at[0])
        keep = pltpu.make_async_copy(wt_vm, wt_out, wlocal.at[1])

        @pl.when((jj == 0) & (i == 0))
        def _():
            own.start()
            copy(0, me, sibling, src=wt_hbm).start()
            for j, chip in enumerate(chips):
                copy(1 + j, me, (*chip, c), src=wt_hbm).start()
            for a in range(nsm - 1):
                stage[a][...] = small_in[1 + a][...].astype(bf16)
            _start_all(small_copies())
            own.wait()
            copy(0, sibling, me).wait_recv()

        for j, chip in enumerate(chips):
            @pl.when((jj == 1 + j) & (i == 0))
            def _(j=j, chip=chip):
                copy(1 + j, (*chip, c), me).wait_recv()
                copy(4 + j, (*chip, c), sibling).start()
                copy(4 + j, (*chip, 1 - c), me).wait_recv()

        @pl.when((jj == 3) & (i == 0))
        def _():
            keep.start()

        @pl.when((jj == 3) & (i == ntok - 1))
        def _():
            copy(0, me, sibling, src=wt_hbm).wait_send()
            for j, chip in enumerate(chips):
                copy(1 + j, me, (*chip, c), src=wt_hbm).wait_send()
                copy(4 + j, (*chip, c), sibling).wait_send()
            _wait_all(small_copies())
            keep.wait()

        tok = pl.ds(pl.multiple_of(i * tb, tb), tb)

        @pl.when(jj == 0)
        def _():
            xv = x_ref[...]
            ms = jnp.mean(xv * xv, axis=-1, keepdims=True)
            hb = (xv * lax.rsqrt(ms + EPS) * g_ref[...]).astype(bf16)
            h_ref[...] = hb
            h_vm[tok, :] = hb

        block = order_ref[jj]
        hb = h_vm[tok, :]

        def piece(c0, w):
            w_rows = wt_vm[pl.ds(pl.multiple_of(block * PAIR_ROWS + c0, 128), w), :]
            return _dot(hb, w_rows, NT)

        @pl.when(block != 1)
        def _():
            for c0, w in SUB_COLS:
                proj_ref[:, c0:c0 + w] = piece(c0, w).astype(bf16)

        @pl.when(block == 1)
        def _():
            tab = (c_ref[...], s1_ref[...], s2_ref[...])
            for c0, w in SUB_COLS:
                acc = piece(c0, w)
                for l in range(w // 128):
                    slab = (c0 + 128 * l) // 128
                    part = acc[:, 128 * l:128 * (l + 1)]
                    if slab in Q_SLABS:
                        part = _rope(part, *tab) * q_scale
                    elif slab in K_SLABS:
                        part = _rope(part, *tab)
                    proj_ref[:, 128 * slab:128 * (slab + 1)] = part.astype(bf16)

    first_pass = lambda jj, i, order: (jnp.where(jj == 0, i, ntok - 1), 0)
    const = lambda jj, i, order: (0, 0)
    tab = pl.BlockSpec((tb, 128), lambda jj, i, order: (jnp.where(order[jj] == 1, i % nsb, 0), 0))
    grid_spec = pltpu.PrefetchScalarGridSpec(
        num_scalar_prefetch=1, grid=(4, ntok),
        in_specs=[pl.BlockSpec((tb, D), first_pass), pl.BlockSpec((1, D), const), tab, tab, tab, ANY]
        + [pl.BlockSpec(w.shape, const) for w in small],
        out_specs=(pl.BlockSpec((tb, D), first_pass),
                   pl.BlockSpec((tb, PAIR_ROWS), lambda jj, i, order: (i, order[jj])), ANY) + tuple([ANY] * nsm),
        scratch_shapes=[pltpu.VMEM((D_IN, D), bf16), pltpu.VMEM((T, D), bf16)]
        + [pltpu.VMEM(w.shape, bf16) for w in out_shards]
        + [pltpu.SemaphoreType.DMA((7,)), pltpu.SemaphoreType.DMA((7,)), pltpu.SemaphoreType.DMA((2,))]
        + _exchange_scratch(nsm, 7))
    res = _pcall(
        body, name="in_proj", grid_spec=grid_spec,
        out_shape=(_sds((T, D), bf16), _sds((T, D_IN), bf16), _sds((D_IN, D), bf16),
                   _sds((NDEV * cw_shard.shape[0], cw_shard.shape[1]), f32))
        + tuple(_sds((NDEV * w.shape[0], w.shape[1]), bf16) for w in out_shards),
        compiler_params=_params(("arbitrary", "arbitrary")),
    )(chip_order, x2d, norm_g, *tabs, wt_shard, *small)
    return res[0], res[1], res[2], res[3], res[4:]


def _rows_iota(shape):
    return lax.broadcasted_iota(jnp.int32, shape, 0)


def _shift_down(v, k):
    return jnp.where(_rows_iota(v.shape) >= k, pltpu.roll(v, k, 0), 0.0)


def _shift_up(v, k):
    n = v.shape[0]
    return jnp.where(_rows_iota(v.shape) < n - k, pltpu.roll(v, n - k, 0), 0.0)


def _linear_scan(a, b, a_s, b_s, edge_s, out_ref, reverse):
    n = a.shape[0]
    ng = n // 8
    a3, b3 = a.reshape(ng, 8, RB), b.reshape(ng, 8, RB)
    rid = lax.broadcasted_iota(jnp.int32, a3.shape, 1)
    for s in (1, 2, 4):
        keep, shift = (rid < 8 - s, 8 - s) if reverse else (rid >= s, s)
        b3 = jnp.where(keep, a3 * pltpu.roll(b3, shift, 1) + b3, b3)
        a3 = jnp.where(keep, a3 * pltpu.roll(a3, shift, 1), a3)
    a_s[...] = a3.reshape(n, RB)
    b_s[...] = b3.reshape(n, RB)
    edge = 0 if reverse else 7
    ea, eb = a_s[pl.ds(edge, ng, stride=8), :], b_s[pl.ds(edge, ng, stride=8), :]
    r = _rows_iota(ea.shape)
    s = 1
    while s < ng:
        keep, shift = (r < ng - s, ng - s) if reverse else (r >= s, s)
        eb = jnp.where(keep, ea * pltpu.roll(eb, shift, 0) + eb, eb)
        if 2 * s < ng:
            ea = jnp.where(keep, ea * pltpu.roll(ea, shift, 0), ea)
        s *= 2
    edge_s[...] = _shift_up(eb, 1) if reverse else _shift_down(eb, 1)

    def eight_groups(i, carry):
        for k in range(8):
            j = i * 8 + k
            rows = pl.ds(pl.multiple_of(j * 8, 8), 8)
            out_ref[rows, :] = b_s[rows, :] + a_s[rows, :] * edge_s[pl.ds(j, 1), :]
        return carry

    lax.fori_loop(0, ng // 8, eight_groups, 0)


def _neg_expm1(v):
    series = -v * (1.0 + v * (0.5 + v * (1.0 / 6.0)))
    return jnp.where(v > -0.015625, series, 1.0 - jnp.exp(v))


def _softplus_neg(lam):
    return jnp.maximum(-lam, 0.0) + jnp.log(1.0 + jnp.exp(-jnp.abs(lam)))


def _lru_gates(x0, cw, cb, wa, ba, wx, bx, lam):
    taps = [_shift_down(x0, 3 - k) for k in range(3)] + [x0]
    u = cb + cw[3:4, :] * x0
    for k in range(3):
        u = u + cw[k:k + 1, :] * taps[k]
    ub = u.astype(bf16)
    r = _sigmoid_positive(_dot(ub, wa.astype(bf16), NN) + ba)
    i = _sigmoid(_dot(ub, wx.astype(bf16), NN) + bx)
    sp = _softplus_neg(lam)
    log_a = (-LRU_C) * r * sp
    a = jnp.exp(log_a)
    w = _neg_expm1(2.0 * log_a)
    inv_mult = lax.rsqrt(w)
    return u, ub, r, i, sp, a, w * inv_mult, inv_mult, taps


def _lru_specs(S, nb):
    col = lambda off: pl.BlockSpec((S, RB), lambda n, b, off=off: (b, off + n))
    vec = pl.BlockSpec((1, RB), lambda n, b: (0, n))
    wblk = pl.BlockSpec((None, RB, RB), lambda n, b: (n, 0, 0))
    cwblk = pl.BlockSpec((8, RB), lambda n, b: (n, 0))
    return col, vec, wblk, cwblk


def _lru_forward(proj, cw_full, conv_b, w_a, b_a, w_x, b_x, lam, S):
    T = proj.shape[0]
    nb = T // S
    col, vec, wblk, cwblk = _lru_specs(S, nb)

    def body(x0_ref, g_ref, cw_ref, cb_ref, wa_ref, ba_ref, wx_ref, bx_ref, lam_ref, y_ref, h_ref, a_s, b_s, edge_s):
        x0 = x0_ref[...].astype(f32)
        u, ub, r, i, sp, a, mult, _, _ = _lru_gates(x0, cw_ref[...], cb_ref[...], wa_ref[...], ba_ref[...],
                                                    wx_ref[...], bx_ref[...], lam_ref[...])
        _linear_scan(a, mult * (i * u), a_s, b_s, edge_s, h_ref, reverse=False)
        g = g_ref[...].astype(f32)
        y_ref[...] = (h_ref[...] * (g * _sigmoid(g))).astype(bf16)

    out = pl.BlockSpec((S, RB), lambda n, b: (b, n))
    return _pcall(
        body, name="lru_forward", grid=(RNN_BLOCKS, nb),
        in_specs=[col(0), col(8), cwblk, vec, wblk, vec, wblk, vec, vec],
        out_specs=(out, out), out_shape=(_sds((T, D), bf16), _sds((T, D), f32)),
        scratch_shapes=[pltpu.VMEM((S, RB), f32), pltpu.VMEM((S, RB), f32), pltpu.VMEM((S // 8, RB), f32)],
        compiler_params=_params(("arbitrary", "arbitrary")),
    )(proj, proj, cw_full, conv_b, w_a, b_a, w_x, b_x, lam)


def _rope_tables(S):
    pos = jnp.arange(S, dtype=f32)
    inv_freq = ROPE_THETA ** (-jnp.arange(0, ROPE_DIM, 2, dtype=f32) / ROPE_DIM)
    ang = pos[:, None] * inv_freq[None, :]
    cos, sin = jnp.cos(ang), jnp.sin(ang)
    lane = jnp.arange(128) % HEAD
    cosl, sinl = cos[:, lane % 8], sin[:, lane % 8]
    c = jnp.where(lane[None, :] < ROPE_DIM, cosl, 1.0)
    s1 = jnp.where(lane[None, :] < 8, -sinl, 0.0)
    s2 = jnp.where((lane[None, :] >= 8) & (lane[None, :] < ROPE_DIM), sinl, 0.0)
    return c.astype(f32), s1.astype(f32), s2.astype(f32)


def _heads_to_rows(t):
    return jnp.concatenate([t[:, HEAD * h:HEAD * (h + 1)] for h in range(GROUP)], axis=0)


def _rows_to_heads(t):
    return jnp.concatenate([t[QB * h:QB * (h + 1), :] for h in range(GROUP)], axis=1)


def _window_bias(first_block):
    shape = (GROUP * QB, 2 * QB)
    qi = _rows_iota(shape) % QB
    cj = lax.broadcasted_iota(jnp.int32, shape, 1)
    valid = (cj > qi) & (cj <= qi + QB) & ((cj >= QB) | jnp.logical_not(first_block))
    return jnp.where(valid, 0.0, -jnp.inf)


def _attn_probs(q_rows, k_cat, sink_col, bias):
    s = _dot(q_rows, k_cat, NT) + bias
    m = jnp.maximum(jnp.max(s, axis=1, keepdims=True), sink_col)
    p = jnp.exp(s - m)
    e_sink = jnp.exp(sink_col - m)
    inv = 1.0 / (jnp.sum(p, axis=1, keepdims=True) + e_sink)
    return p * inv, e_sink * inv


def _sink_column(sink_ref, kv):
    rid = _rows_iota((GROUP * QB, 1))
    col = jnp.zeros((GROUP * QB, 1), f32)
    for h in range(GROUP):
        col = jnp.where(rid // QB == h, sink_ref[0, GROUP * kv + h], col)
    return col


def _attn_in_specs(S):
    nq = S // QB
    last = nq - 1
    cur = lambda b, j: b * nq + jnp.minimum(j, last)
    prev = lambda b, j: b * nq + jnp.maximum(jnp.minimum(j, last) - 1, 0)
    specs = [
        pl.BlockSpec((QB, D), lambda b, j: (cur(b, j), 2)),
        pl.BlockSpec((QB, 256), lambda b, j: (cur(b, j), 12)),
        pl.BlockSpec((QB, 256), lambda b, j: (prev(b, j), 12)),
        pl.BlockSpec((QB, 256), lambda b, j: (cur(b, j), 13)),
        pl.BlockSpec((QB, 256), lambda b, j: (prev(b, j), 13)),
        pl.BlockSpec((QB, 512), lambda b, j: (cur(b, j), 7)),
        pl.BlockSpec((QB, 512), lambda b, j: (cur(b, j), 8)),
        SMEM_SPEC,
    ]
    return specs, cur, prev


def _attn_forward(proj, sinks, S, out_shards):
    T = proj.shape[0]
    nb, nq = T // S, S // QB
    specs, cur, _ = _attn_in_specs(S)
    nw = len(out_shards)

    def body(q_ref, kc_ref, kp_ref, vc_ref, vp_ref, gl_ref, gh_ref, sink_ref, *rest):
        shards = rest[:nw]
        y_ref = rest[nw]
        gathered = rest[nw + 1:2 * nw + 1]
        stage = rest[2 * nw + 1:3 * nw + 1]
        sems = rest[3 * nw + 1:]
        b, j = pl.program_id(0), pl.program_id(1)

        start, forward, finish = _two_level_gather(stage, gathered, *sems)

        @pl.when((b == 0) & (j == 0))
        def _():
            for a in range(nw):
                stage[a][...] = shards[a][...].astype(bf16)
            start()

        @pl.when((b == nb - 1) & (j == nq // 2))
        def _():
            forward()

        @pl.when((b == nb - 1) & (j == nq - 1))
        def _():
            finish()

        bias = _window_bias(j == 0)
        kc, kp, vc, vp = kc_ref[...], kp_ref[...], vc_ref[...], vp_ref[...]
        for kv in range(KV_HEADS):
            lanes = slice(256 * kv, 256 * (kv + 1))
            hl = slice(HEAD * kv, HEAD * (kv + 1))
            q_rows = _heads_to_rows(q_ref[:, lanes])
            k_cat = jnp.concatenate([kp[:, hl], kc[:, hl]], axis=0)
            v_cat = jnp.concatenate([vp[:, hl], vc[:, hl]], axis=0)
            probs, _ = _attn_probs(q_rows, k_cat, _sink_column(sink_ref, kv), bias)
            o = _rows_to_heads(_dot(probs.astype(bf16), v_cat, NN))
            g_src = gl_ref if kv < 2 else gh_ref
            g = g_src[:, 256 * (kv % 2):256 * (kv % 2 + 1)].astype(f32)
            y_ref[:, lanes] = (o * (g * _sigmoid(g))).astype(bf16)

    args = [proj] * 7 + [sinks] + list(out_shards)
    res = _pcall(
        body, name="attn_forward", grid=(nb, nq),
        in_specs=specs + [pl.BlockSpec(w.shape, lambda b, j: (0, 0)) for w in out_shards],
        out_specs=(pl.BlockSpec((QB, D), lambda b, j: (cur(b, j), 0)),) + tuple([ANY] * nw),
        out_shape=(_sds((T, D), bf16),) + tuple(_sds((NDEV * w.shape[0], w.shape[1]), bf16) for w in out_shards),
        scratch_shapes=[pltpu.VMEM(w.shape, bf16) for w in out_shards] + _exchange_scratch(nw, 7),
        compiler_params=_params(("arbitrary", "arbitrary")),
    )(*args)
    return res[0], res[1:]


def _merge_and_head(x2d, tgt, proj, y_rnn, y_attn, w_r, w_a, w_o, gfin):
    T = x2d.shape[0]
    tb = min(T, 512)
    nsteps = T // tb

    def body(x_ref, t_ref, mr0, mr1, ma0, ma1, yr_ref, ya_ref, wr_ref, wa_ref, wo_ref, gf_ref,
             dx2_ref, dyr_ref, dya_ref, dmr_ref, dma_ref, loss_ref, gfin_ref, gwr_out, gwa_out, gwo_out,
             gwr_acc, gwa_acc, gwo_acc, out_sems):
        step = pl.program_id(0)

        @pl.when(step == 0)
        def _():
            loss_ref[...] = jnp.zeros_like(loss_ref)
            gfin_ref[...] = jnp.zeros_like(gfin_ref)
            gwr_acc[...] = jnp.zeros_like(gwr_acc)
            gwa_acc[...] = jnp.zeros_like(gwa_acc)
            gwo_acc[...] = jnp.zeros_like(gwo_acc)

        sr = _sigmoid(jnp.concatenate([mr0[...], mr1[...]], axis=1).astype(f32))
        sa = _sigmoid(jnp.concatenate([ma0[...], ma1[...]], axis=1).astype(f32))
        p_r = _dot(yr_ref[...], wr_ref[...], NN)
        p_a = _dot(ya_ref[...], wa_ref[...], NN)
        merged = (sr * p_r + sa * p_a).astype(bf16)
        x2 = x_ref[...] + _dot(merged, wo_ref[...], NN)
        rstd = lax.rsqrt(jnp.mean(x2 * x2, axis=-1, keepdims=True) + EPS)
        xh = x2 * rstd
        gf = gf_ref[...]
        err = xh * gf - t_ref[...]
        loss_ref[...] += jnp.sum(err * err)
        dy = err * (1.0 / D)
        gfin_ref[0:1, :] += jnp.sum(dy * xh, axis=0, keepdims=True)
        dxn = dy * gf
        dx2 = rstd * (dxn - xh * jnp.mean(dxn * xh, axis=-1, keepdims=True))
        dx2_ref[...] = dx2
        dx2b = dx2.astype(bf16)
        dmerged = _dot(dx2b, wo_ref[...], NT)
        dmr_ref[...] = (dmerged * p_r * (sr * (1.0 - sr))).astype(bf16)
        dma_ref[...] = (dmerged * p_a * (sa * (1.0 - sa))).astype(bf16)
        dpr = (dmerged * sr).astype(bf16)
        dpa = (dmerged * sa).astype(bf16)
        dyr_ref[...] = _dot(dpr, wr_ref[...], NT).astype(bf16)
        dya_ref[...] = _dot(dpa, wa_ref[...], NT).astype(bf16)
        gwr_acc[...] += _dot(yr_ref[...], dpr, TN)
        gwa_acc[...] += _dot(ya_ref[...], dpa, TN)
        gwo_acc[...] += _dot(merged, dx2b, TN)

        @pl.when(step == nsteps - 1)
        def _():
            copies = [pltpu.make_async_copy(src, dst, out_sems.at[k]) for k, (src, dst) in enumerate(
                ((gwr_acc, gwr_out), (gwa_acc, gwa_out), (gwo_acc, gwo_out)))]
            for cp in copies:
                cp.start()
            for cp in copies:
                cp.wait()

    tok = pl.BlockSpec((tb, D), lambda i: (i, 0))
    half = lambda c: pl.BlockSpec((tb, CH), lambda i, c=c: (i, c))
    wfull = pl.BlockSpec((D, D), lambda i: (0, 0), pipeline_mode=pl.Buffered(1))
    acc = pl.BlockSpec((8, D), lambda i: (0, 0))
    return _pcall(
        body, name="merge_and_head", grid=(nsteps,),
        in_specs=[tok, tok, half(9), half(10), half(11), half(12), tok, tok, wfull, wfull, wfull,
                  pl.BlockSpec((1, D), lambda i: (0, 0))],
        out_specs=(tok, tok, tok, tok, tok, acc, acc, ANY, ANY, ANY),
        out_shape=(_sds((T, D), f32), _sds((T, D), bf16), _sds((T, D), bf16), _sds((T, D), bf16),
                   _sds((T, D), bf16), _sds((8, D), f32), _sds((8, D), f32),
                   _sds((D, D), f32), _sds((D, D), f32), _sds((D, D), f32)),
        scratch_shapes=[pltpu.VMEM((D, D), f32)] * 3 + [pltpu.SemaphoreType.DMA((3,))],
        compiler_params=_params(("arbitrary",)),
    )(x2d, tgt, proj, proj, proj, proj, y_rnn, y_attn, w_r, w_a, w_o, gfin)


def _attn_backward(proj, dy_attn, tabs, sinks, S, chip_sums):
    T = proj.shape[0]
    nb, nq = T // S, S // QB
    nex = len(chip_sums)
    specs, cur, prev = _attn_in_specs(S)
    last = nq - 1
    tab_cur = pl.BlockSpec((QB, 128), lambda b, j: (jnp.minimum(j, last), 0))
    tab_prev = pl.BlockSpec((QB, 128), lambda b, j: (jnp.maximum(jnp.minimum(j, last) - 1, 0), 0))
    specs = specs + [pl.BlockSpec((QB, D), lambda b, j: (cur(b, j), 0))] + [tab_cur] * 3 + [tab_prev] * 3
    q_scale = 1.0 / math.sqrt(HEAD)

    def rope_back(dt, tab):
        return jnp.concatenate([_rope_transposed(dt[:, 128 * l:128 * (l + 1)], *tab) for l in range(2)], axis=1)

    def body(q_ref, kc_ref, kp_ref, vc_ref, vp_ref, gl_ref, gh_ref, sink_ref, dy_ref, cc, s1c, s2c, cp, s1p, s2p,
             *rest):
        ex_src = rest[:nex]
        dq_ref, dkv_ref, dg_ref, dsink_ref = rest[nex:nex + 4]
        ex_dst = rest[nex + 4:2 * nex + 4]
        carry_k, carry_v = rest[2 * nex + 4:2 * nex + 6]
        sems = rest[2 * nex + 6:]
        b, j = pl.program_id(0), pl.program_id(1)

        @pl.when((b == 0) & (j == 0))
        def _():
            dsink_ref[...] = jnp.zeros_like(dsink_ref)
            _start_all(_chip_exchange_copies(ex_src, ex_dst, *sems))

        @pl.when((b == nb - 1) & (j == nq))
        def _():
            _wait_all(_chip_exchange_copies(ex_src, ex_dst, *sems))

        @pl.when(j == 0)
        def _():
            carry_k[...] = jnp.zeros_like(carry_k)
            carry_v[...] = jnp.zeros_like(carry_v)

        @pl.when(j < nq)
        def _():
            bias = _window_bias(j == 0)
            tc = (cc[...], s1c[...], s2c[...])
            tp = (cp[...], s1p[...], s2p[...])
            kc, kp, vc, vp = kc_ref[...], kp_ref[...], vc_ref[...], vp_ref[...]
            dk_prev, dk_cur, dv_prev, dv_cur = [], [], [], []
            dsink_acc = jnp.zeros((8, 128), f32)
            r8 = lax.broadcasted_iota(jnp.int32, (8, 128), 0)
            l8 = lax.broadcasted_iota(jnp.int32, (8, 128), 1)
            for kv in range(KV_HEADS):
                lanes = slice(256 * kv, 256 * (kv + 1))
                hl = slice(HEAD * kv, HEAD * (kv + 1))
                q_rows = _heads_to_rows(q_ref[:, lanes])
                k_cat = jnp.concatenate([kp[:, hl], kc[:, hl]], axis=0)
                v_cat = jnp.concatenate([vp[:, hl], vc[:, hl]], axis=0)
                probs, p_sink = _attn_probs(q_rows, k_cat, _sink_column(sink_ref, kv), bias)
                pb = probs.astype(bf16)
                o = _rows_to_heads(_dot(pb, v_cat, NN))
                g_src = gl_ref if kv < 2 else gh_ref
                g = g_src[:, 256 * (kv % 2):256 * (kv % 2 + 1)].astype(f32)
                sg = _sigmoid(g)
                dy = dy_ref[:, lanes].astype(f32)
                dg_ref[:, lanes] = (dy * o * (sg * (1.0 + g * (1.0 - sg)))).astype(bf16)
                do_rows = _heads_to_rows(dy * (g * sg)).astype(bf16)
                dv = _dot(pb, do_rows, TN)
                dp = _dot(do_rows, v_cat, NT)
                rowdot = jnp.sum(probs * dp, axis=1, keepdims=True)
                ds = (probs * (dp - rowdot)).astype(bf16)
                sink_rows = -(p_sink * rowdot)
                for h in range(GROUP):
                    val = jnp.sum(sink_rows[QB * h:QB * (h + 1), :])
                    dsink_acc = dsink_acc + jnp.where((r8 == 0) & (l8 == GROUP * kv + h), val, 0.0)
                dq = _rows_to_heads(_dot(ds, k_cat, NN)) * q_scale
                dq_ref[:, lanes] = rope_back(dq, tc).astype(bf16)
                dk = _dot(ds, q_rows, TN)
                dk_prev.append(dk[:QB, :])
                dk_cur.append(dk[QB:, :])
                dv_prev.append(dv[:QB, :])
                dv_cur.append(dv[QB:, :])
            dsink_ref[...] += dsink_acc
            dkp = rope_back(jnp.concatenate(dk_prev, axis=1), tp)
            dkc = rope_back(jnp.concatenate(dk_cur, axis=1), tc)
            dkv_ref[:, 0:256] = (carry_k[...] + dkp).astype(bf16)
            dkv_ref[:, 256:512] = (carry_v[...] + jnp.concatenate(dv_prev, axis=1)).astype(bf16)
            carry_k[...] = dkc
            carry_v[...] = jnp.concatenate(dv_cur, axis=1)

        @pl.when(j == nq)
        def _():
            dkv_ref[:, 0:256] = carry_k[...].astype(bf16)
            dkv_ref[:, 256:512] = carry_v[...].astype(bf16)

    lag = lambda b, j: (b * nq + jnp.maximum(j - 1, 0), 0)
    args = [proj] * 7 + [sinks, dy_attn] + list(tabs) + list(tabs) + list(chip_sums)
    res = _pcall(
        body, name="attn_backward", grid=(nb, nq + 1), in_specs=specs + [ANY] * nex,
        out_specs=(pl.BlockSpec((QB, D), lambda b, j: (cur(b, j), 0)), pl.BlockSpec((QB, 512), lag),
                   pl.BlockSpec((QB, D), lambda b, j: (cur(b, j), 0)), pl.BlockSpec((8, 128), lambda b, j: (0, 0)))
        + tuple([ANY] * nex),
        out_shape=(_sds((T, D), bf16), _sds((T, 512), bf16), _sds((T, D), bf16), _sds((8, 128), f32))
        + tuple(_sds(s.shape, s.dtype) for s in chip_sums),
        scratch_shapes=[pltpu.VMEM((QB, 256), f32), pltpu.VMEM((QB, 256), f32)] + _exchange_scratch(nex, 3),
        compiler_params=_params(("arbitrary", "arbitrary")),
    )(*args)
    return res[:4], res[4:]


def _lru_backward(proj, h_all, dy_rnn, cw_full, conv_b, w_a, b_a, w_x, b_x, lam, S):
    T = proj.shape[0]
    nb = T // S
    col, vec, wblk, cwblk = _lru_specs(S, nb)
    tokblk = pl.BlockSpec((S, RB), lambda n, b: (b, n))

    def body(x0_ref, g_ref, h_ref, dy_ref, cw_ref, cb_ref, wa_ref, ba_ref, wx_ref, bx_ref, lam_ref,
             du0_ref, dg_ref, gwa_ref, gwx_ref, vec_ref, gcw_ref, a_s, b_s, dh_s, edge_s):
        @pl.when(pl.program_id(1) == 0)
        def _():
            gwa_ref[...] = jnp.zeros_like(gwa_ref)
            gwx_ref[...] = jnp.zeros_like(gwx_ref)
            vec_ref[...] = jnp.zeros_like(vec_ref)
            gcw_ref[...] = jnp.zeros_like(gcw_ref)

        x0 = x0_ref[...].astype(f32)
        cw = cw_ref[...]
        lam_v = lam_ref[...]
        u, ub, r, i, sp, a, mult, inv_mult, taps = _lru_gates(x0, cw, cb_ref[...], wa_ref[...], ba_ref[...],
                                                              wx_ref[...], bx_ref[...], lam_v)
        h = h_ref[...]
        g = g_ref[...].astype(f32)
        dy = dy_ref[...].astype(f32)
        sg = _sigmoid(g)
        dg_ref[...] = (dy * h * (sg * (1.0 + g * (1.0 - sg)))).astype(bf16)
        _linear_scan(_shift_up(a, 1), dy * (g * sg), a_s, b_s, edge_s, dh_s, reverse=True)
        dh_total = dh_s[...]
        da = dh_total * _shift_down(h, 1)
        dmult = dh_total * (i * u)
        db = dh_total * mult
        di = db * u
        du = db * i
        dlog_a_c = ((-LRU_C) * a) * (da - dmult * (a * inv_mult))
        dr = dlog_a_c * sp
        dsp = jnp.sum(dlog_a_c * r, axis=0, keepdims=True)
        dpre_r = dr * r * (1.0 - r)
        dpre_i = di * i * (1.0 - i)
        dpre_rb = dpre_r.astype(bf16)
        dpre_ib = dpre_i.astype(bf16)
        du = du + _dot(dpre_rb, wa_ref[...].astype(bf16), NT) + _dot(dpre_ib, wx_ref[...].astype(bf16), NT)
        gwa_ref[...] += _dot(ub, dpre_rb, TN)
        gwx_ref[...] += _dot(ub, dpre_ib, TN)
        vec_ref[0:1, :] += jnp.sum(du, axis=0, keepdims=True)
        vec_ref[1:2, :] += jnp.sum(dpre_r, axis=0, keepdims=True)
        vec_ref[2:3, :] += jnp.sum(dpre_i, axis=0, keepdims=True)
        vec_ref[3:4, :] += dsp * (-_sigmoid(-lam_v))
        dx0 = cw[3:4, :] * du
        for k in range(3):
            dx0 = dx0 + cw[k:k + 1, :] * _shift_up(du, 3 - k)
        for k in range(4):
            gcw_ref[k:k + 1, :] += jnp.sum(du * taps[k], axis=0, keepdims=True)
        du0_ref[...] = dx0.astype(bf16)

    wacc = pl.BlockSpec((RB, RB), lambda n, b: (0, n))
    vacc = pl.BlockSpec((8, RB), lambda n, b: (0, n))
    cacc = pl.BlockSpec((8, RB), lambda n, b: (n, 0))
    return _pcall(
        body, name="lru_backward", grid=(RNN_BLOCKS, nb),
        in_specs=[col(0), col(8), tokblk, tokblk, cwblk, vec, wblk, vec, wblk, vec, vec],
        out_specs=(tokblk, tokblk, wacc, wacc, vacc, cacc),
        out_shape=(_sds((T, D), bf16), _sds((T, D), bf16), _sds((RB, D), f32), _sds((RB, D), f32),
                   _sds((8, D), f32), _sds((8 * RNN_BLOCKS, RB), f32)),
        scratch_shapes=[pltpu.VMEM((S, RB), f32)] * 3 + [pltpu.VMEM((S // 8, RB), f32)],
        compiler_params=_params(("arbitrary", "arbitrary")),
    )(proj, proj, h_all, dy_rnn, cw_full, conv_b, w_a, b_a, w_x, b_x, lam)


def _section_of_chunk(s):
    out = []
    for start, n in zip(SEC_START, SEC_CHUNKS):
        inside = (s >= start) & (s < start + n)
        out.append((inside, jnp.clip(s - start, 0, n - 1)))
    return out


EFFECT = pltpu.SideEffectType.DATAFLOW_SIDE_EFFECTING
HBM_SPEC = pl.BlockSpec(memory_space=pltpu.HBM)
SEM_SPEC = pl.BlockSpec(memory_space=pltpu.SEMAPHORE)


def _split_exchange_copies(src_ref, land_ref, send_sems, recv_sems):
    x, y, c = _my_place()
    copies = []
    for k in (3, 1, 2):
        px, py = (x + (k >> 1)) % 2, (y + (k & 1)) % 2
        copies.append(pltpu.make_async_remote_copy(
            src_ref=src_ref.at[2 * px + py], dst_ref=land_ref.at[k - 1], send_sem=send_sems[k - 1],
            recv_sem=recv_sems[k - 1], device_id=(px, py, c), device_id_type=MESH))
    return copies


def _exchange_start(chip_sum):
    _, r, cols = chip_sum.shape

    def body(src_ref, land_ref, s0, s1, s2, r0, r1, r2, src_thru, land_thru, token):
        for cp in _split_exchange_copies(src_ref, land_ref, (s0, s1, s2), (r0, r1, r2)):
            cp.start()
        token[...] = jnp.zeros_like(token)

    land = pltpu.with_memory_space_constraint(lax.empty((3, r, cols), chip_sum.dtype), pltpu.HBM)
    res = _pcall(
        body, name="exchange_start",
        out_shape=tuple([pltpu.SemaphoreType.DMA(())] * 6) + (
            pltpu.HBM(chip_sum.shape, chip_sum.dtype), pltpu.HBM((3, r, cols), chip_sum.dtype), _sds((8, 128), f32)),
        in_specs=(HBM_SPEC, HBM_SPEC), out_specs=tuple([SEM_SPEC] * 6) + (HBM_SPEC, HBM_SPEC, VMEM_SPEC),
        input_output_aliases={0: 6, 1: 7},
        compiler_params=pltpu.CompilerParams(has_side_effects=EFFECT),
    )(pltpu.with_memory_space_constraint(chip_sum, pltpu.HBM), land)
    return res[:6], res[6], res[7], res[8]


def _exchange_wait(sems, src_thru, land_thru, after):
    def body(src_ref, land_ref, s0, s1, s2, r0, r1, r2, after_ref, src_dead, got_ref):
        for cp in _split_exchange_copies(src_ref, land_ref, (s0, s1, s2), (r0, r1, r2)):
            cp.wait_send()
            cp.wait_recv()

    return _pcall(
        body, name="exchange_wait",
        out_shape=(pltpu.HBM(src_thru.shape, src_thru.dtype), pltpu.HBM(land_thru.shape, land_thru.dtype)),
        in_specs=(HBM_SPEC, HBM_SPEC) + tuple([SEM_SPEC] * 6) + (ANY,), out_specs=(HBM_SPEC, HBM_SPEC),
        input_output_aliases={0: 0, 1: 1},
        compiler_params=pltpu.CompilerParams(has_side_effects=EFFECT),
    )(src_thru, land_thru, *sems, after)[1]


def _input_grad(dsecs, wt_full, x2d, dx2, norm_g):
    T = x2d.shape[0]
    tb = min(T, 512)
    nsec = len(dsecs)
    ntok = T // tb

    def body(*refs):
        secs = refs[:nsec]
        wt_ref, x_ref, dx2_ref, g_ref, dx_ref, gnorm_ref = refs[nsec:]
        i = pl.program_id(0)

        @pl.when(i == 0)
        def _():
            gnorm_ref[...] = jnp.zeros_like(gnorm_ref)

        dh = None
        for a, (start, n) in enumerate(zip(SEC_START, SEC_CHUNKS)):
            part = _dot(secs[a][...], wt_ref[CH * start:CH * (start + n), :], NN)
            dh = part if dh is None else dh + part
        xv = x_ref[...]
        rstd = lax.rsqrt(jnp.mean(xv * xv, axis=-1, keepdims=True) + EPS)
        xh = xv * rstd
        gnorm_ref[0:1, :] += jnp.sum(dh * xh, axis=0, keepdims=True)
        dxn = dh * g_ref[...]
        dx_ref[...] = dx2_ref[...] + rstd * (dxn - xh * jnp.mean(dxn * xh, axis=-1, keepdims=True))

    tok = pl.BlockSpec((tb, D), lambda i: (i, 0))
    return _pcall(
        body, name="input_grad", grid=(ntok,),
        in_specs=[pl.BlockSpec((tb, sec.shape[1]), lambda i: (i, 0)) for sec in dsecs]
        + [pl.BlockSpec((D_IN, D), lambda i: (0, 0), pipeline_mode=pl.Buffered(1)), tok, tok,
           pl.BlockSpec((1, D), lambda i: (0, 0))],
        out_specs=(tok, pl.BlockSpec((8, D), lambda i: (0, 0))),
        out_shape=(_sds((T, D), f32), _sds((8, D), f32)),
        compiler_params=_params(("arbitrary",)),
    )(*dsecs, wt_full, x2d, dx2, norm_g)


def _w_in_grad(dsecs, h_bf):
    T = h_bf.shape[0]
    tk = min(T, 2048)
    nchunks = D_IN // CH
    nsec = len(dsecs)
    nt = T // tk

    def body(*refs):
        secs = refs[:nsec]
        h_ref, out_ref, acc = refs[nsec:]
        s, t = pl.program_id(0), pl.program_id(1)

        @pl.when(t == 0)
        def _():
            acc[...] = jnp.zeros_like(acc)

        h_rows = h_ref[pl.ds(pl.multiple_of(t * tk, tk), tk), :]
        for a, (start, n) in enumerate(zip(SEC_START, SEC_CHUNKS)):
            @pl.when((s >= start) & (s < start + n))
            def _(a=a):
                acc[...] += _dot(secs[a][...], h_rows, TN)

        @pl.when(t == nt - 1)
        def _():
            out_ref[...] = acc[...].astype(bf16)

    def sec_spec(a):
        def index(s, t, a=a):
            inside, local = _section_of_chunk(s)[a]
            return (jnp.where(inside, t, 0), local)
        return pl.BlockSpec((tk, CH), index)

    return _pcall(
        body, name="w_in_grad", grid=(nchunks, T // tk),
        in_specs=[sec_spec(a) for a in range(nsec)]
        + [pl.BlockSpec((T, D), lambda s, t: (0, 0), pipeline_mode=pl.Buffered(1))],
        out_specs=pl.BlockSpec((CH, D), lambda s, t: (s, 0)), out_shape=_sds((D_IN, D), bf16),
        scratch_shapes=[pltpu.VMEM((CH, D), f32)],
        compiler_params=_params(("arbitrary", "arbitrary")),
    )(*dsecs, h_bf)


SMALL_NAMES = ("lru_w_a", "lru_w_x", "conv_b", "lru_b_a", "lru_b_x", "lru_lambda", "norm_g", "final_norm_g",
               "attn_sinks", "conv_w")
MISC_ROW = {"conv_b": 0, "lru_b_a": 1, "lru_b_x": 2, "lru_lambda": 3, "norm_g": 8, "final_norm_g": 16,
            "attn_sinks": 24, "loss": 32}


def _small_step(gwa, gwx, gvec, gnorm_blk, gfin_blk, dsink_blk, loss_blk, gcw, params):
    srcs_rows = (RB // NDEV, RB // NDEV, 8, 8)
    flat = [t for n in SMALL_NAMES for t in params[n]]
    nout = 4 * len(SMALL_NAMES) + 1

    def reduce_body(gwa_ref, gwx_ref, gvec_ref, gnorm_ref, gfin_ref, dsink_ref, loss_ref, gcw_ref,
                    all_a, all_x, all_m, conv_out,
                    misc, got_a, got_x, got_m, got_c, red_a, red_x, red_m, sa, ra, sb, rb):
        x, y, c = _my_place()
        me = 4 * x + 2 * y + c

        misc[...] = jnp.zeros_like(misc)
        misc[0:8, :] = gvec_ref[...]
        misc[8:16, :] = gnorm_ref[...]
        misc[16:24, :] = gfin_ref[...]
        misc[24:32, 0:128] = dsink_ref[...]
        misc[32:40, :] = loss_ref[...]

        srcs = (gwa_ref, gwx_ref, misc, gcw_ref)
        gots = (got_a, got_x, got_m, got_c)

        def shard(ref, rows, dev):
            return ref.at[pl.ds(pl.multiple_of(dev * rows, 8), rows), :]

        scatter = []
        for k in range(1, NDEV):
            px, py, pc = _peer(k)
            for a in range(4):
                scatter.append(pltpu.make_async_remote_copy(
                    src_ref=shard(srcs[a], srcs_rows[a], 4 * px + 2 * py + pc), dst_ref=gots[a].at[k - 1],
                    send_sem=sa.at[4 * (k - 1) + a], recv_sem=ra.at[4 * (k - 1) + a],
                    device_id=(px, py, pc), device_id_type=MESH))
        for cp in scatter:
            cp.start()
        for cp in scatter:
            cp.wait()

        def reduced(a):
            rows = srcs_rows[a]
            total = srcs[a][pl.ds(pl.multiple_of(me * rows, 8), rows), :]
            for k in range(NDEV - 1):
                total = total + gots[a][k]
            return total

        reds = (red_a, red_x, red_m)
        alls = (all_a, all_x, all_m)
        for a in range(3):
            val = reduced(a)
            reds[a][...] = val
            alls[a][pl.ds(pl.multiple_of(me * srcs_rows[a], 8), srcs_rows[a]), :] = val
        gather = []
        for k in range(1, NDEV):
            peer = _peer(k)
            for a in range(3):
                gather.append(pltpu.make_async_remote_copy(
                    src_ref=reds[a], dst_ref=shard(alls[a], srcs_rows[a], me),
                    send_sem=sb.at[3 * (k - 1) + a], recv_sem=rb.at[3 * (k - 1) + a],
                    device_id=peer, device_id_type=MESH))
        for cp in gather:
            cp.start()
        conv_out[...] = reduced(3)
        for cp in gather:
            cp.wait()

    def adam_body(*refs):
        all_a, all_x, all_m, conv_ref = refs[:4]
        prm = {n: refs[4 + 3 * k:7 + 3 * k] for k, n in enumerate(SMALL_NAMES)}
        nin = 4 + len(flat)
        outs = {n: refs[nin + 4 * k:nin + 4 * k + 4] for k, n in enumerate(SMALL_NAMES)}
        loss_out = refs[nin + nout - 1]
        g_conv = conv_ref[0:4, :]

        def update(name, g, pick=lambda r: r[...]):
            w_ref, m_ref, v_ref = prm[name]
            delta, m_new, v_new = _adam_math(g, pick(w_ref), pick(m_ref), pick(v_ref))
            return g, delta, m_new, v_new

        for n in range(RNN_BLOCKS):
            lanes = slice(RB * n, RB * (n + 1))
            for name, full in (("lru_w_a", all_a), ("lru_w_x", all_x)):
                for out, val in zip(outs[name], update(name, full[:, lanes], pick=lambda r, n=n: r[n])):
                    out[n] = val
        for name in ("conv_b", "lru_b_a", "lru_b_x", "lru_lambda", "norm_g", "final_norm_g"):
            row = MISC_ROW[name]
            for out, val in zip(outs[name], update(name, all_m[row:row + 1, :])):
                out[...] = val
        row = MISC_ROW["attn_sinks"]
        for out, val in zip(outs["attn_sinks"], update("attn_sinks", all_m[row:row + 1, 0:16])):
            out[...] = val
        for out, val in zip(outs["conv_w"], update("conv_w", g_conv)):
            out[...] = val
        row = MISC_ROW["loss"]
        loss_out[...] = all_m[row:row + 8, 0:128] * (0.5 / D)

    scratch = [pltpu.VMEM((64, D), f32),
               pltpu.VMEM((NDEV - 1, RB // NDEV, D), f32), pltpu.VMEM((NDEV - 1, RB // NDEV, D), f32),
               pltpu.VMEM((NDEV - 1, 8, D), f32), pltpu.VMEM((NDEV - 1, 8, RB), f32),
               pltpu.VMEM((RB // NDEV, D), f32), pltpu.VMEM((RB // NDEV, D), f32), pltpu.VMEM((8, D), f32),
               pltpu.SemaphoreType.DMA((4 * (NDEV - 1),)), pltpu.SemaphoreType.DMA((4 * (NDEV - 1),)),
               pltpu.SemaphoreType.DMA((3 * (NDEV - 1),)), pltpu.SemaphoreType.DMA((3 * (NDEV - 1),))]
    sums = _pcall(
        reduce_body, name="small_reduce",
        out_shape=(_sds((RB, D), f32), _sds((RB, D), f32), _sds((64, D), f32), _sds((8, RB), f32)),
        in_specs=[VMEM_SPEC] * 8, out_specs=tuple([VMEM_SPEC] * 4),
        scratch_shapes=scratch, compiler_params=_params(),
    )(gwa, gwx, gvec, gnorm_blk, gfin_blk, dsink_blk, loss_blk, gcw)
    out_shape = tuple(_sds(params[n][0].shape, f32) for n in SMALL_NAMES for _ in range(4)) + (_sds((8, 128), f32),)
    res = _pcall(
        adam_body, name="small_adamw", out_shape=out_shape,
        in_specs=[VMEM_SPEC] * (4 + len(flat)), out_specs=tuple([VMEM_SPEC] * nout), compiler_params=_params(),
    )(*sums, *flat)
    return {n: res[4 * k:4 * k + 4] for k, n in enumerate(SMALL_NAMES)}, res[-1]


def _pad_rows(v, rows=8):
    return jnp.concatenate([v, jnp.zeros((rows - v.shape[0], v.shape[1]), v.dtype)], axis=0)


def kernel(x, norm_g, w_in, conv_w, conv_b, lru_w_a, lru_b_a, lru_w_x, lru_b_x, lru_lambda, attn_sinks, w_rnn_out, w_attn_out, w_o, final_norm_g, loss_target, m_norm_g, m_w_in, m_conv_w, m_conv_b, m_lru_w_a, m_lru_b_a, m_lru_w_x, m_lru_b_x, m_lru_lambda, m_attn_sinks, m_w_rnn_out, m_w_attn_out, m_w_o, m_final_norm_g, v_norm_g, v_w_in, v_conv_w, v_conv_b, v_lru_w_a, v_lru_b_a, v_lru_w_x, v_lru_b_x, v_lru_lambda, v_attn_sinks, v_w_rnn_out, v_w_attn_out, v_w_o, v_final_norm_g):
    nb, S, _ = x.shape
    T = nb * S
    x2d = x.reshape(T, D)
    tgt = loss_target.reshape(T, D)
    fin_g = final_norm_g.reshape(1, D)
    w_a3, w_x3 = lru_w_a[0], lru_w_x[0]

    my_core = lax.axis_index("c").astype(jnp.int32).reshape(1)
    cx, cy = lax.axis_index("x"), lax.axis_index("y")
    chip_order = jnp.stack([2 * cx + cy, 2 * (1 - cx) + cy, 2 * cx + (1 - cy),
                            2 * (1 - cx) + (1 - cy)]).astype(jnp.int32)

    tabs = _rope_tables(S)
    h_bf, proj, wt_full, cw_full, _ = _in_proj_gather(
        x2d, norm_g, w_in[0].T.astype(bf16), _pad_rows(conv_w[0]), tabs, S, (), chip_order)
    y_rnn, h_all = _lru_forward(proj, cw_full, conv_b, w_a3, lru_b_a, w_x3, lru_b_x, lru_lambda, S)
    y_attn, (wr_full, wa_full, wo_full) = _attn_forward(proj, attn_sinks, S,
                                                        (w_rnn_out[0], w_attn_out[0], w_o[0]))

    (dx2, dy_rnn, dy_attn, dmr, dma, loss_blk, gfin_blk, g_wr, g_wa, g_wo) = _merge_and_head(
        x2d, tgt, proj, y_rnn, y_attn, wr_full, wa_full, wo_full, fin_g)
    sums_out = _pair_sums([g_wr, g_wa, g_wo], bf16, my_core, "out")

    (dq, dkv, dga, dsink_blk), (p_wr, p_wa, p_wo) = _attn_backward(proj, dy_attn, tabs, attn_sinks, S, sums_out)
    du0, dgr, gwa, gwx, gvec, gcw = _lru_backward(proj, h_all, dy_rnn, cw_full, conv_b, w_a3, lru_b_a, w_x3,
                                                  lru_b_x, lru_lambda, S)
    dsecs = (du0, dgr, dq, dkv, dga, dmr, dma)

    g_wt = _w_in_grad(dsecs, h_bf)
    (sum_in,) = _pair_sums([g_wt], bf16, my_core, "in")
    ex_sems, sum_in, landing, token = _exchange_start(sum_in)
    grad_x2d, gnorm_blk = _input_grad(dsecs, wt_full, x2d, dx2, norm_g + token[0, 0])
    p_wt = _exchange_wait(ex_sems, sum_in, landing, gnorm_blk)
    p_wt_own = lax.dynamic_index_in_dim(sum_in, 2 * cx + cy, axis=0, keepdims=False)

    small, loss_out = _small_step(gwa, gwx, gvec, gnorm_blk, gfin_blk, dsink_blk, loss_blk, gcw, {
        "lru_w_a": (w_a3, m_lru_w_a[0], v_lru_w_a[0]), "lru_w_x": (w_x3, m_lru_w_x[0], v_lru_w_x[0]),
        "conv_b": (conv_b, m_conv_b, v_conv_b), "lru_b_a": (lru_b_a, m_lru_b_a, v_lru_b_a),
        "lru_b_x": (lru_b_x, m_lru_b_x, v_lru_b_x), "lru_lambda": (lru_lambda, m_lru_lambda, v_lru_lambda),
        "norm_g": (norm_g, m_norm_g, v_norm_g),
        "final_norm_g": (fin_g, m_final_norm_g.reshape(1, D), v_final_norm_g.reshape(1, D)),
        "attn_sinks": (attn_sinks, m_attn_sinks, v_attn_sinks),
        "conv_w": (conv_w[0], m_conv_w[0], v_conv_w[0])})

    o_wt = _adamw(p_wt_own, p_wt, w_in[0].T, m_w_in[0].T, v_w_in[0].T, "adamw_w_in")
    o_wr, o_wa, o_wo = _adamw_group(
        (p_wr, p_wa, p_wo), (w_rnn_out[0], w_attn_out[0], w_o[0]),
        (m_w_rnn_out[0], m_w_attn_out[0], m_w_o[0]), (v_w_rnn_out[0], v_w_attn_out[0], v_w_o[0]), "adamw_w_out")

    def result(kind):
        d = {n: small[n][kind] for n in ("conv_b", "lru_b_a", "lru_b_x", "lru_lambda", "norm_g", "attn_sinks")}
        d.update({n: small[n][kind][None] for n in ("lru_w_a", "lru_w_x", "conv_w")})
        d["final_norm_g"] = small["final_norm_g"][kind].reshape(D)
        d.update({"w_in": o_wt[kind].T[None], "w_rnn_out": o_wr[kind][None], "w_attn_out": o_wa[kind][None],
                  "w_o": o_wo[kind][None]})
        return d

    order = ("norm_g", "w_in", "conv_w", "conv_b", "lru_w_a", "lru_b_a", "lru_w_x", "lru_b_x", "lru_lambda",
             "attn_sinks", "w_rnn_out", "w_attn_out", "w_o", "final_norm_g")
    outs = [loss_out[0, 0], grad_x2d.reshape(nb, S, D)]
    for kind in range(4):
        d = result(kind)
        outs += [d[n] for n in order]
    return tuple(outs)
```

```python
import functools
import math

import jax
import jax.numpy as jnp
from jax import lax
from jax.experimental import pallas as pl
from jax.experimental.pallas import tpu as pltpu

f32 = jnp.float32
bf16 = jnp.bfloat16

D = 1024
D_IN = 6656
NDEV = 8
RNN_BLOCKS = 8
RB = 128
HEAD = 64
KV_HEADS = 4
GROUP = 4
QB = 128
LRU_C = 8.0
EPS = 1e-6
ROPE_DIM = 16
ROPE_THETA = 500000.0
CH = 512
SEC_START = (0, 2, 4, 6, 7, 9, 11)
SEC_CHUNKS = (2, 2, 2, 1, 2, 2, 2)
VMEM_LIMIT = 62 * 1024 * 1024

ADAM_LR, ADAM_B1, ADAM_B2, ADAM_EPS, ADAM_WD, ADAM_STEP = 0.001, 0.9, 0.999, 1e-08, 0.01, 10

MESH = pl.DeviceIdType.MESH
ANY = pl.BlockSpec(memory_space=pl.ANY)
VMEM_SPEC = pl.BlockSpec(memory_space=pltpu.VMEM)
SMEM_SPEC = pl.BlockSpec(memory_space=pltpu.SMEM)


def _pcall(body, **kw):
    return pl.pallas_call(body, **kw)


def _params(sem=None, **kw):
    if sem is not None:
        kw["dimension_semantics"] = sem
    return pltpu.CompilerParams(vmem_limit_bytes=VMEM_LIMIT, **kw)


def _sds(shape, dtype):
    return jax.ShapeDtypeStruct(shape, dtype)


def _dot(a, b, dims):
    return lax.dot_general(a, b, (dims, ((), ())), preferred_element_type=f32)


NN = ((1,), (0,))
NT = ((1,), (1,))
TN = ((0,), (0,))


def _sigmoid(v):
    return 0.5 * jnp.tanh(0.5 * v) + 0.5


def _sigmoid_positive(v):
    return 1.0 / (1.0 + jnp.exp(-v))


def _my_place():
    return lax.axis_index("x"), lax.axis_index("y"), lax.axis_index("c")


def _peer(k):
    x, y, c = _my_place()
    return (x + ((k >> 2) & 1)) % 2, (y + ((k >> 1) & 1)) % 2, (c + (k & 1)) % 2


def _direct_gather_copies(srcs, outs, send_sems, recv_sems, local_sems):
    x, y, c = _my_place()
    me = 4 * x + 2 * y + c
    local, remote = [], []
    for a, (src, out) in enumerate(zip(srcs, outs)):
        r = src.shape[0]
        mine = out.at[pl.ds(pl.multiple_of(me * r, 8), r), :]
        local.append(pltpu.make_async_copy(src, mine, local_sems.at[a]))
        for k in range(1, NDEV):
            remote.append(pltpu.make_async_remote_copy(
                src_ref=src, dst_ref=mine, send_sem=send_sems.at[7 * a + k - 1], recv_sem=recv_sems.at[7 * a + k - 1],
                device_id=_peer(k), device_id_type=MESH))
    return local, remote


def _chip_exchange_copies(src, dst, send_sems, recv_sems, local_sems):
    x, y, c = _my_place()
    local, remote = [], []
    for a in range(len(src)):
        local.append(pltpu.make_async_copy(src[a].at[2 * x + y], dst[a].at[0], local_sems.at[a]))
    for k in (3, 1, 2):
        px, py = (x + (k >> 1)) % 2, (y + (k & 1)) % 2
        for a in range(len(src)):
            remote.append(pltpu.make_async_remote_copy(
                src_ref=src[a].at[2 * px + py], dst_ref=dst[a].at[k],
                send_sem=send_sems.at[3 * a + k - 1], recv_sem=recv_sems.at[3 * a + k - 1],
                device_id=(px, py, c), device_id_type=MESH))
    return local, remote


def _exchange_scratch(narr, per_array):
    return [pltpu.SemaphoreType.DMA((per_array * narr,)), pltpu.SemaphoreType.DMA((per_array * narr,)),
            pltpu.SemaphoreType.DMA((narr,))]


def _start_all(copies):
    local, remote = copies
    for cp in local + remote:
        cp.start()


def _wait_all(copies):
    local, remote = copies
    for cp in remote + local:
        cp.wait()


def _row_tile(rows, dtype):
    unit = 16 if dtype == bf16 else 8
    for cand in (256, 208, 128, 64, 40, 32, 16, 8):
        if rows % cand == 0 and cand % unit == 0:
            return cand
    return rows


def _pair_sums(grads, wire_dtype, my_core, tag):
    narr = len(grads)
    r, cols = grads[0].shape[0] // NDEV, grads[0].shape[1]
    views = [g.reshape(4, 2, r, cols) for g in grads]
    tr = _row_tile(r, wire_dtype)
    nt = r // tr

    def body(core_ref, *refs):
        mine = refs[:narr]
        whole = refs[narr:2 * narr]
        outs = refs[2 * narr:3 * narr]
        got = refs[3 * narr:4 * narr]
        send_sems, recv_sems = refs[4 * narr:]
        q, i = pl.program_id(0), pl.program_id(1)
        x, y, c = _my_place()

        def copy(a, chip):
            return pltpu.make_async_remote_copy(
                src_ref=whole[a].at[chip, 1 - c], dst_ref=got[a].at[chip],
                send_sem=send_sems.at[4 * a + chip], recv_sem=recv_sems.at[4 * a + chip],
                device_id=(x, y, 1 - c), device_id_type=MESH)

        @pl.when((q == 0) & (i == 0))
        def _():
            for chip in range(4):
                for a in range(narr):
                    copy(a, chip).start()

        for chip in range(4):
            @pl.when((q == chip) & (i == 0))
            def _(chip=chip):
                for a in range(narr):
                    copy(a, chip).wait_recv()

        rows = pl.ds(pl.multiple_of(i * tr, tr), tr)
        for a in range(narr):
            outs[a][...] = (mine[a][...].astype(f32) + got[a][q, rows, :].astype(f32)).astype(wire_dtype)

        @pl.when((q == 3) & (i == nt - 1))
        def _():
            for chip in range(4):
                for a in range(narr):
                    copy(a, chip).wait_send()

    slab = pl.BlockSpec((None, tr, cols), lambda q, i, core: (q, i, 0))
    grid_spec = pltpu.PrefetchScalarGridSpec(
        num_scalar_prefetch=1, grid=(4, nt),
        in_specs=[pl.BlockSpec((None, None, tr, cols), lambda q, i, core: (q, core[0], i, 0))] * narr + [ANY] * narr,
        out_specs=tuple([slab] * narr),
        scratch_shapes=[pltpu.VMEM((4, r, cols), grads[0].dtype)] * narr
        + [pltpu.SemaphoreType.DMA((4 * narr,)), pltpu.SemaphoreType.DMA((4 * narr,))])
    return _pcall(body, name="pair_sums_" + tag, grid_spec=grid_spec,
                  out_shape=tuple(_sds((4, r, cols), wire_dtype) for _ in range(narr)),
                  compiler_params=_params(("arbitrary", "arbitrary")))(my_core, *views, *views)


def _adam_math(g, w, m, v):
    m_new = ADAM_B1 * m + (1.0 - ADAM_B1) * g
    v_new = ADAM_B2 * v + (1.0 - ADAM_B2) * (g * g)
    m_hat = m_new / (1.0 - ADAM_B1 ** ADAM_STEP)
    v_hat = v_new / (1.0 - ADAM_B2 ** ADAM_STEP)
    return -ADAM_LR * (m_hat / (jnp.sqrt(v_hat) + ADAM_EPS) + ADAM_WD * w), m_new, v_new


def _adamw(first, parts, w, m, v, name):
    n, rows, cols = parts.shape
    tr = _row_tile(rows, parts.dtype)

    def body(f_ref, p_ref, w_ref, m_ref, v_ref, g_out, d_out, m_out, v_out):
        g = f_ref[...].astype(f32)
        for s in range(n):
            g = g + p_ref[s].astype(f32)
        g_out[...] = g
        d_out[...], m_out[...], v_out[...] = _adam_math(g, w_ref[...], m_ref[...], v_ref[...])

    blk = pl.BlockSpec((tr, cols), lambda i: (i, 0))
    return _pcall(
        body, name=name, grid=(rows // tr,),
        in_specs=[blk, pl.BlockSpec((n, tr, cols), lambda i: (0, i, 0)), blk, blk, blk],
        out_specs=(blk, blk, blk, blk), out_shape=tuple(_sds((rows, cols), f32) for _ in range(4)),
        compiler_params=_params(("arbitrary",)),
    )(first, parts, w, m, v)


def _adamw_group(parts, ws, ms, vs, name):
    nw = len(ws)

    def body(*refs):
        p_refs, w_refs, m_refs, v_refs = (refs[k * nw:(k + 1) * nw] for k in range(4))
        outs = refs[4 * nw:]
        for k in range(nw):
            g = p_refs[k][0].astype(f32)
            for s in range(1, p_refs[k].shape[0]):
                g = g + p_refs[k][s].astype(f32)
            g_out, d_out, m_out, v_out = outs[4 * k:4 * k + 4]
            g_out[...] = g
            d_out[...], m_out[...], v_out[...] = _adam_math(g, w_refs[k][...], m_refs[k][...], v_refs[k][...])

    res = _pcall(
        body, name=name, out_shape=tuple(_sds(w.shape, f32) for w in ws for _ in range(4)),
        in_specs=[VMEM_SPEC] * (4 * nw), out_specs=tuple([VMEM_SPEC] * (4 * nw)), compiler_params=_params(),
    )(*parts, *ws, *ms, *vs)
    return [res[4 * k:4 * k + 4] for k in range(nw)]


def _rope(t, c, s1, s2):
    w = t.shape[1]
    return t * c + pltpu.roll(t, w - 8, 1) * s1 + pltpu.roll(t, 8, 1) * s2


def _rope_transposed(dt, c, s1, s2):
    w = dt.shape[1]
    return dt * c + pltpu.roll(dt * s1, 8, 1) + pltpu.roll(dt * s2, w - 8, 1)


PAIR_ROWS = D_IN // 4
SUB_COLS = ((0, 512), (512, 512), (1024, 512), (1536, 128))
Q_SLABS = range(3, 11)
K_SLABS = range(11, 13)


def _in_proj_gather(x2d, norm_g, wt_shard, cw_shard, tabs, S, out_shards, chip_order):
    T = x2d.shape[0]
    tb = min(S, 1024)
    ntok = T // tb
    nsb = S // tb
    q_scale = 1.0 / math.sqrt(HEAD)
    shard_rows = wt_shard.shape[0]
    small = (cw_shard,) + tuple(out_shards)
    nsm = len(small)

    def body(order_ref, x_ref, g_ref, c_ref, s1_ref, s2_ref, wt_hbm, *rest):
        small_in = rest[:nsm]
        h_ref, proj_ref, wt_out = rest[nsm:nsm + 3]
        small_out = rest[nsm + 3:2 * nsm + 3]
        wt_vm, h_vm = rest[2 * nsm + 3:2 * nsm + 5]
        stage = rest[2 * nsm + 5:3 * nsm + 4]
        wsend, wrecv, wlocal = rest[3 * nsm + 4:3 * nsm + 7]
        dsems = rest[3 * nsm + 7:]
        jj, i = pl.program_id(0), pl.program_id(1)
        x, y, c = _my_place()
        me, sibling = (x, y, c), (x, y, 1 - c)
        chips = [(1 - x, y), (x, 1 - y), (1 - x, 1 - y)]

        def rows(place):
            px, py, pc = place
            return wt_vm.at[pl.ds(pl.multiple_of((4 * px + 2 * py + pc) * shard_rows, 16), shard_rows), :]

        def copy(k, block, to, src=None):
            return pltpu.make_async_remote_copy(
                src_ref=rows(block) if src is None else src, dst_ref=rows(block),
                send_sem=wsend.at[k], recv_sem=wrecv.at[k], device_id=to, device_id_type=MESH)

        def small_copies():
            srcs = (small_in[0],) + tuple(stage)
            return _direct_gather_copies(srcs, small_out, *dsems)

        own = pltpu.make_async_copy(wt_hbm, rows(me), wlocal.at[0])
        keep = pltpu.make_async_copy(wt_vm, wt_out, wlocal.at[1])

        @pl.when((jj == 0) & (i == 0))
        def _():
            own.start()
            copy(0, me, sibling, src=wt_hbm).start()
            for j, chip in enumerate(chips):
                copy(1 + j, me, (*chip, c), src=wt_hbm).start()
            for a in range(nsm - 1):
                stage[a][...] = small_in[1 + a][...].astype(bf16)
            _start_all(small_copies())
            own.wait()
            copy(0, sibling, me).wait_recv()

        for j, chip in enumerate(chips):
            @pl.when((jj == 1 + j) & (i == 0))
            def _(j=j, chip=chip):
                copy(1 + j, (*chip, c), me).wait_recv()
                copy(4 + j, (*chip, c), sibling).start()
                copy(4 + j, (*chip, 1 - c), me).wait_recv()

        @pl.when((jj == 3) & (i == 0))
        def _():
            keep.start()

        @pl.when((jj == 3) & (i == ntok - 1))
        def _():
            copy(0, me, sibling, src=wt_hbm).wait_send()
            for j, chip in enumerate(chips):
                copy(1 + j, me, (*chip, c), src=wt_hbm).wait_send()
                copy(4 + j, (*chip, c), sibling).wait_send()
            _wait_all(small_copies())
            keep.wait()

        tok = pl.ds(pl.multiple_of(i * tb, tb), tb)

        @pl.when(jj == 0)
        def _():
            xv = x_ref[...]
            ms = jnp.mean(xv * xv, axis=-1, keepdims=True)
            hb = (xv * lax.rsqrt(ms + EPS) * g_ref[...]).astype(bf16)
            h_ref[...] = hb
            h_vm[tok, :] = hb

        block = order_ref[jj]
        hb = h_vm[tok, :]

        def piece(c0, w):
            w_rows = wt_vm[pl.ds(pl.multiple_of(block * PAIR_ROWS + c0, 128), w), :]
            return _dot(hb, w_rows, NT)

        @pl.when(block != 1)
        def _():
            for c0, w in SUB_COLS:
                proj_ref[:, c0:c0 + w] = piece(c0, w).astype(bf16)

        @pl.when(block == 1)
        def _():
            tab = (c_ref[...], s1_ref[...], s2_ref[...])
            for c0, w in SUB_COLS:
                acc = piece(c0, w)
                for l in range(w // 128):
                    slab = (c0 + 128 * l) // 128
                    part = acc[:, 128 * l:128 * (l + 1)]
                    if slab in Q_SLABS:
                        part = _rope(part, *tab) * q_scale
                    elif slab in K_SLABS:
                        part = _rope(part, *tab)
                    proj_ref[:, 128 * slab:128 * (slab + 1)] = part.astype(bf16)

    first_pass = lambda jj, i, order: (jnp.where(jj == 0, i, ntok - 1), 0)
    const = lambda jj, i, order: (0, 0)
    tab = pl.BlockSpec((tb, 128), lambda jj, i, order: (jnp.where(order[jj] == 1, i % nsb, 0), 0))
    grid_spec = pltpu.PrefetchScalarGridSpec(
        num_scalar_prefetch=1, grid=(4, ntok),
        in_specs=[pl.BlockSpec((tb, D), first_pass), pl.BlockSpec((1, D), const), tab, tab, tab, ANY]
        + [pl.BlockSpec(w.shape, const) for w in small],
        out_specs=(pl.BlockSpec((tb, D), first_pass),
                   pl.BlockSpec((tb, PAIR_ROWS), lambda jj, i, order: (i, order[jj])), ANY) + tuple([ANY] * nsm),
        scratch_shapes=[pltpu.VMEM((D_IN, D), bf16), pltpu.VMEM((T, D), bf16)]
        + [pltpu.VMEM(w.shape, bf16) for w in out_shards]
        + [pltpu.SemaphoreType.DMA((7,)), pltpu.SemaphoreType.DMA((7,)), pltpu.SemaphoreType.DMA((2,))]
        + _exchange_scratch(nsm, 7))
    res = _pcall(
        body, name="in_proj", grid_spec=grid_spec,
        out_shape=(_sds((T, D), bf16), _sds((T, D_IN), bf16), _sds((D_IN, D), bf16),
                   _sds((NDEV * cw_shard.shape[0], cw_shard.shape[1]), f32))
        + tuple(_sds((NDEV * w.shape[0], w.shape[1]), bf16) for w in out_shards),
        compiler_params=_params(("arbitrary", "arbitrary")),
    )(chip_order, x2d, norm_g, *tabs, wt_shard, *small)
    return res[0], res[1], res[2], res[3], res[4:]


def _rows_iota(shape):
    return lax.broadcasted_iota(jnp.int32, shape, 0)


def _shift_down(v, k):
    return jnp.where(_rows_iota(v.shape) >= k, pltpu.roll(v, k, 0), 0.0)


def _shift_up(v, k):
    n = v.shape[0]
    return jnp.where(_rows_iota(v.shape) < n - k, pltpu.roll(v, n - k, 0), 0.0)


def _linear_scan(a, b, a_s, b_s, edge_s, out_ref, reverse):
    n = a.shape[0]
    ng = n // 8
    a3, b3 = a.reshape(ng, 8, RB), b.reshape(ng, 8, RB)
    rid = lax.broadcasted_iota(jnp.int32, a3.shape, 1)
    for s in (1, 2, 4):
        keep, shift = (rid < 8 - s, 8 - s) if reverse else (rid >= s, s)
        b3 = jnp.where(keep, a3 * pltpu.roll(b3, shift, 1) + b3, b3)
        a3 = jnp.where(keep, a3 * pltpu.roll(a3, shift, 1), a3)
    a_s[...] = a3.reshape(n, RB)
    b_s[...] = b3.reshape(n, RB)
    edge = 0 if reverse else 7
    ea, eb = a_s[pl.ds(edge, ng, stride=8), :], b_s[pl.ds(edge, ng, stride=8), :]
    r = _rows_iota(ea.shape)
    s = 1
    while s < ng:
        keep, shift = (r < ng - s, ng - s) if reverse else (r >= s, s)
        eb = jnp.where(keep, ea * pltpu.roll(eb, shift, 0) + eb, eb)
        if 2 * s < ng:
            ea = jnp.where(keep, ea * pltpu.roll(ea, shift, 0), ea)
        s *= 2
    edge_s[...] = _shift_up(eb, 1) if reverse else _shift_down(eb, 1)

    def eight_groups(i, carry):
        for k in range(8):
            j = i * 8 + k
            rows = pl.ds(pl.multiple_of(j * 8, 8), 8)
            out_ref[rows, :] = b_s[rows, :] + a_s[rows, :] * edge_s[pl.ds(j, 1), :]
        return carry

    lax.fori_loop(0, ng // 8, eight_groups, 0)


def _neg_expm1(v):
    series = -v * (1.0 + v * (0.5 + v * (1.0 / 6.0)))
    return jnp.where(v > -0.015625, series, 1.0 - jnp.exp(v))


def _softplus_neg(lam):
    return jnp.maximum(-lam, 0.0) + jnp.log(1.0 + jnp.exp(-jnp.abs(lam)))


def _lru_gates(x0, cw, cb, wa, ba, wx, bx, lam):
    taps = [_shift_down(x0, 3 - k) for k in range(3)] + [x0]
    u = cb + cw[3:4, :] * x0
    for k in range(3):
        u = u + cw[k:k + 1, :] * taps[k]
    ub = u.astype(bf16)
    r = _sigmoid_positive(_dot(ub, wa.astype(bf16), NN) + ba)
    i = _sigmoid(_dot(ub, wx.astype(bf16), NN) + bx)
    sp = _softplus_neg(lam)
    log_a = (-LRU_C) * r * sp
    a = jnp.exp(log_a)
    w = _neg_expm1(2.0 * log_a)
    inv_mult = lax.rsqrt(w)
    return u, ub, r, i, sp, a, w * inv_mult, inv_mult, taps


def _lru_specs(S, nb):
    col = lambda off: pl.BlockSpec((S, RB), lambda n, b, off=off: (b, off + n))
    vec = pl.BlockSpec((1, RB), lambda n, b: (0, n))
    wblk = pl.BlockSpec((None, RB, RB), lambda n, b: (n, 0, 0))
    cwblk = pl.BlockSpec((8, RB), lambda n, b: (n, 0))
    return col, vec, wblk, cwblk


def _lru_forward(proj, cw_full, conv_b, w_a, b_a, w_x, b_x, lam, S):
    T = proj.shape[0]
    nb = T // S
    col, vec, wblk, cwblk = _lru_specs(S, nb)

    def body(x0_ref, g_ref, cw_ref, cb_ref, wa_ref, ba_ref, wx_ref, bx_ref, lam_ref, y_ref, h_ref, a_s, b_s, edge_s):
        x0 = x0_ref[...].astype(f32)
        u, ub, r, i, sp, a, mult, _, _ = _lru_gates(x0, cw_ref[...], cb_ref[...], wa_ref[...], ba_ref[...],
                                                    wx_ref[...], bx_ref[...], lam_ref[...])
        _linear_scan(a, mult * (i * u), a_s, b_s, edge_s, h_ref, reverse=False)
        g = g_ref[...].astype(f32)
        y_ref[...] = (h_ref[...] * (g * _sigmoid(g))).astype(bf16)

    out = pl.BlockSpec((S, RB), lambda n, b: (b, n))
    return _pcall(
        body, name="lru_forward", grid=(RNN_BLOCKS, nb),
        in_specs=[col(0), col(8), cwblk, vec, wblk, vec, wblk, vec, vec],
        out_specs=(out, out), out_shape=(_sds((T, D), bf16), _sds((T, D), f32)),
        scratch_shapes=[pltpu.VMEM((S, RB), f32), pltpu.VMEM((S, RB), f32), pltpu.VMEM((S // 8, RB), f32)],
        compiler_params=_params(("arbitrary", "arbitrary")),
    )(proj, proj, cw_full, conv_b, w_a, b_a, w_x, b_x, lam)


def _rope_tables(S):
    pos = jnp.arange(S, dtype=f32)
    inv_freq = ROPE_THETA ** (-jnp.arange(0, ROPE_DIM, 2, dtype=f32) / ROPE_DIM)
    ang = pos[:, None] * inv_freq[None, :]
    cos, sin = jnp.cos(ang), jnp.sin(ang)
    lane = jnp.arange(128) % HEAD
    cosl, sinl = cos[:, lane % 8], sin[:, lane % 8]
    c = jnp.where(lane[None, :] < ROPE_DIM, cosl, 1.0)
    s1 = jnp.where(lane[None, :] < 8, -sinl, 0.0)
    s2 = jnp.where((lane[None, :] >= 8) & (lane[None, :] < ROPE_DIM), sinl, 0.0)
    return c.astype(f32), s1.astype(f32), s2.astype(f32)


def _heads_to_rows(t):
    return jnp.concatenate([t[:, HEAD * h:HEAD * (h + 1)] for h in range(GROUP)], axis=0)


def _rows_to_heads(t):
    return jnp.concatenate([t[QB * h:QB * (h + 1), :] for h in range(GROUP)], axis=1)


def _window_bias(first_block):
    shape = (GROUP * QB, 2 * QB)
    qi = _rows_iota(shape) % QB
    cj = lax.broadcasted_iota(jnp.int32, shape, 1)
    valid = (cj > qi) & (cj <= qi + QB) & ((cj >= QB) | jnp.logical_not(first_block))
    return jnp.where(valid, 0.0, -jnp.inf)


def _attn_probs(q_rows, k_cat, sink_col, bias):
    s = _dot(q_rows, k_cat, NT) + bias
    m = jnp.maximum(jnp.max(s, axis=1, keepdims=True), sink_col)
    p = jnp.exp(s - m)
    e_sink = jnp.exp(sink_col - m)
    inv = 1.0 / (jnp.sum(p, axis=1, keepdims=True) + e_sink)
    return p * inv, e_sink * inv


def _sink_column(sink_ref, kv):
    rid = _rows_iota((GROUP * QB, 1))
    col = jnp.zeros((GROUP * QB, 1), f32)
    for h in range(GROUP):
        col = jnp.where(rid // QB == h, sink_ref[0, GROUP * kv + h], col)
    return col


def _attn_in_specs(S):
    nq = S // QB
    last = nq - 1
    cur = lambda b, j: b * nq + jnp.minimum(j, last)
    prev = lambda b, j: b * nq + jnp.maximum(jnp.minimum(j, last) - 1, 0)
    specs = [
        pl.BlockSpec((QB, D), lambda b, j: (cur(b, j), 2)),
        pl.BlockSpec((QB, 256), lambda b, j: (cur(b, j), 12)),
        pl.BlockSpec((QB, 256), lambda b, j: (prev(b, j), 12)),
        pl.BlockSpec((QB, 256), lambda b, j: (cur(b, j), 13)),
        pl.BlockSpec((QB, 256), lambda b, j: (prev(b, j), 13)),
        pl.BlockSpec((QB, 512), lambda b, j: (cur(b, j), 7)),
        pl.BlockSpec((QB, 512), lambda b, j: (cur(b, j), 8)),
        SMEM_SPEC,
    ]
    return specs, cur, prev


def _attn_forward(proj, sinks, S, out_shards):
    T = proj.shape[0]
    nb, nq = T // S, S // QB
    specs, cur, _ = _attn_in_specs(S)
    nw = len(out_shards)

    def body(q_ref, kc_ref, kp_ref, vc_ref, vp_ref, gl_ref, gh_ref, sink_ref, *rest):
        shards = rest[:nw]
        y_ref = rest[nw]
        gathered = rest[nw + 1:2 * nw + 1]
        stage = rest[2 * nw + 1:3 * nw + 1]
        sems = rest[3 * nw + 1:]
        b, j = pl.program_id(0), pl.program_id(1)

        @pl.when((b == 0) & (j == 0))
        def _():
            for a in range(nw):
                stage[a][...] = shards[a][...].astype(bf16)
            _start_all(_direct_gather_copies(stage, gathered, *sems))

        @pl.when((b == nb - 1) & (j == nq - 1))
        def _():
            _wait_all(_direct_gather_copies(stage, gathered, *sems))

        bias = _window_bias(j == 0)
        kc, kp, vc, vp = kc_ref[...], kp_ref[...], vc_ref[...], vp_ref[...]
        for kv in range(KV_HEADS):
            lanes = slice(256 * kv, 256 * (kv + 1))
            hl = slice(HEAD * kv, HEAD * (kv + 1))
            q_rows = _heads_to_rows(q_ref[:, lanes])
            k_cat = jnp.concatenate([kp[:, hl], kc[:, hl]], axis=0)
            v_cat = jnp.concatenate([vp[:, hl], vc[:, hl]], axis=0)
            probs, _ = _attn_probs(q_rows, k_cat, _sink_column(sink_ref, kv), bias)
            o = _rows_to_heads(_dot(probs.astype(bf16), v_cat, NN))
            g_src = gl_ref if kv < 2 else gh_ref
            g = g_src[:, 256 * (kv % 2):256 * (kv % 2 + 1)].astype(f32)
            y_ref[:, lanes] = (o * (g * _sigmoid(g))).astype(bf16)

    args = [proj] * 7 + [sinks] + list(out_shards)
    res = _pcall(
        body, name="attn_forward", grid=(nb, nq),
        in_specs=specs + [pl.BlockSpec(w.shape, lambda b, j: (0, 0)) for w in out_shards],
        out_specs=(pl.BlockSpec((QB, D), lambda b, j: (cur(b, j), 0)),) + tuple([ANY] * nw),
        out_shape=(_sds((T, D), bf16),) + tuple(_sds((NDEV * w.shape[0], w.shape[1]), bf16) for w in out_shards),
        scratch_shapes=[pltpu.VMEM(w.shape, bf16) for w in out_shards] + _exchange_scratch(nw, 7),
        compiler_params=_params(("arbitrary", "arbitrary")),
    )(*args)
    return res[0], res[1:]


def _merge_and_head(x2d, tgt, proj, y_rnn, y_attn, w_r, w_a, w_o, gfin):
    T = x2d.shape[0]
    tb = min(T, 512)
    nsteps = T // tb

    def body(x_ref, t_ref, mr0, mr1, ma0, ma1, yr_ref, ya_ref, wr_ref, wa_ref, wo_ref, gf_ref,
             dx2_ref, dyr_ref, dya_ref, dmr_ref, dma_ref, loss_ref, gfin_ref, gwr_out, gwa_out, gwo_out,
             gwr_acc, gwa_acc, gwo_acc, out_sems):
        step = pl.program_id(0)

        @pl.when(step == 0)
        def _():
            loss_ref[...] = jnp.zeros_like(loss_ref)
            gfin_ref[...] = jnp.zeros_like(gfin_ref)
            gwr_acc[...] = jnp.zeros_like(gwr_acc)
            gwa_acc[...] = jnp.zeros_like(gwa_acc)
            gwo_acc[...] = jnp.zeros_like(gwo_acc)

        sr = _sigmoid(jnp.concatenate([mr0[...], mr1[...]], axis=1).astype(f32))
        sa = _sigmoid(jnp.concatenate([ma0[...], ma1[...]], axis=1).astype(f32))
        p_r = _dot(yr_ref[...], wr_ref[...], NN)
        p_a = _dot(ya_ref[...], wa_ref[...], NN)
        merged = (sr * p_r + sa * p_a).astype(bf16)
        x2 = x_ref[...] + _dot(merged, wo_ref[...], NN)
        rstd = lax.rsqrt(jnp.mean(x2 * x2, axis=-1, keepdims=True) + EPS)
        xh = x2 * rstd
        gf = gf_ref[...]
        err = xh * gf - t_ref[...]
        loss_ref[...] += jnp.sum(err * err)
        dy = err * (1.0 / D)
        gfin_ref[0:1, :] += jnp.sum(dy * xh, axis=0, keepdims=True)
        dxn = dy * gf
        dx2 = rstd * (dxn - xh * jnp.mean(dxn * xh, axis=-1, keepdims=True))
        dx2_ref[...] = dx2
        dx2b = dx2.astype(bf16)
        dmerged = _dot(dx2b, wo_ref[...], NT)
        dmr_ref[...] = (dmerged * p_r * (sr * (1.0 - sr))).astype(bf16)
        dma_ref[...] = (dmerged * p_a * (sa * (1.0 - sa))).astype(bf16)
        dpr = (dmerged * sr).astype(bf16)
        dpa = (dmerged * sa).astype(bf16)
        dyr_ref[...] = _dot(dpr, wr_ref[...], NT).astype(bf16)
        dya_ref[...] = _dot(dpa, wa_ref[...], NT).astype(bf16)
        gwr_acc[...] += _dot(yr_ref[...], dpr, TN)
        gwa_acc[...] += _dot(ya_ref[...], dpa, TN)
        gwo_acc[...] += _dot(merged, dx2b, TN)

        @pl.when(step == nsteps - 1)
        def _():
            copies = [pltpu.make_async_copy(src, dst, out_sems.at[k]) for k, (src, dst) in enumerate(
                ((gwr_acc, gwr_out), (gwa_acc, gwa_out), (gwo_acc, gwo_out)))]
            for cp in copies:
                cp.start()
            for cp in copies:
                cp.wait()

    tok = pl.BlockSpec((tb, D), lambda i: (i, 0))
    half = lambda c: pl.BlockSpec((tb, CH), lambda i, c=c: (i, c))
    wfull = pl.BlockSpec((D, D), lambda i: (0, 0), pipeline_mode=pl.Buffered(1))
    acc = pl.BlockSpec((8, D), lambda i: (0, 0))
    return _pcall(
        body, name="merge_and_head", grid=(nsteps,),
        in_specs=[tok, tok, half(9), half(10), half(11), half(12), tok, tok, wfull, wfull, wfull,
                  pl.BlockSpec((1, D), lambda i: (0, 0))],
        out_specs=(tok, tok, tok, tok, tok, acc, acc, ANY, ANY, ANY),
        out_shape=(_sds((T, D), f32), _sds((T, D), bf16), _sds((T, D), bf16), _sds((T, D), bf16),
                   _sds((T, D), bf16), _sds((8, D), f32), _sds((8, D), f32),
                   _sds((D, D), f32), _sds((D, D), f32), _sds((D, D), f32)),
        scratch_shapes=[pltpu.VMEM((D, D), f32)] * 3 + [pltpu.SemaphoreType.DMA((3,))],
        compiler_params=_params(("arbitrary",)),
    )(x2d, tgt, proj, proj, proj, proj, y_rnn, y_attn, w_r, w_a, w_o, gfin)


def _attn_backward(proj, dy_attn, tabs, sinks, S, chip_sums):
    T = proj.shape[0]
    nb, nq = T // S, S // QB
    nex = len(chip_sums)
    specs, cur, prev = _attn_in_specs(S)
    last = nq - 1
    tab_cur = pl.BlockSpec((QB, 128), lambda b, j: (jnp.minimum(j, last), 0))
    tab_prev = pl.BlockSpec((QB, 128), lambda b, j: (jnp.maximum(jnp.minimum(j, last) - 1, 0), 0))
    specs = specs + [pl.BlockSpec((QB, D), lambda b, j: (cur(b, j), 0))] + [tab_cur] * 3 + [tab_prev] * 3
    q_scale = 1.0 / math.sqrt(HEAD)

    def rope_back(dt, tab):
        return jnp.concatenate([_rope_transposed(dt[:, 128 * l:128 * (l + 1)], *tab) for l in range(2)], axis=1)

    def body(q_ref, kc_ref, kp_ref, vc_ref, vp_ref, gl_ref, gh_ref, sink_ref, dy_ref, cc, s1c, s2c, cp, s1p, s2p,
             *rest):
        ex_src = rest[:nex]
        dq_ref, dkv_ref, dg_ref, dsink_ref = rest[nex:nex + 4]
        ex_dst = rest[nex + 4:2 * nex + 4]
        carry_k, carry_v = rest[2 * nex + 4:2 * nex + 6]
        sems = rest[2 * nex + 6:]
        b, j = pl.program_id(0), pl.program_id(1)

        @pl.when((b == 0) & (j == 0))
        def _():
            dsink_ref[...] = jnp.zeros_like(dsink_ref)
            _start_all(_chip_exchange_copies(ex_src, ex_dst, *sems))

        @pl.when((b == nb - 1) & (j == nq))
        def _():
            _wait_all(_chip_exchange_copies(ex_src, ex_dst, *sems))

        @pl.when(j == 0)
        def _():
            carry_k[...] = jnp.zeros_like(carry_k)
            carry_v[...] = jnp.zeros_like(carry_v)

        @pl.when(j < nq)
        def _():
            bias = _window_bias(j == 0)
            tc = (cc[...], s1c[...], s2c[...])
            tp = (cp[...], s1p[...], s2p[...])
            kc, kp, vc, vp = kc_ref[...], kp_ref[...], vc_ref[...], vp_ref[...]
            dk_prev, dk_cur, dv_prev, dv_cur = [], [], [], []
            dsink_acc = jnp.zeros((8, 128), f32)
            r8 = lax.broadcasted_iota(jnp.int32, (8, 128), 0)
            l8 = lax.broadcasted_iota(jnp.int32, (8, 128), 1)
            for kv in range(KV_HEADS):
                lanes = slice(256 * kv, 256 * (kv + 1))
                hl = slice(HEAD * kv, HEAD * (kv + 1))
                q_rows = _heads_to_rows(q_ref[:, lanes])
                k_cat = jnp.concatenate([kp[:, hl], kc[:, hl]], axis=0)
                v_cat = jnp.concatenate([vp[:, hl], vc[:, hl]], axis=0)
                probs, p_sink = _attn_probs(q_rows, k_cat, _sink_column(sink_ref, kv), bias)
                pb = probs.astype(bf16)
                o = _rows_to_heads(_dot(pb, v_cat, NN))
                g_src = gl_ref if kv < 2 else gh_ref
                g = g_src[:, 256 * (kv % 2):256 * (kv % 2 + 1)].astype(f32)
                sg = _sigmoid(g)
                dy = dy_ref[:, lanes].astype(f32)
                dg_ref[:, lanes] = (dy * o * (sg * (1.0 + g * (1.0 - sg)))).astype(bf16)
                do_rows = _heads_to_rows(dy * (g * sg)).astype(bf16)
                dv = _dot(pb, do_rows, TN)
                dp = _dot(do_rows, v_cat, NT)
                rowdot = jnp.sum(probs * dp, axis=1, keepdims=True)
                ds = (probs * (dp - rowdot)).astype(bf16)
                sink_rows = -(p_sink * rowdot)
                for h in range(GROUP):
                    val = jnp.sum(sink_rows[QB * h:QB * (h + 1), :])
                    dsink_acc = dsink_acc + jnp.where((r8 == 0) & (l8 == GROUP * kv + h), val, 0.0)
                dq = _rows_to_heads(_dot(ds, k_cat, NN)) * q_scale
                dq_ref[:, lanes] = rope_back(dq, tc).astype(bf16)
                dk = _dot(ds, q_rows, TN)
                dk_prev.append(dk[:QB, :])
                dk_cur.append(dk[QB:, :])
                dv_prev.append(dv[:QB, :])
                dv_cur.append(dv[QB:, :])
            dsink_ref[...] += dsink_acc
            dkp = rope_back(jnp.concatenate(dk_prev, axis=1), tp)
            dkc = rope_back(jnp.concatenate(dk_cur, axis=1), tc)
            dkv_ref[:, 0:256] = (carry_k[...] + dkp).astype(bf16)
            dkv_ref[:, 256:512] = (carry_v[...] + jnp.concatenate(dv_prev, axis=1)).astype(bf16)
            carry_k[...] = dkc
            carry_v[...] = jnp.concatenate(dv_cur, axis=1)

        @pl.when(j == nq)
        def _():
            dkv_ref[:, 0:256] = carry_k[...].astype(bf16)
            dkv_ref[:, 256:512] = carry_v[...].astype(bf16)

    lag = lambda b, j: (b * nq + jnp.maximum(j - 1, 0), 0)
    args = [proj] * 7 + [sinks, dy_attn] + list(tabs) + list(tabs) + list(chip_sums)
    res = _pcall(
        body, name="attn_backward", grid=(nb, nq + 1), in_specs=specs + [ANY] * nex,
        out_specs=(pl.BlockSpec((QB, D), lambda b, j: (cur(b, j), 0)), pl.BlockSpec((QB, 512), lag),
                   pl.BlockSpec((QB, D), lambda b, j: (cur(b, j), 0)), pl.BlockSpec((8, 128), lambda b, j: (0, 0)))
        + tuple([ANY] * nex),
        out_shape=(_sds((T, D), bf16), _sds((T, 512), bf16), _sds((T, D), bf16), _sds((8, 128), f32))
        + tuple(_sds(s.shape, s.dtype) for s in chip_sums),
        scratch_shapes=[pltpu.VMEM((QB, 256), f32), pltpu.VMEM((QB, 256), f32)] + _exchange_scratch(nex, 3),
        compiler_params=_params(("arbitrary", "arbitrary")),
    )(*args)
    return res[:4], res[4:]


def _lru_backward(proj, h_all, dy_rnn, cw_full, conv_b, w_a, b_a, w_x, b_x, lam, S):
    T = proj.shape[0]
    nb = T // S
    col, vec, wblk, cwblk = _lru_specs(S, nb)
    tokblk = pl.BlockSpec((S, RB), lambda n, b: (b, n))

    def body(x0_ref, g_ref, h_ref, dy_ref, cw_ref, cb_ref, wa_ref, ba_ref, wx_ref, bx_ref, lam_ref,
             du0_ref, dg_ref, gwa_ref, gwx_ref, vec_ref, gcw_ref, a_s, b_s, dh_s, edge_s):
        @pl.when(pl.program_id(1) == 0)
        def _():
            gwa_ref[...] = jnp.zeros_like(gwa_ref)
            gwx_ref[...] = jnp.zeros_like(gwx_ref)
            vec_ref[...] = jnp.zeros_like(vec_ref)
            gcw_ref[...] = jnp.zeros_like(gcw_ref)

        x0 = x0_ref[...].astype(f32)
        cw = cw_ref[...]
        lam_v = lam_ref[...]
        u, ub, r, i, sp, a, mult, inv_mult, taps = _lru_gates(x0, cw, cb_ref[...], wa_ref[...], ba_ref[...],
                                                              wx_ref[...], bx_ref[...], lam_v)
        h = h_ref[...]
        g = g_ref[...].astype(f32)
        dy = dy_ref[...].astype(f32)
        sg = _sigmoid(g)
        dg_ref[...] = (dy * h * (sg * (1.0 + g * (1.0 - sg)))).astype(bf16)
        _linear_scan(_shift_up(a, 1), dy * (g * sg), a_s, b_s, edge_s, dh_s, reverse=True)
        dh_total = dh_s[...]
        da = dh_total * _shift_down(h, 1)
        dmult = dh_total * (i * u)
        db = dh_total * mult
        di = db * u
        du = db * i
        dlog_a_c = ((-LRU_C) * a) * (da - dmult * (a * inv_mult))
        dr = dlog_a_c * sp
        dsp = jnp.sum(dlog_a_c * r, axis=0, keepdims=True)
        dpre_r = dr * r * (1.0 - r)
        dpre_i = di * i * (1.0 - i)
        dpre_rb = dpre_r.astype(bf16)
        dpre_ib = dpre_i.astype(bf16)
        du = du + _dot(dpre_rb, wa_ref[...].astype(bf16), NT) + _dot(dpre_ib, wx_ref[...].astype(bf16), NT)
        gwa_ref[...] += _dot(ub, dpre_rb, TN)
        gwx_ref[...] += _dot(ub, dpre_ib, TN)
        vec_ref[0:1, :] += jnp.sum(du, axis=0, keepdims=True)
        vec_ref[1:2, :] += jnp.sum(dpre_r, axis=0, keepdims=True)
        vec_ref[2:3, :] += jnp.sum(dpre_i, axis=0, keepdims=True)
        vec_ref[3:4, :] += dsp * (-_sigmoid(-lam_v))
        dx0 = cw[3:4, :] * du
        for k in range(3):
            dx0 = dx0 + cw[k:k + 1, :] * _shift_up(du, 3 - k)
        for k in range(4):
            gcw_ref[k:k + 1, :] += jnp.sum(du * taps[k], axis=0, keepdims=True)
        du0_ref[...] = dx0.astype(bf16)

    wacc = pl.BlockSpec((RB, RB), lambda n, b: (0, n))
    vacc = pl.BlockSpec((8, RB), lambda n, b: (0, n))
    cacc = pl.BlockSpec((8, RB), lambda n, b: (n, 0))
    return _pcall(
        body, name="lru_backward", grid=(RNN_BLOCKS, nb),
        in_specs=[col(0), col(8), tokblk, tokblk, cwblk, vec, wblk, vec, wblk, vec, vec],
        out_specs=(tokblk, tokblk, wacc, wacc, vacc, cacc),
        out_shape=(_sds((T, D), bf16), _sds((T, D), bf16), _sds((RB, D), f32), _sds((RB, D), f32),
                   _sds((8, D), f32), _sds((8 * RNN_BLOCKS, RB), f32)),
        scratch_shapes=[pltpu.VMEM((S, RB), f32)] * 3 + [pltpu.VMEM((S // 8, RB), f32)],
        compiler_params=_params(("arbitrary", "arbitrary")),
    )(proj, proj, h_all, dy_rnn, cw_full, conv_b, w_a, b_a, w_x, b_x, lam)


def _section_of_chunk(s):
    out = []
    for start, n in zip(SEC_START, SEC_CHUNKS):
        inside = (s >= start) & (s < start + n)
        out.append((inside, jnp.clip(s - start, 0, n - 1)))
    return out


EFFECT = pltpu.SideEffectType.DATAFLOW_SIDE_EFFECTING
HBM_SPEC = pl.BlockSpec(memory_space=pltpu.HBM)
SEM_SPEC = pl.BlockSpec(memory_space=pltpu.SEMAPHORE)


def _split_exchange_copies(src_ref, land_ref, send_sems, recv_sems):
    x, y, c = _my_place()
    copies = []
    for k in (3, 1, 2):
        px, py = (x + (k >> 1)) % 2, (y + (k & 1)) % 2
        copies.append(pltpu.make_async_remote_copy(
            src_ref=src_ref.at[2 * px + py], dst_ref=land_ref.at[k - 1], send_sem=send_sems[k - 1],
            recv_sem=recv_sems[k - 1], device_id=(px, py, c), device_id_type=MESH))
    return copies


def _exchange_start(chip_sum):
    _, r, cols = chip_sum.shape

    def body(src_ref, land_ref, s0, s1, s2, r0, r1, r2, src_thru, land_thru, token):
        for cp in _split_exchange_copies(src_ref, land_ref, (s0, s1, s2), (r0, r1, r2)):
            cp.start()
        token[...] = jnp.zeros_like(token)

    land = pltpu.with_memory_space_constraint(lax.empty((3, r, cols), chip_sum.dtype), pltpu.HBM)
    res = _pcall(
        body, name="exchange_start",
        out_shape=tuple([pltpu.SemaphoreType.DMA(())] * 6) + (
            pltpu.HBM(chip_sum.shape, chip_sum.dtype), pltpu.HBM((3, r, cols), chip_sum.dtype), _sds((8, 128), f32)),
        in_specs=(HBM_SPEC, HBM_SPEC), out_specs=tuple([SEM_SPEC] * 6) + (HBM_SPEC, HBM_SPEC, VMEM_SPEC),
        input_output_aliases={0: 6, 1: 7},
        compiler_params=pltpu.CompilerParams(has_side_effects=EFFECT),
    )(pltpu.with_memory_space_constraint(chip_sum, pltpu.HBM), land)
    return res[:6], res[6], res[7], res[8]


def _exchange_wait(sems, src_thru, land_thru, after):
    def body(src_ref, land_ref, s0, s1, s2, r0, r1, r2, after_ref, src_dead, got_ref):
        for cp in _split_exchange_copies(src_ref, land_ref, (s0, s1, s2), (r0, r1, r2)):
            cp.wait_send()
            cp.wait_recv()

    return _pcall(
        body, name="exchange_wait",
        out_shape=(pltpu.HBM(src_thru.shape, src_thru.dtype), pltpu.HBM(land_thru.shape, land_thru.dtype)),
        in_specs=(HBM_SPEC, HBM_SPEC) + tuple([SEM_SPEC] * 6) + (ANY,), out_specs=(HBM_SPEC, HBM_SPEC),
        input_output_aliases={0: 0, 1: 1},
        compiler_params=pltpu.CompilerParams(has_side_effects=EFFECT),
    )(src_thru, land_thru, *sems, after)[1]


def _input_grad(dsecs, wt_full, x2d, dx2, norm_g):
    T = x2d.shape[0]
    tb = min(T, 512)
    nsec = len(dsecs)
    ntok = T // tb

    def body(*refs):
        secs = refs[:nsec]
        wt_ref, x_ref, dx2_ref, g_ref, dx_ref, gnorm_ref = refs[nsec:]
        i = pl.program_id(0)

        @pl.when(i == 0)
        def _():
            gnorm_ref[...] = jnp.zeros_like(gnorm_ref)

        dh = None
        for a, (start, n) in enumerate(zip(SEC_START, SEC_CHUNKS)):
            part = _dot(secs[a][...], wt_ref[CH * start:CH * (start + n), :], NN)
            dh = part if dh is None else dh + part
        xv = x_ref[...]
        rstd = lax.rsqrt(jnp.mean(xv * xv, axis=-1, keepdims=True) + EPS)
        xh = xv * rstd
        gnorm_ref[0:1, :] += jnp.sum(dh * xh, axis=0, keepdims=True)
        dxn = dh * g_ref[...]
        dx_ref[...] = dx2_ref[...] + rstd * (dxn - xh * jnp.mean(dxn * xh, axis=-1, keepdims=True))

    tok = pl.BlockSpec((tb, D), lambda i: (i, 0))
    return _pcall(
        body, name="input_grad", grid=(ntok,),
        in_specs=[pl.BlockSpec((tb, sec.shape[1]), lambda i: (i, 0)) for sec in dsecs]
        + [pl.BlockSpec((D_IN, D), lambda i: (0, 0), pipeline_mode=pl.Buffered(1)), tok, tok,
           pl.BlockSpec((1, D), lambda i: (0, 0))],
        out_specs=(tok, pl.BlockSpec((8, D), lambda i: (0, 0))),
        out_shape=(_sds((T, D), f32), _sds((8, D), f32)),
        compiler_params=_params(("arbitrary",)),
    )(*dsecs, wt_full, x2d, dx2, norm_g)


def _w_in_grad(dsecs, h_bf):
    T = h_bf.shape[0]
    tk = min(T, 2048)
    nchunks = D_IN // CH
    nsec = len(dsecs)
    nt = T // tk

    def body(*refs):
        secs = refs[:nsec]
        h_ref, out_ref, acc = refs[nsec:]
        s, t = pl.program_id(0), pl.program_id(1)

        @pl.when(t == 0)
        def _():
            acc[...] = jnp.zeros_like(acc)

        h_rows = h_ref[pl.ds(pl.multiple_of(t * tk, tk), tk), :]
        for a, (start, n) in enumerate(zip(SEC_START, SEC_CHUNKS)):
            @pl.when((s >= start) & (s < start + n))
            def _(a=a):
                acc[...] += _dot(secs[a][...], h_rows, TN)

        @pl.when(t == nt - 1)
        def _():
            out_ref[...] = acc[...].astype(bf16)

    def sec_spec(a):
        def index(s, t, a=a):
            inside, local = _section_of_chunk(s)[a]
            return (jnp.where(inside, t, 0), local)
        return pl.BlockSpec((tk, CH), index)

    return _pcall(
        body, name="w_in_grad", grid=(nchunks, T // tk),
        in_specs=[sec_spec(a) for a in range(nsec)]
        + [pl.BlockSpec((T, D), lambda s, t: (0, 0), pipeline_mode=pl.Buffered(1))],
        out_specs=pl.BlockSpec((CH, D), lambda s, t: (s, 0)), out_shape=_sds((D_IN, D), bf16),
        scratch_shapes=[pltpu.VMEM((CH, D), f32)],
        compiler_params=_params(("arbitrary", "arbitrary")),
    )(*dsecs, h_bf)


SMALL_NAMES = ("lru_w_a", "lru_w_x", "conv_b", "lru_b_a", "lru_b_x", "lru_lambda", "norm_g", "final_norm_g",
               "attn_sinks", "conv_w")
MISC_ROW = {"conv_b": 0, "lru_b_a": 1, "lru_b_x": 2, "lru_lambda": 3, "norm_g": 8, "final_norm_g": 16,
            "attn_sinks": 24, "loss": 32}


def _small_step(gwa, gwx, gvec, gnorm_blk, gfin_blk, dsink_blk, loss_blk, gcw, params):
    srcs_rows = (RB // NDEV, RB // NDEV, 8, 8)
    flat = [t for n in SMALL_NAMES for t in params[n]]
    nout = 4 * len(SMALL_NAMES) + 1

    ra_, rx_, rm_, rc_ = srcs_rows
    rh, rf = ra_ + rx_, rm_ + rc_

    def reduce_body(gwa_ref, gwx_ref, gvec_ref, gnorm_ref, gfin_ref, dsink_ref, loss_ref, gcw_ref,
                    all_a, all_x, all_m, conv_out,
                    misc, out_h, out_f, in_h, in_f, mine_h, mine_f, every_h, every_f, sa, ra, sb, rb):
        x, y, c = _my_place()
        me = 4 * x + 2 * y + c

        misc[...] = jnp.zeros_like(misc)
        misc[0:8, :] = gvec_ref[...]
        misc[8:16, :] = gnorm_ref[...]
        misc[16:24, :] = gfin_ref[...]
        misc[24:32, 0:128] = dsink_ref[...]
        misc[32:40, :] = loss_ref[...]

        out_f[...] = jnp.zeros_like(out_f)
        for d in range(NDEV):
            out_h[d, 0:ra_, :] = gwa_ref[ra_ * d:ra_ * (d + 1), :].astype(bf16)
            out_h[d, ra_:rh, :] = gwx_ref[rx_ * d:rx_ * (d + 1), :].astype(bf16)
            out_f[d, 0:rm_, :] = misc[rm_ * d:rm_ * (d + 1), :]
            out_f[d, rm_:rf, 0:RB] = gcw_ref[rc_ * d:rc_ * (d + 1), :]

        def both(k, src_h, dst_h, src_f, dst_f, send, recv, peer):
            return [pltpu.make_async_remote_copy(src_ref=s_, dst_ref=d_, send_sem=send.at[2 * (k - 1) + t],
                                                 recv_sem=recv.at[2 * (k - 1) + t], device_id=peer,
                                                 device_id_type=MESH)
                    for t, (s_, d_) in enumerate(((src_h, dst_h), (src_f, dst_f)))]

        scatter = []
        for k in range(1, NDEV):
            px, py, pc = _peer(k)
            dev = 4 * px + 2 * py + pc
            scatter += both(k, out_h.at[dev], in_h.at[k - 1], out_f.at[dev], in_f.at[k - 1], sa, ra, (px, py, pc))
        for cp in scatter:
            cp.start()
        for cp in scatter:
            cp.wait()

        total_h = out_h[me].astype(f32)
        total_f = out_f[me]
        for k in range(NDEV - 1):
            total_h = total_h + in_h[k].astype(f32)
            total_f = total_f + in_f[k]
        conv_out[...] = total_f[rm_:rf, 0:RB]
        mine_h[...] = total_h.astype(bf16)
        mine_f[...] = total_f[0:rm_, :]
        every_h[me] = total_h.astype(bf16)
        every_f[me] = total_f[0:rm_, :]
        gather = []
        for k in range(1, NDEV):
            gather += both(k, mine_h, every_h.at[me], mine_f, every_f.at[me], sb, rb, _peer(k))
        for cp in gather:
            cp.start()
        for cp in gather:
            cp.wait()
        for d in range(NDEV):
            all_a[ra_ * d:ra_ * (d + 1), :] = every_h[d, 0:ra_, :].astype(f32)
            all_x[rx_ * d:rx_ * (d + 1), :] = every_h[d, ra_:rh, :].astype(f32)
            all_m[rm_ * d:rm_ * (d + 1), :] = every_f[d]

    def adam_body(*refs):
        all_a, all_x, all_m, conv_ref = refs[:4]
        prm = {n: refs[4 + 3 * k:7 + 3 * k] for k, n in enumerate(SMALL_NAMES)}
        nin = 4 + len(flat)
        outs = {n: refs[nin + 4 * k:nin + 4 * k + 4] for k, n in enumerate(SMALL_NAMES)}
        loss_out = refs[nin + nout - 1]
        g_conv = conv_ref[0:4, :]

        def update(name, g, pick=lambda r: r[...]):
            w_ref, m_ref, v_ref = prm[name]
            delta, m_new, v_new = _adam_math(g, pick(w_ref), pick(m_ref), pick(v_ref))
            return g, delta, m_new, v_new

        for n in range(RNN_BLOCKS):
            lanes = slice(RB * n, RB * (n + 1))
            for name, full in (("lru_w_a", all_a), ("lru_w_x", all_x)):
                for out, val in zip(outs[name], update(name, full[:, lanes], pick=lambda r, n=n: r[n])):
                    out[n] = val
        for name in ("conv_b", "lru_b_a", "lru_b_x", "lru_lambda", "norm_g", "final_norm_g"):
            row = MISC_ROW[name]
            for out, val in zip(outs[name], update(name, all_m[row:row + 1, :])):
                out[...] = val
        row = MISC_ROW["attn_sinks"]
        for out, val in zip(outs["attn_sinks"], update("attn_sinks", all_m[row:row + 1, 0:16])):
            out[...] = val
        for out, val in zip(outs["conv_w"], update("conv_w", g_conv)):
            out[...] = val
        row = MISC_ROW["loss"]
        loss_out[...] = all_m[row:row + 8, 0:128] * (0.5 / D)

    scratch = [pltpu.VMEM((64, D), f32), pltpu.VMEM((NDEV, rh, D), bf16), pltpu.VMEM((NDEV, rf, D), f32),
               pltpu.VMEM((NDEV - 1, rh, D), bf16), pltpu.VMEM((NDEV - 1, rf, D), f32),
               pltpu.VMEM((rh, D), bf16), pltpu.VMEM((rm_, D), f32),
               pltpu.VMEM((NDEV, rh, D), bf16), pltpu.VMEM((NDEV, rm_, D), f32)
               ] + [pltpu.SemaphoreType.DMA((2 * (NDEV - 1),))] * 4
    sums = _pcall(
        reduce_body, name="small_reduce",
        out_shape=(_sds((RB, D), f32), _sds((RB, D), f32), _sds((64, D), f32), _sds((8, RB), f32)),
        in_specs=[VMEM_SPEC] * 8, out_specs=tuple([VMEM_SPEC] * 4),
        scratch_shapes=scratch, compiler_params=_params(),
    )(gwa, gwx, gvec, gnorm_blk, gfin_blk, dsink_blk, loss_blk, gcw)
    out_shape = tuple(_sds(params[n][0].shape, f32) for n in SMALL_NAMES for _ in range(4)) + (_sds((8, 128), f32),)
    res = _pcall(
        adam_body, name="small_adamw", out_shape=out_shape,
        in_specs=[VMEM_SPEC] * (4 + len(flat)), out_specs=tuple([VMEM_SPEC] * nout), compiler_params=_params(),
    )(*sums, *flat)
    return {n: res[4 * k:4 * k + 4] for k, n in enumerate(SMALL_NAMES)}, res[-1]


def _pad_rows(v, rows=8):
    return jnp.concatenate([v, jnp.zeros((rows - v.shape[0], v.shape[1]), v.dtype)], axis=0)


def kernel(x, norm_g, w_in, conv_w, conv_b, lru_w_a, lru_b_a, lru_w_x, lru_b_x, lru_lambda, attn_sinks, w_rnn_out, w_attn_out, w_o, final_norm_g, loss_target, m_norm_g, m_w_in, m_conv_w, m_conv_b, m_lru_w_a, m_lru_b_a, m_lru_w_x, m_lru_b_x, m_lru_lambda, m_attn_sinks, m_w_rnn_out, m_w_attn_out, m_w_o, m_final_norm_g, v_norm_g, v_w_in, v_conv_w, v_conv_b, v_lru_w_a, v_lru_b_a, v_lru_w_x, v_lru_b_x, v_lru_lambda, v_attn_sinks, v_w_rnn_out, v_w_attn_out, v_w_o, v_final_norm_g):
    nb, S, _ = x.shape
    T = nb * S
    x2d = x.reshape(T, D)
    tgt = loss_target.reshape(T, D)
    fin_g = final_norm_g.reshape(1, D)
    w_a3, w_x3 = lru_w_a[0], lru_w_x[0]

    my_core = lax.axis_index("c").astype(jnp.int32).reshape(1)
    cx, cy = lax.axis_index("x"), lax.axis_index("y")
    chip_order = jnp.stack([2 * cx + cy, 2 * (1 - cx) + cy, 2 * cx + (1 - cy),
                            2 * (1 - cx) + (1 - cy)]).astype(jnp.int32)

    tabs = _rope_tables(S)
    h_bf, proj, wt_full, cw_full, _ = _in_proj_gather(
        x2d, norm_g, w_in[0].T.astype(bf16), _pad_rows(conv_w[0]), tabs, S, (), chip_order)
    y_rnn, h_all = _lru_forward(proj, cw_full, conv_b, w_a3, lru_b_a, w_x3, lru_b_x, lru_lambda, S)
    y_attn, (wr_full, wa_full, wo_full) = _attn_forward(proj, attn_sinks, S,
                                                        (w_rnn_out[0], w_attn_out[0], w_o[0]))

    (dx2, dy_rnn, dy_attn, dmr, dma, loss_blk, gfin_blk, g_wr, g_wa, g_wo) = _merge_and_head(
        x2d, tgt, proj, y_rnn, y_attn, wr_full, wa_full, wo_full, fin_g)
    sums_out = _pair_sums([g_wr, g_wa, g_wo], bf16, my_core, "out")

    (dq, dkv, dga, dsink_blk), (p_wr, p_wa, p_wo) = _attn_backward(proj, dy_attn, tabs, attn_sinks, S, sums_out)
    du0, dgr, gwa, gwx, gvec, gcw = _lru_backward(proj, h_all, dy_rnn, cw_full, conv_b, w_a3, lru_b_a, w_x3,
                                                  lru_b_x, lru_lambda, S)
    dsecs = (du0, dgr, dq, dkv, dga, dmr, dma)

    g_wt = _w_in_grad(dsecs, h_bf)
    (sum_in,) = _pair_sums([g_wt], bf16, my_core, "in")
    ex_sems, sum_in, landing, token = _exchange_start(sum_in)
    grad_x2d, gnorm_blk = _input_grad(dsecs, wt_full, x2d, dx2, norm_g + token[0, 0])
    p_wt = _exchange_wait(ex_sems, sum_in, landing, gnorm_blk)
    p_wt_own = lax.dynamic_index_in_dim(sum_in, 2 * cx + cy, axis=0, keepdims=False)

    small, loss_out = _small_step(gwa, gwx, gvec, gnorm_blk, gfin_blk, dsink_blk, loss_blk, gcw, {
        "lru_w_a": (w_a3, m_lru_w_a[0], v_lru_w_a[0]), "lru_w_x": (w_x3, m_lru_w_x[0], v_lru_w_x[0]),
        "conv_b": (conv_b, m_conv_b, v_conv_b), "lru_b_a": (lru_b_a, m_lru_b_a, v_lru_b_a),
        "lru_b_x": (lru_b_x, m_lru_b_x, v_lru_b_x), "lru_lambda": (lru_lambda, m_lru_lambda, v_lru_lambda),
        "norm_g": (norm_g, m_norm_g, v_norm_g),
        "final_norm_g": (fin_g, m_final_norm_g.reshape(1, D), v_final_norm_g.reshape(1, D)),
        "attn_sinks": (attn_sinks, m_attn_sinks, v_attn_sinks),
        "conv_w": (conv_w[0], m_conv_w[0], v_conv_w[0])})

    o_wt = _adamw(p_wt_own, p_wt, w_in[0].T, m_w_in[0].T, v_w_in[0].T, "adamw_w_in")
    o_wr, o_wa, o_wo = _adamw_group(
        (p_wr, p_wa, p_wo), (w_rnn_out[0], w_attn_out[0], w_o[0]),
        (m_w_rnn_out[0], m_w_attn_out[0], m_w_o[0]), (v_w_rnn_out[0], v_w_attn_out[0], v_w_o[0]), "adamw_w_out")

    def result(kind):
        d = {n: small[n][kind] for n in ("conv_b", "lru_b_a", "lru_b_x", "lru_lambda", "norm_g", "attn_sinks")}
        d.update({n: small[n][kind][None] for n in ("lru_w_a", "lru_w_x", "conv_w")})
        d["final_norm_g"] = small["final_norm_g"][kind].reshape(D)
        d.update({"w_in": o_wt[kind].T[None], "w_rnn_out": o_wr[kind][None], "w_attn_out": o_wa[kind][None],
                  "w_o": o_wo[kind][None]})
        return d

    order = ("norm_g", "w_in", "conv_w", "conv_b", "lru_w_a", "lru_b_a", "lru_w_x", "lru_b_x", "lru_lambda",
             "attn_sinks", "w_rnn_out", "w_attn_out", "w_o", "final_norm_g")
    outs = [loss_out[0, 0], grad_x2d.reshape(nb, S, D)]
    for kind in range(4):
        d = result(kind)
        outs += [d[n] for n in order]
    return tuple(outs)
```

```python
import math

import jax
import jax.numpy as jnp
from jax import lax
from jax.experimental import pallas as pl
from jax.experimental.pallas import tpu as pltpu

f32 = jnp.float32
bf16 = jnp.bfloat16

D = 1024
D_IN = 6656
NDEV = 8
RNN_BLOCKS = 8
RB = 128
HEAD = 64
KV_HEADS = 4
GROUP = 4
QB = 128
LRU_C = 8.0
EPS = 1e-6
ROPE_DIM = 16
ROPE_THETA = 500000.0
CH = 512
SEC_START = (0, 2, 4, 6, 7, 9, 11)
SEC_CHUNKS = (2, 2, 2, 1, 2, 2, 2)
VMEM_LIMIT = 62 * 1024 * 1024

ADAM_LR, ADAM_B1, ADAM_B2, ADAM_EPS, ADAM_WD, ADAM_STEP = 0.001, 0.9, 0.999, 1e-08, 0.01, 10

MESH = pl.DeviceIdType.MESH
ANY = pl.BlockSpec(memory_space=pl.ANY)
VMEM_SPEC = pl.BlockSpec(memory_space=pltpu.VMEM)
SMEM_SPEC = pl.BlockSpec(memory_space=pltpu.SMEM)


def _pcall(body, **kw):
    return pl.pallas_call(body, **kw)


def _params(sem=None, **kw):
    if sem is not None:
        kw["dimension_semantics"] = sem
    return pltpu.CompilerParams(vmem_limit_bytes=VMEM_LIMIT, **kw)


def _sds(shape, dtype):
    return jax.ShapeDtypeStruct(shape, dtype)


def _dot(a, b, dims):
    return lax.dot_general(a, b, (dims, ((), ())), preferred_element_type=f32)


NN = ((1,), (0,))
NT = ((1,), (1,))
TN = ((0,), (0,))


def _sigmoid(v):
    return 0.5 * jnp.tanh(0.5 * v) + 0.5


def _sigmoid_positive(v):
    return 1.0 / (1.0 + jnp.exp(-v))


def _my_place():
    return lax.axis_index("x"), lax.axis_index("y"), lax.axis_index("c")


def _peer(k):
    x, y, c = _my_place()
    return (x + ((k >> 2) & 1)) % 2, (y + ((k >> 1) & 1)) % 2, (c + (k & 1)) % 2


def _direct_gather_copies(srcs, outs, send_sems, recv_sems, local_sems):
    x, y, c = _my_place()
    me = 4 * x + 2 * y + c
    local, remote = [], []
    for a, (src, out) in enumerate(zip(srcs, outs)):
        r = src.shape[0]
        mine = out.at[pl.ds(pl.multiple_of(me * r, 8), r), :]
        local.append(pltpu.make_async_copy(src, mine, local_sems.at[a]))
        for k in range(1, NDEV):
            remote.append(pltpu.make_async_remote_copy(
                src_ref=src, dst_ref=mine, send_sem=send_sems.at[7 * a + k - 1], recv_sem=recv_sems.at[7 * a + k - 1],
                device_id=_peer(k), device_id_type=MESH))
    return local, remote


def _chip_exchange_copies(src, dst, send_sems, recv_sems, local_sems):
    x, y, c = _my_place()
    local, remote = [], []
    for a in range(len(src)):
        local.append(pltpu.make_async_copy(src[a].at[2 * x + y], dst[a].at[0], local_sems.at[a]))
    for k in (3, 1, 2):
        px, py = (x + (k >> 1)) % 2, (y + (k & 1)) % 2
        for a in range(len(src)):
            remote.append(pltpu.make_async_remote_copy(
                src_ref=src[a].at[2 * px + py], dst_ref=dst[a].at[k],
                send_sem=send_sems.at[3 * a + k - 1], recv_sem=recv_sems.at[3 * a + k - 1],
                device_id=(px, py, c), device_id_type=MESH))
    return local, remote


def _exchange_scratch(narr, per_array):
    return [pltpu.SemaphoreType.DMA((per_array * narr,)), pltpu.SemaphoreType.DMA((per_array * narr,)),
            pltpu.SemaphoreType.DMA((narr,))]


def _start_all(copies):
    local, remote = copies
    for cp in local + remote:
        cp.start()


def _wait_all(copies):
    local, remote = copies
    for cp in remote + local:
        cp.wait()


def _row_tile(rows, dtype):
    unit = 16 if dtype == bf16 else 8
    for cand in (256, 208, 128, 64, 40, 32, 16, 8):
        if rows % cand == 0 and cand % unit == 0:
            return cand
    return rows


def _pair_sums(grads, wire_dtype, my_core, tag):
    narr = len(grads)
    r, cols = grads[0].shape[0] // NDEV, grads[0].shape[1]
    views = [g.reshape(4, 2, r, cols) for g in grads]
    tr = _row_tile(r, wire_dtype)
    nt = r // tr

    def body(core_ref, *refs):
        mine = refs[:narr]
        whole = refs[narr:2 * narr]
        outs = refs[2 * narr:3 * narr]
        got = refs[3 * narr:4 * narr]
        send_sems, recv_sems = refs[4 * narr:]
        q, i = pl.program_id(0), pl.program_id(1)
        x, y, c = _my_place()

        def copy(a, chip):
            return pltpu.make_async_remote_copy(
                src_ref=whole[a].at[chip, 1 - c], dst_ref=got[a].at[chip],
                send_sem=send_sems.at[4 * a + chip], recv_sem=recv_sems.at[4 * a + chip],
                device_id=(x, y, 1 - c), device_id_type=MESH)

        @pl.when((q == 0) & (i == 0))
        def _():
            for chip in range(4):
                for a in range(narr):
                    copy(a, chip).start()

        for chip in range(4):
            @pl.when((q == chip) & (i == 0))
            def _(chip=chip):
                for a in range(narr):
                    copy(a, chip).wait_recv()

        rows = pl.ds(pl.multiple_of(i * tr, tr), tr)
        for a in range(narr):
            outs[a][...] = (mine[a][...].astype(f32) + got[a][q, rows, :].astype(f32)).astype(wire_dtype)

        @pl.when((q == 3) & (i == nt - 1))
        def _():
            for chip in range(4):
                for a in range(narr):
                    copy(a, chip).wait_send()

    slab = pl.BlockSpec((None, tr, cols), lambda q, i, core: (q, i, 0))
    grid_spec = pltpu.PrefetchScalarGridSpec(
        num_scalar_prefetch=1, grid=(4, nt),
        in_specs=[pl.BlockSpec((None, None, tr, cols), lambda q, i, core: (q, core[0], i, 0))] * narr + [ANY] * narr,
        out_specs=tuple([slab] * narr),
        scratch_shapes=[pltpu.VMEM((4, r, cols), grads[0].dtype)] * narr
        + [pltpu.SemaphoreType.DMA((4 * narr,)), pltpu.SemaphoreType.DMA((4 * narr,))])
    return _pcall(body, name="pair_sums_" + tag, grid_spec=grid_spec,
                  out_shape=tuple(_sds((4, r, cols), wire_dtype) for _ in range(narr)),
                  compiler_params=_params(("arbitrary", "arbitrary")))(my_core, *views, *views)


def _adam_math(g, w, m, v):
    m_new = ADAM_B1 * m + (1.0 - ADAM_B1) * g
    v_new = ADAM_B2 * v + (1.0 - ADAM_B2) * (g * g)
    m_hat = m_new / (1.0 - ADAM_B1 ** ADAM_STEP)
    v_hat = v_new / (1.0 - ADAM_B2 ** ADAM_STEP)
    return -ADAM_LR * (m_hat / (jnp.sqrt(v_hat) + ADAM_EPS) + ADAM_WD * w), m_new, v_new


def _adamw(first, parts, w, m, v, name):
    n, rows, cols = parts.shape
    tr = _row_tile(rows, parts.dtype)

    def body(f_ref, p_ref, w_ref, m_ref, v_ref, g_out, d_out, m_out, v_out):
        g = f_ref[...].astype(f32)
        for s in range(n):
            g = g + p_ref[s].astype(f32)
        g_out[...] = g
        d_out[...], m_out[...], v_out[...] = _adam_math(g, w_ref[...], m_ref[...], v_ref[...])

    blk = pl.BlockSpec((tr, cols), lambda i: (i, 0))
    return _pcall(
        body, name=name, grid=(rows // tr,),
        in_specs=[blk, pl.BlockSpec((n, tr, cols), lambda i: (0, i, 0)), blk, blk, blk],
        out_specs=(blk, blk, blk, blk), out_shape=tuple(_sds((rows, cols), f32) for _ in range(4)),
        compiler_params=_params(("arbitrary",)),
    )(first, parts, w, m, v)


def _adamw_group(parts, ws, ms, vs, name):
    nw = len(ws)

    def body(*refs):
        p_refs, w_refs, m_refs, v_refs = (refs[k * nw:(k + 1) * nw] for k in range(4))
        outs = refs[4 * nw:]
        for k in range(nw):
            g = p_refs[k][0].astype(f32)
            for s in range(1, p_refs[k].shape[0]):
                g = g + p_refs[k][s].astype(f32)
            g_out, d_out, m_out, v_out = outs[4 * k:4 * k + 4]
            g_out[...] = g
            d_out[...], m_out[...], v_out[...] = _adam_math(g, w_refs[k][...], m_refs[k][...], v_refs[k][...])

    res = _pcall(
        body, name=name, out_shape=tuple(_sds(w.shape, f32) for w in ws for _ in range(4)),
        in_specs=[VMEM_SPEC] * (4 * nw), out_specs=tuple([VMEM_SPEC] * (4 * nw)), compiler_params=_params(),
    )(*parts, *ws, *ms, *vs)
    return [res[4 * k:4 * k + 4] for k in range(nw)]


def _rope(t, c, s1, s2):
    w = t.shape[1]
    return t * c + pltpu.roll(t, w - 8, 1) * s1 + pltpu.roll(t, 8, 1) * s2


def _rope_transposed(dt, c, s1, s2):
    w = dt.shape[1]
    return dt * c + pltpu.roll(dt * s1, 8, 1) + pltpu.roll(dt * s2, w - 8, 1)


PAIR_ROWS = D_IN // 4
SUB_COLS = ((0, 512), (512, 512), (1024, 512), (1536, 128))
Q_SLABS = range(3, 11)
K_SLABS = range(11, 13)


def _in_proj_gather(x2d, norm_g, wt_shard, cw_shard, tabs, S, out_shards, chip_order):
    T = x2d.shape[0]
    tb = min(S, 1024)
    ntok = T // tb
    nsb = S // tb
    q_scale = 1.0 / math.sqrt(HEAD)
    shard_rows = wt_shard.shape[0]
    small = (cw_shard,) + tuple(out_shards)
    nsm = len(small)

    def body(order_ref, x_ref, g_ref, c_ref, s1_ref, s2_ref, wt_hbm, *rest):
        small_in = rest[:nsm]
        h_ref, proj_ref, wt_out = rest[nsm:nsm + 3]
        small_out = rest[nsm + 3:2 * nsm + 3]
        wt_vm, h_vm = rest[2 * nsm + 3:2 * nsm + 5]
        stage = rest[2 * nsm + 5:3 * nsm + 4]
        wsend, wrecv, wlocal = rest[3 * nsm + 4:3 * nsm + 7]
        dsems = rest[3 * nsm + 7:]
        jj, i = pl.program_id(0), pl.program_id(1)
        x, y, c = _my_place()
        me, sibling = (x, y, c), (x, y, 1 - c)
        chips = [(1 - x, y), (x, 1 - y), (1 - x, 1 - y)]

        def rows(place):
            px, py, pc = place
            return wt_vm.at[pl.ds(pl.multiple_of((4 * px + 2 * py + pc) * shard_rows, 16), shard_rows), :]

        def copy(k, block, to, src=None):
            return pltpu.make_async_remote_copy(
                src_ref=rows(block) if src is None else src, dst_ref=rows(block),
                send_sem=wsend.at[k], recv_sem=wrecv.at[k], device_id=to, device_id_type=MESH)

        def small_copies():
            srcs = (small_in[0],) + tuple(stage)
            return _direct_gather_copies(srcs, small_out, *dsems)

        own = pltpu.make_async_copy(wt_hbm, rows(me), wlocal.at[0])
        keep = pltpu.make_async_copy(wt_vm, wt_out, wlocal.at[1])

        @pl.when((jj == 0) & (i == 0))
        def _():
            own.start()
            copy(0, me, sibling, src=wt_hbm).start()
            for j, chip in enumerate(chips):
                copy(1 + j, me, (*chip, c), src=wt_hbm).start()
            for a in range(nsm - 1):
                stage[a][...] = small_in[1 + a][...].astype(bf16)
            _start_all(small_copies())
            own.wait()
            copy(0, sibling, me).wait_recv()

        for j, chip in enumerate(chips):
            @pl.when((jj == 1 + j) & (i == 0))
            def _(j=j, chip=chip):
                copy(1 + j, (*chip, c), me).wait_recv()
                copy(4 + j, (*chip, c), sibling).start()
                copy(4 + j, (*chip, 1 - c), me).wait_recv()

        @pl.when((jj == 3) & (i == 0))
        def _():
            keep.start()

        @pl.when((jj == 3) & (i == ntok - 1))
        def _():
            copy(0, me, sibling, src=wt_hbm).wait_send()
            for j, chip in enumerate(chips):
                copy(1 + j, me, (*chip, c), src=wt_hbm).wait_send()
                copy(4 + j, (*chip, c), sibling).wait_send()
            _wait_all(small_copies())
            keep.wait()

        tok = pl.ds(pl.multiple_of(i * tb, tb), tb)

        @pl.when(jj == 0)
        def _():
            xv = x_ref[...]
            ms = jnp.mean(xv * xv, axis=-1, keepdims=True)
            hb = (xv * lax.rsqrt(ms + EPS) * g_ref[...]).astype(bf16)
            h_ref[...] = hb
            h_vm[tok, :] = hb

        block = order_ref[jj]
        hb = h_vm[tok, :]

        def piece(c0, w):
            w_rows = wt_vm[pl.ds(pl.multiple_of(block * PAIR_ROWS + c0, 128), w), :]
            return _dot(hb, w_rows, NT)

        @pl.when(block != 1)
        def _():
            for c0, w in SUB_COLS:
                proj_ref[:, c0:c0 + w] = piece(c0, w).astype(bf16)

        @pl.when(block == 1)
        def _():
            tab = (c_ref[...], s1_ref[...], s2_ref[...])
            for c0, w in SUB_COLS:
                acc = piece(c0, w)
                for l in range(w // 128):
                    slab = (c0 + 128 * l) // 128
                    part = acc[:, 128 * l:128 * (l + 1)]
                    if slab in Q_SLABS:
                        part = _rope(part, *tab) * q_scale
                    elif slab in K_SLABS:
                        part = _rope(part, *tab)
                    proj_ref[:, 128 * slab:128 * (slab + 1)] = part.astype(bf16)

    first_pass = lambda jj, i, order: (jnp.where(jj == 0, i, ntok - 1), 0)
    const = lambda jj, i, order: (0, 0)
    tab = pl.BlockSpec((tb, 128), lambda jj, i, order: (jnp.where(order[jj] == 1, i % nsb, 0), 0))
    grid_spec = pltpu.PrefetchScalarGridSpec(
        num_scalar_prefetch=1, grid=(4, ntok),
        in_specs=[pl.BlockSpec((tb, D), first_pass), pl.BlockSpec((1, D), const), tab, tab, tab, ANY]
        + [pl.BlockSpec(w.shape, const) for w in small],
        out_specs=(pl.BlockSpec((tb, D), first_pass),
                   pl.BlockSpec((tb, PAIR_ROWS), lambda jj, i, order: (i, order[jj])), ANY) + tuple([ANY] * nsm),
        scratch_shapes=[pltpu.VMEM((D_IN, D), bf16), pltpu.VMEM((T, D), bf16)]
        + [pltpu.VMEM(w.shape, bf16) for w in out_shards]
        + [pltpu.SemaphoreType.DMA((7,)), pltpu.SemaphoreType.DMA((7,)), pltpu.SemaphoreType.DMA((2,))]
        + _exchange_scratch(nsm, 7))
    res = _pcall(
        body, name="in_proj", grid_spec=grid_spec,
        out_shape=(_sds((T, D), bf16), _sds((T, D_IN), bf16), _sds((D_IN, D), bf16),
                   _sds((NDEV * cw_shard.shape[0], cw_shard.shape[1]), f32))
        + tuple(_sds((NDEV * w.shape[0], w.shape[1]), bf16) for w in out_shards),
        compiler_params=_params(("arbitrary", "arbitrary")),
    )(chip_order, x2d, norm_g, *tabs, wt_shard, *small)
    return res[0], res[1], res[2], res[3], res[4:]


def _rows_iota(shape):
    return lax.broadcasted_iota(jnp.int32, shape, 0)


def _shift_down(v, k):
    return jnp.where(_rows_iota(v.shape) >= k, pltpu.roll(v, k, 0), 0.0)


def _shift_up(v, k):
    n = v.shape[0]
    return jnp.where(_rows_iota(v.shape) < n - k, pltpu.roll(v, n - k, 0), 0.0)


def _linear_scan(a, b, a_s, b_s, edge_s, out_ref, reverse):
    n = a.shape[0]
    ng = n // 8
    a3, b3 = a.reshape(ng, 8, RB), b.reshape(ng, 8, RB)
    rid = lax.broadcasted_iota(jnp.int32, a3.shape, 1)
    for s in (1, 2, 4):
        keep, shift = (rid < 8 - s, 8 - s) if reverse else (rid >= s, s)
        b3 = jnp.where(keep, a3 * pltpu.roll(b3, shift, 1) + b3, b3)
        a3 = jnp.where(keep, a3 * pltpu.roll(a3, shift, 1), a3)
    a_s[...] = a3.reshape(n, RB)
    b_s[...] = b3.reshape(n, RB)
    edge = 0 if reverse else 7
    ea, eb = a_s[pl.ds(edge, ng, stride=8), :], b_s[pl.ds(edge, ng, stride=8), :]
    r = _rows_iota(ea.shape)
    s = 1
    while s < ng:
        keep, shift = (r < ng - s, ng - s) if reverse else (r >= s, s)
        eb = jnp.where(keep, ea * pltpu.roll(eb, shift, 0) + eb, eb)
        if 2 * s < ng:
            ea = jnp.where(keep, ea * pltpu.roll(ea, shift, 0), ea)
        s *= 2
    edge_s[...] = _shift_up(eb, 1) if reverse else _shift_down(eb, 1)

    def eight_groups(i, carry):
        for k in range(8):
            j = i * 8 + k
            rows = pl.ds(pl.multiple_of(j * 8, 8), 8)
            out_ref[rows, :] = b_s[rows, :] + a_s[rows, :] * edge_s[pl.ds(j, 1), :]
        return carry

    lax.fori_loop(0, ng // 8, eight_groups, 0)


def _neg_expm1(v):
    series = -v * (1.0 + v * (0.5 + v * (1.0 / 6.0)))
    return jnp.where(v > -0.015625, series, 1.0 - jnp.exp(v))


def _softplus_neg(lam):
    return jnp.maximum(-lam, 0.0) + jnp.log(1.0 + jnp.exp(-jnp.abs(lam)))


def _lru_gates(x0, cw, cb, wa, ba, wx, bx, lam):
    taps = [_shift_down(x0, 3 - k) for k in range(3)] + [x0]
    u = cb + cw[3:4, :] * x0
    for k in range(3):
        u = u + cw[k:k + 1, :] * taps[k]
    ub = u.astype(bf16)
    r = _sigmoid_positive(_dot(ub, wa.astype(bf16), NN) + ba)
    i = _sigmoid(_dot(ub, wx.astype(bf16), NN) + bx)
    sp = _softplus_neg(lam)
    log_a = (-LRU_C) * r * sp
    a = jnp.exp(log_a)
    w = _neg_expm1(2.0 * log_a)
    inv_mult = lax.rsqrt(w)
    return u, ub, r, i, sp, a, w * inv_mult, inv_mult, taps


def _lru_specs(S, nb):
    col = lambda off: pl.BlockSpec((S, RB), lambda n, b, off=off: (b, off + n))
    vec = pl.BlockSpec((1, RB), lambda n, b: (0, n))
    wblk = pl.BlockSpec((None, RB, RB), lambda n, b: (n, 0, 0))
    cwblk = pl.BlockSpec((8, RB), lambda n, b: (n, 0))
    return col, vec, wblk, cwblk


def _lru_forward(proj, cw_full, conv_b, w_a, b_a, w_x, b_x, lam, S):
    T = proj.shape[0]
    nb = T // S
    col, vec, wblk, cwblk = _lru_specs(S, nb)

    def body(x0_ref, g_ref, cw_ref, cb_ref, wa_ref, ba_ref, wx_ref, bx_ref, lam_ref, y_ref, h_ref, a_s, b_s, edge_s):
        x0 = x0_ref[...].astype(f32)
        u, ub, r, i, sp, a, mult, _, _ = _lru_gates(x0, cw_ref[...], cb_ref[...], wa_ref[...], ba_ref[...],
                                                    wx_ref[...], bx_ref[...], lam_ref[...])
        _linear_scan(a, mult * (i * u), a_s, b_s, edge_s, h_ref, reverse=False)
        g = g_ref[...].astype(f32)
        y_ref[...] = (h_ref[...] * (g * _sigmoid(g))).astype(bf16)

    out = pl.BlockSpec((S, RB), lambda n, b: (b, n))
    return _pcall(
        body, name="lru_forward", grid=(RNN_BLOCKS, nb),
        in_specs=[col(0), col(8), cwblk, vec, wblk, vec, wblk, vec, vec],
        out_specs=(out, out), out_shape=(_sds((T, D), bf16), _sds((T, D), f32)),
        scratch_shapes=[pltpu.VMEM((S, RB), f32), pltpu.VMEM((S, RB), f32), pltpu.VMEM((S // 8, RB), f32)],
        compiler_params=_params(("arbitrary", "arbitrary")),
    )(proj, proj, cw_full, conv_b, w_a, b_a, w_x, b_x, lam)


def _rope_tables(S):
    pos = jnp.arange(S, dtype=f32)
    inv_freq = ROPE_THETA ** (-jnp.arange(0, ROPE_DIM, 2, dtype=f32) / ROPE_DIM)
    ang = pos[:, None] * inv_freq[None, :]
    cos, sin = jnp.cos(ang), jnp.sin(ang)
    lane = jnp.arange(128) % HEAD
    cosl, sinl = cos[:, lane % 8], sin[:, lane % 8]
    c = jnp.where(lane[None, :] < ROPE_DIM, cosl, 1.0)
    s1 = jnp.where(lane[None, :] < 8, -sinl, 0.0)
    s2 = jnp.where((lane[None, :] >= 8) & (lane[None, :] < ROPE_DIM), sinl, 0.0)
    return c.astype(f32), s1.astype(f32), s2.astype(f32)


def _heads_to_rows(t):
    return jnp.concatenate([t[:, HEAD * h:HEAD * (h + 1)] for h in range(GROUP)], axis=0)


def _rows_to_heads(t):
    return jnp.concatenate([t[QB * h:QB * (h + 1), :] for h in range(GROUP)], axis=1)


def _window_bias(first_block):
    shape = (GROUP * QB, 2 * QB)
    qi = _rows_iota(shape) % QB
    cj = lax.broadcasted_iota(jnp.int32, shape, 1)
    valid = (cj > qi) & (cj <= qi + QB) & ((cj >= QB) | jnp.logical_not(first_block))
    return jnp.where(valid, 0.0, -jnp.inf)


def _attn_probs(q_rows, k_cat, sink_col, bias):
    s = _dot(q_rows, k_cat, NT) + bias
    m = jnp.maximum(jnp.max(s, axis=1, keepdims=True), sink_col)
    p = jnp.exp(s - m)
    e_sink = jnp.exp(sink_col - m)
    inv = 1.0 / (jnp.sum(p, axis=1, keepdims=True) + e_sink)
    return p * inv, e_sink * inv


def _sink_column(sink_ref, kv):
    rid = _rows_iota((GROUP * QB, 1))
    col = jnp.zeros((GROUP * QB, 1), f32)
    for h in range(GROUP):
        col = jnp.where(rid // QB == h, sink_ref[0, GROUP * kv + h], col)
    return col


def _attn_in_specs(S):
    nq = S // QB
    last = nq - 1
    cur = lambda b, j: b * nq + jnp.minimum(j, last)
    prev = lambda b, j: b * nq + jnp.maximum(jnp.minimum(j, last) - 1, 0)
    specs = [
        pl.BlockSpec((QB, D), lambda b, j: (cur(b, j), 2)),
        pl.BlockSpec((QB, 256), lambda b, j: (cur(b, j), 12)),
        pl.BlockSpec((QB, 256), lambda b, j: (prev(b, j), 12)),
        pl.BlockSpec((QB, 256), lambda b, j: (cur(b, j), 13)),
        pl.BlockSpec((QB, 256), lambda b, j: (prev(b, j), 13)),
        pl.BlockSpec((QB, 512), lambda b, j: (cur(b, j), 7)),
        pl.BlockSpec((QB, 512), lambda b, j: (cur(b, j), 8)),
        SMEM_SPEC,
    ]
    return specs, cur, prev


def _attn_forward(proj, sinks, S, out_shards):
    T = proj.shape[0]
    nb, nq = T // S, S // QB
    specs, cur, _ = _attn_in_specs(S)
    nw = len(out_shards)

    def body(q_ref, kc_ref, kp_ref, vc_ref, vp_ref, gl_ref, gh_ref, sink_ref, *rest):
        shards = rest[:nw]
        y_ref = rest[nw]
        gathered = rest[nw + 1:2 * nw + 1]
        stage = rest[2 * nw + 1:3 * nw + 1]
        sems = rest[3 * nw + 1:]
        b, j = pl.program_id(0), pl.program_id(1)

        @pl.when((b == 0) & (j == 0))
        def _():
            for a in range(nw):
                stage[a][...] = shards[a][...].astype(bf16)
            _start_all(_direct_gather_copies(stage, gathered, *sems))

        @pl.when((b == nb - 1) & (j == nq - 1))
        def _():
            _wait_all(_direct_gather_copies(stage, gathered, *sems))

        bias = _window_bias(j == 0)
        kc, kp, vc, vp = kc_ref[...], kp_ref[...], vc_ref[...], vp_ref[...]
        for kv in range(KV_HEADS):
            lanes = slice(256 * kv, 256 * (kv + 1))
            hl = slice(HEAD * kv, HEAD * (kv + 1))
            q_rows = _heads_to_rows(q_ref[:, lanes])
            k_cat = jnp.concatenate([kp[:, hl], kc[:, hl]], axis=0)
            v_cat = jnp.concatenate([vp[:, hl], vc[:, hl]], axis=0)
            probs, _ = _attn_probs(q_rows, k_cat, _sink_column(sink_ref, kv), bias)
            o = _rows_to_heads(_dot(probs.astype(bf16), v_cat, NN))
            g_src = gl_ref if kv < 2 else gh_ref
            g = g_src[:, 256 * (kv % 2):256 * (kv % 2 + 1)].astype(f32)
            y_ref[:, lanes] = (o * (g * _sigmoid(g))).astype(bf16)

    args = [proj] * 7 + [sinks] + list(out_shards)
    res = _pcall(
        body, name="attn_forward", grid=(nb, nq),
        in_specs=specs + [pl.BlockSpec(w.shape, lambda b, j: (0, 0)) for w in out_shards],
        out_specs=(pl.BlockSpec((QB, D), lambda b, j: (cur(b, j), 0)),) + tuple([ANY] * nw),
        out_shape=(_sds((T, D), bf16),) + tuple(_sds((NDEV * w.shape[0], w.shape[1]), bf16) for w in out_shards),
        scratch_shapes=[pltpu.VMEM(w.shape, bf16) for w in out_shards] + _exchange_scratch(nw, 7),
        compiler_params=_params(("arbitrary", "arbitrary")),
    )(*args)
    return res[0], res[1:]


def _merge_and_head(x2d, tgt, proj, y_rnn, y_attn, w_r, w_a, w_o, gfin):
    T = x2d.shape[0]
    tb = min(T, 512)
    nsteps = T // tb

    def body(x_ref, t_ref, mr0, mr1, ma0, ma1, yr_ref, ya_ref, wr_ref, wa_ref, wo_ref, gf_ref,
             dx2_ref, dyr_ref, dya_ref, dmr_ref, dma_ref, loss_ref, gfin_ref, gwr_out, gwa_out, gwo_out,
             gwr_acc, gwa_acc, gwo_acc, out_sems):
        step = pl.program_id(0)

        @pl.when(step == 0)
        def _():
            loss_ref[...] = jnp.zeros_like(loss_ref)
            gfin_ref[...] = jnp.zeros_like(gfin_ref)
            gwr_acc[...] = jnp.zeros_like(gwr_acc)
            gwa_acc[...] = jnp.zeros_like(gwa_acc)
            gwo_acc[...] = jnp.zeros_like(gwo_acc)

        sr = _sigmoid(jnp.concatenate([mr0[...], mr1[...]], axis=1).astype(f32))
        sa = _sigmoid(jnp.concatenate([ma0[...], ma1[...]], axis=1).astype(f32))
        p_r = _dot(yr_ref[...], wr_ref[...], NN)
        p_a = _dot(ya_ref[...], wa_ref[...], NN)
        merged = (sr * p_r + sa * p_a).astype(bf16)
        x2 = x_ref[...] + _dot(merged, wo_ref[...], NN)
        rstd = lax.rsqrt(jnp.mean(x2 * x2, axis=-1, keepdims=True) + EPS)
        xh = x2 * rstd
        gf = gf_ref[...]
        err = xh * gf - t_ref[...]
        loss_ref[...] += jnp.sum(err * err)
        dy = err * (1.0 / D)
        gfin_ref[0:1, :] += jnp.sum(dy * xh, axis=0, keepdims=True)
        dxn = dy * gf
        dx2 = rstd * (dxn - xh * jnp.mean(dxn * xh, axis=-1, keepdims=True))
        dx2_ref[...] = dx2
        dx2b = dx2.astype(bf16)
        dmerged = _dot(dx2b, wo_ref[...], NT)
        dmr_ref[...] = (dmerged * p_r * (sr * (1.0 - sr))).astype(bf16)
        dma_ref[...] = (dmerged * p_a * (sa * (1.0 - sa))).astype(bf16)
        dpr = (dmerged * sr).astype(bf16)
        dpa = (dmerged * sa).astype(bf16)
        dyr_ref[...] = _dot(dpr, wr_ref[...], NT).astype(bf16)
        dya_ref[...] = _dot(dpa, wa_ref[...], NT).astype(bf16)
        gwr_acc[...] += _dot(yr_ref[...], dpr, TN)
        gwa_acc[...] += _dot(ya_ref[...], dpa, TN)
        gwo_acc[...] += _dot(merged, dx2b, TN)

        @pl.when(step == nsteps - 1)
        def _():
            copies = [pltpu.make_async_copy(src, dst, out_sems.at[k]) for k, (src, dst) in enumerate(
                ((gwr_acc, gwr_out), (gwa_acc, gwa_out), (gwo_acc, gwo_out)))]
            for cp in copies:
                cp.start()
            for cp in copies:
                cp.wait()

    tok = pl.BlockSpec((tb, D), lambda i: (i, 0))
    half = lambda c: pl.BlockSpec((tb, CH), lambda i, c=c: (i, c))
    wfull = pl.BlockSpec((D, D), lambda i: (0, 0), pipeline_mode=pl.Buffered(1))
    acc = pl.BlockSpec((8, D), lambda i: (0, 0))
    return _pcall(
        body, name="merge_and_head", grid=(nsteps,),
        in_specs=[tok, tok, half(9), half(10), half(11), half(12), tok, tok, wfull, wfull, wfull,
                  pl.BlockSpec((1, D), lambda i: (0, 0))],
        out_specs=(tok, tok, tok, tok, tok, acc, acc, ANY, ANY, ANY),
        out_shape=(_sds((T, D), f32), _sds((T, D), bf16), _sds((T, D), bf16), _sds((T, D), bf16),
                   _sds((T, D), bf16), _sds((8, D), f32), _sds((8, D), f32),
                   _sds((D, D), f32), _sds((D, D), f32), _sds((D, D), f32)),
        scratch_shapes=[pltpu.VMEM((D, D), f32)] * 3 + [pltpu.SemaphoreType.DMA((3,))],
        compiler_params=_params(("arbitrary",)),
    )(x2d, tgt, proj, proj, proj, proj, y_rnn, y_attn, w_r, w_a, w_o, gfin)


def _attn_backward(proj, dy_attn, tabs, sinks, S, chip_sums):
    T = proj.shape[0]
    nb, nq = T // S, S // QB
    nex = len(chip_sums)
    specs, cur, prev = _attn_in_specs(S)
    last = nq - 1
    tab_cur = pl.BlockSpec((QB, 128), lambda b, j: (jnp.minimum(j, last), 0))
    tab_prev = pl.BlockSpec((QB, 128), lambda b, j: (jnp.maximum(jnp.minimum(j, last) - 1, 0), 0))
    specs = specs + [pl.BlockSpec((QB, D), lambda b, j: (cur(b, j), 0))] + [tab_cur] * 3 + [tab_prev] * 3
    q_scale = 1.0 / math.sqrt(HEAD)

    def rope_back(dt, tab):
        return jnp.concatenate([_rope_transposed(dt[:, 128 * l:128 * (l + 1)], *tab) for l in range(2)], axis=1)

    def body(q_ref, kc_ref, kp_ref, vc_ref, vp_ref, gl_ref, gh_ref, sink_ref, dy_ref, cc, s1c, s2c, cp, s1p, s2p,
             *rest):
        ex_src = rest[:nex]
        dq_ref, dkv_ref, dg_ref, dsink_ref = rest[nex:nex + 4]
        ex_dst = rest[nex + 4:2 * nex + 4]
        carry_k, carry_v = rest[2 * nex + 4:2 * nex + 6]
        sems = rest[2 * nex + 6:]
        b, j = pl.program_id(0), pl.program_id(1)

        @pl.when((b == 0) & (j == 0))
        def _():
            dsink_ref[...] = jnp.zeros_like(dsink_ref)
            _start_all(_chip_exchange_copies(ex_src, ex_dst, *sems))

        @pl.when((b == nb - 1) & (j == nq))
        def _():
            _wait_all(_chip_exchange_copies(ex_src, ex_dst, *sems))

        @pl.when(j == 0)
        def _():
            carry_k[...] = jnp.zeros_like(carry_k)
            carry_v[...] = jnp.zeros_like(carry_v)

        @pl.when(j < nq)
        def _():
            bias = _window_bias(j == 0)
            tc = (cc[...], s1c[...], s2c[...])
            tp = (cp[...], s1p[...], s2p[...])
            kc, kp, vc, vp = kc_ref[...], kp_ref[...], vc_ref[...], vp_ref[...]
            dk_prev, dk_cur, dv_prev, dv_cur = [], [], [], []
            dsink_acc = jnp.zeros((8, 128), f32)
            r8 = lax.broadcasted_iota(jnp.int32, (8, 128), 0)
            l8 = lax.broadcasted_iota(jnp.int32, (8, 128), 1)
            for kv in range(KV_HEADS):
                lanes = slice(256 * kv, 256 * (kv + 1))
                hl = slice(HEAD * kv, HEAD * (kv + 1))
                q_rows = _heads_to_rows(q_ref[:, lanes])
                k_cat = jnp.concatenate([kp[:, hl], kc[:, hl]], axis=0)
                v_cat = jnp.concatenate([vp[:, hl], vc[:, hl]], axis=0)
                probs, p_sink = _attn_probs(q_rows, k_cat, _sink_column(sink_ref, kv), bias)
                pb = probs.astype(bf16)
                o = _rows_to_heads(_dot(pb, v_cat, NN))
                g_src = gl_ref if kv < 2 else gh_ref
                g = g_src[:, 256 * (kv % 2):256 * (kv % 2 + 1)].astype(f32)
                sg = _sigmoid(g)
                dy = dy_ref[:, lanes].astype(f32)
                dg_ref[:, lanes] = (dy * o * (sg * (1.0 + g * (1.0 - sg)))).astype(bf16)
                do_rows = _heads_to_rows(dy * (g * sg)).astype(bf16)
                dv = _dot(pb, do_rows, TN)
                dp = _dot(do_rows, v_cat, NT)
                rowdot = jnp.sum(probs * dp, axis=1, keepdims=True)
                ds = (probs * (dp - rowdot)).astype(bf16)
                sink_rows = -(p_sink * rowdot)
                for h in range(GROUP):
                    val = jnp.sum(sink_rows[QB * h:QB * (h + 1), :])
                    dsink_acc = dsink_acc + jnp.where((r8 == 0) & (l8 == GROUP * kv + h), val, 0.0)
                dq = _rows_to_heads(_dot(ds, k_cat, NN)) * q_scale
                dq_ref[:, lanes] = rope_back(dq, tc).astype(bf16)
                dk = _dot(ds, q_rows, TN)
                dk_prev.append(dk[:QB, :])
                dk_cur.append(dk[QB:, :])
                dv_prev.append(dv[:QB, :])
                dv_cur.append(dv[QB:, :])
            dsink_ref[...] += dsink_acc
            dkp = rope_back(jnp.concatenate(dk_prev, axis=1), tp)
            dkc = rope_back(jnp.concatenate(dk_cur, axis=1), tc)
            dkv_ref[:, 0:256] = (carry_k[...] + dkp).astype(bf16)
            dkv_ref[:, 256:512] = (carry_v[...] + jnp.concatenate(dv_prev, axis=1)).astype(bf16)
            carry_k[...] = dkc
            carry_v[...] = jnp.concatenate(dv_cur, axis=1)

        @pl.when(j == nq)
        def _():
            dkv_ref[:, 0:256] = carry_k[...].astype(bf16)
            dkv_ref[:, 256:512] = carry_v[...].astype(bf16)

    lag = lambda b, j: (b * nq + jnp.maximum(j - 1, 0), 0)
    args = [proj] * 7 + [sinks, dy_attn] + list(tabs) + list(tabs) + list(chip_sums)
    res = _pcall(
        body, name="attn_backward", grid=(nb, nq + 1), in_specs=specs + [ANY] * nex,
        out_specs=(pl.BlockSpec((QB, D), lambda b, j: (cur(b, j), 0)), pl.BlockSpec((QB, 512), lag),
                   pl.BlockSpec((QB, D), lambda b, j: (cur(b, j), 0)), pl.BlockSpec((8, 128), lambda b, j: (0, 0)))
        + tuple([ANY] * nex),
        out_shape=(_sds((T, D), bf16), _sds((T, 512), bf16), _sds((T, D), bf16), _sds((8, 128), f32))
        + tuple(_sds(s.shape, s.dtype) for s in chip_sums),
        scratch_shapes=[pltpu.VMEM((QB, 256), f32), pltpu.VMEM((QB, 256), f32)] + _exchange_scratch(nex, 3),
        compiler_params=_params(("arbitrary", "arbitrary")),
    )(*args)
    return res[:4], res[4:]


def _lru_backward(proj, h_all, dy_rnn, cw_full, conv_b, w_a, b_a, w_x, b_x, lam, S):
    T = proj.shape[0]
    nb = T // S
    col, vec, wblk, cwblk = _lru_specs(S, nb)
    tokblk = pl.BlockSpec((S, RB), lambda n, b: (b, n))

    def body(x0_ref, g_ref, h_ref, dy_ref, cw_ref, cb_ref, wa_ref, ba_ref, wx_ref, bx_ref, lam_ref,
             du0_ref, dg_ref, gwa_ref, gwx_ref, vec_ref, gcw_ref, a_s, b_s, dh_s, edge_s):
        @pl.when(pl.program_id(1) == 0)
        def _():
            gwa_ref[...] = jnp.zeros_like(gwa_ref)
            gwx_ref[...] = jnp.zeros_like(gwx_ref)
            vec_ref[...] = jnp.zeros_like(vec_ref)
            gcw_ref[...] = jnp.zeros_like(gcw_ref)

        x0 = x0_ref[...].astype(f32)
        cw = cw_ref[...]
        lam_v = lam_ref[...]
        u, ub, r, i, sp, a, mult, inv_mult, taps = _lru_gates(x0, cw, cb_ref[...], wa_ref[...], ba_ref[...],
                                                              wx_ref[...], bx_ref[...], lam_v)
        h = h_ref[...]
        g = g_ref[...].astype(f32)
        dy = dy_ref[...].astype(f32)
        sg = _sigmoid(g)
        dg_ref[...] = (dy * h * (sg * (1.0 + g * (1.0 - sg)))).astype(bf16)
        _linear_scan(_shift_up(a, 1), dy * (g * sg), a_s, b_s, edge_s, dh_s, reverse=True)
        dh_total = dh_s[...]
        da = dh_total * _shift_down(h, 1)
        dmult = dh_total * (i * u)
        db = dh_total * mult
        di = db * u
        du = db * i
        dlog_a_c = ((-LRU_C) * a) * (da - dmult * (a * inv_mult))
        dr = dlog_a_c * sp
        dsp = jnp.sum(dlog_a_c * r, axis=0, keepdims=True)
        dpre_r = dr * r * (1.0 - r)
        dpre_i = di * i * (1.0 - i)
        dpre_rb = dpre_r.astype(bf16)
        dpre_ib = dpre_i.astype(bf16)
        du = du + _dot(dpre_rb, wa_ref[...].astype(bf16), NT) + _dot(dpre_ib, wx_ref[...].astype(bf16), NT)
        gwa_ref[...] += _dot(ub, dpre_rb, TN)
        gwx_ref[...] += _dot(ub, dpre_ib, TN)
        vec_ref[0:1, :] += jnp.sum(du, axis=0, keepdims=True)
        vec_ref[1:2, :] += jnp.sum(dpre_r, axis=0, keepdims=True)
        vec_ref[2:3, :] += jnp.sum(dpre_i, axis=0, keepdims=True)
        vec_ref[3:4, :] += dsp * (-_sigmoid(-lam_v))
        dx0 = cw[3:4, :] * du
        for k in range(3):
            dx0 = dx0 + cw[k:k + 1, :] * _shift_up(du, 3 - k)
        for k in range(4):
            gcw_ref[k:k + 1, :] += jnp.sum(du * taps[k], axis=0, keepdims=True)
        du0_ref[...] = dx0.astype(bf16)

    wacc = pl.BlockSpec((RB, RB), lambda n, b: (0, n))
    vacc = pl.BlockSpec((8, RB), lambda n, b: (0, n))
    cacc = pl.BlockSpec((8, RB), lambda n, b: (n, 0))
    return _pcall(
        body, name="lru_backward", grid=(RNN_BLOCKS, nb),
        in_specs=[col(0), col(8), tokblk, tokblk, cwblk, vec, wblk, vec, wblk, vec, vec],
        out_specs=(tokblk, tokblk, wacc, wacc, vacc, cacc),
        out_shape=(_sds((T, D), bf16), _sds((T, D), bf16), _sds((RB, D), f32), _sds((RB, D), f32),
                   _sds((8, D), f32), _sds((8 * RNN_BLOCKS, RB), f32)),
        scratch_shapes=[pltpu.VMEM((S, RB), f32)] * 3 + [pltpu.VMEM((S // 8, RB), f32)],
        compiler_params=_params(("arbitrary", "arbitrary")),
    )(proj, proj, h_all, dy_rnn, cw_full, conv_b, w_a, b_a, w_x, b_x, lam)


def _section_of_chunk(s):
    out = []
    for start, n in zip(SEC_START, SEC_CHUNKS):
        inside = (s >= start) & (s < start + n)
        out.append((inside, jnp.clip(s - start, 0, n - 1)))
    return out


EFFECT = pltpu.SideEffectType.DATAFLOW_SIDE_EFFECTING
HBM_SPEC = pl.BlockSpec(memory_space=pltpu.HBM)
SEM_SPEC = pl.BlockSpec(memory_space=pltpu.SEMAPHORE)


def _split_exchange_copies(src_ref, land_ref, send_sems, recv_sems):
    x, y, c = _my_place()
    copies = []
    for k in (3, 1, 2):
        px, py = (x + (k >> 1)) % 2, (y + (k & 1)) % 2
        copies.append(pltpu.make_async_remote_copy(
            src_ref=src_ref.at[2 * px + py], dst_ref=land_ref.at[k - 1], send_sem=send_sems[k - 1],
            recv_sem=recv_sems[k - 1], device_id=(px, py, c), device_id_type=MESH))
    return copies


def _exchange_start(chip_sum):
    _, r, cols = chip_sum.shape

    def body(src_ref, land_ref, s0, s1, s2, r0, r1, r2, src_thru, land_thru, token):
        for cp in _split_exchange_copies(src_ref, land_ref, (s0, s1, s2), (r0, r1, r2)):
            cp.start()
        token[...] = jnp.zeros_like(token)

    land = pltpu.with_memory_space_constraint(lax.empty((3, r, cols), chip_sum.dtype), pltpu.HBM)
    res = _pcall(
        body, name="exchange_start",
        out_shape=tuple([pltpu.SemaphoreType.DMA(())] * 6) + (
            pltpu.HBM(chip_sum.shape, chip_sum.dtype), pltpu.HBM((3, r, cols), chip_sum.dtype), _sds((8, 128), f32)),
        in_specs=(HBM_SPEC, HBM_SPEC), out_specs=tuple([SEM_SPEC] * 6) + (HBM_SPEC, HBM_SPEC, VMEM_SPEC),
        input_output_aliases={0: 6, 1: 7},
        compiler_params=pltpu.CompilerParams(has_side_effects=EFFECT),
    )(pltpu.with_memory_space_constraint(chip_sum, pltpu.HBM), land)
    return res[:6], res[6], res[7], res[8]


def _exchange_wait(sems, src_thru, land_thru, after):
    def body(src_ref, land_ref, s0, s1, s2, r0, r1, r2, after_ref, src_dead, got_ref):
        for cp in _split_exchange_copies(src_ref, land_ref, (s0, s1, s2), (r0, r1, r2)):
            cp.wait_send()
            cp.wait_recv()

    return _pcall(
        body, name="exchange_wait",
        out_shape=(pltpu.HBM(src_thru.shape, src_thru.dtype), pltpu.HBM(land_thru.shape, land_thru.dtype)),
        in_specs=(HBM_SPEC, HBM_SPEC) + tuple([SEM_SPEC] * 6) + (ANY,), out_specs=(HBM_SPEC, HBM_SPEC),
        input_output_aliases={0: 0, 1: 1},
        compiler_params=pltpu.CompilerParams(has_side_effects=EFFECT),
    )(src_thru, land_thru, *sems, after)[1]


def _input_grad(dsecs, wt_full, x2d, dx2, norm_g):
    T = x2d.shape[0]
    tb = min(T, 512)
    nsec = len(dsecs)
    ntok = T // tb

    def body(*refs):
        secs = refs[:nsec]
        wt_hbm, x_ref, dx2_ref, g_ref, dx_ref, gnorm_ref, wt_ref, wt_sems = refs[nsec:]
        i = pl.program_id(0)
        rows = [slice(CH * start, CH * (start + n)) for start, n in zip(SEC_START, SEC_CHUNKS)]
        fetch = [pltpu.make_async_copy(wt_hbm.at[r, :], wt_ref.at[r, :], wt_sems.at[a]) for a, r in enumerate(rows)]

        @pl.when(i == 0)
        def _():
            gnorm_ref[...] = jnp.zeros_like(gnorm_ref)
            for cp in fetch:
                cp.start()

        dh = None
        for a, r in enumerate(rows):
            @pl.when(i == 0)
            def _(a=a):
                fetch[a].wait()
            part = _dot(secs[a][...], wt_ref[r, :], NN)
            dh = part if dh is None else dh + part
        xv = x_ref[...]
        rstd = lax.rsqrt(jnp.mean(xv * xv, axis=-1, keepdims=True) + EPS)
        xh = xv * rstd
        gnorm_ref[0:1, :] += jnp.sum(dh * xh, axis=0, keepdims=True)
        dxn = dh * g_ref[...]
        dx_ref[...] = dx2_ref[...] + rstd * (dxn - xh * jnp.mean(dxn * xh, axis=-1, keepdims=True))

    tok = pl.BlockSpec((tb, D), lambda i: (i, 0))
    return _pcall(
        body, name="input_grad", grid=(ntok,),
        in_specs=[pl.BlockSpec((tb, sec.shape[1]), lambda i: (i, 0)) for sec in dsecs]
        + [ANY, tok, tok, pl.BlockSpec((1, D), lambda i: (0, 0))],
        out_specs=(tok, pl.BlockSpec((8, D), lambda i: (0, 0))),
        out_shape=(_sds((T, D), f32), _sds((8, D), f32)),
        scratch_shapes=[pltpu.VMEM((D_IN, D), bf16), pltpu.SemaphoreType.DMA((nsec,))],
        compiler_params=_params(("arbitrary",)),
    )(*dsecs, wt_full, x2d, dx2, norm_g)


def _w_in_grad(dsecs, h_bf):
    T = h_bf.shape[0]
    tk = min(T, 2048)
    nchunks = D_IN // CH
    nsec = len(dsecs)
    nt = T // tk

    def body(*refs):
        secs = refs[:nsec]
        h_ref, out_ref, acc = refs[nsec:]
        s, t = pl.program_id(0), pl.program_id(1)

        @pl.when(t == 0)
        def _():
            acc[...] = jnp.zeros_like(acc)

        h_rows = h_ref[pl.ds(pl.multiple_of(t * tk, tk), tk), :]
        for a, (start, n) in enumerate(zip(SEC_START, SEC_CHUNKS)):
            @pl.when((s >= start) & (s < start + n))
            def _(a=a):
                acc[...] += _dot(secs[a][...], h_rows, TN)

        @pl.when(t == nt - 1)
        def _():
            out_ref[...] = acc[...].astype(bf16)

    def sec_spec(a):
        def index(s, t, a=a):
            inside, local = _section_of_chunk(s)[a]
            return (jnp.where(inside, t, 0), local)
        return pl.BlockSpec((tk, CH), index)

    return _pcall(
        body, name="w_in_grad", grid=(nchunks, T // tk),
        in_specs=[sec_spec(a) for a in range(nsec)]
        + [pl.BlockSpec((T, D), lambda s, t: (0, 0), pipeline_mode=pl.Buffered(1))],
        out_specs=pl.BlockSpec((CH, D), lambda s, t: (s, 0)), out_shape=_sds((D_IN, D), bf16),
        scratch_shapes=[pltpu.VMEM((CH, D), f32)],
        compiler_params=_params(("arbitrary", "arbitrary")),
    )(*dsecs, h_bf)


SMALL_NAMES = ("lru_w_a", "lru_w_x", "conv_b", "lru_b_a", "lru_b_x", "lru_lambda", "norm_g", "final_norm_g",
               "attn_sinks", "conv_w")
MISC_ROW = {"conv_b": 0, "lru_b_a": 1, "lru_b_x": 2, "lru_lambda": 3, "norm_g": 8, "final_norm_g": 16,
            "attn_sinks": 24, "loss": 32}


def _small_step(gwa, gwx, gvec, gnorm_blk, gfin_blk, dsink_blk, loss_blk, gcw, params):
    srcs_rows = (RB // NDEV, RB // NDEV, 8, 8)
    flat = [t for n in SMALL_NAMES for t in params[n]]
    nout = 4 * len(SMALL_NAMES) + 1

    ra_, rx_, rm_, rc_ = srcs_rows
    rh, rf = ra_ + rx_, rm_ + rc_

    def reduce_body(gwa_ref, gwx_ref, gvec_ref, gnorm_ref, gfin_ref, dsink_ref, loss_ref, gcw_ref,
                    all_a, all_x, all_m, conv_out,
                    misc, out_h, out_f, in_h, in_f, mine_h, mine_f, every_h, every_f, sa, ra, sb, rb):
        x, y, c = _my_place()
        me = 4 * x + 2 * y + c

        misc[...] = jnp.zeros_like(misc)
        misc[0:8, :] = gvec_ref[...]
        misc[8:16, :] = gnorm_ref[...]
        misc[16:24, :] = gfin_ref[...]
        misc[24:32, 0:128] = dsink_ref[...]
        misc[32:40, :] = loss_ref[...]

        out_f[...] = jnp.zeros_like(out_f)
        for d in range(NDEV):
            out_h[d, 0:ra_, :] = gwa_ref[ra_ * d:ra_ * (d + 1), :].astype(bf16)
            out_h[d, ra_:rh, :] = gwx_ref[rx_ * d:rx_ * (d + 1), :].astype(bf16)
            out_f[d, 0:rm_, :] = misc[rm_ * d:rm_ * (d + 1), :]
            out_f[d, rm_:rf, 0:RB] = gcw_ref[rc_ * d:rc_ * (d + 1), :]

        def both(k, src_h, dst_h, src_f, dst_f, send, recv, peer):
            return [pltpu.make_async_remote_copy(src_ref=s_, dst_ref=d_, send_sem=send.at[2 * (k - 1) + t],
                                                 recv_sem=recv.at[2 * (k - 1) + t], device_id=peer,
                                                 device_id_type=MESH)
                    for t, (s_, d_) in enumerate(((src_h, dst_h), (src_f, dst_f)))]

        scatter = []
        for k in range(1, NDEV):
            px, py, pc = _peer(k)
            dev = 4 * px + 2 * py + pc
            scatter += both(k, out_h.at[dev], in_h.at[k - 1], out_f.at[dev], in_f.at[k - 1], sa, ra, (px, py, pc))
        for cp in scatter:
            cp.start()
        for cp in scatter:
            cp.wait()

        total_h = out_h[me].astype(f32)
        total_f = out_f[me]
        for k in range(NDEV - 1):
            total_h = total_h + in_h[k].astype(f32)
            total_f = total_f + in_f[k]
        conv_out[...] = total_f[rm_:rf, 0:RB]
        mine_h[...] = total_h.astype(bf16)
        mine_f[...] = total_f[0:rm_, :]
        every_h[me] = total_h.astype(bf16)
        every_f[me] = total_f[0:rm_, :]
        gather = []
        for k in range(1, NDEV):
            gather += both(k, mine_h, every_h.at[me], mine_f, every_f.at[me], sb, rb, _peer(k))
        for cp in gather:
            cp.start()
        for cp in gather:
            cp.wait()
        for d in range(NDEV):
            all_a[ra_ * d:ra_ * (d + 1), :] = every_h[d, 0:ra_, :].astype(f32)
            all_x[rx_ * d:rx_ * (d + 1), :] = every_h[d, ra_:rh, :].astype(f32)
            all_m[rm_ * d:rm_ * (d + 1), :] = every_f[d]

    def adam_body(*refs):
        all_a, all_x, all_m, conv_ref = refs[:4]
        prm = {n: refs[4 + 3 * k:7 + 3 * k] for k, n in enumerate(SMALL_NAMES)}
        nin = 4 + len(flat)
        outs = {n: refs[nin + 4 * k:nin + 4 * k + 4] for k, n in enumerate(SMALL_NAMES)}
        loss_out = refs[nin + nout - 1]
        g_conv = conv_ref[0:4, :]

        def update(name, g, pick=lambda r: r[...]):
            w_ref, m_ref, v_ref = prm[name]
            delta, m_new, v_new = _adam_math(g, pick(w_ref), pick(m_ref), pick(v_ref))
            return g, delta, m_new, v_new

        for n in range(RNN_BLOCKS):
            lanes = slice(RB * n, RB * (n + 1))
            for name, full in (("lru_w_a", all_a), ("lru_w_x", all_x)):
                for out, val in zip(outs[name], update(name, full[:, lanes], pick=lambda r, n=n: r[n])):
                    out[n] = val
        for name in ("conv_b", "lru_b_a", "lru_b_x", "lru_lambda", "norm_g", "final_norm_g"):
            row = MISC_ROW[name]
            for out, val in zip(outs[name], update(name, all_m[row:row + 1, :])):
                out[...] = val
        row = MISC_ROW["attn_sinks"]
        for out, val in zip(outs["attn_sinks"], update("attn_sinks", all_m[row:row + 1, 0:16])):
            out[...] = val
        for out, val in zip(outs["conv_w"], update("conv_w", g_conv)):
            out[...] = val
        row = MISC_ROW["loss"]
        loss_out[...] = all_m[row:row + 8, 0:128] * (0.5 / D)

    scratch = [pltpu.VMEM((64, D), f32), pltpu.VMEM((NDEV, rh, D), bf16), pltpu.VMEM((NDEV, rf, D), f32),
               pltpu.VMEM((NDEV - 1, rh, D), bf16), pltpu.VMEM((NDEV - 1, rf, D), f32),
               pltpu.VMEM((rh, D), bf16), pltpu.VMEM((rm_, D), f32),
               pltpu.VMEM((NDEV, rh, D), bf16), pltpu.VMEM((NDEV, rm_, D), f32)
               ] + [pltpu.SemaphoreType.DMA((2 * (NDEV - 1),))] * 4
    sums = _pcall(
        reduce_body, name="small_reduce",
        out_shape=(_sds((RB, D), f32), _sds((RB, D), f32), _sds((64, D), f32), _sds((8, RB), f32)),
        in_specs=[VMEM_SPEC] * 8, out_specs=tuple([VMEM_SPEC] * 4),
        scratch_shapes=scratch, compiler_params=_params(),
    )(gwa, gwx, gvec, gnorm_blk, gfin_blk, dsink_blk, loss_blk, gcw)
    out_shape = tuple(_sds(params[n][0].shape, f32) for n in SMALL_NAMES for _ in range(4)) + (_sds((8, 128), f32),)
    res = _pcall(
        adam_body, name="small_adamw", out_shape=out_shape,
        in_specs=[VMEM_SPEC] * (4 + len(flat)), out_specs=tuple([VMEM_SPEC] * nout), compiler_params=_params(),
    )(*sums, *flat)
    return {n: res[4 * k:4 * k + 4] for k, n in enumerate(SMALL_NAMES)}, res[-1]


def _pad_rows(v, rows=8):
    return jnp.concatenate([v, jnp.zeros((rows - v.shape[0], v.shape[1]), v.dtype)], axis=0)


def kernel(x, norm_g, w_in, conv_w, conv_b, lru_w_a, lru_b_a, lru_w_x, lru_b_x, lru_lambda, attn_sinks, w_rnn_out, w_attn_out, w_o, final_norm_g, loss_target, m_norm_g, m_w_in, m_conv_w, m_conv_b, m_lru_w_a, m_lru_b_a, m_lru_w_x, m_lru_b_x, m_lru_lambda, m_attn_sinks, m_w_rnn_out, m_w_attn_out, m_w_o, m_final_norm_g, v_norm_g, v_w_in, v_conv_w, v_conv_b, v_lru_w_a, v_lru_b_a, v_lru_w_x, v_lru_b_x, v_lru_lambda, v_attn_sinks, v_w_rnn_out, v_w_attn_out, v_w_o, v_final_norm_g):
    nb, S, _ = x.shape
    T = nb * S
    x2d = x.reshape(T, D)
    tgt = loss_target.reshape(T, D)
    fin_g = final_norm_g.reshape(1, D)
    w_a3, w_x3 = lru_w_a[0], lru_w_x[0]

    my_core = lax.axis_index("c").astype(jnp.int32).reshape(1)
    cx, cy = lax.axis_index("x"), lax.axis_index("y")
    chip_order = jnp.stack([2 * cx + cy, 2 * (1 - cx) + cy, 2 * cx + (1 - cy),
                            2 * (1 - cx) + (1 - cy)]).astype(jnp.int32)

    tabs = _rope_tables(S)
    h_bf, proj, wt_full, cw_full, _ = _in_proj_gather(
        x2d, norm_g, w_in[0].T.astype(bf16), _pad_rows(conv_w[0]), tabs, S, (), chip_order)
    y_rnn, h_all = _lru_forward(proj, cw_full, conv_b, w_a3, lru_b_a, w_x3, lru_b_x, lru_lambda, S)
    y_attn, (wr_full, wa_full, wo_full) = _attn_forward(proj, attn_sinks, S,
                                                        (w_rnn_out[0], w_attn_out[0], w_o[0]))

    (dx2, dy_rnn, dy_attn, dmr, dma, loss_blk, gfin_blk, g_wr, g_wa, g_wo) = _merge_and_head(
        x2d, tgt, proj, y_rnn, y_attn, wr_full, wa_full, wo_full, fin_g)
    sums_out = _pair_sums([g_wr, g_wa, g_wo], bf16, my_core, "out")

    (dq, dkv, dga, dsink_blk), (p_wr, p_wa, p_wo) = _attn_backward(proj, dy_attn, tabs, attn_sinks, S, sums_out)
    du0, dgr, gwa, gwx, gvec, gcw = _lru_backward(proj, h_all, dy_rnn, cw_full, conv_b, w_a3, lru_b_a, w_x3,
                                                  lru_b_x, lru_lambda, S)
    dsecs = (du0, dgr, dq, dkv, dga, dmr, dma)

    g_wt = _w_in_grad(dsecs, h_bf)
    (sum_in,) = _pair_sums([g_wt], bf16, my_core, "in")
    ex_sems, sum_in, landing, token = _exchange_start(sum_in)
    grad_x2d, gnorm_blk = _input_grad(dsecs, wt_full, x2d, dx2, norm_g + token[0, 0])
    p_wt = _exchange_wait(ex_sems, sum_in, landing, gnorm_blk)
    p_wt_own = lax.dynamic_index_in_dim(sum_in, 2 * cx + cy, axis=0, keepdims=False)

    small, loss_out = _small_step(gwa, gwx, gvec, gnorm_blk, gfin_blk, dsink_blk, loss_blk, gcw, {
        "lru_w_a": (w_a3, m_lru_w_a[0], v_lru_w_a[0]), "lru_w_x": (w_x3, m_lru_w_x[0], v_lru_w_x[0]),
        "conv_b": (conv_b, m_conv_b, v_conv_b), "lru_b_a": (lru_b_a, m_lru_b_a, v_lru_b_a),
        "lru_b_x": (lru_b_x, m_lru_b_x, v_lru_b_x), "lru_lambda": (lru_lambda, m_lru_lambda, v_lru_lambda),
        "norm_g": (norm_g, m_norm_g, v_norm_g),
        "final_norm_g": (fin_g, m_final_norm_g.reshape(1, D), v_final_norm_g.reshape(1, D)),
        "attn_sinks": (attn_sinks, m_attn_sinks, v_attn_sinks),
        "conv_w": (conv_w[0], m_conv_w[0], v_conv_w[0])})

    o_wt = _adamw(p_wt_own, p_wt, w_in[0].T, m_w_in[0].T, v_w_in[0].T, "adamw_w_in")
    o_wr, o_wa, o_wo = _adamw_group(
        (p_wr, p_wa, p_wo), (w_rnn_out[0], w_attn_out[0], w_o[0]),
        (m_w_rnn_out[0], m_w_attn_out[0], m_w_o[0]), (v_w_rnn_out[0], v_w_attn_out[0], v_w_o[0]), "adamw_w_out")

    def result(kind):
        d = {n: small[n][kind] for n in ("conv_b", "lru_b_a", "lru_b_x", "lru_lambda", "norm_g", "attn_sinks")}
        d.update({n: small[n][kind][None] for n in ("lru_w_a", "lru_w_x", "conv_w")})
        d["final_norm_g"] = small["final_norm_g"][kind].reshape(D)
        d.update({"w_in": o_wt[kind].T[None], "w_rnn_out": o_wr[kind][None], "w_attn_out": o_wa[kind][None],
                  "w_o": o_wo[kind][None]})
        return d

    order = ("norm_g", "w_in", "conv_w", "conv_b", "lru_w_a", "lru_b_a", "lru_w_x", "lru_b_x", "lru_lambda",
             "attn_sinks", "w_rnn_out", "w_attn_out", "w_o", "final_norm_g")
    outs = [loss_out[0, 0], grad_x2d.reshape(nb, S, D)]
    for kind in range(4):
        d = result(kind)
        outs += [d[n] for n in order]
    return tuple(outs)
```

```python
import math

import jax
import jax.numpy as jnp
from jax import lax
from jax.experimental import pallas as pl
from jax.experimental.pallas import tpu as pltpu

f32 = jnp.float32
bf16 = jnp.bfloat16

D = 1024
D_IN = 6656
NDEV = 8
RNN_BLOCKS = 8
RB = 128
HEAD = 64
KV_HEADS = 4
GROUP = 4
QB = 128
LRU_C = 8.0
EPS = 1e-6
ROPE_DIM = 16
ROPE_THETA = 500000.0
CH = 512
SEC_START = (0, 2, 4, 6, 7, 9, 11)
SEC_CHUNKS = (2, 2, 2, 1, 2, 2, 2)
VMEM_LIMIT = 62 * 1024 * 1024

ADAM_LR, ADAM_B1, ADAM_B2, ADAM_EPS, ADAM_WD, ADAM_STEP = 0.001, 0.9, 0.999, 1e-08, 0.01, 10

MESH = pl.DeviceIdType.MESH
ANY = pl.BlockSpec(memory_space=pl.ANY)
VMEM_SPEC = pl.BlockSpec(memory_space=pltpu.VMEM)
SMEM_SPEC = pl.BlockSpec(memory_space=pltpu.SMEM)


def _pcall(body, **kw):
    return pl.pallas_call(body, **kw)


def _params(sem=None, **kw):
    if sem is not None:
        kw["dimension_semantics"] = sem
    return pltpu.CompilerParams(vmem_limit_bytes=VMEM_LIMIT, **kw)


def _sds(shape, dtype):
    return jax.ShapeDtypeStruct(shape, dtype)


def _dot(a, b, dims):
    return lax.dot_general(a, b, (dims, ((), ())), preferred_element_type=f32)


NN = ((1,), (0,))
NT = ((1,), (1,))
TN = ((0,), (0,))


def _sigmoid(v):
    return 0.5 * jnp.tanh(0.5 * v) + 0.5


def _sigmoid_positive(v):
    return 1.0 / (1.0 + jnp.exp(-v))


def _my_place():
    return lax.axis_index("x"), lax.axis_index("y"), lax.axis_index("c")


def _peer(k):
    x, y, c = _my_place()
    return (x + ((k >> 2) & 1)) % 2, (y + ((k >> 1) & 1)) % 2, (c + (k & 1)) % 2


def _direct_gather_copies(srcs, outs, send_sems, recv_sems, local_sems):
    x, y, c = _my_place()
    me = 4 * x + 2 * y + c
    local, remote = [], []
    for a, (src, out) in enumerate(zip(srcs, outs)):
        r = src.shape[0]
        mine = out.at[pl.ds(pl.multiple_of(me * r, 8), r), :]
        local.append(pltpu.make_async_copy(src, mine, local_sems.at[a]))
        for k in range(1, NDEV):
            remote.append(pltpu.make_async_remote_copy(
                src_ref=src, dst_ref=mine, send_sem=send_sems.at[7 * a + k - 1], recv_sem=recv_sems.at[7 * a + k - 1],
                device_id=_peer(k), device_id_type=MESH))
    return local, remote


def _chip_exchange_copies(src, dst, send_sems, recv_sems, local_sems):
    x, y, c = _my_place()
    local, remote = [], []
    for a in range(len(src)):
        local.append(pltpu.make_async_copy(src[a].at[2 * x + y], dst[a].at[0], local_sems.at[a]))
    for k in (3, 1, 2):
        px, py = (x + (k >> 1)) % 2, (y + (k & 1)) % 2
        for a in range(len(src)):
            remote.append(pltpu.make_async_remote_copy(
                src_ref=src[a].at[2 * px + py], dst_ref=dst[a].at[k],
                send_sem=send_sems.at[3 * a + k - 1], recv_sem=recv_sems.at[3 * a + k - 1],
                device_id=(px, py, c), device_id_type=MESH))
    return local, remote


def _exchange_scratch(narr, per_array):
    return [pltpu.SemaphoreType.DMA((per_array * narr,)), pltpu.SemaphoreType.DMA((per_array * narr,)),
            pltpu.SemaphoreType.DMA((narr,))]


def _start_all(copies):
    local, remote = copies
    for cp in local + remote:
        cp.start()


def _wait_all(copies):
    local, remote = copies
    for cp in remote + local:
        cp.wait()


def _row_tile(rows, dtype):
    unit = 16 if dtype == bf16 else 8
    for cand in (256, 208, 128, 64, 40, 32, 16, 8):
        if rows % cand == 0 and cand % unit == 0:
            return cand
    return rows


def _pair_sums(grads, wire_dtype, my_core, tag):
    narr = len(grads)
    r, cols = grads[0].shape[0] // NDEV, grads[0].shape[1]
    views = [g.reshape(4, 2, r, cols) for g in grads]
    tr = _row_tile(r, wire_dtype)
    nt = r // tr

    def body(core_ref, *refs):
        mine = refs[:narr]
        whole = refs[narr:2 * narr]
        outs = refs[2 * narr:3 * narr]
        got = refs[3 * narr:4 * narr]
        send_sems, recv_sems = refs[4 * narr:]
        q, i = pl.program_id(0), pl.program_id(1)
        x, y, c = _my_place()

        def copy(a, chip):
            return pltpu.make_async_remote_copy(
                src_ref=whole[a].at[chip, 1 - c], dst_ref=got[a].at[chip],
                send_sem=send_sems.at[4 * a + chip], recv_sem=recv_sems.at[4 * a + chip],
                device_id=(x, y, 1 - c), device_id_type=MESH)

        @pl.when((q == 0) & (i == 0))
        def _():
            for chip in range(4):
                for a in range(narr):
                    copy(a, chip).start()

        for chip in range(4):
            @pl.when((q == chip) & (i == 0))
            def _(chip=chip):
                for a in range(narr):
                    copy(a, chip).wait_recv()

        rows = pl.ds(pl.multiple_of(i * tr, tr), tr)
        for a in range(narr):
            outs[a][...] = (mine[a][...].astype(f32) + got[a][q, rows, :].astype(f32)).astype(wire_dtype)

        @pl.when((q == 3) & (i == nt - 1))
        def _():
            for chip in range(4):
                for a in range(narr):
                    copy(a, chip).wait_send()

    slab = pl.BlockSpec((None, tr, cols), lambda q, i, core: (q, i, 0))
    grid_spec = pltpu.PrefetchScalarGridSpec(
        num_scalar_prefetch=1, grid=(4, nt),
        in_specs=[pl.BlockSpec((None, None, tr, cols), lambda q, i, core: (q, core[0], i, 0))] * narr + [ANY] * narr,
        out_specs=tuple([slab] * narr),
        scratch_shapes=[pltpu.VMEM((4, r, cols), grads[0].dtype)] * narr
        + [pltpu.SemaphoreType.DMA((4 * narr,)), pltpu.SemaphoreType.DMA((4 * narr,))])
    return _pcall(body, name="pair_sums_" + tag, grid_spec=grid_spec,
                  out_shape=tuple(_sds((4, r, cols), wire_dtype) for _ in range(narr)),
                  compiler_params=_params(("arbitrary", "arbitrary")))(my_core, *views, *views)


def _adam_math(g, w, m, v):
    m_new = ADAM_B1 * m + (1.0 - ADAM_B1) * g
    v_new = ADAM_B2 * v + (1.0 - ADAM_B2) * (g * g)
    m_hat = m_new / (1.0 - ADAM_B1 ** ADAM_STEP)
    v_hat = v_new / (1.0 - ADAM_B2 ** ADAM_STEP)
    return -ADAM_LR * (m_hat / (jnp.sqrt(v_hat) + ADAM_EPS) + ADAM_WD * w), m_new, v_new


def _adamw(first, parts, w, m, v, name):
    n, rows, cols = parts.shape
    tr = _row_tile(rows, parts.dtype)

    def body(f_ref, p_ref, w_ref, m_ref, v_ref, g_out, d_out, m_out, v_out):
        g = f_ref[...].astype(f32)
        for s in range(n):
            g = g + p_ref[s].astype(f32)
        g_out[...] = g
        d_out[...], m_out[...], v_out[...] = _adam_math(g, w_ref[...], m_ref[...], v_ref[...])

    blk = pl.BlockSpec((tr, cols), lambda i: (i, 0))
    return _pcall(
        body, name=name, grid=(rows // tr,),
        in_specs=[blk, pl.BlockSpec((n, tr, cols), lambda i: (0, i, 0)), blk, blk, blk],
        out_specs=(blk, blk, blk, blk), out_shape=tuple(_sds((rows, cols), f32) for _ in range(4)),
        compiler_params=_params(("arbitrary",)),
    )(first, parts, w, m, v)


def _adamw_group(parts, ws, ms, vs, name):
    nw = len(ws)

    def body(*refs):
        p_refs, w_refs, m_refs, v_refs = (refs[k * nw:(k + 1) * nw] for k in range(4))
        outs = refs[4 * nw:]
        for k in range(nw):
            g = p_refs[k][0].astype(f32)
            for s in range(1, p_refs[k].shape[0]):
                g = g + p_refs[k][s].astype(f32)
            g_out, d_out, m_out, v_out = outs[4 * k:4 * k + 4]
            g_out[...] = g
            d_out[...], m_out[...], v_out[...] = _adam_math(g, w_refs[k][...], m_refs[k][...], v_refs[k][...])

    res = _pcall(
        body, name=name, out_shape=tuple(_sds(w.shape, f32) for w in ws for _ in range(4)),
        in_specs=[VMEM_SPEC] * (4 * nw), out_specs=tuple([VMEM_SPEC] * (4 * nw)), compiler_params=_params(),
    )(*parts, *ws, *ms, *vs)
    return [res[4 * k:4 * k + 4] for k in range(nw)]


def _rope(t, c, s1, s2):
    w = t.shape[1]
    return t * c + pltpu.roll(t, w - 8, 1) * s1 + pltpu.roll(t, 8, 1) * s2


def _rope_transposed(dt, c, s1, s2):
    w = dt.shape[1]
    return dt * c + pltpu.roll(dt * s1, 8, 1) + pltpu.roll(dt * s2, w - 8, 1)


PAIR_ROWS = D_IN // 4
SUB_COLS = ((0, 512), (512, 512), (1024, 512), (1536, 128))
Q_SLABS = range(3, 11)
K_SLABS = range(11, 13)


def _in_proj_gather(x2d, norm_g, wt_shard, cw_shard, tabs, S, out_shards, chip_order):
    T = x2d.shape[0]
    tb = min(S, 1024)
    ntok = T // tb
    nsb = S // tb
    q_scale = 1.0 / math.sqrt(HEAD)
    shard_rows = wt_shard.shape[0]
    small = (cw_shard,) + tuple(out_shards)
    nsm = len(small)

    def body(order_ref, x_ref, g_ref, c_ref, s1_ref, s2_ref, wt_hbm, *rest):
        small_in = rest[:nsm]
        h_ref, proj_ref, wt_out = rest[nsm:nsm + 3]
        small_out = rest[nsm + 3:2 * nsm + 3]
        wt_vm, h_vm = rest[2 * nsm + 3:2 * nsm + 5]
        stage = rest[2 * nsm + 5:3 * nsm + 4]
        wsend, wrecv, wlocal = rest[3 * nsm + 4:3 * nsm + 7]
        dsems = rest[3 * nsm + 7:]
        jj, i = pl.program_id(0), pl.program_id(1)
        x, y, c = _my_place()
        me, sibling = (x, y, c), (x, y, 1 - c)
        chips = [(1 - x, y), (x, 1 - y), (1 - x, 1 - y)]

        def rows(place):
            px, py, pc = place
            return wt_vm.at[pl.ds(pl.multiple_of((4 * px + 2 * py + pc) * shard_rows, 16), shard_rows), :]

        def copy(k, block, to, src=None):
            return pltpu.make_async_remote_copy(
                src_ref=rows(block) if src is None else src, dst_ref=rows(block),
                send_sem=wsend.at[k], recv_sem=wrecv.at[k], device_id=to, device_id_type=MESH)

        def small_copies():
            srcs = (small_in[0],) + tuple(stage)
            return _direct_gather_copies(srcs, small_out, *dsems)

        own = pltpu.make_async_copy(wt_hbm, rows(me), wlocal.at[0])
        keep = pltpu.make_async_copy(wt_vm, wt_out, wlocal.at[1])

        @pl.when((jj == 0) & (i == 0))
        def _():
            own.start()
            copy(0, me, sibling, src=wt_hbm).start()
            for j, chip in enumerate(chips):
                copy(1 + j, me, (*chip, c), src=wt_hbm).start()
            for a in range(nsm - 1):
                stage[a][...] = small_in[1 + a][...].astype(bf16)
            _start_all(small_copies())
            own.wait()
            copy(0, sibling, me).wait_recv()

        for j, chip in enumerate(chips):
            @pl.when((jj == 1 + j) & (i == 0))
            def _(j=j, chip=chip):
                copy(1 + j, (*chip, c), me).wait_recv()
                copy(4 + j, (*chip, c), sibling).start()
                copy(4 + j, (*chip, 1 - c), me).wait_recv()

        @pl.when((jj == 3) & (i == 0))
        def _():
            keep.start()

        @pl.when((jj == 3) & (i == ntok - 1))
        def _():
            copy(0, me, sibling, src=wt_hbm).wait_send()
            for j, chip in enumerate(chips):
                copy(1 + j, me, (*chip, c), src=wt_hbm).wait_send()
                copy(4 + j, (*chip, c), sibling).wait_send()
            _wait_all(small_copies())
            keep.wait()

        tok = pl.ds(pl.multiple_of(i * tb, tb), tb)

        @pl.when(jj == 0)
        def _():
            xv = x_ref[...]
            ms = jnp.mean(xv * xv, axis=-1, keepdims=True)
            hb = (xv * lax.rsqrt(ms + EPS) * g_ref[...]).astype(bf16)
            h_ref[...] = hb
            h_vm[tok, :] = hb

        block = order_ref[jj]
        hb = h_vm[tok, :]

        def piece(c0, w):
            w_rows = wt_vm[pl.ds(pl.multiple_of(block * PAIR_ROWS + c0, 128), w), :]
            return _dot(hb, w_rows, NT)

        @pl.when(block != 1)
        def _():
            for c0, w in SUB_COLS:
                proj_ref[:, c0:c0 + w] = piece(c0, w).astype(bf16)

        @pl.when(block == 1)
        def _():
            tab = (c_ref[...], s1_ref[...], s2_ref[...])
            for c0, w in SUB_COLS:
                acc = piece(c0, w)
                for l in range(w // 128):
                    slab = (c0 + 128 * l) // 128
                    part = acc[:, 128 * l:128 * (l + 1)]
                    if slab in Q_SLABS:
                        part = _rope(part, *tab) * q_scale
                    elif slab in K_SLABS:
                        part = _rope(part, *tab)
                    proj_ref[:, 128 * slab:128 * (slab + 1)] = part.astype(bf16)

    first_pass = lambda jj, i, order: (jnp.where(jj == 0, i, ntok - 1), 0)
    const = lambda jj, i, order: (0, 0)
    tab = pl.BlockSpec((tb, 128), lambda jj, i, order: (jnp.where(order[jj] == 1, i % nsb, 0), 0))
    grid_spec = pltpu.PrefetchScalarGridSpec(
        num_scalar_prefetch=1, grid=(4, ntok),
        in_specs=[pl.BlockSpec((tb, D), first_pass), pl.BlockSpec((1, D), const), tab, tab, tab, ANY]
        + [pl.BlockSpec(w.shape, const) for w in small],
        out_specs=(pl.BlockSpec((tb, D), first_pass),
                   pl.BlockSpec((tb, PAIR_ROWS), lambda jj, i, order: (i, order[jj])), ANY) + tuple([ANY] * nsm),
        scratch_shapes=[pltpu.VMEM((D_IN, D), bf16), pltpu.VMEM((T, D), bf16)]
        + [pltpu.VMEM(w.shape, bf16) for w in out_shards]
        + [pltpu.SemaphoreType.DMA((7,)), pltpu.SemaphoreType.DMA((7,)), pltpu.SemaphoreType.DMA((2,))]
        + _exchange_scratch(nsm, 7))
    res = _pcall(
        body, name="in_proj", grid_spec=grid_spec,
        out_shape=(_sds((T, D), bf16), _sds((T, D_IN), bf16), _sds((D_IN, D), bf16),
                   _sds((NDEV * cw_shard.shape[0], cw_shard.shape[1]), f32))
        + tuple(_sds((NDEV * w.shape[0], w.shape[1]), bf16) for w in out_shards),
        compiler_params=_params(("arbitrary", "arbitrary")),
    )(chip_order, x2d, norm_g, *tabs, wt_shard, *small)
    return res[0], res[1], res[2], res[3], res[4:]


def _rows_iota(shape):
    return lax.broadcasted_iota(jnp.int32, shape, 0)


def _shift_down(v, k):
    return jnp.where(_rows_iota(v.shape) >= k, pltpu.roll(v, k, 0), 0.0)


def _shift_up(v, k):
    n = v.shape[0]
    return jnp.where(_rows_iota(v.shape) < n - k, pltpu.roll(v, n - k, 0), 0.0)


def _linear_scan(a, b, a_s, b_s, edge_s, out_ref, reverse):
    n = a.shape[0]
    ng = n // 8
    a3, b3 = a.reshape(ng, 8, RB), b.reshape(ng, 8, RB)
    rid = lax.broadcasted_iota(jnp.int32, a3.shape, 1)
    for s in (1, 2, 4):
        keep, shift = (rid < 8 - s, 8 - s) if reverse else (rid >= s, s)
        b3 = jnp.where(keep, a3 * pltpu.roll(b3, shift, 1) + b3, b3)
        a3 = jnp.where(keep, a3 * pltpu.roll(a3, shift, 1), a3)
    a_s[...] = a3.reshape(n, RB)
    b_s[...] = b3.reshape(n, RB)
    edge = 0 if reverse else 7
    ea, eb = a_s[pl.ds(edge, ng, stride=8), :], b_s[pl.ds(edge, ng, stride=8), :]
    r = _rows_iota(ea.shape)
    s = 1
    while s < ng:
        keep, shift = (r < ng - s, ng - s) if reverse else (r >= s, s)
        eb = jnp.where(keep, ea * pltpu.roll(eb, shift, 0) + eb, eb)
        if 2 * s < ng:
            ea = jnp.where(keep, ea * pltpu.roll(ea, shift, 0), ea)
        s *= 2
    edge_s[...] = _shift_up(eb, 1) if reverse else _shift_down(eb, 1)

    def eight_groups(i, carry):
        for k in range(8):
            j = i * 8 + k
            rows = pl.ds(pl.multiple_of(j * 8, 8), 8)
            out_ref[rows, :] = b_s[rows, :] + a_s[rows, :] * edge_s[pl.ds(j, 1), :]
        return carry

    lax.fori_loop(0, ng // 8, eight_groups, 0)


def _neg_expm1(v):
    series = -v * (1.0 + v * (0.5 + v * (1.0 / 6.0)))
    return jnp.where(v > -0.015625, series, 1.0 - jnp.exp(v))


def _softplus_neg(lam):
    return jnp.maximum(-lam, 0.0) + jnp.log(1.0 + jnp.exp(-jnp.abs(lam)))


def _lru_gates(x0, cw, cb, wa, ba, wx, bx, lam):
    taps = [_shift_down(x0, 3 - k) for k in range(3)] + [x0]
    u = cb + cw[3:4, :] * x0
    for k in range(3):
        u = u + cw[k:k + 1, :] * taps[k]
    ub = u.astype(bf16)
    r = _sigmoid_positive(_dot(ub, wa.astype(bf16), NN) + ba)
    i = _sigmoid(_dot(ub, wx.astype(bf16), NN) + bx)
    sp = _softplus_neg(lam)
    log_a = (-LRU_C) * r * sp
    a = jnp.exp(log_a)
    w = _neg_expm1(2.0 * log_a)
    inv_mult = lax.rsqrt(w)
    return u, ub, r, i, sp, a, w * inv_mult, inv_mult, taps


def _lru_specs(S, nb):
    col = lambda off: pl.BlockSpec((S, RB), lambda n, b, off=off: (b, off + n))
    vec = pl.BlockSpec((1, RB), lambda n, b: (0, n))
    wblk = pl.BlockSpec((None, RB, RB), lambda n, b: (n, 0, 0))
    cwblk = pl.BlockSpec((8, RB), lambda n, b: (n, 0))
    return col, vec, wblk, cwblk


def _lru_forward(proj, cw_full, conv_b, w_a, b_a, w_x, b_x, lam, S):
    T = proj.shape[0]
    nb = T // S
    col, vec, wblk, cwblk = _lru_specs(S, nb)

    def body(x0_ref, g_ref, cw_ref, cb_ref, wa_ref, ba_ref, wx_ref, bx_ref, lam_ref, y_ref, h_ref, a_s, b_s, edge_s):
        x0 = x0_ref[...].astype(f32)
        u, ub, r, i, sp, a, mult, _, _ = _lru_gates(x0, cw_ref[...], cb_ref[...], wa_ref[...], ba_ref[...],
                                                    wx_ref[...], bx_ref[...], lam_ref[...])
        _linear_scan(a, mult * (i * u), a_s, b_s, edge_s, h_ref, reverse=False)
        g = g_ref[...].astype(f32)
        y_ref[...] = (h_ref[...] * (g * _sigmoid(g))).astype(bf16)

    out = pl.BlockSpec((S, RB), lambda n, b: (b, n))
    return _pcall(
        body, name="lru_forward", grid=(RNN_BLOCKS, nb),
        in_specs=[col(0), col(8), cwblk, vec, wblk, vec, wblk, vec, vec],
        out_specs=(out, out), out_shape=(_sds((T, D), bf16), _sds((T, D), f32)),
        scratch_shapes=[pltpu.VMEM((S, RB), f32), pltpu.VMEM((S, RB), f32), pltpu.VMEM((S // 8, RB), f32)],
        compiler_params=_params(("arbitrary", "arbitrary")),
    )(proj, proj, cw_full, conv_b, w_a, b_a, w_x, b_x, lam)


def _rope_tables(S):
    pos = jnp.arange(S, dtype=f32)
    inv_freq = ROPE_THETA ** (-jnp.arange(0, ROPE_DIM, 2, dtype=f32) / ROPE_DIM)
    ang = pos[:, None] * inv_freq[None, :]
    cos, sin = jnp.cos(ang), jnp.sin(ang)
    lane = jnp.arange(128) % HEAD
    cosl, sinl = cos[:, lane % 8], sin[:, lane % 8]
    c = jnp.where(lane[None, :] < ROPE_DIM, cosl, 1.0)
    s1 = jnp.where(lane[None, :] < 8, -sinl, 0.0)
    s2 = jnp.where((lane[None, :] >= 8) & (lane[None, :] < ROPE_DIM), sinl, 0.0)
    return c.astype(f32), s1.astype(f32), s2.astype(f32)


def _heads_to_rows(t):
    return jnp.concatenate([t[:, HEAD * h:HEAD * (h + 1)] for h in range(GROUP)], axis=0)


def _rows_to_heads(t):
    return jnp.concatenate([t[QB * h:QB * (h + 1), :] for h in range(GROUP)], axis=1)


def _window_bias(first_block):
    shape = (GROUP * QB, 2 * QB)
    qi = _rows_iota(shape) % QB
    cj = lax.broadcasted_iota(jnp.int32, shape, 1)
    valid = (cj > qi) & (cj <= qi + QB) & ((cj >= QB) | jnp.logical_not(first_block))
    return jnp.where(valid, 0.0, -jnp.inf)


def _attn_probs(q_rows, k_cat, sink_col, bias):
    s = _dot(q_rows, k_cat, NT) + bias
    m = jnp.maximum(jnp.max(s, axis=1, keepdims=True), sink_col)
    p = jnp.exp(s - m)
    e_sink = jnp.exp(sink_col - m)
    inv = 1.0 / (jnp.sum(p, axis=1, keepdims=True) + e_sink)
    return p * inv, e_sink * inv


def _sink_column(sink_ref, kv):
    rid = _rows_iota((GROUP * QB, 1))
    col = jnp.zeros((GROUP * QB, 1), f32)
    for h in range(GROUP):
        col = jnp.where(rid // QB == h, sink_ref[0, GROUP * kv + h], col)
    return col


def _attn_in_specs(S):
    nq = S // QB
    last = nq - 1
    cur = lambda b, j: b * nq + jnp.minimum(j, last)
    prev = lambda b, j: b * nq + jnp.maximum(jnp.minimum(j, last) - 1, 0)
    specs = [
        pl.BlockSpec((QB, D), lambda b, j: (cur(b, j), 2)),
        pl.BlockSpec((QB, 256), lambda b, j: (cur(b, j), 12)),
        pl.BlockSpec((QB, 256), lambda b, j: (prev(b, j), 12)),
        pl.BlockSpec((QB, 256), lambda b, j: (cur(b, j), 13)),
        pl.BlockSpec((QB, 256), lambda b, j: (prev(b, j), 13)),
        pl.BlockSpec((QB, 512), lambda b, j: (cur(b, j), 7)),
        pl.BlockSpec((QB, 512), lambda b, j: (cur(b, j), 8)),
        SMEM_SPEC,
    ]
    return specs, cur, prev


def _attn_forward(proj, sinks, S, out_shards):
    T = proj.shape[0]
    nb, nq = T // S, S // QB
    specs, cur, _ = _attn_in_specs(S)
    nw = len(out_shards)

    def body(q_ref, kc_ref, kp_ref, vc_ref, vp_ref, gl_ref, gh_ref, sink_ref, *rest):
        shards = rest[:nw]
        y_ref, o_ref = rest[nw:nw + 2]
        gathered = rest[nw + 2:2 * nw + 2]
        stage = rest[2 * nw + 2:3 * nw + 2]
        sems = rest[3 * nw + 2:]
        b, j = pl.program_id(0), pl.program_id(1)

        @pl.when((b == 0) & (j == 0))
        def _():
            for a in range(nw):
                stage[a][...] = shards[a][...].astype(bf16)
            _start_all(_direct_gather_copies(stage, gathered, *sems))

        @pl.when((b == nb - 1) & (j == nq - 1))
        def _():
            _wait_all(_direct_gather_copies(stage, gathered, *sems))

        bias = _window_bias(j == 0)
        kc, kp, vc, vp = kc_ref[...], kp_ref[...], vc_ref[...], vp_ref[...]
        for kv in range(KV_HEADS):
            lanes = slice(256 * kv, 256 * (kv + 1))
            hl = slice(HEAD * kv, HEAD * (kv + 1))
            q_rows = _heads_to_rows(q_ref[:, lanes])
            k_cat = jnp.concatenate([kp[:, hl], kc[:, hl]], axis=0)
            v_cat = jnp.concatenate([vp[:, hl], vc[:, hl]], axis=0)
            probs, _ = _attn_probs(q_rows, k_cat, _sink_column(sink_ref, kv), bias)
            o = _rows_to_heads(_dot(probs.astype(bf16), v_cat, NN))
            g_src = gl_ref if kv < 2 else gh_ref
            g = g_src[:, 256 * (kv % 2):256 * (kv % 2 + 1)].astype(f32)
            y_ref[:, lanes] = (o * (g * _sigmoid(g))).astype(bf16)
            o_ref[:, lanes] = o.astype(bf16)

    args = [proj] * 7 + [sinks] + list(out_shards)
    tok = pl.BlockSpec((QB, D), lambda b, j: (cur(b, j), 0))
    res = _pcall(
        body, name="attn_forward", grid=(nb, nq),
        in_specs=specs + [pl.BlockSpec(w.shape, lambda b, j: (0, 0)) for w in out_shards],
        out_specs=(tok, tok) + tuple([ANY] * nw),
        out_shape=(_sds((T, D), bf16), _sds((T, D), bf16))
        + tuple(_sds((NDEV * w.shape[0], w.shape[1]), bf16) for w in out_shards),
        scratch_shapes=[pltpu.VMEM(w.shape, bf16) for w in out_shards] + _exchange_scratch(nw, 7),
        compiler_params=_params(("arbitrary", "arbitrary")),
    )(*args)
    return res[0], res[1], res[2:]


def _merge_and_head(x2d, tgt, proj, y_rnn, y_attn, w_r, w_a, w_o, gfin):
    T = x2d.shape[0]
    tb = min(T, 512)
    nsteps = T // tb

    def body(x_ref, t_ref, mr0, mr1, ma0, ma1, yr_ref, ya_ref, wr_ref, wa_ref, wo_ref, gf_ref,
             dx2_ref, dyr_ref, dya_ref, dmr_ref, dma_ref, loss_ref, gfin_ref, gwr_out, gwa_out, gwo_out,
             gwr_acc, gwa_acc, gwo_acc, out_sems):
        step = pl.program_id(0)

        @pl.when(step == 0)
        def _():
            loss_ref[...] = jnp.zeros_like(loss_ref)
            gfin_ref[...] = jnp.zeros_like(gfin_ref)
            gwr_acc[...] = jnp.zeros_like(gwr_acc)
            gwa_acc[...] = jnp.zeros_like(gwa_acc)
            gwo_acc[...] = jnp.zeros_like(gwo_acc)

        sr = _sigmoid(jnp.concatenate([mr0[...], mr1[...]], axis=1).astype(f32))
        sa = _sigmoid(jnp.concatenate([ma0[...], ma1[...]], axis=1).astype(f32))
        p_r = _dot(yr_ref[...], wr_ref[...], NN)
        p_a = _dot(ya_ref[...], wa_ref[...], NN)
        merged = (sr * p_r + sa * p_a).astype(bf16)
        x2 = x_ref[...] + _dot(merged, wo_ref[...], NN)
        rstd = lax.rsqrt(jnp.mean(x2 * x2, axis=-1, keepdims=True) + EPS)
        xh = x2 * rstd
        gf = gf_ref[...]
        err = xh * gf - t_ref[...]
        loss_ref[...] += jnp.sum(err * err)
        dy = err * (1.0 / D)
        gfin_ref[0:1, :] += jnp.sum(dy * xh, axis=0, keepdims=True)
        dxn = dy * gf
        dx2 = rstd * (dxn - xh * jnp.mean(dxn * xh, axis=-1, keepdims=True))
        dx2_ref[...] = dx2
        dx2b = dx2.astype(bf16)
        dmerged = _dot(dx2b, wo_ref[...], NT)
        dmr_ref[...] = (dmerged * p_r * (sr * (1.0 - sr))).astype(bf16)
        dma_ref[...] = (dmerged * p_a * (sa * (1.0 - sa))).astype(bf16)
        dpr = (dmerged * sr).astype(bf16)
        dpa = (dmerged * sa).astype(bf16)
        dyr_ref[...] = _dot(dpr, wr_ref[...], NT).astype(bf16)
        dya_ref[...] = _dot(dpa, wa_ref[...], NT).astype(bf16)
        gwr_acc[...] += _dot(yr_ref[...], dpr, TN)
        gwa_acc[...] += _dot(ya_ref[...], dpa, TN)
        gwo_acc[...] += _dot(merged, dx2b, TN)

        @pl.when(step == nsteps - 1)
        def _():
            copies = [pltpu.make_async_copy(src, dst, out_sems.at[k]) for k, (src, dst) in enumerate(
                ((gwr_acc, gwr_out), (gwa_acc, gwa_out), (gwo_acc, gwo_out)))]
            for cp in copies:
                cp.start()
            for cp in copies:
                cp.wait()

    tok = pl.BlockSpec((tb, D), lambda i: (i, 0))
    half = lambda c: pl.BlockSpec((tb, CH), lambda i, c=c: (i, c))
    wfull = pl.BlockSpec((D, D), lambda i: (0, 0), pipeline_mode=pl.Buffered(1))
    acc = pl.BlockSpec((8, D), lambda i: (0, 0))
    return _pcall(
        body, name="merge_and_head", grid=(nsteps,),
        in_specs=[tok, tok, half(9), half(10), half(11), half(12), tok, tok, wfull, wfull, wfull,
                  pl.BlockSpec((1, D), lambda i: (0, 0))],
        out_specs=(tok, tok, tok, tok, tok, acc, acc, ANY, ANY, ANY),
        out_shape=(_sds((T, D), f32), _sds((T, D), bf16), _sds((T, D), bf16), _sds((T, D), bf16),
                   _sds((T, D), bf16), _sds((8, D), f32), _sds((8, D), f32),
                   _sds((D, D), f32), _sds((D, D), f32), _sds((D, D), f32)),
        scratch_shapes=[pltpu.VMEM((D, D), f32)] * 3 + [pltpu.SemaphoreType.DMA((3,))],
        compiler_params=_params(("arbitrary",)),
    )(x2d, tgt, proj, proj, proj, proj, y_rnn, y_attn, w_r, w_a, w_o, gfin)


def _attn_backward(proj, dy_attn, o_attn, tabs, sinks, S, chip_sums):
    T = proj.shape[0]
    nb, nq = T // S, S // QB
    nex = len(chip_sums)
    specs, cur, prev = _attn_in_specs(S)
    last = nq - 1
    tab_cur = pl.BlockSpec((QB, 128), lambda b, j: (jnp.minimum(j, last), 0))
    tab_prev = pl.BlockSpec((QB, 128), lambda b, j: (jnp.maximum(jnp.minimum(j, last) - 1, 0), 0))
    specs = specs + [pl.BlockSpec((QB, D), lambda b, j: (cur(b, j), 0))] * 2 + [tab_cur] * 3 + [tab_prev] * 3
    q_scale = 1.0 / math.sqrt(HEAD)

    def rope_back(dt, tab):
        return jnp.concatenate([_rope_transposed(dt[:, 128 * l:128 * (l + 1)], *tab) for l in range(2)], axis=1)

    def body(q_ref, kc_ref, kp_ref, vc_ref, vp_ref, gl_ref, gh_ref, sink_ref, dy_ref, o_ref, cc, s1c, s2c, cp, s1p,
             s2p, *rest):
        ex_src = rest[:nex]
        dq_ref, dkv_ref, dg_ref, dsink_ref = rest[nex:nex + 4]
        ex_dst = rest[nex + 4:2 * nex + 4]
        carry_k, carry_v = rest[2 * nex + 4:2 * nex + 6]
        sems = rest[2 * nex + 6:]
        b, j = pl.program_id(0), pl.program_id(1)

        @pl.when((b == 0) & (j == 0))
        def _():
            dsink_ref[...] = jnp.zeros_like(dsink_ref)
            _start_all(_chip_exchange_copies(ex_src, ex_dst, *sems))

        @pl.when((b == nb - 1) & (j == nq))
        def _():
            _wait_all(_chip_exchange_copies(ex_src, ex_dst, *sems))

        @pl.when(j == 0)
        def _():
            carry_k[...] = jnp.zeros_like(carry_k)
            carry_v[...] = jnp.zeros_like(carry_v)

        @pl.when(j < nq)
        def _():
            bias = _window_bias(j == 0)
            tc = (cc[...], s1c[...], s2c[...])
            tp = (cp[...], s1p[...], s2p[...])
            kc, kp, vc, vp = kc_ref[...], kp_ref[...], vc_ref[...], vp_ref[...]
            dk_prev, dk_cur, dv_prev, dv_cur = [], [], [], []
            dsink_acc = jnp.zeros((8, 128), f32)
            r8 = lax.broadcasted_iota(jnp.int32, (8, 128), 0)
            l8 = lax.broadcasted_iota(jnp.int32, (8, 128), 1)
            for kv in range(KV_HEADS):
                lanes = slice(256 * kv, 256 * (kv + 1))
                hl = slice(HEAD * kv, HEAD * (kv + 1))
                q_rows = _heads_to_rows(q_ref[:, lanes])
                k_cat = jnp.concatenate([kp[:, hl], kc[:, hl]], axis=0)
                v_cat = jnp.concatenate([vp[:, hl], vc[:, hl]], axis=0)
                probs, p_sink = _attn_probs(q_rows, k_cat, _sink_column(sink_ref, kv), bias)
                pb = probs.astype(bf16)
                o = o_ref[:, lanes].astype(f32)
                g_src = gl_ref if kv < 2 else gh_ref
                g = g_src[:, 256 * (kv % 2):256 * (kv % 2 + 1)].astype(f32)
                sg = _sigmoid(g)
                dy = dy_ref[:, lanes].astype(f32)
                dg_ref[:, lanes] = (dy * o * (sg * (1.0 + g * (1.0 - sg)))).astype(bf16)
                do_rows = _heads_to_rows(dy * (g * sg)).astype(bf16)
                dv = _dot(pb, do_rows, TN)
                dp = _dot(do_rows, v_cat, NT)
                rowdot = jnp.sum(probs * dp, axis=1, keepdims=True)
                ds = (probs * (dp - rowdot)).astype(bf16)
                sink_rows = -(p_sink * rowdot)
                for h in range(GROUP):
                    val = jnp.sum(sink_rows[QB * h:QB * (h + 1), :])
                    dsink_acc = dsink_acc + jnp.where((r8 == 0) & (l8 == GROUP * kv + h), val, 0.0)
                dq = _rows_to_heads(_dot(ds, k_cat, NN)) * q_scale
                dq_ref[:, lanes] = rope_back(dq, tc).astype(bf16)
                dk = _dot(ds, q_rows, TN)
                dk_prev.append(dk[:QB, :])
                dk_cur.append(dk[QB:, :])
                dv_prev.append(dv[:QB, :])
                dv_cur.append(dv[QB:, :])
            dsink_ref[...] += dsink_acc
            dkp = rope_back(jnp.concatenate(dk_prev, axis=1), tp)
            dkc = rope_back(jnp.concatenate(dk_cur, axis=1), tc)
            dkv_ref[:, 0:256] = (carry_k[...] + dkp).astype(bf16)
            dkv_ref[:, 256:512] = (carry_v[...] + jnp.concatenate(dv_prev, axis=1)).astype(bf16)
            carry_k[...] = dkc
            carry_v[...] = jnp.concatenate(dv_cur, axis=1)

        @pl.when(j == nq)
        def _():
            dkv_ref[:, 0:256] = carry_k[...].astype(bf16)
            dkv_ref[:, 256:512] = carry_v[...].astype(bf16)

    lag = lambda b, j: (b * nq + jnp.maximum(j - 1, 0), 0)
    args = [proj] * 7 + [sinks, dy_attn, o_attn] + list(tabs) + list(tabs) + list(chip_sums)
    res = _pcall(
        body, name="attn_backward", grid=(nb, nq + 1), in_specs=specs + [ANY] * nex,
        out_specs=(pl.BlockSpec((QB, D), lambda b, j: (cur(b, j), 0)), pl.BlockSpec((QB, 512), lag),
                   pl.BlockSpec((QB, D), lambda b, j: (cur(b, j), 0)), pl.BlockSpec((8, 128), lambda b, j: (0, 0)))
        + tuple([ANY] * nex),
        out_shape=(_sds((T, D), bf16), _sds((T, 512), bf16), _sds((T, D), bf16), _sds((8, 128), f32))
        + tuple(_sds(s.shape, s.dtype) for s in chip_sums),
        scratch_shapes=[pltpu.VMEM((QB, 256), f32), pltpu.VMEM((QB, 256), f32)] + _exchange_scratch(nex, 3),
        compiler_params=_params(("arbitrary", "arbitrary")),
    )(*args)
    return res[:4], res[4:]


def _lru_backward(proj, h_all, dy_rnn, cw_full, conv_b, w_a, b_a, w_x, b_x, lam, S):
    T = proj.shape[0]
    nb = T // S
    col, vec, wblk, cwblk = _lru_specs(S, nb)
    tokblk = pl.BlockSpec((S, RB), lambda n, b: (b, n))

    def body(x0_ref, g_ref, h_ref, dy_ref, cw_ref, cb_ref, wa_ref, ba_ref, wx_ref, bx_ref, lam_ref,
             du0_ref, dg_ref, gwa_ref, gwx_ref, vec_ref, gcw_ref, a_s, b_s, dh_s, edge_s):
        @pl.when(pl.program_id(1) == 0)
        def _():
            gwa_ref[...] = jnp.zeros_like(gwa_ref)
            gwx_ref[...] = jnp.zeros_like(gwx_ref)
            vec_ref[...] = jnp.zeros_like(vec_ref)
            gcw_ref[...] = jnp.zeros_like(gcw_ref)

        x0 = x0_ref[...].astype(f32)
        cw = cw_ref[...]
        lam_v = lam_ref[...]
        u, ub, r, i, sp, a, mult, inv_mult, taps = _lru_gates(x0, cw, cb_ref[...], wa_ref[...], ba_ref[...],
                                                              wx_ref[...], bx_ref[...], lam_v)
        h = h_ref[...]
        g = g_ref[...].astype(f32)
        dy = dy_ref[...].astype(f32)
        sg = _sigmoid(g)
        dg_ref[...] = (dy * h * (sg * (1.0 + g * (1.0 - sg)))).astype(bf16)
        _linear_scan(_shift_up(a, 1), dy * (g * sg), a_s, b_s, edge_s, dh_s, reverse=True)
        dh_total = dh_s[...]
        da = dh_total * _shift_down(h, 1)
        dmult = dh_total * (i * u)
        db = dh_total * mult
        di = db * u
        du = db * i
        dlog_a_c = ((-LRU_C) * a) * (da - dmult * (a * inv_mult))
        dr = dlog_a_c * sp
        dsp = jnp.sum(dlog_a_c * r, axis=0, keepdims=True)
        dpre_r = dr * r * (1.0 - r)
        dpre_i = di * i * (1.0 - i)
        dpre_rb = dpre_r.astype(bf16)
        dpre_ib = dpre_i.astype(bf16)
        du = du + _dot(dpre_rb, wa_ref[...].astype(bf16), NT) + _dot(dpre_ib, wx_ref[...].astype(bf16), NT)
        gwa_ref[...] += _dot(ub, dpre_rb, TN)
        gwx_ref[...] += _dot(ub, dpre_ib, TN)
        vec_ref[0:1, :] += jnp.sum(du, axis=0, keepdims=True)
        vec_ref[1:2, :] += jnp.sum(dpre_r, axis=0, keepdims=True)
        vec_ref[2:3, :] += jnp.sum(dpre_i, axis=0, keepdims=True)
        vec_ref[3:4, :] += dsp * (-_sigmoid(-lam_v))
        dx0 = cw[3:4, :] * du
        for k in range(3):
            dx0 = dx0 + cw[k:k + 1, :] * _shift_up(du, 3 - k)
        for k in range(4):
            gcw_ref[k:k + 1, :] += jnp.sum(du * taps[k], axis=0, keepdims=True)
        du0_ref[...] = dx0.astype(bf16)

    wacc = pl.BlockSpec((RB, RB), lambda n, b: (0, n))
    vacc = pl.BlockSpec((8, RB), lambda n, b: (0, n))
    cacc = pl.BlockSpec((8, RB), lambda n, b: (n, 0))
    return _pcall(
        body, name="lru_backward", grid=(RNN_BLOCKS, nb),
        in_specs=[col(0), col(8), tokblk, tokblk, cwblk, vec, wblk, vec, wblk, vec, vec],
        out_specs=(tokblk, tokblk, wacc, wacc, vacc, cacc),
        out_shape=(_sds((T, D), bf16), _sds((T, D), bf16), _sds((RB, D), f32), _sds((RB, D), f32),
                   _sds((8, D), f32), _sds((8 * RNN_BLOCKS, RB), f32)),
        scratch_shapes=[pltpu.VMEM((S, RB), f32)] * 3 + [pltpu.VMEM((S // 8, RB), f32)],
        compiler_params=_params(("arbitrary", "arbitrary")),
    )(proj, proj, h_all, dy_rnn, cw_full, conv_b, w_a, b_a, w_x, b_x, lam)


def _section_of_chunk(s):
    out = []
    for start, n in zip(SEC_START, SEC_CHUNKS):
        inside = (s >= start) & (s < start + n)
        out.append((inside, jnp.clip(s - start, 0, n - 1)))
    return out


EFFECT = pltpu.SideEffectType.DATAFLOW_SIDE_EFFECTING
HBM_SPEC = pl.BlockSpec(memory_space=pltpu.HBM)
SEM_SPEC = pl.BlockSpec(memory_space=pltpu.SEMAPHORE)


def _split_exchange_copies(src_ref, land_ref, send_sems, recv_sems):
    x, y, c = _my_place()
    copies = []
    for k in (3, 1, 2):
        px, py = (x + (k >> 1)) % 2, (y + (k & 1)) % 2
        copies.append(pltpu.make_async_remote_copy(
            src_ref=src_ref.at[2 * px + py], dst_ref=land_ref.at[k - 1], send_sem=send_sems[k - 1],
            recv_sem=recv_sems[k - 1], device_id=(px, py, c), device_id_type=MESH))
    return copies


def _exchange_start(chip_sum):
    _, r, cols = chip_sum.shape

    def body(src_ref, land_ref, s0, s1, s2, r0, r1, r2, src_thru, land_thru, token):
        for cp in _split_exchange_copies(src_ref, land_ref, (s0, s1, s2), (r0, r1, r2)):
            cp.start()
        token[...] = jnp.zeros_like(token)

    land = pltpu.with_memory_space_constraint(lax.empty((3, r, cols), chip_sum.dtype), pltpu.HBM)
    res = _pcall(
        body, name="exchange_start",
        out_shape=tuple([pltpu.SemaphoreType.DMA(())] * 6) + (
            pltpu.HBM(chip_sum.shape, chip_sum.dtype), pltpu.HBM((3, r, cols), chip_sum.dtype), _sds((8, 128), f32)),
        in_specs=(HBM_SPEC, HBM_SPEC), out_specs=tuple([SEM_SPEC] * 6) + (HBM_SPEC, HBM_SPEC, VMEM_SPEC),
        input_output_aliases={0: 6, 1: 7},
        compiler_params=pltpu.CompilerParams(has_side_effects=EFFECT),
    )(pltpu.with_memory_space_constraint(chip_sum, pltpu.HBM), land)
    return res[:6], res[6], res[7], res[8]


def _exchange_wait(sems, src_thru, land_thru, after):
    def body(src_ref, land_ref, s0, s1, s2, r0, r1, r2, after_ref, src_dead, got_ref):
        for cp in _split_exchange_copies(src_ref, land_ref, (s0, s1, s2), (r0, r1, r2)):
            cp.wait_send()
            cp.wait_recv()

    return _pcall(
        body, name="exchange_wait",
        out_shape=(pltpu.HBM(src_thru.shape, src_thru.dtype), pltpu.HBM(land_thru.shape, land_thru.dtype)),
        in_specs=(HBM_SPEC, HBM_SPEC) + tuple([SEM_SPEC] * 6) + (ANY,), out_specs=(HBM_SPEC, HBM_SPEC),
        input_output_aliases={0: 0, 1: 1},
        compiler_params=pltpu.CompilerParams(has_side_effects=EFFECT),
    )(src_thru, land_thru, *sems, after)[1]


def _input_grad(dsecs, wt_full, x2d, dx2, norm_g):
    T = x2d.shape[0]
    tb = min(T, 512)
    nsec = len(dsecs)
    ntok = T // tb

    def body(*refs):
        secs = refs[:nsec]
        wt_ref, x_ref, dx2_ref, g_ref, dx_ref, gnorm_ref = refs[nsec:]
        i = pl.program_id(0)

        @pl.when(i == 0)
        def _():
            gnorm_ref[...] = jnp.zeros_like(gnorm_ref)

        dh = None
        for a, (start, n) in enumerate(zip(SEC_START, SEC_CHUNKS)):
            part = _dot(secs[a][...], wt_ref[CH * start:CH * (start + n), :], NN)
            dh = part if dh is None else dh + part
        xv = x_ref[...]
        rstd = lax.rsqrt(jnp.mean(xv * xv, axis=-1, keepdims=True) + EPS)
        xh = xv * rstd
        gnorm_ref[0:1, :] += jnp.sum(dh * xh, axis=0, keepdims=True)
        dxn = dh * g_ref[...]
        dx_ref[...] = dx2_ref[...] + rstd * (dxn - xh * jnp.mean(dxn * xh, axis=-1, keepdims=True))

    tok = pl.BlockSpec((tb, D), lambda i: (i, 0))
    return _pcall(
        body, name="input_grad", grid=(ntok,),
        in_specs=[pl.BlockSpec((tb, sec.shape[1]), lambda i: (i, 0)) for sec in dsecs]
        + [pl.BlockSpec((D_IN, D), lambda i: (0, 0), pipeline_mode=pl.Buffered(1)), tok, tok,
           pl.BlockSpec((1, D), lambda i: (0, 0))],
        out_specs=(tok, pl.BlockSpec((8, D), lambda i: (0, 0))),
        out_shape=(_sds((T, D), f32), _sds((8, D), f32)),
        compiler_params=_params(("arbitrary",)),
    )(*dsecs, wt_full, x2d, dx2, norm_g)


def _w_in_grad(dsecs, h_bf):
    T = h_bf.shape[0]
    tk = min(T, 2048)
    nchunks = D_IN // CH
    nsec = len(dsecs)
    nt = T // tk

    def body(*refs):
        secs = refs[:nsec]
        h_ref, out_ref, acc = refs[nsec:]
        s, t = pl.program_id(0), pl.program_id(1)

        @pl.when(t == 0)
        def _():
            acc[...] = jnp.zeros_like(acc)

        h_rows = h_ref[pl.ds(pl.multiple_of(t * tk, tk), tk), :]
        for a, (start, n) in enumerate(zip(SEC_START, SEC_CHUNKS)):
            @pl.when((s >= start) & (s < start + n))
            def _(a=a):
                acc[...] += _dot(secs[a][...], h_rows, TN)

        @pl.when(t == nt - 1)
        def _():
            out_ref[...] = acc[...].astype(bf16)

    def sec_spec(a):
        def index(s, t, a=a):
            inside, local = _section_of_chunk(s)[a]
            return (jnp.where(inside, t, 0), local)
        return pl.BlockSpec((tk, CH), index)

    return _pcall(
        body, name="w_in_grad", grid=(nchunks, T // tk),
        in_specs=[sec_spec(a) for a in range(nsec)]
        + [pl.BlockSpec((T, D), lambda s, t: (0, 0), pipeline_mode=pl.Buffered(1))],
        out_specs=pl.BlockSpec((CH, D), lambda s, t: (s, 0)), out_shape=_sds((D_IN, D), bf16),
        scratch_shapes=[pltpu.VMEM((CH, D), f32)],
        compiler_params=_params(("arbitrary", "arbitrary")),
    )(*dsecs, h_bf)


SMALL_NAMES = ("lru_w_a", "lru_w_x", "conv_b", "lru_b_a", "lru_b_x", "lru_lambda", "norm_g", "final_norm_g",
               "attn_sinks", "conv_w")
MISC_ROW = {"conv_b": 0, "lru_b_a": 1, "lru_b_x": 2, "lru_lambda": 3, "norm_g": 8, "final_norm_g": 16,
            "attn_sinks": 24, "loss": 32}


def _small_step(gwa, gwx, gvec, gnorm_blk, gfin_blk, dsink_blk, loss_blk, gcw, params):
    srcs_rows = (RB // NDEV, RB // NDEV, 8, 8)
    flat = [t for n in SMALL_NAMES for t in params[n]]
    nout = 4 * len(SMALL_NAMES) + 1

    ra_, rx_, rm_, rc_ = srcs_rows
    rh, rf = ra_ + rx_, rm_ + rc_

    def reduce_body(gwa_ref, gwx_ref, gvec_ref, gnorm_ref, gfin_ref, dsink_ref, loss_ref, gcw_ref,
                    all_a, all_x, all_m, conv_out,
                    misc, out_h, out_f, in_h, in_f, mine_h, mine_f, every_h, every_f, sa, ra, sb, rb):
        x, y, c = _my_place()
        me = 4 * x + 2 * y + c

        misc[...] = jnp.zeros_like(misc)
        misc[0:8, :] = gvec_ref[...]
        misc[8:16, :] = gnorm_ref[...]
        misc[16:24, :] = gfin_ref[...]
        misc[24:32, 0:128] = dsink_ref[...]
        misc[32:40, :] = loss_ref[...]

        out_f[...] = jnp.zeros_like(out_f)
        for d in range(NDEV):
            out_h[d, 0:ra_, :] = gwa_ref[ra_ * d:ra_ * (d + 1), :].astype(bf16)
            out_h[d, ra_:rh, :] = gwx_ref[rx_ * d:rx_ * (d + 1), :].astype(bf16)
            out_f[d, 0:rm_, :] = misc[rm_ * d:rm_ * (d + 1), :]
            out_f[d, rm_:rf, 0:RB] = gcw_ref[rc_ * d:rc_ * (d + 1), :]

        def both(k, src_h, dst_h, src_f, dst_f, send, recv, peer):
            return [pltpu.make_async_remote_copy(src_ref=s_, dst_ref=d_, send_sem=send.at[2 * (k - 1) + t],
                                                 recv_sem=recv.at[2 * (k - 1) + t], device_id=peer,
                                                 device_id_type=MESH)
                    for t, (s_, d_) in enumerate(((src_h, dst_h), (src_f, dst_f)))]

        scatter = []
        for k in range(1, NDEV):
            px, py, pc = _peer(k)
            dev = 4 * px + 2 * py + pc
            scatter += both(k, out_h.at[dev], in_h.at[k - 1], out_f.at[dev], in_f.at[k - 1], sa, ra, (px, py, pc))
        for cp in scatter:
            cp.start()
        for cp in scatter:
            cp.wait()

        total_h = out_h[me].astype(f32)
        total_f = out_f[me]
        for k in range(NDEV - 1):
            total_h = total_h + in_h[k].astype(f32)
            total_f = total_f + in_f[k]
        conv_out[...] = total_f[rm_:rf, 0:RB]
        mine_h[...] = total_h.astype(bf16)
        mine_f[...] = total_f[0:rm_, :]
        every_h[me] = total_h.astype(bf16)
        every_f[me] = total_f[0:rm_, :]
        gather = []
        for k in range(1, NDEV):
            gather += both(k, mine_h, every_h.at[me], mine_f, every_f.at[me], sb, rb, _peer(k))
        for cp in gather:
            cp.start()
        for cp in gather:
            cp.wait()
        for d in range(NDEV):
            all_a[ra_ * d:ra_ * (d + 1), :] = every_h[d, 0:ra_, :].astype(f32)
            all_x[rx_ * d:rx_ * (d + 1), :] = every_h[d, ra_:rh, :].astype(f32)
            all_m[rm_ * d:rm_ * (d + 1), :] = every_f[d]

    def adam_body(*refs):
        all_a, all_x, all_m, conv_ref = refs[:4]
        prm = {n: refs[4 + 3 * k:7 + 3 * k] for k, n in enumerate(SMALL_NAMES)}
        nin = 4 + len(flat)
        outs = {n: refs[nin + 4 * k:nin + 4 * k + 4] for k, n in enumerate(SMALL_NAMES)}
        loss_out = refs[nin + nout - 1]
        g_conv = conv_ref[0:4, :]

        def update(name, g, pick=lambda r: r[...]):
            w_ref, m_ref, v_ref = prm[name]
            delta, m_new, v_new = _adam_math(g, pick(w_ref), pick(m_ref), pick(v_ref))
            return g, delta, m_new, v_new

        for n in range(RNN_BLOCKS):
            lanes = slice(RB * n, RB * (n + 1))
            for name, full in (("lru_w_a", all_a), ("lru_w_x", all_x)):
                for out, val in zip(outs[name], update(name, full[:, lanes], pick=lambda r, n=n: r[n])):
                    out[n] = val
        for name in ("conv_b", "lru_b_a", "lru_b_x", "lru_lambda", "norm_g", "final_norm_g"):
            row = MISC_ROW[name]
            for out, val in zip(outs[name], update(name, all_m[row:row + 1, :])):
                out[...] = val
        row = MISC_ROW["attn_sinks"]
        for out, val in zip(outs["attn_sinks"], update("attn_sinks", all_m[row:row + 1, 0:16])):
            out[...] = val
        for out, val in zip(outs["conv_w"], update("conv_w", g_conv)):
            out[...] = val
        row = MISC_ROW["loss"]
        loss_out[...] = all_m[row:row + 8, 0:128] * (0.5 / D)

    scratch = [pltpu.VMEM((64, D), f32), pltpu.VMEM((NDEV, rh, D), bf16), pltpu.VMEM((NDEV, rf, D), f32),
               pltpu.VMEM((NDEV - 1, rh, D), bf16), pltpu.VMEM((NDEV - 1, rf, D), f32),
               pltpu.VMEM((rh, D), bf16), pltpu.VMEM((rm_, D), f32),
               pltpu.VMEM((NDEV, rh, D), bf16), pltpu.VMEM((NDEV, rm_, D), f32)
               ] + [pltpu.SemaphoreType.DMA((2 * (NDEV - 1),))] * 4
    sums = _pcall(
        reduce_body, name="small_reduce",
        out_shape=(_sds((RB, D), f32), _sds((RB, D), f32), _sds((64, D), f32), _sds((8, RB), f32)),
        in_specs=[VMEM_SPEC] * 8, out_specs=tuple([VMEM_SPEC] * 4),
        scratch_shapes=scratch, compiler_params=_params(),
    )(gwa, gwx, gvec, gnorm_blk, gfin_blk, dsink_blk, loss_blk, gcw)
    out_shape = tuple(_sds(params[n][0].shape, f32) for n in SMALL_NAMES for _ in range(4)) + (_sds((8, 128), f32),)
    res = _pcall(
        adam_body, name="small_adamw", out_shape=out_shape,
        in_specs=[VMEM_SPEC] * (4 + len(flat)), out_specs=tuple([VMEM_SPEC] * nout), compiler_params=_params(),
    )(*sums, *flat)
    return {n: res[4 * k:4 * k + 4] for k, n in enumerate(SMALL_NAMES)}, res[-1]


def _pad_rows(v, rows=8):
    return jnp.concatenate([v, jnp.zeros((rows - v.shape[0], v.shape[1]), v.dtype)], axis=0)


def kernel(x, norm_g, w_in, conv_w, conv_b, lru_w_a, lru_b_a, lru_w_x, lru_b_x, lru_lambda, attn_sinks, w_rnn_out, w_attn_out, w_o, final_norm_g, loss_target, m_norm_g, m_w_in, m_conv_w, m_conv_b, m_lru_w_a, m_lru_b_a, m_lru_w_x, m_lru_b_x, m_lru_lambda, m_attn_sinks, m_w_rnn_out, m_w_attn_out, m_w_o, m_final_norm_g, v_norm_g, v_w_in, v_conv_w, v_conv_b, v_lru_w_a, v_lru_b_a, v_lru_w_x, v_lru_b_x, v_lru_lambda, v_attn_sinks, v_w_rnn_out, v_w_attn_out, v_w_o, v_final_norm_g):
    nb, S, _ = x.shape
    T = nb * S
    x2d = x.reshape(T, D)
    tgt = loss_target.reshape(T, D)
    fin_g = final_norm_g.reshape(1, D)
    w_a3, w_x3 = lru_w_a[0], lru_w_x[0]

    my_core = lax.axis_index("c").astype(jnp.int32).reshape(1)
    cx, cy = lax.axis_index("x"), lax.axis_index("y")
    chip_order = jnp.stack([2 * cx + cy, 2 * (1 - cx) + cy, 2 * cx + (1 - cy),
                            2 * (1 - cx) + (1 - cy)]).astype(jnp.int32)

    tabs = _rope_tables(S)
    h_bf, proj, wt_full, cw_full, _ = _in_proj_gather(
        x2d, norm_g, w_in[0].T.astype(bf16), _pad_rows(conv_w[0]), tabs, S, (), chip_order)
    y_rnn, h_all = _lru_forward(proj, cw_full, conv_b, w_a3, lru_b_a, w_x3, lru_b_x, lru_lambda, S)
    y_attn, o_attn, (wr_full, wa_full, wo_full) = _attn_forward(proj, attn_sinks, S,
                                                                (w_rnn_out[0], w_attn_out[0], w_o[0]))

    (dx2, dy_rnn, dy_attn, dmr, dma, loss_blk, gfin_blk, g_wr, g_wa, g_wo) = _merge_and_head(
        x2d, tgt, proj, y_rnn, y_attn, wr_full, wa_full, wo_full, fin_g)
    sums_out = _pair_sums([g_wr, g_wa, g_wo], bf16, my_core, "out")

    (dq, dkv, dga, dsink_blk), (p_wr, p_wa, p_wo) = _attn_backward(proj, dy_attn, o_attn, tabs, attn_sinks, S, sums_out)
    du0, dgr, gwa, gwx, gvec, gcw = _lru_backward(proj, h_all, dy_rnn, cw_full, conv_b, w_a3, lru_b_a, w_x3,
                                                  lru_b_x, lru_lambda, S)
    dsecs = (du0, dgr, dq, dkv, dga, dmr, dma)

    g_wt = _w_in_grad(dsecs, h_bf)
    (sum_in,) = _pair_sums([g_wt], bf16, my_core, "in")
    ex_sems, sum_in, landing, token = _exchange_start(sum_in)
    grad_x2d, gnorm_blk = _input_grad(dsecs, wt_full, x2d, dx2, norm_g + token[0, 0])
    p_wt = _exchange_wait(ex_sems, sum_in, landing, gnorm_blk)
    p_wt_own = lax.dynamic_index_in_dim(sum_in, 2 * cx + cy, axis=0, keepdims=False)

    small, loss_out = _small_step(gwa, gwx, gvec, gnorm_blk, gfin_blk, dsink_blk, loss_blk, gcw, {
        "lru_w_a": (w_a3, m_lru_w_a[0], v_lru_w_a[0]), "lru_w_x": (w_x3, m_lru_w_x[0], v_lru_w_x[0]),
        "conv_b": (conv_b, m_conv_b, v_conv_b), "lru_b_a": (lru_b_a, m_lru_b_a, v_lru_b_a),
        "lru_b_x": (lru_b_x, m_lru_b_x, v_lru_b_x), "lru_lambda": (lru_lambda, m_lru_lambda, v_lru_lambda),
        "norm_g": (norm_g, m_norm_g, v_norm_g),
        "final_norm_g": (fin_g, m_final_norm_g.reshape(1, D), v_final_norm_g.reshape(1, D)),
        "attn_sinks": (attn_sinks, m_attn_sinks, v_attn_sinks),
        "conv_w": (conv_w[0], m_conv_w[0], v_conv_w[0])})

    o_wt = _adamw(p_wt_own, p_wt, w_in[0].T, m_w_in[0].T, v_w_in[0].T, "adamw_w_in")
    o_wr, o_wa, o_wo = _adamw_group(
        (p_wr, p_wa, p_wo), (w_rnn_out[0], w_attn_out[0], w_o[0]),
        (m_w_rnn_out[0], m_w_attn_out[0], m_w_o[0]), (v_w_rnn_out[0], v_w_attn_out[0], v_w_o[0]), "adamw_w_out")

    def result(kind):
        d = {n: small[n][kind] for n in ("conv_b", "lru_b_a", "lru_b_x", "lru_lambda", "norm_g", "attn_sinks")}
        d.update({n: small[n][kind][None] for n in ("lru_w_a", "lru_w_x", "conv_w")})
        d["final_norm_g"] = small["final_norm_g"][kind].reshape(D)
        d.update({"w_in": o_wt[kind].T[None], "w_rnn_out": o_wr[kind][None], "w_attn_out": o_wa[kind][None],
                  "w_o": o_wo[kind][None]})
        return d

    order = ("norm_g", "w_in", "conv_w", "conv_b", "lru_w_a", "lru_b_a", "lru_w_x", "lru_b_x", "lru_lambda",
             "attn_sinks", "w_rnn_out", "w_attn_out", "w_o", "final_norm_g")
    outs = [loss_out[0, 0], grad_x2d.reshape(nb, S, D)]
    for kind in range(4):
        d = result(kind)
        outs += [d[n] for n in order]
    return tuple(outs)
```

```python
import math

import jax
import jax.numpy as jnp
from jax import lax
from jax.experimental import pallas as pl
from jax.experimental.pallas import tpu as pltpu

f32 = jnp.float32
bf16 = jnp.bfloat16

D = 1024
D_IN = 6656
NDEV = 8
RNN_BLOCKS = 8
RB = 128
HEAD = 64
KV_HEADS = 4
GROUP = 4
QB = 128
LRU_C = 8.0
EPS = 1e-6
ROPE_DIM = 16
ROPE_THETA = 500000.0
CH = 512
SEC_START = (0, 2, 4, 6, 7, 9, 11)
SEC_CHUNKS = (2, 2, 2, 1, 2, 2, 2)
VMEM_LIMIT = 62 * 1024 * 1024

ADAM_LR, ADAM_B1, ADAM_B2, ADAM_EPS, ADAM_WD, ADAM_STEP = 0.001, 0.9, 0.999, 1e-08, 0.01, 10

MESH = pl.DeviceIdType.MESH
ANY = pl.BlockSpec(memory_space=pl.ANY)
VMEM_SPEC = pl.BlockSpec(memory_space=pltpu.VMEM)
SMEM_SPEC = pl.BlockSpec(memory_space=pltpu.SMEM)


def _pcall(body, **kw):
    return pl.pallas_call(body, **kw)


def _params(sem=None, **kw):
    if sem is not None:
        kw["dimension_semantics"] = sem
    return pltpu.CompilerParams(vmem_limit_bytes=VMEM_LIMIT, **kw)


def _sds(shape, dtype):
    return jax.ShapeDtypeStruct(shape, dtype)


def _dot(a, b, dims):
    return lax.dot_general(a, b, (dims, ((), ())), preferred_element_type=f32)


NN = ((1,), (0,))
NT = ((1,), (1,))
TN = ((0,), (0,))


def _sigmoid(v):
    return 0.5 * jnp.tanh(0.5 * v) + 0.5


def _sigmoid_positive(v):
    return 1.0 / (1.0 + jnp.exp(-v))


def _my_place():
    return lax.axis_index("x"), lax.axis_index("y"), lax.axis_index("c")


def _peer(k):
    x, y, c = _my_place()
    return (x + ((k >> 2) & 1)) % 2, (y + ((k >> 1) & 1)) % 2, (c + (k & 1)) % 2


def _direct_gather_copies(srcs, outs, send_sems, recv_sems, local_sems):
    x, y, c = _my_place()
    me = 4 * x + 2 * y + c
    local, remote = [], []
    for a, (src, out) in enumerate(zip(srcs, outs)):
        r = src.shape[0]
        mine = out.at[pl.ds(pl.multiple_of(me * r, 8), r), :]
        local.append(pltpu.make_async_copy(src, mine, local_sems.at[a]))
        for k in range(1, NDEV):
            remote.append(pltpu.make_async_remote_copy(
                src_ref=src, dst_ref=mine, send_sem=send_sems.at[7 * a + k - 1], recv_sem=recv_sems.at[7 * a + k - 1],
                device_id=_peer(k), device_id_type=MESH))
    return local, remote


def _chip_exchange_copies(src, dst, send_sems, recv_sems, local_sems):
    x, y, c = _my_place()
    local, remote = [], []
    for a in range(len(src)):
        local.append(pltpu.make_async_copy(src[a].at[2 * x + y], dst[a].at[0], local_sems.at[a]))
    for k in (3, 1, 2):
        px, py = (x + (k >> 1)) % 2, (y + (k & 1)) % 2
        for a in range(len(src)):
            remote.append(pltpu.make_async_remote_copy(
                src_ref=src[a].at[2 * px + py], dst_ref=dst[a].at[k],
                send_sem=send_sems.at[3 * a + k - 1], recv_sem=recv_sems.at[3 * a + k - 1],
                device_id=(px, py, c), device_id_type=MESH))
    return local, remote


def _exchange_scratch(narr, per_array):
    return [pltpu.SemaphoreType.DMA((per_array * narr,)), pltpu.SemaphoreType.DMA((per_array * narr,)),
            pltpu.SemaphoreType.DMA((narr,))]


def _start_all(copies):
    local, remote = copies
    for cp in local + remote:
        cp.start()


def _wait_all(copies):
    local, remote = copies
    for cp in remote + local:
        cp.wait()


def _row_tile(rows, dtype):
    unit = 16 if dtype == bf16 else 8
    for cand in (256, 208, 128, 64, 40, 32, 16, 8):
        if rows % cand == 0 and cand % unit == 0:
            return cand
    return rows


def _pair_sums(grads, wire_dtype, my_core, tag):
    narr = len(grads)
    r, cols = grads[0].shape[0] // NDEV, grads[0].shape[1]
    views = [g.reshape(4, 2, r, cols) for g in grads]
    tr = _row_tile(r, wire_dtype)
    nt = r // tr

    def body(core_ref, *refs):
        mine = refs[:narr]
        whole = refs[narr:2 * narr]
        outs = refs[2 * narr:3 * narr]
        got = refs[3 * narr:4 * narr]
        send_sems, recv_sems = refs[4 * narr:]
        q, i = pl.program_id(0), pl.program_id(1)
        x, y, c = _my_place()

        def copy(a, chip):
            return pltpu.make_async_remote_copy(
                src_ref=whole[a].at[chip, 1 - c], dst_ref=got[a].at[chip],
                send_sem=send_sems.at[4 * a + chip], recv_sem=recv_sems.at[4 * a + chip],
                device_id=(x, y, 1 - c), device_id_type=MESH)

        @pl.when((q == 0) & (i == 0))
        def _():
            for chip in range(4):
                for a in range(narr):
                    copy(a, chip).start()

        for chip in range(4):
            @pl.when((q == chip) & (i == 0))
            def _(chip=chip):
                for a in range(narr):
                    copy(a, chip).wait_recv()

        rows = pl.ds(pl.multiple_of(i * tr, tr), tr)
        for a in range(narr):
            outs[a][...] = (mine[a][...].astype(f32) + got[a][q, rows, :].astype(f32)).astype(wire_dtype)

        @pl.when((q == 3) & (i == nt - 1))
        def _():
            for chip in range(4):
                for a in range(narr):
                    copy(a, chip).wait_send()

    slab = pl.BlockSpec((None, tr, cols), lambda q, i, core: (q, i, 0))
    grid_spec = pltpu.PrefetchScalarGridSpec(
        num_scalar_prefetch=1, grid=(4, nt),
        in_specs=[pl.BlockSpec((None, None, tr, cols), lambda q, i, core: (q, core[0], i, 0))] * narr + [ANY] * narr,
        out_specs=tuple([slab] * narr),
        scratch_shapes=[pltpu.VMEM((4, r, cols), grads[0].dtype)] * narr
        + [pltpu.SemaphoreType.DMA((4 * narr,)), pltpu.SemaphoreType.DMA((4 * narr,))])
    return _pcall(body, name="pair_sums_" + tag, grid_spec=grid_spec,
                  out_shape=tuple(_sds((4, r, cols), wire_dtype) for _ in range(narr)),
                  compiler_params=_params(("arbitrary", "arbitrary")))(my_core, *views, *views)


def _adam_math(g, w, m, v):
    m_new = ADAM_B1 * m + (1.0 - ADAM_B1) * g
    v_new = ADAM_B2 * v + (1.0 - ADAM_B2) * (g * g)
    m_hat = m_new / (1.0 - ADAM_B1 ** ADAM_STEP)
    v_hat = v_new / (1.0 - ADAM_B2 ** ADAM_STEP)
    return -ADAM_LR * (m_hat / (jnp.sqrt(v_hat) + ADAM_EPS) + ADAM_WD * w), m_new, v_new


def _adamw(first, parts, w, m, v, name):
    n, rows, cols = parts.shape
    tr = _row_tile(rows, parts.dtype)

    def body(f_ref, p_ref, w_ref, m_ref, v_ref, g_out, d_out, m_out, v_out):
        g = f_ref[...].astype(f32)
        for s in range(n):
            g = g + p_ref[s].astype(f32)
        g_out[...] = g
        d_out[...], m_out[...], v_out[...] = _adam_math(g, w_ref[...], m_ref[...], v_ref[...])

    blk = pl.BlockSpec((tr, cols), lambda i: (i, 0))
    return _pcall(
        body, name=name, grid=(rows // tr,),
        in_specs=[blk, pl.BlockSpec((n, tr, cols), lambda i: (0, i, 0)), blk, blk, blk],
        out_specs=(blk, blk, blk, blk), out_shape=tuple(_sds((rows, cols), f32) for _ in range(4)),
        compiler_params=_params(("arbitrary",)),
    )(first, parts, w, m, v)


def _adamw_group(parts, ws, ms, vs, name):
    nw = len(ws)

    def body(*refs):
        p_refs, w_refs, m_refs, v_refs = (refs[k * nw:(k + 1) * nw] for k in range(4))
        outs = refs[4 * nw:]
        for k in range(nw):
            g = p_refs[k][0].astype(f32)
            for s in range(1, p_refs[k].shape[0]):
                g = g + p_refs[k][s].astype(f32)
            g_out, d_out, m_out, v_out = outs[4 * k:4 * k + 4]
            g_out[...] = g
            d_out[...], m_out[...], v_out[...] = _adam_math(g, w_refs[k][...], m_refs[k][...], v_refs[k][...])

    res = _pcall(
        body, name=name, out_shape=tuple(_sds(w.shape, f32) for w in ws for _ in range(4)),
        in_specs=[VMEM_SPEC] * (4 * nw), out_specs=tuple([VMEM_SPEC] * (4 * nw)), compiler_params=_params(),
    )(*parts, *ws, *ms, *vs)
    return [res[4 * k:4 * k + 4] for k in range(nw)]


def _rope(t, c, s1, s2):
    w = t.shape[1]
    return t * c + pltpu.roll(t, w - 8, 1) * s1 + pltpu.roll(t, 8, 1) * s2


def _rope_transposed(dt, c, s1, s2):
    w = dt.shape[1]
    return dt * c + pltpu.roll(dt * s1, 8, 1) + pltpu.roll(dt * s2, w - 8, 1)


PAIR_ROWS = D_IN // 4
SUB_COLS = ((0, 512), (512, 512), (1024, 512), (1536, 128))
Q_SLABS = range(3, 11)
K_SLABS = range(11, 13)


def _in_proj_gather(x2d, norm_g, wt_shard, cw_shard, tabs, S, out_shards, chip_order):
    T = x2d.shape[0]
    tb = min(S, 1024)
    ntok = T // tb
    nsb = S // tb
    q_scale = 1.0 / math.sqrt(HEAD)
    shard_rows = wt_shard.shape[0]
    small = (cw_shard,) + tuple(out_shards)
    nsm = len(small)

    def body(order_ref, x_ref, g_ref, c_ref, s1_ref, s2_ref, wt_hbm, *rest):
        small_in = rest[:nsm]
        h_ref, proj_ref, wt_out = rest[nsm:nsm + 3]
        small_out = rest[nsm + 3:2 * nsm + 3]
        wt_vm, h_vm = rest[2 * nsm + 3:2 * nsm + 5]
        stage = rest[2 * nsm + 5:3 * nsm + 4]
        wsend, wrecv, wlocal = rest[3 * nsm + 4:3 * nsm + 7]
        dsems = rest[3 * nsm + 7:]
        jj, i = pl.program_id(0), pl.program_id(1)
        x, y, c = _my_place()
        me, sibling = (x, y, c), (x, y, 1 - c)
        chips = [(1 - x, y), (x, 1 - y), (1 - x, 1 - y)]

        def rows(place):
            px, py, pc = place
            return wt_vm.at[pl.ds(pl.multiple_of((4 * px + 2 * py + pc) * shard_rows, 16), shard_rows), :]

        def copy(k, block, to, src=None):
            return pltpu.make_async_remote_copy(
                src_ref=rows(block) if src is None else src, dst_ref=rows(block),
                send_sem=wsend.at[k], recv_sem=wrecv.at[k], device_id=to, device_id_type=MESH)

        def small_copies():
            srcs = (small_in[0],) + tuple(stage)
            return _direct_gather_copies(srcs, small_out, *dsems)

        own = pltpu.make_async_copy(wt_hbm, rows(me), wlocal.at[0])
        keep = pltpu.make_async_copy(wt_vm, wt_out, wlocal.at[1])

        @pl.when((jj == 0) & (i == 0))
        def _():
            own.start()
            copy(0, me, sibling, src=wt_hbm).start()
            for j, chip in enumerate(chips):
                copy(1 + j, me, (*chip, c), src=wt_hbm).start()
            for a in range(nsm - 1):
                stage[a][...] = small_in[1 + a][...].astype(bf16)
            _start_all(small_copies())
            own.wait()
            copy(0, sibling, me).wait_recv()

        for j, chip in enumerate(chips):
            @pl.when((jj == 1 + j) & (i == 0))
            def _(j=j, chip=chip):
                copy(1 + j, (*chip, c), me).wait_recv()
                copy(4 + j, (*chip, c), sibling).start()
                copy(4 + j, (*chip, 1 - c), me).wait_recv()

        @pl.when((jj == 3) & (i == 0))
        def _():
            keep.start()

        @pl.when((jj == 3) & (i == ntok - 1))
        def _():
            copy(0, me, sibling, src=wt_hbm).wait_send()
            for j, chip in enumerate(chips):
                copy(1 + j, me, (*chip, c), src=wt_hbm).wait_send()
                copy(4 + j, (*chip, c), sibling).wait_send()
            _wait_all(small_copies())
            keep.wait()

        tok = pl.ds(pl.multiple_of(i * tb, tb), tb)

        @pl.when(jj == 0)
        def _():
            xv = x_ref[...]
            ms = jnp.mean(xv * xv, axis=-1, keepdims=True)
            hb = (xv * lax.rsqrt(ms + EPS) * g_ref[...]).astype(bf16)
            h_ref[...] = hb
            h_vm[tok, :] = hb

        block = order_ref[jj]
        hb = h_vm[tok, :]

        def piece(c0, w):
            w_rows = wt_vm[pl.ds(pl.multiple_of(block * PAIR_ROWS + c0, 128), w), :]
            return _dot(hb, w_rows, NT)

        @pl.when(block != 1)
        def _():
            for c0, w in SUB_COLS:
                proj_ref[:, c0:c0 + w] = piece(c0, w).astype(bf16)

        @pl.when(block == 1)
        def _():
            tab = (c_ref[...], s1_ref[...], s2_ref[...])
            for c0, w in SUB_COLS:
                acc = piece(c0, w)
                for l in range(w // 128):
                    slab = (c0 + 128 * l) // 128
                    part = acc[:, 128 * l:128 * (l + 1)]
                    if slab in Q_SLABS:
                        part = _rope(part, *tab) * q_scale
                    elif slab in K_SLABS:
                        part = _rope(part, *tab)
                    proj_ref[:, 128 * slab:128 * (slab + 1)] = part.astype(bf16)

    first_pass = lambda jj, i, order: (jnp.where(jj == 0, i, ntok - 1), 0)
    const = lambda jj, i, order: (0, 0)
    tab = pl.BlockSpec((tb, 128), lambda jj, i, order: (jnp.where(order[jj] == 1, i % nsb, 0), 0))
    grid_spec = pltpu.PrefetchScalarGridSpec(
        num_scalar_prefetch=1, grid=(4, ntok),
        in_specs=[pl.BlockSpec((tb, D), first_pass), pl.BlockSpec((1, D), const), tab, tab, tab, ANY]
        + [pl.BlockSpec(w.shape, const) for w in small],
        out_specs=(pl.BlockSpec((tb, D), first_pass),
                   pl.BlockSpec((tb, PAIR_ROWS), lambda jj, i, order: (i, order[jj])), ANY) + tuple([ANY] * nsm),
        scratch_shapes=[pltpu.VMEM((D_IN, D), bf16), pltpu.VMEM((T, D), bf16)]
        + [pltpu.VMEM(w.shape, bf16) for w in out_shards]
        + [pltpu.SemaphoreType.DMA((7,)), pltpu.SemaphoreType.DMA((7,)), pltpu.SemaphoreType.DMA((2,))]
        + _exchange_scratch(nsm, 7))
    res = _pcall(
        body, name="in_proj", grid_spec=grid_spec,
        out_shape=(_sds((T, D), bf16), _sds((T, D_IN), bf16), _sds((D_IN, D), bf16),
                   _sds((NDEV * cw_shard.shape[0], cw_shard.shape[1]), f32))
        + tuple(_sds((NDEV * w.shape[0], w.shape[1]), bf16) for w in out_shards),
        compiler_params=_params(("arbitrary", "arbitrary")),
    )(chip_order, x2d, norm_g, *tabs, wt_shard, *small)
    return res[0], res[1], res[2], res[3], res[4:]


def _rows_iota(shape):
    return lax.broadcasted_iota(jnp.int32, shape, 0)


def _shift_down(v, k):
    return jnp.where(_rows_iota(v.shape) >= k, pltpu.roll(v, k, 0), 0.0)


def _shift_up(v, k):
    n = v.shape[0]
    return jnp.where(_rows_iota(v.shape) < n - k, pltpu.roll(v, n - k, 0), 0.0)


def _linear_scan(a, b, a_s, b_s, edge_s, out_ref, reverse):
    n = a.shape[0]
    ng = n // 8
    a3, b3 = a.reshape(ng, 8, RB), b.reshape(ng, 8, RB)
    rid = lax.broadcasted_iota(jnp.int32, a3.shape, 1)
    for s in (1, 2, 4):
        keep, shift = (rid < 8 - s, 8 - s) if reverse else (rid >= s, s)
        b3 = jnp.where(keep, a3 * pltpu.roll(b3, shift, 1) + b3, b3)
        a3 = jnp.where(keep, a3 * pltpu.roll(a3, shift, 1), a3)
    a_s[...] = a3.reshape(n, RB)
    b_s[...] = b3.reshape(n, RB)
    edge = 0 if reverse else 7
    ea, eb = a_s[pl.ds(edge, ng, stride=8), :], b_s[pl.ds(edge, ng, stride=8), :]
    r = _rows_iota(ea.shape)
    s = 1
    while s < ng:
        keep, shift = (r < ng - s, ng - s) if reverse else (r >= s, s)
        eb = jnp.where(keep, ea * pltpu.roll(eb, shift, 0) + eb, eb)
        if 2 * s < ng:
            ea = jnp.where(keep, ea * pltpu.roll(ea, shift, 0), ea)
        s *= 2
    edge_s[...] = _shift_up(eb, 1) if reverse else _shift_down(eb, 1)

    def eight_groups(i, carry):
        for k in range(8):
            j = i * 8 + k
            rows = pl.ds(pl.multiple_of(j * 8, 8), 8)
            out_ref[rows, :] = b_s[rows, :] + a_s[rows, :] * edge_s[pl.ds(j, 1), :]
        return carry

    lax.fori_loop(0, ng // 8, eight_groups, 0)


def _neg_expm1(v):
    series = -v * (1.0 + v * (0.5 + v * (1.0 / 6.0)))
    return jnp.where(v > -0.015625, series, 1.0 - jnp.exp(v))


def _softplus_neg(lam):
    return jnp.maximum(-lam, 0.0) + jnp.log(1.0 + jnp.exp(-jnp.abs(lam)))


def _lru_gates(x0, cw, cb, wa, ba, wx, bx, lam):
    taps = [_shift_down(x0, 3 - k) for k in range(3)] + [x0]
    u = cb + cw[3:4, :] * x0
    for k in range(3):
        u = u + cw[k:k + 1, :] * taps[k]
    ub = u.astype(bf16)
    r = _sigmoid_positive(_dot(ub, wa.astype(bf16), NN) + ba)
    i = _sigmoid(_dot(ub, wx.astype(bf16), NN) + bx)
    sp = _softplus_neg(lam)
    log_a = (-LRU_C) * r * sp
    a = jnp.exp(log_a)
    w = _neg_expm1(2.0 * log_a)
    inv_mult = lax.rsqrt(w)
    return u, ub, r, i, sp, a, w * inv_mult, inv_mult, taps


def _lru_specs(S, nb):
    col = lambda off: pl.BlockSpec((S, RB), lambda n, b, off=off: (b, off + n))
    vec = pl.BlockSpec((1, RB), lambda n, b: (0, n))
    wblk = pl.BlockSpec((None, RB, RB), lambda n, b: (n, 0, 0))
    cwblk = pl.BlockSpec((8, RB), lambda n, b: (n, 0))
    return col, vec, wblk, cwblk


def _lru_forward(proj, cw_full, conv_b, w_a, b_a, w_x, b_x, lam, S):
    T = proj.shape[0]
    nb = T // S
    col, vec, wblk, cwblk = _lru_specs(S, nb)

    def body(x0_ref, g_ref, cw_ref, cb_ref, wa_ref, ba_ref, wx_ref, bx_ref, lam_ref, y_ref, h_ref, a_s, b_s, edge_s):
        x0 = x0_ref[...].astype(f32)
        u, ub, r, i, sp, a, mult, _, _ = _lru_gates(x0, cw_ref[...], cb_ref[...], wa_ref[...], ba_ref[...],
                                                    wx_ref[...], bx_ref[...], lam_ref[...])
        _linear_scan(a, mult * (i * u), a_s, b_s, edge_s, h_ref, reverse=False)
        g = g_ref[...].astype(f32)
        y_ref[...] = (h_ref[...] * (g * _sigmoid(g))).astype(bf16)

    out = pl.BlockSpec((S, RB), lambda n, b: (b, n))
    return _pcall(
        body, name="lru_forward", grid=(RNN_BLOCKS, nb),
        in_specs=[col(0), col(8), cwblk, vec, wblk, vec, wblk, vec, vec],
        out_specs=(out, out), out_shape=(_sds((T, D), bf16), _sds((T, D), f32)),
        scratch_shapes=[pltpu.VMEM((S, RB), f32), pltpu.VMEM((S, RB), f32), pltpu.VMEM((S // 8, RB), f32)],
        compiler_params=_params(("arbitrary", "arbitrary")),
    )(proj, proj, cw_full, conv_b, w_a, b_a, w_x, b_x, lam)


def _rope_tables(S):
    pos = jnp.arange(S, dtype=f32)
    inv_freq = ROPE_THETA ** (-jnp.arange(0, ROPE_DIM, 2, dtype=f32) / ROPE_DIM)
    ang = pos[:, None] * inv_freq[None, :]
    cos, sin = jnp.cos(ang), jnp.sin(ang)
    lane = jnp.arange(128) % HEAD
    cosl, sinl = cos[:, lane % 8], sin[:, lane % 8]
    c = jnp.where(lane[None, :] < ROPE_DIM, cosl, 1.0)
    s1 = jnp.where(lane[None, :] < 8, -sinl, 0.0)
    s2 = jnp.where((lane[None, :] >= 8) & (lane[None, :] < ROPE_DIM), sinl, 0.0)
    return c.astype(f32), s1.astype(f32), s2.astype(f32)


def _heads_to_rows(t):
    return jnp.concatenate([t[:, HEAD * h:HEAD * (h + 1)] for h in range(GROUP)], axis=0)


def _rows_to_heads(t):
    return jnp.concatenate([t[QB * h:QB * (h + 1), :] for h in range(GROUP)], axis=1)


def _window_bias(first_block):
    shape = (GROUP * QB, 2 * QB)
    qi = _rows_iota(shape) % QB
    cj = lax.broadcasted_iota(jnp.int32, shape, 1)
    valid = (cj > qi) & (cj <= qi + QB) & ((cj >= QB) | jnp.logical_not(first_block))
    return jnp.where(valid, 0.0, -jnp.inf)


def _attn_probs(q_rows, k_cat, sink_col, bias):
    s = _dot(q_rows, k_cat, NT) + bias
    m = jnp.maximum(jnp.max(s, axis=1, keepdims=True), sink_col)
    p = jnp.exp(s - m)
    e_sink = jnp.exp(sink_col - m)
    inv = 1.0 / (jnp.sum(p, axis=1, keepdims=True) + e_sink)
    return p * inv, e_sink * inv


def _sink_column(sink_ref, kv):
    rid = _rows_iota((GROUP * QB, 1))
    col = jnp.zeros((GROUP * QB, 1), f32)
    for h in range(GROUP):
        col = jnp.where(rid // QB == h, sink_ref[0, GROUP * kv + h], col)
    return col


def _attn_in_specs(S):
    nq = S // QB
    last = nq - 1
    cur = lambda b, j: b * nq + jnp.minimum(j, last)
    prev = lambda b, j: b * nq + jnp.maximum(jnp.minimum(j, last) - 1, 0)
    specs = [
        pl.BlockSpec((QB, D), lambda b, j: (cur(b, j), 2)),
        pl.BlockSpec((QB, 256), lambda b, j: (cur(b, j), 12)),
        pl.BlockSpec((QB, 256), lambda b, j: (prev(b, j), 12)),
        pl.BlockSpec((QB, 256), lambda b, j: (cur(b, j), 13)),
        pl.BlockSpec((QB, 256), lambda b, j: (prev(b, j), 13)),
        pl.BlockSpec((QB, 512), lambda b, j: (cur(b, j), 7)),
        pl.BlockSpec((QB, 512), lambda b, j: (cur(b, j), 8)),
        SMEM_SPEC,
    ]
    return specs, cur, prev


def _attn_forward(proj, sinks, S, out_shards):
    T = proj.shape[0]
    nb, nq = T // S, S // QB
    specs, cur, _ = _attn_in_specs(S)
    nw = len(out_shards)

    def body(q_ref, kc_ref, kp_ref, vc_ref, vp_ref, gl_ref, gh_ref, sink_ref, *rest):
        shards = rest[:nw]
        y_ref = rest[nw]
        gathered = rest[nw + 1:2 * nw + 1]
        stage = rest[2 * nw + 1:3 * nw + 1]
        sems = rest[3 * nw + 1:]
        b, j = pl.program_id(0), pl.program_id(1)

        @pl.when((b == 0) & (j == 0))
        def _():
            for a in range(nw):
                stage[a][...] = shards[a][...].astype(bf16)
            _start_all(_direct_gather_copies(stage, gathered, *sems))

        @pl.when((b == nb - 1) & (j == nq - 1))
        def _():
            _wait_all(_direct_gather_copies(stage, gathered, *sems))

        bias = _window_bias(j == 0)
        kc, kp, vc, vp = kc_ref[...], kp_ref[...], vc_ref[...], vp_ref[...]
        for kv in range(KV_HEADS):
            lanes = slice(256 * kv, 256 * (kv + 1))
            hl = slice(HEAD * kv, HEAD * (kv + 1))
            q_rows = _heads_to_rows(q_ref[:, lanes])
            k_cat = jnp.concatenate([kp[:, hl], kc[:, hl]], axis=0)
            v_cat = jnp.concatenate([vp[:, hl], vc[:, hl]], axis=0)
            probs, _ = _attn_probs(q_rows, k_cat, _sink_column(sink_ref, kv), bias)
            o = _rows_to_heads(_dot(probs.astype(bf16), v_cat, NN))
            g_src = gl_ref if kv < 2 else gh_ref
            g = g_src[:, 256 * (kv % 2):256 * (kv % 2 + 1)].astype(f32)
            y_ref[:, lanes] = (o * (g * _sigmoid(g))).astype(bf16)

    args = [proj] * 7 + [sinks] + list(out_shards)
    res = _pcall(
        body, name="attn_forward", grid=(nb, nq),
        in_specs=specs + [pl.BlockSpec(w.shape, lambda b, j: (0, 0)) for w in out_shards],
        out_specs=(pl.BlockSpec((QB, D), lambda b, j: (cur(b, j), 0)),) + tuple([ANY] * nw),
        out_shape=(_sds((T, D), bf16),) + tuple(_sds((NDEV * w.shape[0], w.shape[1]), bf16) for w in out_shards),
        scratch_shapes=[pltpu.VMEM(w.shape, bf16) for w in out_shards] + _exchange_scratch(nw, 7),
        compiler_params=_params(("arbitrary", "arbitrary")),
    )(*args)
    return res[0], res[1:]


def _merge_and_head(x2d, tgt, proj, y_rnn, y_attn, w_r, w_a, w_o, gfin):
    T = x2d.shape[0]
    tb = min(T, 512)
    nsteps = T // tb

    def body(x_ref, t_ref, mr0, mr1, ma0, ma1, yr_ref, ya_ref, wr_ref, wa_ref, wo_ref, gf_ref,
             dx2_ref, dyr_ref, dya_ref, dmr_ref, dma_ref, loss_ref, gfin_ref, gwr_out, gwa_out, gwo_out,
             gwr_acc, gwa_acc, gwo_acc, out_sems):
        step = pl.program_id(0)

        @pl.when(step == 0)
        def _():
            loss_ref[...] = jnp.zeros_like(loss_ref)
            gfin_ref[...] = jnp.zeros_like(gfin_ref)
            gwr_acc[...] = jnp.zeros_like(gwr_acc)
            gwa_acc[...] = jnp.zeros_like(gwa_acc)
            gwo_acc[...] = jnp.zeros_like(gwo_acc)

        sr = _sigmoid(jnp.concatenate([mr0[...], mr1[...]], axis=1).astype(f32))
        sa = _sigmoid(jnp.concatenate([ma0[...], ma1[...]], axis=1).astype(f32))
        p_r = _dot(yr_ref[...], wr_ref[...], NN)
        p_a = _dot(ya_ref[...], wa_ref[...], NN)
        merged = (sr * p_r + sa * p_a).astype(bf16)
        x2 = x_ref[...] + _dot(merged, wo_ref[...], NN)
        rstd = lax.rsqrt(jnp.mean(x2 * x2, axis=-1, keepdims=True) + EPS)
        xh = x2 * rstd
        gf = gf_ref[...]
        err = xh * gf - t_ref[...]
        loss_ref[...] += jnp.sum(err * err)
        dy = err * (1.0 / D)
        gfin_ref[0:1, :] += jnp.sum(dy * xh, axis=0, keepdims=True)
        dxn = dy * gf
        dx2 = rstd * (dxn - xh * jnp.mean(dxn * xh, axis=-1, keepdims=True))
        dx2_ref[...] = dx2
        dx2b = dx2.astype(bf16)
        dmerged = _dot(dx2b, wo_ref[...], NT)
        dmr_ref[...] = (dmerged * p_r * (sr * (1.0 - sr))).astype(bf16)
        dma_ref[...] = (dmerged * p_a * (sa * (1.0 - sa))).astype(bf16)
        dpr = (dmerged * sr).astype(bf16)
        dpa = (dmerged * sa).astype(bf16)
        dyr_ref[...] = _dot(dpr, wr_ref[...], NT).astype(bf16)
        dya_ref[...] = _dot(dpa, wa_ref[...], NT).astype(bf16)
        gwr_acc[...] += _dot(yr_ref[...], dpr, TN)
        gwa_acc[...] += _dot(ya_ref[...], dpa, TN)
        gwo_acc[...] += _dot(merged, dx2b, TN)

        @pl.when(step == nsteps - 1)
        def _():
            copies = [pltpu.make_async_copy(src, dst, out_sems.at[k]) for k, (src, dst) in enumerate(
                ((gwr_acc, gwr_out), (gwa_acc, gwa_out), (gwo_acc, gwo_out)))]
            for cp in copies:
                cp.start()
            for cp in copies:
                cp.wait()

    tok = pl.BlockSpec((tb, D), lambda i: (i, 0))
    half = lambda c: pl.BlockSpec((tb, CH), lambda i, c=c: (i, c))
    wfull = pl.BlockSpec((D, D), lambda i: (0, 0), pipeline_mode=pl.Buffered(1))
    acc = pl.BlockSpec((8, D), lambda i: (0, 0))
    return _pcall(
        body, name="merge_and_head", grid=(nsteps,),
        in_specs=[tok, tok, half(9), half(10), half(11), half(12), tok, tok, wfull, wfull, wfull,
                  pl.BlockSpec((1, D), lambda i: (0, 0))],
        out_specs=(tok, tok, tok, tok, tok, acc, acc, ANY, ANY, ANY),
        out_shape=(_sds((T, D), f32), _sds((T, D), bf16), _sds((T, D), bf16), _sds((T, D), bf16),
                   _sds((T, D), bf16), _sds((8, D), f32), _sds((8, D), f32),
                   _sds((D, D), f32), _sds((D, D), f32), _sds((D, D), f32)),
        scratch_shapes=[pltpu.VMEM((D, D), f32)] * 3 + [pltpu.SemaphoreType.DMA((3,))],
        compiler_params=_params(("arbitrary",)),
    )(x2d, tgt, proj, proj, proj, proj, y_rnn, y_attn, w_r, w_a, w_o, gfin)


def _attn_backward(proj, dy_attn, tabs, sinks, S, chip_sums):
    T = proj.shape[0]
    nb, nq = T // S, S // QB
    nex = len(chip_sums)
    specs, cur, prev = _attn_in_specs(S)
    last = nq - 1
    tab_cur = pl.BlockSpec((QB, 128), lambda b, j: (jnp.minimum(j, last), 0))
    tab_prev = pl.BlockSpec((QB, 128), lambda b, j: (jnp.maximum(jnp.minimum(j, last) - 1, 0), 0))
    specs = specs + [pl.BlockSpec((QB, D), lambda b, j: (cur(b, j), 0))] + [tab_cur] * 3 + [tab_prev] * 3
    q_scale = 1.0 / math.sqrt(HEAD)

    def rope_back(dt, tab):
        return jnp.concatenate([_rope_transposed(dt[:, 128 * l:128 * (l + 1)], *tab) for l in range(2)], axis=1)

    def body(q_ref, kc_ref, kp_ref, vc_ref, vp_ref, gl_ref, gh_ref, sink_ref, dy_ref, cc, s1c, s2c, cp, s1p, s2p,
             *rest):
        ex_src = rest[:nex]
        dq_ref, dkv_ref, dg_ref, dsink_ref = rest[nex:nex + 4]
        ex_dst = rest[nex + 4:2 * nex + 4]
        carry_k, carry_v = rest[2 * nex + 4:2 * nex + 6]
        sems = rest[2 * nex + 6:]
        b, j = pl.program_id(0), pl.program_id(1)

        @pl.when((b == 0) & (j == 0))
        def _():
            dsink_ref[...] = jnp.zeros_like(dsink_ref)
            _start_all(_chip_exchange_copies(ex_src, ex_dst, *sems))

        @pl.when((b == nb - 1) & (j == nq))
        def _():
            _wait_all(_chip_exchange_copies(ex_src, ex_dst, *sems))

        @pl.when(j == 0)
        def _():
            carry_k[...] = jnp.zeros_like(carry_k)
            carry_v[...] = jnp.zeros_like(carry_v)

        @pl.when(j < nq)
        def _():
            bias = _window_bias(j == 0)
            tc = (cc[...], s1c[...], s2c[...])
            tp = (cp[...], s1p[...], s2p[...])
            kc, kp, vc, vp = kc_ref[...], kp_ref[...], vc_ref[...], vp_ref[...]
            dk_prev, dk_cur, dv_prev, dv_cur = [], [], [], []
            dsink_acc = jnp.zeros((8, 128), f32)
            r8 = lax.broadcasted_iota(jnp.int32, (8, 128), 0)
            l8 = lax.broadcasted_iota(jnp.int32, (8, 128), 1)
            for kv in range(KV_HEADS):
                lanes = slice(256 * kv, 256 * (kv + 1))
                hl = slice(HEAD * kv, HEAD * (kv + 1))
                q_rows = _heads_to_rows(q_ref[:, lanes])
                k_cat = jnp.concatenate([kp[:, hl], kc[:, hl]], axis=0)
                v_cat = jnp.concatenate([vp[:, hl], vc[:, hl]], axis=0)
                probs, p_sink = _attn_probs(q_rows, k_cat, _sink_column(sink_ref, kv), bias)
                pb = probs.astype(bf16)
                g_src = gl_ref if kv < 2 else gh_ref
                g = g_src[:, 256 * (kv % 2):256 * (kv % 2 + 1)].astype(f32)
                sg = _sigmoid(g)
                dy = dy_ref[:, lanes].astype(f32)
                do_rows = _heads_to_rows(dy * (g * sg)).astype(bf16)
                dp = _dot(do_rows, v_cat, NT)
                rowdot = jnp.sum(probs * dp, axis=1, keepdims=True)
                ds = (probs * (dp - rowdot)).astype(bf16)
                sink_rows = -(p_sink * rowdot)
                for h in range(GROUP):
                    val = jnp.sum(sink_rows[QB * h:QB * (h + 1), :])
                    dsink_acc = dsink_acc + jnp.where((r8 == 0) & (l8 == GROUP * kv + h), val, 0.0)
                dq = _rows_to_heads(_dot(ds, k_cat, NN)) * q_scale
                dq_ref[:, lanes] = rope_back(dq, tc).astype(bf16)
                dk = _dot(ds, q_rows, TN)
                o = _rows_to_heads(_dot(pb, v_cat, NN))
                dg_ref[:, lanes] = (dy * o * (sg * (1.0 + g * (1.0 - sg)))).astype(bf16)
                dv = _dot(pb, do_rows, TN)
                dk_prev.append(dk[:QB, :])
                dk_cur.append(dk[QB:, :])
                dv_prev.append(dv[:QB, :])
                dv_cur.append(dv[QB:, :])
            dsink_ref[...] += dsink_acc
            dkp = rope_back(jnp.concatenate(dk_prev, axis=1), tp)
            dkc = rope_back(jnp.concatenate(dk_cur, axis=1), tc)
            dkv_ref[:, 0:256] = (carry_k[...] + dkp).astype(bf16)
            dkv_ref[:, 256:512] = (carry_v[...] + jnp.concatenate(dv_prev, axis=1)).astype(bf16)
            carry_k[...] = dkc
            carry_v[...] = jnp.concatenate(dv_cur, axis=1)

        @pl.when(j == nq)
        def _():
            dkv_ref[:, 0:256] = carry_k[...].astype(bf16)
            dkv_ref[:, 256:512] = carry_v[...].astype(bf16)

    lag = lambda b, j: (b * nq + jnp.maximum(j - 1, 0), 0)
    args = [proj] * 7 + [sinks, dy_attn] + list(tabs) + list(tabs) + list(chip_sums)
    res = _pcall(
        body, name="attn_backward", grid=(nb, nq + 1), in_specs=specs + [ANY] * nex,
        out_specs=(pl.BlockSpec((QB, D), lambda b, j: (cur(b, j), 0)), pl.BlockSpec((QB, 512), lag),
                   pl.BlockSpec((QB, D), lambda b, j: (cur(b, j), 0)), pl.BlockSpec((8, 128), lambda b, j: (0, 0)))
        + tuple([ANY] * nex),
        out_shape=(_sds((T, D), bf16), _sds((T, 512), bf16), _sds((T, D), bf16), _sds((8, 128), f32))
        + tuple(_sds(s.shape, s.dtype) for s in chip_sums),
        scratch_shapes=[pltpu.VMEM((QB, 256), f32), pltpu.VMEM((QB, 256), f32)] + _exchange_scratch(nex, 3),
        compiler_params=_params(("arbitrary", "arbitrary")),
    )(*args)
    return res[:4], res[4:]


def _lru_backward(proj, h_all, dy_rnn, cw_full, conv_b, w_a, b_a, w_x, b_x, lam, S):
    T = proj.shape[0]
    nb = T // S
    col, vec, wblk, cwblk = _lru_specs(S, nb)
    tokblk = pl.BlockSpec((S, RB), lambda n, b: (b, n))

    def body(x0_ref, g_ref, h_ref, dy_ref, cw_ref, cb_ref, wa_ref, ba_ref, wx_ref, bx_ref, lam_ref,
             du0_ref, dg_ref, gwa_ref, gwx_ref, vec_ref, gcw_ref, a_s, b_s, dh_s, edge_s):
        @pl.when(pl.program_id(1) == 0)
        def _():
            gwa_ref[...] = jnp.zeros_like(gwa_ref)
            gwx_ref[...] = jnp.zeros_like(gwx_ref)
            vec_ref[...] = jnp.zeros_like(vec_ref)
            gcw_ref[...] = jnp.zeros_like(gcw_ref)

        x0 = x0_ref[...].astype(f32)
        cw = cw_ref[...]
        lam_v = lam_ref[...]
        u, ub, r, i, sp, a, mult, inv_mult, taps = _lru_gates(x0, cw, cb_ref[...], wa_ref[...], ba_ref[...],
                                                              wx_ref[...], bx_ref[...], lam_v)
        h = h_ref[...]
        g = g_ref[...].astype(f32)
        dy = dy_ref[...].astype(f32)
        sg = _sigmoid(g)
        dg_ref[...] = (dy * h * (sg * (1.0 + g * (1.0 - sg)))).astype(bf16)
        _linear_scan(_shift_up(a, 1), dy * (g * sg), a_s, b_s, edge_s, dh_s, reverse=True)
        dh_total = dh_s[...]
        da = dh_total * _shift_down(h, 1)
        dmult = dh_total * (i * u)
        db = dh_total * mult
        di = db * u
        du = db * i
        dlog_a_c = ((-LRU_C) * a) * (da - dmult * (a * inv_mult))
        dr = dlog_a_c * sp
        dsp = jnp.sum(dlog_a_c * r, axis=0, keepdims=True)
        dpre_r = dr * r * (1.0 - r)
        dpre_i = di * i * (1.0 - i)
        dpre_rb = dpre_r.astype(bf16)
        dpre_ib = dpre_i.astype(bf16)
        du = du + _dot(dpre_rb, wa_ref[...].astype(bf16), NT) + _dot(dpre_ib, wx_ref[...].astype(bf16), NT)
        gwa_ref[...] += _dot(ub, dpre_rb, TN)
        gwx_ref[...] += _dot(ub, dpre_ib, TN)
        vec_ref[0:1, :] += jnp.sum(du, axis=0, keepdims=True)
        vec_ref[1:2, :] += jnp.sum(dpre_r, axis=0, keepdims=True)
        vec_ref[2:3, :] += jnp.sum(dpre_i, axis=0, keepdims=True)
        vec_ref[3:4, :] += dsp * (-_sigmoid(-lam_v))
        dx0 = cw[3:4, :] * du
        for k in range(3):
            dx0 = dx0 + cw[k:k + 1, :] * _shift_up(du, 3 - k)
        for k in range(4):
            gcw_ref[k:k + 1, :] += jnp.sum(du * taps[k], axis=0, keepdims=True)
        du0_ref[...] = dx0.astype(bf16)

    wacc = pl.BlockSpec((RB, RB), lambda n, b: (0, n))
    vacc = pl.BlockSpec((8, RB), lambda n, b: (0, n))
    cacc = pl.BlockSpec((8, RB), lambda n, b: (n, 0))
    return _pcall(
        body, name="lru_backward", grid=(RNN_BLOCKS, nb),
        in_specs=[col(0), col(8), tokblk, tokblk, cwblk, vec, wblk, vec, wblk, vec, vec],
        out_specs=(tokblk, tokblk, wacc, wacc, vacc, cacc),
        out_shape=(_sds((T, D), bf16), _sds((T, D), bf16), _sds((RB, D), f32), _sds((RB, D), f32),
                   _sds((8, D), f32), _sds((8 * RNN_BLOCKS, RB), f32)),
        scratch_shapes=[pltpu.VMEM((S, RB), f32)] * 3 + [pltpu.VMEM((S // 8, RB), f32)],
        compiler_params=_params(("arbitrary", "arbitrary")),
    )(proj, proj, h_all, dy_rnn, cw_full, conv_b, w_a, b_a, w_x, b_x, lam)


def _section_of_chunk(s):
    out = []
    for start, n in zip(SEC_START, SEC_CHUNKS):
        inside = (s >= start) & (s < start + n)
        out.append((inside, jnp.clip(s - start, 0, n - 1)))
    return out


EFFECT = pltpu.SideEffectType.DATAFLOW_SIDE_EFFECTING
HBM_SPEC = pl.BlockSpec(memory_space=pltpu.HBM)
SEM_SPEC = pl.BlockSpec(memory_space=pltpu.SEMAPHORE)


def _split_exchange_copies(src_ref, land_ref, send_sems, recv_sems):
    x, y, c = _my_place()
    copies = []
    for k in (3, 1, 2):
        px, py = (x + (k >> 1)) % 2, (y + (k & 1)) % 2
        copies.append(pltpu.make_async_remote_copy(
            src_ref=src_ref.at[2 * px + py], dst_ref=land_ref.at[k - 1], send_sem=send_sems[k - 1],
            recv_sem=recv_sems[k - 1], device_id=(px, py, c), device_id_type=MESH))
    return copies


def _exchange_start(chip_sum):
    _, r, cols = chip_sum.shape

    def body(src_ref, land_ref, s0, s1, s2, r0, r1, r2, src_thru, land_thru, token):
        for cp in _split_exchange_copies(src_ref, land_ref, (s0, s1, s2), (r0, r1, r2)):
            cp.start()
        token[...] = jnp.zeros_like(token)

    land = pltpu.with_memory_space_constraint(lax.empty((3, r, cols), chip_sum.dtype), pltpu.HBM)
    res = _pcall(
        body, name="exchange_start",
        out_shape=tuple([pltpu.SemaphoreType.DMA(())] * 6) + (
            pltpu.HBM(chip_sum.shape, chip_sum.dtype), pltpu.HBM((3, r, cols), chip_sum.dtype), _sds((8, 128), f32)),
        in_specs=(HBM_SPEC, HBM_SPEC), out_specs=tuple([SEM_SPEC] * 6) + (HBM_SPEC, HBM_SPEC, VMEM_SPEC),
        input_output_aliases={0: 6, 1: 7},
        compiler_params=pltpu.CompilerParams(has_side_effects=EFFECT),
    )(pltpu.with_memory_space_constraint(chip_sum, pltpu.HBM), land)
    return res[:6], res[6], res[7], res[8]


def _exchange_wait(sems, src_thru, land_thru, after):
    def body(src_ref, land_ref, s0, s1, s2, r0, r1, r2, after_ref, src_dead, got_ref):
        for cp in _split_exchange_copies(src_ref, land_ref, (s0, s1, s2), (r0, r1, r2)):
            cp.wait_send()
            cp.wait_recv()

    return _pcall(
        body, name="exchange_wait",
        out_shape=(pltpu.HBM(src_thru.shape, src_thru.dtype), pltpu.HBM(land_thru.shape, land_thru.dtype)),
        in_specs=(HBM_SPEC, HBM_SPEC) + tuple([SEM_SPEC] * 6) + (ANY,), out_specs=(HBM_SPEC, HBM_SPEC),
        input_output_aliases={0: 0, 1: 1},
        compiler_params=pltpu.CompilerParams(has_side_effects=EFFECT),
    )(src_thru, land_thru, *sems, after)[1]


def _input_grad(dsecs, wt_full, x2d, dx2, norm_g):
    T = x2d.shape[0]
    tb = min(T, 512)
    nsec = len(dsecs)
    ntok = T // tb

    def body(*refs):
        secs = refs[:nsec]
        wt_ref, x_ref, dx2_ref, g_ref, dx_ref, gnorm_ref = refs[nsec:]
        i = pl.program_id(0)

        @pl.when(i == 0)
        def _():
            gnorm_ref[...] = jnp.zeros_like(gnorm_ref)

        dh = None
        for a, (start, n) in enumerate(zip(SEC_START, SEC_CHUNKS)):
            part = _dot(secs[a][...], wt_ref[CH * start:CH * (start + n), :], NN)
            dh = part if dh is None else dh + part
        xv = x_ref[...]
        rstd = lax.rsqrt(jnp.mean(xv * xv, axis=-1, keepdims=True) + EPS)
        xh = xv * rstd
        gnorm_ref[0:1, :] += jnp.sum(dh * xh, axis=0, keepdims=True)
        dxn = dh * g_ref[...]
        dx_ref[...] = dx2_ref[...] + rstd * (dxn - xh * jnp.mean(dxn * xh, axis=-1, keepdims=True))

    tok = pl.BlockSpec((tb, D), lambda i: (i, 0))
    return _pcall(
        body, name="input_grad", grid=(ntok,),
        in_specs=[pl.BlockSpec((tb, sec.shape[1]), lambda i: (i, 0)) for sec in dsecs]
        + [pl.BlockSpec((D_IN, D), lambda i: (0, 0), pipeline_mode=pl.Buffered(1)), tok, tok,
           pl.BlockSpec((1, D), lambda i: (0, 0))],
        out_specs=(tok, pl.BlockSpec((8, D), lambda i: (0, 0))),
        out_shape=(_sds((T, D), f32), _sds((8, D), f32)),
        compiler_params=_params(("arbitrary",)),
    )(*dsecs, wt_full, x2d, dx2, norm_g)


def _w_in_grad(dsecs, h_bf):
    T = h_bf.shape[0]
    tk = min(T, 2048)
    nchunks = D_IN // CH
    nsec = len(dsecs)
    nt = T // tk

    def body(*refs):
        secs = refs[:nsec]
        h_ref, out_ref, acc = refs[nsec:]
        s, t = pl.program_id(0), pl.program_id(1)

        @pl.when(t == 0)
        def _():
            acc[...] = jnp.zeros_like(acc)

        h_rows = h_ref[pl.ds(pl.multiple_of(t * tk, tk), tk), :]
        for a, (start, n) in enumerate(zip(SEC_START, SEC_CHUNKS)):
            @pl.when((s >= start) & (s < start + n))
            def _(a=a):
                acc[...] += _dot(secs[a][...], h_rows, TN)

        @pl.when(t == nt - 1)
        def _():
            out_ref[...] = acc[...].astype(bf16)

    def sec_spec(a):
        def index(s, t, a=a):
            inside, local = _section_of_chunk(s)[a]
            return (jnp.where(inside, t, 0), local)
        return pl.BlockSpec((tk, CH), index)

    return _pcall(
        body, name="w_in_grad", grid=(nchunks, T // tk),
        in_specs=[sec_spec(a) for a in range(nsec)]
        + [pl.BlockSpec((T, D), lambda s, t: (0, 0), pipeline_mode=pl.Buffered(1))],
        out_specs=pl.BlockSpec((CH, D), lambda s, t: (s, 0)), out_shape=_sds((D_IN, D), bf16),
        scratch_shapes=[pltpu.VMEM((CH, D), f32)],
        compiler_params=_params(("arbitrary", "arbitrary")),
    )(*dsecs, h_bf)


SMALL_NAMES = ("lru_w_a", "lru_w_x", "conv_b", "lru_b_a", "lru_b_x", "lru_lambda", "norm_g", "final_norm_g",
               "attn_sinks", "conv_w")
MISC_ROW = {"conv_b": 0, "lru_b_a": 1, "lru_b_x": 2, "lru_lambda": 3, "norm_g": 8, "final_norm_g": 16,
            "attn_sinks": 24, "loss": 32}


def _small_step(gwa, gwx, gvec, gnorm_blk, gfin_blk, dsink_blk, loss_blk, gcw, params):
    srcs_rows = (RB // NDEV, RB // NDEV, 8, 8)
    flat = [t for n in SMALL_NAMES for t in params[n]]
    nout = 4 * len(SMALL_NAMES) + 1

    ra_, rx_, rm_, rc_ = srcs_rows
    rh, rf = ra_ + rx_, rm_ + rc_

    def reduce_body(gwa_ref, gwx_ref, gvec_ref, gnorm_ref, gfin_ref, dsink_ref, loss_ref, gcw_ref,
                    all_a, all_x, all_m, conv_out,
                    misc, out_h, out_f, in_h, in_f, mine_h, mine_f, every_h, every_f, sa, ra, sb, rb):
        x, y, c = _my_place()
        me = 4 * x + 2 * y + c

        misc[...] = jnp.zeros_like(misc)
        misc[0:8, :] = gvec_ref[...]
        misc[8:16, :] = gnorm_ref[...]
        misc[16:24, :] = gfin_ref[...]
        misc[24:32, 0:128] = dsink_ref[...]
        misc[32:40, :] = loss_ref[...]

        out_f[...] = jnp.zeros_like(out_f)
        for d in range(NDEV):
            out_h[d, 0:ra_, :] = gwa_ref[ra_ * d:ra_ * (d + 1), :].astype(bf16)
            out_h[d, ra_:rh, :] = gwx_ref[rx_ * d:rx_ * (d + 1), :].astype(bf16)
            out_f[d, 0:rm_, :] = misc[rm_ * d:rm_ * (d + 1), :]
            out_f[d, rm_:rf, 0:RB] = gcw_ref[rc_ * d:rc_ * (d + 1), :]

        def both(k, src_h, dst_h, src_f, dst_f, send, recv, peer):
            return [pltpu.make_async_remote_copy(src_ref=s_, dst_ref=d_, send_sem=send.at[2 * (k - 1) + t],
                                                 recv_sem=recv.at[2 * (k - 1) + t], device_id=peer,
                                                 device_id_type=MESH)
                    for t, (s_, d_) in enumerate(((src_h, dst_h), (src_f, dst_f)))]

        scatter = []
        for k in range(1, NDEV):
            px, py, pc = _peer(k)
            dev = 4 * px + 2 * py + pc
            scatter += both(k, out_h.at[dev], in_h.at[k - 1], out_f.at[dev], in_f.at[k - 1], sa, ra, (px, py, pc))
        for cp in scatter:
            cp.start()
        for cp in scatter:
            cp.wait()

        total_h = out_h[me].astype(f32)
        total_f = out_f[me]
        for k in range(NDEV - 1):
            total_h = total_h + in_h[k].astype(f32)
            total_f = total_f + in_f[k]
        conv_out[...] = total_f[rm_:rf, 0:RB]
        mine_h[...] = total_h.astype(bf16)
        mine_f[...] = total_f[0:rm_, :]
        every_h[me] = total_h.astype(bf16)
        every_f[me] = total_f[0:rm_, :]
        gather = []
        for k in range(1, NDEV):
            gather += both(k, mine_h, every_h.at[me], mine_f, every_f.at[me], sb, rb, _peer(k))
        for cp in gather:
            cp.start()
        for cp in gather:
            cp.wait()
        for d in range(NDEV):
            all_a[ra_ * d:ra_ * (d + 1), :] = every_h[d, 0:ra_, :].astype(f32)
            all_x[rx_ * d:rx_ * (d + 1), :] = every_h[d, ra_:rh, :].astype(f32)
            all_m[rm_ * d:rm_ * (d + 1), :] = every_f[d]

    def adam_body(*refs):
        all_a, all_x, all_m, conv_ref = refs[:4]
        prm = {n: refs[4 + 3 * k:7 + 3 * k] for k, n in enumerate(SMALL_NAMES)}
        nin = 4 + len(flat)
        outs = {n: refs[nin + 4 * k:nin + 4 * k + 4] for k, n in enumerate(SMALL_NAMES)}
        loss_out = refs[nin + nout - 1]
        g_conv = conv_ref[0:4, :]

        def update(name, g, pick=lambda r: r[...]):
            w_ref, m_ref, v_ref = prm[name]
            delta, m_new, v_new = _adam_math(g, pick(w_ref), pick(m_ref), pick(v_ref))
            return g, delta, m_new, v_new

        for n in range(RNN_BLOCKS):
            lanes = slice(RB * n, RB * (n + 1))
            for name, full in (("lru_w_a", all_a), ("lru_w_x", all_x)):
                for out, val in zip(outs[name], update(name, full[:, lanes], pick=lambda r, n=n: r[n])):
                    out[n] = val
        for name in ("conv_b", "lru_b_a", "lru_b_x", "lru_lambda", "norm_g", "final_norm_g"):
            row = MISC_ROW[name]
            for out, val in zip(outs[name], update(name, all_m[row:row + 1, :])):
                out[...] = val
        row = MISC_ROW["attn_sinks"]
        for out, val in zip(outs["attn_sinks"], update("attn_sinks", all_m[row:row + 1, 0:16])):
            out[...] = val
        for out, val in zip(outs["conv_w"], update("conv_w", g_conv)):
            out[...] = val
        row = MISC_ROW["loss"]
        loss_out[...] = all_m[row:row + 8, 0:128] * (0.5 / D)

    scratch = [pltpu.VMEM((64, D), f32), pltpu.VMEM((NDEV, rh, D), bf16), pltpu.VMEM((NDEV, rf, D), f32),
               pltpu.VMEM((NDEV - 1, rh, D), bf16), pltpu.VMEM((NDEV - 1, rf, D), f32),
               pltpu.VMEM((rh, D), bf16), pltpu.VMEM((rm_, D), f32),
               pltpu.VMEM((NDEV, rh, D), bf16), pltpu.VMEM((NDEV, rm_, D), f32)
               ] + [pltpu.SemaphoreType.DMA((2 * (NDEV - 1),))] * 4
    sums = _pcall(
        reduce_body, name="small_reduce",
        out_shape=(_sds((RB, D), f32), _sds((RB, D), f32), _sds((64, D), f32), _sds((8, RB), f32)),
        in_specs=[VMEM_SPEC] * 8, out_specs=tuple([VMEM_SPEC] * 4),
        scratch_shapes=scratch, compiler_params=_params(),
    )(gwa, gwx, gvec, gnorm_blk, gfin_blk, dsink_blk, loss_blk, gcw)
    out_shape = tuple(_sds(params[n][0].shape, f32) for n in SMALL_NAMES for _ in range(4)) + (_sds((8, 128), f32),)
    res = _pcall(
        adam_body, name="small_adamw", out_shape=out_shape,
        in_specs=[VMEM_SPEC] * (4 + len(flat)), out_specs=tuple([VMEM_SPEC] * nout), compiler_params=_params(),
    )(*sums, *flat)
    return {n: res[4 * k:4 * k + 4] for k, n in enumerate(SMALL_NAMES)}, res[-1]


def _pad_rows(v, rows=8):
    return jnp.concatenate([v, jnp.zeros((rows - v.shape[0], v.shape[1]), v.dtype)], axis=0)


def kernel(x, norm_g, w_in, conv_w, conv_b, lru_w_a, lru_b_a, lru_w_x, lru_b_x, lru_lambda, attn_sinks, w_rnn_out, w_attn_out, w_o, final_norm_g, loss_target, m_norm_g, m_w_in, m_conv_w, m_conv_b, m_lru_w_a, m_lru_b_a, m_lru_w_x, m_lru_b_x, m_lru_lambda, m_attn_sinks, m_w_rnn_out, m_w_attn_out, m_w_o, m_final_norm_g, v_norm_g, v_w_in, v_conv_w, v_conv_b, v_lru_w_a, v_lru_b_a, v_lru_w_x, v_lru_b_x, v_lru_lambda, v_attn_sinks, v_w_rnn_out, v_w_attn_out, v_w_o, v_final_norm_g):
    nb, S, _ = x.shape
    T = nb * S
    x2d = x.reshape(T, D)
    tgt = loss_target.reshape(T, D)
    fin_g = final_norm_g.reshape(1, D)
    w_a3, w_x3 = lru_w_a[0], lru_w_x[0]

    my_core = lax.axis_index("c").astype(jnp.int32).reshape(1)
    cx, cy = lax.axis_index("x"), lax.axis_index("y")
    chip_order = jnp.stack([2 * cx + cy, 2 * (1 - cx) + cy, 2 * cx + (1 - cy),
                            2 * (1 - cx) + (1 - cy)]).astype(jnp.int32)

    tabs = _rope_tables(S)
    h_bf, proj, wt_full, cw_full, _ = _in_proj_gather(
        x2d, norm_g, w_in[0].T.astype(bf16), _pad_rows(conv_w[0]), tabs, S, (), chip_order)
    y_rnn, h_all = _lru_forward(proj, cw_full, conv_b, w_a3, lru_b_a, w_x3, lru_b_x, lru_lambda, S)
    y_attn, (wr_full, wa_full, wo_full) = _attn_forward(proj, attn_sinks, S,
                                                        (w_rnn_out[0], w_attn_out[0], w_o[0]))

    (dx2, dy_rnn, dy_attn, dmr, dma, loss_blk, gfin_blk, g_wr, g_wa, g_wo) = _merge_and_head(
        x2d, tgt, proj, y_rnn, y_attn, wr_full, wa_full, wo_full, fin_g)
    sums_out = _pair_sums([g_wr, g_wa, g_wo], bf16, my_core, "out")

    (dq, dkv, dga, dsink_blk), (p_wr, p_wa, p_wo) = _attn_backward(proj, dy_attn, tabs, attn_sinks, S, sums_out)
    du0, dgr, gwa, gwx, gvec, gcw = _lru_backward(proj, h_all, dy_rnn, cw_full, conv_b, w_a3, lru_b_a, w_x3,
                                                  lru_b_x, lru_lambda, S)
    dsecs = (du0, dgr, dq, dkv, dga, dmr, dma)

    g_wt = _w_in_grad(dsecs, h_bf)
    (sum_in,) = _pair_sums([g_wt], bf16, my_core, "in")
    ex_sems, sum_in, landing, token = _exchange_start(sum_in)
    grad_x2d, gnorm_blk = _input_grad(dsecs, wt_full, x2d, dx2, norm_g + token[0, 0])
    p_wt = _exchange_wait(ex_sems, sum_in, landing, gnorm_blk)
    p_wt_own = lax.dynamic_index_in_dim(sum_in, 2 * cx + cy, axis=0, keepdims=False)

    small, loss_out = _small_step(gwa, gwx, gvec, gnorm_blk, gfin_blk, dsink_blk, loss_blk, gcw, {
        "lru_w_a": (w_a3, m_lru_w_a[0], v_lru_w_a[0]), "lru_w_x": (w_x3, m_lru_w_x[0], v_lru_w_x[0]),
        "conv_b": (conv_b, m_conv_b, v_conv_b), "lru_b_a": (lru_b_a, m_lru_b_a, v_lru_b_a),
        "lru_b_x": (lru_b_x, m_lru_b_x, v_lru_b_x), "lru_lambda": (lru_lambda, m_lru_lambda, v_lru_lambda),
        "norm_g": (norm_g, m_norm_g, v_norm_g),
        "final_norm_g": (fin_g, m_final_norm_g.reshape(1, D), v_final_norm_g.reshape(1, D)),
        "attn_sinks": (attn_sinks, m_attn_sinks, v_attn_sinks),
        "conv_w": (conv_w[0], m_conv_w[0], v_conv_w[0])})

    o_wt = _adamw(p_wt_own, p_wt, w_in[0].T, m_w_in[0].T, v_w_in[0].T, "adamw_w_in")
    o_wr, o_wa, o_wo = _adamw_group(
        (p_wr, p_wa, p_wo), (w_rnn_out[0], w_attn_out[0], w_o[0]),
        (m_w_rnn_out[0], m_w_attn_out[0], m_w_o[0]), (v_w_rnn_out[0], v_w_attn_out[0], v_w_o[0]), "adamw_w_out")

    def result(kind):
        d = {n: small[n][kind] for n in ("conv_b", "lru_b_a", "lru_b_x", "lru_lambda", "norm_g", "attn_sinks")}
        d.update({n: small[n][kind][None] for n in ("lru_w_a", "lru_w_x", "conv_w")})
        d["final_norm_g"] = small["final_norm_g"][kind].reshape(D)
        d.update({"w_in": o_wt[kind].T[None], "w_rnn_out": o_wr[kind][None], "w_attn_out": o_wa[kind][None],
                  "w_o": o_wo[kind][None]})
        return d

    order = ("norm_g", "w_in", "conv_w", "conv_b", "lru_w_a", "lru_b_a", "lru_w_x", "lru_b_x", "lru_lambda",
             "attn_sinks", "w_rnn_out", "w_attn_out", "w_o", "final_norm_g")
    outs = [loss_out[0, 0], grad_x2d.reshape(nb, S, D)]
    for kind in range(4):
        d = result(kind)
        outs += [d[n] for n in order]
    return tuple(outs)
```

```python
import math

import jax
import jax.numpy as jnp
from jax import lax
from jax.experimental import pallas as pl
from jax.experimental.pallas import tpu as pltpu

f32 = jnp.float32
bf16 = jnp.bfloat16

D = 1024
D_IN = 6656
NDEV = 8
RNN_BLOCKS = 8
RB = 128
HEAD = 64
KV_HEADS = 4
GROUP = 4
QB = 128
LRU_C = 8.0
EPS = 1e-6
ROPE_DIM = 16
ROPE_THETA = 500000.0
CH = 512
SEC_START = (0, 2, 4, 6, 7, 9, 11)
SEC_CHUNKS = (2, 2, 2, 1, 2, 2, 2)
VMEM_LIMIT = 62 * 1024 * 1024

ADAM_LR, ADAM_B1, ADAM_B2, ADAM_EPS, ADAM_WD, ADAM_STEP = 0.001, 0.9, 0.999, 1e-08, 0.01, 10

MESH = pl.DeviceIdType.MESH
ANY = pl.BlockSpec(memory_space=pl.ANY)
VMEM_SPEC = pl.BlockSpec(memory_space=pltpu.VMEM)
SMEM_SPEC = pl.BlockSpec(memory_space=pltpu.SMEM)


def _pcall(body, **kw):
    return pl.pallas_call(body, **kw)


def _params(sem=None, **kw):
    if sem is not None:
        kw["dimension_semantics"] = sem
    return pltpu.CompilerParams(vmem_limit_bytes=VMEM_LIMIT, **kw)


def _sds(shape, dtype):
    return jax.ShapeDtypeStruct(shape, dtype)


def _dot(a, b, dims):
    return lax.dot_general(a, b, (dims, ((), ())), preferred_element_type=f32)


NN = ((1,), (0,))
NT = ((1,), (1,))
TN = ((0,), (0,))


def _sigmoid(v):
    return 0.5 * jnp.tanh(0.5 * v) + 0.5


def _sigmoid_positive(v):
    return 1.0 / (1.0 + jnp.exp(-v))


def _my_place():
    return lax.axis_index("x"), lax.axis_index("y"), lax.axis_index("c")


def _peer(k):
    x, y, c = _my_place()
    return (x + ((k >> 2) & 1)) % 2, (y + ((k >> 1) & 1)) % 2, (c + (k & 1)) % 2


def _direct_gather_copies(srcs, outs, send_sems, recv_sems, local_sems):
    x, y, c = _my_place()
    me = 4 * x + 2 * y + c
    local, remote = [], []
    for a, (src, out) in enumerate(zip(srcs, outs)):
        r = src.shape[0]
        mine = out.at[pl.ds(pl.multiple_of(me * r, 8), r), :]
        local.append(pltpu.make_async_copy(src, mine, local_sems.at[a]))
        for k in range(1, NDEV):
            remote.append(pltpu.make_async_remote_copy(
                src_ref=src, dst_ref=mine, send_sem=send_sems.at[7 * a + k - 1], recv_sem=recv_sems.at[7 * a + k - 1],
                device_id=_peer(k), device_id_type=MESH))
    return local, remote


def _chip_exchange_copies(src, dst, send_sems, recv_sems, local_sems):
    x, y, c = _my_place()
    local, remote = [], []
    for a in range(len(src)):
        local.append(pltpu.make_async_copy(src[a].at[2 * x + y], dst[a].at[0], local_sems.at[a]))
    for k in (3, 1, 2):
        px, py = (x + (k >> 1)) % 2, (y + (k & 1)) % 2
        for a in range(len(src)):
            remote.append(pltpu.make_async_remote_copy(
                src_ref=src[a].at[2 * px + py], dst_ref=dst[a].at[k],
                send_sem=send_sems.at[3 * a + k - 1], recv_sem=recv_sems.at[3 * a + k - 1],
                device_id=(px, py, c), device_id_type=MESH))
    return local, remote


def _exchange_scratch(narr, per_array):
    return [pltpu.SemaphoreType.DMA((per_array * narr,)), pltpu.SemaphoreType.DMA((per_array * narr,)),
            pltpu.SemaphoreType.DMA((narr,))]


def _start_all(copies):
    local, remote = copies
    for cp in local + remote:
        cp.start()


def _wait_all(copies):
    local, remote = copies
    for cp in remote + local:
        cp.wait()


def _row_tile(rows, dtype):
    unit = 16 if dtype == bf16 else 8
    for cand in (256, 208, 128, 64, 40, 32, 16, 8):
        if rows % cand == 0 and cand % unit == 0:
            return cand
    return rows


def _pair_sums(grads, wire_dtype, my_core, tag):
    narr = len(grads)
    r, cols = grads[0].shape[0] // NDEV, grads[0].shape[1]
    views = [g.reshape(4, 2, r, cols) for g in grads]
    tr = _row_tile(r, wire_dtype)
    nt = r // tr

    def body(core_ref, *refs):
        mine = refs[:narr]
        whole = refs[narr:2 * narr]
        outs = refs[2 * narr:3 * narr]
        got = refs[3 * narr:4 * narr]
        send_sems, recv_sems = refs[4 * narr:]
        q, i = pl.program_id(0), pl.program_id(1)
        x, y, c = _my_place()

        def copy(a, chip):
            return pltpu.make_async_remote_copy(
                src_ref=whole[a].at[chip, 1 - c], dst_ref=got[a].at[chip],
                send_sem=send_sems.at[4 * a + chip], recv_sem=recv_sems.at[4 * a + chip],
                device_id=(x, y, 1 - c), device_id_type=MESH)

        @pl.when((q == 0) & (i == 0))
        def _():
            for chip in range(4):
                for a in range(narr):
                    copy(a, chip).start()

        for chip in range(4):
            @pl.when((q == chip) & (i == 0))
            def _(chip=chip):
                for a in range(narr):
                    copy(a, chip).wait_recv()

        rows = pl.ds(pl.multiple_of(i * tr, tr), tr)
        for a in range(narr):
            outs[a][...] = (mine[a][...].astype(f32) + got[a][q, rows, :].astype(f32)).astype(wire_dtype)

        @pl.when((q == 3) & (i == nt - 1))
        def _():
            for chip in range(4):
                for a in range(narr):
                    copy(a, chip).wait_send()

    slab = pl.BlockSpec((None, tr, cols), lambda q, i, core: (q, i, 0))
    grid_spec = pltpu.PrefetchScalarGridSpec(
        num_scalar_prefetch=1, grid=(4, nt),
        in_specs=[pl.BlockSpec((None, None, tr, cols), lambda q, i, core: (q, core[0], i, 0))] * narr + [ANY] * narr,
        out_specs=tuple([slab] * narr),
        scratch_shapes=[pltpu.VMEM((4, r, cols), grads[0].dtype)] * narr
        + [pltpu.SemaphoreType.DMA((4 * narr,)), pltpu.SemaphoreType.DMA((4 * narr,))])
    return _pcall(body, name="pair_sums_" + tag, grid_spec=grid_spec,
                  out_shape=tuple(_sds((4, r, cols), wire_dtype) for _ in range(narr)),
                  compiler_params=_params(("arbitrary", "arbitrary")))(my_core, *views, *views)


def _adam_math(g, w, m, v):
    m_new = ADAM_B1 * m + (1.0 - ADAM_B1) * g
    v_new = ADAM_B2 * v + (1.0 - ADAM_B2) * (g * g)
    m_hat = m_new / (1.0 - ADAM_B1 ** ADAM_STEP)
    v_hat = v_new / (1.0 - ADAM_B2 ** ADAM_STEP)
    return -ADAM_LR * (m_hat / (jnp.sqrt(v_hat) + ADAM_EPS) + ADAM_WD * w), m_new, v_new


def _adamw(first, parts, w, m, v, name):
    n, rows, cols = parts.shape
    tr = _row_tile(rows, parts.dtype)

    def body(f_ref, p_ref, w_ref, m_ref, v_ref, g_out, d_out, m_out, v_out):
        g = f_ref[...].astype(f32)
        for s in range(n):
            g = g + p_ref[s].astype(f32)
        g_out[...] = g
        d_out[...], m_out[...], v_out[...] = _adam_math(g, w_ref[...], m_ref[...], v_ref[...])

    blk = pl.BlockSpec((tr, cols), lambda i: (i, 0))
    return _pcall(
        body, name=name, grid=(rows // tr,),
        in_specs=[blk, pl.BlockSpec((n, tr, cols), lambda i: (0, i, 0)), blk, blk, blk],
        out_specs=(blk, blk, blk, blk), out_shape=tuple(_sds((rows, cols), f32) for _ in range(4)),
        compiler_params=_params(("arbitrary",)),
    )(first, parts, w, m, v)


def _adamw_group(parts, ws, ms, vs, name):
    nw = len(ws)

    def body(*refs):
        p_refs, w_refs, m_refs, v_refs = (refs[k * nw:(k + 1) * nw] for k in range(4))
        outs = refs[4 * nw:]
        for k in range(nw):
            g = p_refs[k][0].astype(f32)
            for s in range(1, p_refs[k].shape[0]):
                g = g + p_refs[k][s].astype(f32)
            g_out, d_out, m_out, v_out = outs[4 * k:4 * k + 4]
            g_out[...] = g
            d_out[...], m_out[...], v_out[...] = _adam_math(g, w_refs[k][...], m_refs[k][...], v_refs[k][...])

    res = _pcall(
        body, name=name, out_shape=tuple(_sds(w.shape, f32) for w in ws for _ in range(4)),
        in_specs=[VMEM_SPEC] * (4 * nw), out_specs=tuple([VMEM_SPEC] * (4 * nw)), compiler_params=_params(),
    )(*parts, *ws, *ms, *vs)
    return [res[4 * k:4 * k + 4] for k in range(nw)]


def _rope(t, c, s1, s2):
    w = t.shape[1]
    return t * c + pltpu.roll(t, w - 8, 1) * s1 + pltpu.roll(t, 8, 1) * s2


def _rope_transposed(dt, c, s1, s2):
    w = dt.shape[1]
    return dt * c + pltpu.roll(dt * s1, 8, 1) + pltpu.roll(dt * s2, w - 8, 1)


PAIR_ROWS = D_IN // 4
SUB_COLS = ((0, 512), (512, 512), (1024, 512), (1536, 128))
Q_SLABS = range(3, 11)
K_SLABS = range(11, 13)


def _in_proj_gather(x2d, norm_g, wt_shard, cw_shard, tabs, S, out_shards, chip_order):
    T = x2d.shape[0]
    tb = min(S, 1024)
    ntok = T // tb
    nsb = S // tb
    q_scale = 1.0 / math.sqrt(HEAD)
    shard_rows = wt_shard.shape[0]
    small = (cw_shard,) + tuple(out_shards)
    nsm = len(small)

    def body(order_ref, x_ref, g_ref, c_ref, s1_ref, s2_ref, wt_hbm, *rest):
        small_in = rest[:nsm]
        h_ref, proj_ref, wt_out = rest[nsm:nsm + 3]
        small_out = rest[nsm + 3:2 * nsm + 3]
        wt_vm, h_vm = rest[2 * nsm + 3:2 * nsm + 5]
        stage = rest[2 * nsm + 5:3 * nsm + 4]
        wsend, wrecv, wlocal = rest[3 * nsm + 4:3 * nsm + 7]
        dsems = rest[3 * nsm + 7:]
        jj, i = pl.program_id(0), pl.program_id(1)
        x, y, c = _my_place()
        me, sibling = (x, y, c), (x, y, 1 - c)
        chips = [(1 - x, y), (x, 1 - y), (1 - x, 1 - y)]

        def rows(place):
            px, py, pc = place
            return wt_vm.at[pl.ds(pl.multiple_of((4 * px + 2 * py + pc) * shard_rows, 16), shard_rows), :]

        def copy(k, block, to, src=None):
            return pltpu.make_async_remote_copy(
                src_ref=rows(block) if src is None else src, dst_ref=rows(block),
                send_sem=wsend.at[k], recv_sem=wrecv.at[k], device_id=to, device_id_type=MESH)

        def small_copies():
            srcs = (small_in[0],) + tuple(stage)
            return _direct_gather_copies(srcs, small_out, *dsems)

        own = pltpu.make_async_copy(wt_hbm, rows(me), wlocal.at[0])
        keep = pltpu.make_async_copy(wt_vm, wt_out, wlocal.at[1])

        @pl.when((jj == 0) & (i == 0))
        def _():
            own.start()
            copy(0, me, sibling, src=wt_hbm).start()
            for j, chip in enumerate(chips):
                copy(1 + j, me, (*chip, c), src=wt_hbm).start()
            for a in range(nsm - 1):
                stage[a][...] = small_in[1 + a][...].astype(bf16)
            _start_all(small_copies())
            own.wait()
            copy(0, sibling, me).wait_recv()

        for j, chip in enumerate(chips):
            @pl.when((jj == 1 + j) & (i == 0))
            def _(j=j, chip=chip):
                copy(1 + j, (*chip, c), me).wait_recv()
                copy(4 + j, (*chip, c), sibling).start()
                copy(4 + j, (*chip, 1 - c), me).wait_recv()

        @pl.when((jj == 3) & (i == 0))
        def _():
            keep.start()

        @pl.when((jj == 3) & (i == ntok - 1))
        def _():
            copy(0, me, sibling, src=wt_hbm).wait_send()
            for j, chip in enumerate(chips):
                copy(1 + j, me, (*chip, c), src=wt_hbm).wait_send()
                copy(4 + j, (*chip, c), sibling).wait_send()
            _wait_all(small_copies())
            keep.wait()

        tok = pl.ds(pl.multiple_of(i * tb, tb), tb)

        @pl.when(jj == 0)
        def _():
            xv = x_ref[...]
            ms = jnp.mean(xv * xv, axis=-1, keepdims=True)
            hb = (xv * lax.rsqrt(ms + EPS) * g_ref[...]).astype(bf16)
            h_ref[...] = hb
            h_vm[tok, :] = hb

        block = order_ref[jj]
        hb = h_vm[tok, :]

        def piece(c0, w):
            w_rows = wt_vm[pl.ds(pl.multiple_of(block * PAIR_ROWS + c0, 128), w), :]
            return _dot(hb, w_rows, NT)

        @pl.when(block != 1)
        def _():
            for c0, w in SUB_COLS:
                proj_ref[:, c0:c0 + w] = piece(c0, w).astype(bf16)

        @pl.when(block == 1)
        def _():
            tab = (c_ref[...], s1_ref[...], s2_ref[...])
            for c0, w in SUB_COLS:
                acc = piece(c0, w)
                for l in range(w // 128):
                    slab = (c0 + 128 * l) // 128
                    part = acc[:, 128 * l:128 * (l + 1)]
                    if slab in Q_SLABS:
                        part = _rope(part, *tab) * q_scale
                    elif slab in K_SLABS:
                        part = _rope(part, *tab)
                    proj_ref[:, 128 * slab:128 * (slab + 1)] = part.astype(bf16)

    first_pass = lambda jj, i, order: (jnp.where(jj == 0, i, ntok - 1), 0)
    const = lambda jj, i, order: (0, 0)
    tab = pl.BlockSpec((tb, 128), lambda jj, i, order: (jnp.where(order[jj] == 1, i % nsb, 0), 0))
    grid_spec = pltpu.PrefetchScalarGridSpec(
        num_scalar_prefetch=1, grid=(4, ntok),
        in_specs=[pl.BlockSpec((tb, D), first_pass), pl.BlockSpec((1, D), const), tab, tab, tab, ANY]
        + [pl.BlockSpec(w.shape, const) for w in small],
        out_specs=(pl.BlockSpec((tb, D), first_pass),
                   pl.BlockSpec((tb, PAIR_ROWS), lambda jj, i, order: (i, order[jj])), ANY) + tuple([ANY] * nsm),
        scratch_shapes=[pltpu.VMEM((D_IN, D), bf16), pltpu.VMEM((T, D), bf16)]
        + [pltpu.VMEM(w.shape, bf16) for w in out_shards]
        + [pltpu.SemaphoreType.DMA((7,)), pltpu.SemaphoreType.DMA((7,)), pltpu.SemaphoreType.DMA((2,))]
        + _exchange_scratch(nsm, 7))
    res = _pcall(
        body, name="in_proj", grid_spec=grid_spec,
        out_shape=(_sds((T, D), bf16), _sds((T, D_IN), bf16), _sds((D_IN, D), bf16),
                   _sds((NDEV * cw_shard.shape[0], cw_shard.shape[1]), f32))
        + tuple(_sds((NDEV * w.shape[0], w.shape[1]), bf16) for w in out_shards),
        compiler_params=_params(("arbitrary", "arbitrary")),
    )(chip_order, x2d, norm_g, *tabs, wt_shard, *small)
    return res[0], res[1], res[2], res[3], res[4:]


def _rows_iota(shape):
    return lax.broadcasted_iota(jnp.int32, shape, 0)


def _shift_down(v, k):
    return jnp.where(_rows_iota(v.shape) >= k, pltpu.roll(v, k, 0), 0.0)


def _shift_up(v, k):
    n = v.shape[0]
    return jnp.where(_rows_iota(v.shape) < n - k, pltpu.roll(v, n - k, 0), 0.0)


def _linear_scan(a, b, a_s, b_s, edge_s, out_ref, reverse):
    n = a.shape[0]
    ng = n // 8
    a3, b3 = a.reshape(ng, 8, RB), b.reshape(ng, 8, RB)
    rid = lax.broadcasted_iota(jnp.int32, a3.shape, 1)
    for s in (1, 2, 4):
        keep, shift = (rid < 8 - s, 8 - s) if reverse else (rid >= s, s)
        b3 = jnp.where(keep, a3 * pltpu.roll(b3, shift, 1) + b3, b3)
        a3 = jnp.where(keep, a3 * pltpu.roll(a3, shift, 1), a3)
    a_s[...] = a3.reshape(n, RB)
    b_s[...] = b3.reshape(n, RB)
    edge = 0 if reverse else 7
    ea, eb = a_s[pl.ds(edge, ng, stride=8), :], b_s[pl.ds(edge, ng, stride=8), :]
    r = _rows_iota(ea.shape)
    s = 1
    while s < ng:
        keep, shift = (r < ng - s, ng - s) if reverse else (r >= s, s)
        eb = jnp.where(keep, ea * pltpu.roll(eb, shift, 0) + eb, eb)
        if 2 * s < ng:
            ea = jnp.where(keep, ea * pltpu.roll(ea, shift, 0), ea)
        s *= 2
    edge_s[...] = _shift_up(eb, 1) if reverse else _shift_down(eb, 1)

    def eight_groups(i, carry):
        for k in range(8):
            j = i * 8 + k
            rows = pl.ds(pl.multiple_of(j * 8, 8), 8)
            out_ref[rows, :] = b_s[rows, :] + a_s[rows, :] * edge_s[pl.ds(j, 1), :]
        return carry

    lax.fori_loop(0, ng // 8, eight_groups, 0)


def _neg_expm1(v):
    series = -v * (1.0 + v * (0.5 + v * (1.0 / 6.0)))
    return jnp.where(v > -0.015625, series, 1.0 - jnp.exp(v))


def _softplus_neg(lam):
    return jnp.maximum(-lam, 0.0) + jnp.log(1.0 + jnp.exp(-jnp.abs(lam)))


def _lru_gates(x0, cw, cb, wa, ba, wx, bx, lam):
    taps = [_shift_down(x0, 3 - k) for k in range(3)] + [x0]
    u = cb + cw[3:4, :] * x0
    for k in range(3):
        u = u + cw[k:k + 1, :] * taps[k]
    ub = u.astype(bf16)
    r = _sigmoid_positive(_dot(ub, wa.astype(bf16), NN) + ba)
    i = _sigmoid(_dot(ub, wx.astype(bf16), NN) + bx)
    sp = _softplus_neg(lam)
    log_a = (-LRU_C) * r * sp
    a = jnp.exp(log_a)
    w = _neg_expm1(2.0 * log_a)
    inv_mult = lax.rsqrt(w)
    return u, ub, r, i, sp, a, w * inv_mult, inv_mult, taps


def _lru_specs(S, nb):
    col = lambda off: pl.BlockSpec((S, RB), lambda n, b, off=off: (b, off + n))
    vec = pl.BlockSpec((1, RB), lambda n, b: (0, n))
    wblk = pl.BlockSpec((None, RB, RB), lambda n, b: (n, 0, 0))
    cwblk = pl.BlockSpec((8, RB), lambda n, b: (n, 0))
    return col, vec, wblk, cwblk


def _lru_forward(proj, cw_full, conv_b, w_a, b_a, w_x, b_x, lam, S):
    T = proj.shape[0]
    nb = T // S
    col, vec, wblk, cwblk = _lru_specs(S, nb)

    def body(x0_ref, g_ref, cw_ref, cb_ref, wa_ref, ba_ref, wx_ref, bx_ref, lam_ref, y_ref, h_ref, a_s, b_s, edge_s):
        x0 = x0_ref[...].astype(f32)
        u, ub, r, i, sp, a, mult, _, _ = _lru_gates(x0, cw_ref[...], cb_ref[...], wa_ref[...], ba_ref[...],
                                                    wx_ref[...], bx_ref[...], lam_ref[...])
        _linear_scan(a, mult * (i * u), a_s, b_s, edge_s, h_ref, reverse=False)
        g = g_ref[...].astype(f32)
        y_ref[...] = (h_ref[...] * (g * _sigmoid(g))).astype(bf16)

    out = pl.BlockSpec((S, RB), lambda n, b: (b, n))
    return _pcall(
        body, name="lru_forward", grid=(RNN_BLOCKS, nb),
        in_specs=[col(0), col(8), cwblk, vec, wblk, vec, wblk, vec, vec],
        out_specs=(out, out), out_shape=(_sds((T, D), bf16), _sds((T, D), f32)),
        scratch_shapes=[pltpu.VMEM((S, RB), f32), pltpu.VMEM((S, RB), f32), pltpu.VMEM((S // 8, RB), f32)],
        compiler_params=_params(("arbitrary", "arbitrary")),
    )(proj, proj, cw_full, conv_b, w_a, b_a, w_x, b_x, lam)


def _rope_tables(S):
    pos = jnp.arange(S, dtype=f32)
    inv_freq = ROPE_THETA ** (-jnp.arange(0, ROPE_DIM, 2, dtype=f32) / ROPE_DIM)
    ang = pos[:, None] * inv_freq[None, :]
    cos, sin = jnp.cos(ang), jnp.sin(ang)
    lane = jnp.arange(128) % HEAD
    cosl, sinl = cos[:, lane % 8], sin[:, lane % 8]
    c = jnp.where(lane[None, :] < ROPE_DIM, cosl, 1.0)
    s1 = jnp.where(lane[None, :] < 8, -sinl, 0.0)
    s2 = jnp.where((lane[None, :] >= 8) & (lane[None, :] < ROPE_DIM), sinl, 0.0)
    return c.astype(f32), s1.astype(f32), s2.astype(f32)


def _heads_to_rows(t):
    return jnp.concatenate([t[:, HEAD * h:HEAD * (h + 1)] for h in range(GROUP)], axis=0)


def _rows_to_heads(t):
    return jnp.concatenate([t[QB * h:QB * (h + 1), :] for h in range(GROUP)], axis=1)


def _window_bias(first_block):
    shape = (GROUP * QB, 2 * QB)
    qi = _rows_iota(shape) % QB
    cj = lax.broadcasted_iota(jnp.int32, shape, 1)
    valid = (cj > qi) & (cj <= qi + QB) & ((cj >= QB) | jnp.logical_not(first_block))
    return jnp.where(valid, 0.0, -jnp.inf)


def _attn_probs(q_rows, k_cat, sink_col, bias):
    s = _dot(q_rows, k_cat, NT) + bias
    m = jnp.maximum(jnp.max(s, axis=1, keepdims=True), sink_col)
    p = jnp.exp(s - m)
    e_sink = jnp.exp(sink_col - m)
    inv = 1.0 / (jnp.sum(p, axis=1, keepdims=True) + e_sink)
    return p * inv, e_sink * inv


def _sink_column(sink_ref, kv):
    rid = _rows_iota((GROUP * QB, 1))
    col = jnp.zeros((GROUP * QB, 1), f32)
    for h in range(GROUP):
        col = jnp.where(rid // QB == h, sink_ref[0, GROUP * kv + h], col)
    return col


def _attn_in_specs(S):
    nq = S // QB
    last = nq - 1
    cur = lambda b, j: b * nq + jnp.minimum(j, last)
    prev = lambda b, j: b * nq + jnp.maximum(jnp.minimum(j, last) - 1, 0)
    specs = [
        pl.BlockSpec((QB, D), lambda b, j: (cur(b, j), 2)),
        pl.BlockSpec((QB, 256), lambda b, j: (cur(b, j), 12)),
        pl.BlockSpec((QB, 256), lambda b, j: (prev(b, j), 12)),
        pl.BlockSpec((QB, 256), lambda b, j: (cur(b, j), 13)),
        pl.BlockSpec((QB, 256), lambda b, j: (prev(b, j), 13)),
        pl.BlockSpec((QB, 512), lambda b, j: (cur(b, j), 7)),
        pl.BlockSpec((QB, 512), lambda b, j: (cur(b, j), 8)),
        SMEM_SPEC,
    ]
    return specs, cur, prev


def _attn_forward(proj, sinks, S, out_shards):
    T = proj.shape[0]
    nb, nq = T // S, S // QB
    specs, cur, _ = _attn_in_specs(S)
    nw = len(out_shards)

    def body(q_ref, kc_ref, kp_ref, vc_ref, vp_ref, gl_ref, gh_ref, sink_ref, *rest):
        shards = rest[:nw]
        y_ref, o_ref = rest[nw:nw + 2]
        gathered = rest[nw + 2:2 * nw + 2]
        stage = rest[2 * nw + 2:3 * nw + 2]
        sems = rest[3 * nw + 2:]
        b, j = pl.program_id(0), pl.program_id(1)

        @pl.when((b == 0) & (j == 0))
        def _():
            for a in range(nw):
                stage[a][...] = shards[a][...].astype(bf16)
            _start_all(_direct_gather_copies(stage, gathered, *sems))

        @pl.when((b == nb - 1) & (j == nq - 1))
        def _():
            _wait_all(_direct_gather_copies(stage, gathered, *sems))

        bias = _window_bias(j == 0)
        kc, kp, vc, vp = kc_ref[...], kp_ref[...], vc_ref[...], vp_ref[...]
        for kv in range(KV_HEADS):
            lanes = slice(256 * kv, 256 * (kv + 1))
            hl = slice(HEAD * kv, HEAD * (kv + 1))
            q_rows = _heads_to_rows(q_ref[:, lanes])
            k_cat = jnp.concatenate([kp[:, hl], kc[:, hl]], axis=0)
            v_cat = jnp.concatenate([vp[:, hl], vc[:, hl]], axis=0)
            probs, _ = _attn_probs(q_rows, k_cat, _sink_column(sink_ref, kv), bias)
            o = _rows_to_heads(_dot(probs.astype(bf16), v_cat, NN))
            g_src = gl_ref if kv < 2 else gh_ref
            g = g_src[:, 256 * (kv % 2):256 * (kv % 2 + 1)].astype(f32)
            y_ref[:, lanes] = (o * (g * _sigmoid(g))).astype(bf16)
            o_ref[:, lanes] = o.astype(bf16)

    args = [proj] * 7 + [sinks] + list(out_shards)
    tok = pl.BlockSpec((QB, D), lambda b, j: (cur(b, j), 0))
    res = _pcall(
        body, name="attn_forward", grid=(nb, nq),
        in_specs=specs + [pl.BlockSpec(w.shape, lambda b, j: (0, 0)) for w in out_shards],
        out_specs=(tok, tok) + tuple([ANY] * nw),
        out_shape=(_sds((T, D), bf16), _sds((T, D), bf16))
        + tuple(_sds((NDEV * w.shape[0], w.shape[1]), bf16) for w in out_shards),
        scratch_shapes=[pltpu.VMEM(w.shape, bf16) for w in out_shards] + _exchange_scratch(nw, 7),
        compiler_params=_params(("arbitrary", "arbitrary")),
    )(*args)
    return res[0], res[1], res[2:]


def _merge_and_head(x2d, tgt, proj, y_rnn, y_attn, w_r, w_a, w_o, gfin):
    T = x2d.shape[0]
    tb = min(T, 512)
    nsteps = T // tb

    def body(x_ref, t_ref, mr0, mr1, ma0, ma1, yr_ref, ya_ref, wr_ref, wa_ref, wo_ref, gf_ref,
             dx2_ref, dyr_ref, dya_ref, dmr_ref, dma_ref, loss_ref, gfin_ref, gwr_out, gwa_out, gwo_out,
             gwr_acc, gwa_acc, gwo_acc, out_sems):
        step = pl.program_id(0)

        @pl.when(step == 0)
        def _():
            loss_ref[...] = jnp.zeros_like(loss_ref)
            gfin_ref[...] = jnp.zeros_like(gfin_ref)
            gwr_acc[...] = jnp.zeros_like(gwr_acc)
            gwa_acc[...] = jnp.zeros_like(gwa_acc)
            gwo_acc[...] = jnp.zeros_like(gwo_acc)

        sr = _sigmoid(jnp.concatenate([mr0[...], mr1[...]], axis=1).astype(f32))
        sa = _sigmoid(jnp.concatenate([ma0[...], ma1[...]], axis=1).astype(f32))
        p_r = _dot(yr_ref[...], wr_ref[...], NN)
        p_a = _dot(ya_ref[...], wa_ref[...], NN)
        merged = (sr * p_r + sa * p_a).astype(bf16)
        x2 = x_ref[...] + _dot(merged, wo_ref[...], NN)
        rstd = lax.rsqrt(jnp.mean(x2 * x2, axis=-1, keepdims=True) + EPS)
        xh = x2 * rstd
        gf = gf_ref[...]
        err = xh * gf - t_ref[...]
        loss_ref[...] += jnp.sum(err * err)
        dy = err * (1.0 / D)
        gfin_ref[0:1, :] += jnp.sum(dy * xh, axis=0, keepdims=True)
        dxn = dy * gf
        dx2 = rstd * (dxn - xh * jnp.mean(dxn * xh, axis=-1, keepdims=True))
        dx2_ref[...] = dx2
        dx2b = dx2.astype(bf16)
        dmerged = _dot(dx2b, wo_ref[...], NT)
        gwo_acc[...] += _dot(merged, dx2b, TN)
        dpr = (dmerged * sr).astype(bf16)
        dyr_ref[...] = _dot(dpr, wr_ref[...], NT).astype(bf16)
        gwr_acc[...] += _dot(yr_ref[...], dpr, TN)
        dpa = (dmerged * sa).astype(bf16)
        dya_ref[...] = _dot(dpa, wa_ref[...], NT).astype(bf16)
        gwa_acc[...] += _dot(ya_ref[...], dpa, TN)
        dmr_ref[...] = (dmerged * p_r * (sr * (1.0 - sr))).astype(bf16)
        dma_ref[...] = (dmerged * p_a * (sa * (1.0 - sa))).astype(bf16)

        @pl.when(step == nsteps - 1)
        def _():
            copies = [pltpu.make_async_copy(src, dst, out_sems.at[k]) for k, (src, dst) in enumerate(
                ((gwr_acc, gwr_out), (gwa_acc, gwa_out), (gwo_acc, gwo_out)))]
            for cp in copies:
                cp.start()
            for cp in copies:
                cp.wait()

    tok = pl.BlockSpec((tb, D), lambda i: (i, 0))
    half = lambda c: pl.BlockSpec((tb, CH), lambda i, c=c: (i, c))
    wfull = pl.BlockSpec((D, D), lambda i: (0, 0), pipeline_mode=pl.Buffered(1))
    acc = pl.BlockSpec((8, D), lambda i: (0, 0))
    return _pcall(
        body, name="merge_and_head", grid=(nsteps,),
        in_specs=[tok, tok, half(9), half(10), half(11), half(12), tok, tok, wfull, wfull, wfull,
                  pl.BlockSpec((1, D), lambda i: (0, 0))],
        out_specs=(tok, tok, tok, tok, tok, acc, acc, ANY, ANY, ANY),
        out_shape=(_sds((T, D), f32), _sds((T, D), bf16), _sds((T, D), bf16), _sds((T, D), bf16),
                   _sds((T, D), bf16), _sds((8, D), f32), _sds((8, D), f32),
                   _sds((D, D), f32), _sds((D, D), f32), _sds((D, D), f32)),
        scratch_shapes=[pltpu.VMEM((D, D), f32)] * 3 + [pltpu.SemaphoreType.DMA((3,))],
        compiler_params=_params(("arbitrary",)),
    )(x2d, tgt, proj, proj, proj, proj, y_rnn, y_attn, w_r, w_a, w_o, gfin)


def _attn_backward(proj, dy_attn, o_attn, tabs, sinks, S, chip_sums):
    T = proj.shape[0]
    nb, nq = T // S, S // QB
    nex = len(chip_sums)
    specs, cur, prev = _attn_in_specs(S)
    last = nq - 1
    tab_cur = pl.BlockSpec((QB, 128), lambda b, j: (jnp.minimum(j, last), 0))
    tab_prev = pl.BlockSpec((QB, 128), lambda b, j: (jnp.maximum(jnp.minimum(j, last) - 1, 0), 0))
    specs = specs + [pl.BlockSpec((QB, D), lambda b, j: (cur(b, j), 0))] * 2 + [tab_cur] * 3 + [tab_prev] * 3
    q_scale = 1.0 / math.sqrt(HEAD)

    def rope_back(dt, tab):
        return jnp.concatenate([_rope_transposed(dt[:, 128 * l:128 * (l + 1)], *tab) for l in range(2)], axis=1)

    def body(q_ref, kc_ref, kp_ref, vc_ref, vp_ref, gl_ref, gh_ref, sink_ref, dy_ref, o_ref, cc, s1c, s2c, cp, s1p,
             s2p, *rest):
        ex_src = rest[:nex]
        dq_ref, dkv_ref, dg_ref, dsink_ref = rest[nex:nex + 4]
        ex_dst = rest[nex + 4:2 * nex + 4]
        carry_k, carry_v = rest[2 * nex + 4:2 * nex + 6]
        sems = rest[2 * nex + 6:]
        b, j = pl.program_id(0), pl.program_id(1)

        @pl.when((b == 0) & (j == 0))
        def _():
            dsink_ref[...] = jnp.zeros_like(dsink_ref)
            _start_all(_chip_exchange_copies(ex_src, ex_dst, *sems))

        @pl.when((b == nb - 1) & (j == nq))
        def _():
            _wait_all(_chip_exchange_copies(ex_src, ex_dst, *sems))

        @pl.when(j == 0)
        def _():
            carry_k[...] = jnp.zeros_like(carry_k)
            carry_v[...] = jnp.zeros_like(carry_v)

        @pl.when(j < nq)
        def _():
            bias = _window_bias(j == 0)
            tc = (cc[...], s1c[...], s2c[...])
            tp = (cp[...], s1p[...], s2p[...])
            kc, kp, vc, vp = kc_ref[...], kp_ref[...], vc_ref[...], vp_ref[...]
            dk_prev, dk_cur, dv_prev, dv_cur = [], [], [], []
            dsink_acc = jnp.zeros((8, 128), f32)
            r8 = lax.broadcasted_iota(jnp.int32, (8, 128), 0)
            l8 = lax.broadcasted_iota(jnp.int32, (8, 128), 1)
            for kv in range(KV_HEADS):
                lanes = slice(256 * kv, 256 * (kv + 1))
                hl = slice(HEAD * kv, HEAD * (kv + 1))
                q_rows = _heads_to_rows(q_ref[:, lanes])
                k_cat = jnp.concatenate([kp[:, hl], kc[:, hl]], axis=0)
                v_cat = jnp.concatenate([vp[:, hl], vc[:, hl]], axis=0)
                probs, p_sink = _attn_probs(q_rows, k_cat, _sink_column(sink_ref, kv), bias)
                pb = probs.astype(bf16)
                g_src = gl_ref if kv < 2 else gh_ref
                g = g_src[:, 256 * (kv % 2):256 * (kv % 2 + 1)].astype(f32)
                sg = _sigmoid(g)
                dy = dy_ref[:, lanes].astype(f32)
                do_rows = _heads_to_rows(dy * (g * sg)).astype(bf16)
                dp = _dot(do_rows, v_cat, NT)
                rowdot = jnp.sum(probs * dp, axis=1, keepdims=True)
                ds = (probs * (dp - rowdot)).astype(bf16)
                sink_rows = -(p_sink * rowdot)
                dq = _rows_to_heads(_dot(ds, k_cat, NN)) * q_scale
                dq_ref[:, lanes] = rope_back(dq, tc).astype(bf16)
                dk = _dot(ds, q_rows, TN)
                o = o_ref[:, lanes].astype(f32)
                dg_ref[:, lanes] = (dy * o * (sg * (1.0 + g * (1.0 - sg)))).astype(bf16)
                dv = _dot(pb, do_rows, TN)
                for h in range(GROUP):
                    val = jnp.sum(sink_rows[QB * h:QB * (h + 1), :])
                    dsink_acc = dsink_acc + jnp.where((r8 == 0) & (l8 == GROUP * kv + h), val, 0.0)
                dk_prev.append(dk[:QB, :])
                dk_cur.append(dk[QB:, :])
                dv_prev.append(dv[:QB, :])
                dv_cur.append(dv[QB:, :])
            dsink_ref[...] += dsink_acc
            dkp = rope_back(jnp.concatenate(dk_prev, axis=1), tp)
            dkc = rope_back(jnp.concatenate(dk_cur, axis=1), tc)
            dkv_ref[:, 0:256] = (carry_k[...] + dkp).astype(bf16)
            dkv_ref[:, 256:512] = (carry_v[...] + jnp.concatenate(dv_prev, axis=1)).astype(bf16)
            carry_k[...] = dkc
            carry_v[...] = jnp.concatenate(dv_cur, axis=1)

        @pl.when(j == nq)
        def _():
            dkv_ref[:, 0:256] = carry_k[...].astype(bf16)
            dkv_ref[:, 256:512] = carry_v[...].astype(bf16)

    lag = lambda b, j: (b * nq + jnp.maximum(j - 1, 0), 0)
    args = [proj] * 7 + [sinks, dy_attn, o_attn] + list(tabs) + list(tabs) + list(chip_sums)
    res = _pcall(
        body, name="attn_backward", grid=(nb, nq + 1), in_specs=specs + [ANY] * nex,
        out_specs=(pl.BlockSpec((QB, D), lambda b, j: (cur(b, j), 0)), pl.BlockSpec((QB, 512), lag),
                   pl.BlockSpec((QB, D), lambda b, j: (cur(b, j), 0)), pl.BlockSpec((8, 128), lambda b, j: (0, 0)))
        + tuple([ANY] * nex),
        out_shape=(_sds((T, D), bf16), _sds((T, 512), bf16), _sds((T, D), bf16), _sds((8, 128), f32))
        + tuple(_sds(s.shape, s.dtype) for s in chip_sums),
        scratch_shapes=[pltpu.VMEM((QB, 256), f32), pltpu.VMEM((QB, 256), f32)] + _exchange_scratch(nex, 3),
        compiler_params=_params(("arbitrary", "arbitrary")),
    )(*args)
    return res[:4], res[4:]


def _lru_backward(proj, h_all, dy_rnn, cw_full, conv_b, w_a, b_a, w_x, b_x, lam, S):
    T = proj.shape[0]
    nb = T // S
    col, vec, wblk, cwblk = _lru_specs(S, nb)
    tokblk = pl.BlockSpec((S, RB), lambda n, b: (b, n))

    def body(x0_ref, g_ref, h_ref, dy_ref, cw_ref, cb_ref, wa_ref, ba_ref, wx_ref, bx_ref, lam_ref,
             du0_ref, dg_ref, gwa_ref, gwx_ref, vec_ref, gcw_ref, a_s, b_s, dh_s, edge_s):
        @pl.when(pl.program_id(1) == 0)
        def _():
            gwa_ref[...] = jnp.zeros_like(gwa_ref)
            gwx_ref[...] = jnp.zeros_like(gwx_ref)
            vec_ref[...] = jnp.zeros_like(vec_ref)
            gcw_ref[...] = jnp.zeros_like(gcw_ref)

        x0 = x0_ref[...].astype(f32)
        cw = cw_ref[...]
        lam_v = lam_ref[...]
        u, ub, r, i, sp, a, mult, inv_mult, taps = _lru_gates(x0, cw, cb_ref[...], wa_ref[...], ba_ref[...],
                                                              wx_ref[...], bx_ref[...], lam_v)
        h = h_ref[...]
        g = g_ref[...].astype(f32)
        dy = dy_ref[...].astype(f32)
        sg = _sigmoid(g)
        dg_ref[...] = (dy * h * (sg * (1.0 + g * (1.0 - sg)))).astype(bf16)
        _linear_scan(_shift_up(a, 1), dy * (g * sg), a_s, b_s, edge_s, dh_s, reverse=True)
        dh_total = dh_s[...]
        da = dh_total * _shift_down(h, 1)
        dmult = dh_total * (i * u)
        db = dh_total * mult
        di = db * u
        du = db * i
        dlog_a_c = ((-LRU_C) * a) * (da - dmult * (a * inv_mult))
        dr = dlog_a_c * sp
        dsp = jnp.sum(dlog_a_c * r, axis=0, keepdims=True)
        dpre_r = dr * r * (1.0 - r)
        dpre_i = di * i * (1.0 - i)
        dpre_rb = dpre_r.astype(bf16)
        dpre_ib = dpre_i.astype(bf16)
        du = du + _dot(dpre_rb, wa_ref[...].astype(bf16), NT) + _dot(dpre_ib, wx_ref[...].astype(bf16), NT)
        gwa_ref[...] += _dot(ub, dpre_rb, TN)
        gwx_ref[...] += _dot(ub, dpre_ib, TN)
        vec_ref[0:1, :] += jnp.sum(du, axis=0, keepdims=True)
        vec_ref[1:2, :] += jnp.sum(dpre_r, axis=0, keepdims=True)
        vec_ref[2:3, :] += jnp.sum(dpre_i, axis=0, keepdims=True)
        vec_ref[3:4, :] += dsp * (-_sigmoid(-lam_v))
        dx0 = cw[3:4, :] * du
        for k in range(3):
            dx0 = dx0 + cw[k:k + 1, :] * _shift_up(du, 3 - k)
        for k in range(4):
            gcw_ref[k:k + 1, :] += jnp.sum(du * taps[k], axis=0, keepdims=True)
        du0_ref[...] = dx0.astype(bf16)

    wacc = pl.BlockSpec((RB, RB), lambda n, b: (0, n))
    vacc = pl.BlockSpec((8, RB), lambda n, b: (0, n))
    cacc = pl.BlockSpec((8, RB), lambda n, b: (n, 0))
    return _pcall(
        body, name="lru_backward", grid=(RNN_BLOCKS, nb),
        in_specs=[col(0), col(8), tokblk, tokblk, cwblk, vec, wblk, vec, wblk, vec, vec],
        out_specs=(tokblk, tokblk, wacc, wacc, vacc, cacc),
        out_shape=(_sds((T, D), bf16), _sds((T, D), bf16), _sds((RB, D), f32), _sds((RB, D), f32),
                   _sds((8, D), f32), _sds((8 * RNN_BLOCKS, RB), f32)),
        scratch_shapes=[pltpu.VMEM((S, RB), f32)] * 3 + [pltpu.VMEM((S // 8, RB), f32)],
        compiler_params=_params(("arbitrary", "arbitrary")),
    )(proj, proj, h_all, dy_rnn, cw_full, conv_b, w_a, b_a, w_x, b_x, lam)


def _section_of_chunk(s):
    out = []
    for start, n in zip(SEC_START, SEC_CHUNKS):
        inside = (s >= start) & (s < start + n)
        out.append((inside, jnp.clip(s - start, 0, n - 1)))
    return out


EFFECT = pltpu.SideEffectType.DATAFLOW_SIDE_EFFECTING
HBM_SPEC = pl.BlockSpec(memory_space=pltpu.HBM)
SEM_SPEC = pl.BlockSpec(memory_space=pltpu.SEMAPHORE)


def _split_exchange_copies(src_ref, land_ref, send_sems, recv_sems):
    x, y, c = _my_place()
    copies = []
    for k in (3, 1, 2):
        px, py = (x + (k >> 1)) % 2, (y + (k & 1)) % 2
        copies.append(pltpu.make_async_remote_copy(
            src_ref=src_ref.at[2 * px + py], dst_ref=land_ref.at[k - 1], send_sem=send_sems[k - 1],
            recv_sem=recv_sems[k - 1], device_id=(px, py, c), device_id_type=MESH))
    return copies


def _exchange_start(chip_sum):
    _, r, cols = chip_sum.shape

    def body(src_ref, land_ref, s0, s1, s2, r0, r1, r2, src_thru, land_thru, token):
        for cp in _split_exchange_copies(src_ref, land_ref, (s0, s1, s2), (r0, r1, r2)):
            cp.start()
        token[...] = jnp.zeros_like(token)

    land = pltpu.with_memory_space_constraint(lax.empty((3, r, cols), chip_sum.dtype), pltpu.HBM)
    res = _pcall(
        body, name="exchange_start",
        out_shape=tuple([pltpu.SemaphoreType.DMA(())] * 6) + (
            pltpu.HBM(chip_sum.shape, chip_sum.dtype), pltpu.HBM((3, r, cols), chip_sum.dtype), _sds((8, 128), f32)),
        in_specs=(HBM_SPEC, HBM_SPEC), out_specs=tuple([SEM_SPEC] * 6) + (HBM_SPEC, HBM_SPEC, VMEM_SPEC),
        input_output_aliases={0: 6, 1: 7},
        compiler_params=pltpu.CompilerParams(has_side_effects=EFFECT),
    )(pltpu.with_memory_space_constraint(chip_sum, pltpu.HBM), land)
    return res[:6], res[6], res[7], res[8]


def _exchange_wait(sems, src_thru, land_thru, after):
    def body(src_ref, land_ref, s0, s1, s2, r0, r1, r2, after_ref, src_dead, got_ref):
        for cp in _split_exchange_copies(src_ref, land_ref, (s0, s1, s2), (r0, r1, r2)):
            cp.wait_send()
            cp.wait_recv()

    return _pcall(
        body, name="exchange_wait",
        out_shape=(pltpu.HBM(src_thru.shape, src_thru.dtype), pltpu.HBM(land_thru.shape, land_thru.dtype)),
        in_specs=(HBM_SPEC, HBM_SPEC) + tuple([SEM_SPEC] * 6) + (ANY,), out_specs=(HBM_SPEC, HBM_SPEC),
        input_output_aliases={0: 0, 1: 1},
        compiler_params=pltpu.CompilerParams(has_side_effects=EFFECT),
    )(src_thru, land_thru, *sems, after)[1]


def _input_grad(dsecs, wt_full, x2d, dx2, norm_g):
    T = x2d.shape[0]
    tb = min(T, 512)
    nsec = len(dsecs)
    ntok = T // tb

    def body(*refs):
        secs = refs[:nsec]
        wt_ref, x_ref, dx2_ref, g_ref, dx_ref, gnorm_ref = refs[nsec:]
        i = pl.program_id(0)

        @pl.when(i == 0)
        def _():
            gnorm_ref[...] = jnp.zeros_like(gnorm_ref)

        dh = None
        for a, (start, n) in enumerate(zip(SEC_START, SEC_CHUNKS)):
            part = _dot(secs[a][...], wt_ref[CH * start:CH * (start + n), :], NN)
            dh = part if dh is None else dh + part
        xv = x_ref[...]
        rstd = lax.rsqrt(jnp.mean(xv * xv, axis=-1, keepdims=True) + EPS)
        xh = xv * rstd
        gnorm_ref[0:1, :] += jnp.sum(dh * xh, axis=0, keepdims=True)
        dxn = dh * g_ref[...]
        dx_ref[...] = dx2_ref[...] + rstd * (dxn - xh * jnp.mean(dxn * xh, axis=-1, keepdims=True))

    tok = pl.BlockSpec((tb, D), lambda i: (i, 0))
    return _pcall(
        body, name="input_grad", grid=(ntok,),
        in_specs=[pl.BlockSpec((tb, sec.shape[1]), lambda i: (i, 0)) for sec in dsecs]
        + [pl.BlockSpec((D_IN, D), lambda i: (0, 0), pipeline_mode=pl.Buffered(1)), tok, tok,
           pl.BlockSpec((1, D), lambda i: (0, 0))],
        out_specs=(tok, pl.BlockSpec((8, D), lambda i: (0, 0))),
        out_shape=(_sds((T, D), f32), _sds((8, D), f32)),
        compiler_params=_params(("arbitrary",)),
    )(*dsecs, wt_full, x2d, dx2, norm_g)


def _w_in_grad(dsecs, h_bf):
    T = h_bf.shape[0]
    tk = min(T, 2048)
    nchunks = D_IN // CH
    nsec = len(dsecs)
    nt = T // tk

    def body(*refs):
        secs = refs[:nsec]
        h_ref, out_ref, acc = refs[nsec:]
        s, t = pl.program_id(0), pl.program_id(1)

        @pl.when(t == 0)
        def _():
            acc[...] = jnp.zeros_like(acc)

        h_rows = h_ref[pl.ds(pl.multiple_of(t * tk, tk), tk), :]
        for a, (start, n) in enumerate(zip(SEC_START, SEC_CHUNKS)):
            @pl.when((s >= start) & (s < start + n))
            def _(a=a):
                acc[...] += _dot(secs[a][...], h_rows, TN)

        @pl.when(t == nt - 1)
        def _():
            out_ref[...] = acc[...].astype(bf16)

    def sec_spec(a):
        def index(s, t, a=a):
            inside, local = _section_of_chunk(s)[a]
            return (jnp.where(inside, t, 0), local)
        return pl.BlockSpec((tk, CH), index)

    return _pcall(
        body, name="w_in_grad", grid=(nchunks, T // tk),
        in_specs=[sec_spec(a) for a in range(nsec)]
        + [pl.BlockSpec((T, D), lambda s, t: (0, 0), pipeline_mode=pl.Buffered(1))],
        out_specs=pl.BlockSpec((CH, D), lambda s, t: (s, 0)), out_shape=_sds((D_IN, D), bf16),
        scratch_shapes=[pltpu.VMEM((CH, D), f32)],
        compiler_params=_params(("arbitrary", "arbitrary")),
    )(*dsecs, h_bf)


SMALL_NAMES = ("lru_w_a", "lru_w_x", "conv_b", "lru_b_a", "lru_b_x", "lru_lambda", "norm_g", "final_norm_g",
               "attn_sinks", "conv_w")
MISC_ROW = {"conv_b": 0, "lru_b_a": 1, "lru_b_x": 2, "lru_lambda": 3, "norm_g": 8, "final_norm_g": 16,
            "attn_sinks": 24, "loss": 32}


def _small_step(gwa, gwx, gvec, gnorm_blk, gfin_blk, dsink_blk, loss_blk, gcw, params):
    srcs_rows = (RB // NDEV, RB // NDEV, 8, 8)
    flat = [t for n in SMALL_NAMES for t in params[n]]
    nout = 4 * len(SMALL_NAMES) + 1

    ra_, rx_, rm_, rc_ = srcs_rows
    rh, rf = ra_ + rx_, rm_ + rc_

    def reduce_body(gwa_ref, gwx_ref, gvec_ref, gnorm_ref, gfin_ref, dsink_ref, loss_ref, gcw_ref,
                    all_a, all_x, all_m, conv_out,
                    misc, out_h, out_f, in_h, in_f, mine_h, mine_f, every_h, every_f, sa, ra, sb, rb):
        x, y, c = _my_place()
        me = 4 * x + 2 * y + c

        misc[...] = jnp.zeros_like(misc)
        misc[0:8, :] = gvec_ref[...]
        misc[8:16, :] = gnorm_ref[...]
        misc[16:24, :] = gfin_ref[...]
        misc[24:32, 0:128] = dsink_ref[...]
        misc[32:40, :] = loss_ref[...]

        out_f[...] = jnp.zeros_like(out_f)
        for d in range(NDEV):
            out_h[d, 0:ra_, :] = gwa_ref[ra_ * d:ra_ * (d + 1), :].astype(bf16)
            out_h[d, ra_:rh, :] = gwx_ref[rx_ * d:rx_ * (d + 1), :].astype(bf16)
            out_f[d, 0:rm_, :] = misc[rm_ * d:rm_ * (d + 1), :]
            out_f[d, rm_:rf, 0:RB] = gcw_ref[rc_ * d:rc_ * (d + 1), :]

        def both(k, src_h, dst_h, src_f, dst_f, send, recv, peer):
            return [pltpu.make_async_remote_copy(src_ref=s_, dst_ref=d_, send_sem=send.at[2 * (k - 1) + t],
                                                 recv_sem=recv.at[2 * (k - 1) + t], device_id=peer,
                                                 device_id_type=MESH)
                    for t, (s_, d_) in enumerate(((src_h, dst_h), (src_f, dst_f)))]

        scatter = []
        for k in range(1, NDEV):
            px, py, pc = _peer(k)
            dev = 4 * px + 2 * py + pc
            scatter += both(k, out_h.at[dev], in_h.at[k - 1], out_f.at[dev], in_f.at[k - 1], sa, ra, (px, py, pc))
        for cp in scatter:
            cp.start()
        for cp in scatter:
            cp.wait()

        total_h = out_h[me].astype(f32)
        total_f = out_f[me]
        for k in range(NDEV - 1):
            total_h = total_h + in_h[k].astype(f32)
            total_f = total_f + in_f[k]
        conv_out[...] = total_f[rm_:rf, 0:RB]
        mine_h[...] = total_h.astype(bf16)
        mine_f[...] = total_f[0:rm_, :]
        every_h[me] = total_h.astype(bf16)
        every_f[me] = total_f[0:rm_, :]
        gather = []
        for k in range(1, NDEV):
            gather += both(k, mine_h, every_h.at[me], mine_f, every_f.at[me], sb, rb, _peer(k))
        for cp in gather:
            cp.start()
        for cp in gather:
            cp.wait()
        for d in range(NDEV):
            all_a[ra_ * d:ra_ * (d + 1), :] = every_h[d, 0:ra_, :].astype(f32)
            all_x[rx_ * d:rx_ * (d + 1), :] = every_h[d, ra_:rh, :].astype(f32)
            all_m[rm_ * d:rm_ * (d + 1), :] = every_f[d]

    def adam_body(*refs):
        all_a, all_x, all_m, conv_ref = refs[:4]
        prm = {n: refs[4 + 3 * k:7 + 3 * k] for k, n in enumerate(SMALL_NAMES)}
        nin = 4 + len(flat)
        outs = {n: refs[nin + 4 * k:nin + 4 * k + 4] for k, n in enumerate(SMALL_NAMES)}
        loss_out = refs[nin + nout - 1]
        g_conv = conv_ref[0:4, :]

        def update(name, g, pick=lambda r: r[...]):
            w_ref, m_ref, v_ref = prm[name]
            delta, m_new, v_new = _adam_math(g, pick(w_ref), pick(m_ref), pick(v_ref))
            return g, delta, m_new, v_new

        for n in range(RNN_BLOCKS):
            lanes = slice(RB * n, RB * (n + 1))
            for name, full in (("lru_w_a", all_a), ("lru_w_x", all_x)):
                for out, val in zip(outs[name], update(name, full[:, lanes], pick=lambda r, n=n: r[n])):
                    out[n] = val
        for name in ("conv_b", "lru_b_a", "lru_b_x", "lru_lambda", "norm_g", "final_norm_g"):
            row = MISC_ROW[name]
            for out, val in zip(outs[name], update(name, all_m[row:row + 1, :])):
                out[...] = val
        row = MISC_ROW["attn_sinks"]
        for out, val in zip(outs["attn_sinks"], update("attn_sinks", all_m[row:row + 1, 0:16])):
            out[...] = val
        for out, val in zip(outs["conv_w"], update("conv_w", g_conv)):
            out[...] = val
        row = MISC_ROW["loss"]
        loss_out[...] = all_m[row:row + 8, 0:128] * (0.5 / D)

    scratch = [pltpu.VMEM((64, D), f32), pltpu.VMEM((NDEV, rh, D), bf16), pltpu.VMEM((NDEV, rf, D), f32),
               pltpu.VMEM((NDEV - 1, rh, D), bf16), pltpu.VMEM((NDEV - 1, rf, D), f32),
               pltpu.VMEM((rh, D), bf16), pltpu.VMEM((rm_, D), f32),
               pltpu.VMEM((NDEV, rh, D), bf16), pltpu.VMEM((NDEV, rm_, D), f32)
               ] + [pltpu.SemaphoreType.DMA((2 * (NDEV - 1),))] * 4
    sums = _pcall(
        reduce_body, name="small_reduce",
        out_shape=(_sds((RB, D), f32), _sds((RB, D), f32), _sds((64, D), f32), _sds((8, RB), f32)),
        in_specs=[VMEM_SPEC] * 8, out_specs=tuple([VMEM_SPEC] * 4),
        scratch_shapes=scratch, compiler_params=_params(),
    )(gwa, gwx, gvec, gnorm_blk, gfin_blk, dsink_blk, loss_blk, gcw)
    out_shape = tuple(_sds(params[n][0].shape, f32) for n in SMALL_NAMES for _ in range(4)) + (_sds((8, 128), f32),)
    res = _pcall(
        adam_body, name="small_adamw", out_shape=out_shape,
        in_specs=[VMEM_SPEC] * (4 + len(flat)), out_specs=tuple([VMEM_SPEC] * nout), compiler_params=_params(),
    )(*sums, *flat)
    return {n: res[4 * k:4 * k + 4] for k, n in enumerate(SMALL_NAMES)}, res[-1]


def _pad_rows(v, rows=8):
    return jnp.concatenate([v, jnp.zeros((rows - v.shape[0], v.shape[1]), v.dtype)], axis=0)


def kernel(x, norm_g, w_in, conv_w, conv_b, lru_w_a, lru_b_a, lru_w_x, lru_b_x, lru_lambda, attn_sinks, w_rnn_out, w_attn_out, w_o, final_norm_g, loss_target, m_norm_g, m_w_in, m_conv_w, m_conv_b, m_lru_w_a, m_lru_b_a, m_lru_w_x, m_lru_b_x, m_lru_lambda, m_attn_sinks, m_w_rnn_out, m_w_attn_out, m_w_o, m_final_norm_g, v_norm_g, v_w_in, v_conv_w, v_conv_b, v_lru_w_a, v_lru_b_a, v_lru_w_x, v_lru_b_x, v_lru_lambda, v_attn_sinks, v_w_rnn_out, v_w_attn_out, v_w_o, v_final_norm_g):
    nb, S, _ = x.shape
    T = nb * S
    x2d = x.reshape(T, D)
    tgt = loss_target.reshape(T, D)
    fin_g = final_norm_g.reshape(1, D)
    w_a3, w_x3 = lru_w_a[0], lru_w_x[0]

    my_core = lax.axis_index("c").astype(jnp.int32).reshape(1)
    cx, cy = lax.axis_index("x"), lax.axis_index("y")
    chip_order = jnp.stack([2 * cx + cy, 2 * (1 - cx) + cy, 2 * cx + (1 - cy),
                            2 * (1 - cx) + (1 - cy)]).astype(jnp.int32)

    tabs = _rope_tables(S)
    h_bf, proj, wt_full, cw_full, _ = _in_proj_gather(
        x2d, norm_g, w_in[0].T.astype(bf16), _pad_rows(conv_w[0]), tabs, S, (), chip_order)
    y_rnn, h_all = _lru_forward(proj, cw_full, conv_b, w_a3, lru_b_a, w_x3, lru_b_x, lru_lambda, S)
    y_attn, o_attn, (wr_full, wa_full, wo_full) = _attn_forward(proj, attn_sinks, S,
                                                                (w_rnn_out[0], w_attn_out[0], w_o[0]))

    (dx2, dy_rnn, dy_attn, dmr, dma, loss_blk, gfin_blk, g_wr, g_wa, g_wo) = _merge_and_head(
        x2d, tgt, proj, y_rnn, y_attn, wr_full, wa_full, wo_full, fin_g)
    sums_out = _pair_sums([g_wr, g_wa, g_wo], bf16, my_core, "out")

    (dq, dkv, dga, dsink_blk), (p_wr, p_wa, p_wo) = _attn_backward(proj, dy_attn, o_attn, tabs, attn_sinks, S, sums_out)
    du0, dgr, gwa, gwx, gvec, gcw = _lru_backward(proj, h_all, dy_rnn, cw_full, conv_b, w_a3, lru_b_a, w_x3,
                                                  lru_b_x, lru_lambda, S)
    dsecs = (du0, dgr, dq, dkv, dga, dmr, dma)

    g_wt = _w_in_grad(dsecs, h_bf)
    (sum_in,) = _pair_sums([g_wt], bf16, my_core, "in")
    ex_sems, sum_in, landing, token = _exchange_start(sum_in)
    grad_x2d, gnorm_blk = _input_grad(dsecs, wt_full, x2d, dx2, norm_g + token[0, 0])
    p_wt = _exchange_wait(ex_sems, sum_in, landing, gnorm_blk)
    p_wt_own = lax.dynamic_index_in_dim(sum_in, 2 * cx + cy, axis=0, keepdims=False)

    small, loss_out = _small_step(gwa, gwx, gvec, gnorm_blk, gfin_blk, dsink_blk, loss_blk, gcw, {
        "lru_w_a": (w_a3, m_lru_w_a[0], v_lru_w_a[0]), "lru_w_x": (w_x3, m_lru_w_x[0], v_lru_w_x[0]),
        "conv_b": (conv_b, m_conv_b, v_conv_b), "lru_b_a": (lru_b_a, m_lru_b_a, v_lru_b_a),
        "lru_b_x": (lru_b_x, m_lru_b_x, v_lru_b_x), "lru_lambda": (lru_lambda, m_lru_lambda, v_lru_lambda),
        "norm_g": (norm_g, m_norm_g, v_norm_g),
        "final_norm_g": (fin_g, m_final_norm_g.reshape(1, D), v_final_norm_g.reshape(1, D)),
        "attn_sinks": (attn_sinks, m_attn_sinks, v_attn_sinks),
        "conv_w": (conv_w[0], m_conv_w[0], v_conv_w[0])})

    o_wt = _adamw(p_wt_own, p_wt, w_in[0].T, m_w_in[0].T, v_w_in[0].T, "adamw_w_in")
    o_wr, o_wa, o_wo = _adamw_group(
        (p_wr, p_wa, p_wo), (w_rnn_out[0], w_attn_out[0], w_o[0]),
        (m_w_rnn_out[0], m_w_attn_out[0], m_w_o[0]), (v_w_rnn_out[0], v_w_attn_out[0], v_w_o[0]), "adamw_w_out")

    def result(kind):
        d = {n: small[n][kind] for n in ("conv_b", "lru_b_a", "lru_b_x", "lru_lambda", "norm_g", "attn_sinks")}
        d.update({n: small[n][kind][None] for n in ("lru_w_a", "lru_w_x", "conv_w")})
        d["final_norm_g"] = small["final_norm_g"][kind].reshape(D)
        d.update({"w_in": o_wt[kind].T[None], "w_rnn_out": o_wr[kind][None], "w_attn_out": o_wa[kind][None],
                  "w_o": o_wo[kind][None]})
        return d

    order = ("norm_g", "w_in", "conv_w", "conv_b", "lru_w_a", "lru_b_a", "lru_w_x", "lru_b_x", "lru_lambda",
             "attn_sinks", "w_rnn_out", "w_attn_out", "w_o", "final_norm_g")
    outs = [loss_out[0, 0], grad_x2d.reshape(nb, S, D)]
    for kind in range(4):
        d = result(kind)
        outs += [d[n] for n in order]
    return tuple(outs)
```

```python
import math

import jax
import jax.numpy as jnp
from jax import lax
from jax.experimental import pallas as pl
from jax.experimental.pallas import tpu as pltpu

f32 = jnp.float32
bf16 = jnp.bfloat16

D = 1024
D_IN = 6656
NDEV = 8
RNN_BLOCKS = 8
RB = 128
HEAD = 64
KV_HEADS = 4
GROUP = 4
QB = 128
LRU_C = 8.0
EPS = 1e-6
ROPE_DIM = 16
ROPE_THETA = 500000.0
CH = 512
SEC_START = (0, 2, 4, 6, 7, 9, 11)
SEC_CHUNKS = (2, 2, 2, 1, 2, 2, 2)
VMEM_LIMIT = 62 * 1024 * 1024

ADAM_LR, ADAM_B1, ADAM_B2, ADAM_EPS, ADAM_WD, ADAM_STEP = 0.001, 0.9, 0.999, 1e-08, 0.01, 10

MESH = pl.DeviceIdType.MESH
ANY = pl.BlockSpec(memory_space=pl.ANY)
VMEM_SPEC = pl.BlockSpec(memory_space=pltpu.VMEM)
SMEM_SPEC = pl.BlockSpec(memory_space=pltpu.SMEM)


def _pcall(body, **kw):
    return pl.pallas_call(body, **kw)


def _params(sem=None, **kw):
    if sem is not None:
        kw["dimension_semantics"] = sem
    return pltpu.CompilerParams(vmem_limit_bytes=VMEM_LIMIT, **kw)


def _sds(shape, dtype):
    return jax.ShapeDtypeStruct(shape, dtype)


def _dot(a, b, dims):
    return lax.dot_general(a, b, (dims, ((), ())), preferred_element_type=f32)


NN = ((1,), (0,))
NT = ((1,), (1,))
TN = ((0,), (0,))


def _sigmoid(v):
    return 0.5 * jnp.tanh(0.5 * v) + 0.5


def _sigmoid_positive(v):
    return 1.0 / (1.0 + jnp.exp(-v))


def _my_place():
    return lax.axis_index("x"), lax.axis_index("y"), lax.axis_index("c")


def _peer(k):
    x, y, c = _my_place()
    return (x + ((k >> 2) & 1)) % 2, (y + ((k >> 1) & 1)) % 2, (c + (k & 1)) % 2


def _direct_gather_copies(srcs, outs, send_sems, recv_sems, local_sems):
    x, y, c = _my_place()
    me = 4 * x + 2 * y + c
    local, remote = [], []
    for a, (src, out) in enumerate(zip(srcs, outs)):
        r = src.shape[0]
        mine = out.at[pl.ds(pl.multiple_of(me * r, 8), r), :]
        local.append(pltpu.make_async_copy(src, mine, local_sems.at[a]))
        for k in range(1, NDEV):
            remote.append(pltpu.make_async_remote_copy(
                src_ref=src, dst_ref=mine, send_sem=send_sems.at[7 * a + k - 1], recv_sem=recv_sems.at[7 * a + k - 1],
                device_id=_peer(k), device_id_type=MESH))
    return local, remote


def _chip_exchange_copies(src, dst, send_sems, recv_sems, local_sems):
    x, y, c = _my_place()
    local, remote = [], []
    for a in range(len(src)):
        local.append(pltpu.make_async_copy(src[a].at[2 * x + y], dst[a].at[0], local_sems.at[a]))
    for k in (3, 1, 2):
        px, py = (x + (k >> 1)) % 2, (y + (k & 1)) % 2
        for a in range(len(src)):
            remote.append(pltpu.make_async_remote_copy(
                src_ref=src[a].at[2 * px + py], dst_ref=dst[a].at[k],
                send_sem=send_sems.at[3 * a + k - 1], recv_sem=recv_sems.at[3 * a + k - 1],
                device_id=(px, py, c), device_id_type=MESH))
    return local, remote


def _exchange_scratch(narr, per_array):
    return [pltpu.SemaphoreType.DMA((per_array * narr,)), pltpu.SemaphoreType.DMA((per_array * narr,)),
            pltpu.SemaphoreType.DMA((narr,))]


def _start_all(copies):
    local, remote = copies
    for cp in local + remote:
        cp.start()


def _wait_all(copies):
    local, remote = copies
    for cp in remote + local:
        cp.wait()


def _row_tile(rows, dtype):
    unit = 16 if dtype == bf16 else 8
    for cand in (256, 208, 128, 64, 40, 32, 16, 8):
        if rows % cand == 0 and cand % unit == 0:
            return cand
    return rows


def _pair_sums(grads, wire_dtype, my_core, tag):
    narr = len(grads)
    r, cols = grads[0].shape[0] // NDEV, grads[0].shape[1]
    views = [g.reshape(4, 2, r, cols) for g in grads]
    tr = _row_tile(r, wire_dtype)
    nt = r // tr

    def body(core_ref, *refs):
        mine = refs[:narr]
        whole = refs[narr:2 * narr]
        outs = refs[2 * narr:3 * narr]
        got = refs[3 * narr:4 * narr]
        send_sems, recv_sems = refs[4 * narr:]
        q, i = pl.program_id(0), pl.program_id(1)
        x, y, c = _my_place()

        def copy(a, chip):
            return pltpu.make_async_remote_copy(
                src_ref=whole[a].at[chip, 1 - c], dst_ref=got[a].at[chip],
                send_sem=send_sems.at[4 * a + chip], recv_sem=recv_sems.at[4 * a + chip],
                device_id=(x, y, 1 - c), device_id_type=MESH)

        @pl.when((q == 0) & (i == 0))
        def _():
            for chip in range(4):
                for a in range(narr):
                    copy(a, chip).start()

        for chip in range(4):
            @pl.when((q == chip) & (i == 0))
            def _(chip=chip):
                for a in range(narr):
                    copy(a, chip).wait_recv()

        rows = pl.ds(pl.multiple_of(i * tr, tr), tr)
        for a in range(narr):
            outs[a][...] = (mine[a][...].astype(f32) + got[a][q, rows, :].astype(f32)).astype(wire_dtype)

        @pl.when((q == 3) & (i == nt - 1))
        def _():
            for chip in range(4):
                for a in range(narr):
                    copy(a, chip).wait_send()

    slab = pl.BlockSpec((None, tr, cols), lambda q, i, core: (q, i, 0))
    grid_spec = pltpu.PrefetchScalarGridSpec(
        num_scalar_prefetch=1, grid=(4, nt),
        in_specs=[pl.BlockSpec((None, None, tr, cols), lambda q, i, core: (q, core[0], i, 0))] * narr + [ANY] * narr,
        out_specs=tuple([slab] * narr),
        scratch_shapes=[pltpu.VMEM((4, r, cols), grads[0].dtype)] * narr
        + [pltpu.SemaphoreType.DMA((4 * narr,)), pltpu.SemaphoreType.DMA((4 * narr,))])
    return _pcall(body, name="pair_sums_" + tag, grid_spec=grid_spec,
                  out_shape=tuple(_sds((4, r, cols), wire_dtype) for _ in range(narr)),
                  compiler_params=_params(("arbitrary", "arbitrary")))(my_core, *views, *views)


def _adam_math(g, w, m, v):
    m_new = ADAM_B1 * m + (1.0 - ADAM_B1) * g
    v_new = ADAM_B2 * v + (1.0 - ADAM_B2) * (g * g)
    m_hat = m_new / (1.0 - ADAM_B1 ** ADAM_STEP)
    v_hat = v_new / (1.0 - ADAM_B2 ** ADAM_STEP)
    return -ADAM_LR * (m_hat / (jnp.sqrt(v_hat) + ADAM_EPS) + ADAM_WD * w), m_new, v_new


def _adamw(first, parts, w, m, v, name):
    n, rows, cols = parts.shape
    tr = _row_tile(rows, parts.dtype)

    def body(f_ref, p_ref, w_ref, m_ref, v_ref, g_out, d_out, m_out, v_out):
        g = f_ref[...].astype(f32)
        for s in range(n):
            g = g + p_ref[s].astype(f32)
        g_out[...] = g
        d_out[...], m_out[...], v_out[...] = _adam_math(g, w_ref[...], m_ref[...], v_ref[...])

    blk = pl.BlockSpec((tr, cols), lambda i: (i, 0))
    return _pcall(
        body, name=name, grid=(rows // tr,),
        in_specs=[blk, pl.BlockSpec((n, tr, cols), lambda i: (0, i, 0)), blk, blk, blk],
        out_specs=(blk, blk, blk, blk), out_shape=tuple(_sds((rows, cols), f32) for _ in range(4)),
        compiler_params=_params(("arbitrary",)),
    )(first, parts, w, m, v)


def _adamw_group(parts, ws, ms, vs, name):
    nw = len(ws)

    def body(*refs):
        p_refs, w_refs, m_refs, v_refs = (refs[k * nw:(k + 1) * nw] for k in range(4))
        outs = refs[4 * nw:]
        for k in range(nw):
            g = p_refs[k][0].astype(f32)
            for s in range(1, p_refs[k].shape[0]):
                g = g + p_refs[k][s].astype(f32)
            g_out, d_out, m_out, v_out = outs[4 * k:4 * k + 4]
            g_out[...] = g
            d_out[...], m_out[...], v_out[...] = _adam_math(g, w_refs[k][...], m_refs[k][...], v_refs[k][...])

    res = _pcall(
        body, name=name, out_shape=tuple(_sds(w.shape, f32) for w in ws for _ in range(4)),
        in_specs=[VMEM_SPEC] * (4 * nw), out_specs=tuple([VMEM_SPEC] * (4 * nw)), compiler_params=_params(),
    )(*parts, *ws, *ms, *vs)
    return [res[4 * k:4 * k + 4] for k in range(nw)]


def _rope(t, c, s1, s2):
    w = t.shape[1]
    return t * c + pltpu.roll(t, w - 8, 1) * s1 + pltpu.roll(t, 8, 1) * s2


def _rope_transposed(dt, c, s1, s2):
    w = dt.shape[1]
    return dt * c + pltpu.roll(dt * s1, 8, 1) + pltpu.roll(dt * s2, w - 8, 1)


PAIR_ROWS = D_IN // 4
SUB_COLS = ((0, 512), (512, 512), (1024, 512), (1536, 128))
Q_SLABS = range(3, 11)
K_SLABS = range(11, 13)


def _in_proj_gather(x2d, norm_g, wt_shard, cw_shard, tabs, S, out_shards, chip_order):
    T = x2d.shape[0]
    tb = min(S, 1024)
    ntok = T // tb
    nsb = S // tb
    q_scale = 1.0 / math.sqrt(HEAD)
    shard_rows = wt_shard.shape[0]
    small = (cw_shard,) + tuple(out_shards)
    nsm = len(small)

    def body(order_ref, x_ref, g_ref, c_ref, s1_ref, s2_ref, wt_hbm, *rest):
        small_in = rest[:nsm]
        h_ref, proj_ref, wt_out = rest[nsm:nsm + 3]
        small_out = rest[nsm + 3:2 * nsm + 3]
        wt_vm, h_vm = rest[2 * nsm + 3:2 * nsm + 5]
        stage = rest[2 * nsm + 5:3 * nsm + 4]
        wsend, wrecv, wlocal = rest[3 * nsm + 4:3 * nsm + 7]
        dsems = rest[3 * nsm + 7:]
        jj, i = pl.program_id(0), pl.program_id(1)
        x, y, c = _my_place()
        me, sibling = (x, y, c), (x, y, 1 - c)
        chips = [(1 - x, y), (x, 1 - y), (1 - x, 1 - y)]

        def rows(place):
            px, py, pc = place
            return wt_vm.at[pl.ds(pl.multiple_of((4 * px + 2 * py + pc) * shard_rows, 16), shard_rows), :]

        def copy(k, block, to, src=None):
            return pltpu.make_async_remote_copy(
                src_ref=rows(block) if src is None else src, dst_ref=rows(block),
                send_sem=wsend.at[k], recv_sem=wrecv.at[k], device_id=to, device_id_type=MESH)

        def small_copies():
            srcs = (small_in[0],) + tuple(stage)
            return _direct_gather_copies(srcs, small_out, *dsems)

        own = pltpu.make_async_copy(wt_hbm, rows(me), wlocal.at[0])
        keep = pltpu.make_async_copy(wt_vm, wt_out, wlocal.at[1])

        @pl.when((jj == 0) & (i == 0))
        def _():
            own.start()
            copy(0, me, sibling, src=wt_hbm).start()
            for j, chip in enumerate(chips):
                copy(1 + j, me, (*chip, c), src=wt_hbm).start()
            for a in range(nsm - 1):
                stage[a][...] = small_in[1 + a][...].astype(bf16)
            _start_all(small_copies())
            own.wait()
            copy(0, sibling, me).wait_recv()

        for j, chip in enumerate(chips):
            @pl.when((jj == 1 + j) & (i == 0))
            def _(j=j, chip=chip):
                copy(1 + j, (*chip, c), me).wait_recv()
                copy(4 + j, (*chip, c), sibling).start()
                copy(4 + j, (*chip, 1 - c), me).wait_recv()

        @pl.when((jj == 3) & (i == 0))
        def _():
            keep.start()

        @pl.when((jj == 3) & (i == ntok - 1))
        def _():
            copy(0, me, sibling, src=wt_hbm).wait_send()
            for j, chip in enumerate(chips):
                copy(1 + j, me, (*chip, c), src=wt_hbm).wait_send()
                copy(4 + j, (*chip, c), sibling).wait_send()
            _wait_all(small_copies())
            keep.wait()

        tok = pl.ds(pl.multiple_of(i * tb, tb), tb)

        @pl.when(jj == 0)
        def _():
            xv = x_ref[...]
            ms = jnp.mean(xv * xv, axis=-1, keepdims=True)
            hb = (xv * lax.rsqrt(ms + EPS) * g_ref[...]).astype(bf16)
            h_ref[...] = hb
            h_vm[tok, :] = hb

        block = order_ref[jj]
        hb = h_vm[tok, :]

        def piece(c0, w):
            w_rows = wt_vm[pl.ds(pl.multiple_of(block * PAIR_ROWS + c0, 128), w), :]
            return _dot(hb, w_rows, NT)

        @pl.when(block != 1)
        def _():
            for c0, w in SUB_COLS:
                proj_ref[:, c0:c0 + w] = piece(c0, w).astype(bf16)

        @pl.when(block == 1)
        def _():
            tab = (c_ref[...], s1_ref[...], s2_ref[...])
            for c0, w in SUB_COLS:
                acc = piece(c0, w)
                for l in range(w // 128):
                    slab = (c0 + 128 * l) // 128
                    part = acc[:, 128 * l:128 * (l + 1)]
                    if slab in Q_SLABS:
                        part = _rope(part, *tab) * q_scale
                    elif slab in K_SLABS:
                        part = _rope(part, *tab)
                    proj_ref[:, 128 * slab:128 * (slab + 1)] = part.astype(bf16)

    first_pass = lambda jj, i, order: (jnp.where(jj == 0, i, ntok - 1), 0)
    const = lambda jj, i, order: (0, 0)
    tab = pl.BlockSpec((tb, 128), lambda jj, i, order: (jnp.where(order[jj] == 1, i % nsb, 0), 0))
    grid_spec = pltpu.PrefetchScalarGridSpec(
        num_scalar_prefetch=1, grid=(4, ntok),
        in_specs=[pl.BlockSpec((tb, D), first_pass), pl.BlockSpec((1, D), const), tab, tab, tab, ANY]
        + [pl.BlockSpec(w.shape, const) for w in small],
        out_specs=(pl.BlockSpec((tb, D), first_pass),
                   pl.BlockSpec((tb, PAIR_ROWS), lambda jj, i, order: (i, order[jj])), ANY) + tuple([ANY] * nsm),
        scratch_shapes=[pltpu.VMEM((D_IN, D), bf16), pltpu.VMEM((T, D), bf16)]
        + [pltpu.VMEM(w.shape, bf16) for w in out_shards]
        + [pltpu.SemaphoreType.DMA((7,)), pltpu.SemaphoreType.DMA((7,)), pltpu.SemaphoreType.DMA((2,))]
        + _exchange_scratch(nsm, 7))
    res = _pcall(
        body, name="in_proj", grid_spec=grid_spec,
        out_shape=(_sds((T, D), bf16), _sds((T, D_IN), bf16), _sds((D_IN, D), bf16),
                   _sds((NDEV * cw_shard.shape[0], cw_shard.shape[1]), f32))
        + tuple(_sds((NDEV * w.shape[0], w.shape[1]), bf16) for w in out_shards),
        compiler_params=_params(("arbitrary", "arbitrary")),
    )(chip_order, x2d, norm_g, *tabs, wt_shard, *small)
    return res[0], res[1], res[2], res[3], res[4:]


def _rows_iota(shape):
    return lax.broadcasted_iota(jnp.int32, shape, 0)


def _shift_down(v, k):
    return jnp.where(_rows_iota(v.shape) >= k, pltpu.roll(v, k, 0), 0.0)


def _shift_up(v, k):
    n = v.shape[0]
    return jnp.where(_rows_iota(v.shape) < n - k, pltpu.roll(v, n - k, 0), 0.0)


def _linear_scan(a, b, a_s, b_s, edge_s, out_ref, reverse):
    n = a.shape[0]
    ng = n // 8
    a3, b3 = a.reshape(ng, 8, RB), b.reshape(ng, 8, RB)
    rid = lax.broadcasted_iota(jnp.int32, a3.shape, 1)
    for s in (1, 2, 4):
        keep, shift = (rid < 8 - s, 8 - s) if reverse else (rid >= s, s)
        b3 = jnp.where(keep, a3 * pltpu.roll(b3, shift, 1) + b3, b3)
        a3 = jnp.where(keep, a3 * pltpu.roll(a3, shift, 1), a3)
    a_s[...] = a3.reshape(n, RB)
    b_s[...] = b3.reshape(n, RB)
    edge = 0 if reverse else 7
    ea, eb = a_s[pl.ds(edge, ng, stride=8), :], b_s[pl.ds(edge, ng, stride=8), :]
    r = _rows_iota(ea.shape)
    s = 1
    while s < ng:
        keep, shift = (r < ng - s, ng - s) if reverse else (r >= s, s)
        eb = jnp.where(keep, ea * pltpu.roll(eb, shift, 0) + eb, eb)
        if 2 * s < ng:
            ea = jnp.where(keep, ea * pltpu.roll(ea, shift, 0), ea)
        s *= 2
    edge_s[...] = _shift_up(eb, 1) if reverse else _shift_down(eb, 1)

    def eight_groups(i, carry):
        for k in range(8):
            j = i * 8 + k
            rows = pl.ds(pl.multiple_of(j * 8, 8), 8)
            out_ref[rows, :] = b_s[rows, :] + a_s[rows, :] * edge_s[pl.ds(j, 1), :]
        return carry

    lax.fori_loop(0, ng // 8, eight_groups, 0)


def _neg_expm1(v):
    series = -v * (1.0 + v * (0.5 + v * (1.0 / 6.0)))
    return jnp.where(v > -0.015625, series, 1.0 - jnp.exp(v))


def _softplus_neg(lam):
    return jnp.maximum(-lam, 0.0) + jnp.log(1.0 + jnp.exp(-jnp.abs(lam)))


def _lru_gates(x0, cw, cb, wa, ba, wx, bx, lam):
    taps = [_shift_down(x0, 3 - k) for k in range(3)] + [x0]
    u = cb + cw[3:4, :] * x0
    for k in range(3):
        u = u + cw[k:k + 1, :] * taps[k]
    ub = u.astype(bf16)
    r = _sigmoid_positive(_dot(ub, wa.astype(bf16), NN) + ba)
    i = _sigmoid(_dot(ub, wx.astype(bf16), NN) + bx)
    sp = _softplus_neg(lam)
    log_a = (-LRU_C) * r * sp
    a = jnp.exp(log_a)
    w = _neg_expm1(2.0 * log_a)
    inv_mult = lax.rsqrt(w)
    return u, ub, r, i, sp, a, w * inv_mult, inv_mult, taps


def _lru_specs(S, nb):
    col = lambda off: pl.BlockSpec((S, RB), lambda n, b, off=off: (b, off + n))
    vec = pl.BlockSpec((1, RB), lambda n, b: (0, n))
    wblk = pl.BlockSpec((None, RB, RB), lambda n, b: (n, 0, 0))
    cwblk = pl.BlockSpec((8, RB), lambda n, b: (n, 0))
    return col, vec, wblk, cwblk


def _lru_forward(proj, cw_full, conv_b, w_a, b_a, w_x, b_x, lam, S):
    T = proj.shape[0]
    nb = T // S
    col, vec, wblk, cwblk = _lru_specs(S, nb)

    def body(x0_ref, g_ref, cw_ref, cb_ref, wa_ref, ba_ref, wx_ref, bx_ref, lam_ref, y_ref, h_ref, a_s, b_s, edge_s):
        x0 = x0_ref[...].astype(f32)
        u, ub, r, i, sp, a, mult, _, _ = _lru_gates(x0, cw_ref[...], cb_ref[...], wa_ref[...], ba_ref[...],
                                                    wx_ref[...], bx_ref[...], lam_ref[...])
        _linear_scan(a, mult * (i * u), a_s, b_s, edge_s, h_ref, reverse=False)
        g = g_ref[...].astype(f32)
        y_ref[...] = (h_ref[...] * (g * _sigmoid(g))).astype(bf16)

    out = pl.BlockSpec((S, RB), lambda n, b: (b, n))
    return _pcall(
        body, name="lru_forward", grid=(RNN_BLOCKS, nb),
        in_specs=[col(0), col(8), cwblk, vec, wblk, vec, wblk, vec, vec],
        out_specs=(out, out), out_shape=(_sds((T, D), bf16), _sds((T, D), f32)),
        scratch_shapes=[pltpu.VMEM((S, RB), f32), pltpu.VMEM((S, RB), f32), pltpu.VMEM((S // 8, RB), f32)],
        compiler_params=_params(("arbitrary", "arbitrary")),
    )(proj, proj, cw_full, conv_b, w_a, b_a, w_x, b_x, lam)


def _rope_tables(S):
    pos = jnp.arange(S, dtype=f32)
    inv_freq = ROPE_THETA ** (-jnp.arange(0, ROPE_DIM, 2, dtype=f32) / ROPE_DIM)
    ang = pos[:, None] * inv_freq[None, :]
    cos, sin = jnp.cos(ang), jnp.sin(ang)
    lane = jnp.arange(128) % HEAD
    cosl, sinl = cos[:, lane % 8], sin[:, lane % 8]
    c = jnp.where(lane[None, :] < ROPE_DIM, cosl, 1.0)
    s1 = jnp.where(lane[None, :] < 8, -sinl, 0.0)
    s2 = jnp.where((lane[None, :] >= 8) & (lane[None, :] < ROPE_DIM), sinl, 0.0)
    return c.astype(f32), s1.astype(f32), s2.astype(f32)


def _heads_to_rows(t):
    return jnp.concatenate([t[:, HEAD * h:HEAD * (h + 1)] for h in range(GROUP)], axis=0)


def _rows_to_heads(t):
    return jnp.concatenate([t[QB * h:QB * (h + 1), :] for h in range(GROUP)], axis=1)


def _window_bias(first_block):
    shape = (GROUP * QB, 2 * QB)
    qi = _rows_iota(shape) % QB
    cj = lax.broadcasted_iota(jnp.int32, shape, 1)
    valid = (cj > qi) & (cj <= qi + QB) & ((cj >= QB) | jnp.logical_not(first_block))
    return jnp.where(valid, 0.0, -jnp.inf)


def _attn_probs(q_rows, k_cat, sink_col, bias):
    s = _dot(q_rows, k_cat, NT) + bias
    m = jnp.maximum(jnp.max(s, axis=1, keepdims=True), sink_col)
    p = jnp.exp(s - m)
    e_sink = jnp.exp(sink_col - m)
    inv = 1.0 / (jnp.sum(p, axis=1, keepdims=True) + e_sink)
    return p * inv, e_sink * inv


def _sink_column(sink_ref, kv):
    rid = _rows_iota((GROUP * QB, 1))
    col = jnp.zeros((GROUP * QB, 1), f32)
    for h in range(GROUP):
        col = jnp.where(rid // QB == h, sink_ref[0, GROUP * kv + h], col)
    return col


def _attn_in_specs(S):
    nq = S // QB
    last = nq - 1
    cur = lambda b, j: b * nq + jnp.minimum(j, last)
    prev = lambda b, j: b * nq + jnp.maximum(jnp.minimum(j, last) - 1, 0)
    specs = [
        pl.BlockSpec((QB, D), lambda b, j: (cur(b, j), 2)),
        pl.BlockSpec((QB, 256), lambda b, j: (cur(b, j), 12)),
        pl.BlockSpec((QB, 256), lambda b, j: (prev(b, j), 12)),
        pl.BlockSpec((QB, 256), lambda b, j: (cur(b, j), 13)),
        pl.BlockSpec((QB, 256), lambda b, j: (prev(b, j), 13)),
        pl.BlockSpec((QB, 512), lambda b, j: (cur(b, j), 7)),
        pl.BlockSpec((QB, 512), lambda b, j: (cur(b, j), 8)),
        SMEM_SPEC,
    ]
    return specs, cur, prev


def _attn_forward(proj, sinks, S, out_shards):
    T = proj.shape[0]
    nb, nq = T // S, S // QB
    specs, cur, _ = _attn_in_specs(S)
    nw = len(out_shards)

    def body(q_ref, kc_ref, kp_ref, vc_ref, vp_ref, gl_ref, gh_ref, sink_ref, *rest):
        shards = rest[:nw]
        y_ref, o_ref = rest[nw:nw + 2]
        gathered = rest[nw + 2:2 * nw + 2]
        stage = rest[2 * nw + 2:3 * nw + 2]
        sems = rest[3 * nw + 2:]
        b, j = pl.program_id(0), pl.program_id(1)

        @pl.when((b == 0) & (j == 0))
        def _():
            for a in range(nw):
                stage[a][...] = shards[a][...].astype(bf16)
            _start_all(_direct_gather_copies(stage, gathered, *sems))

        @pl.when((b == nb - 1) & (j == nq - 1))
        def _():
            _wait_all(_direct_gather_copies(stage, gathered, *sems))

        bias = _window_bias(j == 0)
        kc, kp, vc, vp = kc_ref[...], kp_ref[...], vc_ref[...], vp_ref[...]
        for kv in range(KV_HEADS):
            lanes = slice(256 * kv, 256 * (kv + 1))
            hl = slice(HEAD * kv, HEAD * (kv + 1))
            q_rows = _heads_to_rows(q_ref[:, lanes])
            k_cat = jnp.concatenate([kp[:, hl], kc[:, hl]], axis=0)
            v_cat = jnp.concatenate([vp[:, hl], vc[:, hl]], axis=0)
            probs, _ = _attn_probs(q_rows, k_cat, _sink_column(sink_ref, kv), bias)
            o = _rows_to_heads(_dot(probs.astype(bf16), v_cat, NN))
            g_src = gl_ref if kv < 2 else gh_ref
            g = g_src[:, 256 * (kv % 2):256 * (kv % 2 + 1)].astype(f32)
            y_ref[:, lanes] = (o * (g * _sigmoid(g))).astype(bf16)
            o_ref[:, lanes] = o.astype(bf16)

    args = [proj] * 7 + [sinks] + list(out_shards)
    tok = pl.BlockSpec((QB, D), lambda b, j: (cur(b, j), 0))
    res = _pcall(
        body, name="attn_forward", grid=(nb, nq),
        in_specs=specs + [pl.BlockSpec(w.shape, lambda b, j: (0, 0)) for w in out_shards],
        out_specs=(tok, tok) + tuple([ANY] * nw),
        out_shape=(_sds((T, D), bf16), _sds((T, D), bf16))
        + tuple(_sds((NDEV * w.shape[0], w.shape[1]), bf16) for w in out_shards),
        scratch_shapes=[pltpu.VMEM(w.shape, bf16) for w in out_shards] + _exchange_scratch(nw, 7),
        compiler_params=_params(("arbitrary", "arbitrary")),
    )(*args)
    return res[0], res[1], res[2:]


def _merge_and_head(x2d, tgt, proj, y_rnn, y_attn, w_r, w_a, w_o, gfin):
    T = x2d.shape[0]
    tb = min(T, 512)
    nsteps = T // tb

    def body(x_ref, t_ref, mr0, mr1, ma0, ma1, yr_ref, ya_ref, wr_ref, wa_ref, wo_ref, gf_ref,
             dx2_ref, dyr_ref, dya_ref, dmr_ref, dma_ref, loss_ref, gfin_ref, gwr_out, gwa_out, gwo_out,
             gwr_acc, gwa_acc, gwo_acc, out_sems):
        step = pl.program_id(0)

        @pl.when(step == 0)
        def _():
            loss_ref[...] = jnp.zeros_like(loss_ref)
            gfin_ref[...] = jnp.zeros_like(gfin_ref)
            gwr_acc[...] = jnp.zeros_like(gwr_acc)
            gwa_acc[...] = jnp.zeros_like(gwa_acc)
            gwo_acc[...] = jnp.zeros_like(gwo_acc)

        sr = _sigmoid(jnp.concatenate([mr0[...], mr1[...]], axis=1).astype(f32))
        sa = _sigmoid(jnp.concatenate([ma0[...], ma1[...]], axis=1).astype(f32))
        p_r = _dot(yr_ref[...], wr_ref[...], NN)
        p_a = _dot(ya_ref[...], wa_ref[...], NN)
        merged = (sr * p_r + sa * p_a).astype(bf16)
        x2 = x_ref[...] + _dot(merged, wo_ref[...], NN)
        rstd = lax.rsqrt(jnp.mean(x2 * x2, axis=-1, keepdims=True) + EPS)
        xh = x2 * rstd
        gf = gf_ref[...]
        err = xh * gf - t_ref[...]
        loss_ref[...] += jnp.sum(err * err)
        dy = err * (1.0 / D)
        gfin_ref[0:1, :] += jnp.sum(dy * xh, axis=0, keepdims=True)
        dxn = dy * gf
        dx2 = rstd * (dxn - xh * jnp.mean(dxn * xh, axis=-1, keepdims=True))
        dx2_ref[...] = dx2
        dx2b = dx2.astype(bf16)
        dmerged = _dot(dx2b, wo_ref[...], NT)
        gwo_acc[...] += _dot(merged, dx2b, TN)
        dpr = (dmerged * sr).astype(bf16)
        dyr_ref[...] = _dot(dpr, wr_ref[...], NT).astype(bf16)
        gwr_acc[...] += _dot(yr_ref[...], dpr, TN)
        dpa = (dmerged * sa).astype(bf16)
        dya_ref[...] = _dot(dpa, wa_ref[...], NT).astype(bf16)
        gwa_acc[...] += _dot(ya_ref[...], dpa, TN)
        dmr_ref[...] = (dmerged * p_r * (sr * (1.0 - sr))).astype(bf16)
        dma_ref[...] = (dmerged * p_a * (sa * (1.0 - sa))).astype(bf16)

        @pl.when(step == nsteps - 1)
        def _():
            copies = [pltpu.make_async_copy(src, dst, out_sems.at[k]) for k, (src, dst) in enumerate(
                ((gwr_acc, gwr_out), (gwa_acc, gwa_out), (gwo_acc, gwo_out)))]
            for cp in copies:
                cp.start()
            for cp in copies:
                cp.wait()

    tok = pl.BlockSpec((tb, D), lambda i: (i, 0))
    half = lambda c: pl.BlockSpec((tb, CH), lambda i, c=c: (i, c))
    wfull = pl.BlockSpec((D, D), lambda i: (0, 0), pipeline_mode=pl.Buffered(1))
    acc = pl.BlockSpec((8, D), lambda i: (0, 0))
    return _pcall(
        body, name="merge_and_head", grid=(nsteps,),
        in_specs=[tok, tok, half(9), half(10), half(11), half(12), tok, tok, wfull, wfull, wfull,
                  pl.BlockSpec((1, D), lambda i: (0, 0))],
        out_specs=(tok, tok, tok, tok, tok, acc, acc, ANY, ANY, ANY),
        out_shape=(_sds((T, D), f32), _sds((T, D), bf16), _sds((T, D), bf16), _sds((T, D), bf16),
                   _sds((T, D), bf16), _sds((8, D), f32), _sds((8, D), f32),
                   _sds((D, D), f32), _sds((D, D), f32), _sds((D, D), f32)),
        scratch_shapes=[pltpu.VMEM((D, D), f32)] * 3 + [pltpu.SemaphoreType.DMA((3,))],
        compiler_params=_params(("arbitrary",)),
    )(x2d, tgt, proj, proj, proj, proj, y_rnn, y_attn, w_r, w_a, w_o, gfin)


def _attn_backward(proj, dy_attn, o_attn, tabs, sinks, S, chip_sums):
    T = proj.shape[0]
    nb, nq = T // S, S // QB
    nex = len(chip_sums)
    specs, cur, prev = _attn_in_specs(S)
    last = nq - 1
    tab_cur = pl.BlockSpec((QB, 128), lambda b, j: (jnp.minimum(j, last), 0))
    tab_prev = pl.BlockSpec((QB, 128), lambda b, j: (jnp.maximum(jnp.minimum(j, last) - 1, 0), 0))
    specs = specs + [pl.BlockSpec((QB, D), lambda b, j: (cur(b, j), 0))] * 2 + [tab_cur] * 3 + [tab_prev] * 3
    q_scale = 1.0 / math.sqrt(HEAD)

    def rope_back(dt, tab):
        return jnp.concatenate([_rope_transposed(dt[:, 128 * l:128 * (l + 1)], *tab) for l in range(2)], axis=1)

    def body(q_ref, kc_ref, kp_ref, vc_ref, vp_ref, gl_ref, gh_ref, sink_ref, dy_ref, o_ref, cc, s1c, s2c, cp, s1p,
             s2p, *rest):
        ex_src = rest[:nex]
        dq_ref, dkv_ref, dg_ref, dsink_ref = rest[nex:nex + 4]
        ex_dst = rest[nex + 4:2 * nex + 4]
        carry_k, carry_v = rest[2 * nex + 4:2 * nex + 6]
        sems = rest[2 * nex + 6:]
        b, j = pl.program_id(0), pl.program_id(1)

        @pl.when((b == 0) & (j == 0))
        def _():
            dsink_ref[...] = jnp.zeros_like(dsink_ref)
            _start_all(_chip_exchange_copies(ex_src, ex_dst, *sems))

        @pl.when((b == nb - 1) & (j == nq))
        def _():
            _wait_all(_chip_exchange_copies(ex_src, ex_dst, *sems))

        @pl.when(j == 0)
        def _():
            carry_k[...] = jnp.zeros_like(carry_k)
            carry_v[...] = jnp.zeros_like(carry_v)

        @pl.when(j < nq)
        def _():
            bias = _window_bias(j == 0)
            tc = (cc[...], s1c[...], s2c[...])
            tp = (cp[...], s1p[...], s2p[...])
            kc, kp, vc, vp = kc_ref[...], kp_ref[...], vc_ref[...], vp_ref[...]
            dk_prev, dk_cur, dv_prev, dv_cur = [], [], [], []
            dsink_acc = jnp.zeros((8, 128), f32)
            r8 = lax.broadcasted_iota(jnp.int32, (8, 128), 0)
            l8 = lax.broadcasted_iota(jnp.int32, (8, 128), 1)
            for kv in range(KV_HEADS):
                lanes = slice(256 * kv, 256 * (kv + 1))
                hl = slice(HEAD * kv, HEAD * (kv + 1))
                q_rows = _heads_to_rows(q_ref[:, lanes])
                k_cat = jnp.concatenate([kp[:, hl], kc[:, hl]], axis=0)
                v_cat = jnp.concatenate([vp[:, hl], vc[:, hl]], axis=0)
                sink_col = _sink_column(sink_ref, kv)
                g_src = gl_ref if kv < 2 else gh_ref
                g = g_src[:, 256 * (kv % 2):256 * (kv % 2 + 1)].astype(f32)
                sg = _sigmoid(g)
                dy = dy_ref[:, lanes].astype(f32)
                do_rows = _heads_to_rows(dy * (g * sg)).astype(bf16)
                dq_parts, sink_parts, pb_parts, dk = [], [], [], None
                for half in range(2):
                    rs = slice(2 * QB * half, 2 * QB * (half + 1))
                    probs, p_sink = _attn_probs(q_rows[rs], k_cat, sink_col[rs], bias[rs])
                    dp = _dot(do_rows[rs], v_cat, NT)
                    rowdot = jnp.sum(probs * dp, axis=1, keepdims=True)
                    ds = (probs * (dp - rowdot)).astype(bf16)
                    sink_parts.append(-(p_sink * rowdot))
                    dq_parts.append(_dot(ds, k_cat, NN))
                    dk_half = _dot(ds, q_rows[rs], TN)
                    dk = dk_half if dk is None else dk + dk_half
                    pb_parts.append(probs.astype(bf16))
                dv = _dot(jnp.concatenate(pb_parts, axis=0), do_rows, TN)
                sink_rows = jnp.concatenate(sink_parts, axis=0)
                dq = _rows_to_heads(jnp.concatenate(dq_parts, axis=0)) * q_scale
                dq_ref[:, lanes] = rope_back(dq, tc).astype(bf16)
                o = o_ref[:, lanes].astype(f32)
                dg_ref[:, lanes] = (dy * o * (sg * (1.0 + g * (1.0 - sg)))).astype(bf16)
                for h in range(GROUP):
                    val = jnp.sum(sink_rows[QB * h:QB * (h + 1), :])
                    dsink_acc = dsink_acc + jnp.where((r8 == 0) & (l8 == GROUP * kv + h), val, 0.0)
                dk_prev.append(dk[:QB, :])
                dk_cur.append(dk[QB:, :])
                dv_prev.append(dv[:QB, :])
                dv_cur.append(dv[QB:, :])
            dsink_ref[...] += dsink_acc
            dkp = rope_back(jnp.concatenate(dk_prev, axis=1), tp)
            dkc = rope_back(jnp.concatenate(dk_cur, axis=1), tc)
            dkv_ref[:, 0:256] = (carry_k[...] + dkp).astype(bf16)
            dkv_ref[:, 256:512] = (carry_v[...] + jnp.concatenate(dv_prev, axis=1)).astype(bf16)
            carry_k[...] = dkc
            carry_v[...] = jnp.concatenate(dv_cur, axis=1)

        @pl.when(j == nq)
        def _():
            dkv_ref[:, 0:256] = carry_k[...].astype(bf16)
            dkv_ref[:, 256:512] = carry_v[...].astype(bf16)

    lag = lambda b, j: (b * nq + jnp.maximum(j - 1, 0), 0)
    args = [proj] * 7 + [sinks, dy_attn, o_attn] + list(tabs) + list(tabs) + list(chip_sums)
    res = _pcall(
        body, name="attn_backward", grid=(nb, nq + 1), in_specs=specs + [ANY] * nex,
        out_specs=(pl.BlockSpec((QB, D), lambda b, j: (cur(b, j), 0)), pl.BlockSpec((QB, 512), lag),
                   pl.BlockSpec((QB, D), lambda b, j: (cur(b, j), 0)), pl.BlockSpec((8, 128), lambda b, j: (0, 0)))
        + tuple([ANY] * nex),
        out_shape=(_sds((T, D), bf16), _sds((T, 512), bf16), _sds((T, D), bf16), _sds((8, 128), f32))
        + tuple(_sds(s.shape, s.dtype) for s in chip_sums),
        scratch_shapes=[pltpu.VMEM((QB, 256), f32), pltpu.VMEM((QB, 256), f32)] + _exchange_scratch(nex, 3),
        compiler_params=_params(("arbitrary", "arbitrary")),
    )(*args)
    return res[:4], res[4:]


def _lru_backward(proj, h_all, dy_rnn, cw_full, conv_b, w_a, b_a, w_x, b_x, lam, S):
    T = proj.shape[0]
    nb = T // S
    col, vec, wblk, cwblk = _lru_specs(S, nb)
    tokblk = pl.BlockSpec((S, RB), lambda n, b: (b, n))

    def body(x0_ref, g_ref, h_ref, dy_ref, cw_ref, cb_ref, wa_ref, ba_ref, wx_ref, bx_ref, lam_ref,
             du0_ref, dg_ref, gwa_ref, gwx_ref, vec_ref, gcw_ref, a_s, b_s, dh_s, edge_s):
        @pl.when(pl.program_id(1) == 0)
        def _():
            gwa_ref[...] = jnp.zeros_like(gwa_ref)
            gwx_ref[...] = jnp.zeros_like(gwx_ref)
            vec_ref[...] = jnp.zeros_like(vec_ref)
            gcw_ref[...] = jnp.zeros_like(gcw_ref)

        x0 = x0_ref[...].astype(f32)
        cw = cw_ref[...]
        lam_v = lam_ref[...]
        u, ub, r, i, sp, a, mult, inv_mult, taps = _lru_gates(x0, cw, cb_ref[...], wa_ref[...], ba_ref[...],
                                                              wx_ref[...], bx_ref[...], lam_v)
        h = h_ref[...]
        g = g_ref[...].astype(f32)
        dy = dy_ref[...].astype(f32)
        sg = _sigmoid(g)
        dg_ref[...] = (dy * h * (sg * (1.0 + g * (1.0 - sg)))).astype(bf16)
        _linear_scan(_shift_up(a, 1), dy * (g * sg), a_s, b_s, edge_s, dh_s, reverse=True)
        dh_total = dh_s[...]
        da = dh_total * _shift_down(h, 1)
        dmult = dh_total * (i * u)
        db = dh_total * mult
        di = db * u
        du = db * i
        dlog_a_c = ((-LRU_C) * a) * (da - dmult * (a * inv_mult))
        dr = dlog_a_c * sp
        dsp = jnp.sum(dlog_a_c * r, axis=0, keepdims=True)
        dpre_r = dr * r * (1.0 - r)
        dpre_i = di * i * (1.0 - i)
        dpre_rb = dpre_r.astype(bf16)
        dpre_ib = dpre_i.astype(bf16)
        du = du + _dot(dpre_rb, wa_ref[...].astype(bf16), NT) + _dot(dpre_ib, wx_ref[...].astype(bf16), NT)
        gwa_ref[...] += _dot(ub, dpre_rb, TN)
        gwx_ref[...] += _dot(ub, dpre_ib, TN)
        vec_ref[0:1, :] += jnp.sum(du, axis=0, keepdims=True)
        vec_ref[1:2, :] += jnp.sum(dpre_r, axis=0, keepdims=True)
        vec_ref[2:3, :] += jnp.sum(dpre_i, axis=0, keepdims=True)
        vec_ref[3:4, :] += dsp * (-_sigmoid(-lam_v))
        dx0 = cw[3:4, :] * du
        for k in range(3):
            dx0 = dx0 + cw[k:k + 1, :] * _shift_up(du, 3 - k)
        for k in range(4):
            gcw_ref[k:k + 1, :] += jnp.sum(du * taps[k], axis=0, keepdims=True)
        du0_ref[...] = dx0.astype(bf16)

    wacc = pl.BlockSpec((RB, RB), lambda n, b: (0, n))
    vacc = pl.BlockSpec((8, RB), lambda n, b: (0, n))
    cacc = pl.BlockSpec((8, RB), lambda n, b: (n, 0))
    return _pcall(
        body, name="lru_backward", grid=(RNN_BLOCKS, nb),
        in_specs=[col(0), col(8), tokblk, tokblk, cwblk, vec, wblk, vec, wblk, vec, vec],
        out_specs=(tokblk, tokblk, wacc, wacc, vacc, cacc),
        out_shape=(_sds((T, D), bf16), _sds((T, D), bf16), _sds((RB, D), f32), _sds((RB, D), f32),
                   _sds((8, D), f32), _sds((8 * RNN_BLOCKS, RB), f32)),
        scratch_shapes=[pltpu.VMEM((S, RB), f32)] * 3 + [pltpu.VMEM((S // 8, RB), f32)],
        compiler_params=_params(("arbitrary", "arbitrary")),
    )(proj, proj, h_all, dy_rnn, cw_full, conv_b, w_a, b_a, w_x, b_x, lam)


def _section_of_chunk(s):
    out = []
    for start, n in zip(SEC_START, SEC_CHUNKS):
        inside = (s >= start) & (s < start + n)
        out.append((inside, jnp.clip(s - start, 0, n - 1)))
    return out


EFFECT = pltpu.SideEffectType.DATAFLOW_SIDE_EFFECTING
HBM_SPEC = pl.BlockSpec(memory_space=pltpu.HBM)
SEM_SPEC = pl.BlockSpec(memory_space=pltpu.SEMAPHORE)


def _split_exchange_copies(src_ref, land_ref, send_sems, recv_sems):
    x, y, c = _my_place()
    copies = []
    for k in (3, 1, 2):
        px, py = (x + (k >> 1)) % 2, (y + (k & 1)) % 2
        copies.append(pltpu.make_async_remote_copy(
            src_ref=src_ref.at[2 * px + py], dst_ref=land_ref.at[k - 1], send_sem=send_sems[k - 1],
            recv_sem=recv_sems[k - 1], device_id=(px, py, c), device_id_type=MESH))
    return copies


def _exchange_start(chip_sum):
    _, r, cols = chip_sum.shape

    def body(src_ref, land_ref, s0, s1, s2, r0, r1, r2, src_thru, land_thru, token):
        for cp in _split_exchange_copies(src_ref, land_ref, (s0, s1, s2), (r0, r1, r2)):
            cp.start()
        token[...] = jnp.zeros_like(token)

    land = pltpu.with_memory_space_constraint(lax.empty((3, r, cols), chip_sum.dtype), pltpu.HBM)
    res = _pcall(
        body, name="exchange_start",
        out_shape=tuple([pltpu.SemaphoreType.DMA(())] * 6) + (
            pltpu.HBM(chip_sum.shape, chip_sum.dtype), pltpu.HBM((3, r, cols), chip_sum.dtype), _sds((8, 128), f32)),
        in_specs=(HBM_SPEC, HBM_SPEC), out_specs=tuple([SEM_SPEC] * 6) + (HBM_SPEC, HBM_SPEC, VMEM_SPEC),
        input_output_aliases={0: 6, 1: 7},
        compiler_params=pltpu.CompilerParams(has_side_effects=EFFECT),
    )(pltpu.with_memory_space_constraint(chip_sum, pltpu.HBM), land)
    return res[:6], res[6], res[7], res[8]


def _exchange_wait(sems, src_thru, land_thru, after):
    def body(src_ref, land_ref, s0, s1, s2, r0, r1, r2, after_ref, src_dead, got_ref):
        for cp in _split_exchange_copies(src_ref, land_ref, (s0, s1, s2), (r0, r1, r2)):
            cp.wait_send()
            cp.wait_recv()

    return _pcall(
        body, name="exchange_wait",
        out_shape=(pltpu.HBM(src_thru.shape, src_thru.dtype), pltpu.HBM(land_thru.shape, land_thru.dtype)),
        in_specs=(HBM_SPEC, HBM_SPEC) + tuple([SEM_SPEC] * 6) + (ANY,), out_specs=(HBM_SPEC, HBM_SPEC),
        input_output_aliases={0: 0, 1: 1},
        compiler_params=pltpu.CompilerParams(has_side_effects=EFFECT),
    )(src_thru, land_thru, *sems, after)[1]


def _input_grad(dsecs, wt_full, x2d, dx2, norm_g):
    T = x2d.shape[0]
    tb = min(T, 512)
    nsec = len(dsecs)
    ntok = T // tb

    def body(*refs):
        secs = refs[:nsec]
        wt_ref, x_ref, dx2_ref, g_ref, dx_ref, gnorm_ref = refs[nsec:]
        i = pl.program_id(0)

        @pl.when(i == 0)
        def _():
            gnorm_ref[...] = jnp.zeros_like(gnorm_ref)

        dh = None
        for a, (start, n) in enumerate(zip(SEC_START, SEC_CHUNKS)):
            part = _dot(secs[a][...], wt_ref[CH * start:CH * (start + n), :], NN)
            dh = part if dh is None else dh + part
        xv = x_ref[...]
        rstd = lax.rsqrt(jnp.mean(xv * xv, axis=-1, keepdims=True) + EPS)
        xh = xv * rstd
        gnorm_ref[0:1, :] += jnp.sum(dh * xh, axis=0, keepdims=True)
        dxn = dh * g_ref[...]
        dx_ref[...] = dx2_ref[...] + rstd * (dxn - xh * jnp.mean(dxn * xh, axis=-1, keepdims=True))

    tok = pl.BlockSpec((tb, D), lambda i: (i, 0))
    return _pcall(
        body, name="input_grad", grid=(ntok,),
        in_specs=[pl.BlockSpec((tb, sec.shape[1]), lambda i: (i, 0)) for sec in dsecs]
        + [pl.BlockSpec((D_IN, D), lambda i: (0, 0), pipeline_mode=pl.Buffered(1)), tok, tok,
           pl.BlockSpec((1, D), lambda i: (0, 0))],
        out_specs=(tok, pl.BlockSpec((8, D), lambda i: (0, 0))),
        out_shape=(_sds((T, D), f32), _sds((8, D), f32)),
        compiler_params=_params(("arbitrary",)),
    )(*dsecs, wt_full, x2d, dx2, norm_g)


def _w_in_grad(dsecs, h_bf):
    T = h_bf.shape[0]
    tk = min(T, 2048)
    nchunks = D_IN // CH
    nsec = len(dsecs)
    nt = T // tk

    def body(*refs):
        secs = refs[:nsec]
        h_ref, out_ref, acc = refs[nsec:]
        s, t = pl.program_id(0), pl.program_id(1)

        @pl.when(t == 0)
        def _():
            acc[...] = jnp.zeros_like(acc)

        h_rows = h_ref[pl.ds(pl.multiple_of(t * tk, tk), tk), :]
        for a, (start, n) in enumerate(zip(SEC_START, SEC_CHUNKS)):
            @pl.when((s >= start) & (s < start + n))
            def _(a=a):
                acc[...] += _dot(secs[a][...], h_rows, TN)

        @pl.when(t == nt - 1)
        def _():
            out_ref[...] = acc[...].astype(bf16)

    def sec_spec(a):
        def index(s, t, a=a):
            inside, local = _section_of_chunk(s)[a]
            return (jnp.where(inside, t, 0), local)
        return pl.BlockSpec((tk, CH), index)

    return _pcall(
        body, name="w_in_grad", grid=(nchunks, T // tk),
        in_specs=[sec_spec(a) for a in range(nsec)]
        + [pl.BlockSpec((T, D), lambda s, t: (0, 0), pipeline_mode=pl.Buffered(1))],
        out_specs=pl.BlockSpec((CH, D), lambda s, t: (s, 0)), out_shape=_sds((D_IN, D), bf16),
        scratch_shapes=[pltpu.VMEM((CH, D), f32)],
        compiler_params=_params(("arbitrary", "arbitrary")),
    )(*dsecs, h_bf)


SMALL_NAMES = ("lru_w_a", "lru_w_x", "conv_b", "lru_b_a", "lru_b_x", "lru_lambda", "norm_g", "final_norm_g",
               "attn_sinks", "conv_w")
MISC_ROW = {"conv_b": 0, "lru_b_a": 1, "lru_b_x": 2, "lru_lambda": 3, "norm_g": 8, "final_norm_g": 16,
            "attn_sinks": 24, "loss": 32}


def _small_step(gwa, gwx, gvec, gnorm_blk, gfin_blk, dsink_blk, loss_blk, gcw, params):
    srcs_rows = (RB // NDEV, RB // NDEV, 8, 8)
    flat = [t for n in SMALL_NAMES for t in params[n]]
    nout = 4 * len(SMALL_NAMES) + 1

    ra_, rx_, rm_, rc_ = srcs_rows
    rh, rf = ra_ + rx_, rm_ + rc_

    def reduce_body(gwa_ref, gwx_ref, gvec_ref, gnorm_ref, gfin_ref, dsink_ref, loss_ref, gcw_ref,
                    all_a, all_x, all_m, conv_out,
                    misc, out_h, out_f, in_h, in_f, mine_h, mine_f, every_h, every_f, sa, ra, sb, rb):
        x, y, c = _my_place()
        me = 4 * x + 2 * y + c

        misc[...] = jnp.zeros_like(misc)
        misc[0:8, :] = gvec_ref[...]
        misc[8:16, :] = gnorm_ref[...]
        misc[16:24, :] = gfin_ref[...]
        misc[24:32, 0:128] = dsink_ref[...]
        misc[32:40, :] = loss_ref[...]

        out_f[...] = jnp.zeros_like(out_f)
        for d in range(NDEV):
            out_h[d, 0:ra_, :] = gwa_ref[ra_ * d:ra_ * (d + 1), :].astype(bf16)
            out_h[d, ra_:rh, :] = gwx_ref[rx_ * d:rx_ * (d + 1), :].astype(bf16)
            out_f[d, 0:rm_, :] = misc[rm_ * d:rm_ * (d + 1), :]
            out_f[d, rm_:rf, 0:RB] = gcw_ref[rc_ * d:rc_ * (d + 1), :]

        def both(k, src_h, dst_h, src_f, dst_f, send, recv, peer):
            return [pltpu.make_async_remote_copy(src_ref=s_, dst_ref=d_, send_sem=send.at[2 * (k - 1) + t],
                                                 recv_sem=recv.at[2 * (k - 1) + t], device_id=peer,
                                                 device_id_type=MESH)
                    for t, (s_, d_) in enumerate(((src_h, dst_h), (src_f, dst_f)))]

        scatter = []
        for k in range(1, NDEV):
            px, py, pc = _peer(k)
            dev = 4 * px + 2 * py + pc
            scatter += both(k, out_h.at[dev], in_h.at[k - 1], out_f.at[dev], in_f.at[k - 1], sa, ra, (px, py, pc))
        for cp in scatter:
            cp.start()
        for cp in scatter:
            cp.wait()

        total_h = out_h[me].astype(f32)
        total_f = out_f[me]
        for k in range(NDEV - 1):
            total_h = total_h + in_h[k].astype(f32)
            total_f = total_f + in_f[k]
        conv_out[...] = total_f[rm_:rf, 0:RB]
        mine_h[...] = total_h.astype(bf16)
        mine_f[...] = total_f[0:rm_, :]
        every_h[me] = total_h.astype(bf16)
        every_f[me] = total_f[0:rm_, :]
        gather = []
        for k in range(1, NDEV):
            gather += both(k, mine_h, every_h.at[me], mine_f, every_f.at[me], sb, rb, _peer(k))
        for cp in gather:
            cp.start()
        for cp in gather:
            cp.wait()
        for d in range(NDEV):
            all_a[ra_ * d:ra_ * (d + 1), :] = every_h[d, 0:ra_, :].astype(f32)
            all_x[rx_ * d:rx_ * (d + 1), :] = every_h[d, ra_:rh, :].astype(f32)
            all_m[rm_ * d:rm_ * (d + 1), :] = every_f[d]

    def adam_body(*refs):
        all_a, all_x, all_m, conv_ref = refs[:4]
        prm = {n: refs[4 + 3 * k:7 + 3 * k] for k, n in enumerate(SMALL_NAMES)}
        nin = 4 + len(flat)
        outs = {n: refs[nin + 4 * k:nin + 4 * k + 4] for k, n in enumerate(SMALL_NAMES)}
        loss_out = refs[nin + nout - 1]
        g_conv = conv_ref[0:4, :]

        def update(name, g, pick=lambda r: r[...]):
            w_ref, m_ref, v_ref = prm[name]
            delta, m_new, v_new = _adam_math(g, pick(w_ref), pick(m_ref), pick(v_ref))
            return g, delta, m_new, v_new

        for n in range(RNN_BLOCKS):
            lanes = slice(RB * n, RB * (n + 1))
            for name, full in (("lru_w_a", all_a), ("lru_w_x", all_x)):
                for out, val in zip(outs[name], update(name, full[:, lanes], pick=lambda r, n=n: r[n])):
                    out[n] = val
        for name in ("conv_b", "lru_b_a", "lru_b_x", "lru_lambda", "norm_g", "final_norm_g"):
            row = MISC_ROW[name]
            for out, val in zip(outs[name], update(name, all_m[row:row + 1, :])):
                out[...] = val
        row = MISC_ROW["attn_sinks"]
        for out, val in zip(outs["attn_sinks"], update("attn_sinks", all_m[row:row + 1, 0:16])):
            out[...] = val
        for out, val in zip(outs["conv_w"], update("conv_w", g_conv)):
            out[...] = val
        row = MISC_ROW["loss"]
        loss_out[...] = all_m[row:row + 8, 0:128] * (0.5 / D)

    scratch = [pltpu.VMEM((64, D), f32), pltpu.VMEM((NDEV, rh, D), bf16), pltpu.VMEM((NDEV, rf, D), f32),
               pltpu.VMEM((NDEV - 1, rh, D), bf16), pltpu.VMEM((NDEV - 1, rf, D), f32),
               pltpu.VMEM((rh, D), bf16), pltpu.VMEM((rm_, D), f32),
               pltpu.VMEM((NDEV, rh, D), bf16), pltpu.VMEM((NDEV, rm_, D), f32)
               ] + [pltpu.SemaphoreType.DMA((2 * (NDEV - 1),))] * 4
    sums = _pcall(
        reduce_body, name="small_reduce",
        out_shape=(_sds((RB, D), f32), _sds((RB, D), f32), _sds((64, D), f32), _sds((8, RB), f32)),
        in_specs=[VMEM_SPEC] * 8, out_specs=tuple([VMEM_SPEC] * 4),
        scratch_shapes=scratch, compiler_params=_params(),
    )(gwa, gwx, gvec, gnorm_blk, gfin_blk, dsink_blk, loss_blk, gcw)
    out_shape = tuple(_sds(params[n][0].shape, f32) for n in SMALL_NAMES for _ in range(4)) + (_sds((8, 128), f32),)
    res = _pcall(
        adam_body, name="small_adamw", out_shape=out_shape,
        in_specs=[VMEM_SPEC] * (4 + len(flat)), out_specs=tuple([VMEM_SPEC] * nout), compiler_params=_params(),
    )(*sums, *flat)
    return {n: res[4 * k:4 * k + 4] for k, n in enumerate(SMALL_NAMES)}, res[-1]


def _pad_rows(v, rows=8):
    return jnp.concatenate([v, jnp.zeros((rows - v.shape[0], v.shape[1]), v.dtype)], axis=0)


def kernel(x, norm_g, w_in, conv_w, conv_b, lru_w_a, lru_b_a, lru_w_x, lru_b_x, lru_lambda, attn_sinks, w_rnn_out, w_attn_out, w_o, final_norm_g, loss_target, m_norm_g, m_w_in, m_conv_w, m_conv_b, m_lru_w_a, m_lru_b_a, m_lru_w_x, m_lru_b_x, m_lru_lambda, m_attn_sinks, m_w_rnn_out, m_w_attn_out, m_w_o, m_final_norm_g, v_norm_g, v_w_in, v_conv_w, v_conv_b, v_lru_w_a, v_lru_b_a, v_lru_w_x, v_lru_b_x, v_lru_lambda, v_attn_sinks, v_w_rnn_out, v_w_attn_out, v_w_o, v_final_norm_g):
    nb, S, _ = x.shape
    T = nb * S
    x2d = x.reshape(T, D)
    tgt = loss_target.reshape(T, D)
    fin_g = final_norm_g.reshape(1, D)
    w_a3, w_x3 = lru_w_a[0], lru_w_x[0]

    my_core = lax.axis_index("c").astype(jnp.int32).reshape(1)
    cx, cy = lax.axis_index("x"), lax.axis_index("y")
    chip_order = jnp.stack([2 * cx + cy, 2 * (1 - cx) + cy, 2 * cx + (1 - cy),
                            2 * (1 - cx) + (1 - cy)]).astype(jnp.int32)

    tabs = _rope_tables(S)
    h_bf, proj, wt_full, cw_full, _ = _in_proj_gather(
        x2d, norm_g, w_in[0].T.astype(bf16), _pad_rows(conv_w[0]), tabs, S, (), chip_order)
    y_rnn, h_all = _lru_forward(proj, cw_full, conv_b, w_a3, lru_b_a, w_x3, lru_b_x, lru_lambda, S)
    y_attn, o_attn, (wr_full, wa_full, wo_full) = _attn_forward(proj, attn_sinks, S,
                                                                (w_rnn_out[0], w_attn_out[0], w_o[0]))

    (dx2, dy_rnn, dy_attn, dmr, dma, loss_blk, gfin_blk, g_wr, g_wa, g_wo) = _merge_and_head(
        x2d, tgt, proj, y_rnn, y_attn, wr_full, wa_full, wo_full, fin_g)
    sums_out = _pair_sums([g_wr, g_wa, g_wo], bf16, my_core, "out")

    (dq, dkv, dga, dsink_blk), (p_wr, p_wa, p_wo) = _attn_backward(proj, dy_attn, o_attn, tabs, attn_sinks, S, sums_out)
    du0, dgr, gwa, gwx, gvec, gcw = _lru_backward(proj, h_all, dy_rnn, cw_full, conv_b, w_a3, lru_b_a, w_x3,
                                                  lru_b_x, lru_lambda, S)
    dsecs = (du0, dgr, dq, dkv, dga, dmr, dma)

    g_wt = _w_in_grad(dsecs, h_bf)
    (sum_in,) = _pair_sums([g_wt], bf16, my_core, "in")
    ex_sems, sum_in, landing, token = _exchange_start(sum_in)
    grad_x2d, gnorm_blk = _input_grad(dsecs, wt_full, x2d, dx2, norm_g + token[0, 0])
    p_wt = _exchange_wait(ex_sems, sum_in, landing, gnorm_blk)
    p_wt_own = lax.dynamic_index_in_dim(sum_in, 2 * cx + cy, axis=0, keepdims=False)

    small, loss_out = _small_step(gwa, gwx, gvec, gnorm_blk, gfin_blk, dsink_blk, loss_blk, gcw, {
        "lru_w_a": (w_a3, m_lru_w_a[0], v_lru_w_a[0]), "lru_w_x": (w_x3, m_lru_w_x[0], v_lru_w_x[0]),
        "conv_b": (conv_b, m_conv_b, v_conv_b), "lru_b_a": (lru_b_a, m_lru_b_a, v_lru_b_a),
        "lru_b_x": (lru_b_x, m_lru_b_x, v_lru_b_x), "lru_lambda": (lru_lambda, m_lru_lambda, v_lru_lambda),
        "norm_g": (norm_g, m_norm_g, v_norm_g),
        "final_norm_g": (fin_g, m_final_norm_g.reshape(1, D), v_final_norm_g.reshape(1, D)),
        "attn_sinks": (attn_sinks, m_attn_sinks, v_attn_sinks),
        "conv_w": (conv_w[0], m_conv_w[0], v_conv_w[0])})

    o_wt = _adamw(p_wt_own, p_wt, w_in[0].T, m_w_in[0].T, v_w_in[0].T, "adamw_w_in")
    o_wr, o_wa, o_wo = _adamw_group(
        (p_wr, p_wa, p_wo), (w_rnn_out[0], w_attn_out[0], w_o[0]),
        (m_w_rnn_out[0], m_w_attn_out[0], m_w_o[0]), (v_w_rnn_out[0], v_w_attn_out[0], v_w_o[0]), "adamw_w_out")

    def result(kind):
        d = {n: small[n][kind] for n in ("conv_b", "lru_b_a", "lru_b_x", "lru_lambda", "norm_g", "attn_sinks")}
        d.update({n: small[n][kind][None] for n in ("lru_w_a", "lru_w_x", "conv_w")})
        d["final_norm_g"] = small["final_norm_g"][kind].reshape(D)
        d.update({"w_in": o_wt[kind].T[None], "w_rnn_out": o_wr[kind][None], "w_attn_out": o_wa[kind][None],
                  "w_o": o_wo[kind][None]})
        return d

    order = ("norm_g", "w_in", "conv_w", "conv_b", "lru_w_a", "lru_b_a", "lru_w_x", "lru_b_x", "lru_lambda",
             "attn_sinks", "w_rnn_out", "w_attn_out", "w_o", "final_norm_g")
    outs = [loss_out[0, 0], grad_x2d.reshape(nb, S, D)]
    for kind in range(4):
        d = result(kind)
        outs += [d[n] for n in order]
    return tuple(outs)
```
